```python
import math
import jax, jax.numpy as jnp
from jax import lax
import numpy as np

D_MODEL = 1024
BATCH = 8
SEQ = 4096
DEPTH = 1

N_MEM = 256
MEM_HEADS = 4
MEM_HEAD_DIM = 128
MEM_WIDTH = MEM_HEADS * MEM_HEAD_DIM
D_RNN = 3 * D_MODEL // 4
LRU_BLOCK = 64
N_LRU_BLOCKS = D_RNN // LRU_BLOCK
CONV_WIDTH = 4
LRU_C = 8.0
DIL_GROUPS = ((128, 1), (512, 4), (2048, 16))
N_DIL_GROUPS = len(DIL_GROUPS)
HEADS_PER_GROUP = 4
DIL_HEAD_DIM = 64
N_DIL_HEADS = N_DIL_GROUPS * HEADS_PER_GROUP
DIL_QKV_WIDTH = 3 * N_DIL_HEADS * DIL_HEAD_DIM
DIL_OUT_WIDTH = HEADS_PER_GROUP * DIL_HEAD_DIM
NUM_BUCKETS = 32
MAX_DISTANCE = 2048
N_BRANCHES = 3
D_FF = 4 * D_MODEL
EPS = 1e-6
NEG = -1e30

SPLITS = (D_RNN, 2 * D_RNN, 2 * D_RNN + DIL_QKV_WIDTH, 2 * D_RNN + DIL_QKV_WIDTH + MEM_WIDTH)
D_IN = 2 * D_RNN + DIL_QKV_WIDTH + MEM_WIDTH + N_BRANCHES * D_MODEL

kernel_name = "hybrid_rglru_dilated_attn_memxattn_block"


def _rmsnorm(x, g):
    x32 = x.astype(jnp.float32)
    y = x32 * lax.rsqrt(jnp.mean(x32 * x32, axis=-1, keepdims=True) + EPS)
    return (y * g.astype(jnp.float32)).astype(x.dtype)


def _t5_bucket(dist):
    max_exact = NUM_BUCKETS // 2
    df = jnp.maximum(dist, 1).astype(jnp.float32)
    large = max_exact + (jnp.log(df / max_exact) / math.log(MAX_DISTANCE / max_exact)
                         * (NUM_BUCKETS - max_exact)).astype(jnp.int32)
    large = jnp.minimum(large, NUM_BUCKETS - 1)
    return jnp.where(dist < max_exact, dist, large)


def _rg_lru(xc, w_a, b_a, w_x, b_x, lam):
    B, S, C = xc.shape
    xb = xc.reshape(B, S, N_LRU_BLOCKS, LRU_BLOCK)
    r = jax.nn.sigmoid((jnp.einsum('bshi,hij->bshj', xb, w_a).reshape(B, S, C) + b_a).astype(jnp.float32))
    i = jax.nn.sigmoid((jnp.einsum('bshi,hij->bshj', xb, w_x).reshape(B, S, C) + b_x).astype(jnp.float32))
    log_a = -LRU_C * r * jax.nn.softplus(-lam.astype(jnp.float32))
    a = jnp.exp(log_a)
    mult = jnp.sqrt(-jnp.expm1(2.0 * log_a))
    b = mult * (i * xc.astype(jnp.float32))

    def combine(left, right):
        a1, b1 = left
        a2, b2 = right
        return a1 * a2, a2 * b1 + b2

    _, h = lax.associative_scan(combine, (a, b), axis=1)
    return h


def _dilated_group(q, k, v, table_g, dilation, span):
    B, S, H, Dh = q.shape
    L = S // dilation
    nb = -(-L // span)
    Lp = nb * span

    def to_blocks(t):
        t = t.reshape(B, L, dilation, H, Dh).transpose(0, 2, 3, 1, 4)
        t = jnp.pad(t, ((0, 0), (0, 0), (0, 0), (0, Lp - L), (0, 0)))
        return t.reshape(B, dilation, H, nb, span, Dh)

    def with_prev(t):
        prev = jnp.pad(t, ((0, 0), (0, 0), (0, 0), (1, 0), (0, 0), (0, 0)))[:, :, :, :-1]
        return jnp.concatenate([prev, t], axis=-2)

    qb = to_blocks(q)
    kk = with_prev(to_blocks(k))
    vv = with_prev(to_blocks(v))

    qi = jnp.arange(span)[:, None]
    kj = jnp.arange(2 * span)[None, :]
    off = qi + span - kj
    valid = (off >= 0) & (off <= span)
    mask = valid[None] & ((jnp.arange(nb)[:, None, None] > 0) | (kj >= span)[None])
    bucket = _t5_bucket(jnp.maximum(off, 0) * dilation)
    bias = jnp.transpose(table_g.astype(jnp.float32)[bucket], (2, 0, 1))

    s = jnp.einsum('brhnqc,brhnkc->brhnqk', qb, kk) * (DIL_HEAD_DIM ** -0.5) + bias[None, None, :, None]
    s = jnp.where(mask[None, None, None], s, NEG)
    m = jnp.max(s, axis=-1, keepdims=True)
    p = jnp.exp(s - m)
    den = jnp.sum(p, axis=-1)
    o = jnp.einsum('brhnqk,brhnkc->brhnqc', p, vv) / den[..., None]
    lse = m[..., 0] + jnp.log(den)

    o = o.reshape(B, dilation, H, Lp, Dh)[:, :, :, :L].transpose(0, 3, 1, 2, 4).reshape(B, S, H, Dh)
    lse = lse.reshape(B, dilation, H, Lp)[:, :, :, :L].transpose(0, 3, 1, 2).reshape(B, S, H)
    return o, lse


def _fwd_setup_inputs(seed: int = 0) -> dict:
    key = jax.random.key(seed)
    ks = jax.random.split(key, 24)
    f32 = jnp.float32

    def nrm(k, shape, fan_in):
        return jax.random.normal(k, shape, f32) * (fan_in ** -0.5)

    def gain(k, n):
        return 1.0 + 0.05 * jax.random.normal(k, (n,), f32)

    u = jax.random.uniform(ks[11], (D_RNN,), f32, 0.9, 0.999)
    sig = u ** (1.0 / LRU_C)
    lam = jnp.log(sig) - jnp.log1p(-sig)
    return {
        "x": jax.random.normal(ks[0], (BATCH, SEQ, D_MODEL), f32),
        "mem": jax.random.normal(ks[1], (BATCH, N_MEM, D_MODEL), f32),
        "g_mix": gain(ks[2], D_MODEL),
        "w_in": nrm(ks[3], (D_MODEL, D_IN), D_MODEL),
        "b_gate": 0.01 * jax.random.normal(ks[4], (N_BRANCHES * D_MODEL,), f32),
        "conv_w": nrm(ks[5], (CONV_WIDTH, D_RNN), CONV_WIDTH),
        "conv_b": 0.01 * jax.random.normal(ks[6], (D_RNN,), f32),
        "w_rg_a": nrm(ks[7], (N_LRU_BLOCKS, LRU_BLOCK, LRU_BLOCK), LRU_BLOCK),
        "b_rg_a": 0.01 * jax.random.normal(ks[8], (D_RNN,), f32),
        "w_rg_x": nrm(ks[9], (N_LRU_BLOCKS, LRU_BLOCK, LRU_BLOCK), LRU_BLOCK),
        "b_rg_x": 0.01 * jax.random.normal(ks[10], (D_RNN,), f32),
        "lru_lambda": lam,
        "w_lru_out": nrm(ks[12], (D_RNN, D_MODEL), D_RNN),
        "rel_bias": 0.1 * jax.random.normal(ks[13], (NUM_BUCKETS, N_DIL_HEADS), f32),
        "w_dil_out": nrm(ks[14], (DIL_OUT_WIDTH, D_MODEL), DIL_OUT_WIDTH),
        "g_mem": gain(ks[15], D_MODEL),
        "w_mem_kv": nrm(ks[16], (D_MODEL, 2 * MEM_WIDTH), D_MODEL),
        "w_mem_out": nrm(ks[17], (MEM_WIDTH, D_MODEL), MEM_WIDTH),
        "w_out": nrm(ks[18], (D_MODEL, D_MODEL), D_MODEL),
        "g_mlp": gain(ks[19], D_MODEL),
        "w_mlp_in": nrm(ks[20], (D_MODEL, D_FF), D_MODEL),
        "w_mlp_out": nrm(ks[21], (D_FF, D_MODEL), D_FF),
        "g_final": gain(ks[22], D_MODEL),
    }


def _fwd_reference(x, mem, g_mix, w_in, b_gate, conv_w, conv_b, w_rg_a, b_rg_a, w_rg_x, b_rg_x,
              lru_lambda, w_lru_out, rel_bias, w_dil_out, g_mem, w_mem_kv, w_mem_out, w_out,
              g_mlp, w_mlp_in, w_mlp_out, g_final):
    B, S, D = x.shape
    dt = x.dtype
    f32 = jnp.float32
    mem_n = _rmsnorm(mem, g_mem)

    for _ in range(DEPTH):
        h = _rmsnorm(x, g_mix)
        proj = h @ w_in
        x_lru, gate_lru, qkv, q_mem, gates = jnp.split(proj, SPLITS, axis=-1)

        xc = lax.conv_general_dilated(x_lru, conv_w[:, None, :].astype(x_lru.dtype), window_strides=(1,),
                                      padding=[(CONV_WIDTH - 1, 0)], dimension_numbers=('NWC', 'WIO', 'NWC'),
                                      feature_group_count=D_RNN) + conv_b
        hl = _rg_lru(xc, w_rg_a, b_rg_a, w_rg_x, b_rg_x, lru_lambda)
        y_lru = (jax.nn.gelu(gate_lru.astype(f32)) * hl).astype(dt) @ w_lru_out

        qkv = qkv.astype(f32).reshape(B, S, 3, N_DIL_GROUPS, HEADS_PER_GROUP, DIL_HEAD_DIM)
        outs, lses = [], []
        for g, (window, dil) in enumerate(DIL_GROUPS):
            o_g, lse_g = _dilated_group(qkv[:, :, 0, g], qkv[:, :, 1, g], qkv[:, :, 2, g],
                                        rel_bias[:, g * HEADS_PER_GROUP:(g + 1) * HEADS_PER_GROUP],
                                        dil, window // dil)
            outs.append(o_g)
            lses.append(lse_g)
        alpha = jax.nn.softmax(jnp.stack(lses, axis=0), axis=0)
        o_dil = jnp.sum(alpha[..., None] * jnp.stack(outs, axis=0), axis=0)
        y_dil = o_dil.reshape(B, S, DIL_OUT_WIDTH).astype(dt) @ w_dil_out

        kv = (mem_n @ w_mem_kv).astype(f32).reshape(B, N_MEM, 2, MEM_HEADS, MEM_HEAD_DIM)
        qm = q_mem.astype(f32).reshape(B, S, MEM_HEADS, MEM_HEAD_DIM)
        sm = jnp.einsum('bqhc,bkhc->bhqk', qm, kv[:, :, 0]) * (MEM_HEAD_DIM ** -0.5)
        pm = jax.nn.softmax(sm, axis=-1)
        om = jnp.einsum('bhqk,bkhc->bqhc', pm, kv[:, :, 1]).reshape(B, S, MEM_WIDTH)
        y_mem = om.astype(dt) @ w_mem_out

        gt = jax.nn.sigmoid((gates + b_gate).astype(f32)).reshape(B, S, N_BRANCHES, D)
        merged = (gt[:, :, 0] * y_lru.astype(f32) + gt[:, :, 1] * y_dil.astype(f32)
                  + gt[:, :, 2] * y_mem.astype(f32)).astype(dt)
        x = x + merged @ w_out

        hm = _rmsnorm(x, g_mlp)
        x = x + jnp.square(jax.nn.relu(hm @ w_mlp_in)) @ w_mlp_out

    return _rmsnorm(x, g_final)


import jax as _jax
import jax.numpy as _jnp

TWIN_FORMAT = 'train_step'
FWD_PARAMS = ['x', 'mem', 'g_mix', 'w_in', 'b_gate', 'conv_w', 'conv_b', 'w_rg_a', 'b_rg_a', 'w_rg_x', 'b_rg_x', 'lru_lambda', 'w_lru_out', 'rel_bias', 'w_dil_out', 'g_mem', 'w_mem_kv', 'w_mem_out', 'w_out', 'g_mlp', 'w_mlp_in', 'w_mlp_out', 'g_final']
TWIN_WEIGHTS = ['g_mix', 'w_in', 'b_gate', 'conv_w', 'conv_b', 'w_rg_a', 'b_rg_a', 'w_rg_x', 'b_rg_x', 'lru_lambda', 'w_lru_out', 'rel_bias', 'w_dil_out', 'g_mem', 'w_mem_kv', 'w_mem_out', 'w_out', 'g_mlp', 'w_mlp_in', 'w_mlp_out', 'g_final']
TWIN_DIFF_INPUT = 'x'
TWIN_INPUTS = ['x', 'mem', 'g_mix', 'w_in', 'b_gate', 'conv_w', 'conv_b', 'w_rg_a', 'b_rg_a', 'w_rg_x', 'b_rg_x', 'lru_lambda', 'w_lru_out', 'rel_bias', 'w_dil_out', 'g_mem', 'w_mem_kv', 'w_mem_out', 'w_out', 'g_mlp', 'w_mlp_in', 'w_mlp_out', 'g_final', 'loss_target', 'm_g_mix', 'm_w_in', 'm_b_gate', 'm_conv_w', 'm_conv_b', 'm_w_rg_a', 'm_b_rg_a', 'm_w_rg_x', 'm_b_rg_x', 'm_lru_lambda', 'm_w_lru_out', 'm_rel_bias', 'm_w_dil_out', 'm_g_mem', 'm_w_mem_kv', 'm_w_mem_out', 'm_w_out', 'm_g_mlp', 'm_w_mlp_in', 'm_w_mlp_out', 'm_g_final', 'v_g_mix', 'v_w_in', 'v_b_gate', 'v_conv_w', 'v_conv_b', 'v_w_rg_a', 'v_b_rg_a', 'v_w_rg_x', 'v_b_rg_x', 'v_lru_lambda', 'v_w_lru_out', 'v_rel_bias', 'v_w_dil_out', 'v_g_mem', 'v_w_mem_kv', 'v_w_mem_out', 'v_w_out', 'v_g_mlp', 'v_w_mlp_in', 'v_w_mlp_out', 'v_g_final']
TWIN_OUTPUTS = ['loss', 'grad_x', 'grad_g_mix', 'grad_w_in', 'grad_b_gate', 'grad_conv_w', 'grad_conv_b', 'grad_w_rg_a', 'grad_b_rg_a', 'grad_w_rg_x', 'grad_b_rg_x', 'grad_lru_lambda', 'grad_w_lru_out', 'grad_rel_bias', 'grad_w_dil_out', 'grad_g_mem', 'grad_w_mem_kv', 'grad_w_mem_out', 'grad_w_out', 'grad_g_mlp', 'grad_w_mlp_in', 'grad_w_mlp_out', 'grad_g_final', 'delta_g_mix', 'delta_w_in', 'delta_b_gate', 'delta_conv_w', 'delta_conv_b', 'delta_w_rg_a', 'delta_b_rg_a', 'delta_w_rg_x', 'delta_b_rg_x', 'delta_lru_lambda', 'delta_w_lru_out', 'delta_rel_bias', 'delta_w_dil_out', 'delta_g_mem', 'delta_w_mem_kv', 'delta_w_mem_out', 'delta_w_out', 'delta_g_mlp', 'delta_w_mlp_in', 'delta_w_mlp_out', 'delta_g_final', 'new_m_g_mix', 'new_m_w_in', 'new_m_b_gate', 'new_m_conv_w', 'new_m_conv_b', 'new_m_w_rg_a', 'new_m_b_rg_a', 'new_m_w_rg_x', 'new_m_b_rg_x', 'new_m_lru_lambda', 'new_m_w_lru_out', 'new_m_rel_bias', 'new_m_w_dil_out', 'new_m_g_mem', 'new_m_w_mem_kv', 'new_m_w_mem_out', 'new_m_w_out', 'new_m_g_mlp', 'new_m_w_mlp_in', 'new_m_w_mlp_out', 'new_m_g_final', 'new_v_g_mix', 'new_v_w_in', 'new_v_b_gate', 'new_v_conv_w', 'new_v_conv_b', 'new_v_w_rg_a', 'new_v_b_rg_a', 'new_v_w_rg_x', 'new_v_b_rg_x', 'new_v_lru_lambda', 'new_v_w_lru_out', 'new_v_rel_bias', 'new_v_w_dil_out', 'new_v_g_mem', 'new_v_w_mem_kv', 'new_v_w_mem_out', 'new_v_w_out', 'new_v_g_mlp', 'new_v_w_mlp_in', 'new_v_w_mlp_out', 'new_v_g_final']
TWIN_LEAF_KINDS = {'loss': 'loss', 'grad_x': 'grad_x', 'grad_g_mix': 'grad_w', 'grad_w_in': 'grad_w', 'grad_b_gate': 'grad_w', 'grad_conv_w': 'grad_w', 'grad_conv_b': 'grad_w', 'grad_w_rg_a': 'grad_w', 'grad_b_rg_a': 'grad_w', 'grad_w_rg_x': 'grad_w', 'grad_b_rg_x': 'grad_w', 'grad_lru_lambda': 'grad_w', 'grad_w_lru_out': 'grad_w', 'grad_rel_bias': 'grad_w', 'grad_w_dil_out': 'grad_w', 'grad_g_mem': 'grad_w', 'grad_w_mem_kv': 'grad_w', 'grad_w_mem_out': 'grad_w', 'grad_w_out': 'grad_w', 'grad_g_mlp': 'grad_w', 'grad_w_mlp_in': 'grad_w', 'grad_w_mlp_out': 'grad_w', 'grad_g_final': 'grad_w', 'delta_g_mix': 'delta_w', 'delta_w_in': 'delta_w', 'delta_b_gate': 'delta_w', 'delta_conv_w': 'delta_w', 'delta_conv_b': 'delta_w', 'delta_w_rg_a': 'delta_w', 'delta_b_rg_a': 'delta_w', 'delta_w_rg_x': 'delta_w', 'delta_b_rg_x': 'delta_w', 'delta_lru_lambda': 'delta_w', 'delta_w_lru_out': 'delta_w', 'delta_rel_bias': 'delta_w', 'delta_w_dil_out': 'delta_w', 'delta_g_mem': 'delta_w', 'delta_w_mem_kv': 'delta_w', 'delta_w_mem_out': 'delta_w', 'delta_w_out': 'delta_w', 'delta_g_mlp': 'delta_w', 'delta_w_mlp_in': 'delta_w', 'delta_w_mlp_out': 'delta_w', 'delta_g_final': 'delta_w', 'new_m_g_mix': 'new_m', 'new_m_w_in': 'new_m', 'new_m_b_gate': 'new_m', 'new_m_conv_w': 'new_m', 'new_m_conv_b': 'new_m', 'new_m_w_rg_a': 'new_m', 'new_m_b_rg_a': 'new_m', 'new_m_w_rg_x': 'new_m', 'new_m_b_rg_x': 'new_m', 'new_m_lru_lambda': 'new_m', 'new_m_w_lru_out': 'new_m', 'new_m_rel_bias': 'new_m', 'new_m_w_dil_out': 'new_m', 'new_m_g_mem': 'new_m', 'new_m_w_mem_kv': 'new_m', 'new_m_w_mem_out': 'new_m', 'new_m_w_out': 'new_m', 'new_m_g_mlp': 'new_m', 'new_m_w_mlp_in': 'new_m', 'new_m_w_mlp_out': 'new_m', 'new_m_g_final': 'new_m', 'new_v_g_mix': 'new_v', 'new_v_w_in': 'new_v', 'new_v_b_gate': 'new_v', 'new_v_conv_w': 'new_v', 'new_v_conv_b': 'new_v', 'new_v_w_rg_a': 'new_v', 'new_v_b_rg_a': 'new_v', 'new_v_w_rg_x': 'new_v', 'new_v_b_rg_x': 'new_v', 'new_v_lru_lambda': 'new_v', 'new_v_w_lru_out': 'new_v', 'new_v_rel_bias': 'new_v', 'new_v_w_dil_out': 'new_v', 'new_v_g_mem': 'new_v', 'new_v_w_mem_kv': 'new_v', 'new_v_w_mem_out': 'new_v', 'new_v_w_out': 'new_v', 'new_v_g_mlp': 'new_v', 'new_v_w_mlp_in': 'new_v', 'new_v_w_mlp_out': 'new_v', 'new_v_g_final': 'new_v'}


def _forward(args):
    return _fwd_reference(*[args[k] for k in FWD_PARAMS])


def _output_shape():
    out = _jax.eval_shape(lambda: _forward(_fwd_setup_inputs(0)))
    return out.shape, out.dtype

N_MICROBATCH = 1
ADAM_LR = 0.001
ADAM_B1 = 0.9
ADAM_B2 = 0.999
ADAM_EPS = 1e-08
ADAM_WD = 0.01
ADAM_STEP = 10
PER_EXAMPLE_BATCH_AXIS = {'x': 0, 'mem': 0, 'loss_target': 0}
SHARED_INPUTS = []
_WEIGHT_DTYPES = {'g_mix': _jnp.float32, 'w_in': _jnp.float32, 'b_gate': _jnp.float32, 'conv_w': _jnp.float32, 'conv_b': _jnp.float32, 'w_rg_a': _jnp.float32, 'b_rg_a': _jnp.float32, 'w_rg_x': _jnp.float32, 'b_rg_x': _jnp.float32, 'lru_lambda': _jnp.float32, 'w_lru_out': _jnp.float32, 'rel_bias': _jnp.float32, 'w_dil_out': _jnp.float32, 'g_mem': _jnp.float32, 'w_mem_kv': _jnp.float32, 'w_mem_out': _jnp.float32, 'w_out': _jnp.float32, 'g_mlp': _jnp.float32, 'w_mlp_in': _jnp.float32, 'w_mlp_out': _jnp.float32, 'g_final': _jnp.float32}
MOMENT_SCALE = {'g_mix': 6.813945e-02, 'w_in': 2.681226e-02, 'b_gate': 1.306366e-02, 'conv_w': 6.695017e-02, 'conv_b': 7.910011e-01, 'w_rg_a': 2.759675e-02, 'b_rg_a': 1.936215e-02, 'w_rg_x': 5.083799e-02, 'b_rg_x': 2.071135e-02, 'lru_lambda': 3.569513e-02, 'w_lru_out': 6.445121e-02, 'rel_bias': 2.819346e-02, 'w_dil_out': 1.982128e-02, 'g_mem': 1.819988e-02, 'w_mem_kv': 1.693203e-02, 'w_mem_out': 1.265713e-02, 'w_out': 6.103304e-02, 'g_mlp': 1.599615e-01, 'w_mlp_in': 7.957388e-02, 'w_mlp_out': 2.354704e-01, 'g_final': 3.237479e+01}


def _to_microbatches(a, axis):
    t = _jnp.moveaxis(a, axis, 0)
    t = t.reshape((N_MICROBATCH, t.shape[0] // N_MICROBATCH) + t.shape[1:])
    return _jnp.moveaxis(t, 1, axis + 1)


def setup_inputs(seed: int = 0) -> dict:
    inp = _fwd_setup_inputs(seed)
    key = _jax.random.fold_in(_jax.random.key(seed), 7919)
    shape, _ = _output_shape()
    out = dict(inp)
    out["loss_target"] = _jax.random.normal(_jax.random.fold_in(key, 0), shape, _jnp.float32)
    for i, name in enumerate(TWIN_WEIGHTS):
        w = inp[name].astype(_jnp.float32)
        if MOMENT_SCALE is None:
            s = _jnp.sqrt(_jnp.mean(_jnp.square(w)) + 1e-30)
        else:
            s = MOMENT_SCALE[name]
        km, kv = _jax.random.split(_jax.random.fold_in(key, i + 1))
        out[name] = w
        out["m_" + name] = s * _jax.random.normal(km, w.shape, _jnp.float32)
        out["v_" + name] = (s * s) * _jax.random.uniform(kv, w.shape, _jnp.float32, 0.5, 1.5)
    if N_MICROBATCH > 1:
        for name, axis in PER_EXAMPLE_BATCH_AXIS.items():
            out[name] = _to_microbatches(out[name], axis)
    return {'x': out['x'], 'mem': out['mem'], 'g_mix': out['g_mix'], 'w_in': out['w_in'], 'b_gate': out['b_gate'], 'conv_w': out['conv_w'], 'conv_b': out['conv_b'], 'w_rg_a': out['w_rg_a'], 'b_rg_a': out['b_rg_a'], 'w_rg_x': out['w_rg_x'], 'b_rg_x': out['b_rg_x'], 'lru_lambda': out['lru_lambda'], 'w_lru_out': out['w_lru_out'], 'rel_bias': out['rel_bias'], 'w_dil_out': out['w_dil_out'], 'g_mem': out['g_mem'], 'w_mem_kv': out['w_mem_kv'], 'w_mem_out': out['w_mem_out'], 'w_out': out['w_out'], 'g_mlp': out['g_mlp'], 'w_mlp_in': out['w_mlp_in'], 'w_mlp_out': out['w_mlp_out'], 'g_final': out['g_final'], 'loss_target': out['loss_target'], 'm_g_mix': out['m_g_mix'], 'm_w_in': out['m_w_in'], 'm_b_gate': out['m_b_gate'], 'm_conv_w': out['m_conv_w'], 'm_conv_b': out['m_conv_b'], 'm_w_rg_a': out['m_w_rg_a'], 'm_b_rg_a': out['m_b_rg_a'], 'm_w_rg_x': out['m_w_rg_x'], 'm_b_rg_x': out['m_b_rg_x'], 'm_lru_lambda': out['m_lru_lambda'], 'm_w_lru_out': out['m_w_lru_out'], 'm_rel_bias': out['m_rel_bias'], 'm_w_dil_out': out['m_w_dil_out'], 'm_g_mem': out['m_g_mem'], 'm_w_mem_kv': out['m_w_mem_kv'], 'm_w_mem_out': out['m_w_mem_out'], 'm_w_out': out['m_w_out'], 'm_g_mlp': out['m_g_mlp'], 'm_w_mlp_in': out['m_w_mlp_in'], 'm_w_mlp_out': out['m_w_mlp_out'], 'm_g_final': out['m_g_final'], 'v_g_mix': out['v_g_mix'], 'v_w_in': out['v_w_in'], 'v_b_gate': out['v_b_gate'], 'v_conv_w': out['v_conv_w'], 'v_conv_b': out['v_conv_b'], 'v_w_rg_a': out['v_w_rg_a'], 'v_b_rg_a': out['v_b_rg_a'], 'v_w_rg_x': out['v_w_rg_x'], 'v_b_rg_x': out['v_b_rg_x'], 'v_lru_lambda': out['v_lru_lambda'], 'v_w_lru_out': out['v_w_lru_out'], 'v_rel_bias': out['v_rel_bias'], 'v_w_dil_out': out['v_w_dil_out'], 'v_g_mem': out['v_g_mem'], 'v_w_mem_kv': out['v_w_mem_kv'], 'v_w_mem_out': out['v_w_mem_out'], 'v_w_out': out['v_w_out'], 'v_g_mlp': out['v_g_mlp'], 'v_w_mlp_in': out['v_w_mlp_in'], 'v_w_mlp_out': out['v_w_mlp_out'], 'v_g_final': out['v_g_final']}


def _loss(weights, diff, rest, loss_target):
    with _jax.named_scope("forward"):
        args = {**rest, TWIN_DIFF_INPUT: diff, **{k: w.astype(_WEIGHT_DTYPES[k]) for k, w in weights.items()}}
        y = _forward(args)
    with _jax.named_scope("loss_head"):
        err = _jnp.square(y.astype(_jnp.float32) - loss_target)
        return 0.5 * _jnp.sum(_jnp.mean(err, axis=-1)) if err.ndim else 0.5 * err


def _adamw(w, g, m, v):
    m = ADAM_B1 * m + (1.0 - ADAM_B1) * g
    v = ADAM_B2 * v + (1.0 - ADAM_B2) * _jnp.square(g)
    m_hat = m / (1.0 - ADAM_B1 ** ADAM_STEP)
    v_hat = v / (1.0 - ADAM_B2 ** ADAM_STEP)
    delta = -ADAM_LR * (m_hat / (_jnp.sqrt(v_hat) + ADAM_EPS) + ADAM_WD * w)
    return delta, m, v


def reference(x, mem, g_mix, w_in, b_gate, conv_w, conv_b, w_rg_a, b_rg_a, w_rg_x, b_rg_x, lru_lambda, w_lru_out, rel_bias, w_dil_out, g_mem, w_mem_kv, w_mem_out, w_out, g_mlp, w_mlp_in, w_mlp_out, g_final, loss_target, m_g_mix, m_w_in, m_b_gate, m_conv_w, m_conv_b, m_w_rg_a, m_b_rg_a, m_w_rg_x, m_b_rg_x, m_lru_lambda, m_w_lru_out, m_rel_bias, m_w_dil_out, m_g_mem, m_w_mem_kv, m_w_mem_out, m_w_out, m_g_mlp, m_w_mlp_in, m_w_mlp_out, m_g_final, v_g_mix, v_w_in, v_b_gate, v_conv_w, v_conv_b, v_w_rg_a, v_b_rg_a, v_w_rg_x, v_b_rg_x, v_lru_lambda, v_w_lru_out, v_rel_bias, v_w_dil_out, v_g_mem, v_w_mem_kv, v_w_mem_out, v_w_out, v_g_mlp, v_w_mlp_in, v_w_mlp_out, v_g_final):
    given = dict(x=x, mem=mem, g_mix=g_mix, w_in=w_in, b_gate=b_gate, conv_w=conv_w, conv_b=conv_b, w_rg_a=w_rg_a, b_rg_a=b_rg_a, w_rg_x=w_rg_x, b_rg_x=b_rg_x, lru_lambda=lru_lambda, w_lru_out=w_lru_out, rel_bias=rel_bias, w_dil_out=w_dil_out, g_mem=g_mem, w_mem_kv=w_mem_kv, w_mem_out=w_mem_out, w_out=w_out, g_mlp=g_mlp, w_mlp_in=w_mlp_in, w_mlp_out=w_mlp_out, g_final=g_final, loss_target=loss_target, m_g_mix=m_g_mix, m_w_in=m_w_in, m_b_gate=m_b_gate, m_conv_w=m_conv_w, m_conv_b=m_conv_b, m_w_rg_a=m_w_rg_a, m_b_rg_a=m_b_rg_a, m_w_rg_x=m_w_rg_x, m_b_rg_x=m_b_rg_x, m_lru_lambda=m_lru_lambda, m_w_lru_out=m_w_lru_out, m_rel_bias=m_rel_bias, m_w_dil_out=m_w_dil_out, m_g_mem=m_g_mem, m_w_mem_kv=m_w_mem_kv, m_w_mem_out=m_w_mem_out, m_w_out=m_w_out, m_g_mlp=m_g_mlp, m_w_mlp_in=m_w_mlp_in, m_w_mlp_out=m_w_mlp_out, m_g_final=m_g_final, v_g_mix=v_g_mix, v_w_in=v_w_in, v_b_gate=v_b_gate, v_conv_w=v_conv_w, v_conv_b=v_conv_b, v_w_rg_a=v_w_rg_a, v_b_rg_a=v_b_rg_a, v_w_rg_x=v_w_rg_x, v_b_rg_x=v_b_rg_x, v_lru_lambda=v_lru_lambda, v_w_lru_out=v_w_lru_out, v_rel_bias=v_rel_bias, v_w_dil_out=v_w_dil_out, v_g_mem=v_g_mem, v_w_mem_kv=v_w_mem_kv, v_w_mem_out=v_w_mem_out, v_w_out=v_w_out, v_g_mlp=v_g_mlp, v_w_mlp_in=v_w_mlp_in, v_w_mlp_out=v_w_mlp_out, v_g_final=v_g_final)
    weights = {n: given[n] for n in TWIN_WEIGHTS}
    shared = {n: given[n] for n in SHARED_INPUTS}
    per_example = {n: given[n] for n in ['x', 'mem']}
    grad_fn = _jax.value_and_grad(_loss, argnums=(0, 1))

    def one_microbatch(ex, loss_target):
        ex = dict(ex)
        diff = ex.pop(TWIN_DIFF_INPUT)
        return grad_fn(weights, diff, {**shared, **ex}, loss_target)

    if N_MICROBATCH == 1:
        loss, (grad_w, grad_x) = one_microbatch(per_example, given["loss_target"])
    else:
        def body(carry, xs):
            loss_sum, grad_sum = carry
            l_k, (gw_k, gx_k) = one_microbatch(xs[0], xs[1])
            with _jax.named_scope("update"):
                return (loss_sum + l_k, _jax.tree.map(_jnp.add, grad_sum, gw_k)), gx_k

        init = (_jnp.zeros((), _jnp.float32), _jax.tree.map(_jnp.zeros_like, weights))
        (loss, grad_w), grad_x = _jax.lax.scan(body, init, (per_example, given["loss_target"]))
    with _jax.named_scope("update"):
        delta_w, new_m, new_v = {}, {}, {}
        for n in TWIN_WEIGHTS:
            delta_w[n], new_m[n], new_v[n] = _adamw(weights[n], grad_w[n], given["m_" + n], given["v_" + n])
    return (loss, grad_x, *[grad_w[n] for n in TWIN_WEIGHTS], *[delta_w[n] for n in TWIN_WEIGHTS],
            *[new_m[n] for n in TWIN_WEIGHTS], *[new_v[n] for n in TWIN_WEIGHTS])
```

```python
import functools
import math

import jax
import jax.numpy as jnp
from jax import lax
from jax.experimental import pallas as pl
from jax.experimental.pallas import tpu as pltpu

F32 = jnp.float32
MXU_DTYPE = jnp.bfloat16
VMEM_LIMIT_BYTES = 56 * 1024 * 1024
N_DEV = 8

D_MODEL = 1024
N_MEM = 256
MEM_HEADS = 4
MEM_HEAD_DIM = 128
MEM_WIDTH = 512
D_RNN = 768
LRU_BLOCK = 64
N_LRU_BLOCKS = 12
LRU_GROUP = 256
N_LRU_GROUPS = 3
CONV_WIDTH = 4
LRU_C = 8.0
DIL_GROUPS = ((128, 1), (512, 4), (2048, 16))
SPAN = 128
DIL_HEADS = 4
DIL_HEAD_DIM = 64
NUM_BUCKETS = 32
MAX_DISTANCE = 2048
D_FF = 4096
D_IN = 7424
EPS = 1e-6
NEG = -1e30
C_XL, C_GATE, C_QKV, C_QM, C_GATES = 0, 768, 1536, 3840, 4352

ADAM_LR = 0.001
ADAM_B1 = 0.9
ADAM_B2 = 0.999
ADAM_EPS = 1e-08
ADAM_WD = 0.01
ADAM_STEP = 10

MESH = pl.DeviceIdType.MESH
GELU_K = math.sqrt(2.0 / math.pi)


def _cparams(sem=None):
    kw = dict(vmem_limit_bytes=VMEM_LIMIT_BYTES)
    if sem is not None:
        kw["dimension_semantics"] = sem
    return pltpu.CompilerParams(**kw)


def _mx(v):
    return v.astype(MXU_DTYPE)


def _dot(a, b, mode="nn"):
    dims = {"nn": (((1,), (0,)), ((), ())), "nt": (((1,), (1,)), ((), ())), "tn": (((0,), (0,)), ((), ()))}[mode]
    return lax.dot_general(_mx(a), _mx(b), dims, preferred_element_type=F32)


def _colsum(v):
    return jnp.sum(v, axis=0, keepdims=True)


def _matmul(a, b, *, M, N, K, mode, bm, bn, bk, name, out_dtypes=(F32,), epilogue=None, extras=(),
            a_off=(0, 0), b_off=(0, 0), j_outer=False):
    assert M % bm == 0 and N % bn == 0 and K % bk == 0, (name, M, N, K, bm, bn, bk)
    nm, nn, nk = M // bm, N // bn, K // bk

    def ij(f):
        if j_outer:
            return lambda j, i, k: f(i, j, k)
        return f

    if mode == "tn":
        a_spec = pl.BlockSpec((bk, bm), ij(lambda i, j, k: (k + a_off[0], i + a_off[1])))
    else:
        a_spec = pl.BlockSpec((bm, bk), ij(lambda i, j, k: (i + a_off[0], k + a_off[1])))
    if mode == "nt":
        b_spec = pl.BlockSpec((bn, bk), ij(lambda i, j, k: (j + b_off[0], k + b_off[1])))
    else:
        b_spec = pl.BlockSpec((bk, bn), ij(lambda i, j, k: (k + b_off[0], j + b_off[1])))
    ex_specs = [pl.BlockSpec((bm, bn), ij(functools.partial(lambda i, j, k, o: (i + o[0], j + o[1]), o=off)))
                for _, off in extras]
    out_spec = pl.BlockSpec((bm, bn), ij(lambda i, j, k: (i, j)))
    n_ex, n_out = len(extras), len(out_dtypes)

    def body(*refs):
        a_ref, b_ref = refs[0], refs[1]
        ex = refs[2:2 + n_ex]
        outs = refs[2 + n_ex:2 + n_ex + n_out]
        part = _dot(a_ref[...], b_ref[...], mode)

        def finish(acc):
            vals = epilogue(acc, *[e[...] for e in ex]) if epilogue is not None else (acc,)
            for o, v in zip(outs, vals):
                o[...] = v.astype(o.dtype)

        if nk == 1:
            finish(part)
        else:
            acc_ref = refs[-1]
            k = pl.program_id(2)

            @pl.when(k == 0)
            def _():
                acc_ref[...] = part

            @pl.when(k > 0)
            def _():
                acc_ref[...] += part

            @pl.when(k == nk - 1)
            def _():
                finish(acc_ref[...])

    grid = (nn, nm, nk) if j_outer else (nm, nn, nk)
    res = pl.pallas_call(
        body, name=name, grid=grid,
        in_specs=[a_spec, b_spec] + ex_specs,
        out_specs=[out_spec] * n_out,
        out_shape=[jax.ShapeDtypeStruct((M, N), dt) for dt in out_dtypes],
        scratch_shapes=[pltpu.VMEM((bm, bn), F32)] if nk > 1 else [],
        compiler_params=_cparams(("parallel", "parallel", "arbitrary")),
    )(a, b, *[e for e, _ in extras])
    return res[0] if n_out == 1 else res


def _rmsnorm_fwd(x, g, *, rows, name, bt=512):
    bt = min(bt, rows)

    def body(x_ref, g_ref, o_ref):
        xv = x_ref[...]
        r = lax.rsqrt(jnp.mean(xv * xv, axis=-1, keepdims=True) + EPS)
        o_ref[...] = (xv * r * g_ref[...]).astype(o_ref.dtype)

    return pl.pallas_call(
        body, name=name, grid=(rows // bt,),
        in_specs=[pl.BlockSpec((bt, D_MODEL), lambda i: (i, 0)), pl.BlockSpec((1, D_MODEL), lambda i: (0, 0))],
        out_specs=pl.BlockSpec((bt, D_MODEL), lambda i: (i, 0)),
        out_shape=jax.ShapeDtypeStruct((rows, D_MODEL), MXU_DTYPE),
        compiler_params=_cparams(("parallel",)),
    )(x, g.reshape(1, D_MODEL))


def _rms_bwd_tile(xv, gv, dyv):
    r = lax.rsqrt(jnp.mean(xv * xv, axis=-1, keepdims=True) + EPS)
    w = dyv * gv
    dx = r * w - xv * (r * r * r) * jnp.mean(w * xv, axis=-1, keepdims=True)
    dg = _colsum(dyv * xv * r)
    return dx, dg


def _rmsnorm_bwd(x, g, dy, res, *, rows, name, bt=512, want_dx=True):
    bt = min(bt, rows)
    has_res = res is not None

    def body(*refs):
        x_ref, g_ref, dy_ref = refs[:3]
        res_ref = refs[3] if has_res else None
        outs = refs[3 + int(has_res):]
        dx, dg = _rms_bwd_tile(x_ref[...], g_ref[...], dy_ref[...])
        if has_res:
            dx = dx + res_ref[...]
        dg_ref = outs[-1]

        @pl.when(pl.program_id(0) == 0)
        def _():
            dg_ref[...] = jnp.zeros_like(dg_ref)

        dg_ref[...] += dg
        if want_dx:
            outs[0][...] = dx

    row_spec = pl.BlockSpec((bt, D_MODEL), lambda i: (i, 0))
    vec_spec = pl.BlockSpec((1, D_MODEL), lambda i: (0, 0))
    ins = [x, g.reshape(1, D_MODEL), dy] + ([res] if has_res else [])
    out_shape = ([jax.ShapeDtypeStruct((rows, D_MODEL), F32)] if want_dx else []) + [jax.ShapeDtypeStruct((1, D_MODEL), F32)]
    out_specs = ([row_spec] if want_dx else []) + [vec_spec]
    res_ = pl.pallas_call(
        body, name=name, grid=(rows // bt,),
        in_specs=[row_spec, vec_spec, row_spec] + ([row_spec] if has_res else []),
        out_specs=out_specs, out_shape=out_shape,
        compiler_params=_cparams(("arbitrary",)),
    )(*ins)
    return res_ if want_dx else (None, res_[0])


def _loss_head(x2, g, tgt, *, rows, bt=512):
    def body(x_ref, g_ref, t_ref, loss_ref, dx_ref, dg_ref):
        xv, gv = x_ref[...], g_ref[...]
        r = lax.rsqrt(jnp.mean(xv * xv, axis=-1, keepdims=True) + EPS)
        diff = xv * r * gv - t_ref[...]
        part = jnp.sum(jnp.mean(diff * diff, axis=-1, keepdims=True), axis=0, keepdims=True) * 0.5
        dx, dg = _rms_bwd_tile(xv, gv, diff * (1.0 / D_MODEL))

        @pl.when(pl.program_id(0) == 0)
        def _():
            loss_ref[...] = jnp.zeros_like(loss_ref)
            dg_ref[...] = jnp.zeros_like(dg_ref)

        loss_ref[...] += part
        dg_ref[...] += dg
        dx_ref[...] = dx

    row_spec = pl.BlockSpec((bt, D_MODEL), lambda i: (i, 0))
    vec_spec = pl.BlockSpec((1, D_MODEL), lambda i: (0, 0))
    return pl.pallas_call(
        body, name="loss_head", grid=(rows // bt,),
        in_specs=[row_spec, vec_spec, row_spec],
        out_specs=[pl.BlockSpec((1, 1), lambda i: (0, 0)), row_spec, vec_spec],
        out_shape=[jax.ShapeDtypeStruct((1, 1), F32), jax.ShapeDtypeStruct((rows, D_MODEL), F32),
                   jax.ShapeDtypeStruct((1, D_MODEL), F32)],
        compiler_params=_cparams(("arbitrary",)),
    )(x2, g.reshape(1, D_MODEL), tgt)


LRU_T = 256


def _gelu(x):
    t = jnp.tanh(GELU_K * (x + 0.044715 * x * x * x))
    return 0.5 * x * (1.0 + t), t


def _gelu_grad(x, t):
    return 0.5 * (1.0 + t) + 0.5 * x * (1.0 - t * t) * GELU_K * (1.0 + 3.0 * 0.044715 * x * x)


def _softplus_neg(lam):
    z = -lam
    u = jnp.exp(-jnp.abs(z))
    w = 1.0 + u
    l1p = jnp.where(w == 1.0, u, jnp.log(w) * u / jnp.where(w == 1.0, 1.0, w - 1.0))
    return jnp.maximum(z, 0.0) + l1p


def _shift_down(cur, prev8, k, row8):
    y = pltpu.roll(cur, k, 0)
    head = jnp.where(row8 < k, pltpu.roll(prev8, k, 0), y[0:8])
    return jnp.concatenate([head, y[8:]], axis=0)


def _shift_up(cur, next8, k, row8):
    n = cur.shape[0]
    y = pltpu.roll(cur, n - k, 0)
    tail = jnp.where(row8 >= 8 - k, pltpu.roll(next8, 8 - k, 0), y[n - 8:n])
    return jnp.concatenate([y[0:n - 8], tail], axis=0)


def _lru_gates(xl, p8, cw, cb, wa, wx, ba, bx, lam, row8):
    sh = [xl] + [_shift_down(xl, p8, k, row8) for k in (1, 2, 3)]
    xc = cb + cw[3:4] * sh[0] + cw[2:3] * sh[1] + cw[1:2] * sh[2] + cw[0:1] * sh[3]
    r = jax.nn.sigmoid(_dot(xc, wa) + ba)
    i = jax.nn.sigmoid(_dot(xc, wx) + bx)
    sp = _softplus_neg(lam)
    la = -LRU_C * r * sp
    a = jnp.exp(la)
    mult = jnp.sqrt(jnp.tanh(-la) * (a * a + 1.0))
    return dict(sh=sh, xc=xc, r=r, i=i, sp=sp, a=a, mult=mult)


def _lru_specs(n_t, reverse):
    T = LRU_T
    tt = (lambda t: n_t - 1 - t) if reverse else (lambda t: t)
    blk = lambda col0: pl.BlockSpec((T, LRU_GROUP), lambda g, t: (tt(t), col0 + g))
    prev8 = lambda col0: pl.BlockSpec((8, LRU_GROUP), lambda g, t: (jnp.maximum(tt(t) * (T // 8) - 1, 0), col0 + g))
    vec = lambda rows: pl.BlockSpec((rows, LRU_GROUP), lambda g, t: (0, g))
    wbd = pl.BlockSpec((1, LRU_GROUP, LRU_GROUP), lambda g, t: (g, 0, 0))
    return blk, prev8, vec, wbd


def _lru_fwd(proj, conv_w, conv_b, wa_bd, wx_bd, b_a, b_x, lam, *, S):
    T = LRU_T
    n_t = S // T
    blk, _, vec, wbd = _lru_specs(n_t, False)

    def body(xl_ref, gate_ref, cw_ref, cb_ref, wa_ref, wx_ref, ba_ref, bx_ref, lam_ref,
             hl_ref, z_ref, prev8, hcar, a_s, b_s):
        @pl.when(pl.program_id(1) == 0)
        def _():
            prev8[...] = jnp.zeros_like(prev8)
            hcar[...] = jnp.zeros_like(hcar)

        row8 = lax.broadcasted_iota(jnp.int32, (8, LRU_GROUP), 0)
        xl = xl_ref[...]
        q = _lru_gates(xl, prev8[...], cw_ref[...], cb_ref[...], wa_ref[0], wx_ref[0], ba_ref[...], bx_ref[...],
                       lam_ref[...], row8)
        prev8[...] = xl[T - 8:T]
        a_s[...] = q["a"]
        b_s[...] = q["mult"] * q["i"] * q["xc"]

        def step(c, carry):
            off = pl.multiple_of(c * 8, 8)
            A = a_s[pl.ds(off, 8), :]
            B = b_s[pl.ds(off, 8), :]
            for k in (1, 2, 4):
                a_sh = jnp.where(row8 >= k, pltpu.roll(A, k, 0), 1.0)
                b_sh = jnp.where(row8 >= k, pltpu.roll(B, k, 0), 0.0)
                B = A * b_sh + B
                A = A * a_sh
            h = A * carry + B
            hl_ref[pl.ds(off, 8), :] = h
            return h[7:8, :]

        hcar[...] = lax.fori_loop(0, T // 8, step, hcar[...])
        ge, _ = _gelu(gate_ref[...])
        z_ref[...] = (ge * hl_ref[...]).astype(z_ref.dtype)

    return pl.pallas_call(
        body, name="lru_fwd", grid=(N_LRU_GROUPS, n_t),
        in_specs=[blk(C_XL // LRU_GROUP), blk(C_GATE // LRU_GROUP), vec(4), vec(1), wbd, wbd, vec(1), vec(1), vec(1)],
        out_specs=[blk(0), blk(0)],
        out_shape=[jax.ShapeDtypeStruct((S, D_RNN), F32), jax.ShapeDtypeStruct((S, D_RNN), MXU_DTYPE)],
        scratch_shapes=[pltpu.VMEM((8, LRU_GROUP), F32), pltpu.VMEM((1, LRU_GROUP), F32),
                        pltpu.VMEM((T, LRU_GROUP), F32), pltpu.VMEM((T, LRU_GROUP), F32)],
        compiler_params=_cparams(("parallel", "arbitrary")),
    )(proj, proj, conv_w, conv_b, wa_bd, wx_bd, b_a, b_x, lam)


def _lru_bwd(proj, hl, dz, conv_w, conv_b, wa_bd, wx_bd, b_a, b_x, lam, *, S):
    T = LRU_T
    n_t = S // T
    blk, prev8s, vec, wbd = _lru_specs(n_t, True)

    def body(xl_ref, xlp_ref, gate_ref, hl_ref, hlp_ref, dz_ref, cw_ref, cb_ref, wa_ref, wx_ref, ba_ref, bx_ref,
             lam_ref, dxl_ref, dgate_ref, dcw_ref, dcb_ref, dwa_ref, dwx_ref, dba_ref, dbx_ref, dlam_ref,
             next8, gcar, c_s, b_s, l_s):
        t = pl.program_id(1)
        first_chunk = t == n_t - 1

        @pl.when(t == 0)
        def _():
            next8[...] = jnp.zeros_like(next8)
            gcar[...] = jnp.zeros_like(gcar)
            for ref in (dcw_ref, dcb_ref, dwa_ref, dwx_ref, dba_ref, dbx_ref, dlam_ref):
                ref[...] = jnp.zeros_like(ref)

        row8 = lax.broadcasted_iota(jnp.int32, (8, LRU_GROUP), 0)
        rowT = lax.broadcasted_iota(jnp.int32, (T, LRU_GROUP), 0)
        keep = jnp.where(first_chunk, 0.0, 1.0)
        xl = xl_ref[...]
        wa, wx, lam_v = wa_ref[0], wx_ref[0], lam_ref[...]
        q = _lru_gates(xl, xlp_ref[...] * keep, cw_ref[...], cb_ref[...], wa, wx, ba_ref[...], bx_ref[...], lam_v, row8)
        a, mult, r, i, xc, sp = q["a"], q["mult"], q["r"], q["i"], q["xc"], q["sp"]
        hl_v = hl_ref[...]
        dz_v = dz_ref[...]
        gate = gate_ref[...]
        ge, th = _gelu(gate)
        dgate_ref[...] = (dz_v * hl_v * _gelu_grad(gate, th)).astype(dgate_ref.dtype)

        c_s[...] = jnp.where(rowT == T - 1, 0.0, pltpu.roll(a, T - 1, 0))
        b_s[...] = dz_v * ge + jnp.where(rowT == T - 1, gcar[...], 0.0)

        def step(n, carry):
            off = pl.multiple_of((T // 8 - 1 - n) * 8, 8)
            C = c_s[pl.ds(off, 8), :]
            B = b_s[pl.ds(off, 8), :]
            for k in (1, 2, 4):
                c_sh = jnp.where(row8 < 8 - k, pltpu.roll(C, 8 - k, 0), 1.0)
                b_sh = jnp.where(row8 < 8 - k, pltpu.roll(B, 8 - k, 0), 0.0)
                B = B + C * b_sh
                C = C * c_sh
            lam_t = B + C * carry
            l_s[pl.ds(off, 8), :] = lam_t
            return lam_t[0:1, :]

        lax.fori_loop(0, T // 8, step, jnp.zeros((1, LRU_GROUP), F32))
        lmb = l_s[...]
        gcar[...] = a[0:1, :] * lmb[0:1, :]

        h_prev = _shift_down(hl_v, hlp_ref[...] * keep, 1, row8)
        da = lmb * h_prev
        dmult = lmb * i * xc
        di = lmb * mult * xc
        dxc = lmb * mult * i
        dla = da * a - dmult * (a * a) / mult
        dr = dla * (-LRU_C * sp)
        dlam_ref[...] += _colsum(dla * (-LRU_C * r)) * (-jax.nn.sigmoid(-lam_v))
        dpa = dr * r * (1.0 - r)
        dpx = di * i * (1.0 - i)
        dxc = dxc + _dot(dpa, wa, "nt") + _dot(dpx, wx, "nt")
        dwa_ref[0] += _dot(xc, dpa, "tn")
        dwx_ref[0] += _dot(xc, dpx, "tn")
        dba_ref[...] += _colsum(dpa)
        dbx_ref[...] += _colsum(dpx)
        dcb_ref[...] += _colsum(dxc)
        cw = cw_ref[...]
        n8 = next8[...]
        dxl = cw[3:4] * dxc
        for k in (1, 2, 3):
            dxl = dxl + cw[3 - k:4 - k] * _shift_up(dxc, n8, k, row8)
        for k in range(4):
            dcw_ref[3 - k:4 - k, :] += _colsum(dxc * q["sh"][k])
        next8[...] = dxc[0:8]
        dxl_ref[...] = dxl.astype(dxl_ref.dtype)

    res = pl.pallas_call(
        body, name="lru_bwd", grid=(N_LRU_GROUPS, n_t),
        in_specs=[blk(C_XL // LRU_GROUP), prev8s(C_XL // LRU_GROUP), blk(C_GATE // LRU_GROUP), blk(0), prev8s(0), blk(0),
                  vec(4), vec(1), wbd, wbd, vec(1), vec(1), vec(1)],
        out_specs=[blk(0), blk(0), vec(4), vec(1), wbd, wbd, vec(1), vec(1), vec(1)],
        out_shape=[jax.ShapeDtypeStruct((S, D_RNN), MXU_DTYPE), jax.ShapeDtypeStruct((S, D_RNN), MXU_DTYPE),
                   jax.ShapeDtypeStruct((4, D_RNN), F32), jax.ShapeDtypeStruct((1, D_RNN), F32),
                   jax.ShapeDtypeStruct((N_LRU_GROUPS, LRU_GROUP, LRU_GROUP), F32),
                   jax.ShapeDtypeStruct((N_LRU_GROUPS, LRU_GROUP, LRU_GROUP), F32),
                   jax.ShapeDtypeStruct((1, D_RNN), F32), jax.ShapeDtypeStruct((1, D_RNN), F32),
                   jax.ShapeDtypeStruct((1, D_RNN), F32)],
        scratch_shapes=[pltpu.VMEM((8, LRU_GROUP), F32), pltpu.VMEM((1, LRU_GROUP), F32),
                        pltpu.VMEM((T, LRU_GROUP), F32), pltpu.VMEM((T, LRU_GROUP), F32), pltpu.VMEM((T, LRU_GROUP), F32)],
        compiler_params=_cparams(("parallel", "arbitrary")),
    )(proj, proj, proj, hl, hl, dz, conv_w, conv_b, wa_bd, wx_bd, b_a, b_x, lam)
    return res


def _block_diag(w):
    w4 = w.reshape(N_LRU_GROUPS, 4, LRU_BLOCK, 1, LRU_BLOCK)
    eye = jnp.eye(4, dtype=w.dtype).reshape(1, 4, 1, 4, 1)
    return (w4 * eye).reshape(N_LRU_GROUPS, LRU_GROUP, LRU_GROUP)


def _block_diag_extract(wbd):
    w5 = wbd.reshape(N_LRU_GROUPS, 4, LRU_BLOCK, 4, LRU_BLOCK)
    return jnp.stack([w5[:, a, :, a, :] for a in range(4)], axis=1).reshape(N_LRU_BLOCKS, LRU_BLOCK, LRU_BLOCK)


def _t5_bucket(dist):
    max_exact = NUM_BUCKETS // 2
    df = jnp.maximum(dist, 1).astype(jnp.float32)
    large = max_exact + (jnp.log(df / max_exact) / math.log(MAX_DISTANCE / max_exact)
                         * (NUM_BUCKETS - max_exact)).astype(jnp.int32)
    large = jnp.minimum(large, NUM_BUCKETS - 1)
    return jnp.where(dist < max_exact, dist, large)


def _band_offsets():
    qi = jnp.arange(SPAN)[:, None]
    kj = jnp.arange(2 * SPAN)[None, :]
    return qi + SPAN - kj


def _dil_buckets():
    off = _band_offsets()
    return jnp.stack([_t5_bucket(jnp.maximum(off, 0) * dil) for _, dil in DIL_GROUPS]).astype(jnp.int32)


def _dil_bias(rel_bias, buckets):
    def body(tbl_ref, bk_ref, o_ref):
        g = pl.program_id(0)
        qi = lax.broadcasted_iota(jnp.int32, (SPAN, 2 * SPAN), 0)
        kj = lax.broadcasted_iota(jnp.int32, (SPAN, 2 * SPAN), 1)
        off = qi + SPAN - kj
        valid = (off >= 0) & (off <= SPAN)
        bk = bk_ref[0]
        for h in range(DIL_HEADS):
            acc = jnp.zeros((SPAN, 2 * SPAN), F32)
            for b in range(NUM_BUCKETS):
                acc = jnp.where(bk == b, tbl_ref[b, g * DIL_HEADS + h], acc)
            o_ref[0, h] = jnp.where(valid, acc, NEG)

    return pl.pallas_call(
        body, name="dil_bias", grid=(3,),
        in_specs=[pl.BlockSpec(memory_space=pltpu.SMEM), pl.BlockSpec((1, SPAN, 2 * SPAN), lambda g: (g, 0, 0))],
        out_specs=pl.BlockSpec((1, DIL_HEADS, SPAN, 2 * SPAN), lambda g: (g, 0, 0, 0)),
        out_shape=jax.ShapeDtypeStruct((3, DIL_HEADS, SPAN, 2 * SPAN), F32),
        compiler_params=_cparams(("parallel",)),
    )(rel_bias, buckets)


def _dil_bias_bwd(dbias, buckets):
    def body(db_ref, bk_ref, o_ref):
        lane = lax.broadcasted_iota(jnp.int32, (1, 128), 1)
        rows = [jnp.zeros((1, 128), F32) for _ in range(NUM_BUCKETS)]
        for g in range(3):
            bk = bk_ref[g]
            for h in range(DIL_HEADS):
                d = db_ref[g, h]
                for b in range(NUM_BUCKETS):
                    tot = jnp.sum(_colsum(jnp.where(bk == b, d, 0.0)), axis=1, keepdims=True)
                    rows[b] = jnp.where(lane == g * DIL_HEADS + h, tot, rows[b])
        for b in range(NUM_BUCKETS):
            o_ref[b:b + 1, :] = rows[b]

    return pl.pallas_call(
        body, name="dil_bias_bwd",
        out_shape=jax.ShapeDtypeStruct((NUM_BUCKETS, 128), F32),
        compiler_params=_cparams(),
    )(dbias, buckets)


def _dil_scores(q, kcat, bias, first):
    s = _dot(q, kcat, "nt") * (DIL_HEAD_DIM ** -0.5) + bias
    kj = lax.broadcasted_iota(jnp.int32, s.shape, 1)
    return jnp.where(kj < jnp.where(first, SPAN, 0), NEG, s)


def _dil_attn_fwd(q, k, v, bias, g, *, S, blocks_per_seq):
    nb = S // SPAN
    hd = pl.BlockSpec((DIL_HEADS, SPAN, DIL_HEAD_DIM), lambda i: (0, i, 0))
    hd_prev = pl.BlockSpec((DIL_HEADS, SPAN, DIL_HEAD_DIM), lambda i: (0, jnp.maximum(i - 1, 0), 0))
    st = pl.BlockSpec((DIL_HEADS, SPAN, 1), lambda i: (0, i, 0))

    def body(q_ref, kp_ref, kc_ref, vp_ref, vc_ref, b_ref, o_ref, lse_ref):
        first = (pl.program_id(0) % blocks_per_seq) == 0
        for h in range(DIL_HEADS):
            kcat = jnp.concatenate([kp_ref[h], kc_ref[h]], axis=0)
            vcat = jnp.concatenate([vp_ref[h], vc_ref[h]], axis=0)
            s = _dil_scores(q_ref[h], kcat, b_ref[0, h], first)
            m = jnp.max(s, axis=-1, keepdims=True)
            p = jnp.exp(s - m)
            den = jnp.sum(p, axis=-1, keepdims=True)
            o_ref[h] = _dot(p, vcat) / den
            lse_ref[h] = m + jnp.log(den)

    return pl.pallas_call(
        body, name=f"dil_attn_fwd{g}", grid=(nb,),
        in_specs=[hd, hd_prev, hd, hd_prev, hd,
                  pl.BlockSpec((1, DIL_HEADS, SPAN, 2 * SPAN), lambda i: (g, 0, 0, 0))],
        out_specs=[hd, st],
        out_shape=[jax.ShapeDtypeStruct((DIL_HEADS, S, DIL_HEAD_DIM), F32), jax.ShapeDtypeStruct((DIL_HEADS, S, 1), F32)],
        compiler_params=_cparams(("parallel",)),
    )(q, k, k, v, v, bias)


def _dil_attn_bwd(q, k, v, do, lse, delta, bias, g, *, S, blocks_per_seq):
    nb = S // SPAN
    cl = lambda i: jnp.minimum(i, nb - 1)
    hd = pl.BlockSpec((DIL_HEADS, SPAN, DIL_HEAD_DIM), lambda i: (0, cl(i), 0))
    hd_prev = pl.BlockSpec((DIL_HEADS, SPAN, DIL_HEAD_DIM), lambda i: (0, jnp.maximum(cl(i) - 1, 0), 0))
    st = pl.BlockSpec((DIL_HEADS, SPAN, 1), lambda i: (0, cl(i), 0))
    kv_out = pl.BlockSpec((DIL_HEADS, SPAN, DIL_HEAD_DIM), lambda i: (0, jnp.maximum(i - 1, 0), 0))
    scale = DIL_HEAD_DIM ** -0.5

    def body(q_ref, kp_ref, kc_ref, vp_ref, vc_ref, do_ref, lse_ref, dl_ref, b_ref,
             dq_ref, dk_ref, dv_ref, db_ref, kcar, vcar):
        i = pl.program_id(0)

        @pl.when(i == 0)
        def _():
            kcar[...] = jnp.zeros_like(kcar)
            vcar[...] = jnp.zeros_like(vcar)
            db_ref[...] = jnp.zeros_like(db_ref)

        @pl.when(i < nb)
        def _():
            first = (i % blocks_per_seq) == 0
            for h in range(DIL_HEADS):
                kcat = jnp.concatenate([kp_ref[h], kc_ref[h]], axis=0)
                vcat = jnp.concatenate([vp_ref[h], vc_ref[h]], axis=0)
                qh, doh = q_ref[h], do_ref[h]
                p = jnp.exp(_dil_scores(qh, kcat, b_ref[0, h], first) - lse_ref[h])
                ds = p * (_dot(doh, vcat, "nt") - dl_ref[h])
                db_ref[0, h] += ds
                dq_ref[h] = (_dot(ds, kcat) * scale).astype(dq_ref.dtype)
                dkc = _dot(ds, qh, "tn") * scale
                dvc = _dot(p, doh, "tn")
                dk_ref[h] = (kcar[h] + dkc[0:SPAN]).astype(dk_ref.dtype)
                dv_ref[h] = (vcar[h] + dvc[0:SPAN]).astype(dv_ref.dtype)
                kcar[h] = dkc[SPAN:2 * SPAN]
                vcar[h] = dvc[SPAN:2 * SPAN]

        @pl.when(i == nb)
        def _():
            dk_ref[...] = kcar[...].astype(dk_ref.dtype)
            dv_ref[...] = vcar[...].astype(dv_ref.dtype)

    hshape = jax.ShapeDtypeStruct((DIL_HEADS, S, DIL_HEAD_DIM), MXU_DTYPE)
    bspec = pl.BlockSpec((1, DIL_HEADS, SPAN, 2 * SPAN), lambda i: (g, 0, 0, 0))
    return pl.pallas_call(
        body, name=f"dil_attn_bwd{g}", grid=(nb + 1,),
        in_specs=[hd, hd_prev, hd, hd_prev, hd, hd, st, st, bspec],
        out_specs=[hd, kv_out, kv_out, pl.BlockSpec((1, DIL_HEADS, SPAN, 2 * SPAN), lambda i: (0, 0, 0, 0))],
        out_shape=[hshape, hshape, hshape, jax.ShapeDtypeStruct((1, DIL_HEADS, SPAN, 2 * SPAN), F32)],
        scratch_shapes=[pltpu.VMEM((DIL_HEADS, SPAN, DIL_HEAD_DIM), F32), pltpu.VMEM((DIL_HEADS, SPAN, DIL_HEAD_DIM), F32)],
        compiler_params=_cparams(("arbitrary",)),
    )(q, k, k, v, v, do, lse, delta, bias)


def _dil_merge(os_, lses, *, S, bt=512):
    hd = pl.BlockSpec((DIL_HEADS, bt, DIL_HEAD_DIM), lambda i: (0, i, 0))
    st = pl.BlockSpec((DIL_HEADS, bt, 1), lambda i: (0, i, 0))

    def body(o0, o1, o2, l0, l1, l2, o_ref, lse_ref):
        a0, a1, a2 = l0[...], l1[...], l2[...]
        m = jnp.maximum(jnp.maximum(a0, a1), a2)
        e0, e1, e2 = jnp.exp(a0 - m), jnp.exp(a1 - m), jnp.exp(a2 - m)
        tot = e0 + e1 + e2
        o_ref[...] = (e0 / tot) * o0[...] + (e1 / tot) * o1[...] + (e2 / tot) * o2[...]
        lse_ref[...] = m + jnp.log(tot)

    return pl.pallas_call(
        body, name="dil_merge", grid=(S // bt,),
        in_specs=[hd, hd, hd, st, st, st], out_specs=[hd, st],
        out_shape=[jax.ShapeDtypeStruct((DIL_HEADS, S, DIL_HEAD_DIM), F32), jax.ShapeDtypeStruct((DIL_HEADS, S, 1), F32)],
        compiler_params=_cparams(("parallel",)),
    )(*os_, *lses)


def _row_dot(a, b, *, S, bt=512):
    hd = pl.BlockSpec((DIL_HEADS, bt, DIL_HEAD_DIM), lambda i: (0, i, 0))
    st = pl.BlockSpec((DIL_HEADS, bt, 1), lambda i: (0, i, 0))

    def body(a_ref, b_ref, o_ref):
        o_ref[...] = jnp.sum(a_ref[...] * b_ref[...], axis=-1, keepdims=True)

    return pl.pallas_call(
        body, name="dil_delta", grid=(S // bt,), in_specs=[hd, hd], out_specs=st,
        out_shape=jax.ShapeDtypeStruct((DIL_HEADS, S, 1), F32), compiler_params=_cparams(("parallel",)),
    )(a, b)


def _to_heads(t, dil, dtype):
    S = t.shape[0]
    L = S // dil
    return t.reshape(L, dil, DIL_HEADS, DIL_HEAD_DIM).transpose(2, 1, 0, 3).reshape(DIL_HEADS, S, DIL_HEAD_DIM).astype(dtype)


def _from_heads(t, dil, dtype):
    S = t.shape[1]
    L = S // dil
    return t.reshape(DIL_HEADS, dil, L, DIL_HEAD_DIM).transpose(2, 1, 0, 3).reshape(S, DIL_HEADS * DIL_HEAD_DIM).astype(dtype)


def _stat_to_natural(t, dil):
    S = t.shape[1]
    L = S // dil
    return t.reshape(DIL_HEADS, dil, L).transpose(0, 2, 1).reshape(DIL_HEADS, S, 1)


def _stat_to_dilated(t, dil):
    S = t.shape[1]
    L = S // dil
    return t.reshape(DIL_HEADS, L, dil).transpose(0, 2, 1).reshape(DIL_HEADS, S, 1)


MEM_T = 512
QM_BLK = C_QM // MEM_HEAD_DIM


def _mem_attn_fwd(proj, kv, *, S):
    scale = MEM_HEAD_DIM ** -0.5

    def body(q_ref, k_ref, v_ref, o_ref, lse_ref):
        s = _dot(q_ref[...], k_ref[...], "nt") * scale
        m = jnp.max(s, axis=-1, keepdims=True)
        p = jnp.exp(s - m)
        den = jnp.sum(p, axis=-1, keepdims=True)
        o_ref[...] = _dot(p, v_ref[...]) / den
        lse_ref[0] = m + jnp.log(den)

    return pl.pallas_call(
        body, name="mem_attn_fwd", grid=(S // MEM_T, MEM_HEADS),
        in_specs=[pl.BlockSpec((MEM_T, MEM_HEAD_DIM), lambda i, h: (i, QM_BLK + h)),
                  pl.BlockSpec((N_MEM, MEM_HEAD_DIM), lambda i, h: (0, h)),
                  pl.BlockSpec((N_MEM, MEM_HEAD_DIM), lambda i, h: (0, MEM_HEADS + h))],
        out_specs=[pl.BlockSpec((MEM_T, MEM_HEAD_DIM), lambda i, h: (i, h)),
                   pl.BlockSpec((1, MEM_T, 1), lambda i, h: (h, i, 0))],
        out_shape=[jax.ShapeDtypeStruct((S, MEM_WIDTH), F32), jax.ShapeDtypeStruct((MEM_HEADS, S, 1), F32)],
        compiler_params=_cparams(("parallel", "parallel")),
    )(proj, kv, kv)


def _mem_attn_bwd(proj, kv, om, lse, dom, *, S):
    scale = MEM_HEAD_DIM ** -0.5

    def body(q_ref, k_ref, v_ref, o_ref, lse_ref, do_ref, dq_ref, dk_ref, dv_ref):
        @pl.when(pl.program_id(1) == 0)
        def _():
            dk_ref[...] = jnp.zeros_like(dk_ref)
            dv_ref[...] = jnp.zeros_like(dv_ref)

        qv, kv_, vv, dov = q_ref[...], k_ref[...], v_ref[...], do_ref[...]
        p = jnp.exp(_dot(qv, kv_, "nt") * scale - lse_ref[0])
        delta = jnp.sum(dov * o_ref[...], axis=-1, keepdims=True)
        ds = p * (_dot(dov, vv, "nt") - delta)
        dq_ref[...] = (_dot(ds, kv_) * scale).astype(dq_ref.dtype)
        dk_ref[...] += _dot(ds, qv, "tn") * scale
        dv_ref[...] += _dot(p, dov, "tn")

    tile = pl.BlockSpec((MEM_T, MEM_HEAD_DIM), lambda h, i: (i, h))
    kvo = pl.BlockSpec((N_MEM, MEM_HEAD_DIM), lambda h, i: (0, h))
    return pl.pallas_call(
        body, name="mem_attn_bwd", grid=(MEM_HEADS, S // MEM_T),
        in_specs=[pl.BlockSpec((MEM_T, MEM_HEAD_DIM), lambda h, i: (i, QM_BLK + h)),
                  pl.BlockSpec((N_MEM, MEM_HEAD_DIM), lambda h, i: (0, h)),
                  pl.BlockSpec((N_MEM, MEM_HEAD_DIM), lambda h, i: (0, MEM_HEADS + h)),
                  tile, pl.BlockSpec((1, MEM_T, 1), lambda h, i: (h, i, 0)), tile],
        out_specs=[tile, kvo, kvo],
        out_shape=[jax.ShapeDtypeStruct((S, MEM_WIDTH), MXU_DTYPE), jax.ShapeDtypeStruct((N_MEM, MEM_WIDTH), F32),
                   jax.ShapeDtypeStruct((N_MEM, MEM_WIDTH), F32)],
        compiler_params=_cparams(("parallel", "arbitrary")),
    )(proj, kv, kv, om, lse, dom)


MIX_BM = 1024
MIX_BN = 256
GATES_BLK = C_GATES // MIX_BN


def _mix_specs(j_outer):
    ix = (lambda f: (lambda j, i: f(i, j))) if j_outer else (lambda f: f)
    act = lambda width: pl.BlockSpec((MIX_BM, width), ix(lambda i, j: (i, 0)))
    wgt = lambda width: pl.BlockSpec((width, MIX_BN), ix(lambda i, j: (0, j)))
    gate = lambda b: pl.BlockSpec((MIX_BM, MIX_BN), ix(lambda i, j: (i, GATES_BLK + 4 * b + j)))
    bias = lambda b: pl.BlockSpec((1, MIX_BN), ix(lambda i, j: (0, 4 * b + j)))
    tile = pl.BlockSpec((MIX_BM, MIX_BN), ix(lambda i, j: (i, j)))
    return act, wgt, gate, bias, tile


def _mix_fwd(z_lru, o_dil, om, w_lru, w_dil, w_mem, proj, b_gate, *, S):
    act, wgt, gate, bias, tile = _mix_specs(False)

    def body(zl, od, mo, wl, wd, wm, g0, g1, g2, b0, b1, b2, o_ref):
        acc = jax.nn.sigmoid(g0[...] + b0[...]) * _dot(zl[...], wl[...])
        acc += jax.nn.sigmoid(g1[...] + b1[...]) * _dot(od[...], wd[...])
        acc += jax.nn.sigmoid(g2[...] + b2[...]) * _dot(mo[...], wm[...])
        o_ref[...] = acc.astype(o_ref.dtype)

    return pl.pallas_call(
        body, name="mix_fwd", grid=(S // MIX_BM, D_MODEL // MIX_BN),
        in_specs=[act(D_RNN), act(256), act(MEM_WIDTH), wgt(D_RNN), wgt(256), wgt(MEM_WIDTH),
                  gate(0), gate(1), gate(2), bias(0), bias(1), bias(2)],
        out_specs=tile, out_shape=jax.ShapeDtypeStruct((S, D_MODEL), MXU_DTYPE),
        compiler_params=_cparams(("parallel", "parallel")),
    )(z_lru, o_dil, om, w_lru, w_dil, w_mem, proj, proj, proj, b_gate, b_gate, b_gate)


def _mix_bwd(dmerged, z_lru, o_dil, om, w_lru, w_dil, w_mem, proj, b_gate, *, S):
    act, wgt, gate, bias, tile = _mix_specs(True)

    def body(dm, zl, od, mo, wl, wd, wm, g0, g1, g2, b0, b1, b2,
             dg0, dg1, dg2, dy0, dy1, dy2, db0, db1, db2):
        @pl.when(pl.program_id(1) == 0)
        def _():
            for r in (db0, db1, db2):
                r[...] = jnp.zeros_like(r)

        dmv = dm[...]
        for act_ref, w_ref, g_ref, b_ref, dg_ref, dy_ref, db_ref in (
                (zl, wl, g0, b0, dg0, dy0, db0), (od, wd, g1, b1, dg1, dy1, db1), (mo, wm, g2, b2, dg2, dy2, db2)):
            y = _dot(act_ref[...], w_ref[...])
            gt = jax.nn.sigmoid(g_ref[...] + b_ref[...])
            dgate = dmv * y * gt * (1.0 - gt)
            dg_ref[...] = dgate.astype(dg_ref.dtype)
            dy_ref[...] = (dmv * gt).astype(dy_ref.dtype)
            db_ref[...] += _colsum(dgate)

    big = jax.ShapeDtypeStruct((S, D_MODEL), MXU_DTYPE)
    vec = jax.ShapeDtypeStruct((1, D_MODEL), F32)
    vspec = pl.BlockSpec((1, MIX_BN), lambda j, i: (0, j))
    return pl.pallas_call(
        body, name="mix_bwd", grid=(D_MODEL // MIX_BN, S // MIX_BM),
        in_specs=[tile, act(D_RNN), act(256), act(MEM_WIDTH), wgt(D_RNN), wgt(256), wgt(MEM_WIDTH),
                  gate(0), gate(1), gate(2), bias(0), bias(1), bias(2)],
        out_specs=[tile] * 6 + [vspec] * 3, out_shape=[big] * 6 + [vec] * 3,
        compiler_params=_cparams(("parallel", "arbitrary")),
    )(dmerged, z_lru, o_dil, om, w_lru, w_dil, w_mem, proj, proj, proj, b_gate, b_gate, b_gate)


def _adamw_math(w, g, m, v):
    m = ADAM_B1 * m + (1.0 - ADAM_B1) * g
    v = ADAM_B2 * v + (1.0 - ADAM_B2) * (g * g)
    m_hat = m / (1.0 - ADAM_B1 ** ADAM_STEP)
    v_hat = v / (1.0 - ADAM_B2 ** ADAM_STEP)
    delta = -ADAM_LR * (m_hat / (jnp.sqrt(v_hat) + ADAM_EPS) + ADAM_WD * w)
    return delta, m, v


def _adamw_landed(w, land, m, v, *, name):
    R, C = w.shape
    n_parts = land.shape[0]
    br = min(R, 256)
    tile = pl.BlockSpec((br, C), lambda i: (i, 0))

    def body(w_ref, l_ref, m_ref, v_ref, g_ref, d_ref, nm_ref, nv_ref):
        g = l_ref[0].astype(F32)
        for p in range(1, n_parts):
            g = g + l_ref[p].astype(F32)
        d, nm, nv = _adamw_math(w_ref[...], g, m_ref[...], v_ref[...])
        g_ref[...] = g
        d_ref[...] = d
        nm_ref[...] = nm
        nv_ref[...] = nv

    return pl.pallas_call(
        body, name=name, grid=(R // br,),
        in_specs=[tile, pl.BlockSpec((n_parts, br, C), lambda i: (0, i, 0)), tile, tile],
        out_specs=[tile] * 4, out_shape=[jax.ShapeDtypeStruct((R, C), F32)] * 4,
        compiler_params=_cparams(("parallel",)),
    )(w, land, m, v)


def _adamw_plain(w, g, m, v, *, name):
    def body(w_ref, g_ref, m_ref, v_ref, d_ref, nm_ref, nv_ref):
        d, nm, nv = _adamw_math(w_ref[...], g_ref[...], m_ref[...], v_ref[...])
        d_ref[...] = d
        nm_ref[...] = nm
        nv_ref[...] = nv

    return pl.pallas_call(
        body, name=name, out_shape=[jax.ShapeDtypeStruct(w.shape, F32)] * 3, compiler_params=_cparams(),
    )(w, g, m, v)


def _my_pos():
    return lax.axis_index("x"), lax.axis_index("y"), lax.axis_index("c")


def _dev_index(p):
    return 4 * p[0] + 2 * p[1] + p[2]


def _all_gather(shards):
    n = len(shards)
    hbm = pl.BlockSpec(memory_space=pl.ANY)

    def body(*refs):
        ins, outs = refs[:n], refs[n:2 * n]
        send_sems, recv_sems, local_sems = refs[2 * n:]
        x, y, c = _my_pos()
        me, sibling = (x, y, c), (x, y, 1 - c)
        chips = [(1 - x, y), (x, 1 - y), (1 - x, 1 - y)]

        def copy(a, k, block, to, src=None):
            dst = outs[a].at[_dev_index(block)]
            return pltpu.make_async_remote_copy(
                src_ref=dst if src is None else src, dst_ref=dst,
                send_sem=send_sems.at[a, k], recv_sem=recv_sems.at[a, k], device_id=to, device_id_type=MESH)

        mine = [pltpu.make_async_copy(ins[a], outs[a].at[_dev_index(me)], local_sems.at[a]) for a in range(n)]
        for cp in mine:
            cp.start()
        first = []
        for a in range(n):
            first.append(copy(a, 0, me, sibling, src=ins[a]))
            first += [copy(a, 1 + j, me, (*chip, c), src=ins[a]) for j, chip in enumerate(chips)]
        for cp in first:
            cp.start()
        passed = []
        for j, chip in enumerate(chips):
            for a in range(n):
                copy(a, 1 + j, (*chip, c), me).wait_recv()
                fwd = copy(a, 4 + j, (*chip, c), sibling)
                fwd.start()
                passed.append(fwd)
        for a in range(n):
            copy(a, 0, sibling, me).wait_recv()
        for j, chip in enumerate(chips):
            for a in range(n):
                copy(a, 4 + j, (*chip, 1 - c), me).wait_recv()
        for cp in first + passed:
            cp.wait_send()
        for cp in mine:
            cp.wait()

    return pl.pallas_call(
        body, name="all_gather_weights",
        in_specs=[hbm] * n, out_specs=[hbm] * n,
        out_shape=[jax.ShapeDtypeStruct((N_DEV,) + s.shape, s.dtype) for s in shards],
        scratch_shapes=[pltpu.SemaphoreType.DMA((n, 7)), pltpu.SemaphoreType.DMA((n, 7)), pltpu.SemaphoreType.DMA((n,))],
        compiler_params=pltpu.CompilerParams(has_side_effects=True),
    )(*shards)


def _peers(me):
    x, y, c = me
    out = []
    for k in range(1, 8):
        fx, fy, fc = (k >> 2) & 1, (k >> 1) & 1, k & 1
        out.append((k - 1, (1 - x if fx else x, 1 - y if fy else y, 1 - c if fc else c)))
    return out


def _exchange_partials(parts):
    n = len(parts)
    hbm = pl.BlockSpec(memory_space=pl.ANY)

    def body(*refs):
        ins, outs = refs[:n], refs[n:2 * n]
        send_sems, recv_sems, local_sems = refs[2 * n:]
        me = _my_pos()
        my_idx = _dev_index(me)
        mine = [pltpu.make_async_copy(ins[a].at[my_idx], outs[a].at[my_idx], local_sems.at[a]) for a in range(n)]
        for cp in mine:
            cp.start()
        sent = []
        for k, peer in _peers(me):
            for a in range(n):
                cp = pltpu.make_async_remote_copy(
                    src_ref=ins[a].at[_dev_index(peer)], dst_ref=outs[a].at[my_idx],
                    send_sem=send_sems.at[a, k], recv_sem=recv_sems.at[a, k], device_id=peer, device_id_type=MESH)
                cp.start()
                sent.append(cp)
        for k, peer in _peers(me):
            for a in range(n):
                pltpu.make_async_remote_copy(
                    src_ref=ins[a].at[my_idx], dst_ref=outs[a].at[_dev_index(peer)],
                    send_sem=send_sems.at[a, k], recv_sem=recv_sems.at[a, k], device_id=peer,
                    device_id_type=MESH).wait_recv()
        for cp in sent:
            cp.wait_send()
        for cp in mine:
            cp.wait()

    return pl.pallas_call(
        body, name="exchange_grad_partials",
        in_specs=[hbm] * n, out_specs=[hbm] * n,
        out_shape=[jax.ShapeDtypeStruct(p.shape, p.dtype) for p in parts],
        scratch_shapes=[pltpu.SemaphoreType.DMA((n, 7)), pltpu.SemaphoreType.DMA((n, 7)), pltpu.SemaphoreType.DMA((n,))],
        compiler_params=pltpu.CompilerParams(has_side_effects=True),
    )(*parts)


def _all_reduce_small(pack):
    R = pack.shape[0]

    def body(in_ref, out_ref, land, send_sems, recv_sems):
        me = _my_pos()
        my_idx = _dev_index(me)
        land[my_idx] = in_ref[...]
        sent = []
        for k, peer in _peers(me):
            cp = pltpu.make_async_remote_copy(
                src_ref=in_ref, dst_ref=land.at[my_idx], send_sem=send_sems.at[k], recv_sem=recv_sems.at[k],
                device_id=peer, device_id_type=MESH)
            cp.start()
            sent.append(cp)
        for k, peer in _peers(me):
            pltpu.make_async_remote_copy(
                src_ref=in_ref, dst_ref=land.at[_dev_index(peer)], send_sem=send_sems.at[k], recv_sem=recv_sems.at[k],
                device_id=peer, device_id_type=MESH).wait_recv()
        for cp in sent:
            cp.wait_send()
        acc = land[0]
        for d in range(1, N_DEV):
            acc = acc + land[d]
        out_ref[...] = acc

    return pl.pallas_call(
        body, name="all_reduce_small",
        in_specs=[pl.BlockSpec(memory_space=pltpu.VMEM)], out_specs=pl.BlockSpec(memory_space=pltpu.VMEM),
        out_shape=jax.ShapeDtypeStruct(pack.shape, F32),
        scratch_shapes=[pltpu.VMEM((N_DEV, R, 128), F32), pltpu.SemaphoreType.DMA((7,)), pltpu.SemaphoreType.DMA((7,))],
        compiler_params=pltpu.CompilerParams(has_side_effects=True, vmem_limit_bytes=VMEM_LIMIT_BYTES),
    )(pack)


def _local_step(x, mem, tgt, W, P):
    S = x.shape[0]
    mem_n = _rmsnorm_fwd(mem, P["g_mem"], rows=N_MEM, name="norm_mem")
    kv = _matmul(mem_n, W["w_mem_kv"], M=N_MEM, N=2 * MEM_WIDTH, K=D_MODEL, mode="nn", bm=N_MEM, bn=512, bk=D_MODEL,
                 name="mm_kv")
    h = _rmsnorm_fwd(x, P["g_mix"], rows=S, name="norm_mix")
    proj = _matmul(h, W["w_in"], M=S, N=D_IN, K=D_MODEL, mode="nn", bm=512, bn=D_IN // 2, bk=D_MODEL, name="mm_in",
                   j_outer=True)

    wa_bd, wx_bd = _mx(_block_diag(P["w_rg_a"])), _mx(_block_diag(P["w_rg_x"]))
    lru_args = (W["conv_w"], P["conv_b"].reshape(1, -1), wa_bd, wx_bd, P["b_rg_a"].reshape(1, -1),
                P["b_rg_x"].reshape(1, -1), P["lru_lambda"].reshape(1, -1))
    hl, z_lru = _lru_fwd(proj, *lru_args, S=S)

    buckets = _dil_buckets()
    bias = _dil_bias(P["rel_bias"], buckets)
    qkv_heads, o_nat, lse_nat = [], [], []
    for g, (window, dil) in enumerate(DIL_GROUPS):
        cols = [C_QKV + three * 768 + g * 256 for three in range(3)]
        qh, kh, vh = [_to_heads(proj[:, c0:c0 + 256], dil, MXU_DTYPE) for c0 in cols]
        qkv_heads.append((qh, kh, vh))
        o_g, lse_g = _dil_attn_fwd(qh, kh, vh, bias, g, S=S, blocks_per_seq=S // dil // SPAN)
        o_nat.append(_to_heads(_from_heads(o_g, dil, F32), 1, F32))
        lse_nat.append(_stat_to_natural(lse_g, dil))
    o_dil_h, lse_dil = _dil_merge(o_nat, lse_nat, S=S)
    o_dil = _from_heads(o_dil_h, 1, MXU_DTYPE)

    om, lse_mem = _mem_attn_fwd(proj, kv, S=S)
    b_gate = P["b_gate"].reshape(1, -1)
    merged = _mix_fwd(z_lru, o_dil, om, W["w_lru_out"], W["w_dil_out"], W["w_mem_out"], proj, b_gate, S=S)
    x1 = _matmul(merged, W["w_out"], M=S, N=D_MODEL, K=D_MODEL, mode="nn", bm=512, bn=D_MODEL, bk=D_MODEL, name="mm_out",
                 epilogue=lambda acc, r: (r + acc,), extras=[(x, (0, 0))])
    hm = _rmsnorm_fwd(x1, P["g_mlp"], rows=S, name="norm_mlp")

    def relu2(acc):
        rl = jnp.maximum(acc, 0.0)
        return acc, rl * rl

    u, act = _matmul(hm, W["w_mlp_in"], M=S, N=D_FF, K=D_MODEL, mode="nn", bm=512, bn=1024, bk=D_MODEL, name="mm_mlp_in",
                     out_dtypes=(F32, MXU_DTYPE), epilogue=relu2, j_outer=True)
    x2 = _matmul(act, W["w_mlp_out"], M=S, N=D_MODEL, K=D_FF, mode="nn", bm=512, bn=D_MODEL, bk=1024, name="mm_mlp_out",
                 epilogue=lambda acc, r: (r + acc,), extras=[(x1, (0, 0))])
    loss, dx2, dg_final = _loss_head(x2, P["g_final"], tgt, rows=S)

    G, Gs = {}, {}
    Gs["g_final"] = dg_final
    G["w_mlp_out"] = _matmul(act, dx2, M=D_FF, N=D_MODEL, K=S, mode="tn", bm=1024, bn=D_MODEL, bk=512, name="mm_dw_mlp_out")
    du = _matmul(dx2, W["w_mlp_out"], M=S, N=D_FF, K=D_MODEL, mode="nt", bm=512, bn=1024, bk=D_MODEL, name="mm_du",
                 out_dtypes=(MXU_DTYPE,), epilogue=lambda acc, uu: (acc * 2.0 * jnp.maximum(uu, 0.0),),
                 extras=[(u, (0, 0))], j_outer=True)
    G["w_mlp_in"] = _matmul(hm, du, M=D_MODEL, N=D_FF, K=S, mode="tn", bm=D_MODEL, bn=1024, bk=512, name="mm_dw_mlp_in")
    dhm = _matmul(du, W["w_mlp_in"], M=S, N=D_MODEL, K=D_FF, mode="nt", bm=512, bn=D_MODEL, bk=1024, name="mm_dhm")
    dx1, Gs["g_mlp"] = _rmsnorm_bwd(x1, P["g_mlp"], dhm, dx2, rows=S, name="norm_mlp_bwd")
    G["w_out"] = _matmul(merged, dx1, M=D_MODEL, N=D_MODEL, K=S, mode="tn", bm=D_MODEL, bn=D_MODEL, bk=512, name="mm_dw_out")
    dmerged = _matmul(dx1, W["w_out"], M=S, N=D_MODEL, K=D_MODEL, mode="nt", bm=512, bn=D_MODEL, bk=D_MODEL, name="mm_dmerged")
    (dg0, dg1, dg2, dy_lru, dy_dil, dy_mem, db0, db1, db2) = _mix_bwd(
        dmerged, z_lru, o_dil, om, W["w_lru_out"], W["w_dil_out"], W["w_mem_out"], proj, b_gate, S=S)
    Gs["b_gate"] = jnp.concatenate([db0, db1, db2], axis=1)

    G["w_mem_out"] = _matmul(om, dy_mem, M=MEM_WIDTH, N=D_MODEL, K=S, mode="tn", bm=MEM_WIDTH, bn=D_MODEL, bk=512,
                             name="mm_dw_mem_out")
    dom = _matmul(dy_mem, W["w_mem_out"], M=S, N=MEM_WIDTH, K=D_MODEL, mode="nt", bm=512, bn=MEM_WIDTH, bk=D_MODEL,
                  name="mm_dom")
    dqm, dk_mem, dv_mem = _mem_attn_bwd(proj, kv, om, lse_mem, dom, S=S)
    dkv = jnp.concatenate([dk_mem, dv_mem], axis=1)
    G["w_mem_kv"] = _matmul(mem_n, dkv, M=D_MODEL, N=2 * MEM_WIDTH, K=N_MEM, mode="tn", bm=D_MODEL, bn=2 * MEM_WIDTH,
                            bk=N_MEM, name="mm_dw_kv")
    dmem_n = _matmul(dkv, W["w_mem_kv"], M=N_MEM, N=D_MODEL, K=2 * MEM_WIDTH, mode="nt", bm=N_MEM, bn=D_MODEL,
                     bk=2 * MEM_WIDTH, name="mm_dmem")
    _, Gs["g_mem"] = _rmsnorm_bwd(mem, P["g_mem"], dmem_n, None, rows=N_MEM, name="norm_mem_bwd", want_dx=False)

    G["w_dil_out"] = _matmul(o_dil, dy_dil, M=256, N=D_MODEL, K=S, mode="tn", bm=256, bn=D_MODEL, bk=512, name="mm_dw_dil_out")
    do_dil = _matmul(dy_dil, W["w_dil_out"], M=S, N=256, K=D_MODEL, mode="nt", bm=512, bn=256, bk=D_MODEL, name="mm_do_dil")
    do_h = _to_heads(do_dil, 1, F32)
    delta = _row_dot(do_h, o_dil_h, S=S)
    dq_parts, dk_parts, dv_parts, dbias = [], [], [], []
    for g, (window, dil) in enumerate(DIL_GROUPS):
        qh, kh, vh = qkv_heads[g]
        do_g = _to_heads(do_dil, dil, MXU_DTYPE)
        dq_g, dk_g, dv_g, db_g = _dil_attn_bwd(qh, kh, vh, do_g, _stat_to_dilated(lse_dil, dil), _stat_to_dilated(delta, dil),
                                               bias, g, S=S, blocks_per_seq=S // dil // SPAN)
        dq_parts.append(_from_heads(dq_g, dil, MXU_DTYPE))
        dk_parts.append(_from_heads(dk_g, dil, MXU_DTYPE))
        dv_parts.append(_from_heads(dv_g, dil, MXU_DTYPE))
        dbias.append(db_g)
    drel = _dil_bias_bwd(jnp.concatenate(dbias, axis=0), buckets)
    Gs["rel_bias"] = drel[:, :3 * DIL_HEADS]

    G["w_lru_out"] = _matmul(z_lru, dy_lru, M=D_RNN, N=D_MODEL, K=S, mode="tn", bm=D_RNN, bn=D_MODEL, bk=512, name="mm_dw_lru_out")
    dz = _matmul(dy_lru, W["w_lru_out"], M=S, N=D_RNN, K=D_MODEL, mode="nt", bm=512, bn=D_RNN, bk=D_MODEL, name="mm_dz_lru")
    dxl, dgl, dcw, dcb, dwa, dwx, dba, dbx, dlam = _lru_bwd(proj, hl, dz, *lru_args, S=S)
    Gs["conv_w"], Gs["conv_b"] = dcw, dcb
    Gs["w_rg_a"], Gs["w_rg_x"] = _block_diag_extract(dwa), _block_diag_extract(dwx)
    Gs["b_rg_a"], Gs["b_rg_x"], Gs["lru_lambda"] = dba, dbx, dlam

    dproj = jnp.concatenate([dxl, dgl] + dq_parts + dk_parts + dv_parts + [dqm, dg0, dg1, dg2], axis=1)
    G["w_in"] = _matmul(h, dproj, M=D_MODEL, N=D_IN, K=S, mode="tn", bm=512, bn=D_IN // 2, bk=256, name="mm_dw_in")
    dh = _matmul(dproj, W["w_in"], M=S, N=D_MODEL, K=D_IN, mode="nt", bm=512, bn=D_MODEL, bk=D_IN // 2, name="mm_dh")
    grad_x, Gs["g_mix"] = _rmsnorm_bwd(x, P["g_mix"], dh, dx1, rows=S, name="norm_mix_bwd")
    return loss, grad_x, G, Gs


BIG = ("w_in", "w_lru_out", "w_dil_out", "w_mem_kv", "w_mem_out", "w_out", "w_mlp_in", "w_mlp_out")
COL_SHARDED = ("w_in", "w_lru_out", "w_dil_out", "w_mem_out", "w_mlp_in")
SMALL = ("g_mix", "b_gate", "conv_b", "w_rg_a", "b_rg_a", "w_rg_x", "b_rg_x", "lru_lambda", "rel_bias", "g_mem",
         "g_mlp", "g_final")
WEIGHTS = ("g_mix", "w_in", "b_gate", "conv_w", "conv_b", "w_rg_a", "b_rg_a", "w_rg_x", "b_rg_x", "lru_lambda",
           "w_lru_out", "rel_bias", "w_dil_out", "g_mem", "w_mem_kv", "w_mem_out", "w_out", "g_mlp", "w_mlp_in",
           "w_mlp_out", "g_final")
SMALL_SHAPES = {"g_mix": (1024,), "b_gate": (3072,), "conv_b": (768,), "w_rg_a": (12, 64, 64), "b_rg_a": (768,),
                "w_rg_x": (12, 64, 64), "b_rg_x": (768,), "lru_lambda": (768,), "rel_bias": (32, 12), "g_mem": (1024,),
                "g_mlp": (1024,), "g_final": (1024,)}


def _gathered_to_full(name, gathered):
    if name in COL_SHARDED:
        n, r, c = gathered.shape
        return gathered.transpose(1, 0, 2).reshape(r, n * c)
    n, r, c = gathered.shape
    return gathered.reshape(n * r, c)


def _full_to_parts(name, full):
    if name in COL_SHARDED:
        r, nc = full.shape
        return _mx(full.reshape(r, N_DEV, nc // N_DEV).transpose(1, 0, 2))
    nr, c = full.shape
    return _mx(full.reshape(N_DEV, nr // N_DEV, c))


def _pack(parts):
    flat = jnp.concatenate([p.reshape(-1).astype(F32) for p in parts])
    pad = (-flat.shape[0]) % 1024
    return jnp.pad(flat, (0, pad)).reshape(-1, 128)


def _unpack(pack, shapes):
    flat = pack.reshape(-1)
    out, off = [], 0
    for shp in shapes:
        size = math.prod(shp)
        out.append(flat[off:off + size].reshape(shp))
        off += size
    return out


def kernel(x, mem, g_mix, w_in, b_gate, conv_w, conv_b, w_rg_a, b_rg_a, w_rg_x, b_rg_x, lru_lambda, w_lru_out, rel_bias, w_dil_out, g_mem, w_mem_kv, w_mem_out, w_out, g_mlp, w_mlp_in, w_mlp_out, g_final, loss_target, m_g_mix, m_w_in, m_b_gate, m_conv_w, m_conv_b, m_w_rg_a, m_b_rg_a, m_w_rg_x, m_b_rg_x, m_lru_lambda, m_w_lru_out, m_rel_bias, m_w_dil_out, m_g_mem, m_w_mem_kv, m_w_mem_out, m_w_out, m_g_mlp, m_w_mlp_in, m_w_mlp_out, m_g_final, v_g_mix, v_w_in, v_b_gate, v_conv_w, v_conv_b, v_w_rg_a, v_b_rg_a, v_w_rg_x, v_b_rg_x, v_lru_lambda, v_w_lru_out, v_rel_bias, v_w_dil_out, v_g_mem, v_w_mem_kv, v_w_mem_out, v_w_out, v_g_mlp, v_w_mlp_in, v_w_mlp_out, v_g_final):
    w = dict(g_mix=g_mix, w_in=w_in, b_gate=b_gate, conv_w=conv_w, conv_b=conv_b, w_rg_a=w_rg_a, b_rg_a=b_rg_a,
             w_rg_x=w_rg_x, b_rg_x=b_rg_x, lru_lambda=lru_lambda, w_lru_out=w_lru_out, rel_bias=rel_bias,
             w_dil_out=w_dil_out, g_mem=g_mem, w_mem_kv=w_mem_kv, w_mem_out=w_mem_out, w_out=w_out, g_mlp=g_mlp,
             w_mlp_in=w_mlp_in, w_mlp_out=w_mlp_out, g_final=g_final)
    m = dict(g_mix=m_g_mix, w_in=m_w_in, b_gate=m_b_gate, conv_w=m_conv_w, conv_b=m_conv_b, w_rg_a=m_w_rg_a,
             b_rg_a=m_b_rg_a, w_rg_x=m_w_rg_x, b_rg_x=m_b_rg_x, lru_lambda=m_lru_lambda, w_lru_out=m_w_lru_out,
             rel_bias=m_rel_bias, w_dil_out=m_w_dil_out, g_mem=m_g_mem, w_mem_kv=m_w_mem_kv, w_mem_out=m_w_mem_out,
             w_out=m_w_out, g_mlp=m_g_mlp, w_mlp_in=m_w_mlp_in, w_mlp_out=m_w_mlp_out, g_final=m_g_final)
    v = dict(g_mix=v_g_mix, w_in=v_w_in, b_gate=v_b_gate, conv_w=v_conv_w, conv_b=v_conv_b, w_rg_a=v_w_rg_a,
             b_rg_a=v_b_rg_a, w_rg_x=v_w_rg_x, b_rg_x=v_b_rg_x, lru_lambda=v_lru_lambda, w_lru_out=v_w_lru_out,
             rel_bias=v_rel_bias, w_dil_out=v_w_dil_out, g_mem=v_g_mem, w_mem_kv=v_w_mem_kv, w_mem_out=v_w_mem_out,
             w_out=v_w_out, g_mlp=v_g_mlp, w_mlp_in=v_w_mlp_in, w_mlp_out=v_w_mlp_out, g_final=v_g_final)

    gathered = _all_gather([_mx(w[n]) for n in BIG] + [w["conv_w"]])
    W = {n: _gathered_to_full(n, gathered[i]) for i, n in enumerate(BIG)}
    W["conv_w"] = gathered[-1].transpose(1, 0, 2).reshape(CONV_WIDTH, D_RNN)
    P = {n: w[n] for n in SMALL}

    loss, grad_x, G, Gs = _local_step(x[0], mem[0], loss_target[0], W, P)

    landed = _exchange_partials([_full_to_parts(n, G[n]) for n in BIG])
    small_names = SMALL + ("conv_w",)
    pack = _pack([Gs[n] for n in small_names] + [loss])
    total = _all_reduce_small(pack)
    small_shapes = [SMALL_SHAPES[n] for n in SMALL] + [(CONV_WIDTH, D_RNN), (1,)]
    summed = dict(zip(small_names + ("loss",), _unpack(total, small_shapes)))

    grads, deltas, new_m, new_v = {}, {}, {}, {}
    for i, n in enumerate(BIG):
        grads[n], deltas[n], new_m[n], new_v[n] = _adamw_landed(w[n], landed[i], m[n], v[n], name=f"adamw_{n}")
    zeros_cw = jnp.zeros((CONV_WIDTH, D_RNN), F32)
    w_pack = _pack([w[n] for n in SMALL] + [zeros_cw, jnp.zeros((1,), F32)])
    m_pack = _pack([m[n] for n in SMALL] + [zeros_cw, jnp.zeros((1,), F32)])
    v_pack = _pack([v[n] for n in SMALL] + [zeros_cw, jnp.zeros((1,), F32)])
    d_pack, nm_pack, nv_pack = _adamw_plain(w_pack, total, m_pack, v_pack, name="adamw_small")
    for dst, pk in ((deltas, d_pack), (new_m, nm_pack), (new_v, nv_pack)):
        dst.update(zip(SMALL, _unpack(pk, [SMALL_SHAPES[n] for n in SMALL])))
    for n in SMALL:
        grads[n] = summed[n]
    my_idx = _dev_index(_my_pos())
    cw_cols = D_RNN // N_DEV
    grads["conv_w"] = lax.dynamic_slice(summed["conv_w"], (0, my_idx * cw_cols), (CONV_WIDTH, cw_cols))
    deltas["conv_w"], new_m["conv_w"], new_v["conv_w"] = _adamw_plain(
        w["conv_w"], grads["conv_w"], m["conv_w"], v["conv_w"], name="adamw_conv_w")

    return (summed["loss"].reshape(()), grad_x[None], *[grads[n] for n in WEIGHTS], *[deltas[n] for n in WEIGHTS],
            *[new_m[n] for n in WEIGHTS], *[new_v[n] for n in WEIGHTS])
```

```python
import functools
import math

import jax
import jax.numpy as jnp
from jax import lax
from jax.experimental import pallas as pl
from jax.experimental.pallas import tpu as pltpu

F32 = jnp.float32
MXU_DTYPE = jnp.bfloat16
VMEM_LIMIT_BYTES = 56 * 1024 * 1024
N_DEV = 8

D_MODEL = 1024
N_MEM = 256
MEM_HEADS = 4
MEM_HEAD_DIM = 128
MEM_WIDTH = 512
D_RNN = 768
LRU_BLOCK = 64
N_LRU_BLOCKS = 12
LRU_GROUP = 256
N_LRU_GROUPS = 3
CONV_WIDTH = 4
LRU_C = 8.0
DIL_GROUPS = ((128, 1), (512, 4), (2048, 16))
SPAN = 128
DIL_HEADS = 4
DIL_HEAD_DIM = 64
NUM_BUCKETS = 32
MAX_DISTANCE = 2048
D_FF = 4096
D_IN = 7424
EPS = 1e-6
NEG = -1e30
C_XL, C_GATE, C_QKV, C_QM, C_GATES = 0, 768, 1536, 3840, 4352

ADAM_LR = 0.001
ADAM_B1 = 0.9
ADAM_B2 = 0.999
ADAM_EPS = 1e-08
ADAM_WD = 0.01
ADAM_STEP = 10

MESH = pl.DeviceIdType.MESH
GELU_K = math.sqrt(2.0 / math.pi)


def _cparams(sem=None):
    kw = dict(vmem_limit_bytes=VMEM_LIMIT_BYTES)
    if sem is not None:
        kw["dimension_semantics"] = sem
    return pltpu.CompilerParams(**kw)


def _mx(v):
    return v.astype(MXU_DTYPE)


def _dot(a, b, mode="nn"):
    dims = {"nn": (((1,), (0,)), ((), ())), "nt": (((1,), (1,)), ((), ())), "tn": (((0,), (0,)), ((), ()))}[mode]
    return lax.dot_general(_mx(a), _mx(b), dims, preferred_element_type=F32)


def _colsum(v):
    return jnp.sum(v, axis=0, keepdims=True)


def _matmul(a, b, *, M, N, K, mode, bm, bn, bk, name, out_dtypes=(F32,), epilogue=None, extras=(),
            a_off=(0, 0), b_off=(0, 0), j_outer=False):
    assert M % bm == 0 and N % bn == 0 and K % bk == 0, (name, M, N, K, bm, bn, bk)
    nm, nn, nk = M // bm, N // bn, K // bk

    def ij(f):
        if j_outer:
            return lambda j, i, k: f(i, j, k)
        return f

    if mode == "tn":
        a_spec = pl.BlockSpec((bk, bm), ij(lambda i, j, k: (k + a_off[0], i + a_off[1])))
    else:
        a_spec = pl.BlockSpec((bm, bk), ij(lambda i, j, k: (i + a_off[0], k + a_off[1])))
    if mode == "nt":
        b_spec = pl.BlockSpec((bn, bk), ij(lambda i, j, k: (j + b_off[0], k + b_off[1])))
    else:
        b_spec = pl.BlockSpec((bk, bn), ij(lambda i, j, k: (k + b_off[0], j + b_off[1])))
    ex_specs = [pl.BlockSpec((bm, bn), ij(functools.partial(lambda i, j, k, o: (i + o[0], j + o[1]), o=off)))
                for _, off in extras]
    out_spec = pl.BlockSpec((bm, bn), ij(lambda i, j, k: (i, j)))
    n_ex, n_out = len(extras), len(out_dtypes)

    def body(*refs):
        a_ref, b_ref = refs[0], refs[1]
        ex = refs[2:2 + n_ex]
        outs = refs[2 + n_ex:2 + n_ex + n_out]
        part = _dot(a_ref[...], b_ref[...], mode)

        def finish(acc):
            vals = epilogue(acc, *[e[...] for e in ex]) if epilogue is not None else (acc,)
            for o, v in zip(outs, vals):
                o[...] = v.astype(o.dtype)

        if nk == 1:
            finish(part)
        else:
            acc_ref = refs[-1]
            k = pl.program_id(2)

            @pl.when(k == 0)
            def _():
                acc_ref[...] = part

            @pl.when(k > 0)
            def _():
                acc_ref[...] += part

            @pl.when(k == nk - 1)
            def _():
                finish(acc_ref[...])

    grid = (nn, nm, nk) if j_outer else (nm, nn, nk)
    res = pl.pallas_call(
        body, name=name, grid=grid,
        in_specs=[a_spec, b_spec] + ex_specs,
        out_specs=[out_spec] * n_out,
        out_shape=[jax.ShapeDtypeStruct((M, N), dt) for dt in out_dtypes],
        scratch_shapes=[pltpu.VMEM((bm, bn), F32)] if nk > 1 else [],
        compiler_params=_cparams(("parallel", "parallel", "arbitrary")),
    )(a, b, *[e for e, _ in extras])
    return res[0] if n_out == 1 else res


def _rmsnorm_fwd(x, g, *, rows, name, bt=512):
    bt = min(bt, rows)

    def body(x_ref, g_ref, o_ref):
        xv = x_ref[...]
        r = lax.rsqrt(jnp.mean(xv * xv, axis=-1, keepdims=True) + EPS)
        o_ref[...] = (xv * r * g_ref[...]).astype(o_ref.dtype)

    return pl.pallas_call(
        body, name=name, grid=(rows // bt,),
        in_specs=[pl.BlockSpec((bt, D_MODEL), lambda i: (i, 0)), pl.BlockSpec((1, D_MODEL), lambda i: (0, 0))],
        out_specs=pl.BlockSpec((bt, D_MODEL), lambda i: (i, 0)),
        out_shape=jax.ShapeDtypeStruct((rows, D_MODEL), MXU_DTYPE),
        compiler_params=_cparams(("parallel",)),
    )(x, g.reshape(1, D_MODEL))


def _rms_bwd_tile(xv, gv, dyv):
    r = lax.rsqrt(jnp.mean(xv * xv, axis=-1, keepdims=True) + EPS)
    w = dyv * gv
    dx = r * w - xv * (r * r * r) * jnp.mean(w * xv, axis=-1, keepdims=True)
    dg = _colsum(dyv * xv * r)
    return dx, dg


def _rmsnorm_bwd(x, g, dy, res, *, rows, name, bt=512, want_dx=True):
    bt = min(bt, rows)
    has_res = res is not None

    def body(*refs):
        x_ref, g_ref, dy_ref = refs[:3]
        res_ref = refs[3] if has_res else None
        outs = refs[3 + int(has_res):]
        dx, dg = _rms_bwd_tile(x_ref[...], g_ref[...], dy_ref[...])
        if has_res:
            dx = dx + res_ref[...]
        dg_ref = outs[-1]

        @pl.when(pl.program_id(0) == 0)
        def _():
            dg_ref[...] = jnp.zeros_like(dg_ref)

        dg_ref[...] += dg
        if want_dx:
            outs[0][...] = dx

    row_spec = pl.BlockSpec((bt, D_MODEL), lambda i: (i, 0))
    vec_spec = pl.BlockSpec((1, D_MODEL), lambda i: (0, 0))
    ins = [x, g.reshape(1, D_MODEL), dy] + ([res] if has_res else [])
    out_shape = ([jax.ShapeDtypeStruct((rows, D_MODEL), F32)] if want_dx else []) + [jax.ShapeDtypeStruct((1, D_MODEL), F32)]
    out_specs = ([row_spec] if want_dx else []) + [vec_spec]
    res_ = pl.pallas_call(
        body, name=name, grid=(rows // bt,),
        in_specs=[row_spec, vec_spec, row_spec] + ([row_spec] if has_res else []),
        out_specs=out_specs, out_shape=out_shape,
        compiler_params=_cparams(("arbitrary",)),
    )(*ins)
    return res_ if want_dx else (None, res_[0])


def _loss_head(x2, g, tgt, *, rows, bt=512):
    def body(x_ref, g_ref, t_ref, loss_ref, dx_ref, dg_ref):
        xv, gv = x_ref[...], g_ref[...]
        r = lax.rsqrt(jnp.mean(xv * xv, axis=-1, keepdims=True) + EPS)
        diff = xv * r * gv - t_ref[...]
        part = jnp.sum(jnp.mean(diff * diff, axis=-1, keepdims=True), axis=0, keepdims=True) * 0.5
        dx, dg = _rms_bwd_tile(xv, gv, diff * (1.0 / D_MODEL))

        @pl.when(pl.program_id(0) == 0)
        def _():
            loss_ref[...] = jnp.zeros_like(loss_ref)
            dg_ref[...] = jnp.zeros_like(dg_ref)

        loss_ref[...] += part
        dg_ref[...] += dg
        dx_ref[...] = dx

    row_spec = pl.BlockSpec((bt, D_MODEL), lambda i: (i, 0))
    vec_spec = pl.BlockSpec((1, D_MODEL), lambda i: (0, 0))
    return pl.pallas_call(
        body, name="loss_head", grid=(rows // bt,),
        in_specs=[row_spec, vec_spec, row_spec],
        out_specs=[pl.BlockSpec((1, 1), lambda i: (0, 0)), row_spec, vec_spec],
        out_shape=[jax.ShapeDtypeStruct((1, 1), F32), jax.ShapeDtypeStruct((rows, D_MODEL), F32),
                   jax.ShapeDtypeStruct((1, D_MODEL), F32)],
        compiler_params=_cparams(("arbitrary",)),
    )(x2, g.reshape(1, D_MODEL), tgt)


LRU_T = 256


def _gelu(x):
    t = jnp.tanh(GELU_K * (x + 0.044715 * x * x * x))
    return 0.5 * x * (1.0 + t), t


def _gelu_grad(x, t):
    return 0.5 * (1.0 + t) + 0.5 * x * (1.0 - t * t) * GELU_K * (1.0 + 3.0 * 0.044715 * x * x)


def _softplus_neg(lam):
    z = -lam
    u = jnp.exp(-jnp.abs(z))
    w = 1.0 + u
    l1p = jnp.where(w == 1.0, u, jnp.log(w) * u / jnp.where(w == 1.0, 1.0, w - 1.0))
    return jnp.maximum(z, 0.0) + l1p


def _shift_down(cur, prev8, k, row8):
    y = pltpu.roll(cur, k, 0)
    head = jnp.where(row8 < k, pltpu.roll(prev8, k, 0), y[0:8])
    return jnp.concatenate([head, y[8:]], axis=0)


def _shift_up(cur, next8, k, row8):
    n = cur.shape[0]
    y = pltpu.roll(cur, n - k, 0)
    tail = jnp.where(row8 >= 8 - k, pltpu.roll(next8, 8 - k, 0), y[n - 8:n])
    return jnp.concatenate([y[0:n - 8], tail], axis=0)


def _lru_gates(xl, p8, cw, cb, wa, wx, ba, bx, lam, row8):
    sh = [xl] + [_shift_down(xl, p8, k, row8) for k in (1, 2, 3)]
    xc = cb + cw[3:4] * sh[0] + cw[2:3] * sh[1] + cw[1:2] * sh[2] + cw[0:1] * sh[3]
    r = jax.nn.sigmoid(_dot(xc, wa) + ba)
    i = jax.nn.sigmoid(_dot(xc, wx) + bx)
    sp = _softplus_neg(lam)
    la = -LRU_C * r * sp
    a = jnp.exp(la)
    mult = jnp.sqrt(jnp.tanh(-la) * (a * a + 1.0))
    return dict(sh=sh, xc=xc, r=r, i=i, sp=sp, a=a, mult=mult)


def _lru_specs(n_t, reverse):
    T = LRU_T
    tt = (lambda t: n_t - 1 - t) if reverse else (lambda t: t)
    blk = lambda col0: pl.BlockSpec((T, LRU_GROUP), lambda g, t: (tt(t), col0 + g))
    prev8 = lambda col0: pl.BlockSpec((8, LRU_GROUP), lambda g, t: (jnp.maximum(tt(t) * (T // 8) - 1, 0), col0 + g))
    vec = lambda rows: pl.BlockSpec((rows, LRU_GROUP), lambda g, t: (0, g))
    wbd = pl.BlockSpec((1, LRU_GROUP, LRU_GROUP), lambda g, t: (g, 0, 0))
    return blk, prev8, vec, wbd


def _lru_fwd(proj, conv_w, conv_b, wa_bd, wx_bd, b_a, b_x, lam, *, S):
    T = LRU_T
    n_t = S // T
    blk, _, vec, wbd = _lru_specs(n_t, False)

    def body(xl_ref, gate_ref, cw_ref, cb_ref, wa_ref, wx_ref, ba_ref, bx_ref, lam_ref,
             hl_ref, z_ref, prev8, hcar, a_s, b_s):
        @pl.when(pl.program_id(1) == 0)
        def _():
            prev8[...] = jnp.zeros_like(prev8)
            hcar[...] = jnp.zeros_like(hcar)

        row8 = lax.broadcasted_iota(jnp.int32, (8, LRU_GROUP), 0)
        xl = xl_ref[...]
        q = _lru_gates(xl, prev8[...], cw_ref[...], cb_ref[...], wa_ref[0], wx_ref[0], ba_ref[...], bx_ref[...],
                       lam_ref[...], row8)
        prev8[...] = xl[T - 8:T]
        a_s[...] = q["a"]
        b_s[...] = q["mult"] * q["i"] * q["xc"]

        def step(c, carry):
            off = pl.multiple_of(c * 8, 8)
            A = a_s[pl.ds(off, 8), :]
            B = b_s[pl.ds(off, 8), :]
            for k in (1, 2, 4):
                a_sh = jnp.where(row8 >= k, pltpu.roll(A, k, 0), 1.0)
                b_sh = jnp.where(row8 >= k, pltpu.roll(B, k, 0), 0.0)
                B = A * b_sh + B
                A = A * a_sh
            h = A * carry + B
            hl_ref[pl.ds(off, 8), :] = h
            return h[7:8, :]

        hcar[...] = lax.fori_loop(0, T // 8, step, hcar[...])
        ge, _ = _gelu(gate_ref[...])
        z_ref[...] = (ge * hl_ref[...]).astype(z_ref.dtype)

    return pl.pallas_call(
        body, name="lru_fwd", grid=(N_LRU_GROUPS, n_t),
        in_specs=[blk(C_XL // LRU_GROUP), blk(C_GATE // LRU_GROUP), vec(4), vec(1), wbd, wbd, vec(1), vec(1), vec(1)],
        out_specs=[blk(0), blk(0)],
        out_shape=[jax.ShapeDtypeStruct((S, D_RNN), F32), jax.ShapeDtypeStruct((S, D_RNN), MXU_DTYPE)],
        scratch_shapes=[pltpu.VMEM((8, LRU_GROUP), F32), pltpu.VMEM((1, LRU_GROUP), F32),
                        pltpu.VMEM((T, LRU_GROUP), F32), pltpu.VMEM((T, LRU_GROUP), F32)],
        compiler_params=_cparams(("parallel", "arbitrary")),
    )(proj, proj, conv_w, conv_b, wa_bd, wx_bd, b_a, b_x, lam)


def _lru_bwd(proj, hl, dz, conv_w, conv_b, wa_bd, wx_bd, b_a, b_x, lam, *, S):
    T = LRU_T
    n_t = S // T
    blk, prev8s, vec, wbd = _lru_specs(n_t, True)

    def body(xl_ref, xlp_ref, gate_ref, hl_ref, hlp_ref, dz_ref, cw_ref, cb_ref, wa_ref, wx_ref, ba_ref, bx_ref,
             lam_ref, dxl_ref, dgate_ref, dcw_ref, dcb_ref, dwa_ref, dwx_ref, dba_ref, dbx_ref, dlam_ref,
             next8, gcar, c_s, b_s, l_s):
        t = pl.program_id(1)
        first_chunk = t == n_t - 1

        @pl.when(t == 0)
        def _():
            next8[...] = jnp.zeros_like(next8)
            gcar[...] = jnp.zeros_like(gcar)
            for ref in (dcw_ref, dcb_ref, dwa_ref, dwx_ref, dba_ref, dbx_ref, dlam_ref):
                ref[...] = jnp.zeros_like(ref)

        row8 = lax.broadcasted_iota(jnp.int32, (8, LRU_GROUP), 0)
        rowT = lax.broadcasted_iota(jnp.int32, (T, LRU_GROUP), 0)
        keep = jnp.where(first_chunk, 0.0, 1.0)
        xl = xl_ref[...]
        wa, wx, lam_v = wa_ref[0], wx_ref[0], lam_ref[...]
        q = _lru_gates(xl, xlp_ref[...] * keep, cw_ref[...], cb_ref[...], wa, wx, ba_ref[...], bx_ref[...], lam_v, row8)
        a, mult, r, i, xc, sp = q["a"], q["mult"], q["r"], q["i"], q["xc"], q["sp"]
        hl_v = hl_ref[...]
        dz_v = dz_ref[...]
        gate = gate_ref[...]
        ge, th = _gelu(gate)
        dgate_ref[...] = (dz_v * hl_v * _gelu_grad(gate, th)).astype(dgate_ref.dtype)

        c_s[...] = jnp.where(rowT == T - 1, 0.0, pltpu.roll(a, T - 1, 0))
        b_s[...] = dz_v * ge + jnp.where(rowT == T - 1, gcar[...], 0.0)

        def step(n, carry):
            off = pl.multiple_of((T // 8 - 1 - n) * 8, 8)
            C = c_s[pl.ds(off, 8), :]
            B = b_s[pl.ds(off, 8), :]
            for k in (1, 2, 4):
                c_sh = jnp.where(row8 < 8 - k, pltpu.roll(C, 8 - k, 0), 1.0)
                b_sh = jnp.where(row8 < 8 - k, pltpu.roll(B, 8 - k, 0), 0.0)
                B = B + C * b_sh
                C = C * c_sh
            lam_t = B + C * carry
            l_s[pl.ds(off, 8), :] = lam_t
            return lam_t[0:1, :]

        lax.fori_loop(0, T // 8, step, jnp.zeros((1, LRU_GROUP), F32))
        lmb = l_s[...]
        gcar[...] = a[0:1, :] * lmb[0:1, :]

        h_prev = _shift_down(hl_v, hlp_ref[...] * keep, 1, row8)
        da = lmb * h_prev
        dmult = lmb * i * xc
        di = lmb * mult * xc
        dxc = lmb * mult * i
        dla = da * a - dmult * (a * a) / mult
        dr = dla * (-LRU_C * sp)
        dlam_ref[...] += _colsum(dla * (-LRU_C * r)) * (-jax.nn.sigmoid(-lam_v))
        dpa = dr * r * (1.0 - r)
        dpx = di * i * (1.0 - i)
        dxc = dxc + _dot(dpa, wa, "nt") + _dot(dpx, wx, "nt")
        dwa_ref[0] += _dot(xc, dpa, "tn")
        dwx_ref[0] += _dot(xc, dpx, "tn")
        dba_ref[...] += _colsum(dpa)
        dbx_ref[...] += _colsum(dpx)
        dcb_ref[...] += _colsum(dxc)
        cw = cw_ref[...]
        n8 = next8[...]
        dxl = cw[3:4] * dxc
        for k in (1, 2, 3):
            dxl = dxl + cw[3 - k:4 - k] * _shift_up(dxc, n8, k, row8)
        for k in range(4):
            dcw_ref[3 - k:4 - k, :] += _colsum(dxc * q["sh"][k])
        next8[...] = dxc[0:8]
        dxl_ref[...] = dxl.astype(dxl_ref.dtype)

    res = pl.pallas_call(
        body, name="lru_bwd", grid=(N_LRU_GROUPS, n_t),
        in_specs=[blk(C_XL // LRU_GROUP), prev8s(C_XL // LRU_GROUP), blk(C_GATE // LRU_GROUP), blk(0), prev8s(0), blk(0),
                  vec(4), vec(1), wbd, wbd, vec(1), vec(1), vec(1)],
        out_specs=[blk(0), blk(0), vec(4), vec(1), wbd, wbd, vec(1), vec(1), vec(1)],
        out_shape=[jax.ShapeDtypeStruct((S, D_RNN), MXU_DTYPE), jax.ShapeDtypeStruct((S, D_RNN), MXU_DTYPE),
                   jax.ShapeDtypeStruct((4, D_RNN), F32), jax.ShapeDtypeStruct((1, D_RNN), F32),
                   jax.ShapeDtypeStruct((N_LRU_GROUPS, LRU_GROUP, LRU_GROUP), F32),
                   jax.ShapeDtypeStruct((N_LRU_GROUPS, LRU_GROUP, LRU_GROUP), F32),
                   jax.ShapeDtypeStruct((1, D_RNN), F32), jax.ShapeDtypeStruct((1, D_RNN), F32),
                   jax.ShapeDtypeStruct((1, D_RNN), F32)],
        scratch_shapes=[pltpu.VMEM((8, LRU_GROUP), F32), pltpu.VMEM((1, LRU_GROUP), F32),
                        pltpu.VMEM((T, LRU_GROUP), F32), pltpu.VMEM((T, LRU_GROUP), F32), pltpu.VMEM((T, LRU_GROUP), F32)],
        compiler_params=_cparams(("parallel", "arbitrary")),
    )(proj, proj, proj, hl, hl, dz, conv_w, conv_b, wa_bd, wx_bd, b_a, b_x, lam)
    return res


def _block_diag(w):
    w4 = w.reshape(N_LRU_GROUPS, 4, LRU_BLOCK, 1, LRU_BLOCK)
    eye = jnp.eye(4, dtype=w.dtype).reshape(1, 4, 1, 4, 1)
    return (w4 * eye).reshape(N_LRU_GROUPS, LRU_GROUP, LRU_GROUP)


def _block_diag_extract(wbd):
    w5 = wbd.reshape(N_LRU_GROUPS, 4, LRU_BLOCK, 4, LRU_BLOCK)
    return jnp.stack([w5[:, a, :, a, :] for a in range(4)], axis=1).reshape(N_LRU_BLOCKS, LRU_BLOCK, LRU_BLOCK)


def _t5_bucket(dist):
    max_exact = NUM_BUCKETS // 2
    df = jnp.maximum(dist, 1).astype(jnp.float32)
    large = max_exact + (jnp.log(df / max_exact) / math.log(MAX_DISTANCE / max_exact)
                         * (NUM_BUCKETS - max_exact)).astype(jnp.int32)
    large = jnp.minimum(large, NUM_BUCKETS - 1)
    return jnp.where(dist < max_exact, dist, large)


def _band_offsets():
    qi = jnp.arange(SPAN)[:, None]
    kj = jnp.arange(2 * SPAN)[None, :]
    return qi + SPAN - kj


def _dil_buckets():
    off = _band_offsets()
    return jnp.stack([_t5_bucket(jnp.maximum(off, 0) * dil) for _, dil in DIL_GROUPS]).astype(jnp.int32)


def _dil_bias(rel_bias, buckets):
    def body(tbl_ref, bk_ref, o_ref):
        g = pl.program_id(0)
        qi = lax.broadcasted_iota(jnp.int32, (SPAN, 2 * SPAN), 0)
        kj = lax.broadcasted_iota(jnp.int32, (SPAN, 2 * SPAN), 1)
        off = qi + SPAN - kj
        valid = (off >= 0) & (off <= SPAN)
        bk = bk_ref[0]
        for h in range(DIL_HEADS):
            acc = jnp.zeros((SPAN, 2 * SPAN), F32)
            for b in range(NUM_BUCKETS):
                acc = jnp.where(bk == b, tbl_ref[b, g * DIL_HEADS + h], acc)
            o_ref[0, h] = jnp.where(valid, acc, NEG)

    return pl.pallas_call(
        body, name="dil_bias", grid=(3,),
        in_specs=[pl.BlockSpec(memory_space=pltpu.SMEM), pl.BlockSpec((1, SPAN, 2 * SPAN), lambda g: (g, 0, 0))],
        out_specs=pl.BlockSpec((1, DIL_HEADS, SPAN, 2 * SPAN), lambda g: (g, 0, 0, 0)),
        out_shape=jax.ShapeDtypeStruct((3, DIL_HEADS, SPAN, 2 * SPAN), F32),
        compiler_params=_cparams(("parallel",)),
    )(rel_bias, buckets)


def _dil_bias_bwd(dbias, buckets):
    def body(db_ref, bk_ref, o_ref):
        lane = lax.broadcasted_iota(jnp.int32, (1, 128), 1)
        rows = [jnp.zeros((1, 128), F32) for _ in range(NUM_BUCKETS)]
        for g in range(3):
            bk = bk_ref[g]
            for h in range(DIL_HEADS):
                d = db_ref[g, h]
                for b in range(NUM_BUCKETS):
                    tot = jnp.sum(_colsum(jnp.where(bk == b, d, 0.0)), axis=1, keepdims=True)
                    rows[b] = jnp.where(lane == g * DIL_HEADS + h, tot, rows[b])
        for b in range(NUM_BUCKETS):
            o_ref[b:b + 1, :] = rows[b]

    return pl.pallas_call(
        body, name="dil_bias_bwd",
        out_shape=jax.ShapeDtypeStruct((NUM_BUCKETS, 128), F32),
        compiler_params=_cparams(),
    )(dbias, buckets)


def _dil_scores(q, kcat, bias, first):
    s = _dot(q, kcat, "nt") * (DIL_HEAD_DIM ** -0.5) + bias
    kj = lax.broadcasted_iota(jnp.int32, s.shape, 1)
    return jnp.where(kj < jnp.where(first, SPAN, 0), NEG, s)


def _dil_attn_fwd(q, k, v, bias, g, *, S, blocks_per_seq):
    nb = S // SPAN
    hd = pl.BlockSpec((DIL_HEADS, SPAN, DIL_HEAD_DIM), lambda i: (0, i, 0))
    hd_prev = pl.BlockSpec((DIL_HEADS, SPAN, DIL_HEAD_DIM), lambda i: (0, jnp.maximum(i - 1, 0), 0))
    st = pl.BlockSpec((DIL_HEADS, SPAN, 1), lambda i: (0, i, 0))

    def body(q_ref, kp_ref, kc_ref, vp_ref, vc_ref, b_ref, o_ref, lse_ref):
        first = (pl.program_id(0) % blocks_per_seq) == 0
        for h in range(DIL_HEADS):
            kcat = jnp.concatenate([kp_ref[h], kc_ref[h]], axis=0)
            vcat = jnp.concatenate([vp_ref[h], vc_ref[h]], axis=0)
            s = _dil_scores(q_ref[h], kcat, b_ref[0, h], first)
            m = jnp.max(s, axis=-1, keepdims=True)
            p = jnp.exp(s - m)
            den = jnp.sum(p, axis=-1, keepdims=True)
            o_ref[h] = _dot(p, vcat) / den
            lse_ref[h] = m + jnp.log(den)

    return pl.pallas_call(
        body, name=f"dil_attn_fwd{g}", grid=(nb,),
        in_specs=[hd, hd_prev, hd, hd_prev, hd,
                  pl.BlockSpec((1, DIL_HEADS, SPAN, 2 * SPAN), lambda i: (g, 0, 0, 0))],
        out_specs=[hd, st],
        out_shape=[jax.ShapeDtypeStruct((DIL_HEADS, S, DIL_HEAD_DIM), F32), jax.ShapeDtypeStruct((DIL_HEADS, S, 1), F32)],
        compiler_params=_cparams(("parallel",)),
    )(q, k, k, v, v, bias)


def _dil_attn_bwd(q, k, v, do, lse, delta, bias, g, *, S, blocks_per_seq):
    nb = S // SPAN
    cl = lambda i: jnp.minimum(i, nb - 1)
    hd = pl.BlockSpec((DIL_HEADS, SPAN, DIL_HEAD_DIM), lambda i: (0, cl(i), 0))
    hd_prev = pl.BlockSpec((DIL_HEADS, SPAN, DIL_HEAD_DIM), lambda i: (0, jnp.maximum(cl(i) - 1, 0), 0))
    st = pl.BlockSpec((DIL_HEADS, SPAN, 1), lambda i: (0, cl(i), 0))
    kv_out = pl.BlockSpec((DIL_HEADS, SPAN, DIL_HEAD_DIM), lambda i: (0, jnp.maximum(i - 1, 0), 0))
    scale = DIL_HEAD_DIM ** -0.5

    def body(q_ref, kp_ref, kc_ref, vp_ref, vc_ref, do_ref, lse_ref, dl_ref, b_ref,
             dq_ref, dk_ref, dv_ref, db_ref, kcar, vcar):
        i = pl.program_id(0)

        @pl.when(i == 0)
        def _():
            kcar[...] = jnp.zeros_like(kcar)
            vcar[...] = jnp.zeros_like(vcar)
            db_ref[...] = jnp.zeros_like(db_ref)

        @pl.when(i < nb)
        def _():
            first = (i % blocks_per_seq) == 0
            for h in range(DIL_HEADS):
                kcat = jnp.concatenate([kp_ref[h], kc_ref[h]], axis=0)
                vcat = jnp.concatenate([vp_ref[h], vc_ref[h]], axis=0)
                qh, doh = q_ref[h], do_ref[h]
                p = jnp.exp(_dil_scores(qh, kcat, b_ref[0, h], first) - lse_ref[h])
                ds = p * (_dot(doh, vcat, "nt") - dl_ref[h])
                db_ref[0, h] += ds
                dq_ref[h] = (_dot(ds, kcat) * scale).astype(dq_ref.dtype)
                dkc = _dot(ds, qh, "tn") * scale
                dvc = _dot(p, doh, "tn")
                dk_ref[h] = (kcar[h] + dkc[0:SPAN]).astype(dk_ref.dtype)
                dv_ref[h] = (vcar[h] + dvc[0:SPAN]).astype(dv_ref.dtype)
                kcar[h] = dkc[SPAN:2 * SPAN]
                vcar[h] = dvc[SPAN:2 * SPAN]

        @pl.when(i == nb)
        def _():
            dk_ref[...] = kcar[...].astype(dk_ref.dtype)
            dv_ref[...] = vcar[...].astype(dv_ref.dtype)

    hshape = jax.ShapeDtypeStruct((DIL_HEADS, S, DIL_HEAD_DIM), MXU_DTYPE)
    bspec = pl.BlockSpec((1, DIL_HEADS, SPAN, 2 * SPAN), lambda i: (g, 0, 0, 0))
    return pl.pallas_call(
        body, name=f"dil_attn_bwd{g}", grid=(nb + 1,),
        in_specs=[hd, hd_prev, hd, hd_prev, hd, hd, st, st, bspec],
        out_specs=[hd, kv_out, kv_out, pl.BlockSpec((1, DIL_HEADS, SPAN, 2 * SPAN), lambda i: (0, 0, 0, 0))],
        out_shape=[hshape, hshape, hshape, jax.ShapeDtypeStruct((1, DIL_HEADS, SPAN, 2 * SPAN), F32)],
        scratch_shapes=[pltpu.VMEM((DIL_HEADS, SPAN, DIL_HEAD_DIM), F32), pltpu.VMEM((DIL_HEADS, SPAN, DIL_HEAD_DIM), F32)],
        compiler_params=_cparams(("arbitrary",)),
    )(q, k, k, v, v, do, lse, delta, bias)


def _dil_merge(os_, lses, *, S, bt=512):
    hd = pl.BlockSpec((DIL_HEADS, bt, DIL_HEAD_DIM), lambda i: (0, i, 0))
    st = pl.BlockSpec((DIL_HEADS, bt, 1), lambda i: (0, i, 0))

    def body(o0, o1, o2, l0, l1, l2, o_ref, lse_ref):
        a0, a1, a2 = l0[...], l1[...], l2[...]
        m = jnp.maximum(jnp.maximum(a0, a1), a2)
        e0, e1, e2 = jnp.exp(a0 - m), jnp.exp(a1 - m), jnp.exp(a2 - m)
        tot = e0 + e1 + e2
        o_ref[...] = (e0 / tot) * o0[...] + (e1 / tot) * o1[...] + (e2 / tot) * o2[...]
        lse_ref[...] = m + jnp.log(tot)

    return pl.pallas_call(
        body, name="dil_merge", grid=(S // bt,),
        in_specs=[hd, hd, hd, st, st, st], out_specs=[hd, st],
        out_shape=[jax.ShapeDtypeStruct((DIL_HEADS, S, DIL_HEAD_DIM), F32), jax.ShapeDtypeStruct((DIL_HEADS, S, 1), F32)],
        compiler_params=_cparams(("parallel",)),
    )(*os_, *lses)


def _row_dot(a, b, *, S, bt=512):
    hd = pl.BlockSpec((DIL_HEADS, bt, DIL_HEAD_DIM), lambda i: (0, i, 0))
    st = pl.BlockSpec((DIL_HEADS, bt, 1), lambda i: (0, i, 0))

    def body(a_ref, b_ref, o_ref):
        o_ref[...] = jnp.sum(a_ref[...] * b_ref[...], axis=-1, keepdims=True)

    return pl.pallas_call(
        body, name="dil_delta", grid=(S // bt,), in_specs=[hd, hd], out_specs=st,
        out_shape=jax.ShapeDtypeStruct((DIL_HEADS, S, 1), F32), compiler_params=_cparams(("parallel",)),
    )(a, b)


def _to_heads(t, dil, dtype):
    S = t.shape[0]
    L = S // dil
    return t.reshape(L, dil, DIL_HEADS, DIL_HEAD_DIM).transpose(2, 1, 0, 3).reshape(DIL_HEADS, S, DIL_HEAD_DIM).astype(dtype)


def _from_heads(t, dil, dtype):
    S = t.shape[1]
    L = S // dil
    return t.reshape(DIL_HEADS, dil, L, DIL_HEAD_DIM).transpose(2, 1, 0, 3).reshape(S, DIL_HEADS * DIL_HEAD_DIM).astype(dtype)


def _stat_to_natural(t, dil):
    S = t.shape[1]
    L = S // dil
    return t.reshape(DIL_HEADS, dil, L).transpose(0, 2, 1).reshape(DIL_HEADS, S, 1)


def _stat_to_dilated(t, dil):
    S = t.shape[1]
    L = S // dil
    return t.reshape(DIL_HEADS, L, dil).transpose(0, 2, 1).reshape(DIL_HEADS, S, 1)


MEM_T = 512
QM_BLK = C_QM // MEM_HEAD_DIM


def _mem_attn_fwd(proj, kv, *, S):
    scale = MEM_HEAD_DIM ** -0.5

    def body(q_ref, k_ref, v_ref, o_ref, lse_ref):
        s = _dot(q_ref[...], k_ref[...], "nt") * scale
        m = jnp.max(s, axis=-1, keepdims=True)
        p = jnp.exp(s - m)
        den = jnp.sum(p, axis=-1, keepdims=True)
        o_ref[...] = _dot(p, v_ref[...]) / den
        lse_ref[0] = m + jnp.log(den)

    return pl.pallas_call(
        body, name="mem_attn_fwd", grid=(S // MEM_T, MEM_HEADS),
        in_specs=[pl.BlockSpec((MEM_T, MEM_HEAD_DIM), lambda i, h: (i, QM_BLK + h)),
                  pl.BlockSpec((N_MEM, MEM_HEAD_DIM), lambda i, h: (0, h)),
                  pl.BlockSpec((N_MEM, MEM_HEAD_DIM), lambda i, h: (0, MEM_HEADS + h))],
        out_specs=[pl.BlockSpec((MEM_T, MEM_HEAD_DIM), lambda i, h: (i, h)),
                   pl.BlockSpec((1, MEM_T, 1), lambda i, h: (h, i, 0))],
        out_shape=[jax.ShapeDtypeStruct((S, MEM_WIDTH), F32), jax.ShapeDtypeStruct((MEM_HEADS, S, 1), F32)],
        compiler_params=_cparams(("parallel", "parallel")),
    )(proj, kv, kv)


def _mem_attn_bwd(proj, kv, om, lse, dom, *, S):
    scale = MEM_HEAD_DIM ** -0.5

    def body(q_ref, k_ref, v_ref, o_ref, lse_ref, do_ref, dq_ref, dk_ref, dv_ref):
        @pl.when(pl.program_id(1) == 0)
        def _():
            dk_ref[...] = jnp.zeros_like(dk_ref)
            dv_ref[...] = jnp.zeros_like(dv_ref)

        qv, kv_, vv, dov = q_ref[...], k_ref[...], v_ref[...], do_ref[...]
        p = jnp.exp(_dot(qv, kv_, "nt") * scale - lse_ref[0])
        delta = jnp.sum(dov * o_ref[...], axis=-1, keepdims=True)
        ds = p * (_dot(dov, vv, "nt") - delta)
        dq_ref[...] = (_dot(ds, kv_) * scale).astype(dq_ref.dtype)
        dk_ref[...] += _dot(ds, qv, "tn") * scale
        dv_ref[...] += _dot(p, dov, "tn")

    tile = pl.BlockSpec((MEM_T, MEM_HEAD_DIM), lambda h, i: (i, h))
    kvo = pl.BlockSpec((N_MEM, MEM_HEAD_DIM), lambda h, i: (0, h))
    return pl.pallas_call(
        body, name="mem_attn_bwd", grid=(MEM_HEADS, S // MEM_T),
        in_specs=[pl.BlockSpec((MEM_T, MEM_HEAD_DIM), lambda h, i: (i, QM_BLK + h)),
                  pl.BlockSpec((N_MEM, MEM_HEAD_DIM), lambda h, i: (0, h)),
                  pl.BlockSpec((N_MEM, MEM_HEAD_DIM), lambda h, i: (0, MEM_HEADS + h)),
                  tile, pl.BlockSpec((1, MEM_T, 1), lambda h, i: (h, i, 0)), tile],
        out_specs=[tile, kvo, kvo],
        out_shape=[jax.ShapeDtypeStruct((S, MEM_WIDTH), MXU_DTYPE), jax.ShapeDtypeStruct((N_MEM, MEM_WIDTH), F32),
                   jax.ShapeDtypeStruct((N_MEM, MEM_WIDTH), F32)],
        compiler_params=_cparams(("parallel", "arbitrary")),
    )(proj, kv, kv, om, lse, dom)


MIX_BM = 1024
MIX_BN = 256
GATES_BLK = C_GATES // MIX_BN


def _mix_specs(j_outer):
    ix = (lambda f: (lambda j, i: f(i, j))) if j_outer else (lambda f: f)
    act = lambda width: pl.BlockSpec((MIX_BM, width), ix(lambda i, j: (i, 0)))
    wgt = lambda width: pl.BlockSpec((width, MIX_BN), ix(lambda i, j: (0, j)))
    gate = lambda b: pl.BlockSpec((MIX_BM, MIX_BN), ix(lambda i, j: (i, GATES_BLK + 4 * b + j)))
    bias = lambda b: pl.BlockSpec((1, MIX_BN), ix(lambda i, j: (0, 4 * b + j)))
    tile = pl.BlockSpec((MIX_BM, MIX_BN), ix(lambda i, j: (i, j)))
    return act, wgt, gate, bias, tile


def _mix_fwd(z_lru, o_dil, om, w_lru, w_dil, w_mem, proj, b_gate, *, S):
    act, wgt, gate, bias, tile = _mix_specs(False)

    def body(zl, od, mo, wl, wd, wm, g0, g1, g2, b0, b1, b2, o_ref):
        acc = jax.nn.sigmoid(g0[...] + b0[...]) * _dot(zl[...], wl[...])
        acc += jax.nn.sigmoid(g1[...] + b1[...]) * _dot(od[...], wd[...])
        acc += jax.nn.sigmoid(g2[...] + b2[...]) * _dot(mo[...], wm[...])
        o_ref[...] = acc.astype(o_ref.dtype)

    return pl.pallas_call(
        body, name="mix_fwd", grid=(S // MIX_BM, D_MODEL // MIX_BN),
        in_specs=[act(D_RNN), act(256), act(MEM_WIDTH), wgt(D_RNN), wgt(256), wgt(MEM_WIDTH),
                  gate(0), gate(1), gate(2), bias(0), bias(1), bias(2)],
        out_specs=tile, out_shape=jax.ShapeDtypeStruct((S, D_MODEL), MXU_DTYPE),
        compiler_params=_cparams(("parallel", "parallel")),
    )(z_lru, o_dil, om, w_lru, w_dil, w_mem, proj, proj, proj, b_gate, b_gate, b_gate)


def _mix_bwd(dmerged, z_lru, o_dil, om, w_lru, w_dil, w_mem, proj, b_gate, *, S):
    act, wgt, gate, bias, tile = _mix_specs(True)

    def body(dm, zl, od, mo, wl, wd, wm, g0, g1, g2, b0, b1, b2,
             dg0, dg1, dg2, dy0, dy1, dy2, db0, db1, db2):
        @pl.when(pl.program_id(1) == 0)
        def _():
            for r in (db0, db1, db2):
                r[...] = jnp.zeros_like(r)

        dmv = dm[...]
        for act_ref, w_ref, g_ref, b_ref, dg_ref, dy_ref, db_ref in (
                (zl, wl, g0, b0, dg0, dy0, db0), (od, wd, g1, b1, dg1, dy1, db1), (mo, wm, g2, b2, dg2, dy2, db2)):
            y = _dot(act_ref[...], w_ref[...])
            gt = jax.nn.sigmoid(g_ref[...] + b_ref[...])
            dgate = dmv * y * gt * (1.0 - gt)
            dg_ref[...] = dgate.astype(dg_ref.dtype)
            dy_ref[...] = (dmv * gt).astype(dy_ref.dtype)
            db_ref[...] += _colsum(dgate)

    big = jax.ShapeDtypeStruct((S, D_MODEL), MXU_DTYPE)
    vec = jax.ShapeDtypeStruct((1, D_MODEL), F32)
    vspec = pl.BlockSpec((1, MIX_BN), lambda j, i: (0, j))
    return pl.pallas_call(
        body, name="mix_bwd", grid=(D_MODEL // MIX_BN, S // MIX_BM),
        in_specs=[tile, act(D_RNN), act(256), act(MEM_WIDTH), wgt(D_RNN), wgt(256), wgt(MEM_WIDTH),
                  gate(0), gate(1), gate(2), bias(0), bias(1), bias(2)],
        out_specs=[tile] * 6 + [vspec] * 3, out_shape=[big] * 6 + [vec] * 3,
        compiler_params=_cparams(("parallel", "arbitrary")),
    )(dmerged, z_lru, o_dil, om, w_lru, w_dil, w_mem, proj, proj, proj, b_gate, b_gate, b_gate)


def _adamw_math(w, g, m, v):
    m = ADAM_B1 * m + (1.0 - ADAM_B1) * g
    v = ADAM_B2 * v + (1.0 - ADAM_B2) * (g * g)
    m_hat = m / (1.0 - ADAM_B1 ** ADAM_STEP)
    v_hat = v / (1.0 - ADAM_B2 ** ADAM_STEP)
    delta = -ADAM_LR * (m_hat / (jnp.sqrt(v_hat) + ADAM_EPS) + ADAM_WD * w)
    return delta, m, v


def _adamw_landed(w, own, land, m, v, *, name):
    R, C = w.shape
    n_parts = land.shape[0]
    br = min(R, 256)
    tile = pl.BlockSpec((br, C), lambda i: (i, 0))

    def body(w_ref, o_ref, l_ref, m_ref, v_ref, g_ref, d_ref, nm_ref, nv_ref):
        g = o_ref[...]
        for p in range(n_parts):
            g = g + l_ref[p].astype(F32)
        d, nm, nv = _adamw_math(w_ref[...], g, m_ref[...], v_ref[...])
        g_ref[...] = g
        d_ref[...] = d
        nm_ref[...] = nm
        nv_ref[...] = nv

    return pl.pallas_call(
        body, name=name, grid=(R // br,),
        in_specs=[tile, tile, pl.BlockSpec((n_parts, br, C), lambda i: (0, i, 0)), tile, tile],
        out_specs=[tile] * 4, out_shape=[jax.ShapeDtypeStruct((R, C), F32)] * 4,
        compiler_params=_cparams(("parallel",)),
    )(w, own, land, m, v)


def _adamw_plain(w, g, m, v, *, name):
    def body(w_ref, g_ref, m_ref, v_ref, d_ref, nm_ref, nv_ref):
        d, nm, nv = _adamw_math(w_ref[...], g_ref[...], m_ref[...], v_ref[...])
        d_ref[...] = d
        nm_ref[...] = nm
        nv_ref[...] = nv

    return pl.pallas_call(
        body, name=name, out_shape=[jax.ShapeDtypeStruct(w.shape, F32)] * 3, compiler_params=_cparams(),
    )(w, g, m, v)


def _my_pos():
    return lax.axis_index("x"), lax.axis_index("y"), lax.axis_index("c")


def _dev_index(p):
    return 4 * p[0] + 2 * p[1] + p[2]


def _all_gather(shards):
    n = len(shards)
    hbm = pl.BlockSpec(memory_space=pl.ANY)

    def body(*refs):
        ins, outs = refs[:n], refs[n:2 * n]
        send_sems, recv_sems, local_sems = refs[2 * n:]
        x, y, c = _my_pos()
        me, sibling = (x, y, c), (x, y, 1 - c)
        chips = [(1 - x, y), (x, 1 - y), (1 - x, 1 - y)]

        def copy(a, k, block, to, src=None):
            dst = outs[a].at[_dev_index(block)]
            return pltpu.make_async_remote_copy(
                src_ref=dst if src is None else src, dst_ref=dst,
                send_sem=send_sems.at[a, k], recv_sem=recv_sems.at[a, k], device_id=to, device_id_type=MESH)

        mine = [pltpu.make_async_copy(ins[a], outs[a].at[_dev_index(me)], local_sems.at[a]) for a in range(n)]
        for cp in mine:
            cp.start()
        first = []
        for a in range(n):
            first.append(copy(a, 0, me, sibling, src=ins[a]))
            first += [copy(a, 1 + j, me, (*chip, c), src=ins[a]) for j, chip in enumerate(chips)]
        for cp in first:
            cp.start()
        passed = []
        for j, chip in enumerate(chips):
            for a in range(n):
                copy(a, 1 + j, (*chip, c), me).wait_recv()
                fwd = copy(a, 4 + j, (*chip, c), sibling)
                fwd.start()
                passed.append(fwd)
        for a in range(n):
            copy(a, 0, sibling, me).wait_recv()
        for j, chip in enumerate(chips):
            for a in range(n):
                copy(a, 4 + j, (*chip, 1 - c), me).wait_recv()
        for cp in first + passed:
            cp.wait_send()
        for cp in mine:
            cp.wait()

    return pl.pallas_call(
        body, name="all_gather_weights",
        in_specs=[hbm] * n, out_specs=[hbm] * n,
        out_shape=[jax.ShapeDtypeStruct((N_DEV,) + s.shape, s.dtype) for s in shards],
        scratch_shapes=[pltpu.SemaphoreType.DMA((n, 7)), pltpu.SemaphoreType.DMA((n, 7)), pltpu.SemaphoreType.DMA((n,))],
        compiler_params=pltpu.CompilerParams(has_side_effects=True),
    )(*shards)


def _peers(me):
    x, y, c = me
    out = []
    for k in range(1, 8):
        fx, fy, fc = (k >> 2) & 1, (k >> 1) & 1, k & 1
        out.append((k - 1, (1 - x if fx else x, 1 - y if fy else y, 1 - c if fc else c)))
    return out


HBM_SPEC = pl.BlockSpec(memory_space=pltpu.HBM)
SEM_SPEC = pl.BlockSpec(memory_space=pltpu.SEMAPHORE)
DATAFLOW_EFFECT = pltpu.SideEffectType.DATAFLOW_SIDE_EFFECTING


def _gather_refs(src, land, me, peer, k):
    return src, land.at[_dev_index(me)]


def _scatter_refs(src, land, me, peer, k):
    return src.at[_dev_index(peer)], land.at[k]


def _push_start(srcs, land_shapes, refs_of, name):
    n = len(srcs)

    def body(*refs):
        ins, lands = refs[:n], refs[n:2 * n]
        send_sems, recv_sems, token = refs[2 * n], refs[2 * n + 1], refs[-1]
        me = _my_pos()
        for k, peer in _peers(me):
            for a in range(n):
                src, dst = refs_of(ins[a], lands[a], me, peer, k)
                pltpu.make_async_remote_copy(src_ref=src, dst_ref=dst, send_sem=send_sems.at[7 * a + k],
                                             recv_sem=recv_sems.at[7 * a + k], device_id=peer, device_id_type=MESH).start()
        token[...] = jnp.zeros_like(token)

    lands = [lax.empty(shp, s.dtype) for shp, s in zip(land_shapes, srcs)]
    hbm = lambda a: pltpu.with_memory_space_constraint(a, pltpu.HBM)
    res = pl.pallas_call(
        body, name=name,
        out_shape=(pltpu.SemaphoreType.DMA((7 * n,)), pltpu.SemaphoreType.DMA((7 * n,)),
                   *[pltpu.HBM(s.shape, s.dtype) for s in srcs], *[pltpu.HBM(l.shape, l.dtype) for l in lands],
                   jax.ShapeDtypeStruct((8, 128), F32)),
        in_specs=[HBM_SPEC] * (2 * n),
        out_specs=(SEM_SPEC, SEM_SPEC, *[HBM_SPEC] * (2 * n), pl.BlockSpec(memory_space=pltpu.VMEM)),
        input_output_aliases={i: 2 + i for i in range(2 * n)},
        compiler_params=pltpu.CompilerParams(has_side_effects=DATAFLOW_EFFECT),
    )(*[hbm(s) for s in srcs], *[hbm(l) for l in lands])
    return dict(sems=(res[0], res[1]), srcs=list(res[2:2 + n]), lands=list(res[2 + n:2 + 2 * n]), token=res[-1], n=n,
                refs_of=refs_of, name=name)


def _push_wait(started, after):
    n, refs_of = started["n"], started["refs_of"]

    def body(*refs):
        ins, lands = refs[:n], refs[n:2 * n]
        send_sems, recv_sems = refs[2 * n], refs[2 * n + 1]
        me = _my_pos()
        for k, peer in _peers(me):
            for a in range(n):
                src, dst = refs_of(ins[a], lands[a], me, peer, k)
                cp = pltpu.make_async_remote_copy(src_ref=src, dst_ref=dst, send_sem=send_sems.at[7 * a + k],
                                                  recv_sem=recv_sems.at[7 * a + k], device_id=peer, device_id_type=MESH)
                cp.wait_send()
                cp.wait_recv()

    arrs = started["srcs"] + started["lands"]
    res = pl.pallas_call(
        body, name=started["name"].replace("start", "wait"),
        out_shape=tuple(pltpu.HBM(a.shape, a.dtype) for a in arrs),
        in_specs=[HBM_SPEC] * (2 * n) + [SEM_SPEC, SEM_SPEC, pl.BlockSpec(memory_space=pl.ANY)],
        out_specs=tuple([HBM_SPEC] * (2 * n)),
        input_output_aliases={i: i for i in range(2 * n)},
        compiler_params=pltpu.CompilerParams(has_side_effects=DATAFLOW_EFFECT),
    )(*arrs, *started["sems"], after)
    return list(res[n:2 * n])


def _all_reduce_small(pack):
    R = pack.shape[0]

    def body(in_ref, out_ref, land, send_sems, recv_sems):
        me = _my_pos()
        my_idx = _dev_index(me)
        land[my_idx] = in_ref[...]
        sent = []
        for k, peer in _peers(me):
            cp = pltpu.make_async_remote_copy(
                src_ref=in_ref, dst_ref=land.at[my_idx], send_sem=send_sems.at[k], recv_sem=recv_sems.at[k],
                device_id=peer, device_id_type=MESH)
            cp.start()
            sent.append(cp)
        for k, peer in _peers(me):
            pltpu.make_async_remote_copy(
                src_ref=in_ref, dst_ref=land.at[_dev_index(peer)], send_sem=send_sems.at[k], recv_sem=recv_sems.at[k],
                device_id=peer, device_id_type=MESH).wait_recv()
        for cp in sent:
            cp.wait_send()
        acc = land[0]
        for d in range(1, N_DEV):
            acc = acc + land[d]
        out_ref[...] = acc

    return pl.pallas_call(
        body, name="all_reduce_small",
        in_specs=[pl.BlockSpec(memory_space=pltpu.VMEM)], out_specs=pl.BlockSpec(memory_space=pltpu.VMEM),
        out_shape=jax.ShapeDtypeStruct(pack.shape, F32),
        scratch_shapes=[pltpu.VMEM((N_DEV, R, 128), F32), pltpu.SemaphoreType.DMA((7,)), pltpu.SemaphoreType.DMA((7,))],
        compiler_params=pltpu.CompilerParams(has_side_effects=True, vmem_limit_bytes=VMEM_LIMIT_BYTES),
    )(pack)


def _local_step(x, mem, tgt, W, P, late_weights, send_grads, tie0):
    S = x.shape[0]
    W = dict(W)
    h = _rmsnorm_fwd(x, P["g_mix"] + tie0, rows=S, name="norm_mix")
    proj = _matmul(h, W["w_in"], M=S, N=D_IN, K=D_MODEL, mode="nn", bm=512, bn=D_IN // 2, bk=D_MODEL, name="mm_in",
                   j_outer=True)

    wa_bd, wx_bd = _mx(_block_diag(P["w_rg_a"])), _mx(_block_diag(P["w_rg_x"]))
    lru_args = (W["conv_w"], P["conv_b"].reshape(1, -1), wa_bd, wx_bd, P["b_rg_a"].reshape(1, -1),
                P["b_rg_x"].reshape(1, -1), P["lru_lambda"].reshape(1, -1))
    hl, z_lru = _lru_fwd(proj, *lru_args, S=S)

    buckets = _dil_buckets()
    bias = _dil_bias(P["rel_bias"], buckets)
    qkv_heads, o_nat, lse_nat = [], [], []
    for g, (window, dil) in enumerate(DIL_GROUPS):
        cols = [C_QKV + three * 768 + g * 256 for three in range(3)]
        qh, kh, vh = [_to_heads(proj[:, c0:c0 + 256], dil, MXU_DTYPE) for c0 in cols]
        qkv_heads.append((qh, kh, vh))
        o_g, lse_g = _dil_attn_fwd(qh, kh, vh, bias, g, S=S, blocks_per_seq=S // dil // SPAN)
        o_nat.append(_to_heads(_from_heads(o_g, dil, F32), 1, F32))
        lse_nat.append(_stat_to_natural(lse_g, dil))
    o_dil_h, lse_dil = _dil_merge(o_nat, lse_nat, S=S)
    o_dil = _from_heads(o_dil_h, 1, MXU_DTYPE)

    W.update(late_weights(o_dil_h))
    mem_n = _rmsnorm_fwd(mem, P["g_mem"], rows=N_MEM, name="norm_mem")
    kv = _matmul(mem_n, W["w_mem_kv"], M=N_MEM, N=2 * MEM_WIDTH, K=D_MODEL, mode="nn", bm=N_MEM, bn=512, bk=D_MODEL,
                 name="mm_kv")
    om, lse_mem = _mem_attn_fwd(proj, kv, S=S)
    b_gate = P["b_gate"].reshape(1, -1)
    merged = _mix_fwd(z_lru, o_dil, om, W["w_lru_out"], W["w_dil_out"], W["w_mem_out"], proj, b_gate, S=S)
    x1 = _matmul(merged, W["w_out"], M=S, N=D_MODEL, K=D_MODEL, mode="nn", bm=512, bn=D_MODEL, bk=D_MODEL, name="mm_out",
                 epilogue=lambda acc, r: (r + acc,), extras=[(x, (0, 0))])
    hm = _rmsnorm_fwd(x1, P["g_mlp"], rows=S, name="norm_mlp")

    def relu2(acc):
        rl = jnp.maximum(acc, 0.0)
        return acc, rl * rl

    u, act = _matmul(hm, W["w_mlp_in"], M=S, N=D_FF, K=D_MODEL, mode="nn", bm=512, bn=1024, bk=D_MODEL, name="mm_mlp_in",
                     out_dtypes=(F32, MXU_DTYPE), epilogue=relu2, j_outer=True)
    x2 = _matmul(act, W["w_mlp_out"], M=S, N=D_MODEL, K=D_FF, mode="nn", bm=512, bn=D_MODEL, bk=1024, name="mm_mlp_out",
                 epilogue=lambda acc, r: (r + acc,), extras=[(x1, (0, 0))])
    loss, dx2, dg_final = _loss_head(x2, P["g_final"], tgt, rows=S)

    G, Gs = {}, {}
    Gs["g_final"] = dg_final
    G["w_mlp_out"] = _matmul(act, dx2, M=D_FF, N=D_MODEL, K=S, mode="tn", bm=1024, bn=D_MODEL, bk=512, name="mm_dw_mlp_out")
    du = _matmul(dx2, W["w_mlp_out"], M=S, N=D_FF, K=D_MODEL, mode="nt", bm=512, bn=1024, bk=D_MODEL, name="mm_du",
                 out_dtypes=(MXU_DTYPE,), epilogue=lambda acc, uu: (acc * 2.0 * jnp.maximum(uu, 0.0),),
                 extras=[(u, (0, 0))], j_outer=True)
    G["w_mlp_in"] = _matmul(hm, du, M=D_MODEL, N=D_FF, K=S, mode="tn", bm=D_MODEL, bn=1024, bk=512, name="mm_dw_mlp_in")
    tie1 = send_grads({n: G.pop(n) for n in ("w_mlp_out", "w_mlp_in")})
    dhm = _matmul(du, W["w_mlp_in"], M=S, N=D_MODEL, K=D_FF, mode="nt", bm=512, bn=D_MODEL, bk=1024, name="mm_dhm")
    dx1, Gs["g_mlp"] = _rmsnorm_bwd(x1, P["g_mlp"] + tie1, dhm, dx2, rows=S, name="norm_mlp_bwd")
    G["w_out"] = _matmul(merged, dx1, M=D_MODEL, N=D_MODEL, K=S, mode="tn", bm=D_MODEL, bn=D_MODEL, bk=512, name="mm_dw_out")
    dmerged = _matmul(dx1, W["w_out"], M=S, N=D_MODEL, K=D_MODEL, mode="nt", bm=512, bn=D_MODEL, bk=D_MODEL, name="mm_dmerged")
    (dg0, dg1, dg2, dy_lru, dy_dil, dy_mem, db0, db1, db2) = _mix_bwd(
        dmerged, z_lru, o_dil, om, W["w_lru_out"], W["w_dil_out"], W["w_mem_out"], proj, b_gate, S=S)
    Gs["b_gate"] = jnp.concatenate([db0, db1, db2], axis=1)

    G["w_mem_out"] = _matmul(om, dy_mem, M=MEM_WIDTH, N=D_MODEL, K=S, mode="tn", bm=MEM_WIDTH, bn=D_MODEL, bk=512,
                             name="mm_dw_mem_out")
    dom = _matmul(dy_mem, W["w_mem_out"], M=S, N=MEM_WIDTH, K=D_MODEL, mode="nt", bm=512, bn=MEM_WIDTH, bk=D_MODEL,
                  name="mm_dom")
    dqm, dk_mem, dv_mem = _mem_attn_bwd(proj, kv, om, lse_mem, dom, S=S)
    dkv = jnp.concatenate([dk_mem, dv_mem], axis=1)
    G["w_mem_kv"] = _matmul(mem_n, dkv, M=D_MODEL, N=2 * MEM_WIDTH, K=N_MEM, mode="tn", bm=D_MODEL, bn=2 * MEM_WIDTH,
                            bk=N_MEM, name="mm_dw_kv")
    dmem_n = _matmul(dkv, W["w_mem_kv"], M=N_MEM, N=D_MODEL, K=2 * MEM_WIDTH, mode="nt", bm=N_MEM, bn=D_MODEL,
                     bk=2 * MEM_WIDTH, name="mm_dmem")
    _, Gs["g_mem"] = _rmsnorm_bwd(mem, P["g_mem"], dmem_n, None, rows=N_MEM, name="norm_mem_bwd", want_dx=False)

    G["w_dil_out"] = _matmul(o_dil, dy_dil, M=256, N=D_MODEL, K=S, mode="tn", bm=256, bn=D_MODEL, bk=512, name="mm_dw_dil_out")
    do_dil = _matmul(dy_dil, W["w_dil_out"], M=S, N=256, K=D_MODEL, mode="nt", bm=512, bn=256, bk=D_MODEL, name="mm_do_dil")
    G["w_lru_out"] = _matmul(z_lru, dy_lru, M=D_RNN, N=D_MODEL, K=S, mode="tn", bm=D_RNN, bn=D_MODEL, bk=512, name="mm_dw_lru_out")
    dz = _matmul(dy_lru, W["w_lru_out"], M=S, N=D_RNN, K=D_MODEL, mode="nt", bm=512, bn=D_RNN, bk=D_MODEL, name="mm_dz_lru")
    tie2 = send_grads({n: G.pop(n) for n in ("w_out", "w_mem_out", "w_mem_kv", "w_dil_out", "w_lru_out")})
    bias = bias + tie2
    do_h = _to_heads(do_dil, 1, F32)
    delta = _row_dot(do_h, o_dil_h, S=S)
    dq_parts, dk_parts, dv_parts, dbias = [], [], [], []
    for g, (window, dil) in enumerate(DIL_GROUPS):
        qh, kh, vh = qkv_heads[g]
        do_g = _to_heads(do_dil, dil, MXU_DTYPE)
        dq_g, dk_g, dv_g, db_g = _dil_attn_bwd(qh, kh, vh, do_g, _stat_to_dilated(lse_dil, dil), _stat_to_dilated(delta, dil),
                                               bias, g, S=S, blocks_per_seq=S // dil // SPAN)
        dq_parts.append(_from_heads(dq_g, dil, MXU_DTYPE))
        dk_parts.append(_from_heads(dk_g, dil, MXU_DTYPE))
        dv_parts.append(_from_heads(dv_g, dil, MXU_DTYPE))
        dbias.append(db_g)
    drel = _dil_bias_bwd(jnp.concatenate(dbias, axis=0), buckets)
    Gs["rel_bias"] = drel[:, :3 * DIL_HEADS]

    dxl, dgl, dcw, dcb, dwa, dwx, dba, dbx, dlam = _lru_bwd(proj, hl, dz, *lru_args, S=S)
    Gs["conv_w"], Gs["conv_b"] = dcw, dcb
    Gs["w_rg_a"], Gs["w_rg_x"] = _block_diag_extract(dwa), _block_diag_extract(dwx)
    Gs["b_rg_a"], Gs["b_rg_x"], Gs["lru_lambda"] = dba, dbx, dlam

    dproj = jnp.concatenate([dxl, dgl] + dq_parts + dk_parts + dv_parts + [dqm, dg0, dg1, dg2], axis=1)
    G["w_in"] = _matmul(h, dproj, M=D_MODEL, N=D_IN, K=S, mode="tn", bm=512, bn=D_IN // 2, bk=256, name="mm_dw_in")
    tie3 = send_grads({"w_in": G.pop("w_in")})
    dh = _matmul(dproj, W["w_in"], M=S, N=D_MODEL, K=D_IN, mode="nt", bm=512, bn=D_MODEL, bk=D_IN // 2, name="mm_dh")
    grad_x, Gs["g_mix"] = _rmsnorm_bwd(x, P["g_mix"] + tie3, dh, dx1, rows=S, name="norm_mix_bwd")
    return loss, grad_x, Gs


BIG = ("w_in", "w_lru_out", "w_dil_out", "w_mem_kv", "w_mem_out", "w_out", "w_mlp_in", "w_mlp_out")
COL_SHARDED = ("w_in", "w_lru_out", "w_dil_out", "w_mem_out", "w_mlp_in")
SMALL = ("g_mix", "b_gate", "conv_b", "w_rg_a", "b_rg_a", "w_rg_x", "b_rg_x", "lru_lambda", "rel_bias", "g_mem",
         "g_mlp", "g_final")
WEIGHTS = ("g_mix", "w_in", "b_gate", "conv_w", "conv_b", "w_rg_a", "b_rg_a", "w_rg_x", "b_rg_x", "lru_lambda",
           "w_lru_out", "rel_bias", "w_dil_out", "g_mem", "w_mem_kv", "w_mem_out", "w_out", "g_mlp", "w_mlp_in",
           "w_mlp_out", "g_final")
SMALL_SHAPES = {"g_mix": (1024,), "b_gate": (3072,), "conv_b": (768,), "w_rg_a": (12, 64, 64), "b_rg_a": (768,),
                "w_rg_x": (12, 64, 64), "b_rg_x": (768,), "lru_lambda": (768,), "rel_bias": (32, 12), "g_mem": (1024,),
                "g_mlp": (1024,), "g_final": (1024,)}


def _gathered_to_full(name, gathered):
    if name in COL_SHARDED:
        n, r, c = gathered.shape
        return gathered.transpose(1, 0, 2).reshape(r, n * c)
    n, r, c = gathered.shape
    return gathered.reshape(n * r, c)


def _full_to_parts(name, full):
    if name in COL_SHARDED:
        r, nc = full.shape
        return _mx(full.reshape(r, N_DEV, nc // N_DEV).transpose(1, 0, 2))
    nr, c = full.shape
    return _mx(full.reshape(N_DEV, nr // N_DEV, c))


def _own_part(name, full, my_idx):
    if name in COL_SHARDED:
        r, nc = full.shape
        return lax.dynamic_slice(full, (0, my_idx * (nc // N_DEV)), (r, nc // N_DEV))
    nr, c = full.shape
    return lax.dynamic_slice(full, (my_idx * (nr // N_DEV), 0), (nr // N_DEV, c))


def _pack(parts):
    flat = jnp.concatenate([p.reshape(-1).astype(F32) for p in parts])
    pad = (-flat.shape[0]) % 1024
    return jnp.pad(flat, (0, pad)).reshape(-1, 128)


def _unpack(pack, shapes):
    flat = pack.reshape(-1)
    out, off = [], 0
    for shp in shapes:
        size = math.prod(shp)
        out.append(flat[off:off + size].reshape(shp))
        off += size
    return out


def kernel(x, mem, g_mix, w_in, b_gate, conv_w, conv_b, w_rg_a, b_rg_a, w_rg_x, b_rg_x, lru_lambda, w_lru_out, rel_bias, w_dil_out, g_mem, w_mem_kv, w_mem_out, w_out, g_mlp, w_mlp_in, w_mlp_out, g_final, loss_target, m_g_mix, m_w_in, m_b_gate, m_conv_w, m_conv_b, m_w_rg_a, m_b_rg_a, m_w_rg_x, m_b_rg_x, m_lru_lambda, m_w_lru_out, m_rel_bias, m_w_dil_out, m_g_mem, m_w_mem_kv, m_w_mem_out, m_w_out, m_g_mlp, m_w_mlp_in, m_w_mlp_out, m_g_final, v_g_mix, v_w_in, v_b_gate, v_conv_w, v_conv_b, v_w_rg_a, v_b_rg_a, v_w_rg_x, v_b_rg_x, v_lru_lambda, v_w_lru_out, v_rel_bias, v_w_dil_out, v_g_mem, v_w_mem_kv, v_w_mem_out, v_w_out, v_g_mlp, v_w_mlp_in, v_w_mlp_out, v_g_final):
    w = dict(g_mix=g_mix, w_in=w_in, b_gate=b_gate, conv_w=conv_w, conv_b=conv_b, w_rg_a=w_rg_a, b_rg_a=b_rg_a,
             w_rg_x=w_rg_x, b_rg_x=b_rg_x, lru_lambda=lru_lambda, w_lru_out=w_lru_out, rel_bias=rel_bias,
             w_dil_out=w_dil_out, g_mem=g_mem, w_mem_kv=w_mem_kv, w_mem_out=w_mem_out, w_out=w_out, g_mlp=g_mlp,
             w_mlp_in=w_mlp_in, w_mlp_out=w_mlp_out, g_final=g_final)
    m = dict(g_mix=m_g_mix, w_in=m_w_in, b_gate=m_b_gate, conv_w=m_conv_w, conv_b=m_conv_b, w_rg_a=m_w_rg_a,
             b_rg_a=m_b_rg_a, w_rg_x=m_w_rg_x, b_rg_x=m_b_rg_x, lru_lambda=m_lru_lambda, w_lru_out=m_w_lru_out,
             rel_bias=m_rel_bias, w_dil_out=m_w_dil_out, g_mem=m_g_mem, w_mem_kv=m_w_mem_kv, w_mem_out=m_w_mem_out,
             w_out=m_w_out, g_mlp=m_g_mlp, w_mlp_in=m_w_mlp_in, w_mlp_out=m_w_mlp_out, g_final=m_g_final)
    v = dict(g_mix=v_g_mix, w_in=v_w_in, b_gate=v_b_gate, conv_w=v_conv_w, conv_b=v_conv_b, w_rg_a=v_w_rg_a,
             b_rg_a=v_b_rg_a, w_rg_x=v_w_rg_x, b_rg_x=v_b_rg_x, lru_lambda=v_lru_lambda, w_lru_out=v_w_lru_out,
             rel_bias=v_rel_bias, w_dil_out=v_w_dil_out, g_mem=v_g_mem, w_mem_kv=v_w_mem_kv, w_mem_out=v_w_mem_out,
             w_out=v_w_out, g_mlp=v_g_mlp, w_mlp_in=v_w_mlp_in, w_mlp_out=v_w_mlp_out, g_final=v_g_final)

    my_idx = _dev_index(_my_pos())

    g_in, g_cw = _all_gather([_mx(w["w_in"]), w["conv_w"]])
    W = {"w_in": _gathered_to_full("w_in", g_in), "conv_w": g_cw.transpose(1, 0, 2).reshape(CONV_WIDTH, D_RNN)}
    late = [n for n in BIG if n != "w_in"]
    late_shards = [_mx(w[n]) for n in late]
    late_started = _push_start(late_shards, [(N_DEV,) + s.shape for s in late_shards], _gather_refs,
                                "gather_late_start")
    P = {n: w[n] for n in SMALL}

    def late_weights(after):
        lands = _push_wait(late_started, after)
        out = {}
        for n, land, own in zip(late, lands, late_shards):
            full = lax.dynamic_update_index_in_dim(land, own, my_idx, 0)
            out[n] = _gathered_to_full(n, full)
        return out

    sent = []

    def send_grads(gs):
        names = list(gs)
        parts = [_full_to_parts(n, gs[n]) for n in names]
        own = [_own_part(n, gs[n], my_idx) for n in names]
        started = _push_start(parts, [(N_DEV - 1,) + p.shape[1:] for p in parts], _scatter_refs,
                              f"scatter{len(sent)}_start")
        sent.append((names, own, started))
        return started["token"][0, 0]

    loss, grad_x, Gs = _local_step(x[0], mem[0], loss_target[0], W, P, late_weights, send_grads,
                                   late_started["token"][0, 0])

    landed, own_part = {}, {}
    for names, own, started in sent:
        for n, o, land in zip(names, own, _push_wait(started, grad_x)):
            landed[n], own_part[n] = land, o
    small_names = SMALL + ("conv_w",)
    pack = _pack([Gs[n] for n in small_names] + [loss])
    total = _all_reduce_small(pack)
    small_shapes = [SMALL_SHAPES[n] for n in SMALL] + [(CONV_WIDTH, D_RNN), (1,)]
    summed = dict(zip(small_names + ("loss",), _unpack(total, small_shapes)))

    grads, deltas, new_m, new_v = {}, {}, {}, {}
    for n in BIG:
        grads[n], deltas[n], new_m[n], new_v[n] = _adamw_landed(w[n], own_part[n], landed[n], m[n], v[n],
                                                                name=f"adamw_{n}")
    zeros_cw = jnp.zeros((CONV_WIDTH, D_RNN), F32)
    w_pack = _pack([w[n] for n in SMALL] + [zeros_cw, jnp.zeros((1,), F32)])
    m_pack = _pack([m[n] for n in SMALL] + [zeros_cw, jnp.zeros((1,), F32)])
    v_pack = _pack([v[n] for n in SMALL] + [zeros_cw, jnp.zeros((1,), F32)])
    d_pack, nm_pack, nv_pack = _adamw_plain(w_pack, total, m_pack, v_pack, name="adamw_small")
    for dst, pk in ((deltas, d_pack), (new_m, nm_pack), (new_v, nv_pack)):
        dst.update(zip(SMALL, _unpack(pk, [SMALL_SHAPES[n] for n in SMALL])))
    for n in SMALL:
        grads[n] = summed[n]
    cw_cols = D_RNN // N_DEV
    grads["conv_w"] = lax.dynamic_slice(summed["conv_w"], (0, my_idx * cw_cols), (CONV_WIDTH, cw_cols))
    deltas["conv_w"], new_m["conv_w"], new_v["conv_w"] = _adamw_plain(
        w["conv_w"], grads["conv_w"], m["conv_w"], v["conv_w"], name="adamw_conv_w")

    return (summed["loss"].reshape(()), grad_x[None], *[grads[n] for n in WEIGHTS], *[deltas[n] for n in WEIGHTS],
            *[new_m[n] for n in WEIGHTS], *[new_v[n] for n in WEIGHTS])
```

```python
import functools
import math

import jax
import jax.numpy as jnp
from jax import lax
from jax.experimental import pallas as pl
from jax.experimental.pallas import tpu as pltpu

F32 = jnp.float32
MXU_DTYPE = jnp.bfloat16
VMEM_LIMIT_BYTES = 56 * 1024 * 1024
N_DEV = 8

D_MODEL = 1024
N_MEM = 256
MEM_HEADS = 4
MEM_HEAD_DIM = 128
MEM_WIDTH = 512
D_RNN = 768
LRU_BLOCK = 64
N_LRU_BLOCKS = 12
LRU_GROUP = 256
N_LRU_GROUPS = 3
CONV_WIDTH = 4
LRU_C = 8.0
DIL_GROUPS = ((128, 1), (512, 4), (2048, 16))
SPAN = 128
DIL_HEADS = 4
DIL_HEAD_DIM = 64
NUM_BUCKETS = 32
MAX_DISTANCE = 2048
D_FF = 4096
D_IN = 7424
EPS = 1e-6
NEG = -1e30
C_XL, C_GATE, C_QKV, C_QM, C_GATES = 0, 768, 1536, 3840, 4352

ADAM_LR = 0.001
ADAM_B1 = 0.9
ADAM_B2 = 0.999
ADAM_EPS = 1e-08
ADAM_WD = 0.01
ADAM_STEP = 10

MESH = pl.DeviceIdType.MESH
GELU_K = math.sqrt(2.0 / math.pi)


def _cparams(sem=None):
    kw = dict(vmem_limit_bytes=VMEM_LIMIT_BYTES)
    if sem is not None:
        kw["dimension_semantics"] = sem
    return pltpu.CompilerParams(**kw)


def _mx(v):
    return v.astype(MXU_DTYPE)


def _dot(a, b, mode="nn"):
    dims = {"nn": (((1,), (0,)), ((), ())), "nt": (((1,), (1,)), ((), ())), "tn": (((0,), (0,)), ((), ()))}[mode]
    return lax.dot_general(_mx(a), _mx(b), dims, preferred_element_type=F32)


def _colsum(v):
    return jnp.sum(v, axis=0, keepdims=True)


def _matmul(a, b, *, M, N, K, mode, bm, bn, bk, name, out_dtypes=(F32,), epilogue=None, extras=(),
            a_off=(0, 0), b_off=(0, 0), j_outer=False, deps=()):
    assert M % bm == 0 and N % bn == 0 and K % bk == 0, (name, M, N, K, bm, bn, bk)
    nm, nn, nk = M // bm, N // bn, K // bk

    def ij(f):
        if j_outer:
            return lambda j, i, k: f(i, j, k)
        return f

    if mode == "tn":
        a_spec = pl.BlockSpec((bk, bm), ij(lambda i, j, k: (k + a_off[0], i + a_off[1])))
    else:
        a_spec = pl.BlockSpec((bm, bk), ij(lambda i, j, k: (i + a_off[0], k + a_off[1])))
    if mode == "nt":
        b_spec = pl.BlockSpec((bn, bk), ij(lambda i, j, k: (j + b_off[0], k + b_off[1])))
    else:
        b_spec = pl.BlockSpec((bk, bn), ij(lambda i, j, k: (k + b_off[0], j + b_off[1])))
    ex_specs = [pl.BlockSpec((bm, bn), ij(functools.partial(lambda i, j, k, o: (i + o[0], j + o[1]), o=off)))
                for _, off in extras]
    out_spec = pl.BlockSpec((bm, bn), ij(lambda i, j, k: (i, j)))
    n_ex, n_out, n_dep = len(extras), len(out_dtypes), len(deps)

    def body(*refs):
        a_ref, b_ref = refs[0], refs[1]
        ex = refs[2:2 + n_ex]
        outs = refs[2 + n_ex + n_dep:2 + n_ex + n_dep + n_out]
        part = _dot(a_ref[...], b_ref[...], mode)

        def finish(acc):
            vals = epilogue(acc, *[e[...] for e in ex]) if epilogue is not None else (acc,)
            for o, v in zip(outs, vals):
                o[...] = v.astype(o.dtype)

        if nk == 1:
            finish(part)
        else:
            acc_ref = refs[-1]
            k = pl.program_id(2)

            @pl.when(k == 0)
            def _():
                acc_ref[...] = part

            @pl.when(k > 0)
            def _():
                acc_ref[...] += part

            @pl.when(k == nk - 1)
            def _():
                finish(acc_ref[...])

    grid = (nn, nm, nk) if j_outer else (nm, nn, nk)
    res = pl.pallas_call(
        body, name=name, grid=grid,
        in_specs=[a_spec, b_spec] + ex_specs + [pl.BlockSpec(memory_space=pl.ANY)] * n_dep,
        out_specs=[out_spec] * n_out,
        out_shape=[jax.ShapeDtypeStruct((M, N), dt) for dt in out_dtypes],
        scratch_shapes=[pltpu.VMEM((bm, bn), F32)] if nk > 1 else [],
        compiler_params=_cparams(("parallel", "parallel", "arbitrary")),
    )(a, b, *[e for e, _ in extras], *deps)
    return res[0] if n_out == 1 else res


def _rmsnorm_fwd(x, g, *, rows, name, bt=512):
    bt = min(bt, rows)

    def body(x_ref, g_ref, o_ref):
        xv = x_ref[...]
        r = lax.rsqrt(jnp.mean(xv * xv, axis=-1, keepdims=True) + EPS)
        o_ref[...] = (xv * r * g_ref[...]).astype(o_ref.dtype)

    return pl.pallas_call(
        body, name=name, grid=(rows // bt,),
        in_specs=[pl.BlockSpec((bt, D_MODEL), lambda i: (i, 0)), pl.BlockSpec((1, D_MODEL), lambda i: (0, 0))],
        out_specs=pl.BlockSpec((bt, D_MODEL), lambda i: (i, 0)),
        out_shape=jax.ShapeDtypeStruct((rows, D_MODEL), MXU_DTYPE),
        compiler_params=_cparams(("parallel",)),
    )(x, g.reshape(1, D_MODEL))


def _rms_bwd_tile(xv, gv, dyv):
    r = lax.rsqrt(jnp.mean(xv * xv, axis=-1, keepdims=True) + EPS)
    w = dyv * gv
    dx = r * w - xv * (r * r * r) * jnp.mean(w * xv, axis=-1, keepdims=True)
    dg = _colsum(dyv * xv * r)
    return dx, dg


def _rmsnorm_bwd(x, g, dy, res, *, rows, name, bt=512, want_dx=True):
    bt = min(bt, rows)
    has_res = res is not None

    def body(*refs):
        x_ref, g_ref, dy_ref = refs[:3]
        res_ref = refs[3] if has_res else None
        outs = refs[3 + int(has_res):]
        dx, dg = _rms_bwd_tile(x_ref[...], g_ref[...], dy_ref[...])
        if has_res:
            dx = dx + res_ref[...]
        dg_ref = outs[-1]

        @pl.when(pl.program_id(0) == 0)
        def _():
            dg_ref[...] = jnp.zeros_like(dg_ref)

        dg_ref[...] += dg
        if want_dx:
            outs[0][...] = dx

    row_spec = pl.BlockSpec((bt, D_MODEL), lambda i: (i, 0))
    vec_spec = pl.BlockSpec((1, D_MODEL), lambda i: (0, 0))
    ins = [x, g.reshape(1, D_MODEL), dy] + ([res] if has_res else [])
    out_shape = ([jax.ShapeDtypeStruct((rows, D_MODEL), F32)] if want_dx else []) + [jax.ShapeDtypeStruct((1, D_MODEL), F32)]
    out_specs = ([row_spec] if want_dx else []) + [vec_spec]
    res_ = pl.pallas_call(
        body, name=name, grid=(rows // bt,),
        in_specs=[row_spec, vec_spec, row_spec] + ([row_spec] if has_res else []),
        out_specs=out_specs, out_shape=out_shape,
        compiler_params=_cparams(("arbitrary",)),
    )(*ins)
    return res_ if want_dx else (None, res_[0])


def _loss_head(x2, g, tgt, *, rows, bt=512):
    def body(x_ref, g_ref, t_ref, loss_ref, dx_ref, dg_ref):
        xv, gv = x_ref[...], g_ref[...]
        r = lax.rsqrt(jnp.mean(xv * xv, axis=-1, keepdims=True) + EPS)
        diff = xv * r * gv - t_ref[...]
        part = jnp.sum(jnp.mean(diff * diff, axis=-1, keepdims=True), axis=0, keepdims=True) * 0.5
        dx, dg = _rms_bwd_tile(xv, gv, diff * (1.0 / D_MODEL))

        @pl.when(pl.program_id(0) == 0)
        def _():
            loss_ref[...] = jnp.zeros_like(loss_ref)
            dg_ref[...] = jnp.zeros_like(dg_ref)

        loss_ref[...] += part
        dg_ref[...] += dg
        dx_ref[...] = dx

    row_spec = pl.BlockSpec((bt, D_MODEL), lambda i: (i, 0))
    vec_spec = pl.BlockSpec((1, D_MODEL), lambda i: (0, 0))
    return pl.pallas_call(
        body, name="loss_head", grid=(rows // bt,),
        in_specs=[row_spec, vec_spec, row_spec],
        out_specs=[pl.BlockSpec((1, 1), lambda i: (0, 0)), row_spec, vec_spec],
        out_shape=[jax.ShapeDtypeStruct((1, 1), F32), jax.ShapeDtypeStruct((rows, D_MODEL), F32),
                   jax.ShapeDtypeStruct((1, D_MODEL), F32)],
        compiler_params=_cparams(("arbitrary",)),
    )(x2, g.reshape(1, D_MODEL), tgt)


LRU_T = 256


def _gelu(x):
    t = jnp.tanh(GELU_K * (x + 0.044715 * x * x * x))
    return 0.5 * x * (1.0 + t), t


def _gelu_grad(x, t):
    return 0.5 * (1.0 + t) + 0.5 * x * (1.0 - t * t) * GELU_K * (1.0 + 3.0 * 0.044715 * x * x)


def _softplus_neg(lam):
    z = -lam
    u = jnp.exp(-jnp.abs(z))
    w = 1.0 + u
    l1p = jnp.where(w == 1.0, u, jnp.log(w) * u / jnp.where(w == 1.0, 1.0, w - 1.0))
    return jnp.maximum(z, 0.0) + l1p


def _shift_down(cur, prev8, k, row8):
    y = pltpu.roll(cur, k, 0)
    head = jnp.where(row8 < k, pltpu.roll(prev8, k, 0), y[0:8])
    return jnp.concatenate([head, y[8:]], axis=0)


def _shift_up(cur, next8, k, row8):
    n = cur.shape[0]
    y = pltpu.roll(cur, n - k, 0)
    tail = jnp.where(row8 >= 8 - k, pltpu.roll(next8, 8 - k, 0), y[n - 8:n])
    return jnp.concatenate([y[0:n - 8], tail], axis=0)


def _lru_gates(xl, p8, cw, cb, wa, wx, ba, bx, lam, row8):
    sh = [xl] + [_shift_down(xl, p8, k, row8) for k in (1, 2, 3)]
    xc = cb + cw[3:4] * sh[0] + cw[2:3] * sh[1] + cw[1:2] * sh[2] + cw[0:1] * sh[3]
    r = jax.nn.sigmoid(_dot(xc, wa) + ba)
    i = jax.nn.sigmoid(_dot(xc, wx) + bx)
    sp = _softplus_neg(lam)
    la = -LRU_C * r * sp
    a = jnp.exp(la)
    mult = jnp.sqrt(jnp.tanh(-la) * (a * a + 1.0))
    return dict(sh=sh, xc=xc, r=r, i=i, sp=sp, a=a, mult=mult)


def _lru_specs(n_t, reverse):
    T = LRU_T
    tt = (lambda t: n_t - 1 - t) if reverse else (lambda t: t)
    blk = lambda col0: pl.BlockSpec((T, LRU_GROUP), lambda g, t: (tt(t), col0 + g))
    prev8 = lambda col0: pl.BlockSpec((8, LRU_GROUP), lambda g, t: (jnp.maximum(tt(t) * (T // 8) - 1, 0), col0 + g))
    vec = lambda rows: pl.BlockSpec((rows, LRU_GROUP), lambda g, t: (0, g))
    wbd = pl.BlockSpec((1, LRU_GROUP, LRU_GROUP), lambda g, t: (g, 0, 0))
    return blk, prev8, vec, wbd


def _lru_fwd(proj, conv_w, conv_b, wa_bd, wx_bd, b_a, b_x, lam, *, S):
    T = LRU_T
    n_t = S // T
    blk, _, vec, wbd = _lru_specs(n_t, False)

    def body(xl_ref, gate_ref, cw_ref, cb_ref, wa_ref, wx_ref, ba_ref, bx_ref, lam_ref,
             hl_ref, z_ref, prev8, hcar, a_s, b_s):
        @pl.when(pl.program_id(1) == 0)
        def _():
            prev8[...] = jnp.zeros_like(prev8)
            hcar[...] = jnp.zeros_like(hcar)

        row8 = lax.broadcasted_iota(jnp.int32, (8, LRU_GROUP), 0)
        xl = xl_ref[...]
        q = _lru_gates(xl, prev8[...], cw_ref[...], cb_ref[...], wa_ref[0], wx_ref[0], ba_ref[...], bx_ref[...],
                       lam_ref[...], row8)
        prev8[...] = xl[T - 8:T]
        a_s[...] = q["a"]
        b_s[...] = q["mult"] * q["i"] * q["xc"]

        def step(c, carry):
            off = pl.multiple_of(c * 8, 8)
            A = a_s[pl.ds(off, 8), :]
            B = b_s[pl.ds(off, 8), :]
            for k in (1, 2, 4):
                a_sh = jnp.where(row8 >= k, pltpu.roll(A, k, 0), 1.0)
                b_sh = jnp.where(row8 >= k, pltpu.roll(B, k, 0), 0.0)
                B = A * b_sh + B
                A = A * a_sh
            h = A * carry + B
            hl_ref[pl.ds(off, 8), :] = h
            return h[7:8, :]

        hcar[...] = lax.fori_loop(0, T // 8, step, hcar[...])
        ge, _ = _gelu(gate_ref[...])
        z_ref[...] = (ge * hl_ref[...]).astype(z_ref.dtype)

    return pl.pallas_call(
        body, name="lru_fwd", grid=(N_LRU_GROUPS, n_t),
        in_specs=[blk(C_XL // LRU_GROUP), blk(C_GATE // LRU_GROUP), vec(4), vec(1), wbd, wbd, vec(1), vec(1), vec(1)],
        out_specs=[blk(0), blk(0)],
        out_shape=[jax.ShapeDtypeStruct((S, D_RNN), F32), jax.ShapeDtypeStruct((S, D_RNN), MXU_DTYPE)],
        scratch_shapes=[pltpu.VMEM((8, LRU_GROUP), F32), pltpu.VMEM((1, LRU_GROUP), F32),
                        pltpu.VMEM((T, LRU_GROUP), F32), pltpu.VMEM((T, LRU_GROUP), F32)],
        compiler_params=_cparams(("parallel", "arbitrary")),
    )(proj, proj, conv_w, conv_b, wa_bd, wx_bd, b_a, b_x, lam)


def _lru_bwd(proj, hl, dz, conv_w, conv_b, wa_bd, wx_bd, b_a, b_x, lam, *, S):
    T = LRU_T
    n_t = S // T
    blk, prev8s, vec, wbd = _lru_specs(n_t, True)

    def body(xl_ref, xlp_ref, gate_ref, hl_ref, hlp_ref, dz_ref, cw_ref, cb_ref, wa_ref, wx_ref, ba_ref, bx_ref,
             lam_ref, dxl_ref, dgate_ref, dcw_ref, dcb_ref, dwa_ref, dwx_ref, dba_ref, dbx_ref, dlam_ref,
             next8, gcar, c_s, b_s, l_s):
        t = pl.program_id(1)
        first_chunk = t == n_t - 1

        @pl.when(t == 0)
        def _():
            next8[...] = jnp.zeros_like(next8)
            gcar[...] = jnp.zeros_like(gcar)
            for ref in (dcw_ref, dcb_ref, dwa_ref, dwx_ref, dba_ref, dbx_ref, dlam_ref):
                ref[...] = jnp.zeros_like(ref)

        row8 = lax.broadcasted_iota(jnp.int32, (8, LRU_GROUP), 0)
        rowT = lax.broadcasted_iota(jnp.int32, (T, LRU_GROUP), 0)
        keep = jnp.where(first_chunk, 0.0, 1.0)
        xl = xl_ref[...]
        wa, wx, lam_v = wa_ref[0], wx_ref[0], lam_ref[...]
        q = _lru_gates(xl, xlp_ref[...] * keep, cw_ref[...], cb_ref[...], wa, wx, ba_ref[...], bx_ref[...], lam_v, row8)
        a, mult, r, i, xc, sp = q["a"], q["mult"], q["r"], q["i"], q["xc"], q["sp"]
        hl_v = hl_ref[...]
        dz_v = dz_ref[...]
        gate = gate_ref[...]
        ge, th = _gelu(gate)
        dgate_ref[...] = (dz_v * hl_v * _gelu_grad(gate, th)).astype(dgate_ref.dtype)

        c_s[...] = jnp.where(rowT == T - 1, 0.0, pltpu.roll(a, T - 1, 0))
        b_s[...] = dz_v * ge + jnp.where(rowT == T - 1, gcar[...], 0.0)

        def step(n, carry):
            off = pl.multiple_of((T // 8 - 1 - n) * 8, 8)
            C = c_s[pl.ds(off, 8), :]
            B = b_s[pl.ds(off, 8), :]
            for k in (1, 2, 4):
                c_sh = jnp.where(row8 < 8 - k, pltpu.roll(C, 8 - k, 0), 1.0)
                b_sh = jnp.where(row8 < 8 - k, pltpu.roll(B, 8 - k, 0), 0.0)
                B = B + C * b_sh
                C = C * c_sh
            lam_t = B + C * carry
            l_s[pl.ds(off, 8), :] = lam_t
            return lam_t[0:1, :]

        lax.fori_loop(0, T // 8, step, jnp.zeros((1, LRU_GROUP), F32))
        lmb = l_s[...]
        gcar[...] = a[0:1, :] * lmb[0:1, :]

        h_prev = _shift_down(hl_v, hlp_ref[...] * keep, 1, row8)
        da = lmb * h_prev
        dmult = lmb * i * xc
        di = lmb * mult * xc
        dxc = lmb * mult * i
        dla = da * a - dmult * (a * a) / mult
        dr = dla * (-LRU_C * sp)
        dlam_ref[...] += _colsum(dla * (-LRU_C * r)) * (-jax.nn.sigmoid(-lam_v))
        dpa = dr * r * (1.0 - r)
        dpx = di * i * (1.0 - i)
        dxc = dxc + _dot(dpa, wa, "nt") + _dot(dpx, wx, "nt")
        dwa_ref[0] += _dot(xc, dpa, "tn")
        dwx_ref[0] += _dot(xc, dpx, "tn")
        dba_ref[...] += _colsum(dpa)
        dbx_ref[...] += _colsum(dpx)
        dcb_ref[...] += _colsum(dxc)
        cw = cw_ref[...]
        n8 = next8[...]
        dxl = cw[3:4] * dxc
        for k in (1, 2, 3):
            dxl = dxl + cw[3 - k:4 - k] * _shift_up(dxc, n8, k, row8)
        for k in range(4):
            dcw_ref[3 - k:4 - k, :] += _colsum(dxc * q["sh"][k])
        next8[...] = dxc[0:8]
        dxl_ref[...] = dxl.astype(dxl_ref.dtype)

    res = pl.pallas_call(
        body, name="lru_bwd", grid=(N_LRU_GROUPS, n_t),
        in_specs=[blk(C_XL // LRU_GROUP), prev8s(C_XL // LRU_GROUP), blk(C_GATE // LRU_GROUP), blk(0), prev8s(0), blk(0),
                  vec(4), vec(1), wbd, wbd, vec(1), vec(1), vec(1)],
        out_specs=[blk(0), blk(0), vec(4), vec(1), wbd, wbd, vec(1), vec(1), vec(1)],
        out_shape=[jax.ShapeDtypeStruct((S, D_RNN), MXU_DTYPE), jax.ShapeDtypeStruct((S, D_RNN), MXU_DTYPE),
                   jax.ShapeDtypeStruct((4, D_RNN), F32), jax.ShapeDtypeStruct((1, D_RNN), F32),
                   jax.ShapeDtypeStruct((N_LRU_GROUPS, LRU_GROUP, LRU_GROUP), F32),
                   jax.ShapeDtypeStruct((N_LRU_GROUPS, LRU_GROUP, LRU_GROUP), F32),
                   jax.ShapeDtypeStruct((1, D_RNN), F32), jax.ShapeDtypeStruct((1, D_RNN), F32),
                   jax.ShapeDtypeStruct((1, D_RNN), F32)],
        scratch_shapes=[pltpu.VMEM((8, LRU_GROUP), F32), pltpu.VMEM((1, LRU_GROUP), F32),
                        pltpu.VMEM((T, LRU_GROUP), F32), pltpu.VMEM((T, LRU_GROUP), F32), pltpu.VMEM((T, LRU_GROUP), F32)],
        compiler_params=_cparams(("parallel", "arbitrary")),
    )(proj, proj, proj, hl, hl, dz, conv_w, conv_b, wa_bd, wx_bd, b_a, b_x, lam)
    return res


def _block_diag(w):
    w4 = w.reshape(N_LRU_GROUPS, 4, LRU_BLOCK, 1, LRU_BLOCK)
    eye = jnp.eye(4, dtype=w.dtype).reshape(1, 4, 1, 4, 1)
    return (w4 * eye).reshape(N_LRU_GROUPS, LRU_GROUP, LRU_GROUP)


def _block_diag_extract(wbd):
    w5 = wbd.reshape(N_LRU_GROUPS, 4, LRU_BLOCK, 4, LRU_BLOCK)
    return jnp.stack([w5[:, a, :, a, :] for a in range(4)], axis=1).reshape(N_LRU_BLOCKS, LRU_BLOCK, LRU_BLOCK)


def _t5_bucket(dist):
    max_exact = NUM_BUCKETS // 2
    df = jnp.maximum(dist, 1).astype(jnp.float32)
    large = max_exact + (jnp.log(df / max_exact) / math.log(MAX_DISTANCE / max_exact)
                         * (NUM_BUCKETS - max_exact)).astype(jnp.int32)
    large = jnp.minimum(large, NUM_BUCKETS - 1)
    return jnp.where(dist < max_exact, dist, large)


def _band_offsets():
    qi = jnp.arange(SPAN)[:, None]
    kj = jnp.arange(2 * SPAN)[None, :]
    return qi + SPAN - kj


def _dil_buckets():
    off = _band_offsets()
    return jnp.stack([_t5_bucket(jnp.maximum(off, 0) * dil) for _, dil in DIL_GROUPS]).astype(jnp.int32)


def _dil_bias(rel_bias, buckets):
    def body(tbl_ref, bk_ref, o_ref):
        g = pl.program_id(0)
        qi = lax.broadcasted_iota(jnp.int32, (SPAN, 2 * SPAN), 0)
        kj = lax.broadcasted_iota(jnp.int32, (SPAN, 2 * SPAN), 1)
        off = qi + SPAN - kj
        valid = (off >= 0) & (off <= SPAN)
        bk = bk_ref[0]
        for h in range(DIL_HEADS):
            acc = jnp.zeros((SPAN, 2 * SPAN), F32)
            for b in range(NUM_BUCKETS):
                acc = jnp.where(bk == b, tbl_ref[b, g * DIL_HEADS + h], acc)
            o_ref[0, h] = jnp.where(valid, acc, NEG)

    return pl.pallas_call(
        body, name="dil_bias", grid=(3,),
        in_specs=[pl.BlockSpec(memory_space=pltpu.SMEM), pl.BlockSpec((1, SPAN, 2 * SPAN), lambda g: (g, 0, 0))],
        out_specs=pl.BlockSpec((1, DIL_HEADS, SPAN, 2 * SPAN), lambda g: (g, 0, 0, 0)),
        out_shape=jax.ShapeDtypeStruct((3, DIL_HEADS, SPAN, 2 * SPAN), F32),
        compiler_params=_cparams(("parallel",)),
    )(rel_bias, buckets)


def _dil_bias_bwd(dbias, buckets):
    def body(db_ref, bk_ref, o_ref):
        lane = lax.broadcasted_iota(jnp.int32, (1, 128), 1)
        rows = [jnp.zeros((1, 128), F32) for _ in range(NUM_BUCKETS)]
        for g in range(3):
            bk = bk_ref[g]
            for h in range(DIL_HEADS):
                d = db_ref[g, h]
                for b in range(NUM_BUCKETS):
                    tot = jnp.sum(_colsum(jnp.where(bk == b, d, 0.0)), axis=1, keepdims=True)
                    rows[b] = jnp.where(lane == g * DIL_HEADS + h, tot, rows[b])
        for b in range(NUM_BUCKETS):
            o_ref[b:b + 1, :] = rows[b]

    return pl.pallas_call(
        body, name="dil_bias_bwd",
        out_shape=jax.ShapeDtypeStruct((NUM_BUCKETS, 128), F32),
        compiler_params=_cparams(),
    )(dbias, buckets)


def _dil_scores(q, kcat, bias, first):
    s = _dot(q, kcat, "nt") * (DIL_HEAD_DIM ** -0.5) + bias
    kj = lax.broadcasted_iota(jnp.int32, s.shape, 1)
    return jnp.where(kj < jnp.where(first, SPAN, 0), NEG, s)


def _dil_attn_fwd(q, k, v, bias, g, *, S, blocks_per_seq):
    nb = S // SPAN
    hd = pl.BlockSpec((DIL_HEADS, SPAN, DIL_HEAD_DIM), lambda i: (0, i, 0))
    hd_prev = pl.BlockSpec((DIL_HEADS, SPAN, DIL_HEAD_DIM), lambda i: (0, jnp.maximum(i - 1, 0), 0))
    st = pl.BlockSpec((DIL_HEADS, SPAN, 1), lambda i: (0, i, 0))

    def body(q_ref, kp_ref, kc_ref, vp_ref, vc_ref, b_ref, o_ref, lse_ref):
        first = (pl.program_id(0) % blocks_per_seq) == 0
        for h in range(DIL_HEADS):
            kcat = jnp.concatenate([kp_ref[h], kc_ref[h]], axis=0)
            vcat = jnp.concatenate([vp_ref[h], vc_ref[h]], axis=0)
            s = _dil_scores(q_ref[h], kcat, b_ref[0, h], first)
            m = jnp.max(s, axis=-1, keepdims=True)
            p = jnp.exp(s - m)
            den = jnp.sum(p, axis=-1, keepdims=True)
            o_ref[h] = _dot(p, vcat) / den
            lse_ref[h] = m + jnp.log(den)

    return pl.pallas_call(
        body, name=f"dil_attn_fwd{g}", grid=(nb,),
        in_specs=[hd, hd_prev, hd, hd_prev, hd,
                  pl.BlockSpec((1, DIL_HEADS, SPAN, 2 * SPAN), lambda i: (g, 0, 0, 0))],
        out_specs=[hd, st],
        out_shape=[jax.ShapeDtypeStruct((DIL_HEADS, S, DIL_HEAD_DIM), F32), jax.ShapeDtypeStruct((DIL_HEADS, S, 1), F32)],
        compiler_params=_cparams(("parallel",)),
    )(q, k, k, v, v, bias)


def _dil_attn_bwd(q, k, v, do, lse, delta, bias, g, *, S, blocks_per_seq):
    nb = S // SPAN
    cl = lambda i: jnp.minimum(i, nb - 1)
    hd = pl.BlockSpec((DIL_HEADS, SPAN, DIL_HEAD_DIM), lambda i: (0, cl(i), 0))
    hd_prev = pl.BlockSpec((DIL_HEADS, SPAN, DIL_HEAD_DIM), lambda i: (0, jnp.maximum(cl(i) - 1, 0), 0))
    st = pl.BlockSpec((DIL_HEADS, SPAN, 1), lambda i: (0, cl(i), 0))
    kv_out = pl.BlockSpec((DIL_HEADS, SPAN, DIL_HEAD_DIM), lambda i: (0, jnp.maximum(i - 1, 0), 0))
    scale = DIL_HEAD_DIM ** -0.5

    def body(q_ref, kp_ref, kc_ref, vp_ref, vc_ref, do_ref, lse_ref, dl_ref, b_ref,
             dq_ref, dk_ref, dv_ref, db_ref, kcar, vcar):
        i = pl.program_id(0)

        @pl.when(i == 0)
        def _():
            kcar[...] = jnp.zeros_like(kcar)
            vcar[...] = jnp.zeros_like(vcar)
            db_ref[...] = jnp.zeros_like(db_ref)

        @pl.when(i < nb)
        def _():
            first = (i % blocks_per_seq) == 0
            for h in range(DIL_HEADS):
                kcat = jnp.concatenate([kp_ref[h], kc_ref[h]], axis=0)
                vcat = jnp.concatenate([vp_ref[h], vc_ref[h]], axis=0)
                qh, doh = q_ref[h], do_ref[h]
                p = jnp.exp(_dil_scores(qh, kcat, b_ref[0, h], first) - lse_ref[h])
                ds = p * (_dot(doh, vcat, "nt") - dl_ref[h])
                db_ref[0, h] += ds
                dq_ref[h] = (_dot(ds, kcat) * scale).astype(dq_ref.dtype)
                dkc = _dot(ds, qh, "tn") * scale
                dvc = _dot(p, doh, "tn")
                dk_ref[h] = (kcar[h] + dkc[0:SPAN]).astype(dk_ref.dtype)
                dv_ref[h] = (vcar[h] + dvc[0:SPAN]).astype(dv_ref.dtype)
                kcar[h] = dkc[SPAN:2 * SPAN]
                vcar[h] = dvc[SPAN:2 * SPAN]

        @pl.when(i == nb)
        def _():
            dk_ref[...] = kcar[...].astype(dk_ref.dtype)
            dv_ref[...] = vcar[...].astype(dv_ref.dtype)

    hshape = jax.ShapeDtypeStruct((DIL_HEADS, S, DIL_HEAD_DIM), MXU_DTYPE)
    bspec = pl.BlockSpec((1, DIL_HEADS, SPAN, 2 * SPAN), lambda i: (g, 0, 0, 0))
    return pl.pallas_call(
        body, name=f"dil_attn_bwd{g}", grid=(nb + 1,),
        in_specs=[hd, hd_prev, hd, hd_prev, hd, hd, st, st, bspec],
        out_specs=[hd, kv_out, kv_out, pl.BlockSpec((1, DIL_HEADS, SPAN, 2 * SPAN), lambda i: (0, 0, 0, 0))],
        out_shape=[hshape, hshape, hshape, jax.ShapeDtypeStruct((1, DIL_HEADS, SPAN, 2 * SPAN), F32)],
        scratch_shapes=[pltpu.VMEM((DIL_HEADS, SPAN, DIL_HEAD_DIM), F32), pltpu.VMEM((DIL_HEADS, SPAN, DIL_HEAD_DIM), F32)],
        compiler_params=_cparams(("arbitrary",)),
    )(q, k, k, v, v, do, lse, delta, bias)


def _dil_merge(os_, lses, *, S, bt=512):
    hd = pl.BlockSpec((DIL_HEADS, bt, DIL_HEAD_DIM), lambda i: (0, i, 0))
    st = pl.BlockSpec((DIL_HEADS, bt, 1), lambda i: (0, i, 0))

    def body(o0, o1, o2, l0, l1, l2, o_ref, lse_ref):
        a0, a1, a2 = l0[...], l1[...], l2[...]
        m = jnp.maximum(jnp.maximum(a0, a1), a2)
        e0, e1, e2 = jnp.exp(a0 - m), jnp.exp(a1 - m), jnp.exp(a2 - m)
        tot = e0 + e1 + e2
        o_ref[...] = (e0 / tot) * o0[...] + (e1 / tot) * o1[...] + (e2 / tot) * o2[...]
        lse_ref[...] = m + jnp.log(tot)

    return pl.pallas_call(
        body, name="dil_merge", grid=(S // bt,),
        in_specs=[hd, hd, hd, st, st, st], out_specs=[hd, st],
        out_shape=[jax.ShapeDtypeStruct((DIL_HEADS, S, DIL_HEAD_DIM), F32), jax.ShapeDtypeStruct((DIL_HEADS, S, 1), F32)],
        compiler_params=_cparams(("parallel",)),
    )(*os_, *lses)


def _row_dot(a, b, *, S, bt=512):
    hd = pl.BlockSpec((DIL_HEADS, bt, DIL_HEAD_DIM), lambda i: (0, i, 0))
    st = pl.BlockSpec((DIL_HEADS, bt, 1), lambda i: (0, i, 0))

    def body(a_ref, b_ref, o_ref):
        o_ref[...] = jnp.sum(a_ref[...] * b_ref[...], axis=-1, keepdims=True)

    return pl.pallas_call(
        body, name="dil_delta", grid=(S // bt,), in_specs=[hd, hd], out_specs=st,
        out_shape=jax.ShapeDtypeStruct((DIL_HEADS, S, 1), F32), compiler_params=_cparams(("parallel",)),
    )(a, b)


def _to_heads(t, dil, dtype):
    S = t.shape[0]
    L = S // dil
    return t.reshape(L, dil, DIL_HEADS, DIL_HEAD_DIM).transpose(2, 1, 0, 3).reshape(DIL_HEADS, S, DIL_HEAD_DIM).astype(dtype)


def _from_heads(t, dil, dtype):
    S = t.shape[1]
    L = S // dil
    return t.reshape(DIL_HEADS, dil, L, DIL_HEAD_DIM).transpose(2, 1, 0, 3).reshape(S, DIL_HEADS * DIL_HEAD_DIM).astype(dtype)


def _stat_to_natural(t, dil):
    S = t.shape[1]
    L = S // dil
    return t.reshape(DIL_HEADS, dil, L).transpose(0, 2, 1).reshape(DIL_HEADS, S, 1)


def _stat_to_dilated(t, dil):
    S = t.shape[1]
    L = S // dil
    return t.reshape(DIL_HEADS, L, dil).transpose(0, 2, 1).reshape(DIL_HEADS, S, 1)


MEM_T = 512
QM_BLK = C_QM // MEM_HEAD_DIM


def _mem_attn_fwd(proj, kv, *, S):
    scale = MEM_HEAD_DIM ** -0.5

    def body(q_ref, k_ref, v_ref, o_ref, lse_ref):
        s = _dot(q_ref[...], k_ref[...], "nt") * scale
        m = jnp.max(s, axis=-1, keepdims=True)
        p = jnp.exp(s - m)
        den = jnp.sum(p, axis=-1, keepdims=True)
        o_ref[...] = _dot(p, v_ref[...]) / den
        lse_ref[0] = m + jnp.log(den)

    return pl.pallas_call(
        body, name="mem_attn_fwd", grid=(S // MEM_T, MEM_HEADS),
        in_specs=[pl.BlockSpec((MEM_T, MEM_HEAD_DIM), lambda i, h: (i, QM_BLK + h)),
                  pl.BlockSpec((N_MEM, MEM_HEAD_DIM), lambda i, h: (0, h)),
                  pl.BlockSpec((N_MEM, MEM_HEAD_DIM), lambda i, h: (0, MEM_HEADS + h))],
        out_specs=[pl.BlockSpec((MEM_T, MEM_HEAD_DIM), lambda i, h: (i, h)),
                   pl.BlockSpec((1, MEM_T, 1), lambda i, h: (h, i, 0))],
        out_shape=[jax.ShapeDtypeStruct((S, MEM_WIDTH), F32), jax.ShapeDtypeStruct((MEM_HEADS, S, 1), F32)],
        compiler_params=_cparams(("parallel", "parallel")),
    )(proj, kv, kv)


def _mem_attn_bwd(proj, kv, om, lse, dom, *, S):
    scale = MEM_HEAD_DIM ** -0.5

    def body(q_ref, k_ref, v_ref, o_ref, lse_ref, do_ref, dq_ref, dk_ref, dv_ref):
        @pl.when(pl.program_id(1) == 0)
        def _():
            dk_ref[...] = jnp.zeros_like(dk_ref)
            dv_ref[...] = jnp.zeros_like(dv_ref)

        qv, kv_, vv, dov = q_ref[...], k_ref[...], v_ref[...], do_ref[...]
        p = jnp.exp(_dot(qv, kv_, "nt") * scale - lse_ref[0])
        delta = jnp.sum(dov * o_ref[...], axis=-1, keepdims=True)
        ds = p * (_dot(dov, vv, "nt") - delta)
        dq_ref[...] = (_dot(ds, kv_) * scale).astype(dq_ref.dtype)
        dk_ref[...] += _dot(ds, qv, "tn") * scale
        dv_ref[...] += _dot(p, dov, "tn")

    tile = pl.BlockSpec((MEM_T, MEM_HEAD_DIM), lambda h, i: (i, h))
    kvo = pl.BlockSpec((N_MEM, MEM_HEAD_DIM), lambda h, i: (0, h))
    return pl.pallas_call(
        body, name="mem_attn_bwd", grid=(MEM_HEADS, S // MEM_T),
        in_specs=[pl.BlockSpec((MEM_T, MEM_HEAD_DIM), lambda h, i: (i, QM_BLK + h)),
                  pl.BlockSpec((N_MEM, MEM_HEAD_DIM), lambda h, i: (0, h)),
                  pl.BlockSpec((N_MEM, MEM_HEAD_DIM), lambda h, i: (0, MEM_HEADS + h)),
                  tile, pl.BlockSpec((1, MEM_T, 1), lambda h, i: (h, i, 0)), tile],
        out_specs=[tile, kvo, kvo],
        out_shape=[jax.ShapeDtypeStruct((S, MEM_WIDTH), MXU_DTYPE), jax.ShapeDtypeStruct((N_MEM, MEM_WIDTH), F32),
                   jax.ShapeDtypeStruct((N_MEM, MEM_WIDTH), F32)],
        compiler_params=_cparams(("parallel", "arbitrary")),
    )(proj, kv, kv, om, lse, dom)


MIX_BM = 1024
MIX_BN = 256
GATES_BLK = C_GATES // MIX_BN


def _mix_specs(j_outer):
    ix = (lambda f: (lambda j, i: f(i, j))) if j_outer else (lambda f: f)
    act = lambda width: pl.BlockSpec((MIX_BM, width), ix(lambda i, j: (i, 0)))
    wgt = lambda width: pl.BlockSpec((width, MIX_BN), ix(lambda i, j: (0, j)))
    gate = lambda b: pl.BlockSpec((MIX_BM, MIX_BN), ix(lambda i, j: (i, GATES_BLK + 4 * b + j)))
    bias = lambda b: pl.BlockSpec((1, MIX_BN), ix(lambda i, j: (0, 4 * b + j)))
    tile = pl.BlockSpec((MIX_BM, MIX_BN), ix(lambda i, j: (i, j)))
    return act, wgt, gate, bias, tile


def _mix_fwd(z_lru, o_dil, om, w_lru, w_dil, w_mem, proj, b_gate, *, S):
    act, wgt, gate, bias, tile = _mix_specs(False)

    def body(zl, od, mo, wl, wd, wm, g0, g1, g2, b0, b1, b2, o_ref):
        acc = jax.nn.sigmoid(g0[...] + b0[...]) * _dot(zl[...], wl[...])
        acc += jax.nn.sigmoid(g1[...] + b1[...]) * _dot(od[...], wd[...])
        acc += jax.nn.sigmoid(g2[...] + b2[...]) * _dot(mo[...], wm[...])
        o_ref[...] = acc.astype(o_ref.dtype)

    return pl.pallas_call(
        body, name="mix_fwd", grid=(S // MIX_BM, D_MODEL // MIX_BN),
        in_specs=[act(D_RNN), act(256), act(MEM_WIDTH), wgt(D_RNN), wgt(256), wgt(MEM_WIDTH),
                  gate(0), gate(1), gate(2), bias(0), bias(1), bias(2)],
        out_specs=tile, out_shape=jax.ShapeDtypeStruct((S, D_MODEL), MXU_DTYPE),
        compiler_params=_cparams(("parallel", "parallel")),
    )(z_lru, o_dil, om, w_lru, w_dil, w_mem, proj, proj, proj, b_gate, b_gate, b_gate)


def _mix_bwd(dmerged, z_lru, o_dil, om, w_lru, w_dil, w_mem, proj, b_gate, *, S):
    act, wgt, gate, bias, tile = _mix_specs(True)

    def body(dm, zl, od, mo, wl, wd, wm, g0, g1, g2, b0, b1, b2,
             dg0, dg1, dg2, dy0, dy1, dy2, db0, db1, db2):
        @pl.when(pl.program_id(1) == 0)
        def _():
            for r in (db0, db1, db2):
                r[...] = jnp.zeros_like(r)

        dmv = dm[...]
        for act_ref, w_ref, g_ref, b_ref, dg_ref, dy_ref, db_ref in (
                (zl, wl, g0, b0, dg0, dy0, db0), (od, wd, g1, b1, dg1, dy1, db1), (mo, wm, g2, b2, dg2, dy2, db2)):
            y = _dot(act_ref[...], w_ref[...])
            gt = jax.nn.sigmoid(g_ref[...] + b_ref[...])
            dgate = dmv * y * gt * (1.0 - gt)
            dg_ref[...] = dgate.astype(dg_ref.dtype)
            dy_ref[...] = (dmv * gt).astype(dy_ref.dtype)
            db_ref[...] += _colsum(dgate)

    big = jax.ShapeDtypeStruct((S, D_MODEL), MXU_DTYPE)
    vec = jax.ShapeDtypeStruct((1, D_MODEL), F32)
    vspec = pl.BlockSpec((1, MIX_BN), lambda j, i: (0, j))
    return pl.pallas_call(
        body, name="mix_bwd", grid=(D_MODEL // MIX_BN, S // MIX_BM),
        in_specs=[tile, act(D_RNN), act(256), act(MEM_WIDTH), wgt(D_RNN), wgt(256), wgt(MEM_WIDTH),
                  gate(0), gate(1), gate(2), bias(0), bias(1), bias(2)],
        out_specs=[tile] * 6 + [vspec] * 3, out_shape=[big] * 6 + [vec] * 3,
        compiler_params=_cparams(("parallel", "arbitrary")),
    )(dmerged, z_lru, o_dil, om, w_lru, w_dil, w_mem, proj, proj, proj, b_gate, b_gate, b_gate)


def _adamw_math(w, g, m, v):
    m = ADAM_B1 * m + (1.0 - ADAM_B1) * g
    v = ADAM_B2 * v + (1.0 - ADAM_B2) * (g * g)
    m_hat = m / (1.0 - ADAM_B1 ** ADAM_STEP)
    v_hat = v / (1.0 - ADAM_B2 ** ADAM_STEP)
    delta = -ADAM_LR * (m_hat / (jnp.sqrt(v_hat) + ADAM_EPS) + ADAM_WD * w)
    return delta, m, v


def _adamw_landed(w, own, land, m, v, *, name, row0=0, prev=None):
    R, C = w.shape
    n_parts, rows = land.shape[0], land.shape[1]
    br = min(rows, 256)
    blk0 = row0 // br
    tile = pl.BlockSpec((br, C), lambda i: (i + blk0, 0))
    part = pl.BlockSpec((br, C), lambda i: (i, 0))
    n_prev = 0 if prev is None else 4

    def body(w_ref, o_ref, l_ref, m_ref, v_ref, *rest):
        g_ref, d_ref, nm_ref, nv_ref = rest[n_prev:]
        g = o_ref[...]
        for p in range(n_parts):
            g = g + l_ref[p].astype(F32)
        d, nm, nv = _adamw_math(w_ref[...], g, m_ref[...], v_ref[...])
        g_ref[...] = g
        d_ref[...] = d
        nm_ref[...] = nm
        nv_ref[...] = nv

    return pl.pallas_call(
        body, name=name, grid=(rows // br,),
        in_specs=[tile, part, pl.BlockSpec((n_parts, br, C), lambda i: (0, i, 0)), tile, tile]
        + [pl.BlockSpec(memory_space=pl.ANY)] * n_prev,
        out_specs=[tile] * 4, out_shape=[jax.ShapeDtypeStruct((R, C), F32)] * 4,
        input_output_aliases={5 + j: j for j in range(n_prev)},
        compiler_params=_cparams(("parallel",)),
    )(w, own, land, m, v, *(prev or ()))


def _adamw_plain(w, g, m, v, *, name):
    def body(w_ref, g_ref, m_ref, v_ref, d_ref, nm_ref, nv_ref):
        d, nm, nv = _adamw_math(w_ref[...], g_ref[...], m_ref[...], v_ref[...])
        d_ref[...] = d
        nm_ref[...] = nm
        nv_ref[...] = nv

    return pl.pallas_call(
        body, name=name, out_shape=[jax.ShapeDtypeStruct(w.shape, F32)] * 3, compiler_params=_cparams(),
    )(w, g, m, v)


def _my_pos():
    return lax.axis_index("x"), lax.axis_index("y"), lax.axis_index("c")


def _dev_index(p):
    return 4 * p[0] + 2 * p[1] + p[2]


def _all_gather(shards):
    n = len(shards)
    hbm = pl.BlockSpec(memory_space=pl.ANY)

    def body(*refs):
        ins, outs = refs[:n], refs[n:2 * n]
        send_sems, recv_sems, local_sems = refs[2 * n:]
        x, y, c = _my_pos()
        me, sibling = (x, y, c), (x, y, 1 - c)
        chips = [(1 - x, y), (x, 1 - y), (1 - x, 1 - y)]

        def copy(a, k, block, to, src=None):
            dst = outs[a].at[_dev_index(block)]
            return pltpu.make_async_remote_copy(
                src_ref=dst if src is None else src, dst_ref=dst,
                send_sem=send_sems.at[a, k], recv_sem=recv_sems.at[a, k], device_id=to, device_id_type=MESH)

        mine = [pltpu.make_async_copy(ins[a], outs[a].at[_dev_index(me)], local_sems.at[a]) for a in range(n)]
        for cp in mine:
            cp.start()
        first = []
        for a in range(n):
            first.append(copy(a, 0, me, sibling, src=ins[a]))
            first += [copy(a, 1 + j, me, (*chip, c), src=ins[a]) for j, chip in enumerate(chips)]
        for cp in first:
            cp.start()
        passed = []
        for j, chip in enumerate(chips):
            for a in range(n):
                copy(a, 1 + j, (*chip, c), me).wait_recv()
                fwd = copy(a, 4 + j, (*chip, c), sibling)
                fwd.start()
                passed.append(fwd)
        for a in range(n):
            copy(a, 0, sibling, me).wait_recv()
        for j, chip in enumerate(chips):
            for a in range(n):
                copy(a, 4 + j, (*chip, 1 - c), me).wait_recv()
        for cp in first + passed:
            cp.wait_send()
        for cp in mine:
            cp.wait()

    return pl.pallas_call(
        body, name="all_gather_weights",
        in_specs=[hbm] * n, out_specs=[hbm] * n,
        out_shape=[jax.ShapeDtypeStruct((N_DEV,) + s.shape, s.dtype) for s in shards],
        scratch_shapes=[pltpu.SemaphoreType.DMA((n, 7)), pltpu.SemaphoreType.DMA((n, 7)), pltpu.SemaphoreType.DMA((n,))],
        compiler_params=pltpu.CompilerParams(has_side_effects=True),
    )(*shards)


def _peers(me):
    x, y, c = me
    out = []
    for k in range(1, 8):
        fx, fy, fc = (k >> 2) & 1, (k >> 1) & 1, k & 1
        out.append((k - 1, (1 - x if fx else x, 1 - y if fy else y, 1 - c if fc else c)))
    return out


HBM_SPEC = pl.BlockSpec(memory_space=pltpu.HBM)
SEM_SPEC = pl.BlockSpec(memory_space=pltpu.SEMAPHORE)
DATAFLOW_EFFECT = pltpu.SideEffectType.DATAFLOW_SIDE_EFFECTING


def _gather_refs(src, land, me, peer, k):
    return src, land.at[_dev_index(me)]


def _scatter_refs(src, land, me, peer, k):
    return src.at[_dev_index(peer)], land.at[k]


def _push_start(srcs, land_shapes, refs_of, name):
    n = len(srcs)

    def body(*refs):
        ins, lands = refs[:n], refs[n:2 * n]
        send_sems, recv_sems, token = refs[2 * n], refs[2 * n + 1], refs[-1]
        me = _my_pos()
        for k, peer in _peers(me):
            for a in range(n):
                src, dst = refs_of(ins[a], lands[a], me, peer, k)
                pltpu.make_async_remote_copy(src_ref=src, dst_ref=dst, send_sem=send_sems.at[7 * a + k],
                                             recv_sem=recv_sems.at[7 * a + k], device_id=peer, device_id_type=MESH).start()
        token[...] = jnp.zeros_like(token)

    lands = [lax.empty(shp, s.dtype) for shp, s in zip(land_shapes, srcs)]
    hbm = lambda a: pltpu.with_memory_space_constraint(a, pltpu.HBM)
    res = pl.pallas_call(
        body, name=name,
        out_shape=(pltpu.SemaphoreType.DMA((7 * n,)), pltpu.SemaphoreType.DMA((7 * n,)),
                   *[pltpu.HBM(s.shape, s.dtype) for s in srcs], *[pltpu.HBM(l.shape, l.dtype) for l in lands],
                   jax.ShapeDtypeStruct((8, 128), F32)),
        in_specs=[HBM_SPEC] * (2 * n),
        out_specs=(SEM_SPEC, SEM_SPEC, *[HBM_SPEC] * (2 * n), pl.BlockSpec(memory_space=pltpu.VMEM)),
        input_output_aliases={i: 2 + i for i in range(2 * n)},
        compiler_params=pltpu.CompilerParams(has_side_effects=DATAFLOW_EFFECT),
    )(*[hbm(s) for s in srcs], *[hbm(l) for l in lands])
    return dict(sems=(res[0], res[1]), srcs=list(res[2:2 + n]), lands=list(res[2 + n:2 + 2 * n]), token=res[-1], n=n,
                refs_of=refs_of, name=name)


def _push_wait(started, after):
    n, refs_of = started["n"], started["refs_of"]

    def body(*refs):
        ins, lands = refs[:n], refs[n:2 * n]
        send_sems, recv_sems = refs[2 * n], refs[2 * n + 1]
        me = _my_pos()
        for k, peer in _peers(me):
            for a in range(n):
                src, dst = refs_of(ins[a], lands[a], me, peer, k)
                cp = pltpu.make_async_remote_copy(src_ref=src, dst_ref=dst, send_sem=send_sems.at[7 * a + k],
                                                  recv_sem=recv_sems.at[7 * a + k], device_id=peer, device_id_type=MESH)
                cp.wait_send()
                cp.wait_recv()

    arrs = started["srcs"] + started["lands"]
    res = pl.pallas_call(
        body, name=started["name"].replace("start", "wait"),
        out_shape=tuple(pltpu.HBM(a.shape, a.dtype) for a in arrs),
        in_specs=[HBM_SPEC] * (2 * n) + [SEM_SPEC, SEM_SPEC, pl.BlockSpec(memory_space=pl.ANY)],
        out_specs=tuple([HBM_SPEC] * (2 * n)),
        input_output_aliases={i: i for i in range(2 * n)},
        compiler_params=pltpu.CompilerParams(has_side_effects=DATAFLOW_EFFECT),
    )(*arrs, *started["sems"], after)
    return list(res[n:2 * n])


def _all_reduce_small(pack):
    R = pack.shape[0]

    def body(in_ref, out_ref, land, send_sems, recv_sems):
        me = _my_pos()
        my_idx = _dev_index(me)
        land[my_idx] = in_ref[...]
        sent = []
        for k, peer in _peers(me):
            cp = pltpu.make_async_remote_copy(
                src_ref=in_ref, dst_ref=land.at[my_idx], send_sem=send_sems.at[k], recv_sem=recv_sems.at[k],
                device_id=peer, device_id_type=MESH)
            cp.start()
            sent.append(cp)
        for k, peer in _peers(me):
            pltpu.make_async_remote_copy(
                src_ref=in_ref, dst_ref=land.at[_dev_index(peer)], send_sem=send_sems.at[k], recv_sem=recv_sems.at[k],
                device_id=peer, device_id_type=MESH).wait_recv()
        for cp in sent:
            cp.wait_send()
        acc = land[0]
        for d in range(1, N_DEV):
            acc = acc + land[d]
        out_ref[...] = acc

    return pl.pallas_call(
        body, name="all_reduce_small",
        in_specs=[pl.BlockSpec(memory_space=pltpu.VMEM)], out_specs=pl.BlockSpec(memory_space=pltpu.VMEM),
        out_shape=jax.ShapeDtypeStruct(pack.shape, F32),
        scratch_shapes=[pltpu.VMEM((N_DEV, R, 128), F32), pltpu.SemaphoreType.DMA((7,)), pltpu.SemaphoreType.DMA((7,))],
        compiler_params=pltpu.CompilerParams(has_side_effects=True, vmem_limit_bytes=VMEM_LIMIT_BYTES),
    )(pack)


def _local_step(x, mem, tgt, W, P, late_weights, send_grads, reduce_small, tie0):
    S = x.shape[0]
    W = dict(W)
    h = _rmsnorm_fwd(x, P["g_mix"] + tie0, rows=S, name="norm_mix")
    proj = _matmul(h, W["w_in"], M=S, N=D_IN, K=D_MODEL, mode="nn", bm=512, bn=D_IN // 2, bk=D_MODEL, name="mm_in",
                   j_outer=True)

    wa_bd, wx_bd = _mx(_block_diag(P["w_rg_a"])), _mx(_block_diag(P["w_rg_x"]))
    lru_args = (W["conv_w"], P["conv_b"].reshape(1, -1), wa_bd, wx_bd, P["b_rg_a"].reshape(1, -1),
                P["b_rg_x"].reshape(1, -1), P["lru_lambda"].reshape(1, -1))
    hl, z_lru = _lru_fwd(proj, *lru_args, S=S)

    buckets = _dil_buckets()
    bias = _dil_bias(P["rel_bias"], buckets)
    qkv_heads, o_nat, lse_nat = [], [], []
    for g, (window, dil) in enumerate(DIL_GROUPS):
        cols = [C_QKV + three * 768 + g * 256 for three in range(3)]
        qh, kh, vh = [_to_heads(proj[:, c0:c0 + 256], dil, MXU_DTYPE) for c0 in cols]
        qkv_heads.append((qh, kh, vh))
        o_g, lse_g = _dil_attn_fwd(qh, kh, vh, bias, g, S=S, blocks_per_seq=S // dil // SPAN)
        o_nat.append(_to_heads(_from_heads(o_g, dil, F32), 1, F32))
        lse_nat.append(_stat_to_natural(lse_g, dil))
    o_dil_h, lse_dil = _dil_merge(o_nat, lse_nat, S=S)
    o_dil = _from_heads(o_dil_h, 1, MXU_DTYPE)

    W.update(late_weights(o_dil_h))
    mem_n = _rmsnorm_fwd(mem, P["g_mem"], rows=N_MEM, name="norm_mem")
    kv = _matmul(mem_n, W["w_mem_kv"], M=N_MEM, N=2 * MEM_WIDTH, K=D_MODEL, mode="nn", bm=N_MEM, bn=512, bk=D_MODEL,
                 name="mm_kv")
    om, lse_mem = _mem_attn_fwd(proj, kv, S=S)
    b_gate = P["b_gate"].reshape(1, -1)
    merged = _mix_fwd(z_lru, o_dil, om, W["w_lru_out"], W["w_dil_out"], W["w_mem_out"], proj, b_gate, S=S)
    x1 = _matmul(merged, W["w_out"], M=S, N=D_MODEL, K=D_MODEL, mode="nn", bm=512, bn=D_MODEL, bk=D_MODEL, name="mm_out",
                 epilogue=lambda acc, r: (r + acc,), extras=[(x, (0, 0))])
    hm = _rmsnorm_fwd(x1, P["g_mlp"], rows=S, name="norm_mlp")

    def relu2(acc):
        rl = jnp.maximum(acc, 0.0)
        return acc, rl * rl

    u, act = _matmul(hm, W["w_mlp_in"], M=S, N=D_FF, K=D_MODEL, mode="nn", bm=512, bn=1024, bk=D_MODEL, name="mm_mlp_in",
                     out_dtypes=(F32, MXU_DTYPE), epilogue=relu2, j_outer=True)
    x2 = _matmul(act, W["w_mlp_out"], M=S, N=D_MODEL, K=D_FF, mode="nn", bm=512, bn=D_MODEL, bk=1024, name="mm_mlp_out",
                 epilogue=lambda acc, r: (r + acc,), extras=[(x1, (0, 0))])
    loss, dx2, dg_final = _loss_head(x2, P["g_final"], tgt, rows=S)

    G, Gs = {}, {}
    Gs["g_final"] = dg_final
    G["w_mlp_out"] = _matmul(act, dx2, M=D_FF, N=D_MODEL, K=S, mode="tn", bm=1024, bn=D_MODEL, bk=512, name="mm_dw_mlp_out")
    du = _matmul(dx2, W["w_mlp_out"], M=S, N=D_FF, K=D_MODEL, mode="nt", bm=512, bn=1024, bk=D_MODEL, name="mm_du",
                 out_dtypes=(MXU_DTYPE,), epilogue=lambda acc, uu: (acc * 2.0 * jnp.maximum(uu, 0.0),),
                 extras=[(u, (0, 0))], j_outer=True)
    G["w_mlp_in"] = _matmul(hm, du, M=D_MODEL, N=D_FF, K=S, mode="tn", bm=D_MODEL, bn=1024, bk=512, name="mm_dw_mlp_in")
    tie1 = send_grads({n: G.pop(n) for n in ("w_mlp_out", "w_mlp_in")})
    dhm = _matmul(du, W["w_mlp_in"], M=S, N=D_MODEL, K=D_FF, mode="nt", bm=512, bn=D_MODEL, bk=1024, name="mm_dhm",
                  deps=[tie1])
    dx1, Gs["g_mlp"] = _rmsnorm_bwd(x1, P["g_mlp"], dhm, dx2, rows=S, name="norm_mlp_bwd")
    G["w_out"] = _matmul(merged, dx1, M=D_MODEL, N=D_MODEL, K=S, mode="tn", bm=D_MODEL, bn=D_MODEL, bk=512, name="mm_dw_out")
    dmerged = _matmul(dx1, W["w_out"], M=S, N=D_MODEL, K=D_MODEL, mode="nt", bm=512, bn=D_MODEL, bk=D_MODEL, name="mm_dmerged")
    (dg0, dg1, dg2, dy_lru, dy_dil, dy_mem, db0, db1, db2) = _mix_bwd(
        dmerged, z_lru, o_dil, om, W["w_lru_out"], W["w_dil_out"], W["w_mem_out"], proj, b_gate, S=S)
    Gs["b_gate"] = jnp.concatenate([db0, db1, db2], axis=1)

    G["w_mem_out"] = _matmul(om, dy_mem, M=MEM_WIDTH, N=D_MODEL, K=S, mode="tn", bm=MEM_WIDTH, bn=D_MODEL, bk=512,
                             name="mm_dw_mem_out")
    dom = _matmul(dy_mem, W["w_mem_out"], M=S, N=MEM_WIDTH, K=D_MODEL, mode="nt", bm=512, bn=MEM_WIDTH, bk=D_MODEL,
                  name="mm_dom")
    dqm, dk_mem, dv_mem = _mem_attn_bwd(proj, kv, om, lse_mem, dom, S=S)
    dkv = jnp.concatenate([dk_mem, dv_mem], axis=1)
    G["w_mem_kv"] = _matmul(mem_n, dkv, M=D_MODEL, N=2 * MEM_WIDTH, K=N_MEM, mode="tn", bm=D_MODEL, bn=2 * MEM_WIDTH,
                            bk=N_MEM, name="mm_dw_kv")
    dmem_n = _matmul(dkv, W["w_mem_kv"], M=N_MEM, N=D_MODEL, K=2 * MEM_WIDTH, mode="nt", bm=N_MEM, bn=D_MODEL,
                     bk=2 * MEM_WIDTH, name="mm_dmem")
    _, Gs["g_mem"] = _rmsnorm_bwd(mem, P["g_mem"], dmem_n, None, rows=N_MEM, name="norm_mem_bwd", want_dx=False)

    G["w_dil_out"] = _matmul(o_dil, dy_dil, M=256, N=D_MODEL, K=S, mode="tn", bm=256, bn=D_MODEL, bk=512, name="mm_dw_dil_out")
    do_dil = _matmul(dy_dil, W["w_dil_out"], M=S, N=256, K=D_MODEL, mode="nt", bm=512, bn=256, bk=D_MODEL, name="mm_do_dil")
    G["w_lru_out"] = _matmul(z_lru, dy_lru, M=D_RNN, N=D_MODEL, K=S, mode="tn", bm=D_RNN, bn=D_MODEL, bk=512, name="mm_dw_lru_out")
    dz = _matmul(dy_lru, W["w_lru_out"], M=S, N=D_RNN, K=D_MODEL, mode="nt", bm=512, bn=D_RNN, bk=D_MODEL, name="mm_dz_lru")
    tie2 = send_grads({n: G.pop(n) for n in ("w_out", "w_mem_out", "w_mem_kv", "w_dil_out", "w_lru_out")})
    bias = bias + tie2[0, 0]
    do_h = _to_heads(do_dil, 1, F32)
    delta = _row_dot(do_h, o_dil_h, S=S)
    dq_parts, dk_parts, dv_parts, dbias = [], [], [], []
    for g, (window, dil) in enumerate(DIL_GROUPS):
        qh, kh, vh = qkv_heads[g]
        do_g = _to_heads(do_dil, dil, MXU_DTYPE)
        dq_g, dk_g, dv_g, db_g = _dil_attn_bwd(qh, kh, vh, do_g, _stat_to_dilated(lse_dil, dil), _stat_to_dilated(delta, dil),
                                               bias, g, S=S, blocks_per_seq=S // dil // SPAN)
        dq_parts.append(_from_heads(dq_g, dil, MXU_DTYPE))
        dk_parts.append(_from_heads(dk_g, dil, MXU_DTYPE))
        dv_parts.append(_from_heads(dv_g, dil, MXU_DTYPE))
        dbias.append(db_g)
    drel = _dil_bias_bwd(jnp.concatenate(dbias, axis=0), buckets)
    Gs["rel_bias"] = drel[:, :3 * DIL_HEADS]

    dxl, dgl, dcw, dcb, dwa, dwx, dba, dbx, dlam = _lru_bwd(proj, hl, dz, *lru_args, S=S)
    Gs["conv_w"], Gs["conv_b"] = dcw, dcb
    Gs["w_rg_a"], Gs["w_rg_x"] = _block_diag_extract(dwa), _block_diag_extract(dwx)
    Gs["b_rg_a"], Gs["b_rg_x"], Gs["lru_lambda"] = dba, dbx, dlam

    dproj = jnp.concatenate([dxl, dgl] + dq_parts + dk_parts + dv_parts + [dqm, dg0, dg1, dg2], axis=1)
    dh = _matmul(dproj, W["w_in"], M=S, N=D_MODEL, K=D_IN, mode="nt", bm=512, bn=D_MODEL, bk=D_IN // 2, name="mm_dh")
    grad_x, Gs["g_mix"] = _rmsnorm_bwd(x, P["g_mix"], dh, dx1, rows=S, name="norm_mix_bwd")
    reduced = reduce_small(Gs, loss)
    half = D_MODEL // 2
    dw_a = _matmul(h, dproj, M=half, N=D_IN, K=S, mode="tn", bm=half, bn=D_IN // 2, bk=256, name="mm_dw_in_a",
                   deps=[reduced])
    tie_a = send_grads({"w_in_a": dw_a})
    dw_b = _matmul(h, dproj, M=half, N=D_IN, K=S, mode="tn", bm=half, bn=D_IN // 2, bk=256, name="mm_dw_in_b",
                   a_off=(0, 1), deps=[tie_a])
    tie_b = send_grads({"w_in_b": dw_b})
    return grad_x, tie_b


BIG = ("w_in", "w_lru_out", "w_dil_out", "w_mem_kv", "w_mem_out", "w_out", "w_mlp_in", "w_mlp_out")
COL_SHARDED = ("w_in", "w_in_a", "w_in_b", "w_lru_out", "w_dil_out", "w_mem_out", "w_mlp_in")
SMALL = ("g_mix", "b_gate", "conv_b", "w_rg_a", "b_rg_a", "w_rg_x", "b_rg_x", "lru_lambda", "rel_bias", "g_mem",
         "g_mlp", "g_final")
WEIGHTS = ("g_mix", "w_in", "b_gate", "conv_w", "conv_b", "w_rg_a", "b_rg_a", "w_rg_x", "b_rg_x", "lru_lambda",
           "w_lru_out", "rel_bias", "w_dil_out", "g_mem", "w_mem_kv", "w_mem_out", "w_out", "g_mlp", "w_mlp_in",
           "w_mlp_out", "g_final")
SMALL_SHAPES = {"g_mix": (1024,), "b_gate": (3072,), "conv_b": (768,), "w_rg_a": (12, 64, 64), "b_rg_a": (768,),
                "w_rg_x": (12, 64, 64), "b_rg_x": (768,), "lru_lambda": (768,), "rel_bias": (32, 12), "g_mem": (1024,),
                "g_mlp": (1024,), "g_final": (1024,)}


def _gathered_to_full(name, gathered):
    if name in COL_SHARDED:
        n, r, c = gathered.shape
        return gathered.transpose(1, 0, 2).reshape(r, n * c)
    n, r, c = gathered.shape
    return gathered.reshape(n * r, c)


def _full_to_parts(name, full):
    if name in COL_SHARDED:
        r, nc = full.shape
        return _mx(full.reshape(r, N_DEV, nc // N_DEV).transpose(1, 0, 2))
    nr, c = full.shape
    return _mx(full.reshape(N_DEV, nr // N_DEV, c))


def _own_part(name, full, my_idx):
    if name in COL_SHARDED:
        r, nc = full.shape
        return lax.dynamic_slice(full, (0, my_idx * (nc // N_DEV)), (r, nc // N_DEV))
    nr, c = full.shape
    return lax.dynamic_slice(full, (my_idx * (nr // N_DEV), 0), (nr // N_DEV, c))


def _pack(parts):
    flat = jnp.concatenate([p.reshape(-1).astype(F32) for p in parts])
    pad = (-flat.shape[0]) % 1024
    return jnp.pad(flat, (0, pad)).reshape(-1, 128)


def _unpack(pack, shapes):
    flat = pack.reshape(-1)
    out, off = [], 0
    for shp in shapes:
        size = math.prod(shp)
        out.append(flat[off:off + size].reshape(shp))
        off += size
    return out


def kernel(x, mem, g_mix, w_in, b_gate, conv_w, conv_b, w_rg_a, b_rg_a, w_rg_x, b_rg_x, lru_lambda, w_lru_out, rel_bias, w_dil_out, g_mem, w_mem_kv, w_mem_out, w_out, g_mlp, w_mlp_in, w_mlp_out, g_final, loss_target, m_g_mix, m_w_in, m_b_gate, m_conv_w, m_conv_b, m_w_rg_a, m_b_rg_a, m_w_rg_x, m_b_rg_x, m_lru_lambda, m_w_lru_out, m_rel_bias, m_w_dil_out, m_g_mem, m_w_mem_kv, m_w_mem_out, m_w_out, m_g_mlp, m_w_mlp_in, m_w_mlp_out, m_g_final, v_g_mix, v_w_in, v_b_gate, v_conv_w, v_conv_b, v_w_rg_a, v_b_rg_a, v_w_rg_x, v_b_rg_x, v_lru_lambda, v_w_lru_out, v_rel_bias, v_w_dil_out, v_g_mem, v_w_mem_kv, v_w_mem_out, v_w_out, v_g_mlp, v_w_mlp_in, v_w_mlp_out, v_g_final):
    w = dict(g_mix=g_mix, w_in=w_in, b_gate=b_gate, conv_w=conv_w, conv_b=conv_b, w_rg_a=w_rg_a, b_rg_a=b_rg_a,
             w_rg_x=w_rg_x, b_rg_x=b_rg_x, lru_lambda=lru_lambda, w_lru_out=w_lru_out, rel_bias=rel_bias,
             w_dil_out=w_dil_out, g_mem=g_mem, w_mem_kv=w_mem_kv, w_mem_out=w_mem_out, w_out=w_out, g_mlp=g_mlp,
             w_mlp_in=w_mlp_in, w_mlp_out=w_mlp_out, g_final=g_final)
    m = dict(g_mix=m_g_mix, w_in=m_w_in, b_gate=m_b_gate, conv_w=m_conv_w, conv_b=m_conv_b, w_rg_a=m_w_rg_a,
             b_rg_a=m_b_rg_a, w_rg_x=m_w_rg_x, b_rg_x=m_b_rg_x, lru_lambda=m_lru_lambda, w_lru_out=m_w_lru_out,
             rel_bias=m_rel_bias, w_dil_out=m_w_dil_out, g_mem=m_g_mem, w_mem_kv=m_w_mem_kv, w_mem_out=m_w_mem_out,
             w_out=m_w_out, g_mlp=m_g_mlp, w_mlp_in=m_w_mlp_in, w_mlp_out=m_w_mlp_out, g_final=m_g_final)
    v = dict(g_mix=v_g_mix, w_in=v_w_in, b_gate=v_b_gate, conv_w=v_conv_w, conv_b=v_conv_b, w_rg_a=v_w_rg_a,
             b_rg_a=v_b_rg_a, w_rg_x=v_w_rg_x, b_rg_x=v_b_rg_x, lru_lambda=v_lru_lambda, w_lru_out=v_w_lru_out,
             rel_bias=v_rel_bias, w_dil_out=v_w_dil_out, g_mem=v_g_mem, w_mem_kv=v_w_mem_kv, w_mem_out=v_w_mem_out,
             w_out=v_w_out, g_mlp=v_g_mlp, w_mlp_in=v_w_mlp_in, w_mlp_out=v_w_mlp_out, g_final=v_g_final)

    my_idx = _dev_index(_my_pos())

    g_in, g_cw = _all_gather([_mx(w["w_in"]), w["conv_w"]])
    W = {"w_in": _gathered_to_full("w_in", g_in), "conv_w": g_cw.transpose(1, 0, 2).reshape(CONV_WIDTH, D_RNN)}
    late = [n for n in BIG if n != "w_in"]
    late_shards = [_mx(w[n]) for n in late]
    late_started = _push_start(late_shards, [(N_DEV,) + s.shape for s in late_shards], _gather_refs,
                                "gather_late_start")
    P = {n: w[n] for n in SMALL}

    def late_weights(after):
        lands = _push_wait(late_started, after)
        out = {}
        for n, land, own in zip(late, lands, late_shards):
            full = lax.dynamic_update_index_in_dim(land, own, my_idx, 0)
            out[n] = _gathered_to_full(n, full)
        return out

    sent, small = [], {}
    small_names = SMALL + ("conv_w",)

    def send_grads(gs):
        names = list(gs)
        parts = [_full_to_parts(n, gs[n]) for n in names]
        own = [_own_part(n, gs[n], my_idx) for n in names]
        started = _push_start(parts, [(N_DEV - 1,) + p.shape[1:] for p in parts], _scatter_refs,
                              f"scatter{len(sent)}_start")
        sent.append((names, own, started))
        return started["token"]

    def reduce_small(gs, loss):
        small["total"] = _all_reduce_small(_pack([gs[n] for n in small_names] + [loss]))
        return small["total"]

    grad_x, last_token = _local_step(x[0], mem[0], loss_target[0], W, P, late_weights, send_grads, reduce_small,
                                     late_started["token"][0, 0])
    total = small["total"]
    small_shapes = [SMALL_SHAPES[n] for n in SMALL] + [(CONV_WIDTH, D_RNN), (1,)]
    summed = dict(zip(small_names + ("loss",), _unpack(total, small_shapes)))

    grads, deltas, new_m, new_v = {}, {}, {}, {}
    after = last_token
    for names, own, started in sent[:-2]:
        for n, o, land in zip(names, own, _push_wait(started, after)):
            grads[n], deltas[n], new_m[n], new_v[n] = _adamw_landed(w[n], o, land, m[n], v[n], name=f"adamw_{n}")
            after = deltas[n]
    prev = None
    for (names, own, started), row0 in zip(sent[-2:], (0, D_MODEL // 2)):
        (land,) = _push_wait(started, after)
        prev = _adamw_landed(w["w_in"], own[0], land, m["w_in"], v["w_in"], name=f"adamw_{names[0]}", row0=row0,
                             prev=prev)
    grads["w_in"], deltas["w_in"], new_m["w_in"], new_v["w_in"] = prev
    zeros_cw = jnp.zeros((CONV_WIDTH, D_RNN), F32)
    w_pack = _pack([w[n] for n in SMALL] + [zeros_cw, jnp.zeros((1,), F32)])
    m_pack = _pack([m[n] for n in SMALL] + [zeros_cw, jnp.zeros((1,), F32)])
    v_pack = _pack([v[n] for n in SMALL] + [zeros_cw, jnp.zeros((1,), F32)])
    d_pack, nm_pack, nv_pack = _adamw_plain(w_pack, total, m_pack, v_pack, name="adamw_small")
    for dst, pk in ((deltas, d_pack), (new_m, nm_pack), (new_v, nv_pack)):
        dst.update(zip(SMALL, _unpack(pk, [SMALL_SHAPES[n] for n in SMALL])))
    for n in SMALL:
        grads[n] = summed[n]
    cw_cols = D_RNN // N_DEV
    grads["conv_w"] = lax.dynamic_slice(summed["conv_w"], (0, my_idx * cw_cols), (CONV_WIDTH, cw_cols))
    deltas["conv_w"], new_m["conv_w"], new_v["conv_w"] = _adamw_plain(
        w["conv_w"], grads["conv_w"], m["conv_w"], v["conv_w"], name="adamw_conv_w")

    return (summed["loss"].reshape(()), grad_x[None], *[grads[n] for n in WEIGHTS], *[deltas[n] for n in WEIGHTS],
            *[new_m[n] for n in WEIGHTS], *[new_v[n] for n in WEIGHTS])
```

```python
import functools
import math

import jax
import jax.numpy as jnp
from jax import lax
from jax.experimental import pallas as pl
from jax.experimental.pallas import tpu as pltpu

F32 = jnp.float32
MXU_DTYPE = jnp.bfloat16
VMEM_LIMIT_BYTES = 56 * 1024 * 1024
N_DEV = 8

D_MODEL = 1024
N_MEM = 256
MEM_HEADS = 4
MEM_HEAD_DIM = 128
MEM_WIDTH = 512
D_RNN = 768
LRU_BLOCK = 64
N_LRU_BLOCKS = 12
LRU_GROUP = 256
N_LRU_GROUPS = 3
CONV_WIDTH = 4
LRU_C = 8.0
DIL_GROUPS = ((128, 1), (512, 4), (2048, 16))
SPAN = 128
DIL_HEADS = 4
DIL_HEAD_DIM = 64
NUM_BUCKETS = 32
MAX_DISTANCE = 2048
D_FF = 4096
D_IN = 7424
EPS = 1e-6
NEG = -1e30
C_XL, C_GATE, C_QKV, C_QM, C_GATES = 0, 768, 1536, 3840, 4352

ADAM_LR = 0.001
ADAM_B1 = 0.9
ADAM_B2 = 0.999
ADAM_EPS = 1e-08
ADAM_WD = 0.01
ADAM_STEP = 10

MESH = pl.DeviceIdType.MESH
GELU_K = math.sqrt(2.0 / math.pi)


def _cparams(sem=None):
    kw = dict(vmem_limit_bytes=VMEM_LIMIT_BYTES)
    if sem is not None:
        kw["dimension_semantics"] = sem
    return pltpu.CompilerParams(**kw)


def _mx(v):
    return v.astype(MXU_DTYPE)


def _dot(a, b, mode="nn"):
    dims = {"nn": (((1,), (0,)), ((), ())), "nt": (((1,), (1,)), ((), ())), "tn": (((0,), (0,)), ((), ()))}[mode]
    return lax.dot_general(_mx(a), _mx(b), dims, preferred_element_type=F32)


def _colsum(v):
    return jnp.sum(v, axis=0, keepdims=True)


def _matmul(a, b, *, M, N, K, mode, bm, bn, bk, name, out_dtypes=(F32,), epilogue=None, extras=(),
            a_off=(0, 0), b_off=(0, 0), j_outer=False, deps=()):
    assert M % bm == 0 and N % bn == 0 and K % bk == 0, (name, M, N, K, bm, bn, bk)
    nm, nn, nk = M // bm, N // bn, K // bk

    def ij(f):
        if j_outer:
            return lambda j, i, k: f(i, j, k)
        return f

    if mode == "tn":
        a_spec = pl.BlockSpec((bk, bm), ij(lambda i, j, k: (k + a_off[0], i + a_off[1])))
    else:
        a_spec = pl.BlockSpec((bm, bk), ij(lambda i, j, k: (i + a_off[0], k + a_off[1])))
    if mode == "nt":
        b_spec = pl.BlockSpec((bn, bk), ij(lambda i, j, k: (j + b_off[0], k + b_off[1])))
    else:
        b_spec = pl.BlockSpec((bk, bn), ij(lambda i, j, k: (k + b_off[0], j + b_off[1])))
    ex_specs = [pl.BlockSpec((bm, bn), ij(functools.partial(lambda i, j, k, o: (i + o[0], j + o[1]), o=off)))
                for _, off in extras]
    out_spec = pl.BlockSpec((bm, bn), ij(lambda i, j, k: (i, j)))
    n_ex, n_out, n_dep = len(extras), len(out_dtypes), len(deps)

    def body(*refs):
        a_ref, b_ref = refs[0], refs[1]
        ex = refs[2:2 + n_ex]
        outs = refs[2 + n_ex + n_dep:2 + n_ex + n_dep + n_out]
        part = _dot(a_ref[...], b_ref[...], mode)

        def finish(acc):
            vals = epilogue(acc, *[e[...] for e in ex]) if epilogue is not None else (acc,)
            for o, v in zip(outs, vals):
                o[...] = v.astype(o.dtype)

        if nk == 1:
            finish(part)
        else:
            acc_ref = refs[-1]
            k = pl.program_id(2)

            @pl.when(k == 0)
            def _():
                acc_ref[...] = part

            @pl.when(k > 0)
            def _():
                acc_ref[...] += part

            @pl.when(k == nk - 1)
            def _():
                finish(acc_ref[...])

    grid = (nn, nm, nk) if j_outer else (nm, nn, nk)
    res = pl.pallas_call(
        body, name=name, grid=grid,
        in_specs=[a_spec, b_spec] + ex_specs + [pl.BlockSpec(memory_space=pl.ANY)] * n_dep,
        out_specs=[out_spec] * n_out,
        out_shape=[jax.ShapeDtypeStruct((M, N), dt) for dt in out_dtypes],
        scratch_shapes=[pltpu.VMEM((bm, bn), F32)] if nk > 1 else [],
        compiler_params=_cparams(("parallel", "parallel", "arbitrary")),
    )(a, b, *[e for e, _ in extras], *deps)
    return res[0] if n_out == 1 else res


def _rmsnorm_fwd(x, g, *, rows, name, bt=512):
    bt = min(bt, rows)

    def body(x_ref, g_ref, o_ref):
        xv = x_ref[...]
        r = lax.rsqrt(jnp.mean(xv * xv, axis=-1, keepdims=True) + EPS)
        o_ref[...] = (xv * r * g_ref[...]).astype(o_ref.dtype)

    return pl.pallas_call(
        body, name=name, grid=(rows // bt,),
        in_specs=[pl.BlockSpec((bt, D_MODEL), lambda i: (i, 0)), pl.BlockSpec((1, D_MODEL), lambda i: (0, 0))],
        out_specs=pl.BlockSpec((bt, D_MODEL), lambda i: (i, 0)),
        out_shape=jax.ShapeDtypeStruct((rows, D_MODEL), MXU_DTYPE),
        compiler_params=_cparams(("parallel",)),
    )(x, g.reshape(1, D_MODEL))


def _rms_bwd_tile(xv, gv, dyv):
    r = lax.rsqrt(jnp.mean(xv * xv, axis=-1, keepdims=True) + EPS)
    w = dyv * gv
    dx = r * w - xv * (r * r * r) * jnp.mean(w * xv, axis=-1, keepdims=True)
    dg = _colsum(dyv * xv * r)
    return dx, dg


def _rmsnorm_bwd(x, g, dy, res, *, rows, name, bt=512, want_dx=True):
    bt = min(bt, rows)
    has_res = res is not None

    def body(*refs):
        x_ref, g_ref, dy_ref = refs[:3]
        res_ref = refs[3] if has_res else None
        outs = refs[3 + int(has_res):]
        dx, dg = _rms_bwd_tile(x_ref[...], g_ref[...], dy_ref[...])
        if has_res:
            dx = dx + res_ref[...]
        dg_ref = outs[-1]

        @pl.when(pl.program_id(0) == 0)
        def _():
            dg_ref[...] = jnp.zeros_like(dg_ref)

        dg_ref[...] += dg
        if want_dx:
            outs[0][...] = dx

    row_spec = pl.BlockSpec((bt, D_MODEL), lambda i: (i, 0))
    vec_spec = pl.BlockSpec((1, D_MODEL), lambda i: (0, 0))
    ins = [x, g.reshape(1, D_MODEL), dy] + ([res] if has_res else [])
    out_shape = ([jax.ShapeDtypeStruct((rows, D_MODEL), F32)] if want_dx else []) + [jax.ShapeDtypeStruct((1, D_MODEL), F32)]
    out_specs = ([row_spec] if want_dx else []) + [vec_spec]
    res_ = pl.pallas_call(
        body, name=name, grid=(rows // bt,),
        in_specs=[row_spec, vec_spec, row_spec] + ([row_spec] if has_res else []),
        out_specs=out_specs, out_shape=out_shape,
        compiler_params=_cparams(("arbitrary",)),
    )(*ins)
    return res_ if want_dx else (None, res_[0])


def _loss_head(x2, g, tgt, *, rows, bt=512):
    def body(x_ref, g_ref, t_ref, loss_ref, dx_ref, dg_ref):
        xv, gv = x_ref[...], g_ref[...]
        r = lax.rsqrt(jnp.mean(xv * xv, axis=-1, keepdims=True) + EPS)
        diff = xv * r * gv - t_ref[...]
        part = jnp.sum(jnp.mean(diff * diff, axis=-1, keepdims=True), axis=0, keepdims=True) * 0.5
        dx, dg = _rms_bwd_tile(xv, gv, diff * (1.0 / D_MODEL))

        @pl.when(pl.program_id(0) == 0)
        def _():
            loss_ref[...] = jnp.zeros_like(loss_ref)
            dg_ref[...] = jnp.zeros_like(dg_ref)

        loss_ref[...] += part
        dg_ref[...] += dg
        dx_ref[...] = dx

    row_spec = pl.BlockSpec((bt, D_MODEL), lambda i: (i, 0))
    vec_spec = pl.BlockSpec((1, D_MODEL), lambda i: (0, 0))
    return pl.pallas_call(
        body, name="loss_head", grid=(rows // bt,),
        in_specs=[row_spec, vec_spec, row_spec],
        out_specs=[pl.BlockSpec((1, 1), lambda i: (0, 0)), row_spec, vec_spec],
        out_shape=[jax.ShapeDtypeStruct((1, 1), F32), jax.ShapeDtypeStruct((rows, D_MODEL), F32),
                   jax.ShapeDtypeStruct((1, D_MODEL), F32)],
        compiler_params=_cparams(("arbitrary",)),
    )(x2, g.reshape(1, D_MODEL), tgt)


LRU_T = 256


def _gelu(x):
    t = jnp.tanh(GELU_K * (x + 0.044715 * x * x * x))
    return 0.5 * x * (1.0 + t), t


def _gelu_grad(x, t):
    return 0.5 * (1.0 + t) + 0.5 * x * (1.0 - t * t) * GELU_K * (1.0 + 3.0 * 0.044715 * x * x)


def _softplus_neg(lam):
    z = -lam
    u = jnp.exp(-jnp.abs(z))
    w = 1.0 + u
    l1p = jnp.where(w == 1.0, u, jnp.log(w) * u / jnp.where(w == 1.0, 1.0, w - 1.0))
    return jnp.maximum(z, 0.0) + l1p


def _shift_down(cur, prev8, k, row8):
    y = pltpu.roll(cur, k, 0)
    head = jnp.where(row8 < k, pltpu.roll(prev8, k, 0), y[0:8])
    return jnp.concatenate([head, y[8:]], axis=0)


def _shift_up(cur, next8, k, row8):
    n = cur.shape[0]
    y = pltpu.roll(cur, n - k, 0)
    tail = jnp.where(row8 >= 8 - k, pltpu.roll(next8, 8 - k, 0), y[n - 8:n])
    return jnp.concatenate([y[0:n - 8], tail], axis=0)


def _lru_gates(xl, p8, cw, cb, wa, wx, ba, bx, lam, row8):
    sh = [xl] + [_shift_down(xl, p8, k, row8) for k in (1, 2, 3)]
    xc = cb + cw[3:4] * sh[0] + cw[2:3] * sh[1] + cw[1:2] * sh[2] + cw[0:1] * sh[3]
    r = jax.nn.sigmoid(_dot(xc, wa) + ba)
    i = jax.nn.sigmoid(_dot(xc, wx) + bx)
    sp = _softplus_neg(lam)
    la = -LRU_C * r * sp
    a = jnp.exp(la)
    mult = jnp.sqrt(jnp.tanh(-la) * (a * a + 1.0))
    return dict(sh=sh, xc=xc, r=r, i=i, sp=sp, a=a, mult=mult)


def _lru_specs(n_t, reverse):
    T = LRU_T
    tt = (lambda t: n_t - 1 - t) if reverse else (lambda t: t)
    blk = lambda col0: pl.BlockSpec((T, LRU_GROUP), lambda g, t: (tt(t), col0 + g))
    prev8 = lambda col0: pl.BlockSpec((8, LRU_GROUP), lambda g, t: (jnp.maximum(tt(t) * (T // 8) - 1, 0), col0 + g))
    vec = lambda rows: pl.BlockSpec((rows, LRU_GROUP), lambda g, t: (0, g))
    wbd = pl.BlockSpec((1, LRU_GROUP, LRU_GROUP), lambda g, t: (g, 0, 0))
    return blk, prev8, vec, wbd


def _lru_fwd(proj, conv_w, conv_b, wa_bd, wx_bd, b_a, b_x, lam, *, S):
    T = LRU_T
    n_t = S // T
    blk, _, vec, wbd = _lru_specs(n_t, False)

    def body(xl_ref, gate_ref, cw_ref, cb_ref, wa_ref, wx_ref, ba_ref, bx_ref, lam_ref,
             hl_ref, z_ref, prev8, hcar, a_s, b_s):
        @pl.when(pl.program_id(1) == 0)
        def _():
            prev8[...] = jnp.zeros_like(prev8)
            hcar[...] = jnp.zeros_like(hcar)

        row8 = lax.broadcasted_iota(jnp.int32, (8, LRU_GROUP), 0)
        xl = xl_ref[...]
        q = _lru_gates(xl, prev8[...], cw_ref[...], cb_ref[...], wa_ref[0], wx_ref[0], ba_ref[...], bx_ref[...],
                       lam_ref[...], row8)
        prev8[...] = xl[T - 8:T]
        a_s[...] = q["a"]
        b_s[...] = q["mult"] * q["i"] * q["xc"]

        def step(c, carry):
            off = pl.multiple_of(c * 8, 8)
            A = a_s[pl.ds(off, 8), :]
            B = b_s[pl.ds(off, 8), :]
            for k in (1, 2, 4):
                a_sh = jnp.where(row8 >= k, pltpu.roll(A, k, 0), 1.0)
                b_sh = jnp.where(row8 >= k, pltpu.roll(B, k, 0), 0.0)
                B = A * b_sh + B
                A = A * a_sh
            h = A * carry + B
            hl_ref[pl.ds(off, 8), :] = h
            return h[7:8, :]

        hcar[...] = lax.fori_loop(0, T // 8, step, hcar[...])
        ge, _ = _gelu(gate_ref[...])
        z_ref[...] = (ge * hl_ref[...]).astype(z_ref.dtype)

    return pl.pallas_call(
        body, name="lru_fwd", grid=(N_LRU_GROUPS, n_t),
        in_specs=[blk(C_XL // LRU_GROUP), blk(C_GATE // LRU_GROUP), vec(4), vec(1), wbd, wbd, vec(1), vec(1), vec(1)],
        out_specs=[blk(0), blk(0)],
        out_shape=[jax.ShapeDtypeStruct((S, D_RNN), F32), jax.ShapeDtypeStruct((S, D_RNN), MXU_DTYPE)],
        scratch_shapes=[pltpu.VMEM((8, LRU_GROUP), F32), pltpu.VMEM((1, LRU_GROUP), F32),
                        pltpu.VMEM((T, LRU_GROUP), F32), pltpu.VMEM((T, LRU_GROUP), F32)],
        compiler_params=_cparams(("parallel", "arbitrary")),
    )(proj, proj, conv_w, conv_b, wa_bd, wx_bd, b_a, b_x, lam)


def _lru_bwd(proj, hl, dz, conv_w, conv_b, wa_bd, wx_bd, b_a, b_x, lam, *, S):
    T = LRU_T
    n_t = S // T
    blk, prev8s, vec, wbd = _lru_specs(n_t, True)

    def body(xl_ref, xlp_ref, gate_ref, hl_ref, hlp_ref, dz_ref, cw_ref, cb_ref, wa_ref, wx_ref, ba_ref, bx_ref,
             lam_ref, dxl_ref, dgate_ref, dcw_ref, dcb_ref, dwa_ref, dwx_ref, dba_ref, dbx_ref, dlam_ref,
             next8, gcar, c_s, b_s, l_s):
        t = pl.program_id(1)
        first_chunk = t == n_t - 1

        @pl.when(t == 0)
        def _():
            next8[...] = jnp.zeros_like(next8)
            gcar[...] = jnp.zeros_like(gcar)
            for ref in (dcw_ref, dcb_ref, dwa_ref, dwx_ref, dba_ref, dbx_ref, dlam_ref):
                ref[...] = jnp.zeros_like(ref)

        row8 = lax.broadcasted_iota(jnp.int32, (8, LRU_GROUP), 0)
        rowT = lax.broadcasted_iota(jnp.int32, (T, LRU_GROUP), 0)
        keep = jnp.where(first_chunk, 0.0, 1.0)
        xl = xl_ref[...]
        wa, wx, lam_v = wa_ref[0], wx_ref[0], lam_ref[...]
        q = _lru_gates(xl, xlp_ref[...] * keep, cw_ref[...], cb_ref[...], wa, wx, ba_ref[...], bx_ref[...], lam_v, row8)
        a, mult, r, i, xc, sp = q["a"], q["mult"], q["r"], q["i"], q["xc"], q["sp"]
        hl_v = hl_ref[...]
        dz_v = dz_ref[...]
        gate = gate_ref[...]
        ge, th = _gelu(gate)
        dgate_ref[...] = (dz_v * hl_v * _gelu_grad(gate, th)).astype(dgate_ref.dtype)

        c_s[...] = jnp.where(rowT == T - 1, 0.0, pltpu.roll(a, T - 1, 0))
        b_s[...] = dz_v * ge + jnp.where(rowT == T - 1, gcar[...], 0.0)

        def step(n, carry):
            off = pl.multiple_of((T // 8 - 1 - n) * 8, 8)
            C = c_s[pl.ds(off, 8), :]
            B = b_s[pl.ds(off, 8), :]
            for k in (1, 2, 4):
                c_sh = jnp.where(row8 < 8 - k, pltpu.roll(C, 8 - k, 0), 1.0)
                b_sh = jnp.where(row8 < 8 - k, pltpu.roll(B, 8 - k, 0), 0.0)
                B = B + C * b_sh
                C = C * c_sh
            lam_t = B + C * carry
            l_s[pl.ds(off, 8), :] = lam_t
            return lam_t[0:1, :]

        lax.fori_loop(0, T // 8, step, jnp.zeros((1, LRU_GROUP), F32))
        lmb = l_s[...]
        gcar[...] = a[0:1, :] * lmb[0:1, :]

        h_prev = _shift_down(hl_v, hlp_ref[...] * keep, 1, row8)
        da = lmb * h_prev
        dmult = lmb * i * xc
        di = lmb * mult * xc
        dxc = lmb * mult * i
        dla = da * a - dmult * (a * a) / mult
        dr = dla * (-LRU_C * sp)
        dlam_ref[...] += _colsum(dla * (-LRU_C * r)) * (-jax.nn.sigmoid(-lam_v))
        dpa = dr * r * (1.0 - r)
        dpx = di * i * (1.0 - i)
        dxc = dxc + _dot(dpa, wa, "nt") + _dot(dpx, wx, "nt")
        dwa_ref[0] += _dot(xc, dpa, "tn")
        dwx_ref[0] += _dot(xc, dpx, "tn")
        dba_ref[...] += _colsum(dpa)
        dbx_ref[...] += _colsum(dpx)
        dcb_ref[...] += _colsum(dxc)
        cw = cw_ref[...]
        n8 = next8[...]
        dxl = cw[3:4] * dxc
        for k in (1, 2, 3):
            dxl = dxl + cw[3 - k:4 - k] * _shift_up(dxc, n8, k, row8)
        for k in range(4):
            dcw_ref[3 - k:4 - k, :] += _colsum(dxc * q["sh"][k])
        next8[...] = dxc[0:8]
        dxl_ref[...] = dxl.astype(dxl_ref.dtype)

    res = pl.pallas_call(
        body, name="lru_bwd", grid=(N_LRU_GROUPS, n_t),
        in_specs=[blk(C_XL // LRU_GROUP), prev8s(C_XL // LRU_GROUP), blk(C_GATE // LRU_GROUP), blk(0), prev8s(0), blk(0),
                  vec(4), vec(1), wbd, wbd, vec(1), vec(1), vec(1)],
        out_specs=[blk(0), blk(0), vec(4), vec(1), wbd, wbd, vec(1), vec(1), vec(1)],
        out_shape=[jax.ShapeDtypeStruct((S, D_RNN), MXU_DTYPE), jax.ShapeDtypeStruct((S, D_RNN), MXU_DTYPE),
                   jax.ShapeDtypeStruct((4, D_RNN), F32), jax.ShapeDtypeStruct((1, D_RNN), F32),
                   jax.ShapeDtypeStruct((N_LRU_GROUPS, LRU_GROUP, LRU_GROUP), F32),
                   jax.ShapeDtypeStruct((N_LRU_GROUPS, LRU_GROUP, LRU_GROUP), F32),
                   jax.ShapeDtypeStruct((1, D_RNN), F32), jax.ShapeDtypeStruct((1, D_RNN), F32),
                   jax.ShapeDtypeStruct((1, D_RNN), F32)],
        scratch_shapes=[pltpu.VMEM((8, LRU_GROUP), F32), pltpu.VMEM((1, LRU_GROUP), F32),
                        pltpu.VMEM((T, LRU_GROUP), F32), pltpu.VMEM((T, LRU_GROUP), F32), pltpu.VMEM((T, LRU_GROUP), F32)],
        compiler_params=_cparams(("parallel", "arbitrary")),
    )(proj, proj, proj, hl, hl, dz, conv_w, conv_b, wa_bd, wx_bd, b_a, b_x, lam)
    return res


def _block_diag(w):
    w4 = w.reshape(N_LRU_GROUPS, 4, LRU_BLOCK, 1, LRU_BLOCK)
    eye = jnp.eye(4, dtype=w.dtype).reshape(1, 4, 1, 4, 1)
    return (w4 * eye).reshape(N_LRU_GROUPS, LRU_GROUP, LRU_GROUP)


def _block_diag_extract(wbd):
    w5 = wbd.reshape(N_LRU_GROUPS, 4, LRU_BLOCK, 4, LRU_BLOCK)
    return jnp.stack([w5[:, a, :, a, :] for a in range(4)], axis=1).reshape(N_LRU_BLOCKS, LRU_BLOCK, LRU_BLOCK)


def _t5_bucket(dist):
    max_exact = NUM_BUCKETS // 2
    df = jnp.maximum(dist, 1).astype(jnp.float32)
    large = max_exact + (jnp.log(df / max_exact) / math.log(MAX_DISTANCE / max_exact)
                         * (NUM_BUCKETS - max_exact)).astype(jnp.int32)
    large = jnp.minimum(large, NUM_BUCKETS - 1)
    return jnp.where(dist < max_exact, dist, large)


def _band_offsets():
    qi = jnp.arange(SPAN)[:, None]
    kj = jnp.arange(2 * SPAN)[None, :]
    return qi + SPAN - kj


def _dil_buckets():
    off = _band_offsets()
    return jnp.stack([_t5_bucket(jnp.maximum(off, 0) * dil) for _, dil in DIL_GROUPS]).astype(jnp.int32)


def _dil_bias(rel_bias, buckets):
    def body(tbl_ref, bk_ref, o_ref):
        g = pl.program_id(0)
        qi = lax.broadcasted_iota(jnp.int32, (SPAN, 2 * SPAN), 0)
        kj = lax.broadcasted_iota(jnp.int32, (SPAN, 2 * SPAN), 1)
        off = qi + SPAN - kj
        valid = (off >= 0) & (off <= SPAN)
        bk = bk_ref[0]
        for h in range(DIL_HEADS):
            acc = jnp.zeros((SPAN, 2 * SPAN), F32)
            for b in range(NUM_BUCKETS):
                acc = jnp.where(bk == b, tbl_ref[b, g * DIL_HEADS + h], acc)
            o_ref[0, h] = jnp.where(valid, acc, NEG)

    return pl.pallas_call(
        body, name="dil_bias", grid=(3,),
        in_specs=[pl.BlockSpec(memory_space=pltpu.SMEM), pl.BlockSpec((1, SPAN, 2 * SPAN), lambda g: (g, 0, 0))],
        out_specs=pl.BlockSpec((1, DIL_HEADS, SPAN, 2 * SPAN), lambda g: (g, 0, 0, 0)),
        out_shape=jax.ShapeDtypeStruct((3, DIL_HEADS, SPAN, 2 * SPAN), F32),
        compiler_params=_cparams(("parallel",)),
    )(rel_bias, buckets)


def _dil_bias_bwd(dbias, buckets):
    def body(db_ref, bk_ref, o_ref):
        lane = lax.broadcasted_iota(jnp.int32, (1, 128), 1)
        rows = [jnp.zeros((1, 128), F32) for _ in range(NUM_BUCKETS)]
        for g in range(3):
            bk = bk_ref[g]
            for h in range(DIL_HEADS):
                d = db_ref[g, h]
                for b in range(NUM_BUCKETS):
                    tot = jnp.sum(_colsum(jnp.where(bk == b, d, 0.0)), axis=1, keepdims=True)
                    rows[b] = jnp.where(lane == g * DIL_HEADS + h, tot, rows[b])
        for b in range(NUM_BUCKETS):
            o_ref[b:b + 1, :] = rows[b]

    return pl.pallas_call(
        body, name="dil_bias_bwd",
        out_shape=jax.ShapeDtypeStruct((NUM_BUCKETS, 128), F32),
        compiler_params=_cparams(),
    )(dbias, buckets)


def _dil_layout(g, S):
    dil = DIL_GROUPS[g][1]
    rows = SPAN * dil
    col = [(C_QKV + t * 768 + g * 256) // 128 for t in range(3)]
    return dil, rows, S // rows, col


def _residue_rows(r, dil):
    return pl.ds(r, SPAN, stride=dil) if dil > 1 else pl.ds(0, SPAN)


def _for_residues(dil, fn):
    if dil == 1:
        fn(0)
    else:
        lax.fori_loop(0, dil, lambda r, c: (fn(r), c)[1], 0)


def _pair_scores(qm, k2, bias, first_cols):
    s = _dot(qm, k2, "nt") * (DIL_HEAD_DIM ** -0.5) + bias
    kj = lax.broadcasted_iota(jnp.int32, s.shape, 1)
    return jnp.where(kj < first_cols, NEG, s)


def _dilated_fwd(proj, bias, g, *, S):
    dil, R, nc, (qc, kc, vc) = _dil_layout(g, S)
    cur = lambda cb: pl.BlockSpec((R, 128), lambda p, i: (i, cb + p))
    prv = lambda cb: pl.BlockSpec((R, 128), lambda p, i: (jnp.maximum(i - 1, 0), cb + p))
    out = pl.BlockSpec((R, 128), lambda p, i: (i, p))

    def body(q_ref, kp_ref, kc_ref, vp_ref, vc_ref, b_ref, o_ref, lse_ref):
        first_cols = jnp.where(pl.program_id(1) == 0, SPAN, 0)
        lane = lax.broadcasted_iota(jnp.int32, (SPAN, 128), 1)
        sels = (lane < DIL_HEAD_DIM, lane >= DIL_HEAD_DIM)

        def one(r):
            rows = _residue_rows(r, dil)
            q2 = q_ref[rows, :]
            k2 = _mx(jnp.concatenate([kp_ref[rows, :], kc_ref[rows, :]], axis=0))
            v2 = _mx(jnp.concatenate([vp_ref[rows, :], vc_ref[rows, :]], axis=0))
            stat = jnp.zeros((SPAN, 128), F32)
            o2 = jnp.zeros((SPAN, 128), F32)
            for e in range(2):
                s = _pair_scores(jnp.where(sels[e], q2, 0.0), k2, b_ref[0, e], first_cols)
                m = jnp.max(s, axis=-1, keepdims=True)
                p = jnp.exp(s - m)
                den = jnp.sum(p, axis=-1, keepdims=True)
                o2 = jnp.where(sels[e], _dot(p, v2) / den, o2)
                stat = jnp.where(lane == e, m + jnp.log(den), stat)
            o_ref[rows, :] = o2
            lse_ref[rows, :] = stat

        _for_residues(dil, one)

    return pl.pallas_call(
        body, name=f"dil_fwd{g}", grid=(2, nc),
        in_specs=[cur(qc), prv(kc), cur(kc), prv(vc), cur(vc),
                  pl.BlockSpec((1, 2, SPAN, 2 * SPAN), lambda p, i: (g, p, 0, 0))],
        out_specs=[out, out],
        out_shape=[jax.ShapeDtypeStruct((S, 256), F32), jax.ShapeDtypeStruct((S, 256), F32)],
        compiler_params=_cparams(("parallel", "parallel")),
    )(proj, proj, proj, proj, proj, bias)


def _dilated_bwd(proj, do, lse, delta, bias, g, *, S):
    dil, R, nc, (qc, kc, vc) = _dil_layout(g, S)
    cl = lambda i: jnp.minimum(i, nc - 1)
    cur = lambda cb: pl.BlockSpec((R, 128), lambda p, i: (cl(i), cb + p))
    prv = lambda cb: pl.BlockSpec((R, 128), lambda p, i: (jnp.maximum(cl(i) - 1, 0), cb + p))
    kv_out = pl.BlockSpec((R, 128), lambda p, i: (jnp.maximum(i - 1, 0), p))
    scale = DIL_HEAD_DIM ** -0.5

    def body(q_ref, kp_ref, kc_ref, vp_ref, vc_ref, do_ref, lse_ref, dl_ref, b_ref,
             dq_ref, dk_ref, dv_ref, db_ref, dq_s, kp_s, kc_s, vp_s, vc_s, kcar, vcar):
        i = pl.program_id(1)

        @pl.when(i == 0)
        def _():
            kcar[...] = jnp.zeros_like(kcar)
            vcar[...] = jnp.zeros_like(vcar)
            db_ref[...] = jnp.zeros_like(db_ref)

        @pl.when(i < nc)
        def _():
            first_cols = jnp.where(i == 0, SPAN, 0)
            lane = lax.broadcasted_iota(jnp.int32, (SPAN, 128), 1)
            sels = (lane < DIL_HEAD_DIM, lane >= DIL_HEAD_DIM)

            def one(r):
                rows = _residue_rows(r, dil)
                q2, do2 = q_ref[rows, :], do_ref[rows, :]
                k2 = _mx(jnp.concatenate([kp_ref[rows, :], kc_ref[rows, :]], axis=0))
                v2 = _mx(jnp.concatenate([vp_ref[rows, :], vc_ref[rows, :]], axis=0))
                lse_t, dl_t = lse_ref[rows, :], dl_ref[rows, :]
                dq2 = jnp.zeros((SPAN, 128), F32)
                dk2 = jnp.zeros((2 * SPAN, 128), F32)
                dv2 = jnp.zeros((2 * SPAN, 128), F32)
                for e in range(2):
                    qm = jnp.where(sels[e], q2, 0.0)
                    dom = jnp.where(sels[e], do2, 0.0)
                    p = jnp.exp(_pair_scores(qm, k2, b_ref[0, e], first_cols) - lse_t[:, e:e + 1])
                    ds = p * (_dot(dom, v2, "nt") - dl_t[:, e:e + 1])
                    db_ref[e] += ds
                    dq2 = jnp.where(sels[e], _dot(ds, k2) * scale, dq2)
                    dk2 = dk2 + _dot(ds, qm, "tn") * scale
                    dv2 = dv2 + _dot(p, dom, "tn")
                dq_s[rows, :] = dq2
                kp_s[rows, :] = dk2[0:SPAN]
                kc_s[rows, :] = dk2[SPAN:2 * SPAN]
                vp_s[rows, :] = dv2[0:SPAN]
                vc_s[rows, :] = dv2[SPAN:2 * SPAN]

            _for_residues(dil, one)
            dq_ref[...] = dq_s[...].astype(dq_ref.dtype)
            dk_ref[...] = (kcar[...] + kp_s[...]).astype(dk_ref.dtype)
            dv_ref[...] = (vcar[...] + vp_s[...]).astype(dv_ref.dtype)
            kcar[...] = kc_s[...]
            vcar[...] = vc_s[...]

        @pl.when(i == nc)
        def _():
            dk_ref[...] = kcar[...].astype(dk_ref.dtype)
            dv_ref[...] = vcar[...].astype(dv_ref.dtype)

    stat = pl.BlockSpec((R, 128), lambda p, i: (cl(i), p))
    big = jax.ShapeDtypeStruct((S, 256), MXU_DTYPE)
    return pl.pallas_call(
        body, name=f"dil_bwd{g}", grid=(2, nc + 1),
        in_specs=[cur(qc), prv(kc), cur(kc), prv(vc), cur(vc), stat, stat, stat,
                  pl.BlockSpec((1, 2, SPAN, 2 * SPAN), lambda p, i: (g, p, 0, 0))],
        out_specs=[stat, kv_out, kv_out, pl.BlockSpec((2, SPAN, 2 * SPAN), lambda p, i: (p, 0, 0))],
        out_shape=[big, big, big, jax.ShapeDtypeStruct((DIL_HEADS, SPAN, 2 * SPAN), F32)],
        scratch_shapes=[pltpu.VMEM((R, 128), F32)] * 7,
        compiler_params=_cparams(("parallel", "arbitrary")),
    )(proj, proj, proj, proj, proj, do, lse, delta, bias)


def _dilated_merge(os_, lses, *, S, bt=512):
    tile = pl.BlockSpec((bt, 128), lambda i, p: (i, p))

    def body(o0, o1, o2, l0, l1, l2, o_ref, lse_ref):
        lane = lax.broadcasted_iota(jnp.int32, (bt, 128), 1)
        lo = lane < DIL_HEAD_DIM
        ls = [l0[...], l1[...], l2[...]]
        ws, stat = [], jnp.zeros((bt, 128), F32)
        for e in range(2):
            a = [l[:, e:e + 1] for l in ls]
            m = jnp.maximum(jnp.maximum(a[0], a[1]), a[2])
            ex = [jnp.exp(v - m) for v in a]
            tot = ex[0] + ex[1] + ex[2]
            ws.append([v / tot for v in ex])
            stat = jnp.where(lane == e, m + jnp.log(tot), stat)
        acc = jnp.zeros((bt, 128), F32)
        for gi, o in enumerate((o0, o1, o2)):
            acc = acc + jnp.where(lo, ws[0][gi], ws[1][gi]) * o[...]
        o_ref[...] = acc
        lse_ref[...] = stat

    return pl.pallas_call(
        body, name="dil_merge", grid=(S // bt, 2),
        in_specs=[tile] * 6, out_specs=[tile, tile],
        out_shape=[jax.ShapeDtypeStruct((S, 256), F32), jax.ShapeDtypeStruct((S, 256), F32)],
        compiler_params=_cparams(("parallel", "parallel")),
    )(*os_, *lses)


def _dilated_delta(do, o, *, S, bt=512):
    tile = pl.BlockSpec((bt, 128), lambda i, p: (i, p))

    def body(do_ref, o_ref, d_ref):
        lane = lax.broadcasted_iota(jnp.int32, (bt, 128), 1)
        prod = do_ref[...] * o_ref[...]
        d0 = jnp.sum(jnp.where(lane < DIL_HEAD_DIM, prod, 0.0), axis=-1, keepdims=True)
        d1 = jnp.sum(jnp.where(lane >= DIL_HEAD_DIM, prod, 0.0), axis=-1, keepdims=True)
        d_ref[...] = jnp.where(lane == 0, d0, jnp.where(lane == 1, d1, 0.0))

    return pl.pallas_call(
        body, name="dil_delta", grid=(S // bt, 2), in_specs=[tile, tile], out_specs=tile,
        out_shape=jax.ShapeDtypeStruct((S, 256), F32), compiler_params=_cparams(("parallel", "parallel")),
    )(do, o)


MEM_T = 512
QM_BLK = C_QM // MEM_HEAD_DIM


def _mem_attn_fwd(proj, kv, *, S):
    scale = MEM_HEAD_DIM ** -0.5

    def body(q_ref, k_ref, v_ref, o_ref, lse_ref):
        s = _dot(q_ref[...], k_ref[...], "nt") * scale
        m = jnp.max(s, axis=-1, keepdims=True)
        p = jnp.exp(s - m)
        den = jnp.sum(p, axis=-1, keepdims=True)
        o_ref[...] = _dot(p, v_ref[...]) / den
        lse_ref[0] = m + jnp.log(den)

    return pl.pallas_call(
        body, name="mem_attn_fwd", grid=(S // MEM_T, MEM_HEADS),
        in_specs=[pl.BlockSpec((MEM_T, MEM_HEAD_DIM), lambda i, h: (i, QM_BLK + h)),
                  pl.BlockSpec((N_MEM, MEM_HEAD_DIM), lambda i, h: (0, h)),
                  pl.BlockSpec((N_MEM, MEM_HEAD_DIM), lambda i, h: (0, MEM_HEADS + h))],
        out_specs=[pl.BlockSpec((MEM_T, MEM_HEAD_DIM), lambda i, h: (i, h)),
                   pl.BlockSpec((1, MEM_T, 1), lambda i, h: (h, i, 0))],
        out_shape=[jax.ShapeDtypeStruct((S, MEM_WIDTH), F32), jax.ShapeDtypeStruct((MEM_HEADS, S, 1), F32)],
        compiler_params=_cparams(("parallel", "parallel")),
    )(proj, kv, kv)


def _mem_attn_bwd(proj, kv, om, lse, dom, *, S):
    scale = MEM_HEAD_DIM ** -0.5

    def body(q_ref, k_ref, v_ref, o_ref, lse_ref, do_ref, dq_ref, dk_ref, dv_ref):
        @pl.when(pl.program_id(1) == 0)
        def _():
            dk_ref[...] = jnp.zeros_like(dk_ref)
            dv_ref[...] = jnp.zeros_like(dv_ref)

        qv, kv_, vv, dov = q_ref[...], k_ref[...], v_ref[...], do_ref[...]
        p = jnp.exp(_dot(qv, kv_, "nt") * scale - lse_ref[0])
        delta = jnp.sum(dov * o_ref[...], axis=-1, keepdims=True)
        ds = p * (_dot(dov, vv, "nt") - delta)
        dq_ref[...] = (_dot(ds, kv_) * scale).astype(dq_ref.dtype)
        dk_ref[...] += _dot(ds, qv, "tn") * scale
        dv_ref[...] += _dot(p, dov, "tn")

    tile = pl.BlockSpec((MEM_T, MEM_HEAD_DIM), lambda h, i: (i, h))
    kvo = pl.BlockSpec((N_MEM, MEM_HEAD_DIM), lambda h, i: (0, h))
    return pl.pallas_call(
        body, name="mem_attn_bwd", grid=(MEM_HEADS, S // MEM_T),
        in_specs=[pl.BlockSpec((MEM_T, MEM_HEAD_DIM), lambda h, i: (i, QM_BLK + h)),
                  pl.BlockSpec((N_MEM, MEM_HEAD_DIM), lambda h, i: (0, h)),
                  pl.BlockSpec((N_MEM, MEM_HEAD_DIM), lambda h, i: (0, MEM_HEADS + h)),
                  tile, pl.BlockSpec((1, MEM_T, 1), lambda h, i: (h, i, 0)), tile],
        out_specs=[tile, kvo, kvo],
        out_shape=[jax.ShapeDtypeStruct((S, MEM_WIDTH), MXU_DTYPE), jax.ShapeDtypeStruct((N_MEM, MEM_WIDTH), F32),
                   jax.ShapeDtypeStruct((N_MEM, MEM_WIDTH), F32)],
        compiler_params=_cparams(("parallel", "arbitrary")),
    )(proj, kv, kv, om, lse, dom)


MIX_BM = 1024
MIX_BN = 256
GATES_BLK = C_GATES // MIX_BN


def _mix_specs(j_outer):
    ix = (lambda f: (lambda j, i: f(i, j))) if j_outer else (lambda f: f)
    act = lambda width: pl.BlockSpec((MIX_BM, width), ix(lambda i, j: (i, 0)))
    wgt = lambda width: pl.BlockSpec((width, MIX_BN), ix(lambda i, j: (0, j)))
    gate = lambda b: pl.BlockSpec((MIX_BM, MIX_BN), ix(lambda i, j: (i, GATES_BLK + 4 * b + j)))
    bias = lambda b: pl.BlockSpec((1, MIX_BN), ix(lambda i, j: (0, 4 * b + j)))
    tile = pl.BlockSpec((MIX_BM, MIX_BN), ix(lambda i, j: (i, j)))
    return act, wgt, gate, bias, tile


def _mix_fwd(z_lru, o_dil, om, w_lru, w_dil, w_mem, proj, b_gate, *, S):
    act, wgt, gate, bias, tile = _mix_specs(False)

    def body(zl, od, mo, wl, wd, wm, g0, g1, g2, b0, b1, b2, o_ref):
        acc = jax.nn.sigmoid(g0[...] + b0[...]) * _dot(zl[...], wl[...])
        acc += jax.nn.sigmoid(g1[...] + b1[...]) * _dot(od[...], wd[...])
        acc += jax.nn.sigmoid(g2[...] + b2[...]) * _dot(mo[...], wm[...])
        o_ref[...] = acc.astype(o_ref.dtype)

    return pl.pallas_call(
        body, name="mix_fwd", grid=(S // MIX_BM, D_MODEL // MIX_BN),
        in_specs=[act(D_RNN), act(256), act(MEM_WIDTH), wgt(D_RNN), wgt(256), wgt(MEM_WIDTH),
                  gate(0), gate(1), gate(2), bias(0), bias(1), bias(2)],
        out_specs=tile, out_shape=jax.ShapeDtypeStruct((S, D_MODEL), MXU_DTYPE),
        compiler_params=_cparams(("parallel", "parallel")),
    )(z_lru, o_dil, om, w_lru, w_dil, w_mem, proj, proj, proj, b_gate, b_gate, b_gate)


def _mix_bwd(dmerged, z_lru, o_dil, om, w_lru, w_dil, w_mem, proj, b_gate, *, S):
    act, wgt, gate, bias, tile = _mix_specs(True)

    def body(dm, zl, od, mo, wl, wd, wm, g0, g1, g2, b0, b1, b2,
             dg0, dg1, dg2, dy0, dy1, dy2, db0, db1, db2):
        @pl.when(pl.program_id(1) == 0)
        def _():
            for r in (db0, db1, db2):
                r[...] = jnp.zeros_like(r)

        dmv = dm[...]
        for act_ref, w_ref, g_ref, b_ref, dg_ref, dy_ref, db_ref in (
                (zl, wl, g0, b0, dg0, dy0, db0), (od, wd, g1, b1, dg1, dy1, db1), (mo, wm, g2, b2, dg2, dy2, db2)):
            y = _dot(act_ref[...], w_ref[...])
            gt = jax.nn.sigmoid(g_ref[...] + b_ref[...])
            dgate = dmv * y * gt * (1.0 - gt)
            dg_ref[...] = dgate.astype(dg_ref.dtype)
            dy_ref[...] = (dmv * gt).astype(dy_ref.dtype)
            db_ref[...] += _colsum(dgate)

    big = jax.ShapeDtypeStruct((S, D_MODEL), MXU_DTYPE)
    vec = jax.ShapeDtypeStruct((1, D_MODEL), F32)
    vspec = pl.BlockSpec((1, MIX_BN), lambda j, i: (0, j))
    return pl.pallas_call(
        body, name="mix_bwd", grid=(D_MODEL // MIX_BN, S // MIX_BM),
        in_specs=[tile, act(D_RNN), act(256), act(MEM_WIDTH), wgt(D_RNN), wgt(256), wgt(MEM_WIDTH),
                  gate(0), gate(1), gate(2), bias(0), bias(1), bias(2)],
        out_specs=[tile] * 6 + [vspec] * 3, out_shape=[big] * 6 + [vec] * 3,
        compiler_params=_cparams(("parallel", "arbitrary")),
    )(dmerged, z_lru, o_dil, om, w_lru, w_dil, w_mem, proj, proj, proj, b_gate, b_gate, b_gate)


def _adamw_math(w, g, m, v):
    m = ADAM_B1 * m + (1.0 - ADAM_B1) * g
    v = ADAM_B2 * v + (1.0 - ADAM_B2) * (g * g)
    m_hat = m / (1.0 - ADAM_B1 ** ADAM_STEP)
    v_hat = v / (1.0 - ADAM_B2 ** ADAM_STEP)
    delta = -ADAM_LR * (m_hat / (jnp.sqrt(v_hat) + ADAM_EPS) + ADAM_WD * w)
    return delta, m, v


def _adamw_landed(w, own, land, m, v, *, name, row0=0, prev=None):
    R, C = w.shape
    n_parts, rows = land.shape[0], land.shape[1]
    br = min(rows, 256)
    blk0 = row0 // br
    tile = pl.BlockSpec((br, C), lambda i: (i + blk0, 0))
    part = pl.BlockSpec((br, C), lambda i: (i, 0))
    n_prev = 0 if prev is None else 4

    def body(w_ref, o_ref, l_ref, m_ref, v_ref, *rest):
        g_ref, d_ref, nm_ref, nv_ref = rest[n_prev:]
        g = o_ref[...]
        for p in range(n_parts):
            g = g + l_ref[p].astype(F32)
        d, nm, nv = _adamw_math(w_ref[...], g, m_ref[...], v_ref[...])
        g_ref[...] = g
        d_ref[...] = d
        nm_ref[...] = nm
        nv_ref[...] = nv

    return pl.pallas_call(
        body, name=name, grid=(rows // br,),
        in_specs=[tile, part, pl.BlockSpec((n_parts, br, C), lambda i: (0, i, 0)), tile, tile]
        + [pl.BlockSpec(memory_space=pl.ANY)] * n_prev,
        out_specs=[tile] * 4, out_shape=[jax.ShapeDtypeStruct((R, C), F32)] * 4,
        input_output_aliases={5 + j: j for j in range(n_prev)},
        compiler_params=_cparams(("parallel",)),
    )(w, own, land, m, v, *(prev or ()))


def _adamw_plain(w, g, m, v, *, name):
    def body(w_ref, g_ref, m_ref, v_ref, d_ref, nm_ref, nv_ref):
        d, nm, nv = _adamw_math(w_ref[...], g_ref[...], m_ref[...], v_ref[...])
        d_ref[...] = d
        nm_ref[...] = nm
        nv_ref[...] = nv

    return pl.pallas_call(
        body, name=name, out_shape=[jax.ShapeDtypeStruct(w.shape, F32)] * 3, compiler_params=_cparams(),
    )(w, g, m, v)


def _my_pos():
    return lax.axis_index("x"), lax.axis_index("y"), lax.axis_index("c")


def _dev_index(p):
    return 4 * p[0] + 2 * p[1] + p[2]


def _all_gather(shards):
    n = len(shards)
    hbm = pl.BlockSpec(memory_space=pl.ANY)

    def body(*refs):
        ins, outs = refs[:n], refs[n:2 * n]
        send_sems, recv_sems, local_sems = refs[2 * n:]
        x, y, c = _my_pos()
        me, sibling = (x, y, c), (x, y, 1 - c)
        chips = [(1 - x, y), (x, 1 - y), (1 - x, 1 - y)]

        def copy(a, k, block, to, src=None):
            dst = outs[a].at[_dev_index(block)]
            return pltpu.make_async_remote_copy(
                src_ref=dst if src is None else src, dst_ref=dst,
                send_sem=send_sems.at[a, k], recv_sem=recv_sems.at[a, k], device_id=to, device_id_type=MESH)

        mine = [pltpu.make_async_copy(ins[a], outs[a].at[_dev_index(me)], local_sems.at[a]) for a in range(n)]
        for cp in mine:
            cp.start()
        first = []
        for a in range(n):
            first.append(copy(a, 0, me, sibling, src=ins[a]))
            first += [copy(a, 1 + j, me, (*chip, c), src=ins[a]) for j, chip in enumerate(chips)]
        for cp in first:
            cp.start()
        passed = []
        for j, chip in enumerate(chips):
            for a in range(n):
                copy(a, 1 + j, (*chip, c), me).wait_recv()
                fwd = copy(a, 4 + j, (*chip, c), sibling)
                fwd.start()
                passed.append(fwd)
        for a in range(n):
            copy(a, 0, sibling, me).wait_recv()
        for j, chip in enumerate(chips):
            for a in range(n):
                copy(a, 4 + j, (*chip, 1 - c), me).wait_recv()
        for cp in first + passed:
            cp.wait_send()
        for cp in mine:
            cp.wait()

    return pl.pallas_call(
        body, name="all_gather_weights",
        in_specs=[hbm] * n, out_specs=[hbm] * n,
        out_shape=[jax.ShapeDtypeStruct((N_DEV,) + s.shape, s.dtype) for s in shards],
        scratch_shapes=[pltpu.SemaphoreType.DMA((n, 7)), pltpu.SemaphoreType.DMA((n, 7)), pltpu.SemaphoreType.DMA((n,))],
        compiler_params=pltpu.CompilerParams(has_side_effects=True),
    )(*shards)


def _peers(me):
    x, y, c = me
    out = []
    for k in range(1, 8):
        fx, fy, fc = (k >> 2) & 1, (k >> 1) & 1, k & 1
        out.append((k - 1, (1 - x if fx else x, 1 - y if fy else y, 1 - c if fc else c)))
    return out


HBM_SPEC = pl.BlockSpec(memory_space=pltpu.HBM)
SEM_SPEC = pl.BlockSpec(memory_space=pltpu.SEMAPHORE)
DATAFLOW_EFFECT = pltpu.SideEffectType.DATAFLOW_SIDE_EFFECTING


def _gather_refs(src, land, me, peer, k):
    return src, land.at[_dev_index(me)]


def _scatter_refs(src, land, me, peer, k):
    return src.at[_dev_index(peer)], land.at[k]


def _push_start(srcs, land_shapes, refs_of, name):
    n = len(srcs)

    def body(*refs):
        ins, lands = refs[:n], refs[n:2 * n]
        send_sems, recv_sems, token = refs[2 * n], refs[2 * n + 1], refs[-1]
        me = _my_pos()
        for k, peer in _peers(me):
            for a in range(n):
                src, dst = refs_of(ins[a], lands[a], me, peer, k)
                pltpu.make_async_remote_copy(src_ref=src, dst_ref=dst, send_sem=send_sems.at[7 * a + k],
                                             recv_sem=recv_sems.at[7 * a + k], device_id=peer, device_id_type=MESH).start()
        token[...] = jnp.zeros_like(token)

    lands = [lax.empty(shp, s.dtype) for shp, s in zip(land_shapes, srcs)]
    hbm = lambda a: pltpu.with_memory_space_constraint(a, pltpu.HBM)
    res = pl.pallas_call(
        body, name=name,
        out_shape=(pltpu.SemaphoreType.DMA((7 * n,)), pltpu.SemaphoreType.DMA((7 * n,)),
                   *[pltpu.HBM(s.shape, s.dtype) for s in srcs], *[pltpu.HBM(l.shape, l.dtype) for l in lands],
                   jax.ShapeDtypeStruct((8, 128), F32)),
        in_specs=[HBM_SPEC] * (2 * n),
        out_specs=(SEM_SPEC, SEM_SPEC, *[HBM_SPEC] * (2 * n), pl.BlockSpec(memory_space=pltpu.VMEM)),
        input_output_aliases={i: 2 + i for i in range(2 * n)},
        compiler_params=pltpu.CompilerParams(has_side_effects=DATAFLOW_EFFECT),
    )(*[hbm(s) for s in srcs], *[hbm(l) for l in lands])
    return dict(sems=(res[0], res[1]), srcs=list(res[2:2 + n]), lands=list(res[2 + n:2 + 2 * n]), token=res[-1], n=n,
                refs_of=refs_of, name=name)


def _push_wait(started, after):
    n, refs_of = started["n"], started["refs_of"]

    def body(*refs):
        ins, lands = refs[:n], refs[n:2 * n]
        send_sems, recv_sems = refs[2 * n], refs[2 * n + 1]
        me = _my_pos()
        for k, peer in _peers(me):
            for a in range(n):
                src, dst = refs_of(ins[a], lands[a], me, peer, k)
                cp = pltpu.make_async_remote_copy(src_ref=src, dst_ref=dst, send_sem=send_sems.at[7 * a + k],
                                                  recv_sem=recv_sems.at[7 * a + k], device_id=peer, device_id_type=MESH)
                cp.wait_send()
                cp.wait_recv()

    arrs = started["srcs"] + started["lands"]
    res = pl.pallas_call(
        body, name=started["name"].replace("start", "wait"),
        out_shape=tuple(pltpu.HBM(a.shape, a.dtype) for a in arrs),
        in_specs=[HBM_SPEC] * (2 * n) + [SEM_SPEC, SEM_SPEC, pl.BlockSpec(memory_space=pl.ANY)],
        out_specs=tuple([HBM_SPEC] * (2 * n)),
        input_output_aliases={i: i for i in range(2 * n)},
        compiler_params=pltpu.CompilerParams(has_side_effects=DATAFLOW_EFFECT),
    )(*arrs, *started["sems"], after)
    return list(res[n:2 * n])


def _all_reduce_small(pack):
    R = pack.shape[0]

    def body(in_ref, out_ref, land, send_sems, recv_sems):
        me = _my_pos()
        my_idx = _dev_index(me)
        land[my_idx] = in_ref[...]
        sent = []
        for k, peer in _peers(me):
            cp = pltpu.make_async_remote_copy(
                src_ref=in_ref, dst_ref=land.at[my_idx], send_sem=send_sems.at[k], recv_sem=recv_sems.at[k],
                device_id=peer, device_id_type=MESH)
            cp.start()
            sent.append(cp)
        for k, peer in _peers(me):
            pltpu.make_async_remote_copy(
                src_ref=in_ref, dst_ref=land.at[_dev_index(peer)], send_sem=send_sems.at[k], recv_sem=recv_sems.at[k],
                device_id=peer, device_id_type=MESH).wait_recv()
        for cp in sent:
            cp.wait_send()
        acc = land[0]
        for d in range(1, N_DEV):
            acc = acc + land[d]
        out_ref[...] = acc

    return pl.pallas_call(
        body, name="all_reduce_small",
        in_specs=[pl.BlockSpec(memory_space=pltpu.VMEM)], out_specs=pl.BlockSpec(memory_space=pltpu.VMEM),
        out_shape=jax.ShapeDtypeStruct(pack.shape, F32),
        scratch_shapes=[pltpu.VMEM((N_DEV, R, 128), F32), pltpu.SemaphoreType.DMA((7,)), pltpu.SemaphoreType.DMA((7,))],
        compiler_params=pltpu.CompilerParams(has_side_effects=True, vmem_limit_bytes=VMEM_LIMIT_BYTES),
    )(pack)


def _local_step(x, mem, tgt, W, P, late_weights, send_grads, reduce_small, tie0):
    S = x.shape[0]
    W = dict(W)
    h = _rmsnorm_fwd(x, P["g_mix"] + tie0, rows=S, name="norm_mix")
    proj = _matmul(h, W["w_in"], M=S, N=D_IN, K=D_MODEL, mode="nn", bm=512, bn=D_IN // 2, bk=D_MODEL, name="mm_in",
                   j_outer=True)

    wa_bd, wx_bd = _mx(_block_diag(P["w_rg_a"])), _mx(_block_diag(P["w_rg_x"]))
    lru_args = (W["conv_w"], P["conv_b"].reshape(1, -1), wa_bd, wx_bd, P["b_rg_a"].reshape(1, -1),
                P["b_rg_x"].reshape(1, -1), P["lru_lambda"].reshape(1, -1))
    hl, z_lru = _lru_fwd(proj, *lru_args, S=S)

    buckets = _dil_buckets()
    bias = _dil_bias(P["rel_bias"], buckets)
    group_out = [_dilated_fwd(proj, bias, g, S=S) for g in range(len(DIL_GROUPS))]
    o_dil, lse_dil = _dilated_merge([o for o, _ in group_out], [l for _, l in group_out], S=S)

    W.update(late_weights(o_dil))
    mem_n = _rmsnorm_fwd(mem, P["g_mem"], rows=N_MEM, name="norm_mem")
    kv = _matmul(mem_n, W["w_mem_kv"], M=N_MEM, N=2 * MEM_WIDTH, K=D_MODEL, mode="nn", bm=N_MEM, bn=512, bk=D_MODEL,
                 name="mm_kv")
    om, lse_mem = _mem_attn_fwd(proj, kv, S=S)
    b_gate = P["b_gate"].reshape(1, -1)
    merged = _mix_fwd(z_lru, o_dil, om, W["w_lru_out"], W["w_dil_out"], W["w_mem_out"], proj, b_gate, S=S)
    x1 = _matmul(merged, W["w_out"], M=S, N=D_MODEL, K=D_MODEL, mode="nn", bm=512, bn=D_MODEL, bk=D_MODEL, name="mm_out",
                 epilogue=lambda acc, r: (r + acc,), extras=[(x, (0, 0))])
    hm = _rmsnorm_fwd(x1, P["g_mlp"], rows=S, name="norm_mlp")

    def relu2(acc):
        rl = jnp.maximum(acc, 0.0)
        return acc, rl * rl

    u, act = _matmul(hm, W["w_mlp_in"], M=S, N=D_FF, K=D_MODEL, mode="nn", bm=512, bn=1024, bk=D_MODEL, name="mm_mlp_in",
                     out_dtypes=(F32, MXU_DTYPE), epilogue=relu2, j_outer=True)
    x2 = _matmul(act, W["w_mlp_out"], M=S, N=D_MODEL, K=D_FF, mode="nn", bm=512, bn=D_MODEL, bk=1024, name="mm_mlp_out",
                 epilogue=lambda acc, r: (r + acc,), extras=[(x1, (0, 0))])
    loss, dx2, dg_final = _loss_head(x2, P["g_final"], tgt, rows=S)

    G, Gs = {}, {}
    Gs["g_final"] = dg_final
    G["w_mlp_out"] = _matmul(act, dx2, M=D_FF, N=D_MODEL, K=S, mode="tn", bm=1024, bn=D_MODEL, bk=512, name="mm_dw_mlp_out")
    du = _matmul(dx2, W["w_mlp_out"], M=S, N=D_FF, K=D_MODEL, mode="nt", bm=512, bn=1024, bk=D_MODEL, name="mm_du",
                 out_dtypes=(MXU_DTYPE,), epilogue=lambda acc, uu: (acc * 2.0 * jnp.maximum(uu, 0.0),),
                 extras=[(u, (0, 0))], j_outer=True)
    G["w_mlp_in"] = _matmul(hm, du, M=D_MODEL, N=D_FF, K=S, mode="tn", bm=D_MODEL, bn=1024, bk=512, name="mm_dw_mlp_in")
    tie1 = send_grads({n: G.pop(n) for n in ("w_mlp_out", "w_mlp_in")})
    dhm = _matmul(du, W["w_mlp_in"], M=S, N=D_MODEL, K=D_FF, mode="nt", bm=512, bn=D_MODEL, bk=1024, name="mm_dhm",
                  deps=[tie1])
    dx1, Gs["g_mlp"] = _rmsnorm_bwd(x1, P["g_mlp"], dhm, dx2, rows=S, name="norm_mlp_bwd")
    G["w_out"] = _matmul(merged, dx1, M=D_MODEL, N=D_MODEL, K=S, mode="tn", bm=D_MODEL, bn=D_MODEL, bk=512, name="mm_dw_out")
    dmerged = _matmul(dx1, W["w_out"], M=S, N=D_MODEL, K=D_MODEL, mode="nt", bm=512, bn=D_MODEL, bk=D_MODEL, name="mm_dmerged")
    (dg0, dg1, dg2, dy_lru, dy_dil, dy_mem, db0, db1, db2) = _mix_bwd(
        dmerged, z_lru, o_dil, om, W["w_lru_out"], W["w_dil_out"], W["w_mem_out"], proj, b_gate, S=S)
    Gs["b_gate"] = jnp.concatenate([db0, db1, db2], axis=1)

    G["w_mem_out"] = _matmul(om, dy_mem, M=MEM_WIDTH, N=D_MODEL, K=S, mode="tn", bm=MEM_WIDTH, bn=D_MODEL, bk=512,
                             name="mm_dw_mem_out")
    dom = _matmul(dy_mem, W["w_mem_out"], M=S, N=MEM_WIDTH, K=D_MODEL, mode="nt", bm=512, bn=MEM_WIDTH, bk=D_MODEL,
                  name="mm_dom")
    dqm, dk_mem, dv_mem = _mem_attn_bwd(proj, kv, om, lse_mem, dom, S=S)
    dkv = jnp.concatenate([dk_mem, dv_mem], axis=1)
    G["w_mem_kv"] = _matmul(mem_n, dkv, M=D_MODEL, N=2 * MEM_WIDTH, K=N_MEM, mode="tn", bm=D_MODEL, bn=2 * MEM_WIDTH,
                            bk=N_MEM, name="mm_dw_kv")
    dmem_n = _matmul(dkv, W["w_mem_kv"], M=N_MEM, N=D_MODEL, K=2 * MEM_WIDTH, mode="nt", bm=N_MEM, bn=D_MODEL,
                     bk=2 * MEM_WIDTH, name="mm_dmem")
    _, Gs["g_mem"] = _rmsnorm_bwd(mem, P["g_mem"], dmem_n, None, rows=N_MEM, name="norm_mem_bwd", want_dx=False)

    G["w_dil_out"] = _matmul(o_dil, dy_dil, M=256, N=D_MODEL, K=S, mode="tn", bm=256, bn=D_MODEL, bk=512, name="mm_dw_dil_out")
    do_dil = _matmul(dy_dil, W["w_dil_out"], M=S, N=256, K=D_MODEL, mode="nt", bm=512, bn=256, bk=D_MODEL, name="mm_do_dil")
    G["w_lru_out"] = _matmul(z_lru, dy_lru, M=D_RNN, N=D_MODEL, K=S, mode="tn", bm=D_RNN, bn=D_MODEL, bk=512, name="mm_dw_lru_out")
    dz = _matmul(dy_lru, W["w_lru_out"], M=S, N=D_RNN, K=D_MODEL, mode="nt", bm=512, bn=D_RNN, bk=D_MODEL, name="mm_dz_lru")
    tie2 = send_grads({n: G.pop(n) for n in ("w_out", "w_mem_out", "w_mem_kv", "w_dil_out", "w_lru_out")})
    bias = bias + tie2[0, 0]
    delta = _dilated_delta(do_dil, o_dil, S=S)
    dq_parts, dk_parts, dv_parts, dbias = [], [], [], []
    for g in range(len(DIL_GROUPS)):
        dq_g, dk_g, dv_g, db_g = _dilated_bwd(proj, do_dil, lse_dil, delta, bias, g, S=S)
        dq_parts.append(dq_g)
        dk_parts.append(dk_g)
        dv_parts.append(dv_g)
        dbias.append(db_g)
    drel = _dil_bias_bwd(jnp.stack(dbias, axis=0), buckets)
    Gs["rel_bias"] = drel[:, :3 * DIL_HEADS]

    dxl, dgl, dcw, dcb, dwa, dwx, dba, dbx, dlam = _lru_bwd(proj, hl, dz, *lru_args, S=S)
    Gs["conv_w"], Gs["conv_b"] = dcw, dcb
    Gs["w_rg_a"], Gs["w_rg_x"] = _block_diag_extract(dwa), _block_diag_extract(dwx)
    Gs["b_rg_a"], Gs["b_rg_x"], Gs["lru_lambda"] = dba, dbx, dlam

    dproj = jnp.concatenate([dxl, dgl] + dq_parts + dk_parts + dv_parts + [dqm, dg0, dg1, dg2], axis=1)
    dh = _matmul(dproj, W["w_in"], M=S, N=D_MODEL, K=D_IN, mode="nt", bm=512, bn=D_MODEL, bk=D_IN // 2, name="mm_dh")
    grad_x, Gs["g_mix"] = _rmsnorm_bwd(x, P["g_mix"], dh, dx1, rows=S, name="norm_mix_bwd")
    reduced = reduce_small(Gs, loss)
    half = D_MODEL // 2
    dw_a = _matmul(h, dproj, M=half, N=D_IN, K=S, mode="tn", bm=half, bn=D_IN // 2, bk=256, name="mm_dw_in_a",
                   deps=[reduced])
    tie_a = send_grads({"w_in_a": dw_a})
    dw_b = _matmul(h, dproj, M=half, N=D_IN, K=S, mode="tn", bm=half, bn=D_IN // 2, bk=256, name="mm_dw_in_b",
                   a_off=(0, 1), deps=[tie_a])
    tie_b = send_grads({"w_in_b": dw_b})
    return grad_x, tie_b


BIG = ("w_in", "w_lru_out", "w_dil_out", "w_mem_kv", "w_mem_out", "w_out", "w_mlp_in", "w_mlp_out")
COL_SHARDED = ("w_in", "w_in_a", "w_in_b", "w_lru_out", "w_dil_out", "w_mem_out", "w_mlp_in")
SMALL = ("g_mix", "b_gate", "conv_b", "w_rg_a", "b_rg_a", "w_rg_x", "b_rg_x", "lru_lambda", "rel_bias", "g_mem",
         "g_mlp", "g_final")
WEIGHTS = ("g_mix", "w_in", "b_gate", "conv_w", "conv_b", "w_rg_a", "b_rg_a", "w_rg_x", "b_rg_x", "lru_lambda",
           "w_lru_out", "rel_bias", "w_dil_out", "g_mem", "w_mem_kv", "w_mem_out", "w_out", "g_mlp", "w_mlp_in",
           "w_mlp_out", "g_final")
SMALL_SHAPES = {"g_mix": (1024,), "b_gate": (3072,), "conv_b": (768,), "w_rg_a": (12, 64, 64), "b_rg_a": (768,),
                "w_rg_x": (12, 64, 64), "b_rg_x": (768,), "lru_lambda": (768,), "rel_bias": (32, 12), "g_mem": (1024,),
                "g_mlp": (1024,), "g_final": (1024,)}


def _gathered_to_full(name, gathered):
    if name in COL_SHARDED:
        n, r, c = gathered.shape
        return gathered.transpose(1, 0, 2).reshape(r, n * c)
    n, r, c = gathered.shape
    return gathered.reshape(n * r, c)


def _full_to_parts(name, full):
    if name in COL_SHARDED:
        r, nc = full.shape
        return _mx(full.reshape(r, N_DEV, nc // N_DEV).transpose(1, 0, 2))
    nr, c = full.shape
    return _mx(full.reshape(N_DEV, nr // N_DEV, c))


def _own_part(name, full, my_idx):
    if name in COL_SHARDED:
        r, nc = full.shape
        return lax.dynamic_slice(full, (0, my_idx * (nc // N_DEV)), (r, nc // N_DEV))
    nr, c = full.shape
    return lax.dynamic_slice(full, (my_idx * (nr // N_DEV), 0), (nr // N_DEV, c))


def _pack(parts):
    flat = jnp.concatenate([p.reshape(-1).astype(F32) for p in parts])
    pad = (-flat.shape[0]) % 1024
    return jnp.pad(flat, (0, pad)).reshape(-1, 128)


def _unpack(pack, shapes):
    flat = pack.reshape(-1)
    out, off = [], 0
    for shp in shapes:
        size = math.prod(shp)
        out.append(flat[off:off + size].reshape(shp))
        off += size
    return out


def kernel(x, mem, g_mix, w_in, b_gate, conv_w, conv_b, w_rg_a, b_rg_a, w_rg_x, b_rg_x, lru_lambda, w_lru_out, rel_bias, w_dil_out, g_mem, w_mem_kv, w_mem_out, w_out, g_mlp, w_mlp_in, w_mlp_out, g_final, loss_target, m_g_mix, m_w_in, m_b_gate, m_conv_w, m_conv_b, m_w_rg_a, m_b_rg_a, m_w_rg_x, m_b_rg_x, m_lru_lambda, m_w_lru_out, m_rel_bias, m_w_dil_out, m_g_mem, m_w_mem_kv, m_w_mem_out, m_w_out, m_g_mlp, m_w_mlp_in, m_w_mlp_out, m_g_final, v_g_mix, v_w_in, v_b_gate, v_conv_w, v_conv_b, v_w_rg_a, v_b_rg_a, v_w_rg_x, v_b_rg_x, v_lru_lambda, v_w_lru_out, v_rel_bias, v_w_dil_out, v_g_mem, v_w_mem_kv, v_w_mem_out, v_w_out, v_g_mlp, v_w_mlp_in, v_w_mlp_out, v_g_final):
    w = dict(g_mix=g_mix, w_in=w_in, b_gate=b_gate, conv_w=conv_w, conv_b=conv_b, w_rg_a=w_rg_a, b_rg_a=b_rg_a,
             w_rg_x=w_rg_x, b_rg_x=b_rg_x, lru_lambda=lru_lambda, w_lru_out=w_lru_out, rel_bias=rel_bias,
             w_dil_out=w_dil_out, g_mem=g_mem, w_mem_kv=w_mem_kv, w_mem_out=w_mem_out, w_out=w_out, g_mlp=g_mlp,
             w_mlp_in=w_mlp_in, w_mlp_out=w_mlp_out, g_final=g_final)
    m = dict(g_mix=m_g_mix, w_in=m_w_in, b_gate=m_b_gate, conv_w=m_conv_w, conv_b=m_conv_b, w_rg_a=m_w_rg_a,
             b_rg_a=m_b_rg_a, w_rg_x=m_w_rg_x, b_rg_x=m_b_rg_x, lru_lambda=m_lru_lambda, w_lru_out=m_w_lru_out,
             rel_bias=m_rel_bias, w_dil_out=m_w_dil_out, g_mem=m_g_mem, w_mem_kv=m_w_mem_kv, w_mem_out=m_w_mem_out,
             w_out=m_w_out, g_mlp=m_g_mlp, w_mlp_in=m_w_mlp_in, w_mlp_out=m_w_mlp_out, g_final=m_g_final)
    v = dict(g_mix=v_g_mix, w_in=v_w_in, b_gate=v_b_gate, conv_w=v_conv_w, conv_b=v_conv_b, w_rg_a=v_w_rg_a,
             b_rg_a=v_b_rg_a, w_rg_x=v_w_rg_x, b_rg_x=v_b_rg_x, lru_lambda=v_lru_lambda, w_lru_out=v_w_lru_out,
             rel_bias=v_rel_bias, w_dil_out=v_w_dil_out, g_mem=v_g_mem, w_mem_kv=v_w_mem_kv, w_mem_out=v_w_mem_out,
             w_out=v_w_out, g_mlp=v_g_mlp, w_mlp_in=v_w_mlp_in, w_mlp_out=v_w_mlp_out, g_final=v_g_final)

    my_idx = _dev_index(_my_pos())

    g_in, g_cw = _all_gather([_mx(w["w_in"]), w["conv_w"]])
    W = {"w_in": _gathered_to_full("w_in", g_in), "conv_w": g_cw.transpose(1, 0, 2).reshape(CONV_WIDTH, D_RNN)}
    late = [n for n in BIG if n != "w_in"]
    late_shards = [_mx(w[n]) for n in late]
    late_started = _push_start(late_shards, [(N_DEV,) + s.shape for s in late_shards], _gather_refs,
                                "gather_late_start")
    P = {n: w[n] for n in SMALL}

    def late_weights(after):
        lands = _push_wait(late_started, after)
        out = {}
        for n, land, own in zip(late, lands, late_shards):
            full = lax.dynamic_update_index_in_dim(land, own, my_idx, 0)
            out[n] = _gathered_to_full(n, full)
        return out

    sent, small = [], {}
    small_names = SMALL + ("conv_w",)

    def send_grads(gs):
        names = list(gs)
        parts = [_full_to_parts(n, gs[n]) for n in names]
        own = [_own_part(n, gs[n], my_idx) for n in names]
        started = _push_start(parts, [(N_DEV - 1,) + p.shape[1:] for p in parts], _scatter_refs,
                              f"scatter{len(sent)}_start")
        sent.append((names, own, started))
        return started["token"]

    def reduce_small(gs, loss):
        small["total"] = _all_reduce_small(_pack([gs[n] for n in small_names] + [loss]))
        return small["total"]

    grad_x, last_token = _local_step(x[0], mem[0], loss_target[0], W, P, late_weights, send_grads, reduce_small,
                                     late_started["token"][0, 0])
    total = small["total"]
    small_shapes = [SMALL_SHAPES[n] for n in SMALL] + [(CONV_WIDTH, D_RNN), (1,)]
    summed = dict(zip(small_names + ("loss",), _unpack(total, small_shapes)))

    grads, deltas, new_m, new_v = {}, {}, {}, {}
    after = last_token
    for names, own, started in sent[:-2]:
        for n, o, land in zip(names, own, _push_wait(started, after)):
            grads[n], deltas[n], new_m[n], new_v[n] = _adamw_landed(w[n], o, land, m[n], v[n], name=f"adamw_{n}")
            after = deltas[n]
    prev = None
    for (names, own, started), row0 in zip(sent[-2:], (0, D_MODEL // 2)):
        (land,) = _push_wait(started, after)
        prev = _adamw_landed(w["w_in"], own[0], land, m["w_in"], v["w_in"], name=f"adamw_{names[0]}", row0=row0,
                             prev=prev)
    grads["w_in"], deltas["w_in"], new_m["w_in"], new_v["w_in"] = prev
    zeros_cw = jnp.zeros((CONV_WIDTH, D_RNN), F32)
    w_pack = _pack([w[n] for n in SMALL] + [zeros_cw, jnp.zeros((1,), F32)])
    m_pack = _pack([m[n] for n in SMALL] + [zeros_cw, jnp.zeros((1,), F32)])
    v_pack = _pack([v[n] for n in SMALL] + [zeros_cw, jnp.zeros((1,), F32)])
    d_pack, nm_pack, nv_pack = _adamw_plain(w_pack, total, m_pack, v_pack, name="adamw_small")
    for dst, pk in ((deltas, d_pack), (new_m, nm_pack), (new_v, nv_pack)):
        dst.update(zip(SMALL, _unpack(pk, [SMALL_SHAPES[n] for n in SMALL])))
    for n in SMALL:
        grads[n] = summed[n]
    cw_cols = D_RNN // N_DEV
    grads["conv_w"] = lax.dynamic_slice(summed["conv_w"], (0, my_idx * cw_cols), (CONV_WIDTH, cw_cols))
    deltas["conv_w"], new_m["conv_w"], new_v["conv_w"] = _adamw_plain(
        w["conv_w"], grads["conv_w"], m["conv_w"], v["conv_w"], name="adamw_conv_w")

    return (summed["loss"].reshape(()), grad_x[None], *[grads[n] for n in WEIGHTS], *[deltas[n] for n in WEIGHTS],
            *[new_m[n] for n in WEIGHTS], *[new_v[n] for n in WEIGHTS])
```

```python
import functools
import math

import jax
import jax.numpy as jnp
from jax import lax
from jax.experimental import pallas as pl
from jax.experimental.pallas import tpu as pltpu

F32 = jnp.float32
MXU_DTYPE = jnp.bfloat16
VMEM_LIMIT_BYTES = 56 * 1024 * 1024
N_DEV = 8

D_MODEL = 1024
N_MEM = 256
MEM_HEADS = 4
MEM_HEAD_DIM = 128
MEM_WIDTH = 512
D_RNN = 768
LRU_BLOCK = 64
N_LRU_BLOCKS = 12
LRU_GROUP = 256
N_LRU_GROUPS = 3
CONV_WIDTH = 4
LRU_C = 8.0
DIL_GROUPS = ((128, 1), (512, 4), (2048, 16))
SPAN = 128
DIL_HEADS = 4
DIL_HEAD_DIM = 64
NUM_BUCKETS = 32
MAX_DISTANCE = 2048
D_FF = 4096
D_IN = 7424
EPS = 1e-6
NEG = -1e30
C_XL, C_GATE, C_QKV, C_QM, C_GATES = 0, 768, 1536, 3840, 4352

ADAM_LR = 0.001
ADAM_B1 = 0.9
ADAM_B2 = 0.999
ADAM_EPS = 1e-08
ADAM_WD = 0.01
ADAM_STEP = 10

MESH = pl.DeviceIdType.MESH
GELU_K = math.sqrt(2.0 / math.pi)


def _cparams(sem=None):
    kw = dict(vmem_limit_bytes=VMEM_LIMIT_BYTES)
    if sem is not None:
        kw["dimension_semantics"] = sem
    return pltpu.CompilerParams(**kw)


def _mx(v):
    return v.astype(MXU_DTYPE)


def _dot(a, b, mode="nn"):
    dims = {"nn": (((1,), (0,)), ((), ())), "nt": (((1,), (1,)), ((), ())), "tn": (((0,), (0,)), ((), ()))}[mode]
    return lax.dot_general(_mx(a), _mx(b), dims, preferred_element_type=F32)


def _colsum(v):
    return jnp.sum(v, axis=0, keepdims=True)


def _matmul(a, b, *, M, N, K, mode, bm, bn, bk, name, out_dtypes=(F32,), epilogue=None, extras=(),
            a_off=(0, 0), b_off=(0, 0), j_outer=False, deps=()):
    assert M % bm == 0 and N % bn == 0 and K % bk == 0, (name, M, N, K, bm, bn, bk)
    nm, nn, nk = M // bm, N // bn, K // bk

    def ij(f):
        if j_outer:
            return lambda j, i, k: f(i, j, k)
        return f

    if mode == "tn":
        a_spec = pl.BlockSpec((bk, bm), ij(lambda i, j, k: (k + a_off[0], i + a_off[1])))
    else:
        a_spec = pl.BlockSpec((bm, bk), ij(lambda i, j, k: (i + a_off[0], k + a_off[1])))
    if mode == "nt":
        b_spec = pl.BlockSpec((bn, bk), ij(lambda i, j, k: (j + b_off[0], k + b_off[1])))
    else:
        b_spec = pl.BlockSpec((bk, bn), ij(lambda i, j, k: (k + b_off[0], j + b_off[1])))
    ex_specs = [pl.BlockSpec((bm, bn), ij(functools.partial(lambda i, j, k, o: (i + o[0], j + o[1]), o=off)))
                for _, off in extras]
    out_spec = pl.BlockSpec((bm, bn), ij(lambda i, j, k: (i, j)))
    n_ex, n_out, n_dep = len(extras), len(out_dtypes), len(deps)

    def body(*refs):
        a_ref, b_ref = refs[0], refs[1]
        ex = refs[2:2 + n_ex]
        outs = refs[2 + n_ex + n_dep:2 + n_ex + n_dep + n_out]
        part = _dot(a_ref[...], b_ref[...], mode)

        def finish(acc):
            vals = epilogue(acc, *[e[...] for e in ex]) if epilogue is not None else (acc,)
            for o, v in zip(outs, vals):
                o[...] = v.astype(o.dtype)

        if nk == 1:
            finish(part)
        else:
            acc_ref = refs[-1]
            k = pl.program_id(2)

            @pl.when(k == 0)
            def _():
                acc_ref[...] = part

            @pl.when(k > 0)
            def _():
                acc_ref[...] += part

            @pl.when(k == nk - 1)
            def _():
                finish(acc_ref[...])

    grid = (nn, nm, nk) if j_outer else (nm, nn, nk)
    res = pl.pallas_call(
        body, name=name, grid=grid,
        in_specs=[a_spec, b_spec] + ex_specs + [pl.BlockSpec(memory_space=pl.ANY)] * n_dep,
        out_specs=[out_spec] * n_out,
        out_shape=[jax.ShapeDtypeStruct((M, N), dt) for dt in out_dtypes],
        scratch_shapes=[pltpu.VMEM((bm, bn), F32)] if nk > 1 else [],
        compiler_params=_cparams(("parallel", "parallel", "arbitrary")),
    )(a, b, *[e for e, _ in extras], *deps)
    return res[0] if n_out == 1 else res


def _rmsnorm_fwd(x, g, *, rows, name, bt=512):
    bt = min(bt, rows)

    def body(x_ref, g_ref, o_ref):
        xv = x_ref[...]
        r = lax.rsqrt(jnp.mean(xv * xv, axis=-1, keepdims=True) + EPS)
        o_ref[...] = (xv * r * g_ref[...]).astype(o_ref.dtype)

    return pl.pallas_call(
        body, name=name, grid=(rows // bt,),
        in_specs=[pl.BlockSpec((bt, D_MODEL), lambda i: (i, 0)), pl.BlockSpec((1, D_MODEL), lambda i: (0, 0))],
        out_specs=pl.BlockSpec((bt, D_MODEL), lambda i: (i, 0)),
        out_shape=jax.ShapeDtypeStruct((rows, D_MODEL), MXU_DTYPE),
        compiler_params=_cparams(("parallel",)),
    )(x, g.reshape(1, D_MODEL))


def _rms_bwd_tile(xv, gv, dyv):
    r = lax.rsqrt(jnp.mean(xv * xv, axis=-1, keepdims=True) + EPS)
    w = dyv * gv
    dx = r * w - xv * (r * r * r) * jnp.mean(w * xv, axis=-1, keepdims=True)
    dg = _colsum(dyv * xv * r)
    return dx, dg


def _rmsnorm_bwd(x, g, dy, res, *, rows, name, bt=512, dx_dtypes=(F32,)):
    bt = min(bt, rows)
    has_res = res is not None

    def body(*refs):
        x_ref, g_ref, dy_ref = refs[:3]
        res_ref = refs[3] if has_res else None
        outs = refs[3 + int(has_res):]
        dx, dg = _rms_bwd_tile(x_ref[...], g_ref[...], dy_ref[...])
        if has_res:
            dx = dx + res_ref[...]
        dg_ref = outs[-1]

        @pl.when(pl.program_id(0) == 0)
        def _():
            dg_ref[...] = jnp.zeros_like(dg_ref)

        dg_ref[...] += dg
        for o in outs[:-1]:
            o[...] = dx.astype(o.dtype)

    row_spec = pl.BlockSpec((bt, D_MODEL), lambda i: (i, 0))
    vec_spec = pl.BlockSpec((1, D_MODEL), lambda i: (0, 0))
    ins = [x, g.reshape(1, D_MODEL), dy] + ([res] if has_res else [])
    return pl.pallas_call(
        body, name=name, grid=(rows // bt,),
        in_specs=[row_spec, vec_spec, row_spec] + ([row_spec] if has_res else []),
        out_specs=[row_spec] * len(dx_dtypes) + [vec_spec],
        out_shape=[jax.ShapeDtypeStruct((rows, D_MODEL), dt) for dt in dx_dtypes] + [jax.ShapeDtypeStruct((1, D_MODEL), F32)],
        compiler_params=_cparams(("arbitrary",)),
    )(*ins)


def _loss_head(x2, g, tgt, *, rows, bt=512):
    def body(x_ref, g_ref, t_ref, loss_ref, dx_ref, dxm_ref, dg_ref):
        xv, gv = x_ref[...], g_ref[...]
        r = lax.rsqrt(jnp.mean(xv * xv, axis=-1, keepdims=True) + EPS)
        diff = xv * r * gv - t_ref[...]
        part = jnp.sum(jnp.mean(diff * diff, axis=-1, keepdims=True), axis=0, keepdims=True) * 0.5
        dx, dg = _rms_bwd_tile(xv, gv, diff * (1.0 / D_MODEL))

        @pl.when(pl.program_id(0) == 0)
        def _():
            loss_ref[...] = jnp.zeros_like(loss_ref)
            dg_ref[...] = jnp.zeros_like(dg_ref)

        loss_ref[...] += part
        dg_ref[...] += dg
        dx_ref[...] = dx
        dxm_ref[...] = _mx(dx)

    row_spec = pl.BlockSpec((bt, D_MODEL), lambda i: (i, 0))
    vec_spec = pl.BlockSpec((1, D_MODEL), lambda i: (0, 0))
    return pl.pallas_call(
        body, name="loss_head", grid=(rows // bt,),
        in_specs=[row_spec, vec_spec, row_spec],
        out_specs=[pl.BlockSpec((1, 1), lambda i: (0, 0)), row_spec, row_spec, vec_spec],
        out_shape=[jax.ShapeDtypeStruct((1, 1), F32), jax.ShapeDtypeStruct((rows, D_MODEL), F32),
                   jax.ShapeDtypeStruct((rows, D_MODEL), MXU_DTYPE), jax.ShapeDtypeStruct((1, D_MODEL), F32)],
        compiler_params=_cparams(("arbitrary",)),
    )(x2, g.reshape(1, D_MODEL), tgt)


LRU_T = 256


def _gelu(x):
    t = jnp.tanh(GELU_K * (x + 0.044715 * x * x * x))
    return 0.5 * x * (1.0 + t), t


def _gelu_grad(x, t):
    return 0.5 * (1.0 + t) + 0.5 * x * (1.0 - t * t) * GELU_K * (1.0 + 3.0 * 0.044715 * x * x)


def _softplus_neg(lam):
    z = -lam
    u = jnp.exp(-jnp.abs(z))
    w = 1.0 + u
    l1p = jnp.where(w == 1.0, u, jnp.log(w) * u / jnp.where(w == 1.0, 1.0, w - 1.0))
    return jnp.maximum(z, 0.0) + l1p


def _shift_down(cur, prev8, k, row8):
    y = pltpu.roll(cur, k, 0)
    head = jnp.where(row8 < k, pltpu.roll(prev8, k, 0), y[0:8])
    return jnp.concatenate([head, y[8:]], axis=0)


def _shift_up(cur, next8, k, row8):
    n = cur.shape[0]
    y = pltpu.roll(cur, n - k, 0)
    tail = jnp.where(row8 >= 8 - k, pltpu.roll(next8, 8 - k, 0), y[n - 8:n])
    return jnp.concatenate([y[0:n - 8], tail], axis=0)


def _lru_gates(xl, p8, cw, cb, wa, wx, ba, bx, lam, row8):
    sh = [xl] + [_shift_down(xl, p8, k, row8) for k in (1, 2, 3)]
    xc = cb + cw[3:4] * sh[0] + cw[2:3] * sh[1] + cw[1:2] * sh[2] + cw[0:1] * sh[3]
    r = jax.nn.sigmoid(_dot(xc, wa) + ba)
    i = jax.nn.sigmoid(_dot(xc, wx) + bx)
    sp = _softplus_neg(lam)
    la = -LRU_C * r * sp
    a = jnp.exp(la)
    mult = jnp.sqrt(jnp.tanh(-la) * (a * a + 1.0))
    return dict(sh=sh, xc=xc, r=r, i=i, sp=sp, a=a, mult=mult)


def _lru_specs(n_t, reverse):
    T = LRU_T
    tt = (lambda t: n_t - 1 - t) if reverse else (lambda t: t)
    blk = lambda col0: pl.BlockSpec((T, LRU_GROUP), lambda g, t: (tt(t), col0 + g))
    prev8 = lambda col0: pl.BlockSpec((8, LRU_GROUP), lambda g, t: (jnp.maximum(tt(t) * (T // 8) - 1, 0), col0 + g))
    vec = lambda rows: pl.BlockSpec((rows, LRU_GROUP), lambda g, t: (0, g))
    wbd = pl.BlockSpec((1, LRU_GROUP, LRU_GROUP), lambda g, t: (g, 0, 0))
    return blk, prev8, vec, wbd


def _lru_fwd(proj, conv_w, conv_b, wa_bd, wx_bd, b_a, b_x, lam, *, S):
    T = LRU_T
    n_t = S // T
    blk, _, vec, wbd = _lru_specs(n_t, False)

    def body(xl_ref, gate_ref, cw_ref, cb_ref, wa_ref, wx_ref, ba_ref, bx_ref, lam_ref,
             hl_ref, z_ref, prev8, hcar, a_s, b_s):
        @pl.when(pl.program_id(1) == 0)
        def _():
            prev8[...] = jnp.zeros_like(prev8)
            hcar[...] = jnp.zeros_like(hcar)

        row8 = lax.broadcasted_iota(jnp.int32, (8, LRU_GROUP), 0)
        xl = xl_ref[...]
        q = _lru_gates(xl, prev8[...], cw_ref[...], cb_ref[...], wa_ref[0], wx_ref[0], ba_ref[...], bx_ref[...],
                       lam_ref[...], row8)
        prev8[...] = xl[T - 8:T]
        a_s[...] = q["a"]
        b_s[...] = q["mult"] * q["i"] * q["xc"]

        def step(c, carry):
            off = pl.multiple_of(c * 8, 8)
            A = a_s[pl.ds(off, 8), :]
            B = b_s[pl.ds(off, 8), :]
            for k in (1, 2, 4):
                a_sh = jnp.where(row8 >= k, pltpu.roll(A, k, 0), 1.0)
                b_sh = jnp.where(row8 >= k, pltpu.roll(B, k, 0), 0.0)
                B = A * b_sh + B
                A = A * a_sh
            h = A * carry + B
            hl_ref[pl.ds(off, 8), :] = h
            return h[7:8, :]

        hcar[...] = lax.fori_loop(0, T // 8, step, hcar[...])
        ge, _ = _gelu(gate_ref[...])
        z_ref[...] = (ge * hl_ref[...]).astype(z_ref.dtype)

    return pl.pallas_call(
        body, name="lru_fwd", grid=(N_LRU_GROUPS, n_t),
        in_specs=[blk(C_XL // LRU_GROUP), blk(C_GATE // LRU_GROUP), vec(4), vec(1), wbd, wbd, vec(1), vec(1), vec(1)],
        out_specs=[blk(0), blk(0)],
        out_shape=[jax.ShapeDtypeStruct((S, D_RNN), F32), jax.ShapeDtypeStruct((S, D_RNN), MXU_DTYPE)],
        scratch_shapes=[pltpu.VMEM((8, LRU_GROUP), F32), pltpu.VMEM((1, LRU_GROUP), F32),
                        pltpu.VMEM((T, LRU_GROUP), F32), pltpu.VMEM((T, LRU_GROUP), F32)],
        compiler_params=_cparams(("parallel", "arbitrary")),
    )(proj, proj, conv_w, conv_b, wa_bd, wx_bd, b_a, b_x, lam)


def _lru_bwd(proj, hl, dz, conv_w, conv_b, wa_bd, wx_bd, b_a, b_x, lam, *, S):
    T = LRU_T
    n_t = S // T
    blk, prev8s, vec, wbd = _lru_specs(n_t, True)

    def body(xl_ref, xlp_ref, gate_ref, hl_ref, hlp_ref, dz_ref, cw_ref, cb_ref, wa_ref, wx_ref, ba_ref, bx_ref,
             lam_ref, dxl_ref, dgate_ref, dcw_ref, dcb_ref, dwa_ref, dwx_ref, dba_ref, dbx_ref, dlam_ref,
             next8, gcar, c_s, b_s, l_s):
        t = pl.program_id(1)
        first_chunk = t == n_t - 1

        @pl.when(t == 0)
        def _():
            next8[...] = jnp.zeros_like(next8)
            gcar[...] = jnp.zeros_like(gcar)
            for ref in (dcw_ref, dcb_ref, dwa_ref, dwx_ref, dba_ref, dbx_ref, dlam_ref):
                ref[...] = jnp.zeros_like(ref)

        row8 = lax.broadcasted_iota(jnp.int32, (8, LRU_GROUP), 0)
        rowT = lax.broadcasted_iota(jnp.int32, (T, LRU_GROUP), 0)
        keep = jnp.where(first_chunk, 0.0, 1.0)
        xl = xl_ref[...]
        wa, wx, lam_v = wa_ref[0], wx_ref[0], lam_ref[...]
        q = _lru_gates(xl, xlp_ref[...] * keep, cw_ref[...], cb_ref[...], wa, wx, ba_ref[...], bx_ref[...], lam_v, row8)
        a, mult, r, i, xc, sp = q["a"], q["mult"], q["r"], q["i"], q["xc"], q["sp"]
        hl_v = hl_ref[...]
        dz_v = dz_ref[...]
        gate = gate_ref[...]
        ge, th = _gelu(gate)
        dgate_ref[...] = (dz_v * hl_v * _gelu_grad(gate, th)).astype(dgate_ref.dtype)

        c_s[...] = jnp.where(rowT == T - 1, 0.0, pltpu.roll(a, T - 1, 0))
        b_s[...] = dz_v * ge + jnp.where(rowT == T - 1, gcar[...], 0.0)

        def step(n, carry):
            off = pl.multiple_of((T // 8 - 1 - n) * 8, 8)
            C = c_s[pl.ds(off, 8), :]
            B = b_s[pl.ds(off, 8), :]
            for k in (1, 2, 4):
                c_sh = jnp.where(row8 < 8 - k, pltpu.roll(C, 8 - k, 0), 1.0)
                b_sh = jnp.where(row8 < 8 - k, pltpu.roll(B, 8 - k, 0), 0.0)
                B = B + C * b_sh
                C = C * c_sh
            lam_t = B + C * carry
            l_s[pl.ds(off, 8), :] = lam_t
            return lam_t[0:1, :]

        lax.fori_loop(0, T // 8, step, jnp.zeros((1, LRU_GROUP), F32))
        lmb = l_s[...]
        gcar[...] = a[0:1, :] * lmb[0:1, :]

        h_prev = _shift_down(hl_v, hlp_ref[...] * keep, 1, row8)
        da = lmb * h_prev
        dmult = lmb * i * xc
        di = lmb * mult * xc
        dxc = lmb * mult * i
        dla = da * a - dmult * (a * a) / mult
        dr = dla * (-LRU_C * sp)
        dlam_ref[...] += _colsum(dla * (-LRU_C * r)) * (-jax.nn.sigmoid(-lam_v))
        dpa = dr * r * (1.0 - r)
        dpx = di * i * (1.0 - i)
        dxc = dxc + _dot(dpa, wa, "nt") + _dot(dpx, wx, "nt")
        dwa_ref[0] += _dot(xc, dpa, "tn")
        dwx_ref[0] += _dot(xc, dpx, "tn")
        dba_ref[...] += _colsum(dpa)
        dbx_ref[...] += _colsum(dpx)
        dcb_ref[...] += _colsum(dxc)
        cw = cw_ref[...]
        n8 = next8[...]
        dxl = cw[3:4] * dxc
        for k in (1, 2, 3):
            dxl = dxl + cw[3 - k:4 - k] * _shift_up(dxc, n8, k, row8)
        for k in range(4):
            dcw_ref[3 - k:4 - k, :] += _colsum(dxc * q["sh"][k])
        next8[...] = dxc[0:8]
        dxl_ref[...] = dxl.astype(dxl_ref.dtype)

    res = pl.pallas_call(
        body, name="lru_bwd", grid=(N_LRU_GROUPS, n_t),
        in_specs=[blk(C_XL // LRU_GROUP), prev8s(C_XL // LRU_GROUP), blk(C_GATE // LRU_GROUP), blk(0), prev8s(0), blk(0),
                  vec(4), vec(1), wbd, wbd, vec(1), vec(1), vec(1)],
        out_specs=[blk(0), blk(0), vec(4), vec(1), wbd, wbd, vec(1), vec(1), vec(1)],
        out_shape=[jax.ShapeDtypeStruct((S, D_RNN), MXU_DTYPE), jax.ShapeDtypeStruct((S, D_RNN), MXU_DTYPE),
                   jax.ShapeDtypeStruct((4, D_RNN), F32), jax.ShapeDtypeStruct((1, D_RNN), F32),
                   jax.ShapeDtypeStruct((N_LRU_GROUPS, LRU_GROUP, LRU_GROUP), F32),
                   jax.ShapeDtypeStruct((N_LRU_GROUPS, LRU_GROUP, LRU_GROUP), F32),
                   jax.ShapeDtypeStruct((1, D_RNN), F32), jax.ShapeDtypeStruct((1, D_RNN), F32),
                   jax.ShapeDtypeStruct((1, D_RNN), F32)],
        scratch_shapes=[pltpu.VMEM((8, LRU_GROUP), F32), pltpu.VMEM((1, LRU_GROUP), F32),
                        pltpu.VMEM((T, LRU_GROUP), F32), pltpu.VMEM((T, LRU_GROUP), F32), pltpu.VMEM((T, LRU_GROUP), F32)],
        compiler_params=_cparams(("parallel", "arbitrary")),
    )(proj, proj, proj, hl, hl, dz, conv_w, conv_b, wa_bd, wx_bd, b_a, b_x, lam)
    return res


def _block_diag(w):
    w4 = w.reshape(N_LRU_GROUPS, 4, LRU_BLOCK, 1, LRU_BLOCK)
    eye = jnp.eye(4, dtype=w.dtype).reshape(1, 4, 1, 4, 1)
    return (w4 * eye).reshape(N_LRU_GROUPS, LRU_GROUP, LRU_GROUP)


def _block_diag_extract(wbd):
    w5 = wbd.reshape(N_LRU_GROUPS, 4, LRU_BLOCK, 4, LRU_BLOCK)
    return jnp.stack([w5[:, a, :, a, :] for a in range(4)], axis=1).reshape(N_LRU_BLOCKS, LRU_BLOCK, LRU_BLOCK)


def _t5_bucket(dist):
    max_exact = NUM_BUCKETS // 2
    df = jnp.maximum(dist, 1).astype(jnp.float32)
    large = max_exact + (jnp.log(df / max_exact) / math.log(MAX_DISTANCE / max_exact)
                         * (NUM_BUCKETS - max_exact)).astype(jnp.int32)
    large = jnp.minimum(large, NUM_BUCKETS - 1)
    return jnp.where(dist < max_exact, dist, large)


def _band_offsets():
    qi = jnp.arange(SPAN)[:, None]
    kj = jnp.arange(2 * SPAN)[None, :]
    return qi + SPAN - kj


def _dil_buckets():
    off = _band_offsets()
    return jnp.stack([_t5_bucket(jnp.maximum(off, 0) * dil) for _, dil in DIL_GROUPS]).astype(jnp.int32)


def _dil_bias(rel_bias, buckets):
    def body(tbl_ref, bk_ref, o_ref):
        g = pl.program_id(0)
        qi = lax.broadcasted_iota(jnp.int32, (SPAN, 2 * SPAN), 0)
        kj = lax.broadcasted_iota(jnp.int32, (SPAN, 2 * SPAN), 1)
        off = qi + SPAN - kj
        valid = (off >= 0) & (off <= SPAN)
        bk = bk_ref[0]
        for h in range(DIL_HEADS):
            acc = jnp.zeros((SPAN, 2 * SPAN), F32)
            for b in range(NUM_BUCKETS):
                acc = jnp.where(bk == b, tbl_ref[b, g * DIL_HEADS + h], acc)
            o_ref[0, h] = jnp.where(valid, acc, NEG)

    return pl.pallas_call(
        body, name="dil_bias", grid=(3,),
        in_specs=[pl.BlockSpec(memory_space=pltpu.SMEM), pl.BlockSpec((1, SPAN, 2 * SPAN), lambda g: (g, 0, 0))],
        out_specs=pl.BlockSpec((1, DIL_HEADS, SPAN, 2 * SPAN), lambda g: (g, 0, 0, 0)),
        out_shape=jax.ShapeDtypeStruct((3, DIL_HEADS, SPAN, 2 * SPAN), F32),
        compiler_params=_cparams(("parallel",)),
    )(rel_bias, buckets)


def _dil_bias_bwd(dbias, buckets):
    def body(db_ref, bk_ref, o_ref):
        lane = lax.broadcasted_iota(jnp.int32, (1, 128), 1)
        rows = [jnp.zeros((1, 128), F32) for _ in range(NUM_BUCKETS)]
        for g in range(3):
            bk = bk_ref[g]
            for h in range(DIL_HEADS):
                d = db_ref[g, h]
                for b in range(NUM_BUCKETS):
                    tot = jnp.sum(_colsum(jnp.where(bk == b, d, 0.0)), axis=1, keepdims=True)
                    rows[b] = jnp.where(lane == g * DIL_HEADS + h, tot, rows[b])
        for b in range(NUM_BUCKETS):
            o_ref[b:b + 1, :] = rows[b]

    return pl.pallas_call(
        body, name="dil_bias_bwd",
        out_shape=jax.ShapeDtypeStruct((NUM_BUCKETS, 128), F32),
        compiler_params=_cparams(),
    )(dbias, buckets)


def _dil_layout(g, S):
    dil = DIL_GROUPS[g][1]
    rows = SPAN * dil
    col = [(C_QKV + t * 768 + g * 256) // 128 for t in range(3)]
    return dil, rows, S // rows, col


def _residue_rows(r, dil):
    return pl.ds(r, SPAN, stride=dil) if dil > 1 else pl.ds(0, SPAN)


def _for_residues(dil, fn):
    if dil == 1:
        fn(0)
    else:
        lax.fori_loop(0, dil, lambda r, c: (fn(r), c)[1], 0)


def _pair_scores(qm, k2, bias, first_cols):
    s = _dot(qm, k2, "nt") * (DIL_HEAD_DIM ** -0.5) + bias
    kj = lax.broadcasted_iota(jnp.int32, s.shape, 1)
    return jnp.where(kj < first_cols, NEG, s)


def _dilated_fwd(proj, bias, g, *, S):
    dil, R, nc, (qc, kc, vc) = _dil_layout(g, S)
    cur = lambda cb: pl.BlockSpec((R, 128), lambda p, i: (i, cb + p))
    prv = lambda cb: pl.BlockSpec((R, 128), lambda p, i: (jnp.maximum(i - 1, 0), cb + p))
    out = pl.BlockSpec((R, 128), lambda p, i: (i, p))

    def body(q_ref, kp_ref, kc_ref, vp_ref, vc_ref, b_ref, o_ref, lse_ref):
        first_cols = jnp.where(pl.program_id(1) == 0, SPAN, 0)
        lane = lax.broadcasted_iota(jnp.int32, (SPAN, 128), 1)
        sels = (lane < DIL_HEAD_DIM, lane >= DIL_HEAD_DIM)

        def one(r):
            rows = _residue_rows(r, dil)
            q2 = q_ref[rows, :]
            k2 = _mx(jnp.concatenate([kp_ref[rows, :], kc_ref[rows, :]], axis=0))
            v2 = _mx(jnp.concatenate([vp_ref[rows, :], vc_ref[rows, :]], axis=0))
            stat = jnp.zeros((SPAN, 128), F32)
            o2 = jnp.zeros((SPAN, 128), F32)
            for e in range(2):
                s = _pair_scores(jnp.where(sels[e], q2, 0.0), k2, b_ref[0, e], first_cols)
                m = jnp.max(s, axis=-1, keepdims=True)
                p = jnp.exp(s - m)
                den = jnp.sum(p, axis=-1, keepdims=True)
                o2 = jnp.where(sels[e], _dot(p, v2) / den, o2)
                stat = jnp.where(lane == e, m + jnp.log(den), stat)
            o_ref[rows, :] = o2
            lse_ref[rows, :] = stat

        _for_residues(dil, one)

    return pl.pallas_call(
        body, name=f"dil_fwd{g}", grid=(2, nc),
        in_specs=[cur(qc), prv(kc), cur(kc), prv(vc), cur(vc),
                  pl.BlockSpec((1, 2, SPAN, 2 * SPAN), lambda p, i: (g, p, 0, 0))],
        out_specs=[out, out],
        out_shape=[jax.ShapeDtypeStruct((S, 256), F32), jax.ShapeDtypeStruct((S, 256), F32)],
        compiler_params=_cparams(("parallel", "parallel")),
    )(proj, proj, proj, proj, proj, bias)


def _dilated_bwd(proj, do, lse, delta, bias, g, *, S):
    dil, R, nc, (qc, kc, vc) = _dil_layout(g, S)
    cl = lambda i: jnp.minimum(i, nc - 1)
    cur = lambda cb: pl.BlockSpec((R, 128), lambda p, i: (cl(i), cb + p))
    prv = lambda cb: pl.BlockSpec((R, 128), lambda p, i: (jnp.maximum(cl(i) - 1, 0), cb + p))
    kv_out = pl.BlockSpec((R, 128), lambda p, i: (jnp.maximum(i - 1, 0), p))
    scale = DIL_HEAD_DIM ** -0.5

    def body(q_ref, kp_ref, kc_ref, vp_ref, vc_ref, do_ref, lse_ref, dl_ref, b_ref,
             dq_ref, dk_ref, dv_ref, db_ref, dq_s, kp_s, kc_s, vp_s, vc_s, kcar, vcar):
        i = pl.program_id(1)

        @pl.when(i == 0)
        def _():
            kcar[...] = jnp.zeros_like(kcar)
            vcar[...] = jnp.zeros_like(vcar)
            db_ref[...] = jnp.zeros_like(db_ref)

        @pl.when(i < nc)
        def _():
            first_cols = jnp.where(i == 0, SPAN, 0)
            lane = lax.broadcasted_iota(jnp.int32, (SPAN, 128), 1)
            sels = (lane < DIL_HEAD_DIM, lane >= DIL_HEAD_DIM)

            def one(r):
                rows = _residue_rows(r, dil)
                q2, do2 = q_ref[rows, :], do_ref[rows, :]
                k2 = _mx(jnp.concatenate([kp_ref[rows, :], kc_ref[rows, :]], axis=0))
                v2 = _mx(jnp.concatenate([vp_ref[rows, :], vc_ref[rows, :]], axis=0))
                lse_t, dl_t = lse_ref[rows, :], dl_ref[rows, :]
                dq2 = jnp.zeros((SPAN, 128), F32)
                dk2 = jnp.zeros((2 * SPAN, 128), F32)
                dv2 = jnp.zeros((2 * SPAN, 128), F32)
                for e in range(2):
                    qm = jnp.where(sels[e], q2, 0.0)
                    dom = jnp.where(sels[e], do2, 0.0)
                    p = jnp.exp(_pair_scores(qm, k2, b_ref[0, e], first_cols) - lse_t[:, e:e + 1])
                    ds = p * (_dot(dom, v2, "nt") - dl_t[:, e:e + 1])
                    db_ref[e] += ds
                    dq2 = jnp.where(sels[e], _dot(ds, k2) * scale, dq2)
                    dk2 = dk2 + _dot(ds, qm, "tn") * scale
                    dv2 = dv2 + _dot(p, dom, "tn")
                dq_s[rows, :] = dq2
                kp_s[rows, :] = dk2[0:SPAN]
                kc_s[rows, :] = dk2[SPAN:2 * SPAN]
                vp_s[rows, :] = dv2[0:SPAN]
                vc_s[rows, :] = dv2[SPAN:2 * SPAN]

            _for_residues(dil, one)
            dq_ref[...] = dq_s[...].astype(dq_ref.dtype)
            dk_ref[...] = (kcar[...] + kp_s[...]).astype(dk_ref.dtype)
            dv_ref[...] = (vcar[...] + vp_s[...]).astype(dv_ref.dtype)
            kcar[...] = kc_s[...]
            vcar[...] = vc_s[...]

        @pl.when(i == nc)
        def _():
            dk_ref[...] = kcar[...].astype(dk_ref.dtype)
            dv_ref[...] = vcar[...].astype(dv_ref.dtype)

    stat = pl.BlockSpec((R, 128), lambda p, i: (cl(i), p))
    big = jax.ShapeDtypeStruct((S, 256), MXU_DTYPE)
    return pl.pallas_call(
        body, name=f"dil_bwd{g}", grid=(2, nc + 1),
        in_specs=[cur(qc), prv(kc), cur(kc), prv(vc), cur(vc), stat, stat, stat,
                  pl.BlockSpec((1, 2, SPAN, 2 * SPAN), lambda p, i: (g, p, 0, 0))],
        out_specs=[stat, kv_out, kv_out, pl.BlockSpec((2, SPAN, 2 * SPAN), lambda p, i: (p, 0, 0))],
        out_shape=[big, big, big, jax.ShapeDtypeStruct((DIL_HEADS, SPAN, 2 * SPAN), F32)],
        scratch_shapes=[pltpu.VMEM((R, 128), F32)] * 7,
        compiler_params=_cparams(("parallel", "arbitrary")),
    )(proj, proj, proj, proj, proj, do, lse, delta, bias)


def _dilated_merge(os_, lses, *, S, bt=512):
    tile = pl.BlockSpec((bt, 128), lambda i, p: (i, p))

    def body(o0, o1, o2, l0, l1, l2, o_ref, om_ref, lse_ref):
        lane = lax.broadcasted_iota(jnp.int32, (bt, 128), 1)
        lo = lane < DIL_HEAD_DIM
        ls = [l0[...], l1[...], l2[...]]
        ws, stat = [], jnp.zeros((bt, 128), F32)
        for e in range(2):
            a = [l[:, e:e + 1] for l in ls]
            m = jnp.maximum(jnp.maximum(a[0], a[1]), a[2])
            ex = [jnp.exp(v - m) for v in a]
            tot = ex[0] + ex[1] + ex[2]
            ws.append([v / tot for v in ex])
            stat = jnp.where(lane == e, m + jnp.log(tot), stat)
        acc = jnp.zeros((bt, 128), F32)
        for gi, o in enumerate((o0, o1, o2)):
            acc = acc + jnp.where(lo, ws[0][gi], ws[1][gi]) * o[...]
        o_ref[...] = acc
        om_ref[...] = _mx(acc)
        lse_ref[...] = stat

    return pl.pallas_call(
        body, name="dil_merge", grid=(S // bt, 2),
        in_specs=[tile] * 6, out_specs=[tile, tile, tile],
        out_shape=[jax.ShapeDtypeStruct((S, 256), F32), jax.ShapeDtypeStruct((S, 256), MXU_DTYPE),
                   jax.ShapeDtypeStruct((S, 256), F32)],
        compiler_params=_cparams(("parallel", "parallel")),
    )(*os_, *lses)


def _dilated_delta(do, o, *, S, bt=512):
    tile = pl.BlockSpec((bt, 128), lambda i, p: (i, p))

    def body(do_ref, o_ref, d_ref):
        lane = lax.broadcasted_iota(jnp.int32, (bt, 128), 1)
        prod = do_ref[...] * o_ref[...]
        d0 = jnp.sum(jnp.where(lane < DIL_HEAD_DIM, prod, 0.0), axis=-1, keepdims=True)
        d1 = jnp.sum(jnp.where(lane >= DIL_HEAD_DIM, prod, 0.0), axis=-1, keepdims=True)
        d_ref[...] = jnp.where(lane == 0, d0, jnp.where(lane == 1, d1, 0.0))

    return pl.pallas_call(
        body, name="dil_delta", grid=(S // bt, 2), in_specs=[tile, tile], out_specs=tile,
        out_shape=jax.ShapeDtypeStruct((S, 256), F32), compiler_params=_cparams(("parallel", "parallel")),
    )(do, o)


MEM_T = 512
QM_BLK = C_QM // MEM_HEAD_DIM


def _mem_attn_fwd(proj, kv, *, S):
    scale = MEM_HEAD_DIM ** -0.5

    def body(q_ref, k_ref, v_ref, o_ref, om_ref, lse_ref):
        s = _dot(q_ref[...], k_ref[...], "nt") * scale
        m = jnp.max(s, axis=-1, keepdims=True)
        p = jnp.exp(s - m)
        den = jnp.sum(p, axis=-1, keepdims=True)
        o = _dot(p, v_ref[...]) / den
        o_ref[...] = o
        om_ref[...] = _mx(o)
        lse_ref[0] = m + jnp.log(den)

    return pl.pallas_call(
        body, name="mem_attn_fwd", grid=(S // MEM_T, MEM_HEADS),
        in_specs=[pl.BlockSpec((MEM_T, MEM_HEAD_DIM), lambda i, h: (i, QM_BLK + h)),
                  pl.BlockSpec((N_MEM, MEM_HEAD_DIM), lambda i, h: (0, h)),
                  pl.BlockSpec((N_MEM, MEM_HEAD_DIM), lambda i, h: (0, MEM_HEADS + h))],
        out_specs=[pl.BlockSpec((MEM_T, MEM_HEAD_DIM), lambda i, h: (i, h)),
                   pl.BlockSpec((MEM_T, MEM_HEAD_DIM), lambda i, h: (i, h)),
                   pl.BlockSpec((1, MEM_T, 1), lambda i, h: (h, i, 0))],
        out_shape=[jax.ShapeDtypeStruct((S, MEM_WIDTH), F32), jax.ShapeDtypeStruct((S, MEM_WIDTH), MXU_DTYPE),
                   jax.ShapeDtypeStruct((MEM_HEADS, S, 1), F32)],
        compiler_params=_cparams(("parallel", "parallel")),
    )(proj, kv, kv)


def _mem_attn_bwd(proj, kv, om, lse, dom, *, S):
    scale = MEM_HEAD_DIM ** -0.5

    def body(q_ref, k_ref, v_ref, o_ref, lse_ref, do_ref, dq_ref, dk_ref, dv_ref):
        @pl.when(pl.program_id(1) == 0)
        def _():
            dk_ref[...] = jnp.zeros_like(dk_ref)
            dv_ref[...] = jnp.zeros_like(dv_ref)

        qv, kv_, vv, dov = q_ref[...], k_ref[...], v_ref[...], do_ref[...]
        p = jnp.exp(_dot(qv, kv_, "nt") * scale - lse_ref[0])
        delta = jnp.sum(dov * o_ref[...], axis=-1, keepdims=True)
        ds = p * (_dot(dov, vv, "nt") - delta)
        dq_ref[...] = (_dot(ds, kv_) * scale).astype(dq_ref.dtype)
        dk_ref[...] += _dot(ds, qv, "tn") * scale
        dv_ref[...] += _dot(p, dov, "tn")

    tile = pl.BlockSpec((MEM_T, MEM_HEAD_DIM), lambda h, i: (i, h))
    kvo = pl.BlockSpec((N_MEM, MEM_HEAD_DIM), lambda h, i: (0, h))
    return pl.pallas_call(
        body, name="mem_attn_bwd", grid=(MEM_HEADS, S // MEM_T),
        in_specs=[pl.BlockSpec((MEM_T, MEM_HEAD_DIM), lambda h, i: (i, QM_BLK + h)),
                  pl.BlockSpec((N_MEM, MEM_HEAD_DIM), lambda h, i: (0, h)),
                  pl.BlockSpec((N_MEM, MEM_HEAD_DIM), lambda h, i: (0, MEM_HEADS + h)),
                  tile, pl.BlockSpec((1, MEM_T, 1), lambda h, i: (h, i, 0)), tile],
        out_specs=[tile, kvo, kvo],
        out_shape=[jax.ShapeDtypeStruct((S, MEM_WIDTH), MXU_DTYPE), jax.ShapeDtypeStruct((N_MEM, MEM_WIDTH), F32),
                   jax.ShapeDtypeStruct((N_MEM, MEM_WIDTH), F32)],
        compiler_params=_cparams(("parallel", "arbitrary")),
    )(proj, kv, kv, om, lse, dom)


MIX_BM = 1024
MIX_BN = 256
GATES_BLK = C_GATES // MIX_BN


def _mix_specs(j_outer):
    ix = (lambda f: (lambda j, i: f(i, j))) if j_outer else (lambda f: f)
    act = lambda width: pl.BlockSpec((MIX_BM, width), ix(lambda i, j: (i, 0)))
    wgt = lambda width: pl.BlockSpec((width, MIX_BN), ix(lambda i, j: (0, j)))
    gate = lambda b: pl.BlockSpec((MIX_BM, MIX_BN), ix(lambda i, j: (i, GATES_BLK + 4 * b + j)))
    bias = lambda b: pl.BlockSpec((1, MIX_BN), ix(lambda i, j: (0, 4 * b + j)))
    tile = pl.BlockSpec((MIX_BM, MIX_BN), ix(lambda i, j: (i, j)))
    return act, wgt, gate, bias, tile


def _mix_fwd(z_lru, o_dil, om, w_lru, w_dil, w_mem, proj, b_gate, *, S):
    act, wgt, gate, bias, tile = _mix_specs(False)

    def body(zl, od, mo, wl, wd, wm, g0, g1, g2, b0, b1, b2, o_ref):
        acc = jax.nn.sigmoid(g0[...] + b0[...]) * _dot(zl[...], wl[...])
        acc += jax.nn.sigmoid(g1[...] + b1[...]) * _dot(od[...], wd[...])
        acc += jax.nn.sigmoid(g2[...] + b2[...]) * _dot(mo[...], wm[...])
        o_ref[...] = acc.astype(o_ref.dtype)

    return pl.pallas_call(
        body, name="mix_fwd", grid=(S // MIX_BM, D_MODEL // MIX_BN),
        in_specs=[act(D_RNN), act(256), act(MEM_WIDTH), wgt(D_RNN), wgt(256), wgt(MEM_WIDTH),
                  gate(0), gate(1), gate(2), bias(0), bias(1), bias(2)],
        out_specs=tile, out_shape=jax.ShapeDtypeStruct((S, D_MODEL), MXU_DTYPE),
        compiler_params=_cparams(("parallel", "parallel")),
    )(z_lru, o_dil, om, w_lru, w_dil, w_mem, proj, proj, proj, b_gate, b_gate, b_gate)


def _mix_bwd(dmerged, z_lru, o_dil, om, w_lru, w_dil, w_mem, proj, b_gate, *, S):
    act, wgt, gate, bias, tile = _mix_specs(True)

    def body(dm, zl, od, mo, wl, wd, wm, g0, g1, g2, b0, b1, b2,
             dg0, dg1, dg2, dy0, dy1, dy2, db0, db1, db2):
        @pl.when(pl.program_id(1) == 0)
        def _():
            for r in (db0, db1, db2):
                r[...] = jnp.zeros_like(r)

        dmv = dm[...]
        for act_ref, w_ref, g_ref, b_ref, dg_ref, dy_ref, db_ref in (
                (zl, wl, g0, b0, dg0, dy0, db0), (od, wd, g1, b1, dg1, dy1, db1), (mo, wm, g2, b2, dg2, dy2, db2)):
            y = _dot(act_ref[...], w_ref[...])
            gt = jax.nn.sigmoid(g_ref[...] + b_ref[...])
            dgate = dmv * y * gt * (1.0 - gt)
            dg_ref[...] = dgate.astype(dg_ref.dtype)
            dy_ref[...] = (dmv * gt).astype(dy_ref.dtype)
            db_ref[...] += _colsum(dgate)

    big = jax.ShapeDtypeStruct((S, D_MODEL), MXU_DTYPE)
    vec = jax.ShapeDtypeStruct((1, D_MODEL), F32)
    vspec = pl.BlockSpec((1, MIX_BN), lambda j, i: (0, j))
    return pl.pallas_call(
        body, name="mix_bwd", grid=(D_MODEL // MIX_BN, S // MIX_BM),
        in_specs=[tile, act(D_RNN), act(256), act(MEM_WIDTH), wgt(D_RNN), wgt(256), wgt(MEM_WIDTH),
                  gate(0), gate(1), gate(2), bias(0), bias(1), bias(2)],
        out_specs=[tile] * 6 + [vspec] * 3, out_shape=[big] * 6 + [vec] * 3,
        compiler_params=_cparams(("parallel", "arbitrary")),
    )(dmerged, z_lru, o_dil, om, w_lru, w_dil, w_mem, proj, proj, proj, b_gate, b_gate, b_gate)


def _adamw_math(w, g, m, v):
    m = ADAM_B1 * m + (1.0 - ADAM_B1) * g
    v = ADAM_B2 * v + (1.0 - ADAM_B2) * (g * g)
    m_hat = m / (1.0 - ADAM_B1 ** ADAM_STEP)
    v_hat = v / (1.0 - ADAM_B2 ** ADAM_STEP)
    delta = -ADAM_LR * (m_hat / (jnp.sqrt(v_hat) + ADAM_EPS) + ADAM_WD * w)
    return delta, m, v


def _adamw_landed(w, own, land, m, v, *, name, row0=0, prev=None):
    R, C = w.shape
    n_parts, rows = land.shape[0], land.shape[1]
    br = min(rows, 256)
    blk0 = row0 // br
    tile = pl.BlockSpec((br, C), lambda i: (i + blk0, 0))
    part = pl.BlockSpec((br, C), lambda i: (i, 0))
    n_prev = 0 if prev is None else 4

    def body(w_ref, o_ref, l_ref, m_ref, v_ref, *rest):
        g_ref, d_ref, nm_ref, nv_ref = rest[n_prev:]
        g = o_ref[...]
        for p in range(n_parts):
            g = g + l_ref[p].astype(F32)
        d, nm, nv = _adamw_math(w_ref[...], g, m_ref[...], v_ref[...])
        g_ref[...] = g
        d_ref[...] = d
        nm_ref[...] = nm
        nv_ref[...] = nv

    return pl.pallas_call(
        body, name=name, grid=(rows // br,),
        in_specs=[tile, part, pl.BlockSpec((n_parts, br, C), lambda i: (0, i, 0)), tile, tile]
        + [pl.BlockSpec(memory_space=pl.ANY)] * n_prev,
        out_specs=[tile] * 4, out_shape=[jax.ShapeDtypeStruct((R, C), F32)] * 4,
        input_output_aliases={5 + j: j for j in range(n_prev)},
        compiler_params=_cparams(("parallel",)),
    )(w, own, land, m, v, *(prev or ()))


def _adamw_plain(w, g, m, v, *, name):
    def body(w_ref, g_ref, m_ref, v_ref, d_ref, nm_ref, nv_ref):
        d, nm, nv = _adamw_math(w_ref[...], g_ref[...], m_ref[...], v_ref[...])
        d_ref[...] = d
        nm_ref[...] = nm
        nv_ref[...] = nv

    return pl.pallas_call(
        body, name=name, out_shape=[jax.ShapeDtypeStruct(w.shape, F32)] * 3, compiler_params=_cparams(),
    )(w, g, m, v)


def _my_pos():
    return lax.axis_index("x"), lax.axis_index("y"), lax.axis_index("c")


def _dev_index(p):
    return 4 * p[0] + 2 * p[1] + p[2]


def _all_gather(shards):
    n = len(shards)
    hbm = pl.BlockSpec(memory_space=pl.ANY)

    def body(*refs):
        ins, outs = refs[:n], refs[n:2 * n]
        send_sems, recv_sems, local_sems = refs[2 * n:]
        x, y, c = _my_pos()
        me, sibling = (x, y, c), (x, y, 1 - c)
        chips = [(1 - x, y), (x, 1 - y), (1 - x, 1 - y)]

        def copy(a, k, block, to, src=None):
            dst = outs[a].at[_dev_index(block)]
            return pltpu.make_async_remote_copy(
                src_ref=dst if src is None else src, dst_ref=dst,
                send_sem=send_sems.at[a, k], recv_sem=recv_sems.at[a, k], device_id=to, device_id_type=MESH)

        mine = [pltpu.make_async_copy(ins[a], outs[a].at[_dev_index(me)], local_sems.at[a]) for a in range(n)]
        for cp in mine:
            cp.start()
        first = []
        for a in range(n):
            first.append(copy(a, 0, me, sibling, src=ins[a]))
            first += [copy(a, 1 + j, me, (*chip, c), src=ins[a]) for j, chip in enumerate(chips)]
        for cp in first:
            cp.start()
        passed = []
        for j, chip in enumerate(chips):
            for a in range(n):
                copy(a, 1 + j, (*chip, c), me).wait_recv()
                fwd = copy(a, 4 + j, (*chip, c), sibling)
                fwd.start()
                passed.append(fwd)
        for a in range(n):
            copy(a, 0, sibling, me).wait_recv()
        for j, chip in enumerate(chips):
            for a in range(n):
                copy(a, 4 + j, (*chip, 1 - c), me).wait_recv()
        for cp in first + passed:
            cp.wait_send()
        for cp in mine:
            cp.wait()

    return pl.pallas_call(
        body, name="all_gather_weights",
        in_specs=[hbm] * n, out_specs=[hbm] * n,
        out_shape=[jax.ShapeDtypeStruct((N_DEV,) + s.shape, s.dtype) for s in shards],
        scratch_shapes=[pltpu.SemaphoreType.DMA((n, 7)), pltpu.SemaphoreType.DMA((n, 7)), pltpu.SemaphoreType.DMA((n,))],
        compiler_params=pltpu.CompilerParams(has_side_effects=True),
    )(*shards)


def _peers(me):
    x, y, c = me
    out = []
    for k in range(1, 8):
        fx, fy, fc = (k >> 2) & 1, (k >> 1) & 1, k & 1
        out.append((k - 1, (1 - x if fx else x, 1 - y if fy else y, 1 - c if fc else c)))
    return out


HBM_SPEC = pl.BlockSpec(memory_space=pltpu.HBM)
SEM_SPEC = pl.BlockSpec(memory_space=pltpu.SEMAPHORE)
DATAFLOW_EFFECT = pltpu.SideEffectType.DATAFLOW_SIDE_EFFECTING


def _gather_refs(src, land, me, peer, k):
    return src, land.at[_dev_index(me)]


def _scatter_refs(src, land, me, peer, k):
    return src.at[_dev_index(peer)], land.at[k]


def _push_start(srcs, land_shapes, refs_of, name):
    n = len(srcs)

    def body(*refs):
        ins, lands = refs[:n], refs[n:2 * n]
        send_sems, recv_sems, token = refs[2 * n], refs[2 * n + 1], refs[-1]
        me = _my_pos()
        for k, peer in _peers(me):
            for a in range(n):
                src, dst = refs_of(ins[a], lands[a], me, peer, k)
                pltpu.make_async_remote_copy(src_ref=src, dst_ref=dst, send_sem=send_sems.at[7 * a + k],
                                             recv_sem=recv_sems.at[7 * a + k], device_id=peer, device_id_type=MESH).start()
        token[...] = jnp.zeros_like(token)

    lands = [lax.empty(shp, s.dtype) for shp, s in zip(land_shapes, srcs)]
    hbm = lambda a: pltpu.with_memory_space_constraint(a, pltpu.HBM)
    res = pl.pallas_call(
        body, name=name,
        out_shape=(pltpu.SemaphoreType.DMA((7 * n,)), pltpu.SemaphoreType.DMA((7 * n,)),
                   *[pltpu.HBM(s.shape, s.dtype) for s in srcs], *[pltpu.HBM(l.shape, l.dtype) for l in lands],
                   jax.ShapeDtypeStruct((8, 128), F32)),
        in_specs=[HBM_SPEC] * (2 * n),
        out_specs=(SEM_SPEC, SEM_SPEC, *[HBM_SPEC] * (2 * n), pl.BlockSpec(memory_space=pltpu.VMEM)),
        input_output_aliases={i: 2 + i for i in range(2 * n)},
        compiler_params=pltpu.CompilerParams(has_side_effects=DATAFLOW_EFFECT),
    )(*[hbm(s) for s in srcs], *[hbm(l) for l in lands])
    return dict(sems=(res[0], res[1]), srcs=list(res[2:2 + n]), lands=list(res[2 + n:2 + 2 * n]), token=res[-1], n=n,
                refs_of=refs_of, name=name)


def _push_wait(started, after):
    n, refs_of = started["n"], started["refs_of"]

    def body(*refs):
        ins, lands = refs[:n], refs[n:2 * n]
        send_sems, recv_sems = refs[2 * n], refs[2 * n + 1]
        me = _my_pos()
        for k, peer in _peers(me):
            for a in range(n):
                src, dst = refs_of(ins[a], lands[a], me, peer, k)
                cp = pltpu.make_async_remote_copy(src_ref=src, dst_ref=dst, send_sem=send_sems.at[7 * a + k],
                                                  recv_sem=recv_sems.at[7 * a + k], device_id=peer, device_id_type=MESH)
                cp.wait_send()
                cp.wait_recv()

    arrs = started["srcs"] + started["lands"]
    res = pl.pallas_call(
        body, name=started["name"].replace("start", "wait"),
        out_shape=tuple(pltpu.HBM(a.shape, a.dtype) for a in arrs),
        in_specs=[HBM_SPEC] * (2 * n) + [SEM_SPEC, SEM_SPEC, pl.BlockSpec(memory_space=pl.ANY)],
        out_specs=tuple([HBM_SPEC] * (2 * n)),
        input_output_aliases={i: i for i in range(2 * n)},
        compiler_params=pltpu.CompilerParams(has_side_effects=DATAFLOW_EFFECT),
    )(*arrs, *started["sems"], after)
    return list(res[n:2 * n])


def _all_reduce_small(pack):
    R = pack.shape[0]

    def body(in_ref, out_ref, land, send_sems, recv_sems):
        me = _my_pos()
        my_idx = _dev_index(me)
        land[my_idx] = in_ref[...]
        sent = []
        for k, peer in _peers(me):
            cp = pltpu.make_async_remote_copy(
                src_ref=in_ref, dst_ref=land.at[my_idx], send_sem=send_sems.at[k], recv_sem=recv_sems.at[k],
                device_id=peer, device_id_type=MESH)
            cp.start()
            sent.append(cp)
        for k, peer in _peers(me):
            pltpu.make_async_remote_copy(
                src_ref=in_ref, dst_ref=land.at[_dev_index(peer)], send_sem=send_sems.at[k], recv_sem=recv_sems.at[k],
                device_id=peer, device_id_type=MESH).wait_recv()
        for cp in sent:
            cp.wait_send()
        acc = land[0]
        for d in range(1, N_DEV):
            acc = acc + land[d]
        out_ref[...] = acc

    return pl.pallas_call(
        body, name="all_reduce_small",
        in_specs=[pl.BlockSpec(memory_space=pltpu.VMEM)], out_specs=pl.BlockSpec(memory_space=pltpu.VMEM),
        out_shape=jax.ShapeDtypeStruct(pack.shape, F32),
        scratch_shapes=[pltpu.VMEM((N_DEV, R, 128), F32), pltpu.SemaphoreType.DMA((7,)), pltpu.SemaphoreType.DMA((7,))],
        compiler_params=pltpu.CompilerParams(has_side_effects=True, vmem_limit_bytes=VMEM_LIMIT_BYTES),
    )(pack)


def _local_step(x, mem, tgt, W, P, late_weights, send_grads, reduce_small, tie0):
    S = x.shape[0]
    W = dict(W)
    h = _rmsnorm_fwd(x, P["g_mix"] + tie0, rows=S, name="norm_mix")
    proj = _matmul(h, W["w_in"], M=S, N=D_IN, K=D_MODEL, mode="nn", bm=512, bn=D_IN // 2, bk=D_MODEL, name="mm_in",
                   j_outer=True)

    wa_bd, wx_bd = _mx(_block_diag(P["w_rg_a"])), _mx(_block_diag(P["w_rg_x"]))
    lru_args = (W["conv_w"], P["conv_b"].reshape(1, -1), wa_bd, wx_bd, P["b_rg_a"].reshape(1, -1),
                P["b_rg_x"].reshape(1, -1), P["lru_lambda"].reshape(1, -1))
    hl, z_lru = _lru_fwd(proj, *lru_args, S=S)

    buckets = _dil_buckets()
    bias = _dil_bias(P["rel_bias"], buckets)
    group_out = [_dilated_fwd(proj, bias, g, S=S) for g in range(len(DIL_GROUPS))]
    o_dil, o_dil_m, lse_dil = _dilated_merge([o for o, _ in group_out], [l for _, l in group_out], S=S)

    W.update(late_weights(o_dil))
    mem_n = _rmsnorm_fwd(mem, P["g_mem"], rows=N_MEM, name="norm_mem")
    kv = _matmul(mem_n, W["w_mem_kv"], M=N_MEM, N=2 * MEM_WIDTH, K=D_MODEL, mode="nn", bm=N_MEM, bn=512, bk=D_MODEL,
                 name="mm_kv")
    om, om_m, lse_mem = _mem_attn_fwd(proj, kv, S=S)
    b_gate = P["b_gate"].reshape(1, -1)
    merged = _mix_fwd(z_lru, o_dil_m, om_m, W["w_lru_out"], W["w_dil_out"], W["w_mem_out"], proj, b_gate, S=S)
    x1 = _matmul(merged, W["w_out"], M=S, N=D_MODEL, K=D_MODEL, mode="nn", bm=512, bn=D_MODEL, bk=D_MODEL, name="mm_out",
                 epilogue=lambda acc, r: (r + acc,), extras=[(x, (0, 0))])
    hm = _rmsnorm_fwd(x1, P["g_mlp"], rows=S, name="norm_mlp")

    def relu2(acc):
        rl = jnp.maximum(acc, 0.0)
        return (rl * rl,)

    act = _matmul(hm, W["w_mlp_in"], M=S, N=D_FF, K=D_MODEL, mode="nn", bm=512, bn=1024, bk=D_MODEL, name="mm_mlp_in",
                  out_dtypes=(MXU_DTYPE,), epilogue=relu2, j_outer=True)
    x2 = _matmul(act, W["w_mlp_out"], M=S, N=D_MODEL, K=D_FF, mode="nn", bm=512, bn=D_MODEL, bk=D_FF, name="mm_mlp_out",
                 epilogue=lambda acc, r: (r + acc,), extras=[(x1, (0, 0))])
    loss, dx2, dx2_m, dg_final = _loss_head(x2, P["g_final"], tgt, rows=S)

    G, Gs = {}, {}
    Gs["g_final"] = dg_final
    G["w_mlp_out"] = _matmul(act, dx2_m, M=D_FF, N=D_MODEL, K=S, mode="tn", bm=512, bn=D_MODEL, bk=S, name="mm_dw_mlp_out")
    du = _matmul(dx2_m, W["w_mlp_out"], M=S, N=D_FF, K=D_MODEL, mode="nt", bm=512, bn=1024, bk=D_MODEL, name="mm_du",
                 out_dtypes=(MXU_DTYPE,), epilogue=lambda acc, a: (acc * (2.0 * jnp.sqrt(a.astype(F32))),),
                 extras=[(act, (0, 0))], j_outer=True)
    G["w_mlp_in"] = _matmul(hm, du, M=D_MODEL, N=D_FF, K=S, mode="tn", bm=D_MODEL, bn=512, bk=S, name="mm_dw_mlp_in")
    tie1 = send_grads({n: G.pop(n) for n in ("w_mlp_out", "w_mlp_in")})
    dhm = _matmul(du, W["w_mlp_in"], M=S, N=D_MODEL, K=D_FF, mode="nt", bm=512, bn=D_MODEL, bk=D_FF, name="mm_dhm",
                  deps=[tie1])
    dx1, dx1_m, Gs["g_mlp"] = _rmsnorm_bwd(x1, P["g_mlp"], dhm, dx2, rows=S, name="norm_mlp_bwd",
                                           dx_dtypes=(F32, MXU_DTYPE))
    G["w_out"] = _matmul(merged, dx1_m, M=D_MODEL, N=D_MODEL, K=S, mode="tn", bm=512, bn=D_MODEL, bk=S, name="mm_dw_out")
    dmerged = _matmul(dx1_m, W["w_out"], M=S, N=D_MODEL, K=D_MODEL, mode="nt", bm=512, bn=D_MODEL, bk=D_MODEL, name="mm_dmerged")
    (dg0, dg1, dg2, dy_lru, dy_dil, dy_mem, db0, db1, db2) = _mix_bwd(
        dmerged, z_lru, o_dil_m, om_m, W["w_lru_out"], W["w_dil_out"], W["w_mem_out"], proj, b_gate, S=S)
    Gs["b_gate"] = jnp.concatenate([db0, db1, db2], axis=1)

    G["w_mem_out"] = _matmul(om_m, dy_mem, M=MEM_WIDTH, N=D_MODEL, K=S, mode="tn", bm=MEM_WIDTH, bn=D_MODEL, bk=S,
                             name="mm_dw_mem_out")
    dom = _matmul(dy_mem, W["w_mem_out"], M=S, N=MEM_WIDTH, K=D_MODEL, mode="nt", bm=512, bn=MEM_WIDTH, bk=D_MODEL,
                  name="mm_dom")
    dqm, dk_mem, dv_mem = _mem_attn_bwd(proj, kv, om, lse_mem, dom, S=S)
    dkv = jnp.concatenate([dk_mem, dv_mem], axis=1)
    G["w_mem_kv"] = _matmul(mem_n, dkv, M=D_MODEL, N=2 * MEM_WIDTH, K=N_MEM, mode="tn", bm=D_MODEL, bn=2 * MEM_WIDTH,
                            bk=N_MEM, name="mm_dw_kv")
    dmem_n = _matmul(dkv, W["w_mem_kv"], M=N_MEM, N=D_MODEL, K=2 * MEM_WIDTH, mode="nt", bm=N_MEM, bn=D_MODEL,
                     bk=2 * MEM_WIDTH, name="mm_dmem")
    (Gs["g_mem"],) = _rmsnorm_bwd(mem, P["g_mem"], dmem_n, None, rows=N_MEM, name="norm_mem_bwd", dx_dtypes=())

    G["w_dil_out"] = _matmul(o_dil_m, dy_dil, M=256, N=D_MODEL, K=S, mode="tn", bm=256, bn=D_MODEL, bk=S, name="mm_dw_dil_out")
    do_dil = _matmul(dy_dil, W["w_dil_out"], M=S, N=256, K=D_MODEL, mode="nt", bm=512, bn=256, bk=D_MODEL, name="mm_do_dil")
    G["w_lru_out"] = _matmul(z_lru, dy_lru, M=D_RNN, N=D_MODEL, K=S, mode="tn", bm=D_RNN, bn=D_MODEL, bk=S, name="mm_dw_lru_out")
    dz = _matmul(dy_lru, W["w_lru_out"], M=S, N=D_RNN, K=D_MODEL, mode="nt", bm=512, bn=D_RNN, bk=D_MODEL, name="mm_dz_lru")
    tie2 = send_grads({n: G.pop(n) for n in ("w_out", "w_mem_out", "w_mem_kv", "w_dil_out", "w_lru_out")})
    bias = bias + tie2[0, 0]
    delta = _dilated_delta(do_dil, o_dil, S=S)
    dq_parts, dk_parts, dv_parts, dbias = [], [], [], []
    for g in range(len(DIL_GROUPS)):
        dq_g, dk_g, dv_g, db_g = _dilated_bwd(proj, do_dil, lse_dil, delta, bias, g, S=S)
        dq_parts.append(dq_g)
        dk_parts.append(dk_g)
        dv_parts.append(dv_g)
        dbias.append(db_g)
    drel = _dil_bias_bwd(jnp.stack(dbias, axis=0), buckets)
    Gs["rel_bias"] = drel[:, :3 * DIL_HEADS]

    dxl, dgl, dcw, dcb, dwa, dwx, dba, dbx, dlam = _lru_bwd(proj, hl, dz, *lru_args, S=S)
    Gs["conv_w"], Gs["conv_b"] = dcw, dcb
    Gs["w_rg_a"], Gs["w_rg_x"] = _block_diag_extract(dwa), _block_diag_extract(dwx)
    Gs["b_rg_a"], Gs["b_rg_x"], Gs["lru_lambda"] = dba, dbx, dlam

    dproj = jnp.concatenate([dxl, dgl] + dq_parts + dk_parts + dv_parts + [dqm, dg0, dg1, dg2], axis=1)
    dh = _matmul(dproj, W["w_in"], M=S, N=D_MODEL, K=D_IN, mode="nt", bm=256, bn=D_MODEL, bk=D_IN, name="mm_dh")
    grad_x, Gs["g_mix"] = _rmsnorm_bwd(x, P["g_mix"], dh, dx1, rows=S, name="norm_mix_bwd")
    reduced = reduce_small(Gs, loss)
    half = D_MODEL // 2
    dw_a = _matmul(h, dproj, M=half, N=D_IN, K=S, mode="tn", bm=half, bn=D_IN // 2, bk=1024, name="mm_dw_in_a",
                   deps=[reduced])
    tie_a = send_grads({"w_in_a": dw_a})
    dw_b = _matmul(h, dproj, M=half, N=D_IN, K=S, mode="tn", bm=half, bn=D_IN // 2, bk=1024, name="mm_dw_in_b",
                   a_off=(0, 1), deps=[tie_a])
    tie_b = send_grads({"w_in_b": dw_b})
    return grad_x, tie_b


BIG = ("w_in", "w_lru_out", "w_dil_out", "w_mem_kv", "w_mem_out", "w_out", "w_mlp_in", "w_mlp_out")
COL_SHARDED = ("w_in", "w_in_a", "w_in_b", "w_lru_out", "w_dil_out", "w_mem_out", "w_mlp_in")
SMALL = ("g_mix", "b_gate", "conv_b", "w_rg_a", "b_rg_a", "w_rg_x", "b_rg_x", "lru_lambda", "rel_bias", "g_mem",
         "g_mlp", "g_final")
WEIGHTS = ("g_mix", "w_in", "b_gate", "conv_w", "conv_b", "w_rg_a", "b_rg_a", "w_rg_x", "b_rg_x", "lru_lambda",
           "w_lru_out", "rel_bias", "w_dil_out", "g_mem", "w_mem_kv", "w_mem_out", "w_out", "g_mlp", "w_mlp_in",
           "w_mlp_out", "g_final")
SMALL_SHAPES = {"g_mix": (1024,), "b_gate": (3072,), "conv_b": (768,), "w_rg_a": (12, 64, 64), "b_rg_a": (768,),
                "w_rg_x": (12, 64, 64), "b_rg_x": (768,), "lru_lambda": (768,), "rel_bias": (32, 12), "g_mem": (1024,),
                "g_mlp": (1024,), "g_final": (1024,)}


def _gathered_to_full(name, gathered):
    if name in COL_SHARDED:
        n, r, c = gathered.shape
        return gathered.transpose(1, 0, 2).reshape(r, n * c)
    n, r, c = gathered.shape
    return gathered.reshape(n * r, c)


def _full_to_parts(name, full):
    if name in COL_SHARDED:
        r, nc = full.shape
        return _mx(full.reshape(r, N_DEV, nc // N_DEV).transpose(1, 0, 2))
    nr, c = full.shape
    return _mx(full.reshape(N_DEV, nr // N_DEV, c))


def _own_part(name, full, my_idx):
    if name in COL_SHARDED:
        r, nc = full.shape
        return lax.dynamic_slice(full, (0, my_idx * (nc // N_DEV)), (r, nc // N_DEV))
    nr, c = full.shape
    return lax.dynamic_slice(full, (my_idx * (nr // N_DEV), 0), (nr // N_DEV, c))


def _pack(parts):
    flat = jnp.concatenate([p.reshape(-1).astype(F32) for p in parts])
    pad = (-flat.shape[0]) % 1024
    return jnp.pad(flat, (0, pad)).reshape(-1, 128)


def _unpack(pack, shapes):
    flat = pack.reshape(-1)
    out, off = [], 0
    for shp in shapes:
        size = math.prod(shp)
        out.append(flat[off:off + size].reshape(shp))
        off += size
    return out


def kernel(x, mem, g_mix, w_in, b_gate, conv_w, conv_b, w_rg_a, b_rg_a, w_rg_x, b_rg_x, lru_lambda, w_lru_out, rel_bias, w_dil_out, g_mem, w_mem_kv, w_mem_out, w_out, g_mlp, w_mlp_in, w_mlp_out, g_final, loss_target, m_g_mix, m_w_in, m_b_gate, m_conv_w, m_conv_b, m_w_rg_a, m_b_rg_a, m_w_rg_x, m_b_rg_x, m_lru_lambda, m_w_lru_out, m_rel_bias, m_w_dil_out, m_g_mem, m_w_mem_kv, m_w_mem_out, m_w_out, m_g_mlp, m_w_mlp_in, m_w_mlp_out, m_g_final, v_g_mix, v_w_in, v_b_gate, v_conv_w, v_conv_b, v_w_rg_a, v_b_rg_a, v_w_rg_x, v_b_rg_x, v_lru_lambda, v_w_lru_out, v_rel_bias, v_w_dil_out, v_g_mem, v_w_mem_kv, v_w_mem_out, v_w_out, v_g_mlp, v_w_mlp_in, v_w_mlp_out, v_g_final):
    w = dict(g_mix=g_mix, w_in=w_in, b_gate=b_gate, conv_w=conv_w, conv_b=conv_b, w_rg_a=w_rg_a, b_rg_a=b_rg_a,
             w_rg_x=w_rg_x, b_rg_x=b_rg_x, lru_lambda=lru_lambda, w_lru_out=w_lru_out, rel_bias=rel_bias,
             w_dil_out=w_dil_out, g_mem=g_mem, w_mem_kv=w_mem_kv, w_mem_out=w_mem_out, w_out=w_out, g_mlp=g_mlp,
             w_mlp_in=w_mlp_in, w_mlp_out=w_mlp_out, g_final=g_final)
    m = dict(g_mix=m_g_mix, w_in=m_w_in, b_gate=m_b_gate, conv_w=m_conv_w, conv_b=m_conv_b, w_rg_a=m_w_rg_a,
             b_rg_a=m_b_rg_a, w_rg_x=m_w_rg_x, b_rg_x=m_b_rg_x, lru_lambda=m_lru_lambda, w_lru_out=m_w_lru_out,
             rel_bias=m_rel_bias, w_dil_out=m_w_dil_out, g_mem=m_g_mem, w_mem_kv=m_w_mem_kv, w_mem_out=m_w_mem_out,
             w_out=m_w_out, g_mlp=m_g_mlp, w_mlp_in=m_w_mlp_in, w_mlp_out=m_w_mlp_out, g_final=m_g_final)
    v = dict(g_mix=v_g_mix, w_in=v_w_in, b_gate=v_b_gate, conv_w=v_conv_w, conv_b=v_conv_b, w_rg_a=v_w_rg_a,
             b_rg_a=v_b_rg_a, w_rg_x=v_w_rg_x, b_rg_x=v_b_rg_x, lru_lambda=v_lru_lambda, w_lru_out=v_w_lru_out,
             rel_bias=v_rel_bias, w_dil_out=v_w_dil_out, g_mem=v_g_mem, w_mem_kv=v_w_mem_kv, w_mem_out=v_w_mem_out,
             w_out=v_w_out, g_mlp=v_g_mlp, w_mlp_in=v_w_mlp_in, w_mlp_out=v_w_mlp_out, g_final=v_g_final)

    my_idx = _dev_index(_my_pos())

    g_in, g_cw = _all_gather([_mx(w["w_in"]), w["conv_w"]])
    W = {"w_in": _gathered_to_full("w_in", g_in), "conv_w": g_cw.transpose(1, 0, 2).reshape(CONV_WIDTH, D_RNN)}
    late = [n for n in BIG if n != "w_in"]
    late_shards = [_mx(w[n]) for n in late]
    late_started = _push_start(late_shards, [(N_DEV,) + s.shape for s in late_shards], _gather_refs,
                                "gather_late_start")
    P = {n: w[n] for n in SMALL}

    def late_weights(after):
        lands = _push_wait(late_started, after)
        out = {}
        for n, land, own in zip(late, lands, late_shards):
            full = lax.dynamic_update_index_in_dim(land, own, my_idx, 0)
            out[n] = _gathered_to_full(n, full)
        return out

    sent, small = [], {}
    small_names = SMALL + ("conv_w",)

    def send_grads(gs):
        names = list(gs)
        parts = [_full_to_parts(n, gs[n]) for n in names]
        own = [_own_part(n, gs[n], my_idx) for n in names]
        started = _push_start(parts, [(N_DEV - 1,) + p.shape[1:] for p in parts], _scatter_refs,
                              f"scatter{len(sent)}_start")
        sent.append((names, own, started))
        return started["token"]

    def reduce_small(gs, loss):
        small["total"] = _all_reduce_small(_pack([gs[n] for n in small_names] + [loss]))
        return small["total"]

    grad_x, last_token = _local_step(x[0], mem[0], loss_target[0], W, P, late_weights, send_grads, reduce_small,
                                     late_started["token"][0, 0])
    total = small["total"]
    small_shapes = [SMALL_SHAPES[n] for n in SMALL] + [(CONV_WIDTH, D_RNN), (1,)]
    summed = dict(zip(small_names + ("loss",), _unpack(total, small_shapes)))

    grads, deltas, new_m, new_v = {}, {}, {}, {}
    after = last_token
    for names, own, started in sent[:-2]:
        for n, o, land in zip(names, own, _push_wait(started, after)):
            grads[n], deltas[n], new_m[n], new_v[n] = _adamw_landed(w[n], o, land, m[n], v[n], name=f"adamw_{n}")
            after = deltas[n]
    prev = None
    for (names, own, started), row0 in zip(sent[-2:], (0, D_MODEL // 2)):
        (land,) = _push_wait(started, after)
        prev = _adamw_landed(w["w_in"], own[0], land, m["w_in"], v["w_in"], name=f"adamw_{names[0]}", row0=row0,
                             prev=prev)
    grads["w_in"], deltas["w_in"], new_m["w_in"], new_v["w_in"] = prev
    zeros_cw = jnp.zeros((CONV_WIDTH, D_RNN), F32)
    w_pack = _pack([w[n] for n in SMALL] + [zeros_cw, jnp.zeros((1,), F32)])
    m_pack = _pack([m[n] for n in SMALL] + [zeros_cw, jnp.zeros((1,), F32)])
    v_pack = _pack([v[n] for n in SMALL] + [zeros_cw, jnp.zeros((1,), F32)])
    d_pack, nm_pack, nv_pack = _adamw_plain(w_pack, total, m_pack, v_pack, name="adamw_small")
    for dst, pk in ((deltas, d_pack), (new_m, nm_pack), (new_v, nv_pack)):
        dst.update(zip(SMALL, _unpack(pk, [SMALL_SHAPES[n] for n in SMALL])))
    for n in SMALL:
        grads[n] = summed[n]
    cw_cols = D_RNN // N_DEV
    grads["conv_w"] = lax.dynamic_slice(summed["conv_w"], (0, my_idx * cw_cols), (CONV_WIDTH, cw_cols))
    deltas["conv_w"], new_m["conv_w"], new_v["conv_w"] = _adamw_plain(
        w["conv_w"], grads["conv_w"], m["conv_w"], v["conv_w"], name="adamw_conv_w")

    return (summed["loss"].reshape(()), grad_x[None], *[grads[n] for n in WEIGHTS], *[deltas[n] for n in WEIGHTS],
            *[new_m[n] for n in WEIGHTS], *[new_v[n] for n in WEIGHTS])
```

```python
import functools
import math

import jax
import jax.numpy as jnp
from jax import lax
from jax.experimental import pallas as pl
from jax.experimental.pallas import tpu as pltpu

F32 = jnp.float32
MXU_DTYPE = jnp.bfloat16
VMEM_LIMIT_BYTES = 56 * 1024 * 1024
N_DEV = 8

D_MODEL = 1024
N_MEM = 256
MEM_HEADS = 4
MEM_HEAD_DIM = 128
MEM_WIDTH = 512
D_RNN = 768
LRU_BLOCK = 64
N_LRU_BLOCKS = 12
LRU_GROUP = 256
N_LRU_GROUPS = 3
CONV_WIDTH = 4
LRU_C = 8.0
DIL_GROUPS = ((128, 1), (512, 4), (2048, 16))
SPAN = 128
DIL_HEADS = 4
DIL_HEAD_DIM = 64
NUM_BUCKETS = 32
MAX_DISTANCE = 2048
D_FF = 4096
D_IN = 7424
EPS = 1e-6
NEG = -1e30
C_XL, C_GATE, C_QKV, C_QM, C_GATES = 0, 768, 1536, 3840, 4352

ADAM_LR = 0.001
ADAM_B1 = 0.9
ADAM_B2 = 0.999
ADAM_EPS = 1e-08
ADAM_WD = 0.01
ADAM_STEP = 10

MESH = pl.DeviceIdType.MESH
GELU_K = math.sqrt(2.0 / math.pi)


def _cparams(sem=None):
    kw = dict(vmem_limit_bytes=VMEM_LIMIT_BYTES)
    if sem is not None:
        kw["dimension_semantics"] = sem
    return pltpu.CompilerParams(**kw)


def _mx(v):
    return v.astype(MXU_DTYPE)


def _dot(a, b, mode="nn"):
    dims = {"nn": (((1,), (0,)), ((), ())), "nt": (((1,), (1,)), ((), ())), "tn": (((0,), (0,)), ((), ()))}[mode]
    return lax.dot_general(_mx(a), _mx(b), dims, preferred_element_type=F32)


def _colsum(v):
    return jnp.sum(v, axis=0, keepdims=True)


def _matmul(a, b, *, M, N, K, mode, bm, bn, bk, name, out_dtypes=(F32,), epilogue=None, extras=(),
            a_off=(0, 0), b_off=(0, 0), j_outer=False, deps=()):
    assert M % bm == 0 and N % bn == 0 and K % bk == 0, (name, M, N, K, bm, bn, bk)
    nm, nn, nk = M // bm, N // bn, K // bk

    def ij(f):
        if j_outer:
            return lambda j, i, k: f(i, j, k)
        return f

    if mode == "tn":
        a_spec = pl.BlockSpec((bk, bm), ij(lambda i, j, k: (k + a_off[0], i + a_off[1])))
    else:
        a_spec = pl.BlockSpec((bm, bk), ij(lambda i, j, k: (i + a_off[0], k + a_off[1])))
    if mode == "nt":
        b_spec = pl.BlockSpec((bn, bk), ij(lambda i, j, k: (j + b_off[0], k + b_off[1])))
    else:
        b_spec = pl.BlockSpec((bk, bn), ij(lambda i, j, k: (k + b_off[0], j + b_off[1])))
    ex_specs = [pl.BlockSpec((bm, bn), ij(functools.partial(lambda i, j, k, o: (i + o[0], j + o[1]), o=off)))
                for _, off in extras]
    out_spec = pl.BlockSpec((bm, bn), ij(lambda i, j, k: (i, j)))
    n_ex, n_out, n_dep = len(extras), len(out_dtypes), len(deps)

    def body(*refs):
        a_ref, b_ref = refs[0], refs[1]
        ex = refs[2:2 + n_ex]
        outs = refs[2 + n_ex + n_dep:2 + n_ex + n_dep + n_out]
        part = _dot(a_ref[...], b_ref[...], mode)

        def finish(acc):
            vals = epilogue(acc, *[e[...] for e in ex]) if epilogue is not None else (acc,)
            for o, v in zip(outs, vals):
                o[...] = v.astype(o.dtype)

        if nk == 1:
            finish(part)
        else:
            acc_ref = refs[-1]
            k = pl.program_id(2)

            @pl.when(k == 0)
            def _():
                acc_ref[...] = part

            @pl.when(k > 0)
            def _():
                acc_ref[...] += part

            @pl.when(k == nk - 1)
            def _():
                finish(acc_ref[...])

    grid = (nn, nm, nk) if j_outer else (nm, nn, nk)
    res = pl.pallas_call(
        body, name=name, grid=grid,
        in_specs=[a_spec, b_spec] + ex_specs + [pl.BlockSpec(memory_space=pl.ANY)] * n_dep,
        out_specs=[out_spec] * n_out,
        out_shape=[jax.ShapeDtypeStruct((M, N), dt) for dt in out_dtypes],
        scratch_shapes=[pltpu.VMEM((bm, bn), F32)] if nk > 1 else [],
        compiler_params=_cparams(("parallel", "parallel", "arbitrary")),
    )(a, b, *[e for e, _ in extras], *deps)
    return res[0] if n_out == 1 else res


def _rmsnorm_fwd(x, g, *, rows, name, bt=512):
    bt = min(bt, rows)

    def body(x_ref, g_ref, o_ref):
        xv = x_ref[...]
        r = lax.rsqrt(jnp.mean(xv * xv, axis=-1, keepdims=True) + EPS)
        o_ref[...] = (xv * r * g_ref[...]).astype(o_ref.dtype)

    return pl.pallas_call(
        body, name=name, grid=(rows // bt,),
        in_specs=[pl.BlockSpec((bt, D_MODEL), lambda i: (i, 0)), pl.BlockSpec((1, D_MODEL), lambda i: (0, 0))],
        out_specs=pl.BlockSpec((bt, D_MODEL), lambda i: (i, 0)),
        out_shape=jax.ShapeDtypeStruct((rows, D_MODEL), MXU_DTYPE),
        compiler_params=_cparams(("parallel",)),
    )(x, g.reshape(1, D_MODEL))


def _rms_bwd_tile(xv, gv, dyv):
    r = lax.rsqrt(jnp.mean(xv * xv, axis=-1, keepdims=True) + EPS)
    w = dyv * gv
    dx = r * w - xv * (r * r * r) * jnp.mean(w * xv, axis=-1, keepdims=True)
    dg = _colsum(dyv * xv * r)
    return dx, dg


def _rmsnorm_bwd(x, g, dy, res, *, rows, name, bt=512, dx_dtypes=(F32,)):
    bt = min(bt, rows)
    has_res = res is not None

    def body(*refs):
        x_ref, g_ref, dy_ref = refs[:3]
        res_ref = refs[3] if has_res else None
        outs = refs[3 + int(has_res):]
        dx, dg = _rms_bwd_tile(x_ref[...], g_ref[...], dy_ref[...])
        if has_res:
            dx = dx + res_ref[...]
        dg_ref = outs[-1]

        @pl.when(pl.program_id(0) == 0)
        def _():
            dg_ref[...] = jnp.zeros_like(dg_ref)

        dg_ref[...] += dg
        for o in outs[:-1]:
            o[...] = dx.astype(o.dtype)

    row_spec = pl.BlockSpec((bt, D_MODEL), lambda i: (i, 0))
    vec_spec = pl.BlockSpec((1, D_MODEL), lambda i: (0, 0))
    ins = [x, g.reshape(1, D_MODEL), dy] + ([res] if has_res else [])
    return pl.pallas_call(
        body, name=name, grid=(rows // bt,),
        in_specs=[row_spec, vec_spec, row_spec] + ([row_spec] if has_res else []),
        out_specs=[row_spec] * len(dx_dtypes) + [vec_spec],
        out_shape=[jax.ShapeDtypeStruct((rows, D_MODEL), dt) for dt in dx_dtypes] + [jax.ShapeDtypeStruct((1, D_MODEL), F32)],
        compiler_params=_cparams(("arbitrary",)),
    )(*ins)


def _loss_head(x2, g, tgt, *, rows, bt=512):
    def body(x_ref, g_ref, t_ref, loss_ref, dx_ref, dxm_ref, dg_ref):
        xv, gv = x_ref[...], g_ref[...]
        r = lax.rsqrt(jnp.mean(xv * xv, axis=-1, keepdims=True) + EPS)
        diff = xv * r * gv - t_ref[...]
        part = jnp.sum(jnp.mean(diff * diff, axis=-1, keepdims=True), axis=0, keepdims=True) * 0.5
        dx, dg = _rms_bwd_tile(xv, gv, diff * (1.0 / D_MODEL))

        @pl.when(pl.program_id(0) == 0)
        def _():
            loss_ref[...] = jnp.zeros_like(loss_ref)
            dg_ref[...] = jnp.zeros_like(dg_ref)

        loss_ref[...] += part
        dg_ref[...] += dg
        dx_ref[...] = dx
        dxm_ref[...] = _mx(dx)

    row_spec = pl.BlockSpec((bt, D_MODEL), lambda i: (i, 0))
    vec_spec = pl.BlockSpec((1, D_MODEL), lambda i: (0, 0))
    return pl.pallas_call(
        body, name="loss_head", grid=(rows // bt,),
        in_specs=[row_spec, vec_spec, row_spec],
        out_specs=[pl.BlockSpec((1, 1), lambda i: (0, 0)), row_spec, row_spec, vec_spec],
        out_shape=[jax.ShapeDtypeStruct((1, 1), F32), jax.ShapeDtypeStruct((rows, D_MODEL), F32),
                   jax.ShapeDtypeStruct((rows, D_MODEL), MXU_DTYPE), jax.ShapeDtypeStruct((1, D_MODEL), F32)],
        compiler_params=_cparams(("arbitrary",)),
    )(x2, g.reshape(1, D_MODEL), tgt)


LRU_T = 256


def _gelu(x):
    t = jnp.tanh(GELU_K * (x + 0.044715 * x * x * x))
    return 0.5 * x * (1.0 + t), t


def _gelu_grad(x, t):
    return 0.5 * (1.0 + t) + 0.5 * x * (1.0 - t * t) * GELU_K * (1.0 + 3.0 * 0.044715 * x * x)


def _softplus_neg(lam):
    z = -lam
    u = jnp.exp(-jnp.abs(z))
    w = 1.0 + u
    l1p = jnp.where(w == 1.0, u, jnp.log(w) * u / jnp.where(w == 1.0, 1.0, w - 1.0))
    return jnp.maximum(z, 0.0) + l1p


def _shift_down(cur, prev8, k, row8):
    y = pltpu.roll(cur, k, 0)
    head = jnp.where(row8 < k, pltpu.roll(prev8, k, 0), y[0:8])
    return jnp.concatenate([head, y[8:]], axis=0)


def _shift_up(cur, next8, k, row8):
    n = cur.shape[0]
    y = pltpu.roll(cur, n - k, 0)
    tail = jnp.where(row8 >= 8 - k, pltpu.roll(next8, 8 - k, 0), y[n - 8:n])
    return jnp.concatenate([y[0:n - 8], tail], axis=0)


def _lru_gates(xl, p8, cw, cb, wa, wx, ba, bx, lam, row8):
    sh = [xl] + [_shift_down(xl, p8, k, row8) for k in (1, 2, 3)]
    xc = cb + cw[3:4] * sh[0] + cw[2:3] * sh[1] + cw[1:2] * sh[2] + cw[0:1] * sh[3]
    r = jax.nn.sigmoid(_dot(xc, wa) + ba)
    i = jax.nn.sigmoid(_dot(xc, wx) + bx)
    sp = _softplus_neg(lam)
    la = -LRU_C * r * sp
    a = jnp.exp(la)
    mult = jnp.sqrt(jnp.tanh(-la) * (a * a + 1.0))
    return dict(sh=sh, xc=xc, r=r, i=i, sp=sp, a=a, mult=mult)


def _lru_specs(n_t, reverse):
    T = LRU_T
    tt = (lambda t: n_t - 1 - t) if reverse else (lambda t: t)
    blk = lambda col0: pl.BlockSpec((T, LRU_GROUP), lambda g, t: (tt(t), col0 + g))
    prev8 = lambda col0: pl.BlockSpec((8, LRU_GROUP), lambda g, t: (jnp.maximum(tt(t) * (T // 8) - 1, 0), col0 + g))
    vec = lambda rows: pl.BlockSpec((rows, LRU_GROUP), lambda g, t: (0, g))
    wbd = pl.BlockSpec((1, LRU_GROUP, LRU_GROUP), lambda g, t: (g, 0, 0))
    return blk, prev8, vec, wbd


def _lru_fwd(proj, conv_w, conv_b, wa_bd, wx_bd, b_a, b_x, lam, *, S):
    T = LRU_T
    n_t = S // T
    blk, _, vec, wbd = _lru_specs(n_t, False)

    def body(xl_ref, gate_ref, cw_ref, cb_ref, wa_ref, wx_ref, ba_ref, bx_ref, lam_ref,
             hl_ref, z_ref, prev8, hcar, a_s, b_s):
        @pl.when(pl.program_id(1) == 0)
        def _():
            prev8[...] = jnp.zeros_like(prev8)
            hcar[...] = jnp.zeros_like(hcar)

        row8 = lax.broadcasted_iota(jnp.int32, (8, LRU_GROUP), 0)
        xl = xl_ref[...]
        q = _lru_gates(xl, prev8[...], cw_ref[...], cb_ref[...], wa_ref[0], wx_ref[0], ba_ref[...], bx_ref[...],
                       lam_ref[...], row8)
        prev8[...] = xl[T - 8:T]
        a_s[...] = q["a"]
        b_s[...] = q["mult"] * q["i"] * q["xc"]

        def step(c, carry):
            off = pl.multiple_of(c * 8, 8)
            A = a_s[pl.ds(off, 8), :]
            B = b_s[pl.ds(off, 8), :]
            for k in (1, 2, 4):
                a_sh = jnp.where(row8 >= k, pltpu.roll(A, k, 0), 1.0)
                b_sh = jnp.where(row8 >= k, pltpu.roll(B, k, 0), 0.0)
                B = A * b_sh + B
                A = A * a_sh
            h = A * carry + B
            hl_ref[pl.ds(off, 8), :] = h
            return h[7:8, :]

        hcar[...] = lax.fori_loop(0, T // 8, step, hcar[...])
        ge, _ = _gelu(gate_ref[...])
        z_ref[...] = (ge * hl_ref[...]).astype(z_ref.dtype)

    return pl.pallas_call(
        body, name="lru_fwd", grid=(N_LRU_GROUPS, n_t),
        in_specs=[blk(C_XL // LRU_GROUP), blk(C_GATE // LRU_GROUP), vec(4), vec(1), wbd, wbd, vec(1), vec(1), vec(1)],
        out_specs=[blk(0), blk(0)],
        out_shape=[jax.ShapeDtypeStruct((S, D_RNN), F32), jax.ShapeDtypeStruct((S, D_RNN), MXU_DTYPE)],
        scratch_shapes=[pltpu.VMEM((8, LRU_GROUP), F32), pltpu.VMEM((1, LRU_GROUP), F32),
                        pltpu.VMEM((T, LRU_GROUP), F32), pltpu.VMEM((T, LRU_GROUP), F32)],
        compiler_params=_cparams(("parallel", "arbitrary")),
    )(proj, proj, conv_w, conv_b, wa_bd, wx_bd, b_a, b_x, lam)


def _lru_bwd(proj, hl, dz, conv_w, conv_b, wa_bd, wx_bd, b_a, b_x, lam, *, S):
    T = LRU_T
    n_t = S // T
    blk, prev8s, vec, wbd = _lru_specs(n_t, True)

    def body(xl_ref, xlp_ref, gate_ref, hl_ref, hlp_ref, dz_ref, cw_ref, cb_ref, wa_ref, wx_ref, ba_ref, bx_ref,
             lam_ref, dxl_ref, dgate_ref, dcw_ref, dcb_ref, dwa_ref, dwx_ref, dba_ref, dbx_ref, dlam_ref,
             next8, gcar, c_s, b_s, l_s):
        t = pl.program_id(1)
        first_chunk = t == n_t - 1

        @pl.when(t == 0)
        def _():
            next8[...] = jnp.zeros_like(next8)
            gcar[...] = jnp.zeros_like(gcar)
            for ref in (dcw_ref, dcb_ref, dwa_ref, dwx_ref, dba_ref, dbx_ref, dlam_ref):
                ref[...] = jnp.zeros_like(ref)

        row8 = lax.broadcasted_iota(jnp.int32, (8, LRU_GROUP), 0)
        rowT = lax.broadcasted_iota(jnp.int32, (T, LRU_GROUP), 0)
        keep = jnp.where(first_chunk, 0.0, 1.0)
        xl = xl_ref[...]
        wa, wx, lam_v = wa_ref[0], wx_ref[0], lam_ref[...]
        q = _lru_gates(xl, xlp_ref[...] * keep, cw_ref[...], cb_ref[...], wa, wx, ba_ref[...], bx_ref[...], lam_v, row8)
        a, mult, r, i, xc, sp = q["a"], q["mult"], q["r"], q["i"], q["xc"], q["sp"]
        hl_v = hl_ref[...]
        dz_v = dz_ref[...]
        gate = gate_ref[...]
        ge, th = _gelu(gate)
        dgate_ref[...] = (dz_v * hl_v * _gelu_grad(gate, th)).astype(dgate_ref.dtype)

        c_s[...] = jnp.where(rowT == T - 1, 0.0, pltpu.roll(a, T - 1, 0))
        b_s[...] = dz_v * ge + jnp.where(rowT == T - 1, gcar[...], 0.0)

        def step(n, carry):
            off = pl.multiple_of((T // 8 - 1 - n) * 8, 8)
            C = c_s[pl.ds(off, 8), :]
            B = b_s[pl.ds(off, 8), :]
            for k in (1, 2, 4):
                c_sh = jnp.where(row8 < 8 - k, pltpu.roll(C, 8 - k, 0), 1.0)
                b_sh = jnp.where(row8 < 8 - k, pltpu.roll(B, 8 - k, 0), 0.0)
                B = B + C * b_sh
                C = C * c_sh
            lam_t = B + C * carry
            l_s[pl.ds(off, 8), :] = lam_t
            return lam_t[0:1, :]

        lax.fori_loop(0, T // 8, step, jnp.zeros((1, LRU_GROUP), F32))
        lmb = l_s[...]
        gcar[...] = a[0:1, :] * lmb[0:1, :]

        h_prev = _shift_down(hl_v, hlp_ref[...] * keep, 1, row8)
        da = lmb * h_prev
        dmult = lmb * i * xc
        di = lmb * mult * xc
        dxc = lmb * mult * i
        dla = da * a - dmult * (a * a) / mult
        dr = dla * (-LRU_C * sp)
        dlam_ref[...] += _colsum(dla * (-LRU_C * r)) * (-jax.nn.sigmoid(-lam_v))
        dpa = dr * r * (1.0 - r)
        dpx = di * i * (1.0 - i)
        dxc = dxc + _dot(dpa, wa, "nt") + _dot(dpx, wx, "nt")
        dwa_ref[0] += _dot(xc, dpa, "tn")
        dwx_ref[0] += _dot(xc, dpx, "tn")
        dba_ref[...] += _colsum(dpa)
        dbx_ref[...] += _colsum(dpx)
        dcb_ref[...] += _colsum(dxc)
        cw = cw_ref[...]
        n8 = next8[...]
        dxl = cw[3:4] * dxc
        for k in (1, 2, 3):
            dxl = dxl + cw[3 - k:4 - k] * _shift_up(dxc, n8, k, row8)
        for k in range(4):
            dcw_ref[3 - k:4 - k, :] += _colsum(dxc * q["sh"][k])
        next8[...] = dxc[0:8]
        dxl_ref[...] = dxl.astype(dxl_ref.dtype)

    res = pl.pallas_call(
        body, name="lru_bwd", grid=(N_LRU_GROUPS, n_t),
        in_specs=[blk(C_XL // LRU_GROUP), prev8s(C_XL // LRU_GROUP), blk(C_GATE // LRU_GROUP), blk(0), prev8s(0), blk(0),
                  vec(4), vec(1), wbd, wbd, vec(1), vec(1), vec(1)],
        out_specs=[blk(0), blk(0), vec(4), vec(1), wbd, wbd, vec(1), vec(1), vec(1)],
        out_shape=[jax.ShapeDtypeStruct((S, D_RNN), MXU_DTYPE), jax.ShapeDtypeStruct((S, D_RNN), MXU_DTYPE),
                   jax.ShapeDtypeStruct((4, D_RNN), F32), jax.ShapeDtypeStruct((1, D_RNN), F32),
                   jax.ShapeDtypeStruct((N_LRU_GROUPS, LRU_GROUP, LRU_GROUP), F32),
                   jax.ShapeDtypeStruct((N_LRU_GROUPS, LRU_GROUP, LRU_GROUP), F32),
                   jax.ShapeDtypeStruct((1, D_RNN), F32), jax.ShapeDtypeStruct((1, D_RNN), F32),
                   jax.ShapeDtypeStruct((1, D_RNN), F32)],
        scratch_shapes=[pltpu.VMEM((8, LRU_GROUP), F32), pltpu.VMEM((1, LRU_GROUP), F32),
                        pltpu.VMEM((T, LRU_GROUP), F32), pltpu.VMEM((T, LRU_GROUP), F32), pltpu.VMEM((T, LRU_GROUP), F32)],
        compiler_params=_cparams(("parallel", "arbitrary")),
    )(proj, proj, proj, hl, hl, dz, conv_w, conv_b, wa_bd, wx_bd, b_a, b_x, lam)
    return res


def _block_diag(w):
    w4 = w.reshape(N_LRU_GROUPS, 4, LRU_BLOCK, 1, LRU_BLOCK)
    eye = jnp.eye(4, dtype=w.dtype).reshape(1, 4, 1, 4, 1)
    return (w4 * eye).reshape(N_LRU_GROUPS, LRU_GROUP, LRU_GROUP)


def _block_diag_extract(wbd):
    w5 = wbd.reshape(N_LRU_GROUPS, 4, LRU_BLOCK, 4, LRU_BLOCK)
    return jnp.stack([w5[:, a, :, a, :] for a in range(4)], axis=1).reshape(N_LRU_BLOCKS, LRU_BLOCK, LRU_BLOCK)


def _t5_bucket(dist):
    max_exact = NUM_BUCKETS // 2
    df = jnp.maximum(dist, 1).astype(jnp.float32)
    large = max_exact + (jnp.log(df / max_exact) / math.log(MAX_DISTANCE / max_exact)
                         * (NUM_BUCKETS - max_exact)).astype(jnp.int32)
    large = jnp.minimum(large, NUM_BUCKETS - 1)
    return jnp.where(dist < max_exact, dist, large)


def _band_offsets():
    qi = jnp.arange(SPAN)[:, None]
    kj = jnp.arange(2 * SPAN)[None, :]
    return qi + SPAN - kj


def _dil_buckets():
    off = _band_offsets()
    return jnp.stack([_t5_bucket(jnp.maximum(off, 0) * dil) for _, dil in DIL_GROUPS]).astype(jnp.int32)


def _dil_bias(rel_bias, buckets):
    def body(tbl_ref, bk_ref, o_ref):
        g = pl.program_id(0)
        qi = lax.broadcasted_iota(jnp.int32, (SPAN, 2 * SPAN), 0)
        kj = lax.broadcasted_iota(jnp.int32, (SPAN, 2 * SPAN), 1)
        off = qi + SPAN - kj
        valid = (off >= 0) & (off <= SPAN)
        bk = bk_ref[0]
        for h in range(DIL_HEADS):
            acc = jnp.zeros((SPAN, 2 * SPAN), F32)
            for b in range(NUM_BUCKETS):
                acc = jnp.where(bk == b, tbl_ref[b, g * DIL_HEADS + h], acc)
            o_ref[0, h] = jnp.where(valid, acc, NEG)

    return pl.pallas_call(
        body, name="dil_bias", grid=(3,),
        in_specs=[pl.BlockSpec(memory_space=pltpu.SMEM), pl.BlockSpec((1, SPAN, 2 * SPAN), lambda g: (g, 0, 0))],
        out_specs=pl.BlockSpec((1, DIL_HEADS, SPAN, 2 * SPAN), lambda g: (g, 0, 0, 0)),
        out_shape=jax.ShapeDtypeStruct((3, DIL_HEADS, SPAN, 2 * SPAN), F32),
        compiler_params=_cparams(("parallel",)),
    )(rel_bias, buckets)


def _dil_bias_bwd(dbias, buckets):
    def body(db_ref, bk_ref, o_ref):
        lane = lax.broadcasted_iota(jnp.int32, (1, 128), 1)
        rows = [jnp.zeros((1, 128), F32) for _ in range(NUM_BUCKETS)]
        for g in range(3):
            bk = bk_ref[g]
            for h in range(DIL_HEADS):
                d = db_ref[g, h]
                for b in range(NUM_BUCKETS):
                    tot = jnp.sum(_colsum(jnp.where(bk == b, d, 0.0)), axis=1, keepdims=True)
                    rows[b] = jnp.where(lane == g * DIL_HEADS + h, tot, rows[b])
        for b in range(NUM_BUCKETS):
            o_ref[b:b + 1, :] = rows[b]

    return pl.pallas_call(
        body, name="dil_bias_bwd",
        out_shape=jax.ShapeDtypeStruct((NUM_BUCKETS, 128), F32),
        compiler_params=_cparams(),
    )(dbias, buckets)


DIL_SUBBLOCKS = (4, 2, 1)


def _dil_layout(g, S):
    dil, m = DIL_GROUPS[g][1], DIL_SUBBLOCKS[g]
    sub = SPAN * dil
    col = [(C_QKV + t * 768 + g * 256) // 128 for t in range(3)]
    return dil, m, sub, S // (sub * m), col


def _residue_rows(b, r, dil):
    return pl.ds(b * SPAN * dil + r, SPAN, stride=dil) if dil > 1 else pl.ds(b * SPAN, SPAN)


def _for_residues(dil, fn):
    if dil <= 4:
        for r in range(dil):
            fn(r)
    else:
        lax.fori_loop(0, dil, lambda r, c: (fn(r), c)[1], 0, unroll=2)


def _pair_scores(qm, k2, bias, first_cols):
    s = _dot(qm, k2, "nt") * (DIL_HEAD_DIM ** -0.5) + bias
    kj = lax.broadcasted_iota(jnp.int32, s.shape, 1)
    return jnp.where(kj < first_cols, NEG, s)


def _dilated_fwd(proj, bias, g, *, S):
    dil, m, sub, nc, (qc, kc, vc) = _dil_layout(g, S)
    R = sub * m
    cur = lambda cb: pl.BlockSpec((R, 128), lambda p, i: (i, cb + p))
    prv = lambda cb: pl.BlockSpec((sub, 128), lambda p, i: (jnp.maximum(i * m - 1, 0), cb + p))
    out = pl.BlockSpec((R, 128), lambda p, i: (i, p))

    def body(q_ref, kp_ref, kc_ref, vp_ref, vc_ref, b_ref, o_ref, lse_ref):
        lane = lax.broadcasted_iota(jnp.int32, (SPAN, 128), 1)
        sels = (lane < DIL_HEAD_DIM, lane >= DIL_HEAD_DIM)
        for b in range(m):
            first_cols = jnp.where(pl.program_id(1) == 0, SPAN, 0) if b == 0 else 0

            def one(r, b=b, first_cols=first_cols):
                rows = _residue_rows(b, r, dil)
                before = (kc_ref, vc_ref, _residue_rows(b - 1, r, dil)) if b else (kp_ref, vp_ref, _residue_rows(0, r, dil))
                q2 = q_ref[rows, :]
                k2 = _mx(jnp.concatenate([before[0][before[2], :], kc_ref[rows, :]], axis=0))
                v2 = _mx(jnp.concatenate([before[1][before[2], :], vc_ref[rows, :]], axis=0))
                stat = jnp.zeros((SPAN, 128), F32)
                o2 = jnp.zeros((SPAN, 128), F32)
                for e in range(2):
                    s = _pair_scores(jnp.where(sels[e], q2, 0.0), k2, b_ref[0, e], first_cols)
                    mx = jnp.max(s, axis=-1, keepdims=True)
                    p = jnp.exp(s - mx)
                    den = jnp.sum(p, axis=-1, keepdims=True)
                    o2 = jnp.where(sels[e], _dot(p, v2) / den, o2)
                    stat = jnp.where(lane == e, mx + jnp.log(den), stat)
                o_ref[rows, :] = o2
                lse_ref[rows, :] = stat

            _for_residues(dil, one)

    return pl.pallas_call(
        body, name=f"dil_fwd{g}", grid=(2, nc),
        in_specs=[cur(qc), prv(kc), cur(kc), prv(vc), cur(vc),
                  pl.BlockSpec((1, 2, SPAN, 2 * SPAN), lambda p, i: (g, p, 0, 0))],
        out_specs=[out, out],
        out_shape=[jax.ShapeDtypeStruct((S, 256), F32), jax.ShapeDtypeStruct((S, 256), F32)],
        compiler_params=_cparams(("parallel", "parallel")),
    )(proj, proj, proj, proj, proj, bias)


def _dilated_bwd(proj, do, lse, delta, bias, g, *, S):
    dil, m, sub, nc, (qc, kc, vc) = _dil_layout(g, S)
    R = sub * m
    cl = lambda i: jnp.minimum(i, nc - 1)
    cur = lambda cb: pl.BlockSpec((R, 128), lambda p, i: (cl(i), cb + p))
    prv = lambda cb: pl.BlockSpec((sub, 128), lambda p, i: (jnp.maximum(cl(i) * m - 1, 0), cb + p))
    kv_out = pl.BlockSpec((R, 128), lambda p, i: (jnp.maximum(i - 1, 0), p))
    scale = DIL_HEAD_DIM ** -0.5

    def body(q_ref, kp_ref, kc_ref, vp_ref, vc_ref, do_ref, lse_ref, dl_ref, b_ref,
             dq_ref, dk_ref, dv_ref, db_ref, dq_s, kc_s, vc_s, kp_s, vp_s, kcar, vcar):
        i = pl.program_id(1)

        @pl.when(i == 0)
        def _():
            kcar[...] = jnp.zeros_like(kcar)
            vcar[...] = jnp.zeros_like(vcar)
            db_ref[...] = jnp.zeros_like(db_ref)

        @pl.when(i < nc)
        def _():
            lane = lax.broadcasted_iota(jnp.int32, (SPAN, 128), 1)
            sels = (lane < DIL_HEAD_DIM, lane >= DIL_HEAD_DIM)
            for b in range(m):
                first_cols = jnp.where(i == 0, SPAN, 0) if b == 0 else 0

                def one(r, b=b, first_cols=first_cols):
                    rows = _residue_rows(b, r, dil)
                    rows_before = _residue_rows(b - 1 if b else 0, r, dil)
                    k_before, v_before = (kc_ref, vc_ref) if b else (kp_ref, vp_ref)
                    q2, do2 = q_ref[rows, :], do_ref[rows, :]
                    k2 = _mx(jnp.concatenate([k_before[rows_before, :], kc_ref[rows, :]], axis=0))
                    v2 = _mx(jnp.concatenate([v_before[rows_before, :], vc_ref[rows, :]], axis=0))
                    lse_t, dl_t = lse_ref[rows, :], dl_ref[rows, :]
                    dq2 = jnp.zeros((SPAN, 128), F32)
                    dk2 = jnp.zeros((2 * SPAN, 128), F32)
                    dv2 = jnp.zeros((2 * SPAN, 128), F32)
                    for e in range(2):
                        qm = jnp.where(sels[e], q2, 0.0)
                        dom = jnp.where(sels[e], do2, 0.0)
                        p = jnp.exp(_pair_scores(qm, k2, b_ref[0, e], first_cols) - lse_t[:, e:e + 1])
                        ds = p * (_dot(dom, v2, "nt") - dl_t[:, e:e + 1])
                        db_ref[e] += ds
                        dq2 = jnp.where(sels[e], _dot(ds, k2) * scale, dq2)
                        dk2 = dk2 + _dot(ds, qm, "tn") * scale
                        dv2 = dv2 + _dot(p, dom, "tn")
                    dq_s[rows, :] = dq2
                    kc_s[rows, :] = dk2[SPAN:2 * SPAN]
                    vc_s[rows, :] = dv2[SPAN:2 * SPAN]
                    if b:
                        kc_s[rows_before, :] += dk2[0:SPAN]
                        vc_s[rows_before, :] += dv2[0:SPAN]
                    else:
                        kp_s[rows_before, :] = dk2[0:SPAN]
                        vp_s[rows_before, :] = dv2[0:SPAN]

                _for_residues(dil, one)
            dq_ref[...] = dq_s[...].astype(dq_ref.dtype)
            last = pl.ds((m - 1) * sub, sub)
            kcar[last, :] += kp_s[...]
            vcar[last, :] += vp_s[...]
            dk_ref[...] = kcar[...].astype(dk_ref.dtype)
            dv_ref[...] = vcar[...].astype(dv_ref.dtype)
            kcar[...] = kc_s[...]
            vcar[...] = vc_s[...]

        @pl.when(i == nc)
        def _():
            dk_ref[...] = kcar[...].astype(dk_ref.dtype)
            dv_ref[...] = vcar[...].astype(dv_ref.dtype)

    stat = pl.BlockSpec((R, 128), lambda p, i: (cl(i), p))
    big = jax.ShapeDtypeStruct((S, 256), MXU_DTYPE)
    return pl.pallas_call(
        body, name=f"dil_bwd{g}", grid=(2, nc + 1),
        in_specs=[cur(qc), prv(kc), cur(kc), prv(vc), cur(vc), stat, stat, stat,
                  pl.BlockSpec((1, 2, SPAN, 2 * SPAN), lambda p, i: (g, p, 0, 0))],
        out_specs=[stat, kv_out, kv_out, pl.BlockSpec((2, SPAN, 2 * SPAN), lambda p, i: (p, 0, 0))],
        out_shape=[big, big, big, jax.ShapeDtypeStruct((DIL_HEADS, SPAN, 2 * SPAN), F32)],
        scratch_shapes=[pltpu.VMEM((R, 128), F32)] * 3 + [pltpu.VMEM((sub, 128), F32)] * 2 + [pltpu.VMEM((R, 128), F32)] * 2,
        compiler_params=_cparams(("parallel", "arbitrary")),
    )(proj, proj, proj, proj, proj, do, lse, delta, bias)


def _dilated_merge(os_, lses, *, S, bt=512):
    tile = pl.BlockSpec((bt, 128), lambda i, p: (i, p))

    def body(o0, o1, o2, l0, l1, l2, o_ref, om_ref, lse_ref):
        lane = lax.broadcasted_iota(jnp.int32, (bt, 128), 1)
        lo = lane < DIL_HEAD_DIM
        ls = [l0[...], l1[...], l2[...]]
        ws, stat = [], jnp.zeros((bt, 128), F32)
        for e in range(2):
            a = [l[:, e:e + 1] for l in ls]
            m = jnp.maximum(jnp.maximum(a[0], a[1]), a[2])
            ex = [jnp.exp(v - m) for v in a]
            tot = ex[0] + ex[1] + ex[2]
            ws.append([v / tot for v in ex])
            stat = jnp.where(lane == e, m + jnp.log(tot), stat)
        acc = jnp.zeros((bt, 128), F32)
        for gi, o in enumerate((o0, o1, o2)):
            acc = acc + jnp.where(lo, ws[0][gi], ws[1][gi]) * o[...]
        o_ref[...] = acc
        om_ref[...] = _mx(acc)
        lse_ref[...] = stat

    return pl.pallas_call(
        body, name="dil_merge", grid=(S // bt, 2),
        in_specs=[tile] * 6, out_specs=[tile, tile, tile],
        out_shape=[jax.ShapeDtypeStruct((S, 256), F32), jax.ShapeDtypeStruct((S, 256), MXU_DTYPE),
                   jax.ShapeDtypeStruct((S, 256), F32)],
        compiler_params=_cparams(("parallel", "parallel")),
    )(*os_, *lses)


def _dilated_delta(do, o, *, S, bt=512):
    tile = pl.BlockSpec((bt, 128), lambda i, p: (i, p))

    def body(do_ref, o_ref, d_ref):
        lane = lax.broadcasted_iota(jnp.int32, (bt, 128), 1)
        prod = do_ref[...] * o_ref[...]
        d0 = jnp.sum(jnp.where(lane < DIL_HEAD_DIM, prod, 0.0), axis=-1, keepdims=True)
        d1 = jnp.sum(jnp.where(lane >= DIL_HEAD_DIM, prod, 0.0), axis=-1, keepdims=True)
        d_ref[...] = jnp.where(lane == 0, d0, jnp.where(lane == 1, d1, 0.0))

    return pl.pallas_call(
        body, name="dil_delta", grid=(S // bt, 2), in_specs=[tile, tile], out_specs=tile,
        out_shape=jax.ShapeDtypeStruct((S, 256), F32), compiler_params=_cparams(("parallel", "parallel")),
    )(do, o)


MEM_T = 512
QM_BLK = C_QM // MEM_HEAD_DIM


def _mem_attn_fwd(proj, kv, *, S):
    scale = MEM_HEAD_DIM ** -0.5

    def body(q_ref, k_ref, v_ref, o_ref, om_ref, lse_ref):
        s = _dot(q_ref[...], k_ref[...], "nt") * scale
        m = jnp.max(s, axis=-1, keepdims=True)
        p = jnp.exp(s - m)
        den = jnp.sum(p, axis=-1, keepdims=True)
        o = _dot(p, v_ref[...]) / den
        o_ref[...] = o
        om_ref[...] = _mx(o)
        lse_ref[0] = m + jnp.log(den)

    return pl.pallas_call(
        body, name="mem_attn_fwd", grid=(S // MEM_T, MEM_HEADS),
        in_specs=[pl.BlockSpec((MEM_T, MEM_HEAD_DIM), lambda i, h: (i, QM_BLK + h)),
                  pl.BlockSpec((N_MEM, MEM_HEAD_DIM), lambda i, h: (0, h)),
                  pl.BlockSpec((N_MEM, MEM_HEAD_DIM), lambda i, h: (0, MEM_HEADS + h))],
        out_specs=[pl.BlockSpec((MEM_T, MEM_HEAD_DIM), lambda i, h: (i, h)),
                   pl.BlockSpec((MEM_T, MEM_HEAD_DIM), lambda i, h: (i, h)),
                   pl.BlockSpec((1, MEM_T, 1), lambda i, h: (h, i, 0))],
        out_shape=[jax.ShapeDtypeStruct((S, MEM_WIDTH), F32), jax.ShapeDtypeStruct((S, MEM_WIDTH), MXU_DTYPE),
                   jax.ShapeDtypeStruct((MEM_HEADS, S, 1), F32)],
        compiler_params=_cparams(("parallel", "parallel")),
    )(proj, kv, kv)


def _mem_attn_bwd(proj, kv, om, lse, dom, *, S):
    scale = MEM_HEAD_DIM ** -0.5

    def body(q_ref, k_ref, v_ref, o_ref, lse_ref, do_ref, dq_ref, dk_ref, dv_ref):
        @pl.when(pl.program_id(1) == 0)
        def _():
            dk_ref[...] = jnp.zeros_like(dk_ref)
            dv_ref[...] = jnp.zeros_like(dv_ref)

        qv, kv_, vv, dov = q_ref[...], k_ref[...], v_ref[...], do_ref[...]
        p = jnp.exp(_dot(qv, kv_, "nt") * scale - lse_ref[0])
        delta = jnp.sum(dov * o_ref[...], axis=-1, keepdims=True)
        ds = p * (_dot(dov, vv, "nt") - delta)
        dq_ref[...] = (_dot(ds, kv_) * scale).astype(dq_ref.dtype)
        dk_ref[...] += _dot(ds, qv, "tn") * scale
        dv_ref[...] += _dot(p, dov, "tn")

    tile = pl.BlockSpec((MEM_T, MEM_HEAD_DIM), lambda h, i: (i, h))
    kvo = pl.BlockSpec((N_MEM, MEM_HEAD_DIM), lambda h, i: (0, h))
    return pl.pallas_call(
        body, name="mem_attn_bwd", grid=(MEM_HEADS, S // MEM_T),
        in_specs=[pl.BlockSpec((MEM_T, MEM_HEAD_DIM), lambda h, i: (i, QM_BLK + h)),
                  pl.BlockSpec((N_MEM, MEM_HEAD_DIM), lambda h, i: (0, h)),
                  pl.BlockSpec((N_MEM, MEM_HEAD_DIM), lambda h, i: (0, MEM_HEADS + h)),
                  tile, pl.BlockSpec((1, MEM_T, 1), lambda h, i: (h, i, 0)), tile],
        out_specs=[tile, kvo, kvo],
        out_shape=[jax.ShapeDtypeStruct((S, MEM_WIDTH), MXU_DTYPE), jax.ShapeDtypeStruct((N_MEM, MEM_WIDTH), F32),
                   jax.ShapeDtypeStruct((N_MEM, MEM_WIDTH), F32)],
        compiler_params=_cparams(("parallel", "arbitrary")),
    )(proj, kv, kv, om, lse, dom)


MIX_BM = 1024
MIX_BN = 256
GATES_BLK = C_GATES // MIX_BN


def _mix_specs(j_outer):
    ix = (lambda f: (lambda j, i: f(i, j))) if j_outer else (lambda f: f)
    act = lambda width: pl.BlockSpec((MIX_BM, width), ix(lambda i, j: (i, 0)))
    wgt = lambda width: pl.BlockSpec((width, MIX_BN), ix(lambda i, j: (0, j)))
    gate = lambda b: pl.BlockSpec((MIX_BM, MIX_BN), ix(lambda i, j: (i, GATES_BLK + 4 * b + j)))
    bias = lambda b: pl.BlockSpec((1, MIX_BN), ix(lambda i, j: (0, 4 * b + j)))
    tile = pl.BlockSpec((MIX_BM, MIX_BN), ix(lambda i, j: (i, j)))
    return act, wgt, gate, bias, tile


def _mix_fwd(z_lru, o_dil, om, w_lru, w_dil, w_mem, proj, b_gate, *, S):
    act, wgt, gate, bias, tile = _mix_specs(False)

    def body(zl, od, mo, wl, wd, wm, g0, g1, g2, b0, b1, b2, o_ref):
        acc = jax.nn.sigmoid(g0[...] + b0[...]) * _dot(zl[...], wl[...])
        acc += jax.nn.sigmoid(g1[...] + b1[...]) * _dot(od[...], wd[...])
        acc += jax.nn.sigmoid(g2[...] + b2[...]) * _dot(mo[...], wm[...])
        o_ref[...] = acc.astype(o_ref.dtype)

    return pl.pallas_call(
        body, name="mix_fwd", grid=(S // MIX_BM, D_MODEL // MIX_BN),
        in_specs=[act(D_RNN), act(256), act(MEM_WIDTH), wgt(D_RNN), wgt(256), wgt(MEM_WIDTH),
                  gate(0), gate(1), gate(2), bias(0), bias(1), bias(2)],
        out_specs=tile, out_shape=jax.ShapeDtypeStruct((S, D_MODEL), MXU_DTYPE),
        compiler_params=_cparams(("parallel", "parallel")),
    )(z_lru, o_dil, om, w_lru, w_dil, w_mem, proj, proj, proj, b_gate, b_gate, b_gate)


def _mix_bwd(dmerged, z_lru, o_dil, om, w_lru, w_dil, w_mem, proj, b_gate, *, S):
    act, wgt, gate, bias, tile = _mix_specs(True)

    def body(dm, zl, od, mo, wl, wd, wm, g0, g1, g2, b0, b1, b2,
             dg0, dg1, dg2, dy0, dy1, dy2, db0, db1, db2):
        @pl.when(pl.program_id(1) == 0)
        def _():
            for r in (db0, db1, db2):
                r[...] = jnp.zeros_like(r)

        dmv = dm[...]
        for act_ref, w_ref, g_ref, b_ref, dg_ref, dy_ref, db_ref in (
                (zl, wl, g0, b0, dg0, dy0, db0), (od, wd, g1, b1, dg1, dy1, db1), (mo, wm, g2, b2, dg2, dy2, db2)):
            y = _dot(act_ref[...], w_ref[...])
            gt = jax.nn.sigmoid(g_ref[...] + b_ref[...])
            dgate = dmv * y * gt * (1.0 - gt)
            dg_ref[...] = dgate.astype(dg_ref.dtype)
            dy_ref[...] = (dmv * gt).astype(dy_ref.dtype)
            db_ref[...] += _colsum(dgate)

    big = jax.ShapeDtypeStruct((S, D_MODEL), MXU_DTYPE)
    vec = jax.ShapeDtypeStruct((1, D_MODEL), F32)
    vspec = pl.BlockSpec((1, MIX_BN), lambda j, i: (0, j))
    return pl.pallas_call(
        body, name="mix_bwd", grid=(D_MODEL // MIX_BN, S // MIX_BM),
        in_specs=[tile, act(D_RNN), act(256), act(MEM_WIDTH), wgt(D_RNN), wgt(256), wgt(MEM_WIDTH),
                  gate(0), gate(1), gate(2), bias(0), bias(1), bias(2)],
        out_specs=[tile] * 6 + [vspec] * 3, out_shape=[big] * 6 + [vec] * 3,
        compiler_params=_cparams(("parallel", "arbitrary")),
    )(dmerged, z_lru, o_dil, om, w_lru, w_dil, w_mem, proj, proj, proj, b_gate, b_gate, b_gate)


def _adamw_math(w, g, m, v):
    m = ADAM_B1 * m + (1.0 - ADAM_B1) * g
    v = ADAM_B2 * v + (1.0 - ADAM_B2) * (g * g)
    m_hat = m / (1.0 - ADAM_B1 ** ADAM_STEP)
    v_hat = v / (1.0 - ADAM_B2 ** ADAM_STEP)
    delta = -ADAM_LR * (m_hat / (jnp.sqrt(v_hat) + ADAM_EPS) + ADAM_WD * w)
    return delta, m, v


def _adamw_landed(w, own, land, m, v, *, name, row0=0, prev=None):
    R, C = w.shape
    n_parts, rows = land.shape[0], land.shape[1]
    br = min(rows, 256)
    blk0 = row0 // br
    tile = pl.BlockSpec((br, C), lambda i: (i + blk0, 0))
    part = pl.BlockSpec((br, C), lambda i: (i, 0))
    n_prev = 0 if prev is None else 4

    def body(w_ref, o_ref, l_ref, m_ref, v_ref, *rest):
        g_ref, d_ref, nm_ref, nv_ref = rest[n_prev:]
        g = o_ref[...]
        for p in range(n_parts):
            g = g + l_ref[p].astype(F32)
        d, nm, nv = _adamw_math(w_ref[...], g, m_ref[...], v_ref[...])
        g_ref[...] = g
        d_ref[...] = d
        nm_ref[...] = nm
        nv_ref[...] = nv

    return pl.pallas_call(
        body, name=name, grid=(rows // br,),
        in_specs=[tile, part, pl.BlockSpec((n_parts, br, C), lambda i: (0, i, 0)), tile, tile]
        + [pl.BlockSpec(memory_space=pl.ANY)] * n_prev,
        out_specs=[tile] * 4, out_shape=[jax.ShapeDtypeStruct((R, C), F32)] * 4,
        input_output_aliases={5 + j: j for j in range(n_prev)},
        compiler_params=_cparams(("parallel",)),
    )(w, own, land, m, v, *(prev or ()))


def _adamw_plain(w, g, m, v, *, name):
    def body(w_ref, g_ref, m_ref, v_ref, d_ref, nm_ref, nv_ref):
        d, nm, nv = _adamw_math(w_ref[...], g_ref[...], m_ref[...], v_ref[...])
        d_ref[...] = d
        nm_ref[...] = nm
        nv_ref[...] = nv

    return pl.pallas_call(
        body, name=name, out_shape=[jax.ShapeDtypeStruct(w.shape, F32)] * 3, compiler_params=_cparams(),
    )(w, g, m, v)


def _my_pos():
    return lax.axis_index("x"), lax.axis_index("y"), lax.axis_index("c")


def _dev_index(p):
    return 4 * p[0] + 2 * p[1] + p[2]


def _all_gather(shards):
    n = len(shards)
    hbm = pl.BlockSpec(memory_space=pl.ANY)

    def body(*refs):
        ins, outs = refs[:n], refs[n:2 * n]
        send_sems, recv_sems, local_sems = refs[2 * n:]
        x, y, c = _my_pos()
        me, sibling = (x, y, c), (x, y, 1 - c)
        chips = [(1 - x, y), (x, 1 - y), (1 - x, 1 - y)]

        def copy(a, k, block, to, src=None):
            dst = outs[a].at[_dev_index(block)]
            return pltpu.make_async_remote_copy(
                src_ref=dst if src is None else src, dst_ref=dst,
                send_sem=send_sems.at[a, k], recv_sem=recv_sems.at[a, k], device_id=to, device_id_type=MESH)

        mine = [pltpu.make_async_copy(ins[a], outs[a].at[_dev_index(me)], local_sems.at[a]) for a in range(n)]
        for cp in mine:
            cp.start()
        first = []
        for a in range(n):
            first.append(copy(a, 0, me, sibling, src=ins[a]))
            first += [copy(a, 1 + j, me, (*chip, c), src=ins[a]) for j, chip in enumerate(chips)]
        for cp in first:
            cp.start()
        passed = []
        for j, chip in enumerate(chips):
            for a in range(n):
                copy(a, 1 + j, (*chip, c), me).wait_recv()
                fwd = copy(a, 4 + j, (*chip, c), sibling)
                fwd.start()
                passed.append(fwd)
        for a in range(n):
            copy(a, 0, sibling, me).wait_recv()
        for j, chip in enumerate(chips):
            for a in range(n):
                copy(a, 4 + j, (*chip, 1 - c), me).wait_recv()
        for cp in first + passed:
            cp.wait_send()
        for cp in mine:
            cp.wait()

    return pl.pallas_call(
        body, name="all_gather_weights",
        in_specs=[hbm] * n, out_specs=[hbm] * n,
        out_shape=[jax.ShapeDtypeStruct((N_DEV,) + s.shape, s.dtype) for s in shards],
        scratch_shapes=[pltpu.SemaphoreType.DMA((n, 7)), pltpu.SemaphoreType.DMA((n, 7)), pltpu.SemaphoreType.DMA((n,))],
        compiler_params=pltpu.CompilerParams(has_side_effects=True),
    )(*shards)


def _peers(me):
    x, y, c = me
    out = []
    for k in range(1, 8):
        fx, fy, fc = (k >> 2) & 1, (k >> 1) & 1, k & 1
        out.append((k - 1, (1 - x if fx else x, 1 - y if fy else y, 1 - c if fc else c)))
    return out


HBM_SPEC = pl.BlockSpec(memory_space=pltpu.HBM)
SEM_SPEC = pl.BlockSpec(memory_space=pltpu.SEMAPHORE)
DATAFLOW_EFFECT = pltpu.SideEffectType.DATAFLOW_SIDE_EFFECTING


def _gather_refs(src, land, me, peer, k):
    return src, land.at[_dev_index(me)]


def _scatter_refs(src, land, me, peer, k):
    return src.at[_dev_index(peer)], land.at[k]


def _push_start(srcs, land_shapes, refs_of, name):
    n = len(srcs)

    def body(*refs):
        ins, lands = refs[:n], refs[n:2 * n]
        send_sems, recv_sems, token = refs[2 * n], refs[2 * n + 1], refs[-1]
        me = _my_pos()
        for k, peer in _peers(me):
            for a in range(n):
                src, dst = refs_of(ins[a], lands[a], me, peer, k)
                pltpu.make_async_remote_copy(src_ref=src, dst_ref=dst, send_sem=send_sems.at[7 * a + k],
                                             recv_sem=recv_sems.at[7 * a + k], device_id=peer, device_id_type=MESH).start()
        token[...] = jnp.zeros_like(token)

    lands = [lax.empty(shp, s.dtype) for shp, s in zip(land_shapes, srcs)]
    hbm = lambda a: pltpu.with_memory_space_constraint(a, pltpu.HBM)
    res = pl.pallas_call(
        body, name=name,
        out_shape=(pltpu.SemaphoreType.DMA((7 * n,)), pltpu.SemaphoreType.DMA((7 * n,)),
                   *[pltpu.HBM(s.shape, s.dtype) for s in srcs], *[pltpu.HBM(l.shape, l.dtype) for l in lands],
                   jax.ShapeDtypeStruct((8, 128), F32)),
        in_specs=[HBM_SPEC] * (2 * n),
        out_specs=(SEM_SPEC, SEM_SPEC, *[HBM_SPEC] * (2 * n), pl.BlockSpec(memory_space=pltpu.VMEM)),
        input_output_aliases={i: 2 + i for i in range(2 * n)},
        compiler_params=pltpu.CompilerParams(has_side_effects=DATAFLOW_EFFECT),
    )(*[hbm(s) for s in srcs], *[hbm(l) for l in lands])
    return dict(sems=(res[0], res[1]), srcs=list(res[2:2 + n]), lands=list(res[2 + n:2 + 2 * n]), token=res[-1], n=n,
                refs_of=refs_of, name=name)


def _push_wait(started, after):
    n, refs_of = started["n"], started["refs_of"]

    def body(*refs):
        ins, lands = refs[:n], refs[n:2 * n]
        send_sems, recv_sems = refs[2 * n], refs[2 * n + 1]
        me = _my_pos()
        for k, peer in _peers(me):
            for a in range(n):
                src, dst = refs_of(ins[a], lands[a], me, peer, k)
                cp = pltpu.make_async_remote_copy(src_ref=src, dst_ref=dst, send_sem=send_sems.at[7 * a + k],
                                                  recv_sem=recv_sems.at[7 * a + k], device_id=peer, device_id_type=MESH)
                cp.wait_send()
                cp.wait_recv()

    arrs = started["srcs"] + started["lands"]
    res = pl.pallas_call(
        body, name=started["name"].replace("start", "wait"),
        out_shape=tuple(pltpu.HBM(a.shape, a.dtype) for a in arrs),
        in_specs=[HBM_SPEC] * (2 * n) + [SEM_SPEC, SEM_SPEC, pl.BlockSpec(memory_space=pl.ANY)],
        out_specs=tuple([HBM_SPEC] * (2 * n)),
        input_output_aliases={i: i for i in range(2 * n)},
        compiler_params=pltpu.CompilerParams(has_side_effects=DATAFLOW_EFFECT),
    )(*arrs, *started["sems"], after)
    return list(res[n:2 * n])


def _sum_slots(slots):
    def body(in_ref, out_ref):
        acc = in_ref[0]
        for d in range(1, N_DEV):
            acc = acc + in_ref[d]
        out_ref[...] = acc

    return pl.pallas_call(body, name="sum_small", out_shape=jax.ShapeDtypeStruct(slots.shape[1:], F32),
                          compiler_params=_cparams())(slots)


def _local_step(x, mem, tgt, W, P, late_weights, send_grads, reduce_small, tie0):
    S = x.shape[0]
    W = dict(W)
    h = _rmsnorm_fwd(x, P["g_mix"] + tie0, rows=S, name="norm_mix")
    proj = _matmul(h, W["w_in"], M=S, N=D_IN, K=D_MODEL, mode="nn", bm=512, bn=D_IN // 2, bk=D_MODEL, name="mm_in",
                   j_outer=True)

    wa_bd, wx_bd = _mx(_block_diag(P["w_rg_a"])), _mx(_block_diag(P["w_rg_x"]))
    lru_args = (W["conv_w"], P["conv_b"].reshape(1, -1), wa_bd, wx_bd, P["b_rg_a"].reshape(1, -1),
                P["b_rg_x"].reshape(1, -1), P["lru_lambda"].reshape(1, -1))
    hl, z_lru = _lru_fwd(proj, *lru_args, S=S)

    buckets = _dil_buckets()
    bias = _dil_bias(P["rel_bias"], buckets)
    group_out = [_dilated_fwd(proj, bias, g, S=S) for g in range(len(DIL_GROUPS))]
    o_dil, o_dil_m, lse_dil = _dilated_merge([o for o, _ in group_out], [l for _, l in group_out], S=S)

    W.update(late_weights(o_dil))
    mem_n = _rmsnorm_fwd(mem, P["g_mem"], rows=N_MEM, name="norm_mem")
    kv = _matmul(mem_n, W["w_mem_kv"], M=N_MEM, N=2 * MEM_WIDTH, K=D_MODEL, mode="nn", bm=N_MEM, bn=512, bk=D_MODEL,
                 name="mm_kv")
    om, om_m, lse_mem = _mem_attn_fwd(proj, kv, S=S)
    b_gate = P["b_gate"].reshape(1, -1)
    merged = _mix_fwd(z_lru, o_dil_m, om_m, W["w_lru_out"], W["w_dil_out"], W["w_mem_out"], proj, b_gate, S=S)
    x1 = _matmul(merged, W["w_out"], M=S, N=D_MODEL, K=D_MODEL, mode="nn", bm=512, bn=D_MODEL, bk=D_MODEL, name="mm_out",
                 epilogue=lambda acc, r: (r + acc,), extras=[(x, (0, 0))])
    hm = _rmsnorm_fwd(x1, P["g_mlp"], rows=S, name="norm_mlp")

    def relu2(acc):
        rl = jnp.maximum(acc, 0.0)
        return (rl * rl,)

    act = _matmul(hm, W["w_mlp_in"], M=S, N=D_FF, K=D_MODEL, mode="nn", bm=1024, bn=1024, bk=D_MODEL, name="mm_mlp_in",
                  out_dtypes=(MXU_DTYPE,), epilogue=relu2, j_outer=True)
    x2 = _matmul(act, W["w_mlp_out"], M=S, N=D_MODEL, K=D_FF, mode="nn", bm=512, bn=D_MODEL, bk=D_FF, name="mm_mlp_out",
                 epilogue=lambda acc, r: (r + acc,), extras=[(x1, (0, 0))])
    loss, dx2, dx2_m, dg_final = _loss_head(x2, P["g_final"], tgt, rows=S)

    G, Gs = {}, {}
    Gs["g_final"] = dg_final
    G["w_mlp_out"] = _matmul(act, dx2_m, M=D_FF, N=D_MODEL, K=S, mode="tn", bm=512, bn=D_MODEL, bk=S, name="mm_dw_mlp_out")
    du = _matmul(dx2_m, W["w_mlp_out"], M=S, N=D_FF, K=D_MODEL, mode="nt", bm=1024, bn=1024, bk=D_MODEL, name="mm_du",
                 out_dtypes=(MXU_DTYPE,), epilogue=lambda acc, a: (acc * (2.0 * jnp.sqrt(a.astype(F32))),),
                 extras=[(act, (0, 0))], j_outer=True)
    G["w_mlp_in"] = _matmul(hm, du, M=D_MODEL, N=D_FF, K=S, mode="tn", bm=D_MODEL, bn=512, bk=S, name="mm_dw_mlp_in")
    tie1 = send_grads({n: G.pop(n) for n in ("w_mlp_out", "w_mlp_in")})
    dhm = _matmul(du, W["w_mlp_in"], M=S, N=D_MODEL, K=D_FF, mode="nt", bm=512, bn=D_MODEL, bk=D_FF, name="mm_dhm",
                  deps=[tie1])
    dx1, dx1_m, Gs["g_mlp"] = _rmsnorm_bwd(x1, P["g_mlp"], dhm, dx2, rows=S, name="norm_mlp_bwd",
                                           dx_dtypes=(F32, MXU_DTYPE))
    G["w_out"] = _matmul(merged, dx1_m, M=D_MODEL, N=D_MODEL, K=S, mode="tn", bm=512, bn=D_MODEL, bk=S, name="mm_dw_out")
    dmerged = _matmul(dx1_m, W["w_out"], M=S, N=D_MODEL, K=D_MODEL, mode="nt", bm=512, bn=D_MODEL, bk=D_MODEL, name="mm_dmerged")
    (dg0, dg1, dg2, dy_lru, dy_dil, dy_mem, db0, db1, db2) = _mix_bwd(
        dmerged, z_lru, o_dil_m, om_m, W["w_lru_out"], W["w_dil_out"], W["w_mem_out"], proj, b_gate, S=S)
    Gs["b_gate"] = jnp.concatenate([db0, db1, db2], axis=1)

    G["w_mem_out"] = _matmul(om_m, dy_mem, M=MEM_WIDTH, N=D_MODEL, K=S, mode="tn", bm=MEM_WIDTH, bn=D_MODEL, bk=S,
                             name="mm_dw_mem_out")
    dom = _matmul(dy_mem, W["w_mem_out"], M=S, N=MEM_WIDTH, K=D_MODEL, mode="nt", bm=512, bn=MEM_WIDTH, bk=D_MODEL,
                  name="mm_dom")
    dqm, dk_mem, dv_mem = _mem_attn_bwd(proj, kv, om, lse_mem, dom, S=S)
    dkv = jnp.concatenate([dk_mem, dv_mem], axis=1)
    G["w_mem_kv"] = _matmul(mem_n, dkv, M=D_MODEL, N=2 * MEM_WIDTH, K=N_MEM, mode="tn", bm=D_MODEL, bn=2 * MEM_WIDTH,
                            bk=N_MEM, name="mm_dw_kv")
    dmem_n = _matmul(dkv, W["w_mem_kv"], M=N_MEM, N=D_MODEL, K=2 * MEM_WIDTH, mode="nt", bm=N_MEM, bn=D_MODEL,
                     bk=2 * MEM_WIDTH, name="mm_dmem")
    (Gs["g_mem"],) = _rmsnorm_bwd(mem, P["g_mem"], dmem_n, None, rows=N_MEM, name="norm_mem_bwd", dx_dtypes=())

    G["w_dil_out"] = _matmul(o_dil_m, dy_dil, M=256, N=D_MODEL, K=S, mode="tn", bm=256, bn=D_MODEL, bk=S, name="mm_dw_dil_out")
    do_dil = _matmul(dy_dil, W["w_dil_out"], M=S, N=256, K=D_MODEL, mode="nt", bm=512, bn=256, bk=D_MODEL, name="mm_do_dil")
    G["w_lru_out"] = _matmul(z_lru, dy_lru, M=D_RNN, N=D_MODEL, K=S, mode="tn", bm=D_RNN, bn=D_MODEL, bk=S, name="mm_dw_lru_out")
    dz = _matmul(dy_lru, W["w_lru_out"], M=S, N=D_RNN, K=D_MODEL, mode="nt", bm=512, bn=D_RNN, bk=D_MODEL, name="mm_dz_lru")
    tie2 = send_grads({n: G.pop(n) for n in ("w_out", "w_mem_out", "w_mem_kv", "w_dil_out", "w_lru_out")})
    bias = bias + tie2[0, 0]
    delta = _dilated_delta(do_dil, o_dil, S=S)
    dq_parts, dk_parts, dv_parts, dbias = [], [], [], []
    for g in range(len(DIL_GROUPS)):
        dq_g, dk_g, dv_g, db_g = _dilated_bwd(proj, do_dil, lse_dil, delta, bias, g, S=S)
        dq_parts.append(dq_g)
        dk_parts.append(dk_g)
        dv_parts.append(dv_g)
        dbias.append(db_g)
    drel = _dil_bias_bwd(jnp.stack(dbias, axis=0), buckets)
    Gs["rel_bias"] = drel[:, :3 * DIL_HEADS]

    dxl, dgl, dcw, dcb, dwa, dwx, dba, dbx, dlam = _lru_bwd(proj, hl, dz, *lru_args, S=S)
    Gs["conv_w"], Gs["conv_b"] = dcw, dcb
    Gs["w_rg_a"], Gs["w_rg_x"] = _block_diag_extract(dwa), _block_diag_extract(dwx)
    Gs["b_rg_a"], Gs["b_rg_x"], Gs["lru_lambda"] = dba, dbx, dlam

    dproj = jnp.concatenate([dxl, dgl] + dq_parts + dk_parts + dv_parts + [dqm, dg0, dg1, dg2], axis=1)
    dh = _matmul(dproj, W["w_in"], M=S, N=D_MODEL, K=D_IN, mode="nt", bm=256, bn=D_MODEL, bk=D_IN, name="mm_dh")
    grad_x, Gs["g_mix"] = _rmsnorm_bwd(x, P["g_mix"], dh, dx1, rows=S, name="norm_mix_bwd")
    tie = reduce_small(Gs, loss)
    rows = D_MODEL // W_IN_PIECES
    for q in range(W_IN_PIECES):
        dw_q = _matmul(h, dproj, M=rows, N=D_IN, K=S, mode="tn", bm=rows, bn=D_IN // 2, bk=1024, name=f"mm_dw_in_{q}",
                       a_off=(0, q), deps=[tie])
        tie = send_grads({f"w_in_{q}": dw_q})
    return grad_x, tie


BIG = ("w_in", "w_lru_out", "w_dil_out", "w_mem_kv", "w_mem_out", "w_out", "w_mlp_in", "w_mlp_out")
W_IN_PIECES = 4
COL_SHARDED = ("w_in", "w_lru_out", "w_dil_out", "w_mem_out", "w_mlp_in") + tuple(f"w_in_{q}" for q in range(W_IN_PIECES))
SMALL = ("g_mix", "b_gate", "conv_b", "w_rg_a", "b_rg_a", "w_rg_x", "b_rg_x", "lru_lambda", "rel_bias", "g_mem",
         "g_mlp", "g_final")
WEIGHTS = ("g_mix", "w_in", "b_gate", "conv_w", "conv_b", "w_rg_a", "b_rg_a", "w_rg_x", "b_rg_x", "lru_lambda",
           "w_lru_out", "rel_bias", "w_dil_out", "g_mem", "w_mem_kv", "w_mem_out", "w_out", "g_mlp", "w_mlp_in",
           "w_mlp_out", "g_final")
SMALL_SHAPES = {"g_mix": (1024,), "b_gate": (3072,), "conv_b": (768,), "w_rg_a": (12, 64, 64), "b_rg_a": (768,),
                "w_rg_x": (12, 64, 64), "b_rg_x": (768,), "lru_lambda": (768,), "rel_bias": (32, 12), "g_mem": (1024,),
                "g_mlp": (1024,), "g_final": (1024,)}


def _gathered_to_full(name, gathered):
    if name in COL_SHARDED:
        n, r, c = gathered.shape
        return gathered.transpose(1, 0, 2).reshape(r, n * c)
    n, r, c = gathered.shape
    return gathered.reshape(n * r, c)


def _full_to_parts(name, full):
    if name in COL_SHARDED:
        r, nc = full.shape
        return _mx(full.reshape(r, N_DEV, nc // N_DEV).transpose(1, 0, 2))
    nr, c = full.shape
    return _mx(full.reshape(N_DEV, nr // N_DEV, c))


def _own_part(name, full, my_idx):
    if name in COL_SHARDED:
        r, nc = full.shape
        return lax.dynamic_slice(full, (0, my_idx * (nc // N_DEV)), (r, nc // N_DEV))
    nr, c = full.shape
    return lax.dynamic_slice(full, (my_idx * (nr // N_DEV), 0), (nr // N_DEV, c))


def _pack(parts):
    flat = jnp.concatenate([p.reshape(-1).astype(F32) for p in parts])
    pad = (-flat.shape[0]) % 1024
    return jnp.pad(flat, (0, pad)).reshape(-1, 128)


def _unpack(pack, shapes):
    flat = pack.reshape(-1)
    out, off = [], 0
    for shp in shapes:
        size = math.prod(shp)
        out.append(flat[off:off + size].reshape(shp))
        off += size
    return out


def kernel(x, mem, g_mix, w_in, b_gate, conv_w, conv_b, w_rg_a, b_rg_a, w_rg_x, b_rg_x, lru_lambda, w_lru_out, rel_bias, w_dil_out, g_mem, w_mem_kv, w_mem_out, w_out, g_mlp, w_mlp_in, w_mlp_out, g_final, loss_target, m_g_mix, m_w_in, m_b_gate, m_conv_w, m_conv_b, m_w_rg_a, m_b_rg_a, m_w_rg_x, m_b_rg_x, m_lru_lambda, m_w_lru_out, m_rel_bias, m_w_dil_out, m_g_mem, m_w_mem_kv, m_w_mem_out, m_w_out, m_g_mlp, m_w_mlp_in, m_w_mlp_out, m_g_final, v_g_mix, v_w_in, v_b_gate, v_conv_w, v_conv_b, v_w_rg_a, v_b_rg_a, v_w_rg_x, v_b_rg_x, v_lru_lambda, v_w_lru_out, v_rel_bias, v_w_dil_out, v_g_mem, v_w_mem_kv, v_w_mem_out, v_w_out, v_g_mlp, v_w_mlp_in, v_w_mlp_out, v_g_final):
    w = dict(g_mix=g_mix, w_in=w_in, b_gate=b_gate, conv_w=conv_w, conv_b=conv_b, w_rg_a=w_rg_a, b_rg_a=b_rg_a,
             w_rg_x=w_rg_x, b_rg_x=b_rg_x, lru_lambda=lru_lambda, w_lru_out=w_lru_out, rel_bias=rel_bias,
             w_dil_out=w_dil_out, g_mem=g_mem, w_mem_kv=w_mem_kv, w_mem_out=w_mem_out, w_out=w_out, g_mlp=g_mlp,
             w_mlp_in=w_mlp_in, w_mlp_out=w_mlp_out, g_final=g_final)
    m = dict(g_mix=m_g_mix, w_in=m_w_in, b_gate=m_b_gate, conv_w=m_conv_w, conv_b=m_conv_b, w_rg_a=m_w_rg_a,
             b_rg_a=m_b_rg_a, w_rg_x=m_w_rg_x, b_rg_x=m_b_rg_x, lru_lambda=m_lru_lambda, w_lru_out=m_w_lru_out,
             rel_bias=m_rel_bias, w_dil_out=m_w_dil_out, g_mem=m_g_mem, w_mem_kv=m_w_mem_kv, w_mem_out=m_w_mem_out,
             w_out=m_w_out, g_mlp=m_g_mlp, w_mlp_in=m_w_mlp_in, w_mlp_out=m_w_mlp_out, g_final=m_g_final)
    v = dict(g_mix=v_g_mix, w_in=v_w_in, b_gate=v_b_gate, conv_w=v_conv_w, conv_b=v_conv_b, w_rg_a=v_w_rg_a,
             b_rg_a=v_b_rg_a, w_rg_x=v_w_rg_x, b_rg_x=v_b_rg_x, lru_lambda=v_lru_lambda, w_lru_out=v_w_lru_out,
             rel_bias=v_rel_bias, w_dil_out=v_w_dil_out, g_mem=v_g_mem, w_mem_kv=v_w_mem_kv, w_mem_out=v_w_mem_out,
             w_out=v_w_out, g_mlp=v_g_mlp, w_mlp_in=v_w_mlp_in, w_mlp_out=v_w_mlp_out, g_final=v_g_final)

    my_idx = _dev_index(_my_pos())

    g_in, g_cw = _all_gather([_mx(w["w_in"]), w["conv_w"]])
    W = {"w_in": _gathered_to_full("w_in", g_in), "conv_w": g_cw.transpose(1, 0, 2).reshape(CONV_WIDTH, D_RNN)}
    late = [n for n in BIG if n != "w_in"]
    late_shards = [_mx(w[n]) for n in late]
    late_started = _push_start(late_shards, [(N_DEV,) + s.shape for s in late_shards], _gather_refs,
                                "gather_late_start")
    P = {n: w[n] for n in SMALL}

    def late_weights(after):
        lands = _push_wait(late_started, after)
        out = {}
        for n, land, own in zip(late, lands, late_shards):
            full = lax.dynamic_update_index_in_dim(land, own, my_idx, 0)
            out[n] = _gathered_to_full(n, full)
        return out

    sent, small = [], {}
    small_names = SMALL + ("conv_w",)

    def send_grads(gs):
        names = list(gs)
        parts = [_full_to_parts(n, gs[n]) for n in names]
        own = [_own_part(n, gs[n], my_idx) for n in names]
        started = _push_start(parts, [(N_DEV - 1,) + p.shape[1:] for p in parts], _scatter_refs,
                              f"scatter{len(sent)}_start")
        sent.append((names, own, started))
        return started["token"]

    def reduce_small(gs, loss):
        pack = _pack([gs[n] for n in small_names] + [loss])
        small["pack"] = pack
        small["started"] = _push_start([pack], [(N_DEV,) + pack.shape], _gather_refs, "small_start")
        return small["started"]["token"]

    grad_x, last_token = _local_step(x[0], mem[0], loss_target[0], W, P, late_weights, send_grads, reduce_small,
                                     late_started["token"][0, 0])
    (small_land,) = _push_wait(small["started"], last_token)
    total = _sum_slots(lax.dynamic_update_index_in_dim(small_land, small["pack"], my_idx, 0))
    small_shapes = [SMALL_SHAPES[n] for n in SMALL] + [(CONV_WIDTH, D_RNN), (1,)]
    summed = dict(zip(small_names + ("loss",), _unpack(total, small_shapes)))

    grads, deltas, new_m, new_v = {}, {}, {}, {}
    after = total
    for names, own, started in sent[:-W_IN_PIECES]:
        for n, o, land in zip(names, own, _push_wait(started, after)):
            grads[n], deltas[n], new_m[n], new_v[n] = _adamw_landed(w[n], o, land, m[n], v[n], name=f"adamw_{n}")
            after = deltas[n]
    prev = None
    for q, (names, own, started) in enumerate(sent[-W_IN_PIECES:]):
        (land,) = _push_wait(started, after)
        prev = _adamw_landed(w["w_in"], own[0], land, m["w_in"], v["w_in"], name=f"adamw_{names[0]}",
                             row0=q * (D_MODEL // W_IN_PIECES), prev=prev)
    grads["w_in"], deltas["w_in"], new_m["w_in"], new_v["w_in"] = prev
    zeros_cw = jnp.zeros((CONV_WIDTH, D_RNN), F32)
    w_pack = _pack([w[n] for n in SMALL] + [zeros_cw, jnp.zeros((1,), F32)])
    m_pack = _pack([m[n] for n in SMALL] + [zeros_cw, jnp.zeros((1,), F32)])
    v_pack = _pack([v[n] for n in SMALL] + [zeros_cw, jnp.zeros((1,), F32)])
    d_pack, nm_pack, nv_pack = _adamw_plain(w_pack, total, m_pack, v_pack, name="adamw_small")
    for dst, pk in ((deltas, d_pack), (new_m, nm_pack), (new_v, nv_pack)):
        dst.update(zip(SMALL, _unpack(pk, [SMALL_SHAPES[n] for n in SMALL])))
    for n in SMALL:
        grads[n] = summed[n]
    cw_cols = D_RNN // N_DEV
    grads["conv_w"] = lax.dynamic_slice(summed["conv_w"], (0, my_idx * cw_cols), (CONV_WIDTH, cw_cols))
    deltas["conv_w"], new_m["conv_w"], new_v["conv_w"] = _adamw_plain(
        w["conv_w"], grads["conv_w"], m["conv_w"], v["conv_w"], name="adamw_conv_w")

    return (summed["loss"].reshape(()), grad_x[None], *[grads[n] for n in WEIGHTS], *[deltas[n] for n in WEIGHTS],
            *[new_m[n] for n in WEIGHTS], *[new_v[n] for n in WEIGHTS])
```

```python
import functools
import math

import jax
import jax.numpy as jnp
from jax import lax
from jax.experimental import pallas as pl
from jax.experimental.pallas import tpu as pltpu

F32 = jnp.float32
MXU_DTYPE = jnp.bfloat16
VMEM_LIMIT_BYTES = 56 * 1024 * 1024
N_DEV = 8

D_MODEL = 1024
N_MEM = 256
MEM_HEADS = 4
MEM_HEAD_DIM = 128
MEM_WIDTH = 512
D_RNN = 768
LRU_BLOCK = 64
N_LRU_BLOCKS = 12
LRU_GROUP = 256
N_LRU_GROUPS = 3
CONV_WIDTH = 4
LRU_C = 8.0
DIL_GROUPS = ((128, 1), (512, 4), (2048, 16))
SPAN = 128
DIL_HEADS = 4
DIL_HEAD_DIM = 64
NUM_BUCKETS = 32
MAX_DISTANCE = 2048
D_FF = 4096
D_IN = 7424
EPS = 1e-6
NEG = -1e30
C_XL, C_GATE, C_QKV, C_QM, C_GATES = 0, 768, 1536, 3840, 4352

ADAM_LR = 0.001
ADAM_B1 = 0.9
ADAM_B2 = 0.999
ADAM_EPS = 1e-08
ADAM_WD = 0.01
ADAM_STEP = 10

MESH = pl.DeviceIdType.MESH
GELU_K = math.sqrt(2.0 / math.pi)


def _cparams(sem=None):
    kw = dict(vmem_limit_bytes=VMEM_LIMIT_BYTES)
    if sem is not None:
        kw["dimension_semantics"] = sem
    return pltpu.CompilerParams(**kw)


def _mx(v):
    return v.astype(MXU_DTYPE)


def _dot(a, b, mode="nn"):
    dims = {"nn": (((1,), (0,)), ((), ())), "nt": (((1,), (1,)), ((), ())), "tn": (((0,), (0,)), ((), ()))}[mode]
    return lax.dot_general(_mx(a), _mx(b), dims, preferred_element_type=F32)


def _colsum(v):
    return jnp.sum(v, axis=0, keepdims=True)


def _matmul(a, b, *, M, N, K, mode, bm, bn, bk, name, out_dtypes=(F32,), epilogue=None, extras=(),
            a_off=(0, 0), b_off=(0, 0), j_outer=False, deps=(), parts=None):
    assert M % bm == 0 and N % bn == 0 and K % bk == 0, (name, M, N, K, bm, bn, bk)
    nm, nn, nk = M // bm, N // bn, K // bk

    def ij(f):
        if j_outer:
            return lambda j, i, k: f(i, j, k)
        return f

    if mode == "tn":
        a_spec = pl.BlockSpec((bk, bm), ij(lambda i, j, k: (k + a_off[0], i + a_off[1])))
    else:
        a_spec = pl.BlockSpec((bm, bk), ij(lambda i, j, k: (i + a_off[0], k + a_off[1])))
    if mode == "nt":
        b_spec = pl.BlockSpec((bn, bk), ij(lambda i, j, k: (j + b_off[0], k + b_off[1])))
    else:
        b_spec = pl.BlockSpec((bk, bn), ij(lambda i, j, k: (k + b_off[0], j + b_off[1])))
    ex_specs = [pl.BlockSpec((bm, bn), ij(functools.partial(lambda i, j, k, o: (i + o[0], j + o[1]), o=off)))
                for _, off in extras]
    if parts is None:
        out_dims = (M, N)
        out_spec = pl.BlockSpec((bm, bn), ij(lambda i, j, k: (i, j)))
    elif parts[0] == "rows":
        r = parts[1]
        assert bm % r == 0
        out_dims = (M // r, r, N)
        out_spec = pl.BlockSpec((bm // r, r, bn), ij(lambda i, j, k: (i, 0, j)))
    else:
        c = parts[1]
        assert bn % c == 0
        out_dims = (N // c, M, c)
        out_spec = pl.BlockSpec((bn // c, bm, c), ij(lambda i, j, k: (j, i, 0)))
    n_ex, n_out, n_dep = len(extras), len(out_dtypes), len(deps)

    def body(*refs):
        a_ref, b_ref = refs[0], refs[1]
        ex = refs[2:2 + n_ex]
        outs = refs[2 + n_ex + n_dep:2 + n_ex + n_dep + n_out]
        part = _dot(a_ref[...], b_ref[...], mode)

        def finish(acc):
            vals = epilogue(acc, *[e[...] for e in ex]) if epilogue is not None else (acc,)
            for o, v in zip(outs, vals):
                v = v.astype(o.dtype)
                if parts is None:
                    o[...] = v
                elif parts[0] == "rows":
                    for ch in range(bm // parts[1]):
                        o[ch] = v[ch * parts[1]:(ch + 1) * parts[1], :]
                else:
                    for ch in range(bn // parts[1]):
                        o[ch] = v[:, ch * parts[1]:(ch + 1) * parts[1]]

        if nk == 1:
            finish(part)
        else:
            acc_ref = refs[-1]
            k = pl.program_id(2)

            @pl.when(k == 0)
            def _():
                acc_ref[...] = part

            @pl.when(k > 0)
            def _():
                acc_ref[...] += part

            @pl.when(k == nk - 1)
            def _():
                finish(acc_ref[...])

    grid = (nn, nm, nk) if j_outer else (nm, nn, nk)
    res = pl.pallas_call(
        body, name=name, grid=grid,
        in_specs=[a_spec, b_spec] + ex_specs + [pl.BlockSpec(memory_space=pl.ANY)] * n_dep,
        out_specs=[out_spec] * n_out,
        out_shape=[jax.ShapeDtypeStruct(out_dims, dt) for dt in out_dtypes],
        scratch_shapes=[pltpu.VMEM((bm, bn), F32)] if nk > 1 else [],
        compiler_params=_cparams(("parallel", "parallel", "arbitrary")),
    )(a, b, *[e for e, _ in extras], *deps)
    return res[0] if n_out == 1 else res


def _rmsnorm_fwd(x, g, *, rows, name, bt=512):
    bt = min(bt, rows)

    def body(x_ref, g_ref, o_ref):
        xv = x_ref[...]
        r = lax.rsqrt(jnp.mean(xv * xv, axis=-1, keepdims=True) + EPS)
        o_ref[...] = (xv * r * g_ref[...]).astype(o_ref.dtype)

    return pl.pallas_call(
        body, name=name, grid=(rows // bt,),
        in_specs=[pl.BlockSpec((bt, D_MODEL), lambda i: (i, 0)), pl.BlockSpec((1, D_MODEL), lambda i: (0, 0))],
        out_specs=pl.BlockSpec((bt, D_MODEL), lambda i: (i, 0)),
        out_shape=jax.ShapeDtypeStruct((rows, D_MODEL), MXU_DTYPE),
        compiler_params=_cparams(("parallel",)),
    )(x, g.reshape(1, D_MODEL))


def _rms_bwd_tile(xv, gv, dyv):
    r = lax.rsqrt(jnp.mean(xv * xv, axis=-1, keepdims=True) + EPS)
    w = dyv * gv
    dx = r * w - xv * (r * r * r) * jnp.mean(w * xv, axis=-1, keepdims=True)
    dg = _colsum(dyv * xv * r)
    return dx, dg


def _rmsnorm_bwd(x, g, dy, res, *, rows, name, bt=512, dx_dtypes=(F32,)):
    bt = min(bt, rows)
    has_res = res is not None

    def body(*refs):
        x_ref, g_ref, dy_ref = refs[:3]
        res_ref = refs[3] if has_res else None
        outs = refs[3 + int(has_res):]
        dx, dg = _rms_bwd_tile(x_ref[...], g_ref[...], dy_ref[...])
        if has_res:
            dx = dx + res_ref[...]
        dg_ref = outs[-1]

        @pl.when(pl.program_id(0) == 0)
        def _():
            dg_ref[...] = jnp.zeros_like(dg_ref)

        dg_ref[...] += dg
        for o in outs[:-1]:
            o[...] = dx.astype(o.dtype)

    row_spec = pl.BlockSpec((bt, D_MODEL), lambda i: (i, 0))
    vec_spec = pl.BlockSpec((1, D_MODEL), lambda i: (0, 0))
    ins = [x, g.reshape(1, D_MODEL), dy] + ([res] if has_res else [])
    return pl.pallas_call(
        body, name=name, grid=(rows // bt,),
        in_specs=[row_spec, vec_spec, row_spec] + ([row_spec] if has_res else []),
        out_specs=[row_spec] * len(dx_dtypes) + [vec_spec],
        out_shape=[jax.ShapeDtypeStruct((rows, D_MODEL), dt) for dt in dx_dtypes] + [jax.ShapeDtypeStruct((1, D_MODEL), F32)],
        compiler_params=_cparams(("arbitrary",)),
    )(*ins)


def _loss_head(x2, g, tgt, *, rows, bt=512):
    def body(x_ref, g_ref, t_ref, loss_ref, dx_ref, dxm_ref, dg_ref):
        xv, gv = x_ref[...], g_ref[...]
        r = lax.rsqrt(jnp.mean(xv * xv, axis=-1, keepdims=True) + EPS)
        diff = xv * r * gv - t_ref[...]
        part = jnp.sum(jnp.mean(diff * diff, axis=-1, keepdims=True), axis=0, keepdims=True) * 0.5
        dx, dg = _rms_bwd_tile(xv, gv, diff * (1.0 / D_MODEL))

        @pl.when(pl.program_id(0) == 0)
        def _():
            loss_ref[...] = jnp.zeros_like(loss_ref)
            dg_ref[...] = jnp.zeros_like(dg_ref)

        loss_ref[...] += part
        dg_ref[...] += dg
        dx_ref[...] = dx
        dxm_ref[...] = _mx(dx)

    row_spec = pl.BlockSpec((bt, D_MODEL), lambda i: (i, 0))
    vec_spec = pl.BlockSpec((1, D_MODEL), lambda i: (0, 0))
    return pl.pallas_call(
        body, name="loss_head", grid=(rows // bt,),
        in_specs=[row_spec, vec_spec, row_spec],
        out_specs=[pl.BlockSpec((1, 1), lambda i: (0, 0)), row_spec, row_spec, vec_spec],
        out_shape=[jax.ShapeDtypeStruct((1, 1), F32), jax.ShapeDtypeStruct((rows, D_MODEL), F32),
                   jax.ShapeDtypeStruct((rows, D_MODEL), MXU_DTYPE), jax.ShapeDtypeStruct((1, D_MODEL), F32)],
        compiler_params=_cparams(("arbitrary",)),
    )(x2, g.reshape(1, D_MODEL), tgt)


LRU_T = 256


def _gelu(x):
    t = jnp.tanh(GELU_K * (x + 0.044715 * x * x * x))
    return 0.5 * x * (1.0 + t), t


def _gelu_grad(x, t):
    return 0.5 * (1.0 + t) + 0.5 * x * (1.0 - t * t) * GELU_K * (1.0 + 3.0 * 0.044715 * x * x)


def _softplus_neg(lam):
    z = -lam
    u = jnp.exp(-jnp.abs(z))
    w = 1.0 + u
    l1p = jnp.where(w == 1.0, u, jnp.log(w) * u / jnp.where(w == 1.0, 1.0, w - 1.0))
    return jnp.maximum(z, 0.0) + l1p


def _shift_down(cur, prev8, k, row8):
    y = pltpu.roll(cur, k, 0)
    head = jnp.where(row8 < k, pltpu.roll(prev8, k, 0), y[0:8])
    return jnp.concatenate([head, y[8:]], axis=0)


def _shift_up(cur, next8, k, row8):
    n = cur.shape[0]
    y = pltpu.roll(cur, n - k, 0)
    tail = jnp.where(row8 >= 8 - k, pltpu.roll(next8, 8 - k, 0), y[n - 8:n])
    return jnp.concatenate([y[0:n - 8], tail], axis=0)


def _lru_gates(xl, p8, cw, cb, wa, wx, ba, bx, lam, row8):
    sh = [xl] + [_shift_down(xl, p8, k, row8) for k in (1, 2, 3)]
    xc = cb + cw[3:4] * sh[0] + cw[2:3] * sh[1] + cw[1:2] * sh[2] + cw[0:1] * sh[3]
    r = jax.nn.sigmoid(_dot(xc, wa) + ba)
    i = jax.nn.sigmoid(_dot(xc, wx) + bx)
    sp = _softplus_neg(lam)
    la = -LRU_C * r * sp
    a = jnp.exp(la)
    mult = jnp.sqrt(jnp.tanh(-la) * (a * a + 1.0))
    return dict(sh=sh, xc=xc, r=r, i=i, sp=sp, a=a, mult=mult)


def _lru_specs(n_t, reverse):
    T = LRU_T
    tt = (lambda t: n_t - 1 - t) if reverse else (lambda t: t)
    blk = lambda col0: pl.BlockSpec((T, LRU_GROUP), lambda g, t: (tt(t), col0 + g))
    prev8 = lambda col0: pl.BlockSpec((8, LRU_GROUP), lambda g, t: (jnp.maximum(tt(t) * (T // 8) - 1, 0), col0 + g))
    vec = lambda rows: pl.BlockSpec((rows, LRU_GROUP), lambda g, t: (0, g))
    wbd = pl.BlockSpec((1, LRU_GROUP, LRU_GROUP), lambda g, t: (g, 0, 0))
    return blk, prev8, vec, wbd


def _lru_fwd(proj, conv_w, conv_b, wa_bd, wx_bd, b_a, b_x, lam, *, S):
    T = LRU_T
    n_t = S // T
    blk, _, vec, wbd = _lru_specs(n_t, False)

    def body(xl_ref, gate_ref, cw_ref, cb_ref, wa_ref, wx_ref, ba_ref, bx_ref, lam_ref,
             hl_ref, z_ref, prev8, hcar, a_s, b_s):
        @pl.when(pl.program_id(1) == 0)
        def _():
            prev8[...] = jnp.zeros_like(prev8)
            hcar[...] = jnp.zeros_like(hcar)

        row8 = lax.broadcasted_iota(jnp.int32, (8, LRU_GROUP), 0)
        xl = xl_ref[...]
        q = _lru_gates(xl, prev8[...], cw_ref[...], cb_ref[...], wa_ref[0], wx_ref[0], ba_ref[...], bx_ref[...],
                       lam_ref[...], row8)
        prev8[...] = xl[T - 8:T]
        a_s[...] = q["a"]
        b_s[...] = q["mult"] * q["i"] * q["xc"]

        def step(c, carry):
            off = pl.multiple_of(c * 8, 8)
            A = a_s[pl.ds(off, 8), :]
            B = b_s[pl.ds(off, 8), :]
            for k in (1, 2, 4):
                a_sh = jnp.where(row8 >= k, pltpu.roll(A, k, 0), 1.0)
                b_sh = jnp.where(row8 >= k, pltpu.roll(B, k, 0), 0.0)
                B = A * b_sh + B
                A = A * a_sh
            h = A * carry + B
            hl_ref[pl.ds(off, 8), :] = h
            return h[7:8, :]

        hcar[...] = lax.fori_loop(0, T // 8, step, hcar[...])
        ge, _ = _gelu(gate_ref[...])
        z_ref[...] = (ge * hl_ref[...]).astype(z_ref.dtype)

    return pl.pallas_call(
        body, name="lru_fwd", grid=(N_LRU_GROUPS, n_t),
        in_specs=[blk(C_XL // LRU_GROUP), blk(C_GATE // LRU_GROUP), vec(4), vec(1), wbd, wbd, vec(1), vec(1), vec(1)],
        out_specs=[blk(0), blk(0)],
        out_shape=[jax.ShapeDtypeStruct((S, D_RNN), F32), jax.ShapeDtypeStruct((S, D_RNN), MXU_DTYPE)],
        scratch_shapes=[pltpu.VMEM((8, LRU_GROUP), F32), pltpu.VMEM((1, LRU_GROUP), F32),
                        pltpu.VMEM((T, LRU_GROUP), F32), pltpu.VMEM((T, LRU_GROUP), F32)],
        compiler_params=_cparams(("parallel", "arbitrary")),
    )(proj, proj, conv_w, conv_b, wa_bd, wx_bd, b_a, b_x, lam)


def _lru_bwd(proj, hl, dz, conv_w, conv_b, wa_bd, wx_bd, b_a, b_x, lam, *, S):
    T = LRU_T
    n_t = S // T
    blk, prev8s, vec, wbd = _lru_specs(n_t, True)

    def body(xl_ref, xlp_ref, gate_ref, hl_ref, hlp_ref, dz_ref, cw_ref, cb_ref, wa_ref, wx_ref, ba_ref, bx_ref,
             lam_ref, dxl_ref, dgate_ref, dcw_ref, dcb_ref, dwa_ref, dwx_ref, dba_ref, dbx_ref, dlam_ref,
             next8, gcar, c_s, b_s, l_s):
        t = pl.program_id(1)
        first_chunk = t == n_t - 1

        @pl.when(t == 0)
        def _():
            next8[...] = jnp.zeros_like(next8)
            gcar[...] = jnp.zeros_like(gcar)
            for ref in (dcw_ref, dcb_ref, dwa_ref, dwx_ref, dba_ref, dbx_ref, dlam_ref):
                ref[...] = jnp.zeros_like(ref)

        row8 = lax.broadcasted_iota(jnp.int32, (8, LRU_GROUP), 0)
        rowT = lax.broadcasted_iota(jnp.int32, (T, LRU_GROUP), 0)
        keep = jnp.where(first_chunk, 0.0, 1.0)
        xl = xl_ref[...]
        wa, wx, lam_v = wa_ref[0], wx_ref[0], lam_ref[...]
        q = _lru_gates(xl, xlp_ref[...] * keep, cw_ref[...], cb_ref[...], wa, wx, ba_ref[...], bx_ref[...], lam_v, row8)
        a, mult, r, i, xc, sp = q["a"], q["mult"], q["r"], q["i"], q["xc"], q["sp"]
        hl_v = hl_ref[...]
        dz_v = dz_ref[...]
        gate = gate_ref[...]
        ge, th = _gelu(gate)
        dgate_ref[...] = (dz_v * hl_v * _gelu_grad(gate, th)).astype(dgate_ref.dtype)

        c_s[...] = jnp.where(rowT == T - 1, 0.0, pltpu.roll(a, T - 1, 0))
        b_s[...] = dz_v * ge + jnp.where(rowT == T - 1, gcar[...], 0.0)

        def step(n, carry):
            off = pl.multiple_of((T // 8 - 1 - n) * 8, 8)
            C = c_s[pl.ds(off, 8), :]
            B = b_s[pl.ds(off, 8), :]
            for k in (1, 2, 4):
                c_sh = jnp.where(row8 < 8 - k, pltpu.roll(C, 8 - k, 0), 1.0)
                b_sh = jnp.where(row8 < 8 - k, pltpu.roll(B, 8 - k, 0), 0.0)
                B = B + C * b_sh
                C = C * c_sh
            lam_t = B + C * carry
            l_s[pl.ds(off, 8), :] = lam_t
            return lam_t[0:1, :]

        lax.fori_loop(0, T // 8, step, jnp.zeros((1, LRU_GROUP), F32))
        lmb = l_s[...]
        gcar[...] = a[0:1, :] * lmb[0:1, :]

        h_prev = _shift_down(hl_v, hlp_ref[...] * keep, 1, row8)
        da = lmb * h_prev
        dmult = lmb * i * xc
        di = lmb * mult * xc
        dxc = lmb * mult * i
        dla = da * a - dmult * (a * a) / mult
        dr = dla * (-LRU_C * sp)
        dlam_ref[...] += _colsum(dla * (-LRU_C * r)) * (-jax.nn.sigmoid(-lam_v))
        dpa = dr * r * (1.0 - r)
        dpx = di * i * (1.0 - i)
        dxc = dxc + _dot(dpa, wa, "nt") + _dot(dpx, wx, "nt")
        dwa_ref[0] += _dot(xc, dpa, "tn")
        dwx_ref[0] += _dot(xc, dpx, "tn")
        dba_ref[...] += _colsum(dpa)
        dbx_ref[...] += _colsum(dpx)
        dcb_ref[...] += _colsum(dxc)
        cw = cw_ref[...]
        n8 = next8[...]
        dxl = cw[3:4] * dxc
        for k in (1, 2, 3):
            dxl = dxl + cw[3 - k:4 - k] * _shift_up(dxc, n8, k, row8)
        for k in range(4):
            dcw_ref[3 - k:4 - k, :] += _colsum(dxc * q["sh"][k])
        next8[...] = dxc[0:8]
        dxl_ref[...] = dxl.astype(dxl_ref.dtype)

    res = pl.pallas_call(
        body, name="lru_bwd", grid=(N_LRU_GROUPS, n_t),
        in_specs=[blk(C_XL // LRU_GROUP), prev8s(C_XL // LRU_GROUP), blk(C_GATE // LRU_GROUP), blk(0), prev8s(0), blk(0),
                  vec(4), vec(1), wbd, wbd, vec(1), vec(1), vec(1)],
        out_specs=[blk(0), blk(0), vec(4), vec(1), wbd, wbd, vec(1), vec(1), vec(1)],
        out_shape=[jax.ShapeDtypeStruct((S, D_RNN), MXU_DTYPE), jax.ShapeDtypeStruct((S, D_RNN), MXU_DTYPE),
                   jax.ShapeDtypeStruct((4, D_RNN), F32), jax.ShapeDtypeStruct((1, D_RNN), F32),
                   jax.ShapeDtypeStruct((N_LRU_GROUPS, LRU_GROUP, LRU_GROUP), F32),
                   jax.ShapeDtypeStruct((N_LRU_GROUPS, LRU_GROUP, LRU_GROUP), F32),
                   jax.ShapeDtypeStruct((1, D_RNN), F32), jax.ShapeDtypeStruct((1, D_RNN), F32),
                   jax.ShapeDtypeStruct((1, D_RNN), F32)],
        scratch_shapes=[pltpu.VMEM((8, LRU_GROUP), F32), pltpu.VMEM((1, LRU_GROUP), F32),
                        pltpu.VMEM((T, LRU_GROUP), F32), pltpu.VMEM((T, LRU_GROUP), F32), pltpu.VMEM((T, LRU_GROUP), F32)],
        compiler_params=_cparams(("parallel", "arbitrary")),
    )(proj, proj, proj, hl, hl, dz, conv_w, conv_b, wa_bd, wx_bd, b_a, b_x, lam)
    return res


def _block_diag(w):
    w4 = w.reshape(N_LRU_GROUPS, 4, LRU_BLOCK, 1, LRU_BLOCK)
    eye = jnp.eye(4, dtype=w.dtype).reshape(1, 4, 1, 4, 1)
    return (w4 * eye).reshape(N_LRU_GROUPS, LRU_GROUP, LRU_GROUP)


def _block_diag_extract(wbd):
    w5 = wbd.reshape(N_LRU_GROUPS, 4, LRU_BLOCK, 4, LRU_BLOCK)
    return jnp.stack([w5[:, a, :, a, :] for a in range(4)], axis=1).reshape(N_LRU_BLOCKS, LRU_BLOCK, LRU_BLOCK)


def _t5_bucket(dist):
    max_exact = NUM_BUCKETS // 2
    df = jnp.maximum(dist, 1).astype(jnp.float32)
    large = max_exact + (jnp.log(df / max_exact) / math.log(MAX_DISTANCE / max_exact)
                         * (NUM_BUCKETS - max_exact)).astype(jnp.int32)
    large = jnp.minimum(large, NUM_BUCKETS - 1)
    return jnp.where(dist < max_exact, dist, large)


def _band_offsets():
    qi = jnp.arange(SPAN)[:, None]
    kj = jnp.arange(2 * SPAN)[None, :]
    return qi + SPAN - kj


def _dil_buckets():
    off = _band_offsets()
    return jnp.stack([_t5_bucket(jnp.maximum(off, 0) * dil) for _, dil in DIL_GROUPS]).astype(jnp.int32)


def _dil_bias(rel_bias, buckets):
    def body(tbl_ref, bk_ref, o_ref):
        g = pl.program_id(0)
        qi = lax.broadcasted_iota(jnp.int32, (SPAN, 2 * SPAN), 0)
        kj = lax.broadcasted_iota(jnp.int32, (SPAN, 2 * SPAN), 1)
        off = qi + SPAN - kj
        valid = (off >= 0) & (off <= SPAN)
        bk = bk_ref[0]
        for h in range(DIL_HEADS):
            acc = jnp.zeros((SPAN, 2 * SPAN), F32)
            for b in range(NUM_BUCKETS):
                acc = jnp.where(bk == b, tbl_ref[b, g * DIL_HEADS + h], acc)
            o_ref[0, h] = jnp.where(valid, acc, NEG)

    return pl.pallas_call(
        body, name="dil_bias", grid=(3,),
        in_specs=[pl.BlockSpec(memory_space=pltpu.SMEM), pl.BlockSpec((1, SPAN, 2 * SPAN), lambda g: (g, 0, 0))],
        out_specs=pl.BlockSpec((1, DIL_HEADS, SPAN, 2 * SPAN), lambda g: (g, 0, 0, 0)),
        out_shape=jax.ShapeDtypeStruct((3, DIL_HEADS, SPAN, 2 * SPAN), F32),
        compiler_params=_cparams(("parallel",)),
    )(rel_bias, buckets)


def _dil_bias_bwd(dbias, buckets):
    def body(db_ref, bk_ref, o_ref):
        lane = lax.broadcasted_iota(jnp.int32, (1, 128), 1)
        rows = [jnp.zeros((1, 128), F32) for _ in range(NUM_BUCKETS)]
        for g in range(3):
            bk = bk_ref[g]
            for h in range(DIL_HEADS):
                d = db_ref[g, h]
                for b in range(NUM_BUCKETS):
                    tot = jnp.sum(_colsum(jnp.where(bk == b, d, 0.0)), axis=1, keepdims=True)
                    rows[b] = jnp.where(lane == g * DIL_HEADS + h, tot, rows[b])
        for b in range(NUM_BUCKETS):
            o_ref[b:b + 1, :] = rows[b]

    return pl.pallas_call(
        body, name="dil_bias_bwd",
        out_shape=jax.ShapeDtypeStruct((NUM_BUCKETS, 128), F32),
        compiler_params=_cparams(),
    )(dbias, buckets)


DIL_SUBBLOCKS = (4, 2, 1)


def _dil_layout(g, S):
    dil, m = DIL_GROUPS[g][1], DIL_SUBBLOCKS[g]
    sub = SPAN * dil
    col = [(C_QKV + t * 768 + g * 256) // 128 for t in range(3)]
    return dil, m, sub, S // (sub * m), col


def _residue_rows(b, r, dil):
    return pl.ds(b * SPAN * dil + r, SPAN, stride=dil) if dil > 1 else pl.ds(b * SPAN, SPAN)


def _for_residues(dil, fn):
    if dil <= 4:
        for r in range(dil):
            fn(r)
    else:
        lax.fori_loop(0, dil, lambda r, c: (fn(r), c)[1], 0, unroll=2)


def _pair_scores(qm, k2, bias, first_cols):
    s = _dot(qm, k2, "nt") * (DIL_HEAD_DIM ** -0.5) + bias
    kj = lax.broadcasted_iota(jnp.int32, s.shape, 1)
    return jnp.where(kj < first_cols, NEG, s)


def _dilated_fwd(proj, bias, g, *, S):
    dil, m, sub, nc, (qc, kc, vc) = _dil_layout(g, S)
    R = sub * m
    cur = lambda cb: pl.BlockSpec((R, 128), lambda p, i: (i, cb + p))
    prv = lambda cb: pl.BlockSpec((sub, 128), lambda p, i: (jnp.maximum(i * m - 1, 0), cb + p))
    out = pl.BlockSpec((R, 128), lambda p, i: (i, p))

    def body(q_ref, kp_ref, kc_ref, vp_ref, vc_ref, b_ref, o_ref, lse_ref):
        lane = lax.broadcasted_iota(jnp.int32, (SPAN, 128), 1)
        sels = (lane < DIL_HEAD_DIM, lane >= DIL_HEAD_DIM)
        for b in range(m):
            first_cols = jnp.where(pl.program_id(1) == 0, SPAN, 0) if b == 0 else 0

            def one(r, b=b, first_cols=first_cols):
                rows = _residue_rows(b, r, dil)
                before = (kc_ref, vc_ref, _residue_rows(b - 1, r, dil)) if b else (kp_ref, vp_ref, _residue_rows(0, r, dil))
                q2 = q_ref[rows, :]
                k2 = _mx(jnp.concatenate([before[0][before[2], :], kc_ref[rows, :]], axis=0))
                v2 = _mx(jnp.concatenate([before[1][before[2], :], vc_ref[rows, :]], axis=0))
                stat = jnp.zeros((SPAN, 128), F32)
                o2 = jnp.zeros((SPAN, 128), F32)
                for e in range(2):
                    s = _pair_scores(jnp.where(sels[e], q2, 0.0), k2, b_ref[0, e], first_cols)
                    mx = jnp.max(s, axis=-1, keepdims=True)
                    p = jnp.exp(s - mx)
                    den = jnp.sum(p, axis=-1, keepdims=True)
                    o2 = jnp.where(sels[e], _dot(p, v2) / den, o2)
                    stat = jnp.where(lane == e, mx + jnp.log(den), stat)
                o_ref[rows, :] = o2
                lse_ref[rows, :] = stat

            _for_residues(dil, one)

    return pl.pallas_call(
        body, name=f"dil_fwd{g}", grid=(2, nc),
        in_specs=[cur(qc), prv(kc), cur(kc), prv(vc), cur(vc),
                  pl.BlockSpec((1, 2, SPAN, 2 * SPAN), lambda p, i: (g, p, 0, 0))],
        out_specs=[out, out],
        out_shape=[jax.ShapeDtypeStruct((S, 256), F32), jax.ShapeDtypeStruct((S, 256), F32)],
        compiler_params=_cparams(("parallel", "parallel")),
    )(proj, proj, proj, proj, proj, bias)


def _dilated_bwd(proj, do, lse, delta, bias, g, *, S):
    dil, m, sub, nc, (qc, kc, vc) = _dil_layout(g, S)
    R = sub * m
    cl = lambda i: jnp.minimum(i, nc - 1)
    cur = lambda cb: pl.BlockSpec((R, 128), lambda p, i: (cl(i), cb + p))
    prv = lambda cb: pl.BlockSpec((sub, 128), lambda p, i: (jnp.maximum(cl(i) * m - 1, 0), cb + p))
    kv_out = pl.BlockSpec((R, 128), lambda p, i: (jnp.maximum(i - 1, 0), p))
    scale = DIL_HEAD_DIM ** -0.5

    def body(q_ref, kp_ref, kc_ref, vp_ref, vc_ref, do_ref, lse_ref, dl_ref, b_ref,
             dq_ref, dk_ref, dv_ref, db_ref, dq_s, kc_s, vc_s, kp_s, vp_s, kcar, vcar):
        i = pl.program_id(1)

        @pl.when(i == 0)
        def _():
            kcar[...] = jnp.zeros_like(kcar)
            vcar[...] = jnp.zeros_like(vcar)
            db_ref[...] = jnp.zeros_like(db_ref)

        @pl.when(i < nc)
        def _():
            lane = lax.broadcasted_iota(jnp.int32, (SPAN, 128), 1)
            sels = (lane < DIL_HEAD_DIM, lane >= DIL_HEAD_DIM)
            for b in range(m):
                first_cols = jnp.where(i == 0, SPAN, 0) if b == 0 else 0

                def one(r, b=b, first_cols=first_cols):
                    rows = _residue_rows(b, r, dil)
                    rows_before = _residue_rows(b - 1 if b else 0, r, dil)
                    k_before, v_before = (kc_ref, vc_ref) if b else (kp_ref, vp_ref)
                    q2, do2 = q_ref[rows, :], do_ref[rows, :]
                    k2 = _mx(jnp.concatenate([k_before[rows_before, :], kc_ref[rows, :]], axis=0))
                    v2 = _mx(jnp.concatenate([v_before[rows_before, :], vc_ref[rows, :]], axis=0))
                    lse_t, dl_t = lse_ref[rows, :], dl_ref[rows, :]
                    dq2 = jnp.zeros((SPAN, 128), F32)
                    dk2 = jnp.zeros((2 * SPAN, 128), F32)
                    dv2 = jnp.zeros((2 * SPAN, 128), F32)
                    for e in range(2):
                        qm = jnp.where(sels[e], q2, 0.0)
                        dom = jnp.where(sels[e], do2, 0.0)
                        p = jnp.exp(_pair_scores(qm, k2, b_ref[0, e], first_cols) - lse_t[:, e:e + 1])
                        ds = p * (_dot(dom, v2, "nt") - dl_t[:, e:e + 1])
                        db_ref[e] += ds
                        dq2 = jnp.where(sels[e], _dot(ds, k2) * scale, dq2)
                        dk2 = dk2 + _dot(ds, qm, "tn") * scale
                        dv2 = dv2 + _dot(p, dom, "tn")
                    dq_s[rows, :] = dq2
                    kc_s[rows, :] = dk2[SPAN:2 * SPAN]
                    vc_s[rows, :] = dv2[SPAN:2 * SPAN]
                    if b:
                        kc_s[rows_before, :] += dk2[0:SPAN]
                        vc_s[rows_before, :] += dv2[0:SPAN]
                    else:
                        kp_s[rows_before, :] = dk2[0:SPAN]
                        vp_s[rows_before, :] = dv2[0:SPAN]

                _for_residues(dil, one)
            dq_ref[...] = dq_s[...].astype(dq_ref.dtype)
            last = pl.ds((m - 1) * sub, sub)
            kcar[last, :] += kp_s[...]
            vcar[last, :] += vp_s[...]
            dk_ref[...] = kcar[...].astype(dk_ref.dtype)
            dv_ref[...] = vcar[...].astype(dv_ref.dtype)
            kcar[...] = kc_s[...]
            vcar[...] = vc_s[...]

        @pl.when(i == nc)
        def _():
            dk_ref[...] = kcar[...].astype(dk_ref.dtype)
            dv_ref[...] = vcar[...].astype(dv_ref.dtype)

    stat = pl.BlockSpec((R, 128), lambda p, i: (cl(i), p))
    big = jax.ShapeDtypeStruct((S, 256), MXU_DTYPE)
    return pl.pallas_call(
        body, name=f"dil_bwd{g}", grid=(2, nc + 1),
        in_specs=[cur(qc), prv(kc), cur(kc), prv(vc), cur(vc), stat, stat, stat,
                  pl.BlockSpec((1, 2, SPAN, 2 * SPAN), lambda p, i: (g, p, 0, 0))],
        out_specs=[stat, kv_out, kv_out, pl.BlockSpec((2, SPAN, 2 * SPAN), lambda p, i: (p, 0, 0))],
        out_shape=[big, big, big, jax.ShapeDtypeStruct((DIL_HEADS, SPAN, 2 * SPAN), F32)],
        scratch_shapes=[pltpu.VMEM((R, 128), F32)] * 3 + [pltpu.VMEM((sub, 128), F32)] * 2 + [pltpu.VMEM((R, 128), F32)] * 2,
        compiler_params=_cparams(("parallel", "arbitrary")),
    )(proj, proj, proj, proj, proj, do, lse, delta, bias)


def _dilated_merge(os_, lses, *, S, bt=512):
    tile = pl.BlockSpec((bt, 128), lambda i, p: (i, p))

    def body(o0, o1, o2, l0, l1, l2, o_ref, om_ref, lse_ref):
        lane = lax.broadcasted_iota(jnp.int32, (bt, 128), 1)
        lo = lane < DIL_HEAD_DIM
        ls = [l0[...], l1[...], l2[...]]
        ws, stat = [], jnp.zeros((bt, 128), F32)
        for e in range(2):
            a = [l[:, e:e + 1] for l in ls]
            m = jnp.maximum(jnp.maximum(a[0], a[1]), a[2])
            ex = [jnp.exp(v - m) for v in a]
            tot = ex[0] + ex[1] + ex[2]
            ws.append([v / tot for v in ex])
            stat = jnp.where(lane == e, m + jnp.log(tot), stat)
        acc = jnp.zeros((bt, 128), F32)
        for gi, o in enumerate((o0, o1, o2)):
            acc = acc + jnp.where(lo, ws[0][gi], ws[1][gi]) * o[...]
        o_ref[...] = acc
        om_ref[...] = _mx(acc)
        lse_ref[...] = stat

    return pl.pallas_call(
        body, name="dil_merge", grid=(S // bt, 2),
        in_specs=[tile] * 6, out_specs=[tile, tile, tile],
        out_shape=[jax.ShapeDtypeStruct((S, 256), F32), jax.ShapeDtypeStruct((S, 256), MXU_DTYPE),
                   jax.ShapeDtypeStruct((S, 256), F32)],
        compiler_params=_cparams(("parallel", "parallel")),
    )(*os_, *lses)


def _dilated_delta(do, o, *, S, bt=512):
    tile = pl.BlockSpec((bt, 128), lambda i, p: (i, p))

    def body(do_ref, o_ref, d_ref):
        lane = lax.broadcasted_iota(jnp.int32, (bt, 128), 1)
        prod = do_ref[...] * o_ref[...]
        d0 = jnp.sum(jnp.where(lane < DIL_HEAD_DIM, prod, 0.0), axis=-1, keepdims=True)
        d1 = jnp.sum(jnp.where(lane >= DIL_HEAD_DIM, prod, 0.0), axis=-1, keepdims=True)
        d_ref[...] = jnp.where(lane == 0, d0, jnp.where(lane == 1, d1, 0.0))

    return pl.pallas_call(
        body, name="dil_delta", grid=(S // bt, 2), in_specs=[tile, tile], out_specs=tile,
        out_shape=jax.ShapeDtypeStruct((S, 256), F32), compiler_params=_cparams(("parallel", "parallel")),
    )(do, o)


MEM_T = 512
QM_BLK = C_QM // MEM_HEAD_DIM


def _mem_attn_fwd(proj, kv, *, S):
    scale = MEM_HEAD_DIM ** -0.5

    def body(q_ref, k_ref, v_ref, o_ref, om_ref, lse_ref):
        s = _dot(q_ref[...], k_ref[...], "nt") * scale
        m = jnp.max(s, axis=-1, keepdims=True)
        p = jnp.exp(s - m)
        den = jnp.sum(p, axis=-1, keepdims=True)
        o = _dot(p, v_ref[...]) / den
        o_ref[...] = o
        om_ref[...] = _mx(o)
        lse_ref[0] = m + jnp.log(den)

    return pl.pallas_call(
        body, name="mem_attn_fwd", grid=(S // MEM_T, MEM_HEADS),
        in_specs=[pl.BlockSpec((MEM_T, MEM_HEAD_DIM), lambda i, h: (i, QM_BLK + h)),
                  pl.BlockSpec((N_MEM, MEM_HEAD_DIM), lambda i, h: (0, h)),
                  pl.BlockSpec((N_MEM, MEM_HEAD_DIM), lambda i, h: (0, MEM_HEADS + h))],
        out_specs=[pl.BlockSpec((MEM_T, MEM_HEAD_DIM), lambda i, h: (i, h)),
                   pl.BlockSpec((MEM_T, MEM_HEAD_DIM), lambda i, h: (i, h)),
                   pl.BlockSpec((1, MEM_T, 1), lambda i, h: (h, i, 0))],
        out_shape=[jax.ShapeDtypeStruct((S, MEM_WIDTH), F32), jax.ShapeDtypeStruct((S, MEM_WIDTH), MXU_DTYPE),
                   jax.ShapeDtypeStruct((MEM_HEADS, S, 1), F32)],
        compiler_params=_cparams(("parallel", "parallel")),
    )(proj, kv, kv)


def _mem_attn_bwd(proj, kv, om, lse, dom, *, S):
    scale = MEM_HEAD_DIM ** -0.5

    def body(q_ref, k_ref, v_ref, o_ref, lse_ref, do_ref, dq_ref, dk_ref, dv_ref):
        @pl.when(pl.program_id(1) == 0)
        def _():
            dk_ref[...] = jnp.zeros_like(dk_ref)
            dv_ref[...] = jnp.zeros_like(dv_ref)

        qv, kv_, vv, dov = q_ref[...], k_ref[...], v_ref[...], do_ref[...]
        p = jnp.exp(_dot(qv, kv_, "nt") * scale - lse_ref[0])
        delta = jnp.sum(dov * o_ref[...], axis=-1, keepdims=True)
        ds = p * (_dot(dov, vv, "nt") - delta)
        dq_ref[...] = (_dot(ds, kv_) * scale).astype(dq_ref.dtype)
        dk_ref[...] += _dot(ds, qv, "tn") * scale
        dv_ref[...] += _dot(p, dov, "tn")

    tile = pl.BlockSpec((MEM_T, MEM_HEAD_DIM), lambda h, i: (i, h))
    kvo = pl.BlockSpec((N_MEM, MEM_HEAD_DIM), lambda h, i: (0, h))
    return pl.pallas_call(
        body, name="mem_attn_bwd", grid=(MEM_HEADS, S // MEM_T),
        in_specs=[pl.BlockSpec((MEM_T, MEM_HEAD_DIM), lambda h, i: (i, QM_BLK + h)),
                  pl.BlockSpec((N_MEM, MEM_HEAD_DIM), lambda h, i: (0, h)),
                  pl.BlockSpec((N_MEM, MEM_HEAD_DIM), lambda h, i: (0, MEM_HEADS + h)),
                  tile, pl.BlockSpec((1, MEM_T, 1), lambda h, i: (h, i, 0)), tile],
        out_specs=[tile, kvo, kvo],
        out_shape=[jax.ShapeDtypeStruct((S, MEM_WIDTH), MXU_DTYPE), jax.ShapeDtypeStruct((N_MEM, MEM_WIDTH), F32),
                   jax.ShapeDtypeStruct((N_MEM, MEM_WIDTH), F32)],
        compiler_params=_cparams(("parallel", "arbitrary")),
    )(proj, kv, kv, om, lse, dom)


MIX_BM = 1024
MIX_BN = 256
GATES_BLK = C_GATES // MIX_BN


def _mix_specs(j_outer):
    ix = (lambda f: (lambda j, i: f(i, j))) if j_outer else (lambda f: f)
    act = lambda width: pl.BlockSpec((MIX_BM, width), ix(lambda i, j: (i, 0)))
    wgt = lambda width: pl.BlockSpec((width, MIX_BN), ix(lambda i, j: (0, j)))
    gate = lambda b: pl.BlockSpec((MIX_BM, MIX_BN), ix(lambda i, j: (i, GATES_BLK + 4 * b + j)))
    bias = lambda b: pl.BlockSpec((1, MIX_BN), ix(lambda i, j: (0, 4 * b + j)))
    tile = pl.BlockSpec((MIX_BM, MIX_BN), ix(lambda i, j: (i, j)))
    return act, wgt, gate, bias, tile


def _mix_fwd(z_lru, o_dil, om, w_lru, w_dil, w_mem, proj, b_gate, *, S):
    act, wgt, gate, bias, tile = _mix_specs(False)

    def body(zl, od, mo, wl, wd, wm, g0, g1, g2, b0, b1, b2, o_ref):
        acc = jax.nn.sigmoid(g0[...] + b0[...]) * _dot(zl[...], wl[...])
        acc += jax.nn.sigmoid(g1[...] + b1[...]) * _dot(od[...], wd[...])
        acc += jax.nn.sigmoid(g2[...] + b2[...]) * _dot(mo[...], wm[...])
        o_ref[...] = acc.astype(o_ref.dtype)

    return pl.pallas_call(
        body, name="mix_fwd", grid=(S // MIX_BM, D_MODEL // MIX_BN),
        in_specs=[act(D_RNN), act(256), act(MEM_WIDTH), wgt(D_RNN), wgt(256), wgt(MEM_WIDTH),
                  gate(0), gate(1), gate(2), bias(0), bias(1), bias(2)],
        out_specs=tile, out_shape=jax.ShapeDtypeStruct((S, D_MODEL), MXU_DTYPE),
        compiler_params=_cparams(("parallel", "parallel")),
    )(z_lru, o_dil, om, w_lru, w_dil, w_mem, proj, proj, proj, b_gate, b_gate, b_gate)


def _mix_bwd(dmerged, z_lru, o_dil, om, w_lru, w_dil, w_mem, proj, b_gate, *, S):
    act, wgt, gate, bias, tile = _mix_specs(True)

    def body(dm, zl, od, mo, wl, wd, wm, g0, g1, g2, b0, b1, b2,
             dg0, dg1, dg2, dy0, dy1, dy2, db0, db1, db2):
        @pl.when(pl.program_id(1) == 0)
        def _():
            for r in (db0, db1, db2):
                r[...] = jnp.zeros_like(r)

        dmv = dm[...]
        for act_ref, w_ref, g_ref, b_ref, dg_ref, dy_ref, db_ref in (
                (zl, wl, g0, b0, dg0, dy0, db0), (od, wd, g1, b1, dg1, dy1, db1), (mo, wm, g2, b2, dg2, dy2, db2)):
            y = _dot(act_ref[...], w_ref[...])
            gt = jax.nn.sigmoid(g_ref[...] + b_ref[...])
            dgate = dmv * y * gt * (1.0 - gt)
            dg_ref[...] = dgate.astype(dg_ref.dtype)
            dy_ref[...] = (dmv * gt).astype(dy_ref.dtype)
            db_ref[...] += _colsum(dgate)

    big = jax.ShapeDtypeStruct((S, D_MODEL), MXU_DTYPE)
    vec = jax.ShapeDtypeStruct((1, D_MODEL), F32)
    vspec = pl.BlockSpec((1, MIX_BN), lambda j, i: (0, j))
    return pl.pallas_call(
        body, name="mix_bwd", grid=(D_MODEL // MIX_BN, S // MIX_BM),
        in_specs=[tile, act(D_RNN), act(256), act(MEM_WIDTH), wgt(D_RNN), wgt(256), wgt(MEM_WIDTH),
                  gate(0), gate(1), gate(2), bias(0), bias(1), bias(2)],
        out_specs=[tile] * 6 + [vspec] * 3, out_shape=[big] * 6 + [vec] * 3,
        compiler_params=_cparams(("parallel", "arbitrary")),
    )(dmerged, z_lru, o_dil, om, w_lru, w_dil, w_mem, proj, proj, proj, b_gate, b_gate, b_gate)


def _adamw_math(w, g, m, v):
    m = ADAM_B1 * m + (1.0 - ADAM_B1) * g
    v = ADAM_B2 * v + (1.0 - ADAM_B2) * (g * g)
    m_hat = m / (1.0 - ADAM_B1 ** ADAM_STEP)
    v_hat = v / (1.0 - ADAM_B2 ** ADAM_STEP)
    delta = -ADAM_LR * (m_hat / (jnp.sqrt(v_hat) + ADAM_EPS) + ADAM_WD * w)
    return delta, m, v


def _adamw_landed(w, own, land, m, v, *, name, col_blk=0, prev=None):
    R = w.shape[0]
    n_parts, C = land.shape[0], land.shape[2]
    br = next(d for d in (256, 464, 128) if R % d == 0)
    tile = pl.BlockSpec((br, C), lambda i: (i, col_blk))
    part = pl.BlockSpec((br, C), lambda i: (i, 0))
    n_prev = 0 if prev is None else 4

    def body(w_ref, o_ref, l_ref, m_ref, v_ref, *rest):
        g_ref, d_ref, nm_ref, nv_ref = rest[n_prev:]
        g = o_ref[...].astype(F32)
        for p in range(n_parts):
            g = g + l_ref[p].astype(F32)
        d, nm, nv = _adamw_math(w_ref[...], g, m_ref[...], v_ref[...])
        g_ref[...] = g
        d_ref[...] = d
        nm_ref[...] = nm
        nv_ref[...] = nv

    return pl.pallas_call(
        body, name=name, grid=(R // br,),
        in_specs=[tile, part, pl.BlockSpec((n_parts, br, C), lambda i: (0, i, 0)), tile, tile]
        + [pl.BlockSpec(memory_space=pl.ANY)] * n_prev,
        out_specs=[tile] * 4, out_shape=[jax.ShapeDtypeStruct(w.shape, F32)] * 4,
        input_output_aliases={5 + j: j for j in range(n_prev)},
        compiler_params=_cparams(("parallel",)),
    )(w, own, land, m, v, *(prev or ()))


def _adamw_plain(w, g, m, v, *, name):
    def body(w_ref, g_ref, m_ref, v_ref, d_ref, nm_ref, nv_ref):
        d, nm, nv = _adamw_math(w_ref[...], g_ref[...], m_ref[...], v_ref[...])
        d_ref[...] = d
        nm_ref[...] = nm
        nv_ref[...] = nv

    return pl.pallas_call(
        body, name=name, out_shape=[jax.ShapeDtypeStruct(w.shape, F32)] * 3, compiler_params=_cparams(),
    )(w, g, m, v)


def _my_pos():
    return lax.axis_index("x"), lax.axis_index("y"), lax.axis_index("c")


def _dev_index(p):
    return 4 * p[0] + 2 * p[1] + p[2]


def _all_gather(shards):
    n = len(shards)
    hbm = pl.BlockSpec(memory_space=pl.ANY)

    def body(*refs):
        ins, outs = refs[:n], refs[n:2 * n]
        send_sems, recv_sems, local_sems = refs[2 * n:]
        x, y, c = _my_pos()
        me, sibling = (x, y, c), (x, y, 1 - c)
        chips = [(1 - x, y), (x, 1 - y), (1 - x, 1 - y)]

        def copy(a, k, block, to, src=None):
            dst = outs[a].at[_dev_index(block)]
            return pltpu.make_async_remote_copy(
                src_ref=dst if src is None else src, dst_ref=dst,
                send_sem=send_sems.at[a, k], recv_sem=recv_sems.at[a, k], device_id=to, device_id_type=MESH)

        mine = [pltpu.make_async_copy(ins[a], outs[a].at[_dev_index(me)], local_sems.at[a]) for a in range(n)]
        for cp in mine:
            cp.start()
        first = []
        for a in range(n):
            first.append(copy(a, 0, me, sibling, src=ins[a]))
            first += [copy(a, 1 + j, me, (*chip, c), src=ins[a]) for j, chip in enumerate(chips)]
        for cp in first:
            cp.start()
        passed = []
        for j, chip in enumerate(chips):
            for a in range(n):
                copy(a, 1 + j, (*chip, c), me).wait_recv()
                fwd = copy(a, 4 + j, (*chip, c), sibling)
                fwd.start()
                passed.append(fwd)
        for a in range(n):
            copy(a, 0, sibling, me).wait_recv()
        for j, chip in enumerate(chips):
            for a in range(n):
                copy(a, 4 + j, (*chip, 1 - c), me).wait_recv()
        for cp in first + passed:
            cp.wait_send()
        for cp in mine:
            cp.wait()

    return pl.pallas_call(
        body, name="all_gather_weights",
        in_specs=[hbm] * n, out_specs=[hbm] * n,
        out_shape=[jax.ShapeDtypeStruct((N_DEV,) + s.shape, s.dtype) for s in shards],
        scratch_shapes=[pltpu.SemaphoreType.DMA((n, 7)), pltpu.SemaphoreType.DMA((n, 7)), pltpu.SemaphoreType.DMA((n,))],
        compiler_params=pltpu.CompilerParams(has_side_effects=True),
    )(*shards)


def _peers(me):
    x, y, c = me
    out = []
    for k in range(1, 8):
        fx, fy, fc = (k >> 2) & 1, (k >> 1) & 1, k & 1
        out.append((k - 1, (1 - x if fx else x, 1 - y if fy else y, 1 - c if fc else c)))
    return out


HBM_SPEC = pl.BlockSpec(memory_space=pltpu.HBM)
SEM_SPEC = pl.BlockSpec(memory_space=pltpu.SEMAPHORE)
DATAFLOW_EFFECT = pltpu.SideEffectType.DATAFLOW_SIDE_EFFECTING


def _gather_refs(src, land, me, peer, k):
    return src, land.at[_dev_index(me)]


def _scatter_refs(src, land, me, peer, k):
    return src.at[_dev_index(peer)], land.at[k]


def _push_start(srcs, land_shapes, refs_of, name):
    n = len(srcs)

    def body(*refs):
        ins, lands = refs[:n], refs[n:2 * n]
        send_sems, recv_sems, token = refs[2 * n], refs[2 * n + 1], refs[-1]
        me = _my_pos()
        for k, peer in _peers(me):
            for a in range(n):
                src, dst = refs_of(ins[a], lands[a], me, peer, k)
                pltpu.make_async_remote_copy(src_ref=src, dst_ref=dst, send_sem=send_sems.at[7 * a + k],
                                             recv_sem=recv_sems.at[7 * a + k], device_id=peer, device_id_type=MESH).start()
        token[...] = jnp.zeros_like(token)

    lands = [lax.empty(shp, s.dtype) for shp, s in zip(land_shapes, srcs)]
    hbm = lambda a: pltpu.with_memory_space_constraint(a, pltpu.HBM)
    res = pl.pallas_call(
        body, name=name,
        out_shape=(pltpu.SemaphoreType.DMA((7 * n,)), pltpu.SemaphoreType.DMA((7 * n,)),
                   *[pltpu.HBM(s.shape, s.dtype) for s in srcs], *[pltpu.HBM(l.shape, l.dtype) for l in lands],
                   jax.ShapeDtypeStruct((8, 128), F32)),
        in_specs=[HBM_SPEC] * (2 * n),
        out_specs=(SEM_SPEC, SEM_SPEC, *[HBM_SPEC] * (2 * n), pl.BlockSpec(memory_space=pltpu.VMEM)),
        input_output_aliases={i: 2 + i for i in range(2 * n)},
        compiler_params=pltpu.CompilerParams(has_side_effects=DATAFLOW_EFFECT),
    )(*[hbm(s) for s in srcs], *[hbm(l) for l in lands])
    return dict(sems=(res[0], res[1]), srcs=list(res[2:2 + n]), lands=list(res[2 + n:2 + 2 * n]), token=res[-1], n=n,
                refs_of=refs_of, name=name)


def _push_wait(started, after):
    n, refs_of = started["n"], started["refs_of"]

    def body(*refs):
        ins, lands = refs[:n], refs[n:2 * n]
        send_sems, recv_sems = refs[2 * n], refs[2 * n + 1]
        me = _my_pos()
        for k, peer in _peers(me):
            for a in range(n):
                src, dst = refs_of(ins[a], lands[a], me, peer, k)
                cp = pltpu.make_async_remote_copy(src_ref=src, dst_ref=dst, send_sem=send_sems.at[7 * a + k],
                                                  recv_sem=recv_sems.at[7 * a + k], device_id=peer, device_id_type=MESH)
                cp.wait_send()
                cp.wait_recv()

    arrs = started["srcs"] + started["lands"]
    res = pl.pallas_call(
        body, name=started["name"].replace("start", "wait"),
        out_shape=tuple(pltpu.HBM(a.shape, a.dtype) for a in arrs),
        in_specs=[HBM_SPEC] * (2 * n) + [SEM_SPEC, SEM_SPEC, pl.BlockSpec(memory_space=pl.ANY)],
        out_specs=tuple([HBM_SPEC] * (2 * n)),
        input_output_aliases={i: i for i in range(2 * n)},
        compiler_params=pltpu.CompilerParams(has_side_effects=DATAFLOW_EFFECT),
    )(*arrs, *started["sems"], after)
    return list(res[n:2 * n])


def _sum_slots(slots):
    def body(in_ref, out_ref):
        acc = in_ref[0]
        for d in range(1, N_DEV):
            acc = acc + in_ref[d]
        out_ref[...] = acc

    return pl.pallas_call(body, name="sum_small", out_shape=jax.ShapeDtypeStruct(slots.shape[1:], F32),
                          compiler_params=_cparams())(slots)


def _local_step(x, mem, tgt, W, P, late_weights, send_grads, reduce_small, tie0):
    S = x.shape[0]
    W = dict(W)
    h = _rmsnorm_fwd(x, P["g_mix"] + tie0, rows=S, name="norm_mix")
    proj = _matmul(h, W["w_in_t"], M=S, N=D_IN, K=D_MODEL, mode="nt", bm=512, bn=D_IN // 2, bk=D_MODEL, name="mm_in",
                   j_outer=True)

    wa_bd, wx_bd = _mx(_block_diag(P["w_rg_a"])), _mx(_block_diag(P["w_rg_x"]))
    lru_args = (W["conv_w"], P["conv_b"].reshape(1, -1), wa_bd, wx_bd, P["b_rg_a"].reshape(1, -1),
                P["b_rg_x"].reshape(1, -1), P["lru_lambda"].reshape(1, -1))
    hl, z_lru = _lru_fwd(proj, *lru_args, S=S)

    buckets = _dil_buckets()
    bias = _dil_bias(P["rel_bias"], buckets)
    group_out = [_dilated_fwd(proj, bias, g, S=S) for g in range(len(DIL_GROUPS))]
    o_dil, o_dil_m, lse_dil = _dilated_merge([o for o, _ in group_out], [l for _, l in group_out], S=S)

    W.update(late_weights(o_dil))
    mem_n = _rmsnorm_fwd(mem, P["g_mem"], rows=N_MEM, name="norm_mem")
    kv = _matmul(mem_n, W["w_mem_kv"], M=N_MEM, N=2 * MEM_WIDTH, K=D_MODEL, mode="nn", bm=N_MEM, bn=512, bk=D_MODEL,
                 name="mm_kv")
    om, om_m, lse_mem = _mem_attn_fwd(proj, kv, S=S)
    b_gate = P["b_gate"].reshape(1, -1)
    merged = _mix_fwd(z_lru, o_dil_m, om_m, W["w_lru_out"], W["w_dil_out"], W["w_mem_out"], proj, b_gate, S=S)
    x1 = _matmul(merged, W["w_out"], M=S, N=D_MODEL, K=D_MODEL, mode="nn", bm=512, bn=D_MODEL, bk=D_MODEL, name="mm_out",
                 epilogue=lambda acc, r: (r + acc,), extras=[(x, (0, 0))])
    hm = _rmsnorm_fwd(x1, P["g_mlp"], rows=S, name="norm_mlp")

    def relu2(acc):
        rl = jnp.maximum(acc, 0.0)
        return (rl * rl,)

    act = _matmul(hm, W["w_mlp_in"], M=S, N=D_FF, K=D_MODEL, mode="nn", bm=1024, bn=1024, bk=D_MODEL, name="mm_mlp_in",
                  out_dtypes=(MXU_DTYPE,), epilogue=relu2, j_outer=True)
    x2 = _matmul(act, W["w_mlp_out"], M=S, N=D_MODEL, K=D_FF, mode="nn", bm=512, bn=D_MODEL, bk=D_FF, name="mm_mlp_out",
                 epilogue=lambda acc, r: (r + acc,), extras=[(x1, (0, 0))])
    loss, dx2, dx2_m, dg_final = _loss_head(x2, P["g_final"], tgt, rows=S)

    G, Gs = {}, {}
    Gs["g_final"] = dg_final
    dw = dict(mode="tn", K=S, bk=S, out_dtypes=(MXU_DTYPE,))
    G["w_mlp_out"] = _matmul(act, dx2_m, M=D_FF, N=D_MODEL, bm=512, bn=D_MODEL, name="mm_dw_mlp_out",
                             parts=("rows", D_FF // N_DEV), **dw)
    du = _matmul(dx2_m, W["w_mlp_out"], M=S, N=D_FF, K=D_MODEL, mode="nt", bm=1024, bn=1024, bk=D_MODEL, name="mm_du",
                 out_dtypes=(MXU_DTYPE,), epilogue=lambda acc, a: (acc * (2.0 * jnp.sqrt(a.astype(F32))),),
                 extras=[(act, (0, 0))], j_outer=True)
    G["w_mlp_in"] = _matmul(hm, du, M=D_MODEL, N=D_FF, bm=D_MODEL, bn=512, name="mm_dw_mlp_in",
                            parts=("cols", D_FF // N_DEV), **dw)
    tie1 = send_grads({n: G.pop(n) for n in ("w_mlp_out", "w_mlp_in")})
    dhm = _matmul(du, W["w_mlp_in"], M=S, N=D_MODEL, K=D_FF, mode="nt", bm=512, bn=D_MODEL, bk=D_FF, name="mm_dhm",
                  deps=[tie1])
    dx1, dx1_m, Gs["g_mlp"] = _rmsnorm_bwd(x1, P["g_mlp"], dhm, dx2, rows=S, name="norm_mlp_bwd",
                                           dx_dtypes=(F32, MXU_DTYPE))
    G["w_out"] = _matmul(merged, dx1_m, M=D_MODEL, N=D_MODEL, bm=512, bn=D_MODEL, name="mm_dw_out",
                         parts=("rows", D_MODEL // N_DEV), **dw)
    dmerged = _matmul(dx1_m, W["w_out"], M=S, N=D_MODEL, K=D_MODEL, mode="nt", bm=512, bn=D_MODEL, bk=D_MODEL, name="mm_dmerged")
    (dg0, dg1, dg2, dy_lru, dy_dil, dy_mem, db0, db1, db2) = _mix_bwd(
        dmerged, z_lru, o_dil_m, om_m, W["w_lru_out"], W["w_dil_out"], W["w_mem_out"], proj, b_gate, S=S)
    Gs["b_gate"] = jnp.concatenate([db0, db1, db2], axis=1)

    G["w_mem_out"] = _matmul(om_m, dy_mem, M=MEM_WIDTH, N=D_MODEL, bm=MEM_WIDTH, bn=D_MODEL, name="mm_dw_mem_out",
                             parts=("cols", D_MODEL // N_DEV), **dw)
    dom = _matmul(dy_mem, W["w_mem_out"], M=S, N=MEM_WIDTH, K=D_MODEL, mode="nt", bm=512, bn=MEM_WIDTH, bk=D_MODEL,
                  name="mm_dom")
    dqm, dk_mem, dv_mem = _mem_attn_bwd(proj, kv, om, lse_mem, dom, S=S)
    dkv = jnp.concatenate([dk_mem, dv_mem], axis=1)
    G["w_mem_kv"] = _matmul(mem_n, dkv, M=D_MODEL, N=2 * MEM_WIDTH, K=N_MEM, mode="tn", bm=D_MODEL, bn=2 * MEM_WIDTH,
                            bk=N_MEM, name="mm_dw_kv", out_dtypes=(MXU_DTYPE,), parts=("rows", D_MODEL // N_DEV))
    dmem_n = _matmul(dkv, W["w_mem_kv"], M=N_MEM, N=D_MODEL, K=2 * MEM_WIDTH, mode="nt", bm=N_MEM, bn=D_MODEL,
                     bk=2 * MEM_WIDTH, name="mm_dmem")
    (Gs["g_mem"],) = _rmsnorm_bwd(mem, P["g_mem"], dmem_n, None, rows=N_MEM, name="norm_mem_bwd", dx_dtypes=())

    G["w_dil_out"] = _matmul(o_dil_m, dy_dil, M=256, N=D_MODEL, bm=256, bn=D_MODEL, name="mm_dw_dil_out",
                             parts=("cols", D_MODEL // N_DEV), **dw)
    do_dil = _matmul(dy_dil, W["w_dil_out"], M=S, N=256, K=D_MODEL, mode="nt", bm=512, bn=256, bk=D_MODEL, name="mm_do_dil")
    G["w_lru_out"] = _matmul(z_lru, dy_lru, M=D_RNN, N=D_MODEL, bm=D_RNN, bn=D_MODEL, name="mm_dw_lru_out",
                             parts=("cols", D_MODEL // N_DEV), **dw)
    dz = _matmul(dy_lru, W["w_lru_out"], M=S, N=D_RNN, K=D_MODEL, mode="nt", bm=512, bn=D_RNN, bk=D_MODEL, name="mm_dz_lru")
    tie2 = send_grads({n: G.pop(n) for n in ("w_out", "w_mem_out", "w_mem_kv", "w_dil_out", "w_lru_out")})
    bias = bias + tie2[0, 0]
    delta = _dilated_delta(do_dil, o_dil, S=S)
    dq_parts, dk_parts, dv_parts, dbias = [], [], [], []
    for g in range(len(DIL_GROUPS)):
        dq_g, dk_g, dv_g, db_g = _dilated_bwd(proj, do_dil, lse_dil, delta, bias, g, S=S)
        dq_parts.append(dq_g)
        dk_parts.append(dk_g)
        dv_parts.append(dv_g)
        dbias.append(db_g)
    drel = _dil_bias_bwd(jnp.stack(dbias, axis=0), buckets)
    Gs["rel_bias"] = drel[:, :3 * DIL_HEADS]

    dxl, dgl, dcw, dcb, dwa, dwx, dba, dbx, dlam = _lru_bwd(proj, hl, dz, *lru_args, S=S)
    Gs["conv_w"], Gs["conv_b"] = dcw, dcb
    Gs["w_rg_a"], Gs["w_rg_x"] = _block_diag_extract(dwa), _block_diag_extract(dwx)
    Gs["b_rg_a"], Gs["b_rg_x"], Gs["lru_lambda"] = dba, dbx, dlam

    dproj = jnp.concatenate([dxl, dgl] + dq_parts + dk_parts + dv_parts + [dqm, dg0, dg1, dg2], axis=1)
    cols = D_MODEL // W_IN_PIECES
    tie = []
    for q in range(W_IN_PIECES):
        dw_q = _matmul(dproj, h, M=D_IN, N=cols, K=S, mode="tn", bm=D_IN // 2, bn=cols, bk=512, name=f"mm_dw_in_{q}",
                       b_off=(0, q), out_dtypes=(MXU_DTYPE,), parts=("rows", D_IN // N_DEV), deps=tie)
        tie = [send_grads({f"w_in_{q}": dw_q})]
    dh = _matmul(dproj, W["w_in_t"], M=S, N=D_MODEL, K=D_IN, mode="nn", bm=256, bn=D_MODEL, bk=D_IN, name="mm_dh",
                 deps=tie)
    grad_x, Gs["g_mix"] = _rmsnorm_bwd(x, P["g_mix"], dh, dx1, rows=S, name="norm_mix_bwd")
    return grad_x, reduce_small(Gs, loss)


BIG = ("w_in", "w_lru_out", "w_dil_out", "w_mem_kv", "w_mem_out", "w_out", "w_mlp_in", "w_mlp_out")
W_IN_PIECES = 2
COL_SHARDED = ("w_lru_out", "w_dil_out", "w_mem_out", "w_mlp_in")
SMALL = ("g_mix", "b_gate", "conv_b", "w_rg_a", "b_rg_a", "w_rg_x", "b_rg_x", "lru_lambda", "rel_bias", "g_mem",
         "g_mlp", "g_final")
WEIGHTS = ("g_mix", "w_in", "b_gate", "conv_w", "conv_b", "w_rg_a", "b_rg_a", "w_rg_x", "b_rg_x", "lru_lambda",
           "w_lru_out", "rel_bias", "w_dil_out", "g_mem", "w_mem_kv", "w_mem_out", "w_out", "g_mlp", "w_mlp_in",
           "w_mlp_out", "g_final")
SMALL_SHAPES = {"g_mix": (1024,), "b_gate": (3072,), "conv_b": (768,), "w_rg_a": (12, 64, 64), "b_rg_a": (768,),
                "w_rg_x": (12, 64, 64), "b_rg_x": (768,), "lru_lambda": (768,), "rel_bias": (32, 12), "g_mem": (1024,),
                "g_mlp": (1024,), "g_final": (1024,)}


def _gathered_to_full(name, gathered):
    if name in COL_SHARDED:
        n, r, c = gathered.shape
        return gathered.transpose(1, 0, 2).reshape(r, n * c)
    n, r, c = gathered.shape
    return gathered.reshape(n * r, c)


def _pack(parts):
    flat = jnp.concatenate([p.reshape(-1).astype(F32) for p in parts])
    pad = (-flat.shape[0]) % 1024
    return jnp.pad(flat, (0, pad)).reshape(-1, 128)


def _unpack(pack, shapes):
    flat = pack.reshape(-1)
    out, off = [], 0
    for shp in shapes:
        size = math.prod(shp)
        out.append(flat[off:off + size].reshape(shp))
        off += size
    return out


def kernel(x, mem, g_mix, w_in, b_gate, conv_w, conv_b, w_rg_a, b_rg_a, w_rg_x, b_rg_x, lru_lambda, w_lru_out, rel_bias, w_dil_out, g_mem, w_mem_kv, w_mem_out, w_out, g_mlp, w_mlp_in, w_mlp_out, g_final, loss_target, m_g_mix, m_w_in, m_b_gate, m_conv_w, m_conv_b, m_w_rg_a, m_b_rg_a, m_w_rg_x, m_b_rg_x, m_lru_lambda, m_w_lru_out, m_rel_bias, m_w_dil_out, m_g_mem, m_w_mem_kv, m_w_mem_out, m_w_out, m_g_mlp, m_w_mlp_in, m_w_mlp_out, m_g_final, v_g_mix, v_w_in, v_b_gate, v_conv_w, v_conv_b, v_w_rg_a, v_b_rg_a, v_w_rg_x, v_b_rg_x, v_lru_lambda, v_w_lru_out, v_rel_bias, v_w_dil_out, v_g_mem, v_w_mem_kv, v_w_mem_out, v_w_out, v_g_mlp, v_w_mlp_in, v_w_mlp_out, v_g_final):
    w = dict(g_mix=g_mix, w_in=w_in, b_gate=b_gate, conv_w=conv_w, conv_b=conv_b, w_rg_a=w_rg_a, b_rg_a=b_rg_a,
             w_rg_x=w_rg_x, b_rg_x=b_rg_x, lru_lambda=lru_lambda, w_lru_out=w_lru_out, rel_bias=rel_bias,
             w_dil_out=w_dil_out, g_mem=g_mem, w_mem_kv=w_mem_kv, w_mem_out=w_mem_out, w_out=w_out, g_mlp=g_mlp,
             w_mlp_in=w_mlp_in, w_mlp_out=w_mlp_out, g_final=g_final)
    m = dict(g_mix=m_g_mix, w_in=m_w_in, b_gate=m_b_gate, conv_w=m_conv_w, conv_b=m_conv_b, w_rg_a=m_w_rg_a,
             b_rg_a=m_b_rg_a, w_rg_x=m_w_rg_x, b_rg_x=m_b_rg_x, lru_lambda=m_lru_lambda, w_lru_out=m_w_lru_out,
             rel_bias=m_rel_bias, w_dil_out=m_w_dil_out, g_mem=m_g_mem, w_mem_kv=m_w_mem_kv, w_mem_out=m_w_mem_out,
             w_out=m_w_out, g_mlp=m_g_mlp, w_mlp_in=m_w_mlp_in, w_mlp_out=m_w_mlp_out, g_final=m_g_final)
    v = dict(g_mix=v_g_mix, w_in=v_w_in, b_gate=v_b_gate, conv_w=v_conv_w, conv_b=v_conv_b, w_rg_a=v_w_rg_a,
             b_rg_a=v_b_rg_a, w_rg_x=v_w_rg_x, b_rg_x=v_b_rg_x, lru_lambda=v_lru_lambda, w_lru_out=v_w_lru_out,
             rel_bias=v_rel_bias, w_dil_out=v_w_dil_out, g_mem=v_g_mem, w_mem_kv=v_w_mem_kv, w_mem_out=v_w_mem_out,
             w_out=v_w_out, g_mlp=v_g_mlp, w_mlp_in=v_w_mlp_in, w_mlp_out=v_w_mlp_out, g_final=v_g_final)

    my_idx = _dev_index(_my_pos())

    g_in, g_cw = _all_gather([_mx(w["w_in"].T), w["conv_w"]])
    W = {"w_in_t": g_in.reshape(D_IN, D_MODEL), "conv_w": g_cw.transpose(1, 0, 2).reshape(CONV_WIDTH, D_RNN)}
    late = [n for n in BIG if n != "w_in"]
    late_shards = [_mx(w[n]) for n in late]
    late_started = _push_start(late_shards, [(N_DEV,) + s.shape for s in late_shards], _gather_refs,
                                "gather_late_start")
    P = {n: w[n] for n in SMALL}

    def late_weights(after):
        lands = _push_wait(late_started, after)
        out = {}
        for n, land, own in zip(late, lands, late_shards):
            full = lax.dynamic_update_index_in_dim(land, own, my_idx, 0)
            out[n] = _gathered_to_full(n, full)
        return out

    sent, small = [], {}
    small_names = SMALL + ("conv_w",)

    def send_grads(gs):
        names = list(gs)
        parts = [gs[n] for n in names]
        own = [lax.dynamic_index_in_dim(p, my_idx, 0, keepdims=False) for p in parts]
        started = _push_start(parts, [(N_DEV - 1,) + p.shape[1:] for p in parts], _scatter_refs,
                              f"scatter{len(sent)}_start")
        sent.append((names, own, started))
        return started["token"]

    def reduce_small(gs, loss):
        pack = _pack([gs[n] for n in small_names] + [loss])
        small["pack"] = pack
        small["started"] = _push_start([pack], [(N_DEV,) + pack.shape], _gather_refs, "small_start")
        return small["started"]["token"]

    grad_x, last_token = _local_step(x[0], mem[0], loss_target[0], W, P, late_weights, send_grads, reduce_small,
                                     late_started["token"][0, 0])

    grads, deltas, new_m, new_v = {}, {}, {}, {}
    after = last_token
    for names, own, started in sent[:-W_IN_PIECES]:
        for n, o, land in zip(names, own, _push_wait(started, after)):
            grads[n], deltas[n], new_m[n], new_v[n] = _adamw_landed(w[n], o, land, m[n], v[n], name=f"adamw_{n}")
            after = deltas[n]
    (small_land,) = _push_wait(small["started"], after)
    total = _sum_slots(lax.dynamic_update_index_in_dim(small_land, small["pack"], my_idx, 0))
    small_shapes = [SMALL_SHAPES[n] for n in SMALL] + [(CONV_WIDTH, D_RNN), (1,)]
    summed = dict(zip(small_names + ("loss",), _unpack(total, small_shapes)))
    after = total
    prev = None
    for q, (names, own, started) in enumerate(sent[-W_IN_PIECES:]):
        (land,) = _push_wait(started, after)
        prev = _adamw_landed(w["w_in"].T, own[0], land, m["w_in"].T, v["w_in"].T, name=f"adamw_{names[0]}",
                             col_blk=q, prev=prev)
    grads["w_in"], deltas["w_in"], new_m["w_in"], new_v["w_in"] = [t.T for t in prev]
    zeros_cw = jnp.zeros((CONV_WIDTH, D_RNN), F32)
    w_pack = _pack([w[n] for n in SMALL] + [zeros_cw, jnp.zeros((1,), F32)])
    m_pack = _pack([m[n] for n in SMALL] + [zeros_cw, jnp.zeros((1,), F32)])
    v_pack = _pack([v[n] for n in SMALL] + [zeros_cw, jnp.zeros((1,), F32)])
    d_pack, nm_pack, nv_pack = _adamw_plain(w_pack, total, m_pack, v_pack, name="adamw_small")
    for dst, pk in ((deltas, d_pack), (new_m, nm_pack), (new_v, nv_pack)):
        dst.update(zip(SMALL, _unpack(pk, [SMALL_SHAPES[n] for n in SMALL])))
    for n in SMALL:
        grads[n] = summed[n]
    cw_cols = D_RNN // N_DEV
    grads["conv_w"] = lax.dynamic_slice(summed["conv_w"], (0, my_idx * cw_cols), (CONV_WIDTH, cw_cols))
    deltas["conv_w"], new_m["conv_w"], new_v["conv_w"] = _adamw_plain(
        w["conv_w"], grads["conv_w"], m["conv_w"], v["conv_w"], name="adamw_conv_w")

    return (summed["loss"].reshape(()), grad_x[None], *[grads[n] for n in WEIGHTS], *[deltas[n] for n in WEIGHTS],
            *[new_m[n] for n in WEIGHTS], *[new_v[n] for n in WEIGHTS])
```

```python
import functools
import math

import jax
import jax.numpy as jnp
from jax import lax
from jax.experimental import pallas as pl
from jax.experimental.pallas import tpu as pltpu

F32 = jnp.float32
MXU_DTYPE = jnp.bfloat16
VMEM_LIMIT_BYTES = 56 * 1024 * 1024
N_DEV = 8

D_MODEL = 1024
N_MEM = 256
MEM_HEADS = 4
MEM_HEAD_DIM = 128
MEM_WIDTH = 512
D_RNN = 768
LRU_BLOCK = 64
N_LRU_BLOCKS = 12
LRU_GROUP = 256
N_LRU_GROUPS = 3
CONV_WIDTH = 4
LRU_C = 8.0
DIL_GROUPS = ((128, 1), (512, 4), (2048, 16))
SPAN = 128
DIL_HEADS = 4
DIL_HEAD_DIM = 64
NUM_BUCKETS = 32
MAX_DISTANCE = 2048
D_FF = 4096
D_IN = 7424
EPS = 1e-6
NEG = -1e30
C_XL, C_GATE, C_QKV, C_QM, C_GATES = 0, 768, 1536, 3840, 4352

ADAM_LR = 0.001
ADAM_B1 = 0.9
ADAM_B2 = 0.999
ADAM_EPS = 1e-08
ADAM_WD = 0.01
ADAM_STEP = 10

MESH = pl.DeviceIdType.MESH
GELU_K = math.sqrt(2.0 / math.pi)


def _cparams(sem=None):
    kw = dict(vmem_limit_bytes=VMEM_LIMIT_BYTES)
    if sem is not None:
        kw["dimension_semantics"] = sem
    return pltpu.CompilerParams(**kw)


def _mx(v):
    return v.astype(MXU_DTYPE)


def _dot(a, b, mode="nn"):
    dims = {"nn": (((1,), (0,)), ((), ())), "nt": (((1,), (1,)), ((), ())), "tn": (((0,), (0,)), ((), ()))}[mode]
    return lax.dot_general(_mx(a), _mx(b), dims, preferred_element_type=F32)


def _colsum(v):
    return jnp.sum(v, axis=0, keepdims=True)


def _matmul(a, b, *, M, N, K, mode, bm, bn, bk, name, out_dtypes=(F32,), epilogue=None, extras=(),
            a_off=(0, 0), b_off=(0, 0), j_outer=False, deps=(), parts=None):
    assert M % bm == 0 and N % bn == 0 and K % bk == 0, (name, M, N, K, bm, bn, bk)
    nm, nn, nk = M // bm, N // bn, K // bk

    def ij(f):
        if j_outer:
            return lambda j, i, k: f(i, j, k)
        return f

    if mode == "tn":
        a_spec = pl.BlockSpec((bk, bm), ij(lambda i, j, k: (k + a_off[0], i + a_off[1])))
    else:
        a_spec = pl.BlockSpec((bm, bk), ij(lambda i, j, k: (i + a_off[0], k + a_off[1])))
    if mode == "nt":
        b_spec = pl.BlockSpec((bn, bk), ij(lambda i, j, k: (j + b_off[0], k + b_off[1])))
    else:
        b_spec = pl.BlockSpec((bk, bn), ij(lambda i, j, k: (k + b_off[0], j + b_off[1])))
    ex_specs = [pl.BlockSpec((bm, bn), ij(functools.partial(lambda i, j, k, o: (i + o[0], j + o[1]), o=off)))
                for _, off in extras]
    if parts is None:
        out_dims = (M, N)
        out_spec = pl.BlockSpec((bm, bn), ij(lambda i, j, k: (i, j)))
    elif parts[0] == "rows":
        r = parts[1]
        assert bm % r == 0
        out_dims = (M // r, r, N)
        out_spec = pl.BlockSpec((bm // r, r, bn), ij(lambda i, j, k: (i, 0, j)))
    else:
        c = parts[1]
        assert bn % c == 0
        out_dims = (N // c, M, c)
        out_spec = pl.BlockSpec((bn // c, bm, c), ij(lambda i, j, k: (j, i, 0)))
    n_ex, n_out, n_dep = len(extras), len(out_dtypes), len(deps)

    def body(*refs):
        a_ref, b_ref = refs[0], refs[1]
        ex = refs[2:2 + n_ex]
        outs = refs[2 + n_ex + n_dep:2 + n_ex + n_dep + n_out]
        part = _dot(a_ref[...], b_ref[...], mode)

        def finish(acc):
            vals = epilogue(acc, *[e[...] for e in ex]) if epilogue is not None else (acc,)
            for o, v in zip(outs, vals):
                v = v.astype(o.dtype)
                if parts is None:
                    o[...] = v
                elif parts[0] == "rows":
                    for ch in range(bm // parts[1]):
                        o[ch] = v[ch * parts[1]:(ch + 1) * parts[1], :]
                else:
                    for ch in range(bn // parts[1]):
                        o[ch] = v[:, ch * parts[1]:(ch + 1) * parts[1]]

        if nk == 1:
            finish(part)
        else:
            acc_ref = refs[-1]
            k = pl.program_id(2)

            @pl.when(k == 0)
            def _():
                acc_ref[...] = part

            @pl.when(k > 0)
            def _():
                acc_ref[...] += part

            @pl.when(k == nk - 1)
            def _():
                finish(acc_ref[...])

    grid = (nn, nm, nk) if j_outer else (nm, nn, nk)
    res = pl.pallas_call(
        body, name=name, grid=grid,
        in_specs=[a_spec, b_spec] + ex_specs + [pl.BlockSpec(memory_space=pl.ANY)] * n_dep,
        out_specs=[out_spec] * n_out,
        out_shape=[jax.ShapeDtypeStruct(out_dims, dt) for dt in out_dtypes],
        scratch_shapes=[pltpu.VMEM((bm, bn), F32)] if nk > 1 else [],
        compiler_params=_cparams(("parallel", "parallel", "arbitrary")),
    )(a, b, *[e for e, _ in extras], *deps)
    return res[0] if n_out == 1 else res


def _rmsnorm_fwd(x, g, *, rows, name, bt=512):
    bt = min(bt, rows)

    def body(x_ref, g_ref, o_ref):
        xv = x_ref[...]
        r = lax.rsqrt(jnp.mean(xv * xv, axis=-1, keepdims=True) + EPS)
        o_ref[...] = (xv * r * g_ref[...]).astype(o_ref.dtype)

    return pl.pallas_call(
        body, name=name, grid=(rows // bt,),
        in_specs=[pl.BlockSpec((bt, D_MODEL), lambda i: (i, 0)), pl.BlockSpec((1, D_MODEL), lambda i: (0, 0))],
        out_specs=pl.BlockSpec((bt, D_MODEL), lambda i: (i, 0)),
        out_shape=jax.ShapeDtypeStruct((rows, D_MODEL), MXU_DTYPE),
        compiler_params=_cparams(("parallel",)),
    )(x, g.reshape(1, D_MODEL))


def _rms_bwd_tile(xv, gv, dyv):
    r = lax.rsqrt(jnp.mean(xv * xv, axis=-1, keepdims=True) + EPS)
    w = dyv * gv
    dx = r * w - xv * (r * r * r) * jnp.mean(w * xv, axis=-1, keepdims=True)
    dg = _colsum(dyv * xv * r)
    return dx, dg


def _rmsnorm_bwd(x, g, dy, res, *, rows, name, bt=512, dx_dtypes=(F32,)):
    bt = min(bt, rows)
    has_res = res is not None

    def body(*refs):
        x_ref, g_ref, dy_ref = refs[:3]
        res_ref = refs[3] if has_res else None
        outs = refs[3 + int(has_res):]
        dx, dg = _rms_bwd_tile(x_ref[...], g_ref[...], dy_ref[...])
        if has_res:
            dx = dx + res_ref[...]
        dg_ref = outs[-1]

        @pl.when(pl.program_id(0) == 0)
        def _():
            dg_ref[...] = jnp.zeros_like(dg_ref)

        dg_ref[...] += dg
        for o in outs[:-1]:
            o[...] = dx.astype(o.dtype)

    row_spec = pl.BlockSpec((bt, D_MODEL), lambda i: (i, 0))
    vec_spec = pl.BlockSpec((1, D_MODEL), lambda i: (0, 0))
    ins = [x, g.reshape(1, D_MODEL), dy] + ([res] if has_res else [])
    return pl.pallas_call(
        body, name=name, grid=(rows // bt,),
        in_specs=[row_spec, vec_spec, row_spec] + ([row_spec] if has_res else []),
        out_specs=[row_spec] * len(dx_dtypes) + [vec_spec],
        out_shape=[jax.ShapeDtypeStruct((rows, D_MODEL), dt) for dt in dx_dtypes] + [jax.ShapeDtypeStruct((1, D_MODEL), F32)],
        compiler_params=_cparams(("arbitrary",)),
    )(*ins)


def _loss_head(x2, g, tgt, *, rows, bt=512):
    def body(x_ref, g_ref, t_ref, loss_ref, dx_ref, dxm_ref, dg_ref):
        xv, gv = x_ref[...], g_ref[...]
        r = lax.rsqrt(jnp.mean(xv * xv, axis=-1, keepdims=True) + EPS)
        diff = xv * r * gv - t_ref[...]
        part = jnp.sum(jnp.mean(diff * diff, axis=-1, keepdims=True), axis=0, keepdims=True) * 0.5
        dx, dg = _rms_bwd_tile(xv, gv, diff * (1.0 / D_MODEL))

        @pl.when(pl.program_id(0) == 0)
        def _():
            loss_ref[...] = jnp.zeros_like(loss_ref)
            dg_ref[...] = jnp.zeros_like(dg_ref)

        loss_ref[...] += part
        dg_ref[...] += dg
        dx_ref[...] = dx
        dxm_ref[...] = _mx(dx)

    row_spec = pl.BlockSpec((bt, D_MODEL), lambda i: (i, 0))
    vec_spec = pl.BlockSpec((1, D_MODEL), lambda i: (0, 0))
    return pl.pallas_call(
        body, name="loss_head", grid=(rows // bt,),
        in_specs=[row_spec, vec_spec, row_spec],
        out_specs=[pl.BlockSpec((1, 1), lambda i: (0, 0)), row_spec, row_spec, vec_spec],
        out_shape=[jax.ShapeDtypeStruct((1, 1), F32), jax.ShapeDtypeStruct((rows, D_MODEL), F32),
                   jax.ShapeDtypeStruct((rows, D_MODEL), MXU_DTYPE), jax.ShapeDtypeStruct((1, D_MODEL), F32)],
        compiler_params=_cparams(("arbitrary",)),
    )(x2, g.reshape(1, D_MODEL), tgt)


LRU_T = 256


def _gelu(x):
    t = jnp.tanh(GELU_K * (x + 0.044715 * x * x * x))
    return 0.5 * x * (1.0 + t), t


def _gelu_grad(x, t):
    return 0.5 * (1.0 + t) + 0.5 * x * (1.0 - t * t) * GELU_K * (1.0 + 3.0 * 0.044715 * x * x)


def _softplus_neg(lam):
    z = -lam
    u = jnp.exp(-jnp.abs(z))
    w = 1.0 + u
    l1p = jnp.where(w == 1.0, u, jnp.log(w) * u / jnp.where(w == 1.0, 1.0, w - 1.0))
    return jnp.maximum(z, 0.0) + l1p


def _shift_down(cur, prev8, k, row8):
    y = pltpu.roll(cur, k, 0)
    head = jnp.where(row8 < k, pltpu.roll(prev8, k, 0), y[0:8])
    return jnp.concatenate([head, y[8:]], axis=0)


def _shift_up(cur, next8, k, row8):
    n = cur.shape[0]
    y = pltpu.roll(cur, n - k, 0)
    tail = jnp.where(row8 >= 8 - k, pltpu.roll(next8, 8 - k, 0), y[n - 8:n])
    return jnp.concatenate([y[0:n - 8], tail], axis=0)


def _lru_gates(xl, p8, cw, cb, wa, wx, ba, bx, lam, row8):
    sh = [xl] + [_shift_down(xl, p8, k, row8) for k in (1, 2, 3)]
    xc = cb + cw[3:4] * sh[0] + cw[2:3] * sh[1] + cw[1:2] * sh[2] + cw[0:1] * sh[3]
    r = jax.nn.sigmoid(_dot(xc, wa) + ba)
    i = jax.nn.sigmoid(_dot(xc, wx) + bx)
    sp = _softplus_neg(lam)
    la = -LRU_C * r * sp
    a = jnp.exp(la)
    mult = jnp.sqrt(jnp.tanh(-la) * (a * a + 1.0))
    return dict(sh=sh, xc=xc, r=r, i=i, sp=sp, a=a, mult=mult)


def _lru_specs(n_t, reverse):
    T = LRU_T
    tt = (lambda t: n_t - 1 - t) if reverse else (lambda t: t)
    blk = lambda col0: pl.BlockSpec((T, LRU_GROUP), lambda g, t: (tt(t), col0 + g))
    prev8 = lambda col0: pl.BlockSpec((8, LRU_GROUP), lambda g, t: (jnp.maximum(tt(t) * (T // 8) - 1, 0), col0 + g))
    vec = lambda rows: pl.BlockSpec((rows, LRU_GROUP), lambda g, t: (0, g))
    wbd = pl.BlockSpec((1, LRU_GROUP, LRU_GROUP), lambda g, t: (g, 0, 0))
    return blk, prev8, vec, wbd


def _lru_fwd(proj, conv_w, conv_b, wa_bd, wx_bd, b_a, b_x, lam, *, S):
    T = LRU_T
    n_t = S // T
    blk, _, vec, wbd = _lru_specs(n_t, False)

    def body(xl_ref, gate_ref, cw_ref, cb_ref, wa_ref, wx_ref, ba_ref, bx_ref, lam_ref,
             hl_ref, z_ref, prev8, hcar, a_s, b_s):
        @pl.when(pl.program_id(1) == 0)
        def _():
            prev8[...] = jnp.zeros_like(prev8)
            hcar[...] = jnp.zeros_like(hcar)

        row8 = lax.broadcasted_iota(jnp.int32, (8, LRU_GROUP), 0)
        xl = xl_ref[...]
        q = _lru_gates(xl, prev8[...], cw_ref[...], cb_ref[...], wa_ref[0], wx_ref[0], ba_ref[...], bx_ref[...],
                       lam_ref[...], row8)
        prev8[...] = xl[T - 8:T]
        a_s[...] = q["a"]
        b_s[...] = q["mult"] * q["i"] * q["xc"]

        def step(c, carry):
            off = pl.multiple_of(c * 8, 8)
            A = a_s[pl.ds(off, 8), :]
            B = b_s[pl.ds(off, 8), :]
            for k in (1, 2, 4):
                a_sh = jnp.where(row8 >= k, pltpu.roll(A, k, 0), 1.0)
                b_sh = jnp.where(row8 >= k, pltpu.roll(B, k, 0), 0.0)
                B = A * b_sh + B
                A = A * a_sh
            h = A * carry + B
            hl_ref[pl.ds(off, 8), :] = h
            return h[7:8, :]

        hcar[...] = lax.fori_loop(0, T // 8, step, hcar[...])
        ge, _ = _gelu(gate_ref[...])
        z_ref[...] = (ge * hl_ref[...]).astype(z_ref.dtype)

    return pl.pallas_call(
        body, name="lru_fwd", grid=(N_LRU_GROUPS, n_t),
        in_specs=[blk(C_XL // LRU_GROUP), blk(C_GATE // LRU_GROUP), vec(4), vec(1), wbd, wbd, vec(1), vec(1), vec(1)],
        out_specs=[blk(0), blk(0)],
        out_shape=[jax.ShapeDtypeStruct((S, D_RNN), F32), jax.ShapeDtypeStruct((S, D_RNN), MXU_DTYPE)],
        scratch_shapes=[pltpu.VMEM((8, LRU_GROUP), F32), pltpu.VMEM((1, LRU_GROUP), F32),
                        pltpu.VMEM((T, LRU_GROUP), F32), pltpu.VMEM((T, LRU_GROUP), F32)],
        compiler_params=_cparams(("parallel", "arbitrary")),
    )(proj, proj, conv_w, conv_b, wa_bd, wx_bd, b_a, b_x, lam)


def _lru_bwd(proj, hl, dz, conv_w, conv_b, wa_bd, wx_bd, b_a, b_x, lam, *, S):
    T = LRU_T
    n_t = S // T
    blk, prev8s, vec, wbd = _lru_specs(n_t, True)

    def body(xl_ref, xlp_ref, gate_ref, hl_ref, hlp_ref, dz_ref, cw_ref, cb_ref, wa_ref, wx_ref, ba_ref, bx_ref,
             lam_ref, dxl_ref, dgate_ref, dcw_ref, dcb_ref, dwa_ref, dwx_ref, dba_ref, dbx_ref, dlam_ref,
             next8, gcar, c_s, b_s, l_s):
        t = pl.program_id(1)
        first_chunk = t == n_t - 1

        @pl.when(t == 0)
        def _():
            next8[...] = jnp.zeros_like(next8)
            gcar[...] = jnp.zeros_like(gcar)
            for ref in (dcw_ref, dcb_ref, dwa_ref, dwx_ref, dba_ref, dbx_ref, dlam_ref):
                ref[...] = jnp.zeros_like(ref)

        row8 = lax.broadcasted_iota(jnp.int32, (8, LRU_GROUP), 0)
        rowT = lax.broadcasted_iota(jnp.int32, (T, LRU_GROUP), 0)
        keep = jnp.where(first_chunk, 0.0, 1.0)
        xl = xl_ref[...]
        wa, wx, lam_v = wa_ref[0], wx_ref[0], lam_ref[...]
        q = _lru_gates(xl, xlp_ref[...] * keep, cw_ref[...], cb_ref[...], wa, wx, ba_ref[...], bx_ref[...], lam_v, row8)
        a, mult, r, i, xc, sp = q["a"], q["mult"], q["r"], q["i"], q["xc"], q["sp"]
        hl_v = hl_ref[...]
        dz_v = dz_ref[...]
        gate = gate_ref[...]
        ge, th = _gelu(gate)
        dgate_ref[...] = (dz_v * hl_v * _gelu_grad(gate, th)).astype(dgate_ref.dtype)

        c_s[...] = jnp.where(rowT == T - 1, 0.0, pltpu.roll(a, T - 1, 0))
        b_s[...] = dz_v * ge + jnp.where(rowT == T - 1, gcar[...], 0.0)

        def step(n, carry):
            off = pl.multiple_of((T // 8 - 1 - n) * 8, 8)
            C = c_s[pl.ds(off, 8), :]
            B = b_s[pl.ds(off, 8), :]
            for k in (1, 2, 4):
                c_sh = jnp.where(row8 < 8 - k, pltpu.roll(C, 8 - k, 0), 1.0)
                b_sh = jnp.where(row8 < 8 - k, pltpu.roll(B, 8 - k, 0), 0.0)
                B = B + C * b_sh
                C = C * c_sh
            lam_t = B + C * carry
            l_s[pl.ds(off, 8), :] = lam_t
            return lam_t[0:1, :]

        lax.fori_loop(0, T // 8, step, jnp.zeros((1, LRU_GROUP), F32))
        lmb = l_s[...]
        gcar[...] = a[0:1, :] * lmb[0:1, :]

        h_prev = _shift_down(hl_v, hlp_ref[...] * keep, 1, row8)
        da = lmb * h_prev
        dmult = lmb * i * xc
        di = lmb * mult * xc
        dxc = lmb * mult * i
        dla = da * a - dmult * (a * a) / mult
        dr = dla * (-LRU_C * sp)
        dlam_ref[...] += _colsum(dla * (-LRU_C * r)) * (-jax.nn.sigmoid(-lam_v))
        dpa = dr * r * (1.0 - r)
        dpx = di * i * (1.0 - i)
        dxc = dxc + _dot(dpa, wa, "nt") + _dot(dpx, wx, "nt")
        dwa_ref[0] += _dot(xc, dpa, "tn")
        dwx_ref[0] += _dot(xc, dpx, "tn")
        dba_ref[...] += _colsum(dpa)
        dbx_ref[...] += _colsum(dpx)
        dcb_ref[...] += _colsum(dxc)
        cw = cw_ref[...]
        n8 = next8[...]
        dxl = cw[3:4] * dxc
        for k in (1, 2, 3):
            dxl = dxl + cw[3 - k:4 - k] * _shift_up(dxc, n8, k, row8)
        for k in range(4):
            dcw_ref[3 - k:4 - k, :] += _colsum(dxc * q["sh"][k])
        next8[...] = dxc[0:8]
        dxl_ref[...] = dxl.astype(dxl_ref.dtype)

    res = pl.pallas_call(
        body, name="lru_bwd", grid=(N_LRU_GROUPS, n_t),
        in_specs=[blk(C_XL // LRU_GROUP), prev8s(C_XL // LRU_GROUP), blk(C_GATE // LRU_GROUP), blk(0), prev8s(0), blk(0),
                  vec(4), vec(1), wbd, wbd, vec(1), vec(1), vec(1)],
        out_specs=[blk(0), blk(0), vec(4), vec(1), wbd, wbd, vec(1), vec(1), vec(1)],
        out_shape=[jax.ShapeDtypeStruct((S, D_RNN), MXU_DTYPE), jax.ShapeDtypeStruct((S, D_RNN), MXU_DTYPE),
                   jax.ShapeDtypeStruct((4, D_RNN), F32), jax.ShapeDtypeStruct((1, D_RNN), F32),
                   jax.ShapeDtypeStruct((N_LRU_GROUPS, LRU_GROUP, LRU_GROUP), F32),
                   jax.ShapeDtypeStruct((N_LRU_GROUPS, LRU_GROUP, LRU_GROUP), F32),
                   jax.ShapeDtypeStruct((1, D_RNN), F32), jax.ShapeDtypeStruct((1, D_RNN), F32),
                   jax.ShapeDtypeStruct((1, D_RNN), F32)],
        scratch_shapes=[pltpu.VMEM((8, LRU_GROUP), F32), pltpu.VMEM((1, LRU_GROUP), F32),
                        pltpu.VMEM((T, LRU_GROUP), F32), pltpu.VMEM((T, LRU_GROUP), F32), pltpu.VMEM((T, LRU_GROUP), F32)],
        compiler_params=_cparams(("parallel", "arbitrary")),
    )(proj, proj, proj, hl, hl, dz, conv_w, conv_b, wa_bd, wx_bd, b_a, b_x, lam)
    return res


def _block_diag(w):
    w4 = w.reshape(N_LRU_GROUPS, 4, LRU_BLOCK, 1, LRU_BLOCK)
    eye = jnp.eye(4, dtype=w.dtype).reshape(1, 4, 1, 4, 1)
    return (w4 * eye).reshape(N_LRU_GROUPS, LRU_GROUP, LRU_GROUP)


def _block_diag_extract(wbd):
    w5 = wbd.reshape(N_LRU_GROUPS, 4, LRU_BLOCK, 4, LRU_BLOCK)
    return jnp.stack([w5[:, a, :, a, :] for a in range(4)], axis=1).reshape(N_LRU_BLOCKS, LRU_BLOCK, LRU_BLOCK)


def _t5_bucket(dist):
    max_exact = NUM_BUCKETS // 2
    df = jnp.maximum(dist, 1).astype(jnp.float32)
    large = max_exact + (jnp.log(df / max_exact) / math.log(MAX_DISTANCE / max_exact)
                         * (NUM_BUCKETS - max_exact)).astype(jnp.int32)
    large = jnp.minimum(large, NUM_BUCKETS - 1)
    return jnp.where(dist < max_exact, dist, large)


def _band_offsets():
    qi = jnp.arange(SPAN)[:, None]
    kj = jnp.arange(2 * SPAN)[None, :]
    return qi + SPAN - kj


def _dil_buckets():
    off = _band_offsets()
    return jnp.stack([_t5_bucket(jnp.maximum(off, 0) * dil) for _, dil in DIL_GROUPS]).astype(jnp.int32)


def _dil_bias(rel_bias, buckets):
    def body(tbl_ref, bk_ref, o_ref):
        g = pl.program_id(0)
        qi = lax.broadcasted_iota(jnp.int32, (SPAN, 2 * SPAN), 0)
        kj = lax.broadcasted_iota(jnp.int32, (SPAN, 2 * SPAN), 1)
        off = qi + SPAN - kj
        valid = (off >= 0) & (off <= SPAN)
        bk = bk_ref[0]
        for h in range(DIL_HEADS):
            acc = jnp.zeros((SPAN, 2 * SPAN), F32)
            for b in range(NUM_BUCKETS):
                acc = jnp.where(bk == b, tbl_ref[b, g * DIL_HEADS + h], acc)
            o_ref[0, h] = jnp.where(valid, acc, NEG)

    return pl.pallas_call(
        body, name="dil_bias", grid=(3,),
        in_specs=[pl.BlockSpec(memory_space=pltpu.SMEM), pl.BlockSpec((1, SPAN, 2 * SPAN), lambda g: (g, 0, 0))],
        out_specs=pl.BlockSpec((1, DIL_HEADS, SPAN, 2 * SPAN), lambda g: (g, 0, 0, 0)),
        out_shape=jax.ShapeDtypeStruct((3, DIL_HEADS, SPAN, 2 * SPAN), F32),
        compiler_params=_cparams(("parallel",)),
    )(rel_bias, buckets)


def _dil_bias_bwd(dbias, buckets):
    def body(db_ref, bk_ref, o_ref):
        lane = lax.broadcasted_iota(jnp.int32, (1, 128), 1)
        rows = [jnp.zeros((1, 128), F32) for _ in range(NUM_BUCKETS)]
        for g in range(3):
            bk = bk_ref[g]
            for h in range(DIL_HEADS):
                d = db_ref[g, h]
                for b in range(NUM_BUCKETS):
                    tot = jnp.sum(_colsum(jnp.where(bk == b, d, 0.0)), axis=1, keepdims=True)
                    rows[b] = jnp.where(lane == g * DIL_HEADS + h, tot, rows[b])
        for b in range(NUM_BUCKETS):
            o_ref[b:b + 1, :] = rows[b]

    return pl.pallas_call(
        body, name="dil_bias_bwd",
        out_shape=jax.ShapeDtypeStruct((NUM_BUCKETS, 128), F32),
        compiler_params=_cparams(),
    )(dbias, buckets)


DIL_SUBBLOCKS = (4, 2, 1)


def _dil_layout(g, S):
    dil, m = DIL_GROUPS[g][1], DIL_SUBBLOCKS[g]
    sub = SPAN * dil
    col = [(C_QKV + t * 768 + g * 256) // 128 for t in range(3)]
    return dil, m, sub, S // (sub * m), col


def _residue_rows(b, r, dil):
    return pl.ds(b * SPAN * dil + r, SPAN, stride=dil) if dil > 1 else pl.ds(b * SPAN, SPAN)


def _for_residues(dil, fn):
    if dil <= 4:
        for r in range(dil):
            fn(r)
    else:
        lax.fori_loop(0, dil, lambda r, c: (fn(r), c)[1], 0, unroll=2)


def _pair_scores(qm, k2, bias, first_cols):
    s = _dot(qm, k2, "nt") * (DIL_HEAD_DIM ** -0.5) + bias
    kj = lax.broadcasted_iota(jnp.int32, s.shape, 1)
    return jnp.where(kj < first_cols, NEG, s)


def _dilated_fwd(proj, bias, g, *, S):
    dil, m, sub, nc, (qc, kc, vc) = _dil_layout(g, S)
    R = sub * m
    cur = lambda cb: pl.BlockSpec((R, 128), lambda p, i: (i, cb + p))
    prv = lambda cb: pl.BlockSpec((sub, 128), lambda p, i: (jnp.maximum(i * m - 1, 0), cb + p))
    out = pl.BlockSpec((R, 128), lambda p, i: (i, p))

    def body(q_ref, kp_ref, kc_ref, vp_ref, vc_ref, b_ref, o_ref, lse_ref):
        lane = lax.broadcasted_iota(jnp.int32, (SPAN, 128), 1)
        sels = (lane < DIL_HEAD_DIM, lane >= DIL_HEAD_DIM)
        for b in range(m):
            first_cols = jnp.where(pl.program_id(1) == 0, SPAN, 0) if b == 0 else 0

            def one(r, b=b, first_cols=first_cols):
                rows = _residue_rows(b, r, dil)
                before = (kc_ref, vc_ref, _residue_rows(b - 1, r, dil)) if b else (kp_ref, vp_ref, _residue_rows(0, r, dil))
                q2 = q_ref[rows, :]
                k2 = _mx(jnp.concatenate([before[0][before[2], :], kc_ref[rows, :]], axis=0))
                v2 = _mx(jnp.concatenate([before[1][before[2], :], vc_ref[rows, :]], axis=0))
                stat = jnp.zeros((SPAN, 128), F32)
                o2 = jnp.zeros((SPAN, 128), F32)
                for e in range(2):
                    s = _pair_scores(jnp.where(sels[e], q2, 0.0), k2, b_ref[0, e], first_cols)
                    mx = jnp.max(s, axis=-1, keepdims=True)
                    p = jnp.exp(s - mx)
                    den = jnp.sum(p, axis=-1, keepdims=True)
                    o2 = jnp.where(sels[e], _dot(p, v2) / den, o2)
                    stat = jnp.where(lane == e, mx + jnp.log(den), stat)
                o_ref[rows, :] = o2
                lse_ref[rows, :] = stat

            _for_residues(dil, one)

    return pl.pallas_call(
        body, name=f"dil_fwd{g}", grid=(2, nc),
        in_specs=[cur(qc), prv(kc), cur(kc), prv(vc), cur(vc),
                  pl.BlockSpec((1, 2, SPAN, 2 * SPAN), lambda p, i: (g, p, 0, 0))],
        out_specs=[out, out],
        out_shape=[jax.ShapeDtypeStruct((S, 256), F32), jax.ShapeDtypeStruct((S, 256), F32)],
        compiler_params=_cparams(("parallel", "parallel")),
    )(proj, proj, proj, proj, proj, bias)


def _dilated_bwd(proj, do, lse, delta, bias, g, *, S):
    dil, m, sub, nc, (qc, kc, vc) = _dil_layout(g, S)
    R = sub * m
    cl = lambda i: jnp.minimum(i, nc - 1)
    cur = lambda cb: pl.BlockSpec((R, 128), lambda p, i: (cl(i), cb + p))
    prv = lambda cb: pl.BlockSpec((sub, 128), lambda p, i: (jnp.maximum(cl(i) * m - 1, 0), cb + p))
    kv_out = pl.BlockSpec((R, 128), lambda p, i: (jnp.maximum(i - 1, 0), p))
    scale = DIL_HEAD_DIM ** -0.5

    def body(q_ref, kp_ref, kc_ref, vp_ref, vc_ref, do_ref, lse_ref, dl_ref, b_ref,
             dq_ref, dk_ref, dv_ref, db_ref, dq_s, kc_s, vc_s, kp_s, vp_s, kcar, vcar):
        i = pl.program_id(1)

        @pl.when(i == 0)
        def _():
            kcar[...] = jnp.zeros_like(kcar)
            vcar[...] = jnp.zeros_like(vcar)
            db_ref[...] = jnp.zeros_like(db_ref)

        @pl.when(i < nc)
        def _():
            lane = lax.broadcasted_iota(jnp.int32, (SPAN, 128), 1)
            sels = (lane < DIL_HEAD_DIM, lane >= DIL_HEAD_DIM)
            for b in range(m):
                first_cols = jnp.where(i == 0, SPAN, 0) if b == 0 else 0

                def one(r, b=b, first_cols=first_cols):
                    rows = _residue_rows(b, r, dil)
                    rows_before = _residue_rows(b - 1 if b else 0, r, dil)
                    k_before, v_before = (kc_ref, vc_ref) if b else (kp_ref, vp_ref)
                    q2, do2 = q_ref[rows, :], do_ref[rows, :]
                    k2 = _mx(jnp.concatenate([k_before[rows_before, :], kc_ref[rows, :]], axis=0))
                    v2 = _mx(jnp.concatenate([v_before[rows_before, :], vc_ref[rows, :]], axis=0))
                    lse_t, dl_t = lse_ref[rows, :], dl_ref[rows, :]
                    dq2 = jnp.zeros((SPAN, 128), F32)
                    dk2 = jnp.zeros((2 * SPAN, 128), F32)
                    dv2 = jnp.zeros((2 * SPAN, 128), F32)
                    for e in range(2):
                        qm = jnp.where(sels[e], q2, 0.0)
                        dom = jnp.where(sels[e], do2, 0.0)
                        p = jnp.exp(_pair_scores(qm, k2, b_ref[0, e], first_cols) - lse_t[:, e:e + 1])
                        ds = p * (_dot(dom, v2, "nt") - dl_t[:, e:e + 1])
                        db_ref[e] += ds
                        dq2 = jnp.where(sels[e], _dot(ds, k2) * scale, dq2)
                        dk2 = dk2 + _dot(ds, qm, "tn") * scale
                        dv2 = dv2 + _dot(p, dom, "tn")
                    dq_s[rows, :] = dq2
                    kc_s[rows, :] = dk2[SPAN:2 * SPAN]
                    vc_s[rows, :] = dv2[SPAN:2 * SPAN]
                    if b:
                        kc_s[rows_before, :] += dk2[0:SPAN]
                        vc_s[rows_before, :] += dv2[0:SPAN]
                    else:
                        kp_s[rows_before, :] = dk2[0:SPAN]
                        vp_s[rows_before, :] = dv2[0:SPAN]

                _for_residues(dil, one)
            dq_ref[...] = dq_s[...].astype(dq_ref.dtype)
            last = pl.ds((m - 1) * sub, sub)
            kcar[last, :] += kp_s[...]
            vcar[last, :] += vp_s[...]
            dk_ref[...] = kcar[...].astype(dk_ref.dtype)
            dv_ref[...] = vcar[...].astype(dv_ref.dtype)
            kcar[...] = kc_s[...]
            vcar[...] = vc_s[...]

        @pl.when(i == nc)
        def _():
            dk_ref[...] = kcar[...].astype(dk_ref.dtype)
            dv_ref[...] = vcar[...].astype(dv_ref.dtype)

    stat = pl.BlockSpec((R, 128), lambda p, i: (cl(i), p))
    big = jax.ShapeDtypeStruct((S, 256), MXU_DTYPE)
    return pl.pallas_call(
        body, name=f"dil_bwd{g}", grid=(2, nc + 1),
        in_specs=[cur(qc), prv(kc), cur(kc), prv(vc), cur(vc), stat, stat, stat,
                  pl.BlockSpec((1, 2, SPAN, 2 * SPAN), lambda p, i: (g, p, 0, 0))],
        out_specs=[stat, kv_out, kv_out, pl.BlockSpec((2, SPAN, 2 * SPAN), lambda p, i: (p, 0, 0))],
        out_shape=[big, big, big, jax.ShapeDtypeStruct((DIL_HEADS, SPAN, 2 * SPAN), F32)],
        scratch_shapes=[pltpu.VMEM((R, 128), F32)] * 3 + [pltpu.VMEM((sub, 128), F32)] * 2 + [pltpu.VMEM((R, 128), F32)] * 2,
        compiler_params=_cparams(("parallel", "arbitrary")),
    )(proj, proj, proj, proj, proj, do, lse, delta, bias)


def _dilated_merge(os_, lses, *, S, bt=512):
    tile = pl.BlockSpec((bt, 128), lambda i, p: (i, p))

    def body(o0, o1, o2, l0, l1, l2, o_ref, om_ref, lse_ref):
        lane = lax.broadcasted_iota(jnp.int32, (bt, 128), 1)
        lo = lane < DIL_HEAD_DIM
        ls = [l0[...], l1[...], l2[...]]
        ws, stat = [], jnp.zeros((bt, 128), F32)
        for e in range(2):
            a = [l[:, e:e + 1] for l in ls]
            m = jnp.maximum(jnp.maximum(a[0], a[1]), a[2])
            ex = [jnp.exp(v - m) for v in a]
            tot = ex[0] + ex[1] + ex[2]
            ws.append([v / tot for v in ex])
            stat = jnp.where(lane == e, m + jnp.log(tot), stat)
        acc = jnp.zeros((bt, 128), F32)
        for gi, o in enumerate((o0, o1, o2)):
            acc = acc + jnp.where(lo, ws[0][gi], ws[1][gi]) * o[...]
        o_ref[...] = acc
        om_ref[...] = _mx(acc)
        lse_ref[...] = stat

    return pl.pallas_call(
        body, name="dil_merge", grid=(S // bt, 2),
        in_specs=[tile] * 6, out_specs=[tile, tile, tile],
        out_shape=[jax.ShapeDtypeStruct((S, 256), F32), jax.ShapeDtypeStruct((S, 256), MXU_DTYPE),
                   jax.ShapeDtypeStruct((S, 256), F32)],
        compiler_params=_cparams(("parallel", "parallel")),
    )(*os_, *lses)


def _dilated_delta(do, o, *, S, bt=512):
    tile = pl.BlockSpec((bt, 128), lambda i, p: (i, p))

    def body(do_ref, o_ref, d_ref):
        lane = lax.broadcasted_iota(jnp.int32, (bt, 128), 1)
        prod = do_ref[...] * o_ref[...]
        d0 = jnp.sum(jnp.where(lane < DIL_HEAD_DIM, prod, 0.0), axis=-1, keepdims=True)
        d1 = jnp.sum(jnp.where(lane >= DIL_HEAD_DIM, prod, 0.0), axis=-1, keepdims=True)
        d_ref[...] = jnp.where(lane == 0, d0, jnp.where(lane == 1, d1, 0.0))

    return pl.pallas_call(
        body, name="dil_delta", grid=(S // bt, 2), in_specs=[tile, tile], out_specs=tile,
        out_shape=jax.ShapeDtypeStruct((S, 256), F32), compiler_params=_cparams(("parallel", "parallel")),
    )(do, o)


MEM_T = 512
QM_BLK = C_QM // MEM_HEAD_DIM


def _mem_attn_fwd(proj, kv, *, S):
    scale = MEM_HEAD_DIM ** -0.5

    def body(q_ref, k_ref, v_ref, o_ref, om_ref, lse_ref):
        s = _dot(q_ref[...], k_ref[...], "nt") * scale
        m = jnp.max(s, axis=-1, keepdims=True)
        p = jnp.exp(s - m)
        den = jnp.sum(p, axis=-1, keepdims=True)
        o = _dot(p, v_ref[...]) / den
        o_ref[...] = o
        om_ref[...] = _mx(o)
        lse_ref[0] = m + jnp.log(den)

    return pl.pallas_call(
        body, name="mem_attn_fwd", grid=(S // MEM_T, MEM_HEADS),
        in_specs=[pl.BlockSpec((MEM_T, MEM_HEAD_DIM), lambda i, h: (i, QM_BLK + h)),
                  pl.BlockSpec((N_MEM, MEM_HEAD_DIM), lambda i, h: (0, h)),
                  pl.BlockSpec((N_MEM, MEM_HEAD_DIM), lambda i, h: (0, MEM_HEADS + h))],
        out_specs=[pl.BlockSpec((MEM_T, MEM_HEAD_DIM), lambda i, h: (i, h)),
                   pl.BlockSpec((MEM_T, MEM_HEAD_DIM), lambda i, h: (i, h)),
                   pl.BlockSpec((1, MEM_T, 1), lambda i, h: (h, i, 0))],
        out_shape=[jax.ShapeDtypeStruct((S, MEM_WIDTH), F32), jax.ShapeDtypeStruct((S, MEM_WIDTH), MXU_DTYPE),
                   jax.ShapeDtypeStruct((MEM_HEADS, S, 1), F32)],
        compiler_params=_cparams(("parallel", "parallel")),
    )(proj, kv, kv)


def _mem_attn_bwd(proj, kv, om, lse, dom, *, S):
    scale = MEM_HEAD_DIM ** -0.5

    def body(q_ref, k_ref, v_ref, o_ref, lse_ref, do_ref, dq_ref, dk_ref, dv_ref):
        @pl.when(pl.program_id(1) == 0)
        def _():
            dk_ref[...] = jnp.zeros_like(dk_ref)
            dv_ref[...] = jnp.zeros_like(dv_ref)

        qv, kv_, vv, dov = q_ref[...], k_ref[...], v_ref[...], do_ref[...]
        p = jnp.exp(_dot(qv, kv_, "nt") * scale - lse_ref[0])
        delta = jnp.sum(dov * o_ref[...], axis=-1, keepdims=True)
        ds = p * (_dot(dov, vv, "nt") - delta)
        dq_ref[...] = (_dot(ds, kv_) * scale).astype(dq_ref.dtype)
        dk_ref[...] += _dot(ds, qv, "tn") * scale
        dv_ref[...] += _dot(p, dov, "tn")

    tile = pl.BlockSpec((MEM_T, MEM_HEAD_DIM), lambda h, i: (i, h))
    kvo = pl.BlockSpec((N_MEM, MEM_HEAD_DIM), lambda h, i: (0, h))
    return pl.pallas_call(
        body, name="mem_attn_bwd", grid=(MEM_HEADS, S // MEM_T),
        in_specs=[pl.BlockSpec((MEM_T, MEM_HEAD_DIM), lambda h, i: (i, QM_BLK + h)),
                  pl.BlockSpec((N_MEM, MEM_HEAD_DIM), lambda h, i: (0, h)),
                  pl.BlockSpec((N_MEM, MEM_HEAD_DIM), lambda h, i: (0, MEM_HEADS + h)),
                  tile, pl.BlockSpec((1, MEM_T, 1), lambda h, i: (h, i, 0)), tile],
        out_specs=[tile, kvo, kvo],
        out_shape=[jax.ShapeDtypeStruct((S, MEM_WIDTH), MXU_DTYPE), jax.ShapeDtypeStruct((N_MEM, MEM_WIDTH), F32),
                   jax.ShapeDtypeStruct((N_MEM, MEM_WIDTH), F32)],
        compiler_params=_cparams(("parallel", "arbitrary")),
    )(proj, kv, kv, om, lse, dom)


MIX_BM = 1024
MIX_BN = 256
GATES_BLK = C_GATES // MIX_BN


def _mix_specs(j_outer):
    ix = (lambda f: (lambda j, i: f(i, j))) if j_outer else (lambda f: f)
    act = lambda width: pl.BlockSpec((MIX_BM, width), ix(lambda i, j: (i, 0)))
    wgt = lambda width: pl.BlockSpec((width, MIX_BN), ix(lambda i, j: (0, j)))
    gate = lambda b: pl.BlockSpec((MIX_BM, MIX_BN), ix(lambda i, j: (i, GATES_BLK + 4 * b + j)))
    bias = lambda b: pl.BlockSpec((1, MIX_BN), ix(lambda i, j: (0, 4 * b + j)))
    tile = pl.BlockSpec((MIX_BM, MIX_BN), ix(lambda i, j: (i, j)))
    return act, wgt, gate, bias, tile


def _mix_fwd(z_lru, o_dil, om, w_lru, w_dil, w_mem, proj, b_gate, *, S):
    act, wgt, gate, bias, tile = _mix_specs(False)

    def body(zl, od, mo, wl, wd, wm, g0, g1, g2, b0, b1, b2, o_ref):
        acc = jax.nn.sigmoid(g0[...] + b0[...]) * _dot(zl[...], wl[...])
        acc += jax.nn.sigmoid(g1[...] + b1[...]) * _dot(od[...], wd[...])
        acc += jax.nn.sigmoid(g2[...] + b2[...]) * _dot(mo[...], wm[...])
        o_ref[...] = acc.astype(o_ref.dtype)

    return pl.pallas_call(
        body, name="mix_fwd", grid=(S // MIX_BM, D_MODEL // MIX_BN),
        in_specs=[act(D_RNN), act(256), act(MEM_WIDTH), wgt(D_RNN), wgt(256), wgt(MEM_WIDTH),
                  gate(0), gate(1), gate(2), bias(0), bias(1), bias(2)],
        out_specs=tile, out_shape=jax.ShapeDtypeStruct((S, D_MODEL), MXU_DTYPE),
        compiler_params=_cparams(("parallel", "parallel")),
    )(z_lru, o_dil, om, w_lru, w_dil, w_mem, proj, proj, proj, b_gate, b_gate, b_gate)


def _mix_bwd(dmerged, z_lru, o_dil, om, w_lru, w_dil, w_mem, proj, b_gate, *, S):
    act, wgt, gate, bias, tile = _mix_specs(True)

    def body(dm, zl, od, mo, wl, wd, wm, g0, g1, g2, b0, b1, b2,
             dg0, dg1, dg2, dy0, dy1, dy2, db0, db1, db2):
        @pl.when(pl.program_id(1) == 0)
        def _():
            for r in (db0, db1, db2):
                r[...] = jnp.zeros_like(r)

        dmv = dm[...]
        for act_ref, w_ref, g_ref, b_ref, dg_ref, dy_ref, db_ref in (
                (zl, wl, g0, b0, dg0, dy0, db0), (od, wd, g1, b1, dg1, dy1, db1), (mo, wm, g2, b2, dg2, dy2, db2)):
            y = _dot(act_ref[...], w_ref[...])
            gt = jax.nn.sigmoid(g_ref[...] + b_ref[...])
            dgate = dmv * y * gt * (1.0 - gt)
            dg_ref[...] = dgate.astype(dg_ref.dtype)
            dy_ref[...] = (dmv * gt).astype(dy_ref.dtype)
            db_ref[...] += _colsum(dgate)

    big = jax.ShapeDtypeStruct((S, D_MODEL), MXU_DTYPE)
    vec = jax.ShapeDtypeStruct((1, D_MODEL), F32)
    vspec = pl.BlockSpec((1, MIX_BN), lambda j, i: (0, j))
    return pl.pallas_call(
        body, name="mix_bwd", grid=(D_MODEL // MIX_BN, S // MIX_BM),
        in_specs=[tile, act(D_RNN), act(256), act(MEM_WIDTH), wgt(D_RNN), wgt(256), wgt(MEM_WIDTH),
                  gate(0), gate(1), gate(2), bias(0), bias(1), bias(2)],
        out_specs=[tile] * 6 + [vspec] * 3, out_shape=[big] * 6 + [vec] * 3,
        compiler_params=_cparams(("parallel", "arbitrary")),
    )(dmerged, z_lru, o_dil, om, w_lru, w_dil, w_mem, proj, proj, proj, b_gate, b_gate, b_gate)


def _adamw_math(w, g, m, v):
    m = ADAM_B1 * m + (1.0 - ADAM_B1) * g
    v = ADAM_B2 * v + (1.0 - ADAM_B2) * (g * g)
    m_hat = m / (1.0 - ADAM_B1 ** ADAM_STEP)
    v_hat = v / (1.0 - ADAM_B2 ** ADAM_STEP)
    delta = -ADAM_LR * (m_hat / (jnp.sqrt(v_hat) + ADAM_EPS) + ADAM_WD * w)
    return delta, m, v


def _adamw_landed(w, own, land, m, v, *, name, col_blk=0, prev=None):
    R = w.shape[0]
    n_parts, C = land.shape[0], land.shape[2]
    br = next(d for d in (256, 464, 128) if R % d == 0)
    tile = pl.BlockSpec((br, C), lambda i: (i, col_blk))
    part = pl.BlockSpec((br, C), lambda i: (i, 0))
    n_prev = 0 if prev is None else 4

    def body(w_ref, o_ref, l_ref, m_ref, v_ref, *rest):
        g_ref, d_ref, nm_ref, nv_ref = rest[n_prev:]
        g = o_ref[...].astype(F32)
        for p in range(n_parts):
            g = g + l_ref[p].astype(F32)
        d, nm, nv = _adamw_math(w_ref[...], g, m_ref[...], v_ref[...])
        g_ref[...] = g
        d_ref[...] = d
        nm_ref[...] = nm
        nv_ref[...] = nv

    return pl.pallas_call(
        body, name=name, grid=(R // br,),
        in_specs=[tile, part, pl.BlockSpec((n_parts, br, C), lambda i: (0, i, 0)), tile, tile]
        + [pl.BlockSpec(memory_space=pl.ANY)] * n_prev,
        out_specs=[tile] * 4, out_shape=[jax.ShapeDtypeStruct(w.shape, F32)] * 4,
        input_output_aliases={5 + j: j for j in range(n_prev)},
        compiler_params=_cparams(("parallel",)),
    )(w, own, land, m, v, *(prev or ()))


def _adamw_plain(w, g, m, v, *, name):
    def body(w_ref, g_ref, m_ref, v_ref, d_ref, nm_ref, nv_ref):
        d, nm, nv = _adamw_math(w_ref[...], g_ref[...], m_ref[...], v_ref[...])
        d_ref[...] = d
        nm_ref[...] = nm
        nv_ref[...] = nv

    return pl.pallas_call(
        body, name=name, out_shape=[jax.ShapeDtypeStruct(w.shape, F32)] * 3, compiler_params=_cparams(),
    )(w, g, m, v)


def _my_pos():
    return lax.axis_index("x"), lax.axis_index("y"), lax.axis_index("c")


def _dev_index(p):
    return 4 * p[0] + 2 * p[1] + p[2]


def _all_gather(shards):
    n = len(shards)
    hbm = pl.BlockSpec(memory_space=pl.ANY)

    def body(*refs):
        ins, outs = refs[:n], refs[n:2 * n]
        send_sems, recv_sems, local_sems = refs[2 * n:]
        x, y, c = _my_pos()
        me, sibling = (x, y, c), (x, y, 1 - c)
        chips = [(1 - x, y), (x, 1 - y), (1 - x, 1 - y)]

        def copy(a, k, block, to, src=None):
            dst = outs[a].at[_dev_index(block)]
            return pltpu.make_async_remote_copy(
                src_ref=dst if src is None else src, dst_ref=dst,
                send_sem=send_sems.at[a, k], recv_sem=recv_sems.at[a, k], device_id=to, device_id_type=MESH)

        mine = [pltpu.make_async_copy(ins[a], outs[a].at[_dev_index(me)], local_sems.at[a]) for a in range(n)]
        for cp in mine:
            cp.start()
        first = []
        for a in range(n):
            first.append(copy(a, 0, me, sibling, src=ins[a]))
            first += [copy(a, 1 + j, me, (*chip, c), src=ins[a]) for j, chip in enumerate(chips)]
        for cp in first:
            cp.start()
        passed = []
        for j, chip in enumerate(chips):
            for a in range(n):
                copy(a, 1 + j, (*chip, c), me).wait_recv()
                fwd = copy(a, 4 + j, (*chip, c), sibling)
                fwd.start()
                passed.append(fwd)
        for a in range(n):
            copy(a, 0, sibling, me).wait_recv()
        for j, chip in enumerate(chips):
            for a in range(n):
                copy(a, 4 + j, (*chip, 1 - c), me).wait_recv()
        for cp in first + passed:
            cp.wait_send()
        for cp in mine:
            cp.wait()

    return pl.pallas_call(
        body, name="all_gather_weights",
        in_specs=[hbm] * n, out_specs=[hbm] * n,
        out_shape=[jax.ShapeDtypeStruct((N_DEV,) + s.shape, s.dtype) for s in shards],
        scratch_shapes=[pltpu.SemaphoreType.DMA((n, 7)), pltpu.SemaphoreType.DMA((n, 7)), pltpu.SemaphoreType.DMA((n,))],
        compiler_params=pltpu.CompilerParams(has_side_effects=True),
    )(*shards)


def _peers(me):
    x, y, c = me
    out = []
    for k in range(1, 8):
        fx, fy, fc = (k >> 2) & 1, (k >> 1) & 1, k & 1
        out.append((k - 1, (1 - x if fx else x, 1 - y if fy else y, 1 - c if fc else c)))
    return out


HBM_SPEC = pl.BlockSpec(memory_space=pltpu.HBM)
SEM_SPEC = pl.BlockSpec(memory_space=pltpu.SEMAPHORE)
DATAFLOW_EFFECT = pltpu.SideEffectType.DATAFLOW_SIDE_EFFECTING


def _gather_refs(src, land, me, peer, k):
    return src, land.at[_dev_index(me)]


def _scatter_refs(src, land, me, peer, k):
    return src.at[_dev_index(peer)], land.at[k]


def _push_start(srcs, land_shapes, refs_of, name, after=()):
    n, n_after = len(srcs), len(after)

    def body(*refs):
        ins, lands = refs[:n], refs[n:2 * n]
        send_sems, recv_sems, token = refs[2 * n + n_after], refs[2 * n + n_after + 1], refs[-1]
        me = _my_pos()
        for k, peer in _peers(me):
            for a in range(n):
                src, dst = refs_of(ins[a], lands[a], me, peer, k)
                pltpu.make_async_remote_copy(src_ref=src, dst_ref=dst, send_sem=send_sems.at[7 * a + k],
                                             recv_sem=recv_sems.at[7 * a + k], device_id=peer, device_id_type=MESH).start()
        token[...] = jnp.zeros_like(token)

    lands = [lax.empty(shp, s.dtype) for shp, s in zip(land_shapes, srcs)]
    hbm = lambda a: pltpu.with_memory_space_constraint(a, pltpu.HBM)
    res = pl.pallas_call(
        body, name=name,
        out_shape=(pltpu.SemaphoreType.DMA((7 * n,)), pltpu.SemaphoreType.DMA((7 * n,)),
                   *[pltpu.HBM(s.shape, s.dtype) for s in srcs], *[pltpu.HBM(l.shape, l.dtype) for l in lands],
                   jax.ShapeDtypeStruct((8, 128), F32)),
        in_specs=[HBM_SPEC] * (2 * n) + [pl.BlockSpec(memory_space=pl.ANY)] * n_after,
        out_specs=(SEM_SPEC, SEM_SPEC, *[HBM_SPEC] * (2 * n), pl.BlockSpec(memory_space=pltpu.VMEM)),
        input_output_aliases={i: 2 + i for i in range(2 * n)},
        compiler_params=pltpu.CompilerParams(has_side_effects=DATAFLOW_EFFECT),
    )(*[hbm(s) for s in srcs], *[hbm(l) for l in lands], *after)
    return dict(sems=(res[0], res[1]), srcs=list(res[2:2 + n]), lands=list(res[2 + n:2 + 2 * n]), token=res[-1], n=n,
                refs_of=refs_of, name=name)


def _push_wait(started, after):
    n, refs_of = started["n"], started["refs_of"]

    def body(*refs):
        ins, lands = refs[:n], refs[n:2 * n]
        send_sems, recv_sems = refs[2 * n], refs[2 * n + 1]
        me = _my_pos()
        for k, peer in _peers(me):
            for a in range(n):
                src, dst = refs_of(ins[a], lands[a], me, peer, k)
                cp = pltpu.make_async_remote_copy(src_ref=src, dst_ref=dst, send_sem=send_sems.at[7 * a + k],
                                                  recv_sem=recv_sems.at[7 * a + k], device_id=peer, device_id_type=MESH)
                cp.wait_send()
                cp.wait_recv()

    arrs = started["srcs"] + started["lands"]
    res = pl.pallas_call(
        body, name=started["name"].replace("start", "wait"),
        out_shape=tuple(pltpu.HBM(a.shape, a.dtype) for a in arrs),
        in_specs=[HBM_SPEC] * (2 * n) + [SEM_SPEC, SEM_SPEC, pl.BlockSpec(memory_space=pl.ANY)],
        out_specs=tuple([HBM_SPEC] * (2 * n)),
        input_output_aliases={i: i for i in range(2 * n)},
        compiler_params=pltpu.CompilerParams(has_side_effects=DATAFLOW_EFFECT),
    )(*arrs, *started["sems"], after)
    return list(res[n:2 * n])


def _sum_slots(slots):
    def body(in_ref, out_ref):
        acc = in_ref[0]
        for d in range(1, N_DEV):
            acc = acc + in_ref[d]
        out_ref[...] = acc

    return pl.pallas_call(body, name="sum_small", out_shape=jax.ShapeDtypeStruct(slots.shape[1:], F32),
                          compiler_params=_cparams())(slots)


def _local_step(x, mem, tgt, W, P, late_weights, send_grads, reduce_small, tie0):
    S = x.shape[0]
    W = dict(W)
    h = _rmsnorm_fwd(x, P["g_mix"] + tie0, rows=S, name="norm_mix")
    proj = _matmul(h, W["w_in_t"], M=S, N=D_IN, K=D_MODEL, mode="nt", bm=512, bn=D_IN // 2, bk=D_MODEL, name="mm_in",
                   j_outer=True)

    wa_bd, wx_bd = _mx(_block_diag(P["w_rg_a"])), _mx(_block_diag(P["w_rg_x"]))
    lru_args = (W["conv_w"], P["conv_b"].reshape(1, -1), wa_bd, wx_bd, P["b_rg_a"].reshape(1, -1),
                P["b_rg_x"].reshape(1, -1), P["lru_lambda"].reshape(1, -1))
    hl, z_lru = _lru_fwd(proj, *lru_args, S=S)

    buckets = _dil_buckets()
    bias = _dil_bias(P["rel_bias"], buckets)
    group_out = [_dilated_fwd(proj, bias, g, S=S) for g in range(len(DIL_GROUPS))]
    o_dil, o_dil_m, lse_dil = _dilated_merge([o for o, _ in group_out], [l for _, l in group_out], S=S)

    W.update(late_weights(o_dil))
    mem_n = _rmsnorm_fwd(mem, P["g_mem"], rows=N_MEM, name="norm_mem")
    kv = _matmul(mem_n, W["w_mem_kv"], M=N_MEM, N=2 * MEM_WIDTH, K=D_MODEL, mode="nn", bm=N_MEM, bn=512, bk=D_MODEL,
                 name="mm_kv")
    om, om_m, lse_mem = _mem_attn_fwd(proj, kv, S=S)
    b_gate = P["b_gate"].reshape(1, -1)
    merged = _mix_fwd(z_lru, o_dil_m, om_m, W["w_lru_out"], W["w_dil_out"], W["w_mem_out"], proj, b_gate, S=S)
    x1 = _matmul(merged, W["w_out"], M=S, N=D_MODEL, K=D_MODEL, mode="nn", bm=512, bn=D_MODEL, bk=D_MODEL, name="mm_out",
                 epilogue=lambda acc, r: (r + acc,), extras=[(x, (0, 0))])
    hm = _rmsnorm_fwd(x1, P["g_mlp"], rows=S, name="norm_mlp")

    def relu2(acc):
        rl = jnp.maximum(acc, 0.0)
        return (rl * rl,)

    act = _matmul(hm, W["w_mlp_in"], M=S, N=D_FF, K=D_MODEL, mode="nn", bm=1024, bn=1024, bk=D_MODEL, name="mm_mlp_in",
                  out_dtypes=(MXU_DTYPE,), epilogue=relu2, j_outer=True)
    x2 = _matmul(act, W["w_mlp_out"], M=S, N=D_MODEL, K=D_FF, mode="nn", bm=512, bn=D_MODEL, bk=D_FF, name="mm_mlp_out",
                 epilogue=lambda acc, r: (r + acc,), extras=[(x1, (0, 0))])
    loss, dx2, dx2_m, dg_final = _loss_head(x2, P["g_final"], tgt, rows=S)

    G, Gs = {}, {}
    Gs["g_final"] = dg_final
    dw = dict(mode="tn", K=S, bk=S, out_dtypes=(MXU_DTYPE,))
    G["w_mlp_out"] = _matmul(act, dx2_m, M=D_FF, N=D_MODEL, bm=512, bn=D_MODEL, name="mm_dw_mlp_out",
                             parts=("rows", D_FF // N_DEV), **dw)
    du = _matmul(dx2_m, W["w_mlp_out"], M=S, N=D_FF, K=D_MODEL, mode="nt", bm=1024, bn=1024, bk=D_MODEL, name="mm_du",
                 out_dtypes=(MXU_DTYPE,), epilogue=lambda acc, a: (acc * (2.0 * jnp.sqrt(a.astype(F32))),),
                 extras=[(act, (0, 0))], j_outer=True)
    G["w_mlp_in"] = _matmul(hm, du, M=D_MODEL, N=D_FF, bm=D_MODEL, bn=512, name="mm_dw_mlp_in",
                            parts=("cols", D_FF // N_DEV), **dw)
    tie1 = send_grads({n: G.pop(n) for n in ("w_mlp_out", "w_mlp_in")})
    dhm = _matmul(du, W["w_mlp_in"], M=S, N=D_MODEL, K=D_FF, mode="nt", bm=512, bn=D_MODEL, bk=D_FF, name="mm_dhm",
                  deps=[tie1])
    dx1, dx1_m, Gs["g_mlp"] = _rmsnorm_bwd(x1, P["g_mlp"], dhm, dx2, rows=S, name="norm_mlp_bwd",
                                           dx_dtypes=(F32, MXU_DTYPE))
    G["w_out"] = _matmul(merged, dx1_m, M=D_MODEL, N=D_MODEL, bm=512, bn=D_MODEL, name="mm_dw_out",
                         parts=("rows", D_MODEL // N_DEV), **dw)
    dmerged = _matmul(dx1_m, W["w_out"], M=S, N=D_MODEL, K=D_MODEL, mode="nt", bm=512, bn=D_MODEL, bk=D_MODEL, name="mm_dmerged")
    (dg0, dg1, dg2, dy_lru, dy_dil, dy_mem, db0, db1, db2) = _mix_bwd(
        dmerged, z_lru, o_dil_m, om_m, W["w_lru_out"], W["w_dil_out"], W["w_mem_out"], proj, b_gate, S=S)
    Gs["b_gate"] = jnp.concatenate([db0, db1, db2], axis=1)

    G["w_mem_out"] = _matmul(om_m, dy_mem, M=MEM_WIDTH, N=D_MODEL, bm=MEM_WIDTH, bn=D_MODEL, name="mm_dw_mem_out",
                             parts=("cols", D_MODEL // N_DEV), **dw)
    dom = _matmul(dy_mem, W["w_mem_out"], M=S, N=MEM_WIDTH, K=D_MODEL, mode="nt", bm=512, bn=MEM_WIDTH, bk=D_MODEL,
                  name="mm_dom")
    dqm, dk_mem, dv_mem = _mem_attn_bwd(proj, kv, om, lse_mem, dom, S=S)
    dkv = jnp.concatenate([dk_mem, dv_mem], axis=1)
    G["w_mem_kv"] = _matmul(mem_n, dkv, M=D_MODEL, N=2 * MEM_WIDTH, K=N_MEM, mode="tn", bm=D_MODEL, bn=2 * MEM_WIDTH,
                            bk=N_MEM, name="mm_dw_kv", out_dtypes=(MXU_DTYPE,), parts=("rows", D_MODEL // N_DEV))
    dmem_n = _matmul(dkv, W["w_mem_kv"], M=N_MEM, N=D_MODEL, K=2 * MEM_WIDTH, mode="nt", bm=N_MEM, bn=D_MODEL,
                     bk=2 * MEM_WIDTH, name="mm_dmem")
    (Gs["g_mem"],) = _rmsnorm_bwd(mem, P["g_mem"], dmem_n, None, rows=N_MEM, name="norm_mem_bwd", dx_dtypes=())

    G["w_dil_out"] = _matmul(o_dil_m, dy_dil, M=256, N=D_MODEL, bm=256, bn=D_MODEL, name="mm_dw_dil_out",
                             parts=("cols", D_MODEL // N_DEV), **dw)
    do_dil = _matmul(dy_dil, W["w_dil_out"], M=S, N=256, K=D_MODEL, mode="nt", bm=512, bn=256, bk=D_MODEL, name="mm_do_dil")
    G["w_lru_out"] = _matmul(z_lru, dy_lru, M=D_RNN, N=D_MODEL, bm=D_RNN, bn=D_MODEL, name="mm_dw_lru_out",
                             parts=("cols", D_MODEL // N_DEV), **dw)
    dz = _matmul(dy_lru, W["w_lru_out"], M=S, N=D_RNN, K=D_MODEL, mode="nt", bm=512, bn=D_RNN, bk=D_MODEL, name="mm_dz_lru")
    tie2 = send_grads({n: G.pop(n) for n in ("w_out", "w_mem_out", "w_mem_kv", "w_dil_out", "w_lru_out")})
    bias = bias + tie2[0, 0]
    delta = _dilated_delta(do_dil, o_dil, S=S)
    dq_parts, dk_parts, dv_parts, dbias = [], [], [], []
    for g in range(len(DIL_GROUPS)):
        dq_g, dk_g, dv_g, db_g = _dilated_bwd(proj, do_dil, lse_dil, delta, bias, g, S=S)
        dq_parts.append(dq_g)
        dk_parts.append(dk_g)
        dv_parts.append(dv_g)
        dbias.append(db_g)
    drel = _dil_bias_bwd(jnp.stack(dbias, axis=0), buckets)
    Gs["rel_bias"] = drel[:, :3 * DIL_HEADS]

    dxl, dgl, dcw, dcb, dwa, dwx, dba, dbx, dlam = _lru_bwd(proj, hl, dz, *lru_args, S=S)
    Gs["conv_w"], Gs["conv_b"] = dcw, dcb
    Gs["w_rg_a"], Gs["w_rg_x"] = _block_diag_extract(dwa), _block_diag_extract(dwx)
    Gs["b_rg_a"], Gs["b_rg_x"], Gs["lru_lambda"] = dba, dbx, dlam

    dproj = jnp.concatenate([dxl, dgl] + dq_parts + dk_parts + dv_parts + [dqm, dg0, dg1, dg2], axis=1)
    cols = D_MODEL // W_IN_PIECES
    tie = []
    for q in range(W_IN_PIECES):
        dw_q = _matmul(dproj, h, M=D_IN, N=cols, K=S, mode="tn", bm=D_IN // 2, bn=cols, bk=512, name=f"mm_dw_in_{q}",
                       b_off=(0, q), out_dtypes=(MXU_DTYPE,), parts=("rows", D_IN // N_DEV), deps=tie)
        tie = [send_grads({f"w_in_{q}": dw_q})]
    dh = _matmul(dproj, W["w_in_t"], M=S, N=D_MODEL, K=D_IN, mode="nn", bm=256, bn=D_MODEL, bk=D_IN, name="mm_dh",
                 deps=tie)
    grad_x, Gs["g_mix"] = _rmsnorm_bwd(x, P["g_mix"], dh, dx1, rows=S, name="norm_mix_bwd")
    return grad_x, reduce_small(Gs, loss)


BIG = ("w_in", "w_lru_out", "w_dil_out", "w_mem_kv", "w_mem_out", "w_out", "w_mlp_in", "w_mlp_out")
W_IN_PIECES = 2
COL_SHARDED = ("w_lru_out", "w_dil_out", "w_mem_out", "w_mlp_in")
SMALL = ("g_mix", "b_gate", "conv_b", "w_rg_a", "b_rg_a", "w_rg_x", "b_rg_x", "lru_lambda", "rel_bias", "g_mem",
         "g_mlp", "g_final")
WEIGHTS = ("g_mix", "w_in", "b_gate", "conv_w", "conv_b", "w_rg_a", "b_rg_a", "w_rg_x", "b_rg_x", "lru_lambda",
           "w_lru_out", "rel_bias", "w_dil_out", "g_mem", "w_mem_kv", "w_mem_out", "w_out", "g_mlp", "w_mlp_in",
           "w_mlp_out", "g_final")
SMALL_SHAPES = {"g_mix": (1024,), "b_gate": (3072,), "conv_b": (768,), "w_rg_a": (12, 64, 64), "b_rg_a": (768,),
                "w_rg_x": (12, 64, 64), "b_rg_x": (768,), "lru_lambda": (768,), "rel_bias": (32, 12), "g_mem": (1024,),
                "g_mlp": (1024,), "g_final": (1024,)}


def _gathered_to_full(name, gathered):
    if name in COL_SHARDED:
        n, r, c = gathered.shape
        return gathered.transpose(1, 0, 2).reshape(r, n * c)
    n, r, c = gathered.shape
    return gathered.reshape(n * r, c)


def _pack(parts):
    flat = jnp.concatenate([p.reshape(-1).astype(F32) for p in parts])
    pad = (-flat.shape[0]) % 1024
    return jnp.pad(flat, (0, pad)).reshape(-1, 128)


def _unpack(pack, shapes):
    flat = pack.reshape(-1)
    out, off = [], 0
    for shp in shapes:
        size = math.prod(shp)
        out.append(flat[off:off + size].reshape(shp))
        off += size
    return out


def kernel(x, mem, g_mix, w_in, b_gate, conv_w, conv_b, w_rg_a, b_rg_a, w_rg_x, b_rg_x, lru_lambda, w_lru_out, rel_bias, w_dil_out, g_mem, w_mem_kv, w_mem_out, w_out, g_mlp, w_mlp_in, w_mlp_out, g_final, loss_target, m_g_mix, m_w_in, m_b_gate, m_conv_w, m_conv_b, m_w_rg_a, m_b_rg_a, m_w_rg_x, m_b_rg_x, m_lru_lambda, m_w_lru_out, m_rel_bias, m_w_dil_out, m_g_mem, m_w_mem_kv, m_w_mem_out, m_w_out, m_g_mlp, m_w_mlp_in, m_w_mlp_out, m_g_final, v_g_mix, v_w_in, v_b_gate, v_conv_w, v_conv_b, v_w_rg_a, v_b_rg_a, v_w_rg_x, v_b_rg_x, v_lru_lambda, v_w_lru_out, v_rel_bias, v_w_dil_out, v_g_mem, v_w_mem_kv, v_w_mem_out, v_w_out, v_g_mlp, v_w_mlp_in, v_w_mlp_out, v_g_final):
    w = dict(g_mix=g_mix, w_in=w_in, b_gate=b_gate, conv_w=conv_w, conv_b=conv_b, w_rg_a=w_rg_a, b_rg_a=b_rg_a,
             w_rg_x=w_rg_x, b_rg_x=b_rg_x, lru_lambda=lru_lambda, w_lru_out=w_lru_out, rel_bias=rel_bias,
             w_dil_out=w_dil_out, g_mem=g_mem, w_mem_kv=w_mem_kv, w_mem_out=w_mem_out, w_out=w_out, g_mlp=g_mlp,
             w_mlp_in=w_mlp_in, w_mlp_out=w_mlp_out, g_final=g_final)
    m = dict(g_mix=m_g_mix, w_in=m_w_in, b_gate=m_b_gate, conv_w=m_conv_w, conv_b=m_conv_b, w_rg_a=m_w_rg_a,
             b_rg_a=m_b_rg_a, w_rg_x=m_w_rg_x, b_rg_x=m_b_rg_x, lru_lambda=m_lru_lambda, w_lru_out=m_w_lru_out,
             rel_bias=m_rel_bias, w_dil_out=m_w_dil_out, g_mem=m_g_mem, w_mem_kv=m_w_mem_kv, w_mem_out=m_w_mem_out,
             w_out=m_w_out, g_mlp=m_g_mlp, w_mlp_in=m_w_mlp_in, w_mlp_out=m_w_mlp_out, g_final=m_g_final)
    v = dict(g_mix=v_g_mix, w_in=v_w_in, b_gate=v_b_gate, conv_w=v_conv_w, conv_b=v_conv_b, w_rg_a=v_w_rg_a,
             b_rg_a=v_b_rg_a, w_rg_x=v_w_rg_x, b_rg_x=v_b_rg_x, lru_lambda=v_lru_lambda, w_lru_out=v_w_lru_out,
             rel_bias=v_rel_bias, w_dil_out=v_w_dil_out, g_mem=v_g_mem, w_mem_kv=v_w_mem_kv, w_mem_out=v_w_mem_out,
             w_out=v_w_out, g_mlp=v_g_mlp, w_mlp_in=v_w_mlp_in, w_mlp_out=v_w_mlp_out, g_final=v_g_final)

    my_idx = _dev_index(_my_pos())

    g_in, g_cw = _all_gather([_mx(w["w_in"].T), w["conv_w"]])
    W = {"w_in_t": g_in.reshape(D_IN, D_MODEL), "conv_w": g_cw.transpose(1, 0, 2).reshape(CONV_WIDTH, D_RNN)}
    late = [n for n in BIG if n != "w_in"]
    late_shards = [_mx(w[n]) for n in late]
    late_started = _push_start(late_shards, [(N_DEV,) + s.shape for s in late_shards], _gather_refs,
                                "gather_late_start", after=[g_in])
    P = {n: w[n] for n in SMALL}

    def late_weights(after):
        lands = _push_wait(late_started, after)
        out = {}
        for n, land, own in zip(late, lands, late_shards):
            full = lax.dynamic_update_index_in_dim(land, own, my_idx, 0)
            out[n] = _gathered_to_full(n, full)
        return out

    sent, small = [], {}
    small_names = SMALL + ("conv_w",)

    def send_grads(gs):
        names = list(gs)
        parts = [gs[n] for n in names]
        own = [lax.dynamic_index_in_dim(p, my_idx, 0, keepdims=False) for p in parts]
        started = _push_start(parts, [(N_DEV - 1,) + p.shape[1:] for p in parts], _scatter_refs,
                              f"scatter{len(sent)}_start")
        sent.append((names, own, started))
        return started["token"]

    def reduce_small(gs, loss):
        pack = _pack([gs[n] for n in small_names] + [loss])
        small["pack"] = pack
        small["started"] = _push_start([pack], [(N_DEV,) + pack.shape], _gather_refs, "small_start")
        return small["started"]["token"]

    grad_x, last_token = _local_step(x[0], mem[0], loss_target[0], W, P, late_weights, send_grads, reduce_small,
                                     late_started["token"][0, 0])

    grads, deltas, new_m, new_v = {}, {}, {}, {}
    after = last_token
    for names, own, started in sent[:-W_IN_PIECES]:
        for n, o, land in zip(names, own, _push_wait(started, after)):
            grads[n], deltas[n], new_m[n], new_v[n] = _adamw_landed(w[n], o, land, m[n], v[n], name=f"adamw_{n}")
            after = deltas[n]
    (small_land,) = _push_wait(small["started"], after)
    total = _sum_slots(lax.dynamic_update_index_in_dim(small_land, small["pack"], my_idx, 0))
    small_shapes = [SMALL_SHAPES[n] for n in SMALL] + [(CONV_WIDTH, D_RNN), (1,)]
    summed = dict(zip(small_names + ("loss",), _unpack(total, small_shapes)))
    after = total
    prev = None
    for q, (names, own, started) in enumerate(sent[-W_IN_PIECES:]):
        (land,) = _push_wait(started, after)
        prev = _adamw_landed(w["w_in"].T, own[0], land, m["w_in"].T, v["w_in"].T, name=f"adamw_{names[0]}",
                             col_blk=q, prev=prev)
    grads["w_in"], deltas["w_in"], new_m["w_in"], new_v["w_in"] = [t.T for t in prev]
    zeros_cw = jnp.zeros((CONV_WIDTH, D_RNN), F32)
    w_pack = _pack([w[n] for n in SMALL] + [zeros_cw, jnp.zeros((1,), F32)])
    m_pack = _pack([m[n] for n in SMALL] + [zeros_cw, jnp.zeros((1,), F32)])
    v_pack = _pack([v[n] for n in SMALL] + [zeros_cw, jnp.zeros((1,), F32)])
    d_pack, nm_pack, nv_pack = _adamw_plain(w_pack, total, m_pack, v_pack, name="adamw_small")
    for dst, pk in ((deltas, d_pack), (new_m, nm_pack), (new_v, nv_pack)):
        dst.update(zip(SMALL, _unpack(pk, [SMALL_SHAPES[n] for n in SMALL])))
    for n in SMALL:
        grads[n] = summed[n]
    cw_cols = D_RNN // N_DEV
    grads["conv_w"] = lax.dynamic_slice(summed["conv_w"], (0, my_idx * cw_cols), (CONV_WIDTH, cw_cols))
    deltas["conv_w"], new_m["conv_w"], new_v["conv_w"] = _adamw_plain(
        w["conv_w"], grads["conv_w"], m["conv_w"], v["conv_w"], name="adamw_conv_w")

    return (summed["loss"].reshape(()), grad_x[None], *[grads[n] for n in WEIGHTS], *[deltas[n] for n in WEIGHTS],
            *[new_m[n] for n in WEIGHTS], *[new_v[n] for n in WEIGHTS])
```

```python
import functools
import math

import jax
import jax.numpy as jnp
from jax import lax
from jax.experimental import pallas as pl
from jax.experimental.pallas import tpu as pltpu

F32 = jnp.float32
MXU_DTYPE = jnp.bfloat16
VMEM_LIMIT_BYTES = 56 * 1024 * 1024
N_DEV = 8

D_MODEL = 1024
N_MEM = 256
MEM_HEADS = 4
MEM_HEAD_DIM = 128
MEM_WIDTH = 512
D_RNN = 768
LRU_BLOCK = 64
N_LRU_BLOCKS = 12
LRU_GROUP = 256
N_LRU_GROUPS = 3
CONV_WIDTH = 4
LRU_C = 8.0
DIL_GROUPS = ((128, 1), (512, 4), (2048, 16))
SPAN = 128
DIL_HEADS = 4
DIL_HEAD_DIM = 64
NUM_BUCKETS = 32
MAX_DISTANCE = 2048
D_FF = 4096
D_IN = 7424
EPS = 1e-6
NEG = -1e30
C_XL, C_GATE, C_QKV, C_QM, C_GATES = 0, 768, 1536, 3840, 4352

ADAM_LR = 0.001
ADAM_B1 = 0.9
ADAM_B2 = 0.999
ADAM_EPS = 1e-08
ADAM_WD = 0.01
ADAM_STEP = 10

MESH = pl.DeviceIdType.MESH
GELU_K = math.sqrt(2.0 / math.pi)


def _cparams(sem=None):
    kw = dict(vmem_limit_bytes=VMEM_LIMIT_BYTES)
    if sem is not None:
        kw["dimension_semantics"] = sem
    return pltpu.CompilerParams(**kw)


def _mx(v):
    return v.astype(MXU_DTYPE)


def _dot(a, b, mode="nn"):
    dims = {"nn": (((1,), (0,)), ((), ())), "nt": (((1,), (1,)), ((), ())), "tn": (((0,), (0,)), ((), ()))}[mode]
    return lax.dot_general(_mx(a), _mx(b), dims, preferred_element_type=F32)


def _colsum(v):
    return jnp.sum(v, axis=0, keepdims=True)


def _matmul(a, b, *, M, N, K, mode, bm, bn, bk, name, out_dtypes=(F32,), epilogue=None, extras=(),
            a_off=(0, 0), b_off=(0, 0), j_outer=False, deps=(), parts=None):
    assert M % bm == 0 and N % bn == 0 and K % bk == 0, (name, M, N, K, bm, bn, bk)
    nm, nn, nk = M // bm, N // bn, K // bk

    def ij(f):
        if j_outer:
            return lambda j, i, k: f(i, j, k)
        return f

    if mode == "tn":
        a_spec = pl.BlockSpec((bk, bm), ij(lambda i, j, k: (k + a_off[0], i + a_off[1])))
    else:
        a_spec = pl.BlockSpec((bm, bk), ij(lambda i, j, k: (i + a_off[0], k + a_off[1])))
    if mode == "nt":
        b_spec = pl.BlockSpec((bn, bk), ij(lambda i, j, k: (j + b_off[0], k + b_off[1])))
    else:
        b_spec = pl.BlockSpec((bk, bn), ij(lambda i, j, k: (k + b_off[0], j + b_off[1])))
    ex_specs = [pl.BlockSpec((bm, bn), ij(functools.partial(lambda i, j, k, o: (i + o[0], j + o[1]), o=off)))
                for _, off in extras]
    if parts is None:
        out_dims = (M, N)
        out_spec = pl.BlockSpec((bm, bn), ij(lambda i, j, k: (i, j)))
    elif parts[0] == "rows":
        r = parts[1]
        assert bm % r == 0
        out_dims = (M // r, r, N)
        out_spec = pl.BlockSpec((bm // r, r, bn), ij(lambda i, j, k: (i, 0, j)))
    else:
        c = parts[1]
        assert bn % c == 0
        out_dims = (N // c, M, c)
        out_spec = pl.BlockSpec((bn // c, bm, c), ij(lambda i, j, k: (j, i, 0)))
    n_ex, n_out, n_dep = len(extras), len(out_dtypes), len(deps)

    def body(*refs):
        a_ref, b_ref = refs[0], refs[1]
        ex = refs[2:2 + n_ex]
        outs = refs[2 + n_ex + n_dep:2 + n_ex + n_dep + n_out]
        part = _dot(a_ref[...], b_ref[...], mode)

        def finish(acc):
            vals = epilogue(acc, *[e[...] for e in ex]) if epilogue is not None else (acc,)
            for o, v in zip(outs, vals):
                v = v.astype(o.dtype)
                if parts is None:
                    o[...] = v
                elif parts[0] == "rows":
                    for ch in range(bm // parts[1]):
                        o[ch] = v[ch * parts[1]:(ch + 1) * parts[1], :]
                else:
                    for ch in range(bn // parts[1]):
                        o[ch] = v[:, ch * parts[1]:(ch + 1) * parts[1]]

        if nk == 1:
            finish(part)
        else:
            acc_ref = refs[-1]
            k = pl.program_id(2)

            @pl.when(k == 0)
            def _():
                acc_ref[...] = part

            @pl.when(k > 0)
            def _():
                acc_ref[...] += part

            @pl.when(k == nk - 1)
            def _():
                finish(acc_ref[...])

    grid = (nn, nm, nk) if j_outer else (nm, nn, nk)
    res = pl.pallas_call(
        body, name=name, grid=grid,
        in_specs=[a_spec, b_spec] + ex_specs + [pl.BlockSpec(memory_space=pl.ANY)] * n_dep,
        out_specs=[out_spec] * n_out,
        out_shape=[jax.ShapeDtypeStruct(out_dims, dt) for dt in out_dtypes],
        scratch_shapes=[pltpu.VMEM((bm, bn), F32)] if nk > 1 else [],
        compiler_params=_cparams(("parallel", "parallel", "arbitrary")),
    )(a, b, *[e for e, _ in extras], *deps)
    return res[0] if n_out == 1 else res


def _rmsnorm_fwd(x, g, *, rows, name, bt=512):
    bt = min(bt, rows)

    def body(x_ref, g_ref, o_ref):
        xv = x_ref[...]
        r = lax.rsqrt(jnp.mean(xv * xv, axis=-1, keepdims=True) + EPS)
        o_ref[...] = (xv * r * g_ref[...]).astype(o_ref.dtype)

    return pl.pallas_call(
        body, name=name, grid=(rows // bt,),
        in_specs=[pl.BlockSpec((bt, D_MODEL), lambda i: (i, 0)), pl.BlockSpec((1, D_MODEL), lambda i: (0, 0))],
        out_specs=pl.BlockSpec((bt, D_MODEL), lambda i: (i, 0)),
        out_shape=jax.ShapeDtypeStruct((rows, D_MODEL), MXU_DTYPE),
        compiler_params=_cparams(("parallel",)),
    )(x, g.reshape(1, D_MODEL))


def _rms_bwd_tile(xv, gv, dyv):
    r = lax.rsqrt(jnp.mean(xv * xv, axis=-1, keepdims=True) + EPS)
    w = dyv * gv
    dx = r * w - xv * (r * r * r) * jnp.mean(w * xv, axis=-1, keepdims=True)
    dg = _colsum(dyv * xv * r)
    return dx, dg


def _rmsnorm_bwd(x, g, dy, res, *, rows, name, bt=512, dx_dtypes=(F32,)):
    bt = min(bt, rows)
    has_res = res is not None

    def body(*refs):
        x_ref, g_ref, dy_ref = refs[:3]
        res_ref = refs[3] if has_res else None
        outs = refs[3 + int(has_res):]
        dx, dg = _rms_bwd_tile(x_ref[...], g_ref[...], dy_ref[...])
        if has_res:
            dx = dx + res_ref[...]
        dg_ref = outs[-1]

        @pl.when(pl.program_id(0) == 0)
        def _():
            dg_ref[...] = jnp.zeros_like(dg_ref)

        dg_ref[...] += dg
        for o in outs[:-1]:
            o[...] = dx.astype(o.dtype)

    row_spec = pl.BlockSpec((bt, D_MODEL), lambda i: (i, 0))
    vec_spec = pl.BlockSpec((1, D_MODEL), lambda i: (0, 0))
    ins = [x, g.reshape(1, D_MODEL), dy] + ([res] if has_res else [])
    return pl.pallas_call(
        body, name=name, grid=(rows // bt,),
        in_specs=[row_spec, vec_spec, row_spec] + ([row_spec] if has_res else []),
        out_specs=[row_spec] * len(dx_dtypes) + [vec_spec],
        out_shape=[jax.ShapeDtypeStruct((rows, D_MODEL), dt) for dt in dx_dtypes] + [jax.ShapeDtypeStruct((1, D_MODEL), F32)],
        compiler_params=_cparams(("arbitrary",)),
    )(*ins)


def _loss_head(x2, g, tgt, *, rows, bt=512):
    def body(x_ref, g_ref, t_ref, loss_ref, dx_ref, dxm_ref, dg_ref):
        xv, gv = x_ref[...], g_ref[...]
        r = lax.rsqrt(jnp.mean(xv * xv, axis=-1, keepdims=True) + EPS)
        diff = xv * r * gv - t_ref[...]
        part = jnp.sum(jnp.mean(diff * diff, axis=-1, keepdims=True), axis=0, keepdims=True) * 0.5
        dx, dg = _rms_bwd_tile(xv, gv, diff * (1.0 / D_MODEL))

        @pl.when(pl.program_id(0) == 0)
        def _():
            loss_ref[...] = jnp.zeros_like(loss_ref)
            dg_ref[...] = jnp.zeros_like(dg_ref)

        loss_ref[...] += part
        dg_ref[...] += dg
        dx_ref[...] = dx
        dxm_ref[...] = _mx(dx)

    row_spec = pl.BlockSpec((bt, D_MODEL), lambda i: (i, 0))
    vec_spec = pl.BlockSpec((1, D_MODEL), lambda i: (0, 0))
    return pl.pallas_call(
        body, name="loss_head", grid=(rows // bt,),
        in_specs=[row_spec, vec_spec, row_spec],
        out_specs=[pl.BlockSpec((1, 1), lambda i: (0, 0)), row_spec, row_spec, vec_spec],
        out_shape=[jax.ShapeDtypeStruct((1, 1), F32), jax.ShapeDtypeStruct((rows, D_MODEL), F32),
                   jax.ShapeDtypeStruct((rows, D_MODEL), MXU_DTYPE), jax.ShapeDtypeStruct((1, D_MODEL), F32)],
        compiler_params=_cparams(("arbitrary",)),
    )(x2, g.reshape(1, D_MODEL), tgt)


LRU_T = 256


def _gelu(x):
    t = jnp.tanh(GELU_K * (x + 0.044715 * x * x * x))
    return 0.5 * x * (1.0 + t), t


def _gelu_grad(x, t):
    return 0.5 * (1.0 + t) + 0.5 * x * (1.0 - t * t) * GELU_K * (1.0 + 3.0 * 0.044715 * x * x)


def _softplus_neg(lam):
    z = -lam
    u = jnp.exp(-jnp.abs(z))
    w = 1.0 + u
    l1p = jnp.where(w == 1.0, u, jnp.log(w) * u / jnp.where(w == 1.0, 1.0, w - 1.0))
    return jnp.maximum(z, 0.0) + l1p


def _shift_down(cur, prev8, k, row8):
    y = pltpu.roll(cur, k, 0)
    head = jnp.where(row8 < k, pltpu.roll(prev8, k, 0), y[0:8])
    return jnp.concatenate([head, y[8:]], axis=0)


def _shift_up(cur, next8, k, row8):
    n = cur.shape[0]
    y = pltpu.roll(cur, n - k, 0)
    tail = jnp.where(row8 >= 8 - k, pltpu.roll(next8, 8 - k, 0), y[n - 8:n])
    return jnp.concatenate([y[0:n - 8], tail], axis=0)


def _lru_gates(xl, p8, cw, cb, wa, wx, ba, bx, lam, row8):
    sh = [xl] + [_shift_down(xl, p8, k, row8) for k in (1, 2, 3)]
    xc = cb + cw[3:4] * sh[0] + cw[2:3] * sh[1] + cw[1:2] * sh[2] + cw[0:1] * sh[3]
    r = jax.nn.sigmoid(_dot(xc, wa) + ba)
    i = jax.nn.sigmoid(_dot(xc, wx) + bx)
    sp = _softplus_neg(lam)
    la = -LRU_C * r * sp
    a = jnp.exp(la)
    mult = jnp.sqrt(jnp.tanh(-la) * (a * a + 1.0))
    return dict(sh=sh, xc=xc, r=r, i=i, sp=sp, a=a, mult=mult)


def _lru_specs(n_t, reverse):
    T = LRU_T
    tt = (lambda t: n_t - 1 - t) if reverse else (lambda t: t)
    blk = lambda col0: pl.BlockSpec((T, LRU_GROUP), lambda g, t: (tt(t), col0 + g))
    prev8 = lambda col0: pl.BlockSpec((8, LRU_GROUP), lambda g, t: (jnp.maximum(tt(t) * (T // 8) - 1, 0), col0 + g))
    vec = lambda rows: pl.BlockSpec((rows, LRU_GROUP), lambda g, t: (0, g))
    wbd = pl.BlockSpec((1, LRU_GROUP, LRU_GROUP), lambda g, t: (g, 0, 0))
    return blk, prev8, vec, wbd


def _lru_fwd(proj, conv_w, conv_b, wa_bd, wx_bd, b_a, b_x, lam, *, S):
    T = LRU_T
    n_t = S // T
    blk, _, vec, wbd = _lru_specs(n_t, False)

    def body(xl_ref, gate_ref, cw_ref, cb_ref, wa_ref, wx_ref, ba_ref, bx_ref, lam_ref,
             hl_ref, z_ref, prev8, hcar, a_s, b_s):
        @pl.when(pl.program_id(1) == 0)
        def _():
            prev8[...] = jnp.zeros_like(prev8)
            hcar[...] = jnp.zeros_like(hcar)

        row8 = lax.broadcasted_iota(jnp.int32, (8, LRU_GROUP), 0)
        xl = xl_ref[...]
        q = _lru_gates(xl, prev8[...], cw_ref[...], cb_ref[...], wa_ref[0], wx_ref[0], ba_ref[...], bx_ref[...],
                       lam_ref[...], row8)
        prev8[...] = xl[T - 8:T]
        a_s[...] = q["a"]
        b_s[...] = q["mult"] * q["i"] * q["xc"]

        def step(c, carry):
            off = pl.multiple_of(c * 8, 8)
            A = a_s[pl.ds(off, 8), :]
            B = b_s[pl.ds(off, 8), :]
            for k in (1, 2, 4):
                a_sh = jnp.where(row8 >= k, pltpu.roll(A, k, 0), 1.0)
                b_sh = jnp.where(row8 >= k, pltpu.roll(B, k, 0), 0.0)
                B = A * b_sh + B
                A = A * a_sh
            h = A * carry + B
            hl_ref[pl.ds(off, 8), :] = h
            return h[7:8, :]

        hcar[...] = lax.fori_loop(0, T // 8, step, hcar[...])
        ge, _ = _gelu(gate_ref[...])
        z_ref[...] = (ge * hl_ref[...]).astype(z_ref.dtype)

    return pl.pallas_call(
        body, name="lru_fwd", grid=(N_LRU_GROUPS, n_t),
        in_specs=[blk(C_XL // LRU_GROUP), blk(C_GATE // LRU_GROUP), vec(4), vec(1), wbd, wbd, vec(1), vec(1), vec(1)],
        out_specs=[blk(0), blk(0)],
        out_shape=[jax.ShapeDtypeStruct((S, D_RNN), F32), jax.ShapeDtypeStruct((S, D_RNN), MXU_DTYPE)],
        scratch_shapes=[pltpu.VMEM((8, LRU_GROUP), F32), pltpu.VMEM((1, LRU_GROUP), F32),
                        pltpu.VMEM((T, LRU_GROUP), F32), pltpu.VMEM((T, LRU_GROUP), F32)],
        compiler_params=_cparams(("parallel", "arbitrary")),
    )(proj, proj, conv_w, conv_b, wa_bd, wx_bd, b_a, b_x, lam)


def _lru_bwd(proj, hl, dz, conv_w, conv_b, wa_bd, wx_bd, b_a, b_x, lam, *, S):
    T = LRU_T
    n_t = S // T
    blk, prev8s, vec, wbd = _lru_specs(n_t, True)

    def body(xl_ref, xlp_ref, gate_ref, hl_ref, hlp_ref, dz_ref, cw_ref, cb_ref, wa_ref, wx_ref, ba_ref, bx_ref,
             lam_ref, dxl_ref, dgate_ref, dcw_ref, dcb_ref, dwa_ref, dwx_ref, dba_ref, dbx_ref, dlam_ref,
             next8, gcar, c_s, b_s, l_s):
        t = pl.program_id(1)
        first_chunk = t == n_t - 1

        @pl.when(t == 0)
        def _():
            next8[...] = jnp.zeros_like(next8)
            gcar[...] = jnp.zeros_like(gcar)
            for ref in (dcw_ref, dcb_ref, dwa_ref, dwx_ref, dba_ref, dbx_ref, dlam_ref):
                ref[...] = jnp.zeros_like(ref)

        row8 = lax.broadcasted_iota(jnp.int32, (8, LRU_GROUP), 0)
        rowT = lax.broadcasted_iota(jnp.int32, (T, LRU_GROUP), 0)
        keep = jnp.where(first_chunk, 0.0, 1.0)
        xl = xl_ref[...]
        wa, wx, lam_v = wa_ref[0], wx_ref[0], lam_ref[...]
        q = _lru_gates(xl, xlp_ref[...] * keep, cw_ref[...], cb_ref[...], wa, wx, ba_ref[...], bx_ref[...], lam_v, row8)
        a, mult, r, i, xc, sp = q["a"], q["mult"], q["r"], q["i"], q["xc"], q["sp"]
        hl_v = hl_ref[...]
        dz_v = dz_ref[...]
        gate = gate_ref[...]
        ge, th = _gelu(gate)
        dgate_ref[...] = (dz_v * hl_v * _gelu_grad(gate, th)).astype(dgate_ref.dtype)

        c_s[...] = jnp.where(rowT == T - 1, 0.0, pltpu.roll(a, T - 1, 0))
        b_s[...] = dz_v * ge + jnp.where(rowT == T - 1, gcar[...], 0.0)

        def step(n, carry):
            off = pl.multiple_of((T // 8 - 1 - n) * 8, 8)
            C = c_s[pl.ds(off, 8), :]
            B = b_s[pl.ds(off, 8), :]
            for k in (1, 2, 4):
                c_sh = jnp.where(row8 < 8 - k, pltpu.roll(C, 8 - k, 0), 1.0)
                b_sh = jnp.where(row8 < 8 - k, pltpu.roll(B, 8 - k, 0), 0.0)
                B = B + C * b_sh
                C = C * c_sh
            lam_t = B + C * carry
            l_s[pl.ds(off, 8), :] = lam_t
            return lam_t[0:1, :]

        lax.fori_loop(0, T // 8, step, jnp.zeros((1, LRU_GROUP), F32))
        lmb = l_s[...]
        gcar[...] = a[0:1, :] * lmb[0:1, :]

        h_prev = _shift_down(hl_v, hlp_ref[...] * keep, 1, row8)
        da = lmb * h_prev
        dmult = lmb * i * xc
        di = lmb * mult * xc
        dxc = lmb * mult * i
        dla = da * a - dmult * (a * a) / mult
        dr = dla * (-LRU_C * sp)
        dlam_ref[...] += _colsum(dla * (-LRU_C * r)) * (-jax.nn.sigmoid(-lam_v))
        dpa = dr * r * (1.0 - r)
        dpx = di * i * (1.0 - i)
        dxc = dxc + _dot(dpa, wa, "nt") + _dot(dpx, wx, "nt")
        dwa_ref[0] += _dot(xc, dpa, "tn")
        dwx_ref[0] += _dot(xc, dpx, "tn")
        dba_ref[...] += _colsum(dpa)
        dbx_ref[...] += _colsum(dpx)
        dcb_ref[...] += _colsum(dxc)
        cw = cw_ref[...]
        n8 = next8[...]
        dxl = cw[3:4] * dxc
        for k in (1, 2, 3):
            dxl = dxl + cw[3 - k:4 - k] * _shift_up(dxc, n8, k, row8)
        for k in range(4):
            dcw_ref[3 - k:4 - k, :] += _colsum(dxc * q["sh"][k])
        next8[...] = dxc[0:8]
        dxl_ref[...] = dxl.astype(dxl_ref.dtype)

    res = pl.pallas_call(
        body, name="lru_bwd", grid=(N_LRU_GROUPS, n_t),
        in_specs=[blk(C_XL // LRU_GROUP), prev8s(C_XL // LRU_GROUP), blk(C_GATE // LRU_GROUP), blk(0), prev8s(0), blk(0),
                  vec(4), vec(1), wbd, wbd, vec(1), vec(1), vec(1)],
        out_specs=[blk(0), blk(0), vec(4), vec(1), wbd, wbd, vec(1), vec(1), vec(1)],
        out_shape=[jax.ShapeDtypeStruct((S, D_RNN), MXU_DTYPE), jax.ShapeDtypeStruct((S, D_RNN), MXU_DTYPE),
                   jax.ShapeDtypeStruct((4, D_RNN), F32), jax.ShapeDtypeStruct((1, D_RNN), F32),
                   jax.ShapeDtypeStruct((N_LRU_GROUPS, LRU_GROUP, LRU_GROUP), F32),
                   jax.ShapeDtypeStruct((N_LRU_GROUPS, LRU_GROUP, LRU_GROUP), F32),
                   jax.ShapeDtypeStruct((1, D_RNN), F32), jax.ShapeDtypeStruct((1, D_RNN), F32),
                   jax.ShapeDtypeStruct((1, D_RNN), F32)],
        scratch_shapes=[pltpu.VMEM((8, LRU_GROUP), F32), pltpu.VMEM((1, LRU_GROUP), F32),
                        pltpu.VMEM((T, LRU_GROUP), F32), pltpu.VMEM((T, LRU_GROUP), F32), pltpu.VMEM((T, LRU_GROUP), F32)],
        compiler_params=_cparams(("parallel", "arbitrary")),
    )(proj, proj, proj, hl, hl, dz, conv_w, conv_b, wa_bd, wx_bd, b_a, b_x, lam)
    return res


def _block_diag(w):
    w4 = w.reshape(N_LRU_GROUPS, 4, LRU_BLOCK, 1, LRU_BLOCK)
    eye = jnp.eye(4, dtype=w.dtype).reshape(1, 4, 1, 4, 1)
    return (w4 * eye).reshape(N_LRU_GROUPS, LRU_GROUP, LRU_GROUP)


def _block_diag_extract(wbd):
    w5 = wbd.reshape(N_LRU_GROUPS, 4, LRU_BLOCK, 4, LRU_BLOCK)
    return jnp.stack([w5[:, a, :, a, :] for a in range(4)], axis=1).reshape(N_LRU_BLOCKS, LRU_BLOCK, LRU_BLOCK)


def _t5_bucket(dist):
    max_exact = NUM_BUCKETS // 2
    df = jnp.maximum(dist, 1).astype(jnp.float32)
    large = max_exact + (jnp.log(df / max_exact) / math.log(MAX_DISTANCE / max_exact)
                         * (NUM_BUCKETS - max_exact)).astype(jnp.int32)
    large = jnp.minimum(large, NUM_BUCKETS - 1)
    return jnp.where(dist < max_exact, dist, large)


def _band_offsets():
    qi = jnp.arange(SPAN)[:, None]
    kj = jnp.arange(2 * SPAN)[None, :]
    return qi + SPAN - kj


def _dil_buckets():
    off = _band_offsets()
    return jnp.stack([_t5_bucket(jnp.maximum(off, 0) * dil) for _, dil in DIL_GROUPS]).astype(jnp.int32)


def _dil_bias(rel_bias, buckets):
    def body(tbl_ref, bk_ref, o_ref):
        g = pl.program_id(0)
        qi = lax.broadcasted_iota(jnp.int32, (SPAN, 2 * SPAN), 0)
        kj = lax.broadcasted_iota(jnp.int32, (SPAN, 2 * SPAN), 1)
        off = qi + SPAN - kj
        valid = (off >= 0) & (off <= SPAN)
        bk = bk_ref[0]
        for h in range(DIL_HEADS):
            acc = jnp.zeros((SPAN, 2 * SPAN), F32)
            for b in range(NUM_BUCKETS):
                acc = jnp.where(bk == b, tbl_ref[b, g * DIL_HEADS + h], acc)
            o_ref[0, h] = jnp.where(valid, acc, NEG)

    return pl.pallas_call(
        body, name="dil_bias", grid=(3,),
        in_specs=[pl.BlockSpec(memory_space=pltpu.SMEM), pl.BlockSpec((1, SPAN, 2 * SPAN), lambda g: (g, 0, 0))],
        out_specs=pl.BlockSpec((1, DIL_HEADS, SPAN, 2 * SPAN), lambda g: (g, 0, 0, 0)),
        out_shape=jax.ShapeDtypeStruct((3, DIL_HEADS, SPAN, 2 * SPAN), F32),
        compiler_params=_cparams(("parallel",)),
    )(rel_bias, buckets)


def _dil_bias_bwd(dbias, buckets):
    def body(db_ref, bk_ref, o_ref):
        lane = lax.broadcasted_iota(jnp.int32, (1, 128), 1)
        rows = [jnp.zeros((1, 128), F32) for _ in range(NUM_BUCKETS)]
        for g in range(3):
            bk = bk_ref[g]
            for h in range(DIL_HEADS):
                d = db_ref[g, h]
                for b in range(NUM_BUCKETS):
                    tot = jnp.sum(_colsum(jnp.where(bk == b, d, 0.0)), axis=1, keepdims=True)
                    rows[b] = jnp.where(lane == g * DIL_HEADS + h, tot, rows[b])
        for b in range(NUM_BUCKETS):
            o_ref[b:b + 1, :] = rows[b]

    return pl.pallas_call(
        body, name="dil_bias_bwd",
        out_shape=jax.ShapeDtypeStruct((NUM_BUCKETS, 128), F32),
        compiler_params=_cparams(),
    )(dbias, buckets)


DIL_SUBBLOCKS = (4, 2, 1)


def _dil_layout(g, S):
    dil, m = DIL_GROUPS[g][1], DIL_SUBBLOCKS[g]
    sub = SPAN * dil
    col = [(C_QKV + t * 768 + g * 256) // 128 for t in range(3)]
    return dil, m, sub, S // (sub * m), col


def _residue_rows(b, r, dil):
    return pl.ds(b * SPAN * dil + r, SPAN, stride=dil) if dil > 1 else pl.ds(b * SPAN, SPAN)


def _for_residues(dil, fn):
    if dil <= 4:
        for r in range(dil):
            fn(r)
    else:
        lax.fori_loop(0, dil, lambda r, c: (fn(r), c)[1], 0, unroll=2)


def _pair_scores(qm, k2, bias, first_cols):
    s = _dot(qm, k2, "nt") * (DIL_HEAD_DIM ** -0.5) + bias
    kj = lax.broadcasted_iota(jnp.int32, s.shape, 1)
    return jnp.where(kj < first_cols, NEG, s)


def _dilated_fwd(proj, bias, g, *, S):
    dil, m, sub, nc, (qc, kc, vc) = _dil_layout(g, S)
    R = sub * m
    cur = lambda cb: pl.BlockSpec((R, 128), lambda p, i: (i, cb + p))
    prv = lambda cb: pl.BlockSpec((sub, 128), lambda p, i: (jnp.maximum(i * m - 1, 0), cb + p))
    out = pl.BlockSpec((R, 128), lambda p, i: (i, p))

    def body(q_ref, kp_ref, kc_ref, vp_ref, vc_ref, b_ref, o_ref, lse_ref):
        lane = lax.broadcasted_iota(jnp.int32, (SPAN, 128), 1)
        sels = (lane < DIL_HEAD_DIM, lane >= DIL_HEAD_DIM)
        for b in range(m):
            first_cols = jnp.where(pl.program_id(1) == 0, SPAN, 0) if b == 0 else 0

            def one(r, b=b, first_cols=first_cols):
                rows = _residue_rows(b, r, dil)
                before = (kc_ref, vc_ref, _residue_rows(b - 1, r, dil)) if b else (kp_ref, vp_ref, _residue_rows(0, r, dil))
                q2 = q_ref[rows, :]
                k2 = _mx(jnp.concatenate([before[0][before[2], :], kc_ref[rows, :]], axis=0))
                v2 = _mx(jnp.concatenate([before[1][before[2], :], vc_ref[rows, :]], axis=0))
                stat = jnp.zeros((SPAN, 128), F32)
                o2 = jnp.zeros((SPAN, 128), F32)
                for e in range(2):
                    s = _pair_scores(jnp.where(sels[e], q2, 0.0), k2, b_ref[0, e], first_cols)
                    mx = jnp.max(s, axis=-1, keepdims=True)
                    p = jnp.exp(s - mx)
                    den = jnp.sum(p, axis=-1, keepdims=True)
                    o2 = jnp.where(sels[e], _dot(p, v2) / den, o2)
                    stat = jnp.where(lane == e, mx + jnp.log(den), stat)
                o_ref[rows, :] = o2
                lse_ref[rows, :] = stat

            _for_residues(dil, one)

    return pl.pallas_call(
        body, name=f"dil_fwd{g}", grid=(2, nc),
        in_specs=[cur(qc), prv(kc), cur(kc), prv(vc), cur(vc),
                  pl.BlockSpec((1, 2, SPAN, 2 * SPAN), lambda p, i: (g, p, 0, 0))],
        out_specs=[out, out],
        out_shape=[jax.ShapeDtypeStruct((S, 256), F32), jax.ShapeDtypeStruct((S, 256), F32)],
        compiler_params=_cparams(("parallel", "parallel")),
    )(proj, proj, proj, proj, proj, bias)


def _dilated_bwd(proj, do, lse, delta, bias, g, *, S):
    dil, m, sub, nc, (qc, kc, vc) = _dil_layout(g, S)
    R = sub * m
    cl = lambda i: jnp.minimum(i, nc - 1)
    cur = lambda cb: pl.BlockSpec((R, 128), lambda p, i: (cl(i), cb + p))
    prv = lambda cb: pl.BlockSpec((sub, 128), lambda p, i: (jnp.maximum(cl(i) * m - 1, 0), cb + p))
    kv_out = pl.BlockSpec((R, 128), lambda p, i: (jnp.maximum(i - 1, 0), p))
    scale = DIL_HEAD_DIM ** -0.5

    def body(q_ref, kp_ref, kc_ref, vp_ref, vc_ref, do_ref, lse_ref, dl_ref, b_ref,
             dq_ref, dk_ref, dv_ref, db_ref, dq_s, kc_s, vc_s, kp_s, vp_s, kcar, vcar):
        i = pl.program_id(1)

        @pl.when(i == 0)
        def _():
            kcar[...] = jnp.zeros_like(kcar)
            vcar[...] = jnp.zeros_like(vcar)
            db_ref[...] = jnp.zeros_like(db_ref)

        @pl.when(i < nc)
        def _():
            lane = lax.broadcasted_iota(jnp.int32, (SPAN, 128), 1)
            sels = (lane < DIL_HEAD_DIM, lane >= DIL_HEAD_DIM)
            for b in range(m):
                first_cols = jnp.where(i == 0, SPAN, 0) if b == 0 else 0

                def one(r, b=b, first_cols=first_cols):
                    rows = _residue_rows(b, r, dil)
                    rows_before = _residue_rows(b - 1 if b else 0, r, dil)
                    k_before, v_before = (kc_ref, vc_ref) if b else (kp_ref, vp_ref)
                    q2, do2 = q_ref[rows, :], do_ref[rows, :]
                    k2 = _mx(jnp.concatenate([k_before[rows_before, :], kc_ref[rows, :]], axis=0))
                    v2 = _mx(jnp.concatenate([v_before[rows_before, :], vc_ref[rows, :]], axis=0))
                    lse_t, dl_t = lse_ref[rows, :], dl_ref[rows, :]
                    dq2 = jnp.zeros((SPAN, 128), F32)
                    dk2 = jnp.zeros((2 * SPAN, 128), F32)
                    dv2 = jnp.zeros((2 * SPAN, 128), F32)
                    for e in range(2):
                        qm = jnp.where(sels[e], q2, 0.0)
                        dom = jnp.where(sels[e], do2, 0.0)
                        p = jnp.exp(_pair_scores(qm, k2, b_ref[0, e], first_cols) - lse_t[:, e:e + 1])
                        ds = p * (_dot(dom, v2, "nt") - dl_t[:, e:e + 1])
                        db_ref[e] += ds
                        dq2 = jnp.where(sels[e], _dot(ds, k2) * scale, dq2)
                        dk2 = dk2 + _dot(ds, qm, "tn") * scale
                        dv2 = dv2 + _dot(p, dom, "tn")
                    dq_s[rows, :] = dq2
                    kc_s[rows, :] = dk2[SPAN:2 * SPAN]
                    vc_s[rows, :] = dv2[SPAN:2 * SPAN]
                    if b:
                        kc_s[rows_before, :] += dk2[0:SPAN]
                        vc_s[rows_before, :] += dv2[0:SPAN]
                    else:
                        kp_s[rows_before, :] = dk2[0:SPAN]
                        vp_s[rows_before, :] = dv2[0:SPAN]

                _for_residues(dil, one)
            dq_ref[...] = dq_s[...].astype(dq_ref.dtype)
            last = pl.ds((m - 1) * sub, sub)
            kcar[last, :] += kp_s[...]
            vcar[last, :] += vp_s[...]
            dk_ref[...] = kcar[...].astype(dk_ref.dtype)
            dv_ref[...] = vcar[...].astype(dv_ref.dtype)
            kcar[...] = kc_s[...]
            vcar[...] = vc_s[...]

        @pl.when(i == nc)
        def _():
            dk_ref[...] = kcar[...].astype(dk_ref.dtype)
            dv_ref[...] = vcar[...].astype(dv_ref.dtype)

    stat = pl.BlockSpec((R, 128), lambda p, i: (cl(i), p))
    big = jax.ShapeDtypeStruct((S, 256), MXU_DTYPE)
    return pl.pallas_call(
        body, name=f"dil_bwd{g}", grid=(2, nc + 1),
        in_specs=[cur(qc), prv(kc), cur(kc), prv(vc), cur(vc), stat, stat, stat,
                  pl.BlockSpec((1, 2, SPAN, 2 * SPAN), lambda p, i: (g, p, 0, 0))],
        out_specs=[stat, kv_out, kv_out, pl.BlockSpec((2, SPAN, 2 * SPAN), lambda p, i: (p, 0, 0))],
        out_shape=[big, big, big, jax.ShapeDtypeStruct((DIL_HEADS, SPAN, 2 * SPAN), F32)],
        scratch_shapes=[pltpu.VMEM((R, 128), F32)] * 3 + [pltpu.VMEM((sub, 128), F32)] * 2 + [pltpu.VMEM((R, 128), F32)] * 2,
        compiler_params=_cparams(("parallel", "arbitrary")),
    )(proj, proj, proj, proj, proj, do, lse, delta, bias)


def _dilated_merge(os_, lses, *, S, bt=512):
    tile = pl.BlockSpec((bt, 128), lambda i, p: (i, p))

    def body(o0, o1, o2, l0, l1, l2, o_ref, om_ref, lse_ref):
        lane = lax.broadcasted_iota(jnp.int32, (bt, 128), 1)
        lo = lane < DIL_HEAD_DIM
        ls = [l0[...], l1[...], l2[...]]
        ws, stat = [], jnp.zeros((bt, 128), F32)
        for e in range(2):
            a = [l[:, e:e + 1] for l in ls]
            m = jnp.maximum(jnp.maximum(a[0], a[1]), a[2])
            ex = [jnp.exp(v - m) for v in a]
            tot = ex[0] + ex[1] + ex[2]
            ws.append([v / tot for v in ex])
            stat = jnp.where(lane == e, m + jnp.log(tot), stat)
        acc = jnp.zeros((bt, 128), F32)
        for gi, o in enumerate((o0, o1, o2)):
            acc = acc + jnp.where(lo, ws[0][gi], ws[1][gi]) * o[...]
        o_ref[...] = acc
        om_ref[...] = _mx(acc)
        lse_ref[...] = stat

    return pl.pallas_call(
        body, name="dil_merge", grid=(S // bt, 2),
        in_specs=[tile] * 6, out_specs=[tile, tile, tile],
        out_shape=[jax.ShapeDtypeStruct((S, 256), F32), jax.ShapeDtypeStruct((S, 256), MXU_DTYPE),
                   jax.ShapeDtypeStruct((S, 256), F32)],
        compiler_params=_cparams(("parallel", "parallel")),
    )(*os_, *lses)


def _dilated_delta(do, o, *, S, bt=512):
    tile = pl.BlockSpec((bt, 128), lambda i, p: (i, p))

    def body(do_ref, o_ref, d_ref):
        lane = lax.broadcasted_iota(jnp.int32, (bt, 128), 1)
        prod = do_ref[...] * o_ref[...]
        d0 = jnp.sum(jnp.where(lane < DIL_HEAD_DIM, prod, 0.0), axis=-1, keepdims=True)
        d1 = jnp.sum(jnp.where(lane >= DIL_HEAD_DIM, prod, 0.0), axis=-1, keepdims=True)
        d_ref[...] = jnp.where(lane == 0, d0, jnp.where(lane == 1, d1, 0.0))

    return pl.pallas_call(
        body, name="dil_delta", grid=(S // bt, 2), in_specs=[tile, tile], out_specs=tile,
        out_shape=jax.ShapeDtypeStruct((S, 256), F32), compiler_params=_cparams(("parallel", "parallel")),
    )(do, o)


MEM_T = 512
QM_BLK = C_QM // MEM_HEAD_DIM


def _mem_attn_fwd(proj, kv, *, S):
    scale = MEM_HEAD_DIM ** -0.5

    def body(q_ref, k_ref, v_ref, o_ref, om_ref, lse_ref):
        s = _dot(q_ref[...], k_ref[...], "nt") * scale
        m = jnp.max(s, axis=-1, keepdims=True)
        p = jnp.exp(s - m)
        den = jnp.sum(p, axis=-1, keepdims=True)
        o = _dot(p, v_ref[...]) / den
        o_ref[...] = o
        om_ref[...] = _mx(o)
        lse_ref[0] = m + jnp.log(den)

    return pl.pallas_call(
        body, name="mem_attn_fwd", grid=(S // MEM_T, MEM_HEADS),
        in_specs=[pl.BlockSpec((MEM_T, MEM_HEAD_DIM), lambda i, h: (i, QM_BLK + h)),
                  pl.BlockSpec((N_MEM, MEM_HEAD_DIM), lambda i, h: (0, h)),
                  pl.BlockSpec((N_MEM, MEM_HEAD_DIM), lambda i, h: (0, MEM_HEADS + h))],
        out_specs=[pl.BlockSpec((MEM_T, MEM_HEAD_DIM), lambda i, h: (i, h)),
                   pl.BlockSpec((MEM_T, MEM_HEAD_DIM), lambda i, h: (i, h)),
                   pl.BlockSpec((1, MEM_T, 1), lambda i, h: (h, i, 0))],
        out_shape=[jax.ShapeDtypeStruct((S, MEM_WIDTH), F32), jax.ShapeDtypeStruct((S, MEM_WIDTH), MXU_DTYPE),
                   jax.ShapeDtypeStruct((MEM_HEADS, S, 1), F32)],
        compiler_params=_cparams(("parallel", "parallel")),
    )(proj, kv, kv)


def _mem_attn_bwd(proj, kv, om, lse, dom, *, S):
    scale = MEM_HEAD_DIM ** -0.5

    def body(q_ref, k_ref, v_ref, o_ref, lse_ref, do_ref, dq_ref, dk_ref, dv_ref):
        @pl.when(pl.program_id(1) == 0)
        def _():
            dk_ref[...] = jnp.zeros_like(dk_ref)
            dv_ref[...] = jnp.zeros_like(dv_ref)

        qv, kv_, vv, dov = q_ref[...], k_ref[...], v_ref[...], do_ref[...]
        p = jnp.exp(_dot(qv, kv_, "nt") * scale - lse_ref[0])
        delta = jnp.sum(dov * o_ref[...], axis=-1, keepdims=True)
        ds = p * (_dot(dov, vv, "nt") - delta)
        dq_ref[...] = (_dot(ds, kv_) * scale).astype(dq_ref.dtype)
        dk_ref[...] += _dot(ds, qv, "tn") * scale
        dv_ref[...] += _dot(p, dov, "tn")

    tile = pl.BlockSpec((MEM_T, MEM_HEAD_DIM), lambda h, i: (i, h))
    kvo = pl.BlockSpec((N_MEM, MEM_HEAD_DIM), lambda h, i: (0, h))
    return pl.pallas_call(
        body, name="mem_attn_bwd", grid=(MEM_HEADS, S // MEM_T),
        in_specs=[pl.BlockSpec((MEM_T, MEM_HEAD_DIM), lambda h, i: (i, QM_BLK + h)),
                  pl.BlockSpec((N_MEM, MEM_HEAD_DIM), lambda h, i: (0, h)),
                  pl.BlockSpec((N_MEM, MEM_HEAD_DIM), lambda h, i: (0, MEM_HEADS + h)),
                  tile, pl.BlockSpec((1, MEM_T, 1), lambda h, i: (h, i, 0)), tile],
        out_specs=[tile, kvo, kvo],
        out_shape=[jax.ShapeDtypeStruct((S, MEM_WIDTH), MXU_DTYPE), jax.ShapeDtypeStruct((N_MEM, MEM_WIDTH), F32),
                   jax.ShapeDtypeStruct((N_MEM, MEM_WIDTH), F32)],
        compiler_params=_cparams(("parallel", "arbitrary")),
    )(proj, kv, kv, om, lse, dom)


MIX_BM = 1024
MIX_BN = 256
GATES_BLK = C_GATES // MIX_BN


def _mix_specs(j_outer):
    ix = (lambda f: (lambda j, i: f(i, j))) if j_outer else (lambda f: f)
    act = lambda width: pl.BlockSpec((MIX_BM, width), ix(lambda i, j: (i, 0)))
    wgt = lambda width: pl.BlockSpec((width, MIX_BN), ix(lambda i, j: (0, j)))
    gate = lambda b: pl.BlockSpec((MIX_BM, MIX_BN), ix(lambda i, j: (i, GATES_BLK + 4 * b + j)))
    bias = lambda b: pl.BlockSpec((1, MIX_BN), ix(lambda i, j: (0, 4 * b + j)))
    tile = pl.BlockSpec((MIX_BM, MIX_BN), ix(lambda i, j: (i, j)))
    return act, wgt, gate, bias, tile


def _mix_fwd(z_lru, o_dil, om, w_lru, w_dil, w_mem, proj, b_gate, *, S):
    act, wgt, gate, bias, tile = _mix_specs(False)

    def body(zl, od, mo, wl, wd, wm, g0, g1, g2, b0, b1, b2, o_ref):
        acc = jax.nn.sigmoid(g0[...] + b0[...]) * _dot(zl[...], wl[...])
        acc += jax.nn.sigmoid(g1[...] + b1[...]) * _dot(od[...], wd[...])
        acc += jax.nn.sigmoid(g2[...] + b2[...]) * _dot(mo[...], wm[...])
        o_ref[...] = acc.astype(o_ref.dtype)

    return pl.pallas_call(
        body, name="mix_fwd", grid=(S // MIX_BM, D_MODEL // MIX_BN),
        in_specs=[act(D_RNN), act(256), act(MEM_WIDTH), wgt(D_RNN), wgt(256), wgt(MEM_WIDTH),
                  gate(0), gate(1), gate(2), bias(0), bias(1), bias(2)],
        out_specs=tile, out_shape=jax.ShapeDtypeStruct((S, D_MODEL), MXU_DTYPE),
        compiler_params=_cparams(("parallel", "parallel")),
    )(z_lru, o_dil, om, w_lru, w_dil, w_mem, proj, proj, proj, b_gate, b_gate, b_gate)


def _mix_bwd(dmerged, z_lru, o_dil, om, w_lru, w_dil, w_mem, proj, b_gate, *, S):
    act, wgt, gate, bias, tile = _mix_specs(True)

    def body(dm, zl, od, mo, wl, wd, wm, g0, g1, g2, b0, b1, b2,
             dg0, dg1, dg2, dy0, dy1, dy2, db0, db1, db2):
        @pl.when(pl.program_id(1) == 0)
        def _():
            for r in (db0, db1, db2):
                r[...] = jnp.zeros_like(r)

        dmv = dm[...]
        for act_ref, w_ref, g_ref, b_ref, dg_ref, dy_ref, db_ref in (
                (zl, wl, g0, b0, dg0, dy0, db0), (od, wd, g1, b1, dg1, dy1, db1), (mo, wm, g2, b2, dg2, dy2, db2)):
            y = _dot(act_ref[...], w_ref[...])
            gt = jax.nn.sigmoid(g_ref[...] + b_ref[...])
            dgate = dmv * y * gt * (1.0 - gt)
            dg_ref[...] = dgate.astype(dg_ref.dtype)
            dy_ref[...] = (dmv * gt).astype(dy_ref.dtype)
            db_ref[...] += _colsum(dgate)

    big = jax.ShapeDtypeStruct((S, D_MODEL), MXU_DTYPE)
    vec = jax.ShapeDtypeStruct((1, D_MODEL), F32)
    vspec = pl.BlockSpec((1, MIX_BN), lambda j, i: (0, j))
    return pl.pallas_call(
        body, name="mix_bwd", grid=(D_MODEL // MIX_BN, S // MIX_BM),
        in_specs=[tile, act(D_RNN), act(256), act(MEM_WIDTH), wgt(D_RNN), wgt(256), wgt(MEM_WIDTH),
                  gate(0), gate(1), gate(2), bias(0), bias(1), bias(2)],
        out_specs=[tile] * 6 + [vspec] * 3, out_shape=[big] * 6 + [vec] * 3,
        compiler_params=_cparams(("parallel", "arbitrary")),
    )(dmerged, z_lru, o_dil, om, w_lru, w_dil, w_mem, proj, proj, proj, b_gate, b_gate, b_gate)


def _adamw_math(w, g, m, v):
    m = ADAM_B1 * m + (1.0 - ADAM_B1) * g
    v = ADAM_B2 * v + (1.0 - ADAM_B2) * (g * g)
    m_hat = m / (1.0 - ADAM_B1 ** ADAM_STEP)
    v_hat = v / (1.0 - ADAM_B2 ** ADAM_STEP)
    delta = -ADAM_LR * (m_hat / (jnp.sqrt(v_hat) + ADAM_EPS) + ADAM_WD * w)
    return delta, m, v


def _adamw_landed(w, own, land, m, v, *, name, col_blk=0, prev=None):
    R = w.shape[0]
    n_parts, C = land.shape[0], land.shape[2]
    br = next(d for d in (256, 464, 128) if R % d == 0)
    tile = pl.BlockSpec((br, C), lambda i: (i, col_blk))
    part = pl.BlockSpec((br, C), lambda i: (i, 0))
    n_prev = 0 if prev is None else 4

    def body(w_ref, o_ref, l_ref, m_ref, v_ref, *rest):
        g_ref, d_ref, nm_ref, nv_ref = rest[n_prev:]
        g = o_ref[...].astype(F32)
        for p in range(n_parts):
            g = g + l_ref[p].astype(F32)
        d, nm, nv = _adamw_math(w_ref[...], g, m_ref[...], v_ref[...])
        g_ref[...] = g
        d_ref[...] = d
        nm_ref[...] = nm
        nv_ref[...] = nv

    return pl.pallas_call(
        body, name=name, grid=(R // br,),
        in_specs=[tile, part, pl.BlockSpec((n_parts, br, C), lambda i: (0, i, 0)), tile, tile]
        + [pl.BlockSpec(memory_space=pl.ANY)] * n_prev,
        out_specs=[tile] * 4, out_shape=[jax.ShapeDtypeStruct(w.shape, F32)] * 4,
        input_output_aliases={5 + j: j for j in range(n_prev)},
        compiler_params=_cparams(("parallel",)),
    )(w, own, land, m, v, *(prev or ()))


def _adamw_plain(w, g, m, v, *, name):
    def body(w_ref, g_ref, m_ref, v_ref, d_ref, nm_ref, nv_ref):
        d, nm, nv = _adamw_math(w_ref[...], g_ref[...], m_ref[...], v_ref[...])
        d_ref[...] = d
        nm_ref[...] = nm
        nv_ref[...] = nv

    return pl.pallas_call(
        body, name=name, out_shape=[jax.ShapeDtypeStruct(w.shape, F32)] * 3, compiler_params=_cparams(),
    )(w, g, m, v)


def _my_pos():
    return lax.axis_index("x"), lax.axis_index("y"), lax.axis_index("c")


def _dev_index(p):
    return 4 * p[0] + 2 * p[1] + p[2]


def _all_gather(shards):
    n = len(shards)
    hbm = pl.BlockSpec(memory_space=pl.ANY)

    def body(*refs):
        ins, outs = refs[:n], refs[n:2 * n]
        send_sems, recv_sems, local_sems = refs[2 * n:]
        x, y, c = _my_pos()
        me, sibling = (x, y, c), (x, y, 1 - c)
        chips = [(1 - x, y), (x, 1 - y), (1 - x, 1 - y)]

        def copy(a, k, block, to, src=None):
            dst = outs[a].at[_dev_index(block)]
            return pltpu.make_async_remote_copy(
                src_ref=dst if src is None else src, dst_ref=dst,
                send_sem=send_sems.at[a, k], recv_sem=recv_sems.at[a, k], device_id=to, device_id_type=MESH)

        mine = [pltpu.make_async_copy(ins[a], outs[a].at[_dev_index(me)], local_sems.at[a]) for a in range(n)]
        for cp in mine:
            cp.start()
        first = []
        for a in range(n):
            first.append(copy(a, 0, me, sibling, src=ins[a]))
            first += [copy(a, 1 + j, me, (*chip, c), src=ins[a]) for j, chip in enumerate(chips)]
        for cp in first:
            cp.start()
        passed = []
        for j, chip in enumerate(chips):
            for a in range(n):
                copy(a, 1 + j, (*chip, c), me).wait_recv()
                fwd = copy(a, 4 + j, (*chip, c), sibling)
                fwd.start()
                passed.append(fwd)
        for a in range(n):
            copy(a, 0, sibling, me).wait_recv()
        for j, chip in enumerate(chips):
            for a in range(n):
                copy(a, 4 + j, (*chip, 1 - c), me).wait_recv()
        for cp in first + passed:
            cp.wait_send()
        for cp in mine:
            cp.wait()

    return pl.pallas_call(
        body, name="all_gather_weights",
        in_specs=[hbm] * n, out_specs=[hbm] * n,
        out_shape=[jax.ShapeDtypeStruct((N_DEV,) + s.shape, s.dtype) for s in shards],
        scratch_shapes=[pltpu.SemaphoreType.DMA((n, 7)), pltpu.SemaphoreType.DMA((n, 7)), pltpu.SemaphoreType.DMA((n,))],
        compiler_params=pltpu.CompilerParams(has_side_effects=True),
    )(*shards)


def _peers(me):
    x, y, c = me
    out = []
    for k in range(1, 8):
        fx, fy, fc = (k >> 2) & 1, (k >> 1) & 1, k & 1
        out.append((k - 1, (1 - x if fx else x, 1 - y if fy else y, 1 - c if fc else c)))
    return out


HBM_SPEC = pl.BlockSpec(memory_space=pltpu.HBM)
SEM_SPEC = pl.BlockSpec(memory_space=pltpu.SEMAPHORE)
DATAFLOW_EFFECT = pltpu.SideEffectType.DATAFLOW_SIDE_EFFECTING


def _gather_refs(src, land, me, peer, k):
    return src, land.at[_dev_index(me)]


def _scatter_refs(src, land, me, peer, k):
    return src.at[_dev_index(peer)], land.at[k]


def _push_start(srcs, land_shapes, refs_of, name, after=()):
    n, n_after = len(srcs), len(after)

    def body(*refs):
        ins, lands = refs[:n], refs[n:2 * n]
        send_sems, recv_sems, token = refs[2 * n + n_after], refs[2 * n + n_after + 1], refs[-1]
        me = _my_pos()
        for k, peer in _peers(me):
            for a in range(n):
                src, dst = refs_of(ins[a], lands[a], me, peer, k)
                pltpu.make_async_remote_copy(src_ref=src, dst_ref=dst, send_sem=send_sems.at[7 * a + k],
                                             recv_sem=recv_sems.at[7 * a + k], device_id=peer, device_id_type=MESH).start()
        token[...] = jnp.zeros_like(token)

    lands = [lax.empty(shp, s.dtype) for shp, s in zip(land_shapes, srcs)]
    hbm = lambda a: pltpu.with_memory_space_constraint(a, pltpu.HBM)
    res = pl.pallas_call(
        body, name=name,
        out_shape=(pltpu.SemaphoreType.DMA((7 * n,)), pltpu.SemaphoreType.DMA((7 * n,)),
                   *[pltpu.HBM(s.shape, s.dtype) for s in srcs], *[pltpu.HBM(l.shape, l.dtype) for l in lands],
                   jax.ShapeDtypeStruct((8, 128), F32)),
        in_specs=[HBM_SPEC] * (2 * n) + [pl.BlockSpec(memory_space=pl.ANY)] * n_after,
        out_specs=(SEM_SPEC, SEM_SPEC, *[HBM_SPEC] * (2 * n), pl.BlockSpec(memory_space=pltpu.VMEM)),
        input_output_aliases={i: 2 + i for i in range(2 * n)},
        compiler_params=pltpu.CompilerParams(has_side_effects=DATAFLOW_EFFECT),
    )(*[hbm(s) for s in srcs], *[hbm(l) for l in lands], *after)
    return dict(sems=(res[0], res[1]), srcs=list(res[2:2 + n]), lands=list(res[2 + n:2 + 2 * n]), token=res[-1], n=n,
                refs_of=refs_of, name=name)


def _push_wait(started, after):
    n, refs_of = started["n"], started["refs_of"]
    after = list(after) if isinstance(after, (list, tuple)) else [after]

    def body(*refs):
        ins, lands = refs[:n], refs[n:2 * n]
        send_sems, recv_sems = refs[2 * n], refs[2 * n + 1]
        me = _my_pos()
        for k, peer in _peers(me):
            for a in range(n):
                src, dst = refs_of(ins[a], lands[a], me, peer, k)
                cp = pltpu.make_async_remote_copy(src_ref=src, dst_ref=dst, send_sem=send_sems.at[7 * a + k],
                                                  recv_sem=recv_sems.at[7 * a + k], device_id=peer, device_id_type=MESH)
                cp.wait_send()
                cp.wait_recv()

    arrs = started["srcs"] + started["lands"]
    res = pl.pallas_call(
        body, name=started["name"].replace("start", "wait"),
        out_shape=tuple(pltpu.HBM(a.shape, a.dtype) for a in arrs),
        in_specs=[HBM_SPEC] * (2 * n) + [SEM_SPEC, SEM_SPEC] + [pl.BlockSpec(memory_space=pl.ANY)] * len(after),
        out_specs=tuple([HBM_SPEC] * (2 * n)),
        input_output_aliases={i: i for i in range(2 * n)},
        compiler_params=pltpu.CompilerParams(has_side_effects=DATAFLOW_EFFECT),
    )(*arrs, *started["sems"], *after)
    return list(res[n:2 * n])


def _sum_slots(slots):
    def body(in_ref, out_ref):
        acc = in_ref[0]
        for d in range(1, N_DEV):
            acc = acc + in_ref[d]
        out_ref[...] = acc

    return pl.pallas_call(body, name="sum_small", out_shape=jax.ShapeDtypeStruct(slots.shape[1:], F32),
                          compiler_params=_cparams())(slots)


def _adamw_many(ws, gs, ms, vs):
    n = len(ws)

    def body(*refs):
        for i in range(n):
            w_ref, g_ref, m_ref, v_ref = (refs[j * n + i] for j in range(4))
            d_, nm, nv = _adamw_math(w_ref[...], g_ref[...], m_ref[...], v_ref[...])
            for j, val in enumerate((d_, nm, nv)):
                refs[(4 + j) * n + i][...] = val

    res = pl.pallas_call(body, name="adamw_small", out_shape=[jax.ShapeDtypeStruct(w_.shape, F32) for w_ in ws] * 3,
                         compiler_params=_cparams())(*ws, *gs, *ms, *vs)
    return [(res[i], res[n + i], res[2 * n + i]) for i in range(n)]


def _local_step(x, mem, tgt, W, P, late_weights, send_grads, reduce_small, tie0):
    S = x.shape[0]
    W = dict(W)
    h = _rmsnorm_fwd(x, P["g_mix"] + tie0, rows=S, name="norm_mix")
    proj = _matmul(h, W["w_in_t"], M=S, N=D_IN, K=D_MODEL, mode="nt", bm=512, bn=D_IN // 2, bk=D_MODEL, name="mm_in",
                   j_outer=True)

    wa_bd, wx_bd = _mx(_block_diag(P["w_rg_a"])), _mx(_block_diag(P["w_rg_x"]))
    lru_args = (W["conv_w"], P["conv_b"].reshape(1, -1), wa_bd, wx_bd, P["b_rg_a"].reshape(1, -1),
                P["b_rg_x"].reshape(1, -1), P["lru_lambda"].reshape(1, -1))
    hl, z_lru = _lru_fwd(proj, *lru_args, S=S)

    buckets = _dil_buckets()
    bias = _dil_bias(P["rel_bias"], buckets)
    group_out = [_dilated_fwd(proj, bias, g, S=S) for g in range(len(DIL_GROUPS))]
    o_dil, o_dil_m, lse_dil = _dilated_merge([o for o, _ in group_out], [l for _, l in group_out], S=S)

    W.update(late_weights("branch", [o_dil, z_lru]))
    mem_n = _rmsnorm_fwd(mem, P["g_mem"], rows=N_MEM, name="norm_mem")
    kv = _matmul(mem_n, W["w_mem_kv"], M=N_MEM, N=2 * MEM_WIDTH, K=D_MODEL, mode="nn", bm=N_MEM, bn=512, bk=D_MODEL,
                 name="mm_kv")
    om, om_m, lse_mem = _mem_attn_fwd(proj, kv, S=S)
    b_gate = P["b_gate"].reshape(1, -1)
    merged = _mix_fwd(z_lru, o_dil_m, om_m, W["w_lru_out"], W["w_dil_out"], W["w_mem_out"], proj, b_gate, S=S)
    x1 = _matmul(merged, W["w_out"], M=S, N=D_MODEL, K=D_MODEL, mode="nn", bm=512, bn=D_MODEL, bk=D_MODEL, name="mm_out",
                 epilogue=lambda acc, r: (r + acc,), extras=[(x, (0, 0))])
    hm = _rmsnorm_fwd(x1, P["g_mlp"], rows=S, name="norm_mlp")
    W.update(late_weights("mlp", [hm]))

    def relu2(acc):
        rl = jnp.maximum(acc, 0.0)
        return (rl * rl,)

    act = _matmul(hm, W["w_mlp_in"], M=S, N=D_FF, K=D_MODEL, mode="nn", bm=1024, bn=1024, bk=D_MODEL, name="mm_mlp_in",
                  out_dtypes=(MXU_DTYPE,), epilogue=relu2, j_outer=True)
    x2 = _matmul(act, W["w_mlp_out"], M=S, N=D_MODEL, K=D_FF, mode="nn", bm=512, bn=D_MODEL, bk=D_FF, name="mm_mlp_out",
                 epilogue=lambda acc, r: (r + acc,), extras=[(x1, (0, 0))])
    loss, dx2, dx2_m, dg_final = _loss_head(x2, P["g_final"], tgt, rows=S)

    G, Gs = {}, {}
    Gs["g_final"] = dg_final
    dw = dict(mode="tn", K=S, bk=S, out_dtypes=(MXU_DTYPE,))
    G["w_mlp_out"] = _matmul(act, dx2_m, M=D_FF, N=D_MODEL, bm=512, bn=D_MODEL, name="mm_dw_mlp_out",
                             parts=("rows", D_FF // N_DEV), **dw)
    du = _matmul(dx2_m, W["w_mlp_out"], M=S, N=D_FF, K=D_MODEL, mode="nt", bm=1024, bn=1024, bk=D_MODEL, name="mm_du",
                 out_dtypes=(MXU_DTYPE,), epilogue=lambda acc, a: (acc * (2.0 * jnp.sqrt(a.astype(F32))),),
                 extras=[(act, (0, 0))], j_outer=True)
    G["w_mlp_in"] = _matmul(hm, du, M=D_MODEL, N=D_FF, bm=D_MODEL, bn=512, name="mm_dw_mlp_in",
                            parts=("cols", D_FF // N_DEV), **dw)
    tie1 = send_grads({n: G.pop(n) for n in ("w_mlp_out", "w_mlp_in")})
    dhm = _matmul(du, W["w_mlp_in"], M=S, N=D_MODEL, K=D_FF, mode="nt", bm=512, bn=D_MODEL, bk=D_FF, name="mm_dhm",
                  deps=[tie1])
    dx1, dx1_m, Gs["g_mlp"] = _rmsnorm_bwd(x1, P["g_mlp"], dhm, dx2, rows=S, name="norm_mlp_bwd",
                                           dx_dtypes=(F32, MXU_DTYPE))
    G["w_out"] = _matmul(merged, dx1_m, M=D_MODEL, N=D_MODEL, bm=512, bn=D_MODEL, name="mm_dw_out",
                         parts=("rows", D_MODEL // N_DEV), **dw)
    dmerged = _matmul(dx1_m, W["w_out"], M=S, N=D_MODEL, K=D_MODEL, mode="nt", bm=512, bn=D_MODEL, bk=D_MODEL, name="mm_dmerged")
    (dg0, dg1, dg2, dy_lru, dy_dil, dy_mem, db0, db1, db2) = _mix_bwd(
        dmerged, z_lru, o_dil_m, om_m, W["w_lru_out"], W["w_dil_out"], W["w_mem_out"], proj, b_gate, S=S)
    Gs["b_gate0"], Gs["b_gate1"], Gs["b_gate2"] = db0, db1, db2

    G["w_mem_out"] = _matmul(om_m, dy_mem, M=MEM_WIDTH, N=D_MODEL, bm=MEM_WIDTH, bn=D_MODEL, name="mm_dw_mem_out",
                             parts=("cols", D_MODEL // N_DEV), **dw)
    dom = _matmul(dy_mem, W["w_mem_out"], M=S, N=MEM_WIDTH, K=D_MODEL, mode="nt", bm=512, bn=MEM_WIDTH, bk=D_MODEL,
                  name="mm_dom")
    dqm, dk_mem, dv_mem = _mem_attn_bwd(proj, kv, om, lse_mem, dom, S=S)
    dkv = jnp.concatenate([dk_mem, dv_mem], axis=1)
    G["w_mem_kv"] = _matmul(mem_n, dkv, M=D_MODEL, N=2 * MEM_WIDTH, K=N_MEM, mode="tn", bm=D_MODEL, bn=2 * MEM_WIDTH,
                            bk=N_MEM, name="mm_dw_kv", out_dtypes=(MXU_DTYPE,), parts=("rows", D_MODEL // N_DEV))
    dmem_n = _matmul(dkv, W["w_mem_kv"], M=N_MEM, N=D_MODEL, K=2 * MEM_WIDTH, mode="nt", bm=N_MEM, bn=D_MODEL,
                     bk=2 * MEM_WIDTH, name="mm_dmem")
    (Gs["g_mem"],) = _rmsnorm_bwd(mem, P["g_mem"], dmem_n, None, rows=N_MEM, name="norm_mem_bwd", dx_dtypes=())

    G["w_dil_out"] = _matmul(o_dil_m, dy_dil, M=256, N=D_MODEL, bm=256, bn=D_MODEL, name="mm_dw_dil_out",
                             parts=("cols", D_MODEL // N_DEV), **dw)
    do_dil = _matmul(dy_dil, W["w_dil_out"], M=S, N=256, K=D_MODEL, mode="nt", bm=512, bn=256, bk=D_MODEL, name="mm_do_dil")
    G["w_lru_out"] = _matmul(z_lru, dy_lru, M=D_RNN, N=D_MODEL, bm=D_RNN, bn=D_MODEL, name="mm_dw_lru_out",
                             parts=("cols", D_MODEL // N_DEV), **dw)
    dz = _matmul(dy_lru, W["w_lru_out"], M=S, N=D_RNN, K=D_MODEL, mode="nt", bm=512, bn=D_RNN, bk=D_MODEL, name="mm_dz_lru")
    tie2 = send_grads({n: G.pop(n) for n in ("w_out", "w_mem_out", "w_mem_kv", "w_dil_out", "w_lru_out")})
    bias = bias + tie2[0, 0]
    delta = _dilated_delta(do_dil, o_dil, S=S)
    dq_parts, dk_parts, dv_parts, dbias = [], [], [], []
    for g in range(len(DIL_GROUPS)):
        dq_g, dk_g, dv_g, db_g = _dilated_bwd(proj, do_dil, lse_dil, delta, bias, g, S=S)
        dq_parts.append(dq_g)
        dk_parts.append(dk_g)
        dv_parts.append(dv_g)
        dbias.append(db_g)
    drel = _dil_bias_bwd(jnp.stack(dbias, axis=0), buckets)
    Gs["rel_bias"] = drel

    dxl, dgl, dcw, dcb, dwa, dwx, dba, dbx, dlam = _lru_bwd(proj, hl, dz, *lru_args, S=S)
    Gs["conv_w"], Gs["conv_b"] = dcw, dcb
    Gs["w_rg_a"], Gs["w_rg_x"] = _block_diag_extract(dwa), _block_diag_extract(dwx)
    Gs["b_rg_a"], Gs["b_rg_x"], Gs["lru_lambda"] = dba, dbx, dlam
    Gs["loss"] = loss

    dproj = jnp.concatenate([dxl, dgl] + dq_parts + dk_parts + dv_parts + [dqm, dg0, dg1, dg2], axis=1)
    cols = D_MODEL // W_IN_PIECES
    tie = []
    for q in range(W_IN_PIECES):
        dw_q = _matmul(dproj, h, M=D_IN, N=cols, K=S, mode="tn", bm=D_IN // 2, bn=cols, bk=512, name=f"mm_dw_in_{q}",
                       b_off=(0, q), out_dtypes=(MXU_DTYPE,), parts=("rows", D_IN // N_DEV), deps=tie)
        tie = [send_grads({f"w_in_{q}": dw_q})]
    dh = _matmul(dproj, W["w_in_t"], M=S, N=D_MODEL, K=D_IN, mode="nn", bm=256, bn=D_MODEL, bk=D_IN, name="mm_dh",
                 deps=tie)
    grad_x, Gs["g_mix"] = _rmsnorm_bwd(x, P["g_mix"], dh, dx1, rows=S, name="norm_mix_bwd")
    return grad_x, reduce_small(Gs)


BIG = ("w_in", "w_lru_out", "w_dil_out", "w_mem_kv", "w_mem_out", "w_out", "w_mlp_in", "w_mlp_out")
W_IN_PIECES = 2
COL_SHARDED = ("w_lru_out", "w_dil_out", "w_mem_out", "w_mlp_in")
SMALL = ("g_mix", "b_gate", "conv_b", "w_rg_a", "b_rg_a", "w_rg_x", "b_rg_x", "lru_lambda", "rel_bias", "g_mem",
         "g_mlp", "g_final")
WEIGHTS = ("g_mix", "w_in", "b_gate", "conv_w", "conv_b", "w_rg_a", "b_rg_a", "w_rg_x", "b_rg_x", "lru_lambda",
           "w_lru_out", "rel_bias", "w_dil_out", "g_mem", "w_mem_kv", "w_mem_out", "w_out", "g_mlp", "w_mlp_in",
           "w_mlp_out", "g_final")


def _gathered_to_full(name, gathered):
    if name in COL_SHARDED:
        n, r, c = gathered.shape
        return gathered.transpose(1, 0, 2).reshape(r, n * c)
    n, r, c = gathered.shape
    return gathered.reshape(n * r, c)


SMALL_GRADS = (("g_mix", (1, 1024)), ("b_gate0", (1, 1024)), ("b_gate1", (1, 1024)), ("b_gate2", (1, 1024)),
               ("conv_b", (1, 768)), ("w_rg_a", (12, 64, 64)), ("b_rg_a", (1, 768)), ("w_rg_x", (12, 64, 64)),
               ("b_rg_x", (1, 768)), ("lru_lambda", (1, 768)), ("rel_bias", (32, 128)), ("g_mem", (1, 1024)),
               ("g_mlp", (1, 1024)), ("g_final", (1, 1024)), ("conv_w", (4, 768)), ("loss", (1, 1)))


def _pack(parts):
    flat = jnp.concatenate([p.reshape(-1) for p in parts])
    return jnp.pad(flat, (0, (-flat.shape[0]) % 1024)).reshape(-1, 128)


def _unpack(pack, shapes):
    flat = pack.reshape(-1)
    out, off = [], 0
    for shp in shapes:
        size = math.prod(shp)
        out.append(flat[off:off + size].reshape(shp))
        off += size
    return out


def kernel(x, mem, g_mix, w_in, b_gate, conv_w, conv_b, w_rg_a, b_rg_a, w_rg_x, b_rg_x, lru_lambda, w_lru_out, rel_bias, w_dil_out, g_mem, w_mem_kv, w_mem_out, w_out, g_mlp, w_mlp_in, w_mlp_out, g_final, loss_target, m_g_mix, m_w_in, m_b_gate, m_conv_w, m_conv_b, m_w_rg_a, m_b_rg_a, m_w_rg_x, m_b_rg_x, m_lru_lambda, m_w_lru_out, m_rel_bias, m_w_dil_out, m_g_mem, m_w_mem_kv, m_w_mem_out, m_w_out, m_g_mlp, m_w_mlp_in, m_w_mlp_out, m_g_final, v_g_mix, v_w_in, v_b_gate, v_conv_w, v_conv_b, v_w_rg_a, v_b_rg_a, v_w_rg_x, v_b_rg_x, v_lru_lambda, v_w_lru_out, v_rel_bias, v_w_dil_out, v_g_mem, v_w_mem_kv, v_w_mem_out, v_w_out, v_g_mlp, v_w_mlp_in, v_w_mlp_out, v_g_final):
    w = dict(g_mix=g_mix, w_in=w_in, b_gate=b_gate, conv_w=conv_w, conv_b=conv_b, w_rg_a=w_rg_a, b_rg_a=b_rg_a,
             w_rg_x=w_rg_x, b_rg_x=b_rg_x, lru_lambda=lru_lambda, w_lru_out=w_lru_out, rel_bias=rel_bias,
             w_dil_out=w_dil_out, g_mem=g_mem, w_mem_kv=w_mem_kv, w_mem_out=w_mem_out, w_out=w_out, g_mlp=g_mlp,
             w_mlp_in=w_mlp_in, w_mlp_out=w_mlp_out, g_final=g_final)
    m = dict(g_mix=m_g_mix, w_in=m_w_in, b_gate=m_b_gate, conv_w=m_conv_w, conv_b=m_conv_b, w_rg_a=m_w_rg_a,
             b_rg_a=m_b_rg_a, w_rg_x=m_w_rg_x, b_rg_x=m_b_rg_x, lru_lambda=m_lru_lambda, w_lru_out=m_w_lru_out,
             rel_bias=m_rel_bias, w_dil_out=m_w_dil_out, g_mem=m_g_mem, w_mem_kv=m_w_mem_kv, w_mem_out=m_w_mem_out,
             w_out=m_w_out, g_mlp=m_g_mlp, w_mlp_in=m_w_mlp_in, w_mlp_out=m_w_mlp_out, g_final=m_g_final)
    v = dict(g_mix=v_g_mix, w_in=v_w_in, b_gate=v_b_gate, conv_w=v_conv_w, conv_b=v_conv_b, w_rg_a=v_w_rg_a,
             b_rg_a=v_b_rg_a, w_rg_x=v_w_rg_x, b_rg_x=v_b_rg_x, lru_lambda=v_lru_lambda, w_lru_out=v_w_lru_out,
             rel_bias=v_rel_bias, w_dil_out=v_w_dil_out, g_mem=v_g_mem, w_mem_kv=v_w_mem_kv, w_mem_out=v_w_mem_out,
             w_out=v_w_out, g_mlp=v_g_mlp, w_mlp_in=v_w_mlp_in, w_mlp_out=v_w_mlp_out, g_final=v_g_final)

    my_idx = _dev_index(_my_pos())

    g_in, g_cw = _all_gather([_mx(w["w_in"].T), w["conv_w"]])
    W = {"w_in_t": g_in.reshape(D_IN, D_MODEL), "conv_w": g_cw.transpose(1, 0, 2).reshape(CONV_WIDTH, D_RNN)}
    late, order_after = {}, [g_in]
    for group, names in (("branch", ("w_mem_kv", "w_lru_out", "w_dil_out", "w_mem_out", "w_out")),
                         ("mlp", ("w_mlp_in", "w_mlp_out"))):
        shards = [_mx(w[n]) for n in names]
        started = _push_start(shards, [(N_DEV,) + s.shape for s in shards], _gather_refs, f"gather_{group}_start",
                              after=order_after)
        late[group] = (names, shards, started)
        order_after = [started["token"]]
    P = {n: w[n] for n in SMALL}

    def late_weights(group, after):
        names, shards, started = late[group]
        out = {}
        for n, land, own in zip(names, _push_wait(started, after), shards):
            full = lax.dynamic_update_index_in_dim(land, own, my_idx, 0)
            out[n] = _gathered_to_full(n, full)
        return out

    sent, small = [], {}

    def send_grads(gs):
        names = list(gs)
        parts = [gs[n] for n in names]
        own = [lax.dynamic_index_in_dim(p, my_idx, 0, keepdims=False) for p in parts]
        started = _push_start(parts, [(N_DEV - 1,) + p.shape[1:] for p in parts], _scatter_refs,
                              f"scatter{len(sent)}_start")
        sent.append((names, own, started))
        return started["token"]

    def reduce_small(gs):
        small["pack"] = _pack([gs[n] for n, _ in SMALL_GRADS])
        small["started"] = _push_start([small["pack"]], [(N_DEV,) + small["pack"].shape], _gather_refs, "small_start")
        return small["started"]["token"]

    grad_x, last_token = _local_step(x[0], mem[0], loss_target[0], W, P, late_weights, send_grads, reduce_small,
                                     late["mlp"][2]["token"][0, 0])

    grads, deltas, new_m, new_v = {}, {}, {}, {}
    after = last_token
    for names, own, started in sent[:-W_IN_PIECES]:
        for n, o, land in zip(names, own, _push_wait(started, after)):
            grads[n], deltas[n], new_m[n], new_v[n] = _adamw_landed(w[n], o, land, m[n], v[n], name=f"adamw_{n}")
            after = deltas[n]
    (small_land,) = _push_wait(small["started"], after)
    total = _sum_slots(lax.dynamic_update_index_in_dim(small_land, small["pack"], my_idx, 0))
    summed = dict(zip([n for n, _ in SMALL_GRADS], _unpack(total, [shp for _, shp in SMALL_GRADS])))
    summed["b_gate"] = jnp.concatenate([summed.pop(f"b_gate{b}") for b in range(3)], axis=1)
    summed["rel_bias"] = summed["rel_bias"][:, :3 * DIL_HEADS]
    for n in SMALL:
        grads[n] = summed[n].reshape(w[n].shape)
    small_updates = _adamw_many([w[n] for n in SMALL], [grads[n] for n in SMALL], [m[n] for n in SMALL],
                                [v[n] for n in SMALL])
    for n, (d_, nm_, nv_) in zip(SMALL, small_updates):
        deltas[n], new_m[n], new_v[n] = d_, nm_, nv_
    conv_w_sum, loss_sum = summed["conv_w"], summed["loss"]
    after = total
    prev = None
    for q, (names, own, started) in enumerate(sent[-W_IN_PIECES:]):
        (land,) = _push_wait(started, after)
        prev = _adamw_landed(w["w_in"].T, own[0], land, m["w_in"].T, v["w_in"].T, name=f"adamw_{names[0]}",
                             col_blk=q, prev=prev)
    grads["w_in"], deltas["w_in"], new_m["w_in"], new_v["w_in"] = [t.T for t in prev]
    cw_cols = D_RNN // N_DEV
    grads["conv_w"] = lax.dynamic_slice(conv_w_sum, (0, my_idx * cw_cols), (CONV_WIDTH, cw_cols))
    deltas["conv_w"], new_m["conv_w"], new_v["conv_w"] = _adamw_plain(
        w["conv_w"], grads["conv_w"], m["conv_w"], v["conv_w"], name="adamw_conv_w")

    return (loss_sum.reshape(()), grad_x[None], *[grads[n] for n in WEIGHTS], *[deltas[n] for n in WEIGHTS],
            *[new_m[n] for n in WEIGHTS], *[new_v[n] for n in WEIGHTS])
```

```python
import functools
import math

import jax
import jax.numpy as jnp
from jax import lax
from jax.experimental import pallas as pl
from jax.experimental.pallas import tpu as pltpu

F32 = jnp.float32
MXU_DTYPE = jnp.bfloat16
VMEM_LIMIT_BYTES = 56 * 1024 * 1024
N_DEV = 8

D_MODEL = 1024
N_MEM = 256
MEM_HEADS = 4
MEM_HEAD_DIM = 128
MEM_WIDTH = 512
D_RNN = 768
LRU_BLOCK = 64
N_LRU_BLOCKS = 12
LRU_GROUP = 256
N_LRU_GROUPS = 3
CONV_WIDTH = 4
LRU_C = 8.0
DIL_GROUPS = ((128, 1), (512, 4), (2048, 16))
SPAN = 128
DIL_HEADS = 4
DIL_HEAD_DIM = 64
NUM_BUCKETS = 32
MAX_DISTANCE = 2048
D_FF = 4096
D_IN = 7424
EPS = 1e-6
NEG = -1e30
C_XL, C_GATE, C_QKV, C_QM, C_GATES = 0, 768, 1536, 3840, 4352

ADAM_LR = 0.001
ADAM_B1 = 0.9
ADAM_B2 = 0.999
ADAM_EPS = 1e-08
ADAM_WD = 0.01
ADAM_STEP = 10

MESH = pl.DeviceIdType.MESH
GELU_K = math.sqrt(2.0 / math.pi)


def _cparams(sem=None):
    kw = dict(vmem_limit_bytes=VMEM_LIMIT_BYTES)
    if sem is not None:
        kw["dimension_semantics"] = sem
    return pltpu.CompilerParams(**kw)


def _mx(v):
    return v.astype(MXU_DTYPE)


def _dot(a, b, mode="nn"):
    dims = {"nn": (((1,), (0,)), ((), ())), "nt": (((1,), (1,)), ((), ())), "tn": (((0,), (0,)), ((), ()))}[mode]
    return lax.dot_general(_mx(a), _mx(b), dims, preferred_element_type=F32)


def _colsum(v):
    return jnp.sum(v, axis=0, keepdims=True)


def _matmul(a, b, *, M, N, K, mode, bm, bn, bk, name, out_dtypes=(F32,), epilogue=None, extras=(),
            a_off=(0, 0), b_off=(0, 0), j_outer=False, deps=(), parts=None):
    assert M % bm == 0 and N % bn == 0 and K % bk == 0, (name, M, N, K, bm, bn, bk)
    nm, nn, nk = M // bm, N // bn, K // bk

    def ij(f):
        if j_outer:
            return lambda j, i, k: f(i, j, k)
        return f

    if mode == "tn":
        a_spec = pl.BlockSpec((bk, bm), ij(lambda i, j, k: (k + a_off[0], i + a_off[1])))
    else:
        a_spec = pl.BlockSpec((bm, bk), ij(lambda i, j, k: (i + a_off[0], k + a_off[1])))
    if mode == "nt":
        b_spec = pl.BlockSpec((bn, bk), ij(lambda i, j, k: (j + b_off[0], k + b_off[1])))
    else:
        b_spec = pl.BlockSpec((bk, bn), ij(lambda i, j, k: (k + b_off[0], j + b_off[1])))
    ex_specs = [pl.BlockSpec((bm, bn), ij(functools.partial(lambda i, j, k, o: (i + o[0], j + o[1]), o=off)))
                for _, off in extras]
    if parts is None:
        out_dims = (M, N)
        out_spec = pl.BlockSpec((bm, bn), ij(lambda i, j, k: (i, j)))
    elif parts[0] == "rows":
        r = parts[1]
        assert bm % r == 0
        out_dims = (M // r, r, N)
        out_spec = pl.BlockSpec((bm // r, r, bn), ij(lambda i, j, k: (i, 0, j)))
    elif parts[0] == "rows_t":
        r = parts[1]
        assert bn % r == 0
        out_dims = (N // r, r, M)
        out_spec = pl.BlockSpec((bn // r, r, bm), ij(lambda i, j, k: (j, 0, i)))
    else:
        c = parts[1]
        assert bn % c == 0
        out_dims = (N // c, M, c)
        out_spec = pl.BlockSpec((bn // c, bm, c), ij(lambda i, j, k: (j, i, 0)))
    n_ex, n_out, n_dep = len(extras), len(out_dtypes), len(deps)

    def body(*refs):
        a_ref, b_ref = refs[0], refs[1]
        ex = refs[2:2 + n_ex]
        outs = refs[2 + n_ex + n_dep:2 + n_ex + n_dep + n_out]
        part = _dot(a_ref[...], b_ref[...], mode)

        def finish(acc):
            vals = epilogue(acc, *[e[...] for e in ex]) if epilogue is not None else (acc,)
            for o, v in zip(outs, vals):
                if parts is not None and parts[0] == "rows_t":
                    v = v.T
                v = v.astype(o.dtype)
                if parts is None:
                    o[...] = v
                elif parts[0] in ("rows", "rows_t"):
                    for ch in range(v.shape[0] // parts[1]):
                        o[ch] = v[ch * parts[1]:(ch + 1) * parts[1], :]
                else:
                    for ch in range(bn // parts[1]):
                        o[ch] = v[:, ch * parts[1]:(ch + 1) * parts[1]]

        if nk == 1:
            finish(part)
        else:
            acc_ref = refs[-1]
            k = pl.program_id(2)

            @pl.when(k == 0)
            def _():
                acc_ref[...] = part

            @pl.when(k > 0)
            def _():
                acc_ref[...] += part

            @pl.when(k == nk - 1)
            def _():
                finish(acc_ref[...])

    grid = (nn, nm, nk) if j_outer else (nm, nn, nk)
    res = pl.pallas_call(
        body, name=name, grid=grid,
        in_specs=[a_spec, b_spec] + ex_specs + [pl.BlockSpec(memory_space=pl.ANY)] * n_dep,
        out_specs=[out_spec] * n_out,
        out_shape=[jax.ShapeDtypeStruct(out_dims, dt) for dt in out_dtypes],
        scratch_shapes=[pltpu.VMEM((bm, bn), F32)] if nk > 1 else [],
        compiler_params=_cparams(("parallel", "parallel", "arbitrary")),
    )(a, b, *[e for e, _ in extras], *deps)
    return res[0] if n_out == 1 else res


def _rmsnorm_fwd(x, g, *, rows, name, bt=512):
    bt = min(bt, rows)

    def body(x_ref, g_ref, o_ref):
        xv = x_ref[...]
        r = lax.rsqrt(jnp.mean(xv * xv, axis=-1, keepdims=True) + EPS)
        o_ref[...] = (xv * r * g_ref[...]).astype(o_ref.dtype)

    return pl.pallas_call(
        body, name=name, grid=(rows // bt,),
        in_specs=[pl.BlockSpec((bt, D_MODEL), lambda i: (i, 0)), pl.BlockSpec((1, D_MODEL), lambda i: (0, 0))],
        out_specs=pl.BlockSpec((bt, D_MODEL), lambda i: (i, 0)),
        out_shape=jax.ShapeDtypeStruct((rows, D_MODEL), MXU_DTYPE),
        compiler_params=_cparams(("parallel",)),
    )(x, g.reshape(1, D_MODEL))


def _rms_bwd_tile(xv, gv, dyv):
    r = lax.rsqrt(jnp.mean(xv * xv, axis=-1, keepdims=True) + EPS)
    w = dyv * gv
    dx = r * w - xv * (r * r * r) * jnp.mean(w * xv, axis=-1, keepdims=True)
    dg = _colsum(dyv * xv * r)
    return dx, dg


def _rmsnorm_bwd(x, g, dy, res, *, rows, name, bt=512, dx_dtypes=(F32,)):
    bt = min(bt, rows)
    has_res = res is not None

    def body(*refs):
        x_ref, g_ref, dy_ref = refs[:3]
        res_ref = refs[3] if has_res else None
        outs = refs[3 + int(has_res):]
        dx, dg = _rms_bwd_tile(x_ref[...], g_ref[...], dy_ref[...])
        if has_res:
            dx = dx + res_ref[...]
        dg_ref = outs[-1]

        @pl.when(pl.program_id(0) == 0)
        def _():
            dg_ref[...] = jnp.zeros_like(dg_ref)

        dg_ref[...] += dg
        for o in outs[:-1]:
            o[...] = dx.astype(o.dtype)

    row_spec = pl.BlockSpec((bt, D_MODEL), lambda i: (i, 0))
    vec_spec = pl.BlockSpec((1, D_MODEL), lambda i: (0, 0))
    ins = [x, g.reshape(1, D_MODEL), dy] + ([res] if has_res else [])
    return pl.pallas_call(
        body, name=name, grid=(rows // bt,),
        in_specs=[row_spec, vec_spec, row_spec] + ([row_spec] if has_res else []),
        out_specs=[row_spec] * len(dx_dtypes) + [vec_spec],
        out_shape=[jax.ShapeDtypeStruct((rows, D_MODEL), dt) for dt in dx_dtypes] + [jax.ShapeDtypeStruct((1, D_MODEL), F32)],
        compiler_params=_cparams(("arbitrary",)),
    )(*ins)


def _loss_head(x2, g, tgt, *, rows, bt=512):
    def body(x_ref, g_ref, t_ref, loss_ref, dx_ref, dxm_ref, dg_ref):
        xv, gv = x_ref[...], g_ref[...]
        r = lax.rsqrt(jnp.mean(xv * xv, axis=-1, keepdims=True) + EPS)
        diff = xv * r * gv - t_ref[...]
        part = jnp.sum(jnp.mean(diff * diff, axis=-1, keepdims=True), axis=0, keepdims=True) * 0.5
        dx, dg = _rms_bwd_tile(xv, gv, diff * (1.0 / D_MODEL))

        @pl.when(pl.program_id(0) == 0)
        def _():
            loss_ref[...] = jnp.zeros_like(loss_ref)
            dg_ref[...] = jnp.zeros_like(dg_ref)

        loss_ref[...] += part
        dg_ref[...] += dg
        dx_ref[...] = dx
        dxm_ref[...] = _mx(dx)

    row_spec = pl.BlockSpec((bt, D_MODEL), lambda i: (i, 0))
    vec_spec = pl.BlockSpec((1, D_MODEL), lambda i: (0, 0))
    return pl.pallas_call(
        body, name="loss_head", grid=(rows // bt,),
        in_specs=[row_spec, vec_spec, row_spec],
        out_specs=[pl.BlockSpec((1, 1), lambda i: (0, 0)), row_spec, row_spec, vec_spec],
        out_shape=[jax.ShapeDtypeStruct((1, 1), F32), jax.ShapeDtypeStruct((rows, D_MODEL), F32),
                   jax.ShapeDtypeStruct((rows, D_MODEL), MXU_DTYPE), jax.ShapeDtypeStruct((1, D_MODEL), F32)],
        compiler_params=_cparams(("arbitrary",)),
    )(x2, g.reshape(1, D_MODEL), tgt)


LRU_T = 256


def _gelu(x):
    t = jnp.tanh(GELU_K * (x + 0.044715 * x * x * x))
    return 0.5 * x * (1.0 + t), t


def _gelu_grad(x, t):
    return 0.5 * (1.0 + t) + 0.5 * x * (1.0 - t * t) * GELU_K * (1.0 + 3.0 * 0.044715 * x * x)


def _softplus_neg(lam):
    z = -lam
    u = jnp.exp(-jnp.abs(z))
    w = 1.0 + u
    l1p = jnp.where(w == 1.0, u, jnp.log(w) * u / jnp.where(w == 1.0, 1.0, w - 1.0))
    return jnp.maximum(z, 0.0) + l1p


def _shift_down(cur, prev8, k, row8):
    y = pltpu.roll(cur, k, 0)
    head = jnp.where(row8 < k, pltpu.roll(prev8, k, 0), y[0:8])
    return jnp.concatenate([head, y[8:]], axis=0)


def _shift_up(cur, next8, k, row8):
    n = cur.shape[0]
    y = pltpu.roll(cur, n - k, 0)
    tail = jnp.where(row8 >= 8 - k, pltpu.roll(next8, 8 - k, 0), y[n - 8:n])
    return jnp.concatenate([y[0:n - 8], tail], axis=0)


def _lru_gates(xl, p8, cw, cb, wa, wx, ba, bx, lam, row8):
    sh = [xl] + [_shift_down(xl, p8, k, row8) for k in (1, 2, 3)]
    xc = cb + cw[3:4] * sh[0] + cw[2:3] * sh[1] + cw[1:2] * sh[2] + cw[0:1] * sh[3]
    r = jax.nn.sigmoid(_dot(xc, wa) + ba)
    i = jax.nn.sigmoid(_dot(xc, wx) + bx)
    sp = _softplus_neg(lam)
    la = -LRU_C * r * sp
    a = jnp.exp(la)
    mult = jnp.sqrt(jnp.tanh(-la) * (a * a + 1.0))
    return dict(sh=sh, xc=xc, r=r, i=i, sp=sp, a=a, mult=mult)


def _lru_specs(n_t, reverse):
    T = LRU_T
    tt = (lambda t: n_t - 1 - t) if reverse else (lambda t: t)
    blk = lambda col0: pl.BlockSpec((T, LRU_GROUP), lambda g, t: (tt(t), col0 + g))
    prev8 = lambda col0: pl.BlockSpec((8, LRU_GROUP), lambda g, t: (jnp.maximum(tt(t) * (T // 8) - 1, 0), col0 + g))
    vec = lambda rows: pl.BlockSpec((rows, LRU_GROUP), lambda g, t: (0, g))
    wbd = pl.BlockSpec((1, LRU_GROUP, LRU_GROUP), lambda g, t: (g, 0, 0))
    return blk, prev8, vec, wbd


def _lru_fwd(proj, conv_w, conv_b, wa_bd, wx_bd, b_a, b_x, lam, *, S):
    T = LRU_T
    n_t = S // T
    blk, _, vec, wbd = _lru_specs(n_t, False)

    def body(xl_ref, gate_ref, cw_ref, cb_ref, wa_ref, wx_ref, ba_ref, bx_ref, lam_ref,
             hl_ref, z_ref, prev8, hcar, a_s, b_s):
        @pl.when(pl.program_id(1) == 0)
        def _():
            prev8[...] = jnp.zeros_like(prev8)
            hcar[...] = jnp.zeros_like(hcar)

        row8 = lax.broadcasted_iota(jnp.int32, (8, LRU_GROUP), 0)
        xl = xl_ref[...]
        q = _lru_gates(xl, prev8[...], cw_ref[...], cb_ref[...], wa_ref[0], wx_ref[0], ba_ref[...], bx_ref[...],
                       lam_ref[...], row8)
        prev8[...] = xl[T - 8:T]
        a_s[...] = q["a"]
        b_s[...] = q["mult"] * q["i"] * q["xc"]

        def step(c, carry):
            off = pl.multiple_of(c * 8, 8)
            A = a_s[pl.ds(off, 8), :]
            B = b_s[pl.ds(off, 8), :]
            for k in (1, 2, 4):
                a_sh = jnp.where(row8 >= k, pltpu.roll(A, k, 0), 1.0)
                b_sh = jnp.where(row8 >= k, pltpu.roll(B, k, 0), 0.0)
                B = A * b_sh + B
                A = A * a_sh
            h = A * carry + B
            hl_ref[pl.ds(off, 8), :] = h
            return h[7:8, :]

        hcar[...] = lax.fori_loop(0, T // 8, step, hcar[...], unroll=4)
        ge, _ = _gelu(gate_ref[...])
        z_ref[...] = (ge * hl_ref[...]).astype(z_ref.dtype)

    return pl.pallas_call(
        body, name="lru_fwd", grid=(N_LRU_GROUPS, n_t),
        in_specs=[blk(C_XL // LRU_GROUP), blk(C_GATE // LRU_GROUP), vec(4), vec(1), wbd, wbd, vec(1), vec(1), vec(1)],
        out_specs=[blk(0), blk(0)],
        out_shape=[jax.ShapeDtypeStruct((S, D_RNN), F32), jax.ShapeDtypeStruct((S, D_RNN), MXU_DTYPE)],
        scratch_shapes=[pltpu.VMEM((8, LRU_GROUP), F32), pltpu.VMEM((1, LRU_GROUP), F32),
                        pltpu.VMEM((T, LRU_GROUP), F32), pltpu.VMEM((T, LRU_GROUP), F32)],
        compiler_params=_cparams(("parallel", "arbitrary")),
    )(proj, proj, conv_w, conv_b, wa_bd, wx_bd, b_a, b_x, lam)


def _lru_bwd(proj, hl, dz, conv_w, conv_b, wa_bd, wx_bd, b_a, b_x, lam, *, S):
    T = LRU_T
    n_t = S // T
    blk, prev8s, vec, wbd = _lru_specs(n_t, True)

    def body(xl_ref, xlp_ref, gate_ref, hl_ref, hlp_ref, dz_ref, cw_ref, cb_ref, wa_ref, wx_ref, ba_ref, bx_ref,
             lam_ref, dxl_ref, dgate_ref, dcw_ref, dcb_ref, dwa_ref, dwx_ref, dba_ref, dbx_ref, dlam_ref,
             next8, gcar, c_s, b_s, l_s):
        t = pl.program_id(1)
        first_chunk = t == n_t - 1

        @pl.when(t == 0)
        def _():
            next8[...] = jnp.zeros_like(next8)
            gcar[...] = jnp.zeros_like(gcar)
            for ref in (dcw_ref, dcb_ref, dwa_ref, dwx_ref, dba_ref, dbx_ref, dlam_ref):
                ref[...] = jnp.zeros_like(ref)

        row8 = lax.broadcasted_iota(jnp.int32, (8, LRU_GROUP), 0)
        rowT = lax.broadcasted_iota(jnp.int32, (T, LRU_GROUP), 0)
        keep = jnp.where(first_chunk, 0.0, 1.0)
        xl = xl_ref[...]
        wa, wx, lam_v = wa_ref[0], wx_ref[0], lam_ref[...]
        q = _lru_gates(xl, xlp_ref[...] * keep, cw_ref[...], cb_ref[...], wa, wx, ba_ref[...], bx_ref[...], lam_v, row8)
        a, mult, r, i, xc, sp = q["a"], q["mult"], q["r"], q["i"], q["xc"], q["sp"]
        hl_v = hl_ref[...]
        dz_v = dz_ref[...]
        gate = gate_ref[...]
        ge, th = _gelu(gate)
        dgate_ref[...] = (dz_v * hl_v * _gelu_grad(gate, th)).astype(dgate_ref.dtype)

        c_s[...] = jnp.where(rowT == T - 1, 0.0, pltpu.roll(a, T - 1, 0))
        b_s[...] = dz_v * ge + jnp.where(rowT == T - 1, gcar[...], 0.0)

        def step(n, carry):
            off = pl.multiple_of((T // 8 - 1 - n) * 8, 8)
            C = c_s[pl.ds(off, 8), :]
            B = b_s[pl.ds(off, 8), :]
            for k in (1, 2, 4):
                c_sh = jnp.where(row8 < 8 - k, pltpu.roll(C, 8 - k, 0), 1.0)
                b_sh = jnp.where(row8 < 8 - k, pltpu.roll(B, 8 - k, 0), 0.0)
                B = B + C * b_sh
                C = C * c_sh
            lam_t = B + C * carry
            l_s[pl.ds(off, 8), :] = lam_t
            return lam_t[0:1, :]

        lax.fori_loop(0, T // 8, step, jnp.zeros((1, LRU_GROUP), F32), unroll=4)
        lmb = l_s[...]
        gcar[...] = a[0:1, :] * lmb[0:1, :]

        h_prev = _shift_down(hl_v, hlp_ref[...] * keep, 1, row8)
        da = lmb * h_prev
        dmult = lmb * i * xc
        di = lmb * mult * xc
        dxc = lmb * mult * i
        dla = da * a - dmult * (a * a) / mult
        dr = dla * (-LRU_C * sp)
        dlam_ref[...] += _colsum(dla * (-LRU_C * r)) * (-jax.nn.sigmoid(-lam_v))
        dpa = dr * r * (1.0 - r)
        dpx = di * i * (1.0 - i)
        dxc = dxc + _dot(dpa, wa, "nt") + _dot(dpx, wx, "nt")
        dwa_ref[0] += _dot(xc, dpa, "tn")
        dwx_ref[0] += _dot(xc, dpx, "tn")
        dba_ref[...] += _colsum(dpa)
        dbx_ref[...] += _colsum(dpx)
        dcb_ref[...] += _colsum(dxc)
        cw = cw_ref[...]
        n8 = next8[...]
        dxl = cw[3:4] * dxc
        for k in (1, 2, 3):
            dxl = dxl + cw[3 - k:4 - k] * _shift_up(dxc, n8, k, row8)
        for k in range(4):
            dcw_ref[3 - k:4 - k, :] += _colsum(dxc * q["sh"][k])
        next8[...] = dxc[0:8]
        dxl_ref[...] = dxl.astype(dxl_ref.dtype)

    res = pl.pallas_call(
        body, name="lru_bwd", grid=(N_LRU_GROUPS, n_t),
        in_specs=[blk(C_XL // LRU_GROUP), prev8s(C_XL // LRU_GROUP), blk(C_GATE // LRU_GROUP), blk(0), prev8s(0), blk(0),
                  vec(4), vec(1), wbd, wbd, vec(1), vec(1), vec(1)],
        out_specs=[blk(0), blk(0), vec(4), vec(1), wbd, wbd, vec(1), vec(1), vec(1)],
        out_shape=[jax.ShapeDtypeStruct((S, D_RNN), MXU_DTYPE), jax.ShapeDtypeStruct((S, D_RNN), MXU_DTYPE),
                   jax.ShapeDtypeStruct((4, D_RNN), F32), jax.ShapeDtypeStruct((1, D_RNN), F32),
                   jax.ShapeDtypeStruct((N_LRU_GROUPS, LRU_GROUP, LRU_GROUP), F32),
                   jax.ShapeDtypeStruct((N_LRU_GROUPS, LRU_GROUP, LRU_GROUP), F32),
                   jax.ShapeDtypeStruct((1, D_RNN), F32), jax.ShapeDtypeStruct((1, D_RNN), F32),
                   jax.ShapeDtypeStruct((1, D_RNN), F32)],
        scratch_shapes=[pltpu.VMEM((8, LRU_GROUP), F32), pltpu.VMEM((1, LRU_GROUP), F32),
                        pltpu.VMEM((T, LRU_GROUP), F32), pltpu.VMEM((T, LRU_GROUP), F32), pltpu.VMEM((T, LRU_GROUP), F32)],
        compiler_params=_cparams(("parallel", "arbitrary")),
    )(proj, proj, proj, hl, hl, dz, conv_w, conv_b, wa_bd, wx_bd, b_a, b_x, lam)
    return res


def _block_diag(w):
    w4 = w.reshape(N_LRU_GROUPS, 4, LRU_BLOCK, 1, LRU_BLOCK)
    eye = jnp.eye(4, dtype=w.dtype).reshape(1, 4, 1, 4, 1)
    return (w4 * eye).reshape(N_LRU_GROUPS, LRU_GROUP, LRU_GROUP)


def _block_diag_extract(wbd):
    w5 = wbd.reshape(N_LRU_GROUPS, 4, LRU_BLOCK, 4, LRU_BLOCK)
    return jnp.stack([w5[:, a, :, a, :] for a in range(4)], axis=1).reshape(N_LRU_BLOCKS, LRU_BLOCK, LRU_BLOCK)


def _t5_bucket(dist):
    max_exact = NUM_BUCKETS // 2
    df = jnp.maximum(dist, 1).astype(jnp.float32)
    large = max_exact + (jnp.log(df / max_exact) / math.log(MAX_DISTANCE / max_exact)
                         * (NUM_BUCKETS - max_exact)).astype(jnp.int32)
    large = jnp.minimum(large, NUM_BUCKETS - 1)
    return jnp.where(dist < max_exact, dist, large)


def _band_offsets():
    qi = jnp.arange(SPAN)[:, None]
    kj = jnp.arange(2 * SPAN)[None, :]
    return qi + SPAN - kj


def _dil_buckets():
    off = _band_offsets()
    return jnp.stack([_t5_bucket(jnp.maximum(off, 0) * dil) for _, dil in DIL_GROUPS]).astype(jnp.int32)


def _dil_bias(rel_bias, buckets):
    def body(tbl_ref, bk_ref, o_ref):
        g = pl.program_id(0)
        qi = lax.broadcasted_iota(jnp.int32, (SPAN, 2 * SPAN), 0)
        kj = lax.broadcasted_iota(jnp.int32, (SPAN, 2 * SPAN), 1)
        off = qi + SPAN - kj
        valid = (off >= 0) & (off <= SPAN)
        bk = bk_ref[0]
        for h in range(DIL_HEADS):
            acc = jnp.zeros((SPAN, 2 * SPAN), F32)
            for b in range(NUM_BUCKETS):
                acc = jnp.where(bk == b, tbl_ref[b, g * DIL_HEADS + h], acc)
            o_ref[0, h] = jnp.where(valid, acc, NEG)

    return pl.pallas_call(
        body, name="dil_bias", grid=(3,),
        in_specs=[pl.BlockSpec(memory_space=pltpu.SMEM), pl.BlockSpec((1, SPAN, 2 * SPAN), lambda g: (g, 0, 0))],
        out_specs=pl.BlockSpec((1, DIL_HEADS, SPAN, 2 * SPAN), lambda g: (g, 0, 0, 0)),
        out_shape=jax.ShapeDtypeStruct((3, DIL_HEADS, SPAN, 2 * SPAN), F32),
        compiler_params=_cparams(("parallel",)),
    )(rel_bias, buckets)


def _dil_bias_bwd(dbias, buckets):
    def body(db_ref, bk_ref, o_ref):
        lane = lax.broadcasted_iota(jnp.int32, (1, 128), 1)
        rows = [jnp.zeros((1, 128), F32) for _ in range(NUM_BUCKETS)]
        for g in range(3):
            bk = bk_ref[g]
            for h in range(DIL_HEADS):
                d = db_ref[g, h]
                for b in range(NUM_BUCKETS):
                    tot = jnp.sum(_colsum(jnp.where(bk == b, d, 0.0)), axis=1, keepdims=True)
                    rows[b] = jnp.where(lane == g * DIL_HEADS + h, tot, rows[b])
        for b in range(NUM_BUCKETS):
            o_ref[b:b + 1, :] = rows[b]

    return pl.pallas_call(
        body, name="dil_bias_bwd",
        out_shape=jax.ShapeDtypeStruct((NUM_BUCKETS, 128), F32),
        compiler_params=_cparams(),
    )(dbias, buckets)


DIL_SUBBLOCKS = (4, 2, 1)


def _dil_layout(g, S):
    dil, m = DIL_GROUPS[g][1], DIL_SUBBLOCKS[g]
    sub = SPAN * dil
    col = [(C_QKV + t * 768 + g * 256) // 128 for t in range(3)]
    return dil, m, sub, S // (sub * m), col


def _residue_rows(b, r, dil):
    return pl.ds(b * SPAN * dil + r, SPAN, stride=dil) if dil > 1 else pl.ds(b * SPAN, SPAN)


def _for_residues(dil, fn):
    if dil <= 4:
        for r in range(dil):
            fn(r)
    else:
        lax.fori_loop(0, dil, lambda r, c: (fn(r), c)[1], 0, unroll=2)


def _pair_scores(qm, k2, bias, first_cols):
    s = _dot(qm, k2, "nt") * (DIL_HEAD_DIM ** -0.5) + bias
    kj = lax.broadcasted_iota(jnp.int32, s.shape, 1)
    return jnp.where(kj < first_cols, NEG, s)


def _dilated_fwd(proj, bias, g, *, S):
    dil, m, sub, nc, (qc, kc, vc) = _dil_layout(g, S)
    R = sub * m
    cur = lambda cb: pl.BlockSpec((R, 128), lambda p, i: (i, cb + p))
    prv = lambda cb: pl.BlockSpec((sub, 128), lambda p, i: (jnp.maximum(i * m - 1, 0), cb + p))
    out = pl.BlockSpec((R, 128), lambda p, i: (i, p))

    def body(q_ref, kp_ref, kc_ref, vp_ref, vc_ref, b_ref, o_ref, lse_ref):
        lane = lax.broadcasted_iota(jnp.int32, (SPAN, 128), 1)
        sels = (lane < DIL_HEAD_DIM, lane >= DIL_HEAD_DIM)
        for b in range(m):
            first_cols = jnp.where(pl.program_id(1) == 0, SPAN, 0) if b == 0 else 0

            def one(r, b=b, first_cols=first_cols):
                rows = _residue_rows(b, r, dil)
                before = (kc_ref, vc_ref, _residue_rows(b - 1, r, dil)) if b else (kp_ref, vp_ref, _residue_rows(0, r, dil))
                q2 = q_ref[rows, :]
                k2 = _mx(jnp.concatenate([before[0][before[2], :], kc_ref[rows, :]], axis=0))
                v2 = _mx(jnp.concatenate([before[1][before[2], :], vc_ref[rows, :]], axis=0))
                qq = jnp.concatenate([jnp.where(sels[0], q2, 0.0), jnp.where(sels[1], q2, 0.0)], axis=0)
                s = _pair_scores(qq, k2, b_ref[0, 0], first_cols)
                mx = jnp.max(s, axis=-1, keepdims=True)
                p = jnp.exp(s - mx)
                den = jnp.sum(p, axis=-1, keepdims=True)
                o = _dot(p, v2) / den
                st = mx + jnp.log(den)
                o_ref[rows, :] = jnp.where(sels[0], o[0:SPAN], o[SPAN:2 * SPAN])
                lse_ref[rows, :] = jnp.where(lane == 0, st[0:SPAN], jnp.where(lane == 1, st[SPAN:2 * SPAN], 0.0))

            _for_residues(dil, one)

    return pl.pallas_call(
        body, name=f"dil_fwd{g}", grid=(2, nc),
        in_specs=[cur(qc), prv(kc), cur(kc), prv(vc), cur(vc),
                  pl.BlockSpec((1, 1, 2 * SPAN, 2 * SPAN), lambda p, i: (g, p, 0, 0))],
        out_specs=[out, out],
        out_shape=[jax.ShapeDtypeStruct((S, 256), F32), jax.ShapeDtypeStruct((S, 256), F32)],
        compiler_params=_cparams(("parallel", "parallel")),
    )(proj, proj, proj, proj, proj, bias.reshape(3, 2, 2 * SPAN, 2 * SPAN))


def _dilated_bwd(proj, do, lse, delta, bias, g, *, S):
    dil, m, sub, nc, (qc, kc, vc) = _dil_layout(g, S)
    R = sub * m
    cl = lambda i: jnp.minimum(i, nc - 1)
    cur = lambda cb: pl.BlockSpec((R, 128), lambda p, i: (cl(i), cb + p))
    prv = lambda cb: pl.BlockSpec((sub, 128), lambda p, i: (jnp.maximum(cl(i) * m - 1, 0), cb + p))
    kv_out = pl.BlockSpec((R, 128), lambda p, i: (jnp.maximum(i - 1, 0), p))
    scale = DIL_HEAD_DIM ** -0.5

    def body(q_ref, kp_ref, kc_ref, vp_ref, vc_ref, do_ref, lse_ref, dl_ref, b_ref,
             dq_ref, dk_ref, dv_ref, db_ref, dq_s, kc_s, vc_s, kp_s, vp_s, kcar, vcar):
        i = pl.program_id(1)

        @pl.when(i == 0)
        def _():
            kcar[...] = jnp.zeros_like(kcar)
            vcar[...] = jnp.zeros_like(vcar)
            db_ref[...] = jnp.zeros_like(db_ref)

        @pl.when(i < nc)
        def _():
            lane = lax.broadcasted_iota(jnp.int32, (SPAN, 128), 1)
            sels = (lane < DIL_HEAD_DIM, lane >= DIL_HEAD_DIM)
            for b in range(m):
                first_cols = jnp.where(i == 0, SPAN, 0) if b == 0 else 0

                def one(r, b=b, first_cols=first_cols):
                    rows = _residue_rows(b, r, dil)
                    rows_before = _residue_rows(b - 1 if b else 0, r, dil)
                    k_before, v_before = (kc_ref, vc_ref) if b else (kp_ref, vp_ref)
                    q2, do2 = q_ref[rows, :], do_ref[rows, :]
                    k2 = _mx(jnp.concatenate([k_before[rows_before, :], kc_ref[rows, :]], axis=0))
                    v2 = _mx(jnp.concatenate([v_before[rows_before, :], vc_ref[rows, :]], axis=0))
                    lse_t, dl_t = lse_ref[rows, :], dl_ref[rows, :]
                    qq = _mx(jnp.concatenate([jnp.where(sels[0], q2, 0.0), jnp.where(sels[1], q2, 0.0)], axis=0))
                    dd = _mx(jnp.concatenate([jnp.where(sels[0], do2, 0.0), jnp.where(sels[1], do2, 0.0)], axis=0))
                    lse2 = jnp.concatenate([lse_t[:, 0:1], lse_t[:, 1:2]], axis=0)
                    dl2 = jnp.concatenate([dl_t[:, 0:1], dl_t[:, 1:2]], axis=0)
                    p = jnp.exp(_pair_scores(qq, k2, b_ref[0, 0], first_cols) - lse2)
                    ds = p * (_dot(dd, v2, "nt") - dl2)
                    db_ref[0] += ds
                    dqq = _dot(ds, k2) * scale
                    dq2 = jnp.where(sels[0], dqq[0:SPAN], dqq[SPAN:2 * SPAN])
                    dk2 = _dot(ds, qq, "tn") * scale
                    dv2 = _dot(p, dd, "tn")
                    dq_s[rows, :] = dq2
                    kc_s[rows, :] = dk2[SPAN:2 * SPAN]
                    vc_s[rows, :] = dv2[SPAN:2 * SPAN]
                    if b:
                        kc_s[rows_before, :] += dk2[0:SPAN]
                        vc_s[rows_before, :] += dv2[0:SPAN]
                    else:
                        kp_s[rows_before, :] = dk2[0:SPAN]
                        vp_s[rows_before, :] = dv2[0:SPAN]

                _for_residues(dil, one)
            dq_ref[...] = dq_s[...].astype(dq_ref.dtype)
            last = pl.ds((m - 1) * sub, sub)
            kcar[last, :] += kp_s[...]
            vcar[last, :] += vp_s[...]
            dk_ref[...] = kcar[...].astype(dk_ref.dtype)
            dv_ref[...] = vcar[...].astype(dv_ref.dtype)
            kcar[...] = kc_s[...]
            vcar[...] = vc_s[...]

        @pl.when(i == nc)
        def _():
            dk_ref[...] = kcar[...].astype(dk_ref.dtype)
            dv_ref[...] = vcar[...].astype(dv_ref.dtype)

    stat = pl.BlockSpec((R, 128), lambda p, i: (cl(i), p))
    big = jax.ShapeDtypeStruct((S, 256), MXU_DTYPE)
    return pl.pallas_call(
        body, name=f"dil_bwd{g}", grid=(2, nc + 1),
        in_specs=[cur(qc), prv(kc), cur(kc), prv(vc), cur(vc), stat, stat, stat,
                  pl.BlockSpec((1, 1, 2 * SPAN, 2 * SPAN), lambda p, i: (g, p, 0, 0))],
        out_specs=[stat, kv_out, kv_out, pl.BlockSpec((1, 2 * SPAN, 2 * SPAN), lambda p, i: (p, 0, 0))],
        out_shape=[big, big, big, jax.ShapeDtypeStruct((2, 2 * SPAN, 2 * SPAN), F32)],
        scratch_shapes=[pltpu.VMEM((R, 128), F32)] * 3 + [pltpu.VMEM((sub, 128), F32)] * 2 + [pltpu.VMEM((R, 128), F32)] * 2,
        compiler_params=_cparams(("parallel", "arbitrary")),
    )(proj, proj, proj, proj, proj, do, lse, delta, bias.reshape(3, 2, 2 * SPAN, 2 * SPAN))


def _dilated_merge(os_, lses, *, S, bt=512):
    tile = pl.BlockSpec((bt, 128), lambda i, p: (i, p))

    def body(o0, o1, o2, l0, l1, l2, o_ref, om_ref, lse_ref):
        lane = lax.broadcasted_iota(jnp.int32, (bt, 128), 1)
        lo = lane < DIL_HEAD_DIM
        ls = [l0[...], l1[...], l2[...]]
        ws, stat = [], jnp.zeros((bt, 128), F32)
        for e in range(2):
            a = [l[:, e:e + 1] for l in ls]
            m = jnp.maximum(jnp.maximum(a[0], a[1]), a[2])
            ex = [jnp.exp(v - m) for v in a]
            tot = ex[0] + ex[1] + ex[2]
            ws.append([v / tot for v in ex])
            stat = jnp.where(lane == e, m + jnp.log(tot), stat)
        acc = jnp.zeros((bt, 128), F32)
        for gi, o in enumerate((o0, o1, o2)):
            acc = acc + jnp.where(lo, ws[0][gi], ws[1][gi]) * o[...]
        o_ref[...] = acc
        om_ref[...] = _mx(acc)
        lse_ref[...] = stat

    return pl.pallas_call(
        body, name="dil_merge", grid=(S // bt, 2),
        in_specs=[tile] * 6, out_specs=[tile, tile, tile],
        out_shape=[jax.ShapeDtypeStruct((S, 256), F32), jax.ShapeDtypeStruct((S, 256), MXU_DTYPE),
                   jax.ShapeDtypeStruct((S, 256), F32)],
        compiler_params=_cparams(("parallel", "parallel")),
    )(*os_, *lses)


def _dilated_delta(do, o, *, S, bt=512):
    tile = pl.BlockSpec((bt, 128), lambda i, p: (i, p))

    def body(do_ref, o_ref, d_ref):
        lane = lax.broadcasted_iota(jnp.int32, (bt, 128), 1)
        prod = do_ref[...] * o_ref[...]
        d0 = jnp.sum(jnp.where(lane < DIL_HEAD_DIM, prod, 0.0), axis=-1, keepdims=True)
        d1 = jnp.sum(jnp.where(lane >= DIL_HEAD_DIM, prod, 0.0), axis=-1, keepdims=True)
        d_ref[...] = jnp.where(lane == 0, d0, jnp.where(lane == 1, d1, 0.0))

    return pl.pallas_call(
        body, name="dil_delta", grid=(S // bt, 2), in_specs=[tile, tile], out_specs=tile,
        out_shape=jax.ShapeDtypeStruct((S, 256), F32), compiler_params=_cparams(("parallel", "parallel")),
    )(do, o)


MEM_T = 512
QM_BLK = C_QM // MEM_HEAD_DIM


def _mem_attn_fwd(proj, kv, *, S):
    scale = MEM_HEAD_DIM ** -0.5

    def body(q_ref, k_ref, v_ref, o_ref, om_ref, lse_ref):
        s = _dot(q_ref[...], k_ref[...], "nt") * scale
        m = jnp.max(s, axis=-1, keepdims=True)
        p = jnp.exp(s - m)
        den = jnp.sum(p, axis=-1, keepdims=True)
        o = _dot(p, v_ref[...]) / den
        o_ref[...] = o
        om_ref[...] = _mx(o)
        lse_ref[0] = m + jnp.log(den)

    return pl.pallas_call(
        body, name="mem_attn_fwd", grid=(S // MEM_T, MEM_HEADS),
        in_specs=[pl.BlockSpec((MEM_T, MEM_HEAD_DIM), lambda i, h: (i, QM_BLK + h)),
                  pl.BlockSpec((N_MEM, MEM_HEAD_DIM), lambda i, h: (0, h)),
                  pl.BlockSpec((N_MEM, MEM_HEAD_DIM), lambda i, h: (0, MEM_HEADS + h))],
        out_specs=[pl.BlockSpec((MEM_T, MEM_HEAD_DIM), lambda i, h: (i, h)),
                   pl.BlockSpec((MEM_T, MEM_HEAD_DIM), lambda i, h: (i, h)),
                   pl.BlockSpec((1, MEM_T, 1), lambda i, h: (h, i, 0))],
        out_shape=[jax.ShapeDtypeStruct((S, MEM_WIDTH), F32), jax.ShapeDtypeStruct((S, MEM_WIDTH), MXU_DTYPE),
                   jax.ShapeDtypeStruct((MEM_HEADS, S, 1), F32)],
        compiler_params=_cparams(("parallel", "parallel")),
    )(proj, kv, kv)


def _mem_attn_bwd(proj, kv, om, lse, dom, *, S):
    scale = MEM_HEAD_DIM ** -0.5

    def body(q_ref, k_ref, v_ref, o_ref, lse_ref, do_ref, dq_ref, dk_ref, dv_ref):
        @pl.when(pl.program_id(1) == 0)
        def _():
            dk_ref[...] = jnp.zeros_like(dk_ref)
            dv_ref[...] = jnp.zeros_like(dv_ref)

        qv, kv_, vv, dov = q_ref[...], k_ref[...], v_ref[...], do_ref[...]
        p = jnp.exp(_dot(qv, kv_, "nt") * scale - lse_ref[0])
        delta = jnp.sum(dov * o_ref[...], axis=-1, keepdims=True)
        ds = p * (_dot(dov, vv, "nt") - delta)
        dq_ref[...] = (_dot(ds, kv_) * scale).astype(dq_ref.dtype)
        dk_ref[...] += _dot(ds, qv, "tn") * scale
        dv_ref[...] += _dot(p, dov, "tn")

    tile = pl.BlockSpec((MEM_T, MEM_HEAD_DIM), lambda h, i: (i, h))
    kvo = pl.BlockSpec((N_MEM, MEM_HEAD_DIM), lambda h, i: (0, h))
    return pl.pallas_call(
        body, name="mem_attn_bwd", grid=(MEM_HEADS, S // MEM_T),
        in_specs=[pl.BlockSpec((MEM_T, MEM_HEAD_DIM), lambda h, i: (i, QM_BLK + h)),
                  pl.BlockSpec((N_MEM, MEM_HEAD_DIM), lambda h, i: (0, h)),
                  pl.BlockSpec((N_MEM, MEM_HEAD_DIM), lambda h, i: (0, MEM_HEADS + h)),
                  tile, pl.BlockSpec((1, MEM_T, 1), lambda h, i: (h, i, 0)), tile],
        out_specs=[tile, kvo, kvo],
        out_shape=[jax.ShapeDtypeStruct((S, MEM_WIDTH), MXU_DTYPE), jax.ShapeDtypeStruct((N_MEM, MEM_WIDTH), F32),
                   jax.ShapeDtypeStruct((N_MEM, MEM_WIDTH), F32)],
        compiler_params=_cparams(("parallel", "arbitrary")),
    )(proj, kv, kv, om, lse, dom)


MIX_BM = 1024
MIX_BN = 256
GATES_BLK = C_GATES // MIX_BN


def _mix_specs(j_outer):
    ix = (lambda f: (lambda j, i: f(i, j))) if j_outer else (lambda f: f)
    act = lambda width: pl.BlockSpec((MIX_BM, width), ix(lambda i, j: (i, 0)))
    wgt = lambda width: pl.BlockSpec((width, MIX_BN), ix(lambda i, j: (0, j)))
    gate = lambda b: pl.BlockSpec((MIX_BM, MIX_BN), ix(lambda i, j: (i, GATES_BLK + 4 * b + j)))
    bias = lambda b: pl.BlockSpec((1, MIX_BN), ix(lambda i, j: (0, 4 * b + j)))
    tile = pl.BlockSpec((MIX_BM, MIX_BN), ix(lambda i, j: (i, j)))
    return act, wgt, gate, bias, tile


def _mix_fwd(z_lru, o_dil, om, w_lru, w_dil, w_mem, proj, b_gate, *, S):
    act, wgt, gate, bias, tile = _mix_specs(False)

    def body(zl, od, mo, wl, wd, wm, g0, g1, g2, b0, b1, b2, o_ref):
        acc = jax.nn.sigmoid(g0[...] + b0[...]) * _dot(zl[...], wl[...])
        acc += jax.nn.sigmoid(g1[...] + b1[...]) * _dot(od[...], wd[...])
        acc += jax.nn.sigmoid(g2[...] + b2[...]) * _dot(mo[...], wm[...])
        o_ref[...] = acc.astype(o_ref.dtype)

    return pl.pallas_call(
        body, name="mix_fwd", grid=(S // MIX_BM, D_MODEL // MIX_BN),
        in_specs=[act(D_RNN), act(256), act(MEM_WIDTH), wgt(D_RNN), wgt(256), wgt(MEM_WIDTH),
                  gate(0), gate(1), gate(2), bias(0), bias(1), bias(2)],
        out_specs=tile, out_shape=jax.ShapeDtypeStruct((S, D_MODEL), MXU_DTYPE),
        compiler_params=_cparams(("parallel", "parallel")),
    )(z_lru, o_dil, om, w_lru, w_dil, w_mem, proj, proj, proj, b_gate, b_gate, b_gate)


def _mix_bwd(dmerged, z_lru, o_dil, om, w_lru, w_dil, w_mem, proj, b_gate, *, S):
    act, wgt, gate, bias, tile = _mix_specs(True)

    def body(dm, zl, od, mo, wl, wd, wm, g0, g1, g2, b0, b1, b2,
             dg0, dg1, dg2, dy0, dy1, dy2, db0, db1, db2):
        @pl.when(pl.program_id(1) == 0)
        def _():
            for r in (db0, db1, db2):
                r[...] = jnp.zeros_like(r)

        dmv = dm[...]
        for act_ref, w_ref, g_ref, b_ref, dg_ref, dy_ref, db_ref in (
                (zl, wl, g0, b0, dg0, dy0, db0), (od, wd, g1, b1, dg1, dy1, db1), (mo, wm, g2, b2, dg2, dy2, db2)):
            y = _dot(act_ref[...], w_ref[...])
            gt = jax.nn.sigmoid(g_ref[...] + b_ref[...])
            dgate = dmv * y * gt * (1.0 - gt)
            dg_ref[...] = dgate.astype(dg_ref.dtype)
            dy_ref[...] = (dmv * gt).astype(dy_ref.dtype)
            db_ref[...] += _colsum(dgate)

    big = jax.ShapeDtypeStruct((S, D_MODEL), MXU_DTYPE)
    vec = jax.ShapeDtypeStruct((1, D_MODEL), F32)
    vspec = pl.BlockSpec((1, MIX_BN), lambda j, i: (0, j))
    return pl.pallas_call(
        body, name="mix_bwd", grid=(D_MODEL // MIX_BN, S // MIX_BM),
        in_specs=[tile, act(D_RNN), act(256), act(MEM_WIDTH), wgt(D_RNN), wgt(256), wgt(MEM_WIDTH),
                  gate(0), gate(1), gate(2), bias(0), bias(1), bias(2)],
        out_specs=[tile] * 6 + [vspec] * 3, out_shape=[big] * 6 + [vec] * 3,
        compiler_params=_cparams(("parallel", "arbitrary")),
    )(dmerged, z_lru, o_dil, om, w_lru, w_dil, w_mem, proj, proj, proj, b_gate, b_gate, b_gate)


def _adamw_math(w, g, m, v):
    m = ADAM_B1 * m + (1.0 - ADAM_B1) * g
    v = ADAM_B2 * v + (1.0 - ADAM_B2) * (g * g)
    m_hat = m / (1.0 - ADAM_B1 ** ADAM_STEP)
    v_hat = v / (1.0 - ADAM_B2 ** ADAM_STEP)
    delta = -ADAM_LR * (m_hat / (jnp.sqrt(v_hat) + ADAM_EPS) + ADAM_WD * w)
    return delta, m, v


def _adamw_landed(w, own, land, m, v, *, name, col_blk=0, prev=None):
    R = w.shape[0]
    n_parts, C = land.shape[0], land.shape[2]
    br = next(d for d in (256, 464, 128) if R % d == 0)
    tile = pl.BlockSpec((br, C), lambda i: (i, col_blk))
    part = pl.BlockSpec((br, C), lambda i: (i, 0))
    n_prev = 0 if prev is None else 4

    def body(w_ref, o_ref, l_ref, m_ref, v_ref, *rest):
        g_ref, d_ref, nm_ref, nv_ref = rest[n_prev:]
        g = o_ref[...].astype(F32)
        for p in range(n_parts):
            g = g + l_ref[p].astype(F32)
        d, nm, nv = _adamw_math(w_ref[...], g, m_ref[...], v_ref[...])
        g_ref[...] = g
        d_ref[...] = d
        nm_ref[...] = nm
        nv_ref[...] = nv

    return pl.pallas_call(
        body, name=name, grid=(R // br,),
        in_specs=[tile, part, pl.BlockSpec((n_parts, br, C), lambda i: (0, i, 0)), tile, tile]
        + [pl.BlockSpec(memory_space=pl.ANY)] * n_prev,
        out_specs=[tile] * 4, out_shape=[jax.ShapeDtypeStruct(w.shape, F32)] * 4,
        input_output_aliases={5 + j: j for j in range(n_prev)},
        compiler_params=_cparams(("parallel",)),
    )(w, own, land, m, v, *(prev or ()))


def _adamw_plain(w, g, m, v, *, name):
    def body(w_ref, g_ref, m_ref, v_ref, d_ref, nm_ref, nv_ref):
        d, nm, nv = _adamw_math(w_ref[...], g_ref[...], m_ref[...], v_ref[...])
        d_ref[...] = d
        nm_ref[...] = nm
        nv_ref[...] = nv

    return pl.pallas_call(
        body, name=name, out_shape=[jax.ShapeDtypeStruct(w.shape, F32)] * 3, compiler_params=_cparams(),
    )(w, g, m, v)


def _my_pos():
    return lax.axis_index("x"), lax.axis_index("y"), lax.axis_index("c")


def _dev_index(p):
    return 4 * p[0] + 2 * p[1] + p[2]


def _all_gather(shards):
    n = len(shards)
    hbm = pl.BlockSpec(memory_space=pl.ANY)

    def body(*refs):
        ins, outs = refs[:n], refs[n:2 * n]
        send_sems, recv_sems, local_sems = refs[2 * n:]
        x, y, c = _my_pos()
        me, sibling = (x, y, c), (x, y, 1 - c)
        chips = [(1 - x, y), (x, 1 - y), (1 - x, 1 - y)]

        def copy(a, k, block, to, src=None):
            dst = outs[a].at[_dev_index(block)]
            return pltpu.make_async_remote_copy(
                src_ref=dst if src is None else src, dst_ref=dst,
                send_sem=send_sems.at[a, k], recv_sem=recv_sems.at[a, k], device_id=to, device_id_type=MESH)

        mine = [pltpu.make_async_copy(ins[a], outs[a].at[_dev_index(me)], local_sems.at[a]) for a in range(n)]
        for cp in mine:
            cp.start()
        first = []
        for a in range(n):
            first.append(copy(a, 0, me, sibling, src=ins[a]))
            first += [copy(a, 1 + j, me, (*chip, c), src=ins[a]) for j, chip in enumerate(chips)]
        for cp in first:
            cp.start()
        passed = []
        for j, chip in enumerate(chips):
            for a in range(n):
                copy(a, 1 + j, (*chip, c), me).wait_recv()
                fwd = copy(a, 4 + j, (*chip, c), sibling)
                fwd.start()
                passed.append(fwd)
        for a in range(n):
            copy(a, 0, sibling, me).wait_recv()
        for j, chip in enumerate(chips):
            for a in range(n):
                copy(a, 4 + j, (*chip, 1 - c), me).wait_recv()
        for cp in first + passed:
            cp.wait_send()
        for cp in mine:
            cp.wait()

    return pl.pallas_call(
        body, name="all_gather_weights",
        in_specs=[hbm] * n, out_specs=[hbm] * n,
        out_shape=[jax.ShapeDtypeStruct((N_DEV,) + s.shape, s.dtype) for s in shards],
        scratch_shapes=[pltpu.SemaphoreType.DMA((n, 7)), pltpu.SemaphoreType.DMA((n, 7)), pltpu.SemaphoreType.DMA((n,))],
        compiler_params=pltpu.CompilerParams(has_side_effects=True),
    )(*shards)


def _peers(me):
    x, y, c = me
    out = []
    for k in range(1, 8):
        fx, fy, fc = (k >> 2) & 1, (k >> 1) & 1, k & 1
        out.append((k - 1, (1 - x if fx else x, 1 - y if fy else y, 1 - c if fc else c)))
    return out


HBM_SPEC = pl.BlockSpec(memory_space=pltpu.HBM)
SEM_SPEC = pl.BlockSpec(memory_space=pltpu.SEMAPHORE)
DATAFLOW_EFFECT = pltpu.SideEffectType.DATAFLOW_SIDE_EFFECTING


def _gather_refs(src, land, me, peer, k):
    return src, land.at[_dev_index(me)]


def _scatter_refs(src, land, me, peer, k):
    return src.at[_dev_index(peer)], land.at[k]


def _push_start(srcs, land_shapes, refs_of, name, after=()):
    n, n_after = len(srcs), len(after)

    def body(*refs):
        ins, lands = refs[:n], refs[n:2 * n]
        send_sems, recv_sems, token = refs[2 * n + n_after], refs[2 * n + n_after + 1], refs[-1]
        me = _my_pos()
        for k, peer in _peers(me):
            for a in range(n):
                src, dst = refs_of(ins[a], lands[a], me, peer, k)
                pltpu.make_async_remote_copy(src_ref=src, dst_ref=dst, send_sem=send_sems.at[7 * a + k],
                                             recv_sem=recv_sems.at[7 * a + k], device_id=peer, device_id_type=MESH).start()
        token[...] = jnp.zeros_like(token)

    lands = [lax.empty(shp, s.dtype) for shp, s in zip(land_shapes, srcs)]
    hbm = lambda a: pltpu.with_memory_space_constraint(a, pltpu.HBM)
    res = pl.pallas_call(
        body, name=name,
        out_shape=(pltpu.SemaphoreType.DMA((7 * n,)), pltpu.SemaphoreType.DMA((7 * n,)),
                   *[pltpu.HBM(s.shape, s.dtype) for s in srcs], *[pltpu.HBM(l.shape, l.dtype) for l in lands],
                   jax.ShapeDtypeStruct((8, 128), F32)),
        in_specs=[HBM_SPEC] * (2 * n) + [pl.BlockSpec(memory_space=pl.ANY)] * n_after,
        out_specs=(SEM_SPEC, SEM_SPEC, *[HBM_SPEC] * (2 * n), pl.BlockSpec(memory_space=pltpu.VMEM)),
        input_output_aliases={i: 2 + i for i in range(2 * n)},
        compiler_params=pltpu.CompilerParams(has_side_effects=DATAFLOW_EFFECT),
    )(*[hbm(s) for s in srcs], *[hbm(l) for l in lands], *after)
    return dict(sems=(res[0], res[1]), srcs=list(res[2:2 + n]), lands=list(res[2 + n:2 + 2 * n]), token=res[-1], n=n,
                refs_of=refs_of, name=name)


def _push_wait(started, after):
    n, refs_of = started["n"], started["refs_of"]
    after = list(after) if isinstance(after, (list, tuple)) else [after]

    def body(*refs):
        ins, lands = refs[:n], refs[n:2 * n]
        send_sems, recv_sems = refs[2 * n], refs[2 * n + 1]
        me = _my_pos()
        for k, peer in _peers(me):
            for a in range(n):
                src, dst = refs_of(ins[a], lands[a], me, peer, k)
                cp = pltpu.make_async_remote_copy(src_ref=src, dst_ref=dst, send_sem=send_sems.at[7 * a + k],
                                                  recv_sem=recv_sems.at[7 * a + k], device_id=peer, device_id_type=MESH)
                cp.wait_send()
                cp.wait_recv()

    arrs = started["srcs"] + started["lands"]
    res = pl.pallas_call(
        body, name=started["name"].replace("start", "wait"),
        out_shape=tuple(pltpu.HBM(a.shape, a.dtype) for a in arrs),
        in_specs=[HBM_SPEC] * (2 * n) + [SEM_SPEC, SEM_SPEC] + [pl.BlockSpec(memory_space=pl.ANY)] * len(after),
        out_specs=tuple([HBM_SPEC] * (2 * n)),
        input_output_aliases={i: i for i in range(2 * n)},
        compiler_params=pltpu.CompilerParams(has_side_effects=DATAFLOW_EFFECT),
    )(*arrs, *started["sems"], *after)
    return list(res[n:2 * n])


def _sum_slots(slots):
    def body(in_ref, out_ref):
        acc = in_ref[0]
        for d in range(1, N_DEV):
            acc = acc + in_ref[d]
        out_ref[...] = acc

    return pl.pallas_call(body, name="sum_small", out_shape=jax.ShapeDtypeStruct(slots.shape[1:], F32),
                          compiler_params=_cparams())(slots)


def _adamw_many(ws, gs, ms, vs):
    n = len(ws)

    def body(*refs):
        for i in range(n):
            w_ref, g_ref, m_ref, v_ref = (refs[j * n + i] for j in range(4))
            d_, nm, nv = _adamw_math(w_ref[...], g_ref[...], m_ref[...], v_ref[...])
            for j, val in enumerate((d_, nm, nv)):
                refs[(4 + j) * n + i][...] = val

    res = pl.pallas_call(body, name="adamw_small", out_shape=[jax.ShapeDtypeStruct(w_.shape, F32) for w_ in ws] * 3,
                         compiler_params=_cparams())(*ws, *gs, *ms, *vs)
    return [(res[i], res[n + i], res[2 * n + i]) for i in range(n)]


def _local_step(x, mem, tgt, W, P, late_weights, send_grads, reduce_small, tie0):
    S = x.shape[0]
    W = dict(W)
    h = _rmsnorm_fwd(x, P["g_mix"] + tie0, rows=S, name="norm_mix")
    proj = _matmul(h, W["w_in_t"], M=S, N=D_IN, K=D_MODEL, mode="nt", bm=512, bn=D_IN // 2, bk=D_MODEL, name="mm_in",
                   j_outer=True)

    wa_bd, wx_bd = _mx(_block_diag(P["w_rg_a"])), _mx(_block_diag(P["w_rg_x"]))
    lru_args = (W["conv_w"], P["conv_b"].reshape(1, -1), wa_bd, wx_bd, P["b_rg_a"].reshape(1, -1),
                P["b_rg_x"].reshape(1, -1), P["lru_lambda"].reshape(1, -1))
    hl, z_lru = _lru_fwd(proj, *lru_args, S=S)

    buckets = _dil_buckets()
    bias = _dil_bias(P["rel_bias"], buckets)
    group_out = [_dilated_fwd(proj, bias, g, S=S) for g in range(len(DIL_GROUPS))]
    o_dil, o_dil_m, lse_dil = _dilated_merge([o for o, _ in group_out], [l for _, l in group_out], S=S)

    W.update(late_weights("branch", [o_dil, z_lru]))
    mem_n = _rmsnorm_fwd(mem, P["g_mem"], rows=N_MEM, name="norm_mem")
    kv = _matmul(mem_n, W["w_mem_kv"], M=N_MEM, N=2 * MEM_WIDTH, K=D_MODEL, mode="nn", bm=N_MEM, bn=512, bk=D_MODEL,
                 name="mm_kv")
    om, om_m, lse_mem = _mem_attn_fwd(proj, kv, S=S)
    b_gate = P["b_gate"].reshape(1, -1)
    merged = _mix_fwd(z_lru, o_dil_m, om_m, W["w_lru_out"], W["w_dil_out"], W["w_mem_out"], proj, b_gate, S=S)
    x1 = _matmul(merged, W["w_out"], M=S, N=D_MODEL, K=D_MODEL, mode="nn", bm=512, bn=D_MODEL, bk=D_MODEL, name="mm_out",
                 epilogue=lambda acc, r: (r + acc,), extras=[(x, (0, 0))])
    hm = _rmsnorm_fwd(x1, P["g_mlp"], rows=S, name="norm_mlp")
    W.update(late_weights("mlp", [hm]))

    def relu2(acc):
        rl = jnp.maximum(acc, 0.0)
        return (rl * rl,)

    act = _matmul(hm, W["w_mlp_in_t"], M=S, N=D_FF, K=D_MODEL, mode="nt", bm=1024, bn=1024, bk=D_MODEL, name="mm_mlp_in",
                  out_dtypes=(MXU_DTYPE,), epilogue=relu2, j_outer=True)
    x2 = _matmul(act, W["w_mlp_out"], M=S, N=D_MODEL, K=D_FF, mode="nn", bm=512, bn=D_MODEL, bk=D_FF, name="mm_mlp_out",
                 epilogue=lambda acc, r: (r + acc,), extras=[(x1, (0, 0))])
    loss, dx2, dx2_m, dg_final = _loss_head(x2, P["g_final"], tgt, rows=S)

    G, Gs = {}, {}
    Gs["g_final"] = dg_final
    dw = dict(mode="tn", K=S, bk=S, out_dtypes=(MXU_DTYPE,))
    G["w_mlp_out"] = _matmul(act, dx2_m, M=D_FF, N=D_MODEL, bm=512, bn=D_MODEL, name="mm_dw_mlp_out",
                             parts=("rows", D_FF // N_DEV), **dw)
    du = _matmul(dx2_m, W["w_mlp_out"], M=S, N=D_FF, K=D_MODEL, mode="nt", bm=1024, bn=1024, bk=D_MODEL, name="mm_du",
                 out_dtypes=(MXU_DTYPE,), epilogue=lambda acc, a: (acc * (2.0 * jnp.sqrt(a.astype(F32))),),
                 extras=[(act, (0, 0))], j_outer=True)
    G["w_mlp_in"] = _matmul(hm, du, M=D_MODEL, N=D_FF, bm=D_MODEL, bn=512, name="mm_dw_mlp_in",
                            parts=("cols", D_FF // N_DEV), **dw)
    tie1 = send_grads({n: G.pop(n) for n in ("w_mlp_out", "w_mlp_in")})
    dhm = _matmul(du, W["w_mlp_in_t"], M=S, N=D_MODEL, K=D_FF, mode="nn", bm=512, bn=D_MODEL, bk=D_FF, name="mm_dhm",
                  deps=[tie1])
    dx1, dx1_m, Gs["g_mlp"] = _rmsnorm_bwd(x1, P["g_mlp"], dhm, dx2, rows=S, name="norm_mlp_bwd",
                                           dx_dtypes=(F32, MXU_DTYPE))
    G["w_out"] = _matmul(merged, dx1_m, M=D_MODEL, N=D_MODEL, bm=512, bn=D_MODEL, name="mm_dw_out",
                         parts=("rows", D_MODEL // N_DEV), **dw)
    dmerged = _matmul(dx1_m, W["w_out"], M=S, N=D_MODEL, K=D_MODEL, mode="nt", bm=512, bn=D_MODEL, bk=D_MODEL, name="mm_dmerged")
    (dg0, dg1, dg2, dy_lru, dy_dil, dy_mem, db0, db1, db2) = _mix_bwd(
        dmerged, z_lru, o_dil_m, om_m, W["w_lru_out"], W["w_dil_out"], W["w_mem_out"], proj, b_gate, S=S)
    Gs["b_gate0"], Gs["b_gate1"], Gs["b_gate2"] = db0, db1, db2

    G["w_mem_out"] = _matmul(om_m, dy_mem, M=MEM_WIDTH, N=D_MODEL, bm=MEM_WIDTH, bn=D_MODEL, name="mm_dw_mem_out",
                             parts=("cols", D_MODEL // N_DEV), **dw)
    dom = _matmul(dy_mem, W["w_mem_out"], M=S, N=MEM_WIDTH, K=D_MODEL, mode="nt", bm=512, bn=MEM_WIDTH, bk=D_MODEL,
                  name="mm_dom")
    dqm, dk_mem, dv_mem = _mem_attn_bwd(proj, kv, om, lse_mem, dom, S=S)
    dkv = jnp.concatenate([dk_mem, dv_mem], axis=1)
    G["w_mem_kv"] = _matmul(mem_n, dkv, M=D_MODEL, N=2 * MEM_WIDTH, K=N_MEM, mode="tn", bm=D_MODEL, bn=2 * MEM_WIDTH,
                            bk=N_MEM, name="mm_dw_kv", out_dtypes=(MXU_DTYPE,), parts=("rows", D_MODEL // N_DEV))
    dmem_n = _matmul(dkv, W["w_mem_kv"], M=N_MEM, N=D_MODEL, K=2 * MEM_WIDTH, mode="nt", bm=N_MEM, bn=D_MODEL,
                     bk=2 * MEM_WIDTH, name="mm_dmem")
    (Gs["g_mem"],) = _rmsnorm_bwd(mem, P["g_mem"], dmem_n, None, rows=N_MEM, name="norm_mem_bwd", dx_dtypes=())

    G["w_dil_out"] = _matmul(o_dil_m, dy_dil, M=256, N=D_MODEL, bm=256, bn=D_MODEL, name="mm_dw_dil_out",
                             parts=("cols", D_MODEL // N_DEV), **dw)
    do_dil = _matmul(dy_dil, W["w_dil_out"], M=S, N=256, K=D_MODEL, mode="nt", bm=512, bn=256, bk=D_MODEL, name="mm_do_dil")
    G["w_lru_out"] = _matmul(z_lru, dy_lru, M=D_RNN, N=D_MODEL, bm=D_RNN, bn=D_MODEL, name="mm_dw_lru_out",
                             parts=("cols", D_MODEL // N_DEV), **dw)
    dz = _matmul(dy_lru, W["w_lru_out"], M=S, N=D_RNN, K=D_MODEL, mode="nt", bm=512, bn=D_RNN, bk=D_MODEL, name="mm_dz_lru")
    tie2 = send_grads({n: G.pop(n) for n in ("w_out", "w_mem_out", "w_mem_kv", "w_dil_out", "w_lru_out")})
    bias = bias + tie2[0, 0]
    delta = _dilated_delta(do_dil, o_dil, S=S)
    dq_parts, dk_parts, dv_parts, dbias = [], [], [], []
    for g in range(len(DIL_GROUPS)):
        dq_g, dk_g, dv_g, db_g = _dilated_bwd(proj, do_dil, lse_dil, delta, bias, g, S=S)
        dq_parts.append(dq_g)
        dk_parts.append(dk_g)
        dv_parts.append(dv_g)
        dbias.append(db_g)
    drel = _dil_bias_bwd(jnp.stack(dbias, axis=0).reshape(len(DIL_GROUPS), DIL_HEADS, SPAN, 2 * SPAN), buckets)
    Gs["rel_bias"] = drel

    dxl, dgl, dcw, dcb, dwa, dwx, dba, dbx, dlam = _lru_bwd(proj, hl, dz, *lru_args, S=S)
    Gs["conv_w"], Gs["conv_b"] = dcw, dcb
    Gs["w_rg_a"], Gs["w_rg_x"] = _block_diag_extract(dwa), _block_diag_extract(dwx)
    Gs["b_rg_a"], Gs["b_rg_x"], Gs["lru_lambda"] = dba, dbx, dlam
    Gs["loss"] = loss

    dproj = jnp.concatenate([dxl, dgl] + dq_parts + dk_parts + dv_parts + [dqm, dg0, dg1, dg2], axis=1)
    cols = D_MODEL // W_IN_PIECES
    tie = []
    for q in range(W_IN_PIECES):
        dw_q = _matmul(h, dproj, M=cols, N=D_IN, K=S, mode="tn", bm=cols, bn=D_IN // 2, bk=512, name=f"mm_dw_in_{q}",
                       a_off=(0, q), out_dtypes=(MXU_DTYPE,), parts=("rows_t", D_IN // N_DEV), deps=tie)
        tie = [send_grads({f"w_in_{q}": dw_q})]
    dh = _matmul(dproj, W["w_in_t"], M=S, N=D_MODEL, K=D_IN, mode="nn", bm=256, bn=D_MODEL, bk=D_IN, name="mm_dh",
                 deps=tie)
    grad_x, Gs["g_mix"] = _rmsnorm_bwd(x, P["g_mix"], dh, dx1, rows=S, name="norm_mix_bwd")
    return grad_x, reduce_small(Gs)


BIG = ("w_in", "w_lru_out", "w_dil_out", "w_mem_kv", "w_mem_out", "w_out", "w_mlp_in", "w_mlp_out")
W_IN_PIECES = 2
COL_SHARDED = ("w_lru_out", "w_dil_out", "w_mem_out", "w_mlp_in")
GATHERED_TRANSPOSED = ("w_mlp_in",)
SMALL = ("g_mix", "b_gate", "conv_b", "w_rg_a", "b_rg_a", "w_rg_x", "b_rg_x", "lru_lambda", "rel_bias", "g_mem",
         "g_mlp", "g_final")
WEIGHTS = ("g_mix", "w_in", "b_gate", "conv_w", "conv_b", "w_rg_a", "b_rg_a", "w_rg_x", "b_rg_x", "lru_lambda",
           "w_lru_out", "rel_bias", "w_dil_out", "g_mem", "w_mem_kv", "w_mem_out", "w_out", "g_mlp", "w_mlp_in",
           "w_mlp_out", "g_final")


def _gathered_to_full(name, gathered):
    if name in COL_SHARDED:
        n, r, c = gathered.shape
        return gathered.transpose(1, 0, 2).reshape(r, n * c)
    n, r, c = gathered.shape
    return gathered.reshape(n * r, c)


SMALL_GRADS = (("g_mix", (1, 1024)), ("b_gate0", (1, 1024)), ("b_gate1", (1, 1024)), ("b_gate2", (1, 1024)),
               ("conv_b", (1, 768)), ("w_rg_a", (12, 64, 64)), ("b_rg_a", (1, 768)), ("w_rg_x", (12, 64, 64)),
               ("b_rg_x", (1, 768)), ("lru_lambda", (1, 768)), ("rel_bias", (32, 128)), ("g_mem", (1, 1024)),
               ("g_mlp", (1, 1024)), ("g_final", (1, 1024)), ("conv_w", (4, 768)), ("loss", (1, 1)))


def _pack(parts):
    flat = jnp.concatenate([p.reshape(-1) for p in parts])
    return jnp.pad(flat, (0, (-flat.shape[0]) % 1024)).reshape(-1, 128)


def _unpack(pack, shapes):
    flat = pack.reshape(-1)
    out, off = [], 0
    for shp in shapes:
        size = math.prod(shp)
        out.append(flat[off:off + size].reshape(shp))
        off += size
    return out


def kernel(x, mem, g_mix, w_in, b_gate, conv_w, conv_b, w_rg_a, b_rg_a, w_rg_x, b_rg_x, lru_lambda, w_lru_out, rel_bias, w_dil_out, g_mem, w_mem_kv, w_mem_out, w_out, g_mlp, w_mlp_in, w_mlp_out, g_final, loss_target, m_g_mix, m_w_in, m_b_gate, m_conv_w, m_conv_b, m_w_rg_a, m_b_rg_a, m_w_rg_x, m_b_rg_x, m_lru_lambda, m_w_lru_out, m_rel_bias, m_w_dil_out, m_g_mem, m_w_mem_kv, m_w_mem_out, m_w_out, m_g_mlp, m_w_mlp_in, m_w_mlp_out, m_g_final, v_g_mix, v_w_in, v_b_gate, v_conv_w, v_conv_b, v_w_rg_a, v_b_rg_a, v_w_rg_x, v_b_rg_x, v_lru_lambda, v_w_lru_out, v_rel_bias, v_w_dil_out, v_g_mem, v_w_mem_kv, v_w_mem_out, v_w_out, v_g_mlp, v_w_mlp_in, v_w_mlp_out, v_g_final):
    w = dict(g_mix=g_mix, w_in=w_in, b_gate=b_gate, conv_w=conv_w, conv_b=conv_b, w_rg_a=w_rg_a, b_rg_a=b_rg_a,
             w_rg_x=w_rg_x, b_rg_x=b_rg_x, lru_lambda=lru_lambda, w_lru_out=w_lru_out, rel_bias=rel_bias,
             w_dil_out=w_dil_out, g_mem=g_mem, w_mem_kv=w_mem_kv, w_mem_out=w_mem_out, w_out=w_out, g_mlp=g_mlp,
             w_mlp_in=w_mlp_in, w_mlp_out=w_mlp_out, g_final=g_final)
    m = dict(g_mix=m_g_mix, w_in=m_w_in, b_gate=m_b_gate, conv_w=m_conv_w, conv_b=m_conv_b, w_rg_a=m_w_rg_a,
             b_rg_a=m_b_rg_a, w_rg_x=m_w_rg_x, b_rg_x=m_b_rg_x, lru_lambda=m_lru_lambda, w_lru_out=m_w_lru_out,
             rel_bias=m_rel_bias, w_dil_out=m_w_dil_out, g_mem=m_g_mem, w_mem_kv=m_w_mem_kv, w_mem_out=m_w_mem_out,
             w_out=m_w_out, g_mlp=m_g_mlp, w_mlp_in=m_w_mlp_in, w_mlp_out=m_w_mlp_out, g_final=m_g_final)
    v = dict(g_mix=v_g_mix, w_in=v_w_in, b_gate=v_b_gate, conv_w=v_conv_w, conv_b=v_conv_b, w_rg_a=v_w_rg_a,
             b_rg_a=v_b_rg_a, w_rg_x=v_w_rg_x, b_rg_x=v_b_rg_x, lru_lambda=v_lru_lambda, w_lru_out=v_w_lru_out,
             rel_bias=v_rel_bias, w_dil_out=v_w_dil_out, g_mem=v_g_mem, w_mem_kv=v_w_mem_kv, w_mem_out=v_w_mem_out,
             w_out=v_w_out, g_mlp=v_g_mlp, w_mlp_in=v_w_mlp_in, w_mlp_out=v_w_mlp_out, g_final=v_g_final)

    my_idx = _dev_index(_my_pos())

    g_in, g_cw = _all_gather([_mx(w["w_in"].T), w["conv_w"]])
    W = {"w_in_t": g_in.reshape(D_IN, D_MODEL), "conv_w": g_cw.transpose(1, 0, 2).reshape(CONV_WIDTH, D_RNN)}
    late, order_after = {}, [g_in]
    for group, names in (("branch", ("w_mem_kv", "w_lru_out", "w_dil_out", "w_mem_out", "w_out")),
                         ("mlp", ("w_mlp_in", "w_mlp_out"))):
        shards = [_mx(w[n].T if n in GATHERED_TRANSPOSED else w[n]) for n in names]
        started = _push_start(shards, [(N_DEV,) + s.shape for s in shards], _gather_refs, f"gather_{group}_start",
                              after=order_after)
        late[group] = (names, shards, started)
        order_after = [started["token"]]
    P = {n: w[n] for n in SMALL}

    def late_weights(group, after):
        names, shards, started = late[group]
        out = {}
        for n, land, own in zip(names, _push_wait(started, after), shards):
            full = lax.dynamic_update_index_in_dim(land, own, my_idx, 0)
            if n in GATHERED_TRANSPOSED:
                out[n + "_t"] = full.reshape(-1, full.shape[2])
            else:
                out[n] = _gathered_to_full(n, full)
        return out

    sent, small = [], {}

    def send_grads(gs):
        names = list(gs)
        parts = [gs[n] for n in names]
        own = [lax.dynamic_index_in_dim(p, my_idx, 0, keepdims=False) for p in parts]
        started = _push_start(parts, [(N_DEV - 1,) + p.shape[1:] for p in parts], _scatter_refs,
                              f"scatter{len(sent)}_start")
        sent.append((names, own, started))
        return started["token"]

    def reduce_small(gs):
        small["pack"] = _pack([gs[n] for n, _ in SMALL_GRADS])
        small["started"] = _push_start([small["pack"]], [(N_DEV,) + small["pack"].shape], _gather_refs, "small_start")
        return small["started"]["token"]

    grad_x, last_token = _local_step(x[0], mem[0], loss_target[0], W, P, late_weights, send_grads, reduce_small,
                                     late["mlp"][2]["token"][0, 0])

    grads, deltas, new_m, new_v = {}, {}, {}, {}
    after = last_token
    for names, own, started in sent[:-W_IN_PIECES]:
        for n, o, land in zip(names, own, _push_wait(started, after)):
            grads[n], deltas[n], new_m[n], new_v[n] = _adamw_landed(w[n], o, land, m[n], v[n], name=f"adamw_{n}")
            after = deltas[n]
    (small_land,) = _push_wait(small["started"], after)
    total = _sum_slots(lax.dynamic_update_index_in_dim(small_land, small["pack"], my_idx, 0))
    summed = dict(zip([n for n, _ in SMALL_GRADS], _unpack(total, [shp for _, shp in SMALL_GRADS])))
    summed["b_gate"] = jnp.concatenate([summed.pop(f"b_gate{b}") for b in range(3)], axis=1)
    summed["rel_bias"] = summed["rel_bias"][:, :3 * DIL_HEADS]
    for n in SMALL:
        grads[n] = summed[n].reshape(w[n].shape)
    small_updates = _adamw_many([w[n] for n in SMALL], [grads[n] for n in SMALL], [m[n] for n in SMALL],
                                [v[n] for n in SMALL])
    for n, (d_, nm_, nv_) in zip(SMALL, small_updates):
        deltas[n], new_m[n], new_v[n] = d_, nm_, nv_
    conv_w_sum, loss_sum = summed["conv_w"], summed["loss"]
    after = total
    prev = None
    for q, (names, own, started) in enumerate(sent[-W_IN_PIECES:]):
        (land,) = _push_wait(started, after)
        prev = _adamw_landed(w["w_in"].T, own[0], land, m["w_in"].T, v["w_in"].T, name=f"adamw_{names[0]}",
                             col_blk=q, prev=prev)
    grads["w_in"], deltas["w_in"], new_m["w_in"], new_v["w_in"] = [t.T for t in prev]
    cw_cols = D_RNN // N_DEV
    grads["conv_w"] = lax.dynamic_slice(conv_w_sum, (0, my_idx * cw_cols), (CONV_WIDTH, cw_cols))
    deltas["conv_w"], new_m["conv_w"], new_v["conv_w"] = _adamw_plain(
        w["conv_w"], grads["conv_w"], m["conv_w"], v["conv_w"], name="adamw_conv_w")

    return (loss_sum.reshape(()), grad_x[None], *[grads[n] for n in WEIGHTS], *[deltas[n] for n in WEIGHTS],
            *[new_m[n] for n in WEIGHTS], *[new_v[n] for n in WEIGHTS])
```

```python
import functools
import math

import jax
import jax.numpy as jnp
from jax import lax
from jax.experimental import pallas as pl
from jax.experimental.pallas import tpu as pltpu

F32 = jnp.float32
MXU_DTYPE = jnp.bfloat16
VMEM_LIMIT_BYTES = 56 * 1024 * 1024
N_DEV = 8

D_MODEL = 1024
N_MEM = 256
MEM_HEADS = 4
MEM_HEAD_DIM = 128
MEM_WIDTH = 512
D_RNN = 768
LRU_BLOCK = 64
N_LRU_BLOCKS = 12
LRU_GROUP = 256
N_LRU_GROUPS = 3
CONV_WIDTH = 4
LRU_C = 8.0
DIL_GROUPS = ((128, 1), (512, 4), (2048, 16))
SPAN = 128
DIL_HEADS = 4
DIL_HEAD_DIM = 64
NUM_BUCKETS = 32
MAX_DISTANCE = 2048
D_FF = 4096
D_IN = 7424
EPS = 1e-6
NEG = -1e30
C_XL, C_GATE, C_QKV, C_QM, C_GATES = 0, 768, 1536, 3840, 4352

ADAM_LR = 0.001
ADAM_B1 = 0.9
ADAM_B2 = 0.999
ADAM_EPS = 1e-08
ADAM_WD = 0.01
ADAM_STEP = 10

MESH = pl.DeviceIdType.MESH
GELU_K = math.sqrt(2.0 / math.pi)


def _cparams(sem=None):
    kw = dict(vmem_limit_bytes=VMEM_LIMIT_BYTES)
    if sem is not None:
        kw["dimension_semantics"] = sem
    return pltpu.CompilerParams(**kw)


def _mx(v):
    return v.astype(MXU_DTYPE)


def _dot(a, b, mode="nn"):
    dims = {"nn": (((1,), (0,)), ((), ())), "nt": (((1,), (1,)), ((), ())), "tn": (((0,), (0,)), ((), ()))}[mode]
    return lax.dot_general(_mx(a), _mx(b), dims, preferred_element_type=F32)


def _colsum(v):
    return jnp.sum(v, axis=0, keepdims=True)


def _matmul(a, b, *, M, N, K, mode, bm, bn, bk, name, out_dtypes=(F32,), epilogue=None, extras=(),
            a_off=(0, 0), b_off=(0, 0), j_outer=False, deps=(), parts=None):
    assert M % bm == 0 and N % bn == 0 and K % bk == 0, (name, M, N, K, bm, bn, bk)
    nm, nn, nk = M // bm, N // bn, K // bk

    def ij(f):
        if j_outer:
            return lambda j, i, k: f(i, j, k)
        return f

    if mode == "tn":
        a_spec = pl.BlockSpec((bk, bm), ij(lambda i, j, k: (k + a_off[0], i + a_off[1])))
    else:
        a_spec = pl.BlockSpec((bm, bk), ij(lambda i, j, k: (i + a_off[0], k + a_off[1])))
    if mode == "nt":
        b_spec = pl.BlockSpec((bn, bk), ij(lambda i, j, k: (j + b_off[0], k + b_off[1])))
    else:
        b_spec = pl.BlockSpec((bk, bn), ij(lambda i, j, k: (k + b_off[0], j + b_off[1])))
    ex_specs = [pl.BlockSpec((bm, bn), ij(functools.partial(lambda i, j, k, o: (i + o[0], j + o[1]), o=off)))
                for _, off in extras]
    if parts is None:
        out_dims = (M, N)
        out_spec = pl.BlockSpec((bm, bn), ij(lambda i, j, k: (i, j)))
    elif parts[0] == "rows":
        r = parts[1]
        assert bm % r == 0
        out_dims = (M // r, r, N)
        out_spec = pl.BlockSpec((bm // r, r, bn), ij(lambda i, j, k: (i, 0, j)))
    elif parts[0] == "rows_t":
        r = parts[1]
        assert bn % r == 0
        out_dims = (N // r, r, M)
        out_spec = pl.BlockSpec((bn // r, r, bm), ij(lambda i, j, k: (j, 0, i)))
    else:
        c = parts[1]
        assert bn % c == 0
        out_dims = (N // c, M, c)
        out_spec = pl.BlockSpec((bn // c, bm, c), ij(lambda i, j, k: (j, i, 0)))
    n_ex, n_out, n_dep = len(extras), len(out_dtypes), len(deps)

    def body(*refs):
        a_ref, b_ref = refs[0], refs[1]
        ex = refs[2:2 + n_ex]
        outs = refs[2 + n_ex + n_dep:2 + n_ex + n_dep + n_out]
        part = _dot(a_ref[...], b_ref[...], mode)

        def finish(acc):
            vals = epilogue(acc, *[e[...] for e in ex]) if epilogue is not None else (acc,)
            for o, v in zip(outs, vals):
                if parts is not None and parts[0] == "rows_t":
                    v = v.T
                v = v.astype(o.dtype)
                if parts is None:
                    o[...] = v
                elif parts[0] in ("rows", "rows_t"):
                    for ch in range(v.shape[0] // parts[1]):
                        o[ch] = v[ch * parts[1]:(ch + 1) * parts[1], :]
                else:
                    for ch in range(bn // parts[1]):
                        o[ch] = v[:, ch * parts[1]:(ch + 1) * parts[1]]

        if nk == 1:
            finish(part)
        else:
            acc_ref = refs[-1]
            k = pl.program_id(2)

            @pl.when(k == 0)
            def _():
                acc_ref[...] = part

            @pl.when(k > 0)
            def _():
                acc_ref[...] += part

            @pl.when(k == nk - 1)
            def _():
                finish(acc_ref[...])

    grid = (nn, nm, nk) if j_outer else (nm, nn, nk)
    res = pl.pallas_call(
        body, name=name, grid=grid,
        in_specs=[a_spec, b_spec] + ex_specs + [pl.BlockSpec(memory_space=pl.ANY)] * n_dep,
        out_specs=[out_spec] * n_out,
        out_shape=[jax.ShapeDtypeStruct(out_dims, dt) for dt in out_dtypes],
        scratch_shapes=[pltpu.VMEM((bm, bn), F32)] if nk > 1 else [],
        compiler_params=_cparams(("parallel", "parallel", "arbitrary")),
    )(a, b, *[e for e, _ in extras], *deps)
    return res[0] if n_out == 1 else res


def _rmsnorm_fwd(x, g, *, rows, name, bt=512):
    bt = min(bt, rows)

    def body(x_ref, g_ref, o_ref):
        xv = x_ref[...]
        r = lax.rsqrt(jnp.mean(xv * xv, axis=-1, keepdims=True) + EPS)
        o_ref[...] = (xv * r * g_ref[...]).astype(o_ref.dtype)

    return pl.pallas_call(
        body, name=name, grid=(rows // bt,),
        in_specs=[pl.BlockSpec((bt, D_MODEL), lambda i: (i, 0)), pl.BlockSpec((1, D_MODEL), lambda i: (0, 0))],
        out_specs=pl.BlockSpec((bt, D_MODEL), lambda i: (i, 0)),
        out_shape=jax.ShapeDtypeStruct((rows, D_MODEL), MXU_DTYPE),
        compiler_params=_cparams(("parallel",)),
    )(x, g.reshape(1, D_MODEL))


def _rms_bwd_tile(xv, gv, dyv):
    r = lax.rsqrt(jnp.mean(xv * xv, axis=-1, keepdims=True) + EPS)
    w = dyv * gv
    dx = r * w - xv * (r * r * r) * jnp.mean(w * xv, axis=-1, keepdims=True)
    dg = _colsum(dyv * xv * r)
    return dx, dg


def _rmsnorm_bwd(x, g, dy, res, *, rows, name, bt=512, dx_dtypes=(F32,)):
    bt = min(bt, rows)
    has_res = res is not None

    def body(*refs):
        x_ref, g_ref, dy_ref = refs[:3]
        res_ref = refs[3] if has_res else None
        outs = refs[3 + int(has_res):]
        dx, dg = _rms_bwd_tile(x_ref[...], g_ref[...], dy_ref[...])
        if has_res:
            dx = dx + res_ref[...]
        dg_ref = outs[-1]

        @pl.when(pl.program_id(0) == 0)
        def _():
            dg_ref[...] = jnp.zeros_like(dg_ref)

        dg_ref[...] += dg
        for o in outs[:-1]:
            o[...] = dx.astype(o.dtype)

    row_spec = pl.BlockSpec((bt, D_MODEL), lambda i: (i, 0))
    vec_spec = pl.BlockSpec((1, D_MODEL), lambda i: (0, 0))
    ins = [x, g.reshape(1, D_MODEL), dy] + ([res] if has_res else [])
    return pl.pallas_call(
        body, name=name, grid=(rows // bt,),
        in_specs=[row_spec, vec_spec, row_spec] + ([row_spec] if has_res else []),
        out_specs=[row_spec] * len(dx_dtypes) + [vec_spec],
        out_shape=[jax.ShapeDtypeStruct((rows, D_MODEL), dt) for dt in dx_dtypes] + [jax.ShapeDtypeStruct((1, D_MODEL), F32)],
        compiler_params=_cparams(("arbitrary",)),
    )(*ins)


def _loss_head(x2, g, tgt, *, rows, bt=512):
    def body(x_ref, g_ref, t_ref, loss_ref, dx_ref, dxm_ref, dg_ref):
        xv, gv = x_ref[...], g_ref[...]
        r = lax.rsqrt(jnp.mean(xv * xv, axis=-1, keepdims=True) + EPS)
        diff = xv * r * gv - t_ref[...]
        part = jnp.sum(jnp.mean(diff * diff, axis=-1, keepdims=True), axis=0, keepdims=True) * 0.5
        dx, dg = _rms_bwd_tile(xv, gv, diff * (1.0 / D_MODEL))

        @pl.when(pl.program_id(0) == 0)
        def _():
            loss_ref[...] = jnp.zeros_like(loss_ref)
            dg_ref[...] = jnp.zeros_like(dg_ref)

        loss_ref[...] += part
        dg_ref[...] += dg
        dx_ref[...] = dx
        dxm_ref[...] = _mx(dx)

    row_spec = pl.BlockSpec((bt, D_MODEL), lambda i: (i, 0))
    vec_spec = pl.BlockSpec((1, D_MODEL), lambda i: (0, 0))
    return pl.pallas_call(
        body, name="loss_head", grid=(rows // bt,),
        in_specs=[row_spec, vec_spec, row_spec],
        out_specs=[pl.BlockSpec((1, 1), lambda i: (0, 0)), row_spec, row_spec, vec_spec],
        out_shape=[jax.ShapeDtypeStruct((1, 1), F32), jax.ShapeDtypeStruct((rows, D_MODEL), F32),
                   jax.ShapeDtypeStruct((rows, D_MODEL), MXU_DTYPE), jax.ShapeDtypeStruct((1, D_MODEL), F32)],
        compiler_params=_cparams(("arbitrary",)),
    )(x2, g.reshape(1, D_MODEL), tgt)


LRU_T = 256


def _gelu(x):
    t = jnp.tanh(GELU_K * (x + 0.044715 * x * x * x))
    return 0.5 * x * (1.0 + t), t


def _gelu_grad(x, t):
    return 0.5 * (1.0 + t) + 0.5 * x * (1.0 - t * t) * GELU_K * (1.0 + 3.0 * 0.044715 * x * x)


def _softplus_neg(lam):
    z = -lam
    u = jnp.exp(-jnp.abs(z))
    w = 1.0 + u
    l1p = jnp.where(w == 1.0, u, jnp.log(w) * u / jnp.where(w == 1.0, 1.0, w - 1.0))
    return jnp.maximum(z, 0.0) + l1p


def _shift_down(cur, prev8, k, row8):
    y = pltpu.roll(cur, k, 0)
    head = jnp.where(row8 < k, pltpu.roll(prev8, k, 0), y[0:8])
    return jnp.concatenate([head, y[8:]], axis=0)


def _shift_up(cur, next8, k, row8):
    n = cur.shape[0]
    y = pltpu.roll(cur, n - k, 0)
    tail = jnp.where(row8 >= 8 - k, pltpu.roll(next8, 8 - k, 0), y[n - 8:n])
    return jnp.concatenate([y[0:n - 8], tail], axis=0)


def _lru_gates(xl, p8, cw, cb, wa, wx, ba, bx, lam, row8, a_mult=None):
    sh = [xl] + [_shift_down(xl, p8, k, row8) for k in (1, 2, 3)]
    xc = cb + cw[3:4] * sh[0] + cw[2:3] * sh[1] + cw[1:2] * sh[2] + cw[0:1] * sh[3]
    r = jax.nn.sigmoid(_dot(xc, wa) + ba)
    i = jax.nn.sigmoid(_dot(xc, wx) + bx)
    sp = _softplus_neg(lam)
    if a_mult is None:
        la = -LRU_C * r * sp
        a = jnp.exp(la)
        mult = jnp.sqrt(jnp.tanh(-la) * (a * a + 1.0))
    else:
        a, mult = a_mult
    return dict(sh=sh, xc=xc, r=r, i=i, sp=sp, a=a, mult=mult)


def _lru_specs(n_t, reverse):
    T = LRU_T
    tt = (lambda t: n_t - 1 - t) if reverse else (lambda t: t)
    blk = lambda col0: pl.BlockSpec((T, LRU_GROUP), lambda g, t: (tt(t), col0 + g))
    prev8 = lambda col0: pl.BlockSpec((8, LRU_GROUP), lambda g, t: (jnp.maximum(tt(t) * (T // 8) - 1, 0), col0 + g))
    vec = lambda rows: pl.BlockSpec((rows, LRU_GROUP), lambda g, t: (0, g))
    wbd = pl.BlockSpec((1, LRU_GROUP, LRU_GROUP), lambda g, t: (g, 0, 0))
    return blk, prev8, vec, wbd


def _lru_fwd(proj, conv_w, conv_b, wa_bd, wx_bd, b_a, b_x, lam, *, S):
    T = LRU_T
    n_t = S // T
    blk, _, vec, wbd = _lru_specs(n_t, False)

    def body(xl_ref, gate_ref, cw_ref, cb_ref, wa_ref, wx_ref, ba_ref, bx_ref, lam_ref,
             hl_ref, z_ref, a_s, m_ref, prev8, hcar, b_s):
        @pl.when(pl.program_id(1) == 0)
        def _():
            prev8[...] = jnp.zeros_like(prev8)
            hcar[...] = jnp.zeros_like(hcar)

        row8 = lax.broadcasted_iota(jnp.int32, (8, LRU_GROUP), 0)
        xl = xl_ref[...]
        q = _lru_gates(xl, prev8[...], cw_ref[...], cb_ref[...], wa_ref[0], wx_ref[0], ba_ref[...], bx_ref[...],
                       lam_ref[...], row8)
        prev8[...] = xl[T - 8:T]
        a_s[...] = q["a"]
        m_ref[...] = q["mult"]
        b_s[...] = q["mult"] * q["i"] * q["xc"]

        def step(c, carry):
            off = pl.multiple_of(c * 8, 8)
            A = a_s[pl.ds(off, 8), :]
            B = b_s[pl.ds(off, 8), :]
            for k in (1, 2, 4):
                a_sh = jnp.where(row8 >= k, pltpu.roll(A, k, 0), 1.0)
                b_sh = jnp.where(row8 >= k, pltpu.roll(B, k, 0), 0.0)
                B = A * b_sh + B
                A = A * a_sh
            h = A * carry + B
            hl_ref[pl.ds(off, 8), :] = h
            return h[7:8, :]

        hcar[...] = lax.fori_loop(0, T // 8, step, hcar[...], unroll=4)
        ge, _ = _gelu(gate_ref[...])
        z_ref[...] = (ge * hl_ref[...]).astype(z_ref.dtype)

    return pl.pallas_call(
        body, name="lru_fwd", grid=(N_LRU_GROUPS, n_t),
        in_specs=[blk(C_XL // LRU_GROUP), blk(C_GATE // LRU_GROUP), vec(4), vec(1), wbd, wbd, vec(1), vec(1), vec(1)],
        out_specs=[blk(0)] * 4,
        out_shape=[jax.ShapeDtypeStruct((S, D_RNN), F32), jax.ShapeDtypeStruct((S, D_RNN), MXU_DTYPE),
                   jax.ShapeDtypeStruct((S, D_RNN), F32), jax.ShapeDtypeStruct((S, D_RNN), F32)],
        scratch_shapes=[pltpu.VMEM((8, LRU_GROUP), F32), pltpu.VMEM((1, LRU_GROUP), F32), pltpu.VMEM((T, LRU_GROUP), F32)],
        compiler_params=_cparams(("parallel", "arbitrary")),
    )(proj, proj, conv_w, conv_b, wa_bd, wx_bd, b_a, b_x, lam)


def _lru_bwd(proj, hl, a_fwd, mult_fwd, dz, conv_w, conv_b, wa_bd, wx_bd, b_a, b_x, lam, *, S):
    T = LRU_T
    n_t = S // T
    blk, prev8s, vec, wbd = _lru_specs(n_t, True)

    def body(xl_ref, xlp_ref, gate_ref, hl_ref, hlp_ref, a_ref, m_ref, dz_ref, cw_ref, cb_ref, wa_ref, wx_ref, ba_ref,
             bx_ref, lam_ref, dxl_ref, dgate_ref, dcw_ref, dcb_ref, dwa_ref, dwx_ref, dba_ref, dbx_ref, dlam_ref,
             next8, gcar, c_s, b_s, l_s):
        t = pl.program_id(1)
        first_chunk = t == n_t - 1

        @pl.when(t == 0)
        def _():
            next8[...] = jnp.zeros_like(next8)
            gcar[...] = jnp.zeros_like(gcar)
            for ref in (dcw_ref, dcb_ref, dwa_ref, dwx_ref, dba_ref, dbx_ref, dlam_ref):
                ref[...] = jnp.zeros_like(ref)

        row8 = lax.broadcasted_iota(jnp.int32, (8, LRU_GROUP), 0)
        rowT = lax.broadcasted_iota(jnp.int32, (T, LRU_GROUP), 0)
        keep = jnp.where(first_chunk, 0.0, 1.0)
        xl = xl_ref[...]
        wa, wx, lam_v = wa_ref[0], wx_ref[0], lam_ref[...]
        q = _lru_gates(xl, xlp_ref[...] * keep, cw_ref[...], cb_ref[...], wa, wx, ba_ref[...], bx_ref[...], lam_v, row8,
                       a_mult=(a_ref[...], m_ref[...]))
        a, mult, r, i, xc, sp = q["a"], q["mult"], q["r"], q["i"], q["xc"], q["sp"]
        hl_v = hl_ref[...]
        dz_v = dz_ref[...]
        gate = gate_ref[...]
        ge, th = _gelu(gate)
        dgate_ref[...] = (dz_v * hl_v * _gelu_grad(gate, th)).astype(dgate_ref.dtype)

        c_s[...] = jnp.where(rowT == T - 1, 0.0, pltpu.roll(a, T - 1, 0))
        b_s[...] = dz_v * ge + jnp.where(rowT == T - 1, gcar[...], 0.0)

        def step(n, carry):
            off = pl.multiple_of((T // 8 - 1 - n) * 8, 8)
            C = c_s[pl.ds(off, 8), :]
            B = b_s[pl.ds(off, 8), :]
            for k in (1, 2, 4):
                c_sh = jnp.where(row8 < 8 - k, pltpu.roll(C, 8 - k, 0), 1.0)
                b_sh = jnp.where(row8 < 8 - k, pltpu.roll(B, 8 - k, 0), 0.0)
                B = B + C * b_sh
                C = C * c_sh
            lam_t = B + C * carry
            l_s[pl.ds(off, 8), :] = lam_t
            return lam_t[0:1, :]

        lax.fori_loop(0, T // 8, step, jnp.zeros((1, LRU_GROUP), F32), unroll=4)
        lmb = l_s[...]
        gcar[...] = a[0:1, :] * lmb[0:1, :]

        h_prev = _shift_down(hl_v, hlp_ref[...] * keep, 1, row8)
        da = lmb * h_prev
        dmult = lmb * i * xc
        di = lmb * mult * xc
        dxc = lmb * mult * i
        dla = da * a - dmult * (a * a) / mult
        dr = dla * (-LRU_C * sp)
        dlam_ref[...] += _colsum(dla * (-LRU_C * r)) * (-jax.nn.sigmoid(-lam_v))
        dpa = dr * r * (1.0 - r)
        dpx = di * i * (1.0 - i)
        dxc = dxc + _dot(dpa, wa, "nt") + _dot(dpx, wx, "nt")
        dwa_ref[0] += _dot(xc, dpa, "tn")
        dwx_ref[0] += _dot(xc, dpx, "tn")
        dba_ref[...] += _colsum(dpa)
        dbx_ref[...] += _colsum(dpx)
        dcb_ref[...] += _colsum(dxc)
        cw = cw_ref[...]
        n8 = next8[...]
        dxl = cw[3:4] * dxc
        for k in (1, 2, 3):
            dxl = dxl + cw[3 - k:4 - k] * _shift_up(dxc, n8, k, row8)
        for k in range(4):
            dcw_ref[3 - k:4 - k, :] += _colsum(dxc * q["sh"][k])
        next8[...] = dxc[0:8]
        dxl_ref[...] = dxl.astype(dxl_ref.dtype)

    res = pl.pallas_call(
        body, name="lru_bwd", grid=(N_LRU_GROUPS, n_t),
        in_specs=[blk(C_XL // LRU_GROUP), prev8s(C_XL // LRU_GROUP), blk(C_GATE // LRU_GROUP), blk(0), prev8s(0), blk(0),
                  blk(0), blk(0), vec(4), vec(1), wbd, wbd, vec(1), vec(1), vec(1)],
        out_specs=[blk(0), blk(0), vec(4), vec(1), wbd, wbd, vec(1), vec(1), vec(1)],
        out_shape=[jax.ShapeDtypeStruct((S, D_RNN), MXU_DTYPE), jax.ShapeDtypeStruct((S, D_RNN), MXU_DTYPE),
                   jax.ShapeDtypeStruct((4, D_RNN), F32), jax.ShapeDtypeStruct((1, D_RNN), F32),
                   jax.ShapeDtypeStruct((N_LRU_GROUPS, LRU_GROUP, LRU_GROUP), F32),
                   jax.ShapeDtypeStruct((N_LRU_GROUPS, LRU_GROUP, LRU_GROUP), F32),
                   jax.ShapeDtypeStruct((1, D_RNN), F32), jax.ShapeDtypeStruct((1, D_RNN), F32),
                   jax.ShapeDtypeStruct((1, D_RNN), F32)],
        scratch_shapes=[pltpu.VMEM((8, LRU_GROUP), F32), pltpu.VMEM((1, LRU_GROUP), F32),
                        pltpu.VMEM((T, LRU_GROUP), F32), pltpu.VMEM((T, LRU_GROUP), F32), pltpu.VMEM((T, LRU_GROUP), F32)],
        compiler_params=_cparams(("parallel", "arbitrary")),
    )(proj, proj, proj, hl, hl, a_fwd, mult_fwd, dz, conv_w, conv_b, wa_bd, wx_bd, b_a, b_x, lam)
    return res


def _block_diag(w):
    w4 = w.reshape(N_LRU_GROUPS, 4, LRU_BLOCK, 1, LRU_BLOCK)
    eye = jnp.eye(4, dtype=w.dtype).reshape(1, 4, 1, 4, 1)
    return (w4 * eye).reshape(N_LRU_GROUPS, LRU_GROUP, LRU_GROUP)


def _block_diag_extract(wbd):
    w5 = wbd.reshape(N_LRU_GROUPS, 4, LRU_BLOCK, 4, LRU_BLOCK)
    return jnp.stack([w5[:, a, :, a, :] for a in range(4)], axis=1).reshape(N_LRU_BLOCKS, LRU_BLOCK, LRU_BLOCK)


def _t5_bucket(dist):
    max_exact = NUM_BUCKETS // 2
    df = jnp.maximum(dist, 1).astype(jnp.float32)
    large = max_exact + (jnp.log(df / max_exact) / math.log(MAX_DISTANCE / max_exact)
                         * (NUM_BUCKETS - max_exact)).astype(jnp.int32)
    large = jnp.minimum(large, NUM_BUCKETS - 1)
    return jnp.where(dist < max_exact, dist, large)


def _band_offsets():
    qi = jnp.arange(SPAN)[:, None]
    kj = jnp.arange(2 * SPAN)[None, :]
    return qi + SPAN - kj


def _dil_buckets():
    off = _band_offsets()
    return jnp.stack([_t5_bucket(jnp.maximum(off, 0) * dil) for _, dil in DIL_GROUPS]).astype(jnp.int32)


def _dil_bias(rel_bias, buckets):
    def body(tbl_ref, bk_ref, o_ref):
        g = pl.program_id(0)
        qi = lax.broadcasted_iota(jnp.int32, (SPAN, 2 * SPAN), 0)
        kj = lax.broadcasted_iota(jnp.int32, (SPAN, 2 * SPAN), 1)
        off = qi + SPAN - kj
        valid = (off >= 0) & (off <= SPAN)
        bk = bk_ref[0]
        for h in range(DIL_HEADS):
            acc = jnp.zeros((SPAN, 2 * SPAN), F32)
            for b in range(NUM_BUCKETS):
                acc = jnp.where(bk == b, tbl_ref[b, g * DIL_HEADS + h], acc)
            o_ref[0, h] = jnp.where(valid, acc, NEG)

    return pl.pallas_call(
        body, name="dil_bias", grid=(3,),
        in_specs=[pl.BlockSpec(memory_space=pltpu.SMEM), pl.BlockSpec((1, SPAN, 2 * SPAN), lambda g: (g, 0, 0))],
        out_specs=pl.BlockSpec((1, DIL_HEADS, SPAN, 2 * SPAN), lambda g: (g, 0, 0, 0)),
        out_shape=jax.ShapeDtypeStruct((3, DIL_HEADS, SPAN, 2 * SPAN), F32),
        compiler_params=_cparams(("parallel",)),
    )(rel_bias, buckets)


def _dil_bias_bwd(dbias, buckets):
    def body(db_ref, bk_ref, o_ref):
        lane = lax.broadcasted_iota(jnp.int32, (1, 128), 1)
        rows = [jnp.zeros((1, 128), F32) for _ in range(NUM_BUCKETS)]
        for g in range(3):
            bk = bk_ref[g]
            for h in range(DIL_HEADS):
                d = db_ref[g, h]
                for b in range(NUM_BUCKETS):
                    tot = jnp.sum(_colsum(jnp.where(bk == b, d, 0.0)), axis=1, keepdims=True)
                    rows[b] = jnp.where(lane == g * DIL_HEADS + h, tot, rows[b])
        for b in range(NUM_BUCKETS):
            o_ref[b:b + 1, :] = rows[b]

    return pl.pallas_call(
        body, name="dil_bias_bwd",
        out_shape=jax.ShapeDtypeStruct((NUM_BUCKETS, 128), F32),
        compiler_params=_cparams(),
    )(dbias, buckets)


DIL_SUBBLOCKS = (4, 2, 1)


def _dil_layout(g, S):
    dil, m = DIL_GROUPS[g][1], DIL_SUBBLOCKS[g]
    sub = SPAN * dil
    col = [(C_QKV + t * 768 + g * 256) // 128 for t in range(3)]
    return dil, m, sub, S // (sub * m), col


def _residue_rows(b, r, dil):
    return pl.ds(b * SPAN * dil + r, SPAN, stride=dil) if dil > 1 else pl.ds(b * SPAN, SPAN)


def _for_residues(dil, fn):
    if dil <= 4:
        for r in range(dil):
            fn(r)
    else:
        lax.fori_loop(0, dil, lambda r, c: (fn(r), c)[1], 0, unroll=2)


def _pair_scores(qm, k2, bias, first_cols):
    s = _dot(qm, k2, "nt") * (DIL_HEAD_DIM ** -0.5) + bias
    kj = lax.broadcasted_iota(jnp.int32, s.shape, 1)
    return jnp.where(kj < first_cols, NEG, s)


def _dilated_fwd(proj, bias, g, *, S):
    dil, m, sub, nc, (qc, kc, vc) = _dil_layout(g, S)
    R = sub * m
    cur = lambda cb: pl.BlockSpec((R, 128), lambda p, i: (i, cb + p))
    prv = lambda cb: pl.BlockSpec((sub, 128), lambda p, i: (jnp.maximum(i * m - 1, 0), cb + p))
    out = pl.BlockSpec((R, 128), lambda p, i: (i, p))

    def body(q_ref, kp_ref, kc_ref, vp_ref, vc_ref, b_ref, o_ref, lse_ref):
        lane = lax.broadcasted_iota(jnp.int32, (SPAN, 128), 1)
        sels = (lane < DIL_HEAD_DIM, lane >= DIL_HEAD_DIM)
        for b in range(m):
            first_cols = jnp.where(pl.program_id(1) == 0, SPAN, 0) if b == 0 else 0

            def one(r, b=b, first_cols=first_cols):
                rows = _residue_rows(b, r, dil)
                before = (kc_ref, vc_ref, _residue_rows(b - 1, r, dil)) if b else (kp_ref, vp_ref, _residue_rows(0, r, dil))
                q2 = q_ref[rows, :]
                k2 = _mx(jnp.concatenate([before[0][before[2], :], kc_ref[rows, :]], axis=0))
                v2 = _mx(jnp.concatenate([before[1][before[2], :], vc_ref[rows, :]], axis=0))
                qq = jnp.concatenate([jnp.where(sels[0], q2, 0.0), jnp.where(sels[1], q2, 0.0)], axis=0)
                s = _pair_scores(qq, k2, b_ref[0, 0], first_cols)
                mx = jnp.max(s, axis=-1, keepdims=True)
                p = jnp.exp(s - mx)
                den = jnp.sum(p, axis=-1, keepdims=True)
                o = _dot(p, v2) / den
                st = mx + jnp.log(den)
                o_ref[rows, :] = jnp.where(sels[0], o[0:SPAN], o[SPAN:2 * SPAN])
                lse_ref[rows, :] = jnp.where(lane == 0, st[0:SPAN], jnp.where(lane == 1, st[SPAN:2 * SPAN], 0.0))

            _for_residues(dil, one)

    return pl.pallas_call(
        body, name=f"dil_fwd{g}", grid=(2, nc),
        in_specs=[cur(qc), prv(kc), cur(kc), prv(vc), cur(vc),
                  pl.BlockSpec((1, 1, 2 * SPAN, 2 * SPAN), lambda p, i: (g, p, 0, 0))],
        out_specs=[out, out],
        out_shape=[jax.ShapeDtypeStruct((S, 256), F32), jax.ShapeDtypeStruct((S, 256), F32)],
        compiler_params=_cparams(("parallel", "parallel")),
    )(proj, proj, proj, proj, proj, bias.reshape(3, 2, 2 * SPAN, 2 * SPAN))


def _dilated_bwd(proj, do, lse, delta, bias, g, *, S, into=None):
    dil, m, sub, nc, (qc, kc, vc) = _dil_layout(g, S)
    R = sub * m
    cl = lambda i: jnp.minimum(i, nc - 1)
    cur = lambda cb: pl.BlockSpec((R, 128), lambda p, i: (cl(i), cb + p))
    prv = lambda cb: pl.BlockSpec((sub, 128), lambda p, i: (jnp.maximum(cl(i) * m - 1, 0), cb + p))
    q_out = pl.BlockSpec((R, 128), lambda p, i: (cl(i), 2 * g + p))
    kv_out = pl.BlockSpec((R, 128), lambda p, i: (jnp.maximum(i - 1, 0), 2 * g + p))
    scale = DIL_HEAD_DIM ** -0.5
    n_into = 0 if into is None else 3

    def body(q_ref, kp_ref, kc_ref, vp_ref, vc_ref, do_ref, lse_ref, dl_ref, b_ref, *rest):
        dq_ref, dk_ref, dv_ref, db_ref, dq_s, kc_s, vc_s, kp_s, vp_s, kcar, vcar = rest[n_into:]
        i = pl.program_id(1)

        @pl.when(i == 0)
        def _():
            kcar[...] = jnp.zeros_like(kcar)
            vcar[...] = jnp.zeros_like(vcar)
            db_ref[...] = jnp.zeros_like(db_ref)

        @pl.when(i < nc)
        def _():
            lane = lax.broadcasted_iota(jnp.int32, (SPAN, 128), 1)
            sels = (lane < DIL_HEAD_DIM, lane >= DIL_HEAD_DIM)
            for b in range(m):
                first_cols = jnp.where(i == 0, SPAN, 0) if b == 0 else 0

                def one(r, b=b, first_cols=first_cols):
                    rows = _residue_rows(b, r, dil)
                    rows_before = _residue_rows(b - 1 if b else 0, r, dil)
                    k_before, v_before = (kc_ref, vc_ref) if b else (kp_ref, vp_ref)
                    q2, do2 = q_ref[rows, :], do_ref[rows, :]
                    k2 = _mx(jnp.concatenate([k_before[rows_before, :], kc_ref[rows, :]], axis=0))
                    v2 = _mx(jnp.concatenate([v_before[rows_before, :], vc_ref[rows, :]], axis=0))
                    lse_t, dl_t = lse_ref[rows, :], dl_ref[rows, :]
                    qq = _mx(jnp.concatenate([jnp.where(sels[0], q2, 0.0), jnp.where(sels[1], q2, 0.0)], axis=0))
                    dd = _mx(jnp.concatenate([jnp.where(sels[0], do2, 0.0), jnp.where(sels[1], do2, 0.0)], axis=0))
                    lse2 = jnp.concatenate([lse_t[:, 0:1], lse_t[:, 1:2]], axis=0)
                    dl2 = jnp.concatenate([dl_t[:, 0:1], dl_t[:, 1:2]], axis=0)
                    p = jnp.exp(_pair_scores(qq, k2, b_ref[0, 0], first_cols) - lse2)
                    ds = p * (_dot(dd, v2, "nt") - dl2)
                    db_ref[0] += ds
                    dqq = _dot(ds, k2) * scale
                    dq2 = jnp.where(sels[0], dqq[0:SPAN], dqq[SPAN:2 * SPAN])
                    dk2 = _dot(ds, qq, "tn") * scale
                    dv2 = _dot(p, dd, "tn")
                    dq_s[rows, :] = dq2
                    kc_s[rows, :] = dk2[SPAN:2 * SPAN]
                    vc_s[rows, :] = dv2[SPAN:2 * SPAN]
                    if b:
                        kc_s[rows_before, :] += dk2[0:SPAN]
                        vc_s[rows_before, :] += dv2[0:SPAN]
                    else:
                        kp_s[rows_before, :] = dk2[0:SPAN]
                        vp_s[rows_before, :] = dv2[0:SPAN]

                _for_residues(dil, one)
            dq_ref[...] = dq_s[...].astype(dq_ref.dtype)
            last = pl.ds((m - 1) * sub, sub)
            kcar[last, :] += kp_s[...]
            vcar[last, :] += vp_s[...]
            dk_ref[...] = kcar[...].astype(dk_ref.dtype)
            dv_ref[...] = vcar[...].astype(dv_ref.dtype)
            kcar[...] = kc_s[...]
            vcar[...] = vc_s[...]

        @pl.when(i == nc)
        def _():
            dk_ref[...] = kcar[...].astype(dk_ref.dtype)
            dv_ref[...] = vcar[...].astype(dv_ref.dtype)

    stat = pl.BlockSpec((R, 128), lambda p, i: (cl(i), p))
    big = jax.ShapeDtypeStruct((S, len(DIL_GROUPS) * 256), MXU_DTYPE)
    return pl.pallas_call(
        body, name=f"dil_bwd{g}", grid=(2, nc + 1),
        in_specs=[cur(qc), prv(kc), cur(kc), prv(vc), cur(vc), stat, stat, stat,
                  pl.BlockSpec((1, 1, 2 * SPAN, 2 * SPAN), lambda p, i: (g, p, 0, 0))]
        + [pl.BlockSpec(memory_space=pl.ANY)] * n_into,
        out_specs=[q_out, kv_out, kv_out, pl.BlockSpec((1, 2 * SPAN, 2 * SPAN), lambda p, i: (p, 0, 0))],
        out_shape=[big, big, big, jax.ShapeDtypeStruct((2, 2 * SPAN, 2 * SPAN), F32)],
        input_output_aliases={9 + j: j for j in range(n_into)},
        scratch_shapes=[pltpu.VMEM((R, 128), F32)] * 3 + [pltpu.VMEM((sub, 128), F32)] * 2 + [pltpu.VMEM((R, 128), F32)] * 2,
        compiler_params=_cparams(("parallel", "arbitrary")),
    )(proj, proj, proj, proj, proj, do, lse, delta, bias.reshape(3, 2, 2 * SPAN, 2 * SPAN), *(into or ()))


def _dilated_merge(os_, lses, *, S, bt=512):
    tile = pl.BlockSpec((bt, 128), lambda i, p: (i, p))

    def body(o0, o1, o2, l0, l1, l2, o_ref, om_ref, lse_ref):
        lane = lax.broadcasted_iota(jnp.int32, (bt, 128), 1)
        lo = lane < DIL_HEAD_DIM
        ls = [l0[...], l1[...], l2[...]]
        ws, stat = [], jnp.zeros((bt, 128), F32)
        for e in range(2):
            a = [l[:, e:e + 1] for l in ls]
            m = jnp.maximum(jnp.maximum(a[0], a[1]), a[2])
            ex = [jnp.exp(v - m) for v in a]
            tot = ex[0] + ex[1] + ex[2]
            ws.append([v / tot for v in ex])
            stat = jnp.where(lane == e, m + jnp.log(tot), stat)
        acc = jnp.zeros((bt, 128), F32)
        for gi, o in enumerate((o0, o1, o2)):
            acc = acc + jnp.where(lo, ws[0][gi], ws[1][gi]) * o[...]
        o_ref[...] = acc
        om_ref[...] = _mx(acc)
        lse_ref[...] = stat

    return pl.pallas_call(
        body, name="dil_merge", grid=(S // bt, 2),
        in_specs=[tile] * 6, out_specs=[tile, tile, tile],
        out_shape=[jax.ShapeDtypeStruct((S, 256), F32), jax.ShapeDtypeStruct((S, 256), MXU_DTYPE),
                   jax.ShapeDtypeStruct((S, 256), F32)],
        compiler_params=_cparams(("parallel", "parallel")),
    )(*os_, *lses)


def _dilated_delta(do, o, *, S, bt=512):
    tile = pl.BlockSpec((bt, 128), lambda i, p: (i, p))

    def body(do_ref, o_ref, d_ref):
        lane = lax.broadcasted_iota(jnp.int32, (bt, 128), 1)
        prod = do_ref[...] * o_ref[...]
        d0 = jnp.sum(jnp.where(lane < DIL_HEAD_DIM, prod, 0.0), axis=-1, keepdims=True)
        d1 = jnp.sum(jnp.where(lane >= DIL_HEAD_DIM, prod, 0.0), axis=-1, keepdims=True)
        d_ref[...] = jnp.where(lane == 0, d0, jnp.where(lane == 1, d1, 0.0))

    return pl.pallas_call(
        body, name="dil_delta", grid=(S // bt, 2), in_specs=[tile, tile], out_specs=tile,
        out_shape=jax.ShapeDtypeStruct((S, 256), F32), compiler_params=_cparams(("parallel", "parallel")),
    )(do, o)


MEM_T = 512
QM_BLK = C_QM // MEM_HEAD_DIM


def _mem_attn_fwd(proj, kv, *, S):
    scale = MEM_HEAD_DIM ** -0.5

    def body(q_ref, k_ref, v_ref, o_ref, om_ref, lse_ref):
        s = _dot(q_ref[...], k_ref[...], "nt") * scale
        m = jnp.max(s, axis=-1, keepdims=True)
        p = jnp.exp(s - m)
        den = jnp.sum(p, axis=-1, keepdims=True)
        o = _dot(p, v_ref[...]) / den
        o_ref[...] = o
        om_ref[...] = _mx(o)
        lse_ref[0] = m + jnp.log(den)

    return pl.pallas_call(
        body, name="mem_attn_fwd", grid=(S // MEM_T, MEM_HEADS),
        in_specs=[pl.BlockSpec((MEM_T, MEM_HEAD_DIM), lambda i, h: (i, QM_BLK + h)),
                  pl.BlockSpec((N_MEM, MEM_HEAD_DIM), lambda i, h: (0, h)),
                  pl.BlockSpec((N_MEM, MEM_HEAD_DIM), lambda i, h: (0, MEM_HEADS + h))],
        out_specs=[pl.BlockSpec((MEM_T, MEM_HEAD_DIM), lambda i, h: (i, h)),
                   pl.BlockSpec((MEM_T, MEM_HEAD_DIM), lambda i, h: (i, h)),
                   pl.BlockSpec((1, MEM_T, 1), lambda i, h: (h, i, 0))],
        out_shape=[jax.ShapeDtypeStruct((S, MEM_WIDTH), F32), jax.ShapeDtypeStruct((S, MEM_WIDTH), MXU_DTYPE),
                   jax.ShapeDtypeStruct((MEM_HEADS, S, 1), F32)],
        compiler_params=_cparams(("parallel", "parallel")),
    )(proj, kv, kv)


def _mem_attn_bwd(proj, kv, om, lse, dom, *, S):
    scale = MEM_HEAD_DIM ** -0.5

    def body(q_ref, k_ref, v_ref, o_ref, lse_ref, do_ref, dq_ref, dk_ref, dv_ref):
        @pl.when(pl.program_id(1) == 0)
        def _():
            dk_ref[...] = jnp.zeros_like(dk_ref)
            dv_ref[...] = jnp.zeros_like(dv_ref)

        qv, kv_, vv, dov = q_ref[...], k_ref[...], v_ref[...], do_ref[...]
        p = jnp.exp(_dot(qv, kv_, "nt") * scale - lse_ref[0])
        delta = jnp.sum(dov * o_ref[...], axis=-1, keepdims=True)
        ds = p * (_dot(dov, vv, "nt") - delta)
        dq_ref[...] = (_dot(ds, kv_) * scale).astype(dq_ref.dtype)
        dk_ref[...] += _dot(ds, qv, "tn") * scale
        dv_ref[...] += _dot(p, dov, "tn")

    tile = pl.BlockSpec((MEM_T, MEM_HEAD_DIM), lambda h, i: (i, h))
    kvo = pl.BlockSpec((N_MEM, MEM_HEAD_DIM), lambda h, i: (0, h))
    return pl.pallas_call(
        body, name="mem_attn_bwd", grid=(MEM_HEADS, S // MEM_T),
        in_specs=[pl.BlockSpec((MEM_T, MEM_HEAD_DIM), lambda h, i: (i, QM_BLK + h)),
                  pl.BlockSpec((N_MEM, MEM_HEAD_DIM), lambda h, i: (0, h)),
                  pl.BlockSpec((N_MEM, MEM_HEAD_DIM), lambda h, i: (0, MEM_HEADS + h)),
                  tile, pl.BlockSpec((1, MEM_T, 1), lambda h, i: (h, i, 0)), tile],
        out_specs=[tile, kvo, kvo],
        out_shape=[jax.ShapeDtypeStruct((S, MEM_WIDTH), MXU_DTYPE), jax.ShapeDtypeStruct((N_MEM, MEM_WIDTH), F32),
                   jax.ShapeDtypeStruct((N_MEM, MEM_WIDTH), F32)],
        compiler_params=_cparams(("parallel", "arbitrary")),
    )(proj, kv, kv, om, lse, dom)


MIX_BM = 1024
MIX_BN = 256
GATES_BLK = C_GATES // MIX_BN


def _mix_specs(j_outer):
    ix = (lambda f: (lambda j, i: f(i, j))) if j_outer else (lambda f: f)
    act = lambda width: pl.BlockSpec((MIX_BM, width), ix(lambda i, j: (i, 0)))
    wgt = lambda width: pl.BlockSpec((width, MIX_BN), ix(lambda i, j: (0, j)))
    gate = lambda b: pl.BlockSpec((MIX_BM, MIX_BN), ix(lambda i, j: (i, GATES_BLK + 4 * b + j)))
    bias = lambda b: pl.BlockSpec((1, MIX_BN), ix(lambda i, j: (0, 4 * b + j)))
    tile = pl.BlockSpec((MIX_BM, MIX_BN), ix(lambda i, j: (i, j)))
    return act, wgt, gate, bias, tile


def _mix_fwd(z_lru, o_dil, om, w_lru, w_dil, w_mem, proj, b_gate, *, S):
    act, wgt, gate, bias, tile = _mix_specs(False)

    def body(zl, od, mo, wl, wd, wm, g0, g1, g2, b0, b1, b2, o_ref):
        acc = jax.nn.sigmoid(g0[...] + b0[...]) * _dot(zl[...], wl[...])
        acc += jax.nn.sigmoid(g1[...] + b1[...]) * _dot(od[...], wd[...])
        acc += jax.nn.sigmoid(g2[...] + b2[...]) * _dot(mo[...], wm[...])
        o_ref[...] = acc.astype(o_ref.dtype)

    return pl.pallas_call(
        body, name="mix_fwd", grid=(S // MIX_BM, D_MODEL // MIX_BN),
        in_specs=[act(D_RNN), act(256), act(MEM_WIDTH), wgt(D_RNN), wgt(256), wgt(MEM_WIDTH),
                  gate(0), gate(1), gate(2), bias(0), bias(1), bias(2)],
        out_specs=tile, out_shape=jax.ShapeDtypeStruct((S, D_MODEL), MXU_DTYPE),
        compiler_params=_cparams(("parallel", "parallel")),
    )(z_lru, o_dil, om, w_lru, w_dil, w_mem, proj, proj, proj, b_gate, b_gate, b_gate)


def _mix_bwd(dmerged, z_lru, o_dil, om, w_lru, w_dil, w_mem, proj, b_gate, *, S):
    act, wgt, gate, bias, tile = _mix_specs(True)

    def body(dm, zl, od, mo, wl, wd, wm, g0, g1, g2, b0, b1, b2,
             dg0, dg1, dg2, dy0, dy1, dy2, db0, db1, db2):
        @pl.when(pl.program_id(1) == 0)
        def _():
            for r in (db0, db1, db2):
                r[...] = jnp.zeros_like(r)

        dmv = dm[...]
        for act_ref, w_ref, g_ref, b_ref, dg_ref, dy_ref, db_ref in (
                (zl, wl, g0, b0, dg0, dy0, db0), (od, wd, g1, b1, dg1, dy1, db1), (mo, wm, g2, b2, dg2, dy2, db2)):
            y = _dot(act_ref[...], w_ref[...])
            gt = jax.nn.sigmoid(g_ref[...] + b_ref[...])
            dgate = dmv * y * gt * (1.0 - gt)
            dg_ref[...] = dgate.astype(dg_ref.dtype)
            dy_ref[...] = (dmv * gt).astype(dy_ref.dtype)
            db_ref[...] += _colsum(dgate)

    big = jax.ShapeDtypeStruct((S, D_MODEL), MXU_DTYPE)
    vec = jax.ShapeDtypeStruct((1, D_MODEL), F32)
    vspec = pl.BlockSpec((1, MIX_BN), lambda j, i: (0, j))
    return pl.pallas_call(
        body, name="mix_bwd", grid=(D_MODEL // MIX_BN, S // MIX_BM),
        in_specs=[tile, act(D_RNN), act(256), act(MEM_WIDTH), wgt(D_RNN), wgt(256), wgt(MEM_WIDTH),
                  gate(0), gate(1), gate(2), bias(0), bias(1), bias(2)],
        out_specs=[tile] * 6 + [vspec] * 3, out_shape=[big] * 6 + [vec] * 3,
        compiler_params=_cparams(("parallel", "arbitrary")),
    )(dmerged, z_lru, o_dil, om, w_lru, w_dil, w_mem, proj, proj, proj, b_gate, b_gate, b_gate)


def _adamw_math(w, g, m, v):
    m = ADAM_B1 * m + (1.0 - ADAM_B1) * g
    v = ADAM_B2 * v + (1.0 - ADAM_B2) * (g * g)
    m_hat = m / (1.0 - ADAM_B1 ** ADAM_STEP)
    v_hat = v / (1.0 - ADAM_B2 ** ADAM_STEP)
    delta = -ADAM_LR * (m_hat / (jnp.sqrt(v_hat) + ADAM_EPS) + ADAM_WD * w)
    return delta, m, v


def _adamw_landed(w, own, land, m, v, *, name, col_blk=0, prev=None):
    R = w.shape[0]
    n_parts, C = land.shape[0], land.shape[2]
    br = next(d for d in (256, 464, 128) if R % d == 0)
    tile = pl.BlockSpec((br, C), lambda i: (i, col_blk))
    part = pl.BlockSpec((br, C), lambda i: (i, 0))
    n_prev = 0 if prev is None else 4

    def body(w_ref, o_ref, l_ref, m_ref, v_ref, *rest):
        g_ref, d_ref, nm_ref, nv_ref = rest[n_prev:]
        g = o_ref[...].astype(F32)
        for p in range(n_parts):
            g = g + l_ref[p].astype(F32)
        d, nm, nv = _adamw_math(w_ref[...], g, m_ref[...], v_ref[...])
        g_ref[...] = g
        d_ref[...] = d
        nm_ref[...] = nm
        nv_ref[...] = nv

    return pl.pallas_call(
        body, name=name, grid=(R // br,),
        in_specs=[tile, part, pl.BlockSpec((n_parts, br, C), lambda i: (0, i, 0)), tile, tile]
        + [pl.BlockSpec(memory_space=pl.ANY)] * n_prev,
        out_specs=[tile] * 4, out_shape=[jax.ShapeDtypeStruct(w.shape, F32)] * 4,
        input_output_aliases={5 + j: j for j in range(n_prev)},
        compiler_params=_cparams(("parallel",)),
    )(w, own, land, m, v, *(prev or ()))


def _adamw_plain(w, g, m, v, *, name):
    def body(w_ref, g_ref, m_ref, v_ref, d_ref, nm_ref, nv_ref):
        d, nm, nv = _adamw_math(w_ref[...], g_ref[...], m_ref[...], v_ref[...])
        d_ref[...] = d
        nm_ref[...] = nm
        nv_ref[...] = nv

    return pl.pallas_call(
        body, name=name, out_shape=[jax.ShapeDtypeStruct(w.shape, F32)] * 3, compiler_params=_cparams(),
    )(w, g, m, v)


def _my_pos():
    return lax.axis_index("x"), lax.axis_index("y"), lax.axis_index("c")


def _dev_index(p):
    return 4 * p[0] + 2 * p[1] + p[2]


def _all_gather(shards):
    n = len(shards)
    hbm = pl.BlockSpec(memory_space=pl.ANY)

    def body(*refs):
        ins, outs = refs[:n], refs[n:2 * n]
        send_sems, recv_sems, local_sems = refs[2 * n:]
        x, y, c = _my_pos()
        me, sibling = (x, y, c), (x, y, 1 - c)
        chips = [(1 - x, y), (x, 1 - y), (1 - x, 1 - y)]

        def copy(a, k, block, to, src=None):
            dst = outs[a].at[_dev_index(block)]
            return pltpu.make_async_remote_copy(
                src_ref=dst if src is None else src, dst_ref=dst,
                send_sem=send_sems.at[a, k], recv_sem=recv_sems.at[a, k], device_id=to, device_id_type=MESH)

        mine = [pltpu.make_async_copy(ins[a], outs[a].at[_dev_index(me)], local_sems.at[a]) for a in range(n)]
        for cp in mine:
            cp.start()
        first = []
        for a in range(n):
            first.append(copy(a, 0, me, sibling, src=ins[a]))
            first += [copy(a, 1 + j, me, (*chip, c), src=ins[a]) for j, chip in enumerate(chips)]
        for cp in first:
            cp.start()
        passed = []
        for j, chip in enumerate(chips):
            for a in range(n):
                copy(a, 1 + j, (*chip, c), me).wait_recv()
                fwd = copy(a, 4 + j, (*chip, c), sibling)
                fwd.start()
                passed.append(fwd)
        for a in range(n):
            copy(a, 0, sibling, me).wait_recv()
        for j, chip in enumerate(chips):
            for a in range(n):
                copy(a, 4 + j, (*chip, 1 - c), me).wait_recv()
        for cp in first + passed:
            cp.wait_send()
        for cp in mine:
            cp.wait()

    return pl.pallas_call(
        body, name="all_gather_weights",
        in_specs=[hbm] * n, out_specs=[hbm] * n,
        out_shape=[jax.ShapeDtypeStruct((N_DEV,) + s.shape, s.dtype) for s in shards],
        scratch_shapes=[pltpu.SemaphoreType.DMA((n, 7)), pltpu.SemaphoreType.DMA((n, 7)), pltpu.SemaphoreType.DMA((n,))],
        compiler_params=pltpu.CompilerParams(has_side_effects=True),
    )(*shards)


def _peers(me):
    x, y, c = me
    out = []
    for k in range(1, 8):
        fx, fy, fc = (k >> 2) & 1, (k >> 1) & 1, k & 1
        out.append((k - 1, (1 - x if fx else x, 1 - y if fy else y, 1 - c if fc else c)))
    return out


HBM_SPEC = pl.BlockSpec(memory_space=pltpu.HBM)
SEM_SPEC = pl.BlockSpec(memory_space=pltpu.SEMAPHORE)
DATAFLOW_EFFECT = pltpu.SideEffectType.DATAFLOW_SIDE_EFFECTING


def _gather_refs(src, land, me, peer, k):
    return src, land.at[_dev_index(me)]


def _scatter_refs(src, land, me, peer, k):
    return src.at[_dev_index(peer)], land.at[k]


def _push_start(srcs, land_shapes, refs_of, name, after=()):
    n, n_after = len(srcs), len(after)

    def body(*refs):
        ins, lands = refs[:n], refs[n:2 * n]
        send_sems, recv_sems, token = refs[2 * n + n_after], refs[2 * n + n_after + 1], refs[-1]
        me = _my_pos()
        for k, peer in _peers(me):
            for a in range(n):
                src, dst = refs_of(ins[a], lands[a], me, peer, k)
                pltpu.make_async_remote_copy(src_ref=src, dst_ref=dst, send_sem=send_sems.at[7 * a + k],
                                             recv_sem=recv_sems.at[7 * a + k], device_id=peer, device_id_type=MESH).start()
        token[...] = jnp.zeros_like(token)

    lands = [lax.empty(shp, s.dtype) for shp, s in zip(land_shapes, srcs)]
    hbm = lambda a: pltpu.with_memory_space_constraint(a, pltpu.HBM)
    res = pl.pallas_call(
        body, name=name,
        out_shape=(pltpu.SemaphoreType.DMA((7 * n,)), pltpu.SemaphoreType.DMA((7 * n,)),
                   *[pltpu.HBM(s.shape, s.dtype) for s in srcs], *[pltpu.HBM(l.shape, l.dtype) for l in lands],
                   jax.ShapeDtypeStruct((8, 128), F32)),
        in_specs=[HBM_SPEC] * (2 * n) + [pl.BlockSpec(memory_space=pl.ANY)] * n_after,
        out_specs=(SEM_SPEC, SEM_SPEC, *[HBM_SPEC] * (2 * n), pl.BlockSpec(memory_space=pltpu.VMEM)),
        input_output_aliases={i: 2 + i for i in range(2 * n)},
        compiler_params=pltpu.CompilerParams(has_side_effects=DATAFLOW_EFFECT),
    )(*[hbm(s) for s in srcs], *[hbm(l) for l in lands], *after)
    return dict(sems=(res[0], res[1]), srcs=list(res[2:2 + n]), lands=list(res[2 + n:2 + 2 * n]), token=res[-1], n=n,
                refs_of=refs_of, name=name)


def _push_wait(started, after):
    n, refs_of = started["n"], started["refs_of"]
    after = list(after) if isinstance(after, (list, tuple)) else [after]

    def body(*refs):
        ins, lands = refs[:n], refs[n:2 * n]
        send_sems, recv_sems = refs[2 * n], refs[2 * n + 1]
        me = _my_pos()
        for k, peer in _peers(me):
            for a in range(n):
                src, dst = refs_of(ins[a], lands[a], me, peer, k)
                cp = pltpu.make_async_remote_copy(src_ref=src, dst_ref=dst, send_sem=send_sems.at[7 * a + k],
                                                  recv_sem=recv_sems.at[7 * a + k], device_id=peer, device_id_type=MESH)
                cp.wait_send()
                cp.wait_recv()

    arrs = started["srcs"] + started["lands"]
    res = pl.pallas_call(
        body, name=started["name"].replace("start", "wait"),
        out_shape=tuple(pltpu.HBM(a.shape, a.dtype) for a in arrs),
        in_specs=[HBM_SPEC] * (2 * n) + [SEM_SPEC, SEM_SPEC] + [pl.BlockSpec(memory_space=pl.ANY)] * len(after),
        out_specs=tuple([HBM_SPEC] * (2 * n)),
        input_output_aliases={i: i for i in range(2 * n)},
        compiler_params=pltpu.CompilerParams(has_side_effects=DATAFLOW_EFFECT),
    )(*arrs, *started["sems"], *after)
    return list(res[n:2 * n])


def _sum_slots(slots):
    def body(in_ref, out_ref):
        acc = in_ref[0]
        for d in range(1, N_DEV):
            acc = acc + in_ref[d]
        out_ref[...] = acc

    return pl.pallas_call(body, name="sum_small", out_shape=jax.ShapeDtypeStruct(slots.shape[1:], F32),
                          compiler_params=_cparams())(slots)


def _adamw_many(ws, gs, ms, vs):
    n = len(ws)

    def body(*refs):
        for i in range(n):
            w_ref, g_ref, m_ref, v_ref = (refs[j * n + i] for j in range(4))
            d_, nm, nv = _adamw_math(w_ref[...], g_ref[...], m_ref[...], v_ref[...])
            for j, val in enumerate((d_, nm, nv)):
                refs[(4 + j) * n + i][...] = val

    res = pl.pallas_call(body, name="adamw_small", out_shape=[jax.ShapeDtypeStruct(w_.shape, F32) for w_ in ws] * 3,
                         compiler_params=_cparams())(*ws, *gs, *ms, *vs)
    return [(res[i], res[n + i], res[2 * n + i]) for i in range(n)]


def _local_step(x, mem, tgt, W, P, late_weights, send_grads, reduce_small, tie0):
    S = x.shape[0]
    W = dict(W)
    h = _rmsnorm_fwd(x, P["g_mix"] + tie0, rows=S, name="norm_mix")
    proj = _matmul(h, W["w_in_t"], M=S, N=D_IN, K=D_MODEL, mode="nt", bm=512, bn=D_IN // 2, bk=D_MODEL, name="mm_in",
                   j_outer=True)

    wa_bd, wx_bd = _mx(_block_diag(P["w_rg_a"])), _mx(_block_diag(P["w_rg_x"]))
    lru_args = (W["conv_w"], P["conv_b"].reshape(1, -1), wa_bd, wx_bd, P["b_rg_a"].reshape(1, -1),
                P["b_rg_x"].reshape(1, -1), P["lru_lambda"].reshape(1, -1))
    hl, z_lru, a_lru, mult_lru = _lru_fwd(proj, *lru_args, S=S)

    buckets = _dil_buckets()
    bias = _dil_bias(P["rel_bias"], buckets)
    group_out = [_dilated_fwd(proj, bias, g, S=S) for g in range(len(DIL_GROUPS))]
    o_dil, o_dil_m, lse_dil = _dilated_merge([o for o, _ in group_out], [l for _, l in group_out], S=S)

    W.update(late_weights("branch", [o_dil, z_lru]))
    mem_n = _rmsnorm_fwd(mem, P["g_mem"], rows=N_MEM, name="norm_mem")
    kv = _matmul(mem_n, W["w_mem_kv"], M=N_MEM, N=2 * MEM_WIDTH, K=D_MODEL, mode="nn", bm=N_MEM, bn=512, bk=D_MODEL,
                 name="mm_kv")
    om, om_m, lse_mem = _mem_attn_fwd(proj, kv, S=S)
    b_gate = P["b_gate"].reshape(1, -1)
    merged = _mix_fwd(z_lru, o_dil_m, om_m, W["w_lru_out"], W["w_dil_out"], W["w_mem_out"], proj, b_gate, S=S)
    x1 = _matmul(merged, W["w_out"], M=S, N=D_MODEL, K=D_MODEL, mode="nn", bm=512, bn=D_MODEL, bk=D_MODEL, name="mm_out",
                 epilogue=lambda acc, r: (r + acc,), extras=[(x, (0, 0))])
    hm = _rmsnorm_fwd(x1, P["g_mlp"], rows=S, name="norm_mlp")
    W.update(late_weights("mlp", [hm]))

    def relu2(acc):
        rl = jnp.maximum(acc, 0.0)
        return (rl * rl,)

    act = _matmul(hm, W["w_mlp_in_t"], M=S, N=D_FF, K=D_MODEL, mode="nt", bm=1024, bn=1024, bk=D_MODEL, name="mm_mlp_in",
                  out_dtypes=(MXU_DTYPE,), epilogue=relu2, j_outer=True)
    x2 = _matmul(act, W["w_mlp_out"], M=S, N=D_MODEL, K=D_FF, mode="nn", bm=512, bn=D_MODEL, bk=D_FF, name="mm_mlp_out",
                 epilogue=lambda acc, r: (r + acc,), extras=[(x1, (0, 0))])
    loss, dx2, dx2_m, dg_final = _loss_head(x2, P["g_final"], tgt, rows=S)

    G, Gs = {}, {}
    Gs["g_final"] = dg_final
    dw = dict(mode="tn", K=S, bk=S, out_dtypes=(MXU_DTYPE,))
    G["w_mlp_out"] = _matmul(act, dx2_m, M=D_FF, N=D_MODEL, bm=512, bn=D_MODEL, name="mm_dw_mlp_out",
                             parts=("rows", D_FF // N_DEV), **dw)
    du = _matmul(dx2_m, W["w_mlp_out"], M=S, N=D_FF, K=D_MODEL, mode="nt", bm=1024, bn=1024, bk=D_MODEL, name="mm_du",
                 out_dtypes=(MXU_DTYPE,), epilogue=lambda acc, a: (acc * (2.0 * jnp.sqrt(a.astype(F32))),),
                 extras=[(act, (0, 0))], j_outer=True)
    G["w_mlp_in"] = _matmul(hm, du, M=D_MODEL, N=D_FF, bm=D_MODEL, bn=512, name="mm_dw_mlp_in",
                            parts=("cols", D_FF // N_DEV), **dw)
    tie1 = send_grads({n: G.pop(n) for n in ("w_mlp_out", "w_mlp_in")})
    dhm = _matmul(du, W["w_mlp_in_t"], M=S, N=D_MODEL, K=D_FF, mode="nn", bm=512, bn=D_MODEL, bk=D_FF, name="mm_dhm",
                  deps=[tie1])
    dx1, dx1_m, Gs["g_mlp"] = _rmsnorm_bwd(x1, P["g_mlp"], dhm, dx2, rows=S, name="norm_mlp_bwd",
                                           dx_dtypes=(F32, MXU_DTYPE))
    G["w_out"] = _matmul(merged, dx1_m, M=D_MODEL, N=D_MODEL, bm=512, bn=D_MODEL, name="mm_dw_out",
                         parts=("rows", D_MODEL // N_DEV), **dw)
    dmerged = _matmul(dx1_m, W["w_out"], M=S, N=D_MODEL, K=D_MODEL, mode="nt", bm=512, bn=D_MODEL, bk=D_MODEL, name="mm_dmerged")
    (dg0, dg1, dg2, dy_lru, dy_dil, dy_mem, db0, db1, db2) = _mix_bwd(
        dmerged, z_lru, o_dil_m, om_m, W["w_lru_out"], W["w_dil_out"], W["w_mem_out"], proj, b_gate, S=S)
    Gs["b_gate0"], Gs["b_gate1"], Gs["b_gate2"] = db0, db1, db2

    G["w_mem_out"] = _matmul(om_m, dy_mem, M=MEM_WIDTH, N=D_MODEL, bm=MEM_WIDTH, bn=D_MODEL, name="mm_dw_mem_out",
                             parts=("cols", D_MODEL // N_DEV), **dw)
    dom = _matmul(dy_mem, W["w_mem_out"], M=S, N=MEM_WIDTH, K=D_MODEL, mode="nt", bm=512, bn=MEM_WIDTH, bk=D_MODEL,
                  name="mm_dom")
    dqm, dk_mem, dv_mem = _mem_attn_bwd(proj, kv, om, lse_mem, dom, S=S)
    dkv = jnp.concatenate([dk_mem, dv_mem], axis=1)
    G["w_mem_kv"] = _matmul(mem_n, dkv, M=D_MODEL, N=2 * MEM_WIDTH, K=N_MEM, mode="tn", bm=D_MODEL, bn=2 * MEM_WIDTH,
                            bk=N_MEM, name="mm_dw_kv", out_dtypes=(MXU_DTYPE,), parts=("rows", D_MODEL // N_DEV))
    dmem_n = _matmul(dkv, W["w_mem_kv"], M=N_MEM, N=D_MODEL, K=2 * MEM_WIDTH, mode="nt", bm=N_MEM, bn=D_MODEL,
                     bk=2 * MEM_WIDTH, name="mm_dmem")
    (Gs["g_mem"],) = _rmsnorm_bwd(mem, P["g_mem"], dmem_n, None, rows=N_MEM, name="norm_mem_bwd", dx_dtypes=())

    G["w_dil_out"] = _matmul(o_dil_m, dy_dil, M=256, N=D_MODEL, bm=256, bn=D_MODEL, name="mm_dw_dil_out",
                             parts=("cols", D_MODEL // N_DEV), **dw)
    do_dil = _matmul(dy_dil, W["w_dil_out"], M=S, N=256, K=D_MODEL, mode="nt", bm=512, bn=256, bk=D_MODEL, name="mm_do_dil")
    G["w_lru_out"] = _matmul(z_lru, dy_lru, M=D_RNN, N=D_MODEL, bm=D_RNN, bn=D_MODEL, name="mm_dw_lru_out",
                             parts=("cols", D_MODEL // N_DEV), **dw)
    dz = _matmul(dy_lru, W["w_lru_out"], M=S, N=D_RNN, K=D_MODEL, mode="nt", bm=512, bn=D_RNN, bk=D_MODEL, name="mm_dz_lru")
    tie2 = send_grads({n: G.pop(n) for n in ("w_out", "w_mem_out", "w_mem_kv", "w_dil_out", "w_lru_out")})
    bias = bias + tie2[0, 0]
    delta = _dilated_delta(do_dil, o_dil, S=S)
    dqkv, dbias = None, []
    for g in range(len(DIL_GROUPS)):
        *dqkv, db_g = _dilated_bwd(proj, do_dil, lse_dil, delta, bias, g, S=S, into=dqkv)
        dbias.append(db_g)
    drel = _dil_bias_bwd(jnp.stack(dbias, axis=0).reshape(len(DIL_GROUPS), DIL_HEADS, SPAN, 2 * SPAN), buckets)
    Gs["rel_bias"] = drel

    dxl, dgl, dcw, dcb, dwa, dwx, dba, dbx, dlam = _lru_bwd(proj, hl, a_lru, mult_lru, dz, *lru_args, S=S)
    Gs["conv_w"], Gs["conv_b"] = dcw, dcb
    Gs["w_rg_a"], Gs["w_rg_x"] = _block_diag_extract(dwa), _block_diag_extract(dwx)
    Gs["b_rg_a"], Gs["b_rg_x"], Gs["lru_lambda"] = dba, dbx, dlam
    Gs["loss"] = loss

    dproj = jnp.concatenate([dxl, dgl] + dqkv + [dqm, dg0, dg1, dg2], axis=1)
    cols = D_MODEL // W_IN_PIECES
    tie = []
    for q in range(W_IN_PIECES):
        dw_q = _matmul(h, dproj, M=cols, N=D_IN, K=S, mode="tn", bm=cols, bn=D_IN // 2, bk=1024, name=f"mm_dw_in_{q}",
                       a_off=(0, q), out_dtypes=(MXU_DTYPE,), parts=("rows_t", D_IN // N_DEV), deps=tie)
        tie = [send_grads({f"w_in_{q}": dw_q})]
    dh = _matmul(dproj, W["w_in_t"], M=S, N=D_MODEL, K=D_IN, mode="nn", bm=256, bn=D_MODEL, bk=D_IN, name="mm_dh",
                 deps=tie)
    grad_x, Gs["g_mix"] = _rmsnorm_bwd(x, P["g_mix"], dh, dx1, rows=S, name="norm_mix_bwd")
    return grad_x, reduce_small(Gs)


BIG = ("w_in", "w_lru_out", "w_dil_out", "w_mem_kv", "w_mem_out", "w_out", "w_mlp_in", "w_mlp_out")
W_IN_PIECES = 2
COL_SHARDED = ("w_lru_out", "w_dil_out", "w_mem_out", "w_mlp_in")
GATHERED_TRANSPOSED = ("w_mlp_in",)
SMALL = ("g_mix", "b_gate", "conv_b", "w_rg_a", "b_rg_a", "w_rg_x", "b_rg_x", "lru_lambda", "rel_bias", "g_mem",
         "g_mlp", "g_final")
WEIGHTS = ("g_mix", "w_in", "b_gate", "conv_w", "conv_b", "w_rg_a", "b_rg_a", "w_rg_x", "b_rg_x", "lru_lambda",
           "w_lru_out", "rel_bias", "w_dil_out", "g_mem", "w_mem_kv", "w_mem_out", "w_out", "g_mlp", "w_mlp_in",
           "w_mlp_out", "g_final")


def _gathered_to_full(name, gathered):
    if name in COL_SHARDED:
        n, r, c = gathered.shape
        return gathered.transpose(1, 0, 2).reshape(r, n * c)
    n, r, c = gathered.shape
    return gathered.reshape(n * r, c)


SMALL_GRADS = (("g_mix", (1, 1024)), ("b_gate0", (1, 1024)), ("b_gate1", (1, 1024)), ("b_gate2", (1, 1024)),
               ("conv_b", (1, 768)), ("w_rg_a", (12, 64, 64)), ("b_rg_a", (1, 768)), ("w_rg_x", (12, 64, 64)),
               ("b_rg_x", (1, 768)), ("lru_lambda", (1, 768)), ("rel_bias", (32, 128)), ("g_mem", (1, 1024)),
               ("g_mlp", (1, 1024)), ("g_final", (1, 1024)), ("conv_w", (4, 768)), ("loss", (1, 1)))


def _pack(parts):
    flat = jnp.concatenate([p.reshape(-1) for p in parts])
    return jnp.pad(flat, (0, (-flat.shape[0]) % 1024)).reshape(-1, 128)


def _unpack(pack, shapes):
    flat = pack.reshape(-1)
    out, off = [], 0
    for shp in shapes:
        size = math.prod(shp)
        out.append(flat[off:off + size].reshape(shp))
        off += size
    return out


def kernel(x, mem, g_mix, w_in, b_gate, conv_w, conv_b, w_rg_a, b_rg_a, w_rg_x, b_rg_x, lru_lambda, w_lru_out, rel_bias, w_dil_out, g_mem, w_mem_kv, w_mem_out, w_out, g_mlp, w_mlp_in, w_mlp_out, g_final, loss_target, m_g_mix, m_w_in, m_b_gate, m_conv_w, m_conv_b, m_w_rg_a, m_b_rg_a, m_w_rg_x, m_b_rg_x, m_lru_lambda, m_w_lru_out, m_rel_bias, m_w_dil_out, m_g_mem, m_w_mem_kv, m_w_mem_out, m_w_out, m_g_mlp, m_w_mlp_in, m_w_mlp_out, m_g_final, v_g_mix, v_w_in, v_b_gate, v_conv_w, v_conv_b, v_w_rg_a, v_b_rg_a, v_w_rg_x, v_b_rg_x, v_lru_lambda, v_w_lru_out, v_rel_bias, v_w_dil_out, v_g_mem, v_w_mem_kv, v_w_mem_out, v_w_out, v_g_mlp, v_w_mlp_in, v_w_mlp_out, v_g_final):
    w = dict(g_mix=g_mix, w_in=w_in, b_gate=b_gate, conv_w=conv_w, conv_b=conv_b, w_rg_a=w_rg_a, b_rg_a=b_rg_a,
             w_rg_x=w_rg_x, b_rg_x=b_rg_x, lru_lambda=lru_lambda, w_lru_out=w_lru_out, rel_bias=rel_bias,
             w_dil_out=w_dil_out, g_mem=g_mem, w_mem_kv=w_mem_kv, w_mem_out=w_mem_out, w_out=w_out, g_mlp=g_mlp,
             w_mlp_in=w_mlp_in, w_mlp_out=w_mlp_out, g_final=g_final)
    m = dict(g_mix=m_g_mix, w_in=m_w_in, b_gate=m_b_gate, conv_w=m_conv_w, conv_b=m_conv_b, w_rg_a=m_w_rg_a,
             b_rg_a=m_b_rg_a, w_rg_x=m_w_rg_x, b_rg_x=m_b_rg_x, lru_lambda=m_lru_lambda, w_lru_out=m_w_lru_out,
             rel_bias=m_rel_bias, w_dil_out=m_w_dil_out, g_mem=m_g_mem, w_mem_kv=m_w_mem_kv, w_mem_out=m_w_mem_out,
             w_out=m_w_out, g_mlp=m_g_mlp, w_mlp_in=m_w_mlp_in, w_mlp_out=m_w_mlp_out, g_final=m_g_final)
    v = dict(g_mix=v_g_mix, w_in=v_w_in, b_gate=v_b_gate, conv_w=v_conv_w, conv_b=v_conv_b, w_rg_a=v_w_rg_a,
             b_rg_a=v_b_rg_a, w_rg_x=v_w_rg_x, b_rg_x=v_b_rg_x, lru_lambda=v_lru_lambda, w_lru_out=v_w_lru_out,
             rel_bias=v_rel_bias, w_dil_out=v_w_dil_out, g_mem=v_g_mem, w_mem_kv=v_w_mem_kv, w_mem_out=v_w_mem_out,
             w_out=v_w_out, g_mlp=v_g_mlp, w_mlp_in=v_w_mlp_in, w_mlp_out=v_w_mlp_out, g_final=v_g_final)

    my_idx = _dev_index(_my_pos())

    g_in, g_cw = _all_gather([_mx(w["w_in"].T), w["conv_w"]])
    W = {"w_in_t": g_in.reshape(D_IN, D_MODEL), "conv_w": g_cw.transpose(1, 0, 2).reshape(CONV_WIDTH, D_RNN)}
    late, order_after = {}, [g_in]
    for group, names in (("branch", ("w_mem_kv", "w_lru_out", "w_dil_out", "w_mem_out", "w_out")),
                         ("mlp", ("w_mlp_in", "w_mlp_out"))):
        shards = [_mx(w[n].T if n in GATHERED_TRANSPOSED else w[n]) for n in names]
        started = _push_start(shards, [(N_DEV,) + s.shape for s in shards], _gather_refs, f"gather_{group}_start",
                              after=order_after)
        late[group] = (names, shards, started)
        order_after = [started["token"]]
    P = {n: w[n] for n in SMALL}

    def late_weights(group, after):
        names, shards, started = late[group]
        out = {}
        for n, land, own in zip(names, _push_wait(started, after), shards):
            full = lax.dynamic_update_index_in_dim(land, own, my_idx, 0)
            if n in GATHERED_TRANSPOSED:
                out[n + "_t"] = full.reshape(-1, full.shape[2])
            else:
                out[n] = _gathered_to_full(n, full)
        return out

    sent, small = [], {}

    def send_grads(gs):
        names = list(gs)
        parts = [gs[n] for n in names]
        own = [lax.dynamic_index_in_dim(p, my_idx, 0, keepdims=False) for p in parts]
        started = _push_start(parts, [(N_DEV - 1,) + p.shape[1:] for p in parts], _scatter_refs,
                              f"scatter{len(sent)}_start")
        sent.append((names, own, started))
        return started["token"]

    def reduce_small(gs):
        small["pack"] = _pack([gs[n] for n, _ in SMALL_GRADS])
        small["started"] = _push_start([small["pack"]], [(N_DEV,) + small["pack"].shape], _gather_refs, "small_start")
        return small["started"]["token"]

    grad_x, last_token = _local_step(x[0], mem[0], loss_target[0], W, P, late_weights, send_grads, reduce_small,
                                     late["mlp"][2]["token"][0, 0])

    grads, deltas, new_m, new_v = {}, {}, {}, {}
    after = last_token
    for names, own, started in sent[:-W_IN_PIECES]:
        for n, o, land in zip(names, own, _push_wait(started, after)):
            grads[n], deltas[n], new_m[n], new_v[n] = _adamw_landed(w[n], o, land, m[n], v[n], name=f"adamw_{n}")
            after = deltas[n]
    (small_land,) = _push_wait(small["started"], after)
    total = _sum_slots(lax.dynamic_update_index_in_dim(small_land, small["pack"], my_idx, 0))
    summed = dict(zip([n for n, _ in SMALL_GRADS], _unpack(total, [shp for _, shp in SMALL_GRADS])))
    summed["b_gate"] = jnp.concatenate([summed.pop(f"b_gate{b}") for b in range(3)], axis=1)
    summed["rel_bias"] = summed["rel_bias"][:, :3 * DIL_HEADS]
    for n in SMALL:
        grads[n] = summed[n].reshape(w[n].shape)
    small_updates = _adamw_many([w[n] for n in SMALL], [grads[n] for n in SMALL], [m[n] for n in SMALL],
                                [v[n] for n in SMALL])
    for n, (d_, nm_, nv_) in zip(SMALL, small_updates):
        deltas[n], new_m[n], new_v[n] = d_, nm_, nv_
    conv_w_sum, loss_sum = summed["conv_w"], summed["loss"]
    after = total
    prev = None
    for q, (names, own, started) in enumerate(sent[-W_IN_PIECES:]):
        (land,) = _push_wait(started, after)
        prev = _adamw_landed(w["w_in"].T, own[0], land, m["w_in"].T, v["w_in"].T, name=f"adamw_{names[0]}",
                             col_blk=q, prev=prev)
    grads["w_in"], deltas["w_in"], new_m["w_in"], new_v["w_in"] = [t.T for t in prev]
    cw_cols = D_RNN // N_DEV
    grads["conv_w"] = lax.dynamic_slice(conv_w_sum, (0, my_idx * cw_cols), (CONV_WIDTH, cw_cols))
    deltas["conv_w"], new_m["conv_w"], new_v["conv_w"] = _adamw_plain(
        w["conv_w"], grads["conv_w"], m["conv_w"], v["conv_w"], name="adamw_conv_w")

    return (loss_sum.reshape(()), grad_x[None], *[grads[n] for n in WEIGHTS], *[deltas[n] for n in WEIGHTS],
            *[new_m[n] for n in WEIGHTS], *[new_v[n] for n in WEIGHTS])
```

```python
import functools
import math

import jax
import jax.numpy as jnp
from jax import lax
from jax.experimental import pallas as pl
from jax.experimental.pallas import tpu as pltpu

F32 = jnp.float32
MXU_DTYPE = jnp.bfloat16
VMEM_LIMIT_BYTES = 56 * 1024 * 1024
N_DEV = 8

D_MODEL = 1024
N_MEM = 256
MEM_HEADS = 4
MEM_HEAD_DIM = 128
MEM_WIDTH = 512
D_RNN = 768
LRU_BLOCK = 64
N_LRU_BLOCKS = 12
LRU_GROUP = 256
N_LRU_GROUPS = 3
CONV_WIDTH = 4
LRU_C = 8.0
DIL_GROUPS = ((128, 1), (512, 4), (2048, 16))
SPAN = 128
DIL_HEADS = 4
DIL_HEAD_DIM = 64
NUM_BUCKETS = 32
MAX_DISTANCE = 2048
D_FF = 4096
D_IN = 7424
EPS = 1e-6
NEG = -1e30
C_XL, C_GATE, C_QKV, C_QM, C_GATES = 0, 768, 1536, 3840, 4352

ADAM_LR = 0.001
ADAM_B1 = 0.9
ADAM_B2 = 0.999
ADAM_EPS = 1e-08
ADAM_WD = 0.01
ADAM_STEP = 10

MESH = pl.DeviceIdType.MESH
GELU_K = math.sqrt(2.0 / math.pi)


def _cparams(sem=None):
    kw = dict(vmem_limit_bytes=VMEM_LIMIT_BYTES)
    if sem is not None:
        kw["dimension_semantics"] = sem
    return pltpu.CompilerParams(**kw)


def _mx(v):
    return v.astype(MXU_DTYPE)


def _dot(a, b, mode="nn"):
    dims = {"nn": (((1,), (0,)), ((), ())), "nt": (((1,), (1,)), ((), ())), "tn": (((0,), (0,)), ((), ()))}[mode]
    return lax.dot_general(_mx(a), _mx(b), dims, preferred_element_type=F32)


def _colsum(v):
    return jnp.sum(v, axis=0, keepdims=True)


def _matmul(a, b, *, M, N, K, mode, bm, bn, bk, name, out_dtypes=(F32,), epilogue=None, extras=(),
            a_off=(0, 0), b_off=(0, 0), j_outer=False, deps=(), parts=None):
    assert M % bm == 0 and N % bn == 0 and K % bk == 0, (name, M, N, K, bm, bn, bk)
    nm, nn, nk = M // bm, N // bn, K // bk

    def ij(f):
        if j_outer:
            return lambda j, i, k: f(i, j, k)
        return f

    if mode == "tn":
        a_spec = pl.BlockSpec((bk, bm), ij(lambda i, j, k: (k + a_off[0], i + a_off[1])))
    else:
        a_spec = pl.BlockSpec((bm, bk), ij(lambda i, j, k: (i + a_off[0], k + a_off[1])))
    if mode == "nt":
        b_spec = pl.BlockSpec((bn, bk), ij(lambda i, j, k: (j + b_off[0], k + b_off[1])))
    else:
        b_spec = pl.BlockSpec((bk, bn), ij(lambda i, j, k: (k + b_off[0], j + b_off[1])))
    ex_specs = [pl.BlockSpec((bm, bn), ij(functools.partial(lambda i, j, k, o: (i + o[0], j + o[1]), o=off)))
                for _, off in extras]
    if parts is None:
        out_dims = (M, N)
        out_spec = pl.BlockSpec((bm, bn), ij(lambda i, j, k: (i, j)))
    elif parts[0] == "rows":
        r = parts[1]
        assert bm % r == 0
        out_dims = (M // r, r, N)
        out_spec = pl.BlockSpec((bm // r, r, bn), ij(lambda i, j, k: (i, 0, j)))
    elif parts[0] == "rows_t":
        r = parts[1]
        assert bn % r == 0
        out_dims = (N // r, r, M)
        out_spec = pl.BlockSpec((bn // r, r, bm), ij(lambda i, j, k: (j, 0, i)))
    else:
        c = parts[1]
        assert bn % c == 0
        out_dims = (N // c, M, c)
        out_spec = pl.BlockSpec((bn // c, bm, c), ij(lambda i, j, k: (j, i, 0)))
    n_ex, n_out, n_dep = len(extras), len(out_dtypes), len(deps)

    def body(*refs):
        a_ref, b_ref = refs[0], refs[1]
        ex = refs[2:2 + n_ex]
        outs = refs[2 + n_ex + n_dep:2 + n_ex + n_dep + n_out]
        part = _dot(a_ref[...], b_ref[...], mode)

        def finish(acc):
            vals = epilogue(acc, *[e[...] for e in ex]) if epilogue is not None else (acc,)
            for o, v in zip(outs, vals):
                if parts is not None and parts[0] == "rows_t":
                    v = v.T
                v = v.astype(o.dtype)
                if parts is None:
                    o[...] = v
                elif parts[0] in ("rows", "rows_t"):
                    for ch in range(v.shape[0] // parts[1]):
                        o[ch] = v[ch * parts[1]:(ch + 1) * parts[1], :]
                else:
                    for ch in range(bn // parts[1]):
                        o[ch] = v[:, ch * parts[1]:(ch + 1) * parts[1]]

        if nk == 1:
            finish(part)
        else:
            acc_ref = refs[-1]
            k = pl.program_id(2)

            @pl.when(k == 0)
            def _():
                acc_ref[...] = part

            @pl.when(k > 0)
            def _():
                acc_ref[...] += part

            @pl.when(k == nk - 1)
            def _():
                finish(acc_ref[...])

    grid = (nn, nm, nk) if j_outer else (nm, nn, nk)
    res = pl.pallas_call(
        body, name=name, grid=grid,
        in_specs=[a_spec, b_spec] + ex_specs + [pl.BlockSpec(memory_space=pl.ANY)] * n_dep,
        out_specs=[out_spec] * n_out,
        out_shape=[jax.ShapeDtypeStruct(out_dims, dt) for dt in out_dtypes],
        scratch_shapes=[pltpu.VMEM((bm, bn), F32)] if nk > 1 else [],
        compiler_params=_cparams(("parallel", "parallel", "arbitrary")),
    )(a, b, *[e for e, _ in extras], *deps)
    return res[0] if n_out == 1 else res


def _rmsnorm_fwd(x, g, *, rows, name, bt=512):
    bt = min(bt, rows)

    def body(x_ref, g_ref, o_ref):
        xv = x_ref[...]
        r = lax.rsqrt(jnp.mean(xv * xv, axis=-1, keepdims=True) + EPS)
        o_ref[...] = (xv * r * g_ref[...]).astype(o_ref.dtype)

    return pl.pallas_call(
        body, name=name, grid=(rows // bt,),
        in_specs=[pl.BlockSpec((bt, D_MODEL), lambda i: (i, 0)), pl.BlockSpec((1, D_MODEL), lambda i: (0, 0))],
        out_specs=pl.BlockSpec((bt, D_MODEL), lambda i: (i, 0)),
        out_shape=jax.ShapeDtypeStruct((rows, D_MODEL), MXU_DTYPE),
        compiler_params=_cparams(("parallel",)),
    )(x, g.reshape(1, D_MODEL))


def _rms_bwd_tile(xv, gv, dyv):
    r = lax.rsqrt(jnp.mean(xv * xv, axis=-1, keepdims=True) + EPS)
    w = dyv * gv
    dx = r * w - xv * (r * r * r) * jnp.mean(w * xv, axis=-1, keepdims=True)
    dg = _colsum(dyv * xv * r)
    return dx, dg


def _rmsnorm_bwd(x, g, dy, res, *, rows, name, bt=512, dx_dtypes=(F32,)):
    bt = min(bt, rows)
    has_res = res is not None

    def body(*refs):
        x_ref, g_ref, dy_ref = refs[:3]
        res_ref = refs[3] if has_res else None
        outs = refs[3 + int(has_res):]
        dx, dg = _rms_bwd_tile(x_ref[...], g_ref[...], dy_ref[...])
        if has_res:
            dx = dx + res_ref[...]
        dg_ref = outs[-1]

        @pl.when(pl.program_id(0) == 0)
        def _():
            dg_ref[...] = jnp.zeros_like(dg_ref)

        dg_ref[...] += dg
        for o in outs[:-1]:
            o[...] = dx.astype(o.dtype)

    row_spec = pl.BlockSpec((bt, D_MODEL), lambda i: (i, 0))
    vec_spec = pl.BlockSpec((1, D_MODEL), lambda i: (0, 0))
    ins = [x, g.reshape(1, D_MODEL), dy] + ([res] if has_res else [])
    return pl.pallas_call(
        body, name=name, grid=(rows // bt,),
        in_specs=[row_spec, vec_spec, row_spec] + ([row_spec] if has_res else []),
        out_specs=[row_spec] * len(dx_dtypes) + [vec_spec],
        out_shape=[jax.ShapeDtypeStruct((rows, D_MODEL), dt) for dt in dx_dtypes] + [jax.ShapeDtypeStruct((1, D_MODEL), F32)],
        compiler_params=_cparams(("arbitrary",)),
    )(*ins)


def _loss_head(x2, g, tgt, *, rows, bt=512):
    def body(x_ref, g_ref, t_ref, loss_ref, dx_ref, dxm_ref, dg_ref):
        xv, gv = x_ref[...], g_ref[...]
        r = lax.rsqrt(jnp.mean(xv * xv, axis=-1, keepdims=True) + EPS)
        diff = xv * r * gv - t_ref[...]
        part = jnp.sum(jnp.mean(diff * diff, axis=-1, keepdims=True), axis=0, keepdims=True) * 0.5
        dx, dg = _rms_bwd_tile(xv, gv, diff * (1.0 / D_MODEL))

        @pl.when(pl.program_id(0) == 0)
        def _():
            loss_ref[...] = jnp.zeros_like(loss_ref)
            dg_ref[...] = jnp.zeros_like(dg_ref)

        loss_ref[...] += part
        dg_ref[...] += dg
        dx_ref[...] = dx
        dxm_ref[...] = _mx(dx)

    row_spec = pl.BlockSpec((bt, D_MODEL), lambda i: (i, 0))
    vec_spec = pl.BlockSpec((1, D_MODEL), lambda i: (0, 0))
    return pl.pallas_call(
        body, name="loss_head", grid=(rows // bt,),
        in_specs=[row_spec, vec_spec, row_spec],
        out_specs=[pl.BlockSpec((1, 1), lambda i: (0, 0)), row_spec, row_spec, vec_spec],
        out_shape=[jax.ShapeDtypeStruct((1, 1), F32), jax.ShapeDtypeStruct((rows, D_MODEL), F32),
                   jax.ShapeDtypeStruct((rows, D_MODEL), MXU_DTYPE), jax.ShapeDtypeStruct((1, D_MODEL), F32)],
        compiler_params=_cparams(("arbitrary",)),
    )(x2, g.reshape(1, D_MODEL), tgt)


LRU_T = 512
SCAN_GROUPS = 4


def _gelu(x):
    t = jnp.tanh(GELU_K * (x + 0.044715 * x * x * x))
    return 0.5 * x * (1.0 + t), t


def _gelu_grad(x, t):
    return 0.5 * (1.0 + t) + 0.5 * x * (1.0 - t * t) * GELU_K * (1.0 + 3.0 * 0.044715 * x * x)


def _softplus_neg(lam):
    z = -lam
    u = jnp.exp(-jnp.abs(z))
    w = 1.0 + u
    l1p = jnp.where(w == 1.0, u, jnp.log(w) * u / jnp.where(w == 1.0, 1.0, w - 1.0))
    return jnp.maximum(z, 0.0) + l1p


def _shift_down(cur, prev8, k, row8):
    y = pltpu.roll(cur, k, 0)
    head = jnp.where(row8 < k, pltpu.roll(prev8, k, 0), y[0:8])
    return jnp.concatenate([head, y[8:]], axis=0)


def _shift_up(cur, next8, k, row8):
    n = cur.shape[0]
    y = pltpu.roll(cur, n - k, 0)
    tail = jnp.where(row8 >= 8 - k, pltpu.roll(next8, 8 - k, 0), y[n - 8:n])
    return jnp.concatenate([y[0:n - 8], tail], axis=0)


def _lru_gates(xl, p8, cw, cb, wa, wx, ba, bx, lam, row8, a_mult=None):
    sh = [xl] + [_shift_down(xl, p8, k, row8) for k in (1, 2, 3)]
    xc = cb + cw[3:4] * sh[0] + cw[2:3] * sh[1] + cw[1:2] * sh[2] + cw[0:1] * sh[3]
    r = jax.nn.sigmoid(_dot(xc, wa) + ba)
    i = jax.nn.sigmoid(_dot(xc, wx) + bx)
    sp = _softplus_neg(lam)
    if a_mult is None:
        la = -LRU_C * r * sp
        a = jnp.exp(la)
        mult = jnp.sqrt(jnp.tanh(-la) * (a * a + 1.0))
    else:
        a, mult = a_mult
    return dict(sh=sh, xc=xc, r=r, i=i, sp=sp, a=a, mult=mult)


def _lru_specs(n_t, reverse):
    T = LRU_T
    tt = (lambda t: n_t - 1 - t) if reverse else (lambda t: t)
    blk = lambda col0: pl.BlockSpec((T, LRU_GROUP), lambda g, t: (tt(t), col0 + g))
    prev8 = lambda col0: pl.BlockSpec((8, LRU_GROUP), lambda g, t: (jnp.maximum(tt(t) * (T // 8) - 1, 0), col0 + g))
    vec = lambda rows: pl.BlockSpec((rows, LRU_GROUP), lambda g, t: (0, g))
    wbd = pl.BlockSpec((1, LRU_GROUP, LRU_GROUP), lambda g, t: (g, 0, 0))
    return blk, prev8, vec, wbd


def _lru_fwd(proj, conv_w, conv_b, wa_bd, wx_bd, b_a, b_x, lam, *, S):
    T = LRU_T
    n_t = S // T
    blk, _, vec, wbd = _lru_specs(n_t, False)

    def body(xl_ref, gate_ref, cw_ref, cb_ref, wa_ref, wx_ref, ba_ref, bx_ref, lam_ref,
             hl_ref, z_ref, a_s, m_ref, prev8, hcar, b_s):
        @pl.when(pl.program_id(1) == 0)
        def _():
            prev8[...] = jnp.zeros_like(prev8)
            hcar[...] = jnp.zeros_like(hcar)

        row8 = lax.broadcasted_iota(jnp.int32, (8, LRU_GROUP), 0)
        xl = xl_ref[...]
        q = _lru_gates(xl, prev8[...], cw_ref[...], cb_ref[...], wa_ref[0], wx_ref[0], ba_ref[...], bx_ref[...],
                       lam_ref[...], row8)
        prev8[...] = xl[T - 8:T]
        a_s[...] = q["a"]
        m_ref[...] = q["mult"]
        b_s[...] = q["mult"] * q["i"] * q["xc"]

        def step(c, carry):
            local = []
            for u in range(SCAN_GROUPS):
                off = pl.multiple_of((c * SCAN_GROUPS + u) * 8, 8)
                A = a_s[pl.ds(off, 8), :]
                B = b_s[pl.ds(off, 8), :]
                for k in (1, 2, 4):
                    a_sh = jnp.where(row8 >= k, pltpu.roll(A, k, 0), 1.0)
                    b_sh = jnp.where(row8 >= k, pltpu.roll(B, k, 0), 0.0)
                    B = A * b_sh + B
                    A = A * a_sh
                local.append((off, A, B))
            for off, A, B in local:
                h = A * carry + B
                hl_ref[pl.ds(off, 8), :] = h
                carry = h[7:8, :]
            return carry

        hcar[...] = lax.fori_loop(0, T // (8 * SCAN_GROUPS), step, hcar[...])
        ge, _ = _gelu(gate_ref[...])
        z_ref[...] = (ge * hl_ref[...]).astype(z_ref.dtype)

    return pl.pallas_call(
        body, name="lru_fwd", grid=(N_LRU_GROUPS, n_t),
        in_specs=[blk(C_XL // LRU_GROUP), blk(C_GATE // LRU_GROUP), vec(4), vec(1), wbd, wbd, vec(1), vec(1), vec(1)],
        out_specs=[blk(0)] * 4,
        out_shape=[jax.ShapeDtypeStruct((S, D_RNN), F32), jax.ShapeDtypeStruct((S, D_RNN), MXU_DTYPE),
                   jax.ShapeDtypeStruct((S, D_RNN), F32), jax.ShapeDtypeStruct((S, D_RNN), F32)],
        scratch_shapes=[pltpu.VMEM((8, LRU_GROUP), F32), pltpu.VMEM((1, LRU_GROUP), F32), pltpu.VMEM((T, LRU_GROUP), F32)],
        compiler_params=_cparams(("parallel", "arbitrary")),
    )(proj, proj, conv_w, conv_b, wa_bd, wx_bd, b_a, b_x, lam)


def _lru_bwd(proj, hl, a_fwd, mult_fwd, dz, conv_w, conv_b, wa_bd, wx_bd, b_a, b_x, lam, *, S):
    T = LRU_T
    n_t = S // T
    blk, prev8s, vec, wbd = _lru_specs(n_t, True)

    def body(xl_ref, xlp_ref, gate_ref, hl_ref, hlp_ref, a_ref, m_ref, dz_ref, cw_ref, cb_ref, wa_ref, wx_ref, ba_ref,
             bx_ref, lam_ref, dxl_ref, dgate_ref, dcw_ref, dcb_ref, dwa_ref, dwx_ref, dba_ref, dbx_ref, dlam_ref,
             next8, gcar, c_s, b_s, l_s):
        t = pl.program_id(1)
        first_chunk = t == n_t - 1

        @pl.when(t == 0)
        def _():
            next8[...] = jnp.zeros_like(next8)
            gcar[...] = jnp.zeros_like(gcar)
            for ref in (dcw_ref, dcb_ref, dwa_ref, dwx_ref, dba_ref, dbx_ref, dlam_ref):
                ref[...] = jnp.zeros_like(ref)

        row8 = lax.broadcasted_iota(jnp.int32, (8, LRU_GROUP), 0)
        rowT = lax.broadcasted_iota(jnp.int32, (T, LRU_GROUP), 0)
        keep = jnp.where(first_chunk, 0.0, 1.0)
        xl = xl_ref[...]
        wa, wx, lam_v = wa_ref[0], wx_ref[0], lam_ref[...]
        q = _lru_gates(xl, xlp_ref[...] * keep, cw_ref[...], cb_ref[...], wa, wx, ba_ref[...], bx_ref[...], lam_v, row8,
                       a_mult=(a_ref[...], m_ref[...]))
        a, mult, r, i, xc, sp = q["a"], q["mult"], q["r"], q["i"], q["xc"], q["sp"]
        hl_v = hl_ref[...]
        dz_v = dz_ref[...]
        gate = gate_ref[...]
        ge, th = _gelu(gate)
        dgate_ref[...] = (dz_v * hl_v * _gelu_grad(gate, th)).astype(dgate_ref.dtype)

        c_s[...] = jnp.where(rowT == T - 1, 0.0, pltpu.roll(a, T - 1, 0))
        b_s[...] = dz_v * ge + jnp.where(rowT == T - 1, gcar[...], 0.0)

        def step(n, carry):
            local = []
            for u in range(SCAN_GROUPS):
                off = pl.multiple_of((T // 8 - 1 - (n * SCAN_GROUPS + u)) * 8, 8)
                C = c_s[pl.ds(off, 8), :]
                B = b_s[pl.ds(off, 8), :]
                for k in (1, 2, 4):
                    c_sh = jnp.where(row8 < 8 - k, pltpu.roll(C, 8 - k, 0), 1.0)
                    b_sh = jnp.where(row8 < 8 - k, pltpu.roll(B, 8 - k, 0), 0.0)
                    B = B + C * b_sh
                    C = C * c_sh
                local.append((off, C, B))
            for off, C, B in local:
                lam_t = B + C * carry
                l_s[pl.ds(off, 8), :] = lam_t
                carry = lam_t[0:1, :]
            return carry

        lax.fori_loop(0, T // (8 * SCAN_GROUPS), step, jnp.zeros((1, LRU_GROUP), F32))
        lmb = l_s[...]
        gcar[...] = a[0:1, :] * lmb[0:1, :]

        h_prev = _shift_down(hl_v, hlp_ref[...] * keep, 1, row8)
        da = lmb * h_prev
        dmult = lmb * i * xc
        di = lmb * mult * xc
        dxc = lmb * mult * i
        dla = da * a - dmult * (a * a) / mult
        dr = dla * (-LRU_C * sp)
        dlam_ref[...] += _colsum(dla * (-LRU_C * r)) * (-jax.nn.sigmoid(-lam_v))
        dpa = dr * r * (1.0 - r)
        dpx = di * i * (1.0 - i)
        dxc = dxc + _dot(dpa, wa, "nt") + _dot(dpx, wx, "nt")
        dwa_ref[0] += _dot(xc, dpa, "tn")
        dwx_ref[0] += _dot(xc, dpx, "tn")
        dba_ref[...] += _colsum(dpa)
        dbx_ref[...] += _colsum(dpx)
        dcb_ref[...] += _colsum(dxc)
        cw = cw_ref[...]
        n8 = next8[...]
        dxl = cw[3:4] * dxc
        for k in (1, 2, 3):
            dxl = dxl + cw[3 - k:4 - k] * _shift_up(dxc, n8, k, row8)
        for k in range(4):
            dcw_ref[3 - k:4 - k, :] += _colsum(dxc * q["sh"][k])
        next8[...] = dxc[0:8]
        dxl_ref[...] = dxl.astype(dxl_ref.dtype)

    res = pl.pallas_call(
        body, name="lru_bwd", grid=(N_LRU_GROUPS, n_t),
        in_specs=[blk(C_XL // LRU_GROUP), prev8s(C_XL // LRU_GROUP), blk(C_GATE // LRU_GROUP), blk(0), prev8s(0), blk(0),
                  blk(0), blk(0), vec(4), vec(1), wbd, wbd, vec(1), vec(1), vec(1)],
        out_specs=[blk(0), blk(0), vec(4), vec(1), wbd, wbd, vec(1), vec(1), vec(1)],
        out_shape=[jax.ShapeDtypeStruct((S, D_RNN), MXU_DTYPE), jax.ShapeDtypeStruct((S, D_RNN), MXU_DTYPE),
                   jax.ShapeDtypeStruct((4, D_RNN), F32), jax.ShapeDtypeStruct((1, D_RNN), F32),
                   jax.ShapeDtypeStruct((N_LRU_GROUPS, LRU_GROUP, LRU_GROUP), F32),
                   jax.ShapeDtypeStruct((N_LRU_GROUPS, LRU_GROUP, LRU_GROUP), F32),
                   jax.ShapeDtypeStruct((1, D_RNN), F32), jax.ShapeDtypeStruct((1, D_RNN), F32),
                   jax.ShapeDtypeStruct((1, D_RNN), F32)],
        scratch_shapes=[pltpu.VMEM((8, LRU_GROUP), F32), pltpu.VMEM((1, LRU_GROUP), F32),
                        pltpu.VMEM((T, LRU_GROUP), F32), pltpu.VMEM((T, LRU_GROUP), F32), pltpu.VMEM((T, LRU_GROUP), F32)],
        compiler_params=_cparams(("parallel", "arbitrary")),
    )(proj, proj, proj, hl, hl, a_fwd, mult_fwd, dz, conv_w, conv_b, wa_bd, wx_bd, b_a, b_x, lam)
    return res


def _block_diag(w):
    w4 = w.reshape(N_LRU_GROUPS, 4, LRU_BLOCK, 1, LRU_BLOCK)
    eye = jnp.eye(4, dtype=w.dtype).reshape(1, 4, 1, 4, 1)
    return (w4 * eye).reshape(N_LRU_GROUPS, LRU_GROUP, LRU_GROUP)


def _block_diag_extract(wbd):
    w5 = wbd.reshape(N_LRU_GROUPS, 4, LRU_BLOCK, 4, LRU_BLOCK)
    return jnp.stack([w5[:, a, :, a, :] for a in range(4)], axis=1).reshape(N_LRU_BLOCKS, LRU_BLOCK, LRU_BLOCK)


def _t5_bucket(dist):
    max_exact = NUM_BUCKETS // 2
    df = jnp.maximum(dist, 1).astype(jnp.float32)
    large = max_exact + (jnp.log(df / max_exact) / math.log(MAX_DISTANCE / max_exact)
                         * (NUM_BUCKETS - max_exact)).astype(jnp.int32)
    large = jnp.minimum(large, NUM_BUCKETS - 1)
    return jnp.where(dist < max_exact, dist, large)


def _band_offsets():
    qi = jnp.arange(SPAN)[:, None]
    kj = jnp.arange(2 * SPAN)[None, :]
    return qi + SPAN - kj


def _dil_buckets():
    off = _band_offsets()
    return jnp.stack([_t5_bucket(jnp.maximum(off, 0) * dil) for _, dil in DIL_GROUPS]).astype(jnp.int32)


def _dil_bias(rel_bias, buckets):
    def body(tbl_ref, bk_ref, o_ref):
        g = pl.program_id(0)
        qi = lax.broadcasted_iota(jnp.int32, (SPAN, 2 * SPAN), 0)
        kj = lax.broadcasted_iota(jnp.int32, (SPAN, 2 * SPAN), 1)
        off = qi + SPAN - kj
        valid = (off >= 0) & (off <= SPAN)
        bk = bk_ref[0]
        for h in range(DIL_HEADS):
            acc = jnp.zeros((SPAN, 2 * SPAN), F32)
            for b in range(NUM_BUCKETS):
                acc = jnp.where(bk == b, tbl_ref[b, g * DIL_HEADS + h], acc)
            o_ref[0, h] = jnp.where(valid, acc, NEG)

    return pl.pallas_call(
        body, name="dil_bias", grid=(3,),
        in_specs=[pl.BlockSpec(memory_space=pltpu.SMEM), pl.BlockSpec((1, SPAN, 2 * SPAN), lambda g: (g, 0, 0))],
        out_specs=pl.BlockSpec((1, DIL_HEADS, SPAN, 2 * SPAN), lambda g: (g, 0, 0, 0)),
        out_shape=jax.ShapeDtypeStruct((3, DIL_HEADS, SPAN, 2 * SPAN), F32),
        compiler_params=_cparams(("parallel",)),
    )(rel_bias, buckets)


def _dil_bias_bwd(dbias, buckets):
    def body(db_ref, bk_ref, o_ref):
        lane = lax.broadcasted_iota(jnp.int32, (1, 128), 1)
        rows = [jnp.zeros((1, 128), F32) for _ in range(NUM_BUCKETS)]
        for g in range(3):
            bk = bk_ref[g]
            for h in range(DIL_HEADS):
                d = db_ref[g, h]
                for b in range(NUM_BUCKETS):
                    tot = jnp.sum(_colsum(jnp.where(bk == b, d, 0.0)), axis=1, keepdims=True)
                    rows[b] = jnp.where(lane == g * DIL_HEADS + h, tot, rows[b])
        for b in range(NUM_BUCKETS):
            o_ref[b:b + 1, :] = rows[b]

    return pl.pallas_call(
        body, name="dil_bias_bwd",
        out_shape=jax.ShapeDtypeStruct((NUM_BUCKETS, 128), F32),
        compiler_params=_cparams(),
    )(dbias, buckets)


DIL_SUBBLOCKS = (4, 2, 1)


def _dil_layout(g, S):
    dil, m = DIL_GROUPS[g][1], DIL_SUBBLOCKS[g]
    sub = SPAN * dil
    col = [(C_QKV + t * 768 + g * 256) // 128 for t in range(3)]
    return dil, m, sub, S // (sub * m), col


def _residue_rows(b, r, dil):
    return pl.ds(b * SPAN * dil + r, SPAN, stride=dil) if dil > 1 else pl.ds(b * SPAN, SPAN)


def _for_residues(dil, fn):
    if dil <= 4:
        for r in range(dil):
            fn(r)
    else:
        lax.fori_loop(0, dil, lambda r, c: (fn(r), c)[1], 0, unroll=2)


def _pair_scores(qm, k2, bias, first_cols):
    s = _dot(qm, k2, "nt") * (DIL_HEAD_DIM ** -0.5) + bias
    kj = lax.broadcasted_iota(jnp.int32, s.shape, 1)
    return jnp.where(kj < first_cols, NEG, s)


def _dilated_fwd(proj, bias, g, *, S):
    dil, m, sub, nc, (qc, kc, vc) = _dil_layout(g, S)
    R = sub * m
    cur = lambda cb: pl.BlockSpec((R, 128), lambda p, i: (i, cb + p))
    prv = lambda cb: pl.BlockSpec((sub, 128), lambda p, i: (jnp.maximum(i * m - 1, 0), cb + p))
    out = pl.BlockSpec((R, 128), lambda p, i: (i, p))

    def body(q_ref, kp_ref, kc_ref, vp_ref, vc_ref, b_ref, o_ref, lse_ref):
        lane = lax.broadcasted_iota(jnp.int32, (SPAN, 128), 1)
        sels = (lane < DIL_HEAD_DIM, lane >= DIL_HEAD_DIM)
        for b in range(m):
            first_cols = jnp.where(pl.program_id(1) == 0, SPAN, 0) if b == 0 else 0

            def one(r, b=b, first_cols=first_cols):
                rows = _residue_rows(b, r, dil)
                before = (kc_ref, vc_ref, _residue_rows(b - 1, r, dil)) if b else (kp_ref, vp_ref, _residue_rows(0, r, dil))
                q2 = q_ref[rows, :]
                k2 = _mx(jnp.concatenate([before[0][before[2], :], kc_ref[rows, :]], axis=0))
                v2 = _mx(jnp.concatenate([before[1][before[2], :], vc_ref[rows, :]], axis=0))
                qq = jnp.concatenate([jnp.where(sels[0], q2, 0.0), jnp.where(sels[1], q2, 0.0)], axis=0)
                s = _pair_scores(qq, k2, b_ref[0, 0], first_cols)
                mx = jnp.max(s, axis=-1, keepdims=True)
                p = jnp.exp(s - mx)
                den = jnp.sum(p, axis=-1, keepdims=True)
                o = _dot(p, v2) / den
                st = mx + jnp.log(den)
                o_ref[rows, :] = jnp.where(sels[0], o[0:SPAN], o[SPAN:2 * SPAN])
                lse_ref[rows, :] = jnp.where(lane == 0, st[0:SPAN], jnp.where(lane == 1, st[SPAN:2 * SPAN], 0.0))

            _for_residues(dil, one)

    return pl.pallas_call(
        body, name=f"dil_fwd{g}", grid=(2, nc),
        in_specs=[cur(qc), prv(kc), cur(kc), prv(vc), cur(vc),
                  pl.BlockSpec((1, 1, 2 * SPAN, 2 * SPAN), lambda p, i: (g, p, 0, 0))],
        out_specs=[out, out],
        out_shape=[jax.ShapeDtypeStruct((S, 256), F32), jax.ShapeDtypeStruct((S, 256), F32)],
        compiler_params=_cparams(("parallel", "parallel")),
    )(proj, proj, proj, proj, proj, bias.reshape(3, 2, 2 * SPAN, 2 * SPAN))


def _dilated_bwd(proj, do, lse, delta, bias, g, *, S, into=None):
    dil, m, sub, nc, (qc, kc, vc) = _dil_layout(g, S)
    R = sub * m
    cl = lambda i: jnp.minimum(i, nc - 1)
    cur = lambda cb: pl.BlockSpec((R, 128), lambda p, i: (cl(i), cb + p))
    prv = lambda cb: pl.BlockSpec((sub, 128), lambda p, i: (jnp.maximum(cl(i) * m - 1, 0), cb + p))
    q_out = pl.BlockSpec((R, 128), lambda p, i: (cl(i), 2 * g + p))
    kv_out = pl.BlockSpec((R, 128), lambda p, i: (jnp.maximum(i - 1, 0), 2 * g + p))
    scale = DIL_HEAD_DIM ** -0.5
    n_into = 0 if into is None else 3

    def body(q_ref, kp_ref, kc_ref, vp_ref, vc_ref, do_ref, lse_ref, dl_ref, b_ref, *rest):
        dq_ref, dk_ref, dv_ref, db_ref, dq_s, kc_s, vc_s, kp_s, vp_s, kcar, vcar = rest[n_into:]
        i = pl.program_id(1)

        @pl.when(i == 0)
        def _():
            kcar[...] = jnp.zeros_like(kcar)
            vcar[...] = jnp.zeros_like(vcar)
            db_ref[...] = jnp.zeros_like(db_ref)

        @pl.when(i < nc)
        def _():
            lane = lax.broadcasted_iota(jnp.int32, (SPAN, 128), 1)
            sels = (lane < DIL_HEAD_DIM, lane >= DIL_HEAD_DIM)
            for b in range(m):
                first_cols = jnp.where(i == 0, SPAN, 0) if b == 0 else 0

                def one(r, b=b, first_cols=first_cols):
                    rows = _residue_rows(b, r, dil)
                    rows_before = _residue_rows(b - 1 if b else 0, r, dil)
                    k_before, v_before = (kc_ref, vc_ref) if b else (kp_ref, vp_ref)
                    q2, do2 = q_ref[rows, :], do_ref[rows, :]
                    k2 = _mx(jnp.concatenate([k_before[rows_before, :], kc_ref[rows, :]], axis=0))
                    v2 = _mx(jnp.concatenate([v_before[rows_before, :], vc_ref[rows, :]], axis=0))
                    lse_t, dl_t = lse_ref[rows, :], dl_ref[rows, :]
                    qq = _mx(jnp.concatenate([jnp.where(sels[0], q2, 0.0), jnp.where(sels[1], q2, 0.0)], axis=0))
                    dd = _mx(jnp.concatenate([jnp.where(sels[0], do2, 0.0), jnp.where(sels[1], do2, 0.0)], axis=0))
                    lse2 = jnp.concatenate([lse_t[:, 0:1], lse_t[:, 1:2]], axis=0)
                    dl2 = jnp.concatenate([dl_t[:, 0:1], dl_t[:, 1:2]], axis=0)
                    p = jnp.exp(_pair_scores(qq, k2, b_ref[0, 0], first_cols) - lse2)
                    ds = p * (_dot(dd, v2, "nt") - dl2)
                    db_ref[0] += ds
                    dqq = _dot(ds, k2) * scale
                    dq2 = jnp.where(sels[0], dqq[0:SPAN], dqq[SPAN:2 * SPAN])
                    dk2 = _dot(ds, qq, "tn") * scale
                    dv2 = _dot(p, dd, "tn")
                    dq_s[rows, :] = dq2
                    kc_s[rows, :] = dk2[SPAN:2 * SPAN]
                    vc_s[rows, :] = dv2[SPAN:2 * SPAN]
                    if b:
                        kc_s[rows_before, :] += dk2[0:SPAN]
                        vc_s[rows_before, :] += dv2[0:SPAN]
                    else:
                        kp_s[rows_before, :] = dk2[0:SPAN]
                        vp_s[rows_before, :] = dv2[0:SPAN]

                _for_residues(dil, one)
            dq_ref[...] = dq_s[...].astype(dq_ref.dtype)
            last = pl.ds((m - 1) * sub, sub)
            kcar[last, :] += kp_s[...]
            vcar[last, :] += vp_s[...]
            dk_ref[...] = kcar[...].astype(dk_ref.dtype)
            dv_ref[...] = vcar[...].astype(dv_ref.dtype)
            kcar[...] = kc_s[...]
            vcar[...] = vc_s[...]

        @pl.when(i == nc)
        def _():
            dk_ref[...] = kcar[...].astype(dk_ref.dtype)
            dv_ref[...] = vcar[...].astype(dv_ref.dtype)

    stat = pl.BlockSpec((R, 128), lambda p, i: (cl(i), p))
    big = jax.ShapeDtypeStruct((S, len(DIL_GROUPS) * 256), MXU_DTYPE)
    return pl.pallas_call(
        body, name=f"dil_bwd{g}", grid=(2, nc + 1),
        in_specs=[cur(qc), prv(kc), cur(kc), prv(vc), cur(vc), stat, stat, stat,
                  pl.BlockSpec((1, 1, 2 * SPAN, 2 * SPAN), lambda p, i: (g, p, 0, 0))]
        + [pl.BlockSpec(memory_space=pl.ANY)] * n_into,
        out_specs=[q_out, kv_out, kv_out, pl.BlockSpec((1, 2 * SPAN, 2 * SPAN), lambda p, i: (p, 0, 0))],
        out_shape=[big, big, big, jax.ShapeDtypeStruct((2, 2 * SPAN, 2 * SPAN), F32)],
        input_output_aliases={9 + j: j for j in range(n_into)},
        scratch_shapes=[pltpu.VMEM((R, 128), F32)] * 3 + [pltpu.VMEM((sub, 128), F32)] * 2 + [pltpu.VMEM((R, 128), F32)] * 2,
        compiler_params=_cparams(("parallel", "arbitrary")),
    )(proj, proj, proj, proj, proj, do, lse, delta, bias.reshape(3, 2, 2 * SPAN, 2 * SPAN), *(into or ()))


def _dilated_merge(os_, lses, *, S, bt=512):
    tile = pl.BlockSpec((bt, 128), lambda i, p: (i, p))

    def body(o0, o1, o2, l0, l1, l2, o_ref, om_ref, lse_ref):
        lane = lax.broadcasted_iota(jnp.int32, (bt, 128), 1)
        lo = lane < DIL_HEAD_DIM
        ls = [l0[...], l1[...], l2[...]]
        ws, stat = [], jnp.zeros((bt, 128), F32)
        for e in range(2):
            a = [l[:, e:e + 1] for l in ls]
            m = jnp.maximum(jnp.maximum(a[0], a[1]), a[2])
            ex = [jnp.exp(v - m) for v in a]
            tot = ex[0] + ex[1] + ex[2]
            ws.append([v / tot for v in ex])
            stat = jnp.where(lane == e, m + jnp.log(tot), stat)
        acc = jnp.zeros((bt, 128), F32)
        for gi, o in enumerate((o0, o1, o2)):
            acc = acc + jnp.where(lo, ws[0][gi], ws[1][gi]) * o[...]
        o_ref[...] = acc
        om_ref[...] = _mx(acc)
        lse_ref[...] = stat

    return pl.pallas_call(
        body, name="dil_merge", grid=(S // bt, 2),
        in_specs=[tile] * 6, out_specs=[tile, tile, tile],
        out_shape=[jax.ShapeDtypeStruct((S, 256), F32), jax.ShapeDtypeStruct((S, 256), MXU_DTYPE),
                   jax.ShapeDtypeStruct((S, 256), F32)],
        compiler_params=_cparams(("parallel", "parallel")),
    )(*os_, *lses)


def _dilated_delta(do, o, *, S, bt=512):
    tile = pl.BlockSpec((bt, 128), lambda i, p: (i, p))

    def body(do_ref, o_ref, d_ref):
        lane = lax.broadcasted_iota(jnp.int32, (bt, 128), 1)
        prod = do_ref[...] * o_ref[...]
        d0 = jnp.sum(jnp.where(lane < DIL_HEAD_DIM, prod, 0.0), axis=-1, keepdims=True)
        d1 = jnp.sum(jnp.where(lane >= DIL_HEAD_DIM, prod, 0.0), axis=-1, keepdims=True)
        d_ref[...] = jnp.where(lane == 0, d0, jnp.where(lane == 1, d1, 0.0))

    return pl.pallas_call(
        body, name="dil_delta", grid=(S // bt, 2), in_specs=[tile, tile], out_specs=tile,
        out_shape=jax.ShapeDtypeStruct((S, 256), F32), compiler_params=_cparams(("parallel", "parallel")),
    )(do, o)


MEM_T = 1024
QM_BLK = C_QM // MEM_HEAD_DIM


def _mem_attn_fwd(proj, kv, *, S):
    scale = MEM_HEAD_DIM ** -0.5

    def body(q_ref, k_ref, v_ref, o_ref, om_ref, lse_ref):
        s = _dot(q_ref[...], k_ref[...], "nt") * scale
        m = jnp.max(s, axis=-1, keepdims=True)
        p = jnp.exp(s - m)
        den = jnp.sum(p, axis=-1, keepdims=True)
        o = _dot(p, v_ref[...]) / den
        o_ref[...] = o
        om_ref[...] = _mx(o)
        lse_ref[0] = m + jnp.log(den)

    return pl.pallas_call(
        body, name="mem_attn_fwd", grid=(S // MEM_T, MEM_HEADS),
        in_specs=[pl.BlockSpec((MEM_T, MEM_HEAD_DIM), lambda i, h: (i, QM_BLK + h)),
                  pl.BlockSpec((N_MEM, MEM_HEAD_DIM), lambda i, h: (0, h)),
                  pl.BlockSpec((N_MEM, MEM_HEAD_DIM), lambda i, h: (0, MEM_HEADS + h))],
        out_specs=[pl.BlockSpec((MEM_T, MEM_HEAD_DIM), lambda i, h: (i, h)),
                   pl.BlockSpec((MEM_T, MEM_HEAD_DIM), lambda i, h: (i, h)),
                   pl.BlockSpec((1, MEM_T, 1), lambda i, h: (h, i, 0))],
        out_shape=[jax.ShapeDtypeStruct((S, MEM_WIDTH), F32), jax.ShapeDtypeStruct((S, MEM_WIDTH), MXU_DTYPE),
                   jax.ShapeDtypeStruct((MEM_HEADS, S, 1), F32)],
        compiler_params=_cparams(("parallel", "parallel")),
    )(proj, kv, kv)


def _mem_attn_bwd(proj, kv, om, lse, dom, *, S):
    scale = MEM_HEAD_DIM ** -0.5

    def body(q_ref, k_ref, v_ref, o_ref, lse_ref, do_ref, dq_ref, dk_ref, dv_ref):
        @pl.when(pl.program_id(1) == 0)
        def _():
            dk_ref[...] = jnp.zeros_like(dk_ref)
            dv_ref[...] = jnp.zeros_like(dv_ref)

        qv, kv_, vv, dov = q_ref[...], k_ref[...], v_ref[...], do_ref[...]
        p = jnp.exp(_dot(qv, kv_, "nt") * scale - lse_ref[0])
        delta = jnp.sum(dov * o_ref[...], axis=-1, keepdims=True)
        ds = p * (_dot(dov, vv, "nt") - delta)
        dq_ref[...] = (_dot(ds, kv_) * scale).astype(dq_ref.dtype)
        dk_ref[...] += _dot(ds, qv, "tn") * scale
        dv_ref[...] += _dot(p, dov, "tn")

    tile = pl.BlockSpec((MEM_T, MEM_HEAD_DIM), lambda h, i: (i, h))
    kvo = pl.BlockSpec((N_MEM, MEM_HEAD_DIM), lambda h, i: (0, h))
    return pl.pallas_call(
        body, name="mem_attn_bwd", grid=(MEM_HEADS, S // MEM_T),
        in_specs=[pl.BlockSpec((MEM_T, MEM_HEAD_DIM), lambda h, i: (i, QM_BLK + h)),
                  pl.BlockSpec((N_MEM, MEM_HEAD_DIM), lambda h, i: (0, h)),
                  pl.BlockSpec((N_MEM, MEM_HEAD_DIM), lambda h, i: (0, MEM_HEADS + h)),
                  tile, pl.BlockSpec((1, MEM_T, 1), lambda h, i: (h, i, 0)), tile],
        out_specs=[tile, kvo, kvo],
        out_shape=[jax.ShapeDtypeStruct((S, MEM_WIDTH), MXU_DTYPE), jax.ShapeDtypeStruct((N_MEM, MEM_WIDTH), F32),
                   jax.ShapeDtypeStruct((N_MEM, MEM_WIDTH), F32)],
        compiler_params=_cparams(("parallel", "arbitrary")),
    )(proj, kv, kv, om, lse, dom)


MIX_BM = 1024
MIX_BN = 256
GATES_BLK = C_GATES // MIX_BN


def _mix_specs(j_outer):
    ix = (lambda f: (lambda j, i: f(i, j))) if j_outer else (lambda f: f)
    act = lambda width: pl.BlockSpec((MIX_BM, width), ix(lambda i, j: (i, 0)))
    wgt = lambda width: pl.BlockSpec((width, MIX_BN), ix(lambda i, j: (0, j)))
    gate = lambda b: pl.BlockSpec((MIX_BM, MIX_BN), ix(lambda i, j: (i, GATES_BLK + 4 * b + j)))
    bias = lambda b: pl.BlockSpec((1, MIX_BN), ix(lambda i, j: (0, 4 * b + j)))
    tile = pl.BlockSpec((MIX_BM, MIX_BN), ix(lambda i, j: (i, j)))
    return act, wgt, gate, bias, tile


def _mix_fwd(z_lru, o_dil, om, w_lru, w_dil, w_mem, proj, b_gate, *, S):
    act, wgt, gate, bias, tile = _mix_specs(False)

    def body(zl, od, mo, wl, wd, wm, g0, g1, g2, b0, b1, b2, o_ref):
        acc = jax.nn.sigmoid(g0[...] + b0[...]) * _dot(zl[...], wl[...])
        acc += jax.nn.sigmoid(g1[...] + b1[...]) * _dot(od[...], wd[...])
        acc += jax.nn.sigmoid(g2[...] + b2[...]) * _dot(mo[...], wm[...])
        o_ref[...] = acc.astype(o_ref.dtype)

    return pl.pallas_call(
        body, name="mix_fwd", grid=(S // MIX_BM, D_MODEL // MIX_BN),
        in_specs=[act(D_RNN), act(256), act(MEM_WIDTH), wgt(D_RNN), wgt(256), wgt(MEM_WIDTH),
                  gate(0), gate(1), gate(2), bias(0), bias(1), bias(2)],
        out_specs=tile, out_shape=jax.ShapeDtypeStruct((S, D_MODEL), MXU_DTYPE),
        compiler_params=_cparams(("parallel", "parallel")),
    )(z_lru, o_dil, om, w_lru, w_dil, w_mem, proj, proj, proj, b_gate, b_gate, b_gate)


def _mix_bwd(dmerged, z_lru, o_dil, om, w_lru, w_dil, w_mem, proj, b_gate, *, S):
    act, wgt, gate, bias, tile = _mix_specs(True)

    def body(dm, zl, od, mo, wl, wd, wm, g0, g1, g2, b0, b1, b2,
             dg0, dg1, dg2, dy0, dy1, dy2, db0, db1, db2):
        @pl.when(pl.program_id(1) == 0)
        def _():
            for r in (db0, db1, db2):
                r[...] = jnp.zeros_like(r)

        dmv = dm[...]
        for act_ref, w_ref, g_ref, b_ref, dg_ref, dy_ref, db_ref in (
                (zl, wl, g0, b0, dg0, dy0, db0), (od, wd, g1, b1, dg1, dy1, db1), (mo, wm, g2, b2, dg2, dy2, db2)):
            y = _dot(act_ref[...], w_ref[...])
            gt = jax.nn.sigmoid(g_ref[...] + b_ref[...])
            dgate = dmv * y * gt * (1.0 - gt)
            dg_ref[...] = dgate.astype(dg_ref.dtype)
            dy_ref[...] = (dmv * gt).astype(dy_ref.dtype)
            db_ref[...] += _colsum(dgate)

    big = jax.ShapeDtypeStruct((S, D_MODEL), MXU_DTYPE)
    vec = jax.ShapeDtypeStruct((1, D_MODEL), F32)
    vspec = pl.BlockSpec((1, MIX_BN), lambda j, i: (0, j))
    return pl.pallas_call(
        body, name="mix_bwd", grid=(D_MODEL // MIX_BN, S // MIX_BM),
        in_specs=[tile, act(D_RNN), act(256), act(MEM_WIDTH), wgt(D_RNN), wgt(256), wgt(MEM_WIDTH),
                  gate(0), gate(1), gate(2), bias(0), bias(1), bias(2)],
        out_specs=[tile] * 6 + [vspec] * 3, out_shape=[big] * 6 + [vec] * 3,
        compiler_params=_cparams(("parallel", "arbitrary")),
    )(dmerged, z_lru, o_dil, om, w_lru, w_dil, w_mem, proj, proj, proj, b_gate, b_gate, b_gate)


def _adamw_math(w, g, m, v):
    m = ADAM_B1 * m + (1.0 - ADAM_B1) * g
    v = ADAM_B2 * v + (1.0 - ADAM_B2) * (g * g)
    m_hat = m / (1.0 - ADAM_B1 ** ADAM_STEP)
    v_hat = v / (1.0 - ADAM_B2 ** ADAM_STEP)
    delta = -ADAM_LR * (m_hat / (jnp.sqrt(v_hat) + ADAM_EPS) + ADAM_WD * w)
    return delta, m, v


def _adamw_landed(w, own, land, m, v, *, name, col_blk=0, prev=None):
    R = w.shape[0]
    n_parts, C = land.shape[0], land.shape[2]
    br = next(d for d in (256, 464, 128) if R % d == 0)
    tile = pl.BlockSpec((br, C), lambda i: (i, col_blk))
    part = pl.BlockSpec((br, C), lambda i: (i, 0))
    n_prev = 0 if prev is None else 4

    def body(w_ref, o_ref, l_ref, m_ref, v_ref, *rest):
        g_ref, d_ref, nm_ref, nv_ref = rest[n_prev:]
        g = o_ref[...].astype(F32)
        for p in range(n_parts):
            g = g + l_ref[p].astype(F32)
        d, nm, nv = _adamw_math(w_ref[...], g, m_ref[...], v_ref[...])
        g_ref[...] = g
        d_ref[...] = d
        nm_ref[...] = nm
        nv_ref[...] = nv

    return pl.pallas_call(
        body, name=name, grid=(R // br,),
        in_specs=[tile, part, pl.BlockSpec((n_parts, br, C), lambda i: (0, i, 0)), tile, tile]
        + [pl.BlockSpec(memory_space=pl.ANY)] * n_prev,
        out_specs=[tile] * 4, out_shape=[jax.ShapeDtypeStruct(w.shape, F32)] * 4,
        input_output_aliases={5 + j: j for j in range(n_prev)},
        compiler_params=_cparams(("parallel",)),
    )(w, own, land, m, v, *(prev or ()))


def _adamw_plain(w, g, m, v, *, name):
    def body(w_ref, g_ref, m_ref, v_ref, d_ref, nm_ref, nv_ref):
        d, nm, nv = _adamw_math(w_ref[...], g_ref[...], m_ref[...], v_ref[...])
        d_ref[...] = d
        nm_ref[...] = nm
        nv_ref[...] = nv

    return pl.pallas_call(
        body, name=name, out_shape=[jax.ShapeDtypeStruct(w.shape, F32)] * 3, compiler_params=_cparams(),
    )(w, g, m, v)


def _my_pos():
    return lax.axis_index("x"), lax.axis_index("y"), lax.axis_index("c")


def _dev_index(p):
    return 4 * p[0] + 2 * p[1] + p[2]


def _all_gather(shards):
    n = len(shards)
    hbm = pl.BlockSpec(memory_space=pl.ANY)

    def body(*refs):
        ins, outs = refs[:n], refs[n:2 * n]
        send_sems, recv_sems, local_sems = refs[2 * n:]
        x, y, c = _my_pos()
        me, sibling = (x, y, c), (x, y, 1 - c)
        chips = [(1 - x, y), (x, 1 - y), (1 - x, 1 - y)]

        def copy(a, k, block, to, src=None):
            dst = outs[a].at[_dev_index(block)]
            return pltpu.make_async_remote_copy(
                src_ref=dst if src is None else src, dst_ref=dst,
                send_sem=send_sems.at[a, k], recv_sem=recv_sems.at[a, k], device_id=to, device_id_type=MESH)

        mine = [pltpu.make_async_copy(ins[a], outs[a].at[_dev_index(me)], local_sems.at[a]) for a in range(n)]
        for cp in mine:
            cp.start()
        first = []
        for a in range(n):
            first.append(copy(a, 0, me, sibling, src=ins[a]))
            first += [copy(a, 1 + j, me, (*chip, c), src=ins[a]) for j, chip in enumerate(chips)]
        for cp in first:
            cp.start()
        passed = []
        for j, chip in enumerate(chips):
            for a in range(n):
                copy(a, 1 + j, (*chip, c), me).wait_recv()
                fwd = copy(a, 4 + j, (*chip, c), sibling)
                fwd.start()
                passed.append(fwd)
        for a in range(n):
            copy(a, 0, sibling, me).wait_recv()
        for j, chip in enumerate(chips):
            for a in range(n):
                copy(a, 4 + j, (*chip, 1 - c), me).wait_recv()
        for cp in first + passed:
            cp.wait_send()
        for cp in mine:
            cp.wait()

    return pl.pallas_call(
        body, name="all_gather_weights",
        in_specs=[hbm] * n, out_specs=[hbm] * n,
        out_shape=[jax.ShapeDtypeStruct((N_DEV,) + s.shape, s.dtype) for s in shards],
        scratch_shapes=[pltpu.SemaphoreType.DMA((n, 7)), pltpu.SemaphoreType.DMA((n, 7)), pltpu.SemaphoreType.DMA((n,))],
        compiler_params=pltpu.CompilerParams(has_side_effects=True),
    )(*shards)


def _peers(me):
    x, y, c = me
    out = []
    for k in range(1, 8):
        fx, fy, fc = (k >> 2) & 1, (k >> 1) & 1, k & 1
        out.append((k - 1, (1 - x if fx else x, 1 - y if fy else y, 1 - c if fc else c)))
    return out


HBM_SPEC = pl.BlockSpec(memory_space=pltpu.HBM)
SEM_SPEC = pl.BlockSpec(memory_space=pltpu.SEMAPHORE)
DATAFLOW_EFFECT = pltpu.SideEffectType.DATAFLOW_SIDE_EFFECTING


def _gather_refs(src, land, me, peer, k):
    return src, land.at[_dev_index(me)]


def _scatter_refs(src, land, me, peer, k):
    return src.at[_dev_index(peer)], land.at[k]


def _push_start(srcs, land_shapes, refs_of, name, after=()):
    n, n_after = len(srcs), len(after)

    def body(*refs):
        ins, lands = refs[:n], refs[n:2 * n]
        send_sems, recv_sems, token = refs[2 * n + n_after], refs[2 * n + n_after + 1], refs[-1]
        me = _my_pos()
        for k, peer in _peers(me):
            for a in range(n):
                src, dst = refs_of(ins[a], lands[a], me, peer, k)
                pltpu.make_async_remote_copy(src_ref=src, dst_ref=dst, send_sem=send_sems.at[7 * a + k],
                                             recv_sem=recv_sems.at[7 * a + k], device_id=peer, device_id_type=MESH).start()
        token[...] = jnp.zeros_like(token)

    lands = [lax.empty(shp, s.dtype) for shp, s in zip(land_shapes, srcs)]
    hbm = lambda a: pltpu.with_memory_space_constraint(a, pltpu.HBM)
    res = pl.pallas_call(
        body, name=name,
        out_shape=(pltpu.SemaphoreType.DMA((7 * n,)), pltpu.SemaphoreType.DMA((7 * n,)),
                   *[pltpu.HBM(s.shape, s.dtype) for s in srcs], *[pltpu.HBM(l.shape, l.dtype) for l in lands],
                   jax.ShapeDtypeStruct((8, 128), F32)),
        in_specs=[HBM_SPEC] * (2 * n) + [pl.BlockSpec(memory_space=pl.ANY)] * n_after,
        out_specs=(SEM_SPEC, SEM_SPEC, *[HBM_SPEC] * (2 * n), pl.BlockSpec(memory_space=pltpu.VMEM)),
        input_output_aliases={i: 2 + i for i in range(2 * n)},
        compiler_params=pltpu.CompilerParams(has_side_effects=DATAFLOW_EFFECT),
    )(*[hbm(s) for s in srcs], *[hbm(l) for l in lands], *after)
    return dict(sems=(res[0], res[1]), srcs=list(res[2:2 + n]), lands=list(res[2 + n:2 + 2 * n]), token=res[-1], n=n,
                refs_of=refs_of, name=name)


def _push_wait(started, after):
    n, refs_of = started["n"], started["refs_of"]
    after = list(after) if isinstance(after, (list, tuple)) else [after]

    def body(*refs):
        ins, lands = refs[:n], refs[n:2 * n]
        send_sems, recv_sems = refs[2 * n], refs[2 * n + 1]
        me = _my_pos()
        for k, peer in _peers(me):
            for a in range(n):
                src, dst = refs_of(ins[a], lands[a], me, peer, k)
                cp = pltpu.make_async_remote_copy(src_ref=src, dst_ref=dst, send_sem=send_sems.at[7 * a + k],
                                                  recv_sem=recv_sems.at[7 * a + k], device_id=peer, device_id_type=MESH)
                cp.wait_send()
                cp.wait_recv()

    arrs = started["srcs"] + started["lands"]
    res = pl.pallas_call(
        body, name=started["name"].replace("start", "wait"),
        out_shape=tuple(pltpu.HBM(a.shape, a.dtype) for a in arrs),
        in_specs=[HBM_SPEC] * (2 * n) + [SEM_SPEC, SEM_SPEC] + [pl.BlockSpec(memory_space=pl.ANY)] * len(after),
        out_specs=tuple([HBM_SPEC] * (2 * n)),
        input_output_aliases={i: i for i in range(2 * n)},
        compiler_params=pltpu.CompilerParams(has_side_effects=DATAFLOW_EFFECT),
    )(*arrs, *started["sems"], *after)
    return list(res[n:2 * n])


def _sum_slots(slots):
    def body(in_ref, out_ref):
        acc = in_ref[0]
        for d in range(1, N_DEV):
            acc = acc + in_ref[d]
        out_ref[...] = acc

    return pl.pallas_call(body, name="sum_small", out_shape=jax.ShapeDtypeStruct(slots.shape[1:], F32),
                          compiler_params=_cparams())(slots)


def _adamw_many(ws, gs, ms, vs):
    n = len(ws)

    def body(*refs):
        for i in range(n):
            w_ref, g_ref, m_ref, v_ref = (refs[j * n + i] for j in range(4))
            d_, nm, nv = _adamw_math(w_ref[...], g_ref[...], m_ref[...], v_ref[...])
            for j, val in enumerate((d_, nm, nv)):
                refs[(4 + j) * n + i][...] = val

    res = pl.pallas_call(body, name="adamw_small", out_shape=[jax.ShapeDtypeStruct(w_.shape, F32) for w_ in ws] * 3,
                         compiler_params=_cparams())(*ws, *gs, *ms, *vs)
    return [(res[i], res[n + i], res[2 * n + i]) for i in range(n)]


def _local_step(x, mem, tgt, W, P, late_weights, send_grads, reduce_small, tie0):
    S = x.shape[0]
    W = dict(W)
    h = _rmsnorm_fwd(x, P["g_mix"] + tie0, rows=S, name="norm_mix")
    proj = _matmul(h, W["w_in_t"], M=S, N=D_IN, K=D_MODEL, mode="nt", bm=512, bn=D_IN // 2, bk=D_MODEL, name="mm_in",
                   j_outer=True)

    wa_bd, wx_bd = _mx(_block_diag(P["w_rg_a"])), _mx(_block_diag(P["w_rg_x"]))
    lru_args = (W["conv_w"], P["conv_b"].reshape(1, -1), wa_bd, wx_bd, P["b_rg_a"].reshape(1, -1),
                P["b_rg_x"].reshape(1, -1), P["lru_lambda"].reshape(1, -1))
    hl, z_lru, a_lru, mult_lru = _lru_fwd(proj, *lru_args, S=S)

    buckets = _dil_buckets()
    bias = _dil_bias(P["rel_bias"], buckets)
    group_out = [_dilated_fwd(proj, bias, g, S=S) for g in range(len(DIL_GROUPS))]
    o_dil, o_dil_m, lse_dil = _dilated_merge([o for o, _ in group_out], [l for _, l in group_out], S=S)

    W.update(late_weights("branch", [o_dil, z_lru]))
    mem_n = _rmsnorm_fwd(mem, P["g_mem"], rows=N_MEM, name="norm_mem")
    kv = _matmul(mem_n, W["w_mem_kv"], M=N_MEM, N=2 * MEM_WIDTH, K=D_MODEL, mode="nn", bm=N_MEM, bn=512, bk=D_MODEL,
                 name="mm_kv")
    om, om_m, lse_mem = _mem_attn_fwd(proj, kv, S=S)
    b_gate = P["b_gate"].reshape(1, -1)
    merged = _mix_fwd(z_lru, o_dil_m, om_m, W["w_lru_out"], W["w_dil_out"], W["w_mem_out"], proj, b_gate, S=S)
    x1 = _matmul(merged, W["w_out"], M=S, N=D_MODEL, K=D_MODEL, mode="nn", bm=512, bn=D_MODEL, bk=D_MODEL, name="mm_out",
                 epilogue=lambda acc, r: (r + acc,), extras=[(x, (0, 0))])
    hm = _rmsnorm_fwd(x1, P["g_mlp"], rows=S, name="norm_mlp")
    W.update(late_weights("mlp", [hm]))

    def relu2(acc):
        rl = jnp.maximum(acc, 0.0)
        return (rl * rl,)

    act = _matmul(hm, W["w_mlp_in_t"], M=S, N=D_FF, K=D_MODEL, mode="nt", bm=1024, bn=1024, bk=D_MODEL, name="mm_mlp_in",
                  out_dtypes=(MXU_DTYPE,), epilogue=relu2, j_outer=True)
    x2 = _matmul(act, W["w_mlp_out"], M=S, N=D_MODEL, K=D_FF, mode="nn", bm=512, bn=D_MODEL, bk=D_FF, name="mm_mlp_out",
                 epilogue=lambda acc, r: (r + acc,), extras=[(x1, (0, 0))])
    loss, dx2, dx2_m, dg_final = _loss_head(x2, P["g_final"], tgt, rows=S)

    G, Gs = {}, {}
    Gs["g_final"] = dg_final
    dw = dict(mode="tn", K=S, bk=S, out_dtypes=(MXU_DTYPE,))
    G["w_mlp_out"] = _matmul(act, dx2_m, M=D_FF, N=D_MODEL, bm=512, bn=D_MODEL, name="mm_dw_mlp_out",
                             parts=("rows", D_FF // N_DEV), **dw)
    du = _matmul(dx2_m, W["w_mlp_out"], M=S, N=D_FF, K=D_MODEL, mode="nt", bm=1024, bn=1024, bk=D_MODEL, name="mm_du",
                 out_dtypes=(MXU_DTYPE,), epilogue=lambda acc, a: (acc * (2.0 * jnp.sqrt(a.astype(F32))),),
                 extras=[(act, (0, 0))], j_outer=True)
    G["w_mlp_in"] = _matmul(hm, du, M=D_MODEL, N=D_FF, bm=D_MODEL, bn=512, name="mm_dw_mlp_in",
                            parts=("cols", D_FF // N_DEV), **dw)
    tie1 = send_grads({n: G.pop(n) for n in ("w_mlp_out", "w_mlp_in")})
    dhm = _matmul(du, W["w_mlp_in_t"], M=S, N=D_MODEL, K=D_FF, mode="nn", bm=512, bn=D_MODEL, bk=D_FF, name="mm_dhm",
                  deps=[tie1])
    dx1, dx1_m, Gs["g_mlp"] = _rmsnorm_bwd(x1, P["g_mlp"], dhm, dx2, rows=S, name="norm_mlp_bwd",
                                           dx_dtypes=(F32, MXU_DTYPE))
    G["w_out"] = _matmul(merged, dx1_m, M=D_MODEL, N=D_MODEL, bm=512, bn=D_MODEL, name="mm_dw_out",
                         parts=("rows", D_MODEL // N_DEV), **dw)
    dmerged = _matmul(dx1_m, W["w_out"], M=S, N=D_MODEL, K=D_MODEL, mode="nt", bm=512, bn=D_MODEL, bk=D_MODEL, name="mm_dmerged")
    (dg0, dg1, dg2, dy_lru, dy_dil, dy_mem, db0, db1, db2) = _mix_bwd(
        dmerged, z_lru, o_dil_m, om_m, W["w_lru_out"], W["w_dil_out"], W["w_mem_out"], proj, b_gate, S=S)
    Gs["b_gate0"], Gs["b_gate1"], Gs["b_gate2"] = db0, db1, db2

    G["w_mem_out"] = _matmul(om_m, dy_mem, M=MEM_WIDTH, N=D_MODEL, bm=MEM_WIDTH, bn=D_MODEL, name="mm_dw_mem_out",
                             parts=("cols", D_MODEL // N_DEV), **dw)
    dom = _matmul(dy_mem, W["w_mem_out"], M=S, N=MEM_WIDTH, K=D_MODEL, mode="nt", bm=512, bn=MEM_WIDTH, bk=D_MODEL,
                  name="mm_dom")
    dqm, dk_mem, dv_mem = _mem_attn_bwd(proj, kv, om, lse_mem, dom, S=S)
    dkv = jnp.concatenate([dk_mem, dv_mem], axis=1)
    G["w_mem_kv"] = _matmul(mem_n, dkv, M=D_MODEL, N=2 * MEM_WIDTH, K=N_MEM, mode="tn", bm=D_MODEL, bn=2 * MEM_WIDTH,
                            bk=N_MEM, name="mm_dw_kv", out_dtypes=(MXU_DTYPE,), parts=("rows", D_MODEL // N_DEV))
    dmem_n = _matmul(dkv, W["w_mem_kv"], M=N_MEM, N=D_MODEL, K=2 * MEM_WIDTH, mode="nt", bm=N_MEM, bn=D_MODEL,
                     bk=2 * MEM_WIDTH, name="mm_dmem")
    (Gs["g_mem"],) = _rmsnorm_bwd(mem, P["g_mem"], dmem_n, None, rows=N_MEM, name="norm_mem_bwd", dx_dtypes=())

    G["w_dil_out"] = _matmul(o_dil_m, dy_dil, M=256, N=D_MODEL, bm=256, bn=D_MODEL, name="mm_dw_dil_out",
                             parts=("cols", D_MODEL // N_DEV), **dw)
    do_dil = _matmul(dy_dil, W["w_dil_out"], M=S, N=256, K=D_MODEL, mode="nt", bm=512, bn=256, bk=D_MODEL, name="mm_do_dil")
    G["w_lru_out"] = _matmul(z_lru, dy_lru, M=D_RNN, N=D_MODEL, bm=D_RNN, bn=D_MODEL, name="mm_dw_lru_out",
                             parts=("cols", D_MODEL // N_DEV), **dw)
    dz = _matmul(dy_lru, W["w_lru_out"], M=S, N=D_RNN, K=D_MODEL, mode="nt", bm=512, bn=D_RNN, bk=D_MODEL, name="mm_dz_lru")
    tie2 = send_grads({n: G.pop(n) for n in ("w_out", "w_mem_out", "w_mem_kv", "w_dil_out", "w_lru_out")})
    bias = bias + tie2[0, 0]
    delta = _dilated_delta(do_dil, o_dil, S=S)
    dqkv, dbias = None, []
    for g in range(len(DIL_GROUPS)):
        *dqkv, db_g = _dilated_bwd(proj, do_dil, lse_dil, delta, bias, g, S=S, into=dqkv)
        dbias.append(db_g)
    drel = _dil_bias_bwd(jnp.stack(dbias, axis=0).reshape(len(DIL_GROUPS), DIL_HEADS, SPAN, 2 * SPAN), buckets)
    Gs["rel_bias"] = drel

    dxl, dgl, dcw, dcb, dwa, dwx, dba, dbx, dlam = _lru_bwd(proj, hl, a_lru, mult_lru, dz, *lru_args, S=S)
    Gs["conv_w"], Gs["conv_b"] = dcw, dcb
    Gs["w_rg_a"], Gs["w_rg_x"] = _block_diag_extract(dwa), _block_diag_extract(dwx)
    Gs["b_rg_a"], Gs["b_rg_x"], Gs["lru_lambda"] = dba, dbx, dlam
    Gs["loss"] = loss

    dproj = jnp.concatenate([dxl, dgl] + dqkv + [dqm, dg0, dg1, dg2], axis=1)
    cols = D_MODEL // W_IN_PIECES
    tie = []
    for q in range(W_IN_PIECES):
        dw_q = _matmul(h, dproj, M=cols, N=D_IN, K=S, mode="tn", bm=cols, bn=D_IN // 2, bk=1024, name=f"mm_dw_in_{q}",
                       a_off=(0, q), out_dtypes=(MXU_DTYPE,), parts=("rows_t", D_IN // N_DEV), deps=tie)
        tie = [send_grads({f"w_in_{q}": dw_q})]
    dh = _matmul(dproj, W["w_in_t"], M=S, N=D_MODEL, K=D_IN, mode="nn", bm=256, bn=D_MODEL, bk=D_IN, name="mm_dh",
                 deps=tie)
    grad_x, Gs["g_mix"] = _rmsnorm_bwd(x, P["g_mix"], dh, dx1, rows=S, name="norm_mix_bwd")
    return grad_x, reduce_small(Gs)


BIG = ("w_in", "w_lru_out", "w_dil_out", "w_mem_kv", "w_mem_out", "w_out", "w_mlp_in", "w_mlp_out")
W_IN_PIECES = 2
COL_SHARDED = ("w_lru_out", "w_dil_out", "w_mem_out", "w_mlp_in")
GATHERED_TRANSPOSED = ("w_mlp_in",)
SMALL = ("g_mix", "b_gate", "conv_b", "w_rg_a", "b_rg_a", "w_rg_x", "b_rg_x", "lru_lambda", "rel_bias", "g_mem",
         "g_mlp", "g_final")
WEIGHTS = ("g_mix", "w_in", "b_gate", "conv_w", "conv_b", "w_rg_a", "b_rg_a", "w_rg_x", "b_rg_x", "lru_lambda",
           "w_lru_out", "rel_bias", "w_dil_out", "g_mem", "w_mem_kv", "w_mem_out", "w_out", "g_mlp", "w_mlp_in",
           "w_mlp_out", "g_final")


def _gathered_to_full(name, gathered):
    if name in COL_SHARDED:
        n, r, c = gathered.shape
        return gathered.transpose(1, 0, 2).reshape(r, n * c)
    n, r, c = gathered.shape
    return gathered.reshape(n * r, c)


SMALL_GRADS = (("g_mix", (1, 1024)), ("b_gate0", (1, 1024)), ("b_gate1", (1, 1024)), ("b_gate2", (1, 1024)),
               ("conv_b", (1, 768)), ("w_rg_a", (12, 64, 64)), ("b_rg_a", (1, 768)), ("w_rg_x", (12, 64, 64)),
               ("b_rg_x", (1, 768)), ("lru_lambda", (1, 768)), ("rel_bias", (32, 128)), ("g_mem", (1, 1024)),
               ("g_mlp", (1, 1024)), ("g_final", (1, 1024)), ("conv_w", (4, 768)), ("loss", (1, 1)))


def _pack(parts):
    flat = jnp.concatenate([p.reshape(-1) for p in parts])
    return jnp.pad(flat, (0, (-flat.shape[0]) % 1024)).reshape(-1, 128)


def _unpack(pack, shapes):
    flat = pack.reshape(-1)
    out, off = [], 0
    for shp in shapes:
        size = math.prod(shp)
        out.append(flat[off:off + size].reshape(shp))
        off += size
    return out


def kernel(x, mem, g_mix, w_in, b_gate, conv_w, conv_b, w_rg_a, b_rg_a, w_rg_x, b_rg_x, lru_lambda, w_lru_out, rel_bias, w_dil_out, g_mem, w_mem_kv, w_mem_out, w_out, g_mlp, w_mlp_in, w_mlp_out, g_final, loss_target, m_g_mix, m_w_in, m_b_gate, m_conv_w, m_conv_b, m_w_rg_a, m_b_rg_a, m_w_rg_x, m_b_rg_x, m_lru_lambda, m_w_lru_out, m_rel_bias, m_w_dil_out, m_g_mem, m_w_mem_kv, m_w_mem_out, m_w_out, m_g_mlp, m_w_mlp_in, m_w_mlp_out, m_g_final, v_g_mix, v_w_in, v_b_gate, v_conv_w, v_conv_b, v_w_rg_a, v_b_rg_a, v_w_rg_x, v_b_rg_x, v_lru_lambda, v_w_lru_out, v_rel_bias, v_w_dil_out, v_g_mem, v_w_mem_kv, v_w_mem_out, v_w_out, v_g_mlp, v_w_mlp_in, v_w_mlp_out, v_g_final):
    w = dict(g_mix=g_mix, w_in=w_in, b_gate=b_gate, conv_w=conv_w, conv_b=conv_b, w_rg_a=w_rg_a, b_rg_a=b_rg_a,
             w_rg_x=w_rg_x, b_rg_x=b_rg_x, lru_lambda=lru_lambda, w_lru_out=w_lru_out, rel_bias=rel_bias,
             w_dil_out=w_dil_out, g_mem=g_mem, w_mem_kv=w_mem_kv, w_mem_out=w_mem_out, w_out=w_out, g_mlp=g_mlp,
             w_mlp_in=w_mlp_in, w_mlp_out=w_mlp_out, g_final=g_final)
    m = dict(g_mix=m_g_mix, w_in=m_w_in, b_gate=m_b_gate, conv_w=m_conv_w, conv_b=m_conv_b, w_rg_a=m_w_rg_a,
             b_rg_a=m_b_rg_a, w_rg_x=m_w_rg_x, b_rg_x=m_b_rg_x, lru_lambda=m_lru_lambda, w_lru_out=m_w_lru_out,
             rel_bias=m_rel_bias, w_dil_out=m_w_dil_out, g_mem=m_g_mem, w_mem_kv=m_w_mem_kv, w_mem_out=m_w_mem_out,
             w_out=m_w_out, g_mlp=m_g_mlp, w_mlp_in=m_w_mlp_in, w_mlp_out=m_w_mlp_out, g_final=m_g_final)
    v = dict(g_mix=v_g_mix, w_in=v_w_in, b_gate=v_b_gate, conv_w=v_conv_w, conv_b=v_conv_b, w_rg_a=v_w_rg_a,
             b_rg_a=v_b_rg_a, w_rg_x=v_w_rg_x, b_rg_x=v_b_rg_x, lru_lambda=v_lru_lambda, w_lru_out=v_w_lru_out,
             rel_bias=v_rel_bias, w_dil_out=v_w_dil_out, g_mem=v_g_mem, w_mem_kv=v_w_mem_kv, w_mem_out=v_w_mem_out,
             w_out=v_w_out, g_mlp=v_g_mlp, w_mlp_in=v_w_mlp_in, w_mlp_out=v_w_mlp_out, g_final=v_g_final)

    my_idx = _dev_index(_my_pos())

    g_in, g_cw = _all_gather([_mx(w["w_in"].T), w["conv_w"]])
    W = {"w_in_t": g_in.reshape(D_IN, D_MODEL), "conv_w": g_cw.transpose(1, 0, 2).reshape(CONV_WIDTH, D_RNN)}
    late, order_after = {}, [g_in]
    for group, names in (("branch", ("w_mem_kv", "w_lru_out", "w_dil_out", "w_mem_out", "w_out")),
                         ("mlp", ("w_mlp_in", "w_mlp_out"))):
        shards = [_mx(w[n].T if n in GATHERED_TRANSPOSED else w[n]) for n in names]
        started = _push_start(shards, [(N_DEV,) + s.shape for s in shards], _gather_refs, f"gather_{group}_start",
                              after=order_after)
        late[group] = (names, shards, started)
        order_after = [started["token"]]
    P = {n: w[n] for n in SMALL}

    def late_weights(group, after):
        names, shards, started = late[group]
        out = {}
        for n, land, own in zip(names, _push_wait(started, after), shards):
            full = lax.dynamic_update_index_in_dim(land, own, my_idx, 0)
            if n in GATHERED_TRANSPOSED:
                out[n + "_t"] = full.reshape(-1, full.shape[2])
            else:
                out[n] = _gathered_to_full(n, full)
        return out

    sent, small = [], {}

    def send_grads(gs):
        names = list(gs)
        parts = [gs[n] for n in names]
        own = [lax.dynamic_index_in_dim(p, my_idx, 0, keepdims=False) for p in parts]
        started = _push_start(parts, [(N_DEV - 1,) + p.shape[1:] for p in parts], _scatter_refs,
                              f"scatter{len(sent)}_start")
        sent.append((names, own, started))
        return started["token"]

    def reduce_small(gs):
        small["pack"] = _pack([gs[n] for n, _ in SMALL_GRADS])
        small["started"] = _push_start([small["pack"]], [(N_DEV,) + small["pack"].shape], _gather_refs, "small_start")
        return small["started"]["token"]

    grad_x, last_token = _local_step(x[0], mem[0], loss_target[0], W, P, late_weights, send_grads, reduce_small,
                                     late["mlp"][2]["token"][0, 0])

    grads, deltas, new_m, new_v = {}, {}, {}, {}
    after = last_token
    for names, own, started in sent[:-W_IN_PIECES]:
        for n, o, land in zip(names, own, _push_wait(started, after)):
            grads[n], deltas[n], new_m[n], new_v[n] = _adamw_landed(w[n], o, land, m[n], v[n], name=f"adamw_{n}")
            after = deltas[n]
    (small_land,) = _push_wait(small["started"], after)
    total = _sum_slots(lax.dynamic_update_index_in_dim(small_land, small["pack"], my_idx, 0))
    summed = dict(zip([n for n, _ in SMALL_GRADS], _unpack(total, [shp for _, shp in SMALL_GRADS])))
    summed["b_gate"] = jnp.concatenate([summed.pop(f"b_gate{b}") for b in range(3)], axis=1)
    summed["rel_bias"] = summed["rel_bias"][:, :3 * DIL_HEADS]
    for n in SMALL:
        grads[n] = summed[n].reshape(w[n].shape)
    small_updates = _adamw_many([w[n] for n in SMALL], [grads[n] for n in SMALL], [m[n] for n in SMALL],
                                [v[n] for n in SMALL])
    for n, (d_, nm_, nv_) in zip(SMALL, small_updates):
        deltas[n], new_m[n], new_v[n] = d_, nm_, nv_
    conv_w_sum, loss_sum = summed["conv_w"], summed["loss"]
    after = total
    prev = None
    for q, (names, own, started) in enumerate(sent[-W_IN_PIECES:]):
        (land,) = _push_wait(started, after)
        prev = _adamw_landed(w["w_in"].T, own[0], land, m["w_in"].T, v["w_in"].T, name=f"adamw_{names[0]}",
                             col_blk=q, prev=prev)
    grads["w_in"], deltas["w_in"], new_m["w_in"], new_v["w_in"] = [t.T for t in prev]
    cw_cols = D_RNN // N_DEV
    grads["conv_w"] = lax.dynamic_slice(conv_w_sum, (0, my_idx * cw_cols), (CONV_WIDTH, cw_cols))
    deltas["conv_w"], new_m["conv_w"], new_v["conv_w"] = _adamw_plain(
        w["conv_w"], grads["conv_w"], m["conv_w"], v["conv_w"], name="adamw_conv_w")

    return (loss_sum.reshape(()), grad_x[None], *[grads[n] for n in WEIGHTS], *[deltas[n] for n in WEIGHTS],
            *[new_m[n] for n in WEIGHTS], *[new_v[n] for n in WEIGHTS])
```

```python
import functools
import math

import jax
import jax.numpy as jnp
from jax import lax
from jax.experimental import pallas as pl
from jax.experimental.pallas import tpu as pltpu

F32 = jnp.float32
MXU_DTYPE = jnp.bfloat16
VMEM_LIMIT_BYTES = 56 * 1024 * 1024
N_DEV = 8

D_MODEL = 1024
N_MEM = 256
MEM_HEADS = 4
MEM_HEAD_DIM = 128
MEM_WIDTH = 512
D_RNN = 768
LRU_BLOCK = 64
N_LRU_BLOCKS = 12
LRU_GROUP = 256
N_LRU_GROUPS = 3
CONV_WIDTH = 4
LRU_C = 8.0
DIL_GROUPS = ((128, 1), (512, 4), (2048, 16))
SPAN = 128
DIL_HEADS = 4
DIL_HEAD_DIM = 64
NUM_BUCKETS = 32
MAX_DISTANCE = 2048
D_FF = 4096
D_IN = 7424
EPS = 1e-6
NEG = -1e30
C_XL, C_GATE, C_QKV, C_QM, C_GATES = 0, 768, 1536, 3840, 4352

ADAM_LR = 0.001
ADAM_B1 = 0.9
ADAM_B2 = 0.999
ADAM_EPS = 1e-08
ADAM_WD = 0.01
ADAM_STEP = 10

MESH = pl.DeviceIdType.MESH
GELU_K = math.sqrt(2.0 / math.pi)


def _cparams(sem=None):
    kw = dict(vmem_limit_bytes=VMEM_LIMIT_BYTES)
    if sem is not None:
        kw["dimension_semantics"] = sem
    return pltpu.CompilerParams(**kw)


def _mx(v):
    return v.astype(MXU_DTYPE)


def _dot(a, b, mode="nn"):
    dims = {"nn": (((1,), (0,)), ((), ())), "nt": (((1,), (1,)), ((), ())), "tn": (((0,), (0,)), ((), ()))}[mode]
    return lax.dot_general(_mx(a), _mx(b), dims, preferred_element_type=F32)


def _colsum(v):
    return jnp.sum(v, axis=0, keepdims=True)


def _matmul(a, b, *, M, N, K, mode, bm, bn, bk, name, out_dtypes=(F32,), epilogue=None, extras=(),
            a_off=(0, 0), b_off=(0, 0), j_outer=False, deps=(), parts=None):
    assert M % bm == 0 and N % bn == 0 and K % bk == 0, (name, M, N, K, bm, bn, bk)
    nm, nn, nk = M // bm, N // bn, K // bk

    def ij(f):
        if j_outer:
            return lambda j, i, k: f(i, j, k)
        return f

    if mode == "tn":
        a_spec = pl.BlockSpec((bk, bm), ij(lambda i, j, k: (k + a_off[0], i + a_off[1])))
    else:
        a_spec = pl.BlockSpec((bm, bk), ij(lambda i, j, k: (i + a_off[0], k + a_off[1])))
    if mode == "nt":
        b_spec = pl.BlockSpec((bn, bk), ij(lambda i, j, k: (j + b_off[0], k + b_off[1])))
    else:
        b_spec = pl.BlockSpec((bk, bn), ij(lambda i, j, k: (k + b_off[0], j + b_off[1])))
    ex_specs = [pl.BlockSpec((bm, bn), ij(functools.partial(lambda i, j, k, o: (i + o[0], j + o[1]), o=off)))
                for _, off in extras]
    if parts is None:
        out_dims = (M, N)
        out_spec = pl.BlockSpec((bm, bn), ij(lambda i, j, k: (i, j)))
    elif parts[0] == "rows":
        r = parts[1]
        assert bm % r == 0
        out_dims = (M // r, r, N)
        out_spec = pl.BlockSpec((bm // r, r, bn), ij(lambda i, j, k: (i, 0, j)))
    elif parts[0] == "rows_t":
        r = parts[1]
        assert bn % r == 0
        out_dims = (N // r, r, M)
        out_spec = pl.BlockSpec((bn // r, r, bm), ij(lambda i, j, k: (j, 0, i)))
    else:
        c = parts[1]
        assert bn % c == 0
        out_dims = (N // c, M, c)
        out_spec = pl.BlockSpec((bn // c, bm, c), ij(lambda i, j, k: (j, i, 0)))
    n_ex, n_out, n_dep = len(extras), len(out_dtypes), len(deps)

    def body(*refs):
        a_ref, b_ref = refs[0], refs[1]
        ex = refs[2:2 + n_ex]
        outs = refs[2 + n_ex + n_dep:2 + n_ex + n_dep + n_out]
        part = _dot(a_ref[...], b_ref[...], mode)

        def finish(acc):
            vals = epilogue(acc, *[e[...] for e in ex]) if epilogue is not None else (acc,)
            for o, v in zip(outs, vals):
                if parts is not None and parts[0] == "rows_t":
                    v = v.T
                v = v.astype(o.dtype)
                if parts is None:
                    o[...] = v
                elif parts[0] in ("rows", "rows_t"):
                    for ch in range(v.shape[0] // parts[1]):
                        o[ch] = v[ch * parts[1]:(ch + 1) * parts[1], :]
                else:
                    for ch in range(bn // parts[1]):
                        o[ch] = v[:, ch * parts[1]:(ch + 1) * parts[1]]

        if nk == 1:
            finish(part)
        else:
            acc_ref = refs[-1]
            k = pl.program_id(2)

            @pl.when(k == 0)
            def _():
                acc_ref[...] = part

            @pl.when(k > 0)
            def _():
                acc_ref[...] += part

            @pl.when(k == nk - 1)
            def _():
                finish(acc_ref[...])

    grid = (nn, nm, nk) if j_outer else (nm, nn, nk)
    res = pl.pallas_call(
        body, name=name, grid=grid,
        in_specs=[a_spec, b_spec] + ex_specs + [pl.BlockSpec(memory_space=pl.ANY)] * n_dep,
        out_specs=[out_spec] * n_out,
        out_shape=[jax.ShapeDtypeStruct(out_dims, dt) for dt in out_dtypes],
        scratch_shapes=[pltpu.VMEM((bm, bn), F32)] if nk > 1 else [],
        compiler_params=_cparams(("parallel", "parallel", "arbitrary")),
    )(a, b, *[e for e, _ in extras], *deps)
    return res[0] if n_out == 1 else res


def _rmsnorm_fwd(x, g, *, rows, name, bt=512):
    bt = min(bt, rows)

    def body(x_ref, g_ref, o_ref):
        xv = x_ref[...]
        r = lax.rsqrt(jnp.mean(xv * xv, axis=-1, keepdims=True) + EPS)
        o_ref[...] = (xv * r * g_ref[...]).astype(o_ref.dtype)

    return pl.pallas_call(
        body, name=name, grid=(rows // bt,),
        in_specs=[pl.BlockSpec((bt, D_MODEL), lambda i: (i, 0)), pl.BlockSpec((1, D_MODEL), lambda i: (0, 0))],
        out_specs=pl.BlockSpec((bt, D_MODEL), lambda i: (i, 0)),
        out_shape=jax.ShapeDtypeStruct((rows, D_MODEL), MXU_DTYPE),
        compiler_params=_cparams(("parallel",)),
    )(x, g.reshape(1, D_MODEL))


def _rms_bwd_tile(xv, gv, dyv):
    r = lax.rsqrt(jnp.mean(xv * xv, axis=-1, keepdims=True) + EPS)
    w = dyv * gv
    dx = r * w - xv * (r * r * r) * jnp.mean(w * xv, axis=-1, keepdims=True)
    dg = _colsum(dyv * xv * r)
    return dx, dg


def _rmsnorm_bwd(x, g, dy, res, *, rows, name, bt=512, dx_dtypes=(F32,)):
    bt = min(bt, rows)
    has_res = res is not None

    def body(*refs):
        x_ref, g_ref, dy_ref = refs[:3]
        res_ref = refs[3] if has_res else None
        outs = refs[3 + int(has_res):]
        dx, dg = _rms_bwd_tile(x_ref[...], g_ref[...], dy_ref[...])
        if has_res:
            dx = dx + res_ref[...]
        dg_ref = outs[-1]

        @pl.when(pl.program_id(0) == 0)
        def _():
            dg_ref[...] = jnp.zeros_like(dg_ref)

        dg_ref[...] += dg
        for o in outs[:-1]:
            o[...] = dx.astype(o.dtype)

    row_spec = pl.BlockSpec((bt, D_MODEL), lambda i: (i, 0))
    vec_spec = pl.BlockSpec((1, D_MODEL), lambda i: (0, 0))
    ins = [x, g.reshape(1, D_MODEL), dy] + ([res] if has_res else [])
    return pl.pallas_call(
        body, name=name, grid=(rows // bt,),
        in_specs=[row_spec, vec_spec, row_spec] + ([row_spec] if has_res else []),
        out_specs=[row_spec] * len(dx_dtypes) + [vec_spec],
        out_shape=[jax.ShapeDtypeStruct((rows, D_MODEL), dt) for dt in dx_dtypes] + [jax.ShapeDtypeStruct((1, D_MODEL), F32)],
        compiler_params=_cparams(("arbitrary",)),
    )(*ins)


def _loss_head(x2, g, tgt, *, rows, bt=512):
    def body(x_ref, g_ref, t_ref, loss_ref, dx_ref, dxm_ref, dg_ref):
        xv, gv = x_ref[...], g_ref[...]
        r = lax.rsqrt(jnp.mean(xv * xv, axis=-1, keepdims=True) + EPS)
        diff = xv * r * gv - t_ref[...]
        part = jnp.sum(jnp.mean(diff * diff, axis=-1, keepdims=True), axis=0, keepdims=True) * 0.5
        dx, dg = _rms_bwd_tile(xv, gv, diff * (1.0 / D_MODEL))

        @pl.when(pl.program_id(0) == 0)
        def _():
            loss_ref[...] = jnp.zeros_like(loss_ref)
            dg_ref[...] = jnp.zeros_like(dg_ref)

        loss_ref[...] += part
        dg_ref[...] += dg
        dx_ref[...] = dx
        dxm_ref[...] = _mx(dx)

    row_spec = pl.BlockSpec((bt, D_MODEL), lambda i: (i, 0))
    vec_spec = pl.BlockSpec((1, D_MODEL), lambda i: (0, 0))
    return pl.pallas_call(
        body, name="loss_head", grid=(rows // bt,),
        in_specs=[row_spec, vec_spec, row_spec],
        out_specs=[pl.BlockSpec((1, 1), lambda i: (0, 0)), row_spec, row_spec, vec_spec],
        out_shape=[jax.ShapeDtypeStruct((1, 1), F32), jax.ShapeDtypeStruct((rows, D_MODEL), F32),
                   jax.ShapeDtypeStruct((rows, D_MODEL), MXU_DTYPE), jax.ShapeDtypeStruct((1, D_MODEL), F32)],
        compiler_params=_cparams(("arbitrary",)),
    )(x2, g.reshape(1, D_MODEL), tgt)


LRU_T = 512
SCAN_GROUPS = 4


def _gelu(x):
    t = jnp.tanh(GELU_K * (x + 0.044715 * x * x * x))
    return 0.5 * x * (1.0 + t), t


def _gelu_grad(x, t):
    return 0.5 * (1.0 + t) + 0.5 * x * (1.0 - t * t) * GELU_K * (1.0 + 3.0 * 0.044715 * x * x)


def _softplus_neg(lam):
    z = -lam
    u = jnp.exp(-jnp.abs(z))
    w = 1.0 + u
    l1p = jnp.where(w == 1.0, u, jnp.log(w) * u / jnp.where(w == 1.0, 1.0, w - 1.0))
    return jnp.maximum(z, 0.0) + l1p


def _shift_down(cur, prev8, k, row8):
    y = pltpu.roll(cur, k, 0)
    head = jnp.where(row8 < k, pltpu.roll(prev8, k, 0), y[0:8])
    return jnp.concatenate([head, y[8:]], axis=0)


def _shift_up(cur, next8, k, row8):
    n = cur.shape[0]
    y = pltpu.roll(cur, n - k, 0)
    tail = jnp.where(row8 >= 8 - k, pltpu.roll(next8, 8 - k, 0), y[n - 8:n])
    return jnp.concatenate([y[0:n - 8], tail], axis=0)


def _lru_gates(xl, p8, cw, cb, wa, wx, ba, bx, lam, row8, a_mult=None):
    sh = [xl] + [_shift_down(xl, p8, k, row8) for k in (1, 2, 3)]
    xc = cb + cw[3:4] * sh[0] + cw[2:3] * sh[1] + cw[1:2] * sh[2] + cw[0:1] * sh[3]
    r = jax.nn.sigmoid(_dot(xc, wa) + ba)
    i = jax.nn.sigmoid(_dot(xc, wx) + bx)
    sp = _softplus_neg(lam)
    if a_mult is None:
        la = -LRU_C * r * sp
        a = jnp.exp(la)
        mult = jnp.sqrt(jnp.tanh(-la) * (a * a + 1.0))
    else:
        a, mult = a_mult
    return dict(sh=sh, xc=xc, r=r, i=i, sp=sp, a=a, mult=mult)


def _lru_specs(n_t, reverse):
    T = LRU_T
    tt = (lambda t: n_t - 1 - t) if reverse else (lambda t: t)
    blk = lambda col0: pl.BlockSpec((T, LRU_GROUP), lambda g, t: (tt(t), col0 + g))
    prev8 = lambda col0: pl.BlockSpec((8, LRU_GROUP), lambda g, t: (jnp.maximum(tt(t) * (T // 8) - 1, 0), col0 + g))
    vec = lambda rows: pl.BlockSpec((rows, LRU_GROUP), lambda g, t: (0, g))
    wbd = pl.BlockSpec((1, LRU_GROUP, LRU_GROUP), lambda g, t: (g, 0, 0))
    return blk, prev8, vec, wbd


def _lru_fwd(proj, conv_w, conv_b, wa_bd, wx_bd, b_a, b_x, lam, *, S):
    T = LRU_T
    n_t = S // T
    blk, _, vec, wbd = _lru_specs(n_t, False)

    def body(xl_ref, gate_ref, cw_ref, cb_ref, wa_ref, wx_ref, ba_ref, bx_ref, lam_ref,
             hl_ref, z_ref, a_s, m_ref, prev8, hcar, b_s):
        @pl.when(pl.program_id(1) == 0)
        def _():
            prev8[...] = jnp.zeros_like(prev8)
            hcar[...] = jnp.zeros_like(hcar)

        row8 = lax.broadcasted_iota(jnp.int32, (8, LRU_GROUP), 0)
        xl = xl_ref[...]
        q = _lru_gates(xl, prev8[...], cw_ref[...], cb_ref[...], wa_ref[0], wx_ref[0], ba_ref[...], bx_ref[...],
                       lam_ref[...], row8)
        prev8[...] = xl[T - 8:T]
        a_s[...] = q["a"]
        m_ref[...] = q["mult"]
        b_s[...] = q["mult"] * q["i"] * q["xc"]

        def step(c, carry):
            local = []
            for u in range(SCAN_GROUPS):
                off = pl.multiple_of((c * SCAN_GROUPS + u) * 8, 8)
                A = a_s[pl.ds(off, 8), :]
                B = b_s[pl.ds(off, 8), :]
                for k in (1, 2, 4):
                    a_sh = jnp.where(row8 >= k, pltpu.roll(A, k, 0), 1.0)
                    b_sh = jnp.where(row8 >= k, pltpu.roll(B, k, 0), 0.0)
                    B = A * b_sh + B
                    A = A * a_sh
                local.append((off, A, B))
            for off, A, B in local:
                h = A * carry + B
                hl_ref[pl.ds(off, 8), :] = h
                carry = h[7:8, :]
            return carry

        hcar[...] = lax.fori_loop(0, T // (8 * SCAN_GROUPS), step, hcar[...])
        ge, _ = _gelu(gate_ref[...])
        z_ref[...] = (ge * hl_ref[...]).astype(z_ref.dtype)

    return pl.pallas_call(
        body, name="lru_fwd", grid=(N_LRU_GROUPS, n_t),
        in_specs=[blk(C_XL // LRU_GROUP), blk(C_GATE // LRU_GROUP), vec(4), vec(1), wbd, wbd, vec(1), vec(1), vec(1)],
        out_specs=[blk(0)] * 4,
        out_shape=[jax.ShapeDtypeStruct((S, D_RNN), F32), jax.ShapeDtypeStruct((S, D_RNN), MXU_DTYPE),
                   jax.ShapeDtypeStruct((S, D_RNN), F32), jax.ShapeDtypeStruct((S, D_RNN), F32)],
        scratch_shapes=[pltpu.VMEM((8, LRU_GROUP), F32), pltpu.VMEM((1, LRU_GROUP), F32), pltpu.VMEM((T, LRU_GROUP), F32)],
        compiler_params=_cparams(("parallel", "arbitrary")),
    )(proj, proj, conv_w, conv_b, wa_bd, wx_bd, b_a, b_x, lam)


def _lru_bwd(proj, hl, a_fwd, mult_fwd, dz, conv_w, conv_b, wa_bd, wx_bd, b_a, b_x, lam, *, S):
    T = LRU_T
    n_t = S // T
    blk, prev8s, vec, wbd = _lru_specs(n_t, True)

    def body(xl_ref, xlp_ref, gate_ref, hl_ref, hlp_ref, a_ref, m_ref, dz_ref, cw_ref, cb_ref, wa_ref, wx_ref, ba_ref,
             bx_ref, lam_ref, dxl_ref, dgate_ref, dcw_ref, dcb_ref, dwa_ref, dwx_ref, dba_ref, dbx_ref, dlam_ref,
             next8, gcar, c_s, b_s, l_s):
        t = pl.program_id(1)
        first_chunk = t == n_t - 1

        @pl.when(t == 0)
        def _():
            next8[...] = jnp.zeros_like(next8)
            gcar[...] = jnp.zeros_like(gcar)
            for ref in (dcw_ref, dcb_ref, dwa_ref, dwx_ref, dba_ref, dbx_ref, dlam_ref):
                ref[...] = jnp.zeros_like(ref)

        row8 = lax.broadcasted_iota(jnp.int32, (8, LRU_GROUP), 0)
        rowT = lax.broadcasted_iota(jnp.int32, (T, LRU_GROUP), 0)
        keep = jnp.where(first_chunk, 0.0, 1.0)
        xl = xl_ref[...]
        wa, wx, lam_v = wa_ref[0], wx_ref[0], lam_ref[...]
        q = _lru_gates(xl, xlp_ref[...] * keep, cw_ref[...], cb_ref[...], wa, wx, ba_ref[...], bx_ref[...], lam_v, row8,
                       a_mult=(a_ref[...], m_ref[...]))
        a, mult, r, i, xc, sp = q["a"], q["mult"], q["r"], q["i"], q["xc"], q["sp"]
        hl_v = hl_ref[...]
        dz_v = dz_ref[...]
        gate = gate_ref[...]
        ge, th = _gelu(gate)
        dgate_ref[...] = (dz_v * hl_v * _gelu_grad(gate, th)).astype(dgate_ref.dtype)

        c_s[...] = jnp.where(rowT == T - 1, 0.0, pltpu.roll(a, T - 1, 0))
        b_s[...] = dz_v * ge + jnp.where(rowT == T - 1, gcar[...], 0.0)

        def step(n, carry):
            local = []
            for u in range(SCAN_GROUPS):
                off = pl.multiple_of((T // 8 - 1 - (n * SCAN_GROUPS + u)) * 8, 8)
                C = c_s[pl.ds(off, 8), :]
                B = b_s[pl.ds(off, 8), :]
                for k in (1, 2, 4):
                    c_sh = jnp.where(row8 < 8 - k, pltpu.roll(C, 8 - k, 0), 1.0)
                    b_sh = jnp.where(row8 < 8 - k, pltpu.roll(B, 8 - k, 0), 0.0)
                    B = B + C * b_sh
                    C = C * c_sh
                local.append((off, C, B))
            for off, C, B in local:
                lam_t = B + C * carry
                l_s[pl.ds(off, 8), :] = lam_t
                carry = lam_t[0:1, :]
            return carry

        lax.fori_loop(0, T // (8 * SCAN_GROUPS), step, jnp.zeros((1, LRU_GROUP), F32))
        lmb = l_s[...]
        gcar[...] = a[0:1, :] * lmb[0:1, :]

        h_prev = _shift_down(hl_v, hlp_ref[...] * keep, 1, row8)
        da = lmb * h_prev
        dmult = lmb * i * xc
        di = lmb * mult * xc
        dxc = lmb * mult * i
        dla = da * a - dmult * (a * a) / mult
        dr = dla * (-LRU_C * sp)
        dlam_ref[...] += _colsum(dla * (-LRU_C * r)) * (-jax.nn.sigmoid(-lam_v))
        dpa = dr * r * (1.0 - r)
        dpx = di * i * (1.0 - i)
        dxc = dxc + _dot(dpa, wa, "nt") + _dot(dpx, wx, "nt")
        dwa_ref[0] += _dot(xc, dpa, "tn")
        dwx_ref[0] += _dot(xc, dpx, "tn")
        dba_ref[...] += _colsum(dpa)
        dbx_ref[...] += _colsum(dpx)
        dcb_ref[...] += _colsum(dxc)
        cw = cw_ref[...]
        n8 = next8[...]
        dxl = cw[3:4] * dxc
        for k in (1, 2, 3):
            dxl = dxl + cw[3 - k:4 - k] * _shift_up(dxc, n8, k, row8)
        for k in range(4):
            dcw_ref[3 - k:4 - k, :] += _colsum(dxc * q["sh"][k])
        next8[...] = dxc[0:8]
        dxl_ref[...] = dxl.astype(dxl_ref.dtype)

    res = pl.pallas_call(
        body, name="lru_bwd", grid=(N_LRU_GROUPS, n_t),
        in_specs=[blk(C_XL // LRU_GROUP), prev8s(C_XL // LRU_GROUP), blk(C_GATE // LRU_GROUP), blk(0), prev8s(0), blk(0),
                  blk(0), blk(0), vec(4), vec(1), wbd, wbd, vec(1), vec(1), vec(1)],
        out_specs=[blk(0), blk(0), vec(4), vec(1), wbd, wbd, vec(1), vec(1), vec(1)],
        out_shape=[jax.ShapeDtypeStruct((S, D_RNN), MXU_DTYPE), jax.ShapeDtypeStruct((S, D_RNN), MXU_DTYPE),
                   jax.ShapeDtypeStruct((4, D_RNN), F32), jax.ShapeDtypeStruct((1, D_RNN), F32),
                   jax.ShapeDtypeStruct((N_LRU_GROUPS, LRU_GROUP, LRU_GROUP), F32),
                   jax.ShapeDtypeStruct((N_LRU_GROUPS, LRU_GROUP, LRU_GROUP), F32),
                   jax.ShapeDtypeStruct((1, D_RNN), F32), jax.ShapeDtypeStruct((1, D_RNN), F32),
                   jax.ShapeDtypeStruct((1, D_RNN), F32)],
        scratch_shapes=[pltpu.VMEM((8, LRU_GROUP), F32), pltpu.VMEM((1, LRU_GROUP), F32),
                        pltpu.VMEM((T, LRU_GROUP), F32), pltpu.VMEM((T, LRU_GROUP), F32), pltpu.VMEM((T, LRU_GROUP), F32)],
        compiler_params=_cparams(("parallel", "arbitrary")),
    )(proj, proj, proj, hl, hl, a_fwd, mult_fwd, dz, conv_w, conv_b, wa_bd, wx_bd, b_a, b_x, lam)
    return res


def _block_diag(w):
    w4 = w.reshape(N_LRU_GROUPS, 4, LRU_BLOCK, 1, LRU_BLOCK)
    eye = jnp.eye(4, dtype=w.dtype).reshape(1, 4, 1, 4, 1)
    return (w4 * eye).reshape(N_LRU_GROUPS, LRU_GROUP, LRU_GROUP)


def _block_diag_extract(wbd):
    w5 = wbd.reshape(N_LRU_GROUPS, 4, LRU_BLOCK, 4, LRU_BLOCK)
    return jnp.stack([w5[:, a, :, a, :] for a in range(4)], axis=1).reshape(N_LRU_BLOCKS, LRU_BLOCK, LRU_BLOCK)


def _t5_bucket(dist):
    max_exact = NUM_BUCKETS // 2
    df = jnp.maximum(dist, 1).astype(jnp.float32)
    large = max_exact + (jnp.log(df / max_exact) / math.log(MAX_DISTANCE / max_exact)
                         * (NUM_BUCKETS - max_exact)).astype(jnp.int32)
    large = jnp.minimum(large, NUM_BUCKETS - 1)
    return jnp.where(dist < max_exact, dist, large)


def _band_offsets():
    qi = jnp.arange(SPAN)[:, None]
    kj = jnp.arange(2 * SPAN)[None, :]
    return qi + SPAN - kj


def _dil_buckets():
    off = _band_offsets()
    return jnp.stack([_t5_bucket(jnp.maximum(off, 0) * dil) for _, dil in DIL_GROUPS]).astype(jnp.int32)


def _dil_bias(rel_bias, buckets):
    def body(tbl_ref, bk_ref, o_ref):
        g = pl.program_id(0)
        qi = lax.broadcasted_iota(jnp.int32, (SPAN, 2 * SPAN), 0)
        kj = lax.broadcasted_iota(jnp.int32, (SPAN, 2 * SPAN), 1)
        off = qi + SPAN - kj
        valid = (off >= 0) & (off <= SPAN)
        bk = bk_ref[0]
        for h in range(DIL_HEADS):
            acc = jnp.zeros((SPAN, 2 * SPAN), F32)
            for b in range(NUM_BUCKETS):
                acc = jnp.where(bk == b, tbl_ref[b, g * DIL_HEADS + h], acc)
            o_ref[0, h] = jnp.where(valid, acc, NEG)

    return pl.pallas_call(
        body, name="dil_bias", grid=(3,),
        in_specs=[pl.BlockSpec(memory_space=pltpu.SMEM), pl.BlockSpec((1, SPAN, 2 * SPAN), lambda g: (g, 0, 0))],
        out_specs=pl.BlockSpec((1, DIL_HEADS, SPAN, 2 * SPAN), lambda g: (g, 0, 0, 0)),
        out_shape=jax.ShapeDtypeStruct((3, DIL_HEADS, SPAN, 2 * SPAN), F32),
        compiler_params=_cparams(("parallel",)),
    )(rel_bias, buckets)


def _dil_bias_bwd(dbias, buckets):
    def body(db_ref, bk_ref, o_ref):
        lane = lax.broadcasted_iota(jnp.int32, (1, 128), 1)
        rows = [jnp.zeros((1, 128), F32) for _ in range(NUM_BUCKETS)]
        for g in range(3):
            bk = bk_ref[g]
            for h in range(DIL_HEADS):
                d = db_ref[g, h]
                for b in range(NUM_BUCKETS):
                    tot = jnp.sum(_colsum(jnp.where(bk == b, d, 0.0)), axis=1, keepdims=True)
                    rows[b] = jnp.where(lane == g * DIL_HEADS + h, tot, rows[b])
        for b in range(NUM_BUCKETS):
            o_ref[b:b + 1, :] = rows[b]

    return pl.pallas_call(
        body, name="dil_bias_bwd",
        out_shape=jax.ShapeDtypeStruct((NUM_BUCKETS, 128), F32),
        compiler_params=_cparams(),
    )(dbias, buckets)


DIL_SUBBLOCKS = (8, 4, 1)


def _dil_layout(g, S):
    dil, m = DIL_GROUPS[g][1], DIL_SUBBLOCKS[g]
    sub = SPAN * dil
    col = [(C_QKV + t * 768 + g * 256) // 128 for t in range(3)]
    return dil, m, sub, S // (sub * m), col


def _residue_rows(b, r, dil):
    return pl.ds(b * SPAN * dil + r, SPAN, stride=dil) if dil > 1 else pl.ds(b * SPAN, SPAN)


def _for_residues(dil, fn):
    if dil <= 4:
        for r in range(dil):
            fn(r)
    else:
        lax.fori_loop(0, dil, lambda r, c: (fn(r), c)[1], 0, unroll=2)


def _pair_scores(qm, k2, bias, first_cols):
    s = _dot(qm, k2, "nt") * (DIL_HEAD_DIM ** -0.5) + bias
    kj = lax.broadcasted_iota(jnp.int32, s.shape, 1)
    return jnp.where(kj < first_cols, NEG, s)


def _dilated_fwd(proj, bias, g, *, S):
    dil, m, sub, nc, (qc, kc, vc) = _dil_layout(g, S)
    R = sub * m
    cur = lambda cb: pl.BlockSpec((R, 128), lambda p, i: (i, cb + p))
    prv = lambda cb: pl.BlockSpec((sub, 128), lambda p, i: (jnp.maximum(i * m - 1, 0), cb + p))
    out = pl.BlockSpec((R, 128), lambda p, i: (i, p))

    def body(q_ref, kp_ref, kc_ref, vp_ref, vc_ref, b_ref, o_ref, lse_ref):
        lane = lax.broadcasted_iota(jnp.int32, (SPAN, 128), 1)
        sels = (lane < DIL_HEAD_DIM, lane >= DIL_HEAD_DIM)
        for b in range(m):
            first_cols = jnp.where(pl.program_id(1) == 0, SPAN, 0) if b == 0 else 0

            def one(r, b=b, first_cols=first_cols):
                rows = _residue_rows(b, r, dil)
                before = (kc_ref, vc_ref, _residue_rows(b - 1, r, dil)) if b else (kp_ref, vp_ref, _residue_rows(0, r, dil))
                q2 = q_ref[rows, :]
                k2 = _mx(jnp.concatenate([before[0][before[2], :], kc_ref[rows, :]], axis=0))
                v2 = _mx(jnp.concatenate([before[1][before[2], :], vc_ref[rows, :]], axis=0))
                qq = jnp.concatenate([jnp.where(sels[0], q2, 0.0), jnp.where(sels[1], q2, 0.0)], axis=0)
                s = _pair_scores(qq, k2, b_ref[0, 0], first_cols)
                mx = jnp.max(s, axis=-1, keepdims=True)
                p = jnp.exp(s - mx)
                den = jnp.sum(p, axis=-1, keepdims=True)
                o = _dot(p, v2) / den
                st = mx + jnp.log(den)
                o_ref[rows, :] = jnp.where(sels[0], o[0:SPAN], o[SPAN:2 * SPAN])
                lse_ref[rows, :] = jnp.where(lane == 0, st[0:SPAN], jnp.where(lane == 1, st[SPAN:2 * SPAN], 0.0))

            _for_residues(dil, one)

    return pl.pallas_call(
        body, name=f"dil_fwd{g}", grid=(2, nc),
        in_specs=[cur(qc), prv(kc), cur(kc), prv(vc), cur(vc),
                  pl.BlockSpec((1, 1, 2 * SPAN, 2 * SPAN), lambda p, i: (g, p, 0, 0))],
        out_specs=[out, out],
        out_shape=[jax.ShapeDtypeStruct((S, 256), F32), jax.ShapeDtypeStruct((S, 256), F32)],
        compiler_params=_cparams(("parallel", "parallel")),
    )(proj, proj, proj, proj, proj, bias.reshape(3, 2, 2 * SPAN, 2 * SPAN))


def _dilated_bwd(proj, do, lse, delta, bias, g, *, S, into=None):
    dil, m, sub, nc, (qc, kc, vc) = _dil_layout(g, S)
    R = sub * m
    cl = lambda i: jnp.minimum(i, nc - 1)
    cur = lambda cb: pl.BlockSpec((R, 128), lambda p, i: (cl(i), cb + p))
    prv = lambda cb: pl.BlockSpec((sub, 128), lambda p, i: (jnp.maximum(cl(i) * m - 1, 0), cb + p))
    q_out = pl.BlockSpec((R, 128), lambda p, i: (cl(i), 2 * g + p))
    kv_out = pl.BlockSpec((R, 128), lambda p, i: (jnp.maximum(i - 1, 0), 2 * g + p))
    scale = DIL_HEAD_DIM ** -0.5
    n_into = 0 if into is None else 3

    def body(q_ref, kp_ref, kc_ref, vp_ref, vc_ref, do_ref, lse_ref, dl_ref, b_ref, *rest):
        dq_ref, dk_ref, dv_ref, db_ref, dq_s, kc_s, vc_s, kp_s, vp_s, kcar, vcar = rest[n_into:]
        i = pl.program_id(1)

        @pl.when(i == 0)
        def _():
            kcar[...] = jnp.zeros_like(kcar)
            vcar[...] = jnp.zeros_like(vcar)
            db_ref[...] = jnp.zeros_like(db_ref)

        @pl.when(i < nc)
        def _():
            lane = lax.broadcasted_iota(jnp.int32, (SPAN, 128), 1)
            sels = (lane < DIL_HEAD_DIM, lane >= DIL_HEAD_DIM)
            for b in range(m):
                first_cols = jnp.where(i == 0, SPAN, 0) if b == 0 else 0

                def one(r, b=b, first_cols=first_cols):
                    rows = _residue_rows(b, r, dil)
                    rows_before = _residue_rows(b - 1 if b else 0, r, dil)
                    k_before, v_before = (kc_ref, vc_ref) if b else (kp_ref, vp_ref)
                    q2, do2 = q_ref[rows, :], do_ref[rows, :]
                    k2 = _mx(jnp.concatenate([k_before[rows_before, :], kc_ref[rows, :]], axis=0))
                    v2 = _mx(jnp.concatenate([v_before[rows_before, :], vc_ref[rows, :]], axis=0))
                    lse_t, dl_t = lse_ref[rows, :], dl_ref[rows, :]
                    qq = _mx(jnp.concatenate([jnp.where(sels[0], q2, 0.0), jnp.where(sels[1], q2, 0.0)], axis=0))
                    dd = _mx(jnp.concatenate([jnp.where(sels[0], do2, 0.0), jnp.where(sels[1], do2, 0.0)], axis=0))
                    lse2 = jnp.concatenate([lse_t[:, 0:1], lse_t[:, 1:2]], axis=0)
                    dl2 = jnp.concatenate([dl_t[:, 0:1], dl_t[:, 1:2]], axis=0)
                    p = jnp.exp(_pair_scores(qq, k2, b_ref[0, 0], first_cols) - lse2)
                    ds = p * (_dot(dd, v2, "nt") - dl2)
                    db_ref[0] += ds
                    dqq = _dot(ds, k2) * scale
                    dq2 = jnp.where(sels[0], dqq[0:SPAN], dqq[SPAN:2 * SPAN])
                    dk2 = _dot(ds, qq, "tn") * scale
                    dv2 = _dot(p, dd, "tn")
                    dq_s[rows, :] = dq2
                    kc_s[rows, :] = dk2[SPAN:2 * SPAN]
                    vc_s[rows, :] = dv2[SPAN:2 * SPAN]
                    if b:
                        kc_s[rows_before, :] += dk2[0:SPAN]
                        vc_s[rows_before, :] += dv2[0:SPAN]
                    else:
                        kp_s[rows_before, :] = dk2[0:SPAN]
                        vp_s[rows_before, :] = dv2[0:SPAN]

                _for_residues(dil, one)
            dq_ref[...] = dq_s[...].astype(dq_ref.dtype)
            last = pl.ds((m - 1) * sub, sub)
            kcar[last, :] += kp_s[...]
            vcar[last, :] += vp_s[...]
            dk_ref[...] = kcar[...].astype(dk_ref.dtype)
            dv_ref[...] = vcar[...].astype(dv_ref.dtype)
            kcar[...] = kc_s[...]
            vcar[...] = vc_s[...]

        @pl.when(i == nc)
        def _():
            dk_ref[...] = kcar[...].astype(dk_ref.dtype)
            dv_ref[...] = vcar[...].astype(dv_ref.dtype)

    stat = pl.BlockSpec((R, 128), lambda p, i: (cl(i), p))
    big = jax.ShapeDtypeStruct((S, len(DIL_GROUPS) * 256), MXU_DTYPE)
    return pl.pallas_call(
        body, name=f"dil_bwd{g}", grid=(2, nc + 1),
        in_specs=[cur(qc), prv(kc), cur(kc), prv(vc), cur(vc), stat, stat, stat,
                  pl.BlockSpec((1, 1, 2 * SPAN, 2 * SPAN), lambda p, i: (g, p, 0, 0))]
        + [pl.BlockSpec(memory_space=pl.ANY)] * n_into,
        out_specs=[q_out, kv_out, kv_out, pl.BlockSpec((1, 2 * SPAN, 2 * SPAN), lambda p, i: (p, 0, 0))],
        out_shape=[big, big, big, jax.ShapeDtypeStruct((2, 2 * SPAN, 2 * SPAN), F32)],
        input_output_aliases={9 + j: j for j in range(n_into)},
        scratch_shapes=[pltpu.VMEM((R, 128), F32)] * 3 + [pltpu.VMEM((sub, 128), F32)] * 2 + [pltpu.VMEM((R, 128), F32)] * 2,
        compiler_params=_cparams(("parallel", "arbitrary")),
    )(proj, proj, proj, proj, proj, do, lse, delta, bias.reshape(3, 2, 2 * SPAN, 2 * SPAN), *(into or ()))


def _dilated_merge(os_, lses, *, S, bt=512):
    tile = pl.BlockSpec((bt, 128), lambda i, p: (i, p))

    def body(o0, o1, o2, l0, l1, l2, o_ref, om_ref, lse_ref):
        lane = lax.broadcasted_iota(jnp.int32, (bt, 128), 1)
        lo = lane < DIL_HEAD_DIM
        ls = [l0[...], l1[...], l2[...]]
        ws, stat = [], jnp.zeros((bt, 128), F32)
        for e in range(2):
            a = [l[:, e:e + 1] for l in ls]
            m = jnp.maximum(jnp.maximum(a[0], a[1]), a[2])
            ex = [jnp.exp(v - m) for v in a]
            tot = ex[0] + ex[1] + ex[2]
            ws.append([v / tot for v in ex])
            stat = jnp.where(lane == e, m + jnp.log(tot), stat)
        acc = jnp.zeros((bt, 128), F32)
        for gi, o in enumerate((o0, o1, o2)):
            acc = acc + jnp.where(lo, ws[0][gi], ws[1][gi]) * o[...]
        o_ref[...] = acc
        om_ref[...] = _mx(acc)
        lse_ref[...] = stat

    return pl.pallas_call(
        body, name="dil_merge", grid=(S // bt, 2),
        in_specs=[tile] * 6, out_specs=[tile, tile, tile],
        out_shape=[jax.ShapeDtypeStruct((S, 256), F32), jax.ShapeDtypeStruct((S, 256), MXU_DTYPE),
                   jax.ShapeDtypeStruct((S, 256), F32)],
        compiler_params=_cparams(("parallel", "parallel")),
    )(*os_, *lses)


def _dilated_delta(do, o, *, S, bt=512):
    tile = pl.BlockSpec((bt, 128), lambda i, p: (i, p))

    def body(do_ref, o_ref, d_ref):
        lane = lax.broadcasted_iota(jnp.int32, (bt, 128), 1)
        prod = do_ref[...] * o_ref[...]
        d0 = jnp.sum(jnp.where(lane < DIL_HEAD_DIM, prod, 0.0), axis=-1, keepdims=True)
        d1 = jnp.sum(jnp.where(lane >= DIL_HEAD_DIM, prod, 0.0), axis=-1, keepdims=True)
        d_ref[...] = jnp.where(lane == 0, d0, jnp.where(lane == 1, d1, 0.0))

    return pl.pallas_call(
        body, name="dil_delta", grid=(S // bt, 2), in_specs=[tile, tile], out_specs=tile,
        out_shape=jax.ShapeDtypeStruct((S, 256), F32), compiler_params=_cparams(("parallel", "parallel")),
    )(do, o)


MEM_T = 2048
QM_BLK = C_QM // MEM_HEAD_DIM


def _mem_attn_fwd(proj, kv, *, S):
    scale = MEM_HEAD_DIM ** -0.5

    def body(q_ref, k_ref, v_ref, o_ref, om_ref, lse_ref):
        s = _dot(q_ref[...], k_ref[...], "nt") * scale
        m = jnp.max(s, axis=-1, keepdims=True)
        p = jnp.exp(s - m)
        den = jnp.sum(p, axis=-1, keepdims=True)
        o = _dot(p, v_ref[...]) / den
        o_ref[...] = o
        om_ref[...] = _mx(o)
        lse_ref[0] = m + jnp.log(den)

    return pl.pallas_call(
        body, name="mem_attn_fwd", grid=(S // MEM_T, MEM_HEADS),
        in_specs=[pl.BlockSpec((MEM_T, MEM_HEAD_DIM), lambda i, h: (i, QM_BLK + h)),
                  pl.BlockSpec((N_MEM, MEM_HEAD_DIM), lambda i, h: (0, h)),
                  pl.BlockSpec((N_MEM, MEM_HEAD_DIM), lambda i, h: (0, MEM_HEADS + h))],
        out_specs=[pl.BlockSpec((MEM_T, MEM_HEAD_DIM), lambda i, h: (i, h)),
                   pl.BlockSpec((MEM_T, MEM_HEAD_DIM), lambda i, h: (i, h)),
                   pl.BlockSpec((1, MEM_T, 1), lambda i, h: (h, i, 0))],
        out_shape=[jax.ShapeDtypeStruct((S, MEM_WIDTH), F32), jax.ShapeDtypeStruct((S, MEM_WIDTH), MXU_DTYPE),
                   jax.ShapeDtypeStruct((MEM_HEADS, S, 1), F32)],
        compiler_params=_cparams(("parallel", "parallel")),
    )(proj, kv, kv)


def _mem_attn_bwd(proj, kv, om, lse, dom, *, S):
    scale = MEM_HEAD_DIM ** -0.5

    def body(q_ref, k_ref, v_ref, o_ref, lse_ref, do_ref, dq_ref, dk_ref, dv_ref):
        @pl.when(pl.program_id(1) == 0)
        def _():
            dk_ref[...] = jnp.zeros_like(dk_ref)
            dv_ref[...] = jnp.zeros_like(dv_ref)

        qv, kv_, vv, dov = q_ref[...], k_ref[...], v_ref[...], do_ref[...]
        p = jnp.exp(_dot(qv, kv_, "nt") * scale - lse_ref[0])
        delta = jnp.sum(dov * o_ref[...], axis=-1, keepdims=True)
        ds = p * (_dot(dov, vv, "nt") - delta)
        dq_ref[...] = (_dot(ds, kv_) * scale).astype(dq_ref.dtype)
        dk_ref[...] += _dot(ds, qv, "tn") * scale
        dv_ref[...] += _dot(p, dov, "tn")

    tile = pl.BlockSpec((MEM_T, MEM_HEAD_DIM), lambda h, i: (i, h))
    kvo = pl.BlockSpec((N_MEM, MEM_HEAD_DIM), lambda h, i: (0, h))
    return pl.pallas_call(
        body, name="mem_attn_bwd", grid=(MEM_HEADS, S // MEM_T),
        in_specs=[pl.BlockSpec((MEM_T, MEM_HEAD_DIM), lambda h, i: (i, QM_BLK + h)),
                  pl.BlockSpec((N_MEM, MEM_HEAD_DIM), lambda h, i: (0, h)),
                  pl.BlockSpec((N_MEM, MEM_HEAD_DIM), lambda h, i: (0, MEM_HEADS + h)),
                  tile, pl.BlockSpec((1, MEM_T, 1), lambda h, i: (h, i, 0)), tile],
        out_specs=[tile, kvo, kvo],
        out_shape=[jax.ShapeDtypeStruct((S, MEM_WIDTH), MXU_DTYPE), jax.ShapeDtypeStruct((N_MEM, MEM_WIDTH), F32),
                   jax.ShapeDtypeStruct((N_MEM, MEM_WIDTH), F32)],
        compiler_params=_cparams(("parallel", "arbitrary")),
    )(proj, kv, kv, om, lse, dom)


MIX_BM = 1024
MIX_BN = 256
GATES_BLK = C_GATES // MIX_BN


def _mix_specs(j_outer):
    ix = (lambda f: (lambda j, i: f(i, j))) if j_outer else (lambda f: f)
    act = lambda width: pl.BlockSpec((MIX_BM, width), ix(lambda i, j: (i, 0)))
    wgt = lambda width: pl.BlockSpec((width, MIX_BN), ix(lambda i, j: (0, j)))
    gate = lambda b: pl.BlockSpec((MIX_BM, MIX_BN), ix(lambda i, j: (i, GATES_BLK + 4 * b + j)))
    bias = lambda b: pl.BlockSpec((1, MIX_BN), ix(lambda i, j: (0, 4 * b + j)))
    tile = pl.BlockSpec((MIX_BM, MIX_BN), ix(lambda i, j: (i, j)))
    return act, wgt, gate, bias, tile


def _mix_fwd(z_lru, o_dil, om, w_lru, w_dil, w_mem, proj, b_gate, *, S):
    act, wgt, gate, bias, tile = _mix_specs(False)

    def body(zl, od, mo, wl, wd, wm, g0, g1, g2, b0, b1, b2, o_ref):
        acc = jax.nn.sigmoid(g0[...] + b0[...]) * _dot(zl[...], wl[...])
        acc += jax.nn.sigmoid(g1[...] + b1[...]) * _dot(od[...], wd[...])
        acc += jax.nn.sigmoid(g2[...] + b2[...]) * _dot(mo[...], wm[...])
        o_ref[...] = acc.astype(o_ref.dtype)

    return pl.pallas_call(
        body, name="mix_fwd", grid=(S // MIX_BM, D_MODEL // MIX_BN),
        in_specs=[act(D_RNN), act(256), act(MEM_WIDTH), wgt(D_RNN), wgt(256), wgt(MEM_WIDTH),
                  gate(0), gate(1), gate(2), bias(0), bias(1), bias(2)],
        out_specs=tile, out_shape=jax.ShapeDtypeStruct((S, D_MODEL), MXU_DTYPE),
        compiler_params=_cparams(("parallel", "parallel")),
    )(z_lru, o_dil, om, w_lru, w_dil, w_mem, proj, proj, proj, b_gate, b_gate, b_gate)


def _mix_bwd(dmerged, z_lru, o_dil, om, w_lru, w_dil, w_mem, proj, b_gate, *, S):
    act, wgt, gate, bias, tile = _mix_specs(True)

    def body(dm, zl, od, mo, wl, wd, wm, g0, g1, g2, b0, b1, b2,
             dg0, dg1, dg2, dy0, dy1, dy2, db0, db1, db2):
        @pl.when(pl.program_id(1) == 0)
        def _():
            for r in (db0, db1, db2):
                r[...] = jnp.zeros_like(r)

        dmv = dm[...]
        for act_ref, w_ref, g_ref, b_ref, dg_ref, dy_ref, db_ref in (
                (zl, wl, g0, b0, dg0, dy0, db0), (od, wd, g1, b1, dg1, dy1, db1), (mo, wm, g2, b2, dg2, dy2, db2)):
            y = _dot(act_ref[...], w_ref[...])
            gt = jax.nn.sigmoid(g_ref[...] + b_ref[...])
            dgate = dmv * y * gt * (1.0 - gt)
            dg_ref[...] = dgate.astype(dg_ref.dtype)
            dy_ref[...] = (dmv * gt).astype(dy_ref.dtype)
            db_ref[...] += _colsum(dgate)

    big = jax.ShapeDtypeStruct((S, D_MODEL), MXU_DTYPE)
    vec = jax.ShapeDtypeStruct((1, D_MODEL), F32)
    vspec = pl.BlockSpec((1, MIX_BN), lambda j, i: (0, j))
    return pl.pallas_call(
        body, name="mix_bwd", grid=(D_MODEL // MIX_BN, S // MIX_BM),
        in_specs=[tile, act(D_RNN), act(256), act(MEM_WIDTH), wgt(D_RNN), wgt(256), wgt(MEM_WIDTH),
                  gate(0), gate(1), gate(2), bias(0), bias(1), bias(2)],
        out_specs=[tile] * 6 + [vspec] * 3, out_shape=[big] * 6 + [vec] * 3,
        compiler_params=_cparams(("parallel", "arbitrary")),
    )(dmerged, z_lru, o_dil, om, w_lru, w_dil, w_mem, proj, proj, proj, b_gate, b_gate, b_gate)


def _adamw_math(w, g, m, v):
    m = ADAM_B1 * m + (1.0 - ADAM_B1) * g
    v = ADAM_B2 * v + (1.0 - ADAM_B2) * (g * g)
    m_hat = m / (1.0 - ADAM_B1 ** ADAM_STEP)
    v_hat = v / (1.0 - ADAM_B2 ** ADAM_STEP)
    delta = -ADAM_LR * (m_hat / (jnp.sqrt(v_hat) + ADAM_EPS) + ADAM_WD * w)
    return delta, m, v


def _adamw_landed(w, own, land, m, v, *, name, col_blk=0, prev=None):
    R = w.shape[0]
    n_parts, C = land.shape[0], land.shape[2]
    br = next(d for d in (256, 464, 128) if R % d == 0)
    tile = pl.BlockSpec((br, C), lambda i: (i, col_blk))
    part = pl.BlockSpec((br, C), lambda i: (i, 0))
    n_prev = 0 if prev is None else 4

    def body(w_ref, o_ref, l_ref, m_ref, v_ref, *rest):
        g_ref, d_ref, nm_ref, nv_ref = rest[n_prev:]
        g = o_ref[...].astype(F32)
        for p in range(n_parts):
            g = g + l_ref[p].astype(F32)
        d, nm, nv = _adamw_math(w_ref[...], g, m_ref[...], v_ref[...])
        g_ref[...] = g
        d_ref[...] = d
        nm_ref[...] = nm
        nv_ref[...] = nv

    return pl.pallas_call(
        body, name=name, grid=(R // br,),
        in_specs=[tile, part, pl.BlockSpec((n_parts, br, C), lambda i: (0, i, 0)), tile, tile]
        + [pl.BlockSpec(memory_space=pl.ANY)] * n_prev,
        out_specs=[tile] * 4, out_shape=[jax.ShapeDtypeStruct(w.shape, F32)] * 4,
        input_output_aliases={5 + j: j for j in range(n_prev)},
        compiler_params=_cparams(("parallel",)),
    )(w, own, land, m, v, *(prev or ()))


def _adamw_plain(w, g, m, v, *, name):
    def body(w_ref, g_ref, m_ref, v_ref, d_ref, nm_ref, nv_ref):
        d, nm, nv = _adamw_math(w_ref[...], g_ref[...], m_ref[...], v_ref[...])
        d_ref[...] = d
        nm_ref[...] = nm
        nv_ref[...] = nv

    return pl.pallas_call(
        body, name=name, out_shape=[jax.ShapeDtypeStruct(w.shape, F32)] * 3, compiler_params=_cparams(),
    )(w, g, m, v)


def _my_pos():
    return lax.axis_index("x"), lax.axis_index("y"), lax.axis_index("c")


def _dev_index(p):
    return 4 * p[0] + 2 * p[1] + p[2]


def _all_gather(shards):
    n = len(shards)
    hbm = pl.BlockSpec(memory_space=pl.ANY)

    def body(*refs):
        ins, outs = refs[:n], refs[n:2 * n]
        send_sems, recv_sems, local_sems = refs[2 * n:]
        x, y, c = _my_pos()
        me, sibling = (x, y, c), (x, y, 1 - c)
        chips = [(1 - x, y), (x, 1 - y), (1 - x, 1 - y)]

        def copy(a, k, block, to, src=None):
            dst = outs[a].at[_dev_index(block)]
            return pltpu.make_async_remote_copy(
                src_ref=dst if src is None else src, dst_ref=dst,
                send_sem=send_sems.at[a, k], recv_sem=recv_sems.at[a, k], device_id=to, device_id_type=MESH)

        mine = [pltpu.make_async_copy(ins[a], outs[a].at[_dev_index(me)], local_sems.at[a]) for a in range(n)]
        for cp in mine:
            cp.start()
        first = []
        for a in range(n):
            first.append(copy(a, 0, me, sibling, src=ins[a]))
            first += [copy(a, 1 + j, me, (*chip, c), src=ins[a]) for j, chip in enumerate(chips)]
        for cp in first:
            cp.start()
        passed = []
        for j, chip in enumerate(chips):
            for a in range(n):
                copy(a, 1 + j, (*chip, c), me).wait_recv()
                fwd = copy(a, 4 + j, (*chip, c), sibling)
                fwd.start()
                passed.append(fwd)
        for a in range(n):
            copy(a, 0, sibling, me).wait_recv()
        for j, chip in enumerate(chips):
            for a in range(n):
                copy(a, 4 + j, (*chip, 1 - c), me).wait_recv()
        for cp in first + passed:
            cp.wait_send()
        for cp in mine:
            cp.wait()

    return pl.pallas_call(
        body, name="all_gather_weights",
        in_specs=[hbm] * n, out_specs=[hbm] * n,
        out_shape=[jax.ShapeDtypeStruct((N_DEV,) + s.shape, s.dtype) for s in shards],
        scratch_shapes=[pltpu.SemaphoreType.DMA((n, 7)), pltpu.SemaphoreType.DMA((n, 7)), pltpu.SemaphoreType.DMA((n,))],
        compiler_params=pltpu.CompilerParams(has_side_effects=True),
    )(*shards)


def _peers(me):
    x, y, c = me
    out = []
    for k in range(1, 8):
        fx, fy, fc = (k >> 2) & 1, (k >> 1) & 1, k & 1
        out.append((k - 1, (1 - x if fx else x, 1 - y if fy else y, 1 - c if fc else c)))
    return out


HBM_SPEC = pl.BlockSpec(memory_space=pltpu.HBM)
SEM_SPEC = pl.BlockSpec(memory_space=pltpu.SEMAPHORE)
DATAFLOW_EFFECT = pltpu.SideEffectType.DATAFLOW_SIDE_EFFECTING


def _gather_refs(src, land, me, peer, k):
    return src, land.at[_dev_index(me)]


def _scatter_refs(src, land, me, peer, k):
    return src.at[_dev_index(peer)], land.at[k]


def _push_start(srcs, land_shapes, refs_of, name, after=()):
    n, n_after = len(srcs), len(after)

    def body(*refs):
        ins, lands = refs[:n], refs[n:2 * n]
        send_sems, recv_sems, token = refs[2 * n + n_after], refs[2 * n + n_after + 1], refs[-1]
        me = _my_pos()
        for k, peer in _peers(me):
            for a in range(n):
                src, dst = refs_of(ins[a], lands[a], me, peer, k)
                pltpu.make_async_remote_copy(src_ref=src, dst_ref=dst, send_sem=send_sems.at[7 * a + k],
                                             recv_sem=recv_sems.at[7 * a + k], device_id=peer, device_id_type=MESH).start()
        token[...] = jnp.zeros_like(token)

    lands = [lax.empty(shp, s.dtype) for shp, s in zip(land_shapes, srcs)]
    hbm = lambda a: pltpu.with_memory_space_constraint(a, pltpu.HBM)
    res = pl.pallas_call(
        body, name=name,
        out_shape=(pltpu.SemaphoreType.DMA((7 * n,)), pltpu.SemaphoreType.DMA((7 * n,)),
                   *[pltpu.HBM(s.shape, s.dtype) for s in srcs], *[pltpu.HBM(l.shape, l.dtype) for l in lands],
                   jax.ShapeDtypeStruct((8, 128), F32)),
        in_specs=[HBM_SPEC] * (2 * n) + [pl.BlockSpec(memory_space=pl.ANY)] * n_after,
        out_specs=(SEM_SPEC, SEM_SPEC, *[HBM_SPEC] * (2 * n), pl.BlockSpec(memory_space=pltpu.VMEM)),
        input_output_aliases={i: 2 + i for i in range(2 * n)},
        compiler_params=pltpu.CompilerParams(has_side_effects=DATAFLOW_EFFECT),
    )(*[hbm(s) for s in srcs], *[hbm(l) for l in lands], *after)
    return dict(sems=(res[0], res[1]), srcs=list(res[2:2 + n]), lands=list(res[2 + n:2 + 2 * n]), token=res[-1], n=n,
                refs_of=refs_of, name=name)


def _push_wait(started, after):
    n, refs_of = started["n"], started["refs_of"]
    after = list(after) if isinstance(after, (list, tuple)) else [after]

    def body(*refs):
        ins, lands = refs[:n], refs[n:2 * n]
        send_sems, recv_sems = refs[2 * n], refs[2 * n + 1]
        me = _my_pos()
        for k, peer in _peers(me):
            for a in range(n):
                src, dst = refs_of(ins[a], lands[a], me, peer, k)
                cp = pltpu.make_async_remote_copy(src_ref=src, dst_ref=dst, send_sem=send_sems.at[7 * a + k],
                                                  recv_sem=recv_sems.at[7 * a + k], device_id=peer, device_id_type=MESH)
                cp.wait_send()
                cp.wait_recv()

    arrs = started["srcs"] + started["lands"]
    res = pl.pallas_call(
        body, name=started["name"].replace("start", "wait"),
        out_shape=tuple(pltpu.HBM(a.shape, a.dtype) for a in arrs),
        in_specs=[HBM_SPEC] * (2 * n) + [SEM_SPEC, SEM_SPEC] + [pl.BlockSpec(memory_space=pl.ANY)] * len(after),
        out_specs=tuple([HBM_SPEC] * (2 * n)),
        input_output_aliases={i: i for i in range(2 * n)},
        compiler_params=pltpu.CompilerParams(has_side_effects=DATAFLOW_EFFECT),
    )(*arrs, *started["sems"], *after)
    return list(res[n:2 * n])


def _sum_slots(slots):
    def body(in_ref, out_ref):
        acc = in_ref[0]
        for d in range(1, N_DEV):
            acc = acc + in_ref[d]
        out_ref[...] = acc

    return pl.pallas_call(body, name="sum_small", out_shape=jax.ShapeDtypeStruct(slots.shape[1:], F32),
                          compiler_params=_cparams())(slots)


def _adamw_many(ws, gs, ms, vs):
    n = len(ws)

    def body(*refs):
        for i in range(n):
            w_ref, g_ref, m_ref, v_ref = (refs[j * n + i] for j in range(4))
            d_, nm, nv = _adamw_math(w_ref[...], g_ref[...], m_ref[...], v_ref[...])
            for j, val in enumerate((d_, nm, nv)):
                refs[(4 + j) * n + i][...] = val

    res = pl.pallas_call(body, name="adamw_small", out_shape=[jax.ShapeDtypeStruct(w_.shape, F32) for w_ in ws] * 3,
                         compiler_params=_cparams())(*ws, *gs, *ms, *vs)
    return [(res[i], res[n + i], res[2 * n + i]) for i in range(n)]


def _local_step(x, mem, tgt, W, P, late_weights, send_grads, reduce_small, tie0):
    S = x.shape[0]
    W = dict(W)
    h = _rmsnorm_fwd(x, P["g_mix"] + tie0, rows=S, name="norm_mix")
    proj = _matmul(h, W["w_in_t"], M=S, N=D_IN, K=D_MODEL, mode="nt", bm=512, bn=D_IN // 2, bk=D_MODEL, name="mm_in",
                   j_outer=True)

    wa_bd, wx_bd = _mx(_block_diag(P["w_rg_a"])), _mx(_block_diag(P["w_rg_x"]))
    lru_args = (W["conv_w"], P["conv_b"].reshape(1, -1), wa_bd, wx_bd, P["b_rg_a"].reshape(1, -1),
                P["b_rg_x"].reshape(1, -1), P["lru_lambda"].reshape(1, -1))
    hl, z_lru, a_lru, mult_lru = _lru_fwd(proj, *lru_args, S=S)

    buckets = _dil_buckets()
    bias = _dil_bias(P["rel_bias"], buckets)
    group_out = [_dilated_fwd(proj, bias, g, S=S) for g in range(len(DIL_GROUPS))]
    o_dil, o_dil_m, lse_dil = _dilated_merge([o for o, _ in group_out], [l for _, l in group_out], S=S)

    W.update(late_weights("branch", [o_dil, z_lru]))
    mem_n = _rmsnorm_fwd(mem, P["g_mem"], rows=N_MEM, name="norm_mem")
    kv = _matmul(mem_n, W["w_mem_kv"], M=N_MEM, N=2 * MEM_WIDTH, K=D_MODEL, mode="nn", bm=N_MEM, bn=512, bk=D_MODEL,
                 name="mm_kv")
    om, om_m, lse_mem = _mem_attn_fwd(proj, kv, S=S)
    b_gate = P["b_gate"].reshape(1, -1)
    merged = _mix_fwd(z_lru, o_dil_m, om_m, W["w_lru_out"], W["w_dil_out"], W["w_mem_out"], proj, b_gate, S=S)
    x1 = _matmul(merged, W["w_out"], M=S, N=D_MODEL, K=D_MODEL, mode="nn", bm=512, bn=D_MODEL, bk=D_MODEL, name="mm_out",
                 epilogue=lambda acc, r: (r + acc,), extras=[(x, (0, 0))])
    hm = _rmsnorm_fwd(x1, P["g_mlp"], rows=S, name="norm_mlp")
    W.update(late_weights("mlp", [hm]))

    def relu2(acc):
        rl = jnp.maximum(acc, 0.0)
        return rl * rl, rl

    act, relu_u = _matmul(hm, W["w_mlp_in_t"], M=S, N=D_FF, K=D_MODEL, mode="nt", bm=1024, bn=1024, bk=D_MODEL,
                          name="mm_mlp_in", out_dtypes=(MXU_DTYPE, MXU_DTYPE), epilogue=relu2, j_outer=True)
    x2 = _matmul(act, W["w_mlp_out"], M=S, N=D_MODEL, K=D_FF, mode="nn", bm=512, bn=D_MODEL, bk=D_FF, name="mm_mlp_out",
                 epilogue=lambda acc, r: (r + acc,), extras=[(x1, (0, 0))])
    loss, dx2, dx2_m, dg_final = _loss_head(x2, P["g_final"], tgt, rows=S)

    G, Gs = {}, {}
    Gs["g_final"] = dg_final
    dw = dict(mode="tn", K=S, bk=S, out_dtypes=(MXU_DTYPE,))
    G["w_mlp_out"] = _matmul(act, dx2_m, M=D_FF, N=D_MODEL, bm=512, bn=D_MODEL, name="mm_dw_mlp_out",
                             parts=("rows", D_FF // N_DEV), **dw)
    du = _matmul(dx2_m, W["w_mlp_out"], M=S, N=D_FF, K=D_MODEL, mode="nt", bm=1024, bn=1024, bk=D_MODEL, name="mm_du",
                 out_dtypes=(MXU_DTYPE,), epilogue=lambda acc, rl: (acc * (2.0 * rl.astype(F32)),),
                 extras=[(relu_u, (0, 0))], j_outer=True)
    G["w_mlp_in"] = _matmul(hm, du, M=D_MODEL, N=D_FF, bm=D_MODEL, bn=512, name="mm_dw_mlp_in",
                            parts=("cols", D_FF // N_DEV), **dw)
    tie1 = send_grads({n: G.pop(n) for n in ("w_mlp_out", "w_mlp_in")})
    dhm = _matmul(du, W["w_mlp_in_t"], M=S, N=D_MODEL, K=D_FF, mode="nn", bm=512, bn=D_MODEL, bk=D_FF, name="mm_dhm",
                  deps=[tie1])
    dx1, dx1_m, Gs["g_mlp"] = _rmsnorm_bwd(x1, P["g_mlp"], dhm, dx2, rows=S, name="norm_mlp_bwd",
                                           dx_dtypes=(F32, MXU_DTYPE))
    G["w_out"] = _matmul(merged, dx1_m, M=D_MODEL, N=D_MODEL, bm=512, bn=D_MODEL, name="mm_dw_out",
                         parts=("rows", D_MODEL // N_DEV), **dw)
    dmerged = _matmul(dx1_m, W["w_out"], M=S, N=D_MODEL, K=D_MODEL, mode="nt", bm=512, bn=D_MODEL, bk=D_MODEL, name="mm_dmerged")
    (dg0, dg1, dg2, dy_lru, dy_dil, dy_mem, db0, db1, db2) = _mix_bwd(
        dmerged, z_lru, o_dil_m, om_m, W["w_lru_out"], W["w_dil_out"], W["w_mem_out"], proj, b_gate, S=S)
    Gs["b_gate0"], Gs["b_gate1"], Gs["b_gate2"] = db0, db1, db2

    G["w_mem_out"] = _matmul(om_m, dy_mem, M=MEM_WIDTH, N=D_MODEL, bm=MEM_WIDTH, bn=D_MODEL, name="mm_dw_mem_out",
                             parts=("cols", D_MODEL // N_DEV), **dw)
    dom = _matmul(dy_mem, W["w_mem_out"], M=S, N=MEM_WIDTH, K=D_MODEL, mode="nt", bm=512, bn=MEM_WIDTH, bk=D_MODEL,
                  name="mm_dom")
    dqm, dk_mem, dv_mem = _mem_attn_bwd(proj, kv, om, lse_mem, dom, S=S)
    dkv = jnp.concatenate([dk_mem, dv_mem], axis=1)
    G["w_mem_kv"] = _matmul(mem_n, dkv, M=D_MODEL, N=2 * MEM_WIDTH, K=N_MEM, mode="tn", bm=D_MODEL, bn=2 * MEM_WIDTH,
                            bk=N_MEM, name="mm_dw_kv", out_dtypes=(MXU_DTYPE,), parts=("rows", D_MODEL // N_DEV))
    dmem_n = _matmul(dkv, W["w_mem_kv"], M=N_MEM, N=D_MODEL, K=2 * MEM_WIDTH, mode="nt", bm=N_MEM, bn=D_MODEL,
                     bk=2 * MEM_WIDTH, name="mm_dmem")
    (Gs["g_mem"],) = _rmsnorm_bwd(mem, P["g_mem"], dmem_n, None, rows=N_MEM, name="norm_mem_bwd", dx_dtypes=())

    G["w_dil_out"] = _matmul(o_dil_m, dy_dil, M=256, N=D_MODEL, bm=256, bn=D_MODEL, name="mm_dw_dil_out",
                             parts=("cols", D_MODEL // N_DEV), **dw)
    do_dil = _matmul(dy_dil, W["w_dil_out"], M=S, N=256, K=D_MODEL, mode="nt", bm=512, bn=256, bk=D_MODEL, name="mm_do_dil")
    G["w_lru_out"] = _matmul(z_lru, dy_lru, M=D_RNN, N=D_MODEL, bm=D_RNN, bn=D_MODEL, name="mm_dw_lru_out",
                             parts=("cols", D_MODEL // N_DEV), **dw)
    dz = _matmul(dy_lru, W["w_lru_out"], M=S, N=D_RNN, K=D_MODEL, mode="nt", bm=512, bn=D_RNN, bk=D_MODEL, name="mm_dz_lru")
    tie2 = send_grads({n: G.pop(n) for n in ("w_out", "w_mem_out", "w_mem_kv", "w_dil_out", "w_lru_out")})
    bias = bias + tie2[0, 0]
    delta = _dilated_delta(do_dil, o_dil, S=S)
    dqkv, dbias = None, []
    for g in range(len(DIL_GROUPS)):
        *dqkv, db_g = _dilated_bwd(proj, do_dil, lse_dil, delta, bias, g, S=S, into=dqkv)
        dbias.append(db_g)
    drel = _dil_bias_bwd(jnp.stack(dbias, axis=0).reshape(len(DIL_GROUPS), DIL_HEADS, SPAN, 2 * SPAN), buckets)
    Gs["rel_bias"] = drel

    dxl, dgl, dcw, dcb, dwa, dwx, dba, dbx, dlam = _lru_bwd(proj, hl, a_lru, mult_lru, dz, *lru_args, S=S)
    Gs["conv_w"], Gs["conv_b"] = dcw, dcb
    Gs["w_rg_a"], Gs["w_rg_x"] = _block_diag_extract(dwa), _block_diag_extract(dwx)
    Gs["b_rg_a"], Gs["b_rg_x"], Gs["lru_lambda"] = dba, dbx, dlam
    Gs["loss"] = loss

    dproj = jnp.concatenate([dxl, dgl] + dqkv + [dqm, dg0, dg1, dg2], axis=1)
    cols = D_MODEL // W_IN_PIECES
    tie = []
    for q in range(W_IN_PIECES):
        dw_q = _matmul(h, dproj, M=cols, N=D_IN, K=S, mode="tn", bm=cols, bn=D_IN // 2, bk=1024, name=f"mm_dw_in_{q}",
                       a_off=(0, q), out_dtypes=(MXU_DTYPE,), parts=("rows_t", D_IN // N_DEV), deps=tie)
        tie = [send_grads({f"w_in_{q}": dw_q})]
    dh = _matmul(dproj, W["w_in_t"], M=S, N=D_MODEL, K=D_IN, mode="nn", bm=256, bn=D_MODEL, bk=D_IN, name="mm_dh",
                 deps=tie)
    grad_x, Gs["g_mix"] = _rmsnorm_bwd(x, P["g_mix"], dh, dx1, rows=S, name="norm_mix_bwd")
    return grad_x, reduce_small(Gs)


BIG = ("w_in", "w_lru_out", "w_dil_out", "w_mem_kv", "w_mem_out", "w_out", "w_mlp_in", "w_mlp_out")
W_IN_PIECES = 2
COL_SHARDED = ("w_lru_out", "w_dil_out", "w_mem_out", "w_mlp_in")
GATHERED_TRANSPOSED = ("w_mlp_in",)
SMALL = ("g_mix", "b_gate", "conv_b", "w_rg_a", "b_rg_a", "w_rg_x", "b_rg_x", "lru_lambda", "rel_bias", "g_mem",
         "g_mlp", "g_final")
WEIGHTS = ("g_mix", "w_in", "b_gate", "conv_w", "conv_b", "w_rg_a", "b_rg_a", "w_rg_x", "b_rg_x", "lru_lambda",
           "w_lru_out", "rel_bias", "w_dil_out", "g_mem", "w_mem_kv", "w_mem_out", "w_out", "g_mlp", "w_mlp_in",
           "w_mlp_out", "g_final")


def _gathered_to_full(name, gathered):
    if name in COL_SHARDED:
        n, r, c = gathered.shape
        return gathered.transpose(1, 0, 2).reshape(r, n * c)
    n, r, c = gathered.shape
    return gathered.reshape(n * r, c)


SMALL_GRADS = (("g_mix", (1, 1024)), ("b_gate0", (1, 1024)), ("b_gate1", (1, 1024)), ("b_gate2", (1, 1024)),
               ("conv_b", (1, 768)), ("w_rg_a", (12, 64, 64)), ("b_rg_a", (1, 768)), ("w_rg_x", (12, 64, 64)),
               ("b_rg_x", (1, 768)), ("lru_lambda", (1, 768)), ("rel_bias", (32, 128)), ("g_mem", (1, 1024)),
               ("g_mlp", (1, 1024)), ("g_final", (1, 1024)), ("conv_w", (4, 768)), ("loss", (1, 1)))


def _pack(parts):
    flat = jnp.concatenate([p.reshape(-1) for p in parts])
    return jnp.pad(flat, (0, (-flat.shape[0]) % 1024)).reshape(-1, 128)


def _unpack(pack, shapes):
    flat = pack.reshape(-1)
    out, off = [], 0
    for shp in shapes:
        size = math.prod(shp)
        out.append(flat[off:off + size].reshape(shp))
        off += size
    return out


def kernel(x, mem, g_mix, w_in, b_gate, conv_w, conv_b, w_rg_a, b_rg_a, w_rg_x, b_rg_x, lru_lambda, w_lru_out, rel_bias, w_dil_out, g_mem, w_mem_kv, w_mem_out, w_out, g_mlp, w_mlp_in, w_mlp_out, g_final, loss_target, m_g_mix, m_w_in, m_b_gate, m_conv_w, m_conv_b, m_w_rg_a, m_b_rg_a, m_w_rg_x, m_b_rg_x, m_lru_lambda, m_w_lru_out, m_rel_bias, m_w_dil_out, m_g_mem, m_w_mem_kv, m_w_mem_out, m_w_out, m_g_mlp, m_w_mlp_in, m_w_mlp_out, m_g_final, v_g_mix, v_w_in, v_b_gate, v_conv_w, v_conv_b, v_w_rg_a, v_b_rg_a, v_w_rg_x, v_b_rg_x, v_lru_lambda, v_w_lru_out, v_rel_bias, v_w_dil_out, v_g_mem, v_w_mem_kv, v_w_mem_out, v_w_out, v_g_mlp, v_w_mlp_in, v_w_mlp_out, v_g_final):
    w = dict(g_mix=g_mix, w_in=w_in, b_gate=b_gate, conv_w=conv_w, conv_b=conv_b, w_rg_a=w_rg_a, b_rg_a=b_rg_a,
             w_rg_x=w_rg_x, b_rg_x=b_rg_x, lru_lambda=lru_lambda, w_lru_out=w_lru_out, rel_bias=rel_bias,
             w_dil_out=w_dil_out, g_mem=g_mem, w_mem_kv=w_mem_kv, w_mem_out=w_mem_out, w_out=w_out, g_mlp=g_mlp,
             w_mlp_in=w_mlp_in, w_mlp_out=w_mlp_out, g_final=g_final)
    m = dict(g_mix=m_g_mix, w_in=m_w_in, b_gate=m_b_gate, conv_w=m_conv_w, conv_b=m_conv_b, w_rg_a=m_w_rg_a,
             b_rg_a=m_b_rg_a, w_rg_x=m_w_rg_x, b_rg_x=m_b_rg_x, lru_lambda=m_lru_lambda, w_lru_out=m_w_lru_out,
             rel_bias=m_rel_bias, w_dil_out=m_w_dil_out, g_mem=m_g_mem, w_mem_kv=m_w_mem_kv, w_mem_out=m_w_mem_out,
             w_out=m_w_out, g_mlp=m_g_mlp, w_mlp_in=m_w_mlp_in, w_mlp_out=m_w_mlp_out, g_final=m_g_final)
    v = dict(g_mix=v_g_mix, w_in=v_w_in, b_gate=v_b_gate, conv_w=v_conv_w, conv_b=v_conv_b, w_rg_a=v_w_rg_a,
             b_rg_a=v_b_rg_a, w_rg_x=v_w_rg_x, b_rg_x=v_b_rg_x, lru_lambda=v_lru_lambda, w_lru_out=v_w_lru_out,
             rel_bias=v_rel_bias, w_dil_out=v_w_dil_out, g_mem=v_g_mem, w_mem_kv=v_w_mem_kv, w_mem_out=v_w_mem_out,
             w_out=v_w_out, g_mlp=v_g_mlp, w_mlp_in=v_w_mlp_in, w_mlp_out=v_w_mlp_out, g_final=v_g_final)

    my_idx = _dev_index(_my_pos())

    g_in, g_cw = _all_gather([_mx(w["w_in"].T), w["conv_w"]])
    W = {"w_in_t": g_in.reshape(D_IN, D_MODEL), "conv_w": g_cw.transpose(1, 0, 2).reshape(CONV_WIDTH, D_RNN)}
    late, order_after = {}, [g_in]
    for group, names in (("branch", ("w_mem_kv", "w_lru_out", "w_dil_out", "w_mem_out", "w_out")),
                         ("mlp", ("w_mlp_in", "w_mlp_out"))):
        shards = [_mx(w[n].T if n in GATHERED_TRANSPOSED else w[n]) for n in names]
        started = _push_start(shards, [(N_DEV,) + s.shape for s in shards], _gather_refs, f"gather_{group}_start",
                              after=order_after)
        late[group] = (names, shards, started)
        order_after = [started["token"]]
    P = {n: w[n] for n in SMALL}

    def late_weights(group, after):
        names, shards, started = late[group]
        out = {}
        for n, land, own in zip(names, _push_wait(started, after), shards):
            full = lax.dynamic_update_index_in_dim(land, own, my_idx, 0)
            if n in GATHERED_TRANSPOSED:
                out[n + "_t"] = full.reshape(-1, full.shape[2])
            else:
                out[n] = _gathered_to_full(n, full)
        return out

    sent, small = [], {}

    def send_grads(gs):
        names = list(gs)
        parts = [gs[n] for n in names]
        own = [lax.dynamic_index_in_dim(p, my_idx, 0, keepdims=False) for p in parts]
        started = _push_start(parts, [(N_DEV - 1,) + p.shape[1:] for p in parts], _scatter_refs,
                              f"scatter{len(sent)}_start")
        sent.append((names, own, started))
        return started["token"]

    def reduce_small(gs):
        small["pack"] = _pack([gs[n] for n, _ in SMALL_GRADS])
        small["started"] = _push_start([small["pack"]], [(N_DEV,) + small["pack"].shape], _gather_refs, "small_start")
        return small["started"]["token"]

    grad_x, last_token = _local_step(x[0], mem[0], loss_target[0], W, P, late_weights, send_grads, reduce_small,
                                     late["mlp"][2]["token"][0, 0])

    grads, deltas, new_m, new_v = {}, {}, {}, {}
    after = last_token
    for names, own, started in sent[:-W_IN_PIECES]:
        for n, o, land in zip(names, own, _push_wait(started, after)):
            grads[n], deltas[n], new_m[n], new_v[n] = _adamw_landed(w[n], o, land, m[n], v[n], name=f"adamw_{n}")
            after = deltas[n]
    prev = None
    for q, (names, own, started) in enumerate(sent[-W_IN_PIECES:]):
        (land,) = _push_wait(started, after)
        prev = _adamw_landed(w["w_in"].T, own[0], land, m["w_in"].T, v["w_in"].T, name=f"adamw_{names[0]}",
                             col_blk=q, prev=prev)
        after = prev[1]
    grads["w_in"], deltas["w_in"], new_m["w_in"], new_v["w_in"] = [t.T for t in prev]
    (small_land,) = _push_wait(small["started"], after)
    total = _sum_slots(lax.dynamic_update_index_in_dim(small_land, small["pack"], my_idx, 0))
    summed = dict(zip([n for n, _ in SMALL_GRADS], _unpack(total, [shp for _, shp in SMALL_GRADS])))
    summed["b_gate"] = jnp.concatenate([summed.pop(f"b_gate{b}") for b in range(3)], axis=1)
    summed["rel_bias"] = summed["rel_bias"][:, :3 * DIL_HEADS]
    for n in SMALL:
        grads[n] = summed[n].reshape(w[n].shape)
    small_updates = _adamw_many([w[n] for n in SMALL], [grads[n] for n in SMALL], [m[n] for n in SMALL],
                                [v[n] for n in SMALL])
    for n, (d_, nm_, nv_) in zip(SMALL, small_updates):
        deltas[n], new_m[n], new_v[n] = d_, nm_, nv_
    conv_w_sum, loss_sum = summed["conv_w"], summed["loss"]
    cw_cols = D_RNN // N_DEV
    grads["conv_w"] = lax.dynamic_slice(conv_w_sum, (0, my_idx * cw_cols), (CONV_WIDTH, cw_cols))
    deltas["conv_w"], new_m["conv_w"], new_v["conv_w"] = _adamw_plain(
        w["conv_w"], grads["conv_w"], m["conv_w"], v["conv_w"], name="adamw_conv_w")

    return (loss_sum.reshape(()), grad_x[None], *[grads[n] for n in WEIGHTS], *[deltas[n] for n in WEIGHTS],
            *[new_m[n] for n in WEIGHTS], *[new_v[n] for n in WEIGHTS])
```

```python
import functools
import math

import jax
import jax.numpy as jnp
from jax import lax
from jax.experimental import pallas as pl
from jax.experimental.pallas import tpu as pltpu

F32 = jnp.float32
MXU_DTYPE = jnp.bfloat16
VMEM_LIMIT_BYTES = 56 * 1024 * 1024
N_DEV = 8

D_MODEL = 1024
N_MEM = 256
MEM_HEADS = 4
MEM_HEAD_DIM = 128
MEM_WIDTH = 512
D_RNN = 768
LRU_BLOCK = 64
N_LRU_BLOCKS = 12
LRU_GROUP = 256
N_LRU_GROUPS = 3
CONV_WIDTH = 4
LRU_C = 8.0
DIL_GROUPS = ((128, 1), (512, 4), (2048, 16))
SPAN = 128
DIL_HEADS = 4
DIL_HEAD_DIM = 64
NUM_BUCKETS = 32
MAX_DISTANCE = 2048
D_FF = 4096
D_IN = 7424
EPS = 1e-6
NEG = -1e30
C_XL, C_GATE, C_QKV, C_QM, C_GATES = 0, 768, 1536, 3840, 4352

ADAM_LR = 0.001
ADAM_B1 = 0.9
ADAM_B2 = 0.999
ADAM_EPS = 1e-08
ADAM_WD = 0.01
ADAM_STEP = 10

MESH = pl.DeviceIdType.MESH
GELU_K = math.sqrt(2.0 / math.pi)


def _cparams(sem=None):
    kw = dict(vmem_limit_bytes=VMEM_LIMIT_BYTES)
    if sem is not None:
        kw["dimension_semantics"] = sem
    return pltpu.CompilerParams(**kw)


def _mx(v):
    return v.astype(MXU_DTYPE)


def _dot(a, b, mode="nn"):
    dims = {"nn": (((1,), (0,)), ((), ())), "nt": (((1,), (1,)), ((), ())), "tn": (((0,), (0,)), ((), ()))}[mode]
    return lax.dot_general(_mx(a), _mx(b), dims, preferred_element_type=F32)


def _colsum(v):
    return jnp.sum(v, axis=0, keepdims=True)


def _matmul(a, b, *, M, N, K, mode, bm, bn, bk, name, out_dtypes=(F32,), epilogue=None, extras=(),
            a_off=(0, 0), b_off=(0, 0), j_outer=False, deps=(), parts=None):
    assert M % bm == 0 and N % bn == 0 and K % bk == 0, (name, M, N, K, bm, bn, bk)
    nm, nn, nk = M // bm, N // bn, K // bk

    def ij(f):
        if j_outer:
            return lambda j, i, k: f(i, j, k)
        return f

    if mode == "tn":
        a_spec = pl.BlockSpec((bk, bm), ij(lambda i, j, k: (k + a_off[0], i + a_off[1])))
    else:
        a_spec = pl.BlockSpec((bm, bk), ij(lambda i, j, k: (i + a_off[0], k + a_off[1])))
    if mode == "nt":
        b_spec = pl.BlockSpec((bn, bk), ij(lambda i, j, k: (j + b_off[0], k + b_off[1])))
    else:
        b_spec = pl.BlockSpec((bk, bn), ij(lambda i, j, k: (k + b_off[0], j + b_off[1])))
    ex_specs = [pl.BlockSpec((bm, bn), ij(functools.partial(lambda i, j, k, o: (i + o[0], j + o[1]), o=off)))
                for _, off in extras]
    if parts is None:
        out_dims = (M, N)
        out_spec = pl.BlockSpec((bm, bn), ij(lambda i, j, k: (i, j)))
    elif parts[0] == "rows":
        r = parts[1]
        assert bm % r == 0
        out_dims = (M // r, r, N)
        out_spec = pl.BlockSpec((bm // r, r, bn), ij(lambda i, j, k: (i, 0, j)))
    elif parts[0] == "rows_t":
        r = parts[1]
        assert bn % r == 0
        out_dims = (N // r, r, M)
        out_spec = pl.BlockSpec((bn // r, r, bm), ij(lambda i, j, k: (j, 0, i)))
    else:
        c = parts[1]
        assert bn % c == 0
        out_dims = (N // c, M, c)
        out_spec = pl.BlockSpec((bn // c, bm, c), ij(lambda i, j, k: (j, i, 0)))
    n_ex, n_out, n_dep = len(extras), len(out_dtypes), len(deps)

    def body(*refs):
        a_ref, b_ref = refs[0], refs[1]
        ex = refs[2:2 + n_ex]
        outs = refs[2 + n_ex + n_dep:2 + n_ex + n_dep + n_out]
        part = _dot(a_ref[...], b_ref[...], mode)

        def finish(acc):
            vals = epilogue(acc, *[e[...] for e in ex]) if epilogue is not None else (acc,)
            for o, v in zip(outs, vals):
                if parts is not None and parts[0] == "rows_t":
                    v = v.T
                v = v.astype(o.dtype)
                if parts is None:
                    o[...] = v
                elif parts[0] in ("rows", "rows_t"):
                    for ch in range(v.shape[0] // parts[1]):
                        o[ch] = v[ch * parts[1]:(ch + 1) * parts[1], :]
                else:
                    for ch in range(bn // parts[1]):
                        o[ch] = v[:, ch * parts[1]:(ch + 1) * parts[1]]

        if nk == 1:
            finish(part)
        else:
            acc_ref = refs[-1]
            k = pl.program_id(2)

            @pl.when(k == 0)
            def _():
                acc_ref[...] = part

            @pl.when(k > 0)
            def _():
                acc_ref[...] += part

            @pl.when(k == nk - 1)
            def _():
                finish(acc_ref[...])

    grid = (nn, nm, nk) if j_outer else (nm, nn, nk)
    res = pl.pallas_call(
        body, name=name, grid=grid,
        in_specs=[a_spec, b_spec] + ex_specs + [pl.BlockSpec(memory_space=pl.ANY)] * n_dep,
        out_specs=[out_spec] * n_out,
        out_shape=[jax.ShapeDtypeStruct(out_dims, dt) for dt in out_dtypes],
        scratch_shapes=[pltpu.VMEM((bm, bn), F32)] if nk > 1 else [],
        compiler_params=_cparams(("parallel", "parallel", "arbitrary")),
    )(a, b, *[e for e, _ in extras], *deps)
    return res[0] if n_out == 1 else res


def _matmul_rows(a, b, *, M, K, mode, bm, name, row_fn, out_dtypes, tiles=(), vecs=(), acc_widths=(), deps=()):
    N = D_MODEL
    assert M % bm == 0
    n_t, n_v, n_o, n_a, n_d = len(tiles), len(vecs), len(out_dtypes), len(acc_widths), len(deps)
    row = pl.BlockSpec((bm, N), lambda i: (i, 0))
    b_shape = (K, N) if mode == "nn" else (N, K)

    def body(*refs):
        a_ref, b_ref = refs[0], refs[1]
        ins = refs[2:2 + n_t + n_v]
        outs = refs[2 + n_t + n_v + n_d:2 + n_t + n_v + n_d + n_o]
        accs = refs[2 + n_t + n_v + n_d + n_o:]
        tile_vals, partials = row_fn(_dot(a_ref[...], b_ref[...], mode), *[r[...] for r in ins])
        for o, val in zip(outs, tile_vals):
            o[...] = val.astype(o.dtype)
        for o, val in zip(accs, partials):
            @pl.when(pl.program_id(0) == 0)
            def _(o=o):
                o[...] = jnp.zeros_like(o)

            o[...] += val

    res = pl.pallas_call(
        body, name=name, grid=(M // bm,),
        in_specs=[pl.BlockSpec((bm, K), lambda i: (i, 0)), pl.BlockSpec(b_shape, lambda i: (0, 0))] + [row] * n_t
        + [pl.BlockSpec((1, N), lambda i: (0, 0))] * n_v + [pl.BlockSpec(memory_space=pl.ANY)] * n_d,
        out_specs=[row] * n_o + [pl.BlockSpec((1, w_), lambda i: (0, 0)) for w_ in acc_widths],
        out_shape=[jax.ShapeDtypeStruct((M, N), dt) for dt in out_dtypes]
        + [jax.ShapeDtypeStruct((1, w_), F32) for w_ in acc_widths],
        compiler_params=_cparams(("arbitrary",) if n_a else ("parallel",)),
    )(a, b, *tiles, *vecs, *deps)
    return res


def _rmsnorm_fwd(x, g, *, rows, name, bt=512):
    bt = min(bt, rows)

    def body(x_ref, g_ref, o_ref):
        xv = x_ref[...]
        r = lax.rsqrt(jnp.mean(xv * xv, axis=-1, keepdims=True) + EPS)
        o_ref[...] = (xv * r * g_ref[...]).astype(o_ref.dtype)

    return pl.pallas_call(
        body, name=name, grid=(rows // bt,),
        in_specs=[pl.BlockSpec((bt, D_MODEL), lambda i: (i, 0)), pl.BlockSpec((1, D_MODEL), lambda i: (0, 0))],
        out_specs=pl.BlockSpec((bt, D_MODEL), lambda i: (i, 0)),
        out_shape=jax.ShapeDtypeStruct((rows, D_MODEL), MXU_DTYPE),
        compiler_params=_cparams(("parallel",)),
    )(x, g.reshape(1, D_MODEL))


def _rms_bwd_tile(xv, gv, dyv):
    r = lax.rsqrt(jnp.mean(xv * xv, axis=-1, keepdims=True) + EPS)
    w = dyv * gv
    dx = r * w - xv * (r * r * r) * jnp.mean(w * xv, axis=-1, keepdims=True)
    dg = _colsum(dyv * xv * r)
    return dx, dg


def _residual_then_norm(acc, x_t, g):
    x1 = x_t + acc
    r = lax.rsqrt(jnp.mean(x1 * x1, axis=-1, keepdims=True) + EPS)
    return (x1, x1 * r * g), ()


def _residual_then_loss(acc, x_t, tgt_t, g):
    x2 = x_t + acc
    r = lax.rsqrt(jnp.mean(x2 * x2, axis=-1, keepdims=True) + EPS)
    diff = x2 * r * g - tgt_t
    part = jnp.sum(jnp.mean(diff * diff, axis=-1, keepdims=True), axis=0, keepdims=True) * 0.5
    dx, dg = _rms_bwd_tile(x2, g, diff * (1.0 / D_MODEL))
    return (dx, dx), (part, dg)


def _norm_bwd_then_residual(n_out):
    def fn(acc, x_t, res_t, g):
        dx, dg = _rms_bwd_tile(x_t, g, acc)
        return (dx + res_t,) * n_out, (dg,)

    return fn


def _rmsnorm_bwd(x, g, dy, res, *, rows, name, bt=512, dx_dtypes=(F32,)):
    bt = min(bt, rows)
    has_res = res is not None

    def body(*refs):
        x_ref, g_ref, dy_ref = refs[:3]
        res_ref = refs[3] if has_res else None
        outs = refs[3 + int(has_res):]
        dx, dg = _rms_bwd_tile(x_ref[...], g_ref[...], dy_ref[...])
        if has_res:
            dx = dx + res_ref[...]
        dg_ref = outs[-1]

        @pl.when(pl.program_id(0) == 0)
        def _():
            dg_ref[...] = jnp.zeros_like(dg_ref)

        dg_ref[...] += dg
        for o in outs[:-1]:
            o[...] = dx.astype(o.dtype)

    row_spec = pl.BlockSpec((bt, D_MODEL), lambda i: (i, 0))
    vec_spec = pl.BlockSpec((1, D_MODEL), lambda i: (0, 0))
    ins = [x, g.reshape(1, D_MODEL), dy] + ([res] if has_res else [])
    return pl.pallas_call(
        body, name=name, grid=(rows // bt,),
        in_specs=[row_spec, vec_spec, row_spec] + ([row_spec] if has_res else []),
        out_specs=[row_spec] * len(dx_dtypes) + [vec_spec],
        out_shape=[jax.ShapeDtypeStruct((rows, D_MODEL), dt) for dt in dx_dtypes] + [jax.ShapeDtypeStruct((1, D_MODEL), F32)],
        compiler_params=_cparams(("arbitrary",)),
    )(*ins)


LRU_T = 512
SCAN_GROUPS = 4


def _gelu(x):
    t = jnp.tanh(GELU_K * (x + 0.044715 * x * x * x))
    return 0.5 * x * (1.0 + t), t


def _gelu_grad(x, t):
    return 0.5 * (1.0 + t) + 0.5 * x * (1.0 - t * t) * GELU_K * (1.0 + 3.0 * 0.044715 * x * x)


def _softplus_neg(lam):
    z = -lam
    u = jnp.exp(-jnp.abs(z))
    w = 1.0 + u
    l1p = jnp.where(w == 1.0, u, jnp.log(w) * u / jnp.where(w == 1.0, 1.0, w - 1.0))
    return jnp.maximum(z, 0.0) + l1p


def _shift_down(cur, prev8, k, row8):
    y = pltpu.roll(cur, k, 0)
    head = jnp.where(row8 < k, pltpu.roll(prev8, k, 0), y[0:8])
    return jnp.concatenate([head, y[8:]], axis=0)


def _shift_up(cur, next8, k, row8):
    n = cur.shape[0]
    y = pltpu.roll(cur, n - k, 0)
    tail = jnp.where(row8 >= 8 - k, pltpu.roll(next8, 8 - k, 0), y[n - 8:n])
    return jnp.concatenate([y[0:n - 8], tail], axis=0)


def _lru_gates(xl, p8, cw, cb, wa, wx, ba, bx, lam, row8, a_mult=None):
    sh = [xl] + [_shift_down(xl, p8, k, row8) for k in (1, 2, 3)]
    xc = cb + cw[3:4] * sh[0] + cw[2:3] * sh[1] + cw[1:2] * sh[2] + cw[0:1] * sh[3]
    r = jax.nn.sigmoid(_dot(xc, wa) + ba)
    i = jax.nn.sigmoid(_dot(xc, wx) + bx)
    sp = _softplus_neg(lam)
    if a_mult is None:
        la = -LRU_C * r * sp
        a = jnp.exp(la)
        mult = jnp.sqrt(jnp.tanh(-la) * (a * a + 1.0))
    else:
        a, mult = a_mult
    return dict(sh=sh, xc=xc, r=r, i=i, sp=sp, a=a, mult=mult)


def _lru_specs(n_t, reverse):
    T = LRU_T
    tt = (lambda t: n_t - 1 - t) if reverse else (lambda t: t)
    blk = lambda col0: pl.BlockSpec((T, LRU_GROUP), lambda g, t: (tt(t), col0 + g))
    prev8 = lambda col0: pl.BlockSpec((8, LRU_GROUP), lambda g, t: (jnp.maximum(tt(t) * (T // 8) - 1, 0), col0 + g))
    vec = lambda rows: pl.BlockSpec((rows, LRU_GROUP), lambda g, t: (0, g))
    wbd = pl.BlockSpec((1, LRU_GROUP, LRU_GROUP), lambda g, t: (g, 0, 0))
    return blk, prev8, vec, wbd


def _lru_fwd(proj, conv_w, conv_b, wa_bd, wx_bd, b_a, b_x, lam, *, S):
    T = LRU_T
    n_t = S // T
    blk, _, vec, wbd = _lru_specs(n_t, False)

    def body(xl_ref, gate_ref, cw_ref, cb_ref, wa_ref, wx_ref, ba_ref, bx_ref, lam_ref,
             hl_ref, z_ref, a_s, m_ref, prev8, hcar, b_s):
        @pl.when(pl.program_id(1) == 0)
        def _():
            prev8[...] = jnp.zeros_like(prev8)
            hcar[...] = jnp.zeros_like(hcar)

        row8 = lax.broadcasted_iota(jnp.int32, (8, LRU_GROUP), 0)
        xl = xl_ref[...]
        q = _lru_gates(xl, prev8[...], cw_ref[...], cb_ref[...], wa_ref[0], wx_ref[0], ba_ref[...], bx_ref[...],
                       lam_ref[...], row8)
        prev8[...] = xl[T - 8:T]
        a_s[...] = q["a"]
        m_ref[...] = q["mult"]
        b_s[...] = q["mult"] * q["i"] * q["xc"]

        def step(c, carry):
            local = []
            for u in range(SCAN_GROUPS):
                off = pl.multiple_of((c * SCAN_GROUPS + u) * 8, 8)
                A = a_s[pl.ds(off, 8), :]
                B = b_s[pl.ds(off, 8), :]
                for k in (1, 2, 4):
                    a_sh = jnp.where(row8 >= k, pltpu.roll(A, k, 0), 1.0)
                    b_sh = jnp.where(row8 >= k, pltpu.roll(B, k, 0), 0.0)
                    B = A * b_sh + B
                    A = A * a_sh
                local.append((off, A, B))
            for off, A, B in local:
                h = A * carry + B
                hl_ref[pl.ds(off, 8), :] = h
                carry = h[7:8, :]
            return carry

        hcar[...] = lax.fori_loop(0, T // (8 * SCAN_GROUPS), step, hcar[...])
        ge, _ = _gelu(gate_ref[...])
        z_ref[...] = (ge * hl_ref[...]).astype(z_ref.dtype)

    return pl.pallas_call(
        body, name="lru_fwd", grid=(N_LRU_GROUPS, n_t),
        in_specs=[blk(C_XL // LRU_GROUP), blk(C_GATE // LRU_GROUP), vec(4), vec(1), wbd, wbd, vec(1), vec(1), vec(1)],
        out_specs=[blk(0)] * 4,
        out_shape=[jax.ShapeDtypeStruct((S, D_RNN), F32), jax.ShapeDtypeStruct((S, D_RNN), MXU_DTYPE),
                   jax.ShapeDtypeStruct((S, D_RNN), F32), jax.ShapeDtypeStruct((S, D_RNN), F32)],
        scratch_shapes=[pltpu.VMEM((8, LRU_GROUP), F32), pltpu.VMEM((1, LRU_GROUP), F32), pltpu.VMEM((T, LRU_GROUP), F32)],
        compiler_params=_cparams(("parallel", "arbitrary")),
    )(proj, proj, conv_w, conv_b, wa_bd, wx_bd, b_a, b_x, lam)


def _lru_bwd(proj, hl, a_fwd, mult_fwd, dz, conv_w, conv_b, wa_bd, wx_bd, b_a, b_x, lam, *, S):
    T = LRU_T
    n_t = S // T
    blk, prev8s, vec, wbd = _lru_specs(n_t, True)

    def body(xl_ref, xlp_ref, gate_ref, hl_ref, hlp_ref, a_ref, m_ref, dz_ref, cw_ref, cb_ref, wa_ref, wx_ref, ba_ref,
             bx_ref, lam_ref, dxl_ref, dgate_ref, dcw_ref, dcb_ref, dwa_ref, dwx_ref, dba_ref, dbx_ref, dlam_ref,
             next8, gcar, c_s, b_s, l_s):
        t = pl.program_id(1)
        first_chunk = t == n_t - 1

        @pl.when(t == 0)
        def _():
            next8[...] = jnp.zeros_like(next8)
            gcar[...] = jnp.zeros_like(gcar)
            for ref in (dcw_ref, dcb_ref, dwa_ref, dwx_ref, dba_ref, dbx_ref, dlam_ref):
                ref[...] = jnp.zeros_like(ref)

        row8 = lax.broadcasted_iota(jnp.int32, (8, LRU_GROUP), 0)
        rowT = lax.broadcasted_iota(jnp.int32, (T, LRU_GROUP), 0)
        keep = jnp.where(first_chunk, 0.0, 1.0)
        xl = xl_ref[...]
        wa, wx, lam_v = wa_ref[0], wx_ref[0], lam_ref[...]
        q = _lru_gates(xl, xlp_ref[...] * keep, cw_ref[...], cb_ref[...], wa, wx, ba_ref[...], bx_ref[...], lam_v, row8,
                       a_mult=(a_ref[...], m_ref[...]))
        a, mult, r, i, xc, sp = q["a"], q["mult"], q["r"], q["i"], q["xc"], q["sp"]
        hl_v = hl_ref[...]
        dz_v = dz_ref[...]
        gate = gate_ref[...]
        ge, th = _gelu(gate)
        dgate_ref[...] = (dz_v * hl_v * _gelu_grad(gate, th)).astype(dgate_ref.dtype)

        c_s[...] = jnp.where(rowT == T - 1, 0.0, pltpu.roll(a, T - 1, 0))
        b_s[...] = dz_v * ge + jnp.where(rowT == T - 1, gcar[...], 0.0)

        def step(n, carry):
            local = []
            for u in range(SCAN_GROUPS):
                off = pl.multiple_of((T // 8 - 1 - (n * SCAN_GROUPS + u)) * 8, 8)
                C = c_s[pl.ds(off, 8), :]
                B = b_s[pl.ds(off, 8), :]
                for k in (1, 2, 4):
                    c_sh = jnp.where(row8 < 8 - k, pltpu.roll(C, 8 - k, 0), 1.0)
                    b_sh = jnp.where(row8 < 8 - k, pltpu.roll(B, 8 - k, 0), 0.0)
                    B = B + C * b_sh
                    C = C * c_sh
                local.append((off, C, B))
            for off, C, B in local:
                lam_t = B + C * carry
                l_s[pl.ds(off, 8), :] = lam_t
                carry = lam_t[0:1, :]
            return carry

        lax.fori_loop(0, T // (8 * SCAN_GROUPS), step, jnp.zeros((1, LRU_GROUP), F32))
        lmb = l_s[...]
        gcar[...] = a[0:1, :] * lmb[0:1, :]

        h_prev = _shift_down(hl_v, hlp_ref[...] * keep, 1, row8)
        da = lmb * h_prev
        dmult = lmb * i * xc
        di = lmb * mult * xc
        dxc = lmb * mult * i
        dla = da * a - dmult * (a * a) / mult
        dr = dla * (-LRU_C * sp)
        dlam_ref[...] += _colsum(dla * (-LRU_C * r)) * (-jax.nn.sigmoid(-lam_v))
        dpa = dr * r * (1.0 - r)
        dpx = di * i * (1.0 - i)
        dxc = dxc + _dot(dpa, wa, "nt") + _dot(dpx, wx, "nt")
        dwa_ref[0] += _dot(xc, dpa, "tn")
        dwx_ref[0] += _dot(xc, dpx, "tn")
        dba_ref[...] += _colsum(dpa)
        dbx_ref[...] += _colsum(dpx)
        dcb_ref[...] += _colsum(dxc)
        cw = cw_ref[...]
        n8 = next8[...]
        dxl = cw[3:4] * dxc
        for k in (1, 2, 3):
            dxl = dxl + cw[3 - k:4 - k] * _shift_up(dxc, n8, k, row8)
        for k in range(4):
            dcw_ref[3 - k:4 - k, :] += _colsum(dxc * q["sh"][k])
        next8[...] = dxc[0:8]
        dxl_ref[...] = dxl.astype(dxl_ref.dtype)

    res = pl.pallas_call(
        body, name="lru_bwd", grid=(N_LRU_GROUPS, n_t),
        in_specs=[blk(C_XL // LRU_GROUP), prev8s(C_XL // LRU_GROUP), blk(C_GATE // LRU_GROUP), blk(0), prev8s(0), blk(0),
                  blk(0), blk(0), vec(4), vec(1), wbd, wbd, vec(1), vec(1), vec(1)],
        out_specs=[blk(0), blk(0), vec(4), vec(1), wbd, wbd, vec(1), vec(1), vec(1)],
        out_shape=[jax.ShapeDtypeStruct((S, D_RNN), MXU_DTYPE), jax.ShapeDtypeStruct((S, D_RNN), MXU_DTYPE),
                   jax.ShapeDtypeStruct((4, D_RNN), F32), jax.ShapeDtypeStruct((1, D_RNN), F32),
                   jax.ShapeDtypeStruct((N_LRU_GROUPS, LRU_GROUP, LRU_GROUP), F32),
                   jax.ShapeDtypeStruct((N_LRU_GROUPS, LRU_GROUP, LRU_GROUP), F32),
                   jax.ShapeDtypeStruct((1, D_RNN), F32), jax.ShapeDtypeStruct((1, D_RNN), F32),
                   jax.ShapeDtypeStruct((1, D_RNN), F32)],
        scratch_shapes=[pltpu.VMEM((8, LRU_GROUP), F32), pltpu.VMEM((1, LRU_GROUP), F32),
                        pltpu.VMEM((T, LRU_GROUP), F32), pltpu.VMEM((T, LRU_GROUP), F32), pltpu.VMEM((T, LRU_GROUP), F32)],
        compiler_params=_cparams(("parallel", "arbitrary")),
    )(proj, proj, proj, hl, hl, a_fwd, mult_fwd, dz, conv_w, conv_b, wa_bd, wx_bd, b_a, b_x, lam)
    return res


def _block_diag(w):
    w4 = w.reshape(N_LRU_GROUPS, 4, LRU_BLOCK, 1, LRU_BLOCK)
    eye = jnp.eye(4, dtype=w.dtype).reshape(1, 4, 1, 4, 1)
    return (w4 * eye).reshape(N_LRU_GROUPS, LRU_GROUP, LRU_GROUP)


def _block_diag_extract(wbd):
    w5 = wbd.reshape(N_LRU_GROUPS, 4, LRU_BLOCK, 4, LRU_BLOCK)
    return jnp.stack([w5[:, a, :, a, :] for a in range(4)], axis=1).reshape(N_LRU_BLOCKS, LRU_BLOCK, LRU_BLOCK)


def _t5_bucket(dist):
    max_exact = NUM_BUCKETS // 2
    df = jnp.maximum(dist, 1).astype(jnp.float32)
    large = max_exact + (jnp.log(df / max_exact) / math.log(MAX_DISTANCE / max_exact)
                         * (NUM_BUCKETS - max_exact)).astype(jnp.int32)
    large = jnp.minimum(large, NUM_BUCKETS - 1)
    return jnp.where(dist < max_exact, dist, large)


def _band_offsets():
    qi = jnp.arange(SPAN)[:, None]
    kj = jnp.arange(2 * SPAN)[None, :]
    return qi + SPAN - kj


def _dil_buckets():
    off = _band_offsets()
    return jnp.stack([_t5_bucket(jnp.maximum(off, 0) * dil) for _, dil in DIL_GROUPS]).astype(jnp.int32)


def _dil_bias(rel_bias, buckets):
    def body(tbl_ref, bk_ref, o_ref):
        g = pl.program_id(0)
        qi = lax.broadcasted_iota(jnp.int32, (SPAN, 2 * SPAN), 0)
        kj = lax.broadcasted_iota(jnp.int32, (SPAN, 2 * SPAN), 1)
        off = qi + SPAN - kj
        valid = (off >= 0) & (off <= SPAN)
        bk = bk_ref[0]
        for h in range(DIL_HEADS):
            acc = jnp.zeros((SPAN, 2 * SPAN), F32)
            for b in range(NUM_BUCKETS):
                acc = jnp.where(bk == b, tbl_ref[b, g * DIL_HEADS + h], acc)
            o_ref[0, h] = jnp.where(valid, acc, NEG)

    return pl.pallas_call(
        body, name="dil_bias", grid=(3,),
        in_specs=[pl.BlockSpec(memory_space=pltpu.SMEM), pl.BlockSpec((1, SPAN, 2 * SPAN), lambda g: (g, 0, 0))],
        out_specs=pl.BlockSpec((1, DIL_HEADS, SPAN, 2 * SPAN), lambda g: (g, 0, 0, 0)),
        out_shape=jax.ShapeDtypeStruct((3, DIL_HEADS, SPAN, 2 * SPAN), F32),
        compiler_params=_cparams(("parallel",)),
    )(rel_bias, buckets)


def _dil_bias_bwd(dbias, buckets):
    def body(db_ref, bk_ref, o_ref):
        lane = lax.broadcasted_iota(jnp.int32, (1, 128), 1)
        rows = [jnp.zeros((1, 128), F32) for _ in range(NUM_BUCKETS)]
        for g in range(3):
            bk = bk_ref[g]
            for h in range(DIL_HEADS):
                d = db_ref[g, h]
                for b in range(NUM_BUCKETS):
                    tot = jnp.sum(_colsum(jnp.where(bk == b, d, 0.0)), axis=1, keepdims=True)
                    rows[b] = jnp.where(lane == g * DIL_HEADS + h, tot, rows[b])
        for b in range(NUM_BUCKETS):
            o_ref[b:b + 1, :] = rows[b]

    return pl.pallas_call(
        body, name="dil_bias_bwd",
        out_shape=jax.ShapeDtypeStruct((NUM_BUCKETS, 128), F32),
        compiler_params=_cparams(),
    )(dbias, buckets)


DIL_SUBBLOCKS = (8, 4, 1)


def _dil_layout(g, S):
    dil, m = DIL_GROUPS[g][1], DIL_SUBBLOCKS[g]
    sub = SPAN * dil
    col = [(C_QKV + t * 768 + g * 256) // 128 for t in range(3)]
    return dil, m, sub, S // (sub * m), col


def _residue_rows(b, r, dil):
    return pl.ds(b * SPAN * dil + r, SPAN, stride=dil) if dil > 1 else pl.ds(b * SPAN, SPAN)


def _for_residues(dil, fn):
    if dil <= 4:
        for r in range(dil):
            fn(r)
    else:
        lax.fori_loop(0, dil, lambda r, c: (fn(r), c)[1], 0, unroll=4)


def _pair_scores(qm, k2, bias, first_cols):
    s = _dot(qm, k2, "nt") * (DIL_HEAD_DIM ** -0.5) + bias
    kj = lax.broadcasted_iota(jnp.int32, s.shape, 1)
    return jnp.where(kj < first_cols, NEG, s)


def _dilated_fwd(proj, bias, g, *, S):
    dil, m, sub, nc, (qc, kc, vc) = _dil_layout(g, S)
    R = sub * m
    cur = lambda cb: pl.BlockSpec((R, 128), lambda p, i: (i, cb + p))
    prv = lambda cb: pl.BlockSpec((sub, 128), lambda p, i: (jnp.maximum(i * m - 1, 0), cb + p))
    out = pl.BlockSpec((R, 128), lambda p, i: (i, p))

    def body(q_ref, kp_ref, kc_ref, vp_ref, vc_ref, b_ref, o_ref, lse_ref):
        lane = lax.broadcasted_iota(jnp.int32, (SPAN, 128), 1)
        sels = (lane < DIL_HEAD_DIM, lane >= DIL_HEAD_DIM)
        for b in range(m):
            first_cols = jnp.where(pl.program_id(1) == 0, SPAN, 0) if b == 0 else 0

            def one(r, b=b, first_cols=first_cols):
                rows = _residue_rows(b, r, dil)
                before = (kc_ref, vc_ref, _residue_rows(b - 1, r, dil)) if b else (kp_ref, vp_ref, _residue_rows(0, r, dil))
                q2 = q_ref[rows, :]
                k2 = _mx(jnp.concatenate([before[0][before[2], :], kc_ref[rows, :]], axis=0))
                v2 = _mx(jnp.concatenate([before[1][before[2], :], vc_ref[rows, :]], axis=0))
                qq = jnp.concatenate([jnp.where(sels[0], q2, 0.0), jnp.where(sels[1], q2, 0.0)], axis=0)
                s = _pair_scores(qq, k2, b_ref[0, 0], first_cols)
                mx = jnp.max(s, axis=-1, keepdims=True)
                p = jnp.exp(s - mx)
                den = jnp.sum(p, axis=-1, keepdims=True)
                o = _dot(p, v2) / den
                st = mx + jnp.log(den)
                o_ref[rows, :] = jnp.where(sels[0], o[0:SPAN], o[SPAN:2 * SPAN])
                lse_ref[rows, :] = jnp.where(lane == 0, st[0:SPAN], jnp.where(lane == 1, st[SPAN:2 * SPAN], 0.0))

            _for_residues(dil, one)

    return pl.pallas_call(
        body, name=f"dil_fwd{g}", grid=(2, nc),
        in_specs=[cur(qc), prv(kc), cur(kc), prv(vc), cur(vc),
                  pl.BlockSpec((1, 1, 2 * SPAN, 2 * SPAN), lambda p, i: (g, p, 0, 0))],
        out_specs=[out, out],
        out_shape=[jax.ShapeDtypeStruct((S, 256), F32), jax.ShapeDtypeStruct((S, 256), F32)],
        compiler_params=_cparams(("parallel", "parallel")),
    )(proj, proj, proj, proj, proj, bias.reshape(3, 2, 2 * SPAN, 2 * SPAN))


def _dilated_bwd(proj, do, lse, delta, bias, g, *, S, into=None):
    dil, m, sub, nc, (qc, kc, vc) = _dil_layout(g, S)
    R = sub * m
    cl = lambda i: jnp.minimum(i, nc - 1)
    cur = lambda cb: pl.BlockSpec((R, 128), lambda p, i: (cl(i), cb + p))
    prv = lambda cb: pl.BlockSpec((sub, 128), lambda p, i: (jnp.maximum(cl(i) * m - 1, 0), cb + p))
    q_out = pl.BlockSpec((R, 128), lambda p, i: (cl(i), 2 * g + p))
    kv_out = pl.BlockSpec((R, 128), lambda p, i: (jnp.maximum(i - 1, 0), 2 * g + p))
    scale = DIL_HEAD_DIM ** -0.5
    n_into = 0 if into is None else 3

    def body(q_ref, kp_ref, kc_ref, vp_ref, vc_ref, do_ref, lse_ref, dl_ref, b_ref, *rest):
        dq_ref, dk_ref, dv_ref, db_ref, dq_s, kc_s, vc_s, kp_s, vp_s, kcar, vcar = rest[n_into:]
        i = pl.program_id(1)

        @pl.when(i == 0)
        def _():
            kcar[...] = jnp.zeros_like(kcar)
            vcar[...] = jnp.zeros_like(vcar)
            db_ref[...] = jnp.zeros_like(db_ref)

        @pl.when(i < nc)
        def _():
            lane = lax.broadcasted_iota(jnp.int32, (SPAN, 128), 1)
            sels = (lane < DIL_HEAD_DIM, lane >= DIL_HEAD_DIM)
            for b in range(m):
                first_cols = jnp.where(i == 0, SPAN, 0) if b == 0 else 0

                def one(r, b=b, first_cols=first_cols):
                    rows = _residue_rows(b, r, dil)
                    rows_before = _residue_rows(b - 1 if b else 0, r, dil)
                    k_before, v_before = (kc_ref, vc_ref) if b else (kp_ref, vp_ref)
                    q2, do2 = q_ref[rows, :], do_ref[rows, :]
                    k2 = _mx(jnp.concatenate([k_before[rows_before, :], kc_ref[rows, :]], axis=0))
                    v2 = _mx(jnp.concatenate([v_before[rows_before, :], vc_ref[rows, :]], axis=0))
                    lse_t, dl_t = lse_ref[rows, :], dl_ref[rows, :]
                    qq = _mx(jnp.concatenate([jnp.where(sels[0], q2, 0.0), jnp.where(sels[1], q2, 0.0)], axis=0))
                    dd = _mx(jnp.concatenate([jnp.where(sels[0], do2, 0.0), jnp.where(sels[1], do2, 0.0)], axis=0))
                    lse2 = jnp.concatenate([lse_t[:, 0:1], lse_t[:, 1:2]], axis=0)
                    dl2 = jnp.concatenate([dl_t[:, 0:1], dl_t[:, 1:2]], axis=0)
                    p = jnp.exp(_pair_scores(qq, k2, b_ref[0, 0], first_cols) - lse2)
                    ds = p * (_dot(dd, v2, "nt") - dl2)
                    db_ref[0] += ds
                    dqq = _dot(ds, k2) * scale
                    dq2 = jnp.where(sels[0], dqq[0:SPAN], dqq[SPAN:2 * SPAN])
                    dk2 = _dot(ds, qq, "tn") * scale
                    dv2 = _dot(p, dd, "tn")
                    dq_s[rows, :] = dq2
                    kc_s[rows, :] = dk2[SPAN:2 * SPAN]
                    vc_s[rows, :] = dv2[SPAN:2 * SPAN]
                    if b:
                        kc_s[rows_before, :] += dk2[0:SPAN]
                        vc_s[rows_before, :] += dv2[0:SPAN]
                    else:
                        kp_s[rows_before, :] = dk2[0:SPAN]
                        vp_s[rows_before, :] = dv2[0:SPAN]

                _for_residues(dil, one)
            dq_ref[...] = dq_s[...].astype(dq_ref.dtype)
            last = pl.ds((m - 1) * sub, sub)
            kcar[last, :] += kp_s[...]
            vcar[last, :] += vp_s[...]
            dk_ref[...] = kcar[...].astype(dk_ref.dtype)
            dv_ref[...] = vcar[...].astype(dv_ref.dtype)
            kcar[...] = kc_s[...]
            vcar[...] = vc_s[...]

        @pl.when(i == nc)
        def _():
            dk_ref[...] = kcar[...].astype(dk_ref.dtype)
            dv_ref[...] = vcar[...].astype(dv_ref.dtype)

    stat = pl.BlockSpec((R, 128), lambda p, i: (cl(i), p))
    big = jax.ShapeDtypeStruct((S, len(DIL_GROUPS) * 256), MXU_DTYPE)
    return pl.pallas_call(
        body, name=f"dil_bwd{g}", grid=(2, nc + 1),
        in_specs=[cur(qc), prv(kc), cur(kc), prv(vc), cur(vc), stat, stat, stat,
                  pl.BlockSpec((1, 1, 2 * SPAN, 2 * SPAN), lambda p, i: (g, p, 0, 0))]
        + [pl.BlockSpec(memory_space=pl.ANY)] * n_into,
        out_specs=[q_out, kv_out, kv_out, pl.BlockSpec((1, 2 * SPAN, 2 * SPAN), lambda p, i: (p, 0, 0))],
        out_shape=[big, big, big, jax.ShapeDtypeStruct((2, 2 * SPAN, 2 * SPAN), F32)],
        input_output_aliases={9 + j: j for j in range(n_into)},
        scratch_shapes=[pltpu.VMEM((R, 128), F32)] * 3 + [pltpu.VMEM((sub, 128), F32)] * 2 + [pltpu.VMEM((R, 128), F32)] * 2,
        compiler_params=_cparams(("parallel", "arbitrary")),
    )(proj, proj, proj, proj, proj, do, lse, delta, bias.reshape(3, 2, 2 * SPAN, 2 * SPAN), *(into or ()))


def _dilated_merge(os_, lses, *, S, bt=512):
    tile = pl.BlockSpec((bt, 128), lambda i, p: (i, p))

    def body(o0, o1, o2, l0, l1, l2, o_ref, om_ref, lse_ref):
        lane = lax.broadcasted_iota(jnp.int32, (bt, 128), 1)
        lo = lane < DIL_HEAD_DIM
        ls = [l0[...], l1[...], l2[...]]
        ws, stat = [], jnp.zeros((bt, 128), F32)
        for e in range(2):
            a = [l[:, e:e + 1] for l in ls]
            m = jnp.maximum(jnp.maximum(a[0], a[1]), a[2])
            ex = [jnp.exp(v - m) for v in a]
            tot = ex[0] + ex[1] + ex[2]
            ws.append([v / tot for v in ex])
            stat = jnp.where(lane == e, m + jnp.log(tot), stat)
        acc = jnp.zeros((bt, 128), F32)
        for gi, o in enumerate((o0, o1, o2)):
            acc = acc + jnp.where(lo, ws[0][gi], ws[1][gi]) * o[...]
        o_ref[...] = acc
        om_ref[...] = _mx(acc)
        lse_ref[...] = stat

    return pl.pallas_call(
        body, name="dil_merge", grid=(S // bt, 2),
        in_specs=[tile] * 6, out_specs=[tile, tile, tile],
        out_shape=[jax.ShapeDtypeStruct((S, 256), F32), jax.ShapeDtypeStruct((S, 256), MXU_DTYPE),
                   jax.ShapeDtypeStruct((S, 256), F32)],
        compiler_params=_cparams(("parallel", "parallel")),
    )(*os_, *lses)


def _dilated_delta(do, o, *, S, bt=512):
    tile = pl.BlockSpec((bt, 128), lambda i, p: (i, p))

    def body(do_ref, o_ref, d_ref):
        lane = lax.broadcasted_iota(jnp.int32, (bt, 128), 1)
        prod = do_ref[...] * o_ref[...]
        d0 = jnp.sum(jnp.where(lane < DIL_HEAD_DIM, prod, 0.0), axis=-1, keepdims=True)
        d1 = jnp.sum(jnp.where(lane >= DIL_HEAD_DIM, prod, 0.0), axis=-1, keepdims=True)
        d_ref[...] = jnp.where(lane == 0, d0, jnp.where(lane == 1, d1, 0.0))

    return pl.pallas_call(
        body, name="dil_delta", grid=(S // bt, 2), in_specs=[tile, tile], out_specs=tile,
        out_shape=jax.ShapeDtypeStruct((S, 256), F32), compiler_params=_cparams(("parallel", "parallel")),
    )(do, o)


MEM_T = 2048
QM_BLK = C_QM // MEM_HEAD_DIM


def _mem_attn_fwd(proj, kv, *, S):
    scale = MEM_HEAD_DIM ** -0.5

    def body(q_ref, k_ref, v_ref, o_ref, om_ref, lse_ref):
        s = _dot(q_ref[...], k_ref[...], "nt") * scale
        m = jnp.max(s, axis=-1, keepdims=True)
        p = jnp.exp(s - m)
        den = jnp.sum(p, axis=-1, keepdims=True)
        o = _dot(p, v_ref[...]) / den
        o_ref[...] = o
        om_ref[...] = _mx(o)
        lse_ref[0] = m + jnp.log(den)

    return pl.pallas_call(
        body, name="mem_attn_fwd", grid=(S // MEM_T, MEM_HEADS),
        in_specs=[pl.BlockSpec((MEM_T, MEM_HEAD_DIM), lambda i, h: (i, QM_BLK + h)),
                  pl.BlockSpec((N_MEM, MEM_HEAD_DIM), lambda i, h: (0, h)),
                  pl.BlockSpec((N_MEM, MEM_HEAD_DIM), lambda i, h: (0, MEM_HEADS + h))],
        out_specs=[pl.BlockSpec((MEM_T, MEM_HEAD_DIM), lambda i, h: (i, h)),
                   pl.BlockSpec((MEM_T, MEM_HEAD_DIM), lambda i, h: (i, h)),
                   pl.BlockSpec((1, MEM_T, 1), lambda i, h: (h, i, 0))],
        out_shape=[jax.ShapeDtypeStruct((S, MEM_WIDTH), F32), jax.ShapeDtypeStruct((S, MEM_WIDTH), MXU_DTYPE),
                   jax.ShapeDtypeStruct((MEM_HEADS, S, 1), F32)],
        compiler_params=_cparams(("parallel", "parallel")),
    )(proj, kv, kv)


def _mem_attn_bwd(proj, kv, om, lse, dom, *, S):
    scale = MEM_HEAD_DIM ** -0.5

    def body(q_ref, k_ref, v_ref, o_ref, lse_ref, do_ref, dq_ref, dk_ref, dv_ref):
        @pl.when(pl.program_id(1) == 0)
        def _():
            dk_ref[...] = jnp.zeros_like(dk_ref)
            dv_ref[...] = jnp.zeros_like(dv_ref)

        qv, kv_, vv, dov = q_ref[...], k_ref[...], v_ref[...], do_ref[...]
        p = jnp.exp(_dot(qv, kv_, "nt") * scale - lse_ref[0])
        delta = jnp.sum(dov * o_ref[...], axis=-1, keepdims=True)
        ds = p * (_dot(dov, vv, "nt") - delta)
        dq_ref[...] = (_dot(ds, kv_) * scale).astype(dq_ref.dtype)
        dk_ref[...] += _dot(ds, qv, "tn") * scale
        dv_ref[...] += _dot(p, dov, "tn")

    tile = pl.BlockSpec((MEM_T, MEM_HEAD_DIM), lambda h, i: (i, h))
    kvo = pl.BlockSpec((N_MEM, MEM_HEAD_DIM), lambda h, i: (0, h))
    return pl.pallas_call(
        body, name="mem_attn_bwd", grid=(MEM_HEADS, S // MEM_T),
        in_specs=[pl.BlockSpec((MEM_T, MEM_HEAD_DIM), lambda h, i: (i, QM_BLK + h)),
                  pl.BlockSpec((N_MEM, MEM_HEAD_DIM), lambda h, i: (0, h)),
                  pl.BlockSpec((N_MEM, MEM_HEAD_DIM), lambda h, i: (0, MEM_HEADS + h)),
                  tile, pl.BlockSpec((1, MEM_T, 1), lambda h, i: (h, i, 0)), tile],
        out_specs=[tile, kvo, kvo],
        out_shape=[jax.ShapeDtypeStruct((S, MEM_WIDTH), MXU_DTYPE), jax.ShapeDtypeStruct((N_MEM, MEM_WIDTH), F32),
                   jax.ShapeDtypeStruct((N_MEM, MEM_WIDTH), F32)],
        compiler_params=_cparams(("parallel", "arbitrary")),
    )(proj, kv, kv, om, lse, dom)


MIX_BM = 1024
MIX_BN = 256
GATES_BLK = C_GATES // MIX_BN


def _mix_specs(j_outer):
    ix = (lambda f: (lambda j, i: f(i, j))) if j_outer else (lambda f: f)
    act = lambda width: pl.BlockSpec((MIX_BM, width), ix(lambda i, j: (i, 0)))
    wgt = lambda width: pl.BlockSpec((width, MIX_BN), ix(lambda i, j: (0, j)))
    gate = lambda b: pl.BlockSpec((MIX_BM, MIX_BN), ix(lambda i, j: (i, GATES_BLK + 4 * b + j)))
    bias = lambda b: pl.BlockSpec((1, MIX_BN), ix(lambda i, j: (0, 4 * b + j)))
    tile = pl.BlockSpec((MIX_BM, MIX_BN), ix(lambda i, j: (i, j)))
    return act, wgt, gate, bias, tile


def _mix_fwd(z_lru, o_dil, om, w_lru, w_dil, w_mem, proj, b_gate, *, S):
    act, wgt, gate, bias, tile = _mix_specs(False)

    def body(zl, od, mo, wl, wd, wm, g0, g1, g2, b0, b1, b2, o_ref):
        acc = jax.nn.sigmoid(g0[...] + b0[...]) * _dot(zl[...], wl[...])
        acc += jax.nn.sigmoid(g1[...] + b1[...]) * _dot(od[...], wd[...])
        acc += jax.nn.sigmoid(g2[...] + b2[...]) * _dot(mo[...], wm[...])
        o_ref[...] = acc.astype(o_ref.dtype)

    return pl.pallas_call(
        body, name="mix_fwd", grid=(S // MIX_BM, D_MODEL // MIX_BN),
        in_specs=[act(D_RNN), act(256), act(MEM_WIDTH), wgt(D_RNN), wgt(256), wgt(MEM_WIDTH),
                  gate(0), gate(1), gate(2), bias(0), bias(1), bias(2)],
        out_specs=tile, out_shape=jax.ShapeDtypeStruct((S, D_MODEL), MXU_DTYPE),
        compiler_params=_cparams(("parallel", "parallel")),
    )(z_lru, o_dil, om, w_lru, w_dil, w_mem, proj, proj, proj, b_gate, b_gate, b_gate)


def _mix_bwd(dmerged, z_lru, o_dil, om, w_lru, w_dil, w_mem, proj, b_gate, *, S):
    act, wgt, gate, bias, tile = _mix_specs(True)

    def body(dm, zl, od, mo, wl, wd, wm, g0, g1, g2, b0, b1, b2,
             dg0, dg1, dg2, dy0, dy1, dy2, db0, db1, db2):
        @pl.when(pl.program_id(1) == 0)
        def _():
            for r in (db0, db1, db2):
                r[...] = jnp.zeros_like(r)

        dmv = dm[...]
        for act_ref, w_ref, g_ref, b_ref, dg_ref, dy_ref, db_ref in (
                (zl, wl, g0, b0, dg0, dy0, db0), (od, wd, g1, b1, dg1, dy1, db1), (mo, wm, g2, b2, dg2, dy2, db2)):
            y = _dot(act_ref[...], w_ref[...])
            gt = jax.nn.sigmoid(g_ref[...] + b_ref[...])
            dgate = dmv * y * gt * (1.0 - gt)
            dg_ref[...] = dgate.astype(dg_ref.dtype)
            dy_ref[...] = (dmv * gt).astype(dy_ref.dtype)
            db_ref[...] += _colsum(dgate)

    big = jax.ShapeDtypeStruct((S, D_MODEL), MXU_DTYPE)
    vec = jax.ShapeDtypeStruct((1, D_MODEL), F32)
    vspec = pl.BlockSpec((1, MIX_BN), lambda j, i: (0, j))
    return pl.pallas_call(
        body, name="mix_bwd", grid=(D_MODEL // MIX_BN, S // MIX_BM),
        in_specs=[tile, act(D_RNN), act(256), act(MEM_WIDTH), wgt(D_RNN), wgt(256), wgt(MEM_WIDTH),
                  gate(0), gate(1), gate(2), bias(0), bias(1), bias(2)],
        out_specs=[tile] * 6 + [vspec] * 3, out_shape=[big] * 6 + [vec] * 3,
        compiler_params=_cparams(("parallel", "arbitrary")),
    )(dmerged, z_lru, o_dil, om, w_lru, w_dil, w_mem, proj, proj, proj, b_gate, b_gate, b_gate)


def _adamw_math(w, g, m, v):
    m = ADAM_B1 * m + (1.0 - ADAM_B1) * g
    v = ADAM_B2 * v + (1.0 - ADAM_B2) * (g * g)
    m_hat = m / (1.0 - ADAM_B1 ** ADAM_STEP)
    v_hat = v / (1.0 - ADAM_B2 ** ADAM_STEP)
    delta = -ADAM_LR * (m_hat / (jnp.sqrt(v_hat) + ADAM_EPS) + ADAM_WD * w)
    return delta, m, v


def _adamw_landed(w, own, land, m, v, *, name, col_blk=0, prev=None):
    R = w.shape[0]
    n_parts, C = land.shape[0], land.shape[2]
    br = next(d for d in (256, 464, 128) if R % d == 0)
    tile = pl.BlockSpec((br, C), lambda i: (i, col_blk))
    part = pl.BlockSpec((br, C), lambda i: (i, 0))
    n_prev = 0 if prev is None else 4

    def body(w_ref, o_ref, l_ref, m_ref, v_ref, *rest):
        g_ref, d_ref, nm_ref, nv_ref = rest[n_prev:]
        g = o_ref[...].astype(F32)
        for p in range(n_parts):
            g = g + l_ref[p].astype(F32)
        d, nm, nv = _adamw_math(w_ref[...], g, m_ref[...], v_ref[...])
        g_ref[...] = g
        d_ref[...] = d
        nm_ref[...] = nm
        nv_ref[...] = nv

    return pl.pallas_call(
        body, name=name, grid=(R // br,),
        in_specs=[tile, part, pl.BlockSpec((n_parts, br, C), lambda i: (0, i, 0)), tile, tile]
        + [pl.BlockSpec(memory_space=pl.ANY)] * n_prev,
        out_specs=[tile] * 4, out_shape=[jax.ShapeDtypeStruct(w.shape, F32)] * 4,
        input_output_aliases={5 + j: j for j in range(n_prev)},
        compiler_params=_cparams(("parallel",)),
    )(w, own, land, m, v, *(prev or ()))


def _adamw_plain(w, g, m, v, *, name):
    def body(w_ref, g_ref, m_ref, v_ref, d_ref, nm_ref, nv_ref):
        d, nm, nv = _adamw_math(w_ref[...], g_ref[...], m_ref[...], v_ref[...])
        d_ref[...] = d
        nm_ref[...] = nm
        nv_ref[...] = nv

    return pl.pallas_call(
        body, name=name, out_shape=[jax.ShapeDtypeStruct(w.shape, F32)] * 3, compiler_params=_cparams(),
    )(w, g, m, v)


def _my_pos():
    return lax.axis_index("x"), lax.axis_index("y"), lax.axis_index("c")


def _dev_index(p):
    return 4 * p[0] + 2 * p[1] + p[2]


def _all_gather(shards):
    n = len(shards)
    hbm = pl.BlockSpec(memory_space=pl.ANY)

    def body(*refs):
        ins, outs = refs[:n], refs[n:2 * n]
        send_sems, recv_sems, local_sems = refs[2 * n:]
        x, y, c = _my_pos()
        me, sibling = (x, y, c), (x, y, 1 - c)
        chips = [(1 - x, y), (x, 1 - y), (1 - x, 1 - y)]

        def copy(a, k, block, to, src=None):
            dst = outs[a].at[_dev_index(block)]
            return pltpu.make_async_remote_copy(
                src_ref=dst if src is None else src, dst_ref=dst,
                send_sem=send_sems.at[a, k], recv_sem=recv_sems.at[a, k], device_id=to, device_id_type=MESH)

        mine = [pltpu.make_async_copy(ins[a], outs[a].at[_dev_index(me)], local_sems.at[a]) for a in range(n)]
        for cp in mine:
            cp.start()
        first = []
        for a in range(n):
            first.append(copy(a, 0, me, sibling, src=ins[a]))
            first += [copy(a, 1 + j, me, (*chip, c), src=ins[a]) for j, chip in enumerate(chips)]
        for cp in first:
            cp.start()
        passed = []
        for j, chip in enumerate(chips):
            for a in range(n):
                copy(a, 1 + j, (*chip, c), me).wait_recv()
                fwd = copy(a, 4 + j, (*chip, c), sibling)
                fwd.start()
                passed.append(fwd)
        for a in range(n):
            copy(a, 0, sibling, me).wait_recv()
        for j, chip in enumerate(chips):
            for a in range(n):
                copy(a, 4 + j, (*chip, 1 - c), me).wait_recv()
        for cp in first + passed:
            cp.wait_send()
        for cp in mine:
            cp.wait()

    return pl.pallas_call(
        body, name="all_gather_weights",
        in_specs=[hbm] * n, out_specs=[hbm] * n,
        out_shape=[jax.ShapeDtypeStruct((N_DEV,) + s.shape, s.dtype) for s in shards],
        scratch_shapes=[pltpu.SemaphoreType.DMA((n, 7)), pltpu.SemaphoreType.DMA((n, 7)), pltpu.SemaphoreType.DMA((n,))],
        compiler_params=pltpu.CompilerParams(has_side_effects=True),
    )(*shards)


def _peers(me):
    x, y, c = me
    out = []
    for k in range(1, 8):
        fx, fy, fc = (k >> 2) & 1, (k >> 1) & 1, k & 1
        out.append((k - 1, (1 - x if fx else x, 1 - y if fy else y, 1 - c if fc else c)))
    return out


HBM_SPEC = pl.BlockSpec(memory_space=pltpu.HBM)
SEM_SPEC = pl.BlockSpec(memory_space=pltpu.SEMAPHORE)
DATAFLOW_EFFECT = pltpu.SideEffectType.DATAFLOW_SIDE_EFFECTING


def _gather_refs(src, land, me, peer, k):
    return src, land.at[_dev_index(me)]


def _scatter_refs(src, land, me, peer, k):
    return src.at[_dev_index(peer)], land.at[k]


def _push_start(srcs, land_shapes, refs_of, name, after=()):
    n, n_after = len(srcs), len(after)

    def body(*refs):
        ins, lands = refs[:n], refs[n:2 * n]
        send_sems, recv_sems, token = refs[2 * n + n_after], refs[2 * n + n_after + 1], refs[-1]
        me = _my_pos()
        for k, peer in _peers(me):
            for a in range(n):
                src, dst = refs_of(ins[a], lands[a], me, peer, k)
                pltpu.make_async_remote_copy(src_ref=src, dst_ref=dst, send_sem=send_sems.at[7 * a + k],
                                             recv_sem=recv_sems.at[7 * a + k], device_id=peer, device_id_type=MESH).start()
        token[...] = jnp.zeros_like(token)

    lands = [lax.empty(shp, s.dtype) for shp, s in zip(land_shapes, srcs)]
    hbm = lambda a: pltpu.with_memory_space_constraint(a, pltpu.HBM)
    res = pl.pallas_call(
        body, name=name,
        out_shape=(pltpu.SemaphoreType.DMA((7 * n,)), pltpu.SemaphoreType.DMA((7 * n,)),
                   *[pltpu.HBM(s.shape, s.dtype) for s in srcs], *[pltpu.HBM(l.shape, l.dtype) for l in lands],
                   jax.ShapeDtypeStruct((8, 128), F32)),
        in_specs=[HBM_SPEC] * (2 * n) + [pl.BlockSpec(memory_space=pl.ANY)] * n_after,
        out_specs=(SEM_SPEC, SEM_SPEC, *[HBM_SPEC] * (2 * n), pl.BlockSpec(memory_space=pltpu.VMEM)),
        input_output_aliases={i: 2 + i for i in range(2 * n)},
        compiler_params=pltpu.CompilerParams(has_side_effects=DATAFLOW_EFFECT),
    )(*[hbm(s) for s in srcs], *[hbm(l) for l in lands], *after)
    return dict(sems=(res[0], res[1]), srcs=list(res[2:2 + n]), lands=list(res[2 + n:2 + 2 * n]), token=res[-1], n=n,
                refs_of=refs_of, name=name)


def _push_wait(started, after):
    n, refs_of = started["n"], started["refs_of"]
    after = list(after) if isinstance(after, (list, tuple)) else [after]

    def body(*refs):
        ins, lands = refs[:n], refs[n:2 * n]
        send_sems, recv_sems = refs[2 * n], refs[2 * n + 1]
        me = _my_pos()
        for k, peer in _peers(me):
            for a in range(n):
                src, dst = refs_of(ins[a], lands[a], me, peer, k)
                cp = pltpu.make_async_remote_copy(src_ref=src, dst_ref=dst, send_sem=send_sems.at[7 * a + k],
                                                  recv_sem=recv_sems.at[7 * a + k], device_id=peer, device_id_type=MESH)
                cp.wait_send()
                cp.wait_recv()

    arrs = started["srcs"] + started["lands"]
    res = pl.pallas_call(
        body, name=started["name"].replace("start", "wait"),
        out_shape=tuple(pltpu.HBM(a.shape, a.dtype) for a in arrs),
        in_specs=[HBM_SPEC] * (2 * n) + [SEM_SPEC, SEM_SPEC] + [pl.BlockSpec(memory_space=pl.ANY)] * len(after),
        out_specs=tuple([HBM_SPEC] * (2 * n)),
        input_output_aliases={i: i for i in range(2 * n)},
        compiler_params=pltpu.CompilerParams(has_side_effects=DATAFLOW_EFFECT),
    )(*arrs, *started["sems"], *after)
    return list(res[n:2 * n])


def _sum_slots(slots):
    def body(in_ref, out_ref):
        acc = in_ref[0]
        for d in range(1, N_DEV):
            acc = acc + in_ref[d]
        out_ref[...] = acc

    return pl.pallas_call(body, name="sum_small", out_shape=jax.ShapeDtypeStruct(slots.shape[1:], F32),
                          compiler_params=_cparams())(slots)


def _adamw_many(ws, gs, ms, vs):
    n = len(ws)

    def body(*refs):
        for i in range(n):
            w_ref, g_ref, m_ref, v_ref = (refs[j * n + i] for j in range(4))
            d_, nm, nv = _adamw_math(w_ref[...], g_ref[...], m_ref[...], v_ref[...])
            for j, val in enumerate((d_, nm, nv)):
                refs[(4 + j) * n + i][...] = val

    res = pl.pallas_call(body, name="adamw_small", out_shape=[jax.ShapeDtypeStruct(w_.shape, F32) for w_ in ws] * 3,
                         compiler_params=_cparams())(*ws, *gs, *ms, *vs)
    return [(res[i], res[n + i], res[2 * n + i]) for i in range(n)]


def _local_step(x, mem, tgt, W, P, late_weights, send_grads, reduce_small, tie0):
    S = x.shape[0]
    W = dict(W)
    h = _rmsnorm_fwd(x, P["g_mix"] + tie0, rows=S, name="norm_mix")
    proj = _matmul(h, W["w_in_t"], M=S, N=D_IN, K=D_MODEL, mode="nt", bm=512, bn=D_IN // 2, bk=D_MODEL, name="mm_in",
                   j_outer=True)

    wa_bd, wx_bd = _mx(_block_diag(P["w_rg_a"])), _mx(_block_diag(P["w_rg_x"]))
    lru_args = (W["conv_w"], P["conv_b"].reshape(1, -1), wa_bd, wx_bd, P["b_rg_a"].reshape(1, -1),
                P["b_rg_x"].reshape(1, -1), P["lru_lambda"].reshape(1, -1))
    hl, z_lru, a_lru, mult_lru = _lru_fwd(proj, *lru_args, S=S)

    buckets = _dil_buckets()
    bias = _dil_bias(P["rel_bias"], buckets)
    group_out = [_dilated_fwd(proj, bias, g, S=S) for g in range(len(DIL_GROUPS))]
    o_dil, o_dil_m, lse_dil = _dilated_merge([o for o, _ in group_out], [l for _, l in group_out], S=S)

    W.update(late_weights("branch", [o_dil, z_lru]))
    mem_n = _rmsnorm_fwd(mem, P["g_mem"], rows=N_MEM, name="norm_mem")
    kv = _matmul(mem_n, W["w_mem_kv"], M=N_MEM, N=2 * MEM_WIDTH, K=D_MODEL, mode="nn", bm=N_MEM, bn=512, bk=D_MODEL,
                 name="mm_kv")
    om, om_m, lse_mem = _mem_attn_fwd(proj, kv, S=S)
    b_gate = P["b_gate"].reshape(1, -1)
    merged = _mix_fwd(z_lru, o_dil_m, om_m, W["w_lru_out"], W["w_dil_out"], W["w_mem_out"], proj, b_gate, S=S)
    g_mlp, g_final, g_mix = (P[n].reshape(1, D_MODEL) for n in ("g_mlp", "g_final", "g_mix"))
    x1, hm = _matmul_rows(merged, W["w_out"], M=S, K=D_MODEL, mode="nn", bm=512, name="mm_out",
                          row_fn=_residual_then_norm, out_dtypes=(F32, MXU_DTYPE), tiles=[x], vecs=[g_mlp])
    W.update(late_weights("mlp", [hm]))

    def relu2(acc):
        rl = jnp.maximum(acc, 0.0)
        return rl * rl, rl

    act, relu_u = _matmul(hm, W["w_mlp_in_t"], M=S, N=D_FF, K=D_MODEL, mode="nt", bm=1024, bn=1024, bk=D_MODEL,
                          name="mm_mlp_in", out_dtypes=(MXU_DTYPE, MXU_DTYPE), epilogue=relu2, j_outer=True)
    dx2, dx2_m, loss, dg_final = _matmul_rows(
        act, W["w_mlp_out"], M=S, K=D_FF, mode="nn", bm=512, name="mm_mlp_out", row_fn=_residual_then_loss,
        out_dtypes=(F32, MXU_DTYPE), tiles=[x1, tgt], vecs=[g_final], acc_widths=(1, D_MODEL))

    G, Gs = {}, {}
    Gs["g_final"] = dg_final
    dw = dict(mode="tn", K=S, bk=S, out_dtypes=(MXU_DTYPE,))
    G["w_mlp_out"] = _matmul(act, dx2_m, M=D_FF, N=D_MODEL, bm=512, bn=D_MODEL, name="mm_dw_mlp_out",
                             parts=("rows", D_FF // N_DEV), **dw)
    du = _matmul(dx2_m, W["w_mlp_out"], M=S, N=D_FF, K=D_MODEL, mode="nt", bm=1024, bn=1024, bk=D_MODEL, name="mm_du",
                 out_dtypes=(MXU_DTYPE,), epilogue=lambda acc, rl: (acc * (2.0 * rl.astype(F32)),),
                 extras=[(relu_u, (0, 0))], j_outer=True)
    G["w_mlp_in"] = _matmul(hm, du, M=D_MODEL, N=D_FF, bm=D_MODEL, bn=512, name="mm_dw_mlp_in",
                            parts=("cols", D_FF // N_DEV), **dw)
    tie1 = send_grads({n: G.pop(n) for n in ("w_mlp_out", "w_mlp_in")})
    dx1, dx1_m, Gs["g_mlp"] = _matmul_rows(
        du, W["w_mlp_in_t"], M=S, K=D_FF, mode="nn", bm=512, name="mm_dhm", row_fn=_norm_bwd_then_residual(2),
        out_dtypes=(F32, MXU_DTYPE), tiles=[x1, dx2], vecs=[g_mlp], acc_widths=(D_MODEL,), deps=[tie1])
    G["w_out"] = _matmul(merged, dx1_m, M=D_MODEL, N=D_MODEL, bm=512, bn=D_MODEL, name="mm_dw_out",
                         parts=("rows", D_MODEL // N_DEV), **dw)
    dmerged = _matmul(dx1_m, W["w_out"], M=S, N=D_MODEL, K=D_MODEL, mode="nt", bm=512, bn=D_MODEL, bk=D_MODEL, name="mm_dmerged")
    (dg0, dg1, dg2, dy_lru, dy_dil, dy_mem, db0, db1, db2) = _mix_bwd(
        dmerged, z_lru, o_dil_m, om_m, W["w_lru_out"], W["w_dil_out"], W["w_mem_out"], proj, b_gate, S=S)
    Gs["b_gate0"], Gs["b_gate1"], Gs["b_gate2"] = db0, db1, db2

    G["w_mem_out"] = _matmul(om_m, dy_mem, M=MEM_WIDTH, N=D_MODEL, bm=MEM_WIDTH, bn=D_MODEL, name="mm_dw_mem_out",
                             parts=("cols", D_MODEL // N_DEV), **dw)
    dom = _matmul(dy_mem, W["w_mem_out"], M=S, N=MEM_WIDTH, K=D_MODEL, mode="nt", bm=512, bn=MEM_WIDTH, bk=D_MODEL,
                  name="mm_dom")
    dqm, dk_mem, dv_mem = _mem_attn_bwd(proj, kv, om, lse_mem, dom, S=S)
    dkv = jnp.concatenate([dk_mem, dv_mem], axis=1)
    G["w_mem_kv"] = _matmul(mem_n, dkv, M=D_MODEL, N=2 * MEM_WIDTH, K=N_MEM, mode="tn", bm=D_MODEL, bn=2 * MEM_WIDTH,
                            bk=N_MEM, name="mm_dw_kv", out_dtypes=(MXU_DTYPE,), parts=("rows", D_MODEL // N_DEV))
    dmem_n = _matmul(dkv, W["w_mem_kv"], M=N_MEM, N=D_MODEL, K=2 * MEM_WIDTH, mode="nt", bm=N_MEM, bn=D_MODEL,
                     bk=2 * MEM_WIDTH, name="mm_dmem")
    (Gs["g_mem"],) = _rmsnorm_bwd(mem, P["g_mem"], dmem_n, None, rows=N_MEM, name="norm_mem_bwd", dx_dtypes=())

    G["w_dil_out"] = _matmul(o_dil_m, dy_dil, M=256, N=D_MODEL, bm=256, bn=D_MODEL, name="mm_dw_dil_out",
                             parts=("cols", D_MODEL // N_DEV), **dw)
    do_dil = _matmul(dy_dil, W["w_dil_out"], M=S, N=256, K=D_MODEL, mode="nt", bm=512, bn=256, bk=D_MODEL, name="mm_do_dil")
    G["w_lru_out"] = _matmul(z_lru, dy_lru, M=D_RNN, N=D_MODEL, bm=D_RNN, bn=D_MODEL, name="mm_dw_lru_out",
                             parts=("cols", D_MODEL // N_DEV), **dw)
    dz = _matmul(dy_lru, W["w_lru_out"], M=S, N=D_RNN, K=D_MODEL, mode="nt", bm=512, bn=D_RNN, bk=D_MODEL, name="mm_dz_lru")
    tie2 = send_grads({n: G.pop(n) for n in ("w_out", "w_mem_out", "w_mem_kv", "w_dil_out", "w_lru_out")})
    bias = bias + tie2[0, 0]
    delta = _dilated_delta(do_dil, o_dil, S=S)
    dqkv, dbias = None, []
    for g in range(len(DIL_GROUPS)):
        *dqkv, db_g = _dilated_bwd(proj, do_dil, lse_dil, delta, bias, g, S=S, into=dqkv)
        dbias.append(db_g)
    drel = _dil_bias_bwd(jnp.stack(dbias, axis=0).reshape(len(DIL_GROUPS), DIL_HEADS, SPAN, 2 * SPAN), buckets)
    Gs["rel_bias"] = drel

    dxl, dgl, dcw, dcb, dwa, dwx, dba, dbx, dlam = _lru_bwd(proj, hl, a_lru, mult_lru, dz, *lru_args, S=S)
    Gs["conv_w"], Gs["conv_b"] = dcw, dcb
    Gs["w_rg_a"], Gs["w_rg_x"] = _block_diag_extract(dwa), _block_diag_extract(dwx)
    Gs["b_rg_a"], Gs["b_rg_x"], Gs["lru_lambda"] = dba, dbx, dlam
    Gs["loss"] = loss

    dproj = jnp.concatenate([dxl, dgl] + dqkv + [dqm, dg0, dg1, dg2], axis=1)
    cols = D_MODEL // W_IN_PIECES
    tie = []
    for q in range(W_IN_PIECES):
        dw_q = _matmul(h, dproj, M=cols, N=D_IN, K=S, mode="tn", bm=cols, bn=D_IN // 2, bk=1024, name=f"mm_dw_in_{q}",
                       a_off=(0, q), out_dtypes=(MXU_DTYPE,), parts=("rows_t", D_IN // N_DEV), deps=tie)
        tie = [send_grads({f"w_in_{q}": dw_q})]
    grad_x, Gs["g_mix"] = _matmul_rows(
        dproj, W["w_in_t"], M=S, K=D_IN, mode="nn", bm=256, name="mm_dh", row_fn=_norm_bwd_then_residual(1),
        out_dtypes=(F32,), tiles=[x, dx1], vecs=[g_mix], acc_widths=(D_MODEL,), deps=tie)
    return grad_x, reduce_small(Gs)


BIG = ("w_in", "w_lru_out", "w_dil_out", "w_mem_kv", "w_mem_out", "w_out", "w_mlp_in", "w_mlp_out")
W_IN_PIECES = 2
COL_SHARDED = ("w_lru_out", "w_dil_out", "w_mem_out", "w_mlp_in")
GATHERED_TRANSPOSED = ("w_mlp_in",)
SMALL = ("g_mix", "b_gate", "conv_b", "w_rg_a", "b_rg_a", "w_rg_x", "b_rg_x", "lru_lambda", "rel_bias", "g_mem",
         "g_mlp", "g_final")
WEIGHTS = ("g_mix", "w_in", "b_gate", "conv_w", "conv_b", "w_rg_a", "b_rg_a", "w_rg_x", "b_rg_x", "lru_lambda",
           "w_lru_out", "rel_bias", "w_dil_out", "g_mem", "w_mem_kv", "w_mem_out", "w_out", "g_mlp", "w_mlp_in",
           "w_mlp_out", "g_final")


def _gathered_to_full(name, gathered):
    if name in COL_SHARDED:
        n, r, c = gathered.shape
        return gathered.transpose(1, 0, 2).reshape(r, n * c)
    n, r, c = gathered.shape
    return gathered.reshape(n * r, c)


SMALL_GRADS = (("g_mix", (1, 1024)), ("b_gate0", (1, 1024)), ("b_gate1", (1, 1024)), ("b_gate2", (1, 1024)),
               ("conv_b", (1, 768)), ("w_rg_a", (12, 64, 64)), ("b_rg_a", (1, 768)), ("w_rg_x", (12, 64, 64)),
               ("b_rg_x", (1, 768)), ("lru_lambda", (1, 768)), ("rel_bias", (32, 128)), ("g_mem", (1, 1024)),
               ("g_mlp", (1, 1024)), ("g_final", (1, 1024)), ("conv_w", (4, 768)), ("loss", (1, 1)))


def _pack(parts):
    flat = jnp.concatenate([p.reshape(-1) for p in parts])
    return jnp.pad(flat, (0, (-flat.shape[0]) % 1024)).reshape(-1, 128)


def _unpack(pack, shapes):
    flat = pack.reshape(-1)
    out, off = [], 0
    for shp in shapes:
        size = math.prod(shp)
        out.append(flat[off:off + size].reshape(shp))
        off += size
    return out


def kernel(x, mem, g_mix, w_in, b_gate, conv_w, conv_b, w_rg_a, b_rg_a, w_rg_x, b_rg_x, lru_lambda, w_lru_out, rel_bias, w_dil_out, g_mem, w_mem_kv, w_mem_out, w_out, g_mlp, w_mlp_in, w_mlp_out, g_final, loss_target, m_g_mix, m_w_in, m_b_gate, m_conv_w, m_conv_b, m_w_rg_a, m_b_rg_a, m_w_rg_x, m_b_rg_x, m_lru_lambda, m_w_lru_out, m_rel_bias, m_w_dil_out, m_g_mem, m_w_mem_kv, m_w_mem_out, m_w_out, m_g_mlp, m_w_mlp_in, m_w_mlp_out, m_g_final, v_g_mix, v_w_in, v_b_gate, v_conv_w, v_conv_b, v_w_rg_a, v_b_rg_a, v_w_rg_x, v_b_rg_x, v_lru_lambda, v_w_lru_out, v_rel_bias, v_w_dil_out, v_g_mem, v_w_mem_kv, v_w_mem_out, v_w_out, v_g_mlp, v_w_mlp_in, v_w_mlp_out, v_g_final):
    w = dict(g_mix=g_mix, w_in=w_in, b_gate=b_gate, conv_w=conv_w, conv_b=conv_b, w_rg_a=w_rg_a, b_rg_a=b_rg_a,
             w_rg_x=w_rg_x, b_rg_x=b_rg_x, lru_lambda=lru_lambda, w_lru_out=w_lru_out, rel_bias=rel_bias,
             w_dil_out=w_dil_out, g_mem=g_mem, w_mem_kv=w_mem_kv, w_mem_out=w_mem_out, w_out=w_out, g_mlp=g_mlp,
             w_mlp_in=w_mlp_in, w_mlp_out=w_mlp_out, g_final=g_final)
    m = dict(g_mix=m_g_mix, w_in=m_w_in, b_gate=m_b_gate, conv_w=m_conv_w, conv_b=m_conv_b, w_rg_a=m_w_rg_a,
             b_rg_a=m_b_rg_a, w_rg_x=m_w_rg_x, b_rg_x=m_b_rg_x, lru_lambda=m_lru_lambda, w_lru_out=m_w_lru_out,
             rel_bias=m_rel_bias, w_dil_out=m_w_dil_out, g_mem=m_g_mem, w_mem_kv=m_w_mem_kv, w_mem_out=m_w_mem_out,
             w_out=m_w_out, g_mlp=m_g_mlp, w_mlp_in=m_w_mlp_in, w_mlp_out=m_w_mlp_out, g_final=m_g_final)
    v = dict(g_mix=v_g_mix, w_in=v_w_in, b_gate=v_b_gate, conv_w=v_conv_w, conv_b=v_conv_b, w_rg_a=v_w_rg_a,
             b_rg_a=v_b_rg_a, w_rg_x=v_w_rg_x, b_rg_x=v_b_rg_x, lru_lambda=v_lru_lambda, w_lru_out=v_w_lru_out,
             rel_bias=v_rel_bias, w_dil_out=v_w_dil_out, g_mem=v_g_mem, w_mem_kv=v_w_mem_kv, w_mem_out=v_w_mem_out,
             w_out=v_w_out, g_mlp=v_g_mlp, w_mlp_in=v_w_mlp_in, w_mlp_out=v_w_mlp_out, g_final=v_g_final)

    my_idx = _dev_index(_my_pos())

    g_in, g_cw = _all_gather([_mx(w["w_in"].T), w["conv_w"]])
    W = {"w_in_t": g_in.reshape(D_IN, D_MODEL), "conv_w": g_cw.transpose(1, 0, 2).reshape(CONV_WIDTH, D_RNN)}
    late, order_after = {}, [g_in]
    for group, names in (("branch", ("w_mem_kv", "w_lru_out", "w_dil_out", "w_mem_out", "w_out")),
                         ("mlp", ("w_mlp_in", "w_mlp_out"))):
        shards = [_mx(w[n].T if n in GATHERED_TRANSPOSED else w[n]) for n in names]
        started = _push_start(shards, [(N_DEV,) + s.shape for s in shards], _gather_refs, f"gather_{group}_start",
                              after=order_after)
        late[group] = (names, shards, started)
        order_after = [started["token"]]
    P = {n: w[n] for n in SMALL}

    def late_weights(group, after):
        names, shards, started = late[group]
        out = {}
        for n, land, own in zip(names, _push_wait(started, after), shards):
            full = lax.dynamic_update_index_in_dim(land, own, my_idx, 0)
            if n in GATHERED_TRANSPOSED:
                out[n + "_t"] = full.reshape(-1, full.shape[2])
            else:
                out[n] = _gathered_to_full(n, full)
        return out

    sent, small = [], {}

    def send_grads(gs):
        names = list(gs)
        parts = [gs[n] for n in names]
        own = [lax.dynamic_index_in_dim(p, my_idx, 0, keepdims=False) for p in parts]
        started = _push_start(parts, [(N_DEV - 1,) + p.shape[1:] for p in parts], _scatter_refs,
                              f"scatter{len(sent)}_start")
        sent.append((names, own, started))
        return started["token"]

    def reduce_small(gs):
        small["pack"] = _pack([gs[n] for n, _ in SMALL_GRADS])
        small["started"] = _push_start([small["pack"]], [(N_DEV,) + small["pack"].shape], _gather_refs, "small_start")
        return small["started"]["token"]

    grad_x, last_token = _local_step(x[0], mem[0], loss_target[0], W, P, late_weights, send_grads, reduce_small,
                                     late["mlp"][2]["token"][0, 0])

    grads, deltas, new_m, new_v = {}, {}, {}, {}
    after = last_token
    for names, own, started in sent[:-W_IN_PIECES]:
        for n, o, land in zip(names, own, _push_wait(started, after)):
            grads[n], deltas[n], new_m[n], new_v[n] = _adamw_landed(w[n], o, land, m[n], v[n], name=f"adamw_{n}")
            after = deltas[n]
    prev = None
    for q, (names, own, started) in enumerate(sent[-W_IN_PIECES:]):
        (land,) = _push_wait(started, after)
        prev = _adamw_landed(w["w_in"].T, own[0], land, m["w_in"].T, v["w_in"].T, name=f"adamw_{names[0]}",
                             col_blk=q, prev=prev)
        after = prev[1]
    grads["w_in"], deltas["w_in"], new_m["w_in"], new_v["w_in"] = [t.T for t in prev]
    (small_land,) = _push_wait(small["started"], after)
    total = _sum_slots(lax.dynamic_update_index_in_dim(small_land, small["pack"], my_idx, 0))
    summed = dict(zip([n for n, _ in SMALL_GRADS], _unpack(total, [shp for _, shp in SMALL_GRADS])))
    summed["b_gate"] = jnp.concatenate([summed.pop(f"b_gate{b}") for b in range(3)], axis=1)
    summed["rel_bias"] = summed["rel_bias"][:, :3 * DIL_HEADS]
    for n in SMALL:
        grads[n] = summed[n].reshape(w[n].shape)
    small_updates = _adamw_many([w[n] for n in SMALL], [grads[n] for n in SMALL], [m[n] for n in SMALL],
                                [v[n] for n in SMALL])
    for n, (d_, nm_, nv_) in zip(SMALL, small_updates):
        deltas[n], new_m[n], new_v[n] = d_, nm_, nv_
    conv_w_sum, loss_sum = summed["conv_w"], summed["loss"]
    cw_cols = D_RNN // N_DEV
    grads["conv_w"] = lax.dynamic_slice(conv_w_sum, (0, my_idx * cw_cols), (CONV_WIDTH, cw_cols))
    deltas["conv_w"], new_m["conv_w"], new_v["conv_w"] = _adamw_plain(
        w["conv_w"], grads["conv_w"], m["conv_w"], v["conv_w"], name="adamw_conv_w")

    return (loss_sum.reshape(()), grad_x[None], *[grads[n] for n in WEIGHTS], *[deltas[n] for n in WEIGHTS],
            *[new_m[n] for n in WEIGHTS], *[new_v[n] for n in WEIGHTS])
```

```python
import functools
import math

import jax
import jax.numpy as jnp
from jax import lax
from jax.experimental import pallas as pl
from jax.experimental.pallas import tpu as pltpu

F32 = jnp.float32
MXU_DTYPE = jnp.bfloat16
VMEM_LIMIT_BYTES = 56 * 1024 * 1024
N_DEV = 8

D_MODEL = 1024
N_MEM = 256
MEM_HEADS = 4
MEM_HEAD_DIM = 128
MEM_WIDTH = 512
D_RNN = 768
LRU_BLOCK = 64
N_LRU_BLOCKS = 12
LRU_GROUP = 256
N_LRU_GROUPS = 3
CONV_WIDTH = 4
LRU_C = 8.0
DIL_GROUPS = ((128, 1), (512, 4), (2048, 16))
SPAN = 128
DIL_HEADS = 4
DIL_HEAD_DIM = 64
NUM_BUCKETS = 32
MAX_DISTANCE = 2048
D_FF = 4096
D_IN = 7424
EPS = 1e-6
NEG = -1e30
C_XL, C_GATE, C_QKV, C_QM, C_GATES = 0, 768, 1536, 3840, 4352

ADAM_LR = 0.001
ADAM_B1 = 0.9
ADAM_B2 = 0.999
ADAM_EPS = 1e-08
ADAM_WD = 0.01
ADAM_STEP = 10

MESH = pl.DeviceIdType.MESH
GELU_K = math.sqrt(2.0 / math.pi)


def _cparams(sem=None):
    kw = dict(vmem_limit_bytes=VMEM_LIMIT_BYTES)
    if sem is not None:
        kw["dimension_semantics"] = sem
    return pltpu.CompilerParams(**kw)


def _mx(v):
    return v.astype(MXU_DTYPE)


def _dot(a, b, mode="nn"):
    dims = {"nn": (((1,), (0,)), ((), ())), "nt": (((1,), (1,)), ((), ())), "tn": (((0,), (0,)), ((), ()))}[mode]
    return lax.dot_general(_mx(a), _mx(b), dims, preferred_element_type=F32)


def _colsum(v):
    return jnp.sum(v, axis=0, keepdims=True)


def _matmul(a, b, *, M, N, K, mode, bm, bn, bk, name, out_dtypes=(F32,), epilogue=None, extras=(),
            a_off=(0, 0), b_off=(0, 0), j_outer=False, deps=(), parts=None):
    assert M % bm == 0 and N % bn == 0 and K % bk == 0, (name, M, N, K, bm, bn, bk)
    nm, nn, nk = M // bm, N // bn, K // bk

    def ij(f):
        if j_outer:
            return lambda j, i, k: f(i, j, k)
        return f

    if mode == "tn":
        a_spec = pl.BlockSpec((bk, bm), ij(lambda i, j, k: (k + a_off[0], i + a_off[1])))
    else:
        a_spec = pl.BlockSpec((bm, bk), ij(lambda i, j, k: (i + a_off[0], k + a_off[1])))
    if mode == "nt":
        b_spec = pl.BlockSpec((bn, bk), ij(lambda i, j, k: (j + b_off[0], k + b_off[1])))
    else:
        b_spec = pl.BlockSpec((bk, bn), ij(lambda i, j, k: (k + b_off[0], j + b_off[1])))
    ex_specs = [pl.BlockSpec((bm, bn), ij(functools.partial(lambda i, j, k, o: (i + o[0], j + o[1]), o=off)))
                for _, off in extras]
    if parts is None:
        out_dims = (M, N)
        out_spec = pl.BlockSpec((bm, bn), ij(lambda i, j, k: (i, j)))
    elif parts[0] == "rows":
        r = parts[1]
        assert bm % r == 0
        out_dims = (M // r, r, N)
        out_spec = pl.BlockSpec((bm // r, r, bn), ij(lambda i, j, k: (i, 0, j)))
    elif parts[0] == "rows_t":
        r = parts[1]
        assert bn % r == 0
        out_dims = (N // r, r, M)
        out_spec = pl.BlockSpec((bn // r, r, bm), ij(lambda i, j, k: (j, 0, i)))
    else:
        c = parts[1]
        assert bn % c == 0
        out_dims = (N // c, M, c)
        out_spec = pl.BlockSpec((bn // c, bm, c), ij(lambda i, j, k: (j, i, 0)))
    n_ex, n_out, n_dep = len(extras), len(out_dtypes), len(deps)

    def body(*refs):
        a_ref, b_ref = refs[0], refs[1]
        ex = refs[2:2 + n_ex]
        outs = refs[2 + n_ex + n_dep:2 + n_ex + n_dep + n_out]
        part = _dot(a_ref[...], b_ref[...], mode)

        def finish(acc):
            vals = epilogue(acc, *[e[...] for e in ex]) if epilogue is not None else (acc,)
            for o, v in zip(outs, vals):
                if parts is not None and parts[0] == "rows_t":
                    v = v.T
                v = v.astype(o.dtype)
                if parts is None:
                    o[...] = v
                elif parts[0] in ("rows", "rows_t"):
                    for ch in range(v.shape[0] // parts[1]):
                        o[ch] = v[ch * parts[1]:(ch + 1) * parts[1], :]
                else:
                    for ch in range(bn // parts[1]):
                        o[ch] = v[:, ch * parts[1]:(ch + 1) * parts[1]]

        if nk == 1:
            finish(part)
        else:
            acc_ref = refs[-1]
            k = pl.program_id(2)

            @pl.when(k == 0)
            def _():
                acc_ref[...] = part

            @pl.when(k > 0)
            def _():
                acc_ref[...] += part

            @pl.when(k == nk - 1)
            def _():
                finish(acc_ref[...])

    grid = (nn, nm, nk) if j_outer else (nm, nn, nk)
    res = pl.pallas_call(
        body, name=name, grid=grid,
        in_specs=[a_spec, b_spec] + ex_specs + [pl.BlockSpec(memory_space=pl.ANY)] * n_dep,
        out_specs=[out_spec] * n_out,
        out_shape=[jax.ShapeDtypeStruct(out_dims, dt) for dt in out_dtypes],
        scratch_shapes=[pltpu.VMEM((bm, bn), F32)] if nk > 1 else [],
        compiler_params=_cparams(("parallel", "parallel", "arbitrary")),
    )(a, b, *[e for e, _ in extras], *deps)
    return res[0] if n_out == 1 else res


def _matmul_rows(a, b, *, M, K, mode, bm, name, row_fn, out_dtypes, tiles=(), vecs=(), acc_widths=(), deps=()):
    N = D_MODEL
    assert M % bm == 0
    segs = list(a) if isinstance(a, (list, tuple)) else [a]
    widths = [s_.shape[1] for s_ in segs]
    assert sum(widths) == K and (len(segs) == 1 or mode == "nn")
    n_s, n_t, n_v, n_o, n_a, n_d = len(segs), len(tiles), len(vecs), len(out_dtypes), len(acc_widths), len(deps)
    row = pl.BlockSpec((bm, N), lambda i: (i, 0))
    b_shape = (K, N) if mode == "nn" else (N, K)

    def body(*refs):
        b_ref = refs[n_s]
        ins = refs[n_s + 1:n_s + 1 + n_t + n_v]
        outs = refs[n_s + 1 + n_t + n_v + n_d:n_s + 1 + n_t + n_v + n_d + n_o]
        accs = refs[n_s + 1 + n_t + n_v + n_d + n_o:]
        if n_s == 1:
            acc = _dot(refs[0][...], b_ref[...], mode)
        else:
            acc, k0 = None, 0
            for a_ref, w_ in zip(refs[:n_s], widths):
                part = _dot(a_ref[...], b_ref[k0:k0 + w_, :])
                acc = part if acc is None else acc + part
                k0 += w_
        tile_vals, partials = row_fn(acc, *[r[...] for r in ins])
        for o, val in zip(outs, tile_vals):
            o[...] = val.astype(o.dtype)
        for o, val in zip(accs, partials):
            @pl.when(pl.program_id(0) == 0)
            def _(o=o):
                o[...] = jnp.zeros_like(o)

            o[...] += val

    res = pl.pallas_call(
        body, name=name, grid=(M // bm,),
        in_specs=[pl.BlockSpec((bm, w_), lambda i: (i, 0)) for w_ in widths] + [pl.BlockSpec(b_shape, lambda i: (0, 0))]
        + [row] * n_t + [pl.BlockSpec((1, N), lambda i: (0, 0))] * n_v + [pl.BlockSpec(memory_space=pl.ANY)] * n_d,
        out_specs=[row] * n_o + [pl.BlockSpec((1, w_), lambda i: (0, 0)) for w_ in acc_widths],
        out_shape=[jax.ShapeDtypeStruct((M, N), dt) for dt in out_dtypes]
        + [jax.ShapeDtypeStruct((1, w_), F32) for w_ in acc_widths],
        compiler_params=_cparams(("arbitrary",) if n_a else ("parallel",)),
    )(*segs, b, *tiles, *vecs, *deps)
    return res


def _dw_in_t_half(h, pieces, q, half, *, S, name, into=None, deps=(), bk=1024):
    half_w, cols = D_IN // 2, D_MODEL // W_IN_PIECES
    lo, hi = half * half_w, (half + 1) * half_w
    use, c0 = [], 0
    for p in pieces:
        w_ = p.shape[1]
        a0, a1 = max(lo, c0), min(hi, c0 + w_)
        if a1 > a0:
            use.append((p, a0 - c0, a1 - a0))
        c0 += w_
    n_p, n_into, n_d, nk = len(use), 0 if into is None else 1, len(deps), S // bk
    rows = D_IN // N_DEV

    def body(*refs):
        h_ref, p_refs = refs[0], refs[1:1 + n_p]
        o_ref, acc_ref = refs[1 + n_p + n_into + n_d], refs[-1]
        k = pl.program_id(0)
        dp = jnp.concatenate([r[:, s0:s0 + w_] for r, (_, s0, w_) in zip(p_refs, use)], axis=1)
        part = _dot(h_ref[...], dp, "tn")

        @pl.when(k == 0)
        def _():
            acc_ref[...] = part

        @pl.when(k > 0)
        def _():
            acc_ref[...] += part

        @pl.when(k == nk - 1)
        def _():
            vt = acc_ref[...].T.astype(o_ref.dtype)
            for ch in range(half_w // rows):
                o_ref[ch] = vt[ch * rows:(ch + 1) * rows, :]

    return pl.pallas_call(
        body, name=name, grid=(nk,),
        in_specs=[pl.BlockSpec((bk, cols), lambda k: (k, q))]
        + [pl.BlockSpec((bk, p.shape[1]), lambda k: (k, 0)) for p, _, _ in use]
        + [pl.BlockSpec(memory_space=pl.ANY)] * (n_into + n_d),
        out_specs=pl.BlockSpec((half_w // rows, rows, cols), lambda k: (half, 0, 0)),
        out_shape=jax.ShapeDtypeStruct((N_DEV, rows, cols), MXU_DTYPE),
        input_output_aliases={1 + n_p: 0} if n_into else {},
        scratch_shapes=[pltpu.VMEM((cols, half_w), F32)],
        compiler_params=_cparams(("arbitrary",)),
    )(h, *[p for p, _, _ in use], *([into] if n_into else []), *deps)


def _rmsnorm_fwd(x, g, *, rows, name, bt=512):
    bt = min(bt, rows)

    def body(x_ref, g_ref, o_ref):
        xv = x_ref[...]
        r = lax.rsqrt(jnp.mean(xv * xv, axis=-1, keepdims=True) + EPS)
        o_ref[...] = (xv * r * g_ref[...]).astype(o_ref.dtype)

    return pl.pallas_call(
        body, name=name, grid=(rows // bt,),
        in_specs=[pl.BlockSpec((bt, D_MODEL), lambda i: (i, 0)), pl.BlockSpec((1, D_MODEL), lambda i: (0, 0))],
        out_specs=pl.BlockSpec((bt, D_MODEL), lambda i: (i, 0)),
        out_shape=jax.ShapeDtypeStruct((rows, D_MODEL), MXU_DTYPE),
        compiler_params=_cparams(("parallel",)),
    )(x, g.reshape(1, D_MODEL))


def _rms_bwd_tile(xv, gv, dyv):
    r = lax.rsqrt(jnp.mean(xv * xv, axis=-1, keepdims=True) + EPS)
    w = dyv * gv
    dx = r * w - xv * (r * r * r) * jnp.mean(w * xv, axis=-1, keepdims=True)
    dg = _colsum(dyv * xv * r)
    return dx, dg


def _residual_then_norm(acc, x_t, g):
    x1 = x_t + acc
    r = lax.rsqrt(jnp.mean(x1 * x1, axis=-1, keepdims=True) + EPS)
    return (x1, x1 * r * g), ()


def _residual_then_loss(acc, x_t, tgt_t, g):
    x2 = x_t + acc
    r = lax.rsqrt(jnp.mean(x2 * x2, axis=-1, keepdims=True) + EPS)
    diff = x2 * r * g - tgt_t
    part = jnp.sum(jnp.mean(diff * diff, axis=-1, keepdims=True), axis=0, keepdims=True) * 0.5
    dx, dg = _rms_bwd_tile(x2, g, diff * (1.0 / D_MODEL))
    return (dx, dx), (part, dg)


def _norm_bwd_then_residual(n_out):
    def fn(acc, x_t, res_t, g):
        dx, dg = _rms_bwd_tile(x_t, g, acc)
        return (dx + res_t,) * n_out, (dg,)

    return fn


def _rmsnorm_bwd(x, g, dy, res, *, rows, name, bt=512, dx_dtypes=(F32,)):
    bt = min(bt, rows)
    has_res = res is not None

    def body(*refs):
        x_ref, g_ref, dy_ref = refs[:3]
        res_ref = refs[3] if has_res else None
        outs = refs[3 + int(has_res):]
        dx, dg = _rms_bwd_tile(x_ref[...], g_ref[...], dy_ref[...])
        if has_res:
            dx = dx + res_ref[...]
        dg_ref = outs[-1]

        @pl.when(pl.program_id(0) == 0)
        def _():
            dg_ref[...] = jnp.zeros_like(dg_ref)

        dg_ref[...] += dg
        for o in outs[:-1]:
            o[...] = dx.astype(o.dtype)

    row_spec = pl.BlockSpec((bt, D_MODEL), lambda i: (i, 0))
    vec_spec = pl.BlockSpec((1, D_MODEL), lambda i: (0, 0))
    ins = [x, g.reshape(1, D_MODEL), dy] + ([res] if has_res else [])
    return pl.pallas_call(
        body, name=name, grid=(rows // bt,),
        in_specs=[row_spec, vec_spec, row_spec] + ([row_spec] if has_res else []),
        out_specs=[row_spec] * len(dx_dtypes) + [vec_spec],
        out_shape=[jax.ShapeDtypeStruct((rows, D_MODEL), dt) for dt in dx_dtypes] + [jax.ShapeDtypeStruct((1, D_MODEL), F32)],
        compiler_params=_cparams(("arbitrary",)),
    )(*ins)


LRU_T = 512
SCAN_GROUPS = 4


def _gelu(x):
    t = jnp.tanh(GELU_K * (x + 0.044715 * x * x * x))
    return 0.5 * x * (1.0 + t), t


def _gelu_grad(x, t):
    return 0.5 * (1.0 + t) + 0.5 * x * (1.0 - t * t) * GELU_K * (1.0 + 3.0 * 0.044715 * x * x)


def _softplus_neg(lam):
    z = -lam
    u = jnp.exp(-jnp.abs(z))
    w = 1.0 + u
    l1p = jnp.where(w == 1.0, u, jnp.log(w) * u / jnp.where(w == 1.0, 1.0, w - 1.0))
    return jnp.maximum(z, 0.0) + l1p


def _shift_down(cur, prev8, k, row8):
    y = pltpu.roll(cur, k, 0)
    head = jnp.where(row8 < k, pltpu.roll(prev8, k, 0), y[0:8])
    return jnp.concatenate([head, y[8:]], axis=0)


def _shift_up(cur, next8, k, row8):
    n = cur.shape[0]
    y = pltpu.roll(cur, n - k, 0)
    tail = jnp.where(row8 >= 8 - k, pltpu.roll(next8, 8 - k, 0), y[n - 8:n])
    return jnp.concatenate([y[0:n - 8], tail], axis=0)


def _lru_gates(xl, p8, cw, cb, wa, wx, ba, bx, lam, row8, a_mult=None):
    sh = [xl] + [_shift_down(xl, p8, k, row8) for k in (1, 2, 3)]
    xc = cb + cw[3:4] * sh[0] + cw[2:3] * sh[1] + cw[1:2] * sh[2] + cw[0:1] * sh[3]
    r = jax.nn.sigmoid(_dot(xc, wa) + ba)
    i = jax.nn.sigmoid(_dot(xc, wx) + bx)
    sp = _softplus_neg(lam)
    if a_mult is None:
        la = -LRU_C * r * sp
        a = jnp.exp(la)
        mult = jnp.sqrt(jnp.tanh(-la) * (a * a + 1.0))
    else:
        a, mult = a_mult
    return dict(sh=sh, xc=xc, r=r, i=i, sp=sp, a=a, mult=mult)


def _lru_specs(n_t, reverse):
    T = LRU_T
    tt = (lambda t: n_t - 1 - t) if reverse else (lambda t: t)
    blk = lambda col0: pl.BlockSpec((T, LRU_GROUP), lambda g, t: (tt(t), col0 + g))
    prev8 = lambda col0: pl.BlockSpec((8, LRU_GROUP), lambda g, t: (jnp.maximum(tt(t) * (T // 8) - 1, 0), col0 + g))
    vec = lambda rows: pl.BlockSpec((rows, LRU_GROUP), lambda g, t: (0, g))
    wbd = pl.BlockSpec((1, LRU_GROUP, LRU_GROUP), lambda g, t: (g, 0, 0))
    return blk, prev8, vec, wbd


def _lru_fwd(proj, conv_w, conv_b, wa_bd, wx_bd, b_a, b_x, lam, *, S):
    T = LRU_T
    n_t = S // T
    blk, _, vec, wbd = _lru_specs(n_t, False)

    def body(xl_ref, gate_ref, cw_ref, cb_ref, wa_ref, wx_ref, ba_ref, bx_ref, lam_ref,
             hl_ref, z_ref, a_s, m_ref, prev8, hcar, b_s):
        @pl.when(pl.program_id(1) == 0)
        def _():
            prev8[...] = jnp.zeros_like(prev8)
            hcar[...] = jnp.zeros_like(hcar)

        row8 = lax.broadcasted_iota(jnp.int32, (8, LRU_GROUP), 0)
        xl = xl_ref[...]
        q = _lru_gates(xl, prev8[...], cw_ref[...], cb_ref[...], wa_ref[0], wx_ref[0], ba_ref[...], bx_ref[...],
                       lam_ref[...], row8)
        prev8[...] = xl[T - 8:T]
        a_s[...] = q["a"]
        m_ref[...] = q["mult"]
        b_s[...] = q["mult"] * q["i"] * q["xc"]

        def step(c, carry):
            local = []
            for u in range(SCAN_GROUPS):
                off = pl.multiple_of((c * SCAN_GROUPS + u) * 8, 8)
                A = a_s[pl.ds(off, 8), :]
                B = b_s[pl.ds(off, 8), :]
                for k in (1, 2, 4):
                    a_sh = jnp.where(row8 >= k, pltpu.roll(A, k, 0), 1.0)
                    b_sh = jnp.where(row8 >= k, pltpu.roll(B, k, 0), 0.0)
                    B = A * b_sh + B
                    A = A * a_sh
                local.append((off, A, B))
            for off, A, B in local:
                h = A * carry + B
                hl_ref[pl.ds(off, 8), :] = h
                carry = h[7:8, :]
            return carry

        hcar[...] = lax.fori_loop(0, T // (8 * SCAN_GROUPS), step, hcar[...])
        ge, _ = _gelu(gate_ref[...])
        z_ref[...] = (ge * hl_ref[...]).astype(z_ref.dtype)

    return pl.pallas_call(
        body, name="lru_fwd", grid=(N_LRU_GROUPS, n_t),
        in_specs=[blk(C_XL // LRU_GROUP), blk(C_GATE // LRU_GROUP), vec(4), vec(1), wbd, wbd, vec(1), vec(1), vec(1)],
        out_specs=[blk(0)] * 4,
        out_shape=[jax.ShapeDtypeStruct((S, D_RNN), F32), jax.ShapeDtypeStruct((S, D_RNN), MXU_DTYPE),
                   jax.ShapeDtypeStruct((S, D_RNN), F32), jax.ShapeDtypeStruct((S, D_RNN), F32)],
        scratch_shapes=[pltpu.VMEM((8, LRU_GROUP), F32), pltpu.VMEM((1, LRU_GROUP), F32), pltpu.VMEM((T, LRU_GROUP), F32)],
        compiler_params=_cparams(("parallel", "arbitrary")),
    )(proj, proj, conv_w, conv_b, wa_bd, wx_bd, b_a, b_x, lam)


def _lru_bwd(proj, hl, a_fwd, mult_fwd, dz, conv_w, conv_b, wa_bd, wx_bd, b_a, b_x, lam, *, S):
    T = LRU_T
    n_t = S // T
    blk, prev8s, vec, wbd = _lru_specs(n_t, True)

    def body(xl_ref, xlp_ref, gate_ref, hl_ref, hlp_ref, a_ref, m_ref, dz_ref, cw_ref, cb_ref, wa_ref, wx_ref, ba_ref,
             bx_ref, lam_ref, dxl_ref, dgate_ref, dcw_ref, dcb_ref, dwa_ref, dwx_ref, dba_ref, dbx_ref, dlam_ref,
             next8, gcar, c_s, b_s, l_s):
        t = pl.program_id(1)
        first_chunk = t == n_t - 1

        @pl.when(t == 0)
        def _():
            next8[...] = jnp.zeros_like(next8)
            gcar[...] = jnp.zeros_like(gcar)
            for ref in (dcw_ref, dcb_ref, dwa_ref, dwx_ref, dba_ref, dbx_ref, dlam_ref):
                ref[...] = jnp.zeros_like(ref)

        row8 = lax.broadcasted_iota(jnp.int32, (8, LRU_GROUP), 0)
        rowT = lax.broadcasted_iota(jnp.int32, (T, LRU_GROUP), 0)
        keep = jnp.where(first_chunk, 0.0, 1.0)
        xl = xl_ref[...]
        wa, wx, lam_v = wa_ref[0], wx_ref[0], lam_ref[...]
        q = _lru_gates(xl, xlp_ref[...] * keep, cw_ref[...], cb_ref[...], wa, wx, ba_ref[...], bx_ref[...], lam_v, row8,
                       a_mult=(a_ref[...], m_ref[...]))
        a, mult, r, i, xc, sp = q["a"], q["mult"], q["r"], q["i"], q["xc"], q["sp"]
        hl_v = hl_ref[...]
        dz_v = dz_ref[...]
        gate = gate_ref[...]
        ge, th = _gelu(gate)
        dgate_ref[...] = (dz_v * hl_v * _gelu_grad(gate, th)).astype(dgate_ref.dtype)

        c_s[...] = jnp.where(rowT == T - 1, 0.0, pltpu.roll(a, T - 1, 0))
        b_s[...] = dz_v * ge + jnp.where(rowT == T - 1, gcar[...], 0.0)

        def step(n, carry):
            local = []
            for u in range(SCAN_GROUPS):
                off = pl.multiple_of((T // 8 - 1 - (n * SCAN_GROUPS + u)) * 8, 8)
                C = c_s[pl.ds(off, 8), :]
                B = b_s[pl.ds(off, 8), :]
                for k in (1, 2, 4):
                    c_sh = jnp.where(row8 < 8 - k, pltpu.roll(C, 8 - k, 0), 1.0)
                    b_sh = jnp.where(row8 < 8 - k, pltpu.roll(B, 8 - k, 0), 0.0)
                    B = B + C * b_sh
                    C = C * c_sh
                local.append((off, C, B))
            for off, C, B in local:
                lam_t = B + C * carry
                l_s[pl.ds(off, 8), :] = lam_t
                carry = lam_t[0:1, :]
            return carry

        lax.fori_loop(0, T // (8 * SCAN_GROUPS), step, jnp.zeros((1, LRU_GROUP), F32))
        lmb = l_s[...]
        gcar[...] = a[0:1, :] * lmb[0:1, :]

        h_prev = _shift_down(hl_v, hlp_ref[...] * keep, 1, row8)
        da = lmb * h_prev
        dmult = lmb * i * xc
        di = lmb * mult * xc
        dxc = lmb * mult * i
        dla = da * a - dmult * (a * a) / mult
        dr = dla * (-LRU_C * sp)
        dlam_ref[...] += _colsum(dla * (-LRU_C * r)) * (-jax.nn.sigmoid(-lam_v))
        dpa = dr * r * (1.0 - r)
        dpx = di * i * (1.0 - i)
        dxc = dxc + _dot(dpa, wa, "nt") + _dot(dpx, wx, "nt")
        dwa_ref[0] += _dot(xc, dpa, "tn")
        dwx_ref[0] += _dot(xc, dpx, "tn")
        dba_ref[...] += _colsum(dpa)
        dbx_ref[...] += _colsum(dpx)
        dcb_ref[...] += _colsum(dxc)
        cw = cw_ref[...]
        n8 = next8[...]
        dxl = cw[3:4] * dxc
        for k in (1, 2, 3):
            dxl = dxl + cw[3 - k:4 - k] * _shift_up(dxc, n8, k, row8)
        for k in range(4):
            dcw_ref[3 - k:4 - k, :] += _colsum(dxc * q["sh"][k])
        next8[...] = dxc[0:8]
        dxl_ref[...] = dxl.astype(dxl_ref.dtype)

    res = pl.pallas_call(
        body, name="lru_bwd", grid=(N_LRU_GROUPS, n_t),
        in_specs=[blk(C_XL // LRU_GROUP), prev8s(C_XL // LRU_GROUP), blk(C_GATE // LRU_GROUP), blk(0), prev8s(0), blk(0),
                  blk(0), blk(0), vec(4), vec(1), wbd, wbd, vec(1), vec(1), vec(1)],
        out_specs=[blk(0), blk(0), vec(4), vec(1), wbd, wbd, vec(1), vec(1), vec(1)],
        out_shape=[jax.ShapeDtypeStruct((S, D_RNN), MXU_DTYPE), jax.ShapeDtypeStruct((S, D_RNN), MXU_DTYPE),
                   jax.ShapeDtypeStruct((4, D_RNN), F32), jax.ShapeDtypeStruct((1, D_RNN), F32),
                   jax.ShapeDtypeStruct((N_LRU_GROUPS, LRU_GROUP, LRU_GROUP), F32),
                   jax.ShapeDtypeStruct((N_LRU_GROUPS, LRU_GROUP, LRU_GROUP), F32),
                   jax.ShapeDtypeStruct((1, D_RNN), F32), jax.ShapeDtypeStruct((1, D_RNN), F32),
                   jax.ShapeDtypeStruct((1, D_RNN), F32)],
        scratch_shapes=[pltpu.VMEM((8, LRU_GROUP), F32), pltpu.VMEM((1, LRU_GROUP), F32),
                        pltpu.VMEM((T, LRU_GROUP), F32), pltpu.VMEM((T, LRU_GROUP), F32), pltpu.VMEM((T, LRU_GROUP), F32)],
        compiler_params=_cparams(("parallel", "arbitrary")),
    )(proj, proj, proj, hl, hl, a_fwd, mult_fwd, dz, conv_w, conv_b, wa_bd, wx_bd, b_a, b_x, lam)
    return res


def _block_diag(w):
    w4 = w.reshape(N_LRU_GROUPS, 4, LRU_BLOCK, 1, LRU_BLOCK)
    eye = jnp.eye(4, dtype=w.dtype).reshape(1, 4, 1, 4, 1)
    return (w4 * eye).reshape(N_LRU_GROUPS, LRU_GROUP, LRU_GROUP)


def _block_diag_extract(wbd):
    w5 = wbd.reshape(N_LRU_GROUPS, 4, LRU_BLOCK, 4, LRU_BLOCK)
    return jnp.stack([w5[:, a, :, a, :] for a in range(4)], axis=1).reshape(N_LRU_BLOCKS, LRU_BLOCK, LRU_BLOCK)


def _t5_bucket(dist):
    max_exact = NUM_BUCKETS // 2
    df = jnp.maximum(dist, 1).astype(jnp.float32)
    large = max_exact + (jnp.log(df / max_exact) / math.log(MAX_DISTANCE / max_exact)
                         * (NUM_BUCKETS - max_exact)).astype(jnp.int32)
    large = jnp.minimum(large, NUM_BUCKETS - 1)
    return jnp.where(dist < max_exact, dist, large)


def _band_offsets():
    qi = jnp.arange(SPAN)[:, None]
    kj = jnp.arange(2 * SPAN)[None, :]
    return qi + SPAN - kj


def _dil_buckets():
    off = _band_offsets()
    return jnp.stack([_t5_bucket(jnp.maximum(off, 0) * dil) for _, dil in DIL_GROUPS]).astype(jnp.int32)


def _dil_bias(rel_bias, buckets):
    def body(tbl_ref, bk_ref, o_ref):
        g = pl.program_id(0)
        qi = lax.broadcasted_iota(jnp.int32, (SPAN, 2 * SPAN), 0)
        kj = lax.broadcasted_iota(jnp.int32, (SPAN, 2 * SPAN), 1)
        off = qi + SPAN - kj
        valid = (off >= 0) & (off <= SPAN)
        bk = bk_ref[0]
        for h in range(DIL_HEADS):
            acc = jnp.zeros((SPAN, 2 * SPAN), F32)
            for b in range(NUM_BUCKETS):
                acc = jnp.where(bk == b, tbl_ref[b, g * DIL_HEADS + h], acc)
            o_ref[0, h] = jnp.where(valid, acc, NEG)

    return pl.pallas_call(
        body, name="dil_bias", grid=(3,),
        in_specs=[pl.BlockSpec(memory_space=pltpu.SMEM), pl.BlockSpec((1, SPAN, 2 * SPAN), lambda g: (g, 0, 0))],
        out_specs=pl.BlockSpec((1, DIL_HEADS, SPAN, 2 * SPAN), lambda g: (g, 0, 0, 0)),
        out_shape=jax.ShapeDtypeStruct((3, DIL_HEADS, SPAN, 2 * SPAN), F32),
        compiler_params=_cparams(("parallel",)),
    )(rel_bias, buckets)


def _dil_bias_bwd(dbias, buckets):
    def body(db_ref, bk_ref, o_ref):
        lane = lax.broadcasted_iota(jnp.int32, (1, 128), 1)
        rows = [jnp.zeros((1, 128), F32) for _ in range(NUM_BUCKETS)]
        for g in range(3):
            bk = bk_ref[g]
            for h in range(DIL_HEADS):
                d = db_ref[g, h]
                for b in range(NUM_BUCKETS):
                    tot = jnp.sum(_colsum(jnp.where(bk == b, d, 0.0)), axis=1, keepdims=True)
                    rows[b] = jnp.where(lane == g * DIL_HEADS + h, tot, rows[b])
        for b in range(NUM_BUCKETS):
            o_ref[b:b + 1, :] = rows[b]

    return pl.pallas_call(
        body, name="dil_bias_bwd",
        out_shape=jax.ShapeDtypeStruct((NUM_BUCKETS, 128), F32),
        compiler_params=_cparams(),
    )(dbias, buckets)


DIL_SUBBLOCKS = (8, 4, 1)


def _dil_layout(g, S):
    dil, m = DIL_GROUPS[g][1], DIL_SUBBLOCKS[g]
    sub = SPAN * dil
    col = [(C_QKV + t * 768 + g * 256) // 128 for t in range(3)]
    return dil, m, sub, S // (sub * m), col


def _residue_rows(b, r, dil):
    return pl.ds(b * SPAN * dil + r, SPAN, stride=dil) if dil > 1 else pl.ds(b * SPAN, SPAN)


def _for_residues(dil, fn):
    if dil <= 4:
        for r in range(dil):
            fn(r)
    else:
        lax.fori_loop(0, dil, lambda r, c: (fn(r), c)[1], 0, unroll=4)


def _pair_scores(qm, k2, bias, first_cols):
    s = _dot(qm, k2, "nt") * (DIL_HEAD_DIM ** -0.5) + bias
    kj = lax.broadcasted_iota(jnp.int32, s.shape, 1)
    return jnp.where(kj < first_cols, NEG, s)


def _dilated_fwd(proj, bias, g, *, S):
    dil, m, sub, nc, (qc, kc, vc) = _dil_layout(g, S)
    R = sub * m
    cur = lambda cb: pl.BlockSpec((R, 128), lambda p, i: (i, cb + p))
    prv = lambda cb: pl.BlockSpec((sub, 128), lambda p, i: (jnp.maximum(i * m - 1, 0), cb + p))
    out = pl.BlockSpec((R, 128), lambda p, i: (i, p))

    def body(q_ref, kp_ref, kc_ref, vp_ref, vc_ref, b_ref, o_ref, lse_ref):
        lane = lax.broadcasted_iota(jnp.int32, (SPAN, 128), 1)
        sels = (lane < DIL_HEAD_DIM, lane >= DIL_HEAD_DIM)
        for b in range(m):
            first_cols = jnp.where(pl.program_id(1) == 0, SPAN, 0) if b == 0 else 0

            def one(r, b=b, first_cols=first_cols):
                rows = _residue_rows(b, r, dil)
                before = (kc_ref, vc_ref, _residue_rows(b - 1, r, dil)) if b else (kp_ref, vp_ref, _residue_rows(0, r, dil))
                q2 = q_ref[rows, :]
                k2 = _mx(jnp.concatenate([before[0][before[2], :], kc_ref[rows, :]], axis=0))
                v2 = _mx(jnp.concatenate([before[1][before[2], :], vc_ref[rows, :]], axis=0))
                qq = jnp.concatenate([jnp.where(sels[0], q2, 0.0), jnp.where(sels[1], q2, 0.0)], axis=0)
                s = _pair_scores(qq, k2, b_ref[0, 0], first_cols)
                mx = jnp.max(s, axis=-1, keepdims=True)
                p = jnp.exp(s - mx)
                den = jnp.sum(p, axis=-1, keepdims=True)
                o = _dot(p, v2) / den
                st = mx + jnp.log(den)
                o_ref[rows, :] = jnp.where(sels[0], o[0:SPAN], o[SPAN:2 * SPAN])
                lse_ref[rows, :] = jnp.where(lane == 0, st[0:SPAN], jnp.where(lane == 1, st[SPAN:2 * SPAN], 0.0))

            _for_residues(dil, one)

    return pl.pallas_call(
        body, name=f"dil_fwd{g}", grid=(2, nc),
        in_specs=[cur(qc), prv(kc), cur(kc), prv(vc), cur(vc),
                  pl.BlockSpec((1, 1, 2 * SPAN, 2 * SPAN), lambda p, i: (g, p, 0, 0))],
        out_specs=[out, out],
        out_shape=[jax.ShapeDtypeStruct((S, 256), F32), jax.ShapeDtypeStruct((S, 256), F32)],
        compiler_params=_cparams(("parallel", "parallel")),
    )(proj, proj, proj, proj, proj, bias.reshape(3, 2, 2 * SPAN, 2 * SPAN))


def _dilated_bwd(proj, do, lse, delta, bias, g, *, S, into=None):
    dil, m, sub, nc, (qc, kc, vc) = _dil_layout(g, S)
    R = sub * m
    cl = lambda i: jnp.minimum(i, nc - 1)
    cur = lambda cb: pl.BlockSpec((R, 128), lambda p, i: (cl(i), cb + p))
    prv = lambda cb: pl.BlockSpec((sub, 128), lambda p, i: (jnp.maximum(cl(i) * m - 1, 0), cb + p))
    q_out = pl.BlockSpec((R, 128), lambda p, i: (cl(i), 2 * g + p))
    kv_out = pl.BlockSpec((R, 128), lambda p, i: (jnp.maximum(i - 1, 0), 2 * g + p))
    scale = DIL_HEAD_DIM ** -0.5
    n_into = 0 if into is None else 3

    def body(q_ref, kp_ref, kc_ref, vp_ref, vc_ref, do_ref, lse_ref, dl_ref, b_ref, *rest):
        dq_ref, dk_ref, dv_ref, db_ref, dq_s, kc_s, vc_s, kp_s, vp_s, kcar, vcar = rest[n_into:]
        i = pl.program_id(1)

        @pl.when(i == 0)
        def _():
            kcar[...] = jnp.zeros_like(kcar)
            vcar[...] = jnp.zeros_like(vcar)
            db_ref[...] = jnp.zeros_like(db_ref)

        @pl.when(i < nc)
        def _():
            lane = lax.broadcasted_iota(jnp.int32, (SPAN, 128), 1)
            sels = (lane < DIL_HEAD_DIM, lane >= DIL_HEAD_DIM)
            for b in range(m):
                first_cols = jnp.where(i == 0, SPAN, 0) if b == 0 else 0

                def one(r, b=b, first_cols=first_cols):
                    rows = _residue_rows(b, r, dil)
                    rows_before = _residue_rows(b - 1 if b else 0, r, dil)
                    k_before, v_before = (kc_ref, vc_ref) if b else (kp_ref, vp_ref)
                    q2, do2 = q_ref[rows, :], do_ref[rows, :]
                    k2 = _mx(jnp.concatenate([k_before[rows_before, :], kc_ref[rows, :]], axis=0))
                    v2 = _mx(jnp.concatenate([v_before[rows_before, :], vc_ref[rows, :]], axis=0))
                    lse_t, dl_t = lse_ref[rows, :], dl_ref[rows, :]
                    qq = _mx(jnp.concatenate([jnp.where(sels[0], q2, 0.0), jnp.where(sels[1], q2, 0.0)], axis=0))
                    dd = _mx(jnp.concatenate([jnp.where(sels[0], do2, 0.0), jnp.where(sels[1], do2, 0.0)], axis=0))
                    lse2 = jnp.concatenate([lse_t[:, 0:1], lse_t[:, 1:2]], axis=0)
                    dl2 = jnp.concatenate([dl_t[:, 0:1], dl_t[:, 1:2]], axis=0)
                    p = jnp.exp(_pair_scores(qq, k2, b_ref[0, 0], first_cols) - lse2)
                    ds = p * (_dot(dd, v2, "nt") - dl2)
                    db_ref[0] += ds
                    dqq = _dot(ds, k2) * scale
                    dq2 = jnp.where(sels[0], dqq[0:SPAN], dqq[SPAN:2 * SPAN])
                    dk2 = _dot(ds, qq, "tn") * scale
                    dv2 = _dot(p, dd, "tn")
                    dq_s[rows, :] = dq2
                    kc_s[rows, :] = dk2[SPAN:2 * SPAN]
                    vc_s[rows, :] = dv2[SPAN:2 * SPAN]
                    if b:
                        kc_s[rows_before, :] += dk2[0:SPAN]
                        vc_s[rows_before, :] += dv2[0:SPAN]
                    else:
                        kp_s[rows_before, :] = dk2[0:SPAN]
                        vp_s[rows_before, :] = dv2[0:SPAN]

                _for_residues(dil, one)
            dq_ref[...] = dq_s[...].astype(dq_ref.dtype)
            last = pl.ds((m - 1) * sub, sub)
            kcar[last, :] += kp_s[...]
            vcar[last, :] += vp_s[...]
            dk_ref[...] = kcar[...].astype(dk_ref.dtype)
            dv_ref[...] = vcar[...].astype(dv_ref.dtype)
            kcar[...] = kc_s[...]
            vcar[...] = vc_s[...]

        @pl.when(i == nc)
        def _():
            dk_ref[...] = kcar[...].astype(dk_ref.dtype)
            dv_ref[...] = vcar[...].astype(dv_ref.dtype)

    stat = pl.BlockSpec((R, 128), lambda p, i: (cl(i), p))
    big = jax.ShapeDtypeStruct((S, len(DIL_GROUPS) * 256), MXU_DTYPE)
    return pl.pallas_call(
        body, name=f"dil_bwd{g}", grid=(2, nc + 1),
        in_specs=[cur(qc), prv(kc), cur(kc), prv(vc), cur(vc), stat, stat, stat,
                  pl.BlockSpec((1, 1, 2 * SPAN, 2 * SPAN), lambda p, i: (g, p, 0, 0))]
        + [pl.BlockSpec(memory_space=pl.ANY)] * n_into,
        out_specs=[q_out, kv_out, kv_out, pl.BlockSpec((1, 2 * SPAN, 2 * SPAN), lambda p, i: (p, 0, 0))],
        out_shape=[big, big, big, jax.ShapeDtypeStruct((2, 2 * SPAN, 2 * SPAN), F32)],
        input_output_aliases={9 + j: j for j in range(n_into)},
        scratch_shapes=[pltpu.VMEM((R, 128), F32)] * 3 + [pltpu.VMEM((sub, 128), F32)] * 2 + [pltpu.VMEM((R, 128), F32)] * 2,
        compiler_params=_cparams(("parallel", "arbitrary")),
    )(proj, proj, proj, proj, proj, do, lse, delta, bias.reshape(3, 2, 2 * SPAN, 2 * SPAN), *(into or ()))


def _dilated_merge(os_, lses, *, S, bt=512):
    tile = pl.BlockSpec((bt, 128), lambda i, p: (i, p))

    def body(o0, o1, o2, l0, l1, l2, o_ref, om_ref, lse_ref):
        lane = lax.broadcasted_iota(jnp.int32, (bt, 128), 1)
        lo = lane < DIL_HEAD_DIM
        ls = [l0[...], l1[...], l2[...]]
        ws, stat = [], jnp.zeros((bt, 128), F32)
        for e in range(2):
            a = [l[:, e:e + 1] for l in ls]
            m = jnp.maximum(jnp.maximum(a[0], a[1]), a[2])
            ex = [jnp.exp(v - m) for v in a]
            tot = ex[0] + ex[1] + ex[2]
            ws.append([v / tot for v in ex])
            stat = jnp.where(lane == e, m + jnp.log(tot), stat)
        acc = jnp.zeros((bt, 128), F32)
        for gi, o in enumerate((o0, o1, o2)):
            acc = acc + jnp.where(lo, ws[0][gi], ws[1][gi]) * o[...]
        o_ref[...] = acc
        om_ref[...] = _mx(acc)
        lse_ref[...] = stat

    return pl.pallas_call(
        body, name="dil_merge", grid=(S // bt, 2),
        in_specs=[tile] * 6, out_specs=[tile, tile, tile],
        out_shape=[jax.ShapeDtypeStruct((S, 256), F32), jax.ShapeDtypeStruct((S, 256), MXU_DTYPE),
                   jax.ShapeDtypeStruct((S, 256), F32)],
        compiler_params=_cparams(("parallel", "parallel")),
    )(*os_, *lses)


def _dilated_delta(do, o, *, S, bt=512):
    tile = pl.BlockSpec((bt, 128), lambda i, p: (i, p))

    def body(do_ref, o_ref, d_ref):
        lane = lax.broadcasted_iota(jnp.int32, (bt, 128), 1)
        prod = do_ref[...] * o_ref[...]
        d0 = jnp.sum(jnp.where(lane < DIL_HEAD_DIM, prod, 0.0), axis=-1, keepdims=True)
        d1 = jnp.sum(jnp.where(lane >= DIL_HEAD_DIM, prod, 0.0), axis=-1, keepdims=True)
        d_ref[...] = jnp.where(lane == 0, d0, jnp.where(lane == 1, d1, 0.0))

    return pl.pallas_call(
        body, name="dil_delta", grid=(S // bt, 2), in_specs=[tile, tile], out_specs=tile,
        out_shape=jax.ShapeDtypeStruct((S, 256), F32), compiler_params=_cparams(("parallel", "parallel")),
    )(do, o)


MEM_T = 2048
QM_BLK = C_QM // MEM_HEAD_DIM


def _mem_attn_fwd(proj, kv, *, S):
    scale = MEM_HEAD_DIM ** -0.5

    def body(q_ref, k_ref, v_ref, o_ref, om_ref, lse_ref):
        s = _dot(q_ref[...], k_ref[...], "nt") * scale
        m = jnp.max(s, axis=-1, keepdims=True)
        p = jnp.exp(s - m)
        den = jnp.sum(p, axis=-1, keepdims=True)
        o = _dot(p, v_ref[...]) / den
        o_ref[...] = o
        om_ref[...] = _mx(o)
        lse_ref[0] = m + jnp.log(den)

    return pl.pallas_call(
        body, name="mem_attn_fwd", grid=(S // MEM_T, MEM_HEADS),
        in_specs=[pl.BlockSpec((MEM_T, MEM_HEAD_DIM), lambda i, h: (i, QM_BLK + h)),
                  pl.BlockSpec((N_MEM, MEM_HEAD_DIM), lambda i, h: (0, h)),
                  pl.BlockSpec((N_MEM, MEM_HEAD_DIM), lambda i, h: (0, MEM_HEADS + h))],
        out_specs=[pl.BlockSpec((MEM_T, MEM_HEAD_DIM), lambda i, h: (i, h)),
                   pl.BlockSpec((MEM_T, MEM_HEAD_DIM), lambda i, h: (i, h)),
                   pl.BlockSpec((1, MEM_T, 1), lambda i, h: (h, i, 0))],
        out_shape=[jax.ShapeDtypeStruct((S, MEM_WIDTH), F32), jax.ShapeDtypeStruct((S, MEM_WIDTH), MXU_DTYPE),
                   jax.ShapeDtypeStruct((MEM_HEADS, S, 1), F32)],
        compiler_params=_cparams(("parallel", "parallel")),
    )(proj, kv, kv)


def _mem_attn_bwd(proj, kv, om, lse, dom, *, S):
    scale = MEM_HEAD_DIM ** -0.5

    def body(q_ref, k_ref, v_ref, o_ref, lse_ref, do_ref, dq_ref, dk_ref, dv_ref):
        @pl.when(pl.program_id(1) == 0)
        def _():
            dk_ref[...] = jnp.zeros_like(dk_ref)
            dv_ref[...] = jnp.zeros_like(dv_ref)

        qv, kv_, vv, dov = q_ref[...], k_ref[...], v_ref[...], do_ref[...]
        p = jnp.exp(_dot(qv, kv_, "nt") * scale - lse_ref[0])
        delta = jnp.sum(dov * o_ref[...], axis=-1, keepdims=True)
        ds = p * (_dot(dov, vv, "nt") - delta)
        dq_ref[...] = (_dot(ds, kv_) * scale).astype(dq_ref.dtype)
        dk_ref[...] += _dot(ds, qv, "tn") * scale
        dv_ref[...] += _dot(p, dov, "tn")

    tile = pl.BlockSpec((MEM_T, MEM_HEAD_DIM), lambda h, i: (i, h))
    kvo = pl.BlockSpec((N_MEM, MEM_HEAD_DIM), lambda h, i: (0, h))
    return pl.pallas_call(
        body, name="mem_attn_bwd", grid=(MEM_HEADS, S // MEM_T),
        in_specs=[pl.BlockSpec((MEM_T, MEM_HEAD_DIM), lambda h, i: (i, QM_BLK + h)),
                  pl.BlockSpec((N_MEM, MEM_HEAD_DIM), lambda h, i: (0, h)),
                  pl.BlockSpec((N_MEM, MEM_HEAD_DIM), lambda h, i: (0, MEM_HEADS + h)),
                  tile, pl.BlockSpec((1, MEM_T, 1), lambda h, i: (h, i, 0)), tile],
        out_specs=[tile, kvo, kvo],
        out_shape=[jax.ShapeDtypeStruct((S, MEM_WIDTH), MXU_DTYPE), jax.ShapeDtypeStruct((N_MEM, MEM_WIDTH), F32),
                   jax.ShapeDtypeStruct((N_MEM, MEM_WIDTH), F32)],
        compiler_params=_cparams(("parallel", "arbitrary")),
    )(proj, kv, kv, om, lse, dom)


MIX_BM = 1024
MIX_BN = 256
GATES_BLK = C_GATES // MIX_BN


def _mix_specs(j_outer):
    ix = (lambda f: (lambda j, i: f(i, j))) if j_outer else (lambda f: f)
    act = lambda width: pl.BlockSpec((MIX_BM, width), ix(lambda i, j: (i, 0)))
    wgt = lambda width: pl.BlockSpec((width, MIX_BN), ix(lambda i, j: (0, j)))
    gate = lambda b: pl.BlockSpec((MIX_BM, MIX_BN), ix(lambda i, j: (i, GATES_BLK + 4 * b + j)))
    bias = lambda b: pl.BlockSpec((1, MIX_BN), ix(lambda i, j: (0, 4 * b + j)))
    tile = pl.BlockSpec((MIX_BM, MIX_BN), ix(lambda i, j: (i, j)))
    return act, wgt, gate, bias, tile


def _mix_fwd(z_lru, o_dil, om, w_lru, w_dil, w_mem, proj, b_gate, *, S):
    act, wgt, gate, bias, tile = _mix_specs(False)

    def body(zl, od, mo, wl, wd, wm, g0, g1, g2, b0, b1, b2, o_ref):
        acc = jax.nn.sigmoid(g0[...] + b0[...]) * _dot(zl[...], wl[...])
        acc += jax.nn.sigmoid(g1[...] + b1[...]) * _dot(od[...], wd[...])
        acc += jax.nn.sigmoid(g2[...] + b2[...]) * _dot(mo[...], wm[...])
        o_ref[...] = acc.astype(o_ref.dtype)

    return pl.pallas_call(
        body, name="mix_fwd", grid=(S // MIX_BM, D_MODEL // MIX_BN),
        in_specs=[act(D_RNN), act(256), act(MEM_WIDTH), wgt(D_RNN), wgt(256), wgt(MEM_WIDTH),
                  gate(0), gate(1), gate(2), bias(0), bias(1), bias(2)],
        out_specs=tile, out_shape=jax.ShapeDtypeStruct((S, D_MODEL), MXU_DTYPE),
        compiler_params=_cparams(("parallel", "parallel")),
    )(z_lru, o_dil, om, w_lru, w_dil, w_mem, proj, proj, proj, b_gate, b_gate, b_gate)


def _mix_bwd(dmerged, z_lru, o_dil, om, w_lru, w_dil, w_mem, proj, b_gate, *, S):
    act, wgt, gate, bias, tile = _mix_specs(True)

    def body(dm, zl, od, mo, wl, wd, wm, g0, g1, g2, b0, b1, b2,
             dg0, dg1, dg2, dy0, dy1, dy2, db0, db1, db2):
        @pl.when(pl.program_id(1) == 0)
        def _():
            for r in (db0, db1, db2):
                r[...] = jnp.zeros_like(r)

        dmv = dm[...]
        for act_ref, w_ref, g_ref, b_ref, dg_ref, dy_ref, db_ref in (
                (zl, wl, g0, b0, dg0, dy0, db0), (od, wd, g1, b1, dg1, dy1, db1), (mo, wm, g2, b2, dg2, dy2, db2)):
            y = _dot(act_ref[...], w_ref[...])
            gt = jax.nn.sigmoid(g_ref[...] + b_ref[...])
            dgate = dmv * y * gt * (1.0 - gt)
            dg_ref[...] = dgate.astype(dg_ref.dtype)
            dy_ref[...] = (dmv * gt).astype(dy_ref.dtype)
            db_ref[...] += _colsum(dgate)

    big = jax.ShapeDtypeStruct((S, D_MODEL), MXU_DTYPE)
    vec = jax.ShapeDtypeStruct((1, D_MODEL), F32)
    vspec = pl.BlockSpec((1, MIX_BN), lambda j, i: (0, j))
    return pl.pallas_call(
        body, name="mix_bwd", grid=(D_MODEL // MIX_BN, S // MIX_BM),
        in_specs=[tile, act(D_RNN), act(256), act(MEM_WIDTH), wgt(D_RNN), wgt(256), wgt(MEM_WIDTH),
                  gate(0), gate(1), gate(2), bias(0), bias(1), bias(2)],
        out_specs=[tile] * 6 + [vspec] * 3, out_shape=[big] * 6 + [vec] * 3,
        compiler_params=_cparams(("parallel", "arbitrary")),
    )(dmerged, z_lru, o_dil, om, w_lru, w_dil, w_mem, proj, proj, proj, b_gate, b_gate, b_gate)


def _adamw_math(w, g, m, v):
    m = ADAM_B1 * m + (1.0 - ADAM_B1) * g
    v = ADAM_B2 * v + (1.0 - ADAM_B2) * (g * g)
    m_hat = m / (1.0 - ADAM_B1 ** ADAM_STEP)
    v_hat = v / (1.0 - ADAM_B2 ** ADAM_STEP)
    delta = -ADAM_LR * (m_hat / (jnp.sqrt(v_hat) + ADAM_EPS) + ADAM_WD * w)
    return delta, m, v


def _adamw_landed(w, own, land, m, v, *, name, col_blk=0, prev=None):
    R = w.shape[0]
    n_parts, C = land.shape[0], land.shape[2]
    br = next(d for d in (256, 464, 128) if R % d == 0)
    tile = pl.BlockSpec((br, C), lambda i: (i, col_blk))
    part = pl.BlockSpec((br, C), lambda i: (i, 0))
    n_prev = 0 if prev is None else 4

    def body(w_ref, o_ref, l_ref, m_ref, v_ref, *rest):
        g_ref, d_ref, nm_ref, nv_ref = rest[n_prev:]
        g = o_ref[...].astype(F32)
        for p in range(n_parts):
            g = g + l_ref[p].astype(F32)
        d, nm, nv = _adamw_math(w_ref[...], g, m_ref[...], v_ref[...])
        g_ref[...] = g
        d_ref[...] = d
        nm_ref[...] = nm
        nv_ref[...] = nv

    return pl.pallas_call(
        body, name=name, grid=(R // br,),
        in_specs=[tile, part, pl.BlockSpec((n_parts, br, C), lambda i: (0, i, 0)), tile, tile]
        + [pl.BlockSpec(memory_space=pl.ANY)] * n_prev,
        out_specs=[tile] * 4, out_shape=[jax.ShapeDtypeStruct(w.shape, F32)] * 4,
        input_output_aliases={5 + j: j for j in range(n_prev)},
        compiler_params=_cparams(("parallel",)),
    )(w, own, land, m, v, *(prev or ()))


def _adamw_plain(w, g, m, v, *, name):
    def body(w_ref, g_ref, m_ref, v_ref, d_ref, nm_ref, nv_ref):
        d, nm, nv = _adamw_math(w_ref[...], g_ref[...], m_ref[...], v_ref[...])
        d_ref[...] = d
        nm_ref[...] = nm
        nv_ref[...] = nv

    return pl.pallas_call(
        body, name=name, out_shape=[jax.ShapeDtypeStruct(w.shape, F32)] * 3, compiler_params=_cparams(),
    )(w, g, m, v)


def _my_pos():
    return lax.axis_index("x"), lax.axis_index("y"), lax.axis_index("c")


def _dev_index(p):
    return 4 * p[0] + 2 * p[1] + p[2]


def _all_gather(shards):
    n = len(shards)
    hbm = pl.BlockSpec(memory_space=pl.ANY)

    def body(*refs):
        ins, outs = refs[:n], refs[n:2 * n]
        send_sems, recv_sems, local_sems = refs[2 * n:]
        x, y, c = _my_pos()
        me, sibling = (x, y, c), (x, y, 1 - c)
        chips = [(1 - x, y), (x, 1 - y), (1 - x, 1 - y)]

        def copy(a, k, block, to, src=None):
            dst = outs[a].at[_dev_index(block)]
            return pltpu.make_async_remote_copy(
                src_ref=dst if src is None else src, dst_ref=dst,
                send_sem=send_sems.at[a, k], recv_sem=recv_sems.at[a, k], device_id=to, device_id_type=MESH)

        mine = [pltpu.make_async_copy(ins[a], outs[a].at[_dev_index(me)], local_sems.at[a]) for a in range(n)]
        for cp in mine:
            cp.start()
        first = []
        for a in range(n):
            first.append(copy(a, 0, me, sibling, src=ins[a]))
            first += [copy(a, 1 + j, me, (*chip, c), src=ins[a]) for j, chip in enumerate(chips)]
        for cp in first:
            cp.start()
        passed = []
        for j, chip in enumerate(chips):
            for a in range(n):
                copy(a, 1 + j, (*chip, c), me).wait_recv()
                fwd = copy(a, 4 + j, (*chip, c), sibling)
                fwd.start()
                passed.append(fwd)
        for a in range(n):
            copy(a, 0, sibling, me).wait_recv()
        for j, chip in enumerate(chips):
            for a in range(n):
                copy(a, 4 + j, (*chip, 1 - c), me).wait_recv()
        for cp in first + passed:
            cp.wait_send()
        for cp in mine:
            cp.wait()

    return pl.pallas_call(
        body, name="all_gather_weights",
        in_specs=[hbm] * n, out_specs=[hbm] * n,
        out_shape=[jax.ShapeDtypeStruct((N_DEV,) + s.shape, s.dtype) for s in shards],
        scratch_shapes=[pltpu.SemaphoreType.DMA((n, 7)), pltpu.SemaphoreType.DMA((n, 7)), pltpu.SemaphoreType.DMA((n,))],
        compiler_params=pltpu.CompilerParams(has_side_effects=True),
    )(*shards)


def _peers(me):
    x, y, c = me
    out = []
    for k in range(1, 8):
        fx, fy, fc = (k >> 2) & 1, (k >> 1) & 1, k & 1
        out.append((k - 1, (1 - x if fx else x, 1 - y if fy else y, 1 - c if fc else c)))
    return out


HBM_SPEC = pl.BlockSpec(memory_space=pltpu.HBM)
SEM_SPEC = pl.BlockSpec(memory_space=pltpu.SEMAPHORE)
DATAFLOW_EFFECT = pltpu.SideEffectType.DATAFLOW_SIDE_EFFECTING


def _gather_refs(src, land, me, peer, k):
    return src, land.at[_dev_index(me)]


def _scatter_refs(src, land, me, peer, k):
    return src.at[_dev_index(peer)], land.at[k]


def _push_start(srcs, land_shapes, refs_of, name, after=()):
    n, n_after = len(srcs), len(after)

    def body(*refs):
        ins, lands = refs[:n], refs[n:2 * n]
        send_sems, recv_sems, token = refs[2 * n + n_after], refs[2 * n + n_after + 1], refs[-1]
        me = _my_pos()
        for k, peer in _peers(me):
            for a in range(n):
                src, dst = refs_of(ins[a], lands[a], me, peer, k)
                pltpu.make_async_remote_copy(src_ref=src, dst_ref=dst, send_sem=send_sems.at[7 * a + k],
                                             recv_sem=recv_sems.at[7 * a + k], device_id=peer, device_id_type=MESH).start()
        token[...] = jnp.zeros_like(token)

    lands = [lax.empty(shp, s.dtype) for shp, s in zip(land_shapes, srcs)]
    hbm = lambda a: pltpu.with_memory_space_constraint(a, pltpu.HBM)
    res = pl.pallas_call(
        body, name=name,
        out_shape=(pltpu.SemaphoreType.DMA((7 * n,)), pltpu.SemaphoreType.DMA((7 * n,)),
                   *[pltpu.HBM(s.shape, s.dtype) for s in srcs], *[pltpu.HBM(l.shape, l.dtype) for l in lands],
                   jax.ShapeDtypeStruct((8, 128), F32)),
        in_specs=[HBM_SPEC] * (2 * n) + [pl.BlockSpec(memory_space=pl.ANY)] * n_after,
        out_specs=(SEM_SPEC, SEM_SPEC, *[HBM_SPEC] * (2 * n), pl.BlockSpec(memory_space=pltpu.VMEM)),
        input_output_aliases={i: 2 + i for i in range(2 * n)},
        compiler_params=pltpu.CompilerParams(has_side_effects=DATAFLOW_EFFECT),
    )(*[hbm(s) for s in srcs], *[hbm(l) for l in lands], *after)
    return dict(sems=(res[0], res[1]), srcs=list(res[2:2 + n]), lands=list(res[2 + n:2 + 2 * n]), token=res[-1], n=n,
                refs_of=refs_of, name=name)


def _push_wait(started, after):
    n, refs_of = started["n"], started["refs_of"]
    after = list(after) if isinstance(after, (list, tuple)) else [after]

    def body(*refs):
        ins, lands = refs[:n], refs[n:2 * n]
        send_sems, recv_sems = refs[2 * n], refs[2 * n + 1]
        me = _my_pos()
        for k, peer in _peers(me):
            for a in range(n):
                src, dst = refs_of(ins[a], lands[a], me, peer, k)
                cp = pltpu.make_async_remote_copy(src_ref=src, dst_ref=dst, send_sem=send_sems.at[7 * a + k],
                                                  recv_sem=recv_sems.at[7 * a + k], device_id=peer, device_id_type=MESH)
                cp.wait_send()
                cp.wait_recv()

    arrs = started["srcs"] + started["lands"]
    res = pl.pallas_call(
        body, name=started["name"].replace("start", "wait"),
        out_shape=tuple(pltpu.HBM(a.shape, a.dtype) for a in arrs),
        in_specs=[HBM_SPEC] * (2 * n) + [SEM_SPEC, SEM_SPEC] + [pl.BlockSpec(memory_space=pl.ANY)] * len(after),
        out_specs=tuple([HBM_SPEC] * (2 * n)),
        input_output_aliases={i: i for i in range(2 * n)},
        compiler_params=pltpu.CompilerParams(has_side_effects=DATAFLOW_EFFECT),
    )(*arrs, *started["sems"], *after)
    return list(res[n:2 * n])


def _sum_slots(slots):
    def body(in_ref, out_ref):
        acc = in_ref[0]
        for d in range(1, N_DEV):
            acc = acc + in_ref[d]
        out_ref[...] = acc

    return pl.pallas_call(body, name="sum_small", out_shape=jax.ShapeDtypeStruct(slots.shape[1:], F32),
                          compiler_params=_cparams())(slots)


def _adamw_many(ws, gs, ms, vs):
    n = len(ws)

    def body(*refs):
        for i in range(n):
            w_ref, g_ref, m_ref, v_ref = (refs[j * n + i] for j in range(4))
            d_, nm, nv = _adamw_math(w_ref[...], g_ref[...], m_ref[...], v_ref[...])
            for j, val in enumerate((d_, nm, nv)):
                refs[(4 + j) * n + i][...] = val

    res = pl.pallas_call(body, name="adamw_small", out_shape=[jax.ShapeDtypeStruct(w_.shape, F32) for w_ in ws] * 3,
                         compiler_params=_cparams())(*ws, *gs, *ms, *vs)
    return [(res[i], res[n + i], res[2 * n + i]) for i in range(n)]


def _local_step(x, mem, tgt, W, P, late_weights, send_grads, reduce_small, tie0):
    S = x.shape[0]
    W = dict(W)
    h = _rmsnorm_fwd(x, P["g_mix"] + tie0, rows=S, name="norm_mix")
    proj = _matmul(h, W["w_in_t"], M=S, N=D_IN, K=D_MODEL, mode="nt", bm=512, bn=D_IN // 2, bk=D_MODEL, name="mm_in",
                   j_outer=True)

    wa_bd, wx_bd = _mx(_block_diag(P["w_rg_a"])), _mx(_block_diag(P["w_rg_x"]))
    lru_args = (W["conv_w"], P["conv_b"].reshape(1, -1), wa_bd, wx_bd, P["b_rg_a"].reshape(1, -1),
                P["b_rg_x"].reshape(1, -1), P["lru_lambda"].reshape(1, -1))
    hl, z_lru, a_lru, mult_lru = _lru_fwd(proj, *lru_args, S=S)

    buckets = _dil_buckets()
    bias = _dil_bias(P["rel_bias"], buckets)
    group_out = [_dilated_fwd(proj, bias, g, S=S) for g in range(len(DIL_GROUPS))]
    o_dil, o_dil_m, lse_dil = _dilated_merge([o for o, _ in group_out], [l for _, l in group_out], S=S)

    W.update(late_weights("branch", [o_dil, z_lru]))
    mem_n = _rmsnorm_fwd(mem, P["g_mem"], rows=N_MEM, name="norm_mem")
    kv = _matmul(mem_n, W["w_mem_kv"], M=N_MEM, N=2 * MEM_WIDTH, K=D_MODEL, mode="nn", bm=N_MEM, bn=512, bk=D_MODEL,
                 name="mm_kv")
    om, om_m, lse_mem = _mem_attn_fwd(proj, kv, S=S)
    b_gate = P["b_gate"].reshape(1, -1)
    merged = _mix_fwd(z_lru, o_dil_m, om_m, W["w_lru_out"], W["w_dil_out"], W["w_mem_out"], proj, b_gate, S=S)
    g_mlp, g_final, g_mix = (P[n].reshape(1, D_MODEL) for n in ("g_mlp", "g_final", "g_mix"))
    x1, hm = _matmul_rows(merged, W["w_out"], M=S, K=D_MODEL, mode="nn", bm=512, name="mm_out",
                          row_fn=_residual_then_norm, out_dtypes=(F32, MXU_DTYPE), tiles=[x], vecs=[g_mlp])
    W.update(late_weights("mlp", [hm]))

    def relu2(acc):
        rl = jnp.maximum(acc, 0.0)
        return rl * rl, rl

    act, relu_u = _matmul(hm, W["w_mlp_in_t"], M=S, N=D_FF, K=D_MODEL, mode="nt", bm=1024, bn=1024, bk=D_MODEL,
                          name="mm_mlp_in", out_dtypes=(MXU_DTYPE, MXU_DTYPE), epilogue=relu2, j_outer=True)
    dx2, dx2_m, loss, dg_final = _matmul_rows(
        act, W["w_mlp_out"], M=S, K=D_FF, mode="nn", bm=512, name="mm_mlp_out", row_fn=_residual_then_loss,
        out_dtypes=(F32, MXU_DTYPE), tiles=[x1, tgt], vecs=[g_final], acc_widths=(1, D_MODEL))

    G, Gs = {}, {}
    Gs["g_final"] = dg_final
    dw = dict(mode="tn", K=S, bk=S, out_dtypes=(MXU_DTYPE,))
    G["w_mlp_out"] = _matmul(act, dx2_m, M=D_FF, N=D_MODEL, bm=512, bn=D_MODEL, name="mm_dw_mlp_out",
                             parts=("rows", D_FF // N_DEV), **dw)
    du = _matmul(dx2_m, W["w_mlp_out"], M=S, N=D_FF, K=D_MODEL, mode="nt", bm=1024, bn=1024, bk=D_MODEL, name="mm_du",
                 out_dtypes=(MXU_DTYPE,), epilogue=lambda acc, rl: (acc * (2.0 * rl.astype(F32)),),
                 extras=[(relu_u, (0, 0))], j_outer=True)
    G["w_mlp_in"] = _matmul(hm, du, M=D_MODEL, N=D_FF, bm=D_MODEL, bn=512, name="mm_dw_mlp_in",
                            parts=("cols", D_FF // N_DEV), **dw)
    tie1 = send_grads({n: G.pop(n) for n in ("w_mlp_out", "w_mlp_in")})
    dx1, dx1_m, Gs["g_mlp"] = _matmul_rows(
        du, W["w_mlp_in_t"], M=S, K=D_FF, mode="nn", bm=512, name="mm_dhm", row_fn=_norm_bwd_then_residual(2),
        out_dtypes=(F32, MXU_DTYPE), tiles=[x1, dx2], vecs=[g_mlp], acc_widths=(D_MODEL,), deps=[tie1])
    G["w_out"] = _matmul(merged, dx1_m, M=D_MODEL, N=D_MODEL, bm=512, bn=D_MODEL, name="mm_dw_out",
                         parts=("rows", D_MODEL // N_DEV), **dw)
    dmerged = _matmul(dx1_m, W["w_out"], M=S, N=D_MODEL, K=D_MODEL, mode="nt", bm=512, bn=D_MODEL, bk=D_MODEL, name="mm_dmerged")
    (dg0, dg1, dg2, dy_lru, dy_dil, dy_mem, db0, db1, db2) = _mix_bwd(
        dmerged, z_lru, o_dil_m, om_m, W["w_lru_out"], W["w_dil_out"], W["w_mem_out"], proj, b_gate, S=S)
    Gs["b_gate0"], Gs["b_gate1"], Gs["b_gate2"] = db0, db1, db2

    G["w_mem_out"] = _matmul(om_m, dy_mem, M=MEM_WIDTH, N=D_MODEL, bm=MEM_WIDTH, bn=D_MODEL, name="mm_dw_mem_out",
                             parts=("cols", D_MODEL // N_DEV), **dw)
    dom = _matmul(dy_mem, W["w_mem_out"], M=S, N=MEM_WIDTH, K=D_MODEL, mode="nt", bm=512, bn=MEM_WIDTH, bk=D_MODEL,
                  name="mm_dom")
    dqm, dk_mem, dv_mem = _mem_attn_bwd(proj, kv, om, lse_mem, dom, S=S)
    dkv = jnp.concatenate([dk_mem, dv_mem], axis=1)
    G["w_mem_kv"] = _matmul(mem_n, dkv, M=D_MODEL, N=2 * MEM_WIDTH, K=N_MEM, mode="tn", bm=D_MODEL, bn=2 * MEM_WIDTH,
                            bk=N_MEM, name="mm_dw_kv", out_dtypes=(MXU_DTYPE,), parts=("rows", D_MODEL // N_DEV))
    dmem_n = _matmul(dkv, W["w_mem_kv"], M=N_MEM, N=D_MODEL, K=2 * MEM_WIDTH, mode="nt", bm=N_MEM, bn=D_MODEL,
                     bk=2 * MEM_WIDTH, name="mm_dmem")
    (Gs["g_mem"],) = _rmsnorm_bwd(mem, P["g_mem"], dmem_n, None, rows=N_MEM, name="norm_mem_bwd", dx_dtypes=())

    G["w_dil_out"] = _matmul(o_dil_m, dy_dil, M=256, N=D_MODEL, bm=256, bn=D_MODEL, name="mm_dw_dil_out",
                             parts=("cols", D_MODEL // N_DEV), **dw)
    do_dil = _matmul(dy_dil, W["w_dil_out"], M=S, N=256, K=D_MODEL, mode="nt", bm=512, bn=256, bk=D_MODEL, name="mm_do_dil")
    G["w_lru_out"] = _matmul(z_lru, dy_lru, M=D_RNN, N=D_MODEL, bm=D_RNN, bn=D_MODEL, name="mm_dw_lru_out",
                             parts=("cols", D_MODEL // N_DEV), **dw)
    dz = _matmul(dy_lru, W["w_lru_out"], M=S, N=D_RNN, K=D_MODEL, mode="nt", bm=512, bn=D_RNN, bk=D_MODEL, name="mm_dz_lru")
    tie2 = send_grads({n: G.pop(n) for n in ("w_out", "w_mem_out", "w_mem_kv", "w_dil_out", "w_lru_out")})
    bias = bias + tie2[0, 0]
    delta = _dilated_delta(do_dil, o_dil, S=S)
    dqkv, dbias = None, []
    for g in range(len(DIL_GROUPS)):
        *dqkv, db_g = _dilated_bwd(proj, do_dil, lse_dil, delta, bias, g, S=S, into=dqkv)
        dbias.append(db_g)
    drel = _dil_bias_bwd(jnp.stack(dbias, axis=0).reshape(len(DIL_GROUPS), DIL_HEADS, SPAN, 2 * SPAN), buckets)
    Gs["rel_bias"] = drel

    dxl, dgl, dcw, dcb, dwa, dwx, dba, dbx, dlam = _lru_bwd(proj, hl, a_lru, mult_lru, dz, *lru_args, S=S)
    Gs["conv_w"], Gs["conv_b"] = dcw, dcb
    Gs["w_rg_a"], Gs["w_rg_x"] = _block_diag_extract(dwa), _block_diag_extract(dwx)
    Gs["b_rg_a"], Gs["b_rg_x"], Gs["lru_lambda"] = dba, dbx, dlam
    Gs["loss"] = loss

    dproj = [dxl, dgl] + dqkv + [dqm, dg0, dg1, dg2]
    tie = []
    for q in range(W_IN_PIECES):
        dw_q = None
        for half in range(2):
            dw_q = _dw_in_t_half(h, dproj, q, half, S=S, name=f"mm_dw_in_{q}_{half}", into=dw_q, deps=tie)
        tie = [send_grads({f"w_in_{q}": dw_q})]
    grad_x, Gs["g_mix"] = _matmul_rows(
        dproj, W["w_in_t"], M=S, K=D_IN, mode="nn", bm=256, name="mm_dh", row_fn=_norm_bwd_then_residual(1),
        out_dtypes=(F32,), tiles=[x, dx1], vecs=[g_mix], acc_widths=(D_MODEL,), deps=tie)
    return grad_x, reduce_small(Gs)


BIG = ("w_in", "w_lru_out", "w_dil_out", "w_mem_kv", "w_mem_out", "w_out", "w_mlp_in", "w_mlp_out")
W_IN_PIECES = 2
COL_SHARDED = ("w_lru_out", "w_dil_out", "w_mem_out", "w_mlp_in")
GATHERED_TRANSPOSED = ("w_mlp_in",)
SMALL = ("g_mix", "b_gate", "conv_b", "w_rg_a", "b_rg_a", "w_rg_x", "b_rg_x", "lru_lambda", "rel_bias", "g_mem",
         "g_mlp", "g_final")
WEIGHTS = ("g_mix", "w_in", "b_gate", "conv_w", "conv_b", "w_rg_a", "b_rg_a", "w_rg_x", "b_rg_x", "lru_lambda",
           "w_lru_out", "rel_bias", "w_dil_out", "g_mem", "w_mem_kv", "w_mem_out", "w_out", "g_mlp", "w_mlp_in",
           "w_mlp_out", "g_final")


def _gathered_to_full(name, gathered):
    if name in COL_SHARDED:
        n, r, c = gathered.shape
        return gathered.transpose(1, 0, 2).reshape(r, n * c)
    n, r, c = gathered.shape
    return gathered.reshape(n * r, c)


SMALL_GRADS = (("g_mix", (1, 1024)), ("b_gate0", (1, 1024)), ("b_gate1", (1, 1024)), ("b_gate2", (1, 1024)),
               ("conv_b", (1, 768)), ("w_rg_a", (12, 64, 64)), ("b_rg_a", (1, 768)), ("w_rg_x", (12, 64, 64)),
               ("b_rg_x", (1, 768)), ("lru_lambda", (1, 768)), ("rel_bias", (32, 128)), ("g_mem", (1, 1024)),
               ("g_mlp", (1, 1024)), ("g_final", (1, 1024)), ("conv_w", (4, 768)), ("loss", (1, 1)))


def _pack(parts):
    flat = jnp.concatenate([p.reshape(-1) for p in parts])
    return jnp.pad(flat, (0, (-flat.shape[0]) % 1024)).reshape(-1, 128)


def _unpack(pack, shapes):
    flat = pack.reshape(-1)
    out, off = [], 0
    for shp in shapes:
        size = math.prod(shp)
        out.append(flat[off:off + size].reshape(shp))
        off += size
    return out


def kernel(x, mem, g_mix, w_in, b_gate, conv_w, conv_b, w_rg_a, b_rg_a, w_rg_x, b_rg_x, lru_lambda, w_lru_out, rel_bias, w_dil_out, g_mem, w_mem_kv, w_mem_out, w_out, g_mlp, w_mlp_in, w_mlp_out, g_final, loss_target, m_g_mix, m_w_in, m_b_gate, m_conv_w, m_conv_b, m_w_rg_a, m_b_rg_a, m_w_rg_x, m_b_rg_x, m_lru_lambda, m_w_lru_out, m_rel_bias, m_w_dil_out, m_g_mem, m_w_mem_kv, m_w_mem_out, m_w_out, m_g_mlp, m_w_mlp_in, m_w_mlp_out, m_g_final, v_g_mix, v_w_in, v_b_gate, v_conv_w, v_conv_b, v_w_rg_a, v_b_rg_a, v_w_rg_x, v_b_rg_x, v_lru_lambda, v_w_lru_out, v_rel_bias, v_w_dil_out, v_g_mem, v_w_mem_kv, v_w_mem_out, v_w_out, v_g_mlp, v_w_mlp_in, v_w_mlp_out, v_g_final):
    w = dict(g_mix=g_mix, w_in=w_in, b_gate=b_gate, conv_w=conv_w, conv_b=conv_b, w_rg_a=w_rg_a, b_rg_a=b_rg_a,
             w_rg_x=w_rg_x, b_rg_x=b_rg_x, lru_lambda=lru_lambda, w_lru_out=w_lru_out, rel_bias=rel_bias,
             w_dil_out=w_dil_out, g_mem=g_mem, w_mem_kv=w_mem_kv, w_mem_out=w_mem_out, w_out=w_out, g_mlp=g_mlp,
             w_mlp_in=w_mlp_in, w_mlp_out=w_mlp_out, g_final=g_final)
    m = dict(g_mix=m_g_mix, w_in=m_w_in, b_gate=m_b_gate, conv_w=m_conv_w, conv_b=m_conv_b, w_rg_a=m_w_rg_a,
             b_rg_a=m_b_rg_a, w_rg_x=m_w_rg_x, b_rg_x=m_b_rg_x, lru_lambda=m_lru_lambda, w_lru_out=m_w_lru_out,
             rel_bias=m_rel_bias, w_dil_out=m_w_dil_out, g_mem=m_g_mem, w_mem_kv=m_w_mem_kv, w_mem_out=m_w_mem_out,
             w_out=m_w_out, g_mlp=m_g_mlp, w_mlp_in=m_w_mlp_in, w_mlp_out=m_w_mlp_out, g_final=m_g_final)
    v = dict(g_mix=v_g_mix, w_in=v_w_in, b_gate=v_b_gate, conv_w=v_conv_w, conv_b=v_conv_b, w_rg_a=v_w_rg_a,
             b_rg_a=v_b_rg_a, w_rg_x=v_w_rg_x, b_rg_x=v_b_rg_x, lru_lambda=v_lru_lambda, w_lru_out=v_w_lru_out,
             rel_bias=v_rel_bias, w_dil_out=v_w_dil_out, g_mem=v_g_mem, w_mem_kv=v_w_mem_kv, w_mem_out=v_w_mem_out,
             w_out=v_w_out, g_mlp=v_g_mlp, w_mlp_in=v_w_mlp_in, w_mlp_out=v_w_mlp_out, g_final=v_g_final)

    my_idx = _dev_index(_my_pos())

    g_in, g_cw = _all_gather([_mx(w["w_in"].T), w["conv_w"]])
    W = {"w_in_t": g_in.reshape(D_IN, D_MODEL), "conv_w": g_cw.transpose(1, 0, 2).reshape(CONV_WIDTH, D_RNN)}
    late, order_after = {}, [g_in]
    for group, names in (("branch", ("w_mem_kv", "w_lru_out", "w_dil_out", "w_mem_out", "w_out")),
                         ("mlp", ("w_mlp_in", "w_mlp_out"))):
        shards = [_mx(w[n].T if n in GATHERED_TRANSPOSED else w[n]) for n in names]
        started = _push_start(shards, [(N_DEV,) + s.shape for s in shards], _gather_refs, f"gather_{group}_start",
                              after=order_after)
        late[group] = (names, shards, started)
        order_after = [started["token"]]
    P = {n: w[n] for n in SMALL}

    def late_weights(group, after):
        names, shards, started = late[group]
        out = {}
        for n, land, own in zip(names, _push_wait(started, after), shards):
            full = lax.dynamic_update_index_in_dim(land, own, my_idx, 0)
            if n in GATHERED_TRANSPOSED:
                out[n + "_t"] = full.reshape(-1, full.shape[2])
            else:
                out[n] = _gathered_to_full(n, full)
        return out

    sent, small = [], {}

    def send_grads(gs):
        names = list(gs)
        parts = [gs[n] for n in names]
        own = [lax.dynamic_index_in_dim(p, my_idx, 0, keepdims=False) for p in parts]
        started = _push_start(parts, [(N_DEV - 1,) + p.shape[1:] for p in parts], _scatter_refs,
                              f"scatter{len(sent)}_start")
        sent.append((names, own, started))
        return started["token"]

    def reduce_small(gs):
        small["pack"] = _pack([gs[n] for n, _ in SMALL_GRADS])
        small["started"] = _push_start([small["pack"]], [(N_DEV,) + small["pack"].shape], _gather_refs, "small_start")
        return small["started"]["token"]

    grad_x, last_token = _local_step(x[0], mem[0], loss_target[0], W, P, late_weights, send_grads, reduce_small,
                                     late["mlp"][2]["token"][0, 0])

    grads, deltas, new_m, new_v = {}, {}, {}, {}
    after = last_token
    for names, own, started in sent[:-W_IN_PIECES]:
        for n, o, land in zip(names, own, _push_wait(started, after)):
            grads[n], deltas[n], new_m[n], new_v[n] = _adamw_landed(w[n], o, land, m[n], v[n], name=f"adamw_{n}")
            after = deltas[n]
    prev = None
    for q, (names, own, started) in enumerate(sent[-W_IN_PIECES:]):
        (land,) = _push_wait(started, after)
        prev = _adamw_landed(w["w_in"].T, own[0], land, m["w_in"].T, v["w_in"].T, name=f"adamw_{names[0]}",
                             col_blk=q, prev=prev)
        after = prev[1]
    grads["w_in"], deltas["w_in"], new_m["w_in"], new_v["w_in"] = [t.T for t in prev]
    (small_land,) = _push_wait(small["started"], after)
    total = _sum_slots(lax.dynamic_update_index_in_dim(small_land, small["pack"], my_idx, 0))
    summed = dict(zip([n for n, _ in SMALL_GRADS], _unpack(total, [shp for _, shp in SMALL_GRADS])))
    summed["b_gate"] = jnp.concatenate([summed.pop(f"b_gate{b}") for b in range(3)], axis=1)
    summed["rel_bias"] = summed["rel_bias"][:, :3 * DIL_HEADS]
    for n in SMALL:
        grads[n] = summed[n].reshape(w[n].shape)
    small_updates = _adamw_many([w[n] for n in SMALL], [grads[n] for n in SMALL], [m[n] for n in SMALL],
                                [v[n] for n in SMALL])
    for n, (d_, nm_, nv_) in zip(SMALL, small_updates):
        deltas[n], new_m[n], new_v[n] = d_, nm_, nv_
    conv_w_sum, loss_sum = summed["conv_w"], summed["loss"]
    cw_cols = D_RNN // N_DEV
    grads["conv_w"] = lax.dynamic_slice(conv_w_sum, (0, my_idx * cw_cols), (CONV_WIDTH, cw_cols))
    deltas["conv_w"], new_m["conv_w"], new_v["conv_w"] = _adamw_plain(
        w["conv_w"], grads["conv_w"], m["conv_w"], v["conv_w"], name="adamw_conv_w")

    return (loss_sum.reshape(()), grad_x[None], *[grads[n] for n in WEIGHTS], *[deltas[n] for n in WEIGHTS],
            *[new_m[n] for n in WEIGHTS], *[new_v[n] for n in WEIGHTS])
```

```python
import functools
import math

import jax
import jax.numpy as jnp
from jax import lax
from jax.experimental import pallas as pl
from jax.experimental.pallas import tpu as pltpu

F32 = jnp.float32
MXU_DTYPE = jnp.bfloat16
VMEM_LIMIT_BYTES = 56 * 1024 * 1024
N_DEV = 8

D_MODEL = 1024
N_MEM = 256
MEM_HEADS = 4
MEM_HEAD_DIM = 128
MEM_WIDTH = 512
D_RNN = 768
LRU_BLOCK = 64
N_LRU_BLOCKS = 12
LRU_GROUP = 256
N_LRU_GROUPS = 3
CONV_WIDTH = 4
LRU_C = 8.0
DIL_GROUPS = ((128, 1), (512, 4), (2048, 16))
SPAN = 128
DIL_HEADS = 4
DIL_HEAD_DIM = 64
NUM_BUCKETS = 32
MAX_DISTANCE = 2048
D_FF = 4096
D_IN = 7424
EPS = 1e-6
NEG = -1e30
C_XL, C_GATE, C_QKV, C_QM, C_GATES = 0, 768, 1536, 3840, 4352

ADAM_LR = 0.001
ADAM_B1 = 0.9
ADAM_B2 = 0.999
ADAM_EPS = 1e-08
ADAM_WD = 0.01
ADAM_STEP = 10

MESH = pl.DeviceIdType.MESH
GELU_K = math.sqrt(2.0 / math.pi)


def _cparams(sem=None):
    kw = dict(vmem_limit_bytes=VMEM_LIMIT_BYTES)
    if sem is not None:
        kw["dimension_semantics"] = sem
    return pltpu.CompilerParams(**kw)


def _mx(v):
    return v.astype(MXU_DTYPE)


def _dot(a, b, mode="nn"):
    dims = {"nn": (((1,), (0,)), ((), ())), "nt": (((1,), (1,)), ((), ())), "tn": (((0,), (0,)), ((), ()))}[mode]
    return lax.dot_general(_mx(a), _mx(b), dims, preferred_element_type=F32)


def _colsum(v):
    return jnp.sum(v, axis=0, keepdims=True)


def _matmul(a, b, *, M, N, K, mode, bm, bn, bk, name, out_dtypes=(F32,), epilogue=None, extras=(),
            a_off=(0, 0), b_off=(0, 0), j_outer=False, deps=(), parts=None):
    assert M % bm == 0 and N % bn == 0 and K % bk == 0, (name, M, N, K, bm, bn, bk)
    nm, nn, nk = M // bm, N // bn, K // bk

    def ij(f):
        if j_outer:
            return lambda j, i, k: f(i, j, k)
        return f

    if mode == "tn":
        a_spec = pl.BlockSpec((bk, bm), ij(lambda i, j, k: (k + a_off[0], i + a_off[1])))
    else:
        a_spec = pl.BlockSpec((bm, bk), ij(lambda i, j, k: (i + a_off[0], k + a_off[1])))
    if mode == "nt":
        b_spec = pl.BlockSpec((bn, bk), ij(lambda i, j, k: (j + b_off[0], k + b_off[1])))
    else:
        b_spec = pl.BlockSpec((bk, bn), ij(lambda i, j, k: (k + b_off[0], j + b_off[1])))
    ex_specs = [pl.BlockSpec((bm, bn), ij(functools.partial(lambda i, j, k, o: (i + o[0], j + o[1]), o=off)))
                for _, off in extras]
    if parts is None:
        out_dims = (M, N)
        out_spec = pl.BlockSpec((bm, bn), ij(lambda i, j, k: (i, j)))
    elif parts[0] == "rows":
        r = parts[1]
        assert bm % r == 0
        out_dims = (M // r, r, N)
        out_spec = pl.BlockSpec((bm // r, r, bn), ij(lambda i, j, k: (i, 0, j)))
    elif parts[0] == "rows_t":
        r = parts[1]
        assert bn % r == 0
        out_dims = (N // r, r, M)
        out_spec = pl.BlockSpec((bn // r, r, bm), ij(lambda i, j, k: (j, 0, i)))
    else:
        c = parts[1]
        assert bn % c == 0
        out_dims = (N // c, M, c)
        out_spec = pl.BlockSpec((bn // c, bm, c), ij(lambda i, j, k: (j, i, 0)))
    n_ex, n_out, n_dep = len(extras), len(out_dtypes), len(deps)

    def body(*refs):
        a_ref, b_ref = refs[0], refs[1]
        ex = refs[2:2 + n_ex]
        outs = refs[2 + n_ex + n_dep:2 + n_ex + n_dep + n_out]
        part = _dot(a_ref[...], b_ref[...], mode)

        def finish(acc):
            vals = epilogue(acc, *[e[...] for e in ex]) if epilogue is not None else (acc,)
            for o, v in zip(outs, vals):
                if parts is not None and parts[0] == "rows_t":
                    v = v.T
                v = v.astype(o.dtype)
                if parts is None:
                    o[...] = v
                elif parts[0] in ("rows", "rows_t"):
                    for ch in range(v.shape[0] // parts[1]):
                        o[ch] = v[ch * parts[1]:(ch + 1) * parts[1], :]
                else:
                    for ch in range(bn // parts[1]):
                        o[ch] = v[:, ch * parts[1]:(ch + 1) * parts[1]]

        if nk == 1:
            finish(part)
        else:
            acc_ref = refs[-1]
            k = pl.program_id(2)

            @pl.when(k == 0)
            def _():
                acc_ref[...] = part

            @pl.when(k > 0)
            def _():
                acc_ref[...] += part

            @pl.when(k == nk - 1)
            def _():
                finish(acc_ref[...])

    grid = (nn, nm, nk) if j_outer else (nm, nn, nk)
    res = pl.pallas_call(
        body, name=name, grid=grid,
        in_specs=[a_spec, b_spec] + ex_specs + [pl.BlockSpec(memory_space=pl.ANY)] * n_dep,
        out_specs=[out_spec] * n_out,
        out_shape=[jax.ShapeDtypeStruct(out_dims, dt) for dt in out_dtypes],
        scratch_shapes=[pltpu.VMEM((bm, bn), F32)] if nk > 1 else [],
        compiler_params=_cparams(("parallel", "parallel", "arbitrary")),
    )(a, b, *[e for e, _ in extras], *deps)
    return res[0] if n_out == 1 else res


ROW_SUBTILES = 2


def _matmul_rows(a, b, *, M, K, mode, bm, name, row_fn, out_dtypes, tiles=(), vecs=(), acc_widths=(), deps=()):
    N = D_MODEL
    assert M % bm == 0
    segs = list(a) if isinstance(a, (list, tuple)) else [a]
    widths = [s_.shape[1] for s_ in segs]
    assert sum(widths) == K and (len(segs) == 1 or mode == "nn")
    n_s, n_t, n_v, n_o, n_a, n_d = len(segs), len(tiles), len(vecs), len(out_dtypes), len(acc_widths), len(deps)
    row = pl.BlockSpec((bm, N), lambda i: (i, 0))
    b_shape = (K, N) if mode == "nn" else (N, K)

    def body(*refs):
        b_ref = refs[n_s]
        ins = refs[n_s + 1:n_s + 1 + n_t + n_v]
        outs = refs[n_s + 1 + n_t + n_v + n_d:n_s + 1 + n_t + n_v + n_d + n_o]
        accs = refs[n_s + 1 + n_t + n_v + n_d + n_o:]
        for o in accs:
            @pl.when(pl.program_id(0) == 0)
            def _(o=o):
                o[...] = jnp.zeros_like(o)

        for s_ in range(ROW_SUBTILES):
            rows = pl.ds(s_ * (bm // ROW_SUBTILES), bm // ROW_SUBTILES)
            if n_s == 1:
                acc = _dot(refs[0][rows, :], b_ref[...], mode)
            else:
                acc, k0 = None, 0
                for a_ref, w_ in zip(refs[:n_s], widths):
                    part = _dot(a_ref[rows, :], b_ref[k0:k0 + w_, :])
                    acc = part if acc is None else acc + part
                    k0 += w_
            tile_vals, partials = row_fn(acc, *[r[rows, :] for r in ins[:n_t]], *[r[...] for r in ins[n_t:]])
            for o, val in zip(outs, tile_vals):
                o[rows, :] = val.astype(o.dtype)
            for o, val in zip(accs, partials):
                o[...] += val

    res = pl.pallas_call(
        body, name=name, grid=(M // bm,),
        in_specs=[pl.BlockSpec((bm, w_), lambda i: (i, 0)) for w_ in widths] + [pl.BlockSpec(b_shape, lambda i: (0, 0))]
        + [row] * n_t + [pl.BlockSpec((1, N), lambda i: (0, 0))] * n_v + [pl.BlockSpec(memory_space=pl.ANY)] * n_d,
        out_specs=[row] * n_o + [pl.BlockSpec((1, w_), lambda i: (0, 0)) for w_ in acc_widths],
        out_shape=[jax.ShapeDtypeStruct((M, N), dt) for dt in out_dtypes]
        + [jax.ShapeDtypeStruct((1, w_), F32) for w_ in acc_widths],
        compiler_params=_cparams(("arbitrary",) if n_a else ("parallel",)),
    )(*segs, b, *tiles, *vecs, *deps)
    return res


def _dw_in_t_half(h, pieces, q, half, *, S, name, into=None, deps=(), bk=1024):
    half_w, cols = D_IN // 2, D_MODEL // W_IN_PIECES
    lo, hi = half * half_w, (half + 1) * half_w
    use, c0 = [], 0
    for p in pieces:
        w_ = p.shape[1]
        a0, a1 = max(lo, c0), min(hi, c0 + w_)
        if a1 > a0:
            use.append((p, a0 - c0, a1 - a0))
        c0 += w_
    n_p, n_into, n_d, nk = len(use), 0 if into is None else 1, len(deps), S // bk
    rows = D_IN // N_DEV

    def body(*refs):
        h_ref, p_refs = refs[0], refs[1:1 + n_p]
        o_ref, acc_ref = refs[1 + n_p + n_into + n_d], refs[-1]
        k = pl.program_id(0)
        dp = jnp.concatenate([r[:, s0:s0 + w_] for r, (_, s0, w_) in zip(p_refs, use)], axis=1)
        part = _dot(h_ref[...], dp, "tn")

        @pl.when(k == 0)
        def _():
            acc_ref[...] = part

        @pl.when(k > 0)
        def _():
            acc_ref[...] += part

        @pl.when(k == nk - 1)
        def _():
            vt = acc_ref[...].T.astype(o_ref.dtype)
            for ch in range(half_w // rows):
                o_ref[ch] = vt[ch * rows:(ch + 1) * rows, :]

    return pl.pallas_call(
        body, name=name, grid=(nk,),
        in_specs=[pl.BlockSpec((bk, cols), lambda k: (k, q))]
        + [pl.BlockSpec((bk, p.shape[1]), lambda k: (k, 0)) for p, _, _ in use]
        + [pl.BlockSpec(memory_space=pl.ANY)] * (n_into + n_d),
        out_specs=pl.BlockSpec((half_w // rows, rows, cols), lambda k: (half, 0, 0)),
        out_shape=jax.ShapeDtypeStruct((N_DEV, rows, cols), MXU_DTYPE),
        input_output_aliases={1 + n_p: 0} if n_into else {},
        scratch_shapes=[pltpu.VMEM((cols, half_w), F32)],
        compiler_params=_cparams(("arbitrary",)),
    )(h, *[p for p, _, _ in use], *([into] if n_into else []), *deps)


def _rmsnorm_fwd(x, g, *, rows, name, bt=512):
    bt = min(bt, rows)

    def body(x_ref, g_ref, o_ref):
        xv = x_ref[...]
        r = lax.rsqrt(jnp.mean(xv * xv, axis=-1, keepdims=True) + EPS)
        o_ref[...] = (xv * r * g_ref[...]).astype(o_ref.dtype)

    return pl.pallas_call(
        body, name=name, grid=(rows // bt,),
        in_specs=[pl.BlockSpec((bt, D_MODEL), lambda i: (i, 0)), pl.BlockSpec((1, D_MODEL), lambda i: (0, 0))],
        out_specs=pl.BlockSpec((bt, D_MODEL), lambda i: (i, 0)),
        out_shape=jax.ShapeDtypeStruct((rows, D_MODEL), MXU_DTYPE),
        compiler_params=_cparams(("parallel",)),
    )(x, g.reshape(1, D_MODEL))


def _rms_bwd_tile(xv, gv, dyv):
    r = lax.rsqrt(jnp.mean(xv * xv, axis=-1, keepdims=True) + EPS)
    w = dyv * gv
    dx = r * w - xv * (r * r * r) * jnp.mean(w * xv, axis=-1, keepdims=True)
    dg = _colsum(dyv * xv * r)
    return dx, dg


def _residual_then_norm(acc, x_t, g):
    x1 = x_t + acc
    r = lax.rsqrt(jnp.mean(x1 * x1, axis=-1, keepdims=True) + EPS)
    return (x1, x1 * r * g), ()


def _residual_then_loss(acc, x_t, tgt_t, g):
    x2 = x_t + acc
    r = lax.rsqrt(jnp.mean(x2 * x2, axis=-1, keepdims=True) + EPS)
    diff = x2 * r * g - tgt_t
    part = jnp.sum(jnp.mean(diff * diff, axis=-1, keepdims=True), axis=0, keepdims=True) * 0.5
    dx, dg = _rms_bwd_tile(x2, g, diff * (1.0 / D_MODEL))
    return (dx, dx), (part, dg)


def _norm_bwd_then_residual(n_out):
    def fn(acc, x_t, res_t, g):
        dx, dg = _rms_bwd_tile(x_t, g, acc)
        return (dx + res_t,) * n_out, (dg,)

    return fn


def _rmsnorm_bwd(x, g, dy, res, *, rows, name, bt=512, dx_dtypes=(F32,)):
    bt = min(bt, rows)
    has_res = res is not None

    def body(*refs):
        x_ref, g_ref, dy_ref = refs[:3]
        res_ref = refs[3] if has_res else None
        outs = refs[3 + int(has_res):]
        dx, dg = _rms_bwd_tile(x_ref[...], g_ref[...], dy_ref[...])
        if has_res:
            dx = dx + res_ref[...]
        dg_ref = outs[-1]

        @pl.when(pl.program_id(0) == 0)
        def _():
            dg_ref[...] = jnp.zeros_like(dg_ref)

        dg_ref[...] += dg
        for o in outs[:-1]:
            o[...] = dx.astype(o.dtype)

    row_spec = pl.BlockSpec((bt, D_MODEL), lambda i: (i, 0))
    vec_spec = pl.BlockSpec((1, D_MODEL), lambda i: (0, 0))
    ins = [x, g.reshape(1, D_MODEL), dy] + ([res] if has_res else [])
    return pl.pallas_call(
        body, name=name, grid=(rows // bt,),
        in_specs=[row_spec, vec_spec, row_spec] + ([row_spec] if has_res else []),
        out_specs=[row_spec] * len(dx_dtypes) + [vec_spec],
        out_shape=[jax.ShapeDtypeStruct((rows, D_MODEL), dt) for dt in dx_dtypes] + [jax.ShapeDtypeStruct((1, D_MODEL), F32)],
        compiler_params=_cparams(("arbitrary",)),
    )(*ins)


LRU_T = 512
SCAN_GROUPS = 4


def _gelu(x):
    t = jnp.tanh(GELU_K * (x + 0.044715 * x * x * x))
    return 0.5 * x * (1.0 + t), t


def _gelu_grad(x, t):
    return 0.5 * (1.0 + t) + 0.5 * x * (1.0 - t * t) * GELU_K * (1.0 + 3.0 * 0.044715 * x * x)


def _softplus_neg(lam):
    z = -lam
    u = jnp.exp(-jnp.abs(z))
    w = 1.0 + u
    l1p = jnp.where(w == 1.0, u, jnp.log(w) * u / jnp.where(w == 1.0, 1.0, w - 1.0))
    return jnp.maximum(z, 0.0) + l1p


def _shift_down(cur, prev8, k, row8):
    y = pltpu.roll(cur, k, 0)
    head = jnp.where(row8 < k, pltpu.roll(prev8, k, 0), y[0:8])
    return jnp.concatenate([head, y[8:]], axis=0)


def _shift_up(cur, next8, k, row8):
    n = cur.shape[0]
    y = pltpu.roll(cur, n - k, 0)
    tail = jnp.where(row8 >= 8 - k, pltpu.roll(next8, 8 - k, 0), y[n - 8:n])
    return jnp.concatenate([y[0:n - 8], tail], axis=0)


def _lru_gates(xl, p8, cw, cb, wa, wx, ba, bx, lam, row8, a_mult=None):
    sh = [xl] + [_shift_down(xl, p8, k, row8) for k in (1, 2, 3)]
    xc = cb + cw[3:4] * sh[0] + cw[2:3] * sh[1] + cw[1:2] * sh[2] + cw[0:1] * sh[3]
    r = jax.nn.sigmoid(_dot(xc, wa) + ba)
    i = jax.nn.sigmoid(_dot(xc, wx) + bx)
    sp = _softplus_neg(lam)
    if a_mult is None:
        la = -LRU_C * r * sp
        a = jnp.exp(la)
        mult = jnp.sqrt(jnp.tanh(-la) * (a * a + 1.0))
    else:
        a, mult = a_mult
    return dict(sh=sh, xc=xc, r=r, i=i, sp=sp, a=a, mult=mult)


def _lru_specs(n_t, reverse):
    T = LRU_T
    tt = (lambda t: n_t - 1 - t) if reverse else (lambda t: t)
    blk = lambda col0: pl.BlockSpec((T, LRU_GROUP), lambda g, t: (tt(t), col0 + g))
    prev8 = lambda col0: pl.BlockSpec((8, LRU_GROUP), lambda g, t: (jnp.maximum(tt(t) * (T // 8) - 1, 0), col0 + g))
    vec = lambda rows: pl.BlockSpec((rows, LRU_GROUP), lambda g, t: (0, g))
    wbd = pl.BlockSpec((1, LRU_GROUP, LRU_GROUP), lambda g, t: (g, 0, 0))
    return blk, prev8, vec, wbd


def _lru_fwd(proj, conv_w, conv_b, wa_bd, wx_bd, b_a, b_x, lam, *, S):
    T = LRU_T
    n_t = S // T
    blk, _, vec, wbd = _lru_specs(n_t, False)

    def body(xl_ref, gate_ref, cw_ref, cb_ref, wa_ref, wx_ref, ba_ref, bx_ref, lam_ref,
             hl_ref, z_ref, a_s, m_ref, prev8, hcar, b_s):
        @pl.when(pl.program_id(1) == 0)
        def _():
            prev8[...] = jnp.zeros_like(prev8)
            hcar[...] = jnp.zeros_like(hcar)

        row8 = lax.broadcasted_iota(jnp.int32, (8, LRU_GROUP), 0)
        xl = xl_ref[...]
        q = _lru_gates(xl, prev8[...], cw_ref[...], cb_ref[...], wa_ref[0], wx_ref[0], ba_ref[...], bx_ref[...],
                       lam_ref[...], row8)
        prev8[...] = xl[T - 8:T]
        a_s[...] = q["a"]
        m_ref[...] = q["mult"]
        b_s[...] = q["mult"] * q["i"] * q["xc"]

        def step(c, carry):
            local = []
            for u in range(SCAN_GROUPS):
                off = pl.multiple_of((c * SCAN_GROUPS + u) * 8, 8)
                A = a_s[pl.ds(off, 8), :]
                B = b_s[pl.ds(off, 8), :]
                for k in (1, 2, 4):
                    a_sh = jnp.where(row8 >= k, pltpu.roll(A, k, 0), 1.0)
                    b_sh = jnp.where(row8 >= k, pltpu.roll(B, k, 0), 0.0)
                    B = A * b_sh + B
                    A = A * a_sh
                local.append((off, A, B))
            for off, A, B in local:
                h = A * carry + B
                hl_ref[pl.ds(off, 8), :] = h
                carry = h[7:8, :]
            return carry

        hcar[...] = lax.fori_loop(0, T // (8 * SCAN_GROUPS), step, hcar[...])
        ge, _ = _gelu(gate_ref[...])
        z_ref[...] = (ge * hl_ref[...]).astype(z_ref.dtype)

    return pl.pallas_call(
        body, name="lru_fwd", grid=(N_LRU_GROUPS, n_t),
        in_specs=[blk(C_XL // LRU_GROUP), blk(C_GATE // LRU_GROUP), vec(4), vec(1), wbd, wbd, vec(1), vec(1), vec(1)],
        out_specs=[blk(0)] * 4,
        out_shape=[jax.ShapeDtypeStruct((S, D_RNN), F32), jax.ShapeDtypeStruct((S, D_RNN), MXU_DTYPE),
                   jax.ShapeDtypeStruct((S, D_RNN), F32), jax.ShapeDtypeStruct((S, D_RNN), F32)],
        scratch_shapes=[pltpu.VMEM((8, LRU_GROUP), F32), pltpu.VMEM((1, LRU_GROUP), F32), pltpu.VMEM((T, LRU_GROUP), F32)],
        compiler_params=_cparams(("parallel", "arbitrary")),
    )(proj, proj, conv_w, conv_b, wa_bd, wx_bd, b_a, b_x, lam)


def _lru_bwd(proj, hl, a_fwd, mult_fwd, dz, conv_w, conv_b, wa_bd, wx_bd, b_a, b_x, lam, *, S):
    T = LRU_T
    n_t = S // T
    blk, prev8s, vec, wbd = _lru_specs(n_t, True)

    def body(xl_ref, xlp_ref, gate_ref, hl_ref, hlp_ref, a_ref, m_ref, dz_ref, cw_ref, cb_ref, wa_ref, wx_ref, ba_ref,
             bx_ref, lam_ref, dxl_ref, dgate_ref, dcw_ref, dcb_ref, dwa_ref, dwx_ref, dba_ref, dbx_ref, dlam_ref,
             next8, gcar, c_s, b_s, l_s):
        t = pl.program_id(1)
        first_chunk = t == n_t - 1

        @pl.when(t == 0)
        def _():
            next8[...] = jnp.zeros_like(next8)
            gcar[...] = jnp.zeros_like(gcar)
            for ref in (dcw_ref, dcb_ref, dwa_ref, dwx_ref, dba_ref, dbx_ref, dlam_ref):
                ref[...] = jnp.zeros_like(ref)

        row8 = lax.broadcasted_iota(jnp.int32, (8, LRU_GROUP), 0)
        rowT = lax.broadcasted_iota(jnp.int32, (T, LRU_GROUP), 0)
        keep = jnp.where(first_chunk, 0.0, 1.0)
        xl = xl_ref[...]
        wa, wx, lam_v = wa_ref[0], wx_ref[0], lam_ref[...]
        q = _lru_gates(xl, xlp_ref[...] * keep, cw_ref[...], cb_ref[...], wa, wx, ba_ref[...], bx_ref[...], lam_v, row8,
                       a_mult=(a_ref[...], m_ref[...]))
        a, mult, r, i, xc, sp = q["a"], q["mult"], q["r"], q["i"], q["xc"], q["sp"]
        hl_v = hl_ref[...]
        dz_v = dz_ref[...]
        gate = gate_ref[...]
        ge, th = _gelu(gate)
        dgate_ref[...] = (dz_v * hl_v * _gelu_grad(gate, th)).astype(dgate_ref.dtype)

        c_s[...] = jnp.where(rowT == T - 1, 0.0, pltpu.roll(a, T - 1, 0))
        b_s[...] = dz_v * ge + jnp.where(rowT == T - 1, gcar[...], 0.0)

        def step(n, carry):
            local = []
            for u in range(SCAN_GROUPS):
                off = pl.multiple_of((T // 8 - 1 - (n * SCAN_GROUPS + u)) * 8, 8)
                C = c_s[pl.ds(off, 8), :]
                B = b_s[pl.ds(off, 8), :]
                for k in (1, 2, 4):
                    c_sh = jnp.where(row8 < 8 - k, pltpu.roll(C, 8 - k, 0), 1.0)
                    b_sh = jnp.where(row8 < 8 - k, pltpu.roll(B, 8 - k, 0), 0.0)
                    B = B + C * b_sh
                    C = C * c_sh
                local.append((off, C, B))
            for off, C, B in local:
                lam_t = B + C * carry
                l_s[pl.ds(off, 8), :] = lam_t
                carry = lam_t[0:1, :]
            return carry

        lax.fori_loop(0, T // (8 * SCAN_GROUPS), step, jnp.zeros((1, LRU_GROUP), F32))
        lmb = l_s[...]
        gcar[...] = a[0:1, :] * lmb[0:1, :]

        h_prev = _shift_down(hl_v, hlp_ref[...] * keep, 1, row8)
        da = lmb * h_prev
        dmult = lmb * i * xc
        di = lmb * mult * xc
        dxc = lmb * mult * i
        dla = da * a - dmult * (a * a) / mult
        dr = dla * (-LRU_C * sp)
        dlam_ref[...] += _colsum(dla * (-LRU_C * r)) * (-jax.nn.sigmoid(-lam_v))
        dpa = dr * r * (1.0 - r)
        dpx = di * i * (1.0 - i)
        dxc = dxc + _dot(dpa, wa, "nt") + _dot(dpx, wx, "nt")
        dwa_ref[0] += _dot(xc, dpa, "tn")
        dwx_ref[0] += _dot(xc, dpx, "tn")
        dba_ref[...] += _colsum(dpa)
        dbx_ref[...] += _colsum(dpx)
        dcb_ref[...] += _colsum(dxc)
        cw = cw_ref[...]
        n8 = next8[...]
        dxl = cw[3:4] * dxc
        for k in (1, 2, 3):
            dxl = dxl + cw[3 - k:4 - k] * _shift_up(dxc, n8, k, row8)
        for k in range(4):
            dcw_ref[3 - k:4 - k, :] += _colsum(dxc * q["sh"][k])
        next8[...] = dxc[0:8]
        dxl_ref[...] = dxl.astype(dxl_ref.dtype)

    res = pl.pallas_call(
        body, name="lru_bwd", grid=(N_LRU_GROUPS, n_t),
        in_specs=[blk(C_XL // LRU_GROUP), prev8s(C_XL // LRU_GROUP), blk(C_GATE // LRU_GROUP), blk(0), prev8s(0), blk(0),
                  blk(0), blk(0), vec(4), vec(1), wbd, wbd, vec(1), vec(1), vec(1)],
        out_specs=[blk(0), blk(0), vec(4), vec(1), wbd, wbd, vec(1), vec(1), vec(1)],
        out_shape=[jax.ShapeDtypeStruct((S, D_RNN), MXU_DTYPE), jax.ShapeDtypeStruct((S, D_RNN), MXU_DTYPE),
                   jax.ShapeDtypeStruct((4, D_RNN), F32), jax.ShapeDtypeStruct((1, D_RNN), F32),
                   jax.ShapeDtypeStruct((N_LRU_GROUPS, LRU_GROUP, LRU_GROUP), F32),
                   jax.ShapeDtypeStruct((N_LRU_GROUPS, LRU_GROUP, LRU_GROUP), F32),
                   jax.ShapeDtypeStruct((1, D_RNN), F32), jax.ShapeDtypeStruct((1, D_RNN), F32),
                   jax.ShapeDtypeStruct((1, D_RNN), F32)],
        scratch_shapes=[pltpu.VMEM((8, LRU_GROUP), F32), pltpu.VMEM((1, LRU_GROUP), F32),
                        pltpu.VMEM((T, LRU_GROUP), F32), pltpu.VMEM((T, LRU_GROUP), F32), pltpu.VMEM((T, LRU_GROUP), F32)],
        compiler_params=_cparams(("parallel", "arbitrary")),
    )(proj, proj, proj, hl, hl, a_fwd, mult_fwd, dz, conv_w, conv_b, wa_bd, wx_bd, b_a, b_x, lam)
    return res


def _block_diag(w):
    w4 = w.reshape(N_LRU_GROUPS, 4, LRU_BLOCK, 1, LRU_BLOCK)
    eye = jnp.eye(4, dtype=w.dtype).reshape(1, 4, 1, 4, 1)
    return (w4 * eye).reshape(N_LRU_GROUPS, LRU_GROUP, LRU_GROUP)


def _block_diag_extract(wbd):
    w5 = wbd.reshape(N_LRU_GROUPS, 4, LRU_BLOCK, 4, LRU_BLOCK)
    return jnp.stack([w5[:, a, :, a, :] for a in range(4)], axis=1).reshape(N_LRU_BLOCKS, LRU_BLOCK, LRU_BLOCK)


def _t5_bucket(dist):
    max_exact = NUM_BUCKETS // 2
    df = jnp.maximum(dist, 1).astype(jnp.float32)
    large = max_exact + (jnp.log(df / max_exact) / math.log(MAX_DISTANCE / max_exact)
                         * (NUM_BUCKETS - max_exact)).astype(jnp.int32)
    large = jnp.minimum(large, NUM_BUCKETS - 1)
    return jnp.where(dist < max_exact, dist, large)


def _band_offsets():
    qi = jnp.arange(SPAN)[:, None]
    kj = jnp.arange(2 * SPAN)[None, :]
    return qi + SPAN - kj


def _dil_buckets():
    off = _band_offsets()
    return jnp.stack([_t5_bucket(jnp.maximum(off, 0) * dil) for _, dil in DIL_GROUPS]).astype(jnp.int32)


def _dil_bias(rel_bias, buckets):
    def body(tbl_ref, bk_ref, o_ref):
        g = pl.program_id(0)
        qi = lax.broadcasted_iota(jnp.int32, (SPAN, 2 * SPAN), 0)
        kj = lax.broadcasted_iota(jnp.int32, (SPAN, 2 * SPAN), 1)
        off = qi + SPAN - kj
        valid = (off >= 0) & (off <= SPAN)
        bk = bk_ref[0]
        for h in range(DIL_HEADS):
            acc = jnp.zeros((SPAN, 2 * SPAN), F32)
            for b in range(NUM_BUCKETS):
                acc = jnp.where(bk == b, tbl_ref[b, g * DIL_HEADS + h], acc)
            o_ref[0, h] = jnp.where(valid, acc, NEG)

    return pl.pallas_call(
        body, name="dil_bias", grid=(3,),
        in_specs=[pl.BlockSpec(memory_space=pltpu.SMEM), pl.BlockSpec((1, SPAN, 2 * SPAN), lambda g: (g, 0, 0))],
        out_specs=pl.BlockSpec((1, DIL_HEADS, SPAN, 2 * SPAN), lambda g: (g, 0, 0, 0)),
        out_shape=jax.ShapeDtypeStruct((3, DIL_HEADS, SPAN, 2 * SPAN), F32),
        compiler_params=_cparams(("parallel",)),
    )(rel_bias, buckets)


def _dil_bias_bwd(dbias, buckets):
    def body(db_ref, bk_ref, o_ref):
        lane = lax.broadcasted_iota(jnp.int32, (1, 128), 1)
        rows = [jnp.zeros((1, 128), F32) for _ in range(NUM_BUCKETS)]
        for g in range(3):
            bk = bk_ref[g]
            for h in range(DIL_HEADS):
                d = db_ref[g, h]
                for b in range(NUM_BUCKETS):
                    tot = jnp.sum(_colsum(jnp.where(bk == b, d, 0.0)), axis=1, keepdims=True)
                    rows[b] = jnp.where(lane == g * DIL_HEADS + h, tot, rows[b])
        for b in range(NUM_BUCKETS):
            o_ref[b:b + 1, :] = rows[b]

    return pl.pallas_call(
        body, name="dil_bias_bwd",
        out_shape=jax.ShapeDtypeStruct((NUM_BUCKETS, 128), F32),
        compiler_params=_cparams(),
    )(dbias, buckets)


DIL_SUBBLOCKS = (8, 4, 1)


def _dil_layout(g, S):
    dil, m = DIL_GROUPS[g][1], DIL_SUBBLOCKS[g]
    sub = SPAN * dil
    col = [(C_QKV + t * 768 + g * 256) // 128 for t in range(3)]
    return dil, m, sub, S // (sub * m), col


def _residue_rows(b, r, dil):
    return pl.ds(b * SPAN * dil + r, SPAN, stride=dil) if dil > 1 else pl.ds(b * SPAN, SPAN)


def _for_residues(dil, fn):
    if dil <= 4:
        for r in range(dil):
            fn(r)
    else:
        lax.fori_loop(0, dil, lambda r, c: (fn(r), c)[1], 0, unroll=4)


def _pair_scores(qm, k2, bias, first_cols):
    s = _dot(qm, k2, "nt") * (DIL_HEAD_DIM ** -0.5) + bias
    kj = lax.broadcasted_iota(jnp.int32, s.shape, 1)
    return jnp.where(kj < first_cols, NEG, s)


def _dilated_fwd(proj, bias, g, *, S):
    dil, m, sub, nc, (qc, kc, vc) = _dil_layout(g, S)
    R = sub * m
    cur = lambda cb: pl.BlockSpec((R, 128), lambda p, i: (i, cb + p))
    prv = lambda cb: pl.BlockSpec((sub, 128), lambda p, i: (jnp.maximum(i * m - 1, 0), cb + p))
    out = pl.BlockSpec((R, 128), lambda p, i: (i, p))

    def body(q_ref, kp_ref, kc_ref, vp_ref, vc_ref, b_ref, o_ref, lse_ref):
        lane = lax.broadcasted_iota(jnp.int32, (SPAN, 128), 1)
        sels = (lane < DIL_HEAD_DIM, lane >= DIL_HEAD_DIM)
        for b in range(m):
            first_cols = jnp.where(pl.program_id(1) == 0, SPAN, 0) if b == 0 else 0

            def one(r, b=b, first_cols=first_cols):
                rows = _residue_rows(b, r, dil)
                before = (kc_ref, vc_ref, _residue_rows(b - 1, r, dil)) if b else (kp_ref, vp_ref, _residue_rows(0, r, dil))
                q2 = q_ref[rows, :]
                k2 = _mx(jnp.concatenate([before[0][before[2], :], kc_ref[rows, :]], axis=0))
                v2 = _mx(jnp.concatenate([before[1][before[2], :], vc_ref[rows, :]], axis=0))
                qq = jnp.concatenate([jnp.where(sels[0], q2, 0.0), jnp.where(sels[1], q2, 0.0)], axis=0)
                s = _pair_scores(qq, k2, b_ref[0, 0], first_cols)
                mx = jnp.max(s, axis=-1, keepdims=True)
                p = jnp.exp(s - mx)
                den = jnp.sum(p, axis=-1, keepdims=True)
                o = _dot(p, v2) / den
                st = mx + jnp.log(den)
                o_ref[rows, :] = jnp.where(sels[0], o[0:SPAN], o[SPAN:2 * SPAN])
                lse_ref[rows, :] = jnp.where(lane == 0, st[0:SPAN], jnp.where(lane == 1, st[SPAN:2 * SPAN], 0.0))

            _for_residues(dil, one)

    return pl.pallas_call(
        body, name=f"dil_fwd{g}", grid=(2, nc),
        in_specs=[cur(qc), prv(kc), cur(kc), prv(vc), cur(vc),
                  pl.BlockSpec((1, 1, 2 * SPAN, 2 * SPAN), lambda p, i: (g, p, 0, 0))],
        out_specs=[out, out],
        out_shape=[jax.ShapeDtypeStruct((S, 256), F32), jax.ShapeDtypeStruct((S, 256), F32)],
        compiler_params=_cparams(("parallel", "parallel")),
    )(proj, proj, proj, proj, proj, bias.reshape(3, 2, 2 * SPAN, 2 * SPAN))


def _dilated_bwd(proj, do, lse, delta, bias, g, *, S, into=None):
    dil, m, sub, nc, (qc, kc, vc) = _dil_layout(g, S)
    R = sub * m
    cl = lambda i: jnp.minimum(i, nc - 1)
    cur = lambda cb: pl.BlockSpec((R, 128), lambda p, i: (cl(i), cb + p))
    prv = lambda cb: pl.BlockSpec((sub, 128), lambda p, i: (jnp.maximum(cl(i) * m - 1, 0), cb + p))
    q_out = pl.BlockSpec((R, 128), lambda p, i: (cl(i), 2 * g + p))
    kv_out = pl.BlockSpec((R, 128), lambda p, i: (jnp.maximum(i - 1, 0), 2 * g + p))
    scale = DIL_HEAD_DIM ** -0.5
    n_into = 0 if into is None else 3

    def body(q_ref, kp_ref, kc_ref, vp_ref, vc_ref, do_ref, lse_ref, dl_ref, b_ref, *rest):
        dq_ref, dk_ref, dv_ref, db_ref, dq_s, kc_s, vc_s, kp_s, vp_s, kcar, vcar = rest[n_into:]
        i = pl.program_id(1)

        @pl.when(i == 0)
        def _():
            kcar[...] = jnp.zeros_like(kcar)
            vcar[...] = jnp.zeros_like(vcar)
            db_ref[...] = jnp.zeros_like(db_ref)

        @pl.when(i < nc)
        def _():
            lane = lax.broadcasted_iota(jnp.int32, (SPAN, 128), 1)
            sels = (lane < DIL_HEAD_DIM, lane >= DIL_HEAD_DIM)
            for b in range(m):
                first_cols = jnp.where(i == 0, SPAN, 0) if b == 0 else 0

                def one(r, b=b, first_cols=first_cols):
                    rows = _residue_rows(b, r, dil)
                    rows_before = _residue_rows(b - 1 if b else 0, r, dil)
                    k_before, v_before = (kc_ref, vc_ref) if b else (kp_ref, vp_ref)
                    q2, do2 = q_ref[rows, :], do_ref[rows, :]
                    k2 = _mx(jnp.concatenate([k_before[rows_before, :], kc_ref[rows, :]], axis=0))
                    v2 = _mx(jnp.concatenate([v_before[rows_before, :], vc_ref[rows, :]], axis=0))
                    lse_t, dl_t = lse_ref[rows, :], dl_ref[rows, :]
                    qq = _mx(jnp.concatenate([jnp.where(sels[0], q2, 0.0), jnp.where(sels[1], q2, 0.0)], axis=0))
                    dd = _mx(jnp.concatenate([jnp.where(sels[0], do2, 0.0), jnp.where(sels[1], do2, 0.0)], axis=0))
                    lse2 = jnp.concatenate([lse_t[:, 0:1], lse_t[:, 1:2]], axis=0)
                    dl2 = jnp.concatenate([dl_t[:, 0:1], dl_t[:, 1:2]], axis=0)
                    p = jnp.exp(_pair_scores(qq, k2, b_ref[0, 0], first_cols) - lse2)
                    ds = p * (_dot(dd, v2, "nt") - dl2)
                    db_ref[0] += ds
                    dqq = _dot(ds, k2) * scale
                    dq2 = jnp.where(sels[0], dqq[0:SPAN], dqq[SPAN:2 * SPAN])
                    dk2 = _dot(ds, qq, "tn") * scale
                    dv2 = _dot(p, dd, "tn")
                    dq_s[rows, :] = dq2
                    kc_s[rows, :] = dk2[SPAN:2 * SPAN]
                    vc_s[rows, :] = dv2[SPAN:2 * SPAN]
                    if b:
                        kc_s[rows_before, :] += dk2[0:SPAN]
                        vc_s[rows_before, :] += dv2[0:SPAN]
                    else:
                        kp_s[rows_before, :] = dk2[0:SPAN]
                        vp_s[rows_before, :] = dv2[0:SPAN]

                _for_residues(dil, one)
            dq_ref[...] = dq_s[...].astype(dq_ref.dtype)
            last = pl.ds((m - 1) * sub, sub)
            kcar[last, :] += kp_s[...]
            vcar[last, :] += vp_s[...]
            dk_ref[...] = kcar[...].astype(dk_ref.dtype)
            dv_ref[...] = vcar[...].astype(dv_ref.dtype)
            kcar[...] = kc_s[...]
            vcar[...] = vc_s[...]

        @pl.when(i == nc)
        def _():
            dk_ref[...] = kcar[...].astype(dk_ref.dtype)
            dv_ref[...] = vcar[...].astype(dv_ref.dtype)

    stat = pl.BlockSpec((R, 128), lambda p, i: (cl(i), p))
    big = jax.ShapeDtypeStruct((S, len(DIL_GROUPS) * 256), MXU_DTYPE)
    return pl.pallas_call(
        body, name=f"dil_bwd{g}", grid=(2, nc + 1),
        in_specs=[cur(qc), prv(kc), cur(kc), prv(vc), cur(vc), stat, stat, stat,
                  pl.BlockSpec((1, 1, 2 * SPAN, 2 * SPAN), lambda p, i: (g, p, 0, 0))]
        + [pl.BlockSpec(memory_space=pl.ANY)] * n_into,
        out_specs=[q_out, kv_out, kv_out, pl.BlockSpec((1, 2 * SPAN, 2 * SPAN), lambda p, i: (p, 0, 0))],
        out_shape=[big, big, big, jax.ShapeDtypeStruct((2, 2 * SPAN, 2 * SPAN), F32)],
        input_output_aliases={9 + j: j for j in range(n_into)},
        scratch_shapes=[pltpu.VMEM((R, 128), F32)] * 3 + [pltpu.VMEM((sub, 128), F32)] * 2 + [pltpu.VMEM((R, 128), F32)] * 2,
        compiler_params=_cparams(("parallel", "arbitrary")),
    )(proj, proj, proj, proj, proj, do, lse, delta, bias.reshape(3, 2, 2 * SPAN, 2 * SPAN), *(into or ()))


def _dilated_merge(os_, lses, *, S, bt=512):
    tile = pl.BlockSpec((bt, 128), lambda i, p: (i, p))

    def body(o0, o1, o2, l0, l1, l2, o_ref, om_ref, lse_ref):
        lane = lax.broadcasted_iota(jnp.int32, (bt, 128), 1)
        lo = lane < DIL_HEAD_DIM
        ls = [l0[...], l1[...], l2[...]]
        ws, stat = [], jnp.zeros((bt, 128), F32)
        for e in range(2):
            a = [l[:, e:e + 1] for l in ls]
            m = jnp.maximum(jnp.maximum(a[0], a[1]), a[2])
            ex = [jnp.exp(v - m) for v in a]
            tot = ex[0] + ex[1] + ex[2]
            ws.append([v / tot for v in ex])
            stat = jnp.where(lane == e, m + jnp.log(tot), stat)
        acc = jnp.zeros((bt, 128), F32)
        for gi, o in enumerate((o0, o1, o2)):
            acc = acc + jnp.where(lo, ws[0][gi], ws[1][gi]) * o[...]
        o_ref[...] = acc
        om_ref[...] = _mx(acc)
        lse_ref[...] = stat

    return pl.pallas_call(
        body, name="dil_merge", grid=(S // bt, 2),
        in_specs=[tile] * 6, out_specs=[tile, tile, tile],
        out_shape=[jax.ShapeDtypeStruct((S, 256), F32), jax.ShapeDtypeStruct((S, 256), MXU_DTYPE),
                   jax.ShapeDtypeStruct((S, 256), F32)],
        compiler_params=_cparams(("parallel", "parallel")),
    )(*os_, *lses)


def _dilated_delta(do, o, *, S, bt=512):
    tile = pl.BlockSpec((bt, 128), lambda i, p: (i, p))

    def body(do_ref, o_ref, d_ref):
        lane = lax.broadcasted_iota(jnp.int32, (bt, 128), 1)
        prod = do_ref[...] * o_ref[...]
        d0 = jnp.sum(jnp.where(lane < DIL_HEAD_DIM, prod, 0.0), axis=-1, keepdims=True)
        d1 = jnp.sum(jnp.where(lane >= DIL_HEAD_DIM, prod, 0.0), axis=-1, keepdims=True)
        d_ref[...] = jnp.where(lane == 0, d0, jnp.where(lane == 1, d1, 0.0))

    return pl.pallas_call(
        body, name="dil_delta", grid=(S // bt, 2), in_specs=[tile, tile], out_specs=tile,
        out_shape=jax.ShapeDtypeStruct((S, 256), F32), compiler_params=_cparams(("parallel", "parallel")),
    )(do, o)


MEM_T = 2048
QM_BLK = C_QM // MEM_HEAD_DIM


def _mem_attn_fwd(proj, kv, *, S):
    scale = MEM_HEAD_DIM ** -0.5

    def body(q_ref, k_ref, v_ref, o_ref, om_ref, lse_ref):
        s = _dot(q_ref[...], k_ref[...], "nt") * scale
        m = jnp.max(s, axis=-1, keepdims=True)
        p = jnp.exp(s - m)
        den = jnp.sum(p, axis=-1, keepdims=True)
        o = _dot(p, v_ref[...]) / den
        o_ref[...] = o
        om_ref[...] = _mx(o)
        lse_ref[0] = m + jnp.log(den)

    return pl.pallas_call(
        body, name="mem_attn_fwd", grid=(S // MEM_T, MEM_HEADS),
        in_specs=[pl.BlockSpec((MEM_T, MEM_HEAD_DIM), lambda i, h: (i, QM_BLK + h)),
                  pl.BlockSpec((N_MEM, MEM_HEAD_DIM), lambda i, h: (0, h)),
                  pl.BlockSpec((N_MEM, MEM_HEAD_DIM), lambda i, h: (0, MEM_HEADS + h))],
        out_specs=[pl.BlockSpec((MEM_T, MEM_HEAD_DIM), lambda i, h: (i, h)),
                   pl.BlockSpec((MEM_T, MEM_HEAD_DIM), lambda i, h: (i, h)),
                   pl.BlockSpec((1, MEM_T, 1), lambda i, h: (h, i, 0))],
        out_shape=[jax.ShapeDtypeStruct((S, MEM_WIDTH), F32), jax.ShapeDtypeStruct((S, MEM_WIDTH), MXU_DTYPE),
                   jax.ShapeDtypeStruct((MEM_HEADS, S, 1), F32)],
        compiler_params=_cparams(("parallel", "parallel")),
    )(proj, kv, kv)


def _mem_attn_bwd(proj, kv, om, lse, dom, *, S):
    scale = MEM_HEAD_DIM ** -0.5

    def body(q_ref, k_ref, v_ref, o_ref, lse_ref, do_ref, dq_ref, dk_ref, dv_ref):
        @pl.when(pl.program_id(1) == 0)
        def _():
            dk_ref[...] = jnp.zeros_like(dk_ref)
            dv_ref[...] = jnp.zeros_like(dv_ref)

        qv, kv_, vv, dov = q_ref[...], k_ref[...], v_ref[...], do_ref[...]
        p = jnp.exp(_dot(qv, kv_, "nt") * scale - lse_ref[0])
        delta = jnp.sum(dov * o_ref[...], axis=-1, keepdims=True)
        ds = p * (_dot(dov, vv, "nt") - delta)
        dq_ref[...] = (_dot(ds, kv_) * scale).astype(dq_ref.dtype)
        dk_ref[...] += _dot(ds, qv, "tn") * scale
        dv_ref[...] += _dot(p, dov, "tn")

    tile = pl.BlockSpec((MEM_T, MEM_HEAD_DIM), lambda h, i: (i, h))
    kvo = pl.BlockSpec((N_MEM, MEM_HEAD_DIM), lambda h, i: (0, h))
    return pl.pallas_call(
        body, name="mem_attn_bwd", grid=(MEM_HEADS, S // MEM_T),
        in_specs=[pl.BlockSpec((MEM_T, MEM_HEAD_DIM), lambda h, i: (i, QM_BLK + h)),
                  pl.BlockSpec((N_MEM, MEM_HEAD_DIM), lambda h, i: (0, h)),
                  pl.BlockSpec((N_MEM, MEM_HEAD_DIM), lambda h, i: (0, MEM_HEADS + h)),
                  tile, pl.BlockSpec((1, MEM_T, 1), lambda h, i: (h, i, 0)), tile],
        out_specs=[tile, kvo, kvo],
        out_shape=[jax.ShapeDtypeStruct((S, MEM_WIDTH), MXU_DTYPE), jax.ShapeDtypeStruct((N_MEM, MEM_WIDTH), F32),
                   jax.ShapeDtypeStruct((N_MEM, MEM_WIDTH), F32)],
        compiler_params=_cparams(("parallel", "arbitrary")),
    )(proj, kv, kv, om, lse, dom)


MIX_BM = 1024
MIX_BN = 256
GATES_BLK = C_GATES // MIX_BN


def _mix_specs(j_outer):
    ix = (lambda f: (lambda j, i: f(i, j))) if j_outer else (lambda f: f)
    act = lambda width: pl.BlockSpec((MIX_BM, width), ix(lambda i, j: (i, 0)))
    wgt = lambda width: pl.BlockSpec((width, MIX_BN), ix(lambda i, j: (0, j)))
    gate = lambda b: pl.BlockSpec((MIX_BM, MIX_BN), ix(lambda i, j: (i, GATES_BLK + 4 * b + j)))
    bias = lambda b: pl.BlockSpec((1, MIX_BN), ix(lambda i, j: (0, 4 * b + j)))
    tile = pl.BlockSpec((MIX_BM, MIX_BN), ix(lambda i, j: (i, j)))
    return act, wgt, gate, bias, tile


def _mix_fwd(z_lru, o_dil, om, w_lru, w_dil, w_mem, proj, b_gate, *, S):
    act, wgt, gate, bias, tile = _mix_specs(False)

    def body(zl, od, mo, wl, wd, wm, g0, g1, g2, b0, b1, b2, o_ref):
        acc = jax.nn.sigmoid(g0[...] + b0[...]) * _dot(zl[...], wl[...])
        acc += jax.nn.sigmoid(g1[...] + b1[...]) * _dot(od[...], wd[...])
        acc += jax.nn.sigmoid(g2[...] + b2[...]) * _dot(mo[...], wm[...])
        o_ref[...] = acc.astype(o_ref.dtype)

    return pl.pallas_call(
        body, name="mix_fwd", grid=(S // MIX_BM, D_MODEL // MIX_BN),
        in_specs=[act(D_RNN), act(256), act(MEM_WIDTH), wgt(D_RNN), wgt(256), wgt(MEM_WIDTH),
                  gate(0), gate(1), gate(2), bias(0), bias(1), bias(2)],
        out_specs=tile, out_shape=jax.ShapeDtypeStruct((S, D_MODEL), MXU_DTYPE),
        compiler_params=_cparams(("parallel", "parallel")),
    )(z_lru, o_dil, om, w_lru, w_dil, w_mem, proj, proj, proj, b_gate, b_gate, b_gate)


def _mix_bwd(dmerged, z_lru, o_dil, om, w_lru, w_dil, w_mem, proj, b_gate, *, S):
    act, wgt, gate, bias, tile = _mix_specs(True)

    def body(dm, zl, od, mo, wl, wd, wm, g0, g1, g2, b0, b1, b2,
             dg0, dg1, dg2, dy0, dy1, dy2, db0, db1, db2):
        @pl.when(pl.program_id(1) == 0)
        def _():
            for r in (db0, db1, db2):
                r[...] = jnp.zeros_like(r)

        dmv = dm[...]
        for act_ref, w_ref, g_ref, b_ref, dg_ref, dy_ref, db_ref in (
                (zl, wl, g0, b0, dg0, dy0, db0), (od, wd, g1, b1, dg1, dy1, db1), (mo, wm, g2, b2, dg2, dy2, db2)):
            y = _dot(act_ref[...], w_ref[...])
            gt = jax.nn.sigmoid(g_ref[...] + b_ref[...])
            dgate = dmv * y * gt * (1.0 - gt)
            dg_ref[...] = dgate.astype(dg_ref.dtype)
            dy_ref[...] = (dmv * gt).astype(dy_ref.dtype)
            db_ref[...] += _colsum(dgate)

    big = jax.ShapeDtypeStruct((S, D_MODEL), MXU_DTYPE)
    vec = jax.ShapeDtypeStruct((1, D_MODEL), F32)
    vspec = pl.BlockSpec((1, MIX_BN), lambda j, i: (0, j))
    return pl.pallas_call(
        body, name="mix_bwd", grid=(D_MODEL // MIX_BN, S // MIX_BM),
        in_specs=[tile, act(D_RNN), act(256), act(MEM_WIDTH), wgt(D_RNN), wgt(256), wgt(MEM_WIDTH),
                  gate(0), gate(1), gate(2), bias(0), bias(1), bias(2)],
        out_specs=[tile] * 6 + [vspec] * 3, out_shape=[big] * 6 + [vec] * 3,
        compiler_params=_cparams(("parallel", "arbitrary")),
    )(dmerged, z_lru, o_dil, om, w_lru, w_dil, w_mem, proj, proj, proj, b_gate, b_gate, b_gate)


def _adamw_math(w, g, m, v):
    m = ADAM_B1 * m + (1.0 - ADAM_B1) * g
    v = ADAM_B2 * v + (1.0 - ADAM_B2) * (g * g)
    m_hat = m / (1.0 - ADAM_B1 ** ADAM_STEP)
    v_hat = v / (1.0 - ADAM_B2 ** ADAM_STEP)
    delta = -ADAM_LR * (m_hat / (jnp.sqrt(v_hat) + ADAM_EPS) + ADAM_WD * w)
    return delta, m, v


def _adamw_landed(w, own, land, m, v, *, name, col_blk=0, prev=None):
    R = w.shape[0]
    n_parts, C = land.shape[0], land.shape[2]
    br = next(d for d in (256, 464, 128) if R % d == 0)
    tile = pl.BlockSpec((br, C), lambda i: (i, col_blk))
    part = pl.BlockSpec((br, C), lambda i: (i, 0))
    n_prev = 0 if prev is None else 4

    def body(w_ref, o_ref, l_ref, m_ref, v_ref, *rest):
        g_ref, d_ref, nm_ref, nv_ref = rest[n_prev:]
        g = o_ref[...].astype(F32)
        for p in range(n_parts):
            g = g + l_ref[p].astype(F32)
        d, nm, nv = _adamw_math(w_ref[...], g, m_ref[...], v_ref[...])
        g_ref[...] = g
        d_ref[...] = d
        nm_ref[...] = nm
        nv_ref[...] = nv

    return pl.pallas_call(
        body, name=name, grid=(R // br,),
        in_specs=[tile, part, pl.BlockSpec((n_parts, br, C), lambda i: (0, i, 0)), tile, tile]
        + [pl.BlockSpec(memory_space=pl.ANY)] * n_prev,
        out_specs=[tile] * 4, out_shape=[jax.ShapeDtypeStruct(w.shape, F32)] * 4,
        input_output_aliases={5 + j: j for j in range(n_prev)},
        compiler_params=_cparams(("parallel",)),
    )(w, own, land, m, v, *(prev or ()))


def _adamw_plain(w, g, m, v, *, name):
    def body(w_ref, g_ref, m_ref, v_ref, d_ref, nm_ref, nv_ref):
        d, nm, nv = _adamw_math(w_ref[...], g_ref[...], m_ref[...], v_ref[...])
        d_ref[...] = d
        nm_ref[...] = nm
        nv_ref[...] = nv

    return pl.pallas_call(
        body, name=name, out_shape=[jax.ShapeDtypeStruct(w.shape, F32)] * 3, compiler_params=_cparams(),
    )(w, g, m, v)


def _my_pos():
    return lax.axis_index("x"), lax.axis_index("y"), lax.axis_index("c")


def _dev_index(p):
    return 4 * p[0] + 2 * p[1] + p[2]


def _all_gather(shards):
    n = len(shards)
    hbm = pl.BlockSpec(memory_space=pl.ANY)

    def body(*refs):
        ins, outs = refs[:n], refs[n:2 * n]
        send_sems, recv_sems, local_sems = refs[2 * n:]
        x, y, c = _my_pos()
        me, sibling = (x, y, c), (x, y, 1 - c)
        chips = [(1 - x, y), (x, 1 - y), (1 - x, 1 - y)]

        def copy(a, k, block, to, src=None):
            dst = outs[a].at[_dev_index(block)]
            return pltpu.make_async_remote_copy(
                src_ref=dst if src is None else src, dst_ref=dst,
                send_sem=send_sems.at[a, k], recv_sem=recv_sems.at[a, k], device_id=to, device_id_type=MESH)

        mine = [pltpu.make_async_copy(ins[a], outs[a].at[_dev_index(me)], local_sems.at[a]) for a in range(n)]
        for cp in mine:
            cp.start()
        first = []
        for a in range(n):
            first.append(copy(a, 0, me, sibling, src=ins[a]))
            first += [copy(a, 1 + j, me, (*chip, c), src=ins[a]) for j, chip in enumerate(chips)]
        for cp in first:
            cp.start()
        passed = []
        for j, chip in enumerate(chips):
            for a in range(n):
                copy(a, 1 + j, (*chip, c), me).wait_recv()
                fwd = copy(a, 4 + j, (*chip, c), sibling)
                fwd.start()
                passed.append(fwd)
        for a in range(n):
            copy(a, 0, sibling, me).wait_recv()
        for j, chip in enumerate(chips):
            for a in range(n):
                copy(a, 4 + j, (*chip, 1 - c), me).wait_recv()
        for cp in first + passed:
            cp.wait_send()
        for cp in mine:
            cp.wait()

    return pl.pallas_call(
        body, name="all_gather_weights",
        in_specs=[hbm] * n, out_specs=[hbm] * n,
        out_shape=[jax.ShapeDtypeStruct((N_DEV,) + s.shape, s.dtype) for s in shards],
        scratch_shapes=[pltpu.SemaphoreType.DMA((n, 7)), pltpu.SemaphoreType.DMA((n, 7)), pltpu.SemaphoreType.DMA((n,))],
        compiler_params=pltpu.CompilerParams(has_side_effects=True),
    )(*shards)


def _peers(me):
    x, y, c = me
    out = []
    for k in range(1, 8):
        fx, fy, fc = (k >> 2) & 1, (k >> 1) & 1, k & 1
        out.append((k - 1, (1 - x if fx else x, 1 - y if fy else y, 1 - c if fc else c)))
    return out


HBM_SPEC = pl.BlockSpec(memory_space=pltpu.HBM)
SEM_SPEC = pl.BlockSpec(memory_space=pltpu.SEMAPHORE)
DATAFLOW_EFFECT = pltpu.SideEffectType.DATAFLOW_SIDE_EFFECTING


def _gather_refs(src, land, me, peer, k):
    return src, land.at[_dev_index(me)]


def _scatter_refs(src, land, me, peer, k):
    return src.at[_dev_index(peer)], land.at[k]


def _push_start(srcs, land_shapes, refs_of, name, after=()):
    n, n_after = len(srcs), len(after)

    def body(*refs):
        ins, lands = refs[:n], refs[n:2 * n]
        send_sems, recv_sems, token = refs[2 * n + n_after], refs[2 * n + n_after + 1], refs[-1]
        me = _my_pos()
        for k, peer in _peers(me):
            for a in range(n):
                src, dst = refs_of(ins[a], lands[a], me, peer, k)
                pltpu.make_async_remote_copy(src_ref=src, dst_ref=dst, send_sem=send_sems.at[7 * a + k],
                                             recv_sem=recv_sems.at[7 * a + k], device_id=peer, device_id_type=MESH).start()
        token[...] = jnp.zeros_like(token)

    lands = [lax.empty(shp, s.dtype) for shp, s in zip(land_shapes, srcs)]
    hbm = lambda a: pltpu.with_memory_space_constraint(a, pltpu.HBM)
    res = pl.pallas_call(
        body, name=name,
        out_shape=(pltpu.SemaphoreType.DMA((7 * n,)), pltpu.SemaphoreType.DMA((7 * n,)),
                   *[pltpu.HBM(s.shape, s.dtype) for s in srcs], *[pltpu.HBM(l.shape, l.dtype) for l in lands],
                   jax.ShapeDtypeStruct((8, 128), F32)),
        in_specs=[HBM_SPEC] * (2 * n) + [pl.BlockSpec(memory_space=pl.ANY)] * n_after,
        out_specs=(SEM_SPEC, SEM_SPEC, *[HBM_SPEC] * (2 * n), pl.BlockSpec(memory_space=pltpu.VMEM)),
        input_output_aliases={i: 2 + i for i in range(2 * n)},
        compiler_params=pltpu.CompilerParams(has_side_effects=DATAFLOW_EFFECT),
    )(*[hbm(s) for s in srcs], *[hbm(l) for l in lands], *after)
    return dict(sems=(res[0], res[1]), srcs=list(res[2:2 + n]), lands=list(res[2 + n:2 + 2 * n]), token=res[-1], n=n,
                refs_of=refs_of, name=name)


def _push_wait(started, after):
    n, refs_of = started["n"], started["refs_of"]
    after = list(after) if isinstance(after, (list, tuple)) else [after]

    def body(*refs):
        ins, lands = refs[:n], refs[n:2 * n]
        send_sems, recv_sems = refs[2 * n], refs[2 * n + 1]
        me = _my_pos()
        for k, peer in _peers(me):
            for a in range(n):
                src, dst = refs_of(ins[a], lands[a], me, peer, k)
                cp = pltpu.make_async_remote_copy(src_ref=src, dst_ref=dst, send_sem=send_sems.at[7 * a + k],
                                                  recv_sem=recv_sems.at[7 * a + k], device_id=peer, device_id_type=MESH)
                cp.wait_send()
                cp.wait_recv()

    arrs = started["srcs"] + started["lands"]
    res = pl.pallas_call(
        body, name=started["name"].replace("start", "wait"),
        out_shape=tuple(pltpu.HBM(a.shape, a.dtype) for a in arrs),
        in_specs=[HBM_SPEC] * (2 * n) + [SEM_SPEC, SEM_SPEC] + [pl.BlockSpec(memory_space=pl.ANY)] * len(after),
        out_specs=tuple([HBM_SPEC] * (2 * n)),
        input_output_aliases={i: i for i in range(2 * n)},
        compiler_params=pltpu.CompilerParams(has_side_effects=DATAFLOW_EFFECT),
    )(*arrs, *started["sems"], *after)
    return list(res[n:2 * n])


def _sum_slots(slots):
    def body(in_ref, out_ref):
        acc = in_ref[0]
        for d in range(1, N_DEV):
            acc = acc + in_ref[d]
        out_ref[...] = acc

    return pl.pallas_call(body, name="sum_small", out_shape=jax.ShapeDtypeStruct(slots.shape[1:], F32),
                          compiler_params=_cparams())(slots)


def _adamw_many(ws, gs, ms, vs):
    n = len(ws)

    def body(*refs):
        for i in range(n):
            w_ref, g_ref, m_ref, v_ref = (refs[j * n + i] for j in range(4))
            d_, nm, nv = _adamw_math(w_ref[...], g_ref[...], m_ref[...], v_ref[...])
            for j, val in enumerate((d_, nm, nv)):
                refs[(4 + j) * n + i][...] = val

    res = pl.pallas_call(body, name="adamw_small", out_shape=[jax.ShapeDtypeStruct(w_.shape, F32) for w_ in ws] * 3,
                         compiler_params=_cparams())(*ws, *gs, *ms, *vs)
    return [(res[i], res[n + i], res[2 * n + i]) for i in range(n)]


def _local_step(x, mem, tgt, W, P, late_weights, send_grads, reduce_small, tie0):
    S = x.shape[0]
    W = dict(W)
    h = _rmsnorm_fwd(x, P["g_mix"] + tie0, rows=S, name="norm_mix")
    proj = _matmul(h, W["w_in_t"], M=S, N=D_IN, K=D_MODEL, mode="nt", bm=512, bn=D_IN // 2, bk=D_MODEL, name="mm_in",
                   j_outer=True)

    wa_bd, wx_bd = _mx(_block_diag(P["w_rg_a"])), _mx(_block_diag(P["w_rg_x"]))
    lru_args = (W["conv_w"], P["conv_b"].reshape(1, -1), wa_bd, wx_bd, P["b_rg_a"].reshape(1, -1),
                P["b_rg_x"].reshape(1, -1), P["lru_lambda"].reshape(1, -1))
    hl, z_lru, a_lru, mult_lru = _lru_fwd(proj, *lru_args, S=S)

    buckets = _dil_buckets()
    bias = _dil_bias(P["rel_bias"], buckets)
    group_out = [_dilated_fwd(proj, bias, g, S=S) for g in range(len(DIL_GROUPS))]
    o_dil, o_dil_m, lse_dil = _dilated_merge([o for o, _ in group_out], [l for _, l in group_out], S=S)

    W.update(late_weights("branch", [o_dil, z_lru]))
    mem_n = _rmsnorm_fwd(mem, P["g_mem"], rows=N_MEM, name="norm_mem")
    kv = _matmul(mem_n, W["w_mem_kv"], M=N_MEM, N=2 * MEM_WIDTH, K=D_MODEL, mode="nn", bm=N_MEM, bn=512, bk=D_MODEL,
                 name="mm_kv")
    om, om_m, lse_mem = _mem_attn_fwd(proj, kv, S=S)
    b_gate = P["b_gate"].reshape(1, -1)
    merged = _mix_fwd(z_lru, o_dil_m, om_m, W["w_lru_out"], W["w_dil_out"], W["w_mem_out"], proj, b_gate, S=S)
    g_mlp, g_final, g_mix = (P[n].reshape(1, D_MODEL) for n in ("g_mlp", "g_final", "g_mix"))
    x1, hm = _matmul_rows(merged, W["w_out"], M=S, K=D_MODEL, mode="nn", bm=512, name="mm_out",
                          row_fn=_residual_then_norm, out_dtypes=(F32, MXU_DTYPE), tiles=[x], vecs=[g_mlp])
    W.update(late_weights("mlp", [hm]))

    def relu2(acc):
        rl = jnp.maximum(acc, 0.0)
        return rl * rl, rl

    act, relu_u = _matmul(hm, W["w_mlp_in_t"], M=S, N=D_FF, K=D_MODEL, mode="nt", bm=1024, bn=1024, bk=D_MODEL,
                          name="mm_mlp_in", out_dtypes=(MXU_DTYPE, MXU_DTYPE), epilogue=relu2, j_outer=True)
    dx2, dx2_m, loss, dg_final = _matmul_rows(
        act, W["w_mlp_out"], M=S, K=D_FF, mode="nn", bm=512, name="mm_mlp_out", row_fn=_residual_then_loss,
        out_dtypes=(F32, MXU_DTYPE), tiles=[x1, tgt], vecs=[g_final], acc_widths=(1, D_MODEL))

    G, Gs = {}, {}
    Gs["g_final"] = dg_final
    dw = dict(mode="tn", K=S, bk=S, out_dtypes=(MXU_DTYPE,))
    G["w_mlp_out"] = _matmul(act, dx2_m, M=D_FF, N=D_MODEL, bm=512, bn=D_MODEL, name="mm_dw_mlp_out",
                             parts=("rows", D_FF // N_DEV), **dw)
    du = _matmul(dx2_m, W["w_mlp_out"], M=S, N=D_FF, K=D_MODEL, mode="nt", bm=1024, bn=1024, bk=D_MODEL, name="mm_du",
                 out_dtypes=(MXU_DTYPE,), epilogue=lambda acc, rl: (acc * (2.0 * rl.astype(F32)),),
                 extras=[(relu_u, (0, 0))], j_outer=True)
    G["w_mlp_in"] = _matmul(hm, du, M=D_MODEL, N=D_FF, bm=D_MODEL, bn=512, name="mm_dw_mlp_in",
                            parts=("cols", D_FF // N_DEV), **dw)
    tie1 = send_grads({n: G.pop(n) for n in ("w_mlp_out", "w_mlp_in")})
    dx1, dx1_m, Gs["g_mlp"] = _matmul_rows(
        du, W["w_mlp_in_t"], M=S, K=D_FF, mode="nn", bm=512, name="mm_dhm", row_fn=_norm_bwd_then_residual(2),
        out_dtypes=(F32, MXU_DTYPE), tiles=[x1, dx2], vecs=[g_mlp], acc_widths=(D_MODEL,), deps=[tie1])
    G["w_out"] = _matmul(merged, dx1_m, M=D_MODEL, N=D_MODEL, bm=512, bn=D_MODEL, name="mm_dw_out",
                         parts=("rows", D_MODEL // N_DEV), **dw)
    dmerged = _matmul(dx1_m, W["w_out"], M=S, N=D_MODEL, K=D_MODEL, mode="nt", bm=512, bn=D_MODEL, bk=D_MODEL, name="mm_dmerged")
    (dg0, dg1, dg2, dy_lru, dy_dil, dy_mem, db0, db1, db2) = _mix_bwd(
        dmerged, z_lru, o_dil_m, om_m, W["w_lru_out"], W["w_dil_out"], W["w_mem_out"], proj, b_gate, S=S)
    Gs["b_gate0"], Gs["b_gate1"], Gs["b_gate2"] = db0, db1, db2

    G["w_mem_out"] = _matmul(om_m, dy_mem, M=MEM_WIDTH, N=D_MODEL, bm=MEM_WIDTH, bn=D_MODEL, name="mm_dw_mem_out",
                             parts=("cols", D_MODEL // N_DEV), **dw)
    dom = _matmul(dy_mem, W["w_mem_out"], M=S, N=MEM_WIDTH, K=D_MODEL, mode="nt", bm=512, bn=MEM_WIDTH, bk=D_MODEL,
                  name="mm_dom")
    dqm, dk_mem, dv_mem = _mem_attn_bwd(proj, kv, om, lse_mem, dom, S=S)
    dkv = jnp.concatenate([dk_mem, dv_mem], axis=1)
    G["w_mem_kv"] = _matmul(mem_n, dkv, M=D_MODEL, N=2 * MEM_WIDTH, K=N_MEM, mode="tn", bm=D_MODEL, bn=2 * MEM_WIDTH,
                            bk=N_MEM, name="mm_dw_kv", out_dtypes=(MXU_DTYPE,), parts=("rows", D_MODEL // N_DEV))
    dmem_n = _matmul(dkv, W["w_mem_kv"], M=N_MEM, N=D_MODEL, K=2 * MEM_WIDTH, mode="nt", bm=N_MEM, bn=D_MODEL,
                     bk=2 * MEM_WIDTH, name="mm_dmem")
    (Gs["g_mem"],) = _rmsnorm_bwd(mem, P["g_mem"], dmem_n, None, rows=N_MEM, name="norm_mem_bwd", dx_dtypes=())

    G["w_dil_out"] = _matmul(o_dil_m, dy_dil, M=256, N=D_MODEL, bm=256, bn=D_MODEL, name="mm_dw_dil_out",
                             parts=("cols", D_MODEL // N_DEV), **dw)
    do_dil = _matmul(dy_dil, W["w_dil_out"], M=S, N=256, K=D_MODEL, mode="nt", bm=512, bn=256, bk=D_MODEL, name="mm_do_dil")
    G["w_lru_out"] = _matmul(z_lru, dy_lru, M=D_RNN, N=D_MODEL, bm=D_RNN, bn=D_MODEL, name="mm_dw_lru_out",
                             parts=("cols", D_MODEL // N_DEV), **dw)
    dz = _matmul(dy_lru, W["w_lru_out"], M=S, N=D_RNN, K=D_MODEL, mode="nt", bm=512, bn=D_RNN, bk=D_MODEL, name="mm_dz_lru")
    tie2 = send_grads({n: G.pop(n) for n in ("w_out", "w_mem_out", "w_mem_kv", "w_dil_out", "w_lru_out")})
    bias = bias + tie2[0, 0]
    delta = _dilated_delta(do_dil, o_dil, S=S)
    dqkv, dbias = None, []
    for g in range(len(DIL_GROUPS)):
        *dqkv, db_g = _dilated_bwd(proj, do_dil, lse_dil, delta, bias, g, S=S, into=dqkv)
        dbias.append(db_g)
    drel = _dil_bias_bwd(jnp.stack(dbias, axis=0).reshape(len(DIL_GROUPS), DIL_HEADS, SPAN, 2 * SPAN), buckets)
    Gs["rel_bias"] = drel

    dxl, dgl, dcw, dcb, dwa, dwx, dba, dbx, dlam = _lru_bwd(proj, hl, a_lru, mult_lru, dz, *lru_args, S=S)
    Gs["conv_w"], Gs["conv_b"] = dcw, dcb
    Gs["w_rg_a"], Gs["w_rg_x"] = _block_diag_extract(dwa), _block_diag_extract(dwx)
    Gs["b_rg_a"], Gs["b_rg_x"], Gs["lru_lambda"] = dba, dbx, dlam
    Gs["loss"] = loss

    dproj = [dxl, dgl] + dqkv + [dqm, dg0, dg1, dg2]
    tie = []
    for q in range(W_IN_PIECES):
        dw_q = None
        for half in range(2):
            dw_q = _dw_in_t_half(h, dproj, q, half, S=S, name=f"mm_dw_in_{q}_{half}", into=dw_q, deps=tie)
        tie = [send_grads({f"w_in_{q}": dw_q})]
    grad_x, Gs["g_mix"] = _matmul_rows(
        dproj, W["w_in_t"], M=S, K=D_IN, mode="nn", bm=256, name="mm_dh", row_fn=_norm_bwd_then_residual(1),
        out_dtypes=(F32,), tiles=[x, dx1], vecs=[g_mix], acc_widths=(D_MODEL,), deps=tie)
    return grad_x, reduce_small(Gs)


BIG = ("w_in", "w_lru_out", "w_dil_out", "w_mem_kv", "w_mem_out", "w_out", "w_mlp_in", "w_mlp_out")
W_IN_PIECES = 2
COL_SHARDED = ("w_lru_out", "w_dil_out", "w_mem_out", "w_mlp_in")
GATHERED_TRANSPOSED = ("w_mlp_in",)
SMALL = ("g_mix", "b_gate", "conv_b", "w_rg_a", "b_rg_a", "w_rg_x", "b_rg_x", "lru_lambda", "rel_bias", "g_mem",
         "g_mlp", "g_final")
WEIGHTS = ("g_mix", "w_in", "b_gate", "conv_w", "conv_b", "w_rg_a", "b_rg_a", "w_rg_x", "b_rg_x", "lru_lambda",
           "w_lru_out", "rel_bias", "w_dil_out", "g_mem", "w_mem_kv", "w_mem_out", "w_out", "g_mlp", "w_mlp_in",
           "w_mlp_out", "g_final")


def _gathered_to_full(name, gathered):
    if name in COL_SHARDED:
        n, r, c = gathered.shape
        return gathered.transpose(1, 0, 2).reshape(r, n * c)
    n, r, c = gathered.shape
    return gathered.reshape(n * r, c)


SMALL_GRADS = (("g_mix", (1, 1024)), ("b_gate0", (1, 1024)), ("b_gate1", (1, 1024)), ("b_gate2", (1, 1024)),
               ("conv_b", (1, 768)), ("w_rg_a", (12, 64, 64)), ("b_rg_a", (1, 768)), ("w_rg_x", (12, 64, 64)),
               ("b_rg_x", (1, 768)), ("lru_lambda", (1, 768)), ("rel_bias", (32, 128)), ("g_mem", (1, 1024)),
               ("g_mlp", (1, 1024)), ("g_final", (1, 1024)), ("conv_w", (4, 768)), ("loss", (1, 1)))


def _pack(parts):
    flat = jnp.concatenate([p.reshape(-1) for p in parts])
    return jnp.pad(flat, (0, (-flat.shape[0]) % 1024)).reshape(-1, 128)


def _unpack(pack, shapes):
    flat = pack.reshape(-1)
    out, off = [], 0
    for shp in shapes:
        size = math.prod(shp)
        out.append(flat[off:off + size].reshape(shp))
        off += size
    return out


def kernel(x, mem, g_mix, w_in, b_gate, conv_w, conv_b, w_rg_a, b_rg_a, w_rg_x, b_rg_x, lru_lambda, w_lru_out, rel_bias, w_dil_out, g_mem, w_mem_kv, w_mem_out, w_out, g_mlp, w_mlp_in, w_mlp_out, g_final, loss_target, m_g_mix, m_w_in, m_b_gate, m_conv_w, m_conv_b, m_w_rg_a, m_b_rg_a, m_w_rg_x, m_b_rg_x, m_lru_lambda, m_w_lru_out, m_rel_bias, m_w_dil_out, m_g_mem, m_w_mem_kv, m_w_mem_out, m_w_out, m_g_mlp, m_w_mlp_in, m_w_mlp_out, m_g_final, v_g_mix, v_w_in, v_b_gate, v_conv_w, v_conv_b, v_w_rg_a, v_b_rg_a, v_w_rg_x, v_b_rg_x, v_lru_lambda, v_w_lru_out, v_rel_bias, v_w_dil_out, v_g_mem, v_w_mem_kv, v_w_mem_out, v_w_out, v_g_mlp, v_w_mlp_in, v_w_mlp_out, v_g_final):
    w = dict(g_mix=g_mix, w_in=w_in, b_gate=b_gate, conv_w=conv_w, conv_b=conv_b, w_rg_a=w_rg_a, b_rg_a=b_rg_a,
             w_rg_x=w_rg_x, b_rg_x=b_rg_x, lru_lambda=lru_lambda, w_lru_out=w_lru_out, rel_bias=rel_bias,
             w_dil_out=w_dil_out, g_mem=g_mem, w_mem_kv=w_mem_kv, w_mem_out=w_mem_out, w_out=w_out, g_mlp=g_mlp,
             w_mlp_in=w_mlp_in, w_mlp_out=w_mlp_out, g_final=g_final)
    m = dict(g_mix=m_g_mix, w_in=m_w_in, b_gate=m_b_gate, conv_w=m_conv_w, conv_b=m_conv_b, w_rg_a=m_w_rg_a,
             b_rg_a=m_b_rg_a, w_rg_x=m_w_rg_x, b_rg_x=m_b_rg_x, lru_lambda=m_lru_lambda, w_lru_out=m_w_lru_out,
             rel_bias=m_rel_bias, w_dil_out=m_w_dil_out, g_mem=m_g_mem, w_mem_kv=m_w_mem_kv, w_mem_out=m_w_mem_out,
             w_out=m_w_out, g_mlp=m_g_mlp, w_mlp_in=m_w_mlp_in, w_mlp_out=m_w_mlp_out, g_final=m_g_final)
    v = dict(g_mix=v_g_mix, w_in=v_w_in, b_gate=v_b_gate, conv_w=v_conv_w, conv_b=v_conv_b, w_rg_a=v_w_rg_a,
             b_rg_a=v_b_rg_a, w_rg_x=v_w_rg_x, b_rg_x=v_b_rg_x, lru_lambda=v_lru_lambda, w_lru_out=v_w_lru_out,
             rel_bias=v_rel_bias, w_dil_out=v_w_dil_out, g_mem=v_g_mem, w_mem_kv=v_w_mem_kv, w_mem_out=v_w_mem_out,
             w_out=v_w_out, g_mlp=v_g_mlp, w_mlp_in=v_w_mlp_in, w_mlp_out=v_w_mlp_out, g_final=v_g_final)

    my_idx = _dev_index(_my_pos())

    g_in, g_cw = _all_gather([_mx(w["w_in"].T), w["conv_w"]])
    W = {"w_in_t": g_in.reshape(D_IN, D_MODEL), "conv_w": g_cw.transpose(1, 0, 2).reshape(CONV_WIDTH, D_RNN)}
    late, order_after = {}, [g_in]
    for group, names in (("branch", ("w_mem_kv", "w_lru_out", "w_dil_out", "w_mem_out", "w_out")),
                         ("mlp", ("w_mlp_in", "w_mlp_out"))):
        shards = [_mx(w[n].T if n in GATHERED_TRANSPOSED else w[n]) for n in names]
        started = _push_start(shards, [(N_DEV,) + s.shape for s in shards], _gather_refs, f"gather_{group}_start",
                              after=order_after)
        late[group] = (names, shards, started)
        order_after = [started["token"]]
    P = {n: w[n] for n in SMALL}

    def late_weights(group, after):
        names, shards, started = late[group]
        out = {}
        for n, land, own in zip(names, _push_wait(started, after), shards):
            full = lax.dynamic_update_index_in_dim(land, own, my_idx, 0)
            if n in GATHERED_TRANSPOSED:
                out[n + "_t"] = full.reshape(-1, full.shape[2])
            else:
                out[n] = _gathered_to_full(n, full)
        return out

    sent, small = [], {}

    def send_grads(gs):
        names = list(gs)
        parts = [gs[n] for n in names]
        own = [lax.dynamic_index_in_dim(p, my_idx, 0, keepdims=False) for p in parts]
        started = _push_start(parts, [(N_DEV - 1,) + p.shape[1:] for p in parts], _scatter_refs,
                              f"scatter{len(sent)}_start")
        sent.append((names, own, started))
        return started["token"]

    def reduce_small(gs):
        small["pack"] = _pack([gs[n] for n, _ in SMALL_GRADS])
        small["started"] = _push_start([small["pack"]], [(N_DEV,) + small["pack"].shape], _gather_refs, "small_start")
        return small["started"]["token"]

    grad_x, last_token = _local_step(x[0], mem[0], loss_target[0], W, P, late_weights, send_grads, reduce_small,
                                     late["mlp"][2]["token"][0, 0])

    grads, deltas, new_m, new_v = {}, {}, {}, {}
    after = last_token
    for names, own, started in sent[:-W_IN_PIECES]:
        for n, o, land in zip(names, own, _push_wait(started, after)):
            grads[n], deltas[n], new_m[n], new_v[n] = _adamw_landed(w[n], o, land, m[n], v[n], name=f"adamw_{n}")
            after = deltas[n]
    prev = None
    for q, (names, own, started) in enumerate(sent[-W_IN_PIECES:]):
        (land,) = _push_wait(started, after)
        prev = _adamw_landed(w["w_in"].T, own[0], land, m["w_in"].T, v["w_in"].T, name=f"adamw_{names[0]}",
                             col_blk=q, prev=prev)
        after = prev[1]
    grads["w_in"], deltas["w_in"], new_m["w_in"], new_v["w_in"] = [t.T for t in prev]
    (small_land,) = _push_wait(small["started"], [after] + [deltas[n] for n in BIG if n != "w_in"])
    total = _sum_slots(lax.dynamic_update_index_in_dim(small_land, small["pack"], my_idx, 0))
    summed = dict(zip([n for n, _ in SMALL_GRADS], _unpack(total, [shp for _, shp in SMALL_GRADS])))
    summed["b_gate"] = jnp.concatenate([summed.pop(f"b_gate{b}") for b in range(3)], axis=1)
    summed["rel_bias"] = summed["rel_bias"][:, :3 * DIL_HEADS]
    for n in SMALL:
        grads[n] = summed[n].reshape(w[n].shape)
    small_updates = _adamw_many([w[n] for n in SMALL], [grads[n] for n in SMALL], [m[n] for n in SMALL],
                                [v[n] for n in SMALL])
    for n, (d_, nm_, nv_) in zip(SMALL, small_updates):
        deltas[n], new_m[n], new_v[n] = d_, nm_, nv_
    conv_w_sum, loss_sum = summed["conv_w"], summed["loss"]
    cw_cols = D_RNN // N_DEV
    grads["conv_w"] = lax.dynamic_slice(conv_w_sum, (0, my_idx * cw_cols), (CONV_WIDTH, cw_cols))
    deltas["conv_w"], new_m["conv_w"], new_v["conv_w"] = _adamw_plain(
        w["conv_w"], grads["conv_w"], m["conv_w"], v["conv_w"], name="adamw_conv_w")

    return (loss_sum.reshape(()), grad_x[None], *[grads[n] for n in WEIGHTS], *[deltas[n] for n in WEIGHTS],
            *[new_m[n] for n in WEIGHTS], *[new_v[n] for n in WEIGHTS])
```

```python
import functools
import math

import jax
import jax.numpy as jnp
from jax import lax
from jax.experimental import pallas as pl
from jax.experimental.pallas import tpu as pltpu

F32 = jnp.float32
MXU_DTYPE = jnp.bfloat16
VMEM_LIMIT_BYTES = 56 * 1024 * 1024
N_DEV = 8

D_MODEL = 1024
N_MEM = 256
MEM_HEADS = 4
MEM_HEAD_DIM = 128
MEM_WIDTH = 512
D_RNN = 768
LRU_BLOCK = 64
N_LRU_BLOCKS = 12
LRU_GROUP = 256
N_LRU_GROUPS = 3
CONV_WIDTH = 4
LRU_C = 8.0
DIL_GROUPS = ((128, 1), (512, 4), (2048, 16))
SPAN = 128
DIL_HEADS = 4
DIL_HEAD_DIM = 64
NUM_BUCKETS = 32
MAX_DISTANCE = 2048
D_FF = 4096
D_IN = 7424
EPS = 1e-6
NEG = -1e30
C_XL, C_GATE, C_QKV, C_QM, C_GATES = 0, 768, 1536, 3840, 4352

ADAM_LR = 0.001
ADAM_B1 = 0.9
ADAM_B2 = 0.999
ADAM_EPS = 1e-08
ADAM_WD = 0.01
ADAM_STEP = 10

MESH = pl.DeviceIdType.MESH
GELU_K = math.sqrt(2.0 / math.pi)


def _cparams(sem=None):
    kw = dict(vmem_limit_bytes=VMEM_LIMIT_BYTES)
    if sem is not None:
        kw["dimension_semantics"] = sem
    return pltpu.CompilerParams(**kw)


def _mx(v):
    return v.astype(MXU_DTYPE)


def _dot(a, b, mode="nn"):
    dims = {"nn": (((1,), (0,)), ((), ())), "nt": (((1,), (1,)), ((), ())), "tn": (((0,), (0,)), ((), ()))}[mode]
    return lax.dot_general(_mx(a), _mx(b), dims, preferred_element_type=F32)


def _colsum(v):
    return jnp.sum(v, axis=0, keepdims=True)


def _matmul(a, b, *, M, N, K, mode, bm, bn, bk, name, out_dtypes=(F32,), epilogue=None, extras=(),
            a_off=(0, 0), b_off=(0, 0), j_outer=False, deps=(), parts=None):
    assert M % bm == 0 and N % bn == 0 and K % bk == 0, (name, M, N, K, bm, bn, bk)
    nm, nn, nk = M // bm, N // bn, K // bk

    def ij(f):
        if j_outer:
            return lambda j, i, k: f(i, j, k)
        return f

    if mode == "tn":
        a_spec = pl.BlockSpec((bk, bm), ij(lambda i, j, k: (k + a_off[0], i + a_off[1])))
    else:
        a_spec = pl.BlockSpec((bm, bk), ij(lambda i, j, k: (i + a_off[0], k + a_off[1])))
    if mode == "nt":
        b_spec = pl.BlockSpec((bn, bk), ij(lambda i, j, k: (j + b_off[0], k + b_off[1])))
    else:
        b_spec = pl.BlockSpec((bk, bn), ij(lambda i, j, k: (k + b_off[0], j + b_off[1])))
    ex_specs = [pl.BlockSpec((bm, bn), ij(functools.partial(lambda i, j, k, o: (i + o[0], j + o[1]), o=off)))
                for _, off in extras]
    if parts is None:
        out_dims = (M, N)
        out_spec = pl.BlockSpec((bm, bn), ij(lambda i, j, k: (i, j)))
    elif parts[0] == "rows":
        r = parts[1]
        assert bm % r == 0
        out_dims = (M // r, r, N)
        out_spec = pl.BlockSpec((bm // r, r, bn), ij(lambda i, j, k: (i, 0, j)))
    elif parts[0] == "rows_t":
        r = parts[1]
        assert bn % r == 0
        out_dims = (N // r, r, M)
        out_spec = pl.BlockSpec((bn // r, r, bm), ij(lambda i, j, k: (j, 0, i)))
    else:
        c = parts[1]
        assert bn % c == 0
        out_dims = (N // c, M, c)
        out_spec = pl.BlockSpec((bn // c, bm, c), ij(lambda i, j, k: (j, i, 0)))
    n_ex, n_out, n_dep = len(extras), len(out_dtypes), len(deps)

    def body(*refs):
        a_ref, b_ref = refs[0], refs[1]
        ex = refs[2:2 + n_ex]
        outs = refs[2 + n_ex + n_dep:2 + n_ex + n_dep + n_out]
        part = _dot(a_ref[...], b_ref[...], mode)

        def finish(acc):
            vals = epilogue(acc, *[e[...] for e in ex]) if epilogue is not None else (acc,)
            for o, v in zip(outs, vals):
                if parts is not None and parts[0] == "rows_t":
                    v = v.T
                v = v.astype(o.dtype)
                if parts is None:
                    o[...] = v
                elif parts[0] in ("rows", "rows_t"):
                    for ch in range(v.shape[0] // parts[1]):
                        o[ch] = v[ch * parts[1]:(ch + 1) * parts[1], :]
                else:
                    for ch in range(bn // parts[1]):
                        o[ch] = v[:, ch * parts[1]:(ch + 1) * parts[1]]

        if nk == 1:
            finish(part)
        else:
            acc_ref = refs[-1]
            k = pl.program_id(2)

            @pl.when(k == 0)
            def _():
                acc_ref[...] = part

            @pl.when(k > 0)
            def _():
                acc_ref[...] += part

            @pl.when(k == nk - 1)
            def _():
                finish(acc_ref[...])

    grid = (nn, nm, nk) if j_outer else (nm, nn, nk)
    res = pl.pallas_call(
        body, name=name, grid=grid,
        in_specs=[a_spec, b_spec] + ex_specs + [pl.BlockSpec(memory_space=pl.ANY)] * n_dep,
        out_specs=[out_spec] * n_out,
        out_shape=[jax.ShapeDtypeStruct(out_dims, dt) for dt in out_dtypes],
        scratch_shapes=[pltpu.VMEM((bm, bn), F32)] if nk > 1 else [],
        compiler_params=_cparams(("parallel", "parallel", "arbitrary")),
    )(a, b, *[e for e, _ in extras], *deps)
    return res[0] if n_out == 1 else res


ROW_SUBTILES = 4


def _matmul_rows(a, b, *, M, K, mode, bm, name, row_fn, out_dtypes, tiles=(), vecs=(), acc_widths=(), deps=()):
    N = D_MODEL
    assert M % bm == 0
    segs = list(a) if isinstance(a, (list, tuple)) else [a]
    widths = [s_.shape[1] for s_ in segs]
    assert sum(widths) == K and (len(segs) == 1 or mode == "nn")
    n_s, n_t, n_v, n_o, n_a, n_d = len(segs), len(tiles), len(vecs), len(out_dtypes), len(acc_widths), len(deps)
    row = pl.BlockSpec((bm, N), lambda i: (i, 0))
    b_shape = (K, N) if mode == "nn" else (N, K)

    def body(*refs):
        b_ref = refs[n_s]
        ins = refs[n_s + 1:n_s + 1 + n_t + n_v]
        outs = refs[n_s + 1 + n_t + n_v + n_d:n_s + 1 + n_t + n_v + n_d + n_o]
        accs = refs[n_s + 1 + n_t + n_v + n_d + n_o:]
        for o in accs:
            @pl.when(pl.program_id(0) == 0)
            def _(o=o):
                o[...] = jnp.zeros_like(o)

        for s_ in range(ROW_SUBTILES):
            rows = pl.ds(s_ * (bm // ROW_SUBTILES), bm // ROW_SUBTILES)
            if n_s == 1:
                acc = _dot(refs[0][rows, :], b_ref[...], mode)
            else:
                acc, k0 = None, 0
                for a_ref, w_ in zip(refs[:n_s], widths):
                    part = _dot(a_ref[rows, :], b_ref[k0:k0 + w_, :])
                    acc = part if acc is None else acc + part
                    k0 += w_
            tile_vals, partials = row_fn(acc, *[r[rows, :] for r in ins[:n_t]], *[r[...] for r in ins[n_t:]])
            for o, val in zip(outs, tile_vals):
                o[rows, :] = val.astype(o.dtype)
            for o, val in zip(accs, partials):
                o[...] += val

    res = pl.pallas_call(
        body, name=name, grid=(M // bm,),
        in_specs=[pl.BlockSpec((bm, w_), lambda i: (i, 0)) for w_ in widths] + [pl.BlockSpec(b_shape, lambda i: (0, 0))]
        + [row] * n_t + [pl.BlockSpec((1, N), lambda i: (0, 0))] * n_v + [pl.BlockSpec(memory_space=pl.ANY)] * n_d,
        out_specs=[row] * n_o + [pl.BlockSpec((1, w_), lambda i: (0, 0)) for w_ in acc_widths],
        out_shape=[jax.ShapeDtypeStruct((M, N), dt) for dt in out_dtypes]
        + [jax.ShapeDtypeStruct((1, w_), F32) for w_ in acc_widths],
        compiler_params=_cparams(("arbitrary",) if n_a else ("parallel",)),
    )(*segs, b, *tiles, *vecs, *deps)
    return res


def _dw_in_t_half(h, pieces, q, half, *, S, name, into=None, deps=(), bk=1024):
    half_w, cols = D_IN // 2, D_MODEL // W_IN_PIECES
    lo, hi = half * half_w, (half + 1) * half_w
    use, c0 = [], 0
    for p in pieces:
        w_ = p.shape[1]
        a0, a1 = max(lo, c0), min(hi, c0 + w_)
        if a1 > a0:
            use.append((p, a0 - c0, a1 - a0))
        c0 += w_
    n_p, n_into, n_d, nk = len(use), 0 if into is None else 1, len(deps), S // bk
    rows = D_IN // N_DEV

    def body(*refs):
        h_ref, p_refs = refs[0], refs[1:1 + n_p]
        o_ref, acc_ref = refs[1 + n_p + n_into + n_d], refs[-1]
        k = pl.program_id(0)
        hv, o0 = h_ref[...], 0
        for r, (_, s0, w_) in zip(p_refs, use):
            part = _dot(hv, r[:, s0:s0 + w_], "tn")
            lanes = pl.ds(o0, w_)

            @pl.when(k == 0)
            def _(part=part, lanes=lanes):
                acc_ref[:, lanes] = part

            @pl.when(k > 0)
            def _(part=part, lanes=lanes):
                acc_ref[:, lanes] += part

            o0 += w_

        @pl.when(k == nk - 1)
        def _():
            vt = acc_ref[...].T.astype(o_ref.dtype)
            for ch in range(half_w // rows):
                o_ref[ch] = vt[ch * rows:(ch + 1) * rows, :]

    return pl.pallas_call(
        body, name=name, grid=(nk,),
        in_specs=[pl.BlockSpec((bk, cols), lambda k: (k, q))]
        + [pl.BlockSpec((bk, p.shape[1]), lambda k: (k, 0)) for p, _, _ in use]
        + [pl.BlockSpec(memory_space=pl.ANY)] * (n_into + n_d),
        out_specs=pl.BlockSpec((half_w // rows, rows, cols), lambda k: (half, 0, 0)),
        out_shape=jax.ShapeDtypeStruct((N_DEV, rows, cols), MXU_DTYPE),
        input_output_aliases={1 + n_p: 0} if n_into else {},
        scratch_shapes=[pltpu.VMEM((cols, half_w), F32)],
        compiler_params=_cparams(("arbitrary",)),
    )(h, *[p for p, _, _ in use], *([into] if n_into else []), *deps)


def _rmsnorm_fwd(x, g, *, rows, name, bt=512):
    bt = min(bt, rows)

    def body(x_ref, g_ref, o_ref):
        xv = x_ref[...]
        r = lax.rsqrt(jnp.mean(xv * xv, axis=-1, keepdims=True) + EPS)
        o_ref[...] = (xv * r * g_ref[...]).astype(o_ref.dtype)

    return pl.pallas_call(
        body, name=name, grid=(rows // bt,),
        in_specs=[pl.BlockSpec((bt, D_MODEL), lambda i: (i, 0)), pl.BlockSpec((1, D_MODEL), lambda i: (0, 0))],
        out_specs=pl.BlockSpec((bt, D_MODEL), lambda i: (i, 0)),
        out_shape=jax.ShapeDtypeStruct((rows, D_MODEL), MXU_DTYPE),
        compiler_params=_cparams(("parallel",)),
    )(x, g.reshape(1, D_MODEL))


def _rms_bwd_tile(xv, gv, dyv):
    r = lax.rsqrt(jnp.mean(xv * xv, axis=-1, keepdims=True) + EPS)
    w = dyv * gv
    dx = r * w - xv * (r * r * r) * jnp.mean(w * xv, axis=-1, keepdims=True)
    dg = _colsum(dyv * xv * r)
    return dx, dg


def _residual_then_norm(acc, x_t, g):
    x1 = x_t + acc
    r = lax.rsqrt(jnp.mean(x1 * x1, axis=-1, keepdims=True) + EPS)
    return (x1, x1 * r * g), ()


def _residual_then_loss(acc, x_t, tgt_t, g):
    x2 = x_t + acc
    r = lax.rsqrt(jnp.mean(x2 * x2, axis=-1, keepdims=True) + EPS)
    diff = x2 * r * g - tgt_t
    part = jnp.sum(jnp.mean(diff * diff, axis=-1, keepdims=True), axis=0, keepdims=True) * 0.5
    dx, dg = _rms_bwd_tile(x2, g, diff * (1.0 / D_MODEL))
    return (dx, dx), (part, dg)


def _norm_bwd_then_residual(n_out):
    def fn(acc, x_t, res_t, g):
        dx, dg = _rms_bwd_tile(x_t, g, acc)
        return (dx + res_t,) * n_out, (dg,)

    return fn


def _rmsnorm_bwd(x, g, dy, res, *, rows, name, bt=512, dx_dtypes=(F32,)):
    bt = min(bt, rows)
    has_res = res is not None

    def body(*refs):
        x_ref, g_ref, dy_ref = refs[:3]
        res_ref = refs[3] if has_res else None
        outs = refs[3 + int(has_res):]
        dx, dg = _rms_bwd_tile(x_ref[...], g_ref[...], dy_ref[...])
        if has_res:
            dx = dx + res_ref[...]
        dg_ref = outs[-1]

        @pl.when(pl.program_id(0) == 0)
        def _():
            dg_ref[...] = jnp.zeros_like(dg_ref)

        dg_ref[...] += dg
        for o in outs[:-1]:
            o[...] = dx.astype(o.dtype)

    row_spec = pl.BlockSpec((bt, D_MODEL), lambda i: (i, 0))
    vec_spec = pl.BlockSpec((1, D_MODEL), lambda i: (0, 0))
    ins = [x, g.reshape(1, D_MODEL), dy] + ([res] if has_res else [])
    return pl.pallas_call(
        body, name=name, grid=(rows // bt,),
        in_specs=[row_spec, vec_spec, row_spec] + ([row_spec] if has_res else []),
        out_specs=[row_spec] * len(dx_dtypes) + [vec_spec],
        out_shape=[jax.ShapeDtypeStruct((rows, D_MODEL), dt) for dt in dx_dtypes] + [jax.ShapeDtypeStruct((1, D_MODEL), F32)],
        compiler_params=_cparams(("arbitrary",)),
    )(*ins)


LRU_T = 512
SCAN_GROUPS = 4


def _gelu(x):
    t = jnp.tanh(GELU_K * (x + 0.044715 * x * x * x))
    return 0.5 * x * (1.0 + t), t


def _gelu_grad(x, t):
    return 0.5 * (1.0 + t) + 0.5 * x * (1.0 - t * t) * GELU_K * (1.0 + 3.0 * 0.044715 * x * x)


def _softplus_neg(lam):
    z = -lam
    u = jnp.exp(-jnp.abs(z))
    w = 1.0 + u
    l1p = jnp.where(w == 1.0, u, jnp.log(w) * u / jnp.where(w == 1.0, 1.0, w - 1.0))
    return jnp.maximum(z, 0.0) + l1p


def _shift_down(cur, prev8, k, row8):
    y = pltpu.roll(cur, k, 0)
    head = jnp.where(row8 < k, pltpu.roll(prev8, k, 0), y[0:8])
    return jnp.concatenate([head, y[8:]], axis=0)


def _shift_up(cur, next8, k, row8):
    n = cur.shape[0]
    y = pltpu.roll(cur, n - k, 0)
    tail = jnp.where(row8 >= 8 - k, pltpu.roll(next8, 8 - k, 0), y[n - 8:n])
    return jnp.concatenate([y[0:n - 8], tail], axis=0)


def _lru_gates(xl, p8, cw, cb, wa, wx, ba, bx, lam, row8, a_mult=None):
    sh = [xl] + [_shift_down(xl, p8, k, row8) for k in (1, 2, 3)]
    xc = cb + cw[3:4] * sh[0] + cw[2:3] * sh[1] + cw[1:2] * sh[2] + cw[0:1] * sh[3]
    r = jax.nn.sigmoid(_dot(xc, wa) + ba)
    i = jax.nn.sigmoid(_dot(xc, wx) + bx)
    sp = _softplus_neg(lam)
    if a_mult is None:
        la = -LRU_C * r * sp
        a = jnp.exp(la)
        mult = jnp.sqrt(jnp.tanh(-la) * (a * a + 1.0))
    else:
        a, mult = a_mult
    return dict(sh=sh, xc=xc, r=r, i=i, sp=sp, a=a, mult=mult)


def _lru_specs(n_t, reverse):
    T = LRU_T
    tt = (lambda t: n_t - 1 - t) if reverse else (lambda t: t)
    blk = lambda col0: pl.BlockSpec((T, LRU_GROUP), lambda g, t: (tt(t), col0 + g))
    prev8 = lambda col0: pl.BlockSpec((8, LRU_GROUP), lambda g, t: (jnp.maximum(tt(t) * (T // 8) - 1, 0), col0 + g))
    vec = lambda rows: pl.BlockSpec((rows, LRU_GROUP), lambda g, t: (0, g))
    wbd = pl.BlockSpec((1, LRU_GROUP, LRU_GROUP), lambda g, t: (g, 0, 0))
    return blk, prev8, vec, wbd


def _lru_fwd(proj, conv_w, conv_b, wa_bd, wx_bd, b_a, b_x, lam, *, S):
    T = LRU_T
    n_t = S // T
    blk, _, vec, wbd = _lru_specs(n_t, False)

    def body(xl_ref, gate_ref, cw_ref, cb_ref, wa_ref, wx_ref, ba_ref, bx_ref, lam_ref,
             hl_ref, z_ref, a_s, m_ref, prev8, hcar, b_s):
        @pl.when(pl.program_id(1) == 0)
        def _():
            prev8[...] = jnp.zeros_like(prev8)
            hcar[...] = jnp.zeros_like(hcar)

        row8 = lax.broadcasted_iota(jnp.int32, (8, LRU_GROUP), 0)
        xl = xl_ref[...]
        q = _lru_gates(xl, prev8[...], cw_ref[...], cb_ref[...], wa_ref[0], wx_ref[0], ba_ref[...], bx_ref[...],
                       lam_ref[...], row8)
        prev8[...] = xl[T - 8:T]
        a_s[...] = q["a"]
        m_ref[...] = q["mult"]
        b_s[...] = q["mult"] * q["i"] * q["xc"]

        def step(c, carry):
            local = []
            for u in range(SCAN_GROUPS):
                off = pl.multiple_of((c * SCAN_GROUPS + u) * 8, 8)
                A = a_s[pl.ds(off, 8), :]
                B = b_s[pl.ds(off, 8), :]
                for k in (1, 2, 4):
                    a_sh = jnp.where(row8 >= k, pltpu.roll(A, k, 0), 1.0)
                    b_sh = jnp.where(row8 >= k, pltpu.roll(B, k, 0), 0.0)
                    B = A * b_sh + B
                    A = A * a_sh
                local.append((off, A, B))
            for off, A, B in local:
                h = A * carry + B
                hl_ref[pl.ds(off, 8), :] = h
                carry = h[7:8, :]
            return carry

        hcar[...] = lax.fori_loop(0, T // (8 * SCAN_GROUPS), step, hcar[...])
        ge, _ = _gelu(gate_ref[...])
        z_ref[...] = (ge * hl_ref[...]).astype(z_ref.dtype)

    return pl.pallas_call(
        body, name="lru_fwd", grid=(N_LRU_GROUPS, n_t),
        in_specs=[blk(C_XL // LRU_GROUP), blk(C_GATE // LRU_GROUP), vec(4), vec(1), wbd, wbd, vec(1), vec(1), vec(1)],
        out_specs=[blk(0)] * 4,
        out_shape=[jax.ShapeDtypeStruct((S, D_RNN), F32), jax.ShapeDtypeStruct((S, D_RNN), MXU_DTYPE),
                   jax.ShapeDtypeStruct((S, D_RNN), F32), jax.ShapeDtypeStruct((S, D_RNN), F32)],
        scratch_shapes=[pltpu.VMEM((8, LRU_GROUP), F32), pltpu.VMEM((1, LRU_GROUP), F32), pltpu.VMEM((T, LRU_GROUP), F32)],
        compiler_params=_cparams(("parallel", "arbitrary")),
    )(proj, proj, conv_w, conv_b, wa_bd, wx_bd, b_a, b_x, lam)


def _lru_bwd(proj, hl, a_fwd, mult_fwd, dz, conv_w, conv_b, wa_bd, wx_bd, b_a, b_x, lam, *, S):
    T = LRU_T
    n_t = S // T
    blk, prev8s, vec, wbd = _lru_specs(n_t, True)

    def body(xl_ref, xlp_ref, gate_ref, hl_ref, hlp_ref, a_ref, m_ref, dz_ref, cw_ref, cb_ref, wa_ref, wx_ref, ba_ref,
             bx_ref, lam_ref, dxl_ref, dgate_ref, dcw_ref, dcb_ref, dwa_ref, dwx_ref, dba_ref, dbx_ref, dlam_ref,
             next8, gcar, c_s, b_s, l_s):
        t = pl.program_id(1)
        first_chunk = t == n_t - 1

        @pl.when(t == 0)
        def _():
            next8[...] = jnp.zeros_like(next8)
            gcar[...] = jnp.zeros_like(gcar)
            for ref in (dcw_ref, dcb_ref, dwa_ref, dwx_ref, dba_ref, dbx_ref, dlam_ref):
                ref[...] = jnp.zeros_like(ref)

        row8 = lax.broadcasted_iota(jnp.int32, (8, LRU_GROUP), 0)
        rowT = lax.broadcasted_iota(jnp.int32, (T, LRU_GROUP), 0)
        keep = jnp.where(first_chunk, 0.0, 1.0)
        xl = xl_ref[...]
        wa, wx, lam_v = wa_ref[0], wx_ref[0], lam_ref[...]
        q = _lru_gates(xl, xlp_ref[...] * keep, cw_ref[...], cb_ref[...], wa, wx, ba_ref[...], bx_ref[...], lam_v, row8,
                       a_mult=(a_ref[...], m_ref[...]))
        a, mult, r, i, xc, sp = q["a"], q["mult"], q["r"], q["i"], q["xc"], q["sp"]
        hl_v = hl_ref[...]
        dz_v = dz_ref[...]
        gate = gate_ref[...]
        ge, th = _gelu(gate)
        dgate_ref[...] = (dz_v * hl_v * _gelu_grad(gate, th)).astype(dgate_ref.dtype)

        c_s[...] = jnp.where(rowT == T - 1, 0.0, pltpu.roll(a, T - 1, 0))
        b_s[...] = dz_v * ge + jnp.where(rowT == T - 1, gcar[...], 0.0)

        def step(n, carry):
            local = []
            for u in range(SCAN_GROUPS):
                off = pl.multiple_of((T // 8 - 1 - (n * SCAN_GROUPS + u)) * 8, 8)
                C = c_s[pl.ds(off, 8), :]
                B = b_s[pl.ds(off, 8), :]
                for k in (1, 2, 4):
                    c_sh = jnp.where(row8 < 8 - k, pltpu.roll(C, 8 - k, 0), 1.0)
                    b_sh = jnp.where(row8 < 8 - k, pltpu.roll(B, 8 - k, 0), 0.0)
                    B = B + C * b_sh
                    C = C * c_sh
                local.append((off, C, B))
            for off, C, B in local:
                lam_t = B + C * carry
                l_s[pl.ds(off, 8), :] = lam_t
                carry = lam_t[0:1, :]
            return carry

        lax.fori_loop(0, T // (8 * SCAN_GROUPS), step, jnp.zeros((1, LRU_GROUP), F32))
        lmb = l_s[...]
        gcar[...] = a[0:1, :] * lmb[0:1, :]

        h_prev = _shift_down(hl_v, hlp_ref[...] * keep, 1, row8)
        da = lmb * h_prev
        dmult = lmb * i * xc
        di = lmb * mult * xc
        dxc = lmb * mult * i
        dla = da * a - dmult * (a * a) / mult
        dr = dla * (-LRU_C * sp)
        dlam_ref[...] += _colsum(dla * (-LRU_C * r)) * (-jax.nn.sigmoid(-lam_v))
        dpa = dr * r * (1.0 - r)
        dpx = di * i * (1.0 - i)
        dxc = dxc + _dot(dpa, wa, "nt") + _dot(dpx, wx, "nt")
        dwa_ref[0] += _dot(xc, dpa, "tn")
        dwx_ref[0] += _dot(xc, dpx, "tn")
        dba_ref[...] += _colsum(dpa)
        dbx_ref[...] += _colsum(dpx)
        dcb_ref[...] += _colsum(dxc)
        cw = cw_ref[...]
        n8 = next8[...]
        dxl = cw[3:4] * dxc
        for k in (1, 2, 3):
            dxl = dxl + cw[3 - k:4 - k] * _shift_up(dxc, n8, k, row8)
        for k in range(4):
            dcw_ref[3 - k:4 - k, :] += _colsum(dxc * q["sh"][k])
        next8[...] = dxc[0:8]
        dxl_ref[...] = dxl.astype(dxl_ref.dtype)

    res = pl.pallas_call(
        body, name="lru_bwd", grid=(N_LRU_GROUPS, n_t),
        in_specs=[blk(C_XL // LRU_GROUP), prev8s(C_XL // LRU_GROUP), blk(C_GATE // LRU_GROUP), blk(0), prev8s(0), blk(0),
                  blk(0), blk(0), vec(4), vec(1), wbd, wbd, vec(1), vec(1), vec(1)],
        out_specs=[blk(0), blk(0), vec(4), vec(1), wbd, wbd, vec(1), vec(1), vec(1)],
        out_shape=[jax.ShapeDtypeStruct((S, D_RNN), MXU_DTYPE), jax.ShapeDtypeStruct((S, D_RNN), MXU_DTYPE),
                   jax.ShapeDtypeStruct((4, D_RNN), F32), jax.ShapeDtypeStruct((1, D_RNN), F32),
                   jax.ShapeDtypeStruct((N_LRU_GROUPS, LRU_GROUP, LRU_GROUP), F32),
                   jax.ShapeDtypeStruct((N_LRU_GROUPS, LRU_GROUP, LRU_GROUP), F32),
                   jax.ShapeDtypeStruct((1, D_RNN), F32), jax.ShapeDtypeStruct((1, D_RNN), F32),
                   jax.ShapeDtypeStruct((1, D_RNN), F32)],
        scratch_shapes=[pltpu.VMEM((8, LRU_GROUP), F32), pltpu.VMEM((1, LRU_GROUP), F32),
                        pltpu.VMEM((T, LRU_GROUP), F32), pltpu.VMEM((T, LRU_GROUP), F32), pltpu.VMEM((T, LRU_GROUP), F32)],
        compiler_params=_cparams(("parallel", "arbitrary")),
    )(proj, proj, proj, hl, hl, a_fwd, mult_fwd, dz, conv_w, conv_b, wa_bd, wx_bd, b_a, b_x, lam)
    return res


def _block_diag(w):
    w4 = w.reshape(N_LRU_GROUPS, 4, LRU_BLOCK, 1, LRU_BLOCK)
    eye = jnp.eye(4, dtype=w.dtype).reshape(1, 4, 1, 4, 1)
    return (w4 * eye).reshape(N_LRU_GROUPS, LRU_GROUP, LRU_GROUP)


def _block_diag_extract(wbd):
    w5 = wbd.reshape(N_LRU_GROUPS, 4, LRU_BLOCK, 4, LRU_BLOCK)
    return jnp.stack([w5[:, a, :, a, :] for a in range(4)], axis=1).reshape(N_LRU_BLOCKS, LRU_BLOCK, LRU_BLOCK)


def _t5_bucket(dist):
    max_exact = NUM_BUCKETS // 2
    df = jnp.maximum(dist, 1).astype(jnp.float32)
    large = max_exact + (jnp.log(df / max_exact) / math.log(MAX_DISTANCE / max_exact)
                         * (NUM_BUCKETS - max_exact)).astype(jnp.int32)
    large = jnp.minimum(large, NUM_BUCKETS - 1)
    return jnp.where(dist < max_exact, dist, large)


def _band_offsets():
    qi = jnp.arange(SPAN)[:, None]
    kj = jnp.arange(2 * SPAN)[None, :]
    return qi + SPAN - kj


def _dil_buckets():
    off = _band_offsets()
    return jnp.stack([_t5_bucket(jnp.maximum(off, 0) * dil) for _, dil in DIL_GROUPS]).astype(jnp.int32)


def _dil_bias(rel_bias, buckets):
    def body(tbl_ref, bk_ref, o_ref):
        g = pl.program_id(0)
        qi = lax.broadcasted_iota(jnp.int32, (SPAN, 2 * SPAN), 0)
        kj = lax.broadcasted_iota(jnp.int32, (SPAN, 2 * SPAN), 1)
        off = qi + SPAN - kj
        valid = (off >= 0) & (off <= SPAN)
        bk = bk_ref[0]
        for h in range(DIL_HEADS):
            acc = jnp.zeros((SPAN, 2 * SPAN), F32)
            for b in range(NUM_BUCKETS):
                acc = jnp.where(bk == b, tbl_ref[b, g * DIL_HEADS + h], acc)
            o_ref[0, h] = jnp.where(valid, acc, NEG)

    return pl.pallas_call(
        body, name="dil_bias", grid=(3,),
        in_specs=[pl.BlockSpec(memory_space=pltpu.SMEM), pl.BlockSpec((1, SPAN, 2 * SPAN), lambda g: (g, 0, 0))],
        out_specs=pl.BlockSpec((1, DIL_HEADS, SPAN, 2 * SPAN), lambda g: (g, 0, 0, 0)),
        out_shape=jax.ShapeDtypeStruct((3, DIL_HEADS, SPAN, 2 * SPAN), F32),
        compiler_params=_cparams(("parallel",)),
    )(rel_bias, buckets)


def _dil_bias_bwd(dbias, buckets):
    def body(db_ref, bk_ref, o_ref):
        lane = lax.broadcasted_iota(jnp.int32, (1, 128), 1)
        rows = [jnp.zeros((1, 128), F32) for _ in range(NUM_BUCKETS)]
        for g in range(3):
            bk = bk_ref[g]
            for h in range(DIL_HEADS):
                d = db_ref[g, h]
                for b in range(NUM_BUCKETS):
                    tot = jnp.sum(_colsum(jnp.where(bk == b, d, 0.0)), axis=1, keepdims=True)
                    rows[b] = jnp.where(lane == g * DIL_HEADS + h, tot, rows[b])
        for b in range(NUM_BUCKETS):
            o_ref[b:b + 1, :] = rows[b]

    return pl.pallas_call(
        body, name="dil_bias_bwd",
        out_shape=jax.ShapeDtypeStruct((NUM_BUCKETS, 128), F32),
        compiler_params=_cparams(),
    )(dbias, buckets)


DIL_SUBBLOCKS = (8, 4, 1)


def _dil_layout(g, S):
    dil, m = DIL_GROUPS[g][1], DIL_SUBBLOCKS[g]
    sub = SPAN * dil
    col = [(C_QKV + t * 768 + g * 256) // 128 for t in range(3)]
    return dil, m, sub, S // (sub * m), col


def _residue_rows(b, r, dil):
    return pl.ds(b * SPAN * dil + r, SPAN, stride=dil) if dil > 1 else pl.ds(b * SPAN, SPAN)


def _for_residues(dil, fn):
    if dil <= 4:
        for r in range(dil):
            fn(r)
    else:
        lax.fori_loop(0, dil, lambda r, c: (fn(r), c)[1], 0, unroll=4)


def _pair_scores(qm, k2, bias, first_cols):
    s = _dot(qm, k2, "nt") * (DIL_HEAD_DIM ** -0.5) + bias
    kj = lax.broadcasted_iota(jnp.int32, s.shape, 1)
    return jnp.where(kj < first_cols, NEG, s)


def _dilated_fwd(proj, bias, g, *, S):
    dil, m, sub, nc, (qc, kc, vc) = _dil_layout(g, S)
    R = sub * m
    cur = lambda cb: pl.BlockSpec((R, 128), lambda p, i: (i, cb + p))
    prv = lambda cb: pl.BlockSpec((sub, 128), lambda p, i: (jnp.maximum(i * m - 1, 0), cb + p))
    out = pl.BlockSpec((R, 128), lambda p, i: (i, p))

    def body(q_ref, kp_ref, kc_ref, vp_ref, vc_ref, b_ref, o_ref, lse_ref):
        lane = lax.broadcasted_iota(jnp.int32, (SPAN, 128), 1)
        sels = (lane < DIL_HEAD_DIM, lane >= DIL_HEAD_DIM)
        for b in range(m):
            first_cols = jnp.where(pl.program_id(1) == 0, SPAN, 0) if b == 0 else 0

            def one(r, b=b, first_cols=first_cols):
                rows = _residue_rows(b, r, dil)
                before = (kc_ref, vc_ref, _residue_rows(b - 1, r, dil)) if b else (kp_ref, vp_ref, _residue_rows(0, r, dil))
                q2 = q_ref[rows, :]
                k2 = _mx(jnp.concatenate([before[0][before[2], :], kc_ref[rows, :]], axis=0))
                v2 = _mx(jnp.concatenate([before[1][before[2], :], vc_ref[rows, :]], axis=0))
                qq = jnp.concatenate([jnp.where(sels[0], q2, 0.0), jnp.where(sels[1], q2, 0.0)], axis=0)
                s = _pair_scores(qq, k2, b_ref[0, 0], first_cols)
                mx = jnp.max(s, axis=-1, keepdims=True)
                p = jnp.exp(s - mx)
                den = jnp.sum(p, axis=-1, keepdims=True)
                o = _dot(p, v2) / den
                st = mx + jnp.log(den)
                o_ref[rows, :] = jnp.where(sels[0], o[0:SPAN], o[SPAN:2 * SPAN])
                lse_ref[rows, :] = jnp.where(lane == 0, st[0:SPAN], jnp.where(lane == 1, st[SPAN:2 * SPAN], 0.0))

            _for_residues(dil, one)

    return pl.pallas_call(
        body, name=f"dil_fwd{g}", grid=(2, nc),
        in_specs=[cur(qc), prv(kc), cur(kc), prv(vc), cur(vc),
                  pl.BlockSpec((1, 1, 2 * SPAN, 2 * SPAN), lambda p, i: (g, p, 0, 0))],
        out_specs=[out, out],
        out_shape=[jax.ShapeDtypeStruct((S, 256), F32), jax.ShapeDtypeStruct((S, 256), F32)],
        compiler_params=_cparams(("parallel", "parallel")),
    )(proj, proj, proj, proj, proj, bias.reshape(3, 2, 2 * SPAN, 2 * SPAN))


def _dilated_bwd(proj, do, lse, delta, bias, g, *, S, into=None):
    dil, m, sub, nc, (qc, kc, vc) = _dil_layout(g, S)
    R = sub * m
    cl = lambda i: jnp.minimum(i, nc - 1)
    cur = lambda cb: pl.BlockSpec((R, 128), lambda p, i: (cl(i), cb + p))
    prv = lambda cb: pl.BlockSpec((sub, 128), lambda p, i: (jnp.maximum(cl(i) * m - 1, 0), cb + p))
    q_out = pl.BlockSpec((R, 128), lambda p, i: (cl(i), 2 * g + p))
    kv_out = pl.BlockSpec((R, 128), lambda p, i: (jnp.maximum(i - 1, 0), 2 * g + p))
    scale = DIL_HEAD_DIM ** -0.5
    n_into = 0 if into is None else 3

    def body(q_ref, kp_ref, kc_ref, vp_ref, vc_ref, do_ref, lse_ref, dl_ref, b_ref, *rest):
        dq_ref, dk_ref, dv_ref, db_ref, dq_s, kc_s, vc_s, kp_s, vp_s, kcar, vcar = rest[n_into:]
        i = pl.program_id(1)

        @pl.when(i == 0)
        def _():
            kcar[...] = jnp.zeros_like(kcar)
            vcar[...] = jnp.zeros_like(vcar)
            db_ref[...] = jnp.zeros_like(db_ref)

        @pl.when(i < nc)
        def _():
            lane = lax.broadcasted_iota(jnp.int32, (SPAN, 128), 1)
            sels = (lane < DIL_HEAD_DIM, lane >= DIL_HEAD_DIM)
            for b in range(m):
                first_cols = jnp.where(i == 0, SPAN, 0) if b == 0 else 0

                def one(r, b=b, first_cols=first_cols):
                    rows = _residue_rows(b, r, dil)
                    rows_before = _residue_rows(b - 1 if b else 0, r, dil)
                    k_before, v_before = (kc_ref, vc_ref) if b else (kp_ref, vp_ref)
                    q2, do2 = q_ref[rows, :], do_ref[rows, :]
                    k2 = _mx(jnp.concatenate([k_before[rows_before, :], kc_ref[rows, :]], axis=0))
                    v2 = _mx(jnp.concatenate([v_before[rows_before, :], vc_ref[rows, :]], axis=0))
                    lse_t, dl_t = lse_ref[rows, :], dl_ref[rows, :]
                    qq = _mx(jnp.concatenate([jnp.where(sels[0], q2, 0.0), jnp.where(sels[1], q2, 0.0)], axis=0))
                    dd = _mx(jnp.concatenate([jnp.where(sels[0], do2, 0.0), jnp.where(sels[1], do2, 0.0)], axis=0))
                    lse2 = jnp.concatenate([lse_t[:, 0:1], lse_t[:, 1:2]], axis=0)
                    dl2 = jnp.concatenate([dl_t[:, 0:1], dl_t[:, 1:2]], axis=0)
                    p = jnp.exp(_pair_scores(qq, k2, b_ref[0, 0], first_cols) - lse2)
                    ds = p * (_dot(dd, v2, "nt") - dl2)
                    db_ref[0] += ds
                    dqq = _dot(ds, k2) * scale
                    dq2 = jnp.where(sels[0], dqq[0:SPAN], dqq[SPAN:2 * SPAN])
                    dk2 = _dot(ds, qq, "tn") * scale
                    dv2 = _dot(p, dd, "tn")
                    dq_s[rows, :] = dq2
                    kc_s[rows, :] = dk2[SPAN:2 * SPAN]
                    vc_s[rows, :] = dv2[SPAN:2 * SPAN]
                    if b:
                        kc_s[rows_before, :] += dk2[0:SPAN]
                        vc_s[rows_before, :] += dv2[0:SPAN]
                    else:
                        kp_s[rows_before, :] = dk2[0:SPAN]
                        vp_s[rows_before, :] = dv2[0:SPAN]

                _for_residues(dil, one)
            dq_ref[...] = dq_s[...].astype(dq_ref.dtype)
            last = pl.ds((m - 1) * sub, sub)
            kcar[last, :] += kp_s[...]
            vcar[last, :] += vp_s[...]
            dk_ref[...] = kcar[...].astype(dk_ref.dtype)
            dv_ref[...] = vcar[...].astype(dv_ref.dtype)
            kcar[...] = kc_s[...]
            vcar[...] = vc_s[...]

        @pl.when(i == nc)
        def _():
            dk_ref[...] = kcar[...].astype(dk_ref.dtype)
            dv_ref[...] = vcar[...].astype(dv_ref.dtype)

    stat = pl.BlockSpec((R, 128), lambda p, i: (cl(i), p))
    big = jax.ShapeDtypeStruct((S, len(DIL_GROUPS) * 256), MXU_DTYPE)
    return pl.pallas_call(
        body, name=f"dil_bwd{g}", grid=(2, nc + 1),
        in_specs=[cur(qc), prv(kc), cur(kc), prv(vc), cur(vc), stat, stat, stat,
                  pl.BlockSpec((1, 1, 2 * SPAN, 2 * SPAN), lambda p, i: (g, p, 0, 0))]
        + [pl.BlockSpec(memory_space=pl.ANY)] * n_into,
        out_specs=[q_out, kv_out, kv_out, pl.BlockSpec((1, 2 * SPAN, 2 * SPAN), lambda p, i: (p, 0, 0))],
        out_shape=[big, big, big, jax.ShapeDtypeStruct((2, 2 * SPAN, 2 * SPAN), F32)],
        input_output_aliases={9 + j: j for j in range(n_into)},
        scratch_shapes=[pltpu.VMEM((R, 128), F32)] * 3 + [pltpu.VMEM((sub, 128), F32)] * 2 + [pltpu.VMEM((R, 128), F32)] * 2,
        compiler_params=_cparams(("parallel", "arbitrary")),
    )(proj, proj, proj, proj, proj, do, lse, delta, bias.reshape(3, 2, 2 * SPAN, 2 * SPAN), *(into or ()))


def _dilated_merge(os_, lses, *, S, bt=512):
    tile = pl.BlockSpec((bt, 128), lambda i, p: (i, p))

    def body(o0, o1, o2, l0, l1, l2, o_ref, om_ref, lse_ref):
        lane = lax.broadcasted_iota(jnp.int32, (bt, 128), 1)
        lo = lane < DIL_HEAD_DIM
        ls = [l0[...], l1[...], l2[...]]
        ws, stat = [], jnp.zeros((bt, 128), F32)
        for e in range(2):
            a = [l[:, e:e + 1] for l in ls]
            m = jnp.maximum(jnp.maximum(a[0], a[1]), a[2])
            ex = [jnp.exp(v - m) for v in a]
            tot = ex[0] + ex[1] + ex[2]
            ws.append([v / tot for v in ex])
            stat = jnp.where(lane == e, m + jnp.log(tot), stat)
        acc = jnp.zeros((bt, 128), F32)
        for gi, o in enumerate((o0, o1, o2)):
            acc = acc + jnp.where(lo, ws[0][gi], ws[1][gi]) * o[...]
        o_ref[...] = acc
        om_ref[...] = _mx(acc)
        lse_ref[...] = stat

    return pl.pallas_call(
        body, name="dil_merge", grid=(S // bt, 2),
        in_specs=[tile] * 6, out_specs=[tile, tile, tile],
        out_shape=[jax.ShapeDtypeStruct((S, 256), F32), jax.ShapeDtypeStruct((S, 256), MXU_DTYPE),
                   jax.ShapeDtypeStruct((S, 256), F32)],
        compiler_params=_cparams(("parallel", "parallel")),
    )(*os_, *lses)


def _with_delta(do, o):
    lane = lax.broadcasted_iota(jnp.int32, (do.shape[0], 128), 1)
    stats = []
    for p in range(2):
        prod = do[:, 128 * p:128 * (p + 1)] * o[:, 128 * p:128 * (p + 1)]
        d0 = jnp.sum(jnp.where(lane < DIL_HEAD_DIM, prod, 0.0), axis=-1, keepdims=True)
        d1 = jnp.sum(jnp.where(lane >= DIL_HEAD_DIM, prod, 0.0), axis=-1, keepdims=True)
        stats.append(jnp.where(lane == 0, d0, jnp.where(lane == 1, d1, 0.0)))
    return do, jnp.concatenate(stats, axis=1)


MEM_T = 2048
QM_BLK = C_QM // MEM_HEAD_DIM


def _mem_attn_fwd(proj, kv, *, S):
    scale = MEM_HEAD_DIM ** -0.5

    def body(q_ref, k_ref, v_ref, o_ref, om_ref, lse_ref):
        s = _dot(q_ref[...], k_ref[...], "nt") * scale
        m = jnp.max(s, axis=-1, keepdims=True)
        p = jnp.exp(s - m)
        den = jnp.sum(p, axis=-1, keepdims=True)
        o = _dot(p, v_ref[...]) / den
        o_ref[...] = o
        om_ref[...] = _mx(o)
        lse_ref[0] = m + jnp.log(den)

    return pl.pallas_call(
        body, name="mem_attn_fwd", grid=(S // MEM_T, MEM_HEADS),
        in_specs=[pl.BlockSpec((MEM_T, MEM_HEAD_DIM), lambda i, h: (i, QM_BLK + h)),
                  pl.BlockSpec((N_MEM, MEM_HEAD_DIM), lambda i, h: (0, h)),
                  pl.BlockSpec((N_MEM, MEM_HEAD_DIM), lambda i, h: (0, MEM_HEADS + h))],
        out_specs=[pl.BlockSpec((MEM_T, MEM_HEAD_DIM), lambda i, h: (i, h)),
                   pl.BlockSpec((MEM_T, MEM_HEAD_DIM), lambda i, h: (i, h)),
                   pl.BlockSpec((1, MEM_T, 1), lambda i, h: (h, i, 0))],
        out_shape=[jax.ShapeDtypeStruct((S, MEM_WIDTH), F32), jax.ShapeDtypeStruct((S, MEM_WIDTH), MXU_DTYPE),
                   jax.ShapeDtypeStruct((MEM_HEADS, S, 1), F32)],
        compiler_params=_cparams(("parallel", "parallel")),
    )(proj, kv, kv)


def _mem_attn_bwd(proj, kv, om, lse, dom, *, S):
    scale = MEM_HEAD_DIM ** -0.5

    def body(q_ref, k_ref, v_ref, o_ref, lse_ref, do_ref, dq_ref, dk_ref, dv_ref):
        @pl.when(pl.program_id(1) == 0)
        def _():
            dk_ref[...] = jnp.zeros_like(dk_ref)
            dv_ref[...] = jnp.zeros_like(dv_ref)

        qv, kv_, vv, dov = q_ref[...], k_ref[...], v_ref[...], do_ref[...]
        p = jnp.exp(_dot(qv, kv_, "nt") * scale - lse_ref[0])
        delta = jnp.sum(dov * o_ref[...], axis=-1, keepdims=True)
        ds = p * (_dot(dov, vv, "nt") - delta)
        dq_ref[...] = (_dot(ds, kv_) * scale).astype(dq_ref.dtype)
        dk_ref[...] += _dot(ds, qv, "tn") * scale
        dv_ref[...] += _dot(p, dov, "tn")

    tile = pl.BlockSpec((MEM_T, MEM_HEAD_DIM), lambda h, i: (i, h))
    kvo = pl.BlockSpec((N_MEM, MEM_HEAD_DIM), lambda h, i: (0, h))
    return pl.pallas_call(
        body, name="mem_attn_bwd", grid=(MEM_HEADS, S // MEM_T),
        in_specs=[pl.BlockSpec((MEM_T, MEM_HEAD_DIM), lambda h, i: (i, QM_BLK + h)),
                  pl.BlockSpec((N_MEM, MEM_HEAD_DIM), lambda h, i: (0, h)),
                  pl.BlockSpec((N_MEM, MEM_HEAD_DIM), lambda h, i: (0, MEM_HEADS + h)),
                  tile, pl.BlockSpec((1, MEM_T, 1), lambda h, i: (h, i, 0)), tile],
        out_specs=[tile, kvo, kvo],
        out_shape=[jax.ShapeDtypeStruct((S, MEM_WIDTH), MXU_DTYPE), jax.ShapeDtypeStruct((N_MEM, MEM_WIDTH), F32),
                   jax.ShapeDtypeStruct((N_MEM, MEM_WIDTH), F32)],
        compiler_params=_cparams(("parallel", "arbitrary")),
    )(proj, kv, kv, om, lse, dom)


MIX_BM = 1024
MIX_BN = 256
GATES_BLK = C_GATES // MIX_BN


def _mix_specs(j_outer):
    ix = (lambda f: (lambda j, i: f(i, j))) if j_outer else (lambda f: f)
    act = lambda width: pl.BlockSpec((MIX_BM, width), ix(lambda i, j: (i, 0)))
    wgt = lambda width: pl.BlockSpec((width, MIX_BN), ix(lambda i, j: (0, j)))
    gate = lambda b: pl.BlockSpec((MIX_BM, MIX_BN), ix(lambda i, j: (i, GATES_BLK + 4 * b + j)))
    bias = lambda b: pl.BlockSpec((1, MIX_BN), ix(lambda i, j: (0, 4 * b + j)))
    tile = pl.BlockSpec((MIX_BM, MIX_BN), ix(lambda i, j: (i, j)))
    return act, wgt, gate, bias, tile


def _mix_fwd(z_lru, o_dil, om, w_lru, w_dil, w_mem, proj, b_gate, *, S):
    act, wgt, gate, bias, tile = _mix_specs(False)

    def body(zl, od, mo, wl, wd, wm, g0, g1, g2, b0, b1, b2, o_ref):
        acc = jax.nn.sigmoid(g0[...] + b0[...]) * _dot(zl[...], wl[...])
        acc += jax.nn.sigmoid(g1[...] + b1[...]) * _dot(od[...], wd[...])
        acc += jax.nn.sigmoid(g2[...] + b2[...]) * _dot(mo[...], wm[...])
        o_ref[...] = acc.astype(o_ref.dtype)

    return pl.pallas_call(
        body, name="mix_fwd", grid=(S // MIX_BM, D_MODEL // MIX_BN),
        in_specs=[act(D_RNN), act(256), act(MEM_WIDTH), wgt(D_RNN), wgt(256), wgt(MEM_WIDTH),
                  gate(0), gate(1), gate(2), bias(0), bias(1), bias(2)],
        out_specs=tile, out_shape=jax.ShapeDtypeStruct((S, D_MODEL), MXU_DTYPE),
        compiler_params=_cparams(("parallel", "parallel")),
    )(z_lru, o_dil, om, w_lru, w_dil, w_mem, proj, proj, proj, b_gate, b_gate, b_gate)


def _mix_bwd(dmerged, z_lru, o_dil, om, w_lru, w_dil, w_mem, proj, b_gate, *, S):
    act, wgt, gate, bias, tile = _mix_specs(True)

    def body(dm, zl, od, mo, wl, wd, wm, g0, g1, g2, b0, b1, b2,
             dg0, dg1, dg2, dy0, dy1, dy2, db0, db1, db2):
        @pl.when(pl.program_id(1) == 0)
        def _():
            for r in (db0, db1, db2):
                r[...] = jnp.zeros_like(r)

        dmv = dm[...]
        for act_ref, w_ref, g_ref, b_ref, dg_ref, dy_ref, db_ref in (
                (zl, wl, g0, b0, dg0, dy0, db0), (od, wd, g1, b1, dg1, dy1, db1), (mo, wm, g2, b2, dg2, dy2, db2)):
            y = _dot(act_ref[...], w_ref[...])
            gt = jax.nn.sigmoid(g_ref[...] + b_ref[...])
            dgate = dmv * y * gt * (1.0 - gt)
            dg_ref[...] = dgate.astype(dg_ref.dtype)
            dy_ref[...] = (dmv * gt).astype(dy_ref.dtype)
            db_ref[...] += _colsum(dgate)

    big = jax.ShapeDtypeStruct((S, D_MODEL), MXU_DTYPE)
    vec = jax.ShapeDtypeStruct((1, D_MODEL), F32)
    vspec = pl.BlockSpec((1, MIX_BN), lambda j, i: (0, j))
    return pl.pallas_call(
        body, name="mix_bwd", grid=(D_MODEL // MIX_BN, S // MIX_BM),
        in_specs=[tile, act(D_RNN), act(256), act(MEM_WIDTH), wgt(D_RNN), wgt(256), wgt(MEM_WIDTH),
                  gate(0), gate(1), gate(2), bias(0), bias(1), bias(2)],
        out_specs=[tile] * 6 + [vspec] * 3, out_shape=[big] * 6 + [vec] * 3,
        compiler_params=_cparams(("parallel", "arbitrary")),
    )(dmerged, z_lru, o_dil, om, w_lru, w_dil, w_mem, proj, proj, proj, b_gate, b_gate, b_gate)


def _adamw_math(w, g, m, v):
    m = ADAM_B1 * m + (1.0 - ADAM_B1) * g
    v = ADAM_B2 * v + (1.0 - ADAM_B2) * (g * g)
    m_hat = m / (1.0 - ADAM_B1 ** ADAM_STEP)
    v_hat = v / (1.0 - ADAM_B2 ** ADAM_STEP)
    delta = -ADAM_LR * (m_hat / (jnp.sqrt(v_hat) + ADAM_EPS) + ADAM_WD * w)
    return delta, m, v


def _adamw_landed(w, own, land, m, v, *, name, col_blk=0, prev=None):
    R = w.shape[0]
    n_parts, C = land.shape[0], land.shape[2]
    br = next(d for d in (256, 464, 128) if R % d == 0)
    tile = pl.BlockSpec((br, C), lambda i: (i, col_blk))
    part = pl.BlockSpec((br, C), lambda i: (i, 0))
    n_prev = 0 if prev is None else 4

    def body(w_ref, o_ref, l_ref, m_ref, v_ref, *rest):
        g_ref, d_ref, nm_ref, nv_ref = rest[n_prev:]
        g = o_ref[...].astype(F32)
        for p in range(n_parts):
            g = g + l_ref[p].astype(F32)
        d, nm, nv = _adamw_math(w_ref[...], g, m_ref[...], v_ref[...])
        g_ref[...] = g
        d_ref[...] = d
        nm_ref[...] = nm
        nv_ref[...] = nv

    return pl.pallas_call(
        body, name=name, grid=(R // br,),
        in_specs=[tile, part, pl.BlockSpec((n_parts, br, C), lambda i: (0, i, 0)), tile, tile]
        + [pl.BlockSpec(memory_space=pl.ANY)] * n_prev,
        out_specs=[tile] * 4, out_shape=[jax.ShapeDtypeStruct(w.shape, F32)] * 4,
        input_output_aliases={5 + j: j for j in range(n_prev)},
        compiler_params=_cparams(("parallel",)),
    )(w, own, land, m, v, *(prev or ()))


def _adamw_plain(w, g, m, v, *, name):
    def body(w_ref, g_ref, m_ref, v_ref, d_ref, nm_ref, nv_ref):
        d, nm, nv = _adamw_math(w_ref[...], g_ref[...], m_ref[...], v_ref[...])
        d_ref[...] = d
        nm_ref[...] = nm
        nv_ref[...] = nv

    return pl.pallas_call(
        body, name=name, out_shape=[jax.ShapeDtypeStruct(w.shape, F32)] * 3, compiler_params=_cparams(),
    )(w, g, m, v)


def _my_pos():
    return lax.axis_index("x"), lax.axis_index("y"), lax.axis_index("c")


def _dev_index(p):
    return 4 * p[0] + 2 * p[1] + p[2]


def _all_gather(shards):
    n = len(shards)
    hbm = pl.BlockSpec(memory_space=pl.ANY)

    def body(*refs):
        ins, outs = refs[:n], refs[n:2 * n]
        send_sems, recv_sems, local_sems = refs[2 * n:]
        x, y, c = _my_pos()
        me, sibling = (x, y, c), (x, y, 1 - c)
        chips = [(1 - x, y), (x, 1 - y), (1 - x, 1 - y)]

        def copy(a, k, block, to, src=None):
            dst = outs[a].at[_dev_index(block)]
            return pltpu.make_async_remote_copy(
                src_ref=dst if src is None else src, dst_ref=dst,
                send_sem=send_sems.at[a, k], recv_sem=recv_sems.at[a, k], device_id=to, device_id_type=MESH)

        mine = [pltpu.make_async_copy(ins[a], outs[a].at[_dev_index(me)], local_sems.at[a]) for a in range(n)]
        for cp in mine:
            cp.start()
        first = []
        for a in range(n):
            first.append(copy(a, 0, me, sibling, src=ins[a]))
            first += [copy(a, 1 + j, me, (*chip, c), src=ins[a]) for j, chip in enumerate(chips)]
        for cp in first:
            cp.start()
        passed = []
        for j, chip in enumerate(chips):
            for a in range(n):
                copy(a, 1 + j, (*chip, c), me).wait_recv()
                fwd = copy(a, 4 + j, (*chip, c), sibling)
                fwd.start()
                passed.append(fwd)
        for a in range(n):
            copy(a, 0, sibling, me).wait_recv()
        for j, chip in enumerate(chips):
            for a in range(n):
                copy(a, 4 + j, (*chip, 1 - c), me).wait_recv()
        for cp in first + passed:
            cp.wait_send()
        for cp in mine:
            cp.wait()

    return pl.pallas_call(
        body, name="all_gather_weights",
        in_specs=[hbm] * n, out_specs=[hbm] * n,
        out_shape=[jax.ShapeDtypeStruct((N_DEV,) + s.shape, s.dtype) for s in shards],
        scratch_shapes=[pltpu.SemaphoreType.DMA((n, 7)), pltpu.SemaphoreType.DMA((n, 7)), pltpu.SemaphoreType.DMA((n,))],
        compiler_params=pltpu.CompilerParams(has_side_effects=True),
    )(*shards)


def _peers(me):
    x, y, c = me
    out = []
    for k in range(1, 8):
        fx, fy, fc = (k >> 2) & 1, (k >> 1) & 1, k & 1
        out.append((k - 1, (1 - x if fx else x, 1 - y if fy else y, 1 - c if fc else c)))
    return out


HBM_SPEC = pl.BlockSpec(memory_space=pltpu.HBM)
SEM_SPEC = pl.BlockSpec(memory_space=pltpu.SEMAPHORE)
DATAFLOW_EFFECT = pltpu.SideEffectType.DATAFLOW_SIDE_EFFECTING


def _gather_refs(src, land, me, peer, k):
    return src, land.at[_dev_index(me)]


def _scatter_refs(src, land, me, peer, k):
    return src.at[_dev_index(peer)], land.at[k]


def _push_start(srcs, land_shapes, refs_of, name, after=()):
    n, n_after = len(srcs), len(after)

    def body(*refs):
        ins, lands = refs[:n], refs[n:2 * n]
        send_sems, recv_sems, token = refs[2 * n + n_after], refs[2 * n + n_after + 1], refs[-1]
        me = _my_pos()
        for k, peer in _peers(me):
            for a in range(n):
                src, dst = refs_of(ins[a], lands[a], me, peer, k)
                pltpu.make_async_remote_copy(src_ref=src, dst_ref=dst, send_sem=send_sems.at[7 * a + k],
                                             recv_sem=recv_sems.at[7 * a + k], device_id=peer, device_id_type=MESH).start()
        token[...] = jnp.zeros_like(token)

    lands = [lax.empty(shp, s.dtype) for shp, s in zip(land_shapes, srcs)]
    hbm = lambda a: pltpu.with_memory_space_constraint(a, pltpu.HBM)
    res = pl.pallas_call(
        body, name=name,
        out_shape=(pltpu.SemaphoreType.DMA((7 * n,)), pltpu.SemaphoreType.DMA((7 * n,)),
                   *[pltpu.HBM(s.shape, s.dtype) for s in srcs], *[pltpu.HBM(l.shape, l.dtype) for l in lands],
                   jax.ShapeDtypeStruct((8, 128), F32)),
        in_specs=[HBM_SPEC] * (2 * n) + [pl.BlockSpec(memory_space=pl.ANY)] * n_after,
        out_specs=(SEM_SPEC, SEM_SPEC, *[HBM_SPEC] * (2 * n), pl.BlockSpec(memory_space=pltpu.VMEM)),
        input_output_aliases={i: 2 + i for i in range(2 * n)},
        compiler_params=pltpu.CompilerParams(has_side_effects=DATAFLOW_EFFECT),
    )(*[hbm(s) for s in srcs], *[hbm(l) for l in lands], *after)
    return dict(sems=(res[0], res[1]), srcs=list(res[2:2 + n]), lands=list(res[2 + n:2 + 2 * n]), token=res[-1], n=n,
                refs_of=refs_of, name=name)


def _push_wait(started, after):
    n, refs_of = started["n"], started["refs_of"]
    after = list(after) if isinstance(after, (list, tuple)) else [after]

    def body(*refs):
        ins, lands = refs[:n], refs[n:2 * n]
        send_sems, recv_sems = refs[2 * n], refs[2 * n + 1]
        me = _my_pos()
        for k, peer in _peers(me):
            for a in range(n):
                src, dst = refs_of(ins[a], lands[a], me, peer, k)
                cp = pltpu.make_async_remote_copy(src_ref=src, dst_ref=dst, send_sem=send_sems.at[7 * a + k],
                                                  recv_sem=recv_sems.at[7 * a + k], device_id=peer, device_id_type=MESH)
                cp.wait_send()
                cp.wait_recv()

    arrs = started["srcs"] + started["lands"]
    res = pl.pallas_call(
        body, name=started["name"].replace("start", "wait"),
        out_shape=tuple(pltpu.HBM(a.shape, a.dtype) for a in arrs),
        in_specs=[HBM_SPEC] * (2 * n) + [SEM_SPEC, SEM_SPEC] + [pl.BlockSpec(memory_space=pl.ANY)] * len(after),
        out_specs=tuple([HBM_SPEC] * (2 * n)),
        input_output_aliases={i: i for i in range(2 * n)},
        compiler_params=pltpu.CompilerParams(has_side_effects=DATAFLOW_EFFECT),
    )(*arrs, *started["sems"], *after)
    return list(res[n:2 * n])


def _sum_slots(slots):
    def body(in_ref, out_ref):
        acc = in_ref[0]
        for d in range(1, N_DEV):
            acc = acc + in_ref[d]
        out_ref[...] = acc

    return pl.pallas_call(body, name="sum_small", out_shape=jax.ShapeDtypeStruct(slots.shape[1:], F32),
                          compiler_params=_cparams())(slots)


def _adamw_many(ws, gs, ms, vs):
    n = len(ws)

    def body(*refs):
        for i in range(n):
            w_ref, g_ref, m_ref, v_ref = (refs[j * n + i] for j in range(4))
            d_, nm, nv = _adamw_math(w_ref[...], g_ref[...], m_ref[...], v_ref[...])
            for j, val in enumerate((d_, nm, nv)):
                refs[(4 + j) * n + i][...] = val

    res = pl.pallas_call(body, name="adamw_small", out_shape=[jax.ShapeDtypeStruct(w_.shape, F32) for w_ in ws] * 3,
                         compiler_params=_cparams())(*ws, *gs, *ms, *vs)
    return [(res[i], res[n + i], res[2 * n + i]) for i in range(n)]


def _local_step(x, mem, tgt, W, P, late_weights, send_grads, reduce_small, tie0):
    S = x.shape[0]
    W = dict(W)
    h = _rmsnorm_fwd(x, P["g_mix"] + tie0, rows=S, name="norm_mix")
    proj = _matmul(h, W["w_in_t"], M=S, N=D_IN, K=D_MODEL, mode="nt", bm=512, bn=D_IN // 2, bk=D_MODEL, name="mm_in",
                   j_outer=True)

    wa_bd, wx_bd = _mx(_block_diag(P["w_rg_a"])), _mx(_block_diag(P["w_rg_x"]))
    lru_args = (W["conv_w"], P["conv_b"].reshape(1, -1), wa_bd, wx_bd, P["b_rg_a"].reshape(1, -1),
                P["b_rg_x"].reshape(1, -1), P["lru_lambda"].reshape(1, -1))
    hl, z_lru, a_lru, mult_lru = _lru_fwd(proj, *lru_args, S=S)

    buckets = _dil_buckets()
    bias = _dil_bias(P["rel_bias"], buckets)
    group_out = [_dilated_fwd(proj, bias, g, S=S) for g in range(len(DIL_GROUPS))]
    o_dil, o_dil_m, lse_dil = _dilated_merge([o for o, _ in group_out], [l for _, l in group_out], S=S)

    W.update(late_weights("branch", [o_dil, z_lru]))
    mem_n = _rmsnorm_fwd(mem, P["g_mem"], rows=N_MEM, name="norm_mem")
    kv = _matmul(mem_n, W["w_mem_kv"], M=N_MEM, N=2 * MEM_WIDTH, K=D_MODEL, mode="nn", bm=N_MEM, bn=512, bk=D_MODEL,
                 name="mm_kv")
    om, om_m, lse_mem = _mem_attn_fwd(proj, kv, S=S)
    b_gate = P["b_gate"].reshape(1, -1)
    merged = _mix_fwd(z_lru, o_dil_m, om_m, W["w_lru_out"], W["w_dil_out"], W["w_mem_out"], proj, b_gate, S=S)
    g_mlp, g_final, g_mix = (P[n].reshape(1, D_MODEL) for n in ("g_mlp", "g_final", "g_mix"))
    x1, hm = _matmul_rows(merged, W["w_out"], M=S, K=D_MODEL, mode="nn", bm=512, name="mm_out",
                          row_fn=_residual_then_norm, out_dtypes=(F32, MXU_DTYPE), tiles=[x], vecs=[g_mlp])
    W.update(late_weights("mlp", [hm]))

    def relu2(acc):
        rl = jnp.maximum(acc, 0.0)
        return rl * rl, rl

    act, relu_u = _matmul(hm, W["w_mlp_in_t"], M=S, N=D_FF, K=D_MODEL, mode="nt", bm=1024, bn=1024, bk=D_MODEL,
                          name="mm_mlp_in", out_dtypes=(MXU_DTYPE, MXU_DTYPE), epilogue=relu2, j_outer=True)
    dx2, dx2_m, loss, dg_final = _matmul_rows(
        act, W["w_mlp_out"], M=S, K=D_FF, mode="nn", bm=512, name="mm_mlp_out", row_fn=_residual_then_loss,
        out_dtypes=(F32, MXU_DTYPE), tiles=[x1, tgt], vecs=[g_final], acc_widths=(1, D_MODEL))

    G, Gs = {}, {}
    Gs["g_final"] = dg_final
    dw = dict(mode="tn", K=S, bk=S, out_dtypes=(MXU_DTYPE,))
    G["w_mlp_out"] = _matmul(act, dx2_m, M=D_FF, N=D_MODEL, bm=512, bn=D_MODEL, name="mm_dw_mlp_out",
                             parts=("rows", D_FF // N_DEV), **dw)
    du = _matmul(dx2_m, W["w_mlp_out"], M=S, N=D_FF, K=D_MODEL, mode="nt", bm=1024, bn=1024, bk=D_MODEL, name="mm_du",
                 out_dtypes=(MXU_DTYPE,), epilogue=lambda acc, rl: (acc * (2.0 * rl.astype(F32)),),
                 extras=[(relu_u, (0, 0))], j_outer=True)
    G["w_mlp_in"] = _matmul(hm, du, M=D_MODEL, N=D_FF, bm=D_MODEL, bn=512, name="mm_dw_mlp_in",
                            parts=("cols", D_FF // N_DEV), **dw)
    tie1 = send_grads({n: G.pop(n) for n in ("w_mlp_out", "w_mlp_in")})
    dx1, dx1_m, Gs["g_mlp"] = _matmul_rows(
        du, W["w_mlp_in_t"], M=S, K=D_FF, mode="nn", bm=512, name="mm_dhm", row_fn=_norm_bwd_then_residual(2),
        out_dtypes=(F32, MXU_DTYPE), tiles=[x1, dx2], vecs=[g_mlp], acc_widths=(D_MODEL,), deps=[tie1])
    G["w_out"] = _matmul(merged, dx1_m, M=D_MODEL, N=D_MODEL, bm=512, bn=D_MODEL, name="mm_dw_out",
                         parts=("rows", D_MODEL // N_DEV), **dw)
    dmerged = _matmul(dx1_m, W["w_out"], M=S, N=D_MODEL, K=D_MODEL, mode="nt", bm=512, bn=D_MODEL, bk=D_MODEL, name="mm_dmerged")
    (dg0, dg1, dg2, dy_lru, dy_dil, dy_mem, db0, db1, db2) = _mix_bwd(
        dmerged, z_lru, o_dil_m, om_m, W["w_lru_out"], W["w_dil_out"], W["w_mem_out"], proj, b_gate, S=S)
    Gs["b_gate0"], Gs["b_gate1"], Gs["b_gate2"] = db0, db1, db2

    G["w_mem_out"] = _matmul(om_m, dy_mem, M=MEM_WIDTH, N=D_MODEL, bm=MEM_WIDTH, bn=D_MODEL, name="mm_dw_mem_out",
                             parts=("cols", D_MODEL // N_DEV), **dw)
    dom = _matmul(dy_mem, W["w_mem_out"], M=S, N=MEM_WIDTH, K=D_MODEL, mode="nt", bm=512, bn=MEM_WIDTH, bk=D_MODEL,
                  name="mm_dom")
    dqm, dk_mem, dv_mem = _mem_attn_bwd(proj, kv, om, lse_mem, dom, S=S)
    dkv = jnp.concatenate([dk_mem, dv_mem], axis=1)
    G["w_mem_kv"] = _matmul(mem_n, dkv, M=D_MODEL, N=2 * MEM_WIDTH, K=N_MEM, mode="tn", bm=D_MODEL, bn=2 * MEM_WIDTH,
                            bk=N_MEM, name="mm_dw_kv", out_dtypes=(MXU_DTYPE,), parts=("rows", D_MODEL // N_DEV))
    dmem_n = _matmul(dkv, W["w_mem_kv"], M=N_MEM, N=D_MODEL, K=2 * MEM_WIDTH, mode="nt", bm=N_MEM, bn=D_MODEL,
                     bk=2 * MEM_WIDTH, name="mm_dmem")
    (Gs["g_mem"],) = _rmsnorm_bwd(mem, P["g_mem"], dmem_n, None, rows=N_MEM, name="norm_mem_bwd", dx_dtypes=())

    G["w_dil_out"] = _matmul(o_dil_m, dy_dil, M=256, N=D_MODEL, bm=256, bn=D_MODEL, name="mm_dw_dil_out",
                             parts=("cols", D_MODEL // N_DEV), **dw)
    do_dil, delta = _matmul(dy_dil, W["w_dil_out"], M=S, N=256, K=D_MODEL, mode="nt", bm=512, bn=256, bk=D_MODEL,
                            name="mm_do_dil", out_dtypes=(F32, F32), epilogue=_with_delta, extras=[(o_dil, (0, 0))])
    G["w_lru_out"] = _matmul(z_lru, dy_lru, M=D_RNN, N=D_MODEL, bm=D_RNN, bn=D_MODEL, name="mm_dw_lru_out",
                             parts=("cols", D_MODEL // N_DEV), **dw)
    dz = _matmul(dy_lru, W["w_lru_out"], M=S, N=D_RNN, K=D_MODEL, mode="nt", bm=512, bn=D_RNN, bk=D_MODEL, name="mm_dz_lru")
    tie2 = send_grads({n: G.pop(n) for n in ("w_out", "w_mem_out", "w_mem_kv", "w_dil_out", "w_lru_out")})
    bias = bias + tie2[0, 0]
    dqkv, dbias = None, []
    for g in range(len(DIL_GROUPS)):
        *dqkv, db_g = _dilated_bwd(proj, do_dil, lse_dil, delta, bias, g, S=S, into=dqkv)
        dbias.append(db_g)
    drel = _dil_bias_bwd(jnp.stack(dbias, axis=0).reshape(len(DIL_GROUPS), DIL_HEADS, SPAN, 2 * SPAN), buckets)
    Gs["rel_bias"] = drel

    dxl, dgl, dcw, dcb, dwa, dwx, dba, dbx, dlam = _lru_bwd(proj, hl, a_lru, mult_lru, dz, *lru_args, S=S)
    Gs["conv_w"], Gs["conv_b"] = dcw, dcb
    Gs["w_rg_a"], Gs["w_rg_x"] = _block_diag_extract(dwa), _block_diag_extract(dwx)
    Gs["b_rg_a"], Gs["b_rg_x"], Gs["lru_lambda"] = dba, dbx, dlam
    Gs["loss"] = loss

    dproj = [dxl, dgl] + dqkv + [dqm, dg0, dg1, dg2]
    tie = []
    for q in range(W_IN_PIECES):
        dw_q = None
        for half in range(2):
            dw_q = _dw_in_t_half(h, dproj, q, half, S=S, name=f"mm_dw_in_{q}_{half}", into=dw_q, deps=tie)
        tie = [send_grads({f"w_in_{q}": dw_q})]
    grad_x, Gs["g_mix"] = _matmul_rows(
        dproj, W["w_in_t"], M=S, K=D_IN, mode="nn", bm=256, name="mm_dh", row_fn=_norm_bwd_then_residual(1),
        out_dtypes=(F32,), tiles=[x, dx1], vecs=[g_mix], acc_widths=(D_MODEL,), deps=tie)
    return grad_x, reduce_small(Gs)


BIG = ("w_in", "w_lru_out", "w_dil_out", "w_mem_kv", "w_mem_out", "w_out", "w_mlp_in", "w_mlp_out")
W_IN_PIECES = 2
COL_SHARDED = ("w_lru_out", "w_dil_out", "w_mem_out", "w_mlp_in")
GATHERED_TRANSPOSED = ("w_mlp_in",)
SMALL = ("g_mix", "b_gate", "conv_b", "w_rg_a", "b_rg_a", "w_rg_x", "b_rg_x", "lru_lambda", "rel_bias", "g_mem",
         "g_mlp", "g_final")
WEIGHTS = ("g_mix", "w_in", "b_gate", "conv_w", "conv_b", "w_rg_a", "b_rg_a", "w_rg_x", "b_rg_x", "lru_lambda",
           "w_lru_out", "rel_bias", "w_dil_out", "g_mem", "w_mem_kv", "w_mem_out", "w_out", "g_mlp", "w_mlp_in",
           "w_mlp_out", "g_final")


def _gathered_to_full(name, gathered):
    if name in COL_SHARDED:
        n, r, c = gathered.shape
        return gathered.transpose(1, 0, 2).reshape(r, n * c)
    n, r, c = gathered.shape
    return gathered.reshape(n * r, c)


SMALL_GRADS = (("g_mix", (1, 1024)), ("b_gate0", (1, 1024)), ("b_gate1", (1, 1024)), ("b_gate2", (1, 1024)),
               ("conv_b", (1, 768)), ("w_rg_a", (12, 64, 64)), ("b_rg_a", (1, 768)), ("w_rg_x", (12, 64, 64)),
               ("b_rg_x", (1, 768)), ("lru_lambda", (1, 768)), ("rel_bias", (32, 128)), ("g_mem", (1, 1024)),
               ("g_mlp", (1, 1024)), ("g_final", (1, 1024)), ("conv_w", (4, 768)), ("loss", (1, 1)))


def _pack(parts):
    flat = jnp.concatenate([p.reshape(-1) for p in parts])
    return jnp.pad(flat, (0, (-flat.shape[0]) % 1024)).reshape(-1, 128)


def _unpack(pack, shapes):
    flat = pack.reshape(-1)
    out, off = [], 0
    for shp in shapes:
        size = math.prod(shp)
        out.append(flat[off:off + size].reshape(shp))
        off += size
    return out


def kernel(x, mem, g_mix, w_in, b_gate, conv_w, conv_b, w_rg_a, b_rg_a, w_rg_x, b_rg_x, lru_lambda, w_lru_out, rel_bias, w_dil_out, g_mem, w_mem_kv, w_mem_out, w_out, g_mlp, w_mlp_in, w_mlp_out, g_final, loss_target, m_g_mix, m_w_in, m_b_gate, m_conv_w, m_conv_b, m_w_rg_a, m_b_rg_a, m_w_rg_x, m_b_rg_x, m_lru_lambda, m_w_lru_out, m_rel_bias, m_w_dil_out, m_g_mem, m_w_mem_kv, m_w_mem_out, m_w_out, m_g_mlp, m_w_mlp_in, m_w_mlp_out, m_g_final, v_g_mix, v_w_in, v_b_gate, v_conv_w, v_conv_b, v_w_rg_a, v_b_rg_a, v_w_rg_x, v_b_rg_x, v_lru_lambda, v_w_lru_out, v_rel_bias, v_w_dil_out, v_g_mem, v_w_mem_kv, v_w_mem_out, v_w_out, v_g_mlp, v_w_mlp_in, v_w_mlp_out, v_g_final):
    w = dict(g_mix=g_mix, w_in=w_in, b_gate=b_gate, conv_w=conv_w, conv_b=conv_b, w_rg_a=w_rg_a, b_rg_a=b_rg_a,
             w_rg_x=w_rg_x, b_rg_x=b_rg_x, lru_lambda=lru_lambda, w_lru_out=w_lru_out, rel_bias=rel_bias,
             w_dil_out=w_dil_out, g_mem=g_mem, w_mem_kv=w_mem_kv, w_mem_out=w_mem_out, w_out=w_out, g_mlp=g_mlp,
             w_mlp_in=w_mlp_in, w_mlp_out=w_mlp_out, g_final=g_final)
    m = dict(g_mix=m_g_mix, w_in=m_w_in, b_gate=m_b_gate, conv_w=m_conv_w, conv_b=m_conv_b, w_rg_a=m_w_rg_a,
             b_rg_a=m_b_rg_a, w_rg_x=m_w_rg_x, b_rg_x=m_b_rg_x, lru_lambda=m_lru_lambda, w_lru_out=m_w_lru_out,
             rel_bias=m_rel_bias, w_dil_out=m_w_dil_out, g_mem=m_g_mem, w_mem_kv=m_w_mem_kv, w_mem_out=m_w_mem_out,
             w_out=m_w_out, g_mlp=m_g_mlp, w_mlp_in=m_w_mlp_in, w_mlp_out=m_w_mlp_out, g_final=m_g_final)
    v = dict(g_mix=v_g_mix, w_in=v_w_in, b_gate=v_b_gate, conv_w=v_conv_w, conv_b=v_conv_b, w_rg_a=v_w_rg_a,
             b_rg_a=v_b_rg_a, w_rg_x=v_w_rg_x, b_rg_x=v_b_rg_x, lru_lambda=v_lru_lambda, w_lru_out=v_w_lru_out,
             rel_bias=v_rel_bias, w_dil_out=v_w_dil_out, g_mem=v_g_mem, w_mem_kv=v_w_mem_kv, w_mem_out=v_w_mem_out,
             w_out=v_w_out, g_mlp=v_g_mlp, w_mlp_in=v_w_mlp_in, w_mlp_out=v_w_mlp_out, g_final=v_g_final)

    my_idx = _dev_index(_my_pos())

    g_in, g_cw = _all_gather([_mx(w["w_in"].T), w["conv_w"]])
    W = {"w_in_t": g_in.reshape(D_IN, D_MODEL), "conv_w": g_cw.transpose(1, 0, 2).reshape(CONV_WIDTH, D_RNN)}
    late, order_after = {}, [g_in]
    for group, names in (("branch", ("w_mem_kv", "w_lru_out", "w_dil_out", "w_mem_out", "w_out")),
                         ("mlp", ("w_mlp_in", "w_mlp_out"))):
        shards = [_mx(w[n].T if n in GATHERED_TRANSPOSED else w[n]) for n in names]
        started = _push_start(shards, [(N_DEV,) + s.shape for s in shards], _gather_refs, f"gather_{group}_start",
                              after=order_after)
        late[group] = (names, shards, started)
        order_after = [started["token"]]
    P = {n: w[n] for n in SMALL}

    def late_weights(group, after):
        names, shards, started = late[group]
        out = {}
        for n, land, own in zip(names, _push_wait(started, after), shards):
            full = lax.dynamic_update_index_in_dim(land, own, my_idx, 0)
            if n in GATHERED_TRANSPOSED:
                out[n + "_t"] = full.reshape(-1, full.shape[2])
            else:
                out[n] = _gathered_to_full(n, full)
        return out

    sent, small = [], {}

    def send_grads(gs):
        names = list(gs)
        parts = [gs[n] for n in names]
        own = [lax.dynamic_index_in_dim(p, my_idx, 0, keepdims=False) for p in parts]
        started = _push_start(parts, [(N_DEV - 1,) + p.shape[1:] for p in parts], _scatter_refs,
                              f"scatter{len(sent)}_start")
        sent.append((names, own, started))
        return started["token"]

    def reduce_small(gs):
        small["pack"] = _pack([gs[n] for n, _ in SMALL_GRADS])
        small["started"] = _push_start([small["pack"]], [(N_DEV,) + small["pack"].shape], _gather_refs, "small_start")
        return small["started"]["token"]

    grad_x, last_token = _local_step(x[0], mem[0], loss_target[0], W, P, late_weights, send_grads, reduce_small,
                                     late["mlp"][2]["token"][0, 0])

    grads, deltas, new_m, new_v = {}, {}, {}, {}
    after = last_token
    for names, own, started in sent[:-W_IN_PIECES]:
        for n, o, land in zip(names, own, _push_wait(started, after)):
            grads[n], deltas[n], new_m[n], new_v[n] = _adamw_landed(w[n], o, land, m[n], v[n], name=f"adamw_{n}")
            after = deltas[n]
    prev = None
    for q, (names, own, started) in enumerate(sent[-W_IN_PIECES:]):
        (land,) = _push_wait(started, after)
        prev = _adamw_landed(w["w_in"].T, own[0], land, m["w_in"].T, v["w_in"].T, name=f"adamw_{names[0]}",
                             col_blk=q, prev=prev)
        after = prev[1]
    grads["w_in"], deltas["w_in"], new_m["w_in"], new_v["w_in"] = [t.T for t in prev]
    (small_land,) = _push_wait(small["started"], [after] + [deltas[n] for n in BIG if n != "w_in"])
    total = _sum_slots(lax.dynamic_update_index_in_dim(small_land, small["pack"], my_idx, 0))
    summed = dict(zip([n for n, _ in SMALL_GRADS], _unpack(total, [shp for _, shp in SMALL_GRADS])))
    summed["b_gate"] = jnp.concatenate([summed.pop(f"b_gate{b}") for b in range(3)], axis=1)
    summed["rel_bias"] = summed["rel_bias"][:, :3 * DIL_HEADS]
    for n in SMALL:
        grads[n] = summed[n].reshape(w[n].shape)
    small_updates = _adamw_many([w[n] for n in SMALL], [grads[n] for n in SMALL], [m[n] for n in SMALL],
                                [v[n] for n in SMALL])
    for n, (d_, nm_, nv_) in zip(SMALL, small_updates):
        deltas[n], new_m[n], new_v[n] = d_, nm_, nv_
    conv_w_sum, loss_sum = summed["conv_w"], summed["loss"]
    cw_cols = D_RNN // N_DEV
    grads["conv_w"] = lax.dynamic_slice(conv_w_sum, (0, my_idx * cw_cols), (CONV_WIDTH, cw_cols))
    deltas["conv_w"], new_m["conv_w"], new_v["conv_w"] = _adamw_plain(
        w["conv_w"], grads["conv_w"], m["conv_w"], v["conv_w"], name="adamw_conv_w")

    return (loss_sum.reshape(()), grad_x[None], *[grads[n] for n in WEIGHTS], *[deltas[n] for n in WEIGHTS],
            *[new_m[n] for n in WEIGHTS], *[new_v[n] for n in WEIGHTS])
```

```python
import functools
import math

import jax
import jax.numpy as jnp
from jax import lax
from jax.experimental import pallas as pl
from jax.experimental.pallas import tpu as pltpu

F32 = jnp.float32
MXU_DTYPE = jnp.bfloat16
VMEM_LIMIT_BYTES = 56 * 1024 * 1024
N_DEV = 8

D_MODEL = 1024
N_MEM = 256
MEM_HEADS = 4
MEM_HEAD_DIM = 128
MEM_WIDTH = 512
D_RNN = 768
LRU_BLOCK = 64
N_LRU_BLOCKS = 12
LRU_GROUP = 256
N_LRU_GROUPS = 3
CONV_WIDTH = 4
LRU_C = 8.0
DIL_GROUPS = ((128, 1), (512, 4), (2048, 16))
SPAN = 128
DIL_HEADS = 4
DIL_HEAD_DIM = 64
NUM_BUCKETS = 32
MAX_DISTANCE = 2048
D_FF = 4096
D_IN = 7424
EPS = 1e-6
NEG = -1e30
C_XL, C_GATE, C_QKV, C_QM, C_GATES = 0, 768, 1536, 3840, 4352

ADAM_LR = 0.001
ADAM_B1 = 0.9
ADAM_B2 = 0.999
ADAM_EPS = 1e-08
ADAM_WD = 0.01
ADAM_STEP = 10

MESH = pl.DeviceIdType.MESH
GELU_K = math.sqrt(2.0 / math.pi)


def _cparams(sem=None):
    kw = dict(vmem_limit_bytes=VMEM_LIMIT_BYTES)
    if sem is not None:
        kw["dimension_semantics"] = sem
    return pltpu.CompilerParams(**kw)


def _mx(v):
    return v.astype(MXU_DTYPE)


def _dot(a, b, mode="nn"):
    dims = {"nn": (((1,), (0,)), ((), ())), "nt": (((1,), (1,)), ((), ())), "tn": (((0,), (0,)), ((), ()))}[mode]
    return lax.dot_general(_mx(a), _mx(b), dims, preferred_element_type=F32)


def _colsum(v):
    return jnp.sum(v, axis=0, keepdims=True)


def _matmul(a, b, *, M, N, K, mode, bm, bn, bk, name, out_dtypes=(F32,), epilogue=None, extras=(),
            a_off=(0, 0), b_off=(0, 0), j_outer=False, deps=(), parts=None):
    assert M % bm == 0 and N % bn == 0 and K % bk == 0, (name, M, N, K, bm, bn, bk)
    nm, nn, nk = M // bm, N // bn, K // bk

    def ij(f):
        if j_outer:
            return lambda j, i, k: f(i, j, k)
        return f

    if mode == "tn":
        a_spec = pl.BlockSpec((bk, bm), ij(lambda i, j, k: (k + a_off[0], i + a_off[1])))
    else:
        a_spec = pl.BlockSpec((bm, bk), ij(lambda i, j, k: (i + a_off[0], k + a_off[1])))
    if mode == "nt":
        b_spec = pl.BlockSpec((bn, bk), ij(lambda i, j, k: (j + b_off[0], k + b_off[1])))
    else:
        b_spec = pl.BlockSpec((bk, bn), ij(lambda i, j, k: (k + b_off[0], j + b_off[1])))
    ex_specs = [pl.BlockSpec((bm, bn), ij(functools.partial(lambda i, j, k, o: (i + o[0], j + o[1]), o=off)))
                for _, off in extras]
    if parts is None:
        out_dims = (M, N)
        out_spec = pl.BlockSpec((bm, bn), ij(lambda i, j, k: (i, j)))
    elif parts[0] == "rows":
        r = parts[1]
        assert bm % r == 0
        out_dims = (M // r, r, N)
        out_spec = pl.BlockSpec((bm // r, r, bn), ij(lambda i, j, k: (i, 0, j)))
    elif parts[0] == "rows_t":
        r = parts[1]
        assert bn % r == 0
        out_dims = (N // r, r, M)
        out_spec = pl.BlockSpec((bn // r, r, bm), ij(lambda i, j, k: (j, 0, i)))
    else:
        c = parts[1]
        assert bn % c == 0
        out_dims = (N // c, M, c)
        out_spec = pl.BlockSpec((bn // c, bm, c), ij(lambda i, j, k: (j, i, 0)))
    n_ex, n_out, n_dep = len(extras), len(out_dtypes), len(deps)

    def body(*refs):
        a_ref, b_ref = refs[0], refs[1]
        ex = refs[2:2 + n_ex]
        outs = refs[2 + n_ex + n_dep:2 + n_ex + n_dep + n_out]
        part = _dot(a_ref[...], b_ref[...], mode)

        def finish(acc):
            vals = epilogue(acc, *[e[...] for e in ex]) if epilogue is not None else (acc,)
            for o, v in zip(outs, vals):
                if parts is not None and parts[0] == "rows_t":
                    v = v.T
                v = v.astype(o.dtype)
                if parts is None:
                    o[...] = v
                elif parts[0] in ("rows", "rows_t"):
                    for ch in range(v.shape[0] // parts[1]):
                        o[ch] = v[ch * parts[1]:(ch + 1) * parts[1], :]
                else:
                    for ch in range(bn // parts[1]):
                        o[ch] = v[:, ch * parts[1]:(ch + 1) * parts[1]]

        if nk == 1:
            finish(part)
        else:
            acc_ref = refs[-1]
            k = pl.program_id(2)

            @pl.when(k == 0)
            def _():
                acc_ref[...] = part

            @pl.when(k > 0)
            def _():
                acc_ref[...] += part

            @pl.when(k == nk - 1)
            def _():
                finish(acc_ref[...])

    grid = (nn, nm, nk) if j_outer else (nm, nn, nk)
    res = pl.pallas_call(
        body, name=name, grid=grid,
        in_specs=[a_spec, b_spec] + ex_specs + [pl.BlockSpec(memory_space=pl.ANY)] * n_dep,
        out_specs=[out_spec] * n_out,
        out_shape=[jax.ShapeDtypeStruct(out_dims, dt) for dt in out_dtypes],
        scratch_shapes=[pltpu.VMEM((bm, bn), F32)] if nk > 1 else [],
        compiler_params=_cparams(("parallel", "parallel", "arbitrary")),
    )(a, b, *[e for e, _ in extras], *deps)
    return res[0] if n_out == 1 else res


ROW_SUBTILES = 2


def _matmul_rows(a, b, *, M, K, mode, bm, name, row_fn, out_dtypes, tiles=(), vecs=(), acc_widths=(), deps=()):
    N = D_MODEL
    assert M % bm == 0
    segs = list(a) if isinstance(a, (list, tuple)) else [a]
    widths = [s_.shape[1] for s_ in segs]
    assert sum(widths) == K and (len(segs) == 1 or mode == "nn")
    n_s, n_t, n_v, n_o, n_a, n_d = len(segs), len(tiles), len(vecs), len(out_dtypes), len(acc_widths), len(deps)
    row = pl.BlockSpec((bm, N), lambda i: (i, 0))
    b_shape = (K, N) if mode == "nn" else (N, K)

    def body(*refs):
        b_ref = refs[n_s]
        ins = refs[n_s + 1:n_s + 1 + n_t + n_v]
        outs = refs[n_s + 1 + n_t + n_v + n_d:n_s + 1 + n_t + n_v + n_d + n_o]
        accs = refs[n_s + 1 + n_t + n_v + n_d + n_o:]
        for o in accs:
            @pl.when(pl.program_id(0) == 0)
            def _(o=o):
                o[...] = jnp.zeros_like(o)

        for s_ in range(ROW_SUBTILES):
            rows = pl.ds(s_ * (bm // ROW_SUBTILES), bm // ROW_SUBTILES)
            if n_s == 1:
                acc = _dot(refs[0][rows, :], b_ref[...], mode)
            else:
                acc, k0 = None, 0
                for a_ref, w_ in zip(refs[:n_s], widths):
                    part = _dot(a_ref[rows, :], b_ref[k0:k0 + w_, :])
                    acc = part if acc is None else acc + part
                    k0 += w_
            tile_vals, partials = row_fn(acc, *[r[rows, :] for r in ins[:n_t]], *[r[...] for r in ins[n_t:]])
            for o, val in zip(outs, tile_vals):
                o[rows, :] = val.astype(o.dtype)
            for o, val in zip(accs, partials):
                o[...] += val

    res = pl.pallas_call(
        body, name=name, grid=(M // bm,),
        in_specs=[pl.BlockSpec((bm, w_), lambda i: (i, 0)) for w_ in widths] + [pl.BlockSpec(b_shape, lambda i: (0, 0))]
        + [row] * n_t + [pl.BlockSpec((1, N), lambda i: (0, 0))] * n_v + [pl.BlockSpec(memory_space=pl.ANY)] * n_d,
        out_specs=[row] * n_o + [pl.BlockSpec((1, w_), lambda i: (0, 0)) for w_ in acc_widths],
        out_shape=[jax.ShapeDtypeStruct((M, N), dt) for dt in out_dtypes]
        + [jax.ShapeDtypeStruct((1, w_), F32) for w_ in acc_widths],
        compiler_params=_cparams(("arbitrary",) if n_a else ("parallel",)),
    )(*segs, b, *tiles, *vecs, *deps)
    return res


def _dw_in_t_half(h, pieces, q, half, *, S, name, into=None, deps=(), bk=1024):
    half_w, cols = D_IN // 2, D_MODEL // W_IN_PIECES
    lo, hi = half * half_w, (half + 1) * half_w
    use, c0 = [], 0
    for p in pieces:
        w_ = p.shape[1]
        a0, a1 = max(lo, c0), min(hi, c0 + w_)
        if a1 > a0:
            use.append((p, a0 - c0, a1 - a0))
        c0 += w_
    n_p, n_into, n_d, nk = len(use), 0 if into is None else 1, len(deps), S // bk
    rows = D_IN // N_DEV

    def body(*refs):
        h_ref, p_refs = refs[0], refs[1:1 + n_p]
        o_ref, acc_ref = refs[1 + n_p + n_into + n_d], refs[-1]
        k = pl.program_id(0)
        dp = jnp.concatenate([r[:, s0:s0 + w_] for r, (_, s0, w_) in zip(p_refs, use)], axis=1)
        part = _dot(h_ref[...], dp, "tn")

        @pl.when(k == 0)
        def _():
            acc_ref[...] = part

        @pl.when(k > 0)
        def _():
            acc_ref[...] += part

        @pl.when(k == nk - 1)
        def _():
            vt = acc_ref[...].T.astype(o_ref.dtype)
            for ch in range(half_w // rows):
                o_ref[ch] = vt[ch * rows:(ch + 1) * rows, :]

    return pl.pallas_call(
        body, name=name, grid=(nk,),
        in_specs=[pl.BlockSpec((bk, cols), lambda k: (k, q))]
        + [pl.BlockSpec((bk, p.shape[1]), lambda k: (k, 0)) for p, _, _ in use]
        + [pl.BlockSpec(memory_space=pl.ANY)] * (n_into + n_d),
        out_specs=pl.BlockSpec((half_w // rows, rows, cols), lambda k: (half, 0, 0)),
        out_shape=jax.ShapeDtypeStruct((N_DEV, rows, cols), MXU_DTYPE),
        input_output_aliases={1 + n_p: 0} if n_into else {},
        scratch_shapes=[pltpu.VMEM((cols, half_w), F32)],
        compiler_params=_cparams(("arbitrary",)),
    )(h, *[p for p, _, _ in use], *([into] if n_into else []), *deps)


def _rmsnorm_fwd(x, g, *, rows, name, bt=512):
    bt = min(bt, rows)

    def body(x_ref, g_ref, o_ref):
        xv = x_ref[...]
        r = lax.rsqrt(jnp.mean(xv * xv, axis=-1, keepdims=True) + EPS)
        o_ref[...] = (xv * r * g_ref[...]).astype(o_ref.dtype)

    return pl.pallas_call(
        body, name=name, grid=(rows // bt,),
        in_specs=[pl.BlockSpec((bt, D_MODEL), lambda i: (i, 0)), pl.BlockSpec((1, D_MODEL), lambda i: (0, 0))],
        out_specs=pl.BlockSpec((bt, D_MODEL), lambda i: (i, 0)),
        out_shape=jax.ShapeDtypeStruct((rows, D_MODEL), MXU_DTYPE),
        compiler_params=_cparams(("parallel",)),
    )(x, g.reshape(1, D_MODEL))


def _rms_bwd_tile(xv, gv, dyv):
    r = lax.rsqrt(jnp.mean(xv * xv, axis=-1, keepdims=True) + EPS)
    w = dyv * gv
    dx = r * w - xv * (r * r * r) * jnp.mean(w * xv, axis=-1, keepdims=True)
    dg = _colsum(dyv * xv * r)
    return dx, dg


def _residual_then_norm(acc, x_t, g):
    x1 = x_t + acc
    r = lax.rsqrt(jnp.mean(x1 * x1, axis=-1, keepdims=True) + EPS)
    return (x1, x1 * r * g), ()


def _residual_then_loss(acc, x_t, tgt_t, g):
    x2 = x_t + acc
    r = lax.rsqrt(jnp.mean(x2 * x2, axis=-1, keepdims=True) + EPS)
    diff = x2 * r * g - tgt_t
    part = jnp.sum(jnp.mean(diff * diff, axis=-1, keepdims=True), axis=0, keepdims=True) * 0.5
    dx, dg = _rms_bwd_tile(x2, g, diff * (1.0 / D_MODEL))
    return (dx, dx), (part, dg)


def _norm_bwd_then_residual(n_out):
    def fn(acc, x_t, res_t, g):
        dx, dg = _rms_bwd_tile(x_t, g, acc)
        return (dx + res_t,) * n_out, (dg,)

    return fn


def _rmsnorm_bwd(x, g, dy, res, *, rows, name, bt=512, dx_dtypes=(F32,)):
    bt = min(bt, rows)
    has_res = res is not None

    def body(*refs):
        x_ref, g_ref, dy_ref = refs[:3]
        res_ref = refs[3] if has_res else None
        outs = refs[3 + int(has_res):]
        dx, dg = _rms_bwd_tile(x_ref[...], g_ref[...], dy_ref[...])
        if has_res:
            dx = dx + res_ref[...]
        dg_ref = outs[-1]

        @pl.when(pl.program_id(0) == 0)
        def _():
            dg_ref[...] = jnp.zeros_like(dg_ref)

        dg_ref[...] += dg
        for o in outs[:-1]:
            o[...] = dx.astype(o.dtype)

    row_spec = pl.BlockSpec((bt, D_MODEL), lambda i: (i, 0))
    vec_spec = pl.BlockSpec((1, D_MODEL), lambda i: (0, 0))
    ins = [x, g.reshape(1, D_MODEL), dy] + ([res] if has_res else [])
    return pl.pallas_call(
        body, name=name, grid=(rows // bt,),
        in_specs=[row_spec, vec_spec, row_spec] + ([row_spec] if has_res else []),
        out_specs=[row_spec] * len(dx_dtypes) + [vec_spec],
        out_shape=[jax.ShapeDtypeStruct((rows, D_MODEL), dt) for dt in dx_dtypes] + [jax.ShapeDtypeStruct((1, D_MODEL), F32)],
        compiler_params=_cparams(("arbitrary",)),
    )(*ins)


LRU_T = 512
SCAN_GROUPS = 4


def _gelu(x):
    t = jnp.tanh(GELU_K * (x + 0.044715 * x * x * x))
    return 0.5 * x * (1.0 + t), t


def _gelu_grad(x, t):
    return 0.5 * (1.0 + t) + 0.5 * x * (1.0 - t * t) * GELU_K * (1.0 + 3.0 * 0.044715 * x * x)


def _softplus_neg(lam):
    z = -lam
    u = jnp.exp(-jnp.abs(z))
    w = 1.0 + u
    l1p = jnp.where(w == 1.0, u, jnp.log(w) * u / jnp.where(w == 1.0, 1.0, w - 1.0))
    return jnp.maximum(z, 0.0) + l1p


def _shift_down(cur, prev8, k, row8):
    y = pltpu.roll(cur, k, 0)
    head = jnp.where(row8 < k, pltpu.roll(prev8, k, 0), y[0:8])
    return jnp.concatenate([head, y[8:]], axis=0)


def _shift_up(cur, next8, k, row8):
    n = cur.shape[0]
    y = pltpu.roll(cur, n - k, 0)
    tail = jnp.where(row8 >= 8 - k, pltpu.roll(next8, 8 - k, 0), y[n - 8:n])
    return jnp.concatenate([y[0:n - 8], tail], axis=0)


def _lru_gates(xl, p8, cw, cb, wa, wx, ba, bx, lam, row8, a_mult=None):
    sh = [xl] + [_shift_down(xl, p8, k, row8) for k in (1, 2, 3)]
    xc = cb + cw[3:4] * sh[0] + cw[2:3] * sh[1] + cw[1:2] * sh[2] + cw[0:1] * sh[3]
    r = jax.nn.sigmoid(_dot(xc, wa) + ba)
    i = jax.nn.sigmoid(_dot(xc, wx) + bx)
    sp = _softplus_neg(lam)
    if a_mult is None:
        la = -LRU_C * r * sp
        a = jnp.exp(la)
        mult = jnp.sqrt(jnp.tanh(-la) * (a * a + 1.0))
    else:
        a, mult = a_mult
    return dict(sh=sh, xc=xc, r=r, i=i, sp=sp, a=a, mult=mult)


def _lru_specs(n_t, reverse):
    T = LRU_T
    tt = (lambda t: n_t - 1 - t) if reverse else (lambda t: t)
    blk = lambda col0: pl.BlockSpec((T, LRU_GROUP), lambda g, t: (tt(t), col0 + g))
    prev8 = lambda col0: pl.BlockSpec((8, LRU_GROUP), lambda g, t: (jnp.maximum(tt(t) * (T // 8) - 1, 0), col0 + g))
    vec = lambda rows: pl.BlockSpec((rows, LRU_GROUP), lambda g, t: (0, g))
    wbd = pl.BlockSpec((1, LRU_GROUP, LRU_GROUP), lambda g, t: (g, 0, 0))
    return blk, prev8, vec, wbd


def _lru_fwd(proj, conv_w, conv_b, wa_bd, wx_bd, b_a, b_x, lam, *, S):
    T = LRU_T
    n_t = S // T
    blk, _, vec, wbd = _lru_specs(n_t, False)

    def body(xl_ref, gate_ref, cw_ref, cb_ref, wa_ref, wx_ref, ba_ref, bx_ref, lam_ref,
             hl_ref, z_ref, a_s, m_ref, prev8, hcar, b_s):
        @pl.when(pl.program_id(1) == 0)
        def _():
            prev8[...] = jnp.zeros_like(prev8)
            hcar[...] = jnp.zeros_like(hcar)

        row8 = lax.broadcasted_iota(jnp.int32, (8, LRU_GROUP), 0)
        xl = xl_ref[...]
        q = _lru_gates(xl, prev8[...], cw_ref[...], cb_ref[...], wa_ref[0], wx_ref[0], ba_ref[...], bx_ref[...],
                       lam_ref[...], row8)
        prev8[...] = xl[T - 8:T]
        a_s[...] = q["a"]
        m_ref[...] = q["mult"]
        b_s[...] = q["mult"] * q["i"] * q["xc"]

        def step(c, carry):
            local = []
            for u in range(SCAN_GROUPS):
                off = pl.multiple_of((c * SCAN_GROUPS + u) * 8, 8)
                A = a_s[pl.ds(off, 8), :]
                B = b_s[pl.ds(off, 8), :]
                for k in (1, 2, 4):
                    a_sh = jnp.where(row8 >= k, pltpu.roll(A, k, 0), 1.0)
                    b_sh = jnp.where(row8 >= k, pltpu.roll(B, k, 0), 0.0)
                    B = A * b_sh + B
                    A = A * a_sh
                local.append((off, A, B))
            for off, A, B in local:
                h = A * carry + B
                hl_ref[pl.ds(off, 8), :] = h
                carry = h[7:8, :]
            return carry

        hcar[...] = lax.fori_loop(0, T // (8 * SCAN_GROUPS), step, hcar[...])
        ge, _ = _gelu(gate_ref[...])
        z_ref[...] = (ge * hl_ref[...]).astype(z_ref.dtype)

    return pl.pallas_call(
        body, name="lru_fwd", grid=(N_LRU_GROUPS, n_t),
        in_specs=[blk(C_XL // LRU_GROUP), blk(C_GATE // LRU_GROUP), vec(4), vec(1), wbd, wbd, vec(1), vec(1), vec(1)],
        out_specs=[blk(0)] * 4,
        out_shape=[jax.ShapeDtypeStruct((S, D_RNN), F32), jax.ShapeDtypeStruct((S, D_RNN), MXU_DTYPE),
                   jax.ShapeDtypeStruct((S, D_RNN), F32), jax.ShapeDtypeStruct((S, D_RNN), F32)],
        scratch_shapes=[pltpu.VMEM((8, LRU_GROUP), F32), pltpu.VMEM((1, LRU_GROUP), F32), pltpu.VMEM((T, LRU_GROUP), F32)],
        compiler_params=_cparams(("parallel", "arbitrary")),
    )(proj, proj, conv_w, conv_b, wa_bd, wx_bd, b_a, b_x, lam)


def _lru_bwd(proj, hl, a_fwd, mult_fwd, dz, conv_w, conv_b, wa_bd, wx_bd, b_a, b_x, lam, *, S):
    T = LRU_T
    n_t = S // T
    blk, prev8s, vec, wbd = _lru_specs(n_t, True)

    def body(xl_ref, xlp_ref, gate_ref, hl_ref, hlp_ref, a_ref, m_ref, dz_ref, cw_ref, cb_ref, wa_ref, wx_ref, ba_ref,
             bx_ref, lam_ref, dxl_ref, dgate_ref, dcw_ref, dcb_ref, dwa_ref, dwx_ref, dba_ref, dbx_ref, dlam_ref,
             next8, gcar, c_s, b_s, l_s):
        t = pl.program_id(1)
        first_chunk = t == n_t - 1

        @pl.when(t == 0)
        def _():
            next8[...] = jnp.zeros_like(next8)
            gcar[...] = jnp.zeros_like(gcar)
            for ref in (dcw_ref, dcb_ref, dwa_ref, dwx_ref, dba_ref, dbx_ref, dlam_ref):
                ref[...] = jnp.zeros_like(ref)

        row8 = lax.broadcasted_iota(jnp.int32, (8, LRU_GROUP), 0)
        rowT = lax.broadcasted_iota(jnp.int32, (T, LRU_GROUP), 0)
        keep = jnp.where(first_chunk, 0.0, 1.0)
        xl = xl_ref[...]
        wa, wx, lam_v = wa_ref[0], wx_ref[0], lam_ref[...]
        q = _lru_gates(xl, xlp_ref[...] * keep, cw_ref[...], cb_ref[...], wa, wx, ba_ref[...], bx_ref[...], lam_v, row8,
                       a_mult=(a_ref[...], m_ref[...]))
        a, mult, r, i, xc, sp = q["a"], q["mult"], q["r"], q["i"], q["xc"], q["sp"]
        hl_v = hl_ref[...]
        dz_v = dz_ref[...]
        gate = gate_ref[...]
        ge, th = _gelu(gate)
        dgate_ref[...] = (dz_v * hl_v * _gelu_grad(gate, th)).astype(dgate_ref.dtype)

        c_s[...] = jnp.where(rowT == T - 1, 0.0, pltpu.roll(a, T - 1, 0))
        b_s[...] = dz_v * ge + jnp.where(rowT == T - 1, gcar[...], 0.0)

        def step(n, carry):
            local = []
            for u in range(SCAN_GROUPS):
                off = pl.multiple_of((T // 8 - 1 - (n * SCAN_GROUPS + u)) * 8, 8)
                C = c_s[pl.ds(off, 8), :]
                B = b_s[pl.ds(off, 8), :]
                for k in (1, 2, 4):
                    c_sh = jnp.where(row8 < 8 - k, pltpu.roll(C, 8 - k, 0), 1.0)
                    b_sh = jnp.where(row8 < 8 - k, pltpu.roll(B, 8 - k, 0), 0.0)
                    B = B + C * b_sh
                    C = C * c_sh
                local.append((off, C, B))
            for off, C, B in local:
                lam_t = B + C * carry
                l_s[pl.ds(off, 8), :] = lam_t
                carry = lam_t[0:1, :]
            return carry

        lax.fori_loop(0, T // (8 * SCAN_GROUPS), step, jnp.zeros((1, LRU_GROUP), F32))
        lmb = l_s[...]
        gcar[...] = a[0:1, :] * lmb[0:1, :]

        h_prev = _shift_down(hl_v, hlp_ref[...] * keep, 1, row8)
        da = lmb * h_prev
        dmult = lmb * i * xc
        di = lmb * mult * xc
        dxc = lmb * mult * i
        dla = da * a - dmult * (a * a) / mult
        dr = dla * (-LRU_C * sp)
        dlam_ref[...] += _colsum(dla * (-LRU_C * r)) * (-jax.nn.sigmoid(-lam_v))
        dpa = dr * r * (1.0 - r)
        dpx = di * i * (1.0 - i)
        dxc = dxc + _dot(dpa, wa, "nt") + _dot(dpx, wx, "nt")
        dwa_ref[0] += _dot(xc, dpa, "tn")
        dwx_ref[0] += _dot(xc, dpx, "tn")
        dba_ref[...] += _colsum(dpa)
        dbx_ref[...] += _colsum(dpx)
        dcb_ref[...] += _colsum(dxc)
        cw = cw_ref[...]
        n8 = next8[...]
        dxl = cw[3:4] * dxc
        for k in (1, 2, 3):
            dxl = dxl + cw[3 - k:4 - k] * _shift_up(dxc, n8, k, row8)
        for k in range(4):
            dcw_ref[3 - k:4 - k, :] += _colsum(dxc * q["sh"][k])
        next8[...] = dxc[0:8]
        dxl_ref[...] = dxl.astype(dxl_ref.dtype)

    res = pl.pallas_call(
        body, name="lru_bwd", grid=(N_LRU_GROUPS, n_t),
        in_specs=[blk(C_XL // LRU_GROUP), prev8s(C_XL // LRU_GROUP), blk(C_GATE // LRU_GROUP), blk(0), prev8s(0), blk(0),
                  blk(0), blk(0), vec(4), vec(1), wbd, wbd, vec(1), vec(1), vec(1)],
        out_specs=[blk(0), blk(0), vec(4), vec(1), wbd, wbd, vec(1), vec(1), vec(1)],
        out_shape=[jax.ShapeDtypeStruct((S, D_RNN), MXU_DTYPE), jax.ShapeDtypeStruct((S, D_RNN), MXU_DTYPE),
                   jax.ShapeDtypeStruct((4, D_RNN), F32), jax.ShapeDtypeStruct((1, D_RNN), F32),
                   jax.ShapeDtypeStruct((N_LRU_GROUPS, LRU_GROUP, LRU_GROUP), F32),
                   jax.ShapeDtypeStruct((N_LRU_GROUPS, LRU_GROUP, LRU_GROUP), F32),
                   jax.ShapeDtypeStruct((1, D_RNN), F32), jax.ShapeDtypeStruct((1, D_RNN), F32),
                   jax.ShapeDtypeStruct((1, D_RNN), F32)],
        scratch_shapes=[pltpu.VMEM((8, LRU_GROUP), F32), pltpu.VMEM((1, LRU_GROUP), F32),
                        pltpu.VMEM((T, LRU_GROUP), F32), pltpu.VMEM((T, LRU_GROUP), F32), pltpu.VMEM((T, LRU_GROUP), F32)],
        compiler_params=_cparams(("parallel", "arbitrary")),
    )(proj, proj, proj, hl, hl, a_fwd, mult_fwd, dz, conv_w, conv_b, wa_bd, wx_bd, b_a, b_x, lam)
    return res


def _block_diag(w):
    w4 = w.reshape(N_LRU_GROUPS, 4, LRU_BLOCK, 1, LRU_BLOCK)
    eye = jnp.eye(4, dtype=w.dtype).reshape(1, 4, 1, 4, 1)
    return (w4 * eye).reshape(N_LRU_GROUPS, LRU_GROUP, LRU_GROUP)


def _block_diag_extract(wbd):
    w5 = wbd.reshape(N_LRU_GROUPS, 4, LRU_BLOCK, 4, LRU_BLOCK)
    return jnp.stack([w5[:, a, :, a, :] for a in range(4)], axis=1).reshape(N_LRU_BLOCKS, LRU_BLOCK, LRU_BLOCK)


def _t5_bucket(dist):
    max_exact = NUM_BUCKETS // 2
    df = jnp.maximum(dist, 1).astype(jnp.float32)
    large = max_exact + (jnp.log(df / max_exact) / math.log(MAX_DISTANCE / max_exact)
                         * (NUM_BUCKETS - max_exact)).astype(jnp.int32)
    large = jnp.minimum(large, NUM_BUCKETS - 1)
    return jnp.where(dist < max_exact, dist, large)


def _band_offsets():
    qi = jnp.arange(SPAN)[:, None]
    kj = jnp.arange(2 * SPAN)[None, :]
    return qi + SPAN - kj


def _dil_buckets():
    off = _band_offsets()
    return jnp.stack([_t5_bucket(jnp.maximum(off, 0) * dil) for _, dil in DIL_GROUPS]).astype(jnp.int32)


def _dil_bias(rel_bias, buckets):
    def body(tbl_ref, bk_ref, o_ref):
        g = pl.program_id(0)
        qi = lax.broadcasted_iota(jnp.int32, (SPAN, 2 * SPAN), 0)
        kj = lax.broadcasted_iota(jnp.int32, (SPAN, 2 * SPAN), 1)
        off = qi + SPAN - kj
        valid = (off >= 0) & (off <= SPAN)
        bk = bk_ref[0]
        for h in range(DIL_HEADS):
            acc = jnp.zeros((SPAN, 2 * SPAN), F32)
            for b in range(NUM_BUCKETS):
                acc = jnp.where(bk == b, tbl_ref[b, g * DIL_HEADS + h], acc)
            o_ref[0, h] = jnp.where(valid, acc, NEG)

    return pl.pallas_call(
        body, name="dil_bias", grid=(3,),
        in_specs=[pl.BlockSpec(memory_space=pltpu.SMEM), pl.BlockSpec((1, SPAN, 2 * SPAN), lambda g: (g, 0, 0))],
        out_specs=pl.BlockSpec((1, DIL_HEADS, SPAN, 2 * SPAN), lambda g: (g, 0, 0, 0)),
        out_shape=jax.ShapeDtypeStruct((3, DIL_HEADS, SPAN, 2 * SPAN), F32),
        compiler_params=_cparams(("parallel",)),
    )(rel_bias, buckets)


def _dil_bias_bwd(dbias, buckets):
    def body(db_ref, bk_ref, o_ref):
        lane = lax.broadcasted_iota(jnp.int32, (1, 128), 1)
        rows = [jnp.zeros((1, 128), F32) for _ in range(NUM_BUCKETS)]
        for g in range(3):
            bk = bk_ref[g]
            for h in range(DIL_HEADS):
                d = db_ref[g, h]
                for b in range(NUM_BUCKETS):
                    tot = jnp.sum(_colsum(jnp.where(bk == b, d, 0.0)), axis=1, keepdims=True)
                    rows[b] = jnp.where(lane == g * DIL_HEADS + h, tot, rows[b])
        for b in range(NUM_BUCKETS):
            o_ref[b:b + 1, :] = rows[b]

    return pl.pallas_call(
        body, name="dil_bias_bwd",
        out_shape=jax.ShapeDtypeStruct((NUM_BUCKETS, 128), F32),
        compiler_params=_cparams(),
    )(dbias, buckets)


DIL_SUBBLOCKS = (8, 4, 1)


def _dil_layout(g, S):
    dil, m = DIL_GROUPS[g][1], DIL_SUBBLOCKS[g]
    sub = SPAN * dil
    col = [(C_QKV + t * 768 + g * 256) // 128 for t in range(3)]
    return dil, m, sub, S // (sub * m), col


def _residue_rows(b, r, dil):
    return pl.ds(b * SPAN * dil + r, SPAN, stride=dil) if dil > 1 else pl.ds(b * SPAN, SPAN)


def _for_residues(dil, fn):
    if dil <= 4:
        for r in range(dil):
            fn(r)
    else:
        lax.fori_loop(0, dil, lambda r, c: (fn(r), c)[1], 0, unroll=4)


def _pair_scores(qm, k2, bias, first_cols):
    s = _dot(qm, k2, "nt") * (DIL_HEAD_DIM ** -0.5) + bias
    kj = lax.broadcasted_iota(jnp.int32, s.shape, 1)
    return jnp.where(kj < first_cols, NEG, s)


def _dilated_fwd(proj, bias, g, *, S):
    dil, m, sub, nc, (qc, kc, vc) = _dil_layout(g, S)
    R = sub * m
    cur = lambda cb: pl.BlockSpec((R, 128), lambda p, i: (i, cb + p))
    prv = lambda cb: pl.BlockSpec((sub, 128), lambda p, i: (jnp.maximum(i * m - 1, 0), cb + p))
    out = pl.BlockSpec((R, 128), lambda p, i: (i, p))

    def body(q_ref, kp_ref, kc_ref, vp_ref, vc_ref, b_ref, o_ref, lse_ref):
        lane = lax.broadcasted_iota(jnp.int32, (SPAN, 128), 1)
        sels = (lane < DIL_HEAD_DIM, lane >= DIL_HEAD_DIM)
        for b in range(m):
            first_cols = jnp.where(pl.program_id(1) == 0, SPAN, 0) if b == 0 else 0

            def one(r, b=b, first_cols=first_cols):
                rows = _residue_rows(b, r, dil)
                before = (kc_ref, vc_ref, _residue_rows(b - 1, r, dil)) if b else (kp_ref, vp_ref, _residue_rows(0, r, dil))
                q2 = q_ref[rows, :]
                k2 = _mx(jnp.concatenate([before[0][before[2], :], kc_ref[rows, :]], axis=0))
                v2 = _mx(jnp.concatenate([before[1][before[2], :], vc_ref[rows, :]], axis=0))
                qq = jnp.concatenate([jnp.where(sels[0], q2, 0.0), jnp.where(sels[1], q2, 0.0)], axis=0)
                s = _pair_scores(qq, k2, b_ref[0, 0], first_cols)
                mx = jnp.max(s, axis=-1, keepdims=True)
                p = jnp.exp(s - mx)
                den = jnp.sum(p, axis=-1, keepdims=True)
                o = _dot(p, v2) / den
                st = mx + jnp.log(den)
                o_ref[rows, :] = jnp.where(sels[0], o[0:SPAN], o[SPAN:2 * SPAN])
                lse_ref[rows, :] = jnp.where(lane == 0, st[0:SPAN], jnp.where(lane == 1, st[SPAN:2 * SPAN], 0.0))

            _for_residues(dil, one)

    return pl.pallas_call(
        body, name=f"dil_fwd{g}", grid=(2, nc),
        in_specs=[cur(qc), prv(kc), cur(kc), prv(vc), cur(vc),
                  pl.BlockSpec((1, 1, 2 * SPAN, 2 * SPAN), lambda p, i: (g, p, 0, 0))],
        out_specs=[out, out],
        out_shape=[jax.ShapeDtypeStruct((S, 256), F32), jax.ShapeDtypeStruct((S, 256), F32)],
        compiler_params=_cparams(("parallel", "parallel")),
    )(proj, proj, proj, proj, proj, bias.reshape(3, 2, 2 * SPAN, 2 * SPAN))


def _dilated_bwd(proj, do, lse, delta, bias, g, *, S, into=None):
    dil, m, sub, nc, (qc, kc, vc) = _dil_layout(g, S)
    R = sub * m
    cl = lambda i: jnp.minimum(i, nc - 1)
    cur = lambda cb: pl.BlockSpec((R, 128), lambda p, i: (cl(i), cb + p))
    prv = lambda cb: pl.BlockSpec((sub, 128), lambda p, i: (jnp.maximum(cl(i) * m - 1, 0), cb + p))
    q_out = pl.BlockSpec((R, 128), lambda p, i: (cl(i), 2 * g + p))
    kv_out = pl.BlockSpec((R, 128), lambda p, i: (jnp.maximum(i - 1, 0), 2 * g + p))
    scale = DIL_HEAD_DIM ** -0.5
    n_into = 0 if into is None else 3

    def body(q_ref, kp_ref, kc_ref, vp_ref, vc_ref, do_ref, lse_ref, dl_ref, b_ref, *rest):
        dq_ref, dk_ref, dv_ref, db_ref, dq_s, kc_s, vc_s, kp_s, vp_s, kcar, vcar = rest[n_into:]
        i = pl.program_id(1)

        @pl.when(i == 0)
        def _():
            kcar[...] = jnp.zeros_like(kcar)
            vcar[...] = jnp.zeros_like(vcar)
            db_ref[...] = jnp.zeros_like(db_ref)

        @pl.when(i < nc)
        def _():
            lane = lax.broadcasted_iota(jnp.int32, (SPAN, 128), 1)
            sels = (lane < DIL_HEAD_DIM, lane >= DIL_HEAD_DIM)
            for b in range(m):
                first_cols = jnp.where(i == 0, SPAN, 0) if b == 0 else 0

                def one(r, b=b, first_cols=first_cols):
                    rows = _residue_rows(b, r, dil)
                    rows_before = _residue_rows(b - 1 if b else 0, r, dil)
                    k_before, v_before = (kc_ref, vc_ref) if b else (kp_ref, vp_ref)
                    q2, do2 = q_ref[rows, :], do_ref[rows, :]
                    k2 = _mx(jnp.concatenate([k_before[rows_before, :], kc_ref[rows, :]], axis=0))
                    v2 = _mx(jnp.concatenate([v_before[rows_before, :], vc_ref[rows, :]], axis=0))
                    lse_t, dl_t = lse_ref[rows, :], dl_ref[rows, :]
                    qq = _mx(jnp.concatenate([jnp.where(sels[0], q2, 0.0), jnp.where(sels[1], q2, 0.0)], axis=0))
                    dd = _mx(jnp.concatenate([jnp.where(sels[0], do2, 0.0), jnp.where(sels[1], do2, 0.0)], axis=0))
                    lse2 = jnp.concatenate([lse_t[:, 0:1], lse_t[:, 1:2]], axis=0)
                    dl2 = jnp.concatenate([dl_t[:, 0:1], dl_t[:, 1:2]], axis=0)
                    p = jnp.exp(_pair_scores(qq, k2, b_ref[0, 0], first_cols) - lse2)
                    ds = p * (_dot(dd, v2, "nt") - dl2)
                    db_ref[0] += ds
                    dqq = _dot(ds, k2) * scale
                    dq2 = jnp.where(sels[0], dqq[0:SPAN], dqq[SPAN:2 * SPAN])
                    dk2 = _dot(ds, qq, "tn") * scale
                    dv2 = _dot(p, dd, "tn")
                    dq_s[rows, :] = dq2
                    kc_s[rows, :] = dk2[SPAN:2 * SPAN]
                    vc_s[rows, :] = dv2[SPAN:2 * SPAN]
                    if b:
                        kc_s[rows_before, :] += dk2[0:SPAN]
                        vc_s[rows_before, :] += dv2[0:SPAN]
                    else:
                        kp_s[rows_before, :] = dk2[0:SPAN]
                        vp_s[rows_before, :] = dv2[0:SPAN]

                _for_residues(dil, one)
            dq_ref[...] = dq_s[...].astype(dq_ref.dtype)
            last = pl.ds((m - 1) * sub, sub)
            kcar[last, :] += kp_s[...]
            vcar[last, :] += vp_s[...]
            dk_ref[...] = kcar[...].astype(dk_ref.dtype)
            dv_ref[...] = vcar[...].astype(dv_ref.dtype)
            kcar[...] = kc_s[...]
            vcar[...] = vc_s[...]

        @pl.when(i == nc)
        def _():
            dk_ref[...] = kcar[...].astype(dk_ref.dtype)
            dv_ref[...] = vcar[...].astype(dv_ref.dtype)

    stat = pl.BlockSpec((R, 128), lambda p, i: (cl(i), p))
    big = jax.ShapeDtypeStruct((S, len(DIL_GROUPS) * 256), MXU_DTYPE)
    return pl.pallas_call(
        body, name=f"dil_bwd{g}", grid=(2, nc + 1),
        in_specs=[cur(qc), prv(kc), cur(kc), prv(vc), cur(vc), stat, stat, stat,
                  pl.BlockSpec((1, 1, 2 * SPAN, 2 * SPAN), lambda p, i: (g, p, 0, 0))]
        + [pl.BlockSpec(memory_space=pl.ANY)] * n_into,
        out_specs=[q_out, kv_out, kv_out, pl.BlockSpec((1, 2 * SPAN, 2 * SPAN), lambda p, i: (p, 0, 0))],
        out_shape=[big, big, big, jax.ShapeDtypeStruct((2, 2 * SPAN, 2 * SPAN), F32)],
        input_output_aliases={9 + j: j for j in range(n_into)},
        scratch_shapes=[pltpu.VMEM((R, 128), F32)] * 3 + [pltpu.VMEM((sub, 128), F32)] * 2 + [pltpu.VMEM((R, 128), F32)] * 2,
        compiler_params=_cparams(("parallel", "arbitrary")),
    )(proj, proj, proj, proj, proj, do, lse, delta, bias.reshape(3, 2, 2 * SPAN, 2 * SPAN), *(into or ()))


def _dilated_merge(os_, lses, *, S, bt=512):
    tile = pl.BlockSpec((bt, 128), lambda i, p: (i, p))

    def body(o0, o1, o2, l0, l1, l2, o_ref, om_ref, lse_ref):
        lane = lax.broadcasted_iota(jnp.int32, (bt, 128), 1)
        lo = lane < DIL_HEAD_DIM
        ls = [l0[...], l1[...], l2[...]]
        ws, stat = [], jnp.zeros((bt, 128), F32)
        for e in range(2):
            a = [l[:, e:e + 1] for l in ls]
            m = jnp.maximum(jnp.maximum(a[0], a[1]), a[2])
            ex = [jnp.exp(v - m) for v in a]
            tot = ex[0] + ex[1] + ex[2]
            ws.append([v / tot for v in ex])
            stat = jnp.where(lane == e, m + jnp.log(tot), stat)
        acc = jnp.zeros((bt, 128), F32)
        for gi, o in enumerate((o0, o1, o2)):
            acc = acc + jnp.where(lo, ws[0][gi], ws[1][gi]) * o[...]
        o_ref[...] = acc
        om_ref[...] = _mx(acc)
        lse_ref[...] = stat

    return pl.pallas_call(
        body, name="dil_merge", grid=(S // bt, 2),
        in_specs=[tile] * 6, out_specs=[tile, tile, tile],
        out_shape=[jax.ShapeDtypeStruct((S, 256), F32), jax.ShapeDtypeStruct((S, 256), MXU_DTYPE),
                   jax.ShapeDtypeStruct((S, 256), F32)],
        compiler_params=_cparams(("parallel", "parallel")),
    )(*os_, *lses)


def _with_delta(do, o):
    lane = lax.broadcasted_iota(jnp.int32, (do.shape[0], 128), 1)
    stats = []
    for p in range(2):
        prod = do[:, 128 * p:128 * (p + 1)] * o[:, 128 * p:128 * (p + 1)]
        d0 = jnp.sum(jnp.where(lane < DIL_HEAD_DIM, prod, 0.0), axis=-1, keepdims=True)
        d1 = jnp.sum(jnp.where(lane >= DIL_HEAD_DIM, prod, 0.0), axis=-1, keepdims=True)
        stats.append(jnp.where(lane == 0, d0, jnp.where(lane == 1, d1, 0.0)))
    return do, jnp.concatenate(stats, axis=1)


MEM_T = 2048
QM_BLK = C_QM // MEM_HEAD_DIM


def _mem_attn_fwd(proj, kv, *, S):
    scale = MEM_HEAD_DIM ** -0.5

    def body(q_ref, k_ref, v_ref, o_ref, om_ref, lse_ref):
        s = _dot(q_ref[...], k_ref[...], "nt") * scale
        m = jnp.max(s, axis=-1, keepdims=True)
        p = jnp.exp(s - m)
        den = jnp.sum(p, axis=-1, keepdims=True)
        o = _dot(p, v_ref[...]) / den
        o_ref[...] = o
        om_ref[...] = _mx(o)
        lse_ref[0] = m + jnp.log(den)

    return pl.pallas_call(
        body, name="mem_attn_fwd", grid=(S // MEM_T, MEM_HEADS),
        in_specs=[pl.BlockSpec((MEM_T, MEM_HEAD_DIM), lambda i, h: (i, QM_BLK + h)),
                  pl.BlockSpec((N_MEM, MEM_HEAD_DIM), lambda i, h: (0, h)),
                  pl.BlockSpec((N_MEM, MEM_HEAD_DIM), lambda i, h: (0, MEM_HEADS + h))],
        out_specs=[pl.BlockSpec((MEM_T, MEM_HEAD_DIM), lambda i, h: (i, h)),
                   pl.BlockSpec((MEM_T, MEM_HEAD_DIM), lambda i, h: (i, h)),
                   pl.BlockSpec((1, MEM_T, 1), lambda i, h: (h, i, 0))],
        out_shape=[jax.ShapeDtypeStruct((S, MEM_WIDTH), F32), jax.ShapeDtypeStruct((S, MEM_WIDTH), MXU_DTYPE),
                   jax.ShapeDtypeStruct((MEM_HEADS, S, 1), F32)],
        compiler_params=_cparams(("parallel", "parallel")),
    )(proj, kv, kv)


def _mem_attn_bwd(proj, kv, om, lse, dom, *, S):
    scale = MEM_HEAD_DIM ** -0.5

    def body(q_ref, k_ref, v_ref, o_ref, lse_ref, do_ref, dq_ref, dk_ref, dv_ref):
        @pl.when(pl.program_id(1) == 0)
        def _():
            dk_ref[...] = jnp.zeros_like(dk_ref)
            dv_ref[...] = jnp.zeros_like(dv_ref)

        qv, kv_, vv, dov = q_ref[...], k_ref[...], v_ref[...], do_ref[...]
        p = jnp.exp(_dot(qv, kv_, "nt") * scale - lse_ref[0])
        delta = jnp.sum(dov * o_ref[...], axis=-1, keepdims=True)
        ds = p * (_dot(dov, vv, "nt") - delta)
        dq_ref[...] = (_dot(ds, kv_) * scale).astype(dq_ref.dtype)
        dk_ref[...] += _dot(ds, qv, "tn") * scale
        dv_ref[...] += _dot(p, dov, "tn")

    tile = pl.BlockSpec((MEM_T, MEM_HEAD_DIM), lambda h, i: (i, h))
    kvo = pl.BlockSpec((N_MEM, MEM_HEAD_DIM), lambda h, i: (0, h))
    return pl.pallas_call(
        body, name="mem_attn_bwd", grid=(MEM_HEADS, S // MEM_T),
        in_specs=[pl.BlockSpec((MEM_T, MEM_HEAD_DIM), lambda h, i: (i, QM_BLK + h)),
                  pl.BlockSpec((N_MEM, MEM_HEAD_DIM), lambda h, i: (0, h)),
                  pl.BlockSpec((N_MEM, MEM_HEAD_DIM), lambda h, i: (0, MEM_HEADS + h)),
                  tile, pl.BlockSpec((1, MEM_T, 1), lambda h, i: (h, i, 0)), tile],
        out_specs=[tile, kvo, kvo],
        out_shape=[jax.ShapeDtypeStruct((S, MEM_WIDTH), MXU_DTYPE), jax.ShapeDtypeStruct((N_MEM, MEM_WIDTH), F32),
                   jax.ShapeDtypeStruct((N_MEM, MEM_WIDTH), F32)],
        compiler_params=_cparams(("parallel", "arbitrary")),
    )(proj, kv, kv, om, lse, dom)


MIX_BM = 1024
MIX_BN = 256
GATES_BLK = C_GATES // MIX_BN


def _mix_specs(j_outer):
    ix = (lambda f: (lambda j, i: f(i, j))) if j_outer else (lambda f: f)
    act = lambda width: pl.BlockSpec((MIX_BM, width), ix(lambda i, j: (i, 0)))
    wgt = lambda width: pl.BlockSpec((width, MIX_BN), ix(lambda i, j: (0, j)))
    gate = lambda b: pl.BlockSpec((MIX_BM, MIX_BN), ix(lambda i, j: (i, GATES_BLK + 4 * b + j)))
    bias = lambda b: pl.BlockSpec((1, MIX_BN), ix(lambda i, j: (0, 4 * b + j)))
    tile = pl.BlockSpec((MIX_BM, MIX_BN), ix(lambda i, j: (i, j)))
    return act, wgt, gate, bias, tile


def _mix_fwd(z_lru, o_dil, om, w_lru, w_dil, w_mem, proj, b_gate, *, S):
    act, wgt, gate, bias, tile = _mix_specs(False)

    def body(zl, od, mo, wl, wd, wm, g0, g1, g2, b0, b1, b2, o_ref):
        acc = jax.nn.sigmoid(g0[...] + b0[...]) * _dot(zl[...], wl[...])
        acc += jax.nn.sigmoid(g1[...] + b1[...]) * _dot(od[...], wd[...])
        acc += jax.nn.sigmoid(g2[...] + b2[...]) * _dot(mo[...], wm[...])
        o_ref[...] = acc.astype(o_ref.dtype)

    return pl.pallas_call(
        body, name="mix_fwd", grid=(S // MIX_BM, D_MODEL // MIX_BN),
        in_specs=[act(D_RNN), act(256), act(MEM_WIDTH), wgt(D_RNN), wgt(256), wgt(MEM_WIDTH),
                  gate(0), gate(1), gate(2), bias(0), bias(1), bias(2)],
        out_specs=tile, out_shape=jax.ShapeDtypeStruct((S, D_MODEL), MXU_DTYPE),
        compiler_params=_cparams(("parallel", "parallel")),
    )(z_lru, o_dil, om, w_lru, w_dil, w_mem, proj, proj, proj, b_gate, b_gate, b_gate)


def _mix_bwd(dmerged, z_lru, o_dil, om, w_lru, w_dil, w_mem, proj, b_gate, *, S):
    act, wgt, gate, bias, tile = _mix_specs(True)

    def body(dm, zl, od, mo, wl, wd, wm, g0, g1, g2, b0, b1, b2,
             dg0, dg1, dg2, dy0, dy1, dy2, db0, db1, db2):
        @pl.when(pl.program_id(1) == 0)
        def _():
            for r in (db0, db1, db2):
                r[...] = jnp.zeros_like(r)

        dmv = dm[...]
        for act_ref, w_ref, g_ref, b_ref, dg_ref, dy_ref, db_ref in (
                (zl, wl, g0, b0, dg0, dy0, db0), (od, wd, g1, b1, dg1, dy1, db1), (mo, wm, g2, b2, dg2, dy2, db2)):
            y = _dot(act_ref[...], w_ref[...])
            gt = jax.nn.sigmoid(g_ref[...] + b_ref[...])
            dgate = dmv * y * gt * (1.0 - gt)
            dg_ref[...] = dgate.astype(dg_ref.dtype)
            dy_ref[...] = (dmv * gt).astype(dy_ref.dtype)
            db_ref[...] += _colsum(dgate)

    big = jax.ShapeDtypeStruct((S, D_MODEL), MXU_DTYPE)
    vec = jax.ShapeDtypeStruct((1, D_MODEL), F32)
    vspec = pl.BlockSpec((1, MIX_BN), lambda j, i: (0, j))
    return pl.pallas_call(
        body, name="mix_bwd", grid=(D_MODEL // MIX_BN, S // MIX_BM),
        in_specs=[tile, act(D_RNN), act(256), act(MEM_WIDTH), wgt(D_RNN), wgt(256), wgt(MEM_WIDTH),
                  gate(0), gate(1), gate(2), bias(0), bias(1), bias(2)],
        out_specs=[tile] * 6 + [vspec] * 3, out_shape=[big] * 6 + [vec] * 3,
        compiler_params=_cparams(("parallel", "arbitrary")),
    )(dmerged, z_lru, o_dil, om, w_lru, w_dil, w_mem, proj, proj, proj, b_gate, b_gate, b_gate)


def _adamw_math(w, g, m, v):
    m = ADAM_B1 * m + (1.0 - ADAM_B1) * g
    v = ADAM_B2 * v + (1.0 - ADAM_B2) * (g * g)
    m_hat = m / (1.0 - ADAM_B1 ** ADAM_STEP)
    v_hat = v / (1.0 - ADAM_B2 ** ADAM_STEP)
    delta = -ADAM_LR * (m_hat / (jnp.sqrt(v_hat) + ADAM_EPS) + ADAM_WD * w)
    return delta, m, v


def _adamw_landed(w, own, land, m, v, *, name, col_blk=0, prev=None):
    R = w.shape[0]
    n_parts, C = land.shape[0], land.shape[2]
    br = next(d for d in (256, 464, 128) if R % d == 0)
    tile = pl.BlockSpec((br, C), lambda i: (i, col_blk))
    part = pl.BlockSpec((br, C), lambda i: (i, 0))
    n_prev = 0 if prev is None else 4

    def body(w_ref, o_ref, l_ref, m_ref, v_ref, *rest):
        g_ref, d_ref, nm_ref, nv_ref = rest[n_prev:]
        g = o_ref[...].astype(F32)
        for p in range(n_parts):
            g = g + l_ref[p].astype(F32)
        d, nm, nv = _adamw_math(w_ref[...], g, m_ref[...], v_ref[...])
        g_ref[...] = g
        d_ref[...] = d
        nm_ref[...] = nm
        nv_ref[...] = nv

    return pl.pallas_call(
        body, name=name, grid=(R // br,),
        in_specs=[tile, part, pl.BlockSpec((n_parts, br, C), lambda i: (0, i, 0)), tile, tile]
        + [pl.BlockSpec(memory_space=pl.ANY)] * n_prev,
        out_specs=[tile] * 4, out_shape=[jax.ShapeDtypeStruct(w.shape, F32)] * 4,
        input_output_aliases={5 + j: j for j in range(n_prev)},
        compiler_params=_cparams(("parallel",)),
    )(w, own, land, m, v, *(prev or ()))


def _adamw_plain(w, g, m, v, *, name):
    def body(w_ref, g_ref, m_ref, v_ref, d_ref, nm_ref, nv_ref):
        d, nm, nv = _adamw_math(w_ref[...], g_ref[...], m_ref[...], v_ref[...])
        d_ref[...] = d
        nm_ref[...] = nm
        nv_ref[...] = nv

    return pl.pallas_call(
        body, name=name, out_shape=[jax.ShapeDtypeStruct(w.shape, F32)] * 3, compiler_params=_cparams(),
    )(w, g, m, v)


def _my_pos():
    return lax.axis_index("x"), lax.axis_index("y"), lax.axis_index("c")


def _dev_index(p):
    return 4 * p[0] + 2 * p[1] + p[2]


def _all_gather(shards):
    n = len(shards)
    hbm = pl.BlockSpec(memory_space=pl.ANY)

    def body(*refs):
        ins, outs = refs[:n], refs[n:2 * n]
        send_sems, recv_sems, local_sems = refs[2 * n:]
        x, y, c = _my_pos()
        me, sibling = (x, y, c), (x, y, 1 - c)
        chips = [(1 - x, y), (x, 1 - y), (1 - x, 1 - y)]

        def copy(a, k, block, to, src=None):
            dst = outs[a].at[_dev_index(block)]
            return pltpu.make_async_remote_copy(
                src_ref=dst if src is None else src, dst_ref=dst,
                send_sem=send_sems.at[a, k], recv_sem=recv_sems.at[a, k], device_id=to, device_id_type=MESH)

        mine = [pltpu.make_async_copy(ins[a], outs[a].at[_dev_index(me)], local_sems.at[a]) for a in range(n)]
        for cp in mine:
            cp.start()
        first = []
        for a in range(n):
            first.append(copy(a, 0, me, sibling, src=ins[a]))
            first += [copy(a, 1 + j, me, (*chip, c), src=ins[a]) for j, chip in enumerate(chips)]
        for cp in first:
            cp.start()
        passed = []
        for j, chip in enumerate(chips):
            for a in range(n):
                copy(a, 1 + j, (*chip, c), me).wait_recv()
                fwd = copy(a, 4 + j, (*chip, c), sibling)
                fwd.start()
                passed.append(fwd)
        for a in range(n):
            copy(a, 0, sibling, me).wait_recv()
        for j, chip in enumerate(chips):
            for a in range(n):
                copy(a, 4 + j, (*chip, 1 - c), me).wait_recv()
        for cp in first + passed:
            cp.wait_send()
        for cp in mine:
            cp.wait()

    return pl.pallas_call(
        body, name="all_gather_weights",
        in_specs=[hbm] * n, out_specs=[hbm] * n,
        out_shape=[jax.ShapeDtypeStruct((N_DEV,) + s.shape, s.dtype) for s in shards],
        scratch_shapes=[pltpu.SemaphoreType.DMA((n, 7)), pltpu.SemaphoreType.DMA((n, 7)), pltpu.SemaphoreType.DMA((n,))],
        compiler_params=pltpu.CompilerParams(has_side_effects=True),
    )(*shards)


def _peers(me):
    x, y, c = me
    out = []
    for k in range(1, 8):
        fx, fy, fc = (k >> 2) & 1, (k >> 1) & 1, k & 1
        out.append((k - 1, (1 - x if fx else x, 1 - y if fy else y, 1 - c if fc else c)))
    return out


HBM_SPEC = pl.BlockSpec(memory_space=pltpu.HBM)
SEM_SPEC = pl.BlockSpec(memory_space=pltpu.SEMAPHORE)
DATAFLOW_EFFECT = pltpu.SideEffectType.DATAFLOW_SIDE_EFFECTING


def _gather_refs(src, land, me, peer, k):
    return src, land.at[_dev_index(me)]


def _scatter_refs(src, land, me, peer, k):
    return src.at[_dev_index(peer)], land.at[k]


def _push_start(srcs, land_shapes, refs_of, name, after=()):
    n, n_after = len(srcs), len(after)

    def body(*refs):
        ins, lands = refs[:n], refs[n:2 * n]
        send_sems, recv_sems, token = refs[2 * n + n_after], refs[2 * n + n_after + 1], refs[-1]
        me = _my_pos()
        for k, peer in _peers(me):
            for a in range(n):
                src, dst = refs_of(ins[a], lands[a], me, peer, k)
                pltpu.make_async_remote_copy(src_ref=src, dst_ref=dst, send_sem=send_sems.at[7 * a + k],
                                             recv_sem=recv_sems.at[7 * a + k], device_id=peer, device_id_type=MESH).start()
        token[...] = jnp.zeros_like(token)

    lands = [lax.empty(shp, s.dtype) for shp, s in zip(land_shapes, srcs)]
    hbm = lambda a: pltpu.with_memory_space_constraint(a, pltpu.HBM)
    res = pl.pallas_call(
        body, name=name,
        out_shape=(pltpu.SemaphoreType.DMA((7 * n,)), pltpu.SemaphoreType.DMA((7 * n,)),
                   *[pltpu.HBM(s.shape, s.dtype) for s in srcs], *[pltpu.HBM(l.shape, l.dtype) for l in lands],
                   jax.ShapeDtypeStruct((8, 128), F32)),
        in_specs=[HBM_SPEC] * (2 * n) + [pl.BlockSpec(memory_space=pl.ANY)] * n_after,
        out_specs=(SEM_SPEC, SEM_SPEC, *[HBM_SPEC] * (2 * n), pl.BlockSpec(memory_space=pltpu.VMEM)),
        input_output_aliases={i: 2 + i for i in range(2 * n)},
        compiler_params=pltpu.CompilerParams(has_side_effects=DATAFLOW_EFFECT),
    )(*[hbm(s) for s in srcs], *[hbm(l) for l in lands], *after)
    return dict(sems=(res[0], res[1]), srcs=list(res[2:2 + n]), lands=list(res[2 + n:2 + 2 * n]), token=res[-1], n=n,
                refs_of=refs_of, name=name)


def _push_wait(started, after):
    n, refs_of = started["n"], started["refs_of"]
    after = list(after) if isinstance(after, (list, tuple)) else [after]

    def body(*refs):
        ins, lands = refs[:n], refs[n:2 * n]
        send_sems, recv_sems = refs[2 * n], refs[2 * n + 1]
        me = _my_pos()
        for k, peer in _peers(me):
            for a in range(n):
                src, dst = refs_of(ins[a], lands[a], me, peer, k)
                cp = pltpu.make_async_remote_copy(src_ref=src, dst_ref=dst, send_sem=send_sems.at[7 * a + k],
                                                  recv_sem=recv_sems.at[7 * a + k], device_id=peer, device_id_type=MESH)
                cp.wait_send()
                cp.wait_recv()

    arrs = started["srcs"] + started["lands"]
    res = pl.pallas_call(
        body, name=started["name"].replace("start", "wait"),
        out_shape=tuple(pltpu.HBM(a.shape, a.dtype) for a in arrs),
        in_specs=[HBM_SPEC] * (2 * n) + [SEM_SPEC, SEM_SPEC] + [pl.BlockSpec(memory_space=pl.ANY)] * len(after),
        out_specs=tuple([HBM_SPEC] * (2 * n)),
        input_output_aliases={i: i for i in range(2 * n)},
        compiler_params=pltpu.CompilerParams(has_side_effects=DATAFLOW_EFFECT),
    )(*arrs, *started["sems"], *after)
    return list(res[n:2 * n])


def _sum_slots(slots):
    def body(in_ref, out_ref):
        acc = in_ref[0]
        for d in range(1, N_DEV):
            acc = acc + in_ref[d]
        out_ref[...] = acc

    return pl.pallas_call(body, name="sum_small", out_shape=jax.ShapeDtypeStruct(slots.shape[1:], F32),
                          compiler_params=_cparams())(slots)


def _adamw_many(ws, gs, ms, vs):
    n = len(ws)

    def body(*refs):
        for i in range(n):
            w_ref, g_ref, m_ref, v_ref = (refs[j * n + i] for j in range(4))
            d_, nm, nv = _adamw_math(w_ref[...], g_ref[...], m_ref[...], v_ref[...])
            for j, val in enumerate((d_, nm, nv)):
                refs[(4 + j) * n + i][...] = val

    res = pl.pallas_call(body, name="adamw_small", out_shape=[jax.ShapeDtypeStruct(w_.shape, F32) for w_ in ws] * 3,
                         compiler_params=_cparams())(*ws, *gs, *ms, *vs)
    return [(res[i], res[n + i], res[2 * n + i]) for i in range(n)]


def _local_step(x, mem, tgt, W, P, late_weights, send_grads, reduce_small, tie0):
    S = x.shape[0]
    W = dict(W)
    h = _rmsnorm_fwd(x, P["g_mix"] + tie0, rows=S, name="norm_mix")
    proj = _matmul(h, W["w_in_t"], M=S, N=D_IN, K=D_MODEL, mode="nt", bm=512, bn=D_IN // 2, bk=D_MODEL, name="mm_in",
                   j_outer=True)

    wa_bd, wx_bd = _mx(_block_diag(P["w_rg_a"])), _mx(_block_diag(P["w_rg_x"]))
    lru_args = (W["conv_w"], P["conv_b"].reshape(1, -1), wa_bd, wx_bd, P["b_rg_a"].reshape(1, -1),
                P["b_rg_x"].reshape(1, -1), P["lru_lambda"].reshape(1, -1))
    hl, z_lru, a_lru, mult_lru = _lru_fwd(proj, *lru_args, S=S)

    buckets = _dil_buckets()
    bias = _dil_bias(P["rel_bias"], buckets)
    group_out = [_dilated_fwd(proj, bias, g, S=S) for g in range(len(DIL_GROUPS))]
    o_dil, o_dil_m, lse_dil = _dilated_merge([o for o, _ in group_out], [l for _, l in group_out], S=S)

    W.update(late_weights("branch", [o_dil, z_lru]))
    mem_n = _rmsnorm_fwd(mem, P["g_mem"], rows=N_MEM, name="norm_mem")
    kv = _matmul(mem_n, W["w_mem_kv"], M=N_MEM, N=2 * MEM_WIDTH, K=D_MODEL, mode="nn", bm=N_MEM, bn=512, bk=D_MODEL,
                 name="mm_kv")
    om, om_m, lse_mem = _mem_attn_fwd(proj, kv, S=S)
    b_gate = P["b_gate"].reshape(1, -1)
    merged = _mix_fwd(z_lru, o_dil_m, om_m, W["w_lru_out"], W["w_dil_out"], W["w_mem_out"], proj, b_gate, S=S)
    g_mlp, g_final, g_mix = (P[n].reshape(1, D_MODEL) for n in ("g_mlp", "g_final", "g_mix"))
    x1, hm = _matmul_rows(merged, W["w_out"], M=S, K=D_MODEL, mode="nn", bm=512, name="mm_out",
                          row_fn=_residual_then_norm, out_dtypes=(F32, MXU_DTYPE), tiles=[x], vecs=[g_mlp])
    W.update(late_weights("mlp", [hm]))

    def relu2(acc):
        rl = jnp.maximum(acc, 0.0)
        return rl * rl, rl

    act, relu_u = _matmul(hm, W["w_mlp_in_t"], M=S, N=D_FF, K=D_MODEL, mode="nt", bm=1024, bn=1024, bk=D_MODEL,
                          name="mm_mlp_in", out_dtypes=(MXU_DTYPE, MXU_DTYPE), epilogue=relu2, j_outer=True)
    dx2, dx2_m, loss, dg_final = _matmul_rows(
        act, W["w_mlp_out"], M=S, K=D_FF, mode="nn", bm=512, name="mm_mlp_out", row_fn=_residual_then_loss,
        out_dtypes=(F32, MXU_DTYPE), tiles=[x1, tgt], vecs=[g_final], acc_widths=(1, D_MODEL))

    G, Gs = {}, {}
    Gs["g_final"] = dg_final
    dw = dict(mode="tn", K=S, bk=S, out_dtypes=(MXU_DTYPE,))
    G["w_mlp_out"] = _matmul(act, dx2_m, M=D_FF, N=D_MODEL, bm=512, bn=D_MODEL, name="mm_dw_mlp_out",
                             parts=("rows", D_FF // N_DEV), **dw)
    du = _matmul(dx2_m, W["w_mlp_out"], M=S, N=D_FF, K=D_MODEL, mode="nt", bm=1024, bn=1024, bk=D_MODEL, name="mm_du",
                 out_dtypes=(MXU_DTYPE,), epilogue=lambda acc, rl: (acc * (2.0 * rl.astype(F32)),),
                 extras=[(relu_u, (0, 0))], j_outer=True)
    G["w_mlp_in"] = _matmul(hm, du, M=D_MODEL, N=D_FF, bm=D_MODEL, bn=512, name="mm_dw_mlp_in",
                            parts=("cols", D_FF // N_DEV), **dw)
    tie1 = send_grads({n: G.pop(n) for n in ("w_mlp_out", "w_mlp_in")})
    dx1, dx1_m, Gs["g_mlp"] = _matmul_rows(
        du, W["w_mlp_in_t"], M=S, K=D_FF, mode="nn", bm=512, name="mm_dhm", row_fn=_norm_bwd_then_residual(2),
        out_dtypes=(F32, MXU_DTYPE), tiles=[x1, dx2], vecs=[g_mlp], acc_widths=(D_MODEL,), deps=[tie1])
    G["w_out"] = _matmul(merged, dx1_m, M=D_MODEL, N=D_MODEL, bm=512, bn=D_MODEL, name="mm_dw_out",
                         parts=("rows", D_MODEL // N_DEV), **dw)
    dmerged = _matmul(dx1_m, W["w_out"], M=S, N=D_MODEL, K=D_MODEL, mode="nt", bm=512, bn=D_MODEL, bk=D_MODEL, name="mm_dmerged")
    (dg0, dg1, dg2, dy_lru, dy_dil, dy_mem, db0, db1, db2) = _mix_bwd(
        dmerged, z_lru, o_dil_m, om_m, W["w_lru_out"], W["w_dil_out"], W["w_mem_out"], proj, b_gate, S=S)
    Gs["b_gate0"], Gs["b_gate1"], Gs["b_gate2"] = db0, db1, db2

    G["w_mem_out"] = _matmul(om_m, dy_mem, M=MEM_WIDTH, N=D_MODEL, bm=MEM_WIDTH, bn=D_MODEL, name="mm_dw_mem_out",
                             parts=("cols", D_MODEL // N_DEV), **dw)
    dom = _matmul(dy_mem, W["w_mem_out"], M=S, N=MEM_WIDTH, K=D_MODEL, mode="nt", bm=512, bn=MEM_WIDTH, bk=D_MODEL,
                  name="mm_dom")
    dqm, dk_mem, dv_mem = _mem_attn_bwd(proj, kv, om, lse_mem, dom, S=S)
    dkv = jnp.concatenate([dk_mem, dv_mem], axis=1)
    G["w_mem_kv"] = _matmul(mem_n, dkv, M=D_MODEL, N=2 * MEM_WIDTH, K=N_MEM, mode="tn", bm=D_MODEL, bn=2 * MEM_WIDTH,
                            bk=N_MEM, name="mm_dw_kv", out_dtypes=(MXU_DTYPE,), parts=("rows", D_MODEL // N_DEV))
    dmem_n = _matmul(dkv, W["w_mem_kv"], M=N_MEM, N=D_MODEL, K=2 * MEM_WIDTH, mode="nt", bm=N_MEM, bn=D_MODEL,
                     bk=2 * MEM_WIDTH, name="mm_dmem")
    (Gs["g_mem"],) = _rmsnorm_bwd(mem, P["g_mem"], dmem_n, None, rows=N_MEM, name="norm_mem_bwd", dx_dtypes=())

    G["w_dil_out"] = _matmul(o_dil_m, dy_dil, M=256, N=D_MODEL, bm=256, bn=D_MODEL, name="mm_dw_dil_out",
                             parts=("cols", D_MODEL // N_DEV), **dw)
    do_dil, delta = _matmul(dy_dil, W["w_dil_out"], M=S, N=256, K=D_MODEL, mode="nt", bm=512, bn=256, bk=D_MODEL,
                            name="mm_do_dil", out_dtypes=(F32, F32), epilogue=_with_delta, extras=[(o_dil, (0, 0))])
    G["w_lru_out"] = _matmul(z_lru, dy_lru, M=D_RNN, N=D_MODEL, bm=D_RNN, bn=D_MODEL, name="mm_dw_lru_out",
                             parts=("cols", D_MODEL // N_DEV), **dw)
    dz = _matmul(dy_lru, W["w_lru_out"], M=S, N=D_RNN, K=D_MODEL, mode="nt", bm=512, bn=D_RNN, bk=D_MODEL, name="mm_dz_lru")
    tie2 = send_grads({n: G.pop(n) for n in ("w_out", "w_mem_out", "w_mem_kv", "w_dil_out", "w_lru_out")})
    bias = bias + tie2[0, 0]
    dqkv, dbias = None, []
    for g in range(len(DIL_GROUPS)):
        *dqkv, db_g = _dilated_bwd(proj, do_dil, lse_dil, delta, bias, g, S=S, into=dqkv)
        dbias.append(db_g)
    drel = _dil_bias_bwd(jnp.stack(dbias, axis=0).reshape(len(DIL_GROUPS), DIL_HEADS, SPAN, 2 * SPAN), buckets)
    Gs["rel_bias"] = drel

    dxl, dgl, dcw, dcb, dwa, dwx, dba, dbx, dlam = _lru_bwd(proj, hl, a_lru, mult_lru, dz, *lru_args, S=S)
    Gs["conv_w"], Gs["conv_b"] = dcw, dcb
    Gs["w_rg_a"], Gs["w_rg_x"] = _block_diag_extract(dwa), _block_diag_extract(dwx)
    Gs["b_rg_a"], Gs["b_rg_x"], Gs["lru_lambda"] = dba, dbx, dlam
    Gs["loss"] = loss

    dproj = [dxl, dgl] + dqkv + [dqm, dg0, dg1, dg2]
    tie = []
    for q in range(W_IN_PIECES):
        dw_q = None
        for half in range(2):
            dw_q = _dw_in_t_half(h, dproj, q, half, S=S, name=f"mm_dw_in_{q}_{half}", into=dw_q, deps=tie)
        tie = [send_grads({f"w_in_{q}": dw_q})]
    grad_x, Gs["g_mix"] = _matmul_rows(
        dproj, W["w_in_t"], M=S, K=D_IN, mode="nn", bm=256, name="mm_dh", row_fn=_norm_bwd_then_residual(1),
        out_dtypes=(F32,), tiles=[x, dx1], vecs=[g_mix], acc_widths=(D_MODEL,), deps=tie)
    return grad_x, reduce_small(Gs)


BIG = ("w_in", "w_lru_out", "w_dil_out", "w_mem_kv", "w_mem_out", "w_out", "w_mlp_in", "w_mlp_out")
W_IN_PIECES = 2
COL_SHARDED = ("w_lru_out", "w_dil_out", "w_mem_out", "w_mlp_in")
GATHERED_TRANSPOSED = ("w_mlp_in",)
SMALL = ("g_mix", "b_gate", "conv_b", "w_rg_a", "b_rg_a", "w_rg_x", "b_rg_x", "lru_lambda", "rel_bias", "g_mem",
         "g_mlp", "g_final")
WEIGHTS = ("g_mix", "w_in", "b_gate", "conv_w", "conv_b", "w_rg_a", "b_rg_a", "w_rg_x", "b_rg_x", "lru_lambda",
           "w_lru_out", "rel_bias", "w_dil_out", "g_mem", "w_mem_kv", "w_mem_out", "w_out", "g_mlp", "w_mlp_in",
           "w_mlp_out", "g_final")


def _gathered_to_full(name, gathered):
    if name in COL_SHARDED:
        n, r, c = gathered.shape
        return gathered.transpose(1, 0, 2).reshape(r, n * c)
    n, r, c = gathered.shape
    return gathered.reshape(n * r, c)


SMALL_GRADS = (("g_mix", (1, 1024)), ("b_gate0", (1, 1024)), ("b_gate1", (1, 1024)), ("b_gate2", (1, 1024)),
               ("conv_b", (1, 768)), ("w_rg_a", (12, 64, 64)), ("b_rg_a", (1, 768)), ("w_rg_x", (12, 64, 64)),
               ("b_rg_x", (1, 768)), ("lru_lambda", (1, 768)), ("rel_bias", (32, 128)), ("g_mem", (1, 1024)),
               ("g_mlp", (1, 1024)), ("g_final", (1, 1024)), ("conv_w", (4, 768)), ("loss", (1, 1)))


def _pack(parts):
    flat = jnp.concatenate([p.reshape(-1) for p in parts])
    return jnp.pad(flat, (0, (-flat.shape[0]) % 1024)).reshape(-1, 128)


def _unpack(pack, shapes):
    flat = pack.reshape(-1)
    out, off = [], 0
    for shp in shapes:
        size = math.prod(shp)
        out.append(flat[off:off + size].reshape(shp))
        off += size
    return out


def kernel(x, mem, g_mix, w_in, b_gate, conv_w, conv_b, w_rg_a, b_rg_a, w_rg_x, b_rg_x, lru_lambda, w_lru_out, rel_bias, w_dil_out, g_mem, w_mem_kv, w_mem_out, w_out, g_mlp, w_mlp_in, w_mlp_out, g_final, loss_target, m_g_mix, m_w_in, m_b_gate, m_conv_w, m_conv_b, m_w_rg_a, m_b_rg_a, m_w_rg_x, m_b_rg_x, m_lru_lambda, m_w_lru_out, m_rel_bias, m_w_dil_out, m_g_mem, m_w_mem_kv, m_w_mem_out, m_w_out, m_g_mlp, m_w_mlp_in, m_w_mlp_out, m_g_final, v_g_mix, v_w_in, v_b_gate, v_conv_w, v_conv_b, v_w_rg_a, v_b_rg_a, v_w_rg_x, v_b_rg_x, v_lru_lambda, v_w_lru_out, v_rel_bias, v_w_dil_out, v_g_mem, v_w_mem_kv, v_w_mem_out, v_w_out, v_g_mlp, v_w_mlp_in, v_w_mlp_out, v_g_final):
    w = dict(g_mix=g_mix, w_in=w_in, b_gate=b_gate, conv_w=conv_w, conv_b=conv_b, w_rg_a=w_rg_a, b_rg_a=b_rg_a,
             w_rg_x=w_rg_x, b_rg_x=b_rg_x, lru_lambda=lru_lambda, w_lru_out=w_lru_out, rel_bias=rel_bias,
             w_dil_out=w_dil_out, g_mem=g_mem, w_mem_kv=w_mem_kv, w_mem_out=w_mem_out, w_out=w_out, g_mlp=g_mlp,
             w_mlp_in=w_mlp_in, w_mlp_out=w_mlp_out, g_final=g_final)
    m = dict(g_mix=m_g_mix, w_in=m_w_in, b_gate=m_b_gate, conv_w=m_conv_w, conv_b=m_conv_b, w_rg_a=m_w_rg_a,
             b_rg_a=m_b_rg_a, w_rg_x=m_w_rg_x, b_rg_x=m_b_rg_x, lru_lambda=m_lru_lambda, w_lru_out=m_w_lru_out,
             rel_bias=m_rel_bias, w_dil_out=m_w_dil_out, g_mem=m_g_mem, w_mem_kv=m_w_mem_kv, w_mem_out=m_w_mem_out,
             w_out=m_w_out, g_mlp=m_g_mlp, w_mlp_in=m_w_mlp_in, w_mlp_out=m_w_mlp_out, g_final=m_g_final)
    v = dict(g_mix=v_g_mix, w_in=v_w_in, b_gate=v_b_gate, conv_w=v_conv_w, conv_b=v_conv_b, w_rg_a=v_w_rg_a,
             b_rg_a=v_b_rg_a, w_rg_x=v_w_rg_x, b_rg_x=v_b_rg_x, lru_lambda=v_lru_lambda, w_lru_out=v_w_lru_out,
             rel_bias=v_rel_bias, w_dil_out=v_w_dil_out, g_mem=v_g_mem, w_mem_kv=v_w_mem_kv, w_mem_out=v_w_mem_out,
             w_out=v_w_out, g_mlp=v_g_mlp, w_mlp_in=v_w_mlp_in, w_mlp_out=v_w_mlp_out, g_final=v_g_final)

    my_idx = _dev_index(_my_pos())

    g_in, g_cw = _all_gather([_mx(w["w_in"].T), w["conv_w"]])
    W = {"w_in_t": g_in.reshape(D_IN, D_MODEL), "conv_w": g_cw.transpose(1, 0, 2).reshape(CONV_WIDTH, D_RNN)}
    late, order_after = {}, [g_in]
    for group, names in (("branch", ("w_mem_kv", "w_lru_out", "w_dil_out", "w_mem_out", "w_out")),
                         ("mlp", ("w_mlp_in", "w_mlp_out"))):
        shards = [_mx(w[n].T if n in GATHERED_TRANSPOSED else w[n]) for n in names]
        started = _push_start(shards, [(N_DEV,) + s.shape for s in shards], _gather_refs, f"gather_{group}_start",
                              after=order_after)
        late[group] = (names, shards, started)
        order_after = [started["token"]]
    P = {n: w[n] for n in SMALL}

    def late_weights(group, after):
        names, shards, started = late[group]
        out = {}
        for n, land, own in zip(names, _push_wait(started, after), shards):
            full = lax.dynamic_update_index_in_dim(land, own, my_idx, 0)
            if n in GATHERED_TRANSPOSED:
                out[n + "_t"] = full.reshape(-1, full.shape[2])
            else:
                out[n] = _gathered_to_full(n, full)
        return out

    sent, small = [], {}

    def send_grads(gs):
        names = list(gs)
        parts = [gs[n] for n in names]
        own = [lax.dynamic_index_in_dim(p, my_idx, 0, keepdims=False) for p in parts]
        started = _push_start(parts, [(N_DEV - 1,) + p.shape[1:] for p in parts], _scatter_refs,
                              f"scatter{len(sent)}_start")
        sent.append((names, own, started))
        return started["token"]

    def reduce_small(gs):
        small["pack"] = _pack([gs[n] for n, _ in SMALL_GRADS])
        small["started"] = _push_start([small["pack"]], [(N_DEV,) + small["pack"].shape], _gather_refs, "small_start")
        return small["started"]["token"]

    grad_x, last_token = _local_step(x[0], mem[0], loss_target[0], W, P, late_weights, send_grads, reduce_small,
                                     late["mlp"][2]["token"][0, 0])

    grads, deltas, new_m, new_v = {}, {}, {}, {}
    after = last_token
    for names, own, started in sent[:-W_IN_PIECES]:
        for n, o, land in zip(names, own, _push_wait(started, after)):
            grads[n], deltas[n], new_m[n], new_v[n] = _adamw_landed(w[n], o, land, m[n], v[n], name=f"adamw_{n}")
            after = deltas[n]
    prev = None
    for q, (names, own, started) in enumerate(sent[-W_IN_PIECES:]):
        (land,) = _push_wait(started, after)
        prev = _adamw_landed(w["w_in"].T, own[0], land, m["w_in"].T, v["w_in"].T, name=f"adamw_{names[0]}",
                             col_blk=q, prev=prev)
        after = prev[1]
    grads["w_in"], deltas["w_in"], new_m["w_in"], new_v["w_in"] = [t.T for t in prev]
    (small_land,) = _push_wait(small["started"], [after] + [deltas[n] for n in BIG if n != "w_in"])
    total = _sum_slots(lax.dynamic_update_index_in_dim(small_land, small["pack"], my_idx, 0))
    summed = dict(zip([n for n, _ in SMALL_GRADS], _unpack(total, [shp for _, shp in SMALL_GRADS])))
    summed["b_gate"] = jnp.concatenate([summed.pop(f"b_gate{b}") for b in range(3)], axis=1)
    summed["rel_bias"] = summed["rel_bias"][:, :3 * DIL_HEADS]
    for n in SMALL:
        grads[n] = summed[n].reshape(w[n].shape)
    small_updates = _adamw_many([w[n] for n in SMALL], [grads[n] for n in SMALL], [m[n] for n in SMALL],
                                [v[n] for n in SMALL])
    for n, (d_, nm_, nv_) in zip(SMALL, small_updates):
        deltas[n], new_m[n], new_v[n] = d_, nm_, nv_
    conv_w_sum, loss_sum = summed["conv_w"], summed["loss"]
    cw_cols = D_RNN // N_DEV
    grads["conv_w"] = lax.dynamic_slice(conv_w_sum, (0, my_idx * cw_cols), (CONV_WIDTH, cw_cols))
    deltas["conv_w"], new_m["conv_w"], new_v["conv_w"] = _adamw_plain(
        w["conv_w"], grads["conv_w"], m["conv_w"], v["conv_w"], name="adamw_conv_w")

    return (loss_sum.reshape(()), grad_x[None], *[grads[n] for n in WEIGHTS], *[deltas[n] for n in WEIGHTS],
            *[new_m[n] for n in WEIGHTS], *[new_v[n] for n in WEIGHTS])
```

```python
import functools
import math

import jax
import jax.numpy as jnp
from jax import lax
from jax.experimental import pallas as pl
from jax.experimental.pallas import tpu as pltpu

F32 = jnp.float32
MXU_DTYPE = jnp.bfloat16
VMEM_LIMIT_BYTES = 56 * 1024 * 1024
N_DEV = 8

D_MODEL = 1024
N_MEM = 256
MEM_HEADS = 4
MEM_HEAD_DIM = 128
MEM_WIDTH = 512
D_RNN = 768
LRU_BLOCK = 64
N_LRU_BLOCKS = 12
LRU_GROUP = 256
N_LRU_GROUPS = 3
CONV_WIDTH = 4
LRU_C = 8.0
DIL_GROUPS = ((128, 1), (512, 4), (2048, 16))
SPAN = 128
DIL_HEADS = 4
DIL_HEAD_DIM = 64
NUM_BUCKETS = 32
MAX_DISTANCE = 2048
D_FF = 4096
D_IN = 7424
EPS = 1e-6
NEG = -1e30
C_XL, C_GATE, C_QKV, C_QM, C_GATES = 0, 768, 1536, 3840, 4352

ADAM_LR = 0.001
ADAM_B1 = 0.9
ADAM_B2 = 0.999
ADAM_EPS = 1e-08
ADAM_WD = 0.01
ADAM_STEP = 10

MESH = pl.DeviceIdType.MESH
GELU_K = math.sqrt(2.0 / math.pi)


def _cparams(sem=None):
    kw = dict(vmem_limit_bytes=VMEM_LIMIT_BYTES)
    if sem is not None:
        kw["dimension_semantics"] = sem
    return pltpu.CompilerParams(**kw)


def _mx(v):
    return v.astype(MXU_DTYPE)


def _dot(a, b, mode="nn"):
    dims = {"nn": (((1,), (0,)), ((), ())), "nt": (((1,), (1,)), ((), ())), "tn": (((0,), (0,)), ((), ()))}[mode]
    return lax.dot_general(_mx(a), _mx(b), dims, preferred_element_type=F32)


def _colsum(v):
    return jnp.sum(v, axis=0, keepdims=True)


def _matmul(a, b, *, M, N, K, mode, bm, bn, bk, name, out_dtypes=(F32,), epilogue=None, extras=(),
            a_off=(0, 0), b_off=(0, 0), j_outer=False, deps=(), parts=None):
    assert M % bm == 0 and N % bn == 0 and K % bk == 0, (name, M, N, K, bm, bn, bk)
    nm, nn, nk = M // bm, N // bn, K // bk

    def ij(f):
        if j_outer:
            return lambda j, i, k: f(i, j, k)
        return f

    if mode == "tn":
        a_spec = pl.BlockSpec((bk, bm), ij(lambda i, j, k: (k + a_off[0], i + a_off[1])))
    else:
        a_spec = pl.BlockSpec((bm, bk), ij(lambda i, j, k: (i + a_off[0], k + a_off[1])))
    if mode == "nt":
        b_spec = pl.BlockSpec((bn, bk), ij(lambda i, j, k: (j + b_off[0], k + b_off[1])))
    else:
        b_spec = pl.BlockSpec((bk, bn), ij(lambda i, j, k: (k + b_off[0], j + b_off[1])))
    ex_specs = [pl.BlockSpec((bm, bn), ij(functools.partial(lambda i, j, k, o: (i + o[0], j + o[1]), o=off)))
                for _, off in extras]
    if parts is None:
        out_dims = (M, N)
        out_spec = pl.BlockSpec((bm, bn), ij(lambda i, j, k: (i, j)))
    elif parts[0] == "rows":
        r = parts[1]
        assert bm % r == 0
        out_dims = (M // r, r, N)
        out_spec = pl.BlockSpec((bm // r, r, bn), ij(lambda i, j, k: (i, 0, j)))
    elif parts[0] == "rows_t":
        r = parts[1]
        assert bn % r == 0
        out_dims = (N // r, r, M)
        out_spec = pl.BlockSpec((bn // r, r, bm), ij(lambda i, j, k: (j, 0, i)))
    else:
        c = parts[1]
        assert bn % c == 0
        out_dims = (N // c, M, c)
        out_spec = pl.BlockSpec((bn // c, bm, c), ij(lambda i, j, k: (j, i, 0)))
    n_ex, n_out, n_dep = len(extras), len(out_dtypes), len(deps)

    def body(*refs):
        a_ref, b_ref = refs[0], refs[1]
        ex = refs[2:2 + n_ex]
        outs = refs[2 + n_ex + n_dep:2 + n_ex + n_dep + n_out]
        part = _dot(a_ref[...], b_ref[...], mode)

        def finish(acc):
            vals = epilogue(acc, *[e[...] for e in ex]) if epilogue is not None else (acc,)
            for o, v in zip(outs, vals):
                if parts is not None and parts[0] == "rows_t":
                    v = v.T
                v = v.astype(o.dtype)
                if parts is None:
                    o[...] = v
                elif parts[0] in ("rows", "rows_t"):
                    for ch in range(v.shape[0] // parts[1]):
                        o[ch] = v[ch * parts[1]:(ch + 1) * parts[1], :]
                else:
                    for ch in range(bn // parts[1]):
                        o[ch] = v[:, ch * parts[1]:(ch + 1) * parts[1]]

        if nk == 1:
            finish(part)
        else:
            acc_ref = refs[-1]
            k = pl.program_id(2)

            @pl.when(k == 0)
            def _():
                acc_ref[...] = part

            @pl.when(k > 0)
            def _():
                acc_ref[...] += part

            @pl.when(k == nk - 1)
            def _():
                finish(acc_ref[...])

    grid = (nn, nm, nk) if j_outer else (nm, nn, nk)
    res = pl.pallas_call(
        body, name=name, grid=grid,
        in_specs=[a_spec, b_spec] + ex_specs + [pl.BlockSpec(memory_space=pl.ANY)] * n_dep,
        out_specs=[out_spec] * n_out,
        out_shape=[jax.ShapeDtypeStruct(out_dims, dt) for dt in out_dtypes],
        scratch_shapes=[pltpu.VMEM((bm, bn), F32)] if nk > 1 else [],
        compiler_params=_cparams(("parallel", "parallel", "arbitrary")),
    )(a, b, *[e for e, _ in extras], *deps)
    return res[0] if n_out == 1 else res


ROW_SUBTILES = 2


def _matmul_rows(a, b, *, M, K, mode, bm, name, row_fn, out_dtypes, tiles=(), vecs=(), acc_widths=(), deps=()):
    N = D_MODEL
    assert M % bm == 0
    segs = list(a) if isinstance(a, (list, tuple)) else [a]
    widths = [s_.shape[1] for s_ in segs]
    assert sum(widths) == K and (len(segs) == 1 or mode == "nn")
    n_s, n_t, n_v, n_o, n_a, n_d = len(segs), len(tiles), len(vecs), len(out_dtypes), len(acc_widths), len(deps)
    row = pl.BlockSpec((bm, N), lambda i: (i, 0))
    b_shape = (K, N) if mode == "nn" else (N, K)

    def body(*refs):
        b_ref = refs[n_s]
        ins = refs[n_s + 1:n_s + 1 + n_t + n_v]
        outs = refs[n_s + 1 + n_t + n_v + n_d:n_s + 1 + n_t + n_v + n_d + n_o]
        accs = refs[n_s + 1 + n_t + n_v + n_d + n_o:]
        for o in accs:
            @pl.when(pl.program_id(0) == 0)
            def _(o=o):
                o[...] = jnp.zeros_like(o)

        for s_ in range(ROW_SUBTILES):
            rows = pl.ds(s_ * (bm // ROW_SUBTILES), bm // ROW_SUBTILES)
            if n_s == 1:
                acc = _dot(refs[0][rows, :], b_ref[...], mode)
            else:
                acc, k0 = None, 0
                for a_ref, w_ in zip(refs[:n_s], widths):
                    part = _dot(a_ref[rows, :], b_ref[k0:k0 + w_, :])
                    acc = part if acc is None else acc + part
                    k0 += w_
            tile_vals, partials = row_fn(acc, *[r[rows, :] for r in ins[:n_t]], *[r[...] for r in ins[n_t:]])
            for o, val in zip(outs, tile_vals):
                o[rows, :] = val.astype(o.dtype)
            for o, val in zip(accs, partials):
                o[...] += val

    res = pl.pallas_call(
        body, name=name, grid=(M // bm,),
        in_specs=[pl.BlockSpec((bm, w_), lambda i: (i, 0)) for w_ in widths] + [pl.BlockSpec(b_shape, lambda i: (0, 0))]
        + [row] * n_t + [pl.BlockSpec((1, N), lambda i: (0, 0))] * n_v + [pl.BlockSpec(memory_space=pl.ANY)] * n_d,
        out_specs=[row] * n_o + [pl.BlockSpec((1, w_), lambda i: (0, 0)) for w_ in acc_widths],
        out_shape=[jax.ShapeDtypeStruct((M, N), dt) for dt in out_dtypes]
        + [jax.ShapeDtypeStruct((1, w_), F32) for w_ in acc_widths],
        compiler_params=_cparams(("arbitrary",) if n_a else ("parallel",)),
    )(*segs, b, *tiles, *vecs, *deps)
    return res


def _dw_in_t_half(h, pieces, q, half, *, S, name, into=None, deps=(), bk=1024):
    half_w, cols = D_IN // 2, D_MODEL // W_IN_PIECES
    lo, hi = half * half_w, (half + 1) * half_w
    use, c0 = [], 0
    for p in pieces:
        w_ = p.shape[1]
        a0, a1 = max(lo, c0), min(hi, c0 + w_)
        if a1 > a0:
            use.append((p, a0 - c0, a1 - a0))
        c0 += w_
    n_p, n_into, n_d, nk = len(use), 0 if into is None else 1, len(deps), S // bk
    rows = D_IN // N_DEV

    def body(*refs):
        h_ref, p_refs = refs[0], refs[1:1 + n_p]
        o_ref, acc_ref = refs[1 + n_p + n_into + n_d], refs[-1]
        k = pl.program_id(0)
        dp = jnp.concatenate([r[:, s0:s0 + w_] for r, (_, s0, w_) in zip(p_refs, use)], axis=1)
        part = _dot(h_ref[...], dp, "tn")

        @pl.when(k == 0)
        def _():
            acc_ref[...] = part

        @pl.when(k > 0)
        def _():
            acc_ref[...] += part

        @pl.when(k == nk - 1)
        def _():
            vt = acc_ref[...].T.astype(o_ref.dtype)
            for ch in range(half_w // rows):
                o_ref[ch] = vt[ch * rows:(ch + 1) * rows, :]

    return pl.pallas_call(
        body, name=name, grid=(nk,),
        in_specs=[pl.BlockSpec((bk, cols), lambda k: (k, q))]
        + [pl.BlockSpec((bk, p.shape[1]), lambda k: (k, 0)) for p, _, _ in use]
        + [pl.BlockSpec(memory_space=pl.ANY)] * (n_into + n_d),
        out_specs=pl.BlockSpec((half_w // rows, rows, cols), lambda k: (half, 0, 0)),
        out_shape=jax.ShapeDtypeStruct((N_DEV, rows, cols), MXU_DTYPE),
        input_output_aliases={1 + n_p: 0} if n_into else {},
        scratch_shapes=[pltpu.VMEM((cols, half_w), F32)],
        compiler_params=_cparams(("arbitrary",)),
    )(h, *[p for p, _, _ in use], *([into] if n_into else []), *deps)


def _rmsnorm_fwd(x, g, *, rows, name, bt=512):
    bt = min(bt, rows)

    def body(x_ref, g_ref, o_ref):
        xv = x_ref[...]
        r = lax.rsqrt(jnp.mean(xv * xv, axis=-1, keepdims=True) + EPS)
        o_ref[...] = (xv * r * g_ref[...]).astype(o_ref.dtype)

    return pl.pallas_call(
        body, name=name, grid=(rows // bt,),
        in_specs=[pl.BlockSpec((bt, D_MODEL), lambda i: (i, 0)), pl.BlockSpec((1, D_MODEL), lambda i: (0, 0))],
        out_specs=pl.BlockSpec((bt, D_MODEL), lambda i: (i, 0)),
        out_shape=jax.ShapeDtypeStruct((rows, D_MODEL), MXU_DTYPE),
        compiler_params=_cparams(("parallel",)),
    )(x, g.reshape(1, D_MODEL))


def _rms_bwd_tile(xv, gv, dyv):
    r = lax.rsqrt(jnp.mean(xv * xv, axis=-1, keepdims=True) + EPS)
    w = dyv * gv
    dx = r * w - xv * (r * r * r) * jnp.mean(w * xv, axis=-1, keepdims=True)
    dg = _colsum(dyv * xv * r)
    return dx, dg


def _residual_then_norm(acc, x_t, g):
    x1 = x_t + acc
    r = lax.rsqrt(jnp.mean(x1 * x1, axis=-1, keepdims=True) + EPS)
    return (x1, x1 * r * g), ()


def _residual_then_loss(acc, x_t, tgt_t, g):
    x2 = x_t + acc
    r = lax.rsqrt(jnp.mean(x2 * x2, axis=-1, keepdims=True) + EPS)
    diff = x2 * r * g - tgt_t
    part = jnp.sum(jnp.mean(diff * diff, axis=-1, keepdims=True), axis=0, keepdims=True) * 0.5
    dx, dg = _rms_bwd_tile(x2, g, diff * (1.0 / D_MODEL))
    return (dx, dx), (part, dg)


def _norm_bwd_then_residual(n_out):
    def fn(acc, x_t, res_t, g):
        dx, dg = _rms_bwd_tile(x_t, g, acc)
        return (dx + res_t,) * n_out, (dg,)

    return fn


def _rmsnorm_bwd(x, g, dy, res, *, rows, name, bt=512, dx_dtypes=(F32,)):
    bt = min(bt, rows)
    has_res = res is not None

    def body(*refs):
        x_ref, g_ref, dy_ref = refs[:3]
        res_ref = refs[3] if has_res else None
        outs = refs[3 + int(has_res):]
        dx, dg = _rms_bwd_tile(x_ref[...], g_ref[...], dy_ref[...])
        if has_res:
            dx = dx + res_ref[...]
        dg_ref = outs[-1]

        @pl.when(pl.program_id(0) == 0)
        def _():
            dg_ref[...] = jnp.zeros_like(dg_ref)

        dg_ref[...] += dg
        for o in outs[:-1]:
            o[...] = dx.astype(o.dtype)

    row_spec = pl.BlockSpec((bt, D_MODEL), lambda i: (i, 0))
    vec_spec = pl.BlockSpec((1, D_MODEL), lambda i: (0, 0))
    ins = [x, g.reshape(1, D_MODEL), dy] + ([res] if has_res else [])
    return pl.pallas_call(
        body, name=name, grid=(rows // bt,),
        in_specs=[row_spec, vec_spec, row_spec] + ([row_spec] if has_res else []),
        out_specs=[row_spec] * len(dx_dtypes) + [vec_spec],
        out_shape=[jax.ShapeDtypeStruct((rows, D_MODEL), dt) for dt in dx_dtypes] + [jax.ShapeDtypeStruct((1, D_MODEL), F32)],
        compiler_params=_cparams(("arbitrary",)),
    )(*ins)


LRU_T = 512
SCAN_GROUPS = 4


def _gelu(x):
    t = jnp.tanh(GELU_K * (x + 0.044715 * x * x * x))
    return 0.5 * x * (1.0 + t), t


def _gelu_grad(x, t):
    return 0.5 * (1.0 + t) + 0.5 * x * (1.0 - t * t) * GELU_K * (1.0 + 3.0 * 0.044715 * x * x)


def _softplus_neg(lam):
    z = -lam
    u = jnp.exp(-jnp.abs(z))
    w = 1.0 + u
    l1p = jnp.where(w == 1.0, u, jnp.log(w) * u / jnp.where(w == 1.0, 1.0, w - 1.0))
    return jnp.maximum(z, 0.0) + l1p


def _shift_down(cur, prev8, k, row8):
    y = pltpu.roll(cur, k, 0)
    head = jnp.where(row8 < k, pltpu.roll(prev8, k, 0), y[0:8])
    return jnp.concatenate([head, y[8:]], axis=0)


def _shift_up(cur, next8, k, row8):
    n = cur.shape[0]
    y = pltpu.roll(cur, n - k, 0)
    tail = jnp.where(row8 >= 8 - k, pltpu.roll(next8, 8 - k, 0), y[n - 8:n])
    return jnp.concatenate([y[0:n - 8], tail], axis=0)


def _lru_gates(xl, p8, cw, cb, wa, wx, ba, bx, lam, row8, a_mult=None):
    sh = [xl] + [_shift_down(xl, p8, k, row8) for k in (1, 2, 3)]
    xc = cb + cw[3:4] * sh[0] + cw[2:3] * sh[1] + cw[1:2] * sh[2] + cw[0:1] * sh[3]
    r = jax.nn.sigmoid(_dot(xc, wa) + ba)
    i = jax.nn.sigmoid(_dot(xc, wx) + bx)
    sp = _softplus_neg(lam)
    if a_mult is None:
        la = -LRU_C * r * sp
        a = jnp.exp(la)
        mult = jnp.sqrt(jnp.tanh(-la) * (a * a + 1.0))
    else:
        a, mult = a_mult
    return dict(sh=sh, xc=xc, r=r, i=i, sp=sp, a=a, mult=mult)


def _lru_specs(n_t, reverse):
    T = LRU_T
    tt = (lambda t: n_t - 1 - t) if reverse else (lambda t: t)
    blk = lambda col0: pl.BlockSpec((T, LRU_GROUP), lambda g, t: (tt(t), col0 + g))
    prev8 = lambda col0: pl.BlockSpec((8, LRU_GROUP), lambda g, t: (jnp.maximum(tt(t) * (T // 8) - 1, 0), col0 + g))
    vec = lambda rows: pl.BlockSpec((rows, LRU_GROUP), lambda g, t: (0, g))
    wbd = pl.BlockSpec((1, LRU_GROUP, LRU_GROUP), lambda g, t: (g, 0, 0))
    return blk, prev8, vec, wbd


def _lru_fwd(proj, conv_w, conv_b, wa_bd, wx_bd, b_a, b_x, lam, *, S):
    T = LRU_T
    n_t = S // T
    blk, _, vec, wbd = _lru_specs(n_t, False)

    def body(xl_ref, gate_ref, cw_ref, cb_ref, wa_ref, wx_ref, ba_ref, bx_ref, lam_ref,
             hl_ref, z_ref, a_s, m_ref, prev8, hcar, b_s):
        @pl.when(pl.program_id(1) == 0)
        def _():
            prev8[...] = jnp.zeros_like(prev8)
            hcar[...] = jnp.zeros_like(hcar)

        row8 = lax.broadcasted_iota(jnp.int32, (8, LRU_GROUP), 0)
        xl = xl_ref[...]
        q = _lru_gates(xl, prev8[...], cw_ref[...], cb_ref[...], wa_ref[0], wx_ref[0], ba_ref[...], bx_ref[...],
                       lam_ref[...], row8)
        prev8[...] = xl[T - 8:T]
        a_s[...] = q["a"]
        m_ref[...] = q["mult"]
        b_s[...] = q["mult"] * q["i"] * q["xc"]

        def step(c, carry):
            local = []
            for u in range(SCAN_GROUPS):
                off = pl.multiple_of((c * SCAN_GROUPS + u) * 8, 8)
                A = a_s[pl.ds(off, 8), :]
                B = b_s[pl.ds(off, 8), :]
                for k in (1, 2, 4):
                    a_sh = jnp.where(row8 >= k, pltpu.roll(A, k, 0), 1.0)
                    b_sh = jnp.where(row8 >= k, pltpu.roll(B, k, 0), 0.0)
                    B = A * b_sh + B
                    A = A * a_sh
                local.append((off, A, B))
            for off, A, B in local:
                h = A * carry + B
                hl_ref[pl.ds(off, 8), :] = h
                carry = h[7:8, :]
            return carry

        hcar[...] = lax.fori_loop(0, T // (8 * SCAN_GROUPS), step, hcar[...])
        ge, _ = _gelu(gate_ref[...])
        z_ref[...] = (ge * hl_ref[...]).astype(z_ref.dtype)

    return pl.pallas_call(
        body, name="lru_fwd", grid=(N_LRU_GROUPS, n_t),
        in_specs=[blk(C_XL // LRU_GROUP), blk(C_GATE // LRU_GROUP), vec(4), vec(1), wbd, wbd, vec(1), vec(1), vec(1)],
        out_specs=[blk(0)] * 4,
        out_shape=[jax.ShapeDtypeStruct((S, D_RNN), F32), jax.ShapeDtypeStruct((S, D_RNN), MXU_DTYPE),
                   jax.ShapeDtypeStruct((S, D_RNN), F32), jax.ShapeDtypeStruct((S, D_RNN), F32)],
        scratch_shapes=[pltpu.VMEM((8, LRU_GROUP), F32), pltpu.VMEM((1, LRU_GROUP), F32), pltpu.VMEM((T, LRU_GROUP), F32)],
        compiler_params=_cparams(("parallel", "arbitrary")),
    )(proj, proj, conv_w, conv_b, wa_bd, wx_bd, b_a, b_x, lam)


def _lru_bwd(proj, hl, a_fwd, mult_fwd, dz, conv_w, conv_b, wa_bd, wx_bd, b_a, b_x, lam, *, S):
    T = LRU_T
    n_t = S // T
    blk, prev8s, vec, wbd = _lru_specs(n_t, True)

    def body(xl_ref, xlp_ref, gate_ref, hl_ref, hlp_ref, a_ref, m_ref, dz_ref, cw_ref, cb_ref, wa_ref, wx_ref, ba_ref,
             bx_ref, lam_ref, dxl_ref, dgate_ref, dcw_ref, dcb_ref, dwa_ref, dwx_ref, dba_ref, dbx_ref, dlam_ref,
             next8, gcar, c_s, b_s, l_s):
        t = pl.program_id(1)
        first_chunk = t == n_t - 1

        @pl.when(t == 0)
        def _():
            next8[...] = jnp.zeros_like(next8)
            gcar[...] = jnp.zeros_like(gcar)
            for ref in (dcw_ref, dcb_ref, dwa_ref, dwx_ref, dba_ref, dbx_ref, dlam_ref):
                ref[...] = jnp.zeros_like(ref)

        row8 = lax.broadcasted_iota(jnp.int32, (8, LRU_GROUP), 0)
        rowT = lax.broadcasted_iota(jnp.int32, (T, LRU_GROUP), 0)
        keep = jnp.where(first_chunk, 0.0, 1.0)
        xl = xl_ref[...]
        wa, wx, lam_v = wa_ref[0], wx_ref[0], lam_ref[...]
        q = _lru_gates(xl, xlp_ref[...] * keep, cw_ref[...], cb_ref[...], wa, wx, ba_ref[...], bx_ref[...], lam_v, row8,
                       a_mult=(a_ref[...], m_ref[...]))
        a, mult, r, i, xc, sp = q["a"], q["mult"], q["r"], q["i"], q["xc"], q["sp"]
        hl_v = hl_ref[...]
        dz_v = dz_ref[...]
        gate = gate_ref[...]
        ge, th = _gelu(gate)
        dgate_ref[...] = (dz_v * hl_v * _gelu_grad(gate, th)).astype(dgate_ref.dtype)

        c_s[...] = jnp.where(rowT == T - 1, 0.0, pltpu.roll(a, T - 1, 0))
        b_s[...] = dz_v * ge + jnp.where(rowT == T - 1, gcar[...], 0.0)

        def step(n, carry):
            local = []
            for u in range(SCAN_GROUPS):
                off = pl.multiple_of((T // 8 - 1 - (n * SCAN_GROUPS + u)) * 8, 8)
                C = c_s[pl.ds(off, 8), :]
                B = b_s[pl.ds(off, 8), :]
                for k in (1, 2, 4):
                    c_sh = jnp.where(row8 < 8 - k, pltpu.roll(C, 8 - k, 0), 1.0)
                    b_sh = jnp.where(row8 < 8 - k, pltpu.roll(B, 8 - k, 0), 0.0)
                    B = B + C * b_sh
                    C = C * c_sh
                local.append((off, C, B))
            for off, C, B in local:
                lam_t = B + C * carry
                l_s[pl.ds(off, 8), :] = lam_t
                carry = lam_t[0:1, :]
            return carry

        lax.fori_loop(0, T // (8 * SCAN_GROUPS), step, jnp.zeros((1, LRU_GROUP), F32))
        lmb = l_s[...]
        gcar[...] = a[0:1, :] * lmb[0:1, :]

        h_prev = _shift_down(hl_v, hlp_ref[...] * keep, 1, row8)
        da = lmb * h_prev
        dmult = lmb * i * xc
        di = lmb * mult * xc
        dxc = lmb * mult * i
        dla = da * a - dmult * (a * a) / mult
        dr = dla * (-LRU_C * sp)
        dlam_ref[...] += _colsum(dla * (-LRU_C * r)) * (-jax.nn.sigmoid(-lam_v))
        dpa = dr * r * (1.0 - r)
        dpx = di * i * (1.0 - i)
        dxc = dxc + _dot(dpa, wa, "nt") + _dot(dpx, wx, "nt")
        dwa_ref[0] += _dot(xc, dpa, "tn")
        dwx_ref[0] += _dot(xc, dpx, "tn")
        dba_ref[...] += _colsum(dpa)
        dbx_ref[...] += _colsum(dpx)
        dcb_ref[...] += _colsum(dxc)
        cw = cw_ref[...]
        n8 = next8[...]
        dxl = cw[3:4] * dxc
        for k in (1, 2, 3):
            dxl = dxl + cw[3 - k:4 - k] * _shift_up(dxc, n8, k, row8)
        for k in range(4):
            dcw_ref[3 - k:4 - k, :] += _colsum(dxc * q["sh"][k])
        next8[...] = dxc[0:8]
        dxl_ref[...] = dxl.astype(dxl_ref.dtype)

    res = pl.pallas_call(
        body, name="lru_bwd", grid=(N_LRU_GROUPS, n_t),
        in_specs=[blk(C_XL // LRU_GROUP), prev8s(C_XL // LRU_GROUP), blk(C_GATE // LRU_GROUP), blk(0), prev8s(0), blk(0),
                  blk(0), blk(0), vec(4), vec(1), wbd, wbd, vec(1), vec(1), vec(1)],
        out_specs=[blk(0), blk(0), vec(4), vec(1), wbd, wbd, vec(1), vec(1), vec(1)],
        out_shape=[jax.ShapeDtypeStruct((S, D_RNN), MXU_DTYPE), jax.ShapeDtypeStruct((S, D_RNN), MXU_DTYPE),
                   jax.ShapeDtypeStruct((4, D_RNN), F32), jax.ShapeDtypeStruct((1, D_RNN), F32),
                   jax.ShapeDtypeStruct((N_LRU_GROUPS, LRU_GROUP, LRU_GROUP), F32),
                   jax.ShapeDtypeStruct((N_LRU_GROUPS, LRU_GROUP, LRU_GROUP), F32),
                   jax.ShapeDtypeStruct((1, D_RNN), F32), jax.ShapeDtypeStruct((1, D_RNN), F32),
                   jax.ShapeDtypeStruct((1, D_RNN), F32)],
        scratch_shapes=[pltpu.VMEM((8, LRU_GROUP), F32), pltpu.VMEM((1, LRU_GROUP), F32),
                        pltpu.VMEM((T, LRU_GROUP), F32), pltpu.VMEM((T, LRU_GROUP), F32), pltpu.VMEM((T, LRU_GROUP), F32)],
        compiler_params=_cparams(("parallel", "arbitrary")),
    )(proj, proj, proj, hl, hl, a_fwd, mult_fwd, dz, conv_w, conv_b, wa_bd, wx_bd, b_a, b_x, lam)
    return res


def _block_diag(w):
    w4 = w.reshape(N_LRU_GROUPS, 4, LRU_BLOCK, 1, LRU_BLOCK)
    eye = jnp.eye(4, dtype=w.dtype).reshape(1, 4, 1, 4, 1)
    return (w4 * eye).reshape(N_LRU_GROUPS, LRU_GROUP, LRU_GROUP)


def _block_diag_extract(wbd):
    w5 = wbd.reshape(N_LRU_GROUPS, 4, LRU_BLOCK, 4, LRU_BLOCK)
    return jnp.stack([w5[:, a, :, a, :] for a in range(4)], axis=1).reshape(N_LRU_BLOCKS, LRU_BLOCK, LRU_BLOCK)


def _t5_bucket(dist):
    max_exact = NUM_BUCKETS // 2
    df = jnp.maximum(dist, 1).astype(jnp.float32)
    large = max_exact + (jnp.log(df / max_exact) / math.log(MAX_DISTANCE / max_exact)
                         * (NUM_BUCKETS - max_exact)).astype(jnp.int32)
    large = jnp.minimum(large, NUM_BUCKETS - 1)
    return jnp.where(dist < max_exact, dist, large)


def _band_offsets():
    qi = jnp.arange(SPAN)[:, None]
    kj = jnp.arange(2 * SPAN)[None, :]
    return qi + SPAN - kj


def _dil_buckets():
    off = _band_offsets()
    return jnp.stack([_t5_bucket(jnp.maximum(off, 0) * dil) for _, dil in DIL_GROUPS]).astype(jnp.int32)


def _dil_bias(rel_bias, buckets):
    def body(tbl_ref, bk_ref, o_ref):
        g = pl.program_id(0)
        qi = lax.broadcasted_iota(jnp.int32, (SPAN, 2 * SPAN), 0)
        kj = lax.broadcasted_iota(jnp.int32, (SPAN, 2 * SPAN), 1)
        off = qi + SPAN - kj
        valid = (off >= 0) & (off <= SPAN)
        bk = bk_ref[0]
        for h in range(DIL_HEADS):
            acc = jnp.zeros((SPAN, 2 * SPAN), F32)
            for b in range(NUM_BUCKETS):
                acc = jnp.where(bk == b, tbl_ref[b, g * DIL_HEADS + h], acc)
            o_ref[0, h] = jnp.where(valid, acc, NEG)

    return pl.pallas_call(
        body, name="dil_bias", grid=(3,),
        in_specs=[pl.BlockSpec(memory_space=pltpu.SMEM), pl.BlockSpec((1, SPAN, 2 * SPAN), lambda g: (g, 0, 0))],
        out_specs=pl.BlockSpec((1, DIL_HEADS, SPAN, 2 * SPAN), lambda g: (g, 0, 0, 0)),
        out_shape=jax.ShapeDtypeStruct((3, DIL_HEADS, SPAN, 2 * SPAN), F32),
        compiler_params=_cparams(("parallel",)),
    )(rel_bias, buckets)


def _dil_bias_bwd(dbias, buckets):
    def body(db_ref, bk_ref, o_ref):
        lane = lax.broadcasted_iota(jnp.int32, (1, 128), 1)
        rows = [jnp.zeros((1, 128), F32) for _ in range(NUM_BUCKETS)]
        for g in range(3):
            bk = bk_ref[g]
            for h in range(DIL_HEADS):
                d = db_ref[g, h]
                for b in range(NUM_BUCKETS):
                    tot = jnp.sum(_colsum(jnp.where(bk == b, d, 0.0)), axis=1, keepdims=True)
                    rows[b] = jnp.where(lane == g * DIL_HEADS + h, tot, rows[b])
        for b in range(NUM_BUCKETS):
            o_ref[b:b + 1, :] = rows[b]

    return pl.pallas_call(
        body, name="dil_bias_bwd",
        out_shape=jax.ShapeDtypeStruct((NUM_BUCKETS, 128), F32),
        compiler_params=_cparams(),
    )(dbias, buckets)


DIL_SUBBLOCKS = (8, 4, 1)


def _dil_layout(g, S):
    dil, m = DIL_GROUPS[g][1], DIL_SUBBLOCKS[g]
    sub = SPAN * dil
    col = [(C_QKV + t * 768 + g * 256) // 128 for t in range(3)]
    return dil, m, sub, S // (sub * m), col


def _residue_rows(b, r, dil):
    return pl.ds(b * SPAN * dil + r, SPAN, stride=dil) if dil > 1 else pl.ds(b * SPAN, SPAN)


def _for_residues(dil, fn):
    if dil <= 4:
        for r in range(dil):
            fn(r)
    else:
        lax.fori_loop(0, dil, lambda r, c: (fn(r), c)[1], 0, unroll=4)


def _pair_scores(qm, k2, bias, first_cols):
    s = _dot(qm, k2, "nt") * (DIL_HEAD_DIM ** -0.5) + bias
    kj = lax.broadcasted_iota(jnp.int32, s.shape, 1)
    return jnp.where(kj < first_cols, NEG, s)


def _dilated_fwd(proj, bias, g, *, S):
    dil, m, sub, nc, (qc, kc, vc) = _dil_layout(g, S)
    R = sub * m
    cur = lambda cb: pl.BlockSpec((R, 128), lambda p, i: (i, cb + p))
    prv = lambda cb: pl.BlockSpec((sub, 128), lambda p, i: (jnp.maximum(i * m - 1, 0), cb + p))
    out = pl.BlockSpec((R, 128), lambda p, i: (i, p))

    def body(q_ref, kp_ref, kc_ref, vp_ref, vc_ref, b_ref, o_ref, lse_ref):
        lane = lax.broadcasted_iota(jnp.int32, (SPAN, 128), 1)
        sels = (lane < DIL_HEAD_DIM, lane >= DIL_HEAD_DIM)
        for b in range(m):
            first_cols = jnp.where(pl.program_id(1) == 0, SPAN, 0) if b == 0 else 0

            def one(r, b=b, first_cols=first_cols):
                rows = _residue_rows(b, r, dil)
                before = (kc_ref, vc_ref, _residue_rows(b - 1, r, dil)) if b else (kp_ref, vp_ref, _residue_rows(0, r, dil))
                q2 = q_ref[rows, :]
                k2 = _mx(jnp.concatenate([before[0][before[2], :], kc_ref[rows, :]], axis=0))
                v2 = _mx(jnp.concatenate([before[1][before[2], :], vc_ref[rows, :]], axis=0))
                qq = jnp.concatenate([jnp.where(sels[0], q2, 0.0), jnp.where(sels[1], q2, 0.0)], axis=0)
                s = _pair_scores(qq, k2, b_ref[0, 0], first_cols)
                mx = jnp.max(s, axis=-1, keepdims=True)
                p = jnp.exp(s - mx)
                den = jnp.sum(p, axis=-1, keepdims=True)
                o = _dot(p, v2) / den
                st = mx + jnp.log(den)
                o_ref[rows, :] = jnp.where(sels[0], o[0:SPAN], o[SPAN:2 * SPAN])
                lse_ref[rows, :] = jnp.where(lane == 0, st[0:SPAN], jnp.where(lane == 1, st[SPAN:2 * SPAN], 0.0))

            _for_residues(dil, one)

    return pl.pallas_call(
        body, name=f"dil_fwd{g}", grid=(2, nc),
        in_specs=[cur(qc), prv(kc), cur(kc), prv(vc), cur(vc),
                  pl.BlockSpec((1, 1, 2 * SPAN, 2 * SPAN), lambda p, i: (g, p, 0, 0))],
        out_specs=[out, out],
        out_shape=[jax.ShapeDtypeStruct((S, 256), F32), jax.ShapeDtypeStruct((S, 256), F32)],
        compiler_params=_cparams(("parallel", "parallel")),
    )(proj, proj, proj, proj, proj, bias.reshape(3, 2, 2 * SPAN, 2 * SPAN))


def _dilated_bwd(proj, do, lse, delta, bias, g, *, S, into=None):
    dil, m, sub, nc, (qc, kc, vc) = _dil_layout(g, S)
    R = sub * m
    cl = lambda i: jnp.minimum(i, nc - 1)
    cur = lambda cb: pl.BlockSpec((R, 128), lambda p, i: (cl(i), cb + p))
    prv = lambda cb: pl.BlockSpec((sub, 128), lambda p, i: (jnp.maximum(cl(i) * m - 1, 0), cb + p))
    q_out = pl.BlockSpec((R, 128), lambda p, i: (cl(i), 2 * g + p))
    kv_out = pl.BlockSpec((R, 128), lambda p, i: (jnp.maximum(i - 1, 0), 2 * g + p))
    scale = DIL_HEAD_DIM ** -0.5
    n_into = 0 if into is None else 3

    def body(q_ref, kp_ref, kc_ref, vp_ref, vc_ref, do_ref, lse_ref, dl_ref, b_ref, *rest):
        dq_ref, dk_ref, dv_ref, db_ref, dq_s, kc_s, vc_s, kp_s, vp_s, kcar, vcar = rest[n_into:]
        i = pl.program_id(1)

        @pl.when(i == 0)
        def _():
            kcar[...] = jnp.zeros_like(kcar)
            vcar[...] = jnp.zeros_like(vcar)
            db_ref[...] = jnp.zeros_like(db_ref)

        @pl.when(i < nc)
        def _():
            lane = lax.broadcasted_iota(jnp.int32, (SPAN, 128), 1)
            sels = (lane < DIL_HEAD_DIM, lane >= DIL_HEAD_DIM)
            for b in range(m):
                first_cols = jnp.where(i == 0, SPAN, 0) if b == 0 else 0

                def one(r, b=b, first_cols=first_cols):
                    rows = _residue_rows(b, r, dil)
                    rows_before = _residue_rows(b - 1 if b else 0, r, dil)
                    k_before, v_before = (kc_ref, vc_ref) if b else (kp_ref, vp_ref)
                    q2, do2 = q_ref[rows, :], do_ref[rows, :]
                    k2 = _mx(jnp.concatenate([k_before[rows_before, :], kc_ref[rows, :]], axis=0))
                    v2 = _mx(jnp.concatenate([v_before[rows_before, :], vc_ref[rows, :]], axis=0))
                    lse_t, dl_t = lse_ref[rows, :], dl_ref[rows, :]
                    qq = _mx(jnp.concatenate([jnp.where(sels[0], q2, 0.0), jnp.where(sels[1], q2, 0.0)], axis=0))
                    dd = _mx(jnp.concatenate([jnp.where(sels[0], do2, 0.0), jnp.where(sels[1], do2, 0.0)], axis=0))
                    lse2 = jnp.concatenate([lse_t[:, 0:1], lse_t[:, 1:2]], axis=0)
                    dl2 = jnp.concatenate([dl_t[:, 0:1], dl_t[:, 1:2]], axis=0)
                    p = jnp.exp(_pair_scores(qq, k2, b_ref[0, 0], first_cols) - lse2)
                    ds = p * (_dot(dd, v2, "nt") - dl2)
                    db_ref[0] += ds
                    dqq = _dot(ds, k2) * scale
                    dq2 = jnp.where(sels[0], dqq[0:SPAN], dqq[SPAN:2 * SPAN])
                    dk2 = _dot(ds, qq, "tn") * scale
                    dv2 = _dot(p, dd, "tn")
                    dq_s[rows, :] = dq2
                    kc_s[rows, :] = dk2[SPAN:2 * SPAN]
                    vc_s[rows, :] = dv2[SPAN:2 * SPAN]
                    if b:
                        kc_s[rows_before, :] += dk2[0:SPAN]
                        vc_s[rows_before, :] += dv2[0:SPAN]
                    else:
                        kp_s[rows_before, :] = dk2[0:SPAN]
                        vp_s[rows_before, :] = dv2[0:SPAN]

                _for_residues(dil, one)
            dq_ref[...] = dq_s[...].astype(dq_ref.dtype)
            last = pl.ds((m - 1) * sub, sub)
            kcar[last, :] += kp_s[...]
            vcar[last, :] += vp_s[...]
            dk_ref[...] = kcar[...].astype(dk_ref.dtype)
            dv_ref[...] = vcar[...].astype(dv_ref.dtype)
            kcar[...] = kc_s[...]
            vcar[...] = vc_s[...]

        @pl.when(i == nc)
        def _():
            dk_ref[...] = kcar[...].astype(dk_ref.dtype)
            dv_ref[...] = vcar[...].astype(dv_ref.dtype)

    stat = pl.BlockSpec((R, 128), lambda p, i: (cl(i), p))
    big = jax.ShapeDtypeStruct((S, len(DIL_GROUPS) * 256), MXU_DTYPE)
    return pl.pallas_call(
        body, name=f"dil_bwd{g}", grid=(2, nc + 1),
        in_specs=[cur(qc), prv(kc), cur(kc), prv(vc), cur(vc), stat, stat, stat,
                  pl.BlockSpec((1, 1, 2 * SPAN, 2 * SPAN), lambda p, i: (g, p, 0, 0))]
        + [pl.BlockSpec(memory_space=pl.ANY)] * n_into,
        out_specs=[q_out, kv_out, kv_out, pl.BlockSpec((1, 2 * SPAN, 2 * SPAN), lambda p, i: (p, 0, 0))],
        out_shape=[big, big, big, jax.ShapeDtypeStruct((2, 2 * SPAN, 2 * SPAN), F32)],
        input_output_aliases={9 + j: j for j in range(n_into)},
        scratch_shapes=[pltpu.VMEM((R, 128), F32)] * 3 + [pltpu.VMEM((sub, 128), F32)] * 2 + [pltpu.VMEM((R, 128), F32)] * 2,
        compiler_params=_cparams(("parallel", "arbitrary")),
    )(proj, proj, proj, proj, proj, do, lse, delta, bias.reshape(3, 2, 2 * SPAN, 2 * SPAN), *(into or ()))


def _dilated_merge(os_, lses, *, S, bt=512):
    tile = pl.BlockSpec((bt, 128), lambda i, p: (i, p))

    def body(o0, o1, o2, l0, l1, l2, o_ref, om_ref, lse_ref):
        lane = lax.broadcasted_iota(jnp.int32, (bt, 128), 1)
        lo = lane < DIL_HEAD_DIM
        ls = [l0[...], l1[...], l2[...]]
        ws, stat = [], jnp.zeros((bt, 128), F32)
        for e in range(2):
            a = [l[:, e:e + 1] for l in ls]
            m = jnp.maximum(jnp.maximum(a[0], a[1]), a[2])
            ex = [jnp.exp(v - m) for v in a]
            tot = ex[0] + ex[1] + ex[2]
            ws.append([v / tot for v in ex])
            stat = jnp.where(lane == e, m + jnp.log(tot), stat)
        acc = jnp.zeros((bt, 128), F32)
        for gi, o in enumerate((o0, o1, o2)):
            acc = acc + jnp.where(lo, ws[0][gi], ws[1][gi]) * o[...]
        o_ref[...] = acc
        om_ref[...] = _mx(acc)
        lse_ref[...] = stat

    return pl.pallas_call(
        body, name="dil_merge", grid=(S // bt, 2),
        in_specs=[tile] * 6, out_specs=[tile, tile, tile],
        out_shape=[jax.ShapeDtypeStruct((S, 256), F32), jax.ShapeDtypeStruct((S, 256), MXU_DTYPE),
                   jax.ShapeDtypeStruct((S, 256), F32)],
        compiler_params=_cparams(("parallel", "parallel")),
    )(*os_, *lses)


def _with_delta(do, o):
    lane = lax.broadcasted_iota(jnp.int32, (do.shape[0], 128), 1)
    stats = []
    for p in range(2):
        prod = do[:, 128 * p:128 * (p + 1)] * o[:, 128 * p:128 * (p + 1)]
        d0 = jnp.sum(jnp.where(lane < DIL_HEAD_DIM, prod, 0.0), axis=-1, keepdims=True)
        d1 = jnp.sum(jnp.where(lane >= DIL_HEAD_DIM, prod, 0.0), axis=-1, keepdims=True)
        stats.append(jnp.where(lane == 0, d0, jnp.where(lane == 1, d1, 0.0)))
    return do, jnp.concatenate(stats, axis=1)


MEM_T = 2048
QM_BLK = C_QM // MEM_HEAD_DIM


def _mem_attn_fwd(proj, kv, *, S):
    scale = MEM_HEAD_DIM ** -0.5

    def body(q_ref, k_ref, v_ref, o_ref, om_ref, lse_ref):
        s = _dot(q_ref[...], k_ref[...], "nt") * scale
        m = jnp.max(s, axis=-1, keepdims=True)
        p = jnp.exp(s - m)
        den = jnp.sum(p, axis=-1, keepdims=True)
        o = _dot(p, v_ref[...]) / den
        o_ref[...] = o
        om_ref[...] = _mx(o)
        lse_ref[0] = m + jnp.log(den)

    return pl.pallas_call(
        body, name="mem_attn_fwd", grid=(S // MEM_T, MEM_HEADS),
        in_specs=[pl.BlockSpec((MEM_T, MEM_HEAD_DIM), lambda i, h: (i, QM_BLK + h)),
                  pl.BlockSpec((N_MEM, MEM_HEAD_DIM), lambda i, h: (0, h)),
                  pl.BlockSpec((N_MEM, MEM_HEAD_DIM), lambda i, h: (0, MEM_HEADS + h))],
        out_specs=[pl.BlockSpec((MEM_T, MEM_HEAD_DIM), lambda i, h: (i, h)),
                   pl.BlockSpec((MEM_T, MEM_HEAD_DIM), lambda i, h: (i, h)),
                   pl.BlockSpec((1, MEM_T, 1), lambda i, h: (h, i, 0))],
        out_shape=[jax.ShapeDtypeStruct((S, MEM_WIDTH), F32), jax.ShapeDtypeStruct((S, MEM_WIDTH), MXU_DTYPE),
                   jax.ShapeDtypeStruct((MEM_HEADS, S, 1), F32)],
        compiler_params=_cparams(("parallel", "parallel")),
    )(proj, kv, kv)


def _mem_attn_bwd(proj, kv, om, lse, dom, *, S):
    scale = MEM_HEAD_DIM ** -0.5

    def body(q_ref, k_ref, v_ref, o_ref, lse_ref, do_ref, dq_ref, dk_ref, dv_ref):
        @pl.when(pl.program_id(1) == 0)
        def _():
            dk_ref[...] = jnp.zeros_like(dk_ref)
            dv_ref[...] = jnp.zeros_like(dv_ref)

        qv, kv_, vv, dov = q_ref[...], k_ref[...], v_ref[...], do_ref[...]
        p = jnp.exp(_dot(qv, kv_, "nt") * scale - lse_ref[0])
        delta = jnp.sum(dov * o_ref[...], axis=-1, keepdims=True)
        ds = p * (_dot(dov, vv, "nt") - delta)
        dq_ref[...] = (_dot(ds, kv_) * scale).astype(dq_ref.dtype)
        dk_ref[...] += _dot(ds, qv, "tn") * scale
        dv_ref[...] += _dot(p, dov, "tn")

    tile = pl.BlockSpec((MEM_T, MEM_HEAD_DIM), lambda h, i: (i, h))
    kvo = pl.BlockSpec((N_MEM, MEM_HEAD_DIM), lambda h, i: (0, h))
    return pl.pallas_call(
        body, name="mem_attn_bwd", grid=(MEM_HEADS, S // MEM_T),
        in_specs=[pl.BlockSpec((MEM_T, MEM_HEAD_DIM), lambda h, i: (i, QM_BLK + h)),
                  pl.BlockSpec((N_MEM, MEM_HEAD_DIM), lambda h, i: (0, h)),
                  pl.BlockSpec((N_MEM, MEM_HEAD_DIM), lambda h, i: (0, MEM_HEADS + h)),
                  tile, pl.BlockSpec((1, MEM_T, 1), lambda h, i: (h, i, 0)), tile],
        out_specs=[tile, kvo, kvo],
        out_shape=[jax.ShapeDtypeStruct((S, MEM_WIDTH), MXU_DTYPE), jax.ShapeDtypeStruct((N_MEM, MEM_WIDTH), F32),
                   jax.ShapeDtypeStruct((N_MEM, MEM_WIDTH), F32)],
        compiler_params=_cparams(("parallel", "arbitrary")),
    )(proj, kv, kv, om, lse, dom)


MIX_BM = 1024
MIX_BN = 256
GATES_BLK = C_GATES // MIX_BN


def _mix_specs(j_outer):
    ix = (lambda f: (lambda j, i: f(i, j))) if j_outer else (lambda f: f)
    act = lambda width: pl.BlockSpec((MIX_BM, width), ix(lambda i, j: (i, 0)))
    wgt = lambda width: pl.BlockSpec((width, MIX_BN), ix(lambda i, j: (0, j)))
    gate = lambda b: pl.BlockSpec((MIX_BM, MIX_BN), ix(lambda i, j: (i, GATES_BLK + 4 * b + j)))
    bias = lambda b: pl.BlockSpec((1, MIX_BN), ix(lambda i, j: (0, 4 * b + j)))
    tile = pl.BlockSpec((MIX_BM, MIX_BN), ix(lambda i, j: (i, j)))
    return act, wgt, gate, bias, tile


def _mix_fwd(z_lru, o_dil, om, w_lru, w_dil, w_mem, proj, b_gate, *, S):
    act, wgt, gate, bias, tile = _mix_specs(False)

    def body(zl, od, mo, wl, wd, wm, g0, g1, g2, b0, b1, b2, o_ref):
        acc = jax.nn.sigmoid(g0[...] + b0[...]) * _dot(zl[...], wl[...])
        acc += jax.nn.sigmoid(g1[...] + b1[...]) * _dot(od[...], wd[...])
        acc += jax.nn.sigmoid(g2[...] + b2[...]) * _dot(mo[...], wm[...])
        o_ref[...] = acc.astype(o_ref.dtype)

    return pl.pallas_call(
        body, name="mix_fwd", grid=(S // MIX_BM, D_MODEL // MIX_BN),
        in_specs=[act(D_RNN), act(256), act(MEM_WIDTH), wgt(D_RNN), wgt(256), wgt(MEM_WIDTH),
                  gate(0), gate(1), gate(2), bias(0), bias(1), bias(2)],
        out_specs=tile, out_shape=jax.ShapeDtypeStruct((S, D_MODEL), MXU_DTYPE),
        compiler_params=_cparams(("parallel", "parallel")),
    )(z_lru, o_dil, om, w_lru, w_dil, w_mem, proj, proj, proj, b_gate, b_gate, b_gate)


def _mix_bwd(dx1, w_out, z_lru, o_dil, om, w_lru, w_dil, w_mem, proj, b_gate, *, S):
    act, wgt, gate, bias, tile = _mix_specs(False)
    n_j = D_MODEL // MIX_BN

    def body(dx, wo, zl, od, mo, wl, wd, wm, g0, g1, g2, b0, b1, b2,
             dg0, dg1, dg2, dy0, dy1, dy2, db0, db1, db2):
        j = pl.program_id(1)

        @pl.when((pl.program_id(0) == 0) & (j == 0))
        def _():
            for r in (db0, db1, db2):
                r[...] = jnp.zeros_like(r)

        dmv = _dot(dx[...], wo[...], "nt")
        for act_ref, w_ref, g_ref, b_ref, dg_ref, dy_ref, db_ref in (
                (zl, wl, g0, b0, dg0, dy0, db0), (od, wd, g1, b1, dg1, dy1, db1), (mo, wm, g2, b2, dg2, dy2, db2)):
            y = _dot(act_ref[...], w_ref[...])
            gt = jax.nn.sigmoid(g_ref[...] + b_ref[...])
            dgate = dmv * y * gt * (1.0 - gt)
            dg_ref[...] = dgate.astype(dg_ref.dtype)
            dy_ref[...] = (dmv * gt).astype(dy_ref.dtype)
            db_ref[j] += _colsum(dgate)

    big = jax.ShapeDtypeStruct((S, D_MODEL), MXU_DTYPE)
    vec = jax.ShapeDtypeStruct((n_j, 1, MIX_BN), F32)
    vspec = pl.BlockSpec((n_j, 1, MIX_BN), lambda i, j: (0, 0, 0))
    res = pl.pallas_call(
        body, name="mix_bwd", grid=(S // MIX_BM, n_j),
        in_specs=[pl.BlockSpec((MIX_BM, D_MODEL), lambda i, j: (i, 0)), pl.BlockSpec((MIX_BN, D_MODEL), lambda i, j: (j, 0)),
                  act(D_RNN), act(256), act(MEM_WIDTH), wgt(D_RNN), wgt(256), wgt(MEM_WIDTH),
                  gate(0), gate(1), gate(2), bias(0), bias(1), bias(2)],
        out_specs=[tile] * 6 + [vspec] * 3, out_shape=[big] * 6 + [vec] * 3,
        compiler_params=_cparams(("arbitrary", "arbitrary")),
    )(dx1, w_out, z_lru, o_dil, om, w_lru, w_dil, w_mem, proj, proj, proj, b_gate, b_gate, b_gate)
    return list(res[:6]) + [r.reshape(1, D_MODEL) for r in res[6:]]


def _adamw_math(w, g, m, v):
    m = ADAM_B1 * m + (1.0 - ADAM_B1) * g
    v = ADAM_B2 * v + (1.0 - ADAM_B2) * (g * g)
    m_hat = m / (1.0 - ADAM_B1 ** ADAM_STEP)
    v_hat = v / (1.0 - ADAM_B2 ** ADAM_STEP)
    delta = -ADAM_LR * (m_hat / (jnp.sqrt(v_hat) + ADAM_EPS) + ADAM_WD * w)
    return delta, m, v


def _adamw_landed(w, own, land, m, v, *, name, col_blk=0, prev=None):
    R = w.shape[0]
    n_parts, C = land.shape[0], land.shape[2]
    br = next(d for d in (256, 464, 128) if R % d == 0)
    tile = pl.BlockSpec((br, C), lambda i: (i, col_blk))
    part = pl.BlockSpec((br, C), lambda i: (i, 0))
    n_prev = 0 if prev is None else 4

    def body(w_ref, o_ref, l_ref, m_ref, v_ref, *rest):
        g_ref, d_ref, nm_ref, nv_ref = rest[n_prev:]
        g = o_ref[...].astype(F32)
        for p in range(n_parts):
            g = g + l_ref[p].astype(F32)
        d, nm, nv = _adamw_math(w_ref[...], g, m_ref[...], v_ref[...])
        g_ref[...] = g
        d_ref[...] = d
        nm_ref[...] = nm
        nv_ref[...] = nv

    return pl.pallas_call(
        body, name=name, grid=(R // br,),
        in_specs=[tile, part, pl.BlockSpec((n_parts, br, C), lambda i: (0, i, 0)), tile, tile]
        + [pl.BlockSpec(memory_space=pl.ANY)] * n_prev,
        out_specs=[tile] * 4, out_shape=[jax.ShapeDtypeStruct(w.shape, F32)] * 4,
        input_output_aliases={5 + j: j for j in range(n_prev)},
        compiler_params=_cparams(("parallel",)),
    )(w, own, land, m, v, *(prev or ()))


def _adamw_plain(w, g, m, v, *, name):
    def body(w_ref, g_ref, m_ref, v_ref, d_ref, nm_ref, nv_ref):
        d, nm, nv = _adamw_math(w_ref[...], g_ref[...], m_ref[...], v_ref[...])
        d_ref[...] = d
        nm_ref[...] = nm
        nv_ref[...] = nv

    return pl.pallas_call(
        body, name=name, out_shape=[jax.ShapeDtypeStruct(w.shape, F32)] * 3, compiler_params=_cparams(),
    )(w, g, m, v)


def _my_pos():
    return lax.axis_index("x"), lax.axis_index("y"), lax.axis_index("c")


def _dev_index(p):
    return 4 * p[0] + 2 * p[1] + p[2]


def _all_gather(shards):
    n = len(shards)
    hbm = pl.BlockSpec(memory_space=pl.ANY)

    def body(*refs):
        ins, outs = refs[:n], refs[n:2 * n]
        send_sems, recv_sems, local_sems = refs[2 * n:]
        x, y, c = _my_pos()
        me, sibling = (x, y, c), (x, y, 1 - c)
        chips = [(1 - x, y), (x, 1 - y), (1 - x, 1 - y)]

        def copy(a, k, block, to, src=None):
            dst = outs[a].at[_dev_index(block)]
            return pltpu.make_async_remote_copy(
                src_ref=dst if src is None else src, dst_ref=dst,
                send_sem=send_sems.at[a, k], recv_sem=recv_sems.at[a, k], device_id=to, device_id_type=MESH)

        mine = [pltpu.make_async_copy(ins[a], outs[a].at[_dev_index(me)], local_sems.at[a]) for a in range(n)]
        for cp in mine:
            cp.start()
        first = []
        for a in range(n):
            first.append(copy(a, 0, me, sibling, src=ins[a]))
            first += [copy(a, 1 + j, me, (*chip, c), src=ins[a]) for j, chip in enumerate(chips)]
        for cp in first:
            cp.start()
        passed = []
        for j, chip in enumerate(chips):
            for a in range(n):
                copy(a, 1 + j, (*chip, c), me).wait_recv()
                fwd = copy(a, 4 + j, (*chip, c), sibling)
                fwd.start()
                passed.append(fwd)
        for a in range(n):
            copy(a, 0, sibling, me).wait_recv()
        for j, chip in enumerate(chips):
            for a in range(n):
                copy(a, 4 + j, (*chip, 1 - c), me).wait_recv()
        for cp in first + passed:
            cp.wait_send()
        for cp in mine:
            cp.wait()

    return pl.pallas_call(
        body, name="all_gather_weights",
        in_specs=[hbm] * n, out_specs=[hbm] * n,
        out_shape=[jax.ShapeDtypeStruct((N_DEV,) + s.shape, s.dtype) for s in shards],
        scratch_shapes=[pltpu.SemaphoreType.DMA((n, 7)), pltpu.SemaphoreType.DMA((n, 7)), pltpu.SemaphoreType.DMA((n,))],
        compiler_params=pltpu.CompilerParams(has_side_effects=True),
    )(*shards)


def _peers(me):
    x, y, c = me
    out = []
    for k in range(1, 8):
        fx, fy, fc = (k >> 2) & 1, (k >> 1) & 1, k & 1
        out.append((k - 1, (1 - x if fx else x, 1 - y if fy else y, 1 - c if fc else c)))
    return out


HBM_SPEC = pl.BlockSpec(memory_space=pltpu.HBM)
SEM_SPEC = pl.BlockSpec(memory_space=pltpu.SEMAPHORE)
DATAFLOW_EFFECT = pltpu.SideEffectType.DATAFLOW_SIDE_EFFECTING


def _gather_refs(src, land, me, peer, k):
    return src, land.at[_dev_index(me)]


def _scatter_refs(src, land, me, peer, k):
    return src.at[_dev_index(peer)], land.at[k]


def _push_start(srcs, land_shapes, refs_of, name, after=()):
    n, n_after = len(srcs), len(after)

    def body(*refs):
        ins, lands = refs[:n], refs[n:2 * n]
        send_sems, recv_sems, token = refs[2 * n + n_after], refs[2 * n + n_after + 1], refs[-1]
        me = _my_pos()
        for k, peer in _peers(me):
            for a in range(n):
                src, dst = refs_of(ins[a], lands[a], me, peer, k)
                pltpu.make_async_remote_copy(src_ref=src, dst_ref=dst, send_sem=send_sems.at[7 * a + k],
                                             recv_sem=recv_sems.at[7 * a + k], device_id=peer, device_id_type=MESH).start()
        token[...] = jnp.zeros_like(token)

    lands = [lax.empty(shp, s.dtype) for shp, s in zip(land_shapes, srcs)]
    hbm = lambda a: pltpu.with_memory_space_constraint(a, pltpu.HBM)
    res = pl.pallas_call(
        body, name=name,
        out_shape=(pltpu.SemaphoreType.DMA((7 * n,)), pltpu.SemaphoreType.DMA((7 * n,)),
                   *[pltpu.HBM(s.shape, s.dtype) for s in srcs], *[pltpu.HBM(l.shape, l.dtype) for l in lands],
                   jax.ShapeDtypeStruct((8, 128), F32)),
        in_specs=[HBM_SPEC] * (2 * n) + [pl.BlockSpec(memory_space=pl.ANY)] * n_after,
        out_specs=(SEM_SPEC, SEM_SPEC, *[HBM_SPEC] * (2 * n), pl.BlockSpec(memory_space=pltpu.VMEM)),
        input_output_aliases={i: 2 + i for i in range(2 * n)},
        compiler_params=pltpu.CompilerParams(has_side_effects=DATAFLOW_EFFECT),
    )(*[hbm(s) for s in srcs], *[hbm(l) for l in lands], *after)
    return dict(sems=(res[0], res[1]), srcs=list(res[2:2 + n]), lands=list(res[2 + n:2 + 2 * n]), token=res[-1], n=n,
                refs_of=refs_of, name=name)


def _push_wait(started, after):
    n, refs_of = started["n"], started["refs_of"]
    after = list(after) if isinstance(after, (list, tuple)) else [after]

    def body(*refs):
        ins, lands = refs[:n], refs[n:2 * n]
        send_sems, recv_sems = refs[2 * n], refs[2 * n + 1]
        me = _my_pos()
        for k, peer in _peers(me):
            for a in range(n):
                src, dst = refs_of(ins[a], lands[a], me, peer, k)
                cp = pltpu.make_async_remote_copy(src_ref=src, dst_ref=dst, send_sem=send_sems.at[7 * a + k],
                                                  recv_sem=recv_sems.at[7 * a + k], device_id=peer, device_id_type=MESH)
                cp.wait_send()
                cp.wait_recv()

    arrs = started["srcs"] + started["lands"]
    res = pl.pallas_call(
        body, name=started["name"].replace("start", "wait"),
        out_shape=tuple(pltpu.HBM(a.shape, a.dtype) for a in arrs),
        in_specs=[HBM_SPEC] * (2 * n) + [SEM_SPEC, SEM_SPEC] + [pl.BlockSpec(memory_space=pl.ANY)] * len(after),
        out_specs=tuple([HBM_SPEC] * (2 * n)),
        input_output_aliases={i: i for i in range(2 * n)},
        compiler_params=pltpu.CompilerParams(has_side_effects=DATAFLOW_EFFECT),
    )(*arrs, *started["sems"], *after)
    return list(res[n:2 * n])


def _sum_slots(slots):
    def body(in_ref, out_ref):
        acc = in_ref[0]
        for d in range(1, N_DEV):
            acc = acc + in_ref[d]
        out_ref[...] = acc

    return pl.pallas_call(body, name="sum_small", out_shape=jax.ShapeDtypeStruct(slots.shape[1:], F32),
                          compiler_params=_cparams())(slots)


def _adamw_many(ws, gs, ms, vs):
    n = len(ws)

    def body(*refs):
        for i in range(n):
            w_ref, g_ref, m_ref, v_ref = (refs[j * n + i] for j in range(4))
            d_, nm, nv = _adamw_math(w_ref[...], g_ref[...], m_ref[...], v_ref[...])
            for j, val in enumerate((d_, nm, nv)):
                refs[(4 + j) * n + i][...] = val

    res = pl.pallas_call(body, name="adamw_small", out_shape=[jax.ShapeDtypeStruct(w_.shape, F32) for w_ in ws] * 3,
                         compiler_params=_cparams())(*ws, *gs, *ms, *vs)
    return [(res[i], res[n + i], res[2 * n + i]) for i in range(n)]


def _local_step(x, mem, tgt, W, P, late_weights, send_grads, reduce_small, tie0):
    S = x.shape[0]
    W = dict(W)
    h = _rmsnorm_fwd(x, P["g_mix"] + tie0, rows=S, name="norm_mix")
    proj = _matmul(h, W["w_in_t"], M=S, N=D_IN, K=D_MODEL, mode="nt", bm=512, bn=D_IN // 2, bk=D_MODEL, name="mm_in",
                   j_outer=True)

    wa_bd, wx_bd = _mx(_block_diag(P["w_rg_a"])), _mx(_block_diag(P["w_rg_x"]))
    lru_args = (W["conv_w"], P["conv_b"].reshape(1, -1), wa_bd, wx_bd, P["b_rg_a"].reshape(1, -1),
                P["b_rg_x"].reshape(1, -1), P["lru_lambda"].reshape(1, -1))
    hl, z_lru, a_lru, mult_lru = _lru_fwd(proj, *lru_args, S=S)

    buckets = _dil_buckets()
    bias = _dil_bias(P["rel_bias"], buckets)
    group_out = [_dilated_fwd(proj, bias, g, S=S) for g in range(len(DIL_GROUPS))]
    o_dil, o_dil_m, lse_dil = _dilated_merge([o for o, _ in group_out], [l for _, l in group_out], S=S)

    W.update(late_weights("branch", [o_dil, z_lru]))
    mem_n = _rmsnorm_fwd(mem, P["g_mem"], rows=N_MEM, name="norm_mem")
    kv = _matmul(mem_n, W["w_mem_kv"], M=N_MEM, N=2 * MEM_WIDTH, K=D_MODEL, mode="nn", bm=N_MEM, bn=512, bk=D_MODEL,
                 name="mm_kv")
    om, om_m, lse_mem = _mem_attn_fwd(proj, kv, S=S)
    b_gate = P["b_gate"].reshape(1, -1)
    merged = _mix_fwd(z_lru, o_dil_m, om_m, W["w_lru_out"], W["w_dil_out"], W["w_mem_out"], proj, b_gate, S=S)
    g_mlp, g_final, g_mix = (P[n].reshape(1, D_MODEL) for n in ("g_mlp", "g_final", "g_mix"))
    x1, hm = _matmul_rows(merged, W["w_out"], M=S, K=D_MODEL, mode="nn", bm=512, name="mm_out",
                          row_fn=_residual_then_norm, out_dtypes=(F32, MXU_DTYPE), tiles=[x], vecs=[g_mlp])
    W.update(late_weights("mlp", [hm]))

    def relu2(acc):
        rl = jnp.maximum(acc, 0.0)
        return rl * rl, rl

    act, relu_u = _matmul(hm, W["w_mlp_in_t"], M=S, N=D_FF, K=D_MODEL, mode="nt", bm=1024, bn=1024, bk=D_MODEL,
                          name="mm_mlp_in", out_dtypes=(MXU_DTYPE, MXU_DTYPE), epilogue=relu2, j_outer=True)
    dx2, dx2_m, loss, dg_final = _matmul_rows(
        act, W["w_mlp_out"], M=S, K=D_FF, mode="nn", bm=512, name="mm_mlp_out", row_fn=_residual_then_loss,
        out_dtypes=(F32, MXU_DTYPE), tiles=[x1, tgt], vecs=[g_final], acc_widths=(1, D_MODEL))

    G, Gs = {}, {}
    Gs["g_final"] = dg_final
    dw = dict(mode="tn", K=S, bk=S, out_dtypes=(MXU_DTYPE,))
    G["w_mlp_out"] = _matmul(act, dx2_m, M=D_FF, N=D_MODEL, bm=512, bn=D_MODEL, name="mm_dw_mlp_out",
                             parts=("rows", D_FF // N_DEV), **dw)
    du = _matmul(dx2_m, W["w_mlp_out"], M=S, N=D_FF, K=D_MODEL, mode="nt", bm=1024, bn=1024, bk=D_MODEL, name="mm_du",
                 out_dtypes=(MXU_DTYPE,), epilogue=lambda acc, rl: (acc * (2.0 * rl.astype(F32)),),
                 extras=[(relu_u, (0, 0))], j_outer=True)
    G["w_mlp_in"] = _matmul(hm, du, M=D_MODEL, N=D_FF, bm=D_MODEL, bn=512, name="mm_dw_mlp_in",
                            parts=("cols", D_FF // N_DEV), **dw)
    tie1 = send_grads({n: G.pop(n) for n in ("w_mlp_out", "w_mlp_in")})
    dx1, dx1_m, Gs["g_mlp"] = _matmul_rows(
        du, W["w_mlp_in_t"], M=S, K=D_FF, mode="nn", bm=512, name="mm_dhm", row_fn=_norm_bwd_then_residual(2),
        out_dtypes=(F32, MXU_DTYPE), tiles=[x1, dx2], vecs=[g_mlp], acc_widths=(D_MODEL,), deps=[tie1])
    G["w_out"] = _matmul(merged, dx1_m, M=D_MODEL, N=D_MODEL, bm=512, bn=D_MODEL, name="mm_dw_out",
                         parts=("rows", D_MODEL // N_DEV), **dw)
    (dg0, dg1, dg2, dy_lru, dy_dil, dy_mem, db0, db1, db2) = _mix_bwd(
        dx1_m, W["w_out"], z_lru, o_dil_m, om_m, W["w_lru_out"], W["w_dil_out"], W["w_mem_out"], proj, b_gate, S=S)
    Gs["b_gate0"], Gs["b_gate1"], Gs["b_gate2"] = db0, db1, db2

    G["w_mem_out"] = _matmul(om_m, dy_mem, M=MEM_WIDTH, N=D_MODEL, bm=MEM_WIDTH, bn=D_MODEL, name="mm_dw_mem_out",
                             parts=("cols", D_MODEL // N_DEV), **dw)
    dom = _matmul(dy_mem, W["w_mem_out"], M=S, N=MEM_WIDTH, K=D_MODEL, mode="nt", bm=512, bn=MEM_WIDTH, bk=D_MODEL,
                  name="mm_dom")
    dqm, dk_mem, dv_mem = _mem_attn_bwd(proj, kv, om, lse_mem, dom, S=S)
    dkv = jnp.concatenate([dk_mem, dv_mem], axis=1)
    G["w_mem_kv"] = _matmul(mem_n, dkv, M=D_MODEL, N=2 * MEM_WIDTH, K=N_MEM, mode="tn", bm=D_MODEL, bn=2 * MEM_WIDTH,
                            bk=N_MEM, name="mm_dw_kv", out_dtypes=(MXU_DTYPE,), parts=("rows", D_MODEL // N_DEV))
    dmem_n = _matmul(dkv, W["w_mem_kv"], M=N_MEM, N=D_MODEL, K=2 * MEM_WIDTH, mode="nt", bm=N_MEM, bn=D_MODEL,
                     bk=2 * MEM_WIDTH, name="mm_dmem")
    (Gs["g_mem"],) = _rmsnorm_bwd(mem, P["g_mem"], dmem_n, None, rows=N_MEM, name="norm_mem_bwd", dx_dtypes=())

    G["w_dil_out"] = _matmul(o_dil_m, dy_dil, M=256, N=D_MODEL, bm=256, bn=D_MODEL, name="mm_dw_dil_out",
                             parts=("cols", D_MODEL // N_DEV), **dw)
    do_dil, delta = _matmul(dy_dil, W["w_dil_out"], M=S, N=256, K=D_MODEL, mode="nt", bm=512, bn=256, bk=D_MODEL,
                            name="mm_do_dil", out_dtypes=(F32, F32), epilogue=_with_delta, extras=[(o_dil, (0, 0))])
    G["w_lru_out"] = _matmul(z_lru, dy_lru, M=D_RNN, N=D_MODEL, bm=D_RNN, bn=D_MODEL, name="mm_dw_lru_out",
                             parts=("cols", D_MODEL // N_DEV), **dw)
    dz = _matmul(dy_lru, W["w_lru_out"], M=S, N=D_RNN, K=D_MODEL, mode="nt", bm=512, bn=D_RNN, bk=D_MODEL, name="mm_dz_lru")
    tie2 = send_grads({n: G.pop(n) for n in ("w_out", "w_mem_out", "w_mem_kv", "w_dil_out", "w_lru_out")})
    bias = bias + tie2[0, 0]
    dqkv, dbias = None, []
    for g in range(len(DIL_GROUPS)):
        *dqkv, db_g = _dilated_bwd(proj, do_dil, lse_dil, delta, bias, g, S=S, into=dqkv)
        dbias.append(db_g)
    drel = _dil_bias_bwd(jnp.stack(dbias, axis=0).reshape(len(DIL_GROUPS), DIL_HEADS, SPAN, 2 * SPAN), buckets)
    Gs["rel_bias"] = drel

    dxl, dgl, dcw, dcb, dwa, dwx, dba, dbx, dlam = _lru_bwd(proj, hl, a_lru, mult_lru, dz, *lru_args, S=S)
    Gs["conv_w"], Gs["conv_b"] = dcw, dcb
    Gs["w_rg_a"], Gs["w_rg_x"] = _block_diag_extract(dwa), _block_diag_extract(dwx)
    Gs["b_rg_a"], Gs["b_rg_x"], Gs["lru_lambda"] = dba, dbx, dlam
    Gs["loss"] = loss

    dproj = [dxl, dgl] + dqkv + [dqm, dg0, dg1, dg2]
    tie = []
    for q in range(W_IN_PIECES):
        dw_q = None
        for half in range(2):
            dw_q = _dw_in_t_half(h, dproj, q, half, S=S, name=f"mm_dw_in_{q}_{half}", into=dw_q, deps=tie)
        tie = [send_grads({f"w_in_{q}": dw_q})]
    grad_x, Gs["g_mix"] = _matmul_rows(
        dproj, W["w_in_t"], M=S, K=D_IN, mode="nn", bm=256, name="mm_dh", row_fn=_norm_bwd_then_residual(1),
        out_dtypes=(F32,), tiles=[x, dx1], vecs=[g_mix], acc_widths=(D_MODEL,), deps=tie)
    return grad_x, reduce_small(Gs)


BIG = ("w_in", "w_lru_out", "w_dil_out", "w_mem_kv", "w_mem_out", "w_out", "w_mlp_in", "w_mlp_out")
W_IN_PIECES = 2
COL_SHARDED = ("w_lru_out", "w_dil_out", "w_mem_out", "w_mlp_in")
GATHERED_TRANSPOSED = ("w_mlp_in",)
SMALL = ("g_mix", "b_gate", "conv_b", "w_rg_a", "b_rg_a", "w_rg_x", "b_rg_x", "lru_lambda", "rel_bias", "g_mem",
         "g_mlp", "g_final")
WEIGHTS = ("g_mix", "w_in", "b_gate", "conv_w", "conv_b", "w_rg_a", "b_rg_a", "w_rg_x", "b_rg_x", "lru_lambda",
           "w_lru_out", "rel_bias", "w_dil_out", "g_mem", "w_mem_kv", "w_mem_out", "w_out", "g_mlp", "w_mlp_in",
           "w_mlp_out", "g_final")


def _gathered_to_full(name, gathered):
    if name in COL_SHARDED:
        n, r, c = gathered.shape
        return gathered.transpose(1, 0, 2).reshape(r, n * c)
    n, r, c = gathered.shape
    return gathered.reshape(n * r, c)


SMALL_GRADS = (("g_mix", (1, 1024)), ("b_gate0", (1, 1024)), ("b_gate1", (1, 1024)), ("b_gate2", (1, 1024)),
               ("conv_b", (1, 768)), ("w_rg_a", (12, 64, 64)), ("b_rg_a", (1, 768)), ("w_rg_x", (12, 64, 64)),
               ("b_rg_x", (1, 768)), ("lru_lambda", (1, 768)), ("rel_bias", (32, 128)), ("g_mem", (1, 1024)),
               ("g_mlp", (1, 1024)), ("g_final", (1, 1024)), ("conv_w", (4, 768)), ("loss", (1, 1)))


def _pack(parts):
    flat = jnp.concatenate([p.reshape(-1) for p in parts])
    return jnp.pad(flat, (0, (-flat.shape[0]) % 1024)).reshape(-1, 128)


def _unpack(pack, shapes):
    flat = pack.reshape(-1)
    out, off = [], 0
    for shp in shapes:
        size = math.prod(shp)
        out.append(flat[off:off + size].reshape(shp))
        off += size
    return out


def kernel(x, mem, g_mix, w_in, b_gate, conv_w, conv_b, w_rg_a, b_rg_a, w_rg_x, b_rg_x, lru_lambda, w_lru_out, rel_bias, w_dil_out, g_mem, w_mem_kv, w_mem_out, w_out, g_mlp, w_mlp_in, w_mlp_out, g_final, loss_target, m_g_mix, m_w_in, m_b_gate, m_conv_w, m_conv_b, m_w_rg_a, m_b_rg_a, m_w_rg_x, m_b_rg_x, m_lru_lambda, m_w_lru_out, m_rel_bias, m_w_dil_out, m_g_mem, m_w_mem_kv, m_w_mem_out, m_w_out, m_g_mlp, m_w_mlp_in, m_w_mlp_out, m_g_final, v_g_mix, v_w_in, v_b_gate, v_conv_w, v_conv_b, v_w_rg_a, v_b_rg_a, v_w_rg_x, v_b_rg_x, v_lru_lambda, v_w_lru_out, v_rel_bias, v_w_dil_out, v_g_mem, v_w_mem_kv, v_w_mem_out, v_w_out, v_g_mlp, v_w_mlp_in, v_w_mlp_out, v_g_final):
    w = dict(g_mix=g_mix, w_in=w_in, b_gate=b_gate, conv_w=conv_w, conv_b=conv_b, w_rg_a=w_rg_a, b_rg_a=b_rg_a,
             w_rg_x=w_rg_x, b_rg_x=b_rg_x, lru_lambda=lru_lambda, w_lru_out=w_lru_out, rel_bias=rel_bias,
             w_dil_out=w_dil_out, g_mem=g_mem, w_mem_kv=w_mem_kv, w_mem_out=w_mem_out, w_out=w_out, g_mlp=g_mlp,
             w_mlp_in=w_mlp_in, w_mlp_out=w_mlp_out, g_final=g_final)
    m = dict(g_mix=m_g_mix, w_in=m_w_in, b_gate=m_b_gate, conv_w=m_conv_w, conv_b=m_conv_b, w_rg_a=m_w_rg_a,
             b_rg_a=m_b_rg_a, w_rg_x=m_w_rg_x, b_rg_x=m_b_rg_x, lru_lambda=m_lru_lambda, w_lru_out=m_w_lru_out,
             rel_bias=m_rel_bias, w_dil_out=m_w_dil_out, g_mem=m_g_mem, w_mem_kv=m_w_mem_kv, w_mem_out=m_w_mem_out,
             w_out=m_w_out, g_mlp=m_g_mlp, w_mlp_in=m_w_mlp_in, w_mlp_out=m_w_mlp_out, g_final=m_g_final)
    v = dict(g_mix=v_g_mix, w_in=v_w_in, b_gate=v_b_gate, conv_w=v_conv_w, conv_b=v_conv_b, w_rg_a=v_w_rg_a,
             b_rg_a=v_b_rg_a, w_rg_x=v_w_rg_x, b_rg_x=v_b_rg_x, lru_lambda=v_lru_lambda, w_lru_out=v_w_lru_out,
             rel_bias=v_rel_bias, w_dil_out=v_w_dil_out, g_mem=v_g_mem, w_mem_kv=v_w_mem_kv, w_mem_out=v_w_mem_out,
             w_out=v_w_out, g_mlp=v_g_mlp, w_mlp_in=v_w_mlp_in, w_mlp_out=v_w_mlp_out, g_final=v_g_final)

    my_idx = _dev_index(_my_pos())

    g_in, g_cw = _all_gather([_mx(w["w_in"].T), w["conv_w"]])
    W = {"w_in_t": g_in.reshape(D_IN, D_MODEL), "conv_w": g_cw.transpose(1, 0, 2).reshape(CONV_WIDTH, D_RNN)}
    late, order_after = {}, [g_in]
    for group, names in (("branch", ("w_mem_kv", "w_lru_out", "w_dil_out", "w_mem_out", "w_out")),
                         ("mlp", ("w_mlp_in", "w_mlp_out"))):
        shards = [_mx(w[n].T if n in GATHERED_TRANSPOSED else w[n]) for n in names]
        started = _push_start(shards, [(N_DEV,) + s.shape for s in shards], _gather_refs, f"gather_{group}_start",
                              after=order_after)
        late[group] = (names, shards, started)
        order_after = [started["token"]]
    P = {n: w[n] for n in SMALL}

    def late_weights(group, after):
        names, shards, started = late[group]
        out = {}
        for n, land, own in zip(names, _push_wait(started, after), shards):
            full = lax.dynamic_update_index_in_dim(land, own, my_idx, 0)
            if n in GATHERED_TRANSPOSED:
                out[n + "_t"] = full.reshape(-1, full.shape[2])
            else:
                out[n] = _gathered_to_full(n, full)
        return out

    sent, small = [], {}

    def send_grads(gs):
        names = list(gs)
        parts = [gs[n] for n in names]
        own = [lax.dynamic_index_in_dim(p, my_idx, 0, keepdims=False) for p in parts]
        started = _push_start(parts, [(N_DEV - 1,) + p.shape[1:] for p in parts], _scatter_refs,
                              f"scatter{len(sent)}_start")
        sent.append((names, own, started))
        return started["token"]

    def reduce_small(gs):
        small["pack"] = _pack([gs[n] for n, _ in SMALL_GRADS])
        small["started"] = _push_start([small["pack"]], [(N_DEV,) + small["pack"].shape], _gather_refs, "small_start")
        return small["started"]["token"]

    grad_x, last_token = _local_step(x[0], mem[0], loss_target[0], W, P, late_weights, send_grads, reduce_small,
                                     late["mlp"][2]["token"][0, 0])

    grads, deltas, new_m, new_v = {}, {}, {}, {}
    after = last_token
    for names, own, started in sent[:-W_IN_PIECES]:
        for n, o, land in zip(names, own, _push_wait(started, after)):
            grads[n], deltas[n], new_m[n], new_v[n] = _adamw_landed(w[n], o, land, m[n], v[n], name=f"adamw_{n}")
            after = deltas[n]
    prev = None
    for q, (names, own, started) in enumerate(sent[-W_IN_PIECES:]):
        (land,) = _push_wait(started, after)
        prev = _adamw_landed(w["w_in"].T, own[0], land, m["w_in"].T, v["w_in"].T, name=f"adamw_{names[0]}",
                             col_blk=q, prev=prev)
        after = prev[1]
    grads["w_in"], deltas["w_in"], new_m["w_in"], new_v["w_in"] = [t.T for t in prev]
    (small_land,) = _push_wait(small["started"], [after] + [deltas[n] for n in BIG if n != "w_in"])
    total = _sum_slots(lax.dynamic_update_index_in_dim(small_land, small["pack"], my_idx, 0))
    summed = dict(zip([n for n, _ in SMALL_GRADS], _unpack(total, [shp for _, shp in SMALL_GRADS])))
    summed["b_gate"] = jnp.concatenate([summed.pop(f"b_gate{b}") for b in range(3)], axis=1)
    summed["rel_bias"] = summed["rel_bias"][:, :3 * DIL_HEADS]
    for n in SMALL:
        grads[n] = summed[n].reshape(w[n].shape)
    small_updates = _adamw_many([w[n] for n in SMALL], [grads[n] for n in SMALL], [m[n] for n in SMALL],
                                [v[n] for n in SMALL])
    for n, (d_, nm_, nv_) in zip(SMALL, small_updates):
        deltas[n], new_m[n], new_v[n] = d_, nm_, nv_
    conv_w_sum, loss_sum = summed["conv_w"], summed["loss"]
    cw_cols = D_RNN // N_DEV
    grads["conv_w"] = lax.dynamic_slice(conv_w_sum, (0, my_idx * cw_cols), (CONV_WIDTH, cw_cols))
    deltas["conv_w"], new_m["conv_w"], new_v["conv_w"] = _adamw_plain(
        w["conv_w"], grads["conv_w"], m["conv_w"], v["conv_w"], name="adamw_conv_w")

    return (loss_sum.reshape(()), grad_x[None], *[grads[n] for n in WEIGHTS], *[deltas[n] for n in WEIGHTS],
            *[new_m[n] for n in WEIGHTS], *[new_v[n] for n in WEIGHTS])
```

```python
import functools
import math

import jax
import jax.numpy as jnp
from jax import lax
from jax.experimental import pallas as pl
from jax.experimental.pallas import tpu as pltpu

F32 = jnp.float32
MXU_DTYPE = jnp.bfloat16
VMEM_LIMIT_BYTES = 56 * 1024 * 1024
N_DEV = 8

D_MODEL = 1024
N_MEM = 256
MEM_HEADS = 4
MEM_HEAD_DIM = 128
MEM_WIDTH = 512
D_RNN = 768
LRU_BLOCK = 64
N_LRU_BLOCKS = 12
LRU_GROUP = 256
N_LRU_GROUPS = 3
CONV_WIDTH = 4
LRU_C = 8.0
DIL_GROUPS = ((128, 1), (512, 4), (2048, 16))
SPAN = 128
DIL_HEADS = 4
DIL_HEAD_DIM = 64
NUM_BUCKETS = 32
MAX_DISTANCE = 2048
D_FF = 4096
D_IN = 7424
EPS = 1e-6
NEG = -1e30
C_XL, C_GATE, C_QKV, C_QM, C_GATES = 0, 768, 1536, 3840, 4352

ADAM_LR = 0.001
ADAM_B1 = 0.9
ADAM_B2 = 0.999
ADAM_EPS = 1e-08
ADAM_WD = 0.01
ADAM_STEP = 10

MESH = pl.DeviceIdType.MESH
GELU_K = math.sqrt(2.0 / math.pi)


def _cparams(sem=None):
    kw = dict(vmem_limit_bytes=VMEM_LIMIT_BYTES)
    if sem is not None:
        kw["dimension_semantics"] = sem
    return pltpu.CompilerParams(**kw)


def _mx(v):
    return v.astype(MXU_DTYPE)


def _dot(a, b, mode="nn"):
    dims = {"nn": (((1,), (0,)), ((), ())), "nt": (((1,), (1,)), ((), ())), "tn": (((0,), (0,)), ((), ()))}[mode]
    return lax.dot_general(_mx(a), _mx(b), dims, preferred_element_type=F32)


def _colsum(v):
    return jnp.sum(v, axis=0, keepdims=True)


def _matmul(a, b, *, M, N, K, mode, bm, bn, bk, name, out_dtypes=(F32,), epilogue=None, extras=(),
            a_off=(0, 0), b_off=(0, 0), j_outer=False, deps=(), parts=None):
    assert M % bm == 0 and N % bn == 0 and K % bk == 0, (name, M, N, K, bm, bn, bk)
    nm, nn, nk = M // bm, N // bn, K // bk

    def ij(f):
        if j_outer:
            return lambda j, i, k: f(i, j, k)
        return f

    if mode == "tn":
        a_spec = pl.BlockSpec((bk, bm), ij(lambda i, j, k: (k + a_off[0], i + a_off[1])))
    else:
        a_spec = pl.BlockSpec((bm, bk), ij(lambda i, j, k: (i + a_off[0], k + a_off[1])))
    if mode == "nt":
        b_spec = pl.BlockSpec((bn, bk), ij(lambda i, j, k: (j + b_off[0], k + b_off[1])))
    else:
        b_spec = pl.BlockSpec((bk, bn), ij(lambda i, j, k: (k + b_off[0], j + b_off[1])))
    ex_specs = [pl.BlockSpec((bm, bn), ij(functools.partial(lambda i, j, k, o: (i + o[0], j + o[1]), o=off)))
                for _, off in extras]
    if parts is None:
        out_dims = (M, N)
        out_spec = pl.BlockSpec((bm, bn), ij(lambda i, j, k: (i, j)))
    elif parts[0] == "rows":
        r = parts[1]
        assert bm % r == 0
        out_dims = (M // r, r, N)
        out_spec = pl.BlockSpec((bm // r, r, bn), ij(lambda i, j, k: (i, 0, j)))
    elif parts[0] == "rows_t":
        r = parts[1]
        assert bn % r == 0
        out_dims = (N // r, r, M)
        out_spec = pl.BlockSpec((bn // r, r, bm), ij(lambda i, j, k: (j, 0, i)))
    else:
        c = parts[1]
        assert bn % c == 0
        out_dims = (N // c, M, c)
        out_spec = pl.BlockSpec((bn // c, bm, c), ij(lambda i, j, k: (j, i, 0)))
    n_ex, n_out, n_dep = len(extras), len(out_dtypes), len(deps)

    def body(*refs):
        a_ref, b_ref = refs[0], refs[1]
        ex = refs[2:2 + n_ex]
        outs = refs[2 + n_ex + n_dep:2 + n_ex + n_dep + n_out]
        part = _dot(a_ref[...], b_ref[...], mode)

        def finish(acc):
            vals = epilogue(acc, *[e[...] for e in ex]) if epilogue is not None else (acc,)
            for o, v in zip(outs, vals):
                if parts is not None and parts[0] == "rows_t":
                    v = v.T
                v = v.astype(o.dtype)
                if parts is None:
                    o[...] = v
                elif parts[0] in ("rows", "rows_t"):
                    for ch in range(v.shape[0] // parts[1]):
                        o[ch] = v[ch * parts[1]:(ch + 1) * parts[1], :]
                else:
                    for ch in range(bn // parts[1]):
                        o[ch] = v[:, ch * parts[1]:(ch + 1) * parts[1]]

        if nk == 1:
            finish(part)
        else:
            acc_ref = refs[-1]
            k = pl.program_id(2)

            @pl.when(k == 0)
            def _():
                acc_ref[...] = part

            @pl.when(k > 0)
            def _():
                acc_ref[...] += part

            @pl.when(k == nk - 1)
            def _():
                finish(acc_ref[...])

    grid = (nn, nm, nk) if j_outer else (nm, nn, nk)
    res = pl.pallas_call(
        body, name=name, grid=grid,
        in_specs=[a_spec, b_spec] + ex_specs + [pl.BlockSpec(memory_space=pl.ANY)] * n_dep,
        out_specs=[out_spec] * n_out,
        out_shape=[jax.ShapeDtypeStruct(out_dims, dt) for dt in out_dtypes],
        scratch_shapes=[pltpu.VMEM((bm, bn), F32)] if nk > 1 else [],
        compiler_params=_cparams(("parallel", "parallel", "arbitrary")),
    )(a, b, *[e for e, _ in extras], *deps)
    return res[0] if n_out == 1 else res


ROW_SUBTILES = 2


def _matmul_rows(a, b, *, M, K, mode, bm, name, row_fn, out_dtypes, tiles=(), vecs=(), acc_widths=(), deps=()):
    N = D_MODEL
    assert M % bm == 0
    segs = list(a) if isinstance(a, (list, tuple)) else [a]
    widths = [s_.shape[1] for s_ in segs]
    assert sum(widths) == K and (len(segs) == 1 or mode == "nn")
    n_s, n_t, n_v, n_o, n_a, n_d = len(segs), len(tiles), len(vecs), len(out_dtypes), len(acc_widths), len(deps)
    row = pl.BlockSpec((bm, N), lambda i: (i, 0))
    b_shape = (K, N) if mode == "nn" else (N, K)

    def body(*refs):
        b_ref = refs[n_s]
        ins = refs[n_s + 1:n_s + 1 + n_t + n_v]
        outs = refs[n_s + 1 + n_t + n_v + n_d:n_s + 1 + n_t + n_v + n_d + n_o]
        accs = refs[n_s + 1 + n_t + n_v + n_d + n_o:]
        for o in accs:
            @pl.when(pl.program_id(0) == 0)
            def _(o=o):
                o[...] = jnp.zeros_like(o)

        for s_ in range(ROW_SUBTILES):
            rows = pl.ds(s_ * (bm // ROW_SUBTILES), bm // ROW_SUBTILES)
            if n_s == 1:
                acc = _dot(refs[0][rows, :], b_ref[...], mode)
            else:
                acc, k0 = None, 0
                for a_ref, w_ in zip(refs[:n_s], widths):
                    part = _dot(a_ref[rows, :], b_ref[k0:k0 + w_, :])
                    acc = part if acc is None else acc + part
                    k0 += w_
            tile_vals, partials = row_fn(acc, *[r[rows, :] for r in ins[:n_t]], *[r[...] for r in ins[n_t:]])
            for o, val in zip(outs, tile_vals):
                o[rows, :] = val.astype(o.dtype)
            for o, val in zip(accs, partials):
                o[...] += val

    res = pl.pallas_call(
        body, name=name, grid=(M // bm,),
        in_specs=[pl.BlockSpec((bm, w_), lambda i: (i, 0)) for w_ in widths] + [pl.BlockSpec(b_shape, lambda i: (0, 0))]
        + [row] * n_t + [pl.BlockSpec((1, N), lambda i: (0, 0))] * n_v + [pl.BlockSpec(memory_space=pl.ANY)] * n_d,
        out_specs=[row] * n_o + [pl.BlockSpec((1, w_), lambda i: (0, 0)) for w_ in acc_widths],
        out_shape=[jax.ShapeDtypeStruct((M, N), dt) for dt in out_dtypes]
        + [jax.ShapeDtypeStruct((1, w_), F32) for w_ in acc_widths],
        compiler_params=_cparams(("arbitrary",) if n_a else ("parallel",)),
    )(*segs, b, *tiles, *vecs, *deps)
    return res


def _dw_in_t_half(h, pieces, q, half, *, S, name, into=None, deps=(), bk=1024):
    half_w, cols = D_IN // 2, D_MODEL // W_IN_PIECES
    lo, hi = half * half_w, (half + 1) * half_w
    use, c0 = [], 0
    for p in pieces:
        w_ = p.shape[1]
        a0, a1 = max(lo, c0), min(hi, c0 + w_)
        if a1 > a0:
            use.append((p, a0 - c0, a1 - a0))
        c0 += w_
    n_p, n_into, n_d, nk = len(use), 0 if into is None else 1, len(deps), S // bk
    rows = D_IN // N_DEV

    def body(*refs):
        h_ref, p_refs = refs[0], refs[1:1 + n_p]
        o_ref, acc_ref = refs[1 + n_p + n_into + n_d], refs[-1]
        k = pl.program_id(0)
        dp = jnp.concatenate([r[:, s0:s0 + w_] for r, (_, s0, w_) in zip(p_refs, use)], axis=1)
        part = _dot(h_ref[...], dp, "tn")

        @pl.when(k == 0)
        def _():
            acc_ref[...] = part

        @pl.when(k > 0)
        def _():
            acc_ref[...] += part

        @pl.when(k == nk - 1)
        def _():
            vt = acc_ref[...].T.astype(o_ref.dtype)
            for ch in range(half_w // rows):
                o_ref[ch] = vt[ch * rows:(ch + 1) * rows, :]

    return pl.pallas_call(
        body, name=name, grid=(nk,),
        in_specs=[pl.BlockSpec((bk, cols), lambda k: (k, q))]
        + [pl.BlockSpec((bk, p.shape[1]), lambda k: (k, 0)) for p, _, _ in use]
        + [pl.BlockSpec(memory_space=pl.ANY)] * (n_into + n_d),
        out_specs=pl.BlockSpec((half_w // rows, rows, cols), lambda k: (half, 0, 0)),
        out_shape=jax.ShapeDtypeStruct((N_DEV, rows, cols), MXU_DTYPE),
        input_output_aliases={1 + n_p: 0} if n_into else {},
        scratch_shapes=[pltpu.VMEM((cols, half_w), F32)],
        compiler_params=_cparams(("arbitrary",)),
    )(h, *[p for p, _, _ in use], *([into] if n_into else []), *deps)


def _rmsnorm_fwd(x, g, *, rows, name, bt=512):
    bt = min(bt, rows)

    def body(x_ref, g_ref, o_ref):
        xv = x_ref[...]
        r = lax.rsqrt(jnp.mean(xv * xv, axis=-1, keepdims=True) + EPS)
        o_ref[...] = (xv * r * g_ref[...]).astype(o_ref.dtype)

    return pl.pallas_call(
        body, name=name, grid=(rows // bt,),
        in_specs=[pl.BlockSpec((bt, D_MODEL), lambda i: (i, 0)), pl.BlockSpec((1, D_MODEL), lambda i: (0, 0))],
        out_specs=pl.BlockSpec((bt, D_MODEL), lambda i: (i, 0)),
        out_shape=jax.ShapeDtypeStruct((rows, D_MODEL), MXU_DTYPE),
        compiler_params=_cparams(("parallel",)),
    )(x, g.reshape(1, D_MODEL))


def _rms_bwd_tile(xv, gv, dyv):
    r = lax.rsqrt(jnp.mean(xv * xv, axis=-1, keepdims=True) + EPS)
    w = dyv * gv
    dx = r * w - xv * (r * r * r) * jnp.mean(w * xv, axis=-1, keepdims=True)
    dg = _colsum(dyv * xv * r)
    return dx, dg


def _residual_then_norm(acc, x_t, g):
    x1 = x_t + acc
    r = lax.rsqrt(jnp.mean(x1 * x1, axis=-1, keepdims=True) + EPS)
    return (x1, x1 * r * g), ()


def _residual_then_loss(acc, x_t, tgt_t, g):
    x2 = x_t + acc
    r = lax.rsqrt(jnp.mean(x2 * x2, axis=-1, keepdims=True) + EPS)
    diff = x2 * r * g - tgt_t
    part = jnp.sum(jnp.mean(diff * diff, axis=-1, keepdims=True), axis=0, keepdims=True) * 0.5
    dx, dg = _rms_bwd_tile(x2, g, diff * (1.0 / D_MODEL))
    return (dx, dx), (part, dg)


def _norm_bwd_then_residual(n_out):
    def fn(acc, x_t, res_t, g):
        dx, dg = _rms_bwd_tile(x_t, g, acc)
        return (dx + res_t,) * n_out, (dg,)

    return fn


def _rmsnorm_bwd(x, g, dy, res, *, rows, name, bt=512, dx_dtypes=(F32,)):
    bt = min(bt, rows)
    has_res = res is not None

    def body(*refs):
        x_ref, g_ref, dy_ref = refs[:3]
        res_ref = refs[3] if has_res else None
        outs = refs[3 + int(has_res):]
        dx, dg = _rms_bwd_tile(x_ref[...], g_ref[...], dy_ref[...])
        if has_res:
            dx = dx + res_ref[...]
        dg_ref = outs[-1]

        @pl.when(pl.program_id(0) == 0)
        def _():
            dg_ref[...] = jnp.zeros_like(dg_ref)

        dg_ref[...] += dg
        for o in outs[:-1]:
            o[...] = dx.astype(o.dtype)

    row_spec = pl.BlockSpec((bt, D_MODEL), lambda i: (i, 0))
    vec_spec = pl.BlockSpec((1, D_MODEL), lambda i: (0, 0))
    ins = [x, g.reshape(1, D_MODEL), dy] + ([res] if has_res else [])
    return pl.pallas_call(
        body, name=name, grid=(rows // bt,),
        in_specs=[row_spec, vec_spec, row_spec] + ([row_spec] if has_res else []),
        out_specs=[row_spec] * len(dx_dtypes) + [vec_spec],
        out_shape=[jax.ShapeDtypeStruct((rows, D_MODEL), dt) for dt in dx_dtypes] + [jax.ShapeDtypeStruct((1, D_MODEL), F32)],
        compiler_params=_cparams(("arbitrary",)),
    )(*ins)


LRU_T = 512
SCAN_GROUPS = 4


def _gelu(x):
    t = jnp.tanh(GELU_K * (x + 0.044715 * x * x * x))
    return 0.5 * x * (1.0 + t), t


def _gelu_grad(x, t):
    return 0.5 * (1.0 + t) + 0.5 * x * (1.0 - t * t) * GELU_K * (1.0 + 3.0 * 0.044715 * x * x)


def _softplus_neg(lam):
    z = -lam
    u = jnp.exp(-jnp.abs(z))
    w = 1.0 + u
    l1p = jnp.where(w == 1.0, u, jnp.log(w) * u / jnp.where(w == 1.0, 1.0, w - 1.0))
    return jnp.maximum(z, 0.0) + l1p


def _shift_down(cur, prev8, k, row8):
    y = pltpu.roll(cur, k, 0)
    head = jnp.where(row8 < k, pltpu.roll(prev8, k, 0), y[0:8])
    return jnp.concatenate([head, y[8:]], axis=0)


def _shift_up(cur, next8, k, row8):
    n = cur.shape[0]
    y = pltpu.roll(cur, n - k, 0)
    tail = jnp.where(row8 >= 8 - k, pltpu.roll(next8, 8 - k, 0), y[n - 8:n])
    return jnp.concatenate([y[0:n - 8], tail], axis=0)


def _lru_gates(xl, p8, cw, cb, wa, wx, ba, bx, lam, row8, a_mult=None):
    sh = [xl] + [_shift_down(xl, p8, k, row8) for k in (1, 2, 3)]
    xc = cb + cw[3:4] * sh[0] + cw[2:3] * sh[1] + cw[1:2] * sh[2] + cw[0:1] * sh[3]
    r = jax.nn.sigmoid(_dot(xc, wa) + ba)
    i = jax.nn.sigmoid(_dot(xc, wx) + bx)
    sp = _softplus_neg(lam)
    if a_mult is None:
        la = -LRU_C * r * sp
        a = jnp.exp(la)
        mult = jnp.sqrt(jnp.tanh(-la) * (a * a + 1.0))
    else:
        a, mult = a_mult
    return dict(sh=sh, xc=xc, r=r, i=i, sp=sp, a=a, mult=mult)


def _lru_specs(n_t, reverse):
    T = LRU_T
    tt = (lambda t: n_t - 1 - t) if reverse else (lambda t: t)
    blk = lambda col0: pl.BlockSpec((T, LRU_GROUP), lambda g, t: (tt(t), col0 + g))
    prev8 = lambda col0: pl.BlockSpec((8, LRU_GROUP), lambda g, t: (jnp.maximum(tt(t) * (T // 8) - 1, 0), col0 + g))
    vec = lambda rows: pl.BlockSpec((rows, LRU_GROUP), lambda g, t: (0, g))
    wbd = pl.BlockSpec((1, LRU_GROUP, LRU_GROUP), lambda g, t: (g, 0, 0))
    return blk, prev8, vec, wbd


def _lru_fwd(proj, conv_w, conv_b, wa_bd, wx_bd, b_a, b_x, lam, *, S):
    T = LRU_T
    n_t = S // T
    blk, _, vec, wbd = _lru_specs(n_t, False)

    def body(xl_ref, gate_ref, cw_ref, cb_ref, wa_ref, wx_ref, ba_ref, bx_ref, lam_ref,
             hl_ref, z_ref, a_s, m_ref, prev8, hcar, b_s):
        @pl.when(pl.program_id(1) == 0)
        def _():
            prev8[...] = jnp.zeros_like(prev8)
            hcar[...] = jnp.zeros_like(hcar)

        row8 = lax.broadcasted_iota(jnp.int32, (8, LRU_GROUP), 0)
        xl = xl_ref[...]
        q = _lru_gates(xl, prev8[...], cw_ref[...], cb_ref[...], wa_ref[0], wx_ref[0], ba_ref[...], bx_ref[...],
                       lam_ref[...], row8)
        prev8[...] = xl[T - 8:T]
        a_s[...] = q["a"]
        m_ref[...] = q["mult"]
        b_s[...] = q["mult"] * q["i"] * q["xc"]

        def step(c, carry):
            local = []
            for u in range(SCAN_GROUPS):
                off = pl.multiple_of((c * SCAN_GROUPS + u) * 8, 8)
                A = a_s[pl.ds(off, 8), :]
                B = b_s[pl.ds(off, 8), :]
                for k in (1, 2, 4):
                    a_sh = jnp.where(row8 >= k, pltpu.roll(A, k, 0), 1.0)
                    b_sh = jnp.where(row8 >= k, pltpu.roll(B, k, 0), 0.0)
                    B = A * b_sh + B
                    A = A * a_sh
                local.append((off, A, B))
            for off, A, B in local:
                h = A * carry + B
                hl_ref[pl.ds(off, 8), :] = h
                carry = h[7:8, :]
            return carry

        hcar[...] = lax.fori_loop(0, T // (8 * SCAN_GROUPS), step, hcar[...])
        ge, _ = _gelu(gate_ref[...])
        z_ref[...] = (ge * hl_ref[...]).astype(z_ref.dtype)

    return pl.pallas_call(
        body, name="lru_fwd", grid=(N_LRU_GROUPS, n_t),
        in_specs=[blk(C_XL // LRU_GROUP), blk(C_GATE // LRU_GROUP), vec(4), vec(1), wbd, wbd, vec(1), vec(1), vec(1)],
        out_specs=[blk(0)] * 4,
        out_shape=[jax.ShapeDtypeStruct((S, D_RNN), F32), jax.ShapeDtypeStruct((S, D_RNN), MXU_DTYPE),
                   jax.ShapeDtypeStruct((S, D_RNN), F32), jax.ShapeDtypeStruct((S, D_RNN), F32)],
        scratch_shapes=[pltpu.VMEM((8, LRU_GROUP), F32), pltpu.VMEM((1, LRU_GROUP), F32), pltpu.VMEM((T, LRU_GROUP), F32)],
        compiler_params=_cparams(("parallel", "arbitrary")),
    )(proj, proj, conv_w, conv_b, wa_bd, wx_bd, b_a, b_x, lam)


def _lru_bwd(proj, hl, a_fwd, mult_fwd, dz, conv_w, conv_b, wa_bd, wx_bd, b_a, b_x, lam, *, S):
    T = LRU_T
    n_t = S // T
    blk, prev8s, vec, wbd = _lru_specs(n_t, True)

    def body(xl_ref, xlp_ref, gate_ref, hl_ref, hlp_ref, a_ref, m_ref, dz_ref, cw_ref, cb_ref, wa_ref, wx_ref, ba_ref,
             bx_ref, lam_ref, dxl_ref, dgate_ref, dcw_ref, dcb_ref, dwa_ref, dwx_ref, dba_ref, dbx_ref, dlam_ref,
             next8, gcar, c_s, b_s, l_s):
        t = pl.program_id(1)
        first_chunk = t == n_t - 1

        @pl.when(t == 0)
        def _():
            next8[...] = jnp.zeros_like(next8)
            gcar[...] = jnp.zeros_like(gcar)
            for ref in (dcw_ref, dcb_ref, dwa_ref, dwx_ref, dba_ref, dbx_ref, dlam_ref):
                ref[...] = jnp.zeros_like(ref)

        row8 = lax.broadcasted_iota(jnp.int32, (8, LRU_GROUP), 0)
        rowT = lax.broadcasted_iota(jnp.int32, (T, LRU_GROUP), 0)
        keep = jnp.where(first_chunk, 0.0, 1.0)
        xl = xl_ref[...]
        wa, wx, lam_v = wa_ref[0], wx_ref[0], lam_ref[...]
        q = _lru_gates(xl, xlp_ref[...] * keep, cw_ref[...], cb_ref[...], wa, wx, ba_ref[...], bx_ref[...], lam_v, row8,
                       a_mult=(a_ref[...], m_ref[...]))
        a, mult, r, i, xc, sp = q["a"], q["mult"], q["r"], q["i"], q["xc"], q["sp"]
        hl_v = hl_ref[...]
        dz_v = dz_ref[...]
        gate = gate_ref[...]
        ge, th = _gelu(gate)
        dgate_ref[...] = (dz_v * hl_v * _gelu_grad(gate, th)).astype(dgate_ref.dtype)

        c_s[...] = jnp.where(rowT == T - 1, 0.0, pltpu.roll(a, T - 1, 0))
        b_s[...] = dz_v * ge + jnp.where(rowT == T - 1, gcar[...], 0.0)

        def step(n, carry):
            local = []
            for u in range(SCAN_GROUPS):
                off = pl.multiple_of((T // 8 - 1 - (n * SCAN_GROUPS + u)) * 8, 8)
                C = c_s[pl.ds(off, 8), :]
                B = b_s[pl.ds(off, 8), :]
                for k in (1, 2, 4):
                    c_sh = jnp.where(row8 < 8 - k, pltpu.roll(C, 8 - k, 0), 1.0)
                    b_sh = jnp.where(row8 < 8 - k, pltpu.roll(B, 8 - k, 0), 0.0)
                    B = B + C * b_sh
                    C = C * c_sh
                local.append((off, C, B))
            for off, C, B in local:
                lam_t = B + C * carry
                l_s[pl.ds(off, 8), :] = lam_t
                carry = lam_t[0:1, :]
            return carry

        lax.fori_loop(0, T // (8 * SCAN_GROUPS), step, jnp.zeros((1, LRU_GROUP), F32))
        lmb = l_s[...]
        gcar[...] = a[0:1, :] * lmb[0:1, :]

        h_prev = _shift_down(hl_v, hlp_ref[...] * keep, 1, row8)
        da = lmb * h_prev
        dmult = lmb * i * xc
        di = lmb * mult * xc
        dxc = lmb * mult * i
        dla = da * a - dmult * (a * a) / mult
        dr = dla * (-LRU_C * sp)
        dlam_ref[...] += _colsum(dla * (-LRU_C * r)) * (-jax.nn.sigmoid(-lam_v))
        dpa = dr * r * (1.0 - r)
        dpx = di * i * (1.0 - i)
        dxc = dxc + _dot(dpa, wa, "nt") + _dot(dpx, wx, "nt")
        dwa_ref[0] += _dot(xc, dpa, "tn")
        dwx_ref[0] += _dot(xc, dpx, "tn")
        dba_ref[...] += _colsum(dpa)
        dbx_ref[...] += _colsum(dpx)
        dcb_ref[...] += _colsum(dxc)
        cw = cw_ref[...]
        n8 = next8[...]
        dxl = cw[3:4] * dxc
        for k in (1, 2, 3):
            dxl = dxl + cw[3 - k:4 - k] * _shift_up(dxc, n8, k, row8)
        for k in range(4):
            dcw_ref[3 - k:4 - k, :] += _colsum(dxc * q["sh"][k])
        next8[...] = dxc[0:8]
        dxl_ref[...] = dxl.astype(dxl_ref.dtype)

    res = pl.pallas_call(
        body, name="lru_bwd", grid=(N_LRU_GROUPS, n_t),
        in_specs=[blk(C_XL // LRU_GROUP), prev8s(C_XL // LRU_GROUP), blk(C_GATE // LRU_GROUP), blk(0), prev8s(0), blk(0),
                  blk(0), blk(0), vec(4), vec(1), wbd, wbd, vec(1), vec(1), vec(1)],
        out_specs=[blk(0), blk(0), vec(4), vec(1), wbd, wbd, vec(1), vec(1), vec(1)],
        out_shape=[jax.ShapeDtypeStruct((S, D_RNN), MXU_DTYPE), jax.ShapeDtypeStruct((S, D_RNN), MXU_DTYPE),
                   jax.ShapeDtypeStruct((4, D_RNN), F32), jax.ShapeDtypeStruct((1, D_RNN), F32),
                   jax.ShapeDtypeStruct((N_LRU_GROUPS, LRU_GROUP, LRU_GROUP), F32),
                   jax.ShapeDtypeStruct((N_LRU_GROUPS, LRU_GROUP, LRU_GROUP), F32),
                   jax.ShapeDtypeStruct((1, D_RNN), F32), jax.ShapeDtypeStruct((1, D_RNN), F32),
                   jax.ShapeDtypeStruct((1, D_RNN), F32)],
        scratch_shapes=[pltpu.VMEM((8, LRU_GROUP), F32), pltpu.VMEM((1, LRU_GROUP), F32),
                        pltpu.VMEM((T, LRU_GROUP), F32), pltpu.VMEM((T, LRU_GROUP), F32), pltpu.VMEM((T, LRU_GROUP), F32)],
        compiler_params=_cparams(("parallel", "arbitrary")),
    )(proj, proj, proj, hl, hl, a_fwd, mult_fwd, dz, conv_w, conv_b, wa_bd, wx_bd, b_a, b_x, lam)
    return res


def _block_diag(w):
    w4 = w.reshape(N_LRU_GROUPS, 4, LRU_BLOCK, 1, LRU_BLOCK)
    eye = jnp.eye(4, dtype=w.dtype).reshape(1, 4, 1, 4, 1)
    return (w4 * eye).reshape(N_LRU_GROUPS, LRU_GROUP, LRU_GROUP)


def _block_diag_extract(wbd):
    w5 = wbd.reshape(N_LRU_GROUPS, 4, LRU_BLOCK, 4, LRU_BLOCK)
    return jnp.stack([w5[:, a, :, a, :] for a in range(4)], axis=1).reshape(N_LRU_BLOCKS, LRU_BLOCK, LRU_BLOCK)


def _t5_bucket(dist):
    max_exact = NUM_BUCKETS // 2
    df = jnp.maximum(dist, 1).astype(jnp.float32)
    large = max_exact + (jnp.log(df / max_exact) / math.log(MAX_DISTANCE / max_exact)
                         * (NUM_BUCKETS - max_exact)).astype(jnp.int32)
    large = jnp.minimum(large, NUM_BUCKETS - 1)
    return jnp.where(dist < max_exact, dist, large)


def _band_offsets():
    qi = jnp.arange(SPAN)[:, None]
    kj = jnp.arange(2 * SPAN)[None, :]
    return qi + SPAN - kj


def _dil_buckets():
    off = _band_offsets()
    return jnp.stack([_t5_bucket(jnp.maximum(off, 0) * dil) for _, dil in DIL_GROUPS]).astype(jnp.int32)


def _dil_bias(rel_bias, buckets):
    def body(tbl_ref, bk_ref, o_ref):
        g = pl.program_id(0)
        qi = lax.broadcasted_iota(jnp.int32, (SPAN, 2 * SPAN), 0)
        kj = lax.broadcasted_iota(jnp.int32, (SPAN, 2 * SPAN), 1)
        off = qi + SPAN - kj
        valid = (off >= 0) & (off <= SPAN)
        bk = bk_ref[0]
        for h in range(DIL_HEADS):
            acc = jnp.zeros((SPAN, 2 * SPAN), F32)
            for b in range(NUM_BUCKETS):
                acc = jnp.where(bk == b, tbl_ref[b, g * DIL_HEADS + h], acc)
            o_ref[0, h] = jnp.where(valid, acc, NEG)

    return pl.pallas_call(
        body, name="dil_bias", grid=(3,),
        in_specs=[pl.BlockSpec(memory_space=pltpu.SMEM), pl.BlockSpec((1, SPAN, 2 * SPAN), lambda g: (g, 0, 0))],
        out_specs=pl.BlockSpec((1, DIL_HEADS, SPAN, 2 * SPAN), lambda g: (g, 0, 0, 0)),
        out_shape=jax.ShapeDtypeStruct((3, DIL_HEADS, SPAN, 2 * SPAN), F32),
        compiler_params=_cparams(("parallel",)),
    )(rel_bias, buckets)


def _dil_bias_bwd(dbias, buckets):
    def body(db_ref, bk_ref, o_ref):
        lane = lax.broadcasted_iota(jnp.int32, (1, 128), 1)
        rows = [jnp.zeros((1, 128), F32) for _ in range(NUM_BUCKETS)]
        for g in range(3):
            bk = bk_ref[g]
            for h in range(DIL_HEADS):
                d = db_ref[g, h]
                for b in range(NUM_BUCKETS):
                    tot = jnp.sum(_colsum(jnp.where(bk == b, d, 0.0)), axis=1, keepdims=True)
                    rows[b] = jnp.where(lane == g * DIL_HEADS + h, tot, rows[b])
        for b in range(NUM_BUCKETS):
            o_ref[b:b + 1, :] = rows[b]

    return pl.pallas_call(
        body, name="dil_bias_bwd",
        out_shape=jax.ShapeDtypeStruct((NUM_BUCKETS, 128), F32),
        compiler_params=_cparams(),
    )(dbias, buckets)


DIL_SUBBLOCKS = (8, 4, 1)


def _dil_layout(g, S):
    dil, m = DIL_GROUPS[g][1], DIL_SUBBLOCKS[g]
    sub = SPAN * dil
    col = [(C_QKV + t * 768 + g * 256) // 128 for t in range(3)]
    return dil, m, sub, S // (sub * m), col


def _residue_rows(b, r, dil):
    return pl.ds(b * SPAN * dil + r, SPAN, stride=dil) if dil > 1 else pl.ds(b * SPAN, SPAN)


def _for_residues(dil, fn):
    if dil <= 4:
        for r in range(dil):
            fn(r)
    else:
        lax.fori_loop(0, dil, lambda r, c: (fn(r), c)[1], 0, unroll=4)


def _pair_scores(qm, k2, bias, first_cols):
    s = _dot(qm, k2, "nt") * (DIL_HEAD_DIM ** -0.5) + bias
    kj = lax.broadcasted_iota(jnp.int32, s.shape, 1)
    return jnp.where(kj < first_cols, NEG, s)


def _dilated_fwd(proj, bias, g, *, S):
    dil, m, sub, nc, (qc, kc, vc) = _dil_layout(g, S)
    R = sub * m
    cur = lambda cb: pl.BlockSpec((R, 128), lambda p, i: (i, cb + p))
    prv = lambda cb: pl.BlockSpec((sub, 128), lambda p, i: (jnp.maximum(i * m - 1, 0), cb + p))
    out = pl.BlockSpec((R, 128), lambda p, i: (i, p))

    def body(q_ref, kp_ref, kc_ref, vp_ref, vc_ref, b_ref, o_ref, lse_ref):
        lane = lax.broadcasted_iota(jnp.int32, (SPAN, 128), 1)
        sels = (lane < DIL_HEAD_DIM, lane >= DIL_HEAD_DIM)
        for b in range(m):
            first_cols = jnp.where(pl.program_id(1) == 0, SPAN, 0) if b == 0 else 0

            def one(r, b=b, first_cols=first_cols):
                rows = _residue_rows(b, r, dil)
                before = (kc_ref, vc_ref, _residue_rows(b - 1, r, dil)) if b else (kp_ref, vp_ref, _residue_rows(0, r, dil))
                q2 = q_ref[rows, :]
                k2 = _mx(jnp.concatenate([before[0][before[2], :], kc_ref[rows, :]], axis=0))
                v2 = _mx(jnp.concatenate([before[1][before[2], :], vc_ref[rows, :]], axis=0))
                qq = jnp.concatenate([jnp.where(sels[0], q2, 0.0), jnp.where(sels[1], q2, 0.0)], axis=0)
                s = _pair_scores(qq, k2, b_ref[0, 0], first_cols)
                mx = jnp.max(s, axis=-1, keepdims=True)
                p = jnp.exp(s - mx)
                den = jnp.sum(p, axis=-1, keepdims=True)
                o = _dot(p, v2) / den
                st = mx + jnp.log(den)
                o_ref[rows, :] = jnp.where(sels[0], o[0:SPAN], o[SPAN:2 * SPAN])
                lse_ref[rows, :] = jnp.where(lane == 0, st[0:SPAN], jnp.where(lane == 1, st[SPAN:2 * SPAN], 0.0))

            _for_residues(dil, one)

    return pl.pallas_call(
        body, name=f"dil_fwd{g}", grid=(2, nc),
        in_specs=[cur(qc), prv(kc), cur(kc), prv(vc), cur(vc),
                  pl.BlockSpec((1, 1, 2 * SPAN, 2 * SPAN), lambda p, i: (g, p, 0, 0))],
        out_specs=[out, out],
        out_shape=[jax.ShapeDtypeStruct((S, 256), F32), jax.ShapeDtypeStruct((S, 256), F32)],
        compiler_params=_cparams(("parallel", "parallel")),
    )(proj, proj, proj, proj, proj, bias.reshape(3, 2, 2 * SPAN, 2 * SPAN))


def _dilated_bwd(proj, do, lse, delta, bias, g, *, S, into=None):
    dil, m, sub, nc, (qc, kc, vc) = _dil_layout(g, S)
    R = sub * m
    cl = lambda i: jnp.minimum(i, nc - 1)
    cur = lambda cb: pl.BlockSpec((R, 128), lambda p, i: (cl(i), cb + p))
    prv = lambda cb: pl.BlockSpec((sub, 128), lambda p, i: (jnp.maximum(cl(i) * m - 1, 0), cb + p))
    q_out = pl.BlockSpec((R, 128), lambda p, i: (cl(i), 2 * g + p))
    kv_out = pl.BlockSpec((R, 128), lambda p, i: (jnp.maximum(i - 1, 0), 2 * g + p))
    scale = DIL_HEAD_DIM ** -0.5
    n_into = 0 if into is None else 3

    def body(q_ref, kp_ref, kc_ref, vp_ref, vc_ref, do_ref, lse_ref, dl_ref, b_ref, *rest):
        dq_ref, dk_ref, dv_ref, db_ref, dq_s, kc_s, vc_s, kp_s, vp_s, kcar, vcar = rest[n_into:]
        i = pl.program_id(1)

        @pl.when(i == 0)
        def _():
            kcar[...] = jnp.zeros_like(kcar)
            vcar[...] = jnp.zeros_like(vcar)
            db_ref[...] = jnp.zeros_like(db_ref)

        @pl.when(i < nc)
        def _():
            lane = lax.broadcasted_iota(jnp.int32, (SPAN, 128), 1)
            sels = (lane < DIL_HEAD_DIM, lane >= DIL_HEAD_DIM)
            for b in range(m):
                first_cols = jnp.where(i == 0, SPAN, 0) if b == 0 else 0

                def one(r, b=b, first_cols=first_cols):
                    rows = _residue_rows(b, r, dil)
                    rows_before = _residue_rows(b - 1 if b else 0, r, dil)
                    k_before, v_before = (kc_ref, vc_ref) if b else (kp_ref, vp_ref)
                    q2, do2 = q_ref[rows, :], do_ref[rows, :]
                    k2 = _mx(jnp.concatenate([k_before[rows_before, :], kc_ref[rows, :]], axis=0))
                    v2 = _mx(jnp.concatenate([v_before[rows_before, :], vc_ref[rows, :]], axis=0))
                    lse_t, dl_t = lse_ref[rows, :], dl_ref[rows, :]
                    qq = _mx(jnp.concatenate([jnp.where(sels[0], q2, 0.0), jnp.where(sels[1], q2, 0.0)], axis=0))
                    dd = _mx(jnp.concatenate([jnp.where(sels[0], do2, 0.0), jnp.where(sels[1], do2, 0.0)], axis=0))
                    lse2 = jnp.concatenate([lse_t[:, 0:1], lse_t[:, 1:2]], axis=0)
                    dl2 = jnp.concatenate([dl_t[:, 0:1], dl_t[:, 1:2]], axis=0)
                    p = jnp.exp(_pair_scores(qq, k2, b_ref[0, 0], first_cols) - lse2)
                    ds = p * (_dot(dd, v2, "nt") - dl2)
                    db_ref[0] += ds
                    dqq = _dot(ds, k2) * scale
                    dq2 = jnp.where(sels[0], dqq[0:SPAN], dqq[SPAN:2 * SPAN])
                    dk2 = _dot(ds, qq, "tn") * scale
                    dv2 = _dot(p, dd, "tn")
                    dq_s[rows, :] = dq2
                    kc_s[rows, :] = dk2[SPAN:2 * SPAN]
                    vc_s[rows, :] = dv2[SPAN:2 * SPAN]
                    if b:
                        kc_s[rows_before, :] += dk2[0:SPAN]
                        vc_s[rows_before, :] += dv2[0:SPAN]
                    else:
                        kp_s[rows_before, :] = dk2[0:SPAN]
                        vp_s[rows_before, :] = dv2[0:SPAN]

                _for_residues(dil, one)
            dq_ref[...] = dq_s[...].astype(dq_ref.dtype)
            last = pl.ds((m - 1) * sub, sub)
            kcar[last, :] += kp_s[...]
            vcar[last, :] += vp_s[...]
            dk_ref[...] = kcar[...].astype(dk_ref.dtype)
            dv_ref[...] = vcar[...].astype(dv_ref.dtype)
            kcar[...] = kc_s[...]
            vcar[...] = vc_s[...]

        @pl.when(i == nc)
        def _():
            dk_ref[...] = kcar[...].astype(dk_ref.dtype)
            dv_ref[...] = vcar[...].astype(dv_ref.dtype)

    stat = pl.BlockSpec((R, 128), lambda p, i: (cl(i), p))
    big = jax.ShapeDtypeStruct((S, len(DIL_GROUPS) * 256), MXU_DTYPE)
    return pl.pallas_call(
        body, name=f"dil_bwd{g}", grid=(2, nc + 1),
        in_specs=[cur(qc), prv(kc), cur(kc), prv(vc), cur(vc), stat, stat, stat,
                  pl.BlockSpec((1, 1, 2 * SPAN, 2 * SPAN), lambda p, i: (g, p, 0, 0))]
        + [pl.BlockSpec(memory_space=pl.ANY)] * n_into,
        out_specs=[q_out, kv_out, kv_out, pl.BlockSpec((1, 2 * SPAN, 2 * SPAN), lambda p, i: (p, 0, 0))],
        out_shape=[big, big, big, jax.ShapeDtypeStruct((2, 2 * SPAN, 2 * SPAN), F32)],
        input_output_aliases={9 + j: j for j in range(n_into)},
        scratch_shapes=[pltpu.VMEM((R, 128), F32)] * 3 + [pltpu.VMEM((sub, 128), F32)] * 2 + [pltpu.VMEM((R, 128), F32)] * 2,
        compiler_params=_cparams(("parallel", "arbitrary")),
    )(proj, proj, proj, proj, proj, do, lse, delta, bias.reshape(3, 2, 2 * SPAN, 2 * SPAN), *(into or ()))


def _dilated_merge(os_, lses, *, S, bt=512):
    tile = pl.BlockSpec((bt, 128), lambda i, p: (i, p))

    def body(o0, o1, o2, l0, l1, l2, o_ref, om_ref, lse_ref):
        lane = lax.broadcasted_iota(jnp.int32, (bt, 128), 1)
        lo = lane < DIL_HEAD_DIM
        ls = [l0[...], l1[...], l2[...]]
        ws, stat = [], jnp.zeros((bt, 128), F32)
        for e in range(2):
            a = [l[:, e:e + 1] for l in ls]
            m = jnp.maximum(jnp.maximum(a[0], a[1]), a[2])
            ex = [jnp.exp(v - m) for v in a]
            tot = ex[0] + ex[1] + ex[2]
            ws.append([v / tot for v in ex])
            stat = jnp.where(lane == e, m + jnp.log(tot), stat)
        acc = jnp.zeros((bt, 128), F32)
        for gi, o in enumerate((o0, o1, o2)):
            acc = acc + jnp.where(lo, ws[0][gi], ws[1][gi]) * o[...]
        o_ref[...] = acc
        om_ref[...] = _mx(acc)
        lse_ref[...] = stat

    return pl.pallas_call(
        body, name="dil_merge", grid=(S // bt, 2),
        in_specs=[tile] * 6, out_specs=[tile, tile, tile],
        out_shape=[jax.ShapeDtypeStruct((S, 256), F32), jax.ShapeDtypeStruct((S, 256), MXU_DTYPE),
                   jax.ShapeDtypeStruct((S, 256), F32)],
        compiler_params=_cparams(("parallel", "parallel")),
    )(*os_, *lses)


def _with_delta(do, o):
    lane = lax.broadcasted_iota(jnp.int32, (do.shape[0], 128), 1)
    stats = []
    for p in range(2):
        prod = do[:, 128 * p:128 * (p + 1)] * o[:, 128 * p:128 * (p + 1)]
        d0 = jnp.sum(jnp.where(lane < DIL_HEAD_DIM, prod, 0.0), axis=-1, keepdims=True)
        d1 = jnp.sum(jnp.where(lane >= DIL_HEAD_DIM, prod, 0.0), axis=-1, keepdims=True)
        stats.append(jnp.where(lane == 0, d0, jnp.where(lane == 1, d1, 0.0)))
    return do, jnp.concatenate(stats, axis=1)


MEM_T = 2048
QM_BLK = C_QM // MEM_HEAD_DIM


def _mem_attn_fwd(proj, kv, *, S):
    scale = MEM_HEAD_DIM ** -0.5

    def body(q_ref, k_ref, v_ref, o_ref, om_ref, lse_ref):
        s = _dot(q_ref[...], k_ref[...], "nt") * scale
        m = jnp.max(s, axis=-1, keepdims=True)
        p = jnp.exp(s - m)
        den = jnp.sum(p, axis=-1, keepdims=True)
        o = _dot(p, v_ref[...]) / den
        o_ref[...] = o
        om_ref[...] = _mx(o)
        lse_ref[0] = m + jnp.log(den)

    return pl.pallas_call(
        body, name="mem_attn_fwd", grid=(S // MEM_T, MEM_HEADS),
        in_specs=[pl.BlockSpec((MEM_T, MEM_HEAD_DIM), lambda i, h: (i, QM_BLK + h)),
                  pl.BlockSpec((N_MEM, MEM_HEAD_DIM), lambda i, h: (0, h)),
                  pl.BlockSpec((N_MEM, MEM_HEAD_DIM), lambda i, h: (0, MEM_HEADS + h))],
        out_specs=[pl.BlockSpec((MEM_T, MEM_HEAD_DIM), lambda i, h: (i, h)),
                   pl.BlockSpec((MEM_T, MEM_HEAD_DIM), lambda i, h: (i, h)),
                   pl.BlockSpec((1, MEM_T, 1), lambda i, h: (h, i, 0))],
        out_shape=[jax.ShapeDtypeStruct((S, MEM_WIDTH), F32), jax.ShapeDtypeStruct((S, MEM_WIDTH), MXU_DTYPE),
                   jax.ShapeDtypeStruct((MEM_HEADS, S, 1), F32)],
        compiler_params=_cparams(("parallel", "parallel")),
    )(proj, kv, kv)


def _mem_attn_bwd(proj, kv, om, lse, dom, *, S):
    scale = MEM_HEAD_DIM ** -0.5

    def body(q_ref, k_ref, v_ref, o_ref, lse_ref, do_ref, dq_ref, dk_ref, dv_ref):
        @pl.when(pl.program_id(1) == 0)
        def _():
            dk_ref[...] = jnp.zeros_like(dk_ref)
            dv_ref[...] = jnp.zeros_like(dv_ref)

        qv, kv_, vv, dov = q_ref[...], k_ref[...], v_ref[...], do_ref[...]
        p = jnp.exp(_dot(qv, kv_, "nt") * scale - lse_ref[0])
        delta = jnp.sum(dov * o_ref[...], axis=-1, keepdims=True)
        ds = p * (_dot(dov, vv, "nt") - delta)
        dq_ref[...] = (_dot(ds, kv_) * scale).astype(dq_ref.dtype)
        dk_ref[...] += _dot(ds, qv, "tn") * scale
        dv_ref[...] += _dot(p, dov, "tn")

    tile = pl.BlockSpec((MEM_T, MEM_HEAD_DIM), lambda h, i: (i, h))
    kvo = pl.BlockSpec((N_MEM, MEM_HEAD_DIM), lambda h, i: (0, h))
    return pl.pallas_call(
        body, name="mem_attn_bwd", grid=(MEM_HEADS, S // MEM_T),
        in_specs=[pl.BlockSpec((MEM_T, MEM_HEAD_DIM), lambda h, i: (i, QM_BLK + h)),
                  pl.BlockSpec((N_MEM, MEM_HEAD_DIM), lambda h, i: (0, h)),
                  pl.BlockSpec((N_MEM, MEM_HEAD_DIM), lambda h, i: (0, MEM_HEADS + h)),
                  tile, pl.BlockSpec((1, MEM_T, 1), lambda h, i: (h, i, 0)), tile],
        out_specs=[tile, kvo, kvo],
        out_shape=[jax.ShapeDtypeStruct((S, MEM_WIDTH), MXU_DTYPE), jax.ShapeDtypeStruct((N_MEM, MEM_WIDTH), F32),
                   jax.ShapeDtypeStruct((N_MEM, MEM_WIDTH), F32)],
        compiler_params=_cparams(("parallel", "arbitrary")),
    )(proj, kv, kv, om, lse, dom)


MIX_BM = 1024
MIX_BN = 256
GATES_BLK = C_GATES // MIX_BN


def _mix_specs(j_outer):
    ix = (lambda f: (lambda j, i: f(i, j))) if j_outer else (lambda f: f)
    act = lambda width: pl.BlockSpec((MIX_BM, width), ix(lambda i, j: (i, 0)))
    wgt = lambda width: pl.BlockSpec((width, MIX_BN), ix(lambda i, j: (0, j)))
    gate = lambda b: pl.BlockSpec((MIX_BM, MIX_BN), ix(lambda i, j: (i, GATES_BLK + 4 * b + j)))
    bias = lambda b: pl.BlockSpec((1, MIX_BN), ix(lambda i, j: (0, 4 * b + j)))
    tile = pl.BlockSpec((MIX_BM, MIX_BN), ix(lambda i, j: (i, j)))
    return act, wgt, gate, bias, tile


def _mix_fwd(z_lru, o_dil, om, w_lru, w_dil, w_mem, proj, b_gate, *, S):
    act, wgt, gate, bias, tile = _mix_specs(False)

    def body(zl, od, mo, wl, wd, wm, g0, g1, g2, b0, b1, b2, o_ref):
        acc = jax.nn.sigmoid(g0[...] + b0[...]) * _dot(zl[...], wl[...])
        acc += jax.nn.sigmoid(g1[...] + b1[...]) * _dot(od[...], wd[...])
        acc += jax.nn.sigmoid(g2[...] + b2[...]) * _dot(mo[...], wm[...])
        o_ref[...] = acc.astype(o_ref.dtype)

    return pl.pallas_call(
        body, name="mix_fwd", grid=(S // MIX_BM, D_MODEL // MIX_BN),
        in_specs=[act(D_RNN), act(256), act(MEM_WIDTH), wgt(D_RNN), wgt(256), wgt(MEM_WIDTH),
                  gate(0), gate(1), gate(2), bias(0), bias(1), bias(2)],
        out_specs=tile, out_shape=jax.ShapeDtypeStruct((S, D_MODEL), MXU_DTYPE),
        compiler_params=_cparams(("parallel", "parallel")),
    )(z_lru, o_dil, om, w_lru, w_dil, w_mem, proj, proj, proj, b_gate, b_gate, b_gate)


def _mix_bwd(dx1, w_out, z_lru, o_dil, om, w_lru, w_dil, w_mem, proj, b_gate, *, S):
    act, wgt, gate, bias, tile = _mix_specs(False)
    n_j = D_MODEL // MIX_BN

    def body(dx, wo, zl, od, mo, wl, wd, wm, g0, g1, g2, b0, b1, b2,
             dg0, dg1, dg2, dy0, dy1, dy2, db0, db1, db2):
        j = pl.program_id(1)

        @pl.when((pl.program_id(0) == 0) & (j == 0))
        def _():
            for r in (db0, db1, db2):
                r[...] = jnp.zeros_like(r)

        dmv = _dot(dx[...], wo[...], "nt")
        for act_ref, w_ref, g_ref, b_ref, dg_ref, dy_ref, db_ref in (
                (zl, wl, g0, b0, dg0, dy0, db0), (od, wd, g1, b1, dg1, dy1, db1), (mo, wm, g2, b2, dg2, dy2, db2)):
            y = _dot(act_ref[...], w_ref[...])
            gt = jax.nn.sigmoid(g_ref[...] + b_ref[...])
            dgate = dmv * y * gt * (1.0 - gt)
            dg_ref[...] = dgate.astype(dg_ref.dtype)
            dy_ref[...] = (dmv * gt).astype(dy_ref.dtype)
            db_ref[j] += _colsum(dgate)

    big = jax.ShapeDtypeStruct((S, D_MODEL), MXU_DTYPE)
    vec = jax.ShapeDtypeStruct((n_j, 1, MIX_BN), F32)
    vspec = pl.BlockSpec((n_j, 1, MIX_BN), lambda i, j: (0, 0, 0))
    res = pl.pallas_call(
        body, name="mix_bwd", grid=(S // MIX_BM, n_j),
        in_specs=[pl.BlockSpec((MIX_BM, D_MODEL), lambda i, j: (i, 0)), pl.BlockSpec((MIX_BN, D_MODEL), lambda i, j: (j, 0)),
                  act(D_RNN), act(256), act(MEM_WIDTH), wgt(D_RNN), wgt(256), wgt(MEM_WIDTH),
                  gate(0), gate(1), gate(2), bias(0), bias(1), bias(2)],
        out_specs=[tile] * 6 + [vspec] * 3, out_shape=[big] * 6 + [vec] * 3,
        compiler_params=_cparams(("arbitrary", "arbitrary")),
    )(dx1, w_out, z_lru, o_dil, om, w_lru, w_dil, w_mem, proj, proj, proj, b_gate, b_gate, b_gate)
    return list(res[:6]) + [r.reshape(1, D_MODEL) for r in res[6:]]


def _adamw_math(w, g, m, v):
    m = ADAM_B1 * m + (1.0 - ADAM_B1) * g
    v = ADAM_B2 * v + (1.0 - ADAM_B2) * (g * g)
    m_hat = m / (1.0 - ADAM_B1 ** ADAM_STEP)
    v_hat = v / (1.0 - ADAM_B2 ** ADAM_STEP)
    delta = -ADAM_LR * (m_hat / (jnp.sqrt(v_hat) + ADAM_EPS) + ADAM_WD * w)
    return delta, m, v


def _adamw_landed(w, own, land, m, v, *, name, col_blk=0, prev=None):
    R = w.shape[0]
    n_parts, C = land.shape[0], land.shape[2]
    br = next(d for d in (256, 464, 128) if R % d == 0)
    tile = pl.BlockSpec((br, C), lambda i: (i, col_blk))
    part = pl.BlockSpec((br, C), lambda i: (i, 0))
    n_prev = 0 if prev is None else 4

    def body(w_ref, o_ref, l_ref, m_ref, v_ref, *rest):
        g_ref, d_ref, nm_ref, nv_ref = rest[n_prev:]
        g = o_ref[...].astype(F32)
        for p in range(n_parts):
            g = g + l_ref[p].astype(F32)
        d, nm, nv = _adamw_math(w_ref[...], g, m_ref[...], v_ref[...])
        g_ref[...] = g
        d_ref[...] = d
        nm_ref[...] = nm
        nv_ref[...] = nv

    return pl.pallas_call(
        body, name=name, grid=(R // br,),
        in_specs=[tile, part, pl.BlockSpec((n_parts, br, C), lambda i: (0, i, 0)), tile, tile]
        + [pl.BlockSpec(memory_space=pl.ANY)] * n_prev,
        out_specs=[tile] * 4, out_shape=[jax.ShapeDtypeStruct(w.shape, F32)] * 4,
        input_output_aliases={5 + j: j for j in range(n_prev)},
        compiler_params=_cparams(("parallel",)),
    )(w, own, land, m, v, *(prev or ()))


def _adamw_plain(w, g, m, v, *, name):
    def body(w_ref, g_ref, m_ref, v_ref, d_ref, nm_ref, nv_ref):
        d, nm, nv = _adamw_math(w_ref[...], g_ref[...], m_ref[...], v_ref[...])
        d_ref[...] = d
        nm_ref[...] = nm
        nv_ref[...] = nv

    return pl.pallas_call(
        body, name=name, out_shape=[jax.ShapeDtypeStruct(w.shape, F32)] * 3, compiler_params=_cparams(),
    )(w, g, m, v)


def _my_pos():
    return lax.axis_index("x"), lax.axis_index("y"), lax.axis_index("c")


def _dev_index(p):
    return 4 * p[0] + 2 * p[1] + p[2]


def _all_gather(shards):
    n = len(shards)
    hbm = pl.BlockSpec(memory_space=pl.ANY)

    def body(*refs):
        ins, outs = refs[:n], refs[n:2 * n]
        send_sems, recv_sems, local_sems = refs[2 * n:]
        x, y, c = _my_pos()
        me, sibling = (x, y, c), (x, y, 1 - c)
        chips = [(1 - x, y), (x, 1 - y), (1 - x, 1 - y)]

        def copy(a, k, block, to, src=None):
            dst = outs[a].at[_dev_index(block)]
            return pltpu.make_async_remote_copy(
                src_ref=dst if src is None else src, dst_ref=dst,
                send_sem=send_sems.at[a, k], recv_sem=recv_sems.at[a, k], device_id=to, device_id_type=MESH)

        mine = [pltpu.make_async_copy(ins[a], outs[a].at[_dev_index(me)], local_sems.at[a]) for a in range(n)]
        for cp in mine:
            cp.start()
        first = []
        for a in range(n):
            first.append(copy(a, 0, me, sibling, src=ins[a]))
            first += [copy(a, 1 + j, me, (*chip, c), src=ins[a]) for j, chip in enumerate(chips)]
        for cp in first:
            cp.start()
        passed = []
        for j, chip in enumerate(chips):
            for a in range(n):
                copy(a, 1 + j, (*chip, c), me).wait_recv()
                fwd = copy(a, 4 + j, (*chip, c), sibling)
                fwd.start()
                passed.append(fwd)
        for a in range(n):
            copy(a, 0, sibling, me).wait_recv()
        for j, chip in enumerate(chips):
            for a in range(n):
                copy(a, 4 + j, (*chip, 1 - c), me).wait_recv()
        for cp in first + passed:
            cp.wait_send()
        for cp in mine:
            cp.wait()

    return pl.pallas_call(
        body, name="all_gather_weights",
        in_specs=[hbm] * n, out_specs=[hbm] * n,
        out_shape=[jax.ShapeDtypeStruct((N_DEV,) + s.shape, s.dtype) for s in shards],
        scratch_shapes=[pltpu.SemaphoreType.DMA((n, 7)), pltpu.SemaphoreType.DMA((n, 7)), pltpu.SemaphoreType.DMA((n,))],
        compiler_params=pltpu.CompilerParams(has_side_effects=True),
    )(*shards)


def _peers(me):
    x, y, c = me
    out = []
    for k in range(1, 8):
        fx, fy, fc = (k >> 2) & 1, (k >> 1) & 1, k & 1
        out.append((k - 1, (1 - x if fx else x, 1 - y if fy else y, 1 - c if fc else c)))
    return out


HBM_SPEC = pl.BlockSpec(memory_space=pltpu.HBM)
SEM_SPEC = pl.BlockSpec(memory_space=pltpu.SEMAPHORE)
DATAFLOW_EFFECT = pltpu.SideEffectType.DATAFLOW_SIDE_EFFECTING


def _gather_refs(src, land, me, peer, k):
    return src, land.at[_dev_index(me)]


def _scatter_refs(src, land, me, peer, k):
    return src.at[_dev_index(peer)], land.at[k]


ALL_RELATIONS = tuple(range(7))
ONE_PER_CHIP = (0, 1, 3, 5)


def _push_start(srcs, land_shapes, refs_of, name, after=(), relations=ALL_RELATIONS):
    n, n_after = len(srcs), len(after)

    def body(*refs):
        ins, lands = refs[:n], refs[n:2 * n]
        send_sems, recv_sems, token = refs[2 * n + n_after], refs[2 * n + n_after + 1], refs[-1]
        me = _my_pos()
        for k, peer in _peers(me):
            if k not in relations:
                continue
            for a in range(n):
                src, dst = refs_of(ins[a], lands[a], me, peer, k)
                pltpu.make_async_remote_copy(src_ref=src, dst_ref=dst, send_sem=send_sems.at[7 * a + k],
                                             recv_sem=recv_sems.at[7 * a + k], device_id=peer, device_id_type=MESH).start()
        token[...] = jnp.zeros_like(token)

    lands = [lax.empty(shp, s.dtype) for shp, s in zip(land_shapes, srcs)]
    hbm = lambda a: pltpu.with_memory_space_constraint(a, pltpu.HBM)
    res = pl.pallas_call(
        body, name=name,
        out_shape=(pltpu.SemaphoreType.DMA((7 * n,)), pltpu.SemaphoreType.DMA((7 * n,)),
                   *[pltpu.HBM(s.shape, s.dtype) for s in srcs], *[pltpu.HBM(l.shape, l.dtype) for l in lands],
                   jax.ShapeDtypeStruct((8, 128), F32)),
        in_specs=[HBM_SPEC] * (2 * n) + [pl.BlockSpec(memory_space=pl.ANY)] * n_after,
        out_specs=(SEM_SPEC, SEM_SPEC, *[HBM_SPEC] * (2 * n), pl.BlockSpec(memory_space=pltpu.VMEM)),
        input_output_aliases={i: 2 + i for i in range(2 * n)},
        compiler_params=pltpu.CompilerParams(has_side_effects=DATAFLOW_EFFECT),
    )(*[hbm(s) for s in srcs], *[hbm(l) for l in lands], *after)
    return dict(sems=(res[0], res[1]), srcs=list(res[2:2 + n]), lands=list(res[2 + n:2 + 2 * n]), token=res[-1], n=n,
                refs_of=refs_of, name=name, relations=relations)


def _push_wait(started, after):
    n, refs_of, relations = started["n"], started["refs_of"], started["relations"]
    after = list(after) if isinstance(after, (list, tuple)) else [after]

    def body(*refs):
        ins, lands = refs[:n], refs[n:2 * n]
        send_sems, recv_sems = refs[2 * n], refs[2 * n + 1]
        me = _my_pos()
        for k, peer in _peers(me):
            if k not in relations:
                continue
            for a in range(n):
                src, dst = refs_of(ins[a], lands[a], me, peer, k)
                cp = pltpu.make_async_remote_copy(src_ref=src, dst_ref=dst, send_sem=send_sems.at[7 * a + k],
                                                  recv_sem=recv_sems.at[7 * a + k], device_id=peer, device_id_type=MESH)
                cp.wait_send()
                cp.wait_recv()

    arrs = started["srcs"] + started["lands"]
    res = pl.pallas_call(
        body, name=started["name"].replace("start", "wait"),
        out_shape=tuple(pltpu.HBM(a.shape, a.dtype) for a in arrs),
        in_specs=[HBM_SPEC] * (2 * n) + [SEM_SPEC, SEM_SPEC] + [pl.BlockSpec(memory_space=pl.ANY)] * len(after),
        out_specs=tuple([HBM_SPEC] * (2 * n)),
        input_output_aliases={i: i for i in range(2 * n)},
        compiler_params=pltpu.CompilerParams(has_side_effects=DATAFLOW_EFFECT),
    )(*arrs, *started["sems"], *after)
    return list(res[n:2 * n])


def _other_chips(x, y):
    return ((1 - x, y), (x, 1 - y), (1 - x, 1 - y))


def _forward_start(land, name, after=()):
    n_after = len(after)

    def body(*refs):
        land_ref, send_sems, recv_sems, token = refs[0], refs[1 + n_after], refs[2 + n_after], refs[-1]
        x, y, c = _my_pos()
        for j, (cx, cy) in enumerate(_other_chips(x, y)):
            blk = land_ref.at[_dev_index((cx, cy, c))]
            pltpu.make_async_remote_copy(src_ref=blk, dst_ref=blk, send_sem=send_sems.at[j], recv_sem=recv_sems.at[j],
                                         device_id=(x, y, 1 - c), device_id_type=MESH).start()
        token[...] = jnp.zeros_like(token)

    res = pl.pallas_call(
        body, name=name,
        out_shape=(pltpu.SemaphoreType.DMA((3,)), pltpu.SemaphoreType.DMA((3,)), pltpu.HBM(land.shape, land.dtype),
                   jax.ShapeDtypeStruct((8, 128), F32)),
        in_specs=[HBM_SPEC] + [pl.BlockSpec(memory_space=pl.ANY)] * n_after,
        out_specs=(SEM_SPEC, SEM_SPEC, HBM_SPEC, pl.BlockSpec(memory_space=pltpu.VMEM)),
        input_output_aliases={0: 2},
        compiler_params=pltpu.CompilerParams(has_side_effects=DATAFLOW_EFFECT),
    )(pltpu.with_memory_space_constraint(land, pltpu.HBM), *after)
    return dict(sems=(res[0], res[1]), land=res[2], token=res[3], name=name)


def _forward_wait(started, after):
    after = list(after) if isinstance(after, (list, tuple)) else [after]

    def body(land_ref, send_sems, recv_sems, *rest):
        x, y, c = _my_pos()
        for j, (cx, cy) in enumerate(_other_chips(x, y)):
            cp = pltpu.make_async_remote_copy(
                src_ref=land_ref.at[_dev_index((cx, cy, c))], dst_ref=land_ref.at[_dev_index((cx, cy, 1 - c))],
                send_sem=send_sems.at[j], recv_sem=recv_sems.at[j], device_id=(x, y, 1 - c), device_id_type=MESH)
            cp.wait_send()
            cp.wait_recv()

    land = started["land"]
    return pl.pallas_call(
        body, name=started["name"].replace("start", "wait"), out_shape=pltpu.HBM(land.shape, land.dtype),
        in_specs=[HBM_SPEC, SEM_SPEC, SEM_SPEC] + [pl.BlockSpec(memory_space=pl.ANY)] * len(after),
        out_specs=HBM_SPEC, input_output_aliases={0: 0},
        compiler_params=pltpu.CompilerParams(has_side_effects=DATAFLOW_EFFECT),
    )(land, *started["sems"], *after)


def _sum_slots(slots):
    def body(in_ref, out_ref):
        acc = in_ref[0]
        for d in range(1, N_DEV):
            acc = acc + in_ref[d]
        out_ref[...] = acc

    return pl.pallas_call(body, name="sum_small", out_shape=jax.ShapeDtypeStruct(slots.shape[1:], F32),
                          compiler_params=_cparams())(slots)


def _adamw_many(ws, gs, ms, vs):
    n = len(ws)

    def body(*refs):
        for i in range(n):
            w_ref, g_ref, m_ref, v_ref = (refs[j * n + i] for j in range(4))
            d_, nm, nv = _adamw_math(w_ref[...], g_ref[...], m_ref[...], v_ref[...])
            for j, val in enumerate((d_, nm, nv)):
                refs[(4 + j) * n + i][...] = val

    res = pl.pallas_call(body, name="adamw_small", out_shape=[jax.ShapeDtypeStruct(w_.shape, F32) for w_ in ws] * 3,
                         compiler_params=_cparams())(*ws, *gs, *ms, *vs)
    return [(res[i], res[n + i], res[2 * n + i]) for i in range(n)]


def _local_step(x, mem, tgt, W, P, late_weights, send_grads, reduce_small, tie0):
    S = x.shape[0]
    W = dict(W)
    h = _rmsnorm_fwd(x, P["g_mix"] + tie0, rows=S, name="norm_mix")
    mem_n = _rmsnorm_fwd(mem, P["g_mem"], rows=N_MEM, name="norm_mem")
    buckets = _dil_buckets()
    bias = _dil_bias(P["rel_bias"], buckets)
    wa_bd, wx_bd = _mx(_block_diag(P["w_rg_a"])), _mx(_block_diag(P["w_rg_x"]))
    W.update(late_weights("first", [h, mem_n, bias, wa_bd, wx_bd]))
    proj = _matmul(h, W["w_in_t"], M=S, N=D_IN, K=D_MODEL, mode="nt", bm=512, bn=D_IN // 2, bk=D_MODEL, name="mm_in",
                   j_outer=True)

    group_out = [_dilated_fwd(proj, bias, g, S=S) for g in range(len(DIL_GROUPS))]
    o_dil, o_dil_m, lse_dil = _dilated_merge([o for o, _ in group_out], [l for _, l in group_out], S=S)

    W.update(late_weights("branch", [o_dil]))
    lru_args = (W["conv_w"], P["conv_b"].reshape(1, -1), wa_bd, wx_bd, P["b_rg_a"].reshape(1, -1),
                P["b_rg_x"].reshape(1, -1), P["lru_lambda"].reshape(1, -1))
    hl, z_lru, a_lru, mult_lru = _lru_fwd(proj, *lru_args, S=S)
    kv = _matmul(mem_n, W["w_mem_kv"], M=N_MEM, N=2 * MEM_WIDTH, K=D_MODEL, mode="nn", bm=N_MEM, bn=512, bk=D_MODEL,
                 name="mm_kv")
    om, om_m, lse_mem = _mem_attn_fwd(proj, kv, S=S)
    b_gate = P["b_gate"].reshape(1, -1)
    merged = _mix_fwd(z_lru, o_dil_m, om_m, W["w_lru_out"], W["w_dil_out"], W["w_mem_out"], proj, b_gate, S=S)
    g_mlp, g_final, g_mix = (P[n].reshape(1, D_MODEL) for n in ("g_mlp", "g_final", "g_mix"))
    x1, hm = _matmul_rows(merged, W["w_out"], M=S, K=D_MODEL, mode="nn", bm=512, name="mm_out",
                          row_fn=_residual_then_norm, out_dtypes=(F32, MXU_DTYPE), tiles=[x], vecs=[g_mlp])
    W.update(late_weights("mlp", [hm]))

    def relu2(acc):
        rl = jnp.maximum(acc, 0.0)
        return rl * rl, rl

    act, relu_u = _matmul(hm, W["w_mlp_in_t"], M=S, N=D_FF, K=D_MODEL, mode="nt", bm=1024, bn=1024, bk=D_MODEL,
                          name="mm_mlp_in", out_dtypes=(MXU_DTYPE, MXU_DTYPE), epilogue=relu2, j_outer=True)
    dx2, dx2_m, loss, dg_final = _matmul_rows(
        act, W["w_mlp_out"], M=S, K=D_FF, mode="nn", bm=512, name="mm_mlp_out", row_fn=_residual_then_loss,
        out_dtypes=(F32, MXU_DTYPE), tiles=[x1, tgt], vecs=[g_final], acc_widths=(1, D_MODEL))

    G, Gs = {}, {}
    Gs["g_final"] = dg_final
    dw = dict(mode="tn", K=S, bk=S, out_dtypes=(MXU_DTYPE,))
    G["w_mlp_out"] = _matmul(act, dx2_m, M=D_FF, N=D_MODEL, bm=512, bn=D_MODEL, name="mm_dw_mlp_out",
                             parts=("rows", D_FF // N_DEV), **dw)
    du = _matmul(dx2_m, W["w_mlp_out"], M=S, N=D_FF, K=D_MODEL, mode="nt", bm=1024, bn=1024, bk=D_MODEL, name="mm_du",
                 out_dtypes=(MXU_DTYPE,), epilogue=lambda acc, rl: (acc * (2.0 * rl.astype(F32)),),
                 extras=[(relu_u, (0, 0))], j_outer=True)
    G["w_mlp_in"] = _matmul(hm, du, M=D_MODEL, N=D_FF, bm=D_MODEL, bn=512, name="mm_dw_mlp_in",
                            parts=("cols", D_FF // N_DEV), **dw)
    tie1 = send_grads({n: G.pop(n) for n in ("w_mlp_out", "w_mlp_in")})
    dx1, dx1_m, Gs["g_mlp"] = _matmul_rows(
        du, W["w_mlp_in_t"], M=S, K=D_FF, mode="nn", bm=512, name="mm_dhm", row_fn=_norm_bwd_then_residual(2),
        out_dtypes=(F32, MXU_DTYPE), tiles=[x1, dx2], vecs=[g_mlp], acc_widths=(D_MODEL,), deps=[tie1])
    G["w_out"] = _matmul(merged, dx1_m, M=D_MODEL, N=D_MODEL, bm=512, bn=D_MODEL, name="mm_dw_out",
                         parts=("rows", D_MODEL // N_DEV), **dw)
    (dg0, dg1, dg2, dy_lru, dy_dil, dy_mem, db0, db1, db2) = _mix_bwd(
        dx1_m, W["w_out"], z_lru, o_dil_m, om_m, W["w_lru_out"], W["w_dil_out"], W["w_mem_out"], proj, b_gate, S=S)
    Gs["b_gate0"], Gs["b_gate1"], Gs["b_gate2"] = db0, db1, db2

    G["w_mem_out"] = _matmul(om_m, dy_mem, M=MEM_WIDTH, N=D_MODEL, bm=MEM_WIDTH, bn=D_MODEL, name="mm_dw_mem_out",
                             parts=("cols", D_MODEL // N_DEV), **dw)
    dom = _matmul(dy_mem, W["w_mem_out"], M=S, N=MEM_WIDTH, K=D_MODEL, mode="nt", bm=512, bn=MEM_WIDTH, bk=D_MODEL,
                  name="mm_dom")
    dqm, dk_mem, dv_mem = _mem_attn_bwd(proj, kv, om, lse_mem, dom, S=S)
    dkv = jnp.concatenate([dk_mem, dv_mem], axis=1)
    G["w_mem_kv"] = _matmul(mem_n, dkv, M=D_MODEL, N=2 * MEM_WIDTH, K=N_MEM, mode="tn", bm=D_MODEL, bn=2 * MEM_WIDTH,
                            bk=N_MEM, name="mm_dw_kv", out_dtypes=(MXU_DTYPE,), parts=("rows", D_MODEL // N_DEV))
    dmem_n = _matmul(dkv, W["w_mem_kv"], M=N_MEM, N=D_MODEL, K=2 * MEM_WIDTH, mode="nt", bm=N_MEM, bn=D_MODEL,
                     bk=2 * MEM_WIDTH, name="mm_dmem")
    (Gs["g_mem"],) = _rmsnorm_bwd(mem, P["g_mem"], dmem_n, None, rows=N_MEM, name="norm_mem_bwd", dx_dtypes=())

    G["w_dil_out"] = _matmul(o_dil_m, dy_dil, M=256, N=D_MODEL, bm=256, bn=D_MODEL, name="mm_dw_dil_out",
                             parts=("cols", D_MODEL // N_DEV), **dw)
    do_dil, delta = _matmul(dy_dil, W["w_dil_out"], M=S, N=256, K=D_MODEL, mode="nt", bm=512, bn=256, bk=D_MODEL,
                            name="mm_do_dil", out_dtypes=(F32, F32), epilogue=_with_delta, extras=[(o_dil, (0, 0))])
    G["w_lru_out"] = _matmul(z_lru, dy_lru, M=D_RNN, N=D_MODEL, bm=D_RNN, bn=D_MODEL, name="mm_dw_lru_out",
                             parts=("cols", D_MODEL // N_DEV), **dw)
    dz = _matmul(dy_lru, W["w_lru_out"], M=S, N=D_RNN, K=D_MODEL, mode="nt", bm=512, bn=D_RNN, bk=D_MODEL, name="mm_dz_lru")
    tie2 = send_grads({n: G.pop(n) for n in ("w_out", "w_mem_out", "w_mem_kv", "w_dil_out", "w_lru_out")})
    bias = bias + tie2[0, 0]
    dqkv, dbias = None, []
    for g in range(len(DIL_GROUPS)):
        *dqkv, db_g = _dilated_bwd(proj, do_dil, lse_dil, delta, bias, g, S=S, into=dqkv)
        dbias.append(db_g)
    drel = _dil_bias_bwd(jnp.stack(dbias, axis=0).reshape(len(DIL_GROUPS), DIL_HEADS, SPAN, 2 * SPAN), buckets)
    Gs["rel_bias"] = drel

    dxl, dgl, dcw, dcb, dwa, dwx, dba, dbx, dlam = _lru_bwd(proj, hl, a_lru, mult_lru, dz, *lru_args, S=S)
    Gs["conv_w"], Gs["conv_b"] = dcw, dcb
    Gs["w_rg_a"], Gs["w_rg_x"] = _block_diag_extract(dwa), _block_diag_extract(dwx)
    Gs["b_rg_a"], Gs["b_rg_x"], Gs["lru_lambda"] = dba, dbx, dlam
    Gs["loss"] = loss

    dproj = [dxl, dgl] + dqkv + [dqm, dg0, dg1, dg2]
    tie = []
    for q in range(W_IN_PIECES):
        dw_q = None
        for half in range(2):
            dw_q = _dw_in_t_half(h, dproj, q, half, S=S, name=f"mm_dw_in_{q}_{half}", into=dw_q, deps=tie)
        tie = [send_grads({f"w_in_{q}": dw_q})]
    grad_x, Gs["g_mix"] = _matmul_rows(
        dproj, W["w_in_t"], M=S, K=D_IN, mode="nn", bm=256, name="mm_dh", row_fn=_norm_bwd_then_residual(1),
        out_dtypes=(F32,), tiles=[x, dx1], vecs=[g_mix], acc_widths=(D_MODEL,), deps=tie)
    return grad_x, reduce_small(Gs)


BIG = ("w_in", "w_lru_out", "w_dil_out", "w_mem_kv", "w_mem_out", "w_out", "w_mlp_in", "w_mlp_out")
W_IN_PIECES = 2
COL_SHARDED = ("w_lru_out", "w_dil_out", "w_mem_out", "w_mlp_in")
GATHERED_TRANSPOSED = ("w_mlp_in",)
SMALL = ("g_mix", "b_gate", "conv_b", "w_rg_a", "b_rg_a", "w_rg_x", "b_rg_x", "lru_lambda", "rel_bias", "g_mem",
         "g_mlp", "g_final")
WEIGHTS = ("g_mix", "w_in", "b_gate", "conv_w", "conv_b", "w_rg_a", "b_rg_a", "w_rg_x", "b_rg_x", "lru_lambda",
           "w_lru_out", "rel_bias", "w_dil_out", "g_mem", "w_mem_kv", "w_mem_out", "w_out", "g_mlp", "w_mlp_in",
           "w_mlp_out", "g_final")


def _gathered_to_full(name, gathered):
    if name in COL_SHARDED:
        n, r, c = gathered.shape
        return gathered.transpose(1, 0, 2).reshape(r, n * c)
    n, r, c = gathered.shape
    return gathered.reshape(n * r, c)


SMALL_GRADS = (("g_mix", (1, 1024)), ("b_gate0", (1, 1024)), ("b_gate1", (1, 1024)), ("b_gate2", (1, 1024)),
               ("conv_b", (1, 768)), ("w_rg_a", (12, 64, 64)), ("b_rg_a", (1, 768)), ("w_rg_x", (12, 64, 64)),
               ("b_rg_x", (1, 768)), ("lru_lambda", (1, 768)), ("rel_bias", (32, 128)), ("g_mem", (1, 1024)),
               ("g_mlp", (1, 1024)), ("g_final", (1, 1024)), ("conv_w", (4, 768)), ("loss", (1, 1)))


def _pack(parts):
    flat = jnp.concatenate([p.reshape(-1) for p in parts])
    return jnp.pad(flat, (0, (-flat.shape[0]) % 1024)).reshape(-1, 128)


def _unpack(pack, shapes):
    flat = pack.reshape(-1)
    out, off = [], 0
    for shp in shapes:
        size = math.prod(shp)
        out.append(flat[off:off + size].reshape(shp))
        off += size
    return out


def kernel(x, mem, g_mix, w_in, b_gate, conv_w, conv_b, w_rg_a, b_rg_a, w_rg_x, b_rg_x, lru_lambda, w_lru_out, rel_bias, w_dil_out, g_mem, w_mem_kv, w_mem_out, w_out, g_mlp, w_mlp_in, w_mlp_out, g_final, loss_target, m_g_mix, m_w_in, m_b_gate, m_conv_w, m_conv_b, m_w_rg_a, m_b_rg_a, m_w_rg_x, m_b_rg_x, m_lru_lambda, m_w_lru_out, m_rel_bias, m_w_dil_out, m_g_mem, m_w_mem_kv, m_w_mem_out, m_w_out, m_g_mlp, m_w_mlp_in, m_w_mlp_out, m_g_final, v_g_mix, v_w_in, v_b_gate, v_conv_w, v_conv_b, v_w_rg_a, v_b_rg_a, v_w_rg_x, v_b_rg_x, v_lru_lambda, v_w_lru_out, v_rel_bias, v_w_dil_out, v_g_mem, v_w_mem_kv, v_w_mem_out, v_w_out, v_g_mlp, v_w_mlp_in, v_w_mlp_out, v_g_final):
    w = dict(g_mix=g_mix, w_in=w_in, b_gate=b_gate, conv_w=conv_w, conv_b=conv_b, w_rg_a=w_rg_a, b_rg_a=b_rg_a,
             w_rg_x=w_rg_x, b_rg_x=b_rg_x, lru_lambda=lru_lambda, w_lru_out=w_lru_out, rel_bias=rel_bias,
             w_dil_out=w_dil_out, g_mem=g_mem, w_mem_kv=w_mem_kv, w_mem_out=w_mem_out, w_out=w_out, g_mlp=g_mlp,
             w_mlp_in=w_mlp_in, w_mlp_out=w_mlp_out, g_final=g_final)
    m = dict(g_mix=m_g_mix, w_in=m_w_in, b_gate=m_b_gate, conv_w=m_conv_w, conv_b=m_conv_b, w_rg_a=m_w_rg_a,
             b_rg_a=m_b_rg_a, w_rg_x=m_w_rg_x, b_rg_x=m_b_rg_x, lru_lambda=m_lru_lambda, w_lru_out=m_w_lru_out,
             rel_bias=m_rel_bias, w_dil_out=m_w_dil_out, g_mem=m_g_mem, w_mem_kv=m_w_mem_kv, w_mem_out=m_w_mem_out,
             w_out=m_w_out, g_mlp=m_g_mlp, w_mlp_in=m_w_mlp_in, w_mlp_out=m_w_mlp_out, g_final=m_g_final)
    v = dict(g_mix=v_g_mix, w_in=v_w_in, b_gate=v_b_gate, conv_w=v_conv_w, conv_b=v_conv_b, w_rg_a=v_w_rg_a,
             b_rg_a=v_b_rg_a, w_rg_x=v_w_rg_x, b_rg_x=v_b_rg_x, lru_lambda=v_lru_lambda, w_lru_out=v_w_lru_out,
             rel_bias=v_rel_bias, w_dil_out=v_w_dil_out, g_mem=v_g_mem, w_mem_kv=v_w_mem_kv, w_mem_out=v_w_mem_out,
             w_out=v_w_out, g_mlp=v_g_mlp, w_mlp_in=v_w_mlp_in, w_mlp_out=v_w_mlp_out, g_final=v_g_final)

    my_idx = _dev_index(_my_pos())

    w_in_shard = _mx(w["w_in"].T)
    first = _push_start([w_in_shard], [(N_DEV,) + w_in_shard.shape], _gather_refs, "gather_in_start",
                        relations=ONE_PER_CHIP)
    cw_cols = D_RNN // N_DEV
    conv_pad = jnp.zeros((64, D_MODEL), F32).at[:CONV_WIDTH, :cw_cols].set(w["conv_w"])
    late = {}
    P = {n: w[n] for n in SMALL}

    def start_late(order_after):
        for group, names in (("branch", ("w_mem_kv", "w_lru_out", "w_dil_out", "w_mem_out", "w_out", "conv_w")),
                             ("mlp", ("w_mlp_in", "w_mlp_out"))):
            shards = [conv_pad if n == "conv_w" else _mx(w[n].T if n in GATHERED_TRANSPOSED else w[n]) for n in names]
            started = _push_start(shards, [(N_DEV,) + s.shape for s in shards], _gather_refs, f"gather_{group}_start",
                                  after=order_after)
            late[group] = (names, shards, started)
            order_after = [started["token"]]

    def late_weights(group, after):
        if group == "first":
            (land,) = _push_wait(first, after)
            forward = _forward_start(land, "forward_in_start")
            start_late([forward["token"]])
            full = lax.dynamic_update_index_in_dim(_forward_wait(forward, forward["token"]), w_in_shard, my_idx, 0)
            return {"w_in_t": full.reshape(D_IN, D_MODEL)}
        names, shards, started = late[group]
        out = {}
        for n, land, own in zip(names, _push_wait(started, after), shards):
            full = lax.dynamic_update_index_in_dim(land, own, my_idx, 0)
            if n == "conv_w":
                out[n] = full[:, :CONV_WIDTH, :cw_cols].transpose(1, 0, 2).reshape(CONV_WIDTH, D_RNN)
            elif n in GATHERED_TRANSPOSED:
                out[n + "_t"] = full.reshape(-1, full.shape[2])
            else:
                out[n] = _gathered_to_full(n, full)
        return out

    sent, small = [], {}

    def send_grads(gs):
        names = list(gs)
        parts = [gs[n] for n in names]
        own = [lax.dynamic_index_in_dim(p, my_idx, 0, keepdims=False) for p in parts]
        started = _push_start(parts, [(N_DEV - 1,) + p.shape[1:] for p in parts], _scatter_refs,
                              f"scatter{len(sent)}_start")
        sent.append((names, own, started))
        return started["token"]

    def reduce_small(gs):
        small["pack"] = _pack([gs[n] for n, _ in SMALL_GRADS])
        small["started"] = _push_start([small["pack"]], [(N_DEV,) + small["pack"].shape], _gather_refs, "small_start")
        return small["started"]["token"]

    grad_x, last_token = _local_step(x[0], mem[0], loss_target[0], {}, P, late_weights, send_grads, reduce_small,
                                     first["token"][0, 0])

    grads, deltas, new_m, new_v = {}, {}, {}, {}
    after = last_token
    for names, own, started in sent[:-W_IN_PIECES]:
        for n, o, land in zip(names, own, _push_wait(started, after)):
            grads[n], deltas[n], new_m[n], new_v[n] = _adamw_landed(w[n], o, land, m[n], v[n], name=f"adamw_{n}")
            after = deltas[n]
    prev = None
    for q, (names, own, started) in enumerate(sent[-W_IN_PIECES:]):
        (land,) = _push_wait(started, after)
        prev = _adamw_landed(w["w_in"].T, own[0], land, m["w_in"].T, v["w_in"].T, name=f"adamw_{names[0]}",
                             col_blk=q, prev=prev)
        after = prev[1]
    grads["w_in"], deltas["w_in"], new_m["w_in"], new_v["w_in"] = [t.T for t in prev]
    (small_land,) = _push_wait(small["started"], [after] + [deltas[n] for n in BIG if n != "w_in"])
    total = _sum_slots(lax.dynamic_update_index_in_dim(small_land, small["pack"], my_idx, 0))
    summed = dict(zip([n for n, _ in SMALL_GRADS], _unpack(total, [shp for _, shp in SMALL_GRADS])))
    summed["b_gate"] = jnp.concatenate([summed.pop(f"b_gate{b}") for b in range(3)], axis=1)
    summed["rel_bias"] = summed["rel_bias"][:, :3 * DIL_HEADS]
    for n in SMALL:
        grads[n] = summed[n].reshape(w[n].shape)
    small_updates = _adamw_many([w[n] for n in SMALL], [grads[n] for n in SMALL], [m[n] for n in SMALL],
                                [v[n] for n in SMALL])
    for n, (d_, nm_, nv_) in zip(SMALL, small_updates):
        deltas[n], new_m[n], new_v[n] = d_, nm_, nv_
    conv_w_sum, loss_sum = summed["conv_w"], summed["loss"]
    grads["conv_w"] = lax.dynamic_slice(conv_w_sum, (0, my_idx * cw_cols), (CONV_WIDTH, cw_cols))
    deltas["conv_w"], new_m["conv_w"], new_v["conv_w"] = _adamw_plain(
        w["conv_w"], grads["conv_w"], m["conv_w"], v["conv_w"], name="adamw_conv_w")

    return (loss_sum.reshape(()), grad_x[None], *[grads[n] for n in WEIGHTS], *[deltas[n] for n in WEIGHTS],
            *[new_m[n] for n in WEIGHTS], *[new_v[n] for n in WEIGHTS])
```

```python
import functools
import math

import jax
import jax.numpy as jnp
from jax import lax
from jax.experimental import pallas as pl
from jax.experimental.pallas import tpu as pltpu

F32 = jnp.float32
MXU_DTYPE = jnp.bfloat16
VMEM_LIMIT_BYTES = 56 * 1024 * 1024
N_DEV = 8

D_MODEL = 1024
N_MEM = 256
MEM_HEADS = 4
MEM_HEAD_DIM = 128
MEM_WIDTH = 512
D_RNN = 768
LRU_BLOCK = 64
N_LRU_BLOCKS = 12
LRU_GROUP = 256
N_LRU_GROUPS = 3
CONV_WIDTH = 4
LRU_C = 8.0
DIL_GROUPS = ((128, 1), (512, 4), (2048, 16))
SPAN = 128
DIL_HEADS = 4
DIL_HEAD_DIM = 64
NUM_BUCKETS = 32
MAX_DISTANCE = 2048
D_FF = 4096
D_IN = 7424
EPS = 1e-6
NEG = -1e30
C_XL, C_GATE, C_QKV, C_QM, C_GATES = 0, 768, 1536, 3840, 4352

ADAM_LR = 0.001
ADAM_B1 = 0.9
ADAM_B2 = 0.999
ADAM_EPS = 1e-08
ADAM_WD = 0.01
ADAM_STEP = 10

MESH = pl.DeviceIdType.MESH
GELU_K = math.sqrt(2.0 / math.pi)


def _cparams(sem=None):
    kw = dict(vmem_limit_bytes=VMEM_LIMIT_BYTES)
    if sem is not None:
        kw["dimension_semantics"] = sem
    return pltpu.CompilerParams(**kw)


def _mx(v):
    return v.astype(MXU_DTYPE)


def _dot(a, b, mode="nn"):
    dims = {"nn": (((1,), (0,)), ((), ())), "nt": (((1,), (1,)), ((), ())), "tn": (((0,), (0,)), ((), ()))}[mode]
    return lax.dot_general(_mx(a), _mx(b), dims, preferred_element_type=F32)


def _colsum(v):
    return jnp.sum(v, axis=0, keepdims=True)


def _matmul(a, b, *, M, N, K, mode, bm, bn, bk, name, out_dtypes=(F32,), epilogue=None, extras=(),
            a_off=(0, 0), b_off=(0, 0), j_outer=False, deps=(), parts=None):
    assert M % bm == 0 and N % bn == 0 and K % bk == 0, (name, M, N, K, bm, bn, bk)
    nm, nn, nk = M // bm, N // bn, K // bk

    def ij(f):
        if j_outer:
            return lambda j, i, k: f(i, j, k)
        return f

    if mode == "tn":
        a_spec = pl.BlockSpec((bk, bm), ij(lambda i, j, k: (k + a_off[0], i + a_off[1])))
    else:
        a_spec = pl.BlockSpec((bm, bk), ij(lambda i, j, k: (i + a_off[0], k + a_off[1])))
    if mode == "nt":
        b_spec = pl.BlockSpec((bn, bk), ij(lambda i, j, k: (j + b_off[0], k + b_off[1])))
    else:
        b_spec = pl.BlockSpec((bk, bn), ij(lambda i, j, k: (k + b_off[0], j + b_off[1])))
    ex_specs = [pl.BlockSpec((bm, bn), ij(functools.partial(lambda i, j, k, o: (i + o[0], j + o[1]), o=off)))
                for _, off in extras]
    if parts is None:
        out_dims = (M, N)
        out_spec = pl.BlockSpec((bm, bn), ij(lambda i, j, k: (i, j)))
    elif parts[0] == "rows":
        r = parts[1]
        assert bm % r == 0
        out_dims = (M // r, r, N)
        out_spec = pl.BlockSpec((bm // r, r, bn), ij(lambda i, j, k: (i, 0, j)))
    elif parts[0] == "rows_t":
        r = parts[1]
        assert bn % r == 0
        out_dims = (N // r, r, M)
        out_spec = pl.BlockSpec((bn // r, r, bm), ij(lambda i, j, k: (j, 0, i)))
    else:
        c = parts[1]
        assert bn % c == 0
        out_dims = (N // c, M, c)
        out_spec = pl.BlockSpec((bn // c, bm, c), ij(lambda i, j, k: (j, i, 0)))
    n_ex, n_out, n_dep = len(extras), len(out_dtypes), len(deps)

    def body(*refs):
        a_ref, b_ref = refs[0], refs[1]
        ex = refs[2:2 + n_ex]
        outs = refs[2 + n_ex + n_dep:2 + n_ex + n_dep + n_out]
        part = _dot(a_ref[...], b_ref[...], mode)

        def finish(acc):
            vals = epilogue(acc, *[e[...] for e in ex]) if epilogue is not None else (acc,)
            for o, v in zip(outs, vals):
                if parts is not None and parts[0] == "rows_t":
                    v = v.T
                v = v.astype(o.dtype)
                if parts is None:
                    o[...] = v
                elif parts[0] in ("rows", "rows_t"):
                    for ch in range(v.shape[0] // parts[1]):
                        o[ch] = v[ch * parts[1]:(ch + 1) * parts[1], :]
                else:
                    for ch in range(bn // parts[1]):
                        o[ch] = v[:, ch * parts[1]:(ch + 1) * parts[1]]

        if nk == 1:
            finish(part)
        else:
            acc_ref = refs[-1]
            k = pl.program_id(2)

            @pl.when(k == 0)
            def _():
                acc_ref[...] = part

            @pl.when(k > 0)
            def _():
                acc_ref[...] += part

            @pl.when(k == nk - 1)
            def _():
                finish(acc_ref[...])

    grid = (nn, nm, nk) if j_outer else (nm, nn, nk)
    res = pl.pallas_call(
        body, name=name, grid=grid,
        in_specs=[a_spec, b_spec] + ex_specs + [pl.BlockSpec(memory_space=pl.ANY)] * n_dep,
        out_specs=[out_spec] * n_out,
        out_shape=[jax.ShapeDtypeStruct(out_dims, dt) for dt in out_dtypes],
        scratch_shapes=[pltpu.VMEM((bm, bn), F32)] if nk > 1 else [],
        compiler_params=_cparams(("parallel", "parallel", "arbitrary")),
    )(a, b, *[e for e, _ in extras], *deps)
    return res[0] if n_out == 1 else res


ROW_SUBTILES = 2


def _matmul_rows(a, b, *, M, K, mode, bm, name, row_fn, out_dtypes, tiles=(), vecs=(), acc_widths=(), deps=()):
    N = D_MODEL
    assert M % bm == 0
    segs = list(a) if isinstance(a, (list, tuple)) else [a]
    widths = [s_.shape[1] for s_ in segs]
    assert sum(widths) == K and (len(segs) == 1 or mode == "nn")
    n_s, n_t, n_v, n_o, n_a, n_d = len(segs), len(tiles), len(vecs), len(out_dtypes), len(acc_widths), len(deps)
    row = pl.BlockSpec((bm, N), lambda i: (i, 0))
    b_shape = (K, N) if mode == "nn" else (N, K)

    def body(*refs):
        b_ref = refs[n_s]
        ins = refs[n_s + 1:n_s + 1 + n_t + n_v]
        outs = refs[n_s + 1 + n_t + n_v + n_d:n_s + 1 + n_t + n_v + n_d + n_o]
        accs = refs[n_s + 1 + n_t + n_v + n_d + n_o:]
        for o in accs:
            @pl.when(pl.program_id(0) == 0)
            def _(o=o):
                o[...] = jnp.zeros_like(o)

        for s_ in range(ROW_SUBTILES):
            rows = pl.ds(s_ * (bm // ROW_SUBTILES), bm // ROW_SUBTILES)
            if n_s == 1:
                acc = _dot(refs[0][rows, :], b_ref[...], mode)
            else:
                acc, k0 = None, 0
                for a_ref, w_ in zip(refs[:n_s], widths):
                    part = _dot(a_ref[rows, :], b_ref[k0:k0 + w_, :])
                    acc = part if acc is None else acc + part
                    k0 += w_
            tile_vals, partials = row_fn(acc, *[r[rows, :] for r in ins[:n_t]], *[r[...] for r in ins[n_t:]])
            for o, val in zip(outs, tile_vals):
                o[rows, :] = val.astype(o.dtype)
            for o, val in zip(accs, partials):
                o[...] += val

    res = pl.pallas_call(
        body, name=name, grid=(M // bm,),
        in_specs=[pl.BlockSpec((bm, w_), lambda i: (i, 0)) for w_ in widths] + [pl.BlockSpec(b_shape, lambda i: (0, 0))]
        + [row] * n_t + [pl.BlockSpec((1, N), lambda i: (0, 0))] * n_v + [pl.BlockSpec(memory_space=pl.ANY)] * n_d,
        out_specs=[row] * n_o + [pl.BlockSpec((1, w_), lambda i: (0, 0)) for w_ in acc_widths],
        out_shape=[jax.ShapeDtypeStruct((M, N), dt) for dt in out_dtypes]
        + [jax.ShapeDtypeStruct((1, w_), F32) for w_ in acc_widths],
        compiler_params=_cparams(("arbitrary",) if n_a else ("parallel",)),
    )(*segs, b, *tiles, *vecs, *deps)
    return res


def _dw_in_t_half(h, pieces, q, half, *, S, name, into=None, deps=(), bk=1024):
    half_w, cols = D_IN // 2, D_MODEL // W_IN_PIECES
    lo, hi = half * half_w, (half + 1) * half_w
    use, c0 = [], 0
    for p in pieces:
        w_ = p.shape[1]
        a0, a1 = max(lo, c0), min(hi, c0 + w_)
        if a1 > a0:
            use.append((p, a0 - c0, a1 - a0))
        c0 += w_
    n_p, n_into, n_d, nk = len(use), 0 if into is None else 1, len(deps), S // bk
    rows = D_IN // N_DEV

    def body(*refs):
        h_ref, p_refs = refs[0], refs[1:1 + n_p]
        o_ref, acc_ref = refs[1 + n_p + n_into + n_d], refs[-1]
        k = pl.program_id(0)
        dp = jnp.concatenate([r[:, s0:s0 + w_] for r, (_, s0, w_) in zip(p_refs, use)], axis=1)
        part = _dot(h_ref[...], dp, "tn")

        @pl.when(k == 0)
        def _():
            acc_ref[...] = part

        @pl.when(k > 0)
        def _():
            acc_ref[...] += part

        @pl.when(k == nk - 1)
        def _():
            vt = acc_ref[...].T.astype(o_ref.dtype)
            for ch in range(half_w // rows):
                o_ref[ch] = vt[ch * rows:(ch + 1) * rows, :]

    return pl.pallas_call(
        body, name=name, grid=(nk,),
        in_specs=[pl.BlockSpec((bk, cols), lambda k: (k, q))]
        + [pl.BlockSpec((bk, p.shape[1]), lambda k: (k, 0)) for p, _, _ in use]
        + [pl.BlockSpec(memory_space=pl.ANY)] * (n_into + n_d),
        out_specs=pl.BlockSpec((half_w // rows, rows, cols), lambda k: (half, 0, 0)),
        out_shape=jax.ShapeDtypeStruct((N_DEV, rows, cols), MXU_DTYPE),
        input_output_aliases={1 + n_p: 0} if n_into else {},
        scratch_shapes=[pltpu.VMEM((cols, half_w), F32)],
        compiler_params=_cparams(("arbitrary",)),
    )(h, *[p for p, _, _ in use], *([into] if n_into else []), *deps)


def _rmsnorm_fwd(x, g, *, rows, name, bt=512):
    bt = min(bt, rows)

    def body(x_ref, g_ref, o_ref):
        xv = x_ref[...]
        r = lax.rsqrt(jnp.mean(xv * xv, axis=-1, keepdims=True) + EPS)
        o_ref[...] = (xv * r * g_ref[...]).astype(o_ref.dtype)

    return pl.pallas_call(
        body, name=name, grid=(rows // bt,),
        in_specs=[pl.BlockSpec((bt, D_MODEL), lambda i: (i, 0)), pl.BlockSpec((1, D_MODEL), lambda i: (0, 0))],
        out_specs=pl.BlockSpec((bt, D_MODEL), lambda i: (i, 0)),
        out_shape=jax.ShapeDtypeStruct((rows, D_MODEL), MXU_DTYPE),
        compiler_params=_cparams(("parallel",)),
    )(x, g.reshape(1, D_MODEL))


def _rms_bwd_tile(xv, gv, dyv):
    r = lax.rsqrt(jnp.mean(xv * xv, axis=-1, keepdims=True) + EPS)
    w = dyv * gv
    dx = r * w - xv * (r * r * r) * jnp.mean(w * xv, axis=-1, keepdims=True)
    dg = _colsum(dyv * xv * r)
    return dx, dg


def _residual_then_norm(acc, x_t, g):
    x1 = x_t + acc
    r = lax.rsqrt(jnp.mean(x1 * x1, axis=-1, keepdims=True) + EPS)
    return (x1, x1 * r * g), ()


def _residual_then_loss(acc, x_t, tgt_t, g):
    x2 = x_t + acc
    r = lax.rsqrt(jnp.mean(x2 * x2, axis=-1, keepdims=True) + EPS)
    diff = x2 * r * g - tgt_t
    part = jnp.sum(jnp.mean(diff * diff, axis=-1, keepdims=True), axis=0, keepdims=True) * 0.5
    dx, dg = _rms_bwd_tile(x2, g, diff * (1.0 / D_MODEL))
    return (dx, dx), (part, dg)


def _norm_bwd_then_residual(n_out):
    def fn(acc, x_t, res_t, g):
        dx, dg = _rms_bwd_tile(x_t, g, acc)
        return (dx + res_t,) * n_out, (dg,)

    return fn


def _rmsnorm_bwd(x, g, dy, res, *, rows, name, bt=512, dx_dtypes=(F32,)):
    bt = min(bt, rows)
    has_res = res is not None

    def body(*refs):
        x_ref, g_ref, dy_ref = refs[:3]
        res_ref = refs[3] if has_res else None
        outs = refs[3 + int(has_res):]
        dx, dg = _rms_bwd_tile(x_ref[...], g_ref[...], dy_ref[...])
        if has_res:
            dx = dx + res_ref[...]
        dg_ref = outs[-1]

        @pl.when(pl.program_id(0) == 0)
        def _():
            dg_ref[...] = jnp.zeros_like(dg_ref)

        dg_ref[...] += dg
        for o in outs[:-1]:
            o[...] = dx.astype(o.dtype)

    row_spec = pl.BlockSpec((bt, D_MODEL), lambda i: (i, 0))
    vec_spec = pl.BlockSpec((1, D_MODEL), lambda i: (0, 0))
    ins = [x, g.reshape(1, D_MODEL), dy] + ([res] if has_res else [])
    return pl.pallas_call(
        body, name=name, grid=(rows // bt,),
        in_specs=[row_spec, vec_spec, row_spec] + ([row_spec] if has_res else []),
        out_specs=[row_spec] * len(dx_dtypes) + [vec_spec],
        out_shape=[jax.ShapeDtypeStruct((rows, D_MODEL), dt) for dt in dx_dtypes] + [jax.ShapeDtypeStruct((1, D_MODEL), F32)],
        compiler_params=_cparams(("arbitrary",)),
    )(*ins)


LRU_T = 512
SCAN_GROUPS = 4


def _gelu(x):
    t = jnp.tanh(GELU_K * (x + 0.044715 * x * x * x))
    return 0.5 * x * (1.0 + t), t


def _gelu_grad(x, t):
    return 0.5 * (1.0 + t) + 0.5 * x * (1.0 - t * t) * GELU_K * (1.0 + 3.0 * 0.044715 * x * x)


def _softplus_neg(lam):
    z = -lam
    u = jnp.exp(-jnp.abs(z))
    w = 1.0 + u
    l1p = jnp.where(w == 1.0, u, jnp.log(w) * u / jnp.where(w == 1.0, 1.0, w - 1.0))
    return jnp.maximum(z, 0.0) + l1p


def _shift_down(cur, prev8, k, row8):
    y = pltpu.roll(cur, k, 0)
    head = jnp.where(row8 < k, pltpu.roll(prev8, k, 0), y[0:8])
    return jnp.concatenate([head, y[8:]], axis=0)


def _shift_up(cur, next8, k, row8):
    n = cur.shape[0]
    y = pltpu.roll(cur, n - k, 0)
    tail = jnp.where(row8 >= 8 - k, pltpu.roll(next8, 8 - k, 0), y[n - 8:n])
    return jnp.concatenate([y[0:n - 8], tail], axis=0)


def _lru_gates(xl, p8, cw, cb, wa, wx, ba, bx, lam, row8, a_mult=None):
    sh = [xl] + [_shift_down(xl, p8, k, row8) for k in (1, 2, 3)]
    xc = cb + cw[3:4] * sh[0] + cw[2:3] * sh[1] + cw[1:2] * sh[2] + cw[0:1] * sh[3]
    r = jax.nn.sigmoid(_dot(xc, wa) + ba)
    i = jax.nn.sigmoid(_dot(xc, wx) + bx)
    sp = _softplus_neg(lam)
    if a_mult is None:
        la = -LRU_C * r * sp
        a = jnp.exp(la)
        mult = jnp.sqrt(jnp.tanh(-la) * (a * a + 1.0))
    else:
        a, mult = a_mult
    return dict(sh=sh, xc=xc, r=r, i=i, sp=sp, a=a, mult=mult)


def _lru_specs(n_t, reverse):
    T = LRU_T
    tt = (lambda t: n_t - 1 - t) if reverse else (lambda t: t)
    blk = lambda col0: pl.BlockSpec((T, LRU_GROUP), lambda g, t: (tt(t), col0 + g))
    prev8 = lambda col0: pl.BlockSpec((8, LRU_GROUP), lambda g, t: (jnp.maximum(tt(t) * (T // 8) - 1, 0), col0 + g))
    vec = lambda rows: pl.BlockSpec((rows, LRU_GROUP), lambda g, t: (0, g))
    wbd = pl.BlockSpec((1, LRU_GROUP, LRU_GROUP), lambda g, t: (g, 0, 0))
    return blk, prev8, vec, wbd


def _lru_fwd(proj, conv_w, conv_b, wa_bd, wx_bd, b_a, b_x, lam, *, S):
    T = LRU_T
    n_t = S // T
    blk, _, vec, wbd = _lru_specs(n_t, False)

    def body(xl_ref, gate_ref, cw_ref, cb_ref, wa_ref, wx_ref, ba_ref, bx_ref, lam_ref,
             hl_ref, z_ref, a_s, m_ref, prev8, hcar, b_s):
        @pl.when(pl.program_id(1) == 0)
        def _():
            prev8[...] = jnp.zeros_like(prev8)
            hcar[...] = jnp.zeros_like(hcar)

        row8 = lax.broadcasted_iota(jnp.int32, (8, LRU_GROUP), 0)
        xl = xl_ref[...]
        q = _lru_gates(xl, prev8[...], cw_ref[...], cb_ref[...], wa_ref[0], wx_ref[0], ba_ref[...], bx_ref[...],
                       lam_ref[...], row8)
        prev8[...] = xl[T - 8:T]
        a_s[...] = q["a"]
        m_ref[...] = q["mult"]
        b_s[...] = q["mult"] * q["i"] * q["xc"]

        def step(c, carry):
            local = []
            for u in range(SCAN_GROUPS):
                off = pl.multiple_of((c * SCAN_GROUPS + u) * 8, 8)
                A = a_s[pl.ds(off, 8), :]
                B = b_s[pl.ds(off, 8), :]
                for k in (1, 2, 4):
                    a_sh = jnp.where(row8 >= k, pltpu.roll(A, k, 0), 1.0)
                    b_sh = jnp.where(row8 >= k, pltpu.roll(B, k, 0), 0.0)
                    B = A * b_sh + B
                    A = A * a_sh
                local.append((off, A, B))
            for off, A, B in local:
                h = A * carry + B
                hl_ref[pl.ds(off, 8), :] = h
                carry = h[7:8, :]
            return carry

        hcar[...] = lax.fori_loop(0, T // (8 * SCAN_GROUPS), step, hcar[...])
        ge, _ = _gelu(gate_ref[...])
        z_ref[...] = (ge * hl_ref[...]).astype(z_ref.dtype)

    return pl.pallas_call(
        body, name="lru_fwd", grid=(N_LRU_GROUPS, n_t),
        in_specs=[blk(C_XL // LRU_GROUP), blk(C_GATE // LRU_GROUP), vec(4), vec(1), wbd, wbd, vec(1), vec(1), vec(1)],
        out_specs=[blk(0)] * 4,
        out_shape=[jax.ShapeDtypeStruct((S, D_RNN), F32), jax.ShapeDtypeStruct((S, D_RNN), MXU_DTYPE),
                   jax.ShapeDtypeStruct((S, D_RNN), F32), jax.ShapeDtypeStruct((S, D_RNN), F32)],
        scratch_shapes=[pltpu.VMEM((8, LRU_GROUP), F32), pltpu.VMEM((1, LRU_GROUP), F32), pltpu.VMEM((T, LRU_GROUP), F32)],
        compiler_params=_cparams(("parallel", "arbitrary")),
    )(proj, proj, conv_w, conv_b, wa_bd, wx_bd, b_a, b_x, lam)


def _lru_bwd(proj, hl, a_fwd, mult_fwd, dz, conv_w, conv_b, wa_bd, wx_bd, b_a, b_x, lam, *, S):
    T = LRU_T
    n_t = S // T
    blk, prev8s, vec, wbd = _lru_specs(n_t, True)

    def body(xl_ref, xlp_ref, gate_ref, hl_ref, hlp_ref, a_ref, m_ref, dz_ref, cw_ref, cb_ref, wa_ref, wx_ref, ba_ref,
             bx_ref, lam_ref, dxl_ref, dgate_ref, dcw_ref, dcb_ref, dwa_ref, dwx_ref, dba_ref, dbx_ref, dlam_ref,
             next8, gcar, c_s, b_s, l_s):
        t = pl.program_id(1)
        first_chunk = t == n_t - 1

        @pl.when(t == 0)
        def _():
            next8[...] = jnp.zeros_like(next8)
            gcar[...] = jnp.zeros_like(gcar)
            for ref in (dcw_ref, dcb_ref, dwa_ref, dwx_ref, dba_ref, dbx_ref, dlam_ref):
                ref[...] = jnp.zeros_like(ref)

        row8 = lax.broadcasted_iota(jnp.int32, (8, LRU_GROUP), 0)
        rowT = lax.broadcasted_iota(jnp.int32, (T, LRU_GROUP), 0)
        keep = jnp.where(first_chunk, 0.0, 1.0)
        xl = xl_ref[...]
        wa, wx, lam_v = wa_ref[0], wx_ref[0], lam_ref[...]
        q = _lru_gates(xl, xlp_ref[...] * keep, cw_ref[...], cb_ref[...], wa, wx, ba_ref[...], bx_ref[...], lam_v, row8,
                       a_mult=(a_ref[...], m_ref[...]))
        a, mult, r, i, xc, sp = q["a"], q["mult"], q["r"], q["i"], q["xc"], q["sp"]
        hl_v = hl_ref[...]
        dz_v = dz_ref[...]
        gate = gate_ref[...]
        ge, th = _gelu(gate)
        dgate_ref[...] = (dz_v * hl_v * _gelu_grad(gate, th)).astype(dgate_ref.dtype)

        c_s[...] = jnp.where(rowT == T - 1, 0.0, pltpu.roll(a, T - 1, 0))
        b_s[...] = dz_v * ge + jnp.where(rowT == T - 1, gcar[...], 0.0)

        def step(n, carry):
            local = []
            for u in range(SCAN_GROUPS):
                off = pl.multiple_of((T // 8 - 1 - (n * SCAN_GROUPS + u)) * 8, 8)
                C = c_s[pl.ds(off, 8), :]
                B = b_s[pl.ds(off, 8), :]
                for k in (1, 2, 4):
                    c_sh = jnp.where(row8 < 8 - k, pltpu.roll(C, 8 - k, 0), 1.0)
                    b_sh = jnp.where(row8 < 8 - k, pltpu.roll(B, 8 - k, 0), 0.0)
                    B = B + C * b_sh
                    C = C * c_sh
                local.append((off, C, B))
            for off, C, B in local:
                lam_t = B + C * carry
                l_s[pl.ds(off, 8), :] = lam_t
                carry = lam_t[0:1, :]
            return carry

        lax.fori_loop(0, T // (8 * SCAN_GROUPS), step, jnp.zeros((1, LRU_GROUP), F32))
        lmb = l_s[...]
        gcar[...] = a[0:1, :] * lmb[0:1, :]

        h_prev = _shift_down(hl_v, hlp_ref[...] * keep, 1, row8)
        da = lmb * h_prev
        dmult = lmb * i * xc
        di = lmb * mult * xc
        dxc = lmb * mult * i
        dla = da * a - dmult * (a * a) / mult
        dr = dla * (-LRU_C * sp)
        dlam_ref[...] += _colsum(dla * (-LRU_C * r)) * (-jax.nn.sigmoid(-lam_v))
        dpa = dr * r * (1.0 - r)
        dpx = di * i * (1.0 - i)
        dxc = dxc + _dot(dpa, wa, "nt") + _dot(dpx, wx, "nt")
        dwa_ref[0] += _dot(xc, dpa, "tn")
        dwx_ref[0] += _dot(xc, dpx, "tn")
        dba_ref[...] += _colsum(dpa)
        dbx_ref[...] += _colsum(dpx)
        dcb_ref[...] += _colsum(dxc)
        cw = cw_ref[...]
        n8 = next8[...]
        dxl = cw[3:4] * dxc
        for k in (1, 2, 3):
            dxl = dxl + cw[3 - k:4 - k] * _shift_up(dxc, n8, k, row8)
        for k in range(4):
            dcw_ref[3 - k:4 - k, :] += _colsum(dxc * q["sh"][k])
        next8[...] = dxc[0:8]
        dxl_ref[...] = dxl.astype(dxl_ref.dtype)

    res = pl.pallas_call(
        body, name="lru_bwd", grid=(N_LRU_GROUPS, n_t),
        in_specs=[blk(C_XL // LRU_GROUP), prev8s(C_XL // LRU_GROUP), blk(C_GATE // LRU_GROUP), blk(0), prev8s(0), blk(0),
                  blk(0), blk(0), vec(4), vec(1), wbd, wbd, vec(1), vec(1), vec(1)],
        out_specs=[blk(0), blk(0), vec(4), vec(1), wbd, wbd, vec(1), vec(1), vec(1)],
        out_shape=[jax.ShapeDtypeStruct((S, D_RNN), MXU_DTYPE), jax.ShapeDtypeStruct((S, D_RNN), MXU_DTYPE),
                   jax.ShapeDtypeStruct((4, D_RNN), F32), jax.ShapeDtypeStruct((1, D_RNN), F32),
                   jax.ShapeDtypeStruct((N_LRU_GROUPS, LRU_GROUP, LRU_GROUP), F32),
                   jax.ShapeDtypeStruct((N_LRU_GROUPS, LRU_GROUP, LRU_GROUP), F32),
                   jax.ShapeDtypeStruct((1, D_RNN), F32), jax.ShapeDtypeStruct((1, D_RNN), F32),
                   jax.ShapeDtypeStruct((1, D_RNN), F32)],
        scratch_shapes=[pltpu.VMEM((8, LRU_GROUP), F32), pltpu.VMEM((1, LRU_GROUP), F32),
                        pltpu.VMEM((T, LRU_GROUP), F32), pltpu.VMEM((T, LRU_GROUP), F32), pltpu.VMEM((T, LRU_GROUP), F32)],
        compiler_params=_cparams(("parallel", "arbitrary")),
    )(proj, proj, proj, hl, hl, a_fwd, mult_fwd, dz, conv_w, conv_b, wa_bd, wx_bd, b_a, b_x, lam)
    return res


def _block_diag(w):
    w4 = w.reshape(N_LRU_GROUPS, 4, LRU_BLOCK, 1, LRU_BLOCK)
    eye = jnp.eye(4, dtype=w.dtype).reshape(1, 4, 1, 4, 1)
    return (w4 * eye).reshape(N_LRU_GROUPS, LRU_GROUP, LRU_GROUP)


def _block_diag_extract(wbd):
    w5 = wbd.reshape(N_LRU_GROUPS, 4, LRU_BLOCK, 4, LRU_BLOCK)
    return jnp.stack([w5[:, a, :, a, :] for a in range(4)], axis=1).reshape(N_LRU_BLOCKS, LRU_BLOCK, LRU_BLOCK)


def _t5_bucket(dist):
    max_exact = NUM_BUCKETS // 2
    df = jnp.maximum(dist, 1).astype(jnp.float32)
    large = max_exact + (jnp.log(df / max_exact) / math.log(MAX_DISTANCE / max_exact)
                         * (NUM_BUCKETS - max_exact)).astype(jnp.int32)
    large = jnp.minimum(large, NUM_BUCKETS - 1)
    return jnp.where(dist < max_exact, dist, large)


def _band_offsets():
    qi = jnp.arange(SPAN)[:, None]
    kj = jnp.arange(2 * SPAN)[None, :]
    return qi + SPAN - kj


def _dil_buckets():
    off = _band_offsets()
    return jnp.stack([_t5_bucket(jnp.maximum(off, 0) * dil) for _, dil in DIL_GROUPS]).astype(jnp.int32)


def _dil_bias(rel_bias, buckets):
    def body(tbl_ref, bk_ref, o_ref):
        g = pl.program_id(0)
        qi = lax.broadcasted_iota(jnp.int32, (SPAN, 2 * SPAN), 0)
        kj = lax.broadcasted_iota(jnp.int32, (SPAN, 2 * SPAN), 1)
        off = qi + SPAN - kj
        valid = (off >= 0) & (off <= SPAN)
        bk = bk_ref[0]
        for h in range(DIL_HEADS):
            acc = jnp.zeros((SPAN, 2 * SPAN), F32)
            for b in range(NUM_BUCKETS):
                acc = jnp.where(bk == b, tbl_ref[b, g * DIL_HEADS + h], acc)
            o_ref[0, h] = jnp.where(valid, acc, NEG)

    return pl.pallas_call(
        body, name="dil_bias", grid=(3,),
        in_specs=[pl.BlockSpec(memory_space=pltpu.SMEM), pl.BlockSpec((1, SPAN, 2 * SPAN), lambda g: (g, 0, 0))],
        out_specs=pl.BlockSpec((1, DIL_HEADS, SPAN, 2 * SPAN), lambda g: (g, 0, 0, 0)),
        out_shape=jax.ShapeDtypeStruct((3, DIL_HEADS, SPAN, 2 * SPAN), F32),
        compiler_params=_cparams(("parallel",)),
    )(rel_bias, buckets)


def _dil_bias_bwd(dbias, buckets):
    def body(db_ref, bk_ref, o_ref):
        lane = lax.broadcasted_iota(jnp.int32, (1, 128), 1)
        rows = [jnp.zeros((1, 128), F32) for _ in range(NUM_BUCKETS)]
        for g in range(3):
            bk = bk_ref[g]
            for h in range(DIL_HEADS):
                d = db_ref[g, h]
                for b in range(NUM_BUCKETS):
                    tot = jnp.sum(_colsum(jnp.where(bk == b, d, 0.0)), axis=1, keepdims=True)
                    rows[b] = jnp.where(lane == g * DIL_HEADS + h, tot, rows[b])
        for b in range(NUM_BUCKETS):
            o_ref[b:b + 1, :] = rows[b]

    return pl.pallas_call(
        body, name="dil_bias_bwd",
        out_shape=jax.ShapeDtypeStruct((NUM_BUCKETS, 128), F32),
        compiler_params=_cparams(),
    )(dbias, buckets)


DIL_SUBBLOCKS = (8, 4, 1)


def _dil_layout(g, S):
    dil, m = DIL_GROUPS[g][1], DIL_SUBBLOCKS[g]
    sub = SPAN * dil
    col = [(C_QKV + t * 768 + g * 256) // 128 for t in range(3)]
    return dil, m, sub, S // (sub * m), col


def _residue_rows(b, r, dil):
    return pl.ds(b * SPAN * dil + r, SPAN, stride=dil) if dil > 1 else pl.ds(b * SPAN, SPAN)


def _for_residues(dil, fn):
    if dil <= 4:
        for r in range(dil):
            fn(r)
    else:
        lax.fori_loop(0, dil, lambda r, c: (fn(r), c)[1], 0, unroll=4)


def _pair_scores(qm, k2, bias, first_cols):
    s = _dot(qm, k2, "nt") * (DIL_HEAD_DIM ** -0.5) + bias
    kj = lax.broadcasted_iota(jnp.int32, s.shape, 1)
    return jnp.where(kj < first_cols, NEG, s)


def _dilated_fwd(proj, bias, g, *, S):
    dil, m, sub, nc, (qc, kc, vc) = _dil_layout(g, S)
    R = sub * m
    cur = lambda cb: pl.BlockSpec((R, 128), lambda p, i: (i, cb + p))
    prv = lambda cb: pl.BlockSpec((sub, 128), lambda p, i: (jnp.maximum(i * m - 1, 0), cb + p))
    out = pl.BlockSpec((R, 128), lambda p, i: (i, p))

    def body(q_ref, kp_ref, kc_ref, vp_ref, vc_ref, b_ref, o_ref, lse_ref):
        lane = lax.broadcasted_iota(jnp.int32, (SPAN, 128), 1)
        sels = (lane < DIL_HEAD_DIM, lane >= DIL_HEAD_DIM)
        for b in range(m):
            first_cols = jnp.where(pl.program_id(1) == 0, SPAN, 0) if b == 0 else 0

            def one(r, b=b, first_cols=first_cols):
                rows = _residue_rows(b, r, dil)
                before = (kc_ref, vc_ref, _residue_rows(b - 1, r, dil)) if b else (kp_ref, vp_ref, _residue_rows(0, r, dil))
                q2 = q_ref[rows, :]
                k2 = _mx(jnp.concatenate([before[0][before[2], :], kc_ref[rows, :]], axis=0))
                v2 = _mx(jnp.concatenate([before[1][before[2], :], vc_ref[rows, :]], axis=0))
                qq = jnp.concatenate([jnp.where(sels[0], q2, 0.0), jnp.where(sels[1], q2, 0.0)], axis=0)
                s = _pair_scores(qq, k2, b_ref[0, 0], first_cols)
                mx = jnp.max(s, axis=-1, keepdims=True)
                p = jnp.exp(s - mx)
                den = jnp.sum(p, axis=-1, keepdims=True)
                o = _dot(p, v2) / den
                st = mx + jnp.log(den)
                o_ref[rows, :] = jnp.where(sels[0], o[0:SPAN], o[SPAN:2 * SPAN])
                lse_ref[rows, :] = jnp.where(lane == 0, st[0:SPAN], jnp.where(lane == 1, st[SPAN:2 * SPAN], 0.0))

            _for_residues(dil, one)

    return pl.pallas_call(
        body, name=f"dil_fwd{g}", grid=(2, nc),
        in_specs=[cur(qc), prv(kc), cur(kc), prv(vc), cur(vc),
                  pl.BlockSpec((1, 1, 2 * SPAN, 2 * SPAN), lambda p, i: (g, p, 0, 0))],
        out_specs=[out, out],
        out_shape=[jax.ShapeDtypeStruct((S, 256), F32), jax.ShapeDtypeStruct((S, 256), F32)],
        compiler_params=_cparams(("parallel", "parallel")),
    )(proj, proj, proj, proj, proj, bias.reshape(3, 2, 2 * SPAN, 2 * SPAN))


def _dilated_bwd(proj, do, lse, delta, bias, g, *, S, into=None):
    dil, m, sub, nc, (qc, kc, vc) = _dil_layout(g, S)
    R = sub * m
    cl = lambda i: jnp.minimum(i, nc - 1)
    cur = lambda cb: pl.BlockSpec((R, 128), lambda p, i: (cl(i), cb + p))
    prv = lambda cb: pl.BlockSpec((sub, 128), lambda p, i: (jnp.maximum(cl(i) * m - 1, 0), cb + p))
    q_out = pl.BlockSpec((R, 128), lambda p, i: (cl(i), 2 * g + p))
    kv_out = pl.BlockSpec((R, 128), lambda p, i: (jnp.maximum(i - 1, 0), 2 * g + p))
    scale = DIL_HEAD_DIM ** -0.5
    n_into = 0 if into is None else 3

    def body(q_ref, kp_ref, kc_ref, vp_ref, vc_ref, do_ref, lse_ref, dl_ref, b_ref, *rest):
        dq_ref, dk_ref, dv_ref, db_ref, dq_s, kc_s, vc_s, kp_s, vp_s, kcar, vcar = rest[n_into:]
        i = pl.program_id(1)

        @pl.when(i == 0)
        def _():
            kcar[...] = jnp.zeros_like(kcar)
            vcar[...] = jnp.zeros_like(vcar)
            db_ref[...] = jnp.zeros_like(db_ref)

        @pl.when(i < nc)
        def _():
            lane = lax.broadcasted_iota(jnp.int32, (SPAN, 128), 1)
            sels = (lane < DIL_HEAD_DIM, lane >= DIL_HEAD_DIM)
            for b in range(m):
                first_cols = jnp.where(i == 0, SPAN, 0) if b == 0 else 0

                def one(r, b=b, first_cols=first_cols):
                    rows = _residue_rows(b, r, dil)
                    rows_before = _residue_rows(b - 1 if b else 0, r, dil)
                    k_before, v_before = (kc_ref, vc_ref) if b else (kp_ref, vp_ref)
                    q2, do2 = q_ref[rows, :], do_ref[rows, :]
                    k2 = _mx(jnp.concatenate([k_before[rows_before, :], kc_ref[rows, :]], axis=0))
                    v2 = _mx(jnp.concatenate([v_before[rows_before, :], vc_ref[rows, :]], axis=0))
                    lse_t, dl_t = lse_ref[rows, :], dl_ref[rows, :]
                    qq = _mx(jnp.concatenate([jnp.where(sels[0], q2, 0.0), jnp.where(sels[1], q2, 0.0)], axis=0))
                    dd = _mx(jnp.concatenate([jnp.where(sels[0], do2, 0.0), jnp.where(sels[1], do2, 0.0)], axis=0))
                    lse2 = jnp.concatenate([lse_t[:, 0:1], lse_t[:, 1:2]], axis=0)
                    dl2 = jnp.concatenate([dl_t[:, 0:1], dl_t[:, 1:2]], axis=0)
                    p = jnp.exp(_pair_scores(qq, k2, b_ref[0, 0], first_cols) - lse2)
                    ds = p * (_dot(dd, v2, "nt") - dl2)
                    db_ref[0] += ds
                    dqq = _dot(ds, k2) * scale
                    dq2 = jnp.where(sels[0], dqq[0:SPAN], dqq[SPAN:2 * SPAN])
                    dk2 = _dot(ds, qq, "tn") * scale
                    dv2 = _dot(p, dd, "tn")
                    dq_s[rows, :] = dq2
                    kc_s[rows, :] = dk2[SPAN:2 * SPAN]
                    vc_s[rows, :] = dv2[SPAN:2 * SPAN]
                    if b:
                        kc_s[rows_before, :] += dk2[0:SPAN]
                        vc_s[rows_before, :] += dv2[0:SPAN]
                    else:
                        kp_s[rows_before, :] = dk2[0:SPAN]
                        vp_s[rows_before, :] = dv2[0:SPAN]

                _for_residues(dil, one)
            dq_ref[...] = dq_s[...].astype(dq_ref.dtype)
            last = pl.ds((m - 1) * sub, sub)
            kcar[last, :] += kp_s[...]
            vcar[last, :] += vp_s[...]
            dk_ref[...] = kcar[...].astype(dk_ref.dtype)
            dv_ref[...] = vcar[...].astype(dv_ref.dtype)
            kcar[...] = kc_s[...]
            vcar[...] = vc_s[...]

        @pl.when(i == nc)
        def _():
            dk_ref[...] = kcar[...].astype(dk_ref.dtype)
            dv_ref[...] = vcar[...].astype(dv_ref.dtype)

    stat = pl.BlockSpec((R, 128), lambda p, i: (cl(i), p))
    big = jax.ShapeDtypeStruct((S, len(DIL_GROUPS) * 256), MXU_DTYPE)
    return pl.pallas_call(
        body, name=f"dil_bwd{g}", grid=(2, nc + 1),
        in_specs=[cur(qc), prv(kc), cur(kc), prv(vc), cur(vc), stat, stat, stat,
                  pl.BlockSpec((1, 1, 2 * SPAN, 2 * SPAN), lambda p, i: (g, p, 0, 0))]
        + [pl.BlockSpec(memory_space=pl.ANY)] * n_into,
        out_specs=[q_out, kv_out, kv_out, pl.BlockSpec((1, 2 * SPAN, 2 * SPAN), lambda p, i: (p, 0, 0))],
        out_shape=[big, big, big, jax.ShapeDtypeStruct((2, 2 * SPAN, 2 * SPAN), F32)],
        input_output_aliases={9 + j: j for j in range(n_into)},
        scratch_shapes=[pltpu.VMEM((R, 128), F32)] * 3 + [pltpu.VMEM((sub, 128), F32)] * 2 + [pltpu.VMEM((R, 128), F32)] * 2,
        compiler_params=_cparams(("parallel", "arbitrary")),
    )(proj, proj, proj, proj, proj, do, lse, delta, bias.reshape(3, 2, 2 * SPAN, 2 * SPAN), *(into or ()))


def _dilated_merge(os_, lses, *, S, bt=512):
    tile = pl.BlockSpec((bt, 128), lambda i, p: (i, p))

    def body(o0, o1, o2, l0, l1, l2, o_ref, om_ref, lse_ref):
        lane = lax.broadcasted_iota(jnp.int32, (bt, 128), 1)
        lo = lane < DIL_HEAD_DIM
        ls = [l0[...], l1[...], l2[...]]
        ws, stat = [], jnp.zeros((bt, 128), F32)
        for e in range(2):
            a = [l[:, e:e + 1] for l in ls]
            m = jnp.maximum(jnp.maximum(a[0], a[1]), a[2])
            ex = [jnp.exp(v - m) for v in a]
            tot = ex[0] + ex[1] + ex[2]
            ws.append([v / tot for v in ex])
            stat = jnp.where(lane == e, m + jnp.log(tot), stat)
        acc = jnp.zeros((bt, 128), F32)
        for gi, o in enumerate((o0, o1, o2)):
            acc = acc + jnp.where(lo, ws[0][gi], ws[1][gi]) * o[...]
        o_ref[...] = acc
        om_ref[...] = _mx(acc)
        lse_ref[...] = stat

    return pl.pallas_call(
        body, name="dil_merge", grid=(S // bt, 2),
        in_specs=[tile] * 6, out_specs=[tile, tile, tile],
        out_shape=[jax.ShapeDtypeStruct((S, 256), F32), jax.ShapeDtypeStruct((S, 256), MXU_DTYPE),
                   jax.ShapeDtypeStruct((S, 256), F32)],
        compiler_params=_cparams(("parallel", "parallel")),
    )(*os_, *lses)


def _with_delta(do, o):
    lane = lax.broadcasted_iota(jnp.int32, (do.shape[0], 128), 1)
    stats = []
    for p in range(2):
        prod = do[:, 128 * p:128 * (p + 1)] * o[:, 128 * p:128 * (p + 1)]
        d0 = jnp.sum(jnp.where(lane < DIL_HEAD_DIM, prod, 0.0), axis=-1, keepdims=True)
        d1 = jnp.sum(jnp.where(lane >= DIL_HEAD_DIM, prod, 0.0), axis=-1, keepdims=True)
        stats.append(jnp.where(lane == 0, d0, jnp.where(lane == 1, d1, 0.0)))
    return do, jnp.concatenate(stats, axis=1)


MEM_T = 2048
QM_BLK = C_QM // MEM_HEAD_DIM


def _mem_attn_fwd(proj, kv, *, S):
    scale = MEM_HEAD_DIM ** -0.5

    def body(q_ref, k_ref, v_ref, o_ref, om_ref, lse_ref):
        s = _dot(q_ref[...], k_ref[...], "nt") * scale
        m = jnp.max(s, axis=-1, keepdims=True)
        p = jnp.exp(s - m)
        den = jnp.sum(p, axis=-1, keepdims=True)
        o = _dot(p, v_ref[...]) / den
        o_ref[...] = o
        om_ref[...] = _mx(o)
        lse_ref[0] = m + jnp.log(den)

    return pl.pallas_call(
        body, name="mem_attn_fwd", grid=(S // MEM_T, MEM_HEADS),
        in_specs=[pl.BlockSpec((MEM_T, MEM_HEAD_DIM), lambda i, h: (i, QM_BLK + h)),
                  pl.BlockSpec((N_MEM, MEM_HEAD_DIM), lambda i, h: (0, h)),
                  pl.BlockSpec((N_MEM, MEM_HEAD_DIM), lambda i, h: (0, MEM_HEADS + h))],
        out_specs=[pl.BlockSpec((MEM_T, MEM_HEAD_DIM), lambda i, h: (i, h)),
                   pl.BlockSpec((MEM_T, MEM_HEAD_DIM), lambda i, h: (i, h)),
                   pl.BlockSpec((1, MEM_T, 1), lambda i, h: (h, i, 0))],
        out_shape=[jax.ShapeDtypeStruct((S, MEM_WIDTH), F32), jax.ShapeDtypeStruct((S, MEM_WIDTH), MXU_DTYPE),
                   jax.ShapeDtypeStruct((MEM_HEADS, S, 1), F32)],
        compiler_params=_cparams(("parallel", "parallel")),
    )(proj, kv, kv)


def _mem_attn_bwd(proj, kv, om, lse, dom, *, S):
    scale = MEM_HEAD_DIM ** -0.5

    def body(q_ref, k_ref, v_ref, o_ref, lse_ref, do_ref, dq_ref, dk_ref, dv_ref):
        @pl.when(pl.program_id(1) == 0)
        def _():
            dk_ref[...] = jnp.zeros_like(dk_ref)
            dv_ref[...] = jnp.zeros_like(dv_ref)

        qv, kv_, vv, dov = q_ref[...], k_ref[...], v_ref[...], do_ref[...]
        p = jnp.exp(_dot(qv, kv_, "nt") * scale - lse_ref[0])
        delta = jnp.sum(dov * o_ref[...], axis=-1, keepdims=True)
        ds = p * (_dot(dov, vv, "nt") - delta)
        dq_ref[...] = (_dot(ds, kv_) * scale).astype(dq_ref.dtype)
        dk_ref[...] += _dot(ds, qv, "tn") * scale
        dv_ref[...] += _dot(p, dov, "tn")

    tile = pl.BlockSpec((MEM_T, MEM_HEAD_DIM), lambda h, i: (i, h))
    kvo = pl.BlockSpec((N_MEM, MEM_HEAD_DIM), lambda h, i: (0, h))
    return pl.pallas_call(
        body, name="mem_attn_bwd", grid=(MEM_HEADS, S // MEM_T),
        in_specs=[pl.BlockSpec((MEM_T, MEM_HEAD_DIM), lambda h, i: (i, QM_BLK + h)),
                  pl.BlockSpec((N_MEM, MEM_HEAD_DIM), lambda h, i: (0, h)),
                  pl.BlockSpec((N_MEM, MEM_HEAD_DIM), lambda h, i: (0, MEM_HEADS + h)),
                  tile, pl.BlockSpec((1, MEM_T, 1), lambda h, i: (h, i, 0)), tile],
        out_specs=[tile, kvo, kvo],
        out_shape=[jax.ShapeDtypeStruct((S, MEM_WIDTH), MXU_DTYPE), jax.ShapeDtypeStruct((N_MEM, MEM_WIDTH), F32),
                   jax.ShapeDtypeStruct((N_MEM, MEM_WIDTH), F32)],
        compiler_params=_cparams(("parallel", "arbitrary")),
    )(proj, kv, kv, om, lse, dom)


MIX_BM = 1024
MIX_BN = 256
GATES_BLK = C_GATES // MIX_BN


def _mix_specs(j_outer):
    ix = (lambda f: (lambda j, i: f(i, j))) if j_outer else (lambda f: f)
    act = lambda width: pl.BlockSpec((MIX_BM, width), ix(lambda i, j: (i, 0)))
    wgt = lambda width: pl.BlockSpec((width, MIX_BN), ix(lambda i, j: (0, j)))
    gate = lambda b: pl.BlockSpec((MIX_BM, MIX_BN), ix(lambda i, j: (i, GATES_BLK + 4 * b + j)))
    bias = lambda b: pl.BlockSpec((1, MIX_BN), ix(lambda i, j: (0, 4 * b + j)))
    tile = pl.BlockSpec((MIX_BM, MIX_BN), ix(lambda i, j: (i, j)))
    return act, wgt, gate, bias, tile


def _mix_fwd(z_lru, o_dil, om, w_lru, w_dil, w_mem, proj, b_gate, *, S):
    act, wgt, gate, bias, tile = _mix_specs(False)

    def body(zl, od, mo, wl, wd, wm, g0, g1, g2, b0, b1, b2, o_ref):
        acc = jax.nn.sigmoid(g0[...] + b0[...]) * _dot(zl[...], wl[...])
        acc += jax.nn.sigmoid(g1[...] + b1[...]) * _dot(od[...], wd[...])
        acc += jax.nn.sigmoid(g2[...] + b2[...]) * _dot(mo[...], wm[...])
        o_ref[...] = acc.astype(o_ref.dtype)

    return pl.pallas_call(
        body, name="mix_fwd", grid=(S // MIX_BM, D_MODEL // MIX_BN),
        in_specs=[act(D_RNN), act(256), act(MEM_WIDTH), wgt(D_RNN), wgt(256), wgt(MEM_WIDTH),
                  gate(0), gate(1), gate(2), bias(0), bias(1), bias(2)],
        out_specs=tile, out_shape=jax.ShapeDtypeStruct((S, D_MODEL), MXU_DTYPE),
        compiler_params=_cparams(("parallel", "parallel")),
    )(z_lru, o_dil, om, w_lru, w_dil, w_mem, proj, proj, proj, b_gate, b_gate, b_gate)


def _mix_bwd(dx1, w_out, z_lru, o_dil, om, w_lru, w_dil, w_mem, proj, b_gate, *, S):
    act, wgt, gate, bias, tile = _mix_specs(False)
    n_j = D_MODEL // MIX_BN

    def body(dx, wo, zl, od, mo, wl, wd, wm, g0, g1, g2, b0, b1, b2,
             dg0, dg1, dg2, dy0, dy1, dy2, db0, db1, db2):
        j = pl.program_id(1)

        @pl.when((pl.program_id(0) == 0) & (j == 0))
        def _():
            for r in (db0, db1, db2):
                r[...] = jnp.zeros_like(r)

        dmv = _dot(dx[...], wo[...], "nt")
        for act_ref, w_ref, g_ref, b_ref, dg_ref, dy_ref, db_ref in (
                (zl, wl, g0, b0, dg0, dy0, db0), (od, wd, g1, b1, dg1, dy1, db1), (mo, wm, g2, b2, dg2, dy2, db2)):
            y = _dot(act_ref[...], w_ref[...])
            gt = jax.nn.sigmoid(g_ref[...] + b_ref[...])
            dgate = dmv * y * gt * (1.0 - gt)
            dg_ref[...] = dgate.astype(dg_ref.dtype)
            dy_ref[...] = (dmv * gt).astype(dy_ref.dtype)
            db_ref[j] += _colsum(dgate)

    big = jax.ShapeDtypeStruct((S, D_MODEL), MXU_DTYPE)
    vec = jax.ShapeDtypeStruct((n_j, 1, MIX_BN), F32)
    vspec = pl.BlockSpec((n_j, 1, MIX_BN), lambda i, j: (0, 0, 0))
    res = pl.pallas_call(
        body, name="mix_bwd", grid=(S // MIX_BM, n_j),
        in_specs=[pl.BlockSpec((MIX_BM, D_MODEL), lambda i, j: (i, 0)), pl.BlockSpec((MIX_BN, D_MODEL), lambda i, j: (j, 0)),
                  act(D_RNN), act(256), act(MEM_WIDTH), wgt(D_RNN), wgt(256), wgt(MEM_WIDTH),
                  gate(0), gate(1), gate(2), bias(0), bias(1), bias(2)],
        out_specs=[tile] * 6 + [vspec] * 3, out_shape=[big] * 6 + [vec] * 3,
        compiler_params=_cparams(("arbitrary", "arbitrary")),
    )(dx1, w_out, z_lru, o_dil, om, w_lru, w_dil, w_mem, proj, proj, proj, b_gate, b_gate, b_gate)
    return list(res[:6]) + [r.reshape(1, D_MODEL) for r in res[6:]]


def _adamw_math(w, g, m, v):
    m = ADAM_B1 * m + (1.0 - ADAM_B1) * g
    v = ADAM_B2 * v + (1.0 - ADAM_B2) * (g * g)
    m_hat = m / (1.0 - ADAM_B1 ** ADAM_STEP)
    v_hat = v / (1.0 - ADAM_B2 ** ADAM_STEP)
    delta = -ADAM_LR * (m_hat / (jnp.sqrt(v_hat) + ADAM_EPS) + ADAM_WD * w)
    return delta, m, v


def _adamw_landed(w, own, land, m, v, *, name, col_blk=0, prev=None):
    R = w.shape[0]
    n_parts, C = land.shape[0], land.shape[2]
    br = next(d for d in (256, 464, 128) if R % d == 0)
    tile = pl.BlockSpec((br, C), lambda i: (i, col_blk))
    part = pl.BlockSpec((br, C), lambda i: (i, 0))
    n_prev = 0 if prev is None else 4

    def body(w_ref, o_ref, l_ref, m_ref, v_ref, *rest):
        g_ref, d_ref, nm_ref, nv_ref = rest[n_prev:]
        g = o_ref[...].astype(F32)
        for p in range(n_parts):
            g = g + l_ref[p].astype(F32)
        d, nm, nv = _adamw_math(w_ref[...], g, m_ref[...], v_ref[...])
        g_ref[...] = g
        d_ref[...] = d
        nm_ref[...] = nm
        nv_ref[...] = nv

    return pl.pallas_call(
        body, name=name, grid=(R // br,),
        in_specs=[tile, part, pl.BlockSpec((n_parts, br, C), lambda i: (0, i, 0)), tile, tile]
        + [pl.BlockSpec(memory_space=pl.ANY)] * n_prev,
        out_specs=[tile] * 4, out_shape=[jax.ShapeDtypeStruct(w.shape, F32)] * 4,
        input_output_aliases={5 + j: j for j in range(n_prev)},
        compiler_params=_cparams(("parallel",)),
    )(w, own, land, m, v, *(prev or ()))


def _adamw_plain(w, g, m, v, *, name):
    def body(w_ref, g_ref, m_ref, v_ref, d_ref, nm_ref, nv_ref):
        d, nm, nv = _adamw_math(w_ref[...], g_ref[...], m_ref[...], v_ref[...])
        d_ref[...] = d
        nm_ref[...] = nm
        nv_ref[...] = nv

    return pl.pallas_call(
        body, name=name, out_shape=[jax.ShapeDtypeStruct(w.shape, F32)] * 3, compiler_params=_cparams(),
    )(w, g, m, v)


def _my_pos():
    return lax.axis_index("x"), lax.axis_index("y"), lax.axis_index("c")


def _dev_index(p):
    return 4 * p[0] + 2 * p[1] + p[2]


def _all_gather(shards):
    n = len(shards)
    hbm = pl.BlockSpec(memory_space=pl.ANY)

    def body(*refs):
        ins, outs = refs[:n], refs[n:2 * n]
        send_sems, recv_sems, local_sems = refs[2 * n:]
        x, y, c = _my_pos()
        me, sibling = (x, y, c), (x, y, 1 - c)
        chips = [(1 - x, y), (x, 1 - y), (1 - x, 1 - y)]

        def copy(a, k, block, to, src=None):
            dst = outs[a].at[_dev_index(block)]
            return pltpu.make_async_remote_copy(
                src_ref=dst if src is None else src, dst_ref=dst,
                send_sem=send_sems.at[a, k], recv_sem=recv_sems.at[a, k], device_id=to, device_id_type=MESH)

        mine = [pltpu.make_async_copy(ins[a], outs[a].at[_dev_index(me)], local_sems.at[a]) for a in range(n)]
        for cp in mine:
            cp.start()
        first = []
        for a in range(n):
            first.append(copy(a, 0, me, sibling, src=ins[a]))
            first += [copy(a, 1 + j, me, (*chip, c), src=ins[a]) for j, chip in enumerate(chips)]
        for cp in first:
            cp.start()
        passed = []
        for j, chip in enumerate(chips):
            for a in range(n):
                copy(a, 1 + j, (*chip, c), me).wait_recv()
                fwd = copy(a, 4 + j, (*chip, c), sibling)
                fwd.start()
                passed.append(fwd)
        for a in range(n):
            copy(a, 0, sibling, me).wait_recv()
        for j, chip in enumerate(chips):
            for a in range(n):
                copy(a, 4 + j, (*chip, 1 - c), me).wait_recv()
        for cp in first + passed:
            cp.wait_send()
        for cp in mine:
            cp.wait()

    return pl.pallas_call(
        body, name="all_gather_weights",
        in_specs=[hbm] * n, out_specs=[hbm] * n,
        out_shape=[jax.ShapeDtypeStruct((N_DEV,) + s.shape, s.dtype) for s in shards],
        scratch_shapes=[pltpu.SemaphoreType.DMA((n, 7)), pltpu.SemaphoreType.DMA((n, 7)), pltpu.SemaphoreType.DMA((n,))],
        compiler_params=pltpu.CompilerParams(has_side_effects=True),
    )(*shards)


def _peers(me):
    x, y, c = me
    out = []
    for k in range(1, 8):
        fx, fy, fc = (k >> 2) & 1, (k >> 1) & 1, k & 1
        out.append((k - 1, (1 - x if fx else x, 1 - y if fy else y, 1 - c if fc else c)))
    return out


HBM_SPEC = pl.BlockSpec(memory_space=pltpu.HBM)
SEM_SPEC = pl.BlockSpec(memory_space=pltpu.SEMAPHORE)
DATAFLOW_EFFECT = pltpu.SideEffectType.DATAFLOW_SIDE_EFFECTING


def _gather_refs(src, land, me, peer, k):
    return src, land.at[_dev_index(me)]


def _scatter_refs(src, land, me, peer, k):
    return src.at[_dev_index(peer)], land.at[k]


ALL_RELATIONS = tuple(range(7))
ONE_PER_CHIP = (0, 1, 3, 5)


def _push_start(srcs, land_shapes, refs_of, name, after=(), relations=ALL_RELATIONS):
    n, n_after = len(srcs), len(after)

    def body(*refs):
        ins, lands = refs[:n], refs[n:2 * n]
        send_sems, recv_sems, token = refs[2 * n + n_after], refs[2 * n + n_after + 1], refs[-1]
        me = _my_pos()
        for k, peer in _peers(me):
            if k not in relations:
                continue
            for a in range(n):
                src, dst = refs_of(ins[a], lands[a], me, peer, k)
                pltpu.make_async_remote_copy(src_ref=src, dst_ref=dst, send_sem=send_sems.at[7 * a + k],
                                             recv_sem=recv_sems.at[7 * a + k], device_id=peer, device_id_type=MESH).start()
        token[...] = jnp.zeros_like(token)

    lands = [lax.empty(shp, s.dtype) for shp, s in zip(land_shapes, srcs)]
    hbm = lambda a: pltpu.with_memory_space_constraint(a, pltpu.HBM)
    res = pl.pallas_call(
        body, name=name,
        out_shape=(pltpu.SemaphoreType.DMA((7 * n,)), pltpu.SemaphoreType.DMA((7 * n,)),
                   *[pltpu.HBM(s.shape, s.dtype) for s in srcs], *[pltpu.HBM(l.shape, l.dtype) for l in lands],
                   jax.ShapeDtypeStruct((8, 128), F32)),
        in_specs=[HBM_SPEC] * (2 * n) + [pl.BlockSpec(memory_space=pl.ANY)] * n_after,
        out_specs=(SEM_SPEC, SEM_SPEC, *[HBM_SPEC] * (2 * n), pl.BlockSpec(memory_space=pltpu.VMEM)),
        input_output_aliases={i: 2 + i for i in range(2 * n)},
        compiler_params=pltpu.CompilerParams(has_side_effects=DATAFLOW_EFFECT),
    )(*[hbm(s) for s in srcs], *[hbm(l) for l in lands], *after)
    return dict(sems=(res[0], res[1]), srcs=list(res[2:2 + n]), lands=list(res[2 + n:2 + 2 * n]), token=res[-1], n=n,
                refs_of=refs_of, name=name, relations=relations)


def _push_wait(started, after):
    n, refs_of, relations = started["n"], started["refs_of"], started["relations"]
    after = list(after) if isinstance(after, (list, tuple)) else [after]

    def body(*refs):
        ins, lands = refs[:n], refs[n:2 * n]
        send_sems, recv_sems = refs[2 * n], refs[2 * n + 1]
        me = _my_pos()
        for k, peer in _peers(me):
            if k not in relations:
                continue
            for a in range(n):
                src, dst = refs_of(ins[a], lands[a], me, peer, k)
                cp = pltpu.make_async_remote_copy(src_ref=src, dst_ref=dst, send_sem=send_sems.at[7 * a + k],
                                                  recv_sem=recv_sems.at[7 * a + k], device_id=peer, device_id_type=MESH)
                cp.wait_send()
                cp.wait_recv()

    arrs = started["srcs"] + started["lands"]
    res = pl.pallas_call(
        body, name=started["name"].replace("start", "wait"),
        out_shape=tuple(pltpu.HBM(a.shape, a.dtype) for a in arrs),
        in_specs=[HBM_SPEC] * (2 * n) + [SEM_SPEC, SEM_SPEC] + [pl.BlockSpec(memory_space=pl.ANY)] * len(after),
        out_specs=tuple([HBM_SPEC] * (2 * n)),
        input_output_aliases={i: i for i in range(2 * n)},
        compiler_params=pltpu.CompilerParams(has_side_effects=DATAFLOW_EFFECT),
    )(*arrs, *started["sems"], *after)
    return list(res[n:2 * n])


def _other_chips(x, y):
    return ((1 - x, y), (x, 1 - y), (1 - x, 1 - y))


def _forward_start(land, name, after=()):
    n_after = len(after)

    def body(*refs):
        land_ref, send_sems, recv_sems, token = refs[0], refs[1 + n_after], refs[2 + n_after], refs[-1]
        x, y, c = _my_pos()
        for j, (cx, cy) in enumerate(_other_chips(x, y)):
            blk = land_ref.at[_dev_index((cx, cy, c))]
            pltpu.make_async_remote_copy(src_ref=blk, dst_ref=blk, send_sem=send_sems.at[j], recv_sem=recv_sems.at[j],
                                         device_id=(x, y, 1 - c), device_id_type=MESH).start()
        token[...] = jnp.zeros_like(token)

    res = pl.pallas_call(
        body, name=name,
        out_shape=(pltpu.SemaphoreType.DMA((3,)), pltpu.SemaphoreType.DMA((3,)), pltpu.HBM(land.shape, land.dtype),
                   jax.ShapeDtypeStruct((8, 128), F32)),
        in_specs=[HBM_SPEC] + [pl.BlockSpec(memory_space=pl.ANY)] * n_after,
        out_specs=(SEM_SPEC, SEM_SPEC, HBM_SPEC, pl.BlockSpec(memory_space=pltpu.VMEM)),
        input_output_aliases={0: 2},
        compiler_params=pltpu.CompilerParams(has_side_effects=DATAFLOW_EFFECT),
    )(pltpu.with_memory_space_constraint(land, pltpu.HBM), *after)
    return dict(sems=(res[0], res[1]), land=res[2], token=res[3], name=name)


def _forward_wait(started, after):
    after = list(after) if isinstance(after, (list, tuple)) else [after]

    def body(land_ref, send_sems, recv_sems, *rest):
        x, y, c = _my_pos()
        for j, (cx, cy) in enumerate(_other_chips(x, y)):
            cp = pltpu.make_async_remote_copy(
                src_ref=land_ref.at[_dev_index((cx, cy, c))], dst_ref=land_ref.at[_dev_index((cx, cy, 1 - c))],
                send_sem=send_sems.at[j], recv_sem=recv_sems.at[j], device_id=(x, y, 1 - c), device_id_type=MESH)
            cp.wait_send()
            cp.wait_recv()

    land = started["land"]
    return pl.pallas_call(
        body, name=started["name"].replace("start", "wait"), out_shape=pltpu.HBM(land.shape, land.dtype),
        in_specs=[HBM_SPEC, SEM_SPEC, SEM_SPEC] + [pl.BlockSpec(memory_space=pl.ANY)] * len(after),
        out_specs=HBM_SPEC, input_output_aliases={0: 0},
        compiler_params=pltpu.CompilerParams(has_side_effects=DATAFLOW_EFFECT),
    )(land, *started["sems"], *after)


def _sum_slots(slots):
    def body(in_ref, out_ref):
        acc = in_ref[0]
        for d in range(1, N_DEV):
            acc = acc + in_ref[d]
        out_ref[...] = acc

    return pl.pallas_call(body, name="sum_small", out_shape=jax.ShapeDtypeStruct(slots.shape[1:], F32),
                          compiler_params=_cparams())(slots)


def _adamw_many(ws, gs, ms, vs):
    n = len(ws)

    def body(*refs):
        for i in range(n):
            w_ref, g_ref, m_ref, v_ref = (refs[j * n + i] for j in range(4))
            d_, nm, nv = _adamw_math(w_ref[...], g_ref[...], m_ref[...], v_ref[...])
            for j, val in enumerate((d_, nm, nv)):
                refs[(4 + j) * n + i][...] = val

    res = pl.pallas_call(body, name="adamw_small", out_shape=[jax.ShapeDtypeStruct(w_.shape, F32) for w_ in ws] * 3,
                         compiler_params=_cparams())(*ws, *gs, *ms, *vs)
    return [(res[i], res[n + i], res[2 * n + i]) for i in range(n)]


def _local_step(x, mem, tgt, W, P, late_weights, send_grads, reduce_small, tie0):
    S = x.shape[0]
    W = dict(W)
    h = _rmsnorm_fwd(x, P["g_mix"] + tie0, rows=S, name="norm_mix")
    mem_n = _rmsnorm_fwd(mem, P["g_mem"], rows=N_MEM, name="norm_mem")
    buckets = _dil_buckets()
    bias = _dil_bias(P["rel_bias"], buckets)
    wa_bd, wx_bd = _mx(_block_diag(P["w_rg_a"])), _mx(_block_diag(P["w_rg_x"]))
    W.update(late_weights("first", [h, mem_n, bias, wa_bd, wx_bd]))
    proj = _matmul(h, W["w_in_t"], M=S, N=D_IN, K=D_MODEL, mode="nt", bm=512, bn=D_IN // 2, bk=D_MODEL, name="mm_in",
                   j_outer=True, deps=[W["started"]])

    group_out = [_dilated_fwd(proj, bias, g, S=S) for g in range(len(DIL_GROUPS))]
    o_dil, o_dil_m, lse_dil = _dilated_merge([o for o, _ in group_out], [l for _, l in group_out], S=S)

    W.update(late_weights("branch", [o_dil]))
    lru_args = (W["conv_w"], P["conv_b"].reshape(1, -1), wa_bd, wx_bd, P["b_rg_a"].reshape(1, -1),
                P["b_rg_x"].reshape(1, -1), P["lru_lambda"].reshape(1, -1))
    hl, z_lru, a_lru, mult_lru = _lru_fwd(proj, *lru_args, S=S)
    kv = _matmul(mem_n, W["w_mem_kv"], M=N_MEM, N=2 * MEM_WIDTH, K=D_MODEL, mode="nn", bm=N_MEM, bn=512, bk=D_MODEL,
                 name="mm_kv")
    om, om_m, lse_mem = _mem_attn_fwd(proj, kv, S=S)
    b_gate = P["b_gate"].reshape(1, -1)
    merged = _mix_fwd(z_lru, o_dil_m, om_m, W["w_lru_out"], W["w_dil_out"], W["w_mem_out"], proj, b_gate, S=S)
    g_mlp, g_final, g_mix = (P[n].reshape(1, D_MODEL) for n in ("g_mlp", "g_final", "g_mix"))
    x1, hm = _matmul_rows(merged, W["w_out"], M=S, K=D_MODEL, mode="nn", bm=512, name="mm_out",
                          row_fn=_residual_then_norm, out_dtypes=(F32, MXU_DTYPE), tiles=[x], vecs=[g_mlp])
    W.update(late_weights("mlp", [hm]))

    def relu2(acc):
        rl = jnp.maximum(acc, 0.0)
        return rl * rl, rl

    act, relu_u = _matmul(hm, W["w_mlp_in_t"], M=S, N=D_FF, K=D_MODEL, mode="nt", bm=1024, bn=1024, bk=D_MODEL,
                          name="mm_mlp_in", out_dtypes=(MXU_DTYPE, MXU_DTYPE), epilogue=relu2, j_outer=True)
    dx2, dx2_m, loss, dg_final = _matmul_rows(
        act, W["w_mlp_out"], M=S, K=D_FF, mode="nn", bm=512, name="mm_mlp_out", row_fn=_residual_then_loss,
        out_dtypes=(F32, MXU_DTYPE), tiles=[x1, tgt], vecs=[g_final], acc_widths=(1, D_MODEL))

    G, Gs = {}, {}
    Gs["g_final"] = dg_final
    dw = dict(mode="tn", K=S, bk=S, out_dtypes=(MXU_DTYPE,))
    G["w_mlp_out"] = _matmul(act, dx2_m, M=D_FF, N=D_MODEL, bm=512, bn=D_MODEL, name="mm_dw_mlp_out",
                             parts=("rows", D_FF // N_DEV), **dw)
    du = _matmul(dx2_m, W["w_mlp_out"], M=S, N=D_FF, K=D_MODEL, mode="nt", bm=1024, bn=1024, bk=D_MODEL, name="mm_du",
                 out_dtypes=(MXU_DTYPE,), epilogue=lambda acc, rl: (acc * (2.0 * rl.astype(F32)),),
                 extras=[(relu_u, (0, 0))], j_outer=True)
    G["w_mlp_in"] = _matmul(hm, du, M=D_MODEL, N=D_FF, bm=D_MODEL, bn=512, name="mm_dw_mlp_in",
                            parts=("cols", D_FF // N_DEV), **dw)
    tie1 = send_grads({n: G.pop(n) for n in ("w_mlp_out", "w_mlp_in")})
    dx1, dx1_m, Gs["g_mlp"] = _matmul_rows(
        du, W["w_mlp_in_t"], M=S, K=D_FF, mode="nn", bm=512, name="mm_dhm", row_fn=_norm_bwd_then_residual(2),
        out_dtypes=(F32, MXU_DTYPE), tiles=[x1, dx2], vecs=[g_mlp], acc_widths=(D_MODEL,), deps=[tie1])
    G["w_out"] = _matmul(merged, dx1_m, M=D_MODEL, N=D_MODEL, bm=512, bn=D_MODEL, name="mm_dw_out",
                         parts=("rows", D_MODEL // N_DEV), **dw)
    (dg0, dg1, dg2, dy_lru, dy_dil, dy_mem, db0, db1, db2) = _mix_bwd(
        dx1_m, W["w_out"], z_lru, o_dil_m, om_m, W["w_lru_out"], W["w_dil_out"], W["w_mem_out"], proj, b_gate, S=S)
    Gs["b_gate0"], Gs["b_gate1"], Gs["b_gate2"] = db0, db1, db2

    G["w_mem_out"] = _matmul(om_m, dy_mem, M=MEM_WIDTH, N=D_MODEL, bm=MEM_WIDTH, bn=D_MODEL, name="mm_dw_mem_out",
                             parts=("cols", D_MODEL // N_DEV), **dw)
    dom = _matmul(dy_mem, W["w_mem_out"], M=S, N=MEM_WIDTH, K=D_MODEL, mode="nt", bm=512, bn=MEM_WIDTH, bk=D_MODEL,
                  name="mm_dom")
    dqm, dk_mem, dv_mem = _mem_attn_bwd(proj, kv, om, lse_mem, dom, S=S)
    dkv = jnp.concatenate([dk_mem, dv_mem], axis=1)
    G["w_mem_kv"] = _matmul(mem_n, dkv, M=D_MODEL, N=2 * MEM_WIDTH, K=N_MEM, mode="tn", bm=D_MODEL, bn=2 * MEM_WIDTH,
                            bk=N_MEM, name="mm_dw_kv", out_dtypes=(MXU_DTYPE,), parts=("rows", D_MODEL // N_DEV))
    dmem_n = _matmul(dkv, W["w_mem_kv"], M=N_MEM, N=D_MODEL, K=2 * MEM_WIDTH, mode="nt", bm=N_MEM, bn=D_MODEL,
                     bk=2 * MEM_WIDTH, name="mm_dmem")
    (Gs["g_mem"],) = _rmsnorm_bwd(mem, P["g_mem"], dmem_n, None, rows=N_MEM, name="norm_mem_bwd", dx_dtypes=())

    G["w_dil_out"] = _matmul(o_dil_m, dy_dil, M=256, N=D_MODEL, bm=256, bn=D_MODEL, name="mm_dw_dil_out",
                             parts=("cols", D_MODEL // N_DEV), **dw)
    do_dil, delta = _matmul(dy_dil, W["w_dil_out"], M=S, N=256, K=D_MODEL, mode="nt", bm=512, bn=256, bk=D_MODEL,
                            name="mm_do_dil", out_dtypes=(F32, F32), epilogue=_with_delta, extras=[(o_dil, (0, 0))])
    G["w_lru_out"] = _matmul(z_lru, dy_lru, M=D_RNN, N=D_MODEL, bm=D_RNN, bn=D_MODEL, name="mm_dw_lru_out",
                             parts=("cols", D_MODEL // N_DEV), **dw)
    dz = _matmul(dy_lru, W["w_lru_out"], M=S, N=D_RNN, K=D_MODEL, mode="nt", bm=512, bn=D_RNN, bk=D_MODEL, name="mm_dz_lru")
    tie2 = send_grads({n: G.pop(n) for n in ("w_out", "w_mem_out", "w_mem_kv", "w_dil_out", "w_lru_out")})
    bias = bias + tie2[0, 0]
    dqkv, dbias = None, []
    for g in range(len(DIL_GROUPS)):
        *dqkv, db_g = _dilated_bwd(proj, do_dil, lse_dil, delta, bias, g, S=S, into=dqkv)
        dbias.append(db_g)
    drel = _dil_bias_bwd(jnp.stack(dbias, axis=0).reshape(len(DIL_GROUPS), DIL_HEADS, SPAN, 2 * SPAN), buckets)
    Gs["rel_bias"] = drel

    dxl, dgl, dcw, dcb, dwa, dwx, dba, dbx, dlam = _lru_bwd(proj, hl, a_lru, mult_lru, dz, *lru_args, S=S)
    Gs["conv_w"], Gs["conv_b"] = dcw, dcb
    Gs["w_rg_a"], Gs["w_rg_x"] = _block_diag_extract(dwa), _block_diag_extract(dwx)
    Gs["b_rg_a"], Gs["b_rg_x"], Gs["lru_lambda"] = dba, dbx, dlam
    Gs["loss"] = loss

    dproj = [dxl, dgl] + dqkv + [dqm, dg0, dg1, dg2]
    tie = []
    for q in range(W_IN_PIECES):
        dw_q = None
        for half in range(2):
            dw_q = _dw_in_t_half(h, dproj, q, half, S=S, name=f"mm_dw_in_{q}_{half}", into=dw_q, deps=tie)
        tie = [send_grads({f"w_in_{q}": dw_q})]
    grad_x, Gs["g_mix"] = _matmul_rows(
        dproj, W["w_in_t"], M=S, K=D_IN, mode="nn", bm=256, name="mm_dh", row_fn=_norm_bwd_then_residual(1),
        out_dtypes=(F32,), tiles=[x, dx1], vecs=[g_mix], acc_widths=(D_MODEL,), deps=tie)
    return grad_x, reduce_small(Gs)


BIG = ("w_in", "w_lru_out", "w_dil_out", "w_mem_kv", "w_mem_out", "w_out", "w_mlp_in", "w_mlp_out")
W_IN_PIECES = 2
COL_SHARDED = ("w_lru_out", "w_dil_out", "w_mem_out", "w_mlp_in")
GATHERED_TRANSPOSED = ("w_mlp_in",)
SMALL = ("g_mix", "b_gate", "conv_b", "w_rg_a", "b_rg_a", "w_rg_x", "b_rg_x", "lru_lambda", "rel_bias", "g_mem",
         "g_mlp", "g_final")
WEIGHTS = ("g_mix", "w_in", "b_gate", "conv_w", "conv_b", "w_rg_a", "b_rg_a", "w_rg_x", "b_rg_x", "lru_lambda",
           "w_lru_out", "rel_bias", "w_dil_out", "g_mem", "w_mem_kv", "w_mem_out", "w_out", "g_mlp", "w_mlp_in",
           "w_mlp_out", "g_final")


def _gathered_to_full(name, gathered):
    if name in COL_SHARDED:
        n, r, c = gathered.shape
        return gathered.transpose(1, 0, 2).reshape(r, n * c)
    n, r, c = gathered.shape
    return gathered.reshape(n * r, c)


SMALL_GRADS = (("g_mix", (1, 1024)), ("b_gate0", (1, 1024)), ("b_gate1", (1, 1024)), ("b_gate2", (1, 1024)),
               ("conv_b", (1, 768)), ("w_rg_a", (12, 64, 64)), ("b_rg_a", (1, 768)), ("w_rg_x", (12, 64, 64)),
               ("b_rg_x", (1, 768)), ("lru_lambda", (1, 768)), ("rel_bias", (32, 128)), ("g_mem", (1, 1024)),
               ("g_mlp", (1, 1024)), ("g_final", (1, 1024)), ("conv_w", (4, 768)), ("loss", (1, 1)))


def _pack(parts):
    flat = jnp.concatenate([p.reshape(-1) for p in parts])
    return jnp.pad(flat, (0, (-flat.shape[0]) % 1024)).reshape(-1, 128)


def _unpack(pack, shapes):
    flat = pack.reshape(-1)
    out, off = [], 0
    for shp in shapes:
        size = math.prod(shp)
        out.append(flat[off:off + size].reshape(shp))
        off += size
    return out


def kernel(x, mem, g_mix, w_in, b_gate, conv_w, conv_b, w_rg_a, b_rg_a, w_rg_x, b_rg_x, lru_lambda, w_lru_out, rel_bias, w_dil_out, g_mem, w_mem_kv, w_mem_out, w_out, g_mlp, w_mlp_in, w_mlp_out, g_final, loss_target, m_g_mix, m_w_in, m_b_gate, m_conv_w, m_conv_b, m_w_rg_a, m_b_rg_a, m_w_rg_x, m_b_rg_x, m_lru_lambda, m_w_lru_out, m_rel_bias, m_w_dil_out, m_g_mem, m_w_mem_kv, m_w_mem_out, m_w_out, m_g_mlp, m_w_mlp_in, m_w_mlp_out, m_g_final, v_g_mix, v_w_in, v_b_gate, v_conv_w, v_conv_b, v_w_rg_a, v_b_rg_a, v_w_rg_x, v_b_rg_x, v_lru_lambda, v_w_lru_out, v_rel_bias, v_w_dil_out, v_g_mem, v_w_mem_kv, v_w_mem_out, v_w_out, v_g_mlp, v_w_mlp_in, v_w_mlp_out, v_g_final):
    w = dict(g_mix=g_mix, w_in=w_in, b_gate=b_gate, conv_w=conv_w, conv_b=conv_b, w_rg_a=w_rg_a, b_rg_a=b_rg_a,
             w_rg_x=w_rg_x, b_rg_x=b_rg_x, lru_lambda=lru_lambda, w_lru_out=w_lru_out, rel_bias=rel_bias,
             w_dil_out=w_dil_out, g_mem=g_mem, w_mem_kv=w_mem_kv, w_mem_out=w_mem_out, w_out=w_out, g_mlp=g_mlp,
             w_mlp_in=w_mlp_in, w_mlp_out=w_mlp_out, g_final=g_final)
    m = dict(g_mix=m_g_mix, w_in=m_w_in, b_gate=m_b_gate, conv_w=m_conv_w, conv_b=m_conv_b, w_rg_a=m_w_rg_a,
             b_rg_a=m_b_rg_a, w_rg_x=m_w_rg_x, b_rg_x=m_b_rg_x, lru_lambda=m_lru_lambda, w_lru_out=m_w_lru_out,
             rel_bias=m_rel_bias, w_dil_out=m_w_dil_out, g_mem=m_g_mem, w_mem_kv=m_w_mem_kv, w_mem_out=m_w_mem_out,
             w_out=m_w_out, g_mlp=m_g_mlp, w_mlp_in=m_w_mlp_in, w_mlp_out=m_w_mlp_out, g_final=m_g_final)
    v = dict(g_mix=v_g_mix, w_in=v_w_in, b_gate=v_b_gate, conv_w=v_conv_w, conv_b=v_conv_b, w_rg_a=v_w_rg_a,
             b_rg_a=v_b_rg_a, w_rg_x=v_w_rg_x, b_rg_x=v_b_rg_x, lru_lambda=v_lru_lambda, w_lru_out=v_w_lru_out,
             rel_bias=v_rel_bias, w_dil_out=v_w_dil_out, g_mem=v_g_mem, w_mem_kv=v_w_mem_kv, w_mem_out=v_w_mem_out,
             w_out=v_w_out, g_mlp=v_g_mlp, w_mlp_in=v_w_mlp_in, w_mlp_out=v_w_mlp_out, g_final=v_g_final)

    my_idx = _dev_index(_my_pos())

    w_in_shard = _mx(w["w_in"].T)
    first = _push_start([w_in_shard], [(N_DEV,) + w_in_shard.shape], _gather_refs, "gather_in_start",
                        relations=ONE_PER_CHIP)
    cw_cols = D_RNN // N_DEV
    conv_pad = jnp.zeros((64, D_MODEL), F32).at[:CONV_WIDTH, :cw_cols].set(w["conv_w"])
    late = {}
    P = {n: w[n] for n in SMALL}

    def start_late(order_after):
        for group, names in (("branch", ("w_mem_kv", "w_lru_out", "w_dil_out", "w_mem_out", "w_out", "conv_w")),
                             ("mlp", ("w_mlp_in", "w_mlp_out"))):
            shards = [conv_pad if n == "conv_w" else _mx(w[n].T if n in GATHERED_TRANSPOSED else w[n]) for n in names]
            started = _push_start(shards, [(N_DEV,) + s.shape for s in shards], _gather_refs, f"gather_{group}_start",
                                  after=order_after)
            late[group] = (names, shards, started)
            order_after = [started["token"]]

    def late_weights(group, after):
        if group == "first":
            (land,) = _push_wait(first, after)
            forward = _forward_start(land, "forward_in_start")
            start_late([forward["token"]])
            full = lax.dynamic_update_index_in_dim(_forward_wait(forward, forward["token"]), w_in_shard, my_idx, 0)
            return {"w_in_t": full.reshape(D_IN, D_MODEL), "started": late["mlp"][2]["token"]}
        names, shards, started = late[group]
        out = {}
        for n, land, own in zip(names, _push_wait(started, after), shards):
            full = lax.dynamic_update_index_in_dim(land, own, my_idx, 0)
            if n == "conv_w":
                out[n] = full[:, :CONV_WIDTH, :cw_cols].transpose(1, 0, 2).reshape(CONV_WIDTH, D_RNN)
            elif n in GATHERED_TRANSPOSED:
                out[n + "_t"] = full.reshape(-1, full.shape[2])
            else:
                out[n] = _gathered_to_full(n, full)
        return out

    sent, small = [], {}

    def send_grads(gs):
        names = list(gs)
        parts = [gs[n] for n in names]
        own = [lax.dynamic_index_in_dim(p, my_idx, 0, keepdims=False) for p in parts]
        started = _push_start(parts, [(N_DEV - 1,) + p.shape[1:] for p in parts], _scatter_refs,
                              f"scatter{len(sent)}_start")
        sent.append((names, own, started))
        return started["token"]

    def reduce_small(gs):
        small["pack"] = _pack([gs[n] for n, _ in SMALL_GRADS])
        small["started"] = _push_start([small["pack"]], [(N_DEV,) + small["pack"].shape], _gather_refs, "small_start")
        return small["started"]["token"]

    grad_x, last_token = _local_step(x[0], mem[0], loss_target[0], {}, P, late_weights, send_grads, reduce_small,
                                     first["token"][0, 0])

    grads, deltas, new_m, new_v = {}, {}, {}, {}
    after = last_token
    for names, own, started in sent[:-W_IN_PIECES]:
        for n, o, land in zip(names, own, _push_wait(started, after)):
            grads[n], deltas[n], new_m[n], new_v[n] = _adamw_landed(w[n], o, land, m[n], v[n], name=f"adamw_{n}")
            after = deltas[n]
    prev = None
    for q, (names, own, started) in enumerate(sent[-W_IN_PIECES:]):
        (land,) = _push_wait(started, after)
        prev = _adamw_landed(w["w_in"].T, own[0], land, m["w_in"].T, v["w_in"].T, name=f"adamw_{names[0]}",
                             col_blk=q, prev=prev)
        after = prev[1]
    grads["w_in"], deltas["w_in"], new_m["w_in"], new_v["w_in"] = [t.T for t in prev]
    (small_land,) = _push_wait(small["started"], [after] + [deltas[n] for n in BIG if n != "w_in"])
    total = _sum_slots(lax.dynamic_update_index_in_dim(small_land, small["pack"], my_idx, 0))
    summed = dict(zip([n for n, _ in SMALL_GRADS], _unpack(total, [shp for _, shp in SMALL_GRADS])))
    summed["b_gate"] = jnp.concatenate([summed.pop(f"b_gate{b}") for b in range(3)], axis=1)
    summed["rel_bias"] = summed["rel_bias"][:, :3 * DIL_HEADS]
    for n in SMALL:
        grads[n] = summed[n].reshape(w[n].shape)
    small_updates = _adamw_many([w[n] for n in SMALL], [grads[n] for n in SMALL], [m[n] for n in SMALL],
                                [v[n] for n in SMALL])
    for n, (d_, nm_, nv_) in zip(SMALL, small_updates):
        deltas[n], new_m[n], new_v[n] = d_, nm_, nv_
    conv_w_sum, loss_sum = summed["conv_w"], summed["loss"]
    grads["conv_w"] = lax.dynamic_slice(conv_w_sum, (0, my_idx * cw_cols), (CONV_WIDTH, cw_cols))
    deltas["conv_w"], new_m["conv_w"], new_v["conv_w"] = _adamw_plain(
        w["conv_w"], grads["conv_w"], m["conv_w"], v["conv_w"], name="adamw_conv_w")

    return (loss_sum.reshape(()), grad_x[None], *[grads[n] for n in WEIGHTS], *[deltas[n] for n in WEIGHTS],
            *[new_m[n] for n in WEIGHTS], *[new_v[n] for n in WEIGHTS])
```

```python
import functools
import math

import jax
import jax.numpy as jnp
from jax import lax
from jax.experimental import pallas as pl
from jax.experimental.pallas import tpu as pltpu

F32 = jnp.float32
MXU_DTYPE = jnp.bfloat16
VMEM_LIMIT_BYTES = 56 * 1024 * 1024
N_DEV = 8

D_MODEL = 1024
N_MEM = 256
MEM_HEADS = 4
MEM_HEAD_DIM = 128
MEM_WIDTH = 512
D_RNN = 768
LRU_BLOCK = 64
N_LRU_BLOCKS = 12
LRU_GROUP = 256
N_LRU_GROUPS = 3
CONV_WIDTH = 4
LRU_C = 8.0
DIL_GROUPS = ((128, 1), (512, 4), (2048, 16))
SPAN = 128
DIL_HEADS = 4
DIL_HEAD_DIM = 64
NUM_BUCKETS = 32
MAX_DISTANCE = 2048
D_FF = 4096
D_IN = 7424
EPS = 1e-6
NEG = -1e30
C_XL, C_GATE, C_QKV, C_QM, C_GATES = 0, 768, 1536, 3840, 4352

ADAM_LR = 0.001
ADAM_B1 = 0.9
ADAM_B2 = 0.999
ADAM_EPS = 1e-08
ADAM_WD = 0.01
ADAM_STEP = 10

MESH = pl.DeviceIdType.MESH
GELU_K = math.sqrt(2.0 / math.pi)


def _cparams(sem=None):
    kw = dict(vmem_limit_bytes=VMEM_LIMIT_BYTES)
    if sem is not None:
        kw["dimension_semantics"] = sem
    return pltpu.CompilerParams(**kw)


def _mx(v):
    return v.astype(MXU_DTYPE)


def _dot(a, b, mode="nn"):
    dims = {"nn": (((1,), (0,)), ((), ())), "nt": (((1,), (1,)), ((), ())), "tn": (((0,), (0,)), ((), ()))}[mode]
    return lax.dot_general(_mx(a), _mx(b), dims, preferred_element_type=F32)


def _colsum(v):
    return jnp.sum(v, axis=0, keepdims=True)


def _matmul(a, b, *, M, N, K, mode, bm, bn, bk, name, out_dtypes=(F32,), epilogue=None, extras=(),
            a_off=(0, 0), b_off=(0, 0), j_outer=False, deps=(), parts=None, sub=1):
    assert M % bm == 0 and N % bn == 0 and K % bk == 0, (name, M, N, K, bm, bn, bk)
    nm, nn, nk = M // bm, N // bn, K // bk

    def ij(f):
        if j_outer:
            return lambda j, i, k: f(i, j, k)
        return f

    if mode == "tn":
        a_spec = pl.BlockSpec((bk, bm), ij(lambda i, j, k: (k + a_off[0], i + a_off[1])))
    else:
        a_spec = pl.BlockSpec((bm, bk), ij(lambda i, j, k: (i + a_off[0], k + a_off[1])))
    if mode == "nt":
        b_spec = pl.BlockSpec((bn, bk), ij(lambda i, j, k: (j + b_off[0], k + b_off[1])))
    else:
        b_spec = pl.BlockSpec((bk, bn), ij(lambda i, j, k: (k + b_off[0], j + b_off[1])))
    ex_specs = [pl.BlockSpec((bm, bn), ij(functools.partial(lambda i, j, k, o: (i + o[0], j + o[1]), o=off)))
                for _, off in extras]
    if parts is None:
        out_dims = (M, N)
        out_spec = pl.BlockSpec((bm, bn), ij(lambda i, j, k: (i, j)))
    elif parts[0] == "rows":
        r = parts[1]
        assert bm % r == 0
        out_dims = (M // r, r, N)
        out_spec = pl.BlockSpec((bm // r, r, bn), ij(lambda i, j, k: (i, 0, j)))
    elif parts[0] == "rows_t":
        r = parts[1]
        assert bn % r == 0
        out_dims = (N // r, r, M)
        out_spec = pl.BlockSpec((bn // r, r, bm), ij(lambda i, j, k: (j, 0, i)))
    else:
        c = parts[1]
        assert bn % c == 0
        out_dims = (N // c, M, c)
        out_spec = pl.BlockSpec((bn // c, bm, c), ij(lambda i, j, k: (j, i, 0)))
    n_ex, n_out, n_dep = len(extras), len(out_dtypes), len(deps)

    def body(*refs):
        a_ref, b_ref = refs[0], refs[1]
        ex = refs[2:2 + n_ex]
        outs = refs[2 + n_ex + n_dep:2 + n_ex + n_dep + n_out]
        if sub > 1:
            assert nk == 1 and parts is None and mode != "tn" and epilogue is not None
            for s_ in range(sub):
                rows = pl.ds(s_ * (bm // sub), bm // sub)
                vals = epilogue(_dot(a_ref[rows, :], b_ref[...], mode), *[e[rows, :] for e in ex])
                for o, v in zip(outs, vals):
                    o[rows, :] = v.astype(o.dtype)
            return
        part = _dot(a_ref[...], b_ref[...], mode)

        def finish(acc):
            vals = epilogue(acc, *[e[...] for e in ex]) if epilogue is not None else (acc,)
            for o, v in zip(outs, vals):
                if parts is not None and parts[0] == "rows_t":
                    v = v.T
                v = v.astype(o.dtype)
                if parts is None:
                    o[...] = v
                elif parts[0] in ("rows", "rows_t"):
                    for ch in range(v.shape[0] // parts[1]):
                        o[ch] = v[ch * parts[1]:(ch + 1) * parts[1], :]
                else:
                    for ch in range(bn // parts[1]):
                        o[ch] = v[:, ch * parts[1]:(ch + 1) * parts[1]]

        if nk == 1:
            finish(part)
        else:
            acc_ref = refs[-1]
            k = pl.program_id(2)

            @pl.when(k == 0)
            def _():
                acc_ref[...] = part

            @pl.when(k > 0)
            def _():
                acc_ref[...] += part

            @pl.when(k == nk - 1)
            def _():
                finish(acc_ref[...])

    grid = (nn, nm, nk) if j_outer else (nm, nn, nk)
    res = pl.pallas_call(
        body, name=name, grid=grid,
        in_specs=[a_spec, b_spec] + ex_specs + [pl.BlockSpec(memory_space=pl.ANY)] * n_dep,
        out_specs=[out_spec] * n_out,
        out_shape=[jax.ShapeDtypeStruct(out_dims, dt) for dt in out_dtypes],
        scratch_shapes=[pltpu.VMEM((bm, bn), F32)] if nk > 1 else [],
        compiler_params=_cparams(("parallel", "parallel", "arbitrary")),
    )(a, b, *[e for e, _ in extras], *deps)
    return res[0] if n_out == 1 else res


ROW_SUBTILES = 2


def _matmul_rows(a, b, *, M, K, mode, bm, name, row_fn, out_dtypes, tiles=(), vecs=(), acc_widths=(), deps=()):
    N = D_MODEL
    assert M % bm == 0
    segs = list(a) if isinstance(a, (list, tuple)) else [a]
    widths = [s_.shape[1] for s_ in segs]
    assert sum(widths) == K and (len(segs) == 1 or mode == "nn")
    n_s, n_t, n_v, n_o, n_a, n_d = len(segs), len(tiles), len(vecs), len(out_dtypes), len(acc_widths), len(deps)
    row = pl.BlockSpec((bm, N), lambda i: (i, 0))
    b_shape = (K, N) if mode == "nn" else (N, K)

    def body(*refs):
        b_ref = refs[n_s]
        ins = refs[n_s + 1:n_s + 1 + n_t + n_v]
        outs = refs[n_s + 1 + n_t + n_v + n_d:n_s + 1 + n_t + n_v + n_d + n_o]
        accs = refs[n_s + 1 + n_t + n_v + n_d + n_o:]
        for o in accs:
            @pl.when(pl.program_id(0) == 0)
            def _(o=o):
                o[...] = jnp.zeros_like(o)

        for s_ in range(ROW_SUBTILES):
            rows = pl.ds(s_ * (bm // ROW_SUBTILES), bm // ROW_SUBTILES)
            if n_s == 1:
                acc = _dot(refs[0][rows, :], b_ref[...], mode)
            else:
                acc, k0 = None, 0
                for a_ref, w_ in zip(refs[:n_s], widths):
                    part = _dot(a_ref[rows, :], b_ref[k0:k0 + w_, :])
                    acc = part if acc is None else acc + part
                    k0 += w_
            tile_vals, partials = row_fn(acc, *[r[rows, :] for r in ins[:n_t]], *[r[...] for r in ins[n_t:]])
            for o, val in zip(outs, tile_vals):
                o[rows, :] = val.astype(o.dtype)
            for o, val in zip(accs, partials):
                o[...] += val

    res = pl.pallas_call(
        body, name=name, grid=(M // bm,),
        in_specs=[pl.BlockSpec((bm, w_), lambda i: (i, 0)) for w_ in widths] + [pl.BlockSpec(b_shape, lambda i: (0, 0))]
        + [row] * n_t + [pl.BlockSpec((1, N), lambda i: (0, 0))] * n_v + [pl.BlockSpec(memory_space=pl.ANY)] * n_d,
        out_specs=[row] * n_o + [pl.BlockSpec((1, w_), lambda i: (0, 0)) for w_ in acc_widths],
        out_shape=[jax.ShapeDtypeStruct((M, N), dt) for dt in out_dtypes]
        + [jax.ShapeDtypeStruct((1, w_), F32) for w_ in acc_widths],
        compiler_params=_cparams(("arbitrary",) if n_a else ("parallel",)),
    )(*segs, b, *tiles, *vecs, *deps)
    return res


def _dw_in_t_half(h, pieces, q, half, *, S, name, into=None, deps=(), bk=1024):
    half_w, cols = D_IN // 2, D_MODEL // W_IN_PIECES
    lo, hi = half * half_w, (half + 1) * half_w
    use, c0 = [], 0
    for p in pieces:
        w_ = p.shape[1]
        a0, a1 = max(lo, c0), min(hi, c0 + w_)
        if a1 > a0:
            use.append((p, a0 - c0, a1 - a0))
        c0 += w_
    n_p, n_into, n_d, nk = len(use), 0 if into is None else 1, len(deps), S // bk
    rows = D_IN // N_DEV

    def body(*refs):
        h_ref, p_refs = refs[0], refs[1:1 + n_p]
        o_ref, acc_ref = refs[1 + n_p + n_into + n_d], refs[-1]
        k = pl.program_id(0)
        dp = jnp.concatenate([r[:, s0:s0 + w_] for r, (_, s0, w_) in zip(p_refs, use)], axis=1)
        part = _dot(h_ref[...], dp, "tn")

        @pl.when(k == 0)
        def _():
            acc_ref[...] = part

        @pl.when(k > 0)
        def _():
            acc_ref[...] += part

        @pl.when(k == nk - 1)
        def _():
            vt = acc_ref[...].T.astype(o_ref.dtype)
            for ch in range(half_w // rows):
                o_ref[ch] = vt[ch * rows:(ch + 1) * rows, :]

    return pl.pallas_call(
        body, name=name, grid=(nk,),
        in_specs=[pl.BlockSpec((bk, cols), lambda k: (k, q))]
        + [pl.BlockSpec((bk, p.shape[1]), lambda k: (k, 0)) for p, _, _ in use]
        + [pl.BlockSpec(memory_space=pl.ANY)] * (n_into + n_d),
        out_specs=pl.BlockSpec((half_w // rows, rows, cols), lambda k: (half, 0, 0)),
        out_shape=jax.ShapeDtypeStruct((N_DEV, rows, cols), MXU_DTYPE),
        input_output_aliases={1 + n_p: 0} if n_into else {},
        scratch_shapes=[pltpu.VMEM((cols, half_w), F32)],
        compiler_params=_cparams(("arbitrary",)),
    )(h, *[p for p, _, _ in use], *([into] if n_into else []), *deps)


def _rmsnorm_fwd(x, g, *, rows, name, bt=512):
    bt = min(bt, rows)

    def body(x_ref, g_ref, o_ref):
        xv = x_ref[...]
        r = lax.rsqrt(jnp.mean(xv * xv, axis=-1, keepdims=True) + EPS)
        o_ref[...] = (xv * r * g_ref[...]).astype(o_ref.dtype)

    return pl.pallas_call(
        body, name=name, grid=(rows // bt,),
        in_specs=[pl.BlockSpec((bt, D_MODEL), lambda i: (i, 0)), pl.BlockSpec((1, D_MODEL), lambda i: (0, 0))],
        out_specs=pl.BlockSpec((bt, D_MODEL), lambda i: (i, 0)),
        out_shape=jax.ShapeDtypeStruct((rows, D_MODEL), MXU_DTYPE),
        compiler_params=_cparams(("parallel",)),
    )(x, g.reshape(1, D_MODEL))


def _rms_bwd_tile(xv, gv, dyv):
    r = lax.rsqrt(jnp.mean(xv * xv, axis=-1, keepdims=True) + EPS)
    w = dyv * gv
    dx = r * w - xv * (r * r * r) * jnp.mean(w * xv, axis=-1, keepdims=True)
    dg = _colsum(dyv * xv * r)
    return dx, dg


def _residual_then_norm(acc, x_t, g):
    x1 = x_t + acc
    r = lax.rsqrt(jnp.mean(x1 * x1, axis=-1, keepdims=True) + EPS)
    return (x1, x1 * r * g), ()


def _residual_then_loss(acc, x_t, tgt_t, g):
    x2 = x_t + acc
    r = lax.rsqrt(jnp.mean(x2 * x2, axis=-1, keepdims=True) + EPS)
    diff = x2 * r * g - tgt_t
    part = jnp.sum(jnp.mean(diff * diff, axis=-1, keepdims=True), axis=0, keepdims=True) * 0.5
    dx, dg = _rms_bwd_tile(x2, g, diff * (1.0 / D_MODEL))
    return (dx, dx), (part, dg)


def _norm_bwd_then_residual(n_out):
    def fn(acc, x_t, res_t, g):
        dx, dg = _rms_bwd_tile(x_t, g, acc)
        return (dx + res_t,) * n_out, (dg,)

    return fn


def _rmsnorm_bwd(x, g, dy, res, *, rows, name, bt=512, dx_dtypes=(F32,)):
    bt = min(bt, rows)
    has_res = res is not None

    def body(*refs):
        x_ref, g_ref, dy_ref = refs[:3]
        res_ref = refs[3] if has_res else None
        outs = refs[3 + int(has_res):]
        dx, dg = _rms_bwd_tile(x_ref[...], g_ref[...], dy_ref[...])
        if has_res:
            dx = dx + res_ref[...]
        dg_ref = outs[-1]

        @pl.when(pl.program_id(0) == 0)
        def _():
            dg_ref[...] = jnp.zeros_like(dg_ref)

        dg_ref[...] += dg
        for o in outs[:-1]:
            o[...] = dx.astype(o.dtype)

    row_spec = pl.BlockSpec((bt, D_MODEL), lambda i: (i, 0))
    vec_spec = pl.BlockSpec((1, D_MODEL), lambda i: (0, 0))
    ins = [x, g.reshape(1, D_MODEL), dy] + ([res] if has_res else [])
    return pl.pallas_call(
        body, name=name, grid=(rows // bt,),
        in_specs=[row_spec, vec_spec, row_spec] + ([row_spec] if has_res else []),
        out_specs=[row_spec] * len(dx_dtypes) + [vec_spec],
        out_shape=[jax.ShapeDtypeStruct((rows, D_MODEL), dt) for dt in dx_dtypes] + [jax.ShapeDtypeStruct((1, D_MODEL), F32)],
        compiler_params=_cparams(("arbitrary",)),
    )(*ins)


LRU_T = 512
SCAN_GROUPS = 4


def _gelu(x):
    t = jnp.tanh(GELU_K * (x + 0.044715 * x * x * x))
    return 0.5 * x * (1.0 + t), t


def _gelu_grad(x, t):
    return 0.5 * (1.0 + t) + 0.5 * x * (1.0 - t * t) * GELU_K * (1.0 + 3.0 * 0.044715 * x * x)


def _softplus_neg(lam):
    z = -lam
    u = jnp.exp(-jnp.abs(z))
    w = 1.0 + u
    l1p = jnp.where(w == 1.0, u, jnp.log(w) * u / jnp.where(w == 1.0, 1.0, w - 1.0))
    return jnp.maximum(z, 0.0) + l1p


def _shift_down(cur, prev8, k, row8):
    y = pltpu.roll(cur, k, 0)
    head = jnp.where(row8 < k, pltpu.roll(prev8, k, 0), y[0:8])
    return jnp.concatenate([head, y[8:]], axis=0)


def _shift_up(cur, next8, k, row8):
    n = cur.shape[0]
    y = pltpu.roll(cur, n - k, 0)
    tail = jnp.where(row8 >= 8 - k, pltpu.roll(next8, 8 - k, 0), y[n - 8:n])
    return jnp.concatenate([y[0:n - 8], tail], axis=0)


def _lru_gates(xl, p8, cw, cb, wa, wx, ba, bx, lam, row8, a_mult=None):
    sh = [xl] + [_shift_down(xl, p8, k, row8) for k in (1, 2, 3)]
    xc = cb + cw[3:4] * sh[0] + cw[2:3] * sh[1] + cw[1:2] * sh[2] + cw[0:1] * sh[3]
    r = jax.nn.sigmoid(_dot(xc, wa) + ba)
    i = jax.nn.sigmoid(_dot(xc, wx) + bx)
    sp = _softplus_neg(lam)
    if a_mult is None:
        la = -LRU_C * r * sp
        a = jnp.exp(la)
        mult = jnp.sqrt(jnp.tanh(-la) * (a * a + 1.0))
    else:
        a, mult = a_mult
    return dict(sh=sh, xc=xc, r=r, i=i, sp=sp, a=a, mult=mult)


def _lru_specs(n_t, reverse):
    T = LRU_T
    tt = (lambda t: n_t - 1 - t) if reverse else (lambda t: t)
    blk = lambda col0: pl.BlockSpec((T, LRU_GROUP), lambda g, t: (tt(t), col0 + g))
    prev8 = lambda col0: pl.BlockSpec((8, LRU_GROUP), lambda g, t: (jnp.maximum(tt(t) * (T // 8) - 1, 0), col0 + g))
    vec = lambda rows: pl.BlockSpec((rows, LRU_GROUP), lambda g, t: (0, g))
    wbd = pl.BlockSpec((1, LRU_GROUP, LRU_GROUP), lambda g, t: (g, 0, 0))
    return blk, prev8, vec, wbd


def _lru_fwd(proj, conv_w, conv_b, wa_bd, wx_bd, b_a, b_x, lam, *, S):
    T = LRU_T
    n_t = S // T
    blk, _, vec, wbd = _lru_specs(n_t, False)

    def body(xl_ref, gate_ref, cw_ref, cb_ref, wa_ref, wx_ref, ba_ref, bx_ref, lam_ref,
             hl_ref, z_ref, a_s, m_ref, prev8, hcar, b_s):
        @pl.when(pl.program_id(1) == 0)
        def _():
            prev8[...] = jnp.zeros_like(prev8)
            hcar[...] = jnp.zeros_like(hcar)

        row8 = lax.broadcasted_iota(jnp.int32, (8, LRU_GROUP), 0)
        xl = xl_ref[...]
        q = _lru_gates(xl, prev8[...], cw_ref[...], cb_ref[...], wa_ref[0], wx_ref[0], ba_ref[...], bx_ref[...],
                       lam_ref[...], row8)
        prev8[...] = xl[T - 8:T]
        a_s[...] = q["a"]
        m_ref[...] = q["mult"]
        b_s[...] = q["mult"] * q["i"] * q["xc"]

        def step(c, carry):
            local = []
            for u in range(SCAN_GROUPS):
                off = pl.multiple_of((c * SCAN_GROUPS + u) * 8, 8)
                A = a_s[pl.ds(off, 8), :]
                B = b_s[pl.ds(off, 8), :]
                for k in (1, 2, 4):
                    a_sh = jnp.where(row8 >= k, pltpu.roll(A, k, 0), 1.0)
                    b_sh = jnp.where(row8 >= k, pltpu.roll(B, k, 0), 0.0)
                    B = A * b_sh + B
                    A = A * a_sh
                local.append((off, A, B))
            for off, A, B in local:
                h = A * carry + B
                hl_ref[pl.ds(off, 8), :] = h
                carry = h[7:8, :]
            return carry

        hcar[...] = lax.fori_loop(0, T // (8 * SCAN_GROUPS), step, hcar[...])
        ge, _ = _gelu(gate_ref[...])
        z_ref[...] = (ge * hl_ref[...]).astype(z_ref.dtype)

    return pl.pallas_call(
        body, name="lru_fwd", grid=(N_LRU_GROUPS, n_t),
        in_specs=[blk(C_XL // LRU_GROUP), blk(C_GATE // LRU_GROUP), vec(4), vec(1), wbd, wbd, vec(1), vec(1), vec(1)],
        out_specs=[blk(0)] * 4,
        out_shape=[jax.ShapeDtypeStruct((S, D_RNN), F32), jax.ShapeDtypeStruct((S, D_RNN), MXU_DTYPE),
                   jax.ShapeDtypeStruct((S, D_RNN), F32), jax.ShapeDtypeStruct((S, D_RNN), F32)],
        scratch_shapes=[pltpu.VMEM((8, LRU_GROUP), F32), pltpu.VMEM((1, LRU_GROUP), F32), pltpu.VMEM((T, LRU_GROUP), F32)],
        compiler_params=_cparams(("parallel", "arbitrary")),
    )(proj, proj, conv_w, conv_b, wa_bd, wx_bd, b_a, b_x, lam)


def _lru_bwd(proj, hl, a_fwd, mult_fwd, dz, conv_w, conv_b, wa_bd, wx_bd, b_a, b_x, lam, *, S):
    T = LRU_T
    n_t = S // T
    blk, prev8s, vec, wbd = _lru_specs(n_t, True)

    def body(xl_ref, xlp_ref, gate_ref, hl_ref, hlp_ref, a_ref, m_ref, dz_ref, cw_ref, cb_ref, wa_ref, wx_ref, ba_ref,
             bx_ref, lam_ref, dxl_ref, dgate_ref, dcw_ref, dcb_ref, dwa_ref, dwx_ref, dba_ref, dbx_ref, dlam_ref,
             next8, gcar, c_s, b_s, l_s):
        t = pl.program_id(1)
        first_chunk = t == n_t - 1

        @pl.when(t == 0)
        def _():
            next8[...] = jnp.zeros_like(next8)
            gcar[...] = jnp.zeros_like(gcar)
            for ref in (dcw_ref, dcb_ref, dwa_ref, dwx_ref, dba_ref, dbx_ref, dlam_ref):
                ref[...] = jnp.zeros_like(ref)

        row8 = lax.broadcasted_iota(jnp.int32, (8, LRU_GROUP), 0)
        rowT = lax.broadcasted_iota(jnp.int32, (T, LRU_GROUP), 0)
        keep = jnp.where(first_chunk, 0.0, 1.0)
        xl = xl_ref[...]
        wa, wx, lam_v = wa_ref[0], wx_ref[0], lam_ref[...]
        q = _lru_gates(xl, xlp_ref[...] * keep, cw_ref[...], cb_ref[...], wa, wx, ba_ref[...], bx_ref[...], lam_v, row8,
                       a_mult=(a_ref[...], m_ref[...]))
        a, mult, r, i, xc, sp = q["a"], q["mult"], q["r"], q["i"], q["xc"], q["sp"]
        hl_v = hl_ref[...]
        dz_v = dz_ref[...]
        gate = gate_ref[...]
        ge, th = _gelu(gate)
        dgate_ref[...] = (dz_v * hl_v * _gelu_grad(gate, th)).astype(dgate_ref.dtype)

        c_s[...] = jnp.where(rowT == T - 1, 0.0, pltpu.roll(a, T - 1, 0))
        b_s[...] = dz_v * ge + jnp.where(rowT == T - 1, gcar[...], 0.0)

        def step(n, carry):
            local = []
            for u in range(SCAN_GROUPS):
                off = pl.multiple_of((T // 8 - 1 - (n * SCAN_GROUPS + u)) * 8, 8)
                C = c_s[pl.ds(off, 8), :]
                B = b_s[pl.ds(off, 8), :]
                for k in (1, 2, 4):
                    c_sh = jnp.where(row8 < 8 - k, pltpu.roll(C, 8 - k, 0), 1.0)
                    b_sh = jnp.where(row8 < 8 - k, pltpu.roll(B, 8 - k, 0), 0.0)
                    B = B + C * b_sh
                    C = C * c_sh
                local.append((off, C, B))
            for off, C, B in local:
                lam_t = B + C * carry
                l_s[pl.ds(off, 8), :] = lam_t
                carry = lam_t[0:1, :]
            return carry

        lax.fori_loop(0, T // (8 * SCAN_GROUPS), step, jnp.zeros((1, LRU_GROUP), F32))
        lmb = l_s[...]
        gcar[...] = a[0:1, :] * lmb[0:1, :]

        h_prev = _shift_down(hl_v, hlp_ref[...] * keep, 1, row8)
        da = lmb * h_prev
        dmult = lmb * i * xc
        di = lmb * mult * xc
        dxc = lmb * mult * i
        dla = da * a - dmult * (a * a) / mult
        dr = dla * (-LRU_C * sp)
        dlam_ref[...] += _colsum(dla * (-LRU_C * r)) * (-jax.nn.sigmoid(-lam_v))
        dpa = dr * r * (1.0 - r)
        dpx = di * i * (1.0 - i)
        dxc = dxc + _dot(dpa, wa, "nt") + _dot(dpx, wx, "nt")
        dwa_ref[0] += _dot(xc, dpa, "tn")
        dwx_ref[0] += _dot(xc, dpx, "tn")
        dba_ref[...] += _colsum(dpa)
        dbx_ref[...] += _colsum(dpx)
        dcb_ref[...] += _colsum(dxc)
        cw = cw_ref[...]
        n8 = next8[...]
        dxl = cw[3:4] * dxc
        for k in (1, 2, 3):
            dxl = dxl + cw[3 - k:4 - k] * _shift_up(dxc, n8, k, row8)
        for k in range(4):
            dcw_ref[3 - k:4 - k, :] += _colsum(dxc * q["sh"][k])
        next8[...] = dxc[0:8]
        dxl_ref[...] = dxl.astype(dxl_ref.dtype)

    res = pl.pallas_call(
        body, name="lru_bwd", grid=(N_LRU_GROUPS, n_t),
        in_specs=[blk(C_XL // LRU_GROUP), prev8s(C_XL // LRU_GROUP), blk(C_GATE // LRU_GROUP), blk(0), prev8s(0), blk(0),
                  blk(0), blk(0), vec(4), vec(1), wbd, wbd, vec(1), vec(1), vec(1)],
        out_specs=[blk(0), blk(0), vec(4), vec(1), wbd, wbd, vec(1), vec(1), vec(1)],
        out_shape=[jax.ShapeDtypeStruct((S, D_RNN), MXU_DTYPE), jax.ShapeDtypeStruct((S, D_RNN), MXU_DTYPE),
                   jax.ShapeDtypeStruct((4, D_RNN), F32), jax.ShapeDtypeStruct((1, D_RNN), F32),
                   jax.ShapeDtypeStruct((N_LRU_GROUPS, LRU_GROUP, LRU_GROUP), F32),
                   jax.ShapeDtypeStruct((N_LRU_GROUPS, LRU_GROUP, LRU_GROUP), F32),
                   jax.ShapeDtypeStruct((1, D_RNN), F32), jax.ShapeDtypeStruct((1, D_RNN), F32),
                   jax.ShapeDtypeStruct((1, D_RNN), F32)],
        scratch_shapes=[pltpu.VMEM((8, LRU_GROUP), F32), pltpu.VMEM((1, LRU_GROUP), F32),
                        pltpu.VMEM((T, LRU_GROUP), F32), pltpu.VMEM((T, LRU_GROUP), F32), pltpu.VMEM((T, LRU_GROUP), F32)],
        compiler_params=_cparams(("parallel", "arbitrary")),
    )(proj, proj, proj, hl, hl, a_fwd, mult_fwd, dz, conv_w, conv_b, wa_bd, wx_bd, b_a, b_x, lam)
    return res


def _block_diag(w):
    w4 = w.reshape(N_LRU_GROUPS, 4, LRU_BLOCK, 1, LRU_BLOCK)
    eye = jnp.eye(4, dtype=w.dtype).reshape(1, 4, 1, 4, 1)
    return (w4 * eye).reshape(N_LRU_GROUPS, LRU_GROUP, LRU_GROUP)


def _block_diag_extract(wbd):
    w5 = wbd.reshape(N_LRU_GROUPS, 4, LRU_BLOCK, 4, LRU_BLOCK)
    return jnp.stack([w5[:, a, :, a, :] for a in range(4)], axis=1).reshape(N_LRU_BLOCKS, LRU_BLOCK, LRU_BLOCK)


def _t5_bucket(dist):
    max_exact = NUM_BUCKETS // 2
    df = jnp.maximum(dist, 1).astype(jnp.float32)
    large = max_exact + (jnp.log(df / max_exact) / math.log(MAX_DISTANCE / max_exact)
                         * (NUM_BUCKETS - max_exact)).astype(jnp.int32)
    large = jnp.minimum(large, NUM_BUCKETS - 1)
    return jnp.where(dist < max_exact, dist, large)


def _band_offsets():
    qi = jnp.arange(SPAN)[:, None]
    kj = jnp.arange(2 * SPAN)[None, :]
    return qi + SPAN - kj


def _dil_buckets():
    off = _band_offsets()
    return jnp.stack([_t5_bucket(jnp.maximum(off, 0) * dil) for _, dil in DIL_GROUPS]).astype(jnp.int32)


def _dil_bias(rel_bias, buckets):
    def body(tbl_ref, bk_ref, o_ref):
        g = pl.program_id(0)
        qi = lax.broadcasted_iota(jnp.int32, (SPAN, 2 * SPAN), 0)
        kj = lax.broadcasted_iota(jnp.int32, (SPAN, 2 * SPAN), 1)
        off = qi + SPAN - kj
        valid = (off >= 0) & (off <= SPAN)
        bk = bk_ref[0]
        for h in range(DIL_HEADS):
            acc = jnp.zeros((SPAN, 2 * SPAN), F32)
            for b in range(NUM_BUCKETS):
                acc = jnp.where(bk == b, tbl_ref[b, g * DIL_HEADS + h], acc)
            o_ref[0, h] = jnp.where(valid, acc, NEG)

    return pl.pallas_call(
        body, name="dil_bias", grid=(3,),
        in_specs=[pl.BlockSpec(memory_space=pltpu.SMEM), pl.BlockSpec((1, SPAN, 2 * SPAN), lambda g: (g, 0, 0))],
        out_specs=pl.BlockSpec((1, DIL_HEADS, SPAN, 2 * SPAN), lambda g: (g, 0, 0, 0)),
        out_shape=jax.ShapeDtypeStruct((3, DIL_HEADS, SPAN, 2 * SPAN), F32),
        compiler_params=_cparams(("parallel",)),
    )(rel_bias, buckets)


def _dil_bias_bwd(dbias, buckets):
    def body(db_ref, bk_ref, o_ref):
        lane = lax.broadcasted_iota(jnp.int32, (1, 128), 1)
        rows = [jnp.zeros((1, 128), F32) for _ in range(NUM_BUCKETS)]
        for g in range(3):
            bk = bk_ref[g]
            for h in range(DIL_HEADS):
                d = db_ref[g, h]
                for b in range(NUM_BUCKETS):
                    tot = jnp.sum(_colsum(jnp.where(bk == b, d, 0.0)), axis=1, keepdims=True)
                    rows[b] = jnp.where(lane == g * DIL_HEADS + h, tot, rows[b])
        for b in range(NUM_BUCKETS):
            o_ref[b:b + 1, :] = rows[b]

    return pl.pallas_call(
        body, name="dil_bias_bwd",
        out_shape=jax.ShapeDtypeStruct((NUM_BUCKETS, 128), F32),
        compiler_params=_cparams(),
    )(dbias, buckets)


DIL_SUBBLOCKS = (8, 4, 1)


def _dil_layout(g, S):
    dil, m = DIL_GROUPS[g][1], DIL_SUBBLOCKS[g]
    sub = SPAN * dil
    col = [(C_QKV + t * 768 + g * 256) // 128 for t in range(3)]
    return dil, m, sub, S // (sub * m), col


def _residue_rows(b, r, dil):
    return pl.ds(b * SPAN * dil + r, SPAN, stride=dil) if dil > 1 else pl.ds(b * SPAN, SPAN)


def _for_residues(dil, fn):
    if dil <= 4:
        for r in range(dil):
            fn(r)
    else:
        lax.fori_loop(0, dil, lambda r, c: (fn(r), c)[1], 0, unroll=4)


def _pair_scores(qm, k2, bias, first_cols):
    s = _dot(qm, k2, "nt") * (DIL_HEAD_DIM ** -0.5) + bias
    kj = lax.broadcasted_iota(jnp.int32, s.shape, 1)
    return jnp.where(kj < first_cols, NEG, s)


def _dilated_fwd(proj, bias, g, *, S):
    dil, m, sub, nc, (qc, kc, vc) = _dil_layout(g, S)
    R = sub * m
    cur = lambda cb: pl.BlockSpec((R, 128), lambda p, i: (i, cb + p))
    prv = lambda cb: pl.BlockSpec((sub, 128), lambda p, i: (jnp.maximum(i * m - 1, 0), cb + p))
    out = pl.BlockSpec((R, 128), lambda p, i: (i, p))

    def body(q_ref, kp_ref, kc_ref, vp_ref, vc_ref, b_ref, o_ref, lse_ref):
        lane = lax.broadcasted_iota(jnp.int32, (SPAN, 128), 1)
        sels = (lane < DIL_HEAD_DIM, lane >= DIL_HEAD_DIM)
        for b in range(m):
            first_cols = jnp.where(pl.program_id(1) == 0, SPAN, 0) if b == 0 else 0

            def one(r, b=b, first_cols=first_cols):
                rows = _residue_rows(b, r, dil)
                before = (kc_ref, vc_ref, _residue_rows(b - 1, r, dil)) if b else (kp_ref, vp_ref, _residue_rows(0, r, dil))
                q2 = q_ref[rows, :]
                k2 = _mx(jnp.concatenate([before[0][before[2], :], kc_ref[rows, :]], axis=0))
                v2 = _mx(jnp.concatenate([before[1][before[2], :], vc_ref[rows, :]], axis=0))
                qq = jnp.concatenate([jnp.where(sels[0], q2, 0.0), jnp.where(sels[1], q2, 0.0)], axis=0)
                s = _pair_scores(qq, k2, b_ref[0, 0], first_cols)
                mx = jnp.max(s, axis=-1, keepdims=True)
                p = jnp.exp(s - mx)
                den = jnp.sum(p, axis=-1, keepdims=True)
                o = _dot(p, v2) / den
                st = mx + jnp.log(den)
                o_ref[rows, :] = jnp.where(sels[0], o[0:SPAN], o[SPAN:2 * SPAN])
                lse_ref[rows, :] = jnp.where(lane == 0, st[0:SPAN], jnp.where(lane == 1, st[SPAN:2 * SPAN], 0.0))

            _for_residues(dil, one)

    return pl.pallas_call(
        body, name=f"dil_fwd{g}", grid=(2, nc),
        in_specs=[cur(qc), prv(kc), cur(kc), prv(vc), cur(vc),
                  pl.BlockSpec((1, 1, 2 * SPAN, 2 * SPAN), lambda p, i: (g, p, 0, 0))],
        out_specs=[out, out],
        out_shape=[jax.ShapeDtypeStruct((S, 256), F32), jax.ShapeDtypeStruct((S, 256), F32)],
        compiler_params=_cparams(("parallel", "parallel")),
    )(proj, proj, proj, proj, proj, bias.reshape(3, 2, 2 * SPAN, 2 * SPAN))


def _dilated_bwd(proj, do, lse, delta, bias, g, *, S, into=None):
    dil, m, sub, nc, (qc, kc, vc) = _dil_layout(g, S)
    R = sub * m
    cl = lambda i: jnp.minimum(i, nc - 1)
    cur = lambda cb: pl.BlockSpec((R, 128), lambda p, i: (cl(i), cb + p))
    prv = lambda cb: pl.BlockSpec((sub, 128), lambda p, i: (jnp.maximum(cl(i) * m - 1, 0), cb + p))
    q_out = pl.BlockSpec((R, 128), lambda p, i: (cl(i), 2 * g + p))
    kv_out = pl.BlockSpec((R, 128), lambda p, i: (jnp.maximum(i - 1, 0), 2 * g + p))
    scale = DIL_HEAD_DIM ** -0.5
    n_into = 0 if into is None else 3

    def body(q_ref, kp_ref, kc_ref, vp_ref, vc_ref, do_ref, lse_ref, dl_ref, b_ref, *rest):
        dq_ref, dk_ref, dv_ref, db_ref, dq_s, kc_s, vc_s, kp_s, vp_s, kcar, vcar = rest[n_into:]
        i = pl.program_id(1)

        @pl.when(i == 0)
        def _():
            kcar[...] = jnp.zeros_like(kcar)
            vcar[...] = jnp.zeros_like(vcar)
            db_ref[...] = jnp.zeros_like(db_ref)

        @pl.when(i < nc)
        def _():
            lane = lax.broadcasted_iota(jnp.int32, (SPAN, 128), 1)
            sels = (lane < DIL_HEAD_DIM, lane >= DIL_HEAD_DIM)
            for b in range(m):
                first_cols = jnp.where(i == 0, SPAN, 0) if b == 0 else 0

                def one(r, b=b, first_cols=first_cols):
                    rows = _residue_rows(b, r, dil)
                    rows_before = _residue_rows(b - 1 if b else 0, r, dil)
                    k_before, v_before = (kc_ref, vc_ref) if b else (kp_ref, vp_ref)
                    q2, do2 = q_ref[rows, :], do_ref[rows, :]
                    k2 = _mx(jnp.concatenate([k_before[rows_before, :], kc_ref[rows, :]], axis=0))
                    v2 = _mx(jnp.concatenate([v_before[rows_before, :], vc_ref[rows, :]], axis=0))
                    lse_t, dl_t = lse_ref[rows, :], dl_ref[rows, :]
                    qq = _mx(jnp.concatenate([jnp.where(sels[0], q2, 0.0), jnp.where(sels[1], q2, 0.0)], axis=0))
                    dd = _mx(jnp.concatenate([jnp.where(sels[0], do2, 0.0), jnp.where(sels[1], do2, 0.0)], axis=0))
                    lse2 = jnp.concatenate([lse_t[:, 0:1], lse_t[:, 1:2]], axis=0)
                    dl2 = jnp.concatenate([dl_t[:, 0:1], dl_t[:, 1:2]], axis=0)
                    p = jnp.exp(_pair_scores(qq, k2, b_ref[0, 0], first_cols) - lse2)
                    ds = p * (_dot(dd, v2, "nt") - dl2)
                    db_ref[0] += ds
                    dqq = _dot(ds, k2) * scale
                    dq2 = jnp.where(sels[0], dqq[0:SPAN], dqq[SPAN:2 * SPAN])
                    dk2 = _dot(ds, qq, "tn") * scale
                    dv2 = _dot(p, dd, "tn")
                    dq_s[rows, :] = dq2
                    kc_s[rows, :] = dk2[SPAN:2 * SPAN]
                    vc_s[rows, :] = dv2[SPAN:2 * SPAN]
                    if b:
                        kc_s[rows_before, :] += dk2[0:SPAN]
                        vc_s[rows_before, :] += dv2[0:SPAN]
                    else:
                        kp_s[rows_before, :] = dk2[0:SPAN]
                        vp_s[rows_before, :] = dv2[0:SPAN]

                _for_residues(dil, one)
            dq_ref[...] = dq_s[...].astype(dq_ref.dtype)
            last = pl.ds((m - 1) * sub, sub)
            kcar[last, :] += kp_s[...]
            vcar[last, :] += vp_s[...]
            dk_ref[...] = kcar[...].astype(dk_ref.dtype)
            dv_ref[...] = vcar[...].astype(dv_ref.dtype)
            kcar[...] = kc_s[...]
            vcar[...] = vc_s[...]

        @pl.when(i == nc)
        def _():
            dk_ref[...] = kcar[...].astype(dk_ref.dtype)
            dv_ref[...] = vcar[...].astype(dv_ref.dtype)

    stat = pl.BlockSpec((R, 128), lambda p, i: (cl(i), p))
    big = jax.ShapeDtypeStruct((S, len(DIL_GROUPS) * 256), MXU_DTYPE)
    return pl.pallas_call(
        body, name=f"dil_bwd{g}", grid=(2, nc + 1),
        in_specs=[cur(qc), prv(kc), cur(kc), prv(vc), cur(vc), stat, stat, stat,
                  pl.BlockSpec((1, 1, 2 * SPAN, 2 * SPAN), lambda p, i: (g, p, 0, 0))]
        + [pl.BlockSpec(memory_space=pl.ANY)] * n_into,
        out_specs=[q_out, kv_out, kv_out, pl.BlockSpec((1, 2 * SPAN, 2 * SPAN), lambda p, i: (p, 0, 0))],
        out_shape=[big, big, big, jax.ShapeDtypeStruct((2, 2 * SPAN, 2 * SPAN), F32)],
        input_output_aliases={9 + j: j for j in range(n_into)},
        scratch_shapes=[pltpu.VMEM((R, 128), F32)] * 3 + [pltpu.VMEM((sub, 128), F32)] * 2 + [pltpu.VMEM((R, 128), F32)] * 2,
        compiler_params=_cparams(("parallel", "arbitrary")),
    )(proj, proj, proj, proj, proj, do, lse, delta, bias.reshape(3, 2, 2 * SPAN, 2 * SPAN), *(into or ()))


def _dilated_merge(os_, lses, *, S, bt=512):
    tile = pl.BlockSpec((bt, 128), lambda i, p: (i, p))

    def body(o0, o1, o2, l0, l1, l2, o_ref, om_ref, lse_ref):
        lane = lax.broadcasted_iota(jnp.int32, (bt, 128), 1)
        lo = lane < DIL_HEAD_DIM
        ls = [l0[...], l1[...], l2[...]]
        ws, stat = [], jnp.zeros((bt, 128), F32)
        for e in range(2):
            a = [l[:, e:e + 1] for l in ls]
            m = jnp.maximum(jnp.maximum(a[0], a[1]), a[2])
            ex = [jnp.exp(v - m) for v in a]
            tot = ex[0] + ex[1] + ex[2]
            ws.append([v / tot for v in ex])
            stat = jnp.where(lane == e, m + jnp.log(tot), stat)
        acc = jnp.zeros((bt, 128), F32)
        for gi, o in enumerate((o0, o1, o2)):
            acc = acc + jnp.where(lo, ws[0][gi], ws[1][gi]) * o[...]
        o_ref[...] = acc
        om_ref[...] = _mx(acc)
        lse_ref[...] = stat

    return pl.pallas_call(
        body, name="dil_merge", grid=(S // bt, 2),
        in_specs=[tile] * 6, out_specs=[tile, tile, tile],
        out_shape=[jax.ShapeDtypeStruct((S, 256), F32), jax.ShapeDtypeStruct((S, 256), MXU_DTYPE),
                   jax.ShapeDtypeStruct((S, 256), F32)],
        compiler_params=_cparams(("parallel", "parallel")),
    )(*os_, *lses)


def _with_delta(do, o):
    lane = lax.broadcasted_iota(jnp.int32, (do.shape[0], 128), 1)
    stats = []
    for p in range(2):
        prod = do[:, 128 * p:128 * (p + 1)] * o[:, 128 * p:128 * (p + 1)]
        d0 = jnp.sum(jnp.where(lane < DIL_HEAD_DIM, prod, 0.0), axis=-1, keepdims=True)
        d1 = jnp.sum(jnp.where(lane >= DIL_HEAD_DIM, prod, 0.0), axis=-1, keepdims=True)
        stats.append(jnp.where(lane == 0, d0, jnp.where(lane == 1, d1, 0.0)))
    return do, jnp.concatenate(stats, axis=1)


MEM_T = 2048
QM_BLK = C_QM // MEM_HEAD_DIM


def _mem_attn_fwd(proj, kv, *, S):
    scale = MEM_HEAD_DIM ** -0.5

    def body(q_ref, k_ref, v_ref, o_ref, om_ref, lse_ref):
        s = _dot(q_ref[...], k_ref[...], "nt") * scale
        m = jnp.max(s, axis=-1, keepdims=True)
        p = jnp.exp(s - m)
        den = jnp.sum(p, axis=-1, keepdims=True)
        o = _dot(p, v_ref[...]) / den
        o_ref[...] = o
        om_ref[...] = _mx(o)
        lse_ref[0] = m + jnp.log(den)

    return pl.pallas_call(
        body, name="mem_attn_fwd", grid=(S // MEM_T, MEM_HEADS),
        in_specs=[pl.BlockSpec((MEM_T, MEM_HEAD_DIM), lambda i, h: (i, QM_BLK + h)),
                  pl.BlockSpec((N_MEM, MEM_HEAD_DIM), lambda i, h: (0, h)),
                  pl.BlockSpec((N_MEM, MEM_HEAD_DIM), lambda i, h: (0, MEM_HEADS + h))],
        out_specs=[pl.BlockSpec((MEM_T, MEM_HEAD_DIM), lambda i, h: (i, h)),
                   pl.BlockSpec((MEM_T, MEM_HEAD_DIM), lambda i, h: (i, h)),
                   pl.BlockSpec((1, MEM_T, 1), lambda i, h: (h, i, 0))],
        out_shape=[jax.ShapeDtypeStruct((S, MEM_WIDTH), F32), jax.ShapeDtypeStruct((S, MEM_WIDTH), MXU_DTYPE),
                   jax.ShapeDtypeStruct((MEM_HEADS, S, 1), F32)],
        compiler_params=_cparams(("parallel", "parallel")),
    )(proj, kv, kv)


def _mem_attn_bwd(proj, kv, om, lse, dom, *, S):
    scale = MEM_HEAD_DIM ** -0.5

    def body(q_ref, k_ref, v_ref, o_ref, lse_ref, do_ref, dq_ref, dk_ref, dv_ref):
        @pl.when(pl.program_id(1) == 0)
        def _():
            dk_ref[...] = jnp.zeros_like(dk_ref)
            dv_ref[...] = jnp.zeros_like(dv_ref)

        qv, kv_, vv, dov = q_ref[...], k_ref[...], v_ref[...], do_ref[...]
        p = jnp.exp(_dot(qv, kv_, "nt") * scale - lse_ref[0])
        delta = jnp.sum(dov * o_ref[...], axis=-1, keepdims=True)
        ds = p * (_dot(dov, vv, "nt") - delta)
        dq_ref[...] = (_dot(ds, kv_) * scale).astype(dq_ref.dtype)
        dk_ref[...] += _dot(ds, qv, "tn") * scale
        dv_ref[...] += _dot(p, dov, "tn")

    tile = pl.BlockSpec((MEM_T, MEM_HEAD_DIM), lambda h, i: (i, h))
    kvo = pl.BlockSpec((N_MEM, MEM_HEAD_DIM), lambda h, i: (0, h))
    return pl.pallas_call(
        body, name="mem_attn_bwd", grid=(MEM_HEADS, S // MEM_T),
        in_specs=[pl.BlockSpec((MEM_T, MEM_HEAD_DIM), lambda h, i: (i, QM_BLK + h)),
                  pl.BlockSpec((N_MEM, MEM_HEAD_DIM), lambda h, i: (0, h)),
                  pl.BlockSpec((N_MEM, MEM_HEAD_DIM), lambda h, i: (0, MEM_HEADS + h)),
                  tile, pl.BlockSpec((1, MEM_T, 1), lambda h, i: (h, i, 0)), tile],
        out_specs=[tile, kvo, kvo],
        out_shape=[jax.ShapeDtypeStruct((S, MEM_WIDTH), MXU_DTYPE), jax.ShapeDtypeStruct((N_MEM, MEM_WIDTH), F32),
                   jax.ShapeDtypeStruct((N_MEM, MEM_WIDTH), F32)],
        compiler_params=_cparams(("parallel", "arbitrary")),
    )(proj, kv, kv, om, lse, dom)


MIX_BM = 1024
MIX_BN = 256
GATES_BLK = C_GATES // MIX_BN


def _mix_specs(j_outer):
    ix = (lambda f: (lambda j, i: f(i, j))) if j_outer else (lambda f: f)
    act = lambda width: pl.BlockSpec((MIX_BM, width), ix(lambda i, j: (i, 0)))
    wgt = lambda width: pl.BlockSpec((width, MIX_BN), ix(lambda i, j: (0, j)))
    gate = lambda b: pl.BlockSpec((MIX_BM, MIX_BN), ix(lambda i, j: (i, GATES_BLK + 4 * b + j)))
    bias = lambda b: pl.BlockSpec((1, MIX_BN), ix(lambda i, j: (0, 4 * b + j)))
    tile = pl.BlockSpec((MIX_BM, MIX_BN), ix(lambda i, j: (i, j)))
    return act, wgt, gate, bias, tile


def _mix_fwd(z_lru, o_dil, om, w_lru, w_dil, w_mem, proj, b_gate, *, S):
    act, wgt, gate, bias, tile = _mix_specs(False)

    def body(zl, od, mo, wl, wd, wm, g0, g1, g2, b0, b1, b2, o_ref):
        acc = jax.nn.sigmoid(g0[...] + b0[...]) * _dot(zl[...], wl[...])
        acc += jax.nn.sigmoid(g1[...] + b1[...]) * _dot(od[...], wd[...])
        acc += jax.nn.sigmoid(g2[...] + b2[...]) * _dot(mo[...], wm[...])
        o_ref[...] = acc.astype(o_ref.dtype)

    return pl.pallas_call(
        body, name="mix_fwd", grid=(S // MIX_BM, D_MODEL // MIX_BN),
        in_specs=[act(D_RNN), act(256), act(MEM_WIDTH), wgt(D_RNN), wgt(256), wgt(MEM_WIDTH),
                  gate(0), gate(1), gate(2), bias(0), bias(1), bias(2)],
        out_specs=tile, out_shape=jax.ShapeDtypeStruct((S, D_MODEL), MXU_DTYPE),
        compiler_params=_cparams(("parallel", "parallel")),
    )(z_lru, o_dil, om, w_lru, w_dil, w_mem, proj, proj, proj, b_gate, b_gate, b_gate)


def _mix_bwd(dx1, w_out, z_lru, o_dil, om, w_lru, w_dil, w_mem, proj, b_gate, *, S):
    act, wgt, gate, bias, tile = _mix_specs(False)
    n_j = D_MODEL // MIX_BN

    def body(dx, wo, zl, od, mo, wl, wd, wm, g0, g1, g2, b0, b1, b2,
             dg0, dg1, dg2, dy0, dy1, dy2, db0, db1, db2):
        j = pl.program_id(1)

        @pl.when((pl.program_id(0) == 0) & (j == 0))
        def _():
            for r in (db0, db1, db2):
                r[...] = jnp.zeros_like(r)

        dmv = _dot(dx[...], wo[...], "nt")
        for act_ref, w_ref, g_ref, b_ref, dg_ref, dy_ref, db_ref in (
                (zl, wl, g0, b0, dg0, dy0, db0), (od, wd, g1, b1, dg1, dy1, db1), (mo, wm, g2, b2, dg2, dy2, db2)):
            y = _dot(act_ref[...], w_ref[...])
            gt = jax.nn.sigmoid(g_ref[...] + b_ref[...])
            dgate = dmv * y * gt * (1.0 - gt)
            dg_ref[...] = dgate.astype(dg_ref.dtype)
            dy_ref[...] = (dmv * gt).astype(dy_ref.dtype)
            db_ref[j] += _colsum(dgate)

    big = jax.ShapeDtypeStruct((S, D_MODEL), MXU_DTYPE)
    vec = jax.ShapeDtypeStruct((n_j, 1, MIX_BN), F32)
    vspec = pl.BlockSpec((n_j, 1, MIX_BN), lambda i, j: (0, 0, 0))
    res = pl.pallas_call(
        body, name="mix_bwd", grid=(S // MIX_BM, n_j),
        in_specs=[pl.BlockSpec((MIX_BM, D_MODEL), lambda i, j: (i, 0)), pl.BlockSpec((MIX_BN, D_MODEL), lambda i, j: (j, 0)),
                  act(D_RNN), act(256), act(MEM_WIDTH), wgt(D_RNN), wgt(256), wgt(MEM_WIDTH),
                  gate(0), gate(1), gate(2), bias(0), bias(1), bias(2)],
        out_specs=[tile] * 6 + [vspec] * 3, out_shape=[big] * 6 + [vec] * 3,
        compiler_params=_cparams(("arbitrary", "arbitrary")),
    )(dx1, w_out, z_lru, o_dil, om, w_lru, w_dil, w_mem, proj, proj, proj, b_gate, b_gate, b_gate)
    return list(res[:6]) + [r.reshape(1, D_MODEL) for r in res[6:]]


def _adamw_math(w, g, m, v):
    m = ADAM_B1 * m + (1.0 - ADAM_B1) * g
    v = ADAM_B2 * v + (1.0 - ADAM_B2) * (g * g)
    m_hat = m / (1.0 - ADAM_B1 ** ADAM_STEP)
    v_hat = v / (1.0 - ADAM_B2 ** ADAM_STEP)
    delta = -ADAM_LR * (m_hat / (jnp.sqrt(v_hat) + ADAM_EPS) + ADAM_WD * w)
    return delta, m, v


def _adamw_landed(w, own, land, m, v, *, name, col_blk=0, prev=None):
    R = w.shape[0]
    n_parts, C = land.shape[0], land.shape[2]
    br = next(d for d in (256, 464, 128) if R % d == 0)
    tile = pl.BlockSpec((br, C), lambda i: (i, col_blk))
    part = pl.BlockSpec((br, C), lambda i: (i, 0))
    n_prev = 0 if prev is None else 4

    def body(w_ref, o_ref, l_ref, m_ref, v_ref, *rest):
        g_ref, d_ref, nm_ref, nv_ref = rest[n_prev:]
        g = o_ref[...].astype(F32)
        for p in range(n_parts):
            g = g + l_ref[p].astype(F32)
        d, nm, nv = _adamw_math(w_ref[...], g, m_ref[...], v_ref[...])
        g_ref[...] = g
        d_ref[...] = d
        nm_ref[...] = nm
        nv_ref[...] = nv

    return pl.pallas_call(
        body, name=name, grid=(R // br,),
        in_specs=[tile, part, pl.BlockSpec((n_parts, br, C), lambda i: (0, i, 0)), tile, tile]
        + [pl.BlockSpec(memory_space=pl.ANY)] * n_prev,
        out_specs=[tile] * 4, out_shape=[jax.ShapeDtypeStruct(w.shape, F32)] * 4,
        input_output_aliases={5 + j: j for j in range(n_prev)},
        compiler_params=_cparams(("parallel",)),
    )(w, own, land, m, v, *(prev or ()))


def _adamw_plain(w, g, m, v, *, name):
    def body(w_ref, g_ref, m_ref, v_ref, d_ref, nm_ref, nv_ref):
        d, nm, nv = _adamw_math(w_ref[...], g_ref[...], m_ref[...], v_ref[...])
        d_ref[...] = d
        nm_ref[...] = nm
        nv_ref[...] = nv

    return pl.pallas_call(
        body, name=name, out_shape=[jax.ShapeDtypeStruct(w.shape, F32)] * 3, compiler_params=_cparams(),
    )(w, g, m, v)


def _my_pos():
    return lax.axis_index("x"), lax.axis_index("y"), lax.axis_index("c")


def _dev_index(p):
    return 4 * p[0] + 2 * p[1] + p[2]


def _peers(me):
    x, y, c = me
    out = []
    for k in range(1, 8):
        fx, fy, fc = (k >> 2) & 1, (k >> 1) & 1, k & 1
        out.append((k - 1, (1 - x if fx else x, 1 - y if fy else y, 1 - c if fc else c)))
    return out


HBM_SPEC = pl.BlockSpec(memory_space=pltpu.HBM)
SEM_SPEC = pl.BlockSpec(memory_space=pltpu.SEMAPHORE)
DATAFLOW_EFFECT = pltpu.SideEffectType.DATAFLOW_SIDE_EFFECTING


def _gather_refs(src, land, me, peer, k):
    return src, land.at[_dev_index(me)]


def _scatter_refs(src, land, me, peer, k):
    return src.at[_dev_index(peer)], land.at[k]


ALL_RELATIONS = tuple(range(7))
ONE_PER_CHIP = (0, 1, 3, 5)


def _push_start(srcs, land_shapes, refs_of, name, after=(), relations=ALL_RELATIONS):
    n, n_after = len(srcs), len(after)

    def body(*refs):
        ins, lands = refs[:n], refs[n:2 * n]
        send_sems, recv_sems, token = refs[2 * n + n_after], refs[2 * n + n_after + 1], refs[-1]
        me = _my_pos()
        for k, peer in _peers(me):
            if k not in relations:
                continue
            for a in range(n):
                src, dst = refs_of(ins[a], lands[a], me, peer, k)
                pltpu.make_async_remote_copy(src_ref=src, dst_ref=dst, send_sem=send_sems.at[7 * a + k],
                                             recv_sem=recv_sems.at[7 * a + k], device_id=peer, device_id_type=MESH).start()
        token[...] = jnp.zeros_like(token)

    lands = [lax.empty(shp, s.dtype) for shp, s in zip(land_shapes, srcs)]
    hbm = lambda a: pltpu.with_memory_space_constraint(a, pltpu.HBM)
    res = pl.pallas_call(
        body, name=name,
        out_shape=(pltpu.SemaphoreType.DMA((7 * n,)), pltpu.SemaphoreType.DMA((7 * n,)),
                   *[pltpu.HBM(s.shape, s.dtype) for s in srcs], *[pltpu.HBM(l.shape, l.dtype) for l in lands],
                   jax.ShapeDtypeStruct((8, 128), F32)),
        in_specs=[HBM_SPEC] * (2 * n) + [pl.BlockSpec(memory_space=pl.ANY)] * n_after,
        out_specs=(SEM_SPEC, SEM_SPEC, *[HBM_SPEC] * (2 * n), pl.BlockSpec(memory_space=pltpu.VMEM)),
        input_output_aliases={i: 2 + i for i in range(2 * n)},
        compiler_params=pltpu.CompilerParams(has_side_effects=DATAFLOW_EFFECT),
    )(*[hbm(s) for s in srcs], *[hbm(l) for l in lands], *after)
    return dict(sems=(res[0], res[1]), srcs=list(res[2:2 + n]), lands=list(res[2 + n:2 + 2 * n]), token=res[-1], n=n,
                refs_of=refs_of, name=name, relations=relations)


def _push_wait(started, after):
    n, refs_of, relations = started["n"], started["refs_of"], started["relations"]
    after = list(after) if isinstance(after, (list, tuple)) else [after]

    def body(*refs):
        ins, lands = refs[:n], refs[n:2 * n]
        send_sems, recv_sems = refs[2 * n], refs[2 * n + 1]
        me = _my_pos()
        for k, peer in _peers(me):
            if k not in relations:
                continue
            for a in range(n):
                src, dst = refs_of(ins[a], lands[a], me, peer, k)
                cp = pltpu.make_async_remote_copy(src_ref=src, dst_ref=dst, send_sem=send_sems.at[7 * a + k],
                                                  recv_sem=recv_sems.at[7 * a + k], device_id=peer, device_id_type=MESH)
                cp.wait_send()
                cp.wait_recv()

    arrs = started["srcs"] + started["lands"]
    res = pl.pallas_call(
        body, name=started["name"].replace("start", "wait"),
        out_shape=tuple(pltpu.HBM(a.shape, a.dtype) for a in arrs),
        in_specs=[HBM_SPEC] * (2 * n) + [SEM_SPEC, SEM_SPEC] + [pl.BlockSpec(memory_space=pl.ANY)] * len(after),
        out_specs=tuple([HBM_SPEC] * (2 * n)),
        input_output_aliases={i: i for i in range(2 * n)},
        compiler_params=pltpu.CompilerParams(has_side_effects=DATAFLOW_EFFECT),
    )(*arrs, *started["sems"], *after)
    return list(res[n:2 * n])


def _other_chips(x, y):
    return ((1 - x, y), (x, 1 - y), (1 - x, 1 - y))


def _forward_start(land, name, after=()):
    n_after = len(after)

    def body(*refs):
        land_ref, send_sems, recv_sems, token = refs[0], refs[1 + n_after], refs[2 + n_after], refs[-1]
        x, y, c = _my_pos()
        for j, (cx, cy) in enumerate(_other_chips(x, y)):
            blk = land_ref.at[_dev_index((cx, cy, c))]
            pltpu.make_async_remote_copy(src_ref=blk, dst_ref=blk, send_sem=send_sems.at[j], recv_sem=recv_sems.at[j],
                                         device_id=(x, y, 1 - c), device_id_type=MESH).start()
        token[...] = jnp.zeros_like(token)

    res = pl.pallas_call(
        body, name=name,
        out_shape=(pltpu.SemaphoreType.DMA((3,)), pltpu.SemaphoreType.DMA((3,)), pltpu.HBM(land.shape, land.dtype),
                   jax.ShapeDtypeStruct((8, 128), F32)),
        in_specs=[HBM_SPEC] + [pl.BlockSpec(memory_space=pl.ANY)] * n_after,
        out_specs=(SEM_SPEC, SEM_SPEC, HBM_SPEC, pl.BlockSpec(memory_space=pltpu.VMEM)),
        input_output_aliases={0: 2},
        compiler_params=pltpu.CompilerParams(has_side_effects=DATAFLOW_EFFECT),
    )(pltpu.with_memory_space_constraint(land, pltpu.HBM), *after)
    return dict(sems=(res[0], res[1]), land=res[2], token=res[3], name=name)


def _forward_wait(started, after):
    after = list(after) if isinstance(after, (list, tuple)) else [after]

    def body(land_ref, send_sems, recv_sems, *rest):
        x, y, c = _my_pos()
        for j, (cx, cy) in enumerate(_other_chips(x, y)):
            cp = pltpu.make_async_remote_copy(
                src_ref=land_ref.at[_dev_index((cx, cy, c))], dst_ref=land_ref.at[_dev_index((cx, cy, 1 - c))],
                send_sem=send_sems.at[j], recv_sem=recv_sems.at[j], device_id=(x, y, 1 - c), device_id_type=MESH)
            cp.wait_send()
            cp.wait_recv()

    land = started["land"]
    return pl.pallas_call(
        body, name=started["name"].replace("start", "wait"), out_shape=pltpu.HBM(land.shape, land.dtype),
        in_specs=[HBM_SPEC, SEM_SPEC, SEM_SPEC] + [pl.BlockSpec(memory_space=pl.ANY)] * len(after),
        out_specs=HBM_SPEC, input_output_aliases={0: 0},
        compiler_params=pltpu.CompilerParams(has_side_effects=DATAFLOW_EFFECT),
    )(land, *started["sems"], *after)


def _sum_slots(slots):
    def body(in_ref, out_ref):
        acc = in_ref[0]
        for d in range(1, N_DEV):
            acc = acc + in_ref[d]
        out_ref[...] = acc

    return pl.pallas_call(body, name="sum_small", out_shape=jax.ShapeDtypeStruct(slots.shape[1:], F32),
                          compiler_params=_cparams())(slots)


def _adamw_many(ws, gs, ms, vs):
    n = len(ws)

    def body(*refs):
        for i in range(n):
            w_ref, g_ref, m_ref, v_ref = (refs[j * n + i] for j in range(4))
            d_, nm, nv = _adamw_math(w_ref[...], g_ref[...], m_ref[...], v_ref[...])
            for j, val in enumerate((d_, nm, nv)):
                refs[(4 + j) * n + i][...] = val

    res = pl.pallas_call(body, name="adamw_small", out_shape=[jax.ShapeDtypeStruct(w_.shape, F32) for w_ in ws] * 3,
                         compiler_params=_cparams())(*ws, *gs, *ms, *vs)
    return [(res[i], res[n + i], res[2 * n + i]) for i in range(n)]


def _local_step(x, mem, tgt, W, P, late_weights, send_grads, reduce_small, tie0):
    S = x.shape[0]
    W = dict(W)
    h = _rmsnorm_fwd(x, P["g_mix"] + tie0, rows=S, name="norm_mix")
    mem_n = _rmsnorm_fwd(mem, P["g_mem"], rows=N_MEM, name="norm_mem")
    buckets = _dil_buckets()
    bias = _dil_bias(P["rel_bias"], buckets)
    wa_bd, wx_bd = _mx(_block_diag(P["w_rg_a"])), _mx(_block_diag(P["w_rg_x"]))
    W.update(late_weights("first", [h, mem_n, bias, wa_bd, wx_bd]))
    proj = _matmul(h, W["w_in_t"], M=S, N=D_IN, K=D_MODEL, mode="nt", bm=512, bn=D_IN // 2, bk=D_MODEL, name="mm_in",
                   j_outer=True, deps=[W["started"]])

    group_out = [_dilated_fwd(proj, bias, g, S=S) for g in range(len(DIL_GROUPS))]
    o_dil, o_dil_m, lse_dil = _dilated_merge([o for o, _ in group_out], [l for _, l in group_out], S=S)

    W.update(late_weights("branch", [o_dil]))
    lru_args = (W["conv_w"], P["conv_b"].reshape(1, -1), wa_bd, wx_bd, P["b_rg_a"].reshape(1, -1),
                P["b_rg_x"].reshape(1, -1), P["lru_lambda"].reshape(1, -1))
    hl, z_lru, a_lru, mult_lru = _lru_fwd(proj, *lru_args, S=S)
    kv = _matmul(mem_n, W["w_mem_kv"], M=N_MEM, N=2 * MEM_WIDTH, K=D_MODEL, mode="nn", bm=N_MEM, bn=512, bk=D_MODEL,
                 name="mm_kv")
    om, om_m, lse_mem = _mem_attn_fwd(proj, kv, S=S)
    b_gate = P["b_gate"].reshape(1, -1)
    merged = _mix_fwd(z_lru, o_dil_m, om_m, W["w_lru_out"], W["w_dil_out"], W["w_mem_out"], proj, b_gate, S=S)
    g_mlp, g_final, g_mix = (P[n].reshape(1, D_MODEL) for n in ("g_mlp", "g_final", "g_mix"))
    x1, hm = _matmul_rows(merged, W["w_out"], M=S, K=D_MODEL, mode="nn", bm=512, name="mm_out",
                          row_fn=_residual_then_norm, out_dtypes=(F32, MXU_DTYPE), tiles=[x], vecs=[g_mlp])
    W.update(late_weights("mlp", [hm]))

    def relu2(acc):
        rl = jnp.maximum(acc, 0.0)
        return rl * rl, rl

    act, relu_u = _matmul(hm, W["w_mlp_in_t"], M=S, N=D_FF, K=D_MODEL, mode="nt", bm=1024, bn=1024, bk=D_MODEL,
                          name="mm_mlp_in", out_dtypes=(MXU_DTYPE, MXU_DTYPE), epilogue=relu2, j_outer=True, sub=2)
    dx2, dx2_m, loss, dg_final = _matmul_rows(
        act, W["w_mlp_out"], M=S, K=D_FF, mode="nn", bm=512, name="mm_mlp_out", row_fn=_residual_then_loss,
        out_dtypes=(F32, MXU_DTYPE), tiles=[x1, tgt], vecs=[g_final], acc_widths=(1, D_MODEL))

    G, Gs = {}, {}
    Gs["g_final"] = dg_final
    dw = dict(mode="tn", K=S, bk=S, out_dtypes=(MXU_DTYPE,))
    G["w_mlp_out"] = _matmul(act, dx2_m, M=D_FF, N=D_MODEL, bm=512, bn=D_MODEL, name="mm_dw_mlp_out",
                             parts=("rows", D_FF // N_DEV), **dw)
    du = _matmul(dx2_m, W["w_mlp_out"], M=S, N=D_FF, K=D_MODEL, mode="nt", bm=1024, bn=1024, bk=D_MODEL, name="mm_du",
                 out_dtypes=(MXU_DTYPE,), epilogue=lambda acc, rl: (acc * (2.0 * rl.astype(F32)),),
                 extras=[(relu_u, (0, 0))], j_outer=True, sub=2)
    G["w_mlp_in"] = _matmul(hm, du, M=D_MODEL, N=D_FF, bm=D_MODEL, bn=512, name="mm_dw_mlp_in",
                            parts=("cols", D_FF // N_DEV), **dw)
    tie1 = send_grads({n: G.pop(n) for n in ("w_mlp_out", "w_mlp_in")})
    dx1, dx1_m, Gs["g_mlp"] = _matmul_rows(
        du, W["w_mlp_in_t"], M=S, K=D_FF, mode="nn", bm=512, name="mm_dhm", row_fn=_norm_bwd_then_residual(2),
        out_dtypes=(F32, MXU_DTYPE), tiles=[x1, dx2], vecs=[g_mlp], acc_widths=(D_MODEL,), deps=[tie1])
    G["w_out"] = _matmul(merged, dx1_m, M=D_MODEL, N=D_MODEL, bm=512, bn=D_MODEL, name="mm_dw_out",
                         parts=("rows", D_MODEL // N_DEV), **dw)
    (dg0, dg1, dg2, dy_lru, dy_dil, dy_mem, db0, db1, db2) = _mix_bwd(
        dx1_m, W["w_out"], z_lru, o_dil_m, om_m, W["w_lru_out"], W["w_dil_out"], W["w_mem_out"], proj, b_gate, S=S)
    Gs["b_gate0"], Gs["b_gate1"], Gs["b_gate2"] = db0, db1, db2

    G["w_mem_out"] = _matmul(om_m, dy_mem, M=MEM_WIDTH, N=D_MODEL, bm=MEM_WIDTH, bn=D_MODEL, name="mm_dw_mem_out",
                             parts=("cols", D_MODEL // N_DEV), **dw)
    dom = _matmul(dy_mem, W["w_mem_out"], M=S, N=MEM_WIDTH, K=D_MODEL, mode="nt", bm=512, bn=MEM_WIDTH, bk=D_MODEL,
                  name="mm_dom")
    dqm, dk_mem, dv_mem = _mem_attn_bwd(proj, kv, om, lse_mem, dom, S=S)
    dkv = jnp.concatenate([dk_mem, dv_mem], axis=1)
    G["w_mem_kv"] = _matmul(mem_n, dkv, M=D_MODEL, N=2 * MEM_WIDTH, K=N_MEM, mode="tn", bm=D_MODEL, bn=2 * MEM_WIDTH,
                            bk=N_MEM, name="mm_dw_kv", out_dtypes=(MXU_DTYPE,), parts=("rows", D_MODEL // N_DEV))
    dmem_n = _matmul(dkv, W["w_mem_kv"], M=N_MEM, N=D_MODEL, K=2 * MEM_WIDTH, mode="nt", bm=N_MEM, bn=D_MODEL,
                     bk=2 * MEM_WIDTH, name="mm_dmem")
    (Gs["g_mem"],) = _rmsnorm_bwd(mem, P["g_mem"], dmem_n, None, rows=N_MEM, name="norm_mem_bwd", dx_dtypes=())

    G["w_dil_out"] = _matmul(o_dil_m, dy_dil, M=256, N=D_MODEL, bm=256, bn=D_MODEL, name="mm_dw_dil_out",
                             parts=("cols", D_MODEL // N_DEV), **dw)
    do_dil, delta = _matmul(dy_dil, W["w_dil_out"], M=S, N=256, K=D_MODEL, mode="nt", bm=512, bn=256, bk=D_MODEL,
                            name="mm_do_dil", out_dtypes=(F32, F32), epilogue=_with_delta, extras=[(o_dil, (0, 0))])
    G["w_lru_out"] = _matmul(z_lru, dy_lru, M=D_RNN, N=D_MODEL, bm=D_RNN, bn=D_MODEL, name="mm_dw_lru_out",
                             parts=("cols", D_MODEL // N_DEV), **dw)
    dz = _matmul(dy_lru, W["w_lru_out"], M=S, N=D_RNN, K=D_MODEL, mode="nt", bm=512, bn=D_RNN, bk=D_MODEL, name="mm_dz_lru")
    tie2 = send_grads({n: G.pop(n) for n in ("w_out", "w_mem_out", "w_mem_kv", "w_dil_out", "w_lru_out")})
    bias = bias + tie2[0, 0]
    dqkv, dbias = None, []
    for g in range(len(DIL_GROUPS)):
        *dqkv, db_g = _dilated_bwd(proj, do_dil, lse_dil, delta, bias, g, S=S, into=dqkv)
        dbias.append(db_g)
    drel = _dil_bias_bwd(jnp.stack(dbias, axis=0).reshape(len(DIL_GROUPS), DIL_HEADS, SPAN, 2 * SPAN), buckets)
    Gs["rel_bias"] = drel

    dxl, dgl, dcw, dcb, dwa, dwx, dba, dbx, dlam = _lru_bwd(proj, hl, a_lru, mult_lru, dz, *lru_args, S=S)
    Gs["conv_w"], Gs["conv_b"] = dcw, dcb
    Gs["w_rg_a"], Gs["w_rg_x"] = _block_diag_extract(dwa), _block_diag_extract(dwx)
    Gs["b_rg_a"], Gs["b_rg_x"], Gs["lru_lambda"] = dba, dbx, dlam
    Gs["loss"] = loss

    dproj = [dxl, dgl] + dqkv + [dqm, dg0, dg1, dg2]
    tie = []
    for q in range(W_IN_PIECES):
        dw_q = None
        for half in range(2):
            dw_q = _dw_in_t_half(h, dproj, q, half, S=S, name=f"mm_dw_in_{q}_{half}", into=dw_q, deps=tie)
        tie = [send_grads({f"w_in_{q}": dw_q})]
    grad_x, Gs["g_mix"] = _matmul_rows(
        dproj, W["w_in_t"], M=S, K=D_IN, mode="nn", bm=256, name="mm_dh", row_fn=_norm_bwd_then_residual(1),
        out_dtypes=(F32,), tiles=[x, dx1], vecs=[g_mix], acc_widths=(D_MODEL,), deps=tie)
    return grad_x, reduce_small(Gs)


BIG = ("w_in", "w_lru_out", "w_dil_out", "w_mem_kv", "w_mem_out", "w_out", "w_mlp_in", "w_mlp_out")
W_IN_PIECES = 2
COL_SHARDED = ("w_lru_out", "w_dil_out", "w_mem_out", "w_mlp_in")
GATHERED_TRANSPOSED = ("w_mlp_in",)
SMALL = ("g_mix", "b_gate", "conv_b", "w_rg_a", "b_rg_a", "w_rg_x", "b_rg_x", "lru_lambda", "rel_bias", "g_mem",
         "g_mlp", "g_final")
WEIGHTS = ("g_mix", "w_in", "b_gate", "conv_w", "conv_b", "w_rg_a", "b_rg_a", "w_rg_x", "b_rg_x", "lru_lambda",
           "w_lru_out", "rel_bias", "w_dil_out", "g_mem", "w_mem_kv", "w_mem_out", "w_out", "g_mlp", "w_mlp_in",
           "w_mlp_out", "g_final")


def _gathered_to_full(name, gathered):
    if name in COL_SHARDED:
        n, r, c = gathered.shape
        return gathered.transpose(1, 0, 2).reshape(r, n * c)
    n, r, c = gathered.shape
    return gathered.reshape(n * r, c)


SMALL_GRADS = (("g_mix", (1, 1024)), ("b_gate0", (1, 1024)), ("b_gate1", (1, 1024)), ("b_gate2", (1, 1024)),
               ("conv_b", (1, 768)), ("w_rg_a", (12, 64, 64)), ("b_rg_a", (1, 768)), ("w_rg_x", (12, 64, 64)),
               ("b_rg_x", (1, 768)), ("lru_lambda", (1, 768)), ("rel_bias", (32, 128)), ("g_mem", (1, 1024)),
               ("g_mlp", (1, 1024)), ("g_final", (1, 1024)), ("conv_w", (4, 768)), ("loss", (1, 1)))


def _pack(parts):
    flat = jnp.concatenate([p.reshape(-1) for p in parts])
    return jnp.pad(flat, (0, (-flat.shape[0]) % 1024)).reshape(-1, 128)


def _unpack(pack, shapes):
    flat = pack.reshape(-1)
    out, off = [], 0
    for shp in shapes:
        size = math.prod(shp)
        out.append(flat[off:off + size].reshape(shp))
        off += size
    return out


def kernel(x, mem, g_mix, w_in, b_gate, conv_w, conv_b, w_rg_a, b_rg_a, w_rg_x, b_rg_x, lru_lambda, w_lru_out, rel_bias, w_dil_out, g_mem, w_mem_kv, w_mem_out, w_out, g_mlp, w_mlp_in, w_mlp_out, g_final, loss_target, m_g_mix, m_w_in, m_b_gate, m_conv_w, m_conv_b, m_w_rg_a, m_b_rg_a, m_w_rg_x, m_b_rg_x, m_lru_lambda, m_w_lru_out, m_rel_bias, m_w_dil_out, m_g_mem, m_w_mem_kv, m_w_mem_out, m_w_out, m_g_mlp, m_w_mlp_in, m_w_mlp_out, m_g_final, v_g_mix, v_w_in, v_b_gate, v_conv_w, v_conv_b, v_w_rg_a, v_b_rg_a, v_w_rg_x, v_b_rg_x, v_lru_lambda, v_w_lru_out, v_rel_bias, v_w_dil_out, v_g_mem, v_w_mem_kv, v_w_mem_out, v_w_out, v_g_mlp, v_w_mlp_in, v_w_mlp_out, v_g_final):
    w = dict(g_mix=g_mix, w_in=w_in, b_gate=b_gate, conv_w=conv_w, conv_b=conv_b, w_rg_a=w_rg_a, b_rg_a=b_rg_a,
             w_rg_x=w_rg_x, b_rg_x=b_rg_x, lru_lambda=lru_lambda, w_lru_out=w_lru_out, rel_bias=rel_bias,
             w_dil_out=w_dil_out, g_mem=g_mem, w_mem_kv=w_mem_kv, w_mem_out=w_mem_out, w_out=w_out, g_mlp=g_mlp,
             w_mlp_in=w_mlp_in, w_mlp_out=w_mlp_out, g_final=g_final)
    m = dict(g_mix=m_g_mix, w_in=m_w_in, b_gate=m_b_gate, conv_w=m_conv_w, conv_b=m_conv_b, w_rg_a=m_w_rg_a,
             b_rg_a=m_b_rg_a, w_rg_x=m_w_rg_x, b_rg_x=m_b_rg_x, lru_lambda=m_lru_lambda, w_lru_out=m_w_lru_out,
             rel_bias=m_rel_bias, w_dil_out=m_w_dil_out, g_mem=m_g_mem, w_mem_kv=m_w_mem_kv, w_mem_out=m_w_mem_out,
             w_out=m_w_out, g_mlp=m_g_mlp, w_mlp_in=m_w_mlp_in, w_mlp_out=m_w_mlp_out, g_final=m_g_final)
    v = dict(g_mix=v_g_mix, w_in=v_w_in, b_gate=v_b_gate, conv_w=v_conv_w, conv_b=v_conv_b, w_rg_a=v_w_rg_a,
             b_rg_a=v_b_rg_a, w_rg_x=v_w_rg_x, b_rg_x=v_b_rg_x, lru_lambda=v_lru_lambda, w_lru_out=v_w_lru_out,
             rel_bias=v_rel_bias, w_dil_out=v_w_dil_out, g_mem=v_g_mem, w_mem_kv=v_w_mem_kv, w_mem_out=v_w_mem_out,
             w_out=v_w_out, g_mlp=v_g_mlp, w_mlp_in=v_w_mlp_in, w_mlp_out=v_w_mlp_out, g_final=v_g_final)

    my_idx = _dev_index(_my_pos())

    w_in_shard = _mx(w["w_in"].T)
    first = _push_start([w_in_shard], [(N_DEV,) + w_in_shard.shape], _gather_refs, "gather_in_start",
                        relations=ONE_PER_CHIP)
    cw_cols = D_RNN // N_DEV
    conv_pad = jnp.zeros((64, D_MODEL), F32).at[:CONV_WIDTH, :cw_cols].set(w["conv_w"])
    late = {}
    P = {n: w[n] for n in SMALL}

    def start_late(order_after):
        for group, names in (("branch", ("w_mem_kv", "w_lru_out", "w_dil_out", "w_mem_out", "w_out", "conv_w")),
                             ("mlp", ("w_mlp_in", "w_mlp_out"))):
            shards = [conv_pad if n == "conv_w" else _mx(w[n].T if n in GATHERED_TRANSPOSED else w[n]) for n in names]
            started = _push_start(shards, [(N_DEV,) + s.shape for s in shards], _gather_refs, f"gather_{group}_start",
                                  after=order_after)
            late[group] = (names, shards, started)
            order_after = [started["token"]]

    def late_weights(group, after):
        if group == "first":
            (land,) = _push_wait(first, after)
            forward = _forward_start(land, "forward_in_start")
            start_late([forward["token"]])
            full = lax.dynamic_update_index_in_dim(_forward_wait(forward, forward["token"]), w_in_shard, my_idx, 0)
            return {"w_in_t": full.reshape(D_IN, D_MODEL), "started": late["mlp"][2]["token"]}
        names, shards, started = late[group]
        out = {}
        for n, land, own in zip(names, _push_wait(started, after), shards):
            full = lax.dynamic_update_index_in_dim(land, own, my_idx, 0)
            if n == "conv_w":
                out[n] = full[:, :CONV_WIDTH, :cw_cols].transpose(1, 0, 2).reshape(CONV_WIDTH, D_RNN)
            elif n in GATHERED_TRANSPOSED:
                out[n + "_t"] = full.reshape(-1, full.shape[2])
            else:
                out[n] = _gathered_to_full(n, full)
        return out

    sent, small = [], {}

    def send_grads(gs):
        names = list(gs)
        parts = [gs[n] for n in names]
        own = [lax.dynamic_index_in_dim(p, my_idx, 0, keepdims=False) for p in parts]
        started = _push_start(parts, [(N_DEV - 1,) + p.shape[1:] for p in parts], _scatter_refs,
                              f"scatter{len(sent)}_start")
        sent.append((names, own, started))
        return started["token"]

    def reduce_small(gs):
        small["pack"] = _pack([gs[n] for n, _ in SMALL_GRADS])
        small["started"] = _push_start([small["pack"]], [(N_DEV,) + small["pack"].shape], _gather_refs, "small_start")
        return small["started"]["token"]

    grad_x, last_token = _local_step(x[0], mem[0], loss_target[0], {}, P, late_weights, send_grads, reduce_small,
                                     first["token"][0, 0])

    grads, deltas, new_m, new_v = {}, {}, {}, {}
    after = last_token
    for names, own, started in sent[:-W_IN_PIECES]:
        for n, o, land in zip(names, own, _push_wait(started, after)):
            grads[n], deltas[n], new_m[n], new_v[n] = _adamw_landed(w[n], o, land, m[n], v[n], name=f"adamw_{n}")
            after = deltas[n]
    prev = None
    for q, (names, own, started) in enumerate(sent[-W_IN_PIECES:]):
        (land,) = _push_wait(started, after)
        prev = _adamw_landed(w["w_in"].T, own[0], land, m["w_in"].T, v["w_in"].T, name=f"adamw_{names[0]}",
                             col_blk=q, prev=prev)
        after = prev[1]
    grads["w_in"], deltas["w_in"], new_m["w_in"], new_v["w_in"] = [t.T for t in prev]
    (small_land,) = _push_wait(small["started"], [after] + [deltas[n] for n in BIG if n != "w_in"])
    total = _sum_slots(lax.dynamic_update_index_in_dim(small_land, small["pack"], my_idx, 0))
    summed = dict(zip([n for n, _ in SMALL_GRADS], _unpack(total, [shp for _, shp in SMALL_GRADS])))
    summed["b_gate"] = jnp.concatenate([summed.pop(f"b_gate{b}") for b in range(3)], axis=1)
    summed["rel_bias"] = summed["rel_bias"][:, :3 * DIL_HEADS]
    for n in SMALL:
        grads[n] = summed[n].reshape(w[n].shape)
    small_updates = _adamw_many([w[n] for n in SMALL], [grads[n] for n in SMALL], [m[n] for n in SMALL],
                                [v[n] for n in SMALL])
    for n, (d_, nm_, nv_) in zip(SMALL, small_updates):
        deltas[n], new_m[n], new_v[n] = d_, nm_, nv_
    conv_w_sum, loss_sum = summed["conv_w"], summed["loss"]
    grads["conv_w"] = lax.dynamic_slice(conv_w_sum, (0, my_idx * cw_cols), (CONV_WIDTH, cw_cols))
    deltas["conv_w"], new_m["conv_w"], new_v["conv_w"] = _adamw_plain(
        w["conv_w"], grads["conv_w"], m["conv_w"], v["conv_w"], name="adamw_conv_w")

    return (loss_sum.reshape(()), grad_x[None], *[grads[n] for n in WEIGHTS], *[deltas[n] for n in WEIGHTS],
            *[new_m[n] for n in WEIGHTS], *[new_v[n] for n in WEIGHTS])
```

```python
import functools
import math

import jax
import jax.numpy as jnp
from jax import lax
from jax.experimental import pallas as pl
from jax.experimental.pallas import tpu as pltpu

F32 = jnp.float32
MXU_DTYPE = jnp.bfloat16
VMEM_LIMIT_BYTES = 56 * 1024 * 1024
N_DEV = 8

D_MODEL = 1024
N_MEM = 256
MEM_HEADS = 4
MEM_HEAD_DIM = 128
MEM_WIDTH = 512
D_RNN = 768
LRU_BLOCK = 64
N_LRU_BLOCKS = 12
LRU_GROUP = 256
N_LRU_GROUPS = 3
CONV_WIDTH = 4
LRU_C = 8.0
DIL_GROUPS = ((128, 1), (512, 4), (2048, 16))
SPAN = 128
DIL_HEADS = 4
DIL_HEAD_DIM = 64
NUM_BUCKETS = 32
MAX_DISTANCE = 2048
D_FF = 4096
D_IN = 7424
EPS = 1e-6
NEG = -1e30
C_XL, C_GATE, C_QKV, C_QM, C_GATES = 0, 768, 1536, 3840, 4352

ADAM_LR = 0.001
ADAM_B1 = 0.9
ADAM_B2 = 0.999
ADAM_EPS = 1e-08
ADAM_WD = 0.01
ADAM_STEP = 10

MESH = pl.DeviceIdType.MESH
GELU_K = math.sqrt(2.0 / math.pi)


def _cparams(sem=None):
    kw = dict(vmem_limit_bytes=VMEM_LIMIT_BYTES)
    if sem is not None:
        kw["dimension_semantics"] = sem
    return pltpu.CompilerParams(**kw)


def _mx(v):
    return v.astype(MXU_DTYPE)


def _dot(a, b, mode="nn"):
    dims = {"nn": (((1,), (0,)), ((), ())), "nt": (((1,), (1,)), ((), ())), "tn": (((0,), (0,)), ((), ()))}[mode]
    return lax.dot_general(_mx(a), _mx(b), dims, preferred_element_type=F32)


def _colsum(v):
    return jnp.sum(v, axis=0, keepdims=True)


def _matmul(a, b, *, M, N, K, mode, bm, bn, bk, name, out_dtypes=(F32,), epilogue=None, extras=(),
            a_off=(0, 0), b_off=(0, 0), j_outer=False, deps=(), parts=None):
    assert M % bm == 0 and N % bn == 0 and K % bk == 0, (name, M, N, K, bm, bn, bk)
    nm, nn, nk = M // bm, N // bn, K // bk

    def ij(f):
        if j_outer:
            return lambda j, i, k: f(i, j, k)
        return f

    if mode == "tn":
        a_spec = pl.BlockSpec((bk, bm), ij(lambda i, j, k: (k + a_off[0], i + a_off[1])))
    else:
        a_spec = pl.BlockSpec((bm, bk), ij(lambda i, j, k: (i + a_off[0], k + a_off[1])))
    if mode == "nt":
        b_spec = pl.BlockSpec((bn, bk), ij(lambda i, j, k: (j + b_off[0], k + b_off[1])))
    else:
        b_spec = pl.BlockSpec((bk, bn), ij(lambda i, j, k: (k + b_off[0], j + b_off[1])))
    ex_specs = [pl.BlockSpec((bm, bn), ij(functools.partial(lambda i, j, k, o: (i + o[0], j + o[1]), o=off)))
                for _, off in extras]
    if parts is None:
        out_dims = (M, N)
        out_spec = pl.BlockSpec((bm, bn), ij(lambda i, j, k: (i, j)))
    elif parts[0] == "rows":
        r = parts[1]
        assert bm % r == 0
        out_dims = (M // r, r, N)
        out_spec = pl.BlockSpec((bm // r, r, bn), ij(lambda i, j, k: (i, 0, j)))
    elif parts[0] == "rows_t":
        r = parts[1]
        assert bn % r == 0
        out_dims = (N // r, r, M)
        out_spec = pl.BlockSpec((bn // r, r, bm), ij(lambda i, j, k: (j, 0, i)))
    else:
        c = parts[1]
        assert bn % c == 0
        out_dims = (N // c, M, c)
        out_spec = pl.BlockSpec((bn // c, bm, c), ij(lambda i, j, k: (j, i, 0)))
    n_ex, n_out, n_dep = len(extras), len(out_dtypes), len(deps)

    def body(*refs):
        a_ref, b_ref = refs[0], refs[1]
        ex = refs[2:2 + n_ex]
        outs = refs[2 + n_ex + n_dep:2 + n_ex + n_dep + n_out]
        part = _dot(a_ref[...], b_ref[...], mode)

        def finish(acc):
            vals = epilogue(acc, *[e[...] for e in ex]) if epilogue is not None else (acc,)
            for o, v in zip(outs, vals):
                if parts is not None and parts[0] == "rows_t":
                    v = v.T
                v = v.astype(o.dtype)
                if parts is None:
                    o[...] = v
                elif parts[0] in ("rows", "rows_t"):
                    for ch in range(v.shape[0] // parts[1]):
                        o[ch] = v[ch * parts[1]:(ch + 1) * parts[1], :]
                else:
                    for ch in range(bn // parts[1]):
                        o[ch] = v[:, ch * parts[1]:(ch + 1) * parts[1]]

        if nk == 1:
            finish(part)
        else:
            acc_ref = refs[-1]
            k = pl.program_id(2)

            @pl.when(k == 0)
            def _():
                acc_ref[...] = part

            @pl.when(k > 0)
            def _():
                acc_ref[...] += part

            @pl.when(k == nk - 1)
            def _():
                finish(acc_ref[...])

    grid = (nn, nm, nk) if j_outer else (nm, nn, nk)
    res = pl.pallas_call(
        body, name=name, grid=grid,
        in_specs=[a_spec, b_spec] + ex_specs + [pl.BlockSpec(memory_space=pl.ANY)] * n_dep,
        out_specs=[out_spec] * n_out,
        out_shape=[jax.ShapeDtypeStruct(out_dims, dt) for dt in out_dtypes],
        scratch_shapes=[pltpu.VMEM((bm, bn), F32)] if nk > 1 else [],
        compiler_params=_cparams(("parallel", "parallel", "arbitrary")),
    )(a, b, *[e for e, _ in extras], *deps)
    return res[0] if n_out == 1 else res


ROW_SUBTILES = 2


def _matmul_rows(a, b, *, M, K, mode, bm, name, row_fn, out_dtypes, tiles=(), vecs=(), acc_widths=(), deps=()):
    N = D_MODEL
    assert M % bm == 0
    segs = list(a) if isinstance(a, (list, tuple)) else [a]
    widths = [s_.shape[1] for s_ in segs]
    assert sum(widths) == K and (len(segs) == 1 or mode == "nn")
    n_s, n_t, n_v, n_o, n_a, n_d = len(segs), len(tiles), len(vecs), len(out_dtypes), len(acc_widths), len(deps)
    row = pl.BlockSpec((bm, N), lambda i: (i, 0))
    b_shape = (K, N) if mode == "nn" else (N, K)

    def body(*refs):
        b_ref = refs[n_s]
        ins = refs[n_s + 1:n_s + 1 + n_t + n_v]
        outs = refs[n_s + 1 + n_t + n_v + n_d:n_s + 1 + n_t + n_v + n_d + n_o]
        accs = refs[n_s + 1 + n_t + n_v + n_d + n_o:]
        for o in accs:
            @pl.when(pl.program_id(0) == 0)
            def _(o=o):
                o[...] = jnp.zeros_like(o)

        for s_ in range(ROW_SUBTILES):
            rows = pl.ds(s_ * (bm // ROW_SUBTILES), bm // ROW_SUBTILES)
            if n_s == 1:
                acc = _dot(refs[0][rows, :], b_ref[...], mode)
            else:
                acc, k0 = None, 0
                for a_ref, w_ in zip(refs[:n_s], widths):
                    part = _dot(a_ref[rows, :], b_ref[k0:k0 + w_, :])
                    acc = part if acc is None else acc + part
                    k0 += w_
            tile_vals, partials = row_fn(acc, *[r[rows, :] for r in ins[:n_t]], *[r[...] for r in ins[n_t:]])
            for o, val in zip(outs, tile_vals):
                o[rows, :] = val.astype(o.dtype)
            for o, val in zip(accs, partials):
                o[...] += val

    res = pl.pallas_call(
        body, name=name, grid=(M // bm,),
        in_specs=[pl.BlockSpec((bm, w_), lambda i: (i, 0)) for w_ in widths] + [pl.BlockSpec(b_shape, lambda i: (0, 0))]
        + [row] * n_t + [pl.BlockSpec((1, N), lambda i: (0, 0))] * n_v + [pl.BlockSpec(memory_space=pl.ANY)] * n_d,
        out_specs=[row] * n_o + [pl.BlockSpec((1, w_), lambda i: (0, 0)) for w_ in acc_widths],
        out_shape=[jax.ShapeDtypeStruct((M, N), dt) for dt in out_dtypes]
        + [jax.ShapeDtypeStruct((1, w_), F32) for w_ in acc_widths],
        compiler_params=_cparams(("arbitrary",) if n_a else ("parallel",)),
    )(*segs, b, *tiles, *vecs, *deps)
    return res


def _dw_in_t_half(h, pieces, q, half, *, S, name, into=None, deps=(), bk=1024):
    half_w, cols = D_IN // 2, D_MODEL // W_IN_PIECES
    lo, hi = half * half_w, (half + 1) * half_w
    use, c0 = [], 0
    for p in pieces:
        w_ = p.shape[1]
        a0, a1 = max(lo, c0), min(hi, c0 + w_)
        if a1 > a0:
            use.append((p, a0 - c0, a1 - a0))
        c0 += w_
    n_p, n_into, n_d, nk = len(use), 0 if into is None else 1, len(deps), S // bk
    rows = D_IN // N_DEV

    def body(*refs):
        h_ref, p_refs = refs[0], refs[1:1 + n_p]
        o_ref, acc_ref = refs[1 + n_p + n_into + n_d], refs[-1]
        k = pl.program_id(0)
        dp = jnp.concatenate([r[:, s0:s0 + w_] for r, (_, s0, w_) in zip(p_refs, use)], axis=1)
        part = _dot(h_ref[...], dp, "tn")

        @pl.when(k == 0)
        def _():
            acc_ref[...] = part

        @pl.when(k > 0)
        def _():
            acc_ref[...] += part

        @pl.when(k == nk - 1)
        def _():
            vt = acc_ref[...].T.astype(o_ref.dtype)
            for ch in range(half_w // rows):
                o_ref[ch] = vt[ch * rows:(ch + 1) * rows, :]

    return pl.pallas_call(
        body, name=name, grid=(nk,),
        in_specs=[pl.BlockSpec((bk, cols), lambda k: (k, q))]
        + [pl.BlockSpec((bk, p.shape[1]), lambda k: (k, 0)) for p, _, _ in use]
        + [pl.BlockSpec(memory_space=pl.ANY)] * (n_into + n_d),
        out_specs=pl.BlockSpec((half_w // rows, rows, cols), lambda k: (half, 0, 0)),
        out_shape=jax.ShapeDtypeStruct((N_DEV, rows, cols), MXU_DTYPE),
        input_output_aliases={1 + n_p: 0} if n_into else {},
        scratch_shapes=[pltpu.VMEM((cols, half_w), F32)],
        compiler_params=_cparams(("arbitrary",)),
    )(h, *[p for p, _, _ in use], *([into] if n_into else []), *deps)


def _rmsnorm_fwd(x, g, *, rows, name, bt=512):
    bt = min(bt, rows)

    def body(x_ref, g_ref, o_ref):
        xv = x_ref[...]
        r = lax.rsqrt(jnp.mean(xv * xv, axis=-1, keepdims=True) + EPS)
        o_ref[...] = (xv * r * g_ref[...]).astype(o_ref.dtype)

    return pl.pallas_call(
        body, name=name, grid=(rows // bt,),
        in_specs=[pl.BlockSpec((bt, D_MODEL), lambda i: (i, 0)), pl.BlockSpec((1, D_MODEL), lambda i: (0, 0))],
        out_specs=pl.BlockSpec((bt, D_MODEL), lambda i: (i, 0)),
        out_shape=jax.ShapeDtypeStruct((rows, D_MODEL), MXU_DTYPE),
        compiler_params=_cparams(("parallel",)),
    )(x, g.reshape(1, D_MODEL))


def _rms_bwd_tile(xv, gv, dyv):
    r = lax.rsqrt(jnp.mean(xv * xv, axis=-1, keepdims=True) + EPS)
    w = dyv * gv
    dx = r * w - xv * (r * r * r) * jnp.mean(w * xv, axis=-1, keepdims=True)
    dg = _colsum(dyv * xv * r)
    return dx, dg


def _residual_then_norm(acc, x_t, g):
    x1 = x_t + acc
    r = lax.rsqrt(jnp.mean(x1 * x1, axis=-1, keepdims=True) + EPS)
    return (x1, x1 * r * g), ()


def _residual_then_loss(acc, x_t, tgt_t, g):
    x2 = x_t + acc
    r = lax.rsqrt(jnp.mean(x2 * x2, axis=-1, keepdims=True) + EPS)
    diff = x2 * r * g - tgt_t
    part = jnp.sum(jnp.mean(diff * diff, axis=-1, keepdims=True), axis=0, keepdims=True) * 0.5
    dx, dg = _rms_bwd_tile(x2, g, diff * (1.0 / D_MODEL))
    return (dx, dx), (part, dg)


def _norm_bwd_then_residual(n_out):
    def fn(acc, x_t, res_t, g):
        dx, dg = _rms_bwd_tile(x_t, g, acc)
        return (dx + res_t,) * n_out, (dg,)

    return fn


def _rmsnorm_bwd(x, g, dy, res, *, rows, name, bt=512, dx_dtypes=(F32,)):
    bt = min(bt, rows)
    has_res = res is not None

    def body(*refs):
        x_ref, g_ref, dy_ref = refs[:3]
        res_ref = refs[3] if has_res else None
        outs = refs[3 + int(has_res):]
        dx, dg = _rms_bwd_tile(x_ref[...], g_ref[...], dy_ref[...])
        if has_res:
            dx = dx + res_ref[...]
        dg_ref = outs[-1]

        @pl.when(pl.program_id(0) == 0)
        def _():
            dg_ref[...] = jnp.zeros_like(dg_ref)

        dg_ref[...] += dg
        for o in outs[:-1]:
            o[...] = dx.astype(o.dtype)

    row_spec = pl.BlockSpec((bt, D_MODEL), lambda i: (i, 0))
    vec_spec = pl.BlockSpec((1, D_MODEL), lambda i: (0, 0))
    ins = [x, g.reshape(1, D_MODEL), dy] + ([res] if has_res else [])
    return pl.pallas_call(
        body, name=name, grid=(rows // bt,),
        in_specs=[row_spec, vec_spec, row_spec] + ([row_spec] if has_res else []),
        out_specs=[row_spec] * len(dx_dtypes) + [vec_spec],
        out_shape=[jax.ShapeDtypeStruct((rows, D_MODEL), dt) for dt in dx_dtypes] + [jax.ShapeDtypeStruct((1, D_MODEL), F32)],
        compiler_params=_cparams(("arbitrary",)),
    )(*ins)


LRU_T = 512
SCAN_GROUPS = 4


def _gelu(x):
    t = jnp.tanh(GELU_K * (x + 0.044715 * x * x * x))
    return 0.5 * x * (1.0 + t), t


def _gelu_grad(x, t):
    return 0.5 * (1.0 + t) + 0.5 * x * (1.0 - t * t) * GELU_K * (1.0 + 3.0 * 0.044715 * x * x)


def _softplus_neg(lam):
    z = -lam
    u = jnp.exp(-jnp.abs(z))
    w = 1.0 + u
    l1p = jnp.where(w == 1.0, u, jnp.log(w) * u / jnp.where(w == 1.0, 1.0, w - 1.0))
    return jnp.maximum(z, 0.0) + l1p


def _shift_down(cur, prev8, k, row8):
    y = pltpu.roll(cur, k, 0)
    head = jnp.where(row8 < k, pltpu.roll(prev8, k, 0), y[0:8])
    return jnp.concatenate([head, y[8:]], axis=0)


def _shift_up(cur, next8, k, row8):
    n = cur.shape[0]
    y = pltpu.roll(cur, n - k, 0)
    tail = jnp.where(row8 >= 8 - k, pltpu.roll(next8, 8 - k, 0), y[n - 8:n])
    return jnp.concatenate([y[0:n - 8], tail], axis=0)


def _lru_gates(xl, p8, cw, cb, wa, wx, ba, bx, lam, row8, a_mult=None):
    sh = [xl] + [_shift_down(xl, p8, k, row8) for k in (1, 2, 3)]
    xc = cb + cw[3:4] * sh[0] + cw[2:3] * sh[1] + cw[1:2] * sh[2] + cw[0:1] * sh[3]
    r = jax.nn.sigmoid(_dot(xc, wa) + ba)
    i = jax.nn.sigmoid(_dot(xc, wx) + bx)
    sp = _softplus_neg(lam)
    if a_mult is None:
        la = -LRU_C * r * sp
        a = jnp.exp(la)
        mult = jnp.sqrt(jnp.tanh(-la) * (a * a + 1.0))
    else:
        a, mult = a_mult
    return dict(sh=sh, xc=xc, r=r, i=i, sp=sp, a=a, mult=mult)


def _lru_specs(n_t, reverse):
    T = LRU_T
    tt = (lambda t: n_t - 1 - t) if reverse else (lambda t: t)
    blk = lambda col0: pl.BlockSpec((T, LRU_GROUP), lambda g, t: (tt(t), col0 + g))
    prev8 = lambda col0: pl.BlockSpec((8, LRU_GROUP), lambda g, t: (jnp.maximum(tt(t) * (T // 8) - 1, 0), col0 + g))
    vec = lambda rows: pl.BlockSpec((rows, LRU_GROUP), lambda g, t: (0, g))
    wbd = pl.BlockSpec((1, LRU_GROUP, LRU_GROUP), lambda g, t: (g, 0, 0))
    return blk, prev8, vec, wbd


def _lru_fwd(proj, conv_w, conv_b, wa_bd, wx_bd, b_a, b_x, lam, *, S):
    T = LRU_T
    n_t = S // T
    blk, _, vec, wbd = _lru_specs(n_t, False)

    def body(xl_ref, gate_ref, cw_ref, cb_ref, wa_ref, wx_ref, ba_ref, bx_ref, lam_ref,
             hl_ref, z_ref, a_s, m_ref, prev8, hcar, b_s):
        @pl.when(pl.program_id(1) == 0)
        def _():
            prev8[...] = jnp.zeros_like(prev8)
            hcar[...] = jnp.zeros_like(hcar)

        row8 = lax.broadcasted_iota(jnp.int32, (8, LRU_GROUP), 0)
        xl = xl_ref[...]
        q = _lru_gates(xl, prev8[...], cw_ref[...], cb_ref[...], wa_ref[0], wx_ref[0], ba_ref[...], bx_ref[...],
                       lam_ref[...], row8)
        prev8[...] = xl[T - 8:T]
        a_s[...] = q["a"]
        m_ref[...] = q["mult"]
        b_s[...] = q["mult"] * q["i"] * q["xc"]

        def step(c, carry):
            local = []
            for u in range(SCAN_GROUPS):
                off = pl.multiple_of((c * SCAN_GROUPS + u) * 8, 8)
                A = a_s[pl.ds(off, 8), :]
                B = b_s[pl.ds(off, 8), :]
                for k in (1, 2, 4):
                    a_sh = jnp.where(row8 >= k, pltpu.roll(A, k, 0), 1.0)
                    b_sh = jnp.where(row8 >= k, pltpu.roll(B, k, 0), 0.0)
                    B = A * b_sh + B
                    A = A * a_sh
                local.append((off, A, B))
            for off, A, B in local:
                h = A * carry + B
                hl_ref[pl.ds(off, 8), :] = h
                carry = h[7:8, :]
            return carry

        hcar[...] = lax.fori_loop(0, T // (8 * SCAN_GROUPS), step, hcar[...])
        ge, _ = _gelu(gate_ref[...])
        z_ref[...] = (ge * hl_ref[...]).astype(z_ref.dtype)

    return pl.pallas_call(
        body, name="lru_fwd", grid=(N_LRU_GROUPS, n_t),
        in_specs=[blk(C_XL // LRU_GROUP), blk(C_GATE // LRU_GROUP), vec(4), vec(1), wbd, wbd, vec(1), vec(1), vec(1)],
        out_specs=[blk(0)] * 4,
        out_shape=[jax.ShapeDtypeStruct((S, D_RNN), F32), jax.ShapeDtypeStruct((S, D_RNN), MXU_DTYPE),
                   jax.ShapeDtypeStruct((S, D_RNN), F32), jax.ShapeDtypeStruct((S, D_RNN), F32)],
        scratch_shapes=[pltpu.VMEM((8, LRU_GROUP), F32), pltpu.VMEM((1, LRU_GROUP), F32), pltpu.VMEM((T, LRU_GROUP), F32)],
        compiler_params=_cparams(("parallel", "arbitrary")),
    )(proj, proj, conv_w, conv_b, wa_bd, wx_bd, b_a, b_x, lam)


def _lru_bwd(proj, hl, a_fwd, mult_fwd, dy, w_out, conv_w, conv_b, wa_bd, wx_bd, b_a, b_x, lam, *, S):
    T = LRU_T
    n_t = S // T
    blk, prev8s, vec, wbd = _lru_specs(n_t, True)

    def body(xl_ref, xlp_ref, gate_ref, hl_ref, hlp_ref, a_ref, m_ref, dy_ref, wo_ref, cw_ref, cb_ref, wa_ref, wx_ref,
             ba_ref, bx_ref, lam_ref, dxl_ref, dgate_ref, dcw_ref, dcb_ref, dwa_ref, dwx_ref, dba_ref, dbx_ref, dlam_ref,
             next8, gcar, c_s, b_s, l_s):
        t = pl.program_id(1)
        first_chunk = t == n_t - 1

        @pl.when(t == 0)
        def _():
            next8[...] = jnp.zeros_like(next8)
            gcar[...] = jnp.zeros_like(gcar)
            for ref in (dcw_ref, dcb_ref, dwa_ref, dwx_ref, dba_ref, dbx_ref, dlam_ref):
                ref[...] = jnp.zeros_like(ref)

        row8 = lax.broadcasted_iota(jnp.int32, (8, LRU_GROUP), 0)
        rowT = lax.broadcasted_iota(jnp.int32, (T, LRU_GROUP), 0)
        keep = jnp.where(first_chunk, 0.0, 1.0)
        xl = xl_ref[...]
        wa, wx, lam_v = wa_ref[0], wx_ref[0], lam_ref[...]
        q = _lru_gates(xl, xlp_ref[...] * keep, cw_ref[...], cb_ref[...], wa, wx, ba_ref[...], bx_ref[...], lam_v, row8,
                       a_mult=(a_ref[...], m_ref[...]))
        a, mult, r, i, xc, sp = q["a"], q["mult"], q["r"], q["i"], q["xc"], q["sp"]
        hl_v = hl_ref[...]
        dz_v = _dot(dy_ref[...], wo_ref[...], "nt")
        gate = gate_ref[...]
        ge, th = _gelu(gate)
        dgate_ref[...] = (dz_v * hl_v * _gelu_grad(gate, th)).astype(dgate_ref.dtype)

        c_s[...] = jnp.where(rowT == T - 1, 0.0, pltpu.roll(a, T - 1, 0))
        b_s[...] = dz_v * ge + jnp.where(rowT == T - 1, gcar[...], 0.0)

        def step(n, carry):
            local = []
            for u in range(SCAN_GROUPS):
                off = pl.multiple_of((T // 8 - 1 - (n * SCAN_GROUPS + u)) * 8, 8)
                C = c_s[pl.ds(off, 8), :]
                B = b_s[pl.ds(off, 8), :]
                for k in (1, 2, 4):
                    c_sh = jnp.where(row8 < 8 - k, pltpu.roll(C, 8 - k, 0), 1.0)
                    b_sh = jnp.where(row8 < 8 - k, pltpu.roll(B, 8 - k, 0), 0.0)
                    B = B + C * b_sh
                    C = C * c_sh
                local.append((off, C, B))
            for off, C, B in local:
                lam_t = B + C * carry
                l_s[pl.ds(off, 8), :] = lam_t
                carry = lam_t[0:1, :]
            return carry

        lax.fori_loop(0, T // (8 * SCAN_GROUPS), step, jnp.zeros((1, LRU_GROUP), F32))
        lmb = l_s[...]
        gcar[...] = a[0:1, :] * lmb[0:1, :]

        h_prev = _shift_down(hl_v, hlp_ref[...] * keep, 1, row8)
        da = lmb * h_prev
        dmult = lmb * i * xc
        di = lmb * mult * xc
        dxc = lmb * mult * i
        dla = da * a - dmult * (a * a) / mult
        dr = dla * (-LRU_C * sp)
        dlam_ref[...] += _colsum(dla * (-LRU_C * r)) * (-jax.nn.sigmoid(-lam_v))
        dpa = dr * r * (1.0 - r)
        dpx = di * i * (1.0 - i)
        dxc = dxc + _dot(dpa, wa, "nt") + _dot(dpx, wx, "nt")
        dwa_ref[0] += _dot(xc, dpa, "tn")
        dwx_ref[0] += _dot(xc, dpx, "tn")
        dba_ref[...] += _colsum(dpa)
        dbx_ref[...] += _colsum(dpx)
        dcb_ref[...] += _colsum(dxc)
        cw = cw_ref[...]
        n8 = next8[...]
        dxl = cw[3:4] * dxc
        for k in (1, 2, 3):
            dxl = dxl + cw[3 - k:4 - k] * _shift_up(dxc, n8, k, row8)
        for k in range(4):
            dcw_ref[3 - k:4 - k, :] += _colsum(dxc * q["sh"][k])
        next8[...] = dxc[0:8]
        dxl_ref[...] = dxl.astype(dxl_ref.dtype)

    res = pl.pallas_call(
        body, name="lru_bwd", grid=(N_LRU_GROUPS, n_t),
        in_specs=[blk(C_XL // LRU_GROUP), prev8s(C_XL // LRU_GROUP), blk(C_GATE // LRU_GROUP), blk(0), prev8s(0), blk(0),
                  blk(0), pl.BlockSpec((T, D_MODEL), lambda g, t: (n_t - 1 - t, 0)),
                  pl.BlockSpec((LRU_GROUP, D_MODEL), lambda g, t: (g, 0)), vec(4), vec(1), wbd, wbd, vec(1), vec(1), vec(1)],
        out_specs=[blk(0), blk(0), vec(4), vec(1), wbd, wbd, vec(1), vec(1), vec(1)],
        out_shape=[jax.ShapeDtypeStruct((S, D_RNN), MXU_DTYPE), jax.ShapeDtypeStruct((S, D_RNN), MXU_DTYPE),
                   jax.ShapeDtypeStruct((4, D_RNN), F32), jax.ShapeDtypeStruct((1, D_RNN), F32),
                   jax.ShapeDtypeStruct((N_LRU_GROUPS, LRU_GROUP, LRU_GROUP), F32),
                   jax.ShapeDtypeStruct((N_LRU_GROUPS, LRU_GROUP, LRU_GROUP), F32),
                   jax.ShapeDtypeStruct((1, D_RNN), F32), jax.ShapeDtypeStruct((1, D_RNN), F32),
                   jax.ShapeDtypeStruct((1, D_RNN), F32)],
        scratch_shapes=[pltpu.VMEM((8, LRU_GROUP), F32), pltpu.VMEM((1, LRU_GROUP), F32),
                        pltpu.VMEM((T, LRU_GROUP), F32), pltpu.VMEM((T, LRU_GROUP), F32), pltpu.VMEM((T, LRU_GROUP), F32)],
        compiler_params=_cparams(("parallel", "arbitrary")),
    )(proj, proj, proj, hl, hl, a_fwd, mult_fwd, dy, w_out, conv_w, conv_b, wa_bd, wx_bd, b_a, b_x, lam)
    return res


def _block_diag(w):
    w4 = w.reshape(N_LRU_GROUPS, 4, LRU_BLOCK, 1, LRU_BLOCK)
    eye = jnp.eye(4, dtype=w.dtype).reshape(1, 4, 1, 4, 1)
    return (w4 * eye).reshape(N_LRU_GROUPS, LRU_GROUP, LRU_GROUP)


def _block_diag_extract(wbd):
    w5 = wbd.reshape(N_LRU_GROUPS, 4, LRU_BLOCK, 4, LRU_BLOCK)
    return jnp.stack([w5[:, a, :, a, :] for a in range(4)], axis=1).reshape(N_LRU_BLOCKS, LRU_BLOCK, LRU_BLOCK)


def _t5_bucket(dist):
    max_exact = NUM_BUCKETS // 2
    df = jnp.maximum(dist, 1).astype(jnp.float32)
    large = max_exact + (jnp.log(df / max_exact) / math.log(MAX_DISTANCE / max_exact)
                         * (NUM_BUCKETS - max_exact)).astype(jnp.int32)
    large = jnp.minimum(large, NUM_BUCKETS - 1)
    return jnp.where(dist < max_exact, dist, large)


def _band_offsets():
    qi = jnp.arange(SPAN)[:, None]
    kj = jnp.arange(2 * SPAN)[None, :]
    return qi + SPAN - kj


def _dil_buckets():
    off = _band_offsets()
    return jnp.stack([_t5_bucket(jnp.maximum(off, 0) * dil) for _, dil in DIL_GROUPS]).astype(jnp.int32)


def _dil_bias(rel_bias, buckets):
    def body(tbl_ref, bk_ref, o_ref):
        g = pl.program_id(0)
        qi = lax.broadcasted_iota(jnp.int32, (SPAN, 2 * SPAN), 0)
        kj = lax.broadcasted_iota(jnp.int32, (SPAN, 2 * SPAN), 1)
        off = qi + SPAN - kj
        valid = (off >= 0) & (off <= SPAN)
        bk = bk_ref[0]
        for h in range(DIL_HEADS):
            acc = jnp.zeros((SPAN, 2 * SPAN), F32)
            for b in range(NUM_BUCKETS):
                acc = jnp.where(bk == b, tbl_ref[b, g * DIL_HEADS + h], acc)
            o_ref[0, h] = jnp.where(valid, acc, NEG)

    return pl.pallas_call(
        body, name="dil_bias", grid=(3,),
        in_specs=[pl.BlockSpec(memory_space=pltpu.SMEM), pl.BlockSpec((1, SPAN, 2 * SPAN), lambda g: (g, 0, 0))],
        out_specs=pl.BlockSpec((1, DIL_HEADS, SPAN, 2 * SPAN), lambda g: (g, 0, 0, 0)),
        out_shape=jax.ShapeDtypeStruct((3, DIL_HEADS, SPAN, 2 * SPAN), F32),
        compiler_params=_cparams(("parallel",)),
    )(rel_bias, buckets)


def _dil_bias_bwd(dbias, buckets):
    def body(db_ref, bk_ref, o_ref):
        lane = lax.broadcasted_iota(jnp.int32, (1, 128), 1)
        rows = [jnp.zeros((1, 128), F32) for _ in range(NUM_BUCKETS)]
        for g in range(3):
            bk = bk_ref[g]
            for h in range(DIL_HEADS):
                d = db_ref[g, h]
                for b in range(NUM_BUCKETS):
                    tot = jnp.sum(_colsum(jnp.where(bk == b, d, 0.0)), axis=1, keepdims=True)
                    rows[b] = jnp.where(lane == g * DIL_HEADS + h, tot, rows[b])
        for b in range(NUM_BUCKETS):
            o_ref[b:b + 1, :] = rows[b]

    return pl.pallas_call(
        body, name="dil_bias_bwd",
        out_shape=jax.ShapeDtypeStruct((NUM_BUCKETS, 128), F32),
        compiler_params=_cparams(),
    )(dbias, buckets)


DIL_SUBBLOCKS = (8, 4, 1)


def _dil_layout(g, S):
    dil, m = DIL_GROUPS[g][1], DIL_SUBBLOCKS[g]
    sub = SPAN * dil
    col = [(C_QKV + t * 768 + g * 256) // 128 for t in range(3)]
    return dil, m, sub, S // (sub * m), col


def _residue_rows(b, r, dil):
    return pl.ds(b * SPAN * dil + r, SPAN, stride=dil) if dil > 1 else pl.ds(b * SPAN, SPAN)


def _for_residues(dil, fn):
    if dil <= 4:
        for r in range(dil):
            fn(r)
    else:
        lax.fori_loop(0, dil, lambda r, c: (fn(r), c)[1], 0, unroll=4)


def _pair_scores(qm, k2, bias, first_cols):
    s = _dot(qm, k2, "nt") * (DIL_HEAD_DIM ** -0.5) + bias
    kj = lax.broadcasted_iota(jnp.int32, s.shape, 1)
    return jnp.where(kj < first_cols, NEG, s)


def _dilated_fwd(proj, bias, g, *, S):
    dil, m, sub, nc, (qc, kc, vc) = _dil_layout(g, S)
    R = sub * m
    cur = lambda cb: pl.BlockSpec((R, 128), lambda p, i: (i, cb + p))
    prv = lambda cb: pl.BlockSpec((sub, 128), lambda p, i: (jnp.maximum(i * m - 1, 0), cb + p))
    out = pl.BlockSpec((R, 128), lambda p, i: (i, p))

    def body(q_ref, kp_ref, kc_ref, vp_ref, vc_ref, b_ref, o_ref, lse_ref):
        lane = lax.broadcasted_iota(jnp.int32, (SPAN, 128), 1)
        sels = (lane < DIL_HEAD_DIM, lane >= DIL_HEAD_DIM)
        for b in range(m):
            first_cols = jnp.where(pl.program_id(1) == 0, SPAN, 0) if b == 0 else 0

            def one(r, b=b, first_cols=first_cols):
                rows = _residue_rows(b, r, dil)
                before = (kc_ref, vc_ref, _residue_rows(b - 1, r, dil)) if b else (kp_ref, vp_ref, _residue_rows(0, r, dil))
                q2 = q_ref[rows, :]
                k2 = _mx(jnp.concatenate([before[0][before[2], :], kc_ref[rows, :]], axis=0))
                v2 = _mx(jnp.concatenate([before[1][before[2], :], vc_ref[rows, :]], axis=0))
                qq = jnp.concatenate([jnp.where(sels[0], q2, 0.0), jnp.where(sels[1], q2, 0.0)], axis=0)
                s = _pair_scores(qq, k2, b_ref[0, 0], first_cols)
                mx = jnp.max(s, axis=-1, keepdims=True)
                p = jnp.exp(s - mx)
                den = jnp.sum(p, axis=-1, keepdims=True)
                o = _dot(p, v2) / den
                st = mx + jnp.log(den)
                o_ref[rows, :] = jnp.where(sels[0], o[0:SPAN], o[SPAN:2 * SPAN])
                lse_ref[rows, :] = jnp.where(lane == 0, st[0:SPAN], jnp.where(lane == 1, st[SPAN:2 * SPAN], 0.0))

            _for_residues(dil, one)

    return pl.pallas_call(
        body, name=f"dil_fwd{g}", grid=(2, nc),
        in_specs=[cur(qc), prv(kc), cur(kc), prv(vc), cur(vc),
                  pl.BlockSpec((1, 1, 2 * SPAN, 2 * SPAN), lambda p, i: (g, p, 0, 0))],
        out_specs=[out, out],
        out_shape=[jax.ShapeDtypeStruct((S, 256), F32), jax.ShapeDtypeStruct((S, 256), F32)],
        compiler_params=_cparams(("parallel", "parallel")),
    )(proj, proj, proj, proj, proj, bias.reshape(3, 2, 2 * SPAN, 2 * SPAN))


def _dilated_bwd(proj, do, lse, delta, bias, g, *, S, into=None):
    dil, m, sub, nc, (qc, kc, vc) = _dil_layout(g, S)
    R = sub * m
    cl = lambda i: jnp.minimum(i, nc - 1)
    cur = lambda cb: pl.BlockSpec((R, 128), lambda p, i: (cl(i), cb + p))
    prv = lambda cb: pl.BlockSpec((sub, 128), lambda p, i: (jnp.maximum(cl(i) * m - 1, 0), cb + p))
    q_out = pl.BlockSpec((R, 128), lambda p, i: (cl(i), 2 * g + p))
    kv_out = pl.BlockSpec((R, 128), lambda p, i: (jnp.maximum(i - 1, 0), 2 * g + p))
    scale = DIL_HEAD_DIM ** -0.5
    n_into = 0 if into is None else 3

    def body(q_ref, kp_ref, kc_ref, vp_ref, vc_ref, do_ref, lse_ref, dl_ref, b_ref, *rest):
        dq_ref, dk_ref, dv_ref, db_ref, dq_s, kc_s, vc_s, kp_s, vp_s, kcar, vcar = rest[n_into:]
        i = pl.program_id(1)

        @pl.when(i == 0)
        def _():
            kcar[...] = jnp.zeros_like(kcar)
            vcar[...] = jnp.zeros_like(vcar)
            db_ref[...] = jnp.zeros_like(db_ref)

        @pl.when(i < nc)
        def _():
            lane = lax.broadcasted_iota(jnp.int32, (SPAN, 128), 1)
            sels = (lane < DIL_HEAD_DIM, lane >= DIL_HEAD_DIM)
            for b in range(m):
                first_cols = jnp.where(i == 0, SPAN, 0) if b == 0 else 0

                def one(r, b=b, first_cols=first_cols):
                    rows = _residue_rows(b, r, dil)
                    rows_before = _residue_rows(b - 1 if b else 0, r, dil)
                    k_before, v_before = (kc_ref, vc_ref) if b else (kp_ref, vp_ref)
                    q2, do2 = q_ref[rows, :], do_ref[rows, :]
                    k2 = _mx(jnp.concatenate([k_before[rows_before, :], kc_ref[rows, :]], axis=0))
                    v2 = _mx(jnp.concatenate([v_before[rows_before, :], vc_ref[rows, :]], axis=0))
                    lse_t, dl_t = lse_ref[rows, :], dl_ref[rows, :]
                    qq = _mx(jnp.concatenate([jnp.where(sels[0], q2, 0.0), jnp.where(sels[1], q2, 0.0)], axis=0))
                    dd = _mx(jnp.concatenate([jnp.where(sels[0], do2, 0.0), jnp.where(sels[1], do2, 0.0)], axis=0))
                    lse2 = jnp.concatenate([lse_t[:, 0:1], lse_t[:, 1:2]], axis=0)
                    dl2 = jnp.concatenate([dl_t[:, 0:1], dl_t[:, 1:2]], axis=0)
                    p = jnp.exp(_pair_scores(qq, k2, b_ref[0, 0], first_cols) - lse2)
                    ds = p * (_dot(dd, v2, "nt") - dl2)
                    db_ref[0] += ds
                    dqq = _dot(ds, k2) * scale
                    dq2 = jnp.where(sels[0], dqq[0:SPAN], dqq[SPAN:2 * SPAN])
                    dk2 = _dot(ds, qq, "tn") * scale
                    dv2 = _dot(p, dd, "tn")
                    dq_s[rows, :] = dq2
                    kc_s[rows, :] = dk2[SPAN:2 * SPAN]
                    vc_s[rows, :] = dv2[SPAN:2 * SPAN]
                    if b:
                        kc_s[rows_before, :] += dk2[0:SPAN]
                        vc_s[rows_before, :] += dv2[0:SPAN]
                    else:
                        kp_s[rows_before, :] = dk2[0:SPAN]
                        vp_s[rows_before, :] = dv2[0:SPAN]

                _for_residues(dil, one)
            dq_ref[...] = dq_s[...].astype(dq_ref.dtype)
            last = pl.ds((m - 1) * sub, sub)
            kcar[last, :] += kp_s[...]
            vcar[last, :] += vp_s[...]
            dk_ref[...] = kcar[...].astype(dk_ref.dtype)
            dv_ref[...] = vcar[...].astype(dv_ref.dtype)
            kcar[...] = kc_s[...]
            vcar[...] = vc_s[...]

        @pl.when(i == nc)
        def _():
            dk_ref[...] = kcar[...].astype(dk_ref.dtype)
            dv_ref[...] = vcar[...].astype(dv_ref.dtype)

    stat = pl.BlockSpec((R, 128), lambda p, i: (cl(i), p))
    big = jax.ShapeDtypeStruct((S, len(DIL_GROUPS) * 256), MXU_DTYPE)
    return pl.pallas_call(
        body, name=f"dil_bwd{g}", grid=(2, nc + 1),
        in_specs=[cur(qc), prv(kc), cur(kc), prv(vc), cur(vc), stat, stat, stat,
                  pl.BlockSpec((1, 1, 2 * SPAN, 2 * SPAN), lambda p, i: (g, p, 0, 0))]
        + [pl.BlockSpec(memory_space=pl.ANY)] * n_into,
        out_specs=[q_out, kv_out, kv_out, pl.BlockSpec((1, 2 * SPAN, 2 * SPAN), lambda p, i: (p, 0, 0))],
        out_shape=[big, big, big, jax.ShapeDtypeStruct((2, 2 * SPAN, 2 * SPAN), F32)],
        input_output_aliases={9 + j: j for j in range(n_into)},
        scratch_shapes=[pltpu.VMEM((R, 128), F32)] * 3 + [pltpu.VMEM((sub, 128), F32)] * 2 + [pltpu.VMEM((R, 128), F32)] * 2,
        compiler_params=_cparams(("parallel", "arbitrary")),
    )(proj, proj, proj, proj, proj, do, lse, delta, bias.reshape(3, 2, 2 * SPAN, 2 * SPAN), *(into or ()))


def _dilated_merge(os_, lses, *, S, bt=512):
    tile = pl.BlockSpec((bt, 128), lambda i, p: (i, p))

    def body(o0, o1, o2, l0, l1, l2, o_ref, om_ref, lse_ref):
        lane = lax.broadcasted_iota(jnp.int32, (bt, 128), 1)
        lo = lane < DIL_HEAD_DIM
        ls = [l0[...], l1[...], l2[...]]
        ws, stat = [], jnp.zeros((bt, 128), F32)
        for e in range(2):
            a = [l[:, e:e + 1] for l in ls]
            m = jnp.maximum(jnp.maximum(a[0], a[1]), a[2])
            ex = [jnp.exp(v - m) for v in a]
            tot = ex[0] + ex[1] + ex[2]
            ws.append([v / tot for v in ex])
            stat = jnp.where(lane == e, m + jnp.log(tot), stat)
        acc = jnp.zeros((bt, 128), F32)
        for gi, o in enumerate((o0, o1, o2)):
            acc = acc + jnp.where(lo, ws[0][gi], ws[1][gi]) * o[...]
        o_ref[...] = acc
        om_ref[...] = _mx(acc)
        lse_ref[...] = stat

    return pl.pallas_call(
        body, name="dil_merge", grid=(S // bt, 2),
        in_specs=[tile] * 6, out_specs=[tile, tile, tile],
        out_shape=[jax.ShapeDtypeStruct((S, 256), F32), jax.ShapeDtypeStruct((S, 256), MXU_DTYPE),
                   jax.ShapeDtypeStruct((S, 256), F32)],
        compiler_params=_cparams(("parallel", "parallel")),
    )(*os_, *lses)


def _with_delta(do, o):
    lane = lax.broadcasted_iota(jnp.int32, (do.shape[0], 128), 1)
    stats = []
    for p in range(2):
        prod = do[:, 128 * p:128 * (p + 1)] * o[:, 128 * p:128 * (p + 1)]
        d0 = jnp.sum(jnp.where(lane < DIL_HEAD_DIM, prod, 0.0), axis=-1, keepdims=True)
        d1 = jnp.sum(jnp.where(lane >= DIL_HEAD_DIM, prod, 0.0), axis=-1, keepdims=True)
        stats.append(jnp.where(lane == 0, d0, jnp.where(lane == 1, d1, 0.0)))
    return do, jnp.concatenate(stats, axis=1)


MEM_T = 2048
QM_BLK = C_QM // MEM_HEAD_DIM


def _mem_attn_fwd(proj, kv, *, S):
    scale = MEM_HEAD_DIM ** -0.5

    def body(q_ref, k_ref, v_ref, o_ref, om_ref, lse_ref):
        s = _dot(q_ref[...], k_ref[...], "nt") * scale
        m = jnp.max(s, axis=-1, keepdims=True)
        p = jnp.exp(s - m)
        den = jnp.sum(p, axis=-1, keepdims=True)
        o = _dot(p, v_ref[...]) / den
        o_ref[...] = o
        om_ref[...] = _mx(o)
        lse_ref[0] = m + jnp.log(den)

    return pl.pallas_call(
        body, name="mem_attn_fwd", grid=(S // MEM_T, MEM_HEADS),
        in_specs=[pl.BlockSpec((MEM_T, MEM_HEAD_DIM), lambda i, h: (i, QM_BLK + h)),
                  pl.BlockSpec((N_MEM, MEM_HEAD_DIM), lambda i, h: (0, h)),
                  pl.BlockSpec((N_MEM, MEM_HEAD_DIM), lambda i, h: (0, MEM_HEADS + h))],
        out_specs=[pl.BlockSpec((MEM_T, MEM_HEAD_DIM), lambda i, h: (i, h)),
                   pl.BlockSpec((MEM_T, MEM_HEAD_DIM), lambda i, h: (i, h)),
                   pl.BlockSpec((1, MEM_T, 1), lambda i, h: (h, i, 0))],
        out_shape=[jax.ShapeDtypeStruct((S, MEM_WIDTH), F32), jax.ShapeDtypeStruct((S, MEM_WIDTH), MXU_DTYPE),
                   jax.ShapeDtypeStruct((MEM_HEADS, S, 1), F32)],
        compiler_params=_cparams(("parallel", "parallel")),
    )(proj, kv, kv)


def _mem_attn_bwd(proj, kv, om, lse, dy, w_out, *, S):
    scale = MEM_HEAD_DIM ** -0.5

    def body(q_ref, k_ref, v_ref, o_ref, lse_ref, dy_ref, wo_ref, dq_ref, dk_ref, dv_ref):
        @pl.when(pl.program_id(1) == 0)
        def _():
            dk_ref[...] = jnp.zeros_like(dk_ref)
            dv_ref[...] = jnp.zeros_like(dv_ref)

        qv, kv_, vv, dov = q_ref[...], k_ref[...], v_ref[...], _dot(dy_ref[...], wo_ref[...], "nt")
        p = jnp.exp(_dot(qv, kv_, "nt") * scale - lse_ref[0])
        delta = jnp.sum(dov * o_ref[...], axis=-1, keepdims=True)
        ds = p * (_dot(dov, vv, "nt") - delta)
        dq_ref[...] = (_dot(ds, kv_) * scale).astype(dq_ref.dtype)
        dk_ref[...] += _dot(ds, qv, "tn") * scale
        dv_ref[...] += _dot(p, dov, "tn")

    tile = pl.BlockSpec((MEM_T, MEM_HEAD_DIM), lambda h, i: (i, h))
    kvo = pl.BlockSpec((N_MEM, MEM_HEAD_DIM), lambda h, i: (0, h))
    return pl.pallas_call(
        body, name="mem_attn_bwd", grid=(MEM_HEADS, S // MEM_T),
        in_specs=[pl.BlockSpec((MEM_T, MEM_HEAD_DIM), lambda h, i: (i, QM_BLK + h)),
                  pl.BlockSpec((N_MEM, MEM_HEAD_DIM), lambda h, i: (0, h)),
                  pl.BlockSpec((N_MEM, MEM_HEAD_DIM), lambda h, i: (0, MEM_HEADS + h)),
                  tile, pl.BlockSpec((1, MEM_T, 1), lambda h, i: (h, i, 0)),
                  pl.BlockSpec((MEM_T, D_MODEL), lambda h, i: (i, 0)),
                  pl.BlockSpec((MEM_HEAD_DIM, D_MODEL), lambda h, i: (h, 0))],
        out_specs=[tile, kvo, kvo],
        out_shape=[jax.ShapeDtypeStruct((S, MEM_WIDTH), MXU_DTYPE), jax.ShapeDtypeStruct((N_MEM, MEM_WIDTH), F32),
                   jax.ShapeDtypeStruct((N_MEM, MEM_WIDTH), F32)],
        compiler_params=_cparams(("parallel", "arbitrary")),
    )(proj, kv, kv, om, lse, dy, w_out)


MIX_BM = 1024
MIX_BN = 256
GATES_BLK = C_GATES // MIX_BN


def _mix_specs(j_outer):
    ix = (lambda f: (lambda j, i: f(i, j))) if j_outer else (lambda f: f)
    act = lambda width: pl.BlockSpec((MIX_BM, width), ix(lambda i, j: (i, 0)))
    wgt = lambda width: pl.BlockSpec((width, MIX_BN), ix(lambda i, j: (0, j)))
    gate = lambda b: pl.BlockSpec((MIX_BM, MIX_BN), ix(lambda i, j: (i, GATES_BLK + 4 * b + j)))
    bias = lambda b: pl.BlockSpec((1, MIX_BN), ix(lambda i, j: (0, 4 * b + j)))
    tile = pl.BlockSpec((MIX_BM, MIX_BN), ix(lambda i, j: (i, j)))
    return act, wgt, gate, bias, tile


def _mix_fwd(z_lru, o_dil, om, w_lru, w_dil, w_mem, proj, b_gate, *, S):
    act, wgt, gate, bias, tile = _mix_specs(False)

    def body(zl, od, mo, wl, wd, wm, g0, g1, g2, b0, b1, b2, o_ref):
        acc = jax.nn.sigmoid(g0[...] + b0[...]) * _dot(zl[...], wl[...])
        acc += jax.nn.sigmoid(g1[...] + b1[...]) * _dot(od[...], wd[...])
        acc += jax.nn.sigmoid(g2[...] + b2[...]) * _dot(mo[...], wm[...])
        o_ref[...] = acc.astype(o_ref.dtype)

    return pl.pallas_call(
        body, name="mix_fwd", grid=(S // MIX_BM, D_MODEL // MIX_BN),
        in_specs=[act(D_RNN), act(256), act(MEM_WIDTH), wgt(D_RNN), wgt(256), wgt(MEM_WIDTH),
                  gate(0), gate(1), gate(2), bias(0), bias(1), bias(2)],
        out_specs=tile, out_shape=jax.ShapeDtypeStruct((S, D_MODEL), MXU_DTYPE),
        compiler_params=_cparams(("parallel", "parallel")),
    )(z_lru, o_dil, om, w_lru, w_dil, w_mem, proj, proj, proj, b_gate, b_gate, b_gate)


def _mix_bwd(dx1, w_out, z_lru, o_dil, om, w_lru, w_dil, w_mem, proj, b_gate, *, S):
    act, wgt, gate, bias, tile = _mix_specs(False)
    n_j = D_MODEL // MIX_BN

    def body(dx, wo, zl, od, mo, wl, wd, wm, g0, g1, g2, b0, b1, b2,
             dg0, dg1, dg2, dy0, dy1, dy2, db0, db1, db2):
        j = pl.program_id(1)

        @pl.when((pl.program_id(0) == 0) & (j == 0))
        def _():
            for r in (db0, db1, db2):
                r[...] = jnp.zeros_like(r)

        dmv = _dot(dx[...], wo[...], "nt")
        for act_ref, w_ref, g_ref, b_ref, dg_ref, dy_ref, db_ref in (
                (zl, wl, g0, b0, dg0, dy0, db0), (od, wd, g1, b1, dg1, dy1, db1), (mo, wm, g2, b2, dg2, dy2, db2)):
            y = _dot(act_ref[...], w_ref[...])
            gt = jax.nn.sigmoid(g_ref[...] + b_ref[...])
            dgate = dmv * y * gt * (1.0 - gt)
            dg_ref[...] = dgate.astype(dg_ref.dtype)
            dy_ref[...] = (dmv * gt).astype(dy_ref.dtype)
            db_ref[j] += _colsum(dgate)

    big = jax.ShapeDtypeStruct((S, D_MODEL), MXU_DTYPE)
    vec = jax.ShapeDtypeStruct((n_j, 1, MIX_BN), F32)
    vspec = pl.BlockSpec((n_j, 1, MIX_BN), lambda i, j: (0, 0, 0))
    res = pl.pallas_call(
        body, name="mix_bwd", grid=(S // MIX_BM, n_j),
        in_specs=[pl.BlockSpec((MIX_BM, D_MODEL), lambda i, j: (i, 0)), pl.BlockSpec((MIX_BN, D_MODEL), lambda i, j: (j, 0)),
                  act(D_RNN), act(256), act(MEM_WIDTH), wgt(D_RNN), wgt(256), wgt(MEM_WIDTH),
                  gate(0), gate(1), gate(2), bias(0), bias(1), bias(2)],
        out_specs=[tile] * 6 + [vspec] * 3, out_shape=[big] * 6 + [vec] * 3,
        compiler_params=_cparams(("arbitrary", "arbitrary")),
    )(dx1, w_out, z_lru, o_dil, om, w_lru, w_dil, w_mem, proj, proj, proj, b_gate, b_gate, b_gate)
    return list(res[:6]) + [r.reshape(1, D_MODEL) for r in res[6:]]


def _adamw_math(w, g, m, v):
    m = ADAM_B1 * m + (1.0 - ADAM_B1) * g
    v = ADAM_B2 * v + (1.0 - ADAM_B2) * (g * g)
    m_hat = m / (1.0 - ADAM_B1 ** ADAM_STEP)
    v_hat = v / (1.0 - ADAM_B2 ** ADAM_STEP)
    delta = -ADAM_LR * (m_hat / (jnp.sqrt(v_hat) + ADAM_EPS) + ADAM_WD * w)
    return delta, m, v


def _adamw_landed(w, own, land, m, v, *, name, col_blk=0, prev=None):
    R = w.shape[0]
    n_parts, C = land.shape[0], land.shape[2]
    br = next(d for d in (256, 464, 128) if R % d == 0)
    tile = pl.BlockSpec((br, C), lambda i: (i, col_blk))
    part = pl.BlockSpec((br, C), lambda i: (i, 0))
    n_prev = 0 if prev is None else 4

    def body(w_ref, o_ref, l_ref, m_ref, v_ref, *rest):
        g_ref, d_ref, nm_ref, nv_ref = rest[n_prev:]
        g = o_ref[...].astype(F32)
        for p in range(n_parts):
            g = g + l_ref[p].astype(F32)
        d, nm, nv = _adamw_math(w_ref[...], g, m_ref[...], v_ref[...])
        g_ref[...] = g
        d_ref[...] = d
        nm_ref[...] = nm
        nv_ref[...] = nv

    return pl.pallas_call(
        body, name=name, grid=(R // br,),
        in_specs=[tile, part, pl.BlockSpec((n_parts, br, C), lambda i: (0, i, 0)), tile, tile]
        + [pl.BlockSpec(memory_space=pl.ANY)] * n_prev,
        out_specs=[tile] * 4, out_shape=[jax.ShapeDtypeStruct(w.shape, F32)] * 4,
        input_output_aliases={5 + j: j for j in range(n_prev)},
        compiler_params=_cparams(("parallel",)),
    )(w, own, land, m, v, *(prev or ()))


def _adamw_plain(w, g, m, v, *, name):
    def body(w_ref, g_ref, m_ref, v_ref, d_ref, nm_ref, nv_ref):
        d, nm, nv = _adamw_math(w_ref[...], g_ref[...], m_ref[...], v_ref[...])
        d_ref[...] = d
        nm_ref[...] = nm
        nv_ref[...] = nv

    return pl.pallas_call(
        body, name=name, out_shape=[jax.ShapeDtypeStruct(w.shape, F32)] * 3, compiler_params=_cparams(),
    )(w, g, m, v)


def _my_pos():
    return lax.axis_index("x"), lax.axis_index("y"), lax.axis_index("c")


def _dev_index(p):
    return 4 * p[0] + 2 * p[1] + p[2]


def _peers(me):
    x, y, c = me
    out = []
    for k in range(1, 8):
        fx, fy, fc = (k >> 2) & 1, (k >> 1) & 1, k & 1
        out.append((k - 1, (1 - x if fx else x, 1 - y if fy else y, 1 - c if fc else c)))
    return out


HBM_SPEC = pl.BlockSpec(memory_space=pltpu.HBM)
SEM_SPEC = pl.BlockSpec(memory_space=pltpu.SEMAPHORE)
DATAFLOW_EFFECT = pltpu.SideEffectType.DATAFLOW_SIDE_EFFECTING


def _gather_refs(src, land, me, peer, k):
    return src, land.at[_dev_index(me)]


def _scatter_refs(src, land, me, peer, k):
    return src.at[_dev_index(peer)], land.at[k]


ALL_RELATIONS = tuple(range(7))
ONE_PER_CHIP = (0, 1, 3, 5)


def _push_start(srcs, land_shapes, refs_of, name, after=(), relations=ALL_RELATIONS):
    n, n_after = len(srcs), len(after)

    def body(*refs):
        ins, lands = refs[:n], refs[n:2 * n]
        send_sems, recv_sems, token = refs[2 * n + n_after], refs[2 * n + n_after + 1], refs[-1]
        me = _my_pos()
        for k, peer in _peers(me):
            if k not in relations:
                continue
            for a in range(n):
                src, dst = refs_of(ins[a], lands[a], me, peer, k)
                pltpu.make_async_remote_copy(src_ref=src, dst_ref=dst, send_sem=send_sems.at[7 * a + k],
                                             recv_sem=recv_sems.at[7 * a + k], device_id=peer, device_id_type=MESH).start()
        token[...] = jnp.zeros_like(token)

    lands = [lax.empty(shp, s.dtype) for shp, s in zip(land_shapes, srcs)]
    hbm = lambda a: pltpu.with_memory_space_constraint(a, pltpu.HBM)
    res = pl.pallas_call(
        body, name=name,
        out_shape=(pltpu.SemaphoreType.DMA((7 * n,)), pltpu.SemaphoreType.DMA((7 * n,)),
                   *[pltpu.HBM(s.shape, s.dtype) for s in srcs], *[pltpu.HBM(l.shape, l.dtype) for l in lands],
                   jax.ShapeDtypeStruct((8, 128), F32)),
        in_specs=[HBM_SPEC] * (2 * n) + [pl.BlockSpec(memory_space=pl.ANY)] * n_after,
        out_specs=(SEM_SPEC, SEM_SPEC, *[HBM_SPEC] * (2 * n), pl.BlockSpec(memory_space=pltpu.VMEM)),
        input_output_aliases={i: 2 + i for i in range(2 * n)},
        compiler_params=pltpu.CompilerParams(has_side_effects=DATAFLOW_EFFECT),
    )(*[hbm(s) for s in srcs], *[hbm(l) for l in lands], *after)
    return dict(sems=(res[0], res[1]), srcs=list(res[2:2 + n]), lands=list(res[2 + n:2 + 2 * n]), token=res[-1], n=n,
                refs_of=refs_of, name=name, relations=relations)


def _push_wait(started, after):
    n, refs_of, relations = started["n"], started["refs_of"], started["relations"]
    after = list(after) if isinstance(after, (list, tuple)) else [after]

    def body(*refs):
        ins, lands = refs[:n], refs[n:2 * n]
        send_sems, recv_sems = refs[2 * n], refs[2 * n + 1]
        me = _my_pos()
        for k, peer in _peers(me):
            if k not in relations:
                continue
            for a in range(n):
                src, dst = refs_of(ins[a], lands[a], me, peer, k)
                cp = pltpu.make_async_remote_copy(src_ref=src, dst_ref=dst, send_sem=send_sems.at[7 * a + k],
                                                  recv_sem=recv_sems.at[7 * a + k], device_id=peer, device_id_type=MESH)
                cp.wait_send()
                cp.wait_recv()

    arrs = started["srcs"] + started["lands"]
    res = pl.pallas_call(
        body, name=started["name"].replace("start", "wait"),
        out_shape=tuple(pltpu.HBM(a.shape, a.dtype) for a in arrs),
        in_specs=[HBM_SPEC] * (2 * n) + [SEM_SPEC, SEM_SPEC] + [pl.BlockSpec(memory_space=pl.ANY)] * len(after),
        out_specs=tuple([HBM_SPEC] * (2 * n)),
        input_output_aliases={i: i for i in range(2 * n)},
        compiler_params=pltpu.CompilerParams(has_side_effects=DATAFLOW_EFFECT),
    )(*arrs, *started["sems"], *after)
    return list(res[n:2 * n])


def _other_chips(x, y):
    return ((1 - x, y), (x, 1 - y), (1 - x, 1 - y))


def _forward_start(land, name, after=()):
    n_after = len(after)

    def body(*refs):
        land_ref, send_sems, recv_sems, token = refs[0], refs[1 + n_after], refs[2 + n_after], refs[-1]
        x, y, c = _my_pos()
        for j, (cx, cy) in enumerate(_other_chips(x, y)):
            blk = land_ref.at[_dev_index((cx, cy, c))]
            pltpu.make_async_remote_copy(src_ref=blk, dst_ref=blk, send_sem=send_sems.at[j], recv_sem=recv_sems.at[j],
                                         device_id=(x, y, 1 - c), device_id_type=MESH).start()
        token[...] = jnp.zeros_like(token)

    res = pl.pallas_call(
        body, name=name,
        out_shape=(pltpu.SemaphoreType.DMA((3,)), pltpu.SemaphoreType.DMA((3,)), pltpu.HBM(land.shape, land.dtype),
                   jax.ShapeDtypeStruct((8, 128), F32)),
        in_specs=[HBM_SPEC] + [pl.BlockSpec(memory_space=pl.ANY)] * n_after,
        out_specs=(SEM_SPEC, SEM_SPEC, HBM_SPEC, pl.BlockSpec(memory_space=pltpu.VMEM)),
        input_output_aliases={0: 2},
        compiler_params=pltpu.CompilerParams(has_side_effects=DATAFLOW_EFFECT),
    )(pltpu.with_memory_space_constraint(land, pltpu.HBM), *after)
    return dict(sems=(res[0], res[1]), land=res[2], token=res[3], name=name)


def _forward_wait(started, after):
    after = list(after) if isinstance(after, (list, tuple)) else [after]

    def body(land_ref, send_sems, recv_sems, *rest):
        x, y, c = _my_pos()
        for j, (cx, cy) in enumerate(_other_chips(x, y)):
            cp = pltpu.make_async_remote_copy(
                src_ref=land_ref.at[_dev_index((cx, cy, c))], dst_ref=land_ref.at[_dev_index((cx, cy, 1 - c))],
                send_sem=send_sems.at[j], recv_sem=recv_sems.at[j], device_id=(x, y, 1 - c), device_id_type=MESH)
            cp.wait_send()
            cp.wait_recv()

    land = started["land"]
    return pl.pallas_call(
        body, name=started["name"].replace("start", "wait"), out_shape=pltpu.HBM(land.shape, land.dtype),
        in_specs=[HBM_SPEC, SEM_SPEC, SEM_SPEC] + [pl.BlockSpec(memory_space=pl.ANY)] * len(after),
        out_specs=HBM_SPEC, input_output_aliases={0: 0},
        compiler_params=pltpu.CompilerParams(has_side_effects=DATAFLOW_EFFECT),
    )(land, *started["sems"], *after)


def _sum_slots(slots):
    def body(in_ref, out_ref):
        acc = in_ref[0]
        for d in range(1, N_DEV):
            acc = acc + in_ref[d]
        out_ref[...] = acc

    return pl.pallas_call(body, name="sum_small", out_shape=jax.ShapeDtypeStruct(slots.shape[1:], F32),
                          compiler_params=_cparams())(slots)


def _adamw_many(ws, gs, ms, vs):
    n = len(ws)

    def body(*refs):
        for i in range(n):
            w_ref, g_ref, m_ref, v_ref = (refs[j * n + i] for j in range(4))
            d_, nm, nv = _adamw_math(w_ref[...], g_ref[...], m_ref[...], v_ref[...])
            for j, val in enumerate((d_, nm, nv)):
                refs[(4 + j) * n + i][...] = val

    res = pl.pallas_call(body, name="adamw_small", out_shape=[jax.ShapeDtypeStruct(w_.shape, F32) for w_ in ws] * 3,
                         compiler_params=_cparams())(*ws, *gs, *ms, *vs)
    return [(res[i], res[n + i], res[2 * n + i]) for i in range(n)]


def _local_step(x, mem, tgt, W, P, late_weights, send_grads, reduce_small, tie0):
    S = x.shape[0]
    W = dict(W)
    h = _rmsnorm_fwd(x, P["g_mix"] + tie0, rows=S, name="norm_mix")
    mem_n = _rmsnorm_fwd(mem, P["g_mem"], rows=N_MEM, name="norm_mem")
    buckets = _dil_buckets()
    bias = _dil_bias(P["rel_bias"], buckets)
    wa_bd, wx_bd = _mx(_block_diag(P["w_rg_a"])), _mx(_block_diag(P["w_rg_x"]))
    W.update(late_weights("first", [h, mem_n, bias, wa_bd, wx_bd]))
    proj = _matmul(h, W["w_in_t"], M=S, N=D_IN, K=D_MODEL, mode="nt", bm=512, bn=D_IN // 2, bk=D_MODEL, name="mm_in",
                   j_outer=True, deps=[W["started"]])

    group_out = [_dilated_fwd(proj, bias, g, S=S) for g in range(len(DIL_GROUPS))]
    o_dil, o_dil_m, lse_dil = _dilated_merge([o for o, _ in group_out], [l for _, l in group_out], S=S)

    W.update(late_weights("branch", [o_dil]))
    lru_args = (W["conv_w"], P["conv_b"].reshape(1, -1), wa_bd, wx_bd, P["b_rg_a"].reshape(1, -1),
                P["b_rg_x"].reshape(1, -1), P["lru_lambda"].reshape(1, -1))
    hl, z_lru, a_lru, mult_lru = _lru_fwd(proj, *lru_args, S=S)
    kv = _matmul(mem_n, W["w_mem_kv"], M=N_MEM, N=2 * MEM_WIDTH, K=D_MODEL, mode="nn", bm=N_MEM, bn=512, bk=D_MODEL,
                 name="mm_kv")
    om, om_m, lse_mem = _mem_attn_fwd(proj, kv, S=S)
    b_gate = P["b_gate"].reshape(1, -1)
    merged = _mix_fwd(z_lru, o_dil_m, om_m, W["w_lru_out"], W["w_dil_out"], W["w_mem_out"], proj, b_gate, S=S)
    g_mlp, g_final, g_mix = (P[n].reshape(1, D_MODEL) for n in ("g_mlp", "g_final", "g_mix"))
    x1, hm = _matmul_rows(merged, W["w_out"], M=S, K=D_MODEL, mode="nn", bm=512, name="mm_out",
                          row_fn=_residual_then_norm, out_dtypes=(F32, MXU_DTYPE), tiles=[x], vecs=[g_mlp])
    W.update(late_weights("mlp", [hm]))

    def relu2(acc):
        rl = jnp.maximum(acc, 0.0)
        return rl * rl, rl

    act, relu_u = _matmul(hm, W["w_mlp_in_t"], M=S, N=D_FF, K=D_MODEL, mode="nt", bm=1024, bn=1024, bk=D_MODEL,
                          name="mm_mlp_in", out_dtypes=(MXU_DTYPE, MXU_DTYPE), epilogue=relu2, j_outer=True)
    dx2, dx2_m, loss, dg_final = _matmul_rows(
        act, W["w_mlp_out"], M=S, K=D_FF, mode="nn", bm=512, name="mm_mlp_out", row_fn=_residual_then_loss,
        out_dtypes=(F32, MXU_DTYPE), tiles=[x1, tgt], vecs=[g_final], acc_widths=(1, D_MODEL))

    G, Gs = {}, {}
    Gs["g_final"] = dg_final
    dw = dict(mode="tn", K=S, bk=S, out_dtypes=(MXU_DTYPE,))
    G["w_mlp_out"] = _matmul(act, dx2_m, M=D_FF, N=D_MODEL, bm=512, bn=D_MODEL, name="mm_dw_mlp_out",
                             parts=("rows", D_FF // N_DEV), **dw)
    du = _matmul(dx2_m, W["w_mlp_out"], M=S, N=D_FF, K=D_MODEL, mode="nt", bm=1024, bn=1024, bk=D_MODEL, name="mm_du",
                 out_dtypes=(MXU_DTYPE,), epilogue=lambda acc, rl: (acc * (2.0 * rl.astype(F32)),),
                 extras=[(relu_u, (0, 0))], j_outer=True)
    G["w_mlp_in"] = _matmul(hm, du, M=D_MODEL, N=D_FF, bm=D_MODEL, bn=512, name="mm_dw_mlp_in",
                            parts=("cols", D_FF // N_DEV), **dw)
    tie1 = send_grads({n: G.pop(n) for n in ("w_mlp_out", "w_mlp_in")})
    dx1, dx1_m, Gs["g_mlp"] = _matmul_rows(
        du, W["w_mlp_in_t"], M=S, K=D_FF, mode="nn", bm=512, name="mm_dhm", row_fn=_norm_bwd_then_residual(2),
        out_dtypes=(F32, MXU_DTYPE), tiles=[x1, dx2], vecs=[g_mlp], acc_widths=(D_MODEL,), deps=[tie1])
    G["w_out"] = _matmul(merged, dx1_m, M=D_MODEL, N=D_MODEL, bm=512, bn=D_MODEL, name="mm_dw_out",
                         parts=("rows", D_MODEL // N_DEV), **dw)
    (dg0, dg1, dg2, dy_lru, dy_dil, dy_mem, db0, db1, db2) = _mix_bwd(
        dx1_m, W["w_out"], z_lru, o_dil_m, om_m, W["w_lru_out"], W["w_dil_out"], W["w_mem_out"], proj, b_gate, S=S)
    Gs["b_gate0"], Gs["b_gate1"], Gs["b_gate2"] = db0, db1, db2

    G["w_mem_out"] = _matmul(om_m, dy_mem, M=MEM_WIDTH, N=D_MODEL, bm=MEM_WIDTH, bn=D_MODEL, name="mm_dw_mem_out",
                             parts=("cols", D_MODEL // N_DEV), **dw)
    dqm, dk_mem, dv_mem = _mem_attn_bwd(proj, kv, om, lse_mem, dy_mem, W["w_mem_out"], S=S)
    dkv = jnp.concatenate([dk_mem, dv_mem], axis=1)
    G["w_mem_kv"] = _matmul(mem_n, dkv, M=D_MODEL, N=2 * MEM_WIDTH, K=N_MEM, mode="tn", bm=D_MODEL, bn=2 * MEM_WIDTH,
                            bk=N_MEM, name="mm_dw_kv", out_dtypes=(MXU_DTYPE,), parts=("rows", D_MODEL // N_DEV))
    dmem_n = _matmul(dkv, W["w_mem_kv"], M=N_MEM, N=D_MODEL, K=2 * MEM_WIDTH, mode="nt", bm=N_MEM, bn=D_MODEL,
                     bk=2 * MEM_WIDTH, name="mm_dmem")
    (Gs["g_mem"],) = _rmsnorm_bwd(mem, P["g_mem"], dmem_n, None, rows=N_MEM, name="norm_mem_bwd", dx_dtypes=())

    G["w_dil_out"] = _matmul(o_dil_m, dy_dil, M=256, N=D_MODEL, bm=256, bn=D_MODEL, name="mm_dw_dil_out",
                             parts=("cols", D_MODEL // N_DEV), **dw)
    do_dil, delta = _matmul(dy_dil, W["w_dil_out"], M=S, N=256, K=D_MODEL, mode="nt", bm=512, bn=256, bk=D_MODEL,
                            name="mm_do_dil", out_dtypes=(F32, F32), epilogue=_with_delta, extras=[(o_dil, (0, 0))])
    G["w_lru_out"] = _matmul(z_lru, dy_lru, M=D_RNN, N=D_MODEL, bm=D_RNN, bn=D_MODEL, name="mm_dw_lru_out",
                             parts=("cols", D_MODEL // N_DEV), **dw)
    tie2 = send_grads({n: G.pop(n) for n in ("w_out", "w_mem_out", "w_mem_kv", "w_dil_out", "w_lru_out")})
    bias = bias + tie2[0, 0]
    dqkv, dbias = None, []
    for g in range(len(DIL_GROUPS)):
        *dqkv, db_g = _dilated_bwd(proj, do_dil, lse_dil, delta, bias, g, S=S, into=dqkv)
        dbias.append(db_g)
    drel = _dil_bias_bwd(jnp.stack(dbias, axis=0).reshape(len(DIL_GROUPS), DIL_HEADS, SPAN, 2 * SPAN), buckets)
    Gs["rel_bias"] = drel

    dxl, dgl, dcw, dcb, dwa, dwx, dba, dbx, dlam = _lru_bwd(proj, hl, a_lru, mult_lru, dy_lru, W["w_lru_out"], *lru_args,
                                                            S=S)
    Gs["conv_w"], Gs["conv_b"] = dcw, dcb
    Gs["w_rg_a"], Gs["w_rg_x"] = _block_diag_extract(dwa), _block_diag_extract(dwx)
    Gs["b_rg_a"], Gs["b_rg_x"], Gs["lru_lambda"] = dba, dbx, dlam
    Gs["loss"] = loss

    dproj = [dxl, dgl] + dqkv + [dqm, dg0, dg1, dg2]
    tie = []
    for q in range(W_IN_PIECES):
        dw_q = None
        for half in range(2):
            dw_q = _dw_in_t_half(h, dproj, q, half, S=S, name=f"mm_dw_in_{q}_{half}", into=dw_q, deps=tie)
        tie = [send_grads({f"w_in_{q}": dw_q})]
    grad_x, Gs["g_mix"] = _matmul_rows(
        dproj, W["w_in_t"], M=S, K=D_IN, mode="nn", bm=256, name="mm_dh", row_fn=_norm_bwd_then_residual(1),
        out_dtypes=(F32,), tiles=[x, dx1], vecs=[g_mix], acc_widths=(D_MODEL,), deps=tie)
    return grad_x, reduce_small(Gs)


BIG = ("w_in", "w_lru_out", "w_dil_out", "w_mem_kv", "w_mem_out", "w_out", "w_mlp_in", "w_mlp_out")
W_IN_PIECES = 2
COL_SHARDED = ("w_lru_out", "w_dil_out", "w_mem_out", "w_mlp_in")
GATHERED_TRANSPOSED = ("w_mlp_in",)
SMALL = ("g_mix", "b_gate", "conv_b", "w_rg_a", "b_rg_a", "w_rg_x", "b_rg_x", "lru_lambda", "rel_bias", "g_mem",
         "g_mlp", "g_final")
WEIGHTS = ("g_mix", "w_in", "b_gate", "conv_w", "conv_b", "w_rg_a", "b_rg_a", "w_rg_x", "b_rg_x", "lru_lambda",
           "w_lru_out", "rel_bias", "w_dil_out", "g_mem", "w_mem_kv", "w_mem_out", "w_out", "g_mlp", "w_mlp_in",
           "w_mlp_out", "g_final")


def _gathered_to_full(name, gathered):
    if name in COL_SHARDED:
        n, r, c = gathered.shape
        return gathered.transpose(1, 0, 2).reshape(r, n * c)
    n, r, c = gathered.shape
    return gathered.reshape(n * r, c)


SMALL_GRADS = (("g_mix", (1, 1024)), ("b_gate0", (1, 1024)), ("b_gate1", (1, 1024)), ("b_gate2", (1, 1024)),
               ("conv_b", (1, 768)), ("w_rg_a", (12, 64, 64)), ("b_rg_a", (1, 768)), ("w_rg_x", (12, 64, 64)),
               ("b_rg_x", (1, 768)), ("lru_lambda", (1, 768)), ("rel_bias", (32, 128)), ("g_mem", (1, 1024)),
               ("g_mlp", (1, 1024)), ("g_final", (1, 1024)), ("conv_w", (4, 768)), ("loss", (1, 1)))


def _pack(parts):
    flat = jnp.concatenate([p.reshape(-1) for p in parts])
    return jnp.pad(flat, (0, (-flat.shape[0]) % 1024)).reshape(-1, 128)


def _unpack(pack, shapes):
    flat = pack.reshape(-1)
    out, off = [], 0
    for shp in shapes:
        size = math.prod(shp)
        out.append(flat[off:off + size].reshape(shp))
        off += size
    return out


def kernel(x, mem, g_mix, w_in, b_gate, conv_w, conv_b, w_rg_a, b_rg_a, w_rg_x, b_rg_x, lru_lambda, w_lru_out, rel_bias, w_dil_out, g_mem, w_mem_kv, w_mem_out, w_out, g_mlp, w_mlp_in, w_mlp_out, g_final, loss_target, m_g_mix, m_w_in, m_b_gate, m_conv_w, m_conv_b, m_w_rg_a, m_b_rg_a, m_w_rg_x, m_b_rg_x, m_lru_lambda, m_w_lru_out, m_rel_bias, m_w_dil_out, m_g_mem, m_w_mem_kv, m_w_mem_out, m_w_out, m_g_mlp, m_w_mlp_in, m_w_mlp_out, m_g_final, v_g_mix, v_w_in, v_b_gate, v_conv_w, v_conv_b, v_w_rg_a, v_b_rg_a, v_w_rg_x, v_b_rg_x, v_lru_lambda, v_w_lru_out, v_rel_bias, v_w_dil_out, v_g_mem, v_w_mem_kv, v_w_mem_out, v_w_out, v_g_mlp, v_w_mlp_in, v_w_mlp_out, v_g_final):
    w = dict(g_mix=g_mix, w_in=w_in, b_gate=b_gate, conv_w=conv_w, conv_b=conv_b, w_rg_a=w_rg_a, b_rg_a=b_rg_a,
             w_rg_x=w_rg_x, b_rg_x=b_rg_x, lru_lambda=lru_lambda, w_lru_out=w_lru_out, rel_bias=rel_bias,
             w_dil_out=w_dil_out, g_mem=g_mem, w_mem_kv=w_mem_kv, w_mem_out=w_mem_out, w_out=w_out, g_mlp=g_mlp,
             w_mlp_in=w_mlp_in, w_mlp_out=w_mlp_out, g_final=g_final)
    m = dict(g_mix=m_g_mix, w_in=m_w_in, b_gate=m_b_gate, conv_w=m_conv_w, conv_b=m_conv_b, w_rg_a=m_w_rg_a,
             b_rg_a=m_b_rg_a, w_rg_x=m_w_rg_x, b_rg_x=m_b_rg_x, lru_lambda=m_lru_lambda, w_lru_out=m_w_lru_out,
             rel_bias=m_rel_bias, w_dil_out=m_w_dil_out, g_mem=m_g_mem, w_mem_kv=m_w_mem_kv, w_mem_out=m_w_mem_out,
             w_out=m_w_out, g_mlp=m_g_mlp, w_mlp_in=m_w_mlp_in, w_mlp_out=m_w_mlp_out, g_final=m_g_final)
    v = dict(g_mix=v_g_mix, w_in=v_w_in, b_gate=v_b_gate, conv_w=v_conv_w, conv_b=v_conv_b, w_rg_a=v_w_rg_a,
             b_rg_a=v_b_rg_a, w_rg_x=v_w_rg_x, b_rg_x=v_b_rg_x, lru_lambda=v_lru_lambda, w_lru_out=v_w_lru_out,
             rel_bias=v_rel_bias, w_dil_out=v_w_dil_out, g_mem=v_g_mem, w_mem_kv=v_w_mem_kv, w_mem_out=v_w_mem_out,
             w_out=v_w_out, g_mlp=v_g_mlp, w_mlp_in=v_w_mlp_in, w_mlp_out=v_w_mlp_out, g_final=v_g_final)

    my_idx = _dev_index(_my_pos())

    w_in_shard = _mx(w["w_in"].T)
    first = _push_start([w_in_shard], [(N_DEV,) + w_in_shard.shape], _gather_refs, "gather_in_start",
                        relations=ONE_PER_CHIP)
    cw_cols = D_RNN // N_DEV
    conv_pad = jnp.zeros((64, D_MODEL), F32).at[:CONV_WIDTH, :cw_cols].set(w["conv_w"])
    late = {}
    P = {n: w[n] for n in SMALL}

    def start_late(order_after):
        for group, names in (("branch", ("w_mem_kv", "w_lru_out", "w_dil_out", "w_mem_out", "w_out", "conv_w")),
                             ("mlp", ("w_mlp_in", "w_mlp_out"))):
            shards = [conv_pad if n == "conv_w" else _mx(w[n].T if n in GATHERED_TRANSPOSED else w[n]) for n in names]
            started = _push_start(shards, [(N_DEV,) + s.shape for s in shards], _gather_refs, f"gather_{group}_start",
                                  after=order_after)
            late[group] = (names, shards, started)
            order_after = [started["token"]]

    def late_weights(group, after):
        if group == "first":
            (land,) = _push_wait(first, after)
            forward = _forward_start(land, "forward_in_start")
            start_late([forward["token"]])
            full = lax.dynamic_update_index_in_dim(_forward_wait(forward, forward["token"]), w_in_shard, my_idx, 0)
            return {"w_in_t": full.reshape(D_IN, D_MODEL), "started": late["mlp"][2]["token"]}
        names, shards, started = late[group]
        out = {}
        for n, land, own in zip(names, _push_wait(started, after), shards):
            full = lax.dynamic_update_index_in_dim(land, own, my_idx, 0)
            if n == "conv_w":
                out[n] = full[:, :CONV_WIDTH, :cw_cols].transpose(1, 0, 2).reshape(CONV_WIDTH, D_RNN)
            elif n in GATHERED_TRANSPOSED:
                out[n + "_t"] = full.reshape(-1, full.shape[2])
            else:
                out[n] = _gathered_to_full(n, full)
        return out

    sent, small = [], {}

    def send_grads(gs):
        names = list(gs)
        parts = [gs[n] for n in names]
        own = [lax.dynamic_index_in_dim(p, my_idx, 0, keepdims=False) for p in parts]
        started = _push_start(parts, [(N_DEV - 1,) + p.shape[1:] for p in parts], _scatter_refs,
                              f"scatter{len(sent)}_start")
        sent.append((names, own, started))
        return started["token"]

    def reduce_small(gs):
        small["pack"] = _pack([gs[n] for n, _ in SMALL_GRADS])
        small["started"] = _push_start([small["pack"]], [(N_DEV,) + small["pack"].shape], _gather_refs, "small_start")
        return small["started"]["token"]

    grad_x, last_token = _local_step(x[0], mem[0], loss_target[0], {}, P, late_weights, send_grads, reduce_small,
                                     first["token"][0, 0])

    grads, deltas, new_m, new_v = {}, {}, {}, {}
    after = last_token
    for names, own, started in sent[:-W_IN_PIECES]:
        for n, o, land in zip(names, own, _push_wait(started, after)):
            grads[n], deltas[n], new_m[n], new_v[n] = _adamw_landed(w[n], o, land, m[n], v[n], name=f"adamw_{n}")
            after = deltas[n]
    prev = None
    for q, (names, own, started) in enumerate(sent[-W_IN_PIECES:]):
        (land,) = _push_wait(started, after)
        prev = _adamw_landed(w["w_in"].T, own[0], land, m["w_in"].T, v["w_in"].T, name=f"adamw_{names[0]}",
                             col_blk=q, prev=prev)
        after = prev[1]
    grads["w_in"], deltas["w_in"], new_m["w_in"], new_v["w_in"] = [t.T for t in prev]
    (small_land,) = _push_wait(small["started"], [after] + [deltas[n] for n in BIG if n != "w_in"])
    total = _sum_slots(lax.dynamic_update_index_in_dim(small_land, small["pack"], my_idx, 0))
    summed = dict(zip([n for n, _ in SMALL_GRADS], _unpack(total, [shp for _, shp in SMALL_GRADS])))
    summed["b_gate"] = jnp.concatenate([summed.pop(f"b_gate{b}") for b in range(3)], axis=1)
    summed["rel_bias"] = summed["rel_bias"][:, :3 * DIL_HEADS]
    for n in SMALL:
        grads[n] = summed[n].reshape(w[n].shape)
    small_updates = _adamw_many([w[n] for n in SMALL], [grads[n] for n in SMALL], [m[n] for n in SMALL],
                                [v[n] for n in SMALL])
    for n, (d_, nm_, nv_) in zip(SMALL, small_updates):
        deltas[n], new_m[n], new_v[n] = d_, nm_, nv_
    conv_w_sum, loss_sum = summed["conv_w"], summed["loss"]
    grads["conv_w"] = lax.dynamic_slice(conv_w_sum, (0, my_idx * cw_cols), (CONV_WIDTH, cw_cols))
    deltas["conv_w"], new_m["conv_w"], new_v["conv_w"] = _adamw_plain(
        w["conv_w"], grads["conv_w"], m["conv_w"], v["conv_w"], name="adamw_conv_w")

    return (loss_sum.reshape(()), grad_x[None], *[grads[n] for n in WEIGHTS], *[deltas[n] for n in WEIGHTS],
            *[new_m[n] for n in WEIGHTS], *[new_v[n] for n in WEIGHTS])
```

```python
import functools
import math

import jax
import jax.numpy as jnp
from jax import lax
from jax.experimental import pallas as pl
from jax.experimental.pallas import tpu as pltpu

F32 = jnp.float32
MXU_DTYPE = jnp.bfloat16
VMEM_LIMIT_BYTES = 56 * 1024 * 1024
N_DEV = 8

D_MODEL = 1024
N_MEM = 256
MEM_HEADS = 4
MEM_HEAD_DIM = 128
MEM_WIDTH = 512
D_RNN = 768
LRU_BLOCK = 64
N_LRU_BLOCKS = 12
LRU_GROUP = 256
N_LRU_GROUPS = 3
CONV_WIDTH = 4
LRU_C = 8.0
DIL_GROUPS = ((128, 1), (512, 4), (2048, 16))
SPAN = 128
DIL_HEADS = 4
DIL_HEAD_DIM = 64
NUM_BUCKETS = 32
MAX_DISTANCE = 2048
D_FF = 4096
D_IN = 7424
EPS = 1e-6
NEG = -1e30
C_XL, C_GATE, C_QKV, C_QM, C_GATES = 0, 768, 1536, 3840, 4352

ADAM_LR = 0.001
ADAM_B1 = 0.9
ADAM_B2 = 0.999
ADAM_EPS = 1e-08
ADAM_WD = 0.01
ADAM_STEP = 10

MESH = pl.DeviceIdType.MESH
GELU_K = math.sqrt(2.0 / math.pi)


def _cparams(sem=None):
    kw = dict(vmem_limit_bytes=VMEM_LIMIT_BYTES)
    if sem is not None:
        kw["dimension_semantics"] = sem
    return pltpu.CompilerParams(**kw)


def _mx(v):
    return v.astype(MXU_DTYPE)


def _dot(a, b, mode="nn"):
    dims = {"nn": (((1,), (0,)), ((), ())), "nt": (((1,), (1,)), ((), ())), "tn": (((0,), (0,)), ((), ()))}[mode]
    return lax.dot_general(_mx(a), _mx(b), dims, preferred_element_type=F32)


def _colsum(v):
    return jnp.sum(v, axis=0, keepdims=True)


def _matmul(a, b, *, M, N, K, mode, bm, bn, bk, name, out_dtypes=(F32,), epilogue=None, extras=(),
            a_off=(0, 0), b_off=(0, 0), j_outer=False, deps=(), parts=None):
    assert M % bm == 0 and N % bn == 0 and K % bk == 0, (name, M, N, K, bm, bn, bk)
    nm, nn, nk = M // bm, N // bn, K // bk

    def ij(f):
        if j_outer:
            return lambda j, i, k: f(i, j, k)
        return f

    if mode == "tn":
        a_spec = pl.BlockSpec((bk, bm), ij(lambda i, j, k: (k + a_off[0], i + a_off[1])))
    else:
        a_spec = pl.BlockSpec((bm, bk), ij(lambda i, j, k: (i + a_off[0], k + a_off[1])))
    if mode == "nt":
        b_spec = pl.BlockSpec((bn, bk), ij(lambda i, j, k: (j + b_off[0], k + b_off[1])))
    else:
        b_spec = pl.BlockSpec((bk, bn), ij(lambda i, j, k: (k + b_off[0], j + b_off[1])))
    ex_specs = [pl.BlockSpec((bm, bn), ij(functools.partial(lambda i, j, k, o: (i + o[0], j + o[1]), o=off)))
                for _, off in extras]
    if parts is None:
        out_dims = (M, N)
        out_spec = pl.BlockSpec((bm, bn), ij(lambda i, j, k: (i, j)))
    elif parts[0] == "rows":
        r = parts[1]
        assert bm % r == 0
        out_dims = (M // r, r, N)
        out_spec = pl.BlockSpec((bm // r, r, bn), ij(lambda i, j, k: (i, 0, j)))
    elif parts[0] == "rows_t":
        r = parts[1]
        assert bn % r == 0
        out_dims = (N // r, r, M)
        out_spec = pl.BlockSpec((bn // r, r, bm), ij(lambda i, j, k: (j, 0, i)))
    else:
        c = parts[1]
        assert bn % c == 0
        out_dims = (N // c, M, c)
        out_spec = pl.BlockSpec((bn // c, bm, c), ij(lambda i, j, k: (j, i, 0)))
    n_ex, n_out, n_dep = len(extras), len(out_dtypes), len(deps)

    def body(*refs):
        a_ref, b_ref = refs[0], refs[1]
        ex = refs[2:2 + n_ex]
        outs = refs[2 + n_ex + n_dep:2 + n_ex + n_dep + n_out]
        part = _dot(a_ref[...], b_ref[...], mode)

        def finish(acc):
            vals = epilogue(acc, *[e[...] for e in ex]) if epilogue is not None else (acc,)
            for o, v in zip(outs, vals):
                if parts is not None and parts[0] == "rows_t":
                    v = v.T
                v = v.astype(o.dtype)
                if parts is None:
                    o[...] = v
                elif parts[0] in ("rows", "rows_t"):
                    for ch in range(v.shape[0] // parts[1]):
                        o[ch] = v[ch * parts[1]:(ch + 1) * parts[1], :]
                else:
                    for ch in range(bn // parts[1]):
                        o[ch] = v[:, ch * parts[1]:(ch + 1) * parts[1]]

        if nk == 1:
            finish(part)
        else:
            acc_ref = refs[-1]
            k = pl.program_id(2)

            @pl.when(k == 0)
            def _():
                acc_ref[...] = part

            @pl.when(k > 0)
            def _():
                acc_ref[...] += part

            @pl.when(k == nk - 1)
            def _():
                finish(acc_ref[...])

    grid = (nn, nm, nk) if j_outer else (nm, nn, nk)
    res = pl.pallas_call(
        body, name=name, grid=grid,
        in_specs=[a_spec, b_spec] + ex_specs + [pl.BlockSpec(memory_space=pl.ANY)] * n_dep,
        out_specs=[out_spec] * n_out,
        out_shape=[jax.ShapeDtypeStruct(out_dims, dt) for dt in out_dtypes],
        scratch_shapes=[pltpu.VMEM((bm, bn), F32)] if nk > 1 else [],
        compiler_params=_cparams(("parallel", "parallel", "arbitrary")),
    )(a, b, *[e for e, _ in extras], *deps)
    return res[0] if n_out == 1 else res


ROW_SUBTILES = 2


def _matmul_rows(a, b, *, M, K, mode, bm, name, row_fn, out_dtypes, tiles=(), vecs=(), acc_widths=(), deps=()):
    N = D_MODEL
    assert M % bm == 0
    segs = list(a) if isinstance(a, (list, tuple)) else [a]
    widths = [s_.shape[1] for s_ in segs]
    assert sum(widths) == K and (len(segs) == 1 or mode == "nn")
    n_s, n_t, n_v, n_o, n_a, n_d = len(segs), len(tiles), len(vecs), len(out_dtypes), len(acc_widths), len(deps)
    row = pl.BlockSpec((bm, N), lambda i: (i, 0))
    b_shape = (K, N) if mode == "nn" else (N, K)

    def body(*refs):
        b_ref = refs[n_s]
        ins = refs[n_s + 1:n_s + 1 + n_t + n_v]
        outs = refs[n_s + 1 + n_t + n_v + n_d:n_s + 1 + n_t + n_v + n_d + n_o]
        accs = refs[n_s + 1 + n_t + n_v + n_d + n_o:]
        for o in accs:
            @pl.when(pl.program_id(0) == 0)
            def _(o=o):
                o[...] = jnp.zeros_like(o)

        for s_ in range(ROW_SUBTILES):
            rows = pl.ds(s_ * (bm // ROW_SUBTILES), bm // ROW_SUBTILES)
            if n_s == 1:
                acc = _dot(refs[0][rows, :], b_ref[...], mode)
            else:
                acc, k0 = None, 0
                for a_ref, w_ in zip(refs[:n_s], widths):
                    part = _dot(a_ref[rows, :], b_ref[k0:k0 + w_, :])
                    acc = part if acc is None else acc + part
                    k0 += w_
            tile_vals, partials = row_fn(acc, *[r[rows, :] for r in ins[:n_t]], *[r[...] for r in ins[n_t:]])
            for o, val in zip(outs, tile_vals):
                o[rows, :] = val.astype(o.dtype)
            for o, val in zip(accs, partials):
                o[...] += val

    res = pl.pallas_call(
        body, name=name, grid=(M // bm,),
        in_specs=[pl.BlockSpec((bm, w_), lambda i: (i, 0)) for w_ in widths] + [pl.BlockSpec(b_shape, lambda i: (0, 0))]
        + [row] * n_t + [pl.BlockSpec((1, N), lambda i: (0, 0))] * n_v + [pl.BlockSpec(memory_space=pl.ANY)] * n_d,
        out_specs=[row] * n_o + [pl.BlockSpec((1, w_), lambda i: (0, 0)) for w_ in acc_widths],
        out_shape=[jax.ShapeDtypeStruct((M, N), dt) for dt in out_dtypes]
        + [jax.ShapeDtypeStruct((1, w_), F32) for w_ in acc_widths],
        compiler_params=_cparams(("arbitrary",) if n_a else ("parallel",)),
    )(*segs, b, *tiles, *vecs, *deps)
    return res


def _dw_in_t_half(h, pieces, q, half, *, S, name, into=None, deps=(), bk=1024):
    half_w, cols = D_IN // 2, D_MODEL // W_IN_PIECES
    lo, hi = half * half_w, (half + 1) * half_w
    use, c0 = [], 0
    for p in pieces:
        w_ = p.shape[1]
        a0, a1 = max(lo, c0), min(hi, c0 + w_)
        if a1 > a0:
            use.append((p, a0 - c0, a1 - a0))
        c0 += w_
    n_p, n_into, n_d, nk = len(use), 0 if into is None else 1, len(deps), S // bk
    rows = D_IN // N_DEV

    def body(*refs):
        h_ref, p_refs = refs[0], refs[1:1 + n_p]
        o_ref, acc_ref = refs[1 + n_p + n_into + n_d], refs[-1]
        k = pl.program_id(0)
        dp = jnp.concatenate([r[:, s0:s0 + w_] for r, (_, s0, w_) in zip(p_refs, use)], axis=1)
        part = _dot(h_ref[...], dp, "tn")

        @pl.when(k == 0)
        def _():
            acc_ref[...] = part

        @pl.when(k > 0)
        def _():
            acc_ref[...] += part

        @pl.when(k == nk - 1)
        def _():
            vt = acc_ref[...].T.astype(o_ref.dtype)
            for ch in range(half_w // rows):
                o_ref[ch] = vt[ch * rows:(ch + 1) * rows, :]

    return pl.pallas_call(
        body, name=name, grid=(nk,),
        in_specs=[pl.BlockSpec((bk, cols), lambda k: (k, q))]
        + [pl.BlockSpec((bk, p.shape[1]), lambda k: (k, 0)) for p, _, _ in use]
        + [pl.BlockSpec(memory_space=pl.ANY)] * (n_into + n_d),
        out_specs=pl.BlockSpec((half_w // rows, rows, cols), lambda k: (half, 0, 0)),
        out_shape=jax.ShapeDtypeStruct((N_DEV, rows, cols), MXU_DTYPE),
        input_output_aliases={1 + n_p: 0} if n_into else {},
        scratch_shapes=[pltpu.VMEM((cols, half_w), F32)],
        compiler_params=_cparams(("arbitrary",)),
    )(h, *[p for p, _, _ in use], *([into] if n_into else []), *deps)


def _rmsnorm_fwd(x, g, *, rows, name, bt=512):
    bt = min(bt, rows)

    def body(x_ref, g_ref, o_ref):
        xv = x_ref[...]
        r = lax.rsqrt(jnp.mean(xv * xv, axis=-1, keepdims=True) + EPS)
        o_ref[...] = (xv * r * g_ref[...]).astype(o_ref.dtype)

    return pl.pallas_call(
        body, name=name, grid=(rows // bt,),
        in_specs=[pl.BlockSpec((bt, D_MODEL), lambda i: (i, 0)), pl.BlockSpec((1, D_MODEL), lambda i: (0, 0))],
        out_specs=pl.BlockSpec((bt, D_MODEL), lambda i: (i, 0)),
        out_shape=jax.ShapeDtypeStruct((rows, D_MODEL), MXU_DTYPE),
        compiler_params=_cparams(("parallel",)),
    )(x, g.reshape(1, D_MODEL))


def _rms_bwd_tile(xv, gv, dyv):
    r = lax.rsqrt(jnp.mean(xv * xv, axis=-1, keepdims=True) + EPS)
    w = dyv * gv
    dx = r * w - xv * (r * r * r) * jnp.mean(w * xv, axis=-1, keepdims=True)
    dg = _colsum(dyv * xv * r)
    return dx, dg


def _residual_then_norm(acc, x_t, g):
    x1 = x_t + acc
    r = lax.rsqrt(jnp.mean(x1 * x1, axis=-1, keepdims=True) + EPS)
    return (x1, x1 * r * g), ()


def _residual_then_loss(acc, x_t, tgt_t, g):
    x2 = x_t + acc
    r = lax.rsqrt(jnp.mean(x2 * x2, axis=-1, keepdims=True) + EPS)
    diff = x2 * r * g - tgt_t
    part = jnp.sum(jnp.mean(diff * diff, axis=-1, keepdims=True), axis=0, keepdims=True) * 0.5
    dx, dg = _rms_bwd_tile(x2, g, diff * (1.0 / D_MODEL))
    return (dx, dx), (part, dg)


def _norm_bwd_then_residual(n_out):
    def fn(acc, x_t, res_t, g):
        dx, dg = _rms_bwd_tile(x_t, g, acc)
        return (dx + res_t,) * n_out, (dg,)

    return fn


def _rmsnorm_bwd(x, g, dy, res, *, rows, name, bt=512, dx_dtypes=(F32,)):
    bt = min(bt, rows)
    has_res = res is not None

    def body(*refs):
        x_ref, g_ref, dy_ref = refs[:3]
        res_ref = refs[3] if has_res else None
        outs = refs[3 + int(has_res):]
        dx, dg = _rms_bwd_tile(x_ref[...], g_ref[...], dy_ref[...])
        if has_res:
            dx = dx + res_ref[...]
        dg_ref = outs[-1]

        @pl.when(pl.program_id(0) == 0)
        def _():
            dg_ref[...] = jnp.zeros_like(dg_ref)

        dg_ref[...] += dg
        for o in outs[:-1]:
            o[...] = dx.astype(o.dtype)

    row_spec = pl.BlockSpec((bt, D_MODEL), lambda i: (i, 0))
    vec_spec = pl.BlockSpec((1, D_MODEL), lambda i: (0, 0))
    ins = [x, g.reshape(1, D_MODEL), dy] + ([res] if has_res else [])
    return pl.pallas_call(
        body, name=name, grid=(rows // bt,),
        in_specs=[row_spec, vec_spec, row_spec] + ([row_spec] if has_res else []),
        out_specs=[row_spec] * len(dx_dtypes) + [vec_spec],
        out_shape=[jax.ShapeDtypeStruct((rows, D_MODEL), dt) for dt in dx_dtypes] + [jax.ShapeDtypeStruct((1, D_MODEL), F32)],
        compiler_params=_cparams(("arbitrary",)),
    )(*ins)


LRU_T = 512
SCAN_GROUPS = 4


def _gelu(x):
    t = jnp.tanh(GELU_K * (x + 0.044715 * x * x * x))
    return 0.5 * x * (1.0 + t), t


def _gelu_grad(x, t):
    return 0.5 * (1.0 + t) + 0.5 * x * (1.0 - t * t) * GELU_K * (1.0 + 3.0 * 0.044715 * x * x)


def _softplus_neg(lam):
    z = -lam
    u = jnp.exp(-jnp.abs(z))
    w = 1.0 + u
    l1p = jnp.where(w == 1.0, u, jnp.log(w) * u / jnp.where(w == 1.0, 1.0, w - 1.0))
    return jnp.maximum(z, 0.0) + l1p


def _shift_down(cur, prev8, k, row8):
    y = pltpu.roll(cur, k, 0)
    head = jnp.where(row8 < k, pltpu.roll(prev8, k, 0), y[0:8])
    return jnp.concatenate([head, y[8:]], axis=0)


def _shift_up(cur, next8, k, row8):
    n = cur.shape[0]
    y = pltpu.roll(cur, n - k, 0)
    tail = jnp.where(row8 >= 8 - k, pltpu.roll(next8, 8 - k, 0), y[n - 8:n])
    return jnp.concatenate([y[0:n - 8], tail], axis=0)


def _lru_gates(xl, p8, cw, cb, wa, wx, ba, bx, lam, row8, a_mult=None):
    sh = [xl] + [_shift_down(xl, p8, k, row8) for k in (1, 2, 3)]
    xc = cb + cw[3:4] * sh[0] + cw[2:3] * sh[1] + cw[1:2] * sh[2] + cw[0:1] * sh[3]
    r = jax.nn.sigmoid(_dot(xc, wa) + ba)
    i = jax.nn.sigmoid(_dot(xc, wx) + bx)
    sp = _softplus_neg(lam)
    if a_mult is None:
        la = -LRU_C * r * sp
        a = jnp.exp(la)
        mult = jnp.sqrt(jnp.tanh(-la) * (a * a + 1.0))
    else:
        a, mult = a_mult
    return dict(sh=sh, xc=xc, r=r, i=i, sp=sp, a=a, mult=mult)


def _lru_specs(n_t, reverse):
    T = LRU_T
    tt = (lambda t: n_t - 1 - t) if reverse else (lambda t: t)
    blk = lambda col0: pl.BlockSpec((T, LRU_GROUP), lambda g, t: (tt(t), col0 + g))
    prev8 = lambda col0: pl.BlockSpec((8, LRU_GROUP), lambda g, t: (jnp.maximum(tt(t) * (T // 8) - 1, 0), col0 + g))
    vec = lambda rows: pl.BlockSpec((rows, LRU_GROUP), lambda g, t: (0, g))
    wbd = pl.BlockSpec((1, LRU_GROUP, LRU_GROUP), lambda g, t: (g, 0, 0))
    return blk, prev8, vec, wbd


def _lru_fwd(proj, conv_w, conv_b, wa_bd, wx_bd, b_a, b_x, lam, *, S):
    T = LRU_T
    n_t = S // T
    blk, _, vec, wbd = _lru_specs(n_t, False)

    def body(xl_ref, gate_ref, cw_ref, cb_ref, wa_ref, wx_ref, ba_ref, bx_ref, lam_ref,
             hl_ref, z_ref, a_s, m_ref, prev8, hcar, b_s):
        @pl.when(pl.program_id(1) == 0)
        def _():
            prev8[...] = jnp.zeros_like(prev8)
            hcar[...] = jnp.zeros_like(hcar)

        row8 = lax.broadcasted_iota(jnp.int32, (8, LRU_GROUP), 0)
        xl = xl_ref[...]
        q = _lru_gates(xl, prev8[...], cw_ref[...], cb_ref[...], wa_ref[0], wx_ref[0], ba_ref[...], bx_ref[...],
                       lam_ref[...], row8)
        prev8[...] = xl[T - 8:T]
        a_s[...] = q["a"]
        m_ref[...] = q["mult"]
        b_s[...] = q["mult"] * q["i"] * q["xc"]

        def step(c, carry):
            local = []
            for u in range(SCAN_GROUPS):
                off = pl.multiple_of((c * SCAN_GROUPS + u) * 8, 8)
                A = a_s[pl.ds(off, 8), :]
                B = b_s[pl.ds(off, 8), :]
                for k in (1, 2, 4):
                    a_sh = jnp.where(row8 >= k, pltpu.roll(A, k, 0), 1.0)
                    b_sh = jnp.where(row8 >= k, pltpu.roll(B, k, 0), 0.0)
                    B = A * b_sh + B
                    A = A * a_sh
                local.append((off, A, B))
            for off, A, B in local:
                h = A * carry + B
                hl_ref[pl.ds(off, 8), :] = h
                carry = h[7:8, :]
            return carry

        hcar[...] = lax.fori_loop(0, T // (8 * SCAN_GROUPS), step, hcar[...])
        ge, _ = _gelu(gate_ref[...])
        z_ref[...] = (ge * hl_ref[...]).astype(z_ref.dtype)

    return pl.pallas_call(
        body, name="lru_fwd", grid=(N_LRU_GROUPS, n_t),
        in_specs=[blk(C_XL // LRU_GROUP), blk(C_GATE // LRU_GROUP), vec(4), vec(1), wbd, wbd, vec(1), vec(1), vec(1)],
        out_specs=[blk(0)] * 4,
        out_shape=[jax.ShapeDtypeStruct((S, D_RNN), F32), jax.ShapeDtypeStruct((S, D_RNN), MXU_DTYPE),
                   jax.ShapeDtypeStruct((S, D_RNN), F32), jax.ShapeDtypeStruct((S, D_RNN), F32)],
        scratch_shapes=[pltpu.VMEM((8, LRU_GROUP), F32), pltpu.VMEM((1, LRU_GROUP), F32), pltpu.VMEM((T, LRU_GROUP), F32)],
        compiler_params=_cparams(("parallel", "arbitrary")),
    )(proj, proj, conv_w, conv_b, wa_bd, wx_bd, b_a, b_x, lam)


def _lru_bwd(proj, hl, a_fwd, mult_fwd, dy, w_out, conv_w, conv_b, wa_bd, wx_bd, b_a, b_x, lam, *, S):
    T = LRU_T
    n_t = S // T
    blk, prev8s, vec, wbd = _lru_specs(n_t, True)

    def body(xl_ref, xlp_ref, gate_ref, hl_ref, hlp_ref, a_ref, m_ref, dy_ref, wo_ref, cw_ref, cb_ref, wa_ref, wx_ref,
             ba_ref, bx_ref, lam_ref, dxl_ref, dgate_ref, dcw_ref, dcb_ref, dwa_ref, dwx_ref, dba_ref, dbx_ref, dlam_ref,
             next8, gcar, c_s, b_s, l_s):
        t = pl.program_id(1)
        first_chunk = t == n_t - 1

        @pl.when(t == 0)
        def _():
            next8[...] = jnp.zeros_like(next8)
            gcar[...] = jnp.zeros_like(gcar)
            for ref in (dcw_ref, dcb_ref, dwa_ref, dwx_ref, dba_ref, dbx_ref, dlam_ref):
                ref[...] = jnp.zeros_like(ref)

        row8 = lax.broadcasted_iota(jnp.int32, (8, LRU_GROUP), 0)
        rowT = lax.broadcasted_iota(jnp.int32, (T, LRU_GROUP), 0)
        keep = jnp.where(first_chunk, 0.0, 1.0)
        xl = xl_ref[...]
        wa, wx, lam_v = wa_ref[0], wx_ref[0], lam_ref[...]
        q = _lru_gates(xl, xlp_ref[...] * keep, cw_ref[...], cb_ref[...], wa, wx, ba_ref[...], bx_ref[...], lam_v, row8,
                       a_mult=(a_ref[...], m_ref[...]))
        a, mult, r, i, xc, sp = q["a"], q["mult"], q["r"], q["i"], q["xc"], q["sp"]
        hl_v = hl_ref[...]
        dz_v = _dot(dy_ref[...], wo_ref[...], "nt")
        gate = gate_ref[...]
        ge, th = _gelu(gate)
        dgate_ref[...] = (dz_v * hl_v * _gelu_grad(gate, th)).astype(dgate_ref.dtype)

        c_s[...] = jnp.where(rowT == T - 1, 0.0, pltpu.roll(a, T - 1, 0))
        b_s[...] = dz_v * ge + jnp.where(rowT == T - 1, gcar[...], 0.0)

        def step(n, carry):
            local = []
            for u in range(SCAN_GROUPS):
                off = pl.multiple_of((T // 8 - 1 - (n * SCAN_GROUPS + u)) * 8, 8)
                C = c_s[pl.ds(off, 8), :]
                B = b_s[pl.ds(off, 8), :]
                for k in (1, 2, 4):
                    c_sh = jnp.where(row8 < 8 - k, pltpu.roll(C, 8 - k, 0), 1.0)
                    b_sh = jnp.where(row8 < 8 - k, pltpu.roll(B, 8 - k, 0), 0.0)
                    B = B + C * b_sh
                    C = C * c_sh
                local.append((off, C, B))
            for off, C, B in local:
                lam_t = B + C * carry
                l_s[pl.ds(off, 8), :] = lam_t
                carry = lam_t[0:1, :]
            return carry

        lax.fori_loop(0, T // (8 * SCAN_GROUPS), step, jnp.zeros((1, LRU_GROUP), F32))
        lmb = l_s[...]
        gcar[...] = a[0:1, :] * lmb[0:1, :]

        h_prev = _shift_down(hl_v, hlp_ref[...] * keep, 1, row8)
        da = lmb * h_prev
        dmult = lmb * i * xc
        di = lmb * mult * xc
        dxc = lmb * mult * i
        dla = da * a - dmult * (a * a) / mult
        dr = dla * (-LRU_C * sp)
        dlam_ref[...] += _colsum(dla * (-LRU_C * r)) * (-jax.nn.sigmoid(-lam_v))
        dpa = dr * r * (1.0 - r)
        dpx = di * i * (1.0 - i)
        dxc = dxc + _dot(dpa, wa, "nt") + _dot(dpx, wx, "nt")
        dwa_ref[0] += _dot(xc, dpa, "tn")
        dwx_ref[0] += _dot(xc, dpx, "tn")
        dba_ref[...] += _colsum(dpa)
        dbx_ref[...] += _colsum(dpx)
        dcb_ref[...] += _colsum(dxc)
        cw = cw_ref[...]
        n8 = next8[...]
        dxl = cw[3:4] * dxc
        for k in (1, 2, 3):
            dxl = dxl + cw[3 - k:4 - k] * _shift_up(dxc, n8, k, row8)
        for k in range(4):
            dcw_ref[3 - k:4 - k, :] += _colsum(dxc * q["sh"][k])
        next8[...] = dxc[0:8]
        dxl_ref[...] = dxl.astype(dxl_ref.dtype)

    res = pl.pallas_call(
        body, name="lru_bwd", grid=(N_LRU_GROUPS, n_t),
        in_specs=[blk(C_XL // LRU_GROUP), prev8s(C_XL // LRU_GROUP), blk(C_GATE // LRU_GROUP), blk(0), prev8s(0), blk(0),
                  blk(0), pl.BlockSpec((T, D_MODEL), lambda g, t: (n_t - 1 - t, 0)),
                  pl.BlockSpec((LRU_GROUP, D_MODEL), lambda g, t: (g, 0)), vec(4), vec(1), wbd, wbd, vec(1), vec(1), vec(1)],
        out_specs=[blk(0), blk(0), vec(4), vec(1), wbd, wbd, vec(1), vec(1), vec(1)],
        out_shape=[jax.ShapeDtypeStruct((S, D_RNN), MXU_DTYPE), jax.ShapeDtypeStruct((S, D_RNN), MXU_DTYPE),
                   jax.ShapeDtypeStruct((4, D_RNN), F32), jax.ShapeDtypeStruct((1, D_RNN), F32),
                   jax.ShapeDtypeStruct((N_LRU_GROUPS, LRU_GROUP, LRU_GROUP), F32),
                   jax.ShapeDtypeStruct((N_LRU_GROUPS, LRU_GROUP, LRU_GROUP), F32),
                   jax.ShapeDtypeStruct((1, D_RNN), F32), jax.ShapeDtypeStruct((1, D_RNN), F32),
                   jax.ShapeDtypeStruct((1, D_RNN), F32)],
        scratch_shapes=[pltpu.VMEM((8, LRU_GROUP), F32), pltpu.VMEM((1, LRU_GROUP), F32),
                        pltpu.VMEM((T, LRU_GROUP), F32), pltpu.VMEM((T, LRU_GROUP), F32), pltpu.VMEM((T, LRU_GROUP), F32)],
        compiler_params=_cparams(("parallel", "arbitrary")),
    )(proj, proj, proj, hl, hl, a_fwd, mult_fwd, dy, w_out, conv_w, conv_b, wa_bd, wx_bd, b_a, b_x, lam)
    return res


def _block_diag(w):
    w4 = w.reshape(N_LRU_GROUPS, 4, LRU_BLOCK, 1, LRU_BLOCK)
    eye = jnp.eye(4, dtype=w.dtype).reshape(1, 4, 1, 4, 1)
    return (w4 * eye).reshape(N_LRU_GROUPS, LRU_GROUP, LRU_GROUP)


def _block_diag_extract(wbd):
    w5 = wbd.reshape(N_LRU_GROUPS, 4, LRU_BLOCK, 4, LRU_BLOCK)
    return jnp.stack([w5[:, a, :, a, :] for a in range(4)], axis=1).reshape(N_LRU_BLOCKS, LRU_BLOCK, LRU_BLOCK)


def _t5_bucket(dist):
    max_exact = NUM_BUCKETS // 2
    df = jnp.maximum(dist, 1).astype(jnp.float32)
    large = max_exact + (jnp.log(df / max_exact) / math.log(MAX_DISTANCE / max_exact)
                         * (NUM_BUCKETS - max_exact)).astype(jnp.int32)
    large = jnp.minimum(large, NUM_BUCKETS - 1)
    return jnp.where(dist < max_exact, dist, large)


def _band_offsets():
    qi = jnp.arange(SPAN)[:, None]
    kj = jnp.arange(2 * SPAN)[None, :]
    return qi + SPAN - kj


def _dil_buckets():
    off = _band_offsets()
    return jnp.stack([_t5_bucket(jnp.maximum(off, 0) * dil) for _, dil in DIL_GROUPS]).astype(jnp.int32)


def _dil_bias(rel_bias, buckets):
    def body(tbl_ref, bk_ref, o_ref):
        g = pl.program_id(0)
        qi = lax.broadcasted_iota(jnp.int32, (SPAN, 2 * SPAN), 0)
        kj = lax.broadcasted_iota(jnp.int32, (SPAN, 2 * SPAN), 1)
        off = qi + SPAN - kj
        valid = (off >= 0) & (off <= SPAN)
        bk = bk_ref[0]
        for h in range(DIL_HEADS):
            acc = jnp.zeros((SPAN, 2 * SPAN), F32)
            for b in range(NUM_BUCKETS):
                acc = jnp.where(bk == b, tbl_ref[b, g * DIL_HEADS + h], acc)
            o_ref[0, h] = jnp.where(valid, acc, NEG)

    return pl.pallas_call(
        body, name="dil_bias", grid=(3,),
        in_specs=[pl.BlockSpec(memory_space=pltpu.SMEM), pl.BlockSpec((1, SPAN, 2 * SPAN), lambda g: (g, 0, 0))],
        out_specs=pl.BlockSpec((1, DIL_HEADS, SPAN, 2 * SPAN), lambda g: (g, 0, 0, 0)),
        out_shape=jax.ShapeDtypeStruct((3, DIL_HEADS, SPAN, 2 * SPAN), F32),
        compiler_params=_cparams(("parallel",)),
    )(rel_bias, buckets)


def _dil_bias_bwd(dbias, buckets):
    def body(db_ref, bk_ref, o_ref):
        lane = lax.broadcasted_iota(jnp.int32, (1, 128), 1)
        rows = [jnp.zeros((1, 128), F32) for _ in range(NUM_BUCKETS)]
        for g in range(3):
            bk = bk_ref[g]
            for h in range(DIL_HEADS):
                d = db_ref[g, h]
                for b in range(NUM_BUCKETS):
                    tot = jnp.sum(_colsum(jnp.where(bk == b, d, 0.0)), axis=1, keepdims=True)
                    rows[b] = jnp.where(lane == g * DIL_HEADS + h, tot, rows[b])
        for b in range(NUM_BUCKETS):
            o_ref[b:b + 1, :] = rows[b]

    return pl.pallas_call(
        body, name="dil_bias_bwd",
        out_shape=jax.ShapeDtypeStruct((NUM_BUCKETS, 128), F32),
        compiler_params=_cparams(),
    )(dbias, buckets)


DIL_SUBBLOCKS = (8, 4, 1)


def _dil_layout(g, S):
    dil, m = DIL_GROUPS[g][1], DIL_SUBBLOCKS[g]
    sub = SPAN * dil
    col = [(C_QKV + t * 768 + g * 256) // 128 for t in range(3)]
    return dil, m, sub, S // (sub * m), col


def _residue_rows(b, r, dil):
    return pl.ds(b * SPAN * dil + r, SPAN, stride=dil) if dil > 1 else pl.ds(b * SPAN, SPAN)


def _for_residues(dil, fn):
    if dil <= 4:
        for r in range(dil):
            fn(r)
    else:
        lax.fori_loop(0, dil, lambda r, c: (fn(r), c)[1], 0, unroll=4)


def _pair_scores(qm, k2, bias, first_cols):
    s = _dot(qm, k2, "nt") * (DIL_HEAD_DIM ** -0.5) + bias
    kj = lax.broadcasted_iota(jnp.int32, s.shape, 1)
    return jnp.where(kj < first_cols, NEG, s)


def _dilated_fwd(proj, bias, g, *, S):
    dil, m, sub, nc, (qc, kc, vc) = _dil_layout(g, S)
    R = sub * m
    cur = lambda cb: pl.BlockSpec((R, 128), lambda p, i: (i, cb + p))
    prv = lambda cb: pl.BlockSpec((sub, 128), lambda p, i: (jnp.maximum(i * m - 1, 0), cb + p))
    out = pl.BlockSpec((R, 128), lambda p, i: (i, p))

    def body(q_ref, kp_ref, kc_ref, vp_ref, vc_ref, b_ref, o_ref, lse_ref):
        lane = lax.broadcasted_iota(jnp.int32, (SPAN, 128), 1)
        sels = (lane < DIL_HEAD_DIM, lane >= DIL_HEAD_DIM)
        for b in range(m):
            first_cols = jnp.where(pl.program_id(1) == 0, SPAN, 0) if b == 0 else 0

            def one(r, b=b, first_cols=first_cols):
                rows = _residue_rows(b, r, dil)
                before = (kc_ref, vc_ref, _residue_rows(b - 1, r, dil)) if b else (kp_ref, vp_ref, _residue_rows(0, r, dil))
                q2 = q_ref[rows, :]
                k2 = _mx(jnp.concatenate([before[0][before[2], :], kc_ref[rows, :]], axis=0))
                v2 = _mx(jnp.concatenate([before[1][before[2], :], vc_ref[rows, :]], axis=0))
                qq = jnp.concatenate([jnp.where(sels[0], q2, 0.0), jnp.where(sels[1], q2, 0.0)], axis=0)
                s = _pair_scores(qq, k2, b_ref[0, 0], first_cols)
                mx = jnp.max(s, axis=-1, keepdims=True)
                p = jnp.exp(s - mx)
                den = jnp.sum(p, axis=-1, keepdims=True)
                o = _dot(p, v2) / den
                st = mx + jnp.log(den)
                o_ref[rows, :] = jnp.where(sels[0], o[0:SPAN], o[SPAN:2 * SPAN])
                lse_ref[rows, :] = jnp.where(lane == 0, st[0:SPAN], jnp.where(lane == 1, st[SPAN:2 * SPAN], 0.0))

            _for_residues(dil, one)

    return pl.pallas_call(
        body, name=f"dil_fwd{g}", grid=(2, nc),
        in_specs=[cur(qc), prv(kc), cur(kc), prv(vc), cur(vc),
                  pl.BlockSpec((1, 1, 2 * SPAN, 2 * SPAN), lambda p, i: (g, p, 0, 0))],
        out_specs=[out, out],
        out_shape=[jax.ShapeDtypeStruct((S, 256), F32), jax.ShapeDtypeStruct((S, 256), F32)],
        compiler_params=_cparams(("parallel", "parallel")),
    )(proj, proj, proj, proj, proj, bias.reshape(3, 2, 2 * SPAN, 2 * SPAN))


def _dilated_bwd(proj, do, lse, delta, bias, g, *, S, into=None):
    dil, m, sub, nc, (qc, kc, vc) = _dil_layout(g, S)
    R = sub * m
    cl = lambda i: jnp.minimum(i, nc - 1)
    cur = lambda cb: pl.BlockSpec((R, 128), lambda p, i: (cl(i), cb + p))
    prv = lambda cb: pl.BlockSpec((sub, 128), lambda p, i: (jnp.maximum(cl(i) * m - 1, 0), cb + p))
    q_out = pl.BlockSpec((R, 128), lambda p, i: (cl(i), 2 * g + p))
    kv_out = pl.BlockSpec((R, 128), lambda p, i: (jnp.maximum(i - 1, 0), 2 * g + p))
    scale = DIL_HEAD_DIM ** -0.5
    n_into = 0 if into is None else 3

    def body(q_ref, kp_ref, kc_ref, vp_ref, vc_ref, do_ref, lse_ref, dl_ref, b_ref, *rest):
        dq_ref, dk_ref, dv_ref, db_ref, dq_s, kc_s, vc_s, kp_s, vp_s, kcar, vcar = rest[n_into:]
        i = pl.program_id(1)

        @pl.when(i == 0)
        def _():
            kcar[...] = jnp.zeros_like(kcar)
            vcar[...] = jnp.zeros_like(vcar)
            db_ref[...] = jnp.zeros_like(db_ref)

        @pl.when(i < nc)
        def _():
            lane = lax.broadcasted_iota(jnp.int32, (SPAN, 128), 1)
            sels = (lane < DIL_HEAD_DIM, lane >= DIL_HEAD_DIM)
            for b in range(m):
                first_cols = jnp.where(i == 0, SPAN, 0) if b == 0 else 0

                def one(r, b=b, first_cols=first_cols):
                    rows = _residue_rows(b, r, dil)
                    rows_before = _residue_rows(b - 1 if b else 0, r, dil)
                    k_before, v_before = (kc_ref, vc_ref) if b else (kp_ref, vp_ref)
                    q2, do2 = q_ref[rows, :], do_ref[rows, :]
                    k2 = _mx(jnp.concatenate([k_before[rows_before, :], kc_ref[rows, :]], axis=0))
                    v2 = _mx(jnp.concatenate([v_before[rows_before, :], vc_ref[rows, :]], axis=0))
                    lse_t, dl_t = lse_ref[rows, :], dl_ref[rows, :]
                    qq = _mx(jnp.concatenate([jnp.where(sels[0], q2, 0.0), jnp.where(sels[1], q2, 0.0)], axis=0))
                    dd = _mx(jnp.concatenate([jnp.where(sels[0], do2, 0.0), jnp.where(sels[1], do2, 0.0)], axis=0))
                    lse2 = jnp.concatenate([lse_t[:, 0:1], lse_t[:, 1:2]], axis=0)
                    dl2 = jnp.concatenate([dl_t[:, 0:1], dl_t[:, 1:2]], axis=0)
                    p = jnp.exp(_pair_scores(qq, k2, b_ref[0, 0], first_cols) - lse2)
                    ds = p * (_dot(dd, v2, "nt") - dl2)
                    db_ref[0] += ds
                    dqq = _dot(ds, k2) * scale
                    dq2 = jnp.where(sels[0], dqq[0:SPAN], dqq[SPAN:2 * SPAN])
                    dk2 = _dot(ds, qq, "tn") * scale
                    dv2 = _dot(p, dd, "tn")
                    dq_s[rows, :] = dq2
                    kc_s[rows, :] = dk2[SPAN:2 * SPAN]
                    vc_s[rows, :] = dv2[SPAN:2 * SPAN]
                    if b:
                        kc_s[rows_before, :] += dk2[0:SPAN]
                        vc_s[rows_before, :] += dv2[0:SPAN]
                    else:
                        kp_s[rows_before, :] = dk2[0:SPAN]
                        vp_s[rows_before, :] = dv2[0:SPAN]

                _for_residues(dil, one)
            dq_ref[...] = dq_s[...].astype(dq_ref.dtype)
            last = pl.ds((m - 1) * sub, sub)
            kcar[last, :] += kp_s[...]
            vcar[last, :] += vp_s[...]
            dk_ref[...] = kcar[...].astype(dk_ref.dtype)
            dv_ref[...] = vcar[...].astype(dv_ref.dtype)
            kcar[...] = kc_s[...]
            vcar[...] = vc_s[...]

        @pl.when(i == nc)
        def _():
            dk_ref[...] = kcar[...].astype(dk_ref.dtype)
            dv_ref[...] = vcar[...].astype(dv_ref.dtype)

    stat = pl.BlockSpec((R, 128), lambda p, i: (cl(i), p))
    big = jax.ShapeDtypeStruct((S, len(DIL_GROUPS) * 256), MXU_DTYPE)
    return pl.pallas_call(
        body, name=f"dil_bwd{g}", grid=(2, nc + 1),
        in_specs=[cur(qc), prv(kc), cur(kc), prv(vc), cur(vc), stat, stat, stat,
                  pl.BlockSpec((1, 1, 2 * SPAN, 2 * SPAN), lambda p, i: (g, p, 0, 0))]
        + [pl.BlockSpec(memory_space=pl.ANY)] * n_into,
        out_specs=[q_out, kv_out, kv_out, pl.BlockSpec((1, 2 * SPAN, 2 * SPAN), lambda p, i: (p, 0, 0))],
        out_shape=[big, big, big, jax.ShapeDtypeStruct((2, 2 * SPAN, 2 * SPAN), F32)],
        input_output_aliases={9 + j: j for j in range(n_into)},
        scratch_shapes=[pltpu.VMEM((R, 128), F32)] * 3 + [pltpu.VMEM((sub, 128), F32)] * 2 + [pltpu.VMEM((R, 128), F32)] * 2,
        compiler_params=_cparams(("parallel", "arbitrary")),
    )(proj, proj, proj, proj, proj, do, lse, delta, bias.reshape(3, 2, 2 * SPAN, 2 * SPAN), *(into or ()))


def _dilated_merge(os_, lses, *, S, bt=512):
    tile = pl.BlockSpec((bt, 128), lambda i, p: (i, p))

    def body(o0, o1, o2, l0, l1, l2, o_ref, om_ref, lse_ref):
        lane = lax.broadcasted_iota(jnp.int32, (bt, 128), 1)
        lo = lane < DIL_HEAD_DIM
        ls = [l0[...], l1[...], l2[...]]
        ws, stat = [], jnp.zeros((bt, 128), F32)
        for e in range(2):
            a = [l[:, e:e + 1] for l in ls]
            m = jnp.maximum(jnp.maximum(a[0], a[1]), a[2])
            ex = [jnp.exp(v - m) for v in a]
            tot = ex[0] + ex[1] + ex[2]
            ws.append([v / tot for v in ex])
            stat = jnp.where(lane == e, m + jnp.log(tot), stat)
        acc = jnp.zeros((bt, 128), F32)
        for gi, o in enumerate((o0, o1, o2)):
            acc = acc + jnp.where(lo, ws[0][gi], ws[1][gi]) * o[...]
        o_ref[...] = acc
        om_ref[...] = _mx(acc)
        lse_ref[...] = stat

    return pl.pallas_call(
        body, name="dil_merge", grid=(S // bt, 2),
        in_specs=[tile] * 6, out_specs=[tile, tile, tile],
        out_shape=[jax.ShapeDtypeStruct((S, 256), F32), jax.ShapeDtypeStruct((S, 256), MXU_DTYPE),
                   jax.ShapeDtypeStruct((S, 256), F32)],
        compiler_params=_cparams(("parallel", "parallel")),
    )(*os_, *lses)


def _with_delta(do, o):
    lane = lax.broadcasted_iota(jnp.int32, (do.shape[0], 128), 1)
    stats = []
    for p in range(2):
        prod = do[:, 128 * p:128 * (p + 1)] * o[:, 128 * p:128 * (p + 1)]
        d0 = jnp.sum(jnp.where(lane < DIL_HEAD_DIM, prod, 0.0), axis=-1, keepdims=True)
        d1 = jnp.sum(jnp.where(lane >= DIL_HEAD_DIM, prod, 0.0), axis=-1, keepdims=True)
        stats.append(jnp.where(lane == 0, d0, jnp.where(lane == 1, d1, 0.0)))
    return do, jnp.concatenate(stats, axis=1)


MEM_T = 2048
QM_BLK = C_QM // MEM_HEAD_DIM


def _mem_attn_fwd(proj, kv, *, S):
    scale = MEM_HEAD_DIM ** -0.5

    def body(q_ref, k_ref, v_ref, o_ref, om_ref, lse_ref):
        s = _dot(q_ref[...], k_ref[...], "nt") * scale
        m = jnp.max(s, axis=-1, keepdims=True)
        p = jnp.exp(s - m)
        den = jnp.sum(p, axis=-1, keepdims=True)
        o = _dot(p, v_ref[...]) / den
        o_ref[...] = o
        om_ref[...] = _mx(o)
        lse_ref[0] = m + jnp.log(den)

    return pl.pallas_call(
        body, name="mem_attn_fwd", grid=(S // MEM_T, MEM_HEADS),
        in_specs=[pl.BlockSpec((MEM_T, MEM_HEAD_DIM), lambda i, h: (i, QM_BLK + h)),
                  pl.BlockSpec((N_MEM, MEM_HEAD_DIM), lambda i, h: (0, h)),
                  pl.BlockSpec((N_MEM, MEM_HEAD_DIM), lambda i, h: (0, MEM_HEADS + h))],
        out_specs=[pl.BlockSpec((MEM_T, MEM_HEAD_DIM), lambda i, h: (i, h)),
                   pl.BlockSpec((MEM_T, MEM_HEAD_DIM), lambda i, h: (i, h)),
                   pl.BlockSpec((1, MEM_T, 1), lambda i, h: (h, i, 0))],
        out_shape=[jax.ShapeDtypeStruct((S, MEM_WIDTH), F32), jax.ShapeDtypeStruct((S, MEM_WIDTH), MXU_DTYPE),
                   jax.ShapeDtypeStruct((MEM_HEADS, S, 1), F32)],
        compiler_params=_cparams(("parallel", "parallel")),
    )(proj, kv, kv)


def _mem_attn_bwd(proj, kv, om, lse, dy, w_out, *, S):
    scale = MEM_HEAD_DIM ** -0.5

    def body(q_ref, k_ref, v_ref, o_ref, lse_ref, dy_ref, wo_ref, dq_ref, dk_ref, dv_ref):
        @pl.when(pl.program_id(1) == 0)
        def _():
            dk_ref[...] = jnp.zeros_like(dk_ref)
            dv_ref[...] = jnp.zeros_like(dv_ref)

        qv, kv_, vv, dov = q_ref[...], k_ref[...], v_ref[...], _dot(dy_ref[...], wo_ref[...], "nt")
        p = jnp.exp(_dot(qv, kv_, "nt") * scale - lse_ref[0])
        delta = jnp.sum(dov * o_ref[...], axis=-1, keepdims=True)
        ds = p * (_dot(dov, vv, "nt") - delta)
        dq_ref[...] = (_dot(ds, kv_) * scale).astype(dq_ref.dtype)
        dk_ref[...] += _dot(ds, qv, "tn") * scale
        dv_ref[...] += _dot(p, dov, "tn")

    tile = pl.BlockSpec((MEM_T, MEM_HEAD_DIM), lambda h, i: (i, h))
    kvo = pl.BlockSpec((N_MEM, MEM_HEAD_DIM), lambda h, i: (0, h))
    return pl.pallas_call(
        body, name="mem_attn_bwd", grid=(MEM_HEADS, S // MEM_T),
        in_specs=[pl.BlockSpec((MEM_T, MEM_HEAD_DIM), lambda h, i: (i, QM_BLK + h)),
                  pl.BlockSpec((N_MEM, MEM_HEAD_DIM), lambda h, i: (0, h)),
                  pl.BlockSpec((N_MEM, MEM_HEAD_DIM), lambda h, i: (0, MEM_HEADS + h)),
                  tile, pl.BlockSpec((1, MEM_T, 1), lambda h, i: (h, i, 0)),
                  pl.BlockSpec((MEM_T, D_MODEL), lambda h, i: (i, 0)),
                  pl.BlockSpec((MEM_HEAD_DIM, D_MODEL), lambda h, i: (h, 0))],
        out_specs=[tile, kvo, kvo],
        out_shape=[jax.ShapeDtypeStruct((S, MEM_WIDTH), MXU_DTYPE), jax.ShapeDtypeStruct((N_MEM, MEM_WIDTH), F32),
                   jax.ShapeDtypeStruct((N_MEM, MEM_WIDTH), F32)],
        compiler_params=_cparams(("parallel", "arbitrary")),
    )(proj, kv, kv, om, lse, dy, w_out)


MIX_BM = 1024
MIX_BN = 256
GATES_BLK = C_GATES // MIX_BN


def _mix_specs(j_outer):
    ix = (lambda f: (lambda j, i: f(i, j))) if j_outer else (lambda f: f)
    act = lambda width: pl.BlockSpec((MIX_BM, width), ix(lambda i, j: (i, 0)))
    wgt = lambda width: pl.BlockSpec((width, MIX_BN), ix(lambda i, j: (0, j)))
    gate = lambda b: pl.BlockSpec((MIX_BM, MIX_BN), ix(lambda i, j: (i, GATES_BLK + 4 * b + j)))
    bias = lambda b: pl.BlockSpec((1, MIX_BN), ix(lambda i, j: (0, 4 * b + j)))
    tile = pl.BlockSpec((MIX_BM, MIX_BN), ix(lambda i, j: (i, j)))
    return act, wgt, gate, bias, tile


def _mix_fwd(z_lru, o_dil, om, w_lru, w_dil, w_mem, proj, b_gate, *, S):
    act, wgt, gate, bias, tile = _mix_specs(False)

    def body(zl, od, mo, wl, wd, wm, g0, g1, g2, b0, b1, b2, o_ref, t0, t1, t2):
        acc = None
        for a_ref, w_ref, g_ref, b_ref, t_ref in ((zl, wl, g0, b0, t0), (od, wd, g1, b1, t1), (mo, wm, g2, b2, t2)):
            gt = jax.nn.sigmoid(g_ref[...] + b_ref[...])
            t_ref[...] = gt.astype(t_ref.dtype)
            term = gt * _dot(a_ref[...], w_ref[...])
            acc = term if acc is None else acc + term
        o_ref[...] = acc.astype(o_ref.dtype)

    return pl.pallas_call(
        body, name="mix_fwd", grid=(S // MIX_BM, D_MODEL // MIX_BN),
        in_specs=[act(D_RNN), act(256), act(MEM_WIDTH), wgt(D_RNN), wgt(256), wgt(MEM_WIDTH),
                  gate(0), gate(1), gate(2), bias(0), bias(1), bias(2)],
        out_specs=[tile] * 4, out_shape=[jax.ShapeDtypeStruct((S, D_MODEL), MXU_DTYPE)] * 4,
        compiler_params=_cparams(("parallel", "parallel")),
    )(z_lru, o_dil, om, w_lru, w_dil, w_mem, proj, proj, proj, b_gate, b_gate, b_gate)


def _mix_bwd(dx1, w_out, z_lru, o_dil, om, w_lru, w_dil, w_mem, gates, *, S):
    act, wgt, _, _, tile = _mix_specs(False)
    n_j = D_MODEL // MIX_BN

    def body(dx, wo, zl, od, mo, wl, wd, wm, t0, t1, t2,
             dg0, dg1, dg2, dy0, dy1, dy2, db0, db1, db2):
        j = pl.program_id(1)

        @pl.when((pl.program_id(0) == 0) & (j == 0))
        def _():
            for r in (db0, db1, db2):
                r[...] = jnp.zeros_like(r)

        dmv = _dot(dx[...], wo[...], "nt")
        for act_ref, w_ref, t_ref, dg_ref, dy_ref, db_ref in (
                (zl, wl, t0, dg0, dy0, db0), (od, wd, t1, dg1, dy1, db1), (mo, wm, t2, dg2, dy2, db2)):
            y = _dot(act_ref[...], w_ref[...])
            gt = t_ref[...].astype(F32)
            dgate = dmv * y * gt * (1.0 - gt)
            dg_ref[...] = dgate.astype(dg_ref.dtype)
            dy_ref[...] = (dmv * gt).astype(dy_ref.dtype)
            db_ref[j] += _colsum(dgate)

    big = jax.ShapeDtypeStruct((S, D_MODEL), MXU_DTYPE)
    vec = jax.ShapeDtypeStruct((n_j, 1, MIX_BN), F32)
    vspec = pl.BlockSpec((n_j, 1, MIX_BN), lambda i, j: (0, 0, 0))
    res = pl.pallas_call(
        body, name="mix_bwd", grid=(S // MIX_BM, n_j),
        in_specs=[pl.BlockSpec((MIX_BM, D_MODEL), lambda i, j: (i, 0)), pl.BlockSpec((MIX_BN, D_MODEL), lambda i, j: (j, 0)),
                  act(D_RNN), act(256), act(MEM_WIDTH), wgt(D_RNN), wgt(256), wgt(MEM_WIDTH), tile, tile, tile],
        out_specs=[tile] * 6 + [vspec] * 3, out_shape=[big] * 6 + [vec] * 3,
        compiler_params=_cparams(("arbitrary", "arbitrary")),
    )(dx1, w_out, z_lru, o_dil, om, w_lru, w_dil, w_mem, *gates)
    return list(res[:6]) + [r.reshape(1, D_MODEL) for r in res[6:]]


def _adamw_math(w, g, m, v):
    m = ADAM_B1 * m + (1.0 - ADAM_B1) * g
    v = ADAM_B2 * v + (1.0 - ADAM_B2) * (g * g)
    m_hat = m / (1.0 - ADAM_B1 ** ADAM_STEP)
    v_hat = v / (1.0 - ADAM_B2 ** ADAM_STEP)
    delta = -ADAM_LR * (m_hat / (jnp.sqrt(v_hat) + ADAM_EPS) + ADAM_WD * w)
    return delta, m, v


def _adamw_landed(w, own, land, m, v, *, name, col_blk=0, prev=None):
    R = w.shape[0]
    n_parts, C = land.shape[0], land.shape[2]
    br = next(d for d in (256, 464, 128) if R % d == 0)
    tile = pl.BlockSpec((br, C), lambda i: (i, col_blk))
    part = pl.BlockSpec((br, C), lambda i: (i, 0))
    n_prev = 0 if prev is None else 4

    def body(w_ref, o_ref, l_ref, m_ref, v_ref, *rest):
        g_ref, d_ref, nm_ref, nv_ref = rest[n_prev:]
        g = o_ref[...].astype(F32)
        for p in range(n_parts):
            g = g + l_ref[p].astype(F32)
        d, nm, nv = _adamw_math(w_ref[...], g, m_ref[...], v_ref[...])
        g_ref[...] = g
        d_ref[...] = d
        nm_ref[...] = nm
        nv_ref[...] = nv

    return pl.pallas_call(
        body, name=name, grid=(R // br,),
        in_specs=[tile, part, pl.BlockSpec((n_parts, br, C), lambda i: (0, i, 0)), tile, tile]
        + [pl.BlockSpec(memory_space=pl.ANY)] * n_prev,
        out_specs=[tile] * 4, out_shape=[jax.ShapeDtypeStruct(w.shape, F32)] * 4,
        input_output_aliases={5 + j: j for j in range(n_prev)},
        compiler_params=_cparams(("parallel",)),
    )(w, own, land, m, v, *(prev or ()))


def _adamw_plain(w, g, m, v, *, name):
    def body(w_ref, g_ref, m_ref, v_ref, d_ref, nm_ref, nv_ref):
        d, nm, nv = _adamw_math(w_ref[...], g_ref[...], m_ref[...], v_ref[...])
        d_ref[...] = d
        nm_ref[...] = nm
        nv_ref[...] = nv

    return pl.pallas_call(
        body, name=name, out_shape=[jax.ShapeDtypeStruct(w.shape, F32)] * 3, compiler_params=_cparams(),
    )(w, g, m, v)


def _my_pos():
    return lax.axis_index("x"), lax.axis_index("y"), lax.axis_index("c")


def _dev_index(p):
    return 4 * p[0] + 2 * p[1] + p[2]


def _peers(me):
    x, y, c = me
    out = []
    for k in range(1, 8):
        fx, fy, fc = (k >> 2) & 1, (k >> 1) & 1, k & 1
        out.append((k - 1, (1 - x if fx else x, 1 - y if fy else y, 1 - c if fc else c)))
    return out


HBM_SPEC = pl.BlockSpec(memory_space=pltpu.HBM)
SEM_SPEC = pl.BlockSpec(memory_space=pltpu.SEMAPHORE)
DATAFLOW_EFFECT = pltpu.SideEffectType.DATAFLOW_SIDE_EFFECTING


def _gather_refs(src, land, me, peer, k):
    return src, land.at[_dev_index(me)]


def _scatter_refs(src, land, me, peer, k):
    return src.at[_dev_index(peer)], land.at[k]


ALL_RELATIONS = tuple(range(7))
ONE_PER_CHIP = (0, 1, 3, 5)


def _push_start(srcs, land_shapes, refs_of, name, after=(), relations=ALL_RELATIONS):
    n, n_after = len(srcs), len(after)

    def body(*refs):
        ins, lands = refs[:n], refs[n:2 * n]
        send_sems, recv_sems, token = refs[2 * n + n_after], refs[2 * n + n_after + 1], refs[-1]
        me = _my_pos()
        for k, peer in _peers(me):
            if k not in relations:
                continue
            for a in range(n):
                src, dst = refs_of(ins[a], lands[a], me, peer, k)
                pltpu.make_async_remote_copy(src_ref=src, dst_ref=dst, send_sem=send_sems.at[7 * a + k],
                                             recv_sem=recv_sems.at[7 * a + k], device_id=peer, device_id_type=MESH).start()
        token[...] = jnp.zeros_like(token)

    lands = [lax.empty(shp, s.dtype) for shp, s in zip(land_shapes, srcs)]
    hbm = lambda a: pltpu.with_memory_space_constraint(a, pltpu.HBM)
    res = pl.pallas_call(
        body, name=name,
        out_shape=(pltpu.SemaphoreType.DMA((7 * n,)), pltpu.SemaphoreType.DMA((7 * n,)),
                   *[pltpu.HBM(s.shape, s.dtype) for s in srcs], *[pltpu.HBM(l.shape, l.dtype) for l in lands],
                   jax.ShapeDtypeStruct((8, 128), F32)),
        in_specs=[HBM_SPEC] * (2 * n) + [pl.BlockSpec(memory_space=pl.ANY)] * n_after,
        out_specs=(SEM_SPEC, SEM_SPEC, *[HBM_SPEC] * (2 * n), pl.BlockSpec(memory_space=pltpu.VMEM)),
        input_output_aliases={i: 2 + i for i in range(2 * n)},
        compiler_params=pltpu.CompilerParams(has_side_effects=DATAFLOW_EFFECT),
    )(*[hbm(s) for s in srcs], *[hbm(l) for l in lands], *after)
    return dict(sems=(res[0], res[1]), srcs=list(res[2:2 + n]), lands=list(res[2 + n:2 + 2 * n]), token=res[-1], n=n,
                refs_of=refs_of, name=name, relations=relations)


def _push_wait(started, after):
    n, refs_of, relations = started["n"], started["refs_of"], started["relations"]
    after = list(after) if isinstance(after, (list, tuple)) else [after]

    def body(*refs):
        ins, lands = refs[:n], refs[n:2 * n]
        send_sems, recv_sems = refs[2 * n], refs[2 * n + 1]
        me = _my_pos()
        for k, peer in _peers(me):
            if k not in relations:
                continue
            for a in range(n):
                src, dst = refs_of(ins[a], lands[a], me, peer, k)
                cp = pltpu.make_async_remote_copy(src_ref=src, dst_ref=dst, send_sem=send_sems.at[7 * a + k],
                                                  recv_sem=recv_sems.at[7 * a + k], device_id=peer, device_id_type=MESH)
                cp.wait_send()
                cp.wait_recv()

    arrs = started["srcs"] + started["lands"]
    res = pl.pallas_call(
        body, name=started["name"].replace("start", "wait"),
        out_shape=tuple(pltpu.HBM(a.shape, a.dtype) for a in arrs),
        in_specs=[HBM_SPEC] * (2 * n) + [SEM_SPEC, SEM_SPEC] + [pl.BlockSpec(memory_space=pl.ANY)] * len(after),
        out_specs=tuple([HBM_SPEC] * (2 * n)),
        input_output_aliases={i: i for i in range(2 * n)},
        compiler_params=pltpu.CompilerParams(has_side_effects=DATAFLOW_EFFECT),
    )(*arrs, *started["sems"], *after)
    return list(res[n:2 * n])


def _other_chips(x, y):
    return ((1 - x, y), (x, 1 - y), (1 - x, 1 - y))


def _forward_start(land, name, after=()):
    n_after = len(after)

    def body(*refs):
        land_ref, send_sems, recv_sems, token = refs[0], refs[1 + n_after], refs[2 + n_after], refs[-1]
        x, y, c = _my_pos()
        for j, (cx, cy) in enumerate(_other_chips(x, y)):
            blk = land_ref.at[_dev_index((cx, cy, c))]
            pltpu.make_async_remote_copy(src_ref=blk, dst_ref=blk, send_sem=send_sems.at[j], recv_sem=recv_sems.at[j],
                                         device_id=(x, y, 1 - c), device_id_type=MESH).start()
        token[...] = jnp.zeros_like(token)

    res = pl.pallas_call(
        body, name=name,
        out_shape=(pltpu.SemaphoreType.DMA((3,)), pltpu.SemaphoreType.DMA((3,)), pltpu.HBM(land.shape, land.dtype),
                   jax.ShapeDtypeStruct((8, 128), F32)),
        in_specs=[HBM_SPEC] + [pl.BlockSpec(memory_space=pl.ANY)] * n_after,
        out_specs=(SEM_SPEC, SEM_SPEC, HBM_SPEC, pl.BlockSpec(memory_space=pltpu.VMEM)),
        input_output_aliases={0: 2},
        compiler_params=pltpu.CompilerParams(has_side_effects=DATAFLOW_EFFECT),
    )(pltpu.with_memory_space_constraint(land, pltpu.HBM), *after)
    return dict(sems=(res[0], res[1]), land=res[2], token=res[3], name=name)


def _forward_wait(started, after):
    after = list(after) if isinstance(after, (list, tuple)) else [after]

    def body(land_ref, send_sems, recv_sems, *rest):
        x, y, c = _my_pos()
        for j, (cx, cy) in enumerate(_other_chips(x, y)):
            cp = pltpu.make_async_remote_copy(
                src_ref=land_ref.at[_dev_index((cx, cy, c))], dst_ref=land_ref.at[_dev_index((cx, cy, 1 - c))],
                send_sem=send_sems.at[j], recv_sem=recv_sems.at[j], device_id=(x, y, 1 - c), device_id_type=MESH)
            cp.wait_send()
            cp.wait_recv()

    land = started["land"]
    return pl.pallas_call(
        body, name=started["name"].replace("start", "wait"), out_shape=pltpu.HBM(land.shape, land.dtype),
        in_specs=[HBM_SPEC, SEM_SPEC, SEM_SPEC] + [pl.BlockSpec(memory_space=pl.ANY)] * len(after),
        out_specs=HBM_SPEC, input_output_aliases={0: 0},
        compiler_params=pltpu.CompilerParams(has_side_effects=DATAFLOW_EFFECT),
    )(land, *started["sems"], *after)


def _sum_slots(slots):
    def body(in_ref, out_ref):
        acc = in_ref[0]
        for d in range(1, N_DEV):
            acc = acc + in_ref[d]
        out_ref[...] = acc

    return pl.pallas_call(body, name="sum_small", out_shape=jax.ShapeDtypeStruct(slots.shape[1:], F32),
                          compiler_params=_cparams())(slots)


def _adamw_many(ws, gs, ms, vs):
    n = len(ws)

    def body(*refs):
        for i in range(n):
            w_ref, g_ref, m_ref, v_ref = (refs[j * n + i] for j in range(4))
            d_, nm, nv = _adamw_math(w_ref[...], g_ref[...], m_ref[...], v_ref[...])
            for j, val in enumerate((d_, nm, nv)):
                refs[(4 + j) * n + i][...] = val

    res = pl.pallas_call(body, name="adamw_small", out_shape=[jax.ShapeDtypeStruct(w_.shape, F32) for w_ in ws] * 3,
                         compiler_params=_cparams())(*ws, *gs, *ms, *vs)
    return [(res[i], res[n + i], res[2 * n + i]) for i in range(n)]


def _local_step(x, mem, tgt, W, P, late_weights, send_grads, reduce_small, tie0):
    S = x.shape[0]
    W = dict(W)
    h = _rmsnorm_fwd(x, P["g_mix"] + tie0, rows=S, name="norm_mix")
    mem_n = _rmsnorm_fwd(mem, P["g_mem"], rows=N_MEM, name="norm_mem")
    buckets = _dil_buckets()
    bias = _dil_bias(P["rel_bias"], buckets)
    wa_bd, wx_bd = _mx(_block_diag(P["w_rg_a"])), _mx(_block_diag(P["w_rg_x"]))
    W.update(late_weights("first", [h, mem_n, bias, wa_bd, wx_bd]))
    proj = _matmul(h, W["w_in_t"], M=S, N=D_IN, K=D_MODEL, mode="nt", bm=512, bn=D_IN // 2, bk=D_MODEL, name="mm_in",
                   j_outer=True, deps=[W["started"]])

    group_out = [_dilated_fwd(proj, bias, g, S=S) for g in range(len(DIL_GROUPS))]
    o_dil, o_dil_m, lse_dil = _dilated_merge([o for o, _ in group_out], [l for _, l in group_out], S=S)

    W.update(late_weights("branch", [o_dil]))
    lru_args = (W["conv_w"], P["conv_b"].reshape(1, -1), wa_bd, wx_bd, P["b_rg_a"].reshape(1, -1),
                P["b_rg_x"].reshape(1, -1), P["lru_lambda"].reshape(1, -1))
    hl, z_lru, a_lru, mult_lru = _lru_fwd(proj, *lru_args, S=S)
    kv = _matmul(mem_n, W["w_mem_kv"], M=N_MEM, N=2 * MEM_WIDTH, K=D_MODEL, mode="nn", bm=N_MEM, bn=512, bk=D_MODEL,
                 name="mm_kv")
    om, om_m, lse_mem = _mem_attn_fwd(proj, kv, S=S)
    b_gate = P["b_gate"].reshape(1, -1)
    merged, *gates = _mix_fwd(z_lru, o_dil_m, om_m, W["w_lru_out"], W["w_dil_out"], W["w_mem_out"], proj, b_gate, S=S)
    g_mlp, g_final, g_mix = (P[n].reshape(1, D_MODEL) for n in ("g_mlp", "g_final", "g_mix"))
    x1, hm = _matmul_rows(merged, W["w_out"], M=S, K=D_MODEL, mode="nn", bm=512, name="mm_out",
                          row_fn=_residual_then_norm, out_dtypes=(F32, MXU_DTYPE), tiles=[x], vecs=[g_mlp])
    W.update(late_weights("mlp", [hm]))

    def relu2(acc):
        rl = jnp.maximum(acc, 0.0)
        return rl * rl, rl

    act, relu_u = _matmul(hm, W["w_mlp_in_t"], M=S, N=D_FF, K=D_MODEL, mode="nt", bm=1024, bn=1024, bk=D_MODEL,
                          name="mm_mlp_in", out_dtypes=(MXU_DTYPE, MXU_DTYPE), epilogue=relu2, j_outer=True)
    dx2, dx2_m, loss, dg_final = _matmul_rows(
        act, W["w_mlp_out"], M=S, K=D_FF, mode="nn", bm=512, name="mm_mlp_out", row_fn=_residual_then_loss,
        out_dtypes=(F32, MXU_DTYPE), tiles=[x1, tgt], vecs=[g_final], acc_widths=(1, D_MODEL))

    G, Gs = {}, {}
    Gs["g_final"] = dg_final
    dw = dict(mode="tn", K=S, bk=S, out_dtypes=(MXU_DTYPE,))
    G["w_mlp_out"] = _matmul(act, dx2_m, M=D_FF, N=D_MODEL, bm=512, bn=D_MODEL, name="mm_dw_mlp_out",
                             parts=("rows", D_FF // N_DEV), **dw)
    du = _matmul(dx2_m, W["w_mlp_out"], M=S, N=D_FF, K=D_MODEL, mode="nt", bm=1024, bn=1024, bk=D_MODEL, name="mm_du",
                 out_dtypes=(MXU_DTYPE,), epilogue=lambda acc, rl: (acc * (2.0 * rl.astype(F32)),),
                 extras=[(relu_u, (0, 0))], j_outer=True)
    G["w_mlp_in"] = _matmul(hm, du, M=D_MODEL, N=D_FF, bm=D_MODEL, bn=512, name="mm_dw_mlp_in",
                            parts=("cols", D_FF // N_DEV), **dw)
    tie1 = send_grads({n: G.pop(n) for n in ("w_mlp_out", "w_mlp_in")})
    dx1, dx1_m, Gs["g_mlp"] = _matmul_rows(
        du, W["w_mlp_in_t"], M=S, K=D_FF, mode="nn", bm=512, name="mm_dhm", row_fn=_norm_bwd_then_residual(2),
        out_dtypes=(F32, MXU_DTYPE), tiles=[x1, dx2], vecs=[g_mlp], acc_widths=(D_MODEL,), deps=[tie1])
    G["w_out"] = _matmul(merged, dx1_m, M=D_MODEL, N=D_MODEL, bm=512, bn=D_MODEL, name="mm_dw_out",
                         parts=("rows", D_MODEL // N_DEV), **dw)
    (dg0, dg1, dg2, dy_lru, dy_dil, dy_mem, db0, db1, db2) = _mix_bwd(
        dx1_m, W["w_out"], z_lru, o_dil_m, om_m, W["w_lru_out"], W["w_dil_out"], W["w_mem_out"], gates, S=S)
    Gs["b_gate0"], Gs["b_gate1"], Gs["b_gate2"] = db0, db1, db2

    G["w_mem_out"] = _matmul(om_m, dy_mem, M=MEM_WIDTH, N=D_MODEL, bm=MEM_WIDTH, bn=D_MODEL, name="mm_dw_mem_out",
                             parts=("cols", D_MODEL // N_DEV), **dw)
    dqm, dk_mem, dv_mem = _mem_attn_bwd(proj, kv, om, lse_mem, dy_mem, W["w_mem_out"], S=S)
    dkv = jnp.concatenate([dk_mem, dv_mem], axis=1)
    G["w_mem_kv"] = _matmul(mem_n, dkv, M=D_MODEL, N=2 * MEM_WIDTH, K=N_MEM, mode="tn", bm=D_MODEL, bn=2 * MEM_WIDTH,
                            bk=N_MEM, name="mm_dw_kv", out_dtypes=(MXU_DTYPE,), parts=("rows", D_MODEL // N_DEV))
    dmem_n = _matmul(dkv, W["w_mem_kv"], M=N_MEM, N=D_MODEL, K=2 * MEM_WIDTH, mode="nt", bm=N_MEM, bn=D_MODEL,
                     bk=2 * MEM_WIDTH, name="mm_dmem")
    (Gs["g_mem"],) = _rmsnorm_bwd(mem, P["g_mem"], dmem_n, None, rows=N_MEM, name="norm_mem_bwd", dx_dtypes=())

    G["w_dil_out"] = _matmul(o_dil_m, dy_dil, M=256, N=D_MODEL, bm=256, bn=D_MODEL, name="mm_dw_dil_out",
                             parts=("cols", D_MODEL // N_DEV), **dw)
    do_dil, delta = _matmul(dy_dil, W["w_dil_out"], M=S, N=256, K=D_MODEL, mode="nt", bm=512, bn=256, bk=D_MODEL,
                            name="mm_do_dil", out_dtypes=(F32, F32), epilogue=_with_delta, extras=[(o_dil, (0, 0))])
    G["w_lru_out"] = _matmul(z_lru, dy_lru, M=D_RNN, N=D_MODEL, bm=D_RNN, bn=D_MODEL, name="mm_dw_lru_out",
                             parts=("cols", D_MODEL // N_DEV), **dw)
    tie2 = send_grads({n: G.pop(n) for n in ("w_out", "w_mem_out", "w_mem_kv", "w_dil_out", "w_lru_out")})
    bias = bias + tie2[0, 0]
    dqkv, dbias = None, []
    for g in range(len(DIL_GROUPS)):
        *dqkv, db_g = _dilated_bwd(proj, do_dil, lse_dil, delta, bias, g, S=S, into=dqkv)
        dbias.append(db_g)
    drel = _dil_bias_bwd(jnp.stack(dbias, axis=0).reshape(len(DIL_GROUPS), DIL_HEADS, SPAN, 2 * SPAN), buckets)
    Gs["rel_bias"] = drel

    dxl, dgl, dcw, dcb, dwa, dwx, dba, dbx, dlam = _lru_bwd(proj, hl, a_lru, mult_lru, dy_lru, W["w_lru_out"], *lru_args,
                                                            S=S)
    Gs["conv_w"], Gs["conv_b"] = dcw, dcb
    Gs["w_rg_a"], Gs["w_rg_x"] = _block_diag_extract(dwa), _block_diag_extract(dwx)
    Gs["b_rg_a"], Gs["b_rg_x"], Gs["lru_lambda"] = dba, dbx, dlam
    Gs["loss"] = loss

    dproj = [dxl, dgl] + dqkv + [dqm, dg0, dg1, dg2]
    tie = []
    for q in range(W_IN_PIECES):
        dw_q = None
        for half in range(2):
            dw_q = _dw_in_t_half(h, dproj, q, half, S=S, name=f"mm_dw_in_{q}_{half}", into=dw_q, deps=tie)
        tie = [send_grads({f"w_in_{q}": dw_q})]
    grad_x, Gs["g_mix"] = _matmul_rows(
        dproj, W["w_in_t"], M=S, K=D_IN, mode="nn", bm=256, name="mm_dh", row_fn=_norm_bwd_then_residual(1),
        out_dtypes=(F32,), tiles=[x, dx1], vecs=[g_mix], acc_widths=(D_MODEL,), deps=tie)
    return grad_x, reduce_small(Gs)


BIG = ("w_in", "w_lru_out", "w_dil_out", "w_mem_kv", "w_mem_out", "w_out", "w_mlp_in", "w_mlp_out")
W_IN_PIECES = 2
COL_SHARDED = ("w_lru_out", "w_dil_out", "w_mem_out", "w_mlp_in")
GATHERED_TRANSPOSED = ("w_mlp_in",)
SMALL = ("g_mix", "b_gate", "conv_b", "w_rg_a", "b_rg_a", "w_rg_x", "b_rg_x", "lru_lambda", "rel_bias", "g_mem",
         "g_mlp", "g_final")
WEIGHTS = ("g_mix", "w_in", "b_gate", "conv_w", "conv_b", "w_rg_a", "b_rg_a", "w_rg_x", "b_rg_x", "lru_lambda",
           "w_lru_out", "rel_bias", "w_dil_out", "g_mem", "w_mem_kv", "w_mem_out", "w_out", "g_mlp", "w_mlp_in",
           "w_mlp_out", "g_final")


def _gathered_to_full(name, gathered):
    if name in COL_SHARDED:
        n, r, c = gathered.shape
        return gathered.transpose(1, 0, 2).reshape(r, n * c)
    n, r, c = gathered.shape
    return gathered.reshape(n * r, c)


SMALL_GRADS = (("g_mix", (1, 1024)), ("b_gate0", (1, 1024)), ("b_gate1", (1, 1024)), ("b_gate2", (1, 1024)),
               ("conv_b", (1, 768)), ("w_rg_a", (12, 64, 64)), ("b_rg_a", (1, 768)), ("w_rg_x", (12, 64, 64)),
               ("b_rg_x", (1, 768)), ("lru_lambda", (1, 768)), ("rel_bias", (32, 128)), ("g_mem", (1, 1024)),
               ("g_mlp", (1, 1024)), ("g_final", (1, 1024)), ("conv_w", (4, 768)), ("loss", (1, 1)))


def _pack(parts):
    flat = jnp.concatenate([p.reshape(-1) for p in parts])
    return jnp.pad(flat, (0, (-flat.shape[0]) % 1024)).reshape(-1, 128)


def _unpack(pack, shapes):
    flat = pack.reshape(-1)
    out, off = [], 0
    for shp in shapes:
        size = math.prod(shp)
        out.append(flat[off:off + size].reshape(shp))
        off += size
    return out


def kernel(x, mem, g_mix, w_in, b_gate, conv_w, conv_b, w_rg_a, b_rg_a, w_rg_x, b_rg_x, lru_lambda, w_lru_out, rel_bias, w_dil_out, g_mem, w_mem_kv, w_mem_out, w_out, g_mlp, w_mlp_in, w_mlp_out, g_final, loss_target, m_g_mix, m_w_in, m_b_gate, m_conv_w, m_conv_b, m_w_rg_a, m_b_rg_a, m_w_rg_x, m_b_rg_x, m_lru_lambda, m_w_lru_out, m_rel_bias, m_w_dil_out, m_g_mem, m_w_mem_kv, m_w_mem_out, m_w_out, m_g_mlp, m_w_mlp_in, m_w_mlp_out, m_g_final, v_g_mix, v_w_in, v_b_gate, v_conv_w, v_conv_b, v_w_rg_a, v_b_rg_a, v_w_rg_x, v_b_rg_x, v_lru_lambda, v_w_lru_out, v_rel_bias, v_w_dil_out, v_g_mem, v_w_mem_kv, v_w_mem_out, v_w_out, v_g_mlp, v_w_mlp_in, v_w_mlp_out, v_g_final):
    w = dict(g_mix=g_mix, w_in=w_in, b_gate=b_gate, conv_w=conv_w, conv_b=conv_b, w_rg_a=w_rg_a, b_rg_a=b_rg_a,
             w_rg_x=w_rg_x, b_rg_x=b_rg_x, lru_lambda=lru_lambda, w_lru_out=w_lru_out, rel_bias=rel_bias,
             w_dil_out=w_dil_out, g_mem=g_mem, w_mem_kv=w_mem_kv, w_mem_out=w_mem_out, w_out=w_out, g_mlp=g_mlp,
             w_mlp_in=w_mlp_in, w_mlp_out=w_mlp_out, g_final=g_final)
    m = dict(g_mix=m_g_mix, w_in=m_w_in, b_gate=m_b_gate, conv_w=m_conv_w, conv_b=m_conv_b, w_rg_a=m_w_rg_a,
             b_rg_a=m_b_rg_a, w_rg_x=m_w_rg_x, b_rg_x=m_b_rg_x, lru_lambda=m_lru_lambda, w_lru_out=m_w_lru_out,
             rel_bias=m_rel_bias, w_dil_out=m_w_dil_out, g_mem=m_g_mem, w_mem_kv=m_w_mem_kv, w_mem_out=m_w_mem_out,
             w_out=m_w_out, g_mlp=m_g_mlp, w_mlp_in=m_w_mlp_in, w_mlp_out=m_w_mlp_out, g_final=m_g_final)
    v = dict(g_mix=v_g_mix, w_in=v_w_in, b_gate=v_b_gate, conv_w=v_conv_w, conv_b=v_conv_b, w_rg_a=v_w_rg_a,
             b_rg_a=v_b_rg_a, w_rg_x=v_w_rg_x, b_rg_x=v_b_rg_x, lru_lambda=v_lru_lambda, w_lru_out=v_w_lru_out,
             rel_bias=v_rel_bias, w_dil_out=v_w_dil_out, g_mem=v_g_mem, w_mem_kv=v_w_mem_kv, w_mem_out=v_w_mem_out,
             w_out=v_w_out, g_mlp=v_g_mlp, w_mlp_in=v_w_mlp_in, w_mlp_out=v_w_mlp_out, g_final=v_g_final)

    my_idx = _dev_index(_my_pos())

    w_in_shard = _mx(w["w_in"].T)
    first = _push_start([w_in_shard], [(N_DEV,) + w_in_shard.shape], _gather_refs, "gather_in_start",
                        relations=ONE_PER_CHIP)
    cw_cols = D_RNN // N_DEV
    conv_pad = jnp.zeros((64, D_MODEL), F32).at[:CONV_WIDTH, :cw_cols].set(w["conv_w"])
    late = {}
    P = {n: w[n] for n in SMALL}

    def start_late(order_after):
        for group, names in (("branch", ("w_mem_kv", "w_lru_out", "w_dil_out", "w_mem_out", "w_out", "conv_w")),
                             ("mlp", ("w_mlp_in", "w_mlp_out"))):
            shards = [conv_pad if n == "conv_w" else _mx(w[n].T if n in GATHERED_TRANSPOSED else w[n]) for n in names]
            started = _push_start(shards, [(N_DEV,) + s.shape for s in shards], _gather_refs, f"gather_{group}_start",
                                  after=order_after)
            late[group] = (names, shards, started)
            order_after = [started["token"]]

    def late_weights(group, after):
        if group == "first":
            (land,) = _push_wait(first, after)
            forward = _forward_start(land, "forward_in_start")
            start_late([forward["token"]])
            full = lax.dynamic_update_index_in_dim(_forward_wait(forward, forward["token"]), w_in_shard, my_idx, 0)
            return {"w_in_t": full.reshape(D_IN, D_MODEL), "started": late["mlp"][2]["token"]}
        names, shards, started = late[group]
        out = {}
        for n, land, own in zip(names, _push_wait(started, after), shards):
            full = lax.dynamic_update_index_in_dim(land, own, my_idx, 0)
            if n == "conv_w":
                out[n] = full[:, :CONV_WIDTH, :cw_cols].transpose(1, 0, 2).reshape(CONV_WIDTH, D_RNN)
            elif n in GATHERED_TRANSPOSED:
                out[n + "_t"] = full.reshape(-1, full.shape[2])
            else:
                out[n] = _gathered_to_full(n, full)
        return out

    sent, small = [], {}

    def send_grads(gs):
        names = list(gs)
        parts = [gs[n] for n in names]
        own = [lax.dynamic_index_in_dim(p, my_idx, 0, keepdims=False) for p in parts]
        started = _push_start(parts, [(N_DEV - 1,) + p.shape[1:] for p in parts], _scatter_refs,
                              f"scatter{len(sent)}_start")
        sent.append((names, own, started))
        return started["token"]

    def reduce_small(gs):
        small["pack"] = _pack([gs[n] for n, _ in SMALL_GRADS])
        small["started"] = _push_start([small["pack"]], [(N_DEV,) + small["pack"].shape], _gather_refs, "small_start")
        return small["started"]["token"]

    grad_x, last_token = _local_step(x[0], mem[0], loss_target[0], {}, P, late_weights, send_grads, reduce_small,
                                     first["token"][0, 0])

    grads, deltas, new_m, new_v = {}, {}, {}, {}
    after = last_token
    for names, own, started in sent[:-W_IN_PIECES]:
        for n, o, land in zip(names, own, _push_wait(started, after)):
            grads[n], deltas[n], new_m[n], new_v[n] = _adamw_landed(w[n], o, land, m[n], v[n], name=f"adamw_{n}")
            after = deltas[n]
    prev = None
    for q, (names, own, started) in enumerate(sent[-W_IN_PIECES:]):
        (land,) = _push_wait(started, after)
        prev = _adamw_landed(w["w_in"].T, own[0], land, m["w_in"].T, v["w_in"].T, name=f"adamw_{names[0]}",
                             col_blk=q, prev=prev)
        after = prev[1]
    grads["w_in"], deltas["w_in"], new_m["w_in"], new_v["w_in"] = [t.T for t in prev]
    (small_land,) = _push_wait(small["started"], [after] + [deltas[n] for n in BIG if n != "w_in"])
    total = _sum_slots(lax.dynamic_update_index_in_dim(small_land, small["pack"], my_idx, 0))
    summed = dict(zip([n for n, _ in SMALL_GRADS], _unpack(total, [shp for _, shp in SMALL_GRADS])))
    summed["b_gate"] = jnp.concatenate([summed.pop(f"b_gate{b}") for b in range(3)], axis=1)
    summed["rel_bias"] = summed["rel_bias"][:, :3 * DIL_HEADS]
    for n in SMALL:
        grads[n] = summed[n].reshape(w[n].shape)
    small_updates = _adamw_many([w[n] for n in SMALL], [grads[n] for n in SMALL], [m[n] for n in SMALL],
                                [v[n] for n in SMALL])
    for n, (d_, nm_, nv_) in zip(SMALL, small_updates):
        deltas[n], new_m[n], new_v[n] = d_, nm_, nv_
    conv_w_sum, loss_sum = summed["conv_w"], summed["loss"]
    grads["conv_w"] = lax.dynamic_slice(conv_w_sum, (0, my_idx * cw_cols), (CONV_WIDTH, cw_cols))
    deltas["conv_w"], new_m["conv_w"], new_v["conv_w"] = _adamw_plain(
        w["conv_w"], grads["conv_w"], m["conv_w"], v["conv_w"], name="adamw_conv_w")

    return (loss_sum.reshape(()), grad_x[None], *[grads[n] for n in WEIGHTS], *[deltas[n] for n in WEIGHTS],
            *[new_m[n] for n in WEIGHTS], *[new_v[n] for n in WEIGHTS])
```

```python
import functools
import math

import jax
import jax.numpy as jnp
from jax import lax
from jax.experimental import pallas as pl
from jax.experimental.pallas import tpu as pltpu

F32 = jnp.float32
MXU_DTYPE = jnp.bfloat16
VMEM_LIMIT_BYTES = 56 * 1024 * 1024
N_DEV = 8

D_MODEL = 1024
N_MEM = 256
MEM_HEADS = 4
MEM_HEAD_DIM = 128
MEM_WIDTH = 512
D_RNN = 768
LRU_BLOCK = 64
N_LRU_BLOCKS = 12
LRU_GROUP = 256
N_LRU_GROUPS = 3
CONV_WIDTH = 4
LRU_C = 8.0
DIL_GROUPS = ((128, 1), (512, 4), (2048, 16))
SPAN = 128
DIL_HEADS = 4
DIL_HEAD_DIM = 64
NUM_BUCKETS = 32
MAX_DISTANCE = 2048
D_FF = 4096
D_IN = 7424
EPS = 1e-6
NEG = -1e30
C_XL, C_GATE, C_QKV, C_QM, C_GATES = 0, 768, 1536, 3840, 4352

ADAM_LR = 0.001
ADAM_B1 = 0.9
ADAM_B2 = 0.999
ADAM_EPS = 1e-08
ADAM_WD = 0.01
ADAM_STEP = 10

MESH = pl.DeviceIdType.MESH
GELU_K = math.sqrt(2.0 / math.pi)


def _cparams(sem=None):
    kw = dict(vmem_limit_bytes=VMEM_LIMIT_BYTES)
    if sem is not None:
        kw["dimension_semantics"] = sem
    return pltpu.CompilerParams(**kw)


def _mx(v):
    return v.astype(MXU_DTYPE)


def _dot(a, b, mode="nn"):
    dims = {"nn": (((1,), (0,)), ((), ())), "nt": (((1,), (1,)), ((), ())), "tn": (((0,), (0,)), ((), ()))}[mode]
    return lax.dot_general(_mx(a), _mx(b), dims, preferred_element_type=F32)


def _colsum(v):
    return jnp.sum(v, axis=0, keepdims=True)


def _matmul(a, b, *, M, N, K, mode, bm, bn, bk, name, out_dtypes=(F32,), epilogue=None, extras=(),
            a_off=(0, 0), b_off=(0, 0), j_outer=False, deps=(), parts=None):
    assert M % bm == 0 and N % bn == 0 and K % bk == 0, (name, M, N, K, bm, bn, bk)
    nm, nn, nk = M // bm, N // bn, K // bk

    def ij(f):
        if j_outer:
            return lambda j, i, k: f(i, j, k)
        return f

    if mode == "tn":
        a_spec = pl.BlockSpec((bk, bm), ij(lambda i, j, k: (k + a_off[0], i + a_off[1])))
    else:
        a_spec = pl.BlockSpec((bm, bk), ij(lambda i, j, k: (i + a_off[0], k + a_off[1])))
    if mode == "nt":
        b_spec = pl.BlockSpec((bn, bk), ij(lambda i, j, k: (j + b_off[0], k + b_off[1])))
    else:
        b_spec = pl.BlockSpec((bk, bn), ij(lambda i, j, k: (k + b_off[0], j + b_off[1])))
    ex_specs = [pl.BlockSpec((bm, bn), ij(functools.partial(lambda i, j, k, o: (i + o[0], j + o[1]), o=off)))
                for _, off in extras]
    if parts is None:
        out_dims = (M, N)
        out_spec = pl.BlockSpec((bm, bn), ij(lambda i, j, k: (i, j)))
    elif parts[0] == "rows":
        r = parts[1]
        assert bm % r == 0
        out_dims = (M // r, r, N)
        out_spec = pl.BlockSpec((bm // r, r, bn), ij(lambda i, j, k: (i, 0, j)))
    elif parts[0] == "rows_t":
        r = parts[1]
        assert bn % r == 0
        out_dims = (N // r, r, M)
        out_spec = pl.BlockSpec((bn // r, r, bm), ij(lambda i, j, k: (j, 0, i)))
    else:
        c = parts[1]
        assert bn % c == 0
        out_dims = (N // c, M, c)
        out_spec = pl.BlockSpec((bn // c, bm, c), ij(lambda i, j, k: (j, i, 0)))
    n_ex, n_out, n_dep = len(extras), len(out_dtypes), len(deps)

    def body(*refs):
        a_ref, b_ref = refs[0], refs[1]
        ex = refs[2:2 + n_ex]
        outs = refs[2 + n_ex + n_dep:2 + n_ex + n_dep + n_out]
        part = _dot(a_ref[...], b_ref[...], mode)

        def finish(acc):
            vals = epilogue(acc, *[e[...] for e in ex]) if epilogue is not None else (acc,)
            for o, v in zip(outs, vals):
                if parts is not None and parts[0] == "rows_t":
                    v = v.T
                v = v.astype(o.dtype)
                if parts is None:
                    o[...] = v
                elif parts[0] in ("rows", "rows_t"):
                    for ch in range(v.shape[0] // parts[1]):
                        o[ch] = v[ch * parts[1]:(ch + 1) * parts[1], :]
                else:
                    for ch in range(bn // parts[1]):
                        o[ch] = v[:, ch * parts[1]:(ch + 1) * parts[1]]

        if nk == 1:
            finish(part)
        else:
            acc_ref = refs[-1]
            k = pl.program_id(2)

            @pl.when(k == 0)
            def _():
                acc_ref[...] = part

            @pl.when(k > 0)
            def _():
                acc_ref[...] += part

            @pl.when(k == nk - 1)
            def _():
                finish(acc_ref[...])

    grid = (nn, nm, nk) if j_outer else (nm, nn, nk)
    res = pl.pallas_call(
        body, name=name, grid=grid,
        in_specs=[a_spec, b_spec] + ex_specs + [pl.BlockSpec(memory_space=pl.ANY)] * n_dep,
        out_specs=[out_spec] * n_out,
        out_shape=[jax.ShapeDtypeStruct(out_dims, dt) for dt in out_dtypes],
        scratch_shapes=[pltpu.VMEM((bm, bn), F32)] if nk > 1 else [],
        compiler_params=_cparams(("parallel", "parallel", "arbitrary")),
    )(a, b, *[e for e, _ in extras], *deps)
    return res[0] if n_out == 1 else res


ROW_SUBTILES = 2


def _matmul_rows(a, b, *, M, K, mode, bm, name, row_fn, out_dtypes, tiles=(), vecs=(), acc_widths=(), deps=()):
    N = D_MODEL
    assert M % bm == 0
    segs = list(a) if isinstance(a, (list, tuple)) else [a]
    widths = [s_.shape[1] for s_ in segs]
    assert sum(widths) == K and (len(segs) == 1 or mode == "nn")
    n_s, n_t, n_v, n_o, n_a, n_d = len(segs), len(tiles), len(vecs), len(out_dtypes), len(acc_widths), len(deps)
    row = pl.BlockSpec((bm, N), lambda i: (i, 0))
    b_shape = (K, N) if mode == "nn" else (N, K)

    def body(*refs):
        b_ref = refs[n_s]
        ins = refs[n_s + 1:n_s + 1 + n_t + n_v]
        outs = refs[n_s + 1 + n_t + n_v + n_d:n_s + 1 + n_t + n_v + n_d + n_o]
        accs = refs[n_s + 1 + n_t + n_v + n_d + n_o:]
        for o in accs:
            @pl.when(pl.program_id(0) == 0)
            def _(o=o):
                o[...] = jnp.zeros_like(o)

        for s_ in range(ROW_SUBTILES):
            rows = pl.ds(s_ * (bm // ROW_SUBTILES), bm // ROW_SUBTILES)
            if n_s == 1:
                acc = _dot(refs[0][rows, :], b_ref[...], mode)
            else:
                acc, k0 = None, 0
                for a_ref, w_ in zip(refs[:n_s], widths):
                    part = _dot(a_ref[rows, :], b_ref[k0:k0 + w_, :])
                    acc = part if acc is None else acc + part
                    k0 += w_
            tile_vals, partials = row_fn(acc, *[r[rows, :] for r in ins[:n_t]], *[r[...] for r in ins[n_t:]])
            for o, val in zip(outs, tile_vals):
                o[rows, :] = val.astype(o.dtype)
            for o, val in zip(accs, partials):
                o[...] += val

    res = pl.pallas_call(
        body, name=name, grid=(M // bm,),
        in_specs=[pl.BlockSpec((bm, w_), lambda i: (i, 0)) for w_ in widths] + [pl.BlockSpec(b_shape, lambda i: (0, 0))]
        + [row] * n_t + [pl.BlockSpec((1, N), lambda i: (0, 0))] * n_v + [pl.BlockSpec(memory_space=pl.ANY)] * n_d,
        out_specs=[row] * n_o + [pl.BlockSpec((1, w_), lambda i: (0, 0)) for w_ in acc_widths],
        out_shape=[jax.ShapeDtypeStruct((M, N), dt) for dt in out_dtypes]
        + [jax.ShapeDtypeStruct((1, w_), F32) for w_ in acc_widths],
        compiler_params=_cparams(("arbitrary",) if n_a else ("parallel",)),
    )(*segs, b, *tiles, *vecs, *deps)
    return res


def _dw_in_t_half(h, pieces, q, half, *, S, name, into=None, deps=(), bk=1024):
    half_w, cols = D_IN // 2, D_MODEL // W_IN_PIECES
    lo, hi = half * half_w, (half + 1) * half_w
    use, c0 = [], 0
    for p in pieces:
        w_ = p.shape[1]
        a0, a1 = max(lo, c0), min(hi, c0 + w_)
        if a1 > a0:
            use.append((p, a0 - c0, a1 - a0))
        c0 += w_
    n_p, n_into, n_d, nk = len(use), 0 if into is None else 1, len(deps), S // bk
    rows = D_IN // N_DEV

    def body(*refs):
        h_ref, p_refs = refs[0], refs[1:1 + n_p]
        o_ref, acc_ref = refs[1 + n_p + n_into + n_d], refs[-1]
        k = pl.program_id(0)
        dp = jnp.concatenate([r[:, s0:s0 + w_] for r, (_, s0, w_) in zip(p_refs, use)], axis=1)
        part = _dot(h_ref[...], dp, "tn")

        @pl.when(k == 0)
        def _():
            acc_ref[...] = part

        @pl.when(k > 0)
        def _():
            acc_ref[...] += part

        @pl.when(k == nk - 1)
        def _():
            vt = acc_ref[...].T.astype(o_ref.dtype)
            for ch in range(half_w // rows):
                o_ref[ch] = vt[ch * rows:(ch + 1) * rows, :]

    return pl.pallas_call(
        body, name=name, grid=(nk,),
        in_specs=[pl.BlockSpec((bk, cols), lambda k: (k, q))]
        + [pl.BlockSpec((bk, p.shape[1]), lambda k: (k, 0)) for p, _, _ in use]
        + [pl.BlockSpec(memory_space=pl.ANY)] * (n_into + n_d),
        out_specs=pl.BlockSpec((half_w // rows, rows, cols), lambda k: (half, 0, 0)),
        out_shape=jax.ShapeDtypeStruct((N_DEV, rows, cols), MXU_DTYPE),
        input_output_aliases={1 + n_p: 0} if n_into else {},
        scratch_shapes=[pltpu.VMEM((cols, half_w), F32)],
        compiler_params=_cparams(("arbitrary",)),
    )(h, *[p for p, _, _ in use], *([into] if n_into else []), *deps)


def _rmsnorm_fwd(x, g, *, rows, name, bt=512):
    bt = min(bt, rows)

    def body(x_ref, g_ref, o_ref):
        xv = x_ref[...]
        r = lax.rsqrt(jnp.mean(xv * xv, axis=-1, keepdims=True) + EPS)
        o_ref[...] = (xv * r * g_ref[...]).astype(o_ref.dtype)

    return pl.pallas_call(
        body, name=name, grid=(rows // bt,),
        in_specs=[pl.BlockSpec((bt, D_MODEL), lambda i: (i, 0)), pl.BlockSpec((1, D_MODEL), lambda i: (0, 0))],
        out_specs=pl.BlockSpec((bt, D_MODEL), lambda i: (i, 0)),
        out_shape=jax.ShapeDtypeStruct((rows, D_MODEL), MXU_DTYPE),
        compiler_params=_cparams(("parallel",)),
    )(x, g.reshape(1, D_MODEL))


def _rms_bwd_tile(xv, gv, dyv):
    r = lax.rsqrt(jnp.mean(xv * xv, axis=-1, keepdims=True) + EPS)
    w = dyv * gv
    dx = r * w - xv * (r * r * r) * jnp.mean(w * xv, axis=-1, keepdims=True)
    dg = _colsum(dyv * xv * r)
    return dx, dg


def _residual_then_norm(acc, x_t, g):
    x1 = x_t + acc
    r = lax.rsqrt(jnp.mean(x1 * x1, axis=-1, keepdims=True) + EPS)
    return (x1, x1 * r * g), ()


def _residual_then_loss(acc, x_t, tgt_t, g):
    x2 = x_t + acc
    r = lax.rsqrt(jnp.mean(x2 * x2, axis=-1, keepdims=True) + EPS)
    diff = x2 * r * g - tgt_t
    part = jnp.sum(jnp.mean(diff * diff, axis=-1, keepdims=True), axis=0, keepdims=True) * 0.5
    dx, dg = _rms_bwd_tile(x2, g, diff * (1.0 / D_MODEL))
    return (dx, dx), (part, dg)


def _norm_bwd_then_residual(n_out):
    def fn(acc, x_t, res_t, g):
        dx, dg = _rms_bwd_tile(x_t, g, acc)
        return (dx + res_t,) * n_out, (dg,)

    return fn


def _rmsnorm_bwd(x, g, dy, res, *, rows, name, bt=512, dx_dtypes=(F32,)):
    bt = min(bt, rows)
    has_res = res is not None

    def body(*refs):
        x_ref, g_ref, dy_ref = refs[:3]
        res_ref = refs[3] if has_res else None
        outs = refs[3 + int(has_res):]
        dx, dg = _rms_bwd_tile(x_ref[...], g_ref[...], dy_ref[...])
        if has_res:
            dx = dx + res_ref[...]
        dg_ref = outs[-1]

        @pl.when(pl.program_id(0) == 0)
        def _():
            dg_ref[...] = jnp.zeros_like(dg_ref)

        dg_ref[...] += dg
        for o in outs[:-1]:
            o[...] = dx.astype(o.dtype)

    row_spec = pl.BlockSpec((bt, D_MODEL), lambda i: (i, 0))
    vec_spec = pl.BlockSpec((1, D_MODEL), lambda i: (0, 0))
    ins = [x, g.reshape(1, D_MODEL), dy] + ([res] if has_res else [])
    return pl.pallas_call(
        body, name=name, grid=(rows // bt,),
        in_specs=[row_spec, vec_spec, row_spec] + ([row_spec] if has_res else []),
        out_specs=[row_spec] * len(dx_dtypes) + [vec_spec],
        out_shape=[jax.ShapeDtypeStruct((rows, D_MODEL), dt) for dt in dx_dtypes] + [jax.ShapeDtypeStruct((1, D_MODEL), F32)],
        compiler_params=_cparams(("arbitrary",)),
    )(*ins)


LRU_T = 512
SCAN_GROUPS = 4


def _gelu(x):
    t = jnp.tanh(GELU_K * (x + 0.044715 * x * x * x))
    return 0.5 * x * (1.0 + t), t


def _gelu_grad(x, t):
    return 0.5 * (1.0 + t) + 0.5 * x * (1.0 - t * t) * GELU_K * (1.0 + 3.0 * 0.044715 * x * x)


def _softplus_neg(lam):
    z = -lam
    u = jnp.exp(-jnp.abs(z))
    w = 1.0 + u
    l1p = jnp.where(w == 1.0, u, jnp.log(w) * u / jnp.where(w == 1.0, 1.0, w - 1.0))
    return jnp.maximum(z, 0.0) + l1p


def _shift_down(cur, prev8, k, row8):
    y = pltpu.roll(cur, k, 0)
    head = jnp.where(row8 < k, pltpu.roll(prev8, k, 0), y[0:8])
    return jnp.concatenate([head, y[8:]], axis=0)


def _shift_up(cur, next8, k, row8):
    n = cur.shape[0]
    y = pltpu.roll(cur, n - k, 0)
    tail = jnp.where(row8 >= 8 - k, pltpu.roll(next8, 8 - k, 0), y[n - 8:n])
    return jnp.concatenate([y[0:n - 8], tail], axis=0)


def _lru_gates(xl, p8, cw, cb, wa, wx, ba, bx, lam, row8, a_mult=None):
    sh = [xl] + [_shift_down(xl, p8, k, row8) for k in (1, 2, 3)]
    xc = cb + cw[3:4] * sh[0] + cw[2:3] * sh[1] + cw[1:2] * sh[2] + cw[0:1] * sh[3]
    r = jax.nn.sigmoid(_dot(xc, wa) + ba)
    i = jax.nn.sigmoid(_dot(xc, wx) + bx)
    sp = _softplus_neg(lam)
    if a_mult is None:
        la = -LRU_C * r * sp
        a = jnp.exp(la)
        mult = jnp.sqrt(jnp.tanh(-la) * (a * a + 1.0))
    else:
        a, mult = a_mult
    return dict(sh=sh, xc=xc, r=r, i=i, sp=sp, a=a, mult=mult)


def _lru_specs(n_t, reverse):
    T = LRU_T
    tt = (lambda t: n_t - 1 - t) if reverse else (lambda t: t)
    blk = lambda col0: pl.BlockSpec((T, LRU_GROUP), lambda g, t: (tt(t), col0 + g))
    prev8 = lambda col0: pl.BlockSpec((8, LRU_GROUP), lambda g, t: (jnp.maximum(tt(t) * (T // 8) - 1, 0), col0 + g))
    vec = lambda rows: pl.BlockSpec((rows, LRU_GROUP), lambda g, t: (0, g))
    wbd = pl.BlockSpec((1, LRU_GROUP, LRU_GROUP), lambda g, t: (g, 0, 0))
    return blk, prev8, vec, wbd


def _lru_fwd(proj, conv_w, conv_b, wa_bd, wx_bd, b_a, b_x, lam, *, S):
    T = LRU_T
    n_t = S // T
    blk, _, vec, wbd = _lru_specs(n_t, False)

    def body(xl_ref, gate_ref, cw_ref, cb_ref, wa_ref, wx_ref, ba_ref, bx_ref, lam_ref,
             hl_ref, z_ref, a_s, m_ref, prev8, hcar, b_s):
        @pl.when(pl.program_id(1) == 0)
        def _():
            prev8[...] = jnp.zeros_like(prev8)
            hcar[...] = jnp.zeros_like(hcar)

        row8 = lax.broadcasted_iota(jnp.int32, (8, LRU_GROUP), 0)
        xl = xl_ref[...]
        q = _lru_gates(xl, prev8[...], cw_ref[...], cb_ref[...], wa_ref[0], wx_ref[0], ba_ref[...], bx_ref[...],
                       lam_ref[...], row8)
        prev8[...] = xl[T - 8:T]
        a_s[...] = q["a"]
        m_ref[...] = q["mult"]
        b_s[...] = q["mult"] * q["i"] * q["xc"]

        def step(c, carry):
            local = []
            for u in range(SCAN_GROUPS):
                off = pl.multiple_of((c * SCAN_GROUPS + u) * 8, 8)
                A = a_s[pl.ds(off, 8), :]
                B = b_s[pl.ds(off, 8), :]
                for k in (1, 2, 4):
                    a_sh = jnp.where(row8 >= k, pltpu.roll(A, k, 0), 1.0)
                    b_sh = jnp.where(row8 >= k, pltpu.roll(B, k, 0), 0.0)
                    B = A * b_sh + B
                    A = A * a_sh
                local.append((off, A, B))
            for off, A, B in local:
                h = A * carry + B
                hl_ref[pl.ds(off, 8), :] = h
                carry = h[7:8, :]
            return carry

        hcar[...] = lax.fori_loop(0, T // (8 * SCAN_GROUPS), step, hcar[...])
        ge, _ = _gelu(gate_ref[...])
        z_ref[...] = (ge * hl_ref[...]).astype(z_ref.dtype)

    return pl.pallas_call(
        body, name="lru_fwd", grid=(N_LRU_GROUPS, n_t),
        in_specs=[blk(C_XL // LRU_GROUP), blk(C_GATE // LRU_GROUP), vec(4), vec(1), wbd, wbd, vec(1), vec(1), vec(1)],
        out_specs=[blk(0)] * 4,
        out_shape=[jax.ShapeDtypeStruct((S, D_RNN), F32), jax.ShapeDtypeStruct((S, D_RNN), MXU_DTYPE),
                   jax.ShapeDtypeStruct((S, D_RNN), F32), jax.ShapeDtypeStruct((S, D_RNN), F32)],
        scratch_shapes=[pltpu.VMEM((8, LRU_GROUP), F32), pltpu.VMEM((1, LRU_GROUP), F32), pltpu.VMEM((T, LRU_GROUP), F32)],
        compiler_params=_cparams(("parallel", "arbitrary")),
    )(proj, proj, conv_w, conv_b, wa_bd, wx_bd, b_a, b_x, lam)


def _lru_bwd(proj, hl, a_fwd, mult_fwd, dy, w_out, conv_w, conv_b, wa_bd, wx_bd, b_a, b_x, lam, *, S):
    T = LRU_T
    n_t = S // T
    blk, prev8s, vec, wbd = _lru_specs(n_t, True)

    def body(xl_ref, xlp_ref, gate_ref, hl_ref, hlp_ref, a_ref, m_ref, dy_ref, wo_ref, cw_ref, cb_ref, wa_ref, wx_ref,
             ba_ref, bx_ref, lam_ref, dxl_ref, dgate_ref, dcw_ref, dcb_ref, dwa_ref, dwx_ref, dba_ref, dbx_ref, dlam_ref,
             next8, gcar, c_s, b_s, l_s):
        t = pl.program_id(1)
        first_chunk = t == n_t - 1

        @pl.when(t == 0)
        def _():
            next8[...] = jnp.zeros_like(next8)
            gcar[...] = jnp.zeros_like(gcar)
            for ref in (dcw_ref, dcb_ref, dwa_ref, dwx_ref, dba_ref, dbx_ref, dlam_ref):
                ref[...] = jnp.zeros_like(ref)

        row8 = lax.broadcasted_iota(jnp.int32, (8, LRU_GROUP), 0)
        rowT = lax.broadcasted_iota(jnp.int32, (T, LRU_GROUP), 0)
        keep = jnp.where(first_chunk, 0.0, 1.0)
        xl = xl_ref[...]
        wa, wx, lam_v = wa_ref[0], wx_ref[0], lam_ref[...]
        q = _lru_gates(xl, xlp_ref[...] * keep, cw_ref[...], cb_ref[...], wa, wx, ba_ref[...], bx_ref[...], lam_v, row8,
                       a_mult=(a_ref[...], m_ref[...]))
        a, mult, r, i, xc, sp = q["a"], q["mult"], q["r"], q["i"], q["xc"], q["sp"]
        hl_v = hl_ref[...]
        dz_v = _dot(dy_ref[...], wo_ref[...], "nt")
        gate = gate_ref[...]
        ge, th = _gelu(gate)
        dgate_ref[...] = (dz_v * hl_v * _gelu_grad(gate, th)).astype(dgate_ref.dtype)

        c_s[...] = jnp.where(rowT == T - 1, 0.0, pltpu.roll(a, T - 1, 0))
        b_s[...] = dz_v * ge + jnp.where(rowT == T - 1, gcar[...], 0.0)

        def step(n, carry):
            local = []
            for u in range(SCAN_GROUPS):
                off = pl.multiple_of((T // 8 - 1 - (n * SCAN_GROUPS + u)) * 8, 8)
                C = c_s[pl.ds(off, 8), :]
                B = b_s[pl.ds(off, 8), :]
                for k in (1, 2, 4):
                    c_sh = jnp.where(row8 < 8 - k, pltpu.roll(C, 8 - k, 0), 1.0)
                    b_sh = jnp.where(row8 < 8 - k, pltpu.roll(B, 8 - k, 0), 0.0)
                    B = B + C * b_sh
                    C = C * c_sh
                local.append((off, C, B))
            for off, C, B in local:
                lam_t = B + C * carry
                l_s[pl.ds(off, 8), :] = lam_t
                carry = lam_t[0:1, :]
            return carry

        lax.fori_loop(0, T // (8 * SCAN_GROUPS), step, jnp.zeros((1, LRU_GROUP), F32))
        lmb = l_s[...]
        gcar[...] = a[0:1, :] * lmb[0:1, :]

        h_prev = _shift_down(hl_v, hlp_ref[...] * keep, 1, row8)
        da = lmb * h_prev
        dmult = lmb * i * xc
        di = lmb * mult * xc
        dxc = lmb * mult * i
        dla = da * a - dmult * (a * a) / mult
        dr = dla * (-LRU_C * sp)
        dlam_ref[...] += _colsum(dla * (-LRU_C * r)) * (-jax.nn.sigmoid(-lam_v))
        dpa = dr * r * (1.0 - r)
        dpx = di * i * (1.0 - i)
        dxc = dxc + _dot(dpa, wa, "nt") + _dot(dpx, wx, "nt")
        dwa_ref[0] += _dot(xc, dpa, "tn")
        dwx_ref[0] += _dot(xc, dpx, "tn")
        dba_ref[...] += _colsum(dpa)
        dbx_ref[...] += _colsum(dpx)
        dcb_ref[...] += _colsum(dxc)
        cw = cw_ref[...]
        n8 = next8[...]
        dxl = cw[3:4] * dxc
        for k in (1, 2, 3):
            dxl = dxl + cw[3 - k:4 - k] * _shift_up(dxc, n8, k, row8)
        for k in range(4):
            dcw_ref[3 - k:4 - k, :] += _colsum(dxc * q["sh"][k])
        next8[...] = dxc[0:8]
        dxl_ref[...] = dxl.astype(dxl_ref.dtype)

    res = pl.pallas_call(
        body, name="lru_bwd", grid=(N_LRU_GROUPS, n_t),
        in_specs=[blk(C_XL // LRU_GROUP), prev8s(C_XL // LRU_GROUP), blk(C_GATE // LRU_GROUP), blk(0), prev8s(0), blk(0),
                  blk(0), pl.BlockSpec((T, D_MODEL), lambda g, t: (n_t - 1 - t, 0)),
                  pl.BlockSpec((LRU_GROUP, D_MODEL), lambda g, t: (g, 0)), vec(4), vec(1), wbd, wbd, vec(1), vec(1), vec(1)],
        out_specs=[blk(0), blk(0), vec(4), vec(1), wbd, wbd, vec(1), vec(1), vec(1)],
        out_shape=[jax.ShapeDtypeStruct((S, D_RNN), MXU_DTYPE), jax.ShapeDtypeStruct((S, D_RNN), MXU_DTYPE),
                   jax.ShapeDtypeStruct((4, D_RNN), F32), jax.ShapeDtypeStruct((1, D_RNN), F32),
                   jax.ShapeDtypeStruct((N_LRU_GROUPS, LRU_GROUP, LRU_GROUP), F32),
                   jax.ShapeDtypeStruct((N_LRU_GROUPS, LRU_GROUP, LRU_GROUP), F32),
                   jax.ShapeDtypeStruct((1, D_RNN), F32), jax.ShapeDtypeStruct((1, D_RNN), F32),
                   jax.ShapeDtypeStruct((1, D_RNN), F32)],
        scratch_shapes=[pltpu.VMEM((8, LRU_GROUP), F32), pltpu.VMEM((1, LRU_GROUP), F32),
                        pltpu.VMEM((T, LRU_GROUP), F32), pltpu.VMEM((T, LRU_GROUP), F32), pltpu.VMEM((T, LRU_GROUP), F32)],
        compiler_params=_cparams(("parallel", "arbitrary")),
    )(proj, proj, proj, hl, hl, a_fwd, mult_fwd, dy, w_out, conv_w, conv_b, wa_bd, wx_bd, b_a, b_x, lam)
    return res


def _block_diag(w):
    w4 = w.reshape(N_LRU_GROUPS, 4, LRU_BLOCK, 1, LRU_BLOCK)
    eye = jnp.eye(4, dtype=w.dtype).reshape(1, 4, 1, 4, 1)
    return (w4 * eye).reshape(N_LRU_GROUPS, LRU_GROUP, LRU_GROUP)


def _block_diag_extract(wbd):
    w5 = wbd.reshape(N_LRU_GROUPS, 4, LRU_BLOCK, 4, LRU_BLOCK)
    return jnp.stack([w5[:, a, :, a, :] for a in range(4)], axis=1).reshape(N_LRU_BLOCKS, LRU_BLOCK, LRU_BLOCK)


def _t5_bucket(dist):
    max_exact = NUM_BUCKETS // 2
    df = jnp.maximum(dist, 1).astype(jnp.float32)
    large = max_exact + (jnp.log(df / max_exact) / math.log(MAX_DISTANCE / max_exact)
                         * (NUM_BUCKETS - max_exact)).astype(jnp.int32)
    large = jnp.minimum(large, NUM_BUCKETS - 1)
    return jnp.where(dist < max_exact, dist, large)


def _band_offsets():
    qi = jnp.arange(SPAN)[:, None]
    kj = jnp.arange(2 * SPAN)[None, :]
    return qi + SPAN - kj


def _dil_buckets():
    off = _band_offsets()
    return jnp.stack([_t5_bucket(jnp.maximum(off, 0) * dil) for _, dil in DIL_GROUPS]).astype(jnp.int32)


def _dil_bias(rel_bias, buckets):
    def body(tbl_ref, bk_ref, o_ref):
        g = pl.program_id(0)
        qi = lax.broadcasted_iota(jnp.int32, (SPAN, 2 * SPAN), 0)
        kj = lax.broadcasted_iota(jnp.int32, (SPAN, 2 * SPAN), 1)
        off = qi + SPAN - kj
        valid = (off >= 0) & (off <= SPAN)
        bk = bk_ref[0]
        for h in range(DIL_HEADS):
            acc = jnp.zeros((SPAN, 2 * SPAN), F32)
            for b in range(NUM_BUCKETS):
                acc = jnp.where(bk == b, tbl_ref[b, g * DIL_HEADS + h], acc)
            o_ref[0, h] = jnp.where(valid, acc, NEG)

    return pl.pallas_call(
        body, name="dil_bias", grid=(3,),
        in_specs=[pl.BlockSpec(memory_space=pltpu.SMEM), pl.BlockSpec((1, SPAN, 2 * SPAN), lambda g: (g, 0, 0))],
        out_specs=pl.BlockSpec((1, DIL_HEADS, SPAN, 2 * SPAN), lambda g: (g, 0, 0, 0)),
        out_shape=jax.ShapeDtypeStruct((3, DIL_HEADS, SPAN, 2 * SPAN), F32),
        compiler_params=_cparams(("parallel",)),
    )(rel_bias, buckets)


def _dil_bias_bwd(dbias, buckets):
    def body(db_ref, bk_ref, o_ref):
        lane = lax.broadcasted_iota(jnp.int32, (1, 128), 1)
        rows = [jnp.zeros((1, 128), F32) for _ in range(NUM_BUCKETS)]
        for g in range(3):
            bk = bk_ref[g]
            for h in range(DIL_HEADS):
                d = db_ref[g, h]
                for b in range(NUM_BUCKETS):
                    tot = jnp.sum(_colsum(jnp.where(bk == b, d, 0.0)), axis=1, keepdims=True)
                    rows[b] = jnp.where(lane == g * DIL_HEADS + h, tot, rows[b])
        for b in range(NUM_BUCKETS):
            o_ref[b:b + 1, :] = rows[b]

    return pl.pallas_call(
        body, name="dil_bias_bwd",
        out_shape=jax.ShapeDtypeStruct((NUM_BUCKETS, 128), F32),
        compiler_params=_cparams(),
    )(dbias, buckets)


DIL_SUBBLOCKS = (8, 4, 1)


def _dil_layout(g, S):
    dil, m = DIL_GROUPS[g][1], DIL_SUBBLOCKS[g]
    sub = SPAN * dil
    col = [(C_QKV + t * 768 + g * 256) // 128 for t in range(3)]
    return dil, m, sub, S // (sub * m), col


def _residue_rows(b, r, dil):
    return pl.ds(b * SPAN * dil + r, SPAN, stride=dil) if dil > 1 else pl.ds(b * SPAN, SPAN)


def _for_residues(dil, fn):
    if dil <= 4:
        for r in range(dil):
            fn(r)
    else:
        lax.fori_loop(0, dil, lambda r, c: (fn(r), c)[1], 0, unroll=4)


def _pair_scores(qm, k2, bias, first_cols):
    s = _dot(qm, k2, "nt") * (DIL_HEAD_DIM ** -0.5) + bias
    kj = lax.broadcasted_iota(jnp.int32, s.shape, 1)
    return jnp.where(kj < first_cols, NEG, s)


def _dilated_fwd(proj, bias, g, *, S):
    dil, m, sub, nc, (qc, kc, vc) = _dil_layout(g, S)
    R = sub * m
    cur = lambda cb: pl.BlockSpec((R, 128), lambda p, i: (i, cb + p))
    prv = lambda cb: pl.BlockSpec((sub, 128), lambda p, i: (jnp.maximum(i * m - 1, 0), cb + p))
    out = pl.BlockSpec((R, 128), lambda p, i: (i, p))

    def body(q_ref, kp_ref, kc_ref, vp_ref, vc_ref, b_ref, o_ref, lse_ref):
        lane = lax.broadcasted_iota(jnp.int32, (SPAN, 128), 1)
        sels = (lane < DIL_HEAD_DIM, lane >= DIL_HEAD_DIM)
        for b in range(m):
            first_cols = jnp.where(pl.program_id(1) == 0, SPAN, 0) if b == 0 else 0

            def one(r, b=b, first_cols=first_cols):
                rows = _residue_rows(b, r, dil)
                before = (kc_ref, vc_ref, _residue_rows(b - 1, r, dil)) if b else (kp_ref, vp_ref, _residue_rows(0, r, dil))
                q2 = q_ref[rows, :]
                k2 = _mx(jnp.concatenate([before[0][before[2], :], kc_ref[rows, :]], axis=0))
                v2 = _mx(jnp.concatenate([before[1][before[2], :], vc_ref[rows, :]], axis=0))
                qq = jnp.concatenate([jnp.where(sels[0], q2, 0.0), jnp.where(sels[1], q2, 0.0)], axis=0)
                s = _pair_scores(qq, k2, b_ref[0, 0], first_cols)
                mx = jnp.max(s, axis=-1, keepdims=True)
                p = jnp.exp(s - mx)
                den = jnp.sum(p, axis=-1, keepdims=True)
                o = _dot(p, v2) / den
                st = mx + jnp.log(den)
                o_ref[rows, :] = jnp.where(sels[0], o[0:SPAN], o[SPAN:2 * SPAN])
                lse_ref[rows, :] = jnp.where(lane == 0, st[0:SPAN], jnp.where(lane == 1, st[SPAN:2 * SPAN], 0.0))

            _for_residues(dil, one)

    return pl.pallas_call(
        body, name=f"dil_fwd{g}", grid=(2, nc),
        in_specs=[cur(qc), prv(kc), cur(kc), prv(vc), cur(vc),
                  pl.BlockSpec((1, 1, 2 * SPAN, 2 * SPAN), lambda p, i: (g, p, 0, 0))],
        out_specs=[out, out],
        out_shape=[jax.ShapeDtypeStruct((S, 256), F32), jax.ShapeDtypeStruct((S, 256), F32)],
        compiler_params=_cparams(("parallel", "parallel")),
    )(proj, proj, proj, proj, proj, bias.reshape(3, 2, 2 * SPAN, 2 * SPAN))


def _dilated_bwd(proj, do, lse, delta, bias, g, *, S, into=None):
    dil, m, sub, nc, (qc, kc, vc) = _dil_layout(g, S)
    R = sub * m
    cl = lambda i: jnp.minimum(i, nc - 1)
    cur = lambda cb: pl.BlockSpec((R, 128), lambda p, i: (cl(i), cb + p))
    prv = lambda cb: pl.BlockSpec((sub, 128), lambda p, i: (jnp.maximum(cl(i) * m - 1, 0), cb + p))
    q_out = pl.BlockSpec((R, 128), lambda p, i: (cl(i), 2 * g + p))
    kv_out = pl.BlockSpec((R, 128), lambda p, i: (jnp.maximum(i - 1, 0), 2 * g + p))
    scale = DIL_HEAD_DIM ** -0.5
    n_into = 0 if into is None else 3

    def body(q_ref, kp_ref, kc_ref, vp_ref, vc_ref, do_ref, lse_ref, dl_ref, b_ref, *rest):
        dq_ref, dk_ref, dv_ref, db_ref, dq_s, kc_s, vc_s, kp_s, vp_s, kcar, vcar = rest[n_into:]
        i = pl.program_id(1)

        @pl.when(i == 0)
        def _():
            kcar[...] = jnp.zeros_like(kcar)
            vcar[...] = jnp.zeros_like(vcar)
            db_ref[...] = jnp.zeros_like(db_ref)

        @pl.when(i < nc)
        def _():
            lane = lax.broadcasted_iota(jnp.int32, (SPAN, 128), 1)
            sels = (lane < DIL_HEAD_DIM, lane >= DIL_HEAD_DIM)
            for b in range(m):
                first_cols = jnp.where(i == 0, SPAN, 0) if b == 0 else 0

                def one(r, b=b, first_cols=first_cols):
                    rows = _residue_rows(b, r, dil)
                    rows_before = _residue_rows(b - 1 if b else 0, r, dil)
                    k_before, v_before = (kc_ref, vc_ref) if b else (kp_ref, vp_ref)
                    q2, do2 = q_ref[rows, :], do_ref[rows, :]
                    k2 = _mx(jnp.concatenate([k_before[rows_before, :], kc_ref[rows, :]], axis=0))
                    v2 = _mx(jnp.concatenate([v_before[rows_before, :], vc_ref[rows, :]], axis=0))
                    lse_t, dl_t = lse_ref[rows, :], dl_ref[rows, :]
                    qq = _mx(jnp.concatenate([jnp.where(sels[0], q2, 0.0), jnp.where(sels[1], q2, 0.0)], axis=0))
                    dd = _mx(jnp.concatenate([jnp.where(sels[0], do2, 0.0), jnp.where(sels[1], do2, 0.0)], axis=0))
                    lse2 = jnp.concatenate([lse_t[:, 0:1], lse_t[:, 1:2]], axis=0)
                    dl2 = jnp.concatenate([dl_t[:, 0:1], dl_t[:, 1:2]], axis=0)
                    p = jnp.exp(_pair_scores(qq, k2, b_ref[0, 0], first_cols) - lse2)
                    ds = p * (_dot(dd, v2, "nt") - dl2)
                    db_ref[0] += ds
                    dqq = _dot(ds, k2) * scale
                    dq2 = jnp.where(sels[0], dqq[0:SPAN], dqq[SPAN:2 * SPAN])
                    dk2 = _dot(ds, qq, "tn") * scale
                    dv2 = _dot(p, dd, "tn")
                    dq_s[rows, :] = dq2
                    kc_s[rows, :] = dk2[SPAN:2 * SPAN]
                    vc_s[rows, :] = dv2[SPAN:2 * SPAN]
                    if b:
                        kc_s[rows_before, :] += dk2[0:SPAN]
                        vc_s[rows_before, :] += dv2[0:SPAN]
                    else:
                        kp_s[rows_before, :] = dk2[0:SPAN]
                        vp_s[rows_before, :] = dv2[0:SPAN]

                _for_residues(dil, one)
            dq_ref[...] = dq_s[...].astype(dq_ref.dtype)
            last = pl.ds((m - 1) * sub, sub)
            kcar[last, :] += kp_s[...]
            vcar[last, :] += vp_s[...]
            dk_ref[...] = kcar[...].astype(dk_ref.dtype)
            dv_ref[...] = vcar[...].astype(dv_ref.dtype)
            kcar[...] = kc_s[...]
            vcar[...] = vc_s[...]

        @pl.when(i == nc)
        def _():
            dk_ref[...] = kcar[...].astype(dk_ref.dtype)
            dv_ref[...] = vcar[...].astype(dv_ref.dtype)

    stat = pl.BlockSpec((R, 128), lambda p, i: (cl(i), p))
    big = jax.ShapeDtypeStruct((S, len(DIL_GROUPS) * 256), MXU_DTYPE)
    return pl.pallas_call(
        body, name=f"dil_bwd{g}", grid=(2, nc + 1),
        in_specs=[cur(qc), prv(kc), cur(kc), prv(vc), cur(vc), stat, stat, stat,
                  pl.BlockSpec((1, 1, 2 * SPAN, 2 * SPAN), lambda p, i: (g, p, 0, 0))]
        + [pl.BlockSpec(memory_space=pl.ANY)] * n_into,
        out_specs=[q_out, kv_out, kv_out, pl.BlockSpec((1, 2 * SPAN, 2 * SPAN), lambda p, i: (p, 0, 0))],
        out_shape=[big, big, big, jax.ShapeDtypeStruct((2, 2 * SPAN, 2 * SPAN), F32)],
        input_output_aliases={9 + j: j for j in range(n_into)},
        scratch_shapes=[pltpu.VMEM((R, 128), F32)] * 3 + [pltpu.VMEM((sub, 128), F32)] * 2 + [pltpu.VMEM((R, 128), F32)] * 2,
        compiler_params=_cparams(("parallel", "arbitrary")),
    )(proj, proj, proj, proj, proj, do, lse, delta, bias.reshape(3, 2, 2 * SPAN, 2 * SPAN), *(into or ()))


def _dilated_merge(os_, lses, *, S, bt=512):
    tile = pl.BlockSpec((bt, 128), lambda i, p: (i, p))

    def body(o0, o1, o2, l0, l1, l2, o_ref, om_ref, lse_ref):
        lane = lax.broadcasted_iota(jnp.int32, (bt, 128), 1)
        lo = lane < DIL_HEAD_DIM
        ls = [l0[...], l1[...], l2[...]]
        ws, stat = [], jnp.zeros((bt, 128), F32)
        for e in range(2):
            a = [l[:, e:e + 1] for l in ls]
            m = jnp.maximum(jnp.maximum(a[0], a[1]), a[2])
            ex = [jnp.exp(v - m) for v in a]
            tot = ex[0] + ex[1] + ex[2]
            ws.append([v / tot for v in ex])
            stat = jnp.where(lane == e, m + jnp.log(tot), stat)
        acc = jnp.zeros((bt, 128), F32)
        for gi, o in enumerate((o0, o1, o2)):
            acc = acc + jnp.where(lo, ws[0][gi], ws[1][gi]) * o[...]
        o_ref[...] = acc
        om_ref[...] = _mx(acc)
        lse_ref[...] = stat

    return pl.pallas_call(
        body, name="dil_merge", grid=(S // bt, 2),
        in_specs=[tile] * 6, out_specs=[tile, tile, tile],
        out_shape=[jax.ShapeDtypeStruct((S, 256), F32), jax.ShapeDtypeStruct((S, 256), MXU_DTYPE),
                   jax.ShapeDtypeStruct((S, 256), F32)],
        compiler_params=_cparams(("parallel", "parallel")),
    )(*os_, *lses)


def _with_delta(do, o):
    lane = lax.broadcasted_iota(jnp.int32, (do.shape[0], 128), 1)
    stats = []
    for p in range(2):
        prod = do[:, 128 * p:128 * (p + 1)] * o[:, 128 * p:128 * (p + 1)]
        d0 = jnp.sum(jnp.where(lane < DIL_HEAD_DIM, prod, 0.0), axis=-1, keepdims=True)
        d1 = jnp.sum(jnp.where(lane >= DIL_HEAD_DIM, prod, 0.0), axis=-1, keepdims=True)
        stats.append(jnp.where(lane == 0, d0, jnp.where(lane == 1, d1, 0.0)))
    return do, jnp.concatenate(stats, axis=1)


MEM_T = 2048
QM_BLK = C_QM // MEM_HEAD_DIM


def _mem_attn_fwd(proj, kv, *, S):
    scale = MEM_HEAD_DIM ** -0.5

    def body(q_ref, k_ref, v_ref, o_ref, om_ref, lse_ref):
        s = _dot(q_ref[...], k_ref[...], "nt") * scale
        m = jnp.max(s, axis=-1, keepdims=True)
        p = jnp.exp(s - m)
        den = jnp.sum(p, axis=-1, keepdims=True)
        o = _dot(p, v_ref[...]) / den
        o_ref[...] = o
        om_ref[...] = _mx(o)
        lse_ref[0] = m + jnp.log(den)

    return pl.pallas_call(
        body, name="mem_attn_fwd", grid=(S // MEM_T, MEM_HEADS),
        in_specs=[pl.BlockSpec((MEM_T, MEM_HEAD_DIM), lambda i, h: (i, QM_BLK + h)),
                  pl.BlockSpec((N_MEM, MEM_HEAD_DIM), lambda i, h: (0, h)),
                  pl.BlockSpec((N_MEM, MEM_HEAD_DIM), lambda i, h: (0, MEM_HEADS + h))],
        out_specs=[pl.BlockSpec((MEM_T, MEM_HEAD_DIM), lambda i, h: (i, h)),
                   pl.BlockSpec((MEM_T, MEM_HEAD_DIM), lambda i, h: (i, h)),
                   pl.BlockSpec((1, MEM_T, 1), lambda i, h: (h, i, 0))],
        out_shape=[jax.ShapeDtypeStruct((S, MEM_WIDTH), F32), jax.ShapeDtypeStruct((S, MEM_WIDTH), MXU_DTYPE),
                   jax.ShapeDtypeStruct((MEM_HEADS, S, 1), F32)],
        compiler_params=_cparams(("parallel", "parallel")),
    )(proj, kv, kv)


def _mem_attn_bwd(proj, kv, om, lse, dy, w_out, *, S):
    scale = MEM_HEAD_DIM ** -0.5

    def body(q_ref, k_ref, v_ref, o_ref, lse_ref, dy_ref, wo_ref, dq_ref, dk_ref, dv_ref):
        @pl.when(pl.program_id(1) == 0)
        def _():
            dk_ref[...] = jnp.zeros_like(dk_ref)
            dv_ref[...] = jnp.zeros_like(dv_ref)

        qv, kv_, vv, dov = q_ref[...], k_ref[...], v_ref[...], _dot(dy_ref[...], wo_ref[...], "nt")
        p = jnp.exp(_dot(qv, kv_, "nt") * scale - lse_ref[0])
        delta = jnp.sum(dov * o_ref[...], axis=-1, keepdims=True)
        ds = p * (_dot(dov, vv, "nt") - delta)
        dq_ref[...] = (_dot(ds, kv_) * scale).astype(dq_ref.dtype)
        dk_ref[...] += _dot(ds, qv, "tn") * scale
        dv_ref[...] += _dot(p, dov, "tn")

    tile = pl.BlockSpec((MEM_T, MEM_HEAD_DIM), lambda h, i: (i, h))
    kvo = pl.BlockSpec((N_MEM, MEM_HEAD_DIM), lambda h, i: (0, h))
    return pl.pallas_call(
        body, name="mem_attn_bwd", grid=(MEM_HEADS, S // MEM_T),
        in_specs=[pl.BlockSpec((MEM_T, MEM_HEAD_DIM), lambda h, i: (i, QM_BLK + h)),
                  pl.BlockSpec((N_MEM, MEM_HEAD_DIM), lambda h, i: (0, h)),
                  pl.BlockSpec((N_MEM, MEM_HEAD_DIM), lambda h, i: (0, MEM_HEADS + h)),
                  tile, pl.BlockSpec((1, MEM_T, 1), lambda h, i: (h, i, 0)),
                  pl.BlockSpec((MEM_T, D_MODEL), lambda h, i: (i, 0)),
                  pl.BlockSpec((MEM_HEAD_DIM, D_MODEL), lambda h, i: (h, 0))],
        out_specs=[tile, kvo, kvo],
        out_shape=[jax.ShapeDtypeStruct((S, MEM_WIDTH), MXU_DTYPE), jax.ShapeDtypeStruct((N_MEM, MEM_WIDTH), F32),
                   jax.ShapeDtypeStruct((N_MEM, MEM_WIDTH), F32)],
        compiler_params=_cparams(("parallel", "arbitrary")),
    )(proj, kv, kv, om, lse, dy, w_out)


MIX_BM = 1024
MIX_BN = 256
GATES_BLK = C_GATES // MIX_BN


def _mix_specs(j_outer):
    ix = (lambda f: (lambda j, i: f(i, j))) if j_outer else (lambda f: f)
    act = lambda width: pl.BlockSpec((MIX_BM, width), ix(lambda i, j: (i, 0)))
    wgt = lambda width: pl.BlockSpec((width, MIX_BN), ix(lambda i, j: (0, j)))
    gate = lambda b: pl.BlockSpec((MIX_BM, MIX_BN), ix(lambda i, j: (i, GATES_BLK + 4 * b + j)))
    bias = lambda b: pl.BlockSpec((1, MIX_BN), ix(lambda i, j: (0, 4 * b + j)))
    tile = pl.BlockSpec((MIX_BM, MIX_BN), ix(lambda i, j: (i, j)))
    return act, wgt, gate, bias, tile


def _mix_fwd(z_lru, o_dil, om, w_lru, w_dil, w_mem, proj, b_gate, *, S):
    act, wgt, gate, bias, tile = _mix_specs(False)

    def body(zl, od, mo, wl, wd, wm, g0, g1, g2, b0, b1, b2, o_ref, t0, t1, t2):
        acc = None
        for a_ref, w_ref, g_ref, b_ref, t_ref in ((zl, wl, g0, b0, t0), (od, wd, g1, b1, t1), (mo, wm, g2, b2, t2)):
            gt = jax.nn.sigmoid(g_ref[...] + b_ref[...])
            t_ref[...] = gt.astype(t_ref.dtype)
            term = gt * _dot(a_ref[...], w_ref[...])
            acc = term if acc is None else acc + term
        o_ref[...] = acc.astype(o_ref.dtype)

    return pl.pallas_call(
        body, name="mix_fwd", grid=(S // MIX_BM, D_MODEL // MIX_BN),
        in_specs=[act(D_RNN), act(256), act(MEM_WIDTH), wgt(D_RNN), wgt(256), wgt(MEM_WIDTH),
                  gate(0), gate(1), gate(2), bias(0), bias(1), bias(2)],
        out_specs=[tile] * 4, out_shape=[jax.ShapeDtypeStruct((S, D_MODEL), MXU_DTYPE)] * 4,
        compiler_params=_cparams(("parallel", "parallel")),
    )(z_lru, o_dil, om, w_lru, w_dil, w_mem, proj, proj, proj, b_gate, b_gate, b_gate)


def _mix_bwd(dx1, w_out, z_lru, o_dil, om, w_lru, w_dil, w_mem, gates, *, S):
    act, wgt, _, _, tile = _mix_specs(False)
    n_j = D_MODEL // MIX_BN

    def body(dx, wo, zl, od, mo, wl, wd, wm, t0, t1, t2,
             dg0, dg1, dg2, dy0, dy1, dy2, db0, db1, db2):
        j = pl.program_id(1)

        @pl.when((pl.program_id(0) == 0) & (j == 0))
        def _():
            for r in (db0, db1, db2):
                r[...] = jnp.zeros_like(r)

        dmv = _dot(dx[...], wo[...], "nt")
        for act_ref, w_ref, t_ref, dg_ref, dy_ref, db_ref in (
                (zl, wl, t0, dg0, dy0, db0), (od, wd, t1, dg1, dy1, db1), (mo, wm, t2, dg2, dy2, db2)):
            y = _dot(act_ref[...], w_ref[...])
            gt = t_ref[...].astype(F32)
            dgate = dmv * y * gt * (1.0 - gt)
            dg_ref[...] = dgate.astype(dg_ref.dtype)
            dy_ref[...] = (dmv * gt).astype(dy_ref.dtype)
            db_ref[j] += _colsum(dgate)

    big = jax.ShapeDtypeStruct((S, D_MODEL), MXU_DTYPE)
    vec = jax.ShapeDtypeStruct((n_j, 1, MIX_BN), F32)
    vspec = pl.BlockSpec((n_j, 1, MIX_BN), lambda i, j: (0, 0, 0))
    res = pl.pallas_call(
        body, name="mix_bwd", grid=(S // MIX_BM, n_j),
        in_specs=[pl.BlockSpec((MIX_BM, D_MODEL), lambda i, j: (i, 0)), pl.BlockSpec((MIX_BN, D_MODEL), lambda i, j: (j, 0)),
                  act(D_RNN), act(256), act(MEM_WIDTH), wgt(D_RNN), wgt(256), wgt(MEM_WIDTH), tile, tile, tile],
        out_specs=[tile] * 6 + [vspec] * 3, out_shape=[big] * 6 + [vec] * 3,
        compiler_params=_cparams(("arbitrary", "arbitrary")),
    )(dx1, w_out, z_lru, o_dil, om, w_lru, w_dil, w_mem, *gates)
    return list(res[:6]) + [r.reshape(1, D_MODEL) for r in res[6:]]


def _adamw_math(w, g, m, v):
    m = ADAM_B1 * m + (1.0 - ADAM_B1) * g
    v = ADAM_B2 * v + (1.0 - ADAM_B2) * (g * g)
    m_hat = m / (1.0 - ADAM_B1 ** ADAM_STEP)
    v_hat = v / (1.0 - ADAM_B2 ** ADAM_STEP)
    delta = -ADAM_LR * (m_hat / (jnp.sqrt(v_hat) + ADAM_EPS) + ADAM_WD * w)
    return delta, m, v


def _adamw_landed(w, own, land, m, v, *, name, col_blk=0, prev=None):
    R = w.shape[0]
    n_parts, C = land.shape[0], land.shape[2]
    br = next(d for d in (256, 464, 128) if R % d == 0)
    tile = pl.BlockSpec((br, C), lambda i: (i, col_blk))
    part = pl.BlockSpec((br, C), lambda i: (i, 0))
    n_prev = 0 if prev is None else 4

    def body(w_ref, o_ref, l_ref, m_ref, v_ref, *rest):
        g_ref, d_ref, nm_ref, nv_ref = rest[n_prev:]
        g = o_ref[...].astype(F32)
        for p in range(n_parts):
            g = g + l_ref[p].astype(F32)
        d, nm, nv = _adamw_math(w_ref[...], g, m_ref[...], v_ref[...])
        g_ref[...] = g
        d_ref[...] = d
        nm_ref[...] = nm
        nv_ref[...] = nv

    return pl.pallas_call(
        body, name=name, grid=(R // br,),
        in_specs=[tile, part, pl.BlockSpec((n_parts, br, C), lambda i: (0, i, 0)), tile, tile]
        + [pl.BlockSpec(memory_space=pl.ANY)] * n_prev,
        out_specs=[tile] * 4, out_shape=[jax.ShapeDtypeStruct(w.shape, F32)] * 4,
        input_output_aliases={5 + j: j for j in range(n_prev)},
        compiler_params=_cparams(("parallel",)),
    )(w, own, land, m, v, *(prev or ()))


def _adamw_plain(w, g, m, v, *, name):
    def body(w_ref, g_ref, m_ref, v_ref, d_ref, nm_ref, nv_ref):
        d, nm, nv = _adamw_math(w_ref[...], g_ref[...], m_ref[...], v_ref[...])
        d_ref[...] = d
        nm_ref[...] = nm
        nv_ref[...] = nv

    return pl.pallas_call(
        body, name=name, out_shape=[jax.ShapeDtypeStruct(w.shape, F32)] * 3, compiler_params=_cparams(),
    )(w, g, m, v)


def _my_pos():
    return lax.axis_index("x"), lax.axis_index("y"), lax.axis_index("c")


def _dev_index(p):
    return 4 * p[0] + 2 * p[1] + p[2]


def _peers(me):
    x, y, c = me
    out = []
    for k in range(1, 8):
        fx, fy, fc = (k >> 2) & 1, (k >> 1) & 1, k & 1
        out.append((k - 1, (1 - x if fx else x, 1 - y if fy else y, 1 - c if fc else c)))
    return out


HBM_SPEC = pl.BlockSpec(memory_space=pltpu.HBM)
SEM_SPEC = pl.BlockSpec(memory_space=pltpu.SEMAPHORE)
DATAFLOW_EFFECT = pltpu.SideEffectType.DATAFLOW_SIDE_EFFECTING


def _gather_refs(src, land, me, peer, k):
    return src, land.at[_dev_index(me)]


def _scatter_refs(src, land, me, peer, k):
    return src.at[_dev_index(peer)], land.at[k]


ALL_RELATIONS = tuple(range(7))
ONE_PER_CHIP = (0, 1, 3, 5)


def _push_start(srcs, land_shapes, refs_of, name, after=(), relations=ALL_RELATIONS):
    n, n_after = len(srcs), len(after)

    def body(*refs):
        ins, lands = refs[:n], refs[n:2 * n]
        send_sems, recv_sems, token = refs[2 * n + n_after], refs[2 * n + n_after + 1], refs[-1]
        me = _my_pos()
        for k, peer in _peers(me):
            if k not in relations:
                continue
            for a in range(n):
                src, dst = refs_of(ins[a], lands[a], me, peer, k)
                pltpu.make_async_remote_copy(src_ref=src, dst_ref=dst, send_sem=send_sems.at[7 * a + k],
                                             recv_sem=recv_sems.at[7 * a + k], device_id=peer, device_id_type=MESH).start()
        token[...] = jnp.zeros_like(token)

    lands = [lax.empty(shp, s.dtype) for shp, s in zip(land_shapes, srcs)]
    hbm = lambda a: pltpu.with_memory_space_constraint(a, pltpu.HBM)
    res = pl.pallas_call(
        body, name=name,
        out_shape=(pltpu.SemaphoreType.DMA((7 * n,)), pltpu.SemaphoreType.DMA((7 * n,)),
                   *[pltpu.HBM(s.shape, s.dtype) for s in srcs], *[pltpu.HBM(l.shape, l.dtype) for l in lands],
                   jax.ShapeDtypeStruct((8, 128), F32)),
        in_specs=[HBM_SPEC] * (2 * n) + [pl.BlockSpec(memory_space=pl.ANY)] * n_after,
        out_specs=(SEM_SPEC, SEM_SPEC, *[HBM_SPEC] * (2 * n), pl.BlockSpec(memory_space=pltpu.VMEM)),
        input_output_aliases={i: 2 + i for i in range(2 * n)},
        compiler_params=pltpu.CompilerParams(has_side_effects=DATAFLOW_EFFECT),
    )(*[hbm(s) for s in srcs], *[hbm(l) for l in lands], *after)
    return dict(sems=(res[0], res[1]), srcs=list(res[2:2 + n]), lands=list(res[2 + n:2 + 2 * n]), token=res[-1], n=n,
                refs_of=refs_of, name=name, relations=relations)


def _push_wait(started, after):
    n, refs_of, relations = started["n"], started["refs_of"], started["relations"]
    after = list(after) if isinstance(after, (list, tuple)) else [after]

    def body(*refs):
        ins, lands = refs[:n], refs[n:2 * n]
        send_sems, recv_sems = refs[2 * n], refs[2 * n + 1]
        me = _my_pos()
        for k, peer in _peers(me):
            if k not in relations:
                continue
            for a in range(n):
                src, dst = refs_of(ins[a], lands[a], me, peer, k)
                cp = pltpu.make_async_remote_copy(src_ref=src, dst_ref=dst, send_sem=send_sems.at[7 * a + k],
                                                  recv_sem=recv_sems.at[7 * a + k], device_id=peer, device_id_type=MESH)
                cp.wait_send()
                cp.wait_recv()

    arrs = started["srcs"] + started["lands"]
    res = pl.pallas_call(
        body, name=started["name"].replace("start", "wait"),
        out_shape=tuple(pltpu.HBM(a.shape, a.dtype) for a in arrs),
        in_specs=[HBM_SPEC] * (2 * n) + [SEM_SPEC, SEM_SPEC] + [pl.BlockSpec(memory_space=pl.ANY)] * len(after),
        out_specs=tuple([HBM_SPEC] * (2 * n)),
        input_output_aliases={i: i for i in range(2 * n)},
        compiler_params=pltpu.CompilerParams(has_side_effects=DATAFLOW_EFFECT),
    )(*arrs, *started["sems"], *after)
    return list(res[n:2 * n])


def _other_chips(x, y):
    return ((1 - x, y), (x, 1 - y), (1 - x, 1 - y))


def _forward_start(land, name, after=()):
    n_after = len(after)

    def body(*refs):
        land_ref, send_sems, recv_sems, token = refs[0], refs[1 + n_after], refs[2 + n_after], refs[-1]
        x, y, c = _my_pos()
        for j, (cx, cy) in enumerate(_other_chips(x, y)):
            blk = land_ref.at[_dev_index((cx, cy, c))]
            pltpu.make_async_remote_copy(src_ref=blk, dst_ref=blk, send_sem=send_sems.at[j], recv_sem=recv_sems.at[j],
                                         device_id=(x, y, 1 - c), device_id_type=MESH).start()
        token[...] = jnp.zeros_like(token)

    res = pl.pallas_call(
        body, name=name,
        out_shape=(pltpu.SemaphoreType.DMA((3,)), pltpu.SemaphoreType.DMA((3,)), pltpu.HBM(land.shape, land.dtype),
                   jax.ShapeDtypeStruct((8, 128), F32)),
        in_specs=[HBM_SPEC] + [pl.BlockSpec(memory_space=pl.ANY)] * n_after,
        out_specs=(SEM_SPEC, SEM_SPEC, HBM_SPEC, pl.BlockSpec(memory_space=pltpu.VMEM)),
        input_output_aliases={0: 2},
        compiler_params=pltpu.CompilerParams(has_side_effects=DATAFLOW_EFFECT),
    )(pltpu.with_memory_space_constraint(land, pltpu.HBM), *after)
    return dict(sems=(res[0], res[1]), land=res[2], token=res[3], name=name)


def _forward_wait(started, after):
    after = list(after) if isinstance(after, (list, tuple)) else [after]

    def body(land_ref, send_sems, recv_sems, *rest):
        x, y, c = _my_pos()
        for j, (cx, cy) in enumerate(_other_chips(x, y)):
            cp = pltpu.make_async_remote_copy(
                src_ref=land_ref.at[_dev_index((cx, cy, c))], dst_ref=land_ref.at[_dev_index((cx, cy, 1 - c))],
                send_sem=send_sems.at[j], recv_sem=recv_sems.at[j], device_id=(x, y, 1 - c), device_id_type=MESH)
            cp.wait_send()
            cp.wait_recv()

    land = started["land"]
    return pl.pallas_call(
        body, name=started["name"].replace("start", "wait"), out_shape=pltpu.HBM(land.shape, land.dtype),
        in_specs=[HBM_SPEC, SEM_SPEC, SEM_SPEC] + [pl.BlockSpec(memory_space=pl.ANY)] * len(after),
        out_specs=HBM_SPEC, input_output_aliases={0: 0},
        compiler_params=pltpu.CompilerParams(has_side_effects=DATAFLOW_EFFECT),
    )(land, *started["sems"], *after)


def _sum_slots(slots):
    def body(in_ref, out_ref):
        acc = in_ref[0]
        for d in range(1, N_DEV):
            acc = acc + in_ref[d]
        out_ref[...] = acc

    return pl.pallas_call(body, name="sum_small", out_shape=jax.ShapeDtypeStruct(slots.shape[1:], F32),
                          compiler_params=_cparams())(slots)


def _adamw_many(ws, gs, ms, vs):
    n = len(ws)

    def body(*refs):
        for i in range(n):
            w_ref, g_ref, m_ref, v_ref = (refs[j * n + i] for j in range(4))
            d_, nm, nv = _adamw_math(w_ref[...], g_ref[...], m_ref[...], v_ref[...])
            for j, val in enumerate((d_, nm, nv)):
                refs[(4 + j) * n + i][...] = val

    res = pl.pallas_call(body, name="adamw_small", out_shape=[jax.ShapeDtypeStruct(w_.shape, F32) for w_ in ws] * 3,
                         compiler_params=_cparams())(*ws, *gs, *ms, *vs)
    return [(res[i], res[n + i], res[2 * n + i]) for i in range(n)]


def _local_step(x, mem, tgt, W, P, late_weights, send_grads, reduce_small, tie0):
    S = x.shape[0]
    W = dict(W)
    h = _rmsnorm_fwd(x, P["g_mix"] + tie0, rows=S, name="norm_mix")
    mem_n = _rmsnorm_fwd(mem, P["g_mem"], rows=N_MEM, name="norm_mem")
    buckets = _dil_buckets()
    bias = _dil_bias(P["rel_bias"], buckets)
    wa_bd, wx_bd = _mx(_block_diag(P["w_rg_a"])), _mx(_block_diag(P["w_rg_x"]))
    W.update(late_weights("first", [h, mem_n, bias, wa_bd, wx_bd]))
    proj = _matmul(h, W["w_in_t"], M=S, N=D_IN, K=D_MODEL, mode="nt", bm=512, bn=D_IN // 2, bk=D_MODEL, name="mm_in",
                   j_outer=True, deps=[W["started"]])

    group_out = [_dilated_fwd(proj, bias, g, S=S) for g in range(len(DIL_GROUPS))]
    o_dil, o_dil_m, lse_dil = _dilated_merge([o for o, _ in group_out], [l for _, l in group_out], S=S)

    W.update(late_weights("branch", [o_dil]))
    lru_args = (W["conv_w"], P["conv_b"].reshape(1, -1), wa_bd, wx_bd, P["b_rg_a"].reshape(1, -1),
                P["b_rg_x"].reshape(1, -1), P["lru_lambda"].reshape(1, -1))
    hl, z_lru, a_lru, mult_lru = _lru_fwd(proj, *lru_args, S=S)
    kv = _matmul(mem_n, W["w_mem_kv"], M=N_MEM, N=2 * MEM_WIDTH, K=D_MODEL, mode="nn", bm=N_MEM, bn=512, bk=D_MODEL,
                 name="mm_kv")
    om, om_m, lse_mem = _mem_attn_fwd(proj, kv, S=S)
    b_gate = P["b_gate"].reshape(1, -1)
    merged, *gates = _mix_fwd(z_lru, o_dil_m, om_m, W["w_lru_out"], W["w_dil_out"], W["w_mem_out"], proj, b_gate, S=S)
    g_mlp, g_final, g_mix = (P[n].reshape(1, D_MODEL) for n in ("g_mlp", "g_final", "g_mix"))
    x1, hm = _matmul_rows(merged, W["w_out"], M=S, K=D_MODEL, mode="nn", bm=512, name="mm_out",
                          row_fn=_residual_then_norm, out_dtypes=(F32, MXU_DTYPE), tiles=[x], vecs=[g_mlp])
    W.update(late_weights("mlp", [hm]))

    def relu2(acc):
        rl = jnp.maximum(acc, 0.0)
        return rl * rl, rl

    act, relu_u = _matmul(hm, W["w_mlp_in_t"], M=S, N=D_FF, K=D_MODEL, mode="nt", bm=1024, bn=1024, bk=D_MODEL,
                          name="mm_mlp_in", out_dtypes=(MXU_DTYPE, MXU_DTYPE), epilogue=relu2, j_outer=True)
    dx2, dx2_m, loss, dg_final = _matmul_rows(
        act, W["w_mlp_out"], M=S, K=D_FF, mode="nn", bm=512, name="mm_mlp_out", row_fn=_residual_then_loss,
        out_dtypes=(F32, MXU_DTYPE), tiles=[x1, tgt], vecs=[g_final], acc_widths=(1, D_MODEL))

    G, Gs = {}, {}
    Gs["g_final"] = dg_final
    dw = dict(mode="tn", K=S, bk=S, out_dtypes=(MXU_DTYPE,))
    G["w_mlp_out"] = _matmul(act, dx2_m, M=D_FF, N=D_MODEL, bm=512, bn=D_MODEL, name="mm_dw_mlp_out",
                             parts=("rows", D_FF // N_DEV), **dw)
    du = _matmul(dx2_m, W["w_mlp_out"], M=S, N=D_FF, K=D_MODEL, mode="nt", bm=1024, bn=1024, bk=D_MODEL, name="mm_du",
                 out_dtypes=(MXU_DTYPE,), epilogue=lambda acc, rl: (acc * (2.0 * rl.astype(F32)),),
                 extras=[(relu_u, (0, 0))], j_outer=True)
    G["w_mlp_in"] = _matmul(hm, du, M=D_MODEL, N=D_FF, bm=D_MODEL, bn=512, name="mm_dw_mlp_in",
                            parts=("cols", D_FF // N_DEV), **dw)
    tie1 = send_grads({n: G.pop(n) for n in ("w_mlp_out", "w_mlp_in")})
    dx1, dx1_m, Gs["g_mlp"] = _matmul_rows(
        du, W["w_mlp_in_t"], M=S, K=D_FF, mode="nn", bm=512, name="mm_dhm", row_fn=_norm_bwd_then_residual(2),
        out_dtypes=(F32, MXU_DTYPE), tiles=[x1, dx2], vecs=[g_mlp], acc_widths=(D_MODEL,), deps=[tie1])
    dws = dict(mode="tn", K=S, bk=1024, out_dtypes=(MXU_DTYPE,))
    G["w_out"] = _matmul(merged, dx1_m, M=D_MODEL, N=D_MODEL, bm=D_MODEL, bn=D_MODEL, name="mm_dw_out",
                         parts=("rows", D_MODEL // N_DEV), **dws)
    (dg0, dg1, dg2, dy_lru, dy_dil, dy_mem, db0, db1, db2) = _mix_bwd(
        dx1_m, W["w_out"], z_lru, o_dil_m, om_m, W["w_lru_out"], W["w_dil_out"], W["w_mem_out"], gates, S=S)
    Gs["b_gate0"], Gs["b_gate1"], Gs["b_gate2"] = db0, db1, db2

    G["w_mem_out"] = _matmul(om_m, dy_mem, M=MEM_WIDTH, N=D_MODEL, bm=MEM_WIDTH, bn=D_MODEL, name="mm_dw_mem_out",
                             parts=("cols", D_MODEL // N_DEV), **dws)
    dqm, dk_mem, dv_mem = _mem_attn_bwd(proj, kv, om, lse_mem, dy_mem, W["w_mem_out"], S=S)
    dkv = jnp.concatenate([dk_mem, dv_mem], axis=1)
    G["w_mem_kv"] = _matmul(mem_n, dkv, M=D_MODEL, N=2 * MEM_WIDTH, K=N_MEM, mode="tn", bm=D_MODEL, bn=2 * MEM_WIDTH,
                            bk=N_MEM, name="mm_dw_kv", out_dtypes=(MXU_DTYPE,), parts=("rows", D_MODEL // N_DEV))
    dmem_n = _matmul(dkv, W["w_mem_kv"], M=N_MEM, N=D_MODEL, K=2 * MEM_WIDTH, mode="nt", bm=N_MEM, bn=D_MODEL,
                     bk=2 * MEM_WIDTH, name="mm_dmem")
    (Gs["g_mem"],) = _rmsnorm_bwd(mem, P["g_mem"], dmem_n, None, rows=N_MEM, name="norm_mem_bwd", dx_dtypes=())

    G["w_dil_out"] = _matmul(o_dil_m, dy_dil, M=256, N=D_MODEL, bm=256, bn=D_MODEL, name="mm_dw_dil_out",
                             parts=("cols", D_MODEL // N_DEV), **dws)
    do_dil, delta = _matmul(dy_dil, W["w_dil_out"], M=S, N=256, K=D_MODEL, mode="nt", bm=512, bn=256, bk=D_MODEL,
                            name="mm_do_dil", out_dtypes=(F32, F32), epilogue=_with_delta, extras=[(o_dil, (0, 0))])
    G["w_lru_out"] = _matmul(z_lru, dy_lru, M=D_RNN, N=D_MODEL, bm=D_RNN, bn=D_MODEL, name="mm_dw_lru_out",
                             parts=("cols", D_MODEL // N_DEV), **dws)
    tie2 = send_grads({n: G.pop(n) for n in ("w_out", "w_mem_out", "w_mem_kv", "w_dil_out", "w_lru_out")})
    bias = bias + tie2[0, 0]
    dqkv, dbias = None, []
    for g in range(len(DIL_GROUPS)):
        *dqkv, db_g = _dilated_bwd(proj, do_dil, lse_dil, delta, bias, g, S=S, into=dqkv)
        dbias.append(db_g)
    drel = _dil_bias_bwd(jnp.stack(dbias, axis=0).reshape(len(DIL_GROUPS), DIL_HEADS, SPAN, 2 * SPAN), buckets)
    Gs["rel_bias"] = drel

    dxl, dgl, dcw, dcb, dwa, dwx, dba, dbx, dlam = _lru_bwd(proj, hl, a_lru, mult_lru, dy_lru, W["w_lru_out"], *lru_args,
                                                            S=S)
    Gs["conv_w"], Gs["conv_b"] = dcw, dcb
    Gs["w_rg_a"], Gs["w_rg_x"] = _block_diag_extract(dwa), _block_diag_extract(dwx)
    Gs["b_rg_a"], Gs["b_rg_x"], Gs["lru_lambda"] = dba, dbx, dlam
    Gs["loss"] = loss

    dproj = [dxl, dgl] + dqkv + [dqm, dg0, dg1, dg2]
    tie = []
    for q in range(W_IN_PIECES):
        dw_q = None
        for half in range(2):
            dw_q = _dw_in_t_half(h, dproj, q, half, S=S, name=f"mm_dw_in_{q}_{half}", into=dw_q, deps=tie)
        tie = [send_grads({f"w_in_{q}": dw_q})]
    grad_x, Gs["g_mix"] = _matmul_rows(
        dproj, W["w_in_t"], M=S, K=D_IN, mode="nn", bm=256, name="mm_dh", row_fn=_norm_bwd_then_residual(1),
        out_dtypes=(F32,), tiles=[x, dx1], vecs=[g_mix], acc_widths=(D_MODEL,), deps=tie)
    return grad_x, reduce_small(Gs)


BIG = ("w_in", "w_lru_out", "w_dil_out", "w_mem_kv", "w_mem_out", "w_out", "w_mlp_in", "w_mlp_out")
W_IN_PIECES = 2
COL_SHARDED = ("w_lru_out", "w_dil_out", "w_mem_out", "w_mlp_in")
GATHERED_TRANSPOSED = ("w_mlp_in",)
SMALL = ("g_mix", "b_gate", "conv_b", "w_rg_a", "b_rg_a", "w_rg_x", "b_rg_x", "lru_lambda", "rel_bias", "g_mem",
         "g_mlp", "g_final")
WEIGHTS = ("g_mix", "w_in", "b_gate", "conv_w", "conv_b", "w_rg_a", "b_rg_a", "w_rg_x", "b_rg_x", "lru_lambda",
           "w_lru_out", "rel_bias", "w_dil_out", "g_mem", "w_mem_kv", "w_mem_out", "w_out", "g_mlp", "w_mlp_in",
           "w_mlp_out", "g_final")


def _gathered_to_full(name, gathered):
    if name in COL_SHARDED:
        n, r, c = gathered.shape
        return gathered.transpose(1, 0, 2).reshape(r, n * c)
    n, r, c = gathered.shape
    return gathered.reshape(n * r, c)


SMALL_GRADS = (("g_mix", (1, 1024)), ("b_gate0", (1, 1024)), ("b_gate1", (1, 1024)), ("b_gate2", (1, 1024)),
               ("conv_b", (1, 768)), ("w_rg_a", (12, 64, 64)), ("b_rg_a", (1, 768)), ("w_rg_x", (12, 64, 64)),
               ("b_rg_x", (1, 768)), ("lru_lambda", (1, 768)), ("rel_bias", (32, 128)), ("g_mem", (1, 1024)),
               ("g_mlp", (1, 1024)), ("g_final", (1, 1024)), ("conv_w", (4, 768)), ("loss", (1, 1)))


def _pack(parts):
    flat = jnp.concatenate([p.reshape(-1) for p in parts])
    return jnp.pad(flat, (0, (-flat.shape[0]) % 1024)).reshape(-1, 128)


def _unpack(pack, shapes):
    flat = pack.reshape(-1)
    out, off = [], 0
    for shp in shapes:
        size = math.prod(shp)
        out.append(flat[off:off + size].reshape(shp))
        off += size
    return out


def kernel(x, mem, g_mix, w_in, b_gate, conv_w, conv_b, w_rg_a, b_rg_a, w_rg_x, b_rg_x, lru_lambda, w_lru_out, rel_bias, w_dil_out, g_mem, w_mem_kv, w_mem_out, w_out, g_mlp, w_mlp_in, w_mlp_out, g_final, loss_target, m_g_mix, m_w_in, m_b_gate, m_conv_w, m_conv_b, m_w_rg_a, m_b_rg_a, m_w_rg_x, m_b_rg_x, m_lru_lambda, m_w_lru_out, m_rel_bias, m_w_dil_out, m_g_mem, m_w_mem_kv, m_w_mem_out, m_w_out, m_g_mlp, m_w_mlp_in, m_w_mlp_out, m_g_final, v_g_mix, v_w_in, v_b_gate, v_conv_w, v_conv_b, v_w_rg_a, v_b_rg_a, v_w_rg_x, v_b_rg_x, v_lru_lambda, v_w_lru_out, v_rel_bias, v_w_dil_out, v_g_mem, v_w_mem_kv, v_w_mem_out, v_w_out, v_g_mlp, v_w_mlp_in, v_w_mlp_out, v_g_final):
    w = dict(g_mix=g_mix, w_in=w_in, b_gate=b_gate, conv_w=conv_w, conv_b=conv_b, w_rg_a=w_rg_a, b_rg_a=b_rg_a,
             w_rg_x=w_rg_x, b_rg_x=b_rg_x, lru_lambda=lru_lambda, w_lru_out=w_lru_out, rel_bias=rel_bias,
             w_dil_out=w_dil_out, g_mem=g_mem, w_mem_kv=w_mem_kv, w_mem_out=w_mem_out, w_out=w_out, g_mlp=g_mlp,
             w_mlp_in=w_mlp_in, w_mlp_out=w_mlp_out, g_final=g_final)
    m = dict(g_mix=m_g_mix, w_in=m_w_in, b_gate=m_b_gate, conv_w=m_conv_w, conv_b=m_conv_b, w_rg_a=m_w_rg_a,
             b_rg_a=m_b_rg_a, w_rg_x=m_w_rg_x, b_rg_x=m_b_rg_x, lru_lambda=m_lru_lambda, w_lru_out=m_w_lru_out,
             rel_bias=m_rel_bias, w_dil_out=m_w_dil_out, g_mem=m_g_mem, w_mem_kv=m_w_mem_kv, w_mem_out=m_w_mem_out,
             w_out=m_w_out, g_mlp=m_g_mlp, w_mlp_in=m_w_mlp_in, w_mlp_out=m_w_mlp_out, g_final=m_g_final)
    v = dict(g_mix=v_g_mix, w_in=v_w_in, b_gate=v_b_gate, conv_w=v_conv_w, conv_b=v_conv_b, w_rg_a=v_w_rg_a,
             b_rg_a=v_b_rg_a, w_rg_x=v_w_rg_x, b_rg_x=v_b_rg_x, lru_lambda=v_lru_lambda, w_lru_out=v_w_lru_out,
             rel_bias=v_rel_bias, w_dil_out=v_w_dil_out, g_mem=v_g_mem, w_mem_kv=v_w_mem_kv, w_mem_out=v_w_mem_out,
             w_out=v_w_out, g_mlp=v_g_mlp, w_mlp_in=v_w_mlp_in, w_mlp_out=v_w_mlp_out, g_final=v_g_final)

    my_idx = _dev_index(_my_pos())

    w_in_shard = _mx(w["w_in"].T)
    first = _push_start([w_in_shard], [(N_DEV,) + w_in_shard.shape], _gather_refs, "gather_in_start",
                        relations=ONE_PER_CHIP)
    cw_cols = D_RNN // N_DEV
    conv_pad = jnp.zeros((64, D_MODEL), F32).at[:CONV_WIDTH, :cw_cols].set(w["conv_w"])
    late = {}
    P = {n: w[n] for n in SMALL}

    def start_late(order_after):
        for group, names in (("branch", ("w_mem_kv", "w_lru_out", "w_dil_out", "w_mem_out", "w_out", "conv_w")),
                             ("mlp", ("w_mlp_in", "w_mlp_out"))):
            shards = [conv_pad if n == "conv_w" else _mx(w[n].T if n in GATHERED_TRANSPOSED else w[n]) for n in names]
            started = _push_start(shards, [(N_DEV,) + s.shape for s in shards], _gather_refs, f"gather_{group}_start",
                                  after=order_after)
            late[group] = (names, shards, started)
            order_after = [started["token"]]

    def late_weights(group, after):
        if group == "first":
            (land,) = _push_wait(first, after)
            forward = _forward_start(land, "forward_in_start")
            start_late([forward["token"]])
            full = lax.dynamic_update_index_in_dim(_forward_wait(forward, forward["token"]), w_in_shard, my_idx, 0)
            return {"w_in_t": full.reshape(D_IN, D_MODEL), "started": late["mlp"][2]["token"]}
        names, shards, started = late[group]
        out = {}
        for n, land, own in zip(names, _push_wait(started, after), shards):
            full = lax.dynamic_update_index_in_dim(land, own, my_idx, 0)
            if n == "conv_w":
                out[n] = full[:, :CONV_WIDTH, :cw_cols].transpose(1, 0, 2).reshape(CONV_WIDTH, D_RNN)
            elif n in GATHERED_TRANSPOSED:
                out[n + "_t"] = full.reshape(-1, full.shape[2])
            else:
                out[n] = _gathered_to_full(n, full)
        return out

    sent, small = [], {}

    def send_grads(gs):
        names = list(gs)
        parts = [gs[n] for n in names]
        own = [lax.dynamic_index_in_dim(p, my_idx, 0, keepdims=False) for p in parts]
        started = _push_start(parts, [(N_DEV - 1,) + p.shape[1:] for p in parts], _scatter_refs,
                              f"scatter{len(sent)}_start")
        sent.append((names, own, started))
        return started["token"]

    def reduce_small(gs):
        small["pack"] = _pack([gs[n] for n, _ in SMALL_GRADS])
        small["started"] = _push_start([small["pack"]], [(N_DEV,) + small["pack"].shape], _gather_refs, "small_start")
        return small["started"]["token"]

    grad_x, last_token = _local_step(x[0], mem[0], loss_target[0], {}, P, late_weights, send_grads, reduce_small,
                                     first["token"][0, 0])

    grads, deltas, new_m, new_v = {}, {}, {}, {}
    after = last_token
    for names, own, started in sent[:-W_IN_PIECES]:
        for n, o, land in zip(names, own, _push_wait(started, after)):
            grads[n], deltas[n], new_m[n], new_v[n] = _adamw_landed(w[n], o, land, m[n], v[n], name=f"adamw_{n}")
            after = deltas[n]
    prev = None
    for q, (names, own, started) in enumerate(sent[-W_IN_PIECES:]):
        (land,) = _push_wait(started, after)
        prev = _adamw_landed(w["w_in"].T, own[0], land, m["w_in"].T, v["w_in"].T, name=f"adamw_{names[0]}",
                             col_blk=q, prev=prev)
        after = prev[1]
    grads["w_in"], deltas["w_in"], new_m["w_in"], new_v["w_in"] = [t.T for t in prev]
    (small_land,) = _push_wait(small["started"], [after] + [deltas[n] for n in BIG if n != "w_in"])
    total = _sum_slots(lax.dynamic_update_index_in_dim(small_land, small["pack"], my_idx, 0))
    summed = dict(zip([n for n, _ in SMALL_GRADS], _unpack(total, [shp for _, shp in SMALL_GRADS])))
    summed["b_gate"] = jnp.concatenate([summed.pop(f"b_gate{b}") for b in range(3)], axis=1)
    summed["rel_bias"] = summed["rel_bias"][:, :3 * DIL_HEADS]
    for n in SMALL:
        grads[n] = summed[n].reshape(w[n].shape)
    small_updates = _adamw_many([w[n] for n in SMALL], [grads[n] for n in SMALL], [m[n] for n in SMALL],
                                [v[n] for n in SMALL])
    for n, (d_, nm_, nv_) in zip(SMALL, small_updates):
        deltas[n], new_m[n], new_v[n] = d_, nm_, nv_
    conv_w_sum, loss_sum = summed["conv_w"], summed["loss"]
    grads["conv_w"] = lax.dynamic_slice(conv_w_sum, (0, my_idx * cw_cols), (CONV_WIDTH, cw_cols))
    deltas["conv_w"], new_m["conv_w"], new_v["conv_w"] = _adamw_plain(
        w["conv_w"], grads["conv_w"], m["conv_w"], v["conv_w"], name="adamw_conv_w")

    return (loss_sum.reshape(()), grad_x[None], *[grads[n] for n in WEIGHTS], *[deltas[n] for n in WEIGHTS],
            *[new_m[n] for n in WEIGHTS], *[new_v[n] for n in WEIGHTS])
```

```python
import functools
import math

import jax
import jax.numpy as jnp
from jax import lax
from jax.experimental import pallas as pl
from jax.experimental.pallas import tpu as pltpu

F32 = jnp.float32
MXU_DTYPE = jnp.bfloat16
VMEM_LIMIT_BYTES = 56 * 1024 * 1024
N_DEV = 8

D_MODEL = 1024
N_MEM = 256
MEM_HEADS = 4
MEM_HEAD_DIM = 128
MEM_WIDTH = 512
D_RNN = 768
LRU_BLOCK = 64
N_LRU_BLOCKS = 12
LRU_GROUP = 256
N_LRU_GROUPS = 3
CONV_WIDTH = 4
LRU_C = 8.0
DIL_GROUPS = ((128, 1), (512, 4), (2048, 16))
SPAN = 128
DIL_HEADS = 4
DIL_HEAD_DIM = 64
NUM_BUCKETS = 32
MAX_DISTANCE = 2048
D_FF = 4096
D_IN = 7424
EPS = 1e-6
NEG = -1e30
C_XL, C_GATE, C_QKV, C_QM, C_GATES = 0, 768, 1536, 3840, 4352

ADAM_LR = 0.001
ADAM_B1 = 0.9
ADAM_B2 = 0.999
ADAM_EPS = 1e-08
ADAM_WD = 0.01
ADAM_STEP = 10

MESH = pl.DeviceIdType.MESH
GELU_K = math.sqrt(2.0 / math.pi)


def _cparams(sem=None):
    kw = dict(vmem_limit_bytes=VMEM_LIMIT_BYTES)
    if sem is not None:
        kw["dimension_semantics"] = sem
    return pltpu.CompilerParams(**kw)


def _mx(v):
    return v.astype(MXU_DTYPE)


def _dot(a, b, mode="nn"):
    dims = {"nn": (((1,), (0,)), ((), ())), "nt": (((1,), (1,)), ((), ())), "tn": (((0,), (0,)), ((), ()))}[mode]
    return lax.dot_general(_mx(a), _mx(b), dims, preferred_element_type=F32)


def _colsum(v):
    return jnp.sum(v, axis=0, keepdims=True)


def _matmul(a, b, *, M, N, K, mode, bm, bn, bk, name, out_dtypes=(F32,), epilogue=None, extras=(),
            a_off=(0, 0), b_off=(0, 0), j_outer=False, deps=(), parts=None):
    assert M % bm == 0 and N % bn == 0 and K % bk == 0, (name, M, N, K, bm, bn, bk)
    nm, nn, nk = M // bm, N // bn, K // bk

    def ij(f):
        if j_outer:
            return lambda j, i, k: f(i, j, k)
        return f

    if mode == "tn":
        a_spec = pl.BlockSpec((bk, bm), ij(lambda i, j, k: (k + a_off[0], i + a_off[1])))
    else:
        a_spec = pl.BlockSpec((bm, bk), ij(lambda i, j, k: (i + a_off[0], k + a_off[1])))
    if mode == "nt":
        b_spec = pl.BlockSpec((bn, bk), ij(lambda i, j, k: (j + b_off[0], k + b_off[1])))
    else:
        b_spec = pl.BlockSpec((bk, bn), ij(lambda i, j, k: (k + b_off[0], j + b_off[1])))
    ex_specs = [pl.BlockSpec((bm, bn), ij(functools.partial(lambda i, j, k, o: (i + o[0], j + o[1]), o=off)))
                for _, off in extras]
    if parts is None:
        out_dims = (M, N)
        out_spec = pl.BlockSpec((bm, bn), ij(lambda i, j, k: (i, j)))
    elif parts[0] == "rows":
        r = parts[1]
        assert bm % r == 0
        out_dims = (M // r, r, N)
        out_spec = pl.BlockSpec((bm // r, r, bn), ij(lambda i, j, k: (i, 0, j)))
    elif parts[0] == "rows_t":
        r = parts[1]
        assert bn % r == 0
        out_dims = (N // r, r, M)
        out_spec = pl.BlockSpec((bn // r, r, bm), ij(lambda i, j, k: (j, 0, i)))
    else:
        c = parts[1]
        assert bn % c == 0
        out_dims = (N // c, M, c)
        out_spec = pl.BlockSpec((bn // c, bm, c), ij(lambda i, j, k: (j, i, 0)))
    n_ex, n_out, n_dep = len(extras), len(out_dtypes), len(deps)

    def body(*refs):
        a_ref, b_ref = refs[0], refs[1]
        ex = refs[2:2 + n_ex]
        outs = refs[2 + n_ex + n_dep:2 + n_ex + n_dep + n_out]
        part = _dot(a_ref[...], b_ref[...], mode)

        def finish(acc):
            vals = epilogue(acc, *[e[...] for e in ex]) if epilogue is not None else (acc,)
            for o, v in zip(outs, vals):
                if parts is not None and parts[0] == "rows_t":
                    v = v.T
                v = v.astype(o.dtype)
                if parts is None:
                    o[...] = v
                elif parts[0] in ("rows", "rows_t"):
                    for ch in range(v.shape[0] // parts[1]):
                        o[ch] = v[ch * parts[1]:(ch + 1) * parts[1], :]
                else:
                    for ch in range(bn // parts[1]):
                        o[ch] = v[:, ch * parts[1]:(ch + 1) * parts[1]]

        if nk == 1:
            finish(part)
        else:
            acc_ref = refs[-1]
            k = pl.program_id(2)

            @pl.when(k == 0)
            def _():
                acc_ref[...] = part

            @pl.when(k > 0)
            def _():
                acc_ref[...] += part

            @pl.when(k == nk - 1)
            def _():
                finish(acc_ref[...])

    grid = (nn, nm, nk) if j_outer else (nm, nn, nk)
    res = pl.pallas_call(
        body, name=name, grid=grid,
        in_specs=[a_spec, b_spec] + ex_specs + [pl.BlockSpec(memory_space=pl.ANY)] * n_dep,
        out_specs=[out_spec] * n_out,
        out_shape=[jax.ShapeDtypeStruct(out_dims, dt) for dt in out_dtypes],
        scratch_shapes=[pltpu.VMEM((bm, bn), F32)] if nk > 1 else [],
        compiler_params=_cparams(("parallel", "parallel", "arbitrary")),
    )(a, b, *[e for e, _ in extras], *deps)
    return res[0] if n_out == 1 else res


ROW_SUBTILES = 2


def _matmul_rows(a, b, *, M, K, mode, bm, name, row_fn, out_dtypes, tiles=(), vecs=(), acc_widths=(), deps=()):
    N = D_MODEL
    assert M % bm == 0
    segs = list(a) if isinstance(a, (list, tuple)) else [a]
    widths = [s_.shape[1] for s_ in segs]
    assert sum(widths) == K and (len(segs) == 1 or mode == "nn")
    n_s, n_t, n_v, n_o, n_a, n_d = len(segs), len(tiles), len(vecs), len(out_dtypes), len(acc_widths), len(deps)
    row = pl.BlockSpec((bm, N), lambda i: (i, 0))
    b_shape = (K, N) if mode == "nn" else (N, K)

    def body(*refs):
        b_ref = refs[n_s]
        ins = refs[n_s + 1:n_s + 1 + n_t + n_v]
        outs = refs[n_s + 1 + n_t + n_v + n_d:n_s + 1 + n_t + n_v + n_d + n_o]
        accs = refs[n_s + 1 + n_t + n_v + n_d + n_o:]
        for o in accs:
            @pl.when(pl.program_id(0) == 0)
            def _(o=o):
                o[...] = jnp.zeros_like(o)

        for s_ in range(ROW_SUBTILES):
            rows = pl.ds(s_ * (bm // ROW_SUBTILES), bm // ROW_SUBTILES)
            if n_s == 1:
                acc = _dot(refs[0][rows, :], b_ref[...], mode)
            else:
                acc, k0 = None, 0
                for a_ref, w_ in zip(refs[:n_s], widths):
                    part = _dot(a_ref[rows, :], b_ref[k0:k0 + w_, :])
                    acc = part if acc is None else acc + part
                    k0 += w_
            tile_vals, partials = row_fn(acc, *[r[rows, :] for r in ins[:n_t]], *[r[...] for r in ins[n_t:]])
            for o, val in zip(outs, tile_vals):
                o[rows, :] = val.astype(o.dtype)
            for o, val in zip(accs, partials):
                o[...] += val

    res = pl.pallas_call(
        body, name=name, grid=(M // bm,),
        in_specs=[pl.BlockSpec((bm, w_), lambda i: (i, 0)) for w_ in widths] + [pl.BlockSpec(b_shape, lambda i: (0, 0))]
        + [row] * n_t + [pl.BlockSpec((1, N), lambda i: (0, 0))] * n_v + [pl.BlockSpec(memory_space=pl.ANY)] * n_d,
        out_specs=[row] * n_o + [pl.BlockSpec((1, w_), lambda i: (0, 0)) for w_ in acc_widths],
        out_shape=[jax.ShapeDtypeStruct((M, N), dt) for dt in out_dtypes]
        + [jax.ShapeDtypeStruct((1, w_), F32) for w_ in acc_widths],
        compiler_params=_cparams(("arbitrary",) if n_a else ("parallel",)),
    )(*segs, b, *tiles, *vecs, *deps)
    return res


def _dw_in_t_half(h, pieces, q, half, *, S, name, into=None, deps=(), bk=1024):
    half_w, cols = D_IN // 2, D_MODEL // W_IN_PIECES
    lo, hi = half * half_w, (half + 1) * half_w
    use, c0 = [], 0
    for p in pieces:
        w_ = p.shape[1]
        a0, a1 = max(lo, c0), min(hi, c0 + w_)
        if a1 > a0:
            use.append((p, a0 - c0, a1 - a0))
        c0 += w_
    n_p, n_into, n_d, nk = len(use), 0 if into is None else 1, len(deps), S // bk
    rows = D_IN // N_DEV

    def body(*refs):
        h_ref, p_refs = refs[0], refs[1:1 + n_p]
        o_ref, acc_ref = refs[1 + n_p + n_into + n_d], refs[-1]
        k = pl.program_id(0)
        dp = jnp.concatenate([r[:, s0:s0 + w_] for r, (_, s0, w_) in zip(p_refs, use)], axis=1)
        part = _dot(h_ref[...], dp, "tn")

        @pl.when(k == 0)
        def _():
            acc_ref[...] = part

        @pl.when(k > 0)
        def _():
            acc_ref[...] += part

        @pl.when(k == nk - 1)
        def _():
            vt = acc_ref[...].T.astype(o_ref.dtype)
            for ch in range(half_w // rows):
                o_ref[ch] = vt[ch * rows:(ch + 1) * rows, :]

    return pl.pallas_call(
        body, name=name, grid=(nk,),
        in_specs=[pl.BlockSpec((bk, cols), lambda k: (k, q))]
        + [pl.BlockSpec((bk, p.shape[1]), lambda k: (k, 0)) for p, _, _ in use]
        + [pl.BlockSpec(memory_space=pl.ANY)] * (n_into + n_d),
        out_specs=pl.BlockSpec((half_w // rows, rows, cols), lambda k: (half, 0, 0)),
        out_shape=jax.ShapeDtypeStruct((N_DEV, rows, cols), MXU_DTYPE),
        input_output_aliases={1 + n_p: 0} if n_into else {},
        scratch_shapes=[pltpu.VMEM((cols, half_w), F32)],
        compiler_params=_cparams(("arbitrary",)),
    )(h, *[p for p, _, _ in use], *([into] if n_into else []), *deps)


def _rmsnorm_fwd(x, g, *, rows, name, bt=512):
    bt = min(bt, rows)

    def body(x_ref, g_ref, o_ref):
        xv = x_ref[...]
        r = lax.rsqrt(jnp.mean(xv * xv, axis=-1, keepdims=True) + EPS)
        o_ref[...] = (xv * r * g_ref[...]).astype(o_ref.dtype)

    return pl.pallas_call(
        body, name=name, grid=(rows // bt,),
        in_specs=[pl.BlockSpec((bt, D_MODEL), lambda i: (i, 0)), pl.BlockSpec((1, D_MODEL), lambda i: (0, 0))],
        out_specs=pl.BlockSpec((bt, D_MODEL), lambda i: (i, 0)),
        out_shape=jax.ShapeDtypeStruct((rows, D_MODEL), MXU_DTYPE),
        compiler_params=_cparams(("parallel",)),
    )(x, g.reshape(1, D_MODEL))


def _rms_bwd_tile(xv, gv, dyv):
    r = lax.rsqrt(jnp.mean(xv * xv, axis=-1, keepdims=True) + EPS)
    w = dyv * gv
    dx = r * w - xv * (r * r * r) * jnp.mean(w * xv, axis=-1, keepdims=True)
    dg = _colsum(dyv * xv * r)
    return dx, dg


def _residual_then_norm(acc, x_t, g):
    x1 = x_t + acc
    r = lax.rsqrt(jnp.mean(x1 * x1, axis=-1, keepdims=True) + EPS)
    return (x1, x1 * r * g), ()


def _residual_then_loss(acc, x_t, tgt_t, g):
    x2 = x_t + acc
    r = lax.rsqrt(jnp.mean(x2 * x2, axis=-1, keepdims=True) + EPS)
    diff = x2 * r * g - tgt_t
    part = jnp.sum(jnp.mean(diff * diff, axis=-1, keepdims=True), axis=0, keepdims=True) * 0.5
    dx, dg = _rms_bwd_tile(x2, g, diff * (1.0 / D_MODEL))
    return (dx, dx), (part, dg)


def _norm_bwd_then_residual(n_out):
    def fn(acc, x_t, res_t, g):
        dx, dg = _rms_bwd_tile(x_t, g, acc)
        return (dx + res_t,) * n_out, (dg,)

    return fn


def _rmsnorm_bwd(x, g, dy, res, *, rows, name, bt=512, dx_dtypes=(F32,)):
    bt = min(bt, rows)
    has_res = res is not None

    def body(*refs):
        x_ref, g_ref, dy_ref = refs[:3]
        res_ref = refs[3] if has_res else None
        outs = refs[3 + int(has_res):]
        dx, dg = _rms_bwd_tile(x_ref[...], g_ref[...], dy_ref[...])
        if has_res:
            dx = dx + res_ref[...]
        dg_ref = outs[-1]

        @pl.when(pl.program_id(0) == 0)
        def _():
            dg_ref[...] = jnp.zeros_like(dg_ref)

        dg_ref[...] += dg
        for o in outs[:-1]:
            o[...] = dx.astype(o.dtype)

    row_spec = pl.BlockSpec((bt, D_MODEL), lambda i: (i, 0))
    vec_spec = pl.BlockSpec((1, D_MODEL), lambda i: (0, 0))
    ins = [x, g.reshape(1, D_MODEL), dy] + ([res] if has_res else [])
    return pl.pallas_call(
        body, name=name, grid=(rows // bt,),
        in_specs=[row_spec, vec_spec, row_spec] + ([row_spec] if has_res else []),
        out_specs=[row_spec] * len(dx_dtypes) + [vec_spec],
        out_shape=[jax.ShapeDtypeStruct((rows, D_MODEL), dt) for dt in dx_dtypes] + [jax.ShapeDtypeStruct((1, D_MODEL), F32)],
        compiler_params=_cparams(("arbitrary",)),
    )(*ins)


LRU_T = 512
SCAN_GROUPS = 4


def _gelu(x):
    t = jnp.tanh(GELU_K * (x + 0.044715 * x * x * x))
    return 0.5 * x * (1.0 + t), t


def _gelu_grad(x, t):
    return 0.5 * (1.0 + t) + 0.5 * x * (1.0 - t * t) * GELU_K * (1.0 + 3.0 * 0.044715 * x * x)


def _softplus_neg(lam):
    z = -lam
    u = jnp.exp(-jnp.abs(z))
    w = 1.0 + u
    l1p = jnp.where(w == 1.0, u, jnp.log(w) * u / jnp.where(w == 1.0, 1.0, w - 1.0))
    return jnp.maximum(z, 0.0) + l1p


def _shift_down(cur, prev8, k, row8):
    y = pltpu.roll(cur, k, 0)
    head = jnp.where(row8 < k, pltpu.roll(prev8, k, 0), y[0:8])
    return jnp.concatenate([head, y[8:]], axis=0)


def _shift_up(cur, next8, k, row8):
    n = cur.shape[0]
    y = pltpu.roll(cur, n - k, 0)
    tail = jnp.where(row8 >= 8 - k, pltpu.roll(next8, 8 - k, 0), y[n - 8:n])
    return jnp.concatenate([y[0:n - 8], tail], axis=0)


def _lru_gates(xl, p8, cw, cb, wa, wx, ba, bx, lam, row8, a_mult=None):
    sh = [xl] + [_shift_down(xl, p8, k, row8) for k in (1, 2, 3)]
    xc = cb + cw[3:4] * sh[0] + cw[2:3] * sh[1] + cw[1:2] * sh[2] + cw[0:1] * sh[3]
    r = jax.nn.sigmoid(_dot(xc, wa) + ba)
    i = jax.nn.sigmoid(_dot(xc, wx) + bx)
    sp = _softplus_neg(lam)
    if a_mult is None:
        la = -LRU_C * r * sp
        a = jnp.exp(la)
        mult = jnp.sqrt(jnp.tanh(-la) * (a * a + 1.0))
    else:
        a, mult = a_mult
    return dict(sh=sh, xc=xc, r=r, i=i, sp=sp, a=a, mult=mult)


def _lru_specs(n_t, reverse):
    T = LRU_T
    tt = (lambda t: n_t - 1 - t) if reverse else (lambda t: t)
    blk = lambda col0: pl.BlockSpec((T, LRU_GROUP), lambda g, t: (tt(t), col0 + g))
    prev8 = lambda col0: pl.BlockSpec((8, LRU_GROUP), lambda g, t: (jnp.maximum(tt(t) * (T // 8) - 1, 0), col0 + g))
    vec = lambda rows: pl.BlockSpec((rows, LRU_GROUP), lambda g, t: (0, g))
    wbd = pl.BlockSpec((1, LRU_GROUP, LRU_GROUP), lambda g, t: (g, 0, 0))
    return blk, prev8, vec, wbd


def _lru_fwd(proj, conv_w, conv_b, wa_bd, wx_bd, b_a, b_x, lam, *, S):
    T = LRU_T
    n_t = S // T
    blk, _, vec, wbd = _lru_specs(n_t, False)

    def body(xl_ref, gate_ref, cw_ref, cb_ref, wa_ref, wx_ref, ba_ref, bx_ref, lam_ref,
             hl_ref, z_ref, a_s, m_ref, prev8, hcar, b_s):
        @pl.when(pl.program_id(1) == 0)
        def _():
            prev8[...] = jnp.zeros_like(prev8)
            hcar[...] = jnp.zeros_like(hcar)

        row8 = lax.broadcasted_iota(jnp.int32, (8, LRU_GROUP), 0)
        xl = xl_ref[...]
        q = _lru_gates(xl, prev8[...], cw_ref[...], cb_ref[...], wa_ref[0], wx_ref[0], ba_ref[...], bx_ref[...],
                       lam_ref[...], row8)
        prev8[...] = xl[T - 8:T]
        a_s[...] = q["a"]
        m_ref[...] = q["mult"]
        b_s[...] = q["mult"] * q["i"] * q["xc"]

        def step(c, carry):
            local = []
            for u in range(SCAN_GROUPS):
                off = pl.multiple_of((c * SCAN_GROUPS + u) * 8, 8)
                A = a_s[pl.ds(off, 8), :]
                B = b_s[pl.ds(off, 8), :]
                for k in (1, 2, 4):
                    a_sh = jnp.where(row8 >= k, pltpu.roll(A, k, 0), 1.0)
                    b_sh = jnp.where(row8 >= k, pltpu.roll(B, k, 0), 0.0)
                    B = A * b_sh + B
                    A = A * a_sh
                local.append((off, A, B))
            for off, A, B in local:
                h = A * carry + B
                hl_ref[pl.ds(off, 8), :] = h
                carry = h[7:8, :]
            return carry

        hcar[...] = lax.fori_loop(0, T // (8 * SCAN_GROUPS), step, hcar[...])
        ge, _ = _gelu(gate_ref[...])
        z_ref[...] = (ge * hl_ref[...]).astype(z_ref.dtype)

    return pl.pallas_call(
        body, name="lru_fwd", grid=(N_LRU_GROUPS, n_t),
        in_specs=[blk(C_XL // LRU_GROUP), blk(C_GATE // LRU_GROUP), vec(4), vec(1), wbd, wbd, vec(1), vec(1), vec(1)],
        out_specs=[blk(0)] * 4,
        out_shape=[jax.ShapeDtypeStruct((S, D_RNN), F32), jax.ShapeDtypeStruct((S, D_RNN), MXU_DTYPE),
                   jax.ShapeDtypeStruct((S, D_RNN), F32), jax.ShapeDtypeStruct((S, D_RNN), F32)],
        scratch_shapes=[pltpu.VMEM((8, LRU_GROUP), F32), pltpu.VMEM((1, LRU_GROUP), F32), pltpu.VMEM((T, LRU_GROUP), F32)],
        compiler_params=_cparams(("parallel", "arbitrary")),
    )(proj, proj, conv_w, conv_b, wa_bd, wx_bd, b_a, b_x, lam)


def _lru_bwd(proj, hl, a_fwd, mult_fwd, dy, w_out, conv_w, conv_b, wa_bd, wx_bd, b_a, b_x, lam, *, S):
    T = LRU_T
    n_t = S // T
    blk, prev8s, vec, wbd = _lru_specs(n_t, True)

    def body(xl_ref, xlp_ref, gate_ref, hl_ref, hlp_ref, a_ref, m_ref, dy_ref, wo_ref, cw_ref, cb_ref, wa_ref, wx_ref,
             ba_ref, bx_ref, lam_ref, dxl_ref, dgate_ref, dcw_ref, dcb_ref, dwa_ref, dwx_ref, dba_ref, dbx_ref, dlam_ref,
             next8, gcar, c_s, b_s, l_s):
        t = pl.program_id(1)
        first_chunk = t == n_t - 1

        @pl.when(t == 0)
        def _():
            next8[...] = jnp.zeros_like(next8)
            gcar[...] = jnp.zeros_like(gcar)
            for ref in (dcw_ref, dcb_ref, dwa_ref, dwx_ref, dba_ref, dbx_ref, dlam_ref):
                ref[...] = jnp.zeros_like(ref)

        row8 = lax.broadcasted_iota(jnp.int32, (8, LRU_GROUP), 0)
        rowT = lax.broadcasted_iota(jnp.int32, (T, LRU_GROUP), 0)
        keep = jnp.where(first_chunk, 0.0, 1.0)
        xl = xl_ref[...]
        wa, wx, lam_v = wa_ref[0], wx_ref[0], lam_ref[...]
        q = _lru_gates(xl, xlp_ref[...] * keep, cw_ref[...], cb_ref[...], wa, wx, ba_ref[...], bx_ref[...], lam_v, row8,
                       a_mult=(a_ref[...], m_ref[...]))
        a, mult, r, i, xc, sp = q["a"], q["mult"], q["r"], q["i"], q["xc"], q["sp"]
        hl_v = hl_ref[...]
        dz_v = _dot(dy_ref[...], wo_ref[...], "nt")
        gate = gate_ref[...]
        ge, th = _gelu(gate)
        dgate_ref[...] = (dz_v * hl_v * _gelu_grad(gate, th)).astype(dgate_ref.dtype)

        c_s[...] = jnp.where(rowT == T - 1, 0.0, pltpu.roll(a, T - 1, 0))
        b_s[...] = dz_v * ge + jnp.where(rowT == T - 1, gcar[...], 0.0)

        def step(n, carry):
            local = []
            for u in range(SCAN_GROUPS):
                off = pl.multiple_of((T // 8 - 1 - (n * SCAN_GROUPS + u)) * 8, 8)
                C = c_s[pl.ds(off, 8), :]
                B = b_s[pl.ds(off, 8), :]
                for k in (1, 2, 4):
                    c_sh = jnp.where(row8 < 8 - k, pltpu.roll(C, 8 - k, 0), 1.0)
                    b_sh = jnp.where(row8 < 8 - k, pltpu.roll(B, 8 - k, 0), 0.0)
                    B = B + C * b_sh
                    C = C * c_sh
                local.append((off, C, B))
            for off, C, B in local:
                lam_t = B + C * carry
                l_s[pl.ds(off, 8), :] = lam_t
                carry = lam_t[0:1, :]
            return carry

        lax.fori_loop(0, T // (8 * SCAN_GROUPS), step, jnp.zeros((1, LRU_GROUP), F32))
        lmb = l_s[...]
        gcar[...] = a[0:1, :] * lmb[0:1, :]

        h_prev = _shift_down(hl_v, hlp_ref[...] * keep, 1, row8)
        da = lmb * h_prev
        dmult = lmb * i * xc
        di = lmb * mult * xc
        dxc = lmb * mult * i
        dla = da * a - dmult * (a * a) / mult
        dr = dla * (-LRU_C * sp)
        dlam_ref[...] += _colsum(dla * (-LRU_C * r)) * (-jax.nn.sigmoid(-lam_v))
        dpa = dr * r * (1.0 - r)
        dpx = di * i * (1.0 - i)
        dxc = dxc + _dot(dpa, wa, "nt") + _dot(dpx, wx, "nt")
        dwa_ref[0] += _dot(xc, dpa, "tn")
        dwx_ref[0] += _dot(xc, dpx, "tn")
        dba_ref[...] += _colsum(dpa)
        dbx_ref[...] += _colsum(dpx)
        dcb_ref[...] += _colsum(dxc)
        cw = cw_ref[...]
        n8 = next8[...]
        dxl = cw[3:4] * dxc
        for k in (1, 2, 3):
            dxl = dxl + cw[3 - k:4 - k] * _shift_up(dxc, n8, k, row8)
        for k in range(4):
            dcw_ref[3 - k:4 - k, :] += _colsum(dxc * q["sh"][k])
        next8[...] = dxc[0:8]
        dxl_ref[...] = dxl.astype(dxl_ref.dtype)

    res = pl.pallas_call(
        body, name="lru_bwd", grid=(N_LRU_GROUPS, n_t),
        in_specs=[blk(C_XL // LRU_GROUP), prev8s(C_XL // LRU_GROUP), blk(C_GATE // LRU_GROUP), blk(0), prev8s(0), blk(0),
                  blk(0), pl.BlockSpec((T, D_MODEL), lambda g, t: (n_t - 1 - t, 0)),
                  pl.BlockSpec((LRU_GROUP, D_MODEL), lambda g, t: (g, 0)), vec(4), vec(1), wbd, wbd, vec(1), vec(1), vec(1)],
        out_specs=[blk(0), blk(0), vec(4), vec(1), wbd, wbd, vec(1), vec(1), vec(1)],
        out_shape=[jax.ShapeDtypeStruct((S, D_RNN), MXU_DTYPE), jax.ShapeDtypeStruct((S, D_RNN), MXU_DTYPE),
                   jax.ShapeDtypeStruct((4, D_RNN), F32), jax.ShapeDtypeStruct((1, D_RNN), F32),
                   jax.ShapeDtypeStruct((N_LRU_GROUPS, LRU_GROUP, LRU_GROUP), F32),
                   jax.ShapeDtypeStruct((N_LRU_GROUPS, LRU_GROUP, LRU_GROUP), F32),
                   jax.ShapeDtypeStruct((1, D_RNN), F32), jax.ShapeDtypeStruct((1, D_RNN), F32),
                   jax.ShapeDtypeStruct((1, D_RNN), F32)],
        scratch_shapes=[pltpu.VMEM((8, LRU_GROUP), F32), pltpu.VMEM((1, LRU_GROUP), F32),
                        pltpu.VMEM((T, LRU_GROUP), F32), pltpu.VMEM((T, LRU_GROUP), F32), pltpu.VMEM((T, LRU_GROUP), F32)],
        compiler_params=_cparams(("parallel", "arbitrary")),
    )(proj, proj, proj, hl, hl, a_fwd, mult_fwd, dy, w_out, conv_w, conv_b, wa_bd, wx_bd, b_a, b_x, lam)
    return res


def _block_diag(w):
    w4 = w.reshape(N_LRU_GROUPS, 4, LRU_BLOCK, 1, LRU_BLOCK)
    eye = jnp.eye(4, dtype=w.dtype).reshape(1, 4, 1, 4, 1)
    return (w4 * eye).reshape(N_LRU_GROUPS, LRU_GROUP, LRU_GROUP)


def _block_diag_extract(wbd):
    w5 = wbd.reshape(N_LRU_GROUPS, 4, LRU_BLOCK, 4, LRU_BLOCK)
    return jnp.stack([w5[:, a, :, a, :] for a in range(4)], axis=1).reshape(N_LRU_BLOCKS, LRU_BLOCK, LRU_BLOCK)


def _t5_bucket(dist):
    max_exact = NUM_BUCKETS // 2
    df = jnp.maximum(dist, 1).astype(jnp.float32)
    large = max_exact + (jnp.log(df / max_exact) / math.log(MAX_DISTANCE / max_exact)
                         * (NUM_BUCKETS - max_exact)).astype(jnp.int32)
    large = jnp.minimum(large, NUM_BUCKETS - 1)
    return jnp.where(dist < max_exact, dist, large)


def _band_offsets():
    qi = jnp.arange(SPAN)[:, None]
    kj = jnp.arange(2 * SPAN)[None, :]
    return qi + SPAN - kj


def _dil_buckets():
    off = _band_offsets()
    return jnp.stack([_t5_bucket(jnp.maximum(off, 0) * dil) for _, dil in DIL_GROUPS]).astype(jnp.int32)


def _dil_bias(rel_bias, buckets):
    def body(tbl_ref, bk_ref, o_ref):
        g = pl.program_id(0)
        qi = lax.broadcasted_iota(jnp.int32, (SPAN, 2 * SPAN), 0)
        kj = lax.broadcasted_iota(jnp.int32, (SPAN, 2 * SPAN), 1)
        off = qi + SPAN - kj
        valid = (off >= 0) & (off <= SPAN)
        bk = bk_ref[0]
        for h in range(DIL_HEADS):
            acc = jnp.zeros((SPAN, 2 * SPAN), F32)
            for b in range(NUM_BUCKETS):
                acc = jnp.where(bk == b, tbl_ref[b, g * DIL_HEADS + h], acc)
            o_ref[0, h] = jnp.where(valid, acc, NEG)

    return pl.pallas_call(
        body, name="dil_bias", grid=(3,),
        in_specs=[pl.BlockSpec(memory_space=pltpu.SMEM), pl.BlockSpec((1, SPAN, 2 * SPAN), lambda g: (g, 0, 0))],
        out_specs=pl.BlockSpec((1, DIL_HEADS, SPAN, 2 * SPAN), lambda g: (g, 0, 0, 0)),
        out_shape=jax.ShapeDtypeStruct((3, DIL_HEADS, SPAN, 2 * SPAN), F32),
        compiler_params=_cparams(("parallel",)),
    )(rel_bias, buckets)


def _dil_bias_bwd(dbias, buckets):
    def body(db_ref, bk_ref, o_ref):
        lane = lax.broadcasted_iota(jnp.int32, (1, 128), 1)
        rows = [jnp.zeros((1, 128), F32) for _ in range(NUM_BUCKETS)]
        for g in range(3):
            bk = bk_ref[g]
            for h in range(DIL_HEADS):
                d = db_ref[g, h]
                for b in range(NUM_BUCKETS):
                    tot = jnp.sum(_colsum(jnp.where(bk == b, d, 0.0)), axis=1, keepdims=True)
                    rows[b] = jnp.where(lane == g * DIL_HEADS + h, tot, rows[b])
        for b in range(NUM_BUCKETS):
            o_ref[b:b + 1, :] = rows[b]

    return pl.pallas_call(
        body, name="dil_bias_bwd",
        out_shape=jax.ShapeDtypeStruct((NUM_BUCKETS, 128), F32),
        compiler_params=_cparams(),
    )(dbias, buckets)


DIL_SUBBLOCKS = (8, 4, 1)


def _dil_layout(g, S):
    dil, m = DIL_GROUPS[g][1], DIL_SUBBLOCKS[g]
    sub = SPAN * dil
    col = [(C_QKV + t * 768 + g * 256) // 128 for t in range(3)]
    return dil, m, sub, S // (sub * m), col


def _residue_rows(b, r, dil):
    return pl.ds(b * SPAN * dil + r, SPAN, stride=dil) if dil > 1 else pl.ds(b * SPAN, SPAN)


def _for_residues(dil, fn):
    if dil <= 4:
        for r in range(dil):
            fn(r)
    else:
        lax.fori_loop(0, dil, lambda r, c: (fn(r), c)[1], 0, unroll=4)


def _pair_scores(qm, k2, bias, first_cols):
    s = _dot(qm, k2, "nt") * (DIL_HEAD_DIM ** -0.5) + bias
    kj = lax.broadcasted_iota(jnp.int32, s.shape, 1)
    return jnp.where(kj < first_cols, NEG, s)


def _dilated_fwd(proj, bias, g, *, S):
    dil, m, sub, nc, (qc, kc, vc) = _dil_layout(g, S)
    R = sub * m
    cur = lambda cb: pl.BlockSpec((R, 128), lambda p, i: (i, cb + p))
    prv = lambda cb: pl.BlockSpec((sub, 128), lambda p, i: (jnp.maximum(i * m - 1, 0), cb + p))
    out = pl.BlockSpec((R, 128), lambda p, i: (i, p))

    def body(q_ref, kp_ref, kc_ref, vp_ref, vc_ref, b_ref, o_ref, lse_ref):
        lane = lax.broadcasted_iota(jnp.int32, (SPAN, 128), 1)
        sels = (lane < DIL_HEAD_DIM, lane >= DIL_HEAD_DIM)
        for b in range(m):
            first_cols = jnp.where(pl.program_id(1) == 0, SPAN, 0) if b == 0 else 0

            def one(r, b=b, first_cols=first_cols):
                rows = _residue_rows(b, r, dil)
                before = (kc_ref, vc_ref, _residue_rows(b - 1, r, dil)) if b else (kp_ref, vp_ref, _residue_rows(0, r, dil))
                q2 = q_ref[rows, :]
                k2 = _mx(jnp.concatenate([before[0][before[2], :], kc_ref[rows, :]], axis=0))
                v2 = _mx(jnp.concatenate([before[1][before[2], :], vc_ref[rows, :]], axis=0))
                qq = jnp.concatenate([jnp.where(sels[0], q2, 0.0), jnp.where(sels[1], q2, 0.0)], axis=0)
                s = _pair_scores(qq, k2, b_ref[0, 0], first_cols)
                mx = jnp.max(s, axis=-1, keepdims=True)
                p = jnp.exp(s - mx)
                den = jnp.sum(p, axis=-1, keepdims=True)
                o = _dot(p, v2) / den
                st = mx + jnp.log(den)
                o_ref[rows, :] = jnp.where(sels[0], o[0:SPAN], o[SPAN:2 * SPAN])
                lse_ref[rows, :] = jnp.where(lane == 0, st[0:SPAN], jnp.where(lane == 1, st[SPAN:2 * SPAN], 0.0))

            _for_residues(dil, one)

    return pl.pallas_call(
        body, name=f"dil_fwd{g}", grid=(2, nc),
        in_specs=[cur(qc), prv(kc), cur(kc), prv(vc), cur(vc),
                  pl.BlockSpec((1, 1, 2 * SPAN, 2 * SPAN), lambda p, i: (g, p, 0, 0))],
        out_specs=[out, out],
        out_shape=[jax.ShapeDtypeStruct((S, 256), F32), jax.ShapeDtypeStruct((S, 256), F32)],
        compiler_params=_cparams(("parallel", "parallel")),
    )(proj, proj, proj, proj, proj, bias.reshape(3, 2, 2 * SPAN, 2 * SPAN))


def _dilated_bwd(proj, do, lse, delta, bias, g, *, S, into=None):
    dil, m, sub, nc, (qc, kc, vc) = _dil_layout(g, S)
    R = sub * m
    cl = lambda i: jnp.minimum(i, nc - 1)
    cur = lambda cb: pl.BlockSpec((R, 128), lambda p, i: (cl(i), cb + p))
    prv = lambda cb: pl.BlockSpec((sub, 128), lambda p, i: (jnp.maximum(cl(i) * m - 1, 0), cb + p))
    q_out = pl.BlockSpec((R, 128), lambda p, i: (cl(i), 2 * g + p))
    kv_out = pl.BlockSpec((R, 128), lambda p, i: (jnp.maximum(i - 1, 0), 2 * g + p))
    scale = DIL_HEAD_DIM ** -0.5
    n_into = 0 if into is None else 3

    def body(q_ref, kp_ref, kc_ref, vp_ref, vc_ref, do_ref, lse_ref, dl_ref, b_ref, *rest):
        dq_ref, dk_ref, dv_ref, db_ref, dq_s, kc_s, vc_s, kp_s, vp_s, kcar, vcar = rest[n_into:]
        i = pl.program_id(1)

        @pl.when(i == 0)
        def _():
            kcar[...] = jnp.zeros_like(kcar)
            vcar[...] = jnp.zeros_like(vcar)
            db_ref[...] = jnp.zeros_like(db_ref)

        @pl.when(i < nc)
        def _():
            lane = lax.broadcasted_iota(jnp.int32, (SPAN, 128), 1)
            sels = (lane < DIL_HEAD_DIM, lane >= DIL_HEAD_DIM)
            for b in range(m):
                first_cols = jnp.where(i == 0, SPAN, 0) if b == 0 else 0

                def one(r, b=b, first_cols=first_cols):
                    rows = _residue_rows(b, r, dil)
                    rows_before = _residue_rows(b - 1 if b else 0, r, dil)
                    k_before, v_before = (kc_ref, vc_ref) if b else (kp_ref, vp_ref)
                    q2, do2 = q_ref[rows, :], do_ref[rows, :]
                    k2 = _mx(jnp.concatenate([k_before[rows_before, :], kc_ref[rows, :]], axis=0))
                    v2 = _mx(jnp.concatenate([v_before[rows_before, :], vc_ref[rows, :]], axis=0))
                    lse_t, dl_t = lse_ref[rows, :], dl_ref[rows, :]
                    qq = _mx(jnp.concatenate([jnp.where(sels[0], q2, 0.0), jnp.where(sels[1], q2, 0.0)], axis=0))
                    dd = _mx(jnp.concatenate([jnp.where(sels[0], do2, 0.0), jnp.where(sels[1], do2, 0.0)], axis=0))
                    lse2 = jnp.concatenate([lse_t[:, 0:1], lse_t[:, 1:2]], axis=0)
                    dl2 = jnp.concatenate([dl_t[:, 0:1], dl_t[:, 1:2]], axis=0)
                    p = jnp.exp(_pair_scores(qq, k2, b_ref[0, 0], first_cols) - lse2)
                    ds = p * (_dot(dd, v2, "nt") - dl2)
                    db_ref[0] += ds
                    dqq = _dot(ds, k2) * scale
                    dq2 = jnp.where(sels[0], dqq[0:SPAN], dqq[SPAN:2 * SPAN])
                    dk2 = _dot(ds, qq, "tn") * scale
                    dv2 = _dot(p, dd, "tn")
                    dq_s[rows, :] = dq2
                    kc_s[rows, :] = dk2[SPAN:2 * SPAN]
                    vc_s[rows, :] = dv2[SPAN:2 * SPAN]
                    if b:
                        kc_s[rows_before, :] += dk2[0:SPAN]
                        vc_s[rows_before, :] += dv2[0:SPAN]
                    else:
                        kp_s[rows_before, :] = dk2[0:SPAN]
                        vp_s[rows_before, :] = dv2[0:SPAN]

                _for_residues(dil, one)
            dq_ref[...] = dq_s[...].astype(dq_ref.dtype)
            last = pl.ds((m - 1) * sub, sub)
            kcar[last, :] += kp_s[...]
            vcar[last, :] += vp_s[...]
            dk_ref[...] = kcar[...].astype(dk_ref.dtype)
            dv_ref[...] = vcar[...].astype(dv_ref.dtype)
            kcar[...] = kc_s[...]
            vcar[...] = vc_s[...]

        @pl.when(i == nc)
        def _():
            dk_ref[...] = kcar[...].astype(dk_ref.dtype)
            dv_ref[...] = vcar[...].astype(dv_ref.dtype)

    stat = pl.BlockSpec((R, 128), lambda p, i: (cl(i), p))
    big = jax.ShapeDtypeStruct((S, len(DIL_GROUPS) * 256), MXU_DTYPE)
    return pl.pallas_call(
        body, name=f"dil_bwd{g}", grid=(2, nc + 1),
        in_specs=[cur(qc), prv(kc), cur(kc), prv(vc), cur(vc), stat, stat, stat,
                  pl.BlockSpec((1, 1, 2 * SPAN, 2 * SPAN), lambda p, i: (g, p, 0, 0))]
        + [pl.BlockSpec(memory_space=pl.ANY)] * n_into,
        out_specs=[q_out, kv_out, kv_out, pl.BlockSpec((1, 2 * SPAN, 2 * SPAN), lambda p, i: (p, 0, 0))],
        out_shape=[big, big, big, jax.ShapeDtypeStruct((2, 2 * SPAN, 2 * SPAN), F32)],
        input_output_aliases={9 + j: j for j in range(n_into)},
        scratch_shapes=[pltpu.VMEM((R, 128), F32)] * 3 + [pltpu.VMEM((sub, 128), F32)] * 2 + [pltpu.VMEM((R, 128), F32)] * 2,
        compiler_params=_cparams(("parallel", "arbitrary")),
    )(proj, proj, proj, proj, proj, do, lse, delta, bias.reshape(3, 2, 2 * SPAN, 2 * SPAN), *(into or ()))


def _dilated_merge(os_, lses, *, S, bt=512):
    tile = pl.BlockSpec((bt, 128), lambda i, p: (i, p))

    def body(o0, o1, o2, l0, l1, l2, o_ref, om_ref, lse_ref):
        lane = lax.broadcasted_iota(jnp.int32, (bt, 128), 1)
        lo = lane < DIL_HEAD_DIM
        ls = [l0[...], l1[...], l2[...]]
        ws, stat = [], jnp.zeros((bt, 128), F32)
        for e in range(2):
            a = [l[:, e:e + 1] for l in ls]
            m = jnp.maximum(jnp.maximum(a[0], a[1]), a[2])
            ex = [jnp.exp(v - m) for v in a]
            tot = ex[0] + ex[1] + ex[2]
            ws.append([v / tot for v in ex])
            stat = jnp.where(lane == e, m + jnp.log(tot), stat)
        acc = jnp.zeros((bt, 128), F32)
        for gi, o in enumerate((o0, o1, o2)):
            acc = acc + jnp.where(lo, ws[0][gi], ws[1][gi]) * o[...]
        o_ref[...] = acc
        om_ref[...] = _mx(acc)
        lse_ref[...] = stat

    return pl.pallas_call(
        body, name="dil_merge", grid=(S // bt, 2),
        in_specs=[tile] * 6, out_specs=[tile, tile, tile],
        out_shape=[jax.ShapeDtypeStruct((S, 256), F32), jax.ShapeDtypeStruct((S, 256), MXU_DTYPE),
                   jax.ShapeDtypeStruct((S, 256), F32)],
        compiler_params=_cparams(("parallel", "parallel")),
    )(*os_, *lses)


def _with_delta(do, o):
    lane = lax.broadcasted_iota(jnp.int32, (do.shape[0], 128), 1)
    stats = []
    for p in range(2):
        prod = do[:, 128 * p:128 * (p + 1)] * o[:, 128 * p:128 * (p + 1)]
        d0 = jnp.sum(jnp.where(lane < DIL_HEAD_DIM, prod, 0.0), axis=-1, keepdims=True)
        d1 = jnp.sum(jnp.where(lane >= DIL_HEAD_DIM, prod, 0.0), axis=-1, keepdims=True)
        stats.append(jnp.where(lane == 0, d0, jnp.where(lane == 1, d1, 0.0)))
    return do, jnp.concatenate(stats, axis=1)


MEM_T = 2048
QM_BLK = C_QM // MEM_HEAD_DIM


def _mem_attn_fwd(proj, kv, *, S):
    scale = MEM_HEAD_DIM ** -0.5

    def body(q_ref, k_ref, v_ref, o_ref, om_ref, lse_ref):
        s = _dot(q_ref[...], k_ref[...], "nt") * scale
        m = jnp.max(s, axis=-1, keepdims=True)
        p = jnp.exp(s - m)
        den = jnp.sum(p, axis=-1, keepdims=True)
        o = _dot(p, v_ref[...]) / den
        o_ref[...] = o
        om_ref[...] = _mx(o)
        lse_ref[0] = m + jnp.log(den)

    return pl.pallas_call(
        body, name="mem_attn_fwd", grid=(S // MEM_T, MEM_HEADS),
        in_specs=[pl.BlockSpec((MEM_T, MEM_HEAD_DIM), lambda i, h: (i, QM_BLK + h)),
                  pl.BlockSpec((N_MEM, MEM_HEAD_DIM), lambda i, h: (0, h)),
                  pl.BlockSpec((N_MEM, MEM_HEAD_DIM), lambda i, h: (0, MEM_HEADS + h))],
        out_specs=[pl.BlockSpec((MEM_T, MEM_HEAD_DIM), lambda i, h: (i, h)),
                   pl.BlockSpec((MEM_T, MEM_HEAD_DIM), lambda i, h: (i, h)),
                   pl.BlockSpec((1, MEM_T, 1), lambda i, h: (h, i, 0))],
        out_shape=[jax.ShapeDtypeStruct((S, MEM_WIDTH), F32), jax.ShapeDtypeStruct((S, MEM_WIDTH), MXU_DTYPE),
                   jax.ShapeDtypeStruct((MEM_HEADS, S, 1), F32)],
        compiler_params=_cparams(("parallel", "parallel")),
    )(proj, kv, kv)


def _mem_attn_bwd(proj, kv, om, lse, dy, w_out, *, S):
    scale = MEM_HEAD_DIM ** -0.5

    def body(q_ref, k_ref, v_ref, o_ref, lse_ref, dy_ref, wo_ref, dq_ref, dk_ref, dv_ref):
        @pl.when(pl.program_id(1) == 0)
        def _():
            dk_ref[...] = jnp.zeros_like(dk_ref)
            dv_ref[...] = jnp.zeros_like(dv_ref)

        qv, kv_, vv, dov = q_ref[...], k_ref[...], v_ref[...], _dot(dy_ref[...], wo_ref[...], "nt")
        p = jnp.exp(_dot(qv, kv_, "nt") * scale - lse_ref[0])
        delta = jnp.sum(dov * o_ref[...], axis=-1, keepdims=True)
        ds = p * (_dot(dov, vv, "nt") - delta)
        dq_ref[...] = (_dot(ds, kv_) * scale).astype(dq_ref.dtype)
        dk_ref[...] += _dot(ds, qv, "tn") * scale
        dv_ref[...] += _dot(p, dov, "tn")

    tile = pl.BlockSpec((MEM_T, MEM_HEAD_DIM), lambda h, i: (i, h))
    kvo = pl.BlockSpec((N_MEM, MEM_HEAD_DIM), lambda h, i: (0, h))
    return pl.pallas_call(
        body, name="mem_attn_bwd", grid=(MEM_HEADS, S // MEM_T),
        in_specs=[pl.BlockSpec((MEM_T, MEM_HEAD_DIM), lambda h, i: (i, QM_BLK + h)),
                  pl.BlockSpec((N_MEM, MEM_HEAD_DIM), lambda h, i: (0, h)),
                  pl.BlockSpec((N_MEM, MEM_HEAD_DIM), lambda h, i: (0, MEM_HEADS + h)),
                  tile, pl.BlockSpec((1, MEM_T, 1), lambda h, i: (h, i, 0)),
                  pl.BlockSpec((MEM_T, D_MODEL), lambda h, i: (i, 0)),
                  pl.BlockSpec((MEM_HEAD_DIM, D_MODEL), lambda h, i: (h, 0))],
        out_specs=[tile, kvo, kvo],
        out_shape=[jax.ShapeDtypeStruct((S, MEM_WIDTH), MXU_DTYPE), jax.ShapeDtypeStruct((N_MEM, MEM_WIDTH), F32),
                   jax.ShapeDtypeStruct((N_MEM, MEM_WIDTH), F32)],
        compiler_params=_cparams(("parallel", "arbitrary")),
    )(proj, kv, kv, om, lse, dy, w_out)


MIX_BM = 1024
MIX_BN = 256
GATES_BLK = C_GATES // MIX_BN


def _mix_specs(j_outer):
    ix = (lambda f: (lambda j, i: f(i, j))) if j_outer else (lambda f: f)
    act = lambda width: pl.BlockSpec((MIX_BM, width), ix(lambda i, j: (i, 0)))
    wgt = lambda width: pl.BlockSpec((width, MIX_BN), ix(lambda i, j: (0, j)))
    gate = lambda b: pl.BlockSpec((MIX_BM, MIX_BN), ix(lambda i, j: (i, GATES_BLK + 4 * b + j)))
    bias = lambda b: pl.BlockSpec((1, MIX_BN), ix(lambda i, j: (0, 4 * b + j)))
    tile = pl.BlockSpec((MIX_BM, MIX_BN), ix(lambda i, j: (i, j)))
    return act, wgt, gate, bias, tile


def _mix_fwd(z_lru, o_dil, om, w_lru, w_dil, w_mem, proj, b_gate, *, S):
    act, wgt, gate, bias, tile = _mix_specs(False)

    def body(zl, od, mo, wl, wd, wm, g0, g1, g2, b0, b1, b2, o_ref, t0, t1, t2):
        acc = None
        for a_ref, w_ref, g_ref, b_ref, t_ref in ((zl, wl, g0, b0, t0), (od, wd, g1, b1, t1), (mo, wm, g2, b2, t2)):
            gt = jax.nn.sigmoid(g_ref[...] + b_ref[...])
            t_ref[...] = gt.astype(t_ref.dtype)
            term = gt * _dot(a_ref[...], w_ref[...])
            acc = term if acc is None else acc + term
        o_ref[...] = acc.astype(o_ref.dtype)

    return pl.pallas_call(
        body, name="mix_fwd", grid=(S // MIX_BM, D_MODEL // MIX_BN),
        in_specs=[act(D_RNN), act(256), act(MEM_WIDTH), wgt(D_RNN), wgt(256), wgt(MEM_WIDTH),
                  gate(0), gate(1), gate(2), bias(0), bias(1), bias(2)],
        out_specs=[tile] * 4, out_shape=[jax.ShapeDtypeStruct((S, D_MODEL), MXU_DTYPE)] * 4,
        compiler_params=_cparams(("parallel", "parallel")),
    )(z_lru, o_dil, om, w_lru, w_dil, w_mem, proj, proj, proj, b_gate, b_gate, b_gate)


def _mix_bwd(dx1, w_out, z_lru, o_dil, om, w_lru, w_dil, w_mem, gates, *, S):
    act, wgt, _, _, tile = _mix_specs(False)
    n_j = D_MODEL // MIX_BN

    def body(dx, wo, zl, od, mo, wl, wd, wm, t0, t1, t2,
             dg0, dg1, dg2, dy0, dy1, dy2, db0, db1, db2):
        j = pl.program_id(1)

        @pl.when((pl.program_id(0) == 0) & (j == 0))
        def _():
            for r in (db0, db1, db2):
                r[...] = jnp.zeros_like(r)

        dmv = _dot(dx[...], wo[...], "nt")
        for act_ref, w_ref, t_ref, dg_ref, dy_ref, db_ref in (
                (zl, wl, t0, dg0, dy0, db0), (od, wd, t1, dg1, dy1, db1), (mo, wm, t2, dg2, dy2, db2)):
            y = _dot(act_ref[...], w_ref[...])
            gt = t_ref[...].astype(F32)
            dgate = dmv * y * gt * (1.0 - gt)
            dg_ref[...] = dgate.astype(dg_ref.dtype)
            dy_ref[...] = (dmv * gt).astype(dy_ref.dtype)
            db_ref[j] += _colsum(dgate)

    big = jax.ShapeDtypeStruct((S, D_MODEL), MXU_DTYPE)
    vec = jax.ShapeDtypeStruct((n_j, 1, MIX_BN), F32)
    vspec = pl.BlockSpec((n_j, 1, MIX_BN), lambda i, j: (0, 0, 0))
    res = pl.pallas_call(
        body, name="mix_bwd", grid=(S // MIX_BM, n_j),
        in_specs=[pl.BlockSpec((MIX_BM, D_MODEL), lambda i, j: (i, 0)), pl.BlockSpec((MIX_BN, D_MODEL), lambda i, j: (j, 0)),
                  act(D_RNN), act(256), act(MEM_WIDTH), wgt(D_RNN), wgt(256), wgt(MEM_WIDTH), tile, tile, tile],
        out_specs=[tile] * 6 + [vspec] * 3, out_shape=[big] * 6 + [vec] * 3,
        compiler_params=_cparams(("arbitrary", "arbitrary")),
    )(dx1, w_out, z_lru, o_dil, om, w_lru, w_dil, w_mem, *gates)
    return list(res[:6]) + [r.reshape(1, D_MODEL) for r in res[6:]]


def _adamw_math(w, g, m, v):
    m = ADAM_B1 * m + (1.0 - ADAM_B1) * g
    v = ADAM_B2 * v + (1.0 - ADAM_B2) * (g * g)
    m_hat = m / (1.0 - ADAM_B1 ** ADAM_STEP)
    v_hat = v / (1.0 - ADAM_B2 ** ADAM_STEP)
    delta = -ADAM_LR * (m_hat / (jnp.sqrt(v_hat) + ADAM_EPS) + ADAM_WD * w)
    return delta, m, v


def _adamw_landed(w, own, land, m, v, *, name, col_blk=0, prev=None):
    R = w.shape[0]
    n_parts, C = land.shape[0], land.shape[2]
    br = next(d for d in (256, 464, 128) if R % d == 0)
    tile = pl.BlockSpec((br, C), lambda i: (i, col_blk))
    part = pl.BlockSpec((br, C), lambda i: (i, 0))
    n_prev = 0 if prev is None else 4

    def body(w_ref, o_ref, l_ref, m_ref, v_ref, *rest):
        g_ref, d_ref, nm_ref, nv_ref = rest[n_prev:]
        g = o_ref[...].astype(F32)
        for p in range(n_parts):
            g = g + l_ref[p].astype(F32)
        d, nm, nv = _adamw_math(w_ref[...], g, m_ref[...], v_ref[...])
        g_ref[...] = g
        d_ref[...] = d
        nm_ref[...] = nm
        nv_ref[...] = nv

    return pl.pallas_call(
        body, name=name, grid=(R // br,),
        in_specs=[tile, part, pl.BlockSpec((n_parts, br, C), lambda i: (0, i, 0)), tile, tile]
        + [pl.BlockSpec(memory_space=pl.ANY)] * n_prev,
        out_specs=[tile] * 4, out_shape=[jax.ShapeDtypeStruct(w.shape, F32)] * 4,
        input_output_aliases={5 + j: j for j in range(n_prev)},
        compiler_params=_cparams(("parallel",)),
    )(w, own, land, m, v, *(prev or ()))


def _adamw_plain(w, g, m, v, *, name):
    def body(w_ref, g_ref, m_ref, v_ref, d_ref, nm_ref, nv_ref):
        d, nm, nv = _adamw_math(w_ref[...], g_ref[...], m_ref[...], v_ref[...])
        d_ref[...] = d
        nm_ref[...] = nm
        nv_ref[...] = nv

    return pl.pallas_call(
        body, name=name, out_shape=[jax.ShapeDtypeStruct(w.shape, F32)] * 3, compiler_params=_cparams(),
    )(w, g, m, v)


def _my_pos():
    return lax.axis_index("x"), lax.axis_index("y"), lax.axis_index("c")


def _dev_index(p):
    return 4 * p[0] + 2 * p[1] + p[2]


def _peers(me):
    x, y, c = me
    out = []
    for k in range(1, 8):
        fx, fy, fc = (k >> 2) & 1, (k >> 1) & 1, k & 1
        out.append((k - 1, (1 - x if fx else x, 1 - y if fy else y, 1 - c if fc else c)))
    return out


HBM_SPEC = pl.BlockSpec(memory_space=pltpu.HBM)
SEM_SPEC = pl.BlockSpec(memory_space=pltpu.SEMAPHORE)
DATAFLOW_EFFECT = pltpu.SideEffectType.DATAFLOW_SIDE_EFFECTING


def _gather_refs(src, land, me, peer, k):
    return src, land.at[_dev_index(me)]


def _scatter_refs(src, land, me, peer, k):
    return src.at[_dev_index(peer)], land.at[k]


ALL_RELATIONS = tuple(range(7))
ONE_PER_CHIP = (0, 1, 3, 5)


def _push_start(srcs, land_shapes, refs_of, name, after=(), relations=ALL_RELATIONS):
    n, n_after = len(srcs), len(after)

    def body(*refs):
        ins, lands = refs[:n], refs[n:2 * n]
        send_sems, recv_sems, token = refs[2 * n + n_after], refs[2 * n + n_after + 1], refs[-1]
        me = _my_pos()
        for k, peer in _peers(me):
            if k not in relations:
                continue
            for a in range(n):
                src, dst = refs_of(ins[a], lands[a], me, peer, k)
                pltpu.make_async_remote_copy(src_ref=src, dst_ref=dst, send_sem=send_sems.at[7 * a + k],
                                             recv_sem=recv_sems.at[7 * a + k], device_id=peer, device_id_type=MESH).start()
        token[...] = jnp.zeros_like(token)

    lands = [lax.empty(shp, s.dtype) for shp, s in zip(land_shapes, srcs)]
    hbm = lambda a: pltpu.with_memory_space_constraint(a, pltpu.HBM)
    res = pl.pallas_call(
        body, name=name,
        out_shape=(pltpu.SemaphoreType.DMA((7 * n,)), pltpu.SemaphoreType.DMA((7 * n,)),
                   *[pltpu.HBM(s.shape, s.dtype) for s in srcs], *[pltpu.HBM(l.shape, l.dtype) for l in lands],
                   jax.ShapeDtypeStruct((8, 128), F32)),
        in_specs=[HBM_SPEC] * (2 * n) + [pl.BlockSpec(memory_space=pl.ANY)] * n_after,
        out_specs=(SEM_SPEC, SEM_SPEC, *[HBM_SPEC] * (2 * n), pl.BlockSpec(memory_space=pltpu.VMEM)),
        input_output_aliases={i: 2 + i for i in range(2 * n)},
        compiler_params=pltpu.CompilerParams(has_side_effects=DATAFLOW_EFFECT),
    )(*[hbm(s) for s in srcs], *[hbm(l) for l in lands], *after)
    return dict(sems=(res[0], res[1]), srcs=list(res[2:2 + n]), lands=list(res[2 + n:2 + 2 * n]), token=res[-1], n=n,
                refs_of=refs_of, name=name, relations=relations)


def _push_wait(started, after):
    n, refs_of, relations = started["n"], started["refs_of"], started["relations"]
    after = list(after) if isinstance(after, (list, tuple)) else [after]

    def body(*refs):
        ins, lands = refs[:n], refs[n:2 * n]
        send_sems, recv_sems = refs[2 * n], refs[2 * n + 1]
        me = _my_pos()
        for k, peer in _peers(me):
            if k not in relations:
                continue
            for a in range(n):
                src, dst = refs_of(ins[a], lands[a], me, peer, k)
                cp = pltpu.make_async_remote_copy(src_ref=src, dst_ref=dst, send_sem=send_sems.at[7 * a + k],
                                                  recv_sem=recv_sems.at[7 * a + k], device_id=peer, device_id_type=MESH)
                cp.wait_send()
                cp.wait_recv()

    arrs = started["srcs"] + started["lands"]
    res = pl.pallas_call(
        body, name=started["name"].replace("start", "wait"),
        out_shape=tuple(pltpu.HBM(a.shape, a.dtype) for a in arrs),
        in_specs=[HBM_SPEC] * (2 * n) + [SEM_SPEC, SEM_SPEC] + [pl.BlockSpec(memory_space=pl.ANY)] * len(after),
        out_specs=tuple([HBM_SPEC] * (2 * n)),
        input_output_aliases={i: i for i in range(2 * n)},
        compiler_params=pltpu.CompilerParams(has_side_effects=DATAFLOW_EFFECT),
    )(*arrs, *started["sems"], *after)
    return list(res[n:2 * n])


def _other_chips(x, y):
    return ((1 - x, y), (x, 1 - y), (1 - x, 1 - y))


def _forward_start(land, name, after=()):
    n_after = len(after)

    def body(*refs):
        land_ref, send_sems, recv_sems, token = refs[0], refs[1 + n_after], refs[2 + n_after], refs[-1]
        x, y, c = _my_pos()
        for j, (cx, cy) in enumerate(_other_chips(x, y)):
            blk = land_ref.at[_dev_index((cx, cy, c))]
            pltpu.make_async_remote_copy(src_ref=blk, dst_ref=blk, send_sem=send_sems.at[j], recv_sem=recv_sems.at[j],
                                         device_id=(x, y, 1 - c), device_id_type=MESH).start()
        token[...] = jnp.zeros_like(token)

    res = pl.pallas_call(
        body, name=name,
        out_shape=(pltpu.SemaphoreType.DMA((3,)), pltpu.SemaphoreType.DMA((3,)), pltpu.HBM(land.shape, land.dtype),
                   jax.ShapeDtypeStruct((8, 128), F32)),
        in_specs=[HBM_SPEC] + [pl.BlockSpec(memory_space=pl.ANY)] * n_after,
        out_specs=(SEM_SPEC, SEM_SPEC, HBM_SPEC, pl.BlockSpec(memory_space=pltpu.VMEM)),
        input_output_aliases={0: 2},
        compiler_params=pltpu.CompilerParams(has_side_effects=DATAFLOW_EFFECT),
    )(pltpu.with_memory_space_constraint(land, pltpu.HBM), *after)
    return dict(sems=(res[0], res[1]), land=res[2], token=res[3], name=name)


def _forward_wait(started, after):
    after = list(after) if isinstance(after, (list, tuple)) else [after]

    def body(land_ref, send_sems, recv_sems, *rest):
        x, y, c = _my_pos()
        for j, (cx, cy) in enumerate(_other_chips(x, y)):
            cp = pltpu.make_async_remote_copy(
                src_ref=land_ref.at[_dev_index((cx, cy, c))], dst_ref=land_ref.at[_dev_index((cx, cy, 1 - c))],
                send_sem=send_sems.at[j], recv_sem=recv_sems.at[j], device_id=(x, y, 1 - c), device_id_type=MESH)
            cp.wait_send()
            cp.wait_recv()

    land = started["land"]
    return pl.pallas_call(
        body, name=started["name"].replace("start", "wait"), out_shape=pltpu.HBM(land.shape, land.dtype),
        in_specs=[HBM_SPEC, SEM_SPEC, SEM_SPEC] + [pl.BlockSpec(memory_space=pl.ANY)] * len(after),
        out_specs=HBM_SPEC, input_output_aliases={0: 0},
        compiler_params=pltpu.CompilerParams(has_side_effects=DATAFLOW_EFFECT),
    )(land, *started["sems"], *after)


def _sum_slots(slots):
    def body(in_ref, out_ref):
        acc = in_ref[0]
        for d in range(1, N_DEV):
            acc = acc + in_ref[d]
        out_ref[...] = acc

    return pl.pallas_call(body, name="sum_small", out_shape=jax.ShapeDtypeStruct(slots.shape[1:], F32),
                          compiler_params=_cparams())(slots)


def _adamw_many(ws, gs, ms, vs):
    n = len(ws)

    def body(*refs):
        for i in range(n):
            w_ref, g_ref, m_ref, v_ref = (refs[j * n + i] for j in range(4))
            d_, nm, nv = _adamw_math(w_ref[...], g_ref[...], m_ref[...], v_ref[...])
            for j, val in enumerate((d_, nm, nv)):
                refs[(4 + j) * n + i][...] = val

    res = pl.pallas_call(body, name="adamw_small", out_shape=[jax.ShapeDtypeStruct(w_.shape, F32) for w_ in ws] * 3,
                         compiler_params=_cparams())(*ws, *gs, *ms, *vs)
    return [(res[i], res[n + i], res[2 * n + i]) for i in range(n)]


def _local_step(x, mem, tgt, W, P, late_weights, send_grads, reduce_small, tie0):
    S = x.shape[0]
    W = dict(W)
    h = _rmsnorm_fwd(x, P["g_mix"] + tie0, rows=S, name="norm_mix")
    mem_n = _rmsnorm_fwd(mem, P["g_mem"], rows=N_MEM, name="norm_mem")
    buckets = _dil_buckets()
    bias = _dil_bias(P["rel_bias"], buckets)
    wa_bd, wx_bd = _mx(_block_diag(P["w_rg_a"])), _mx(_block_diag(P["w_rg_x"]))
    W.update(late_weights("first", [h, mem_n, bias, wa_bd, wx_bd]))
    proj = _matmul(h, W["w_in_t"], M=S, N=D_IN, K=D_MODEL, mode="nt", bm=512, bn=D_IN // 2, bk=D_MODEL, name="mm_in",
                   j_outer=True, deps=[W["started"]])

    group_out = [_dilated_fwd(proj, bias, g, S=S) for g in range(len(DIL_GROUPS))]
    o_dil, o_dil_m, lse_dil = _dilated_merge([o for o, _ in group_out], [l for _, l in group_out], S=S)

    W.update(late_weights("branch", [o_dil]))
    lru_args = (W["conv_w"], P["conv_b"].reshape(1, -1), wa_bd, wx_bd, P["b_rg_a"].reshape(1, -1),
                P["b_rg_x"].reshape(1, -1), P["lru_lambda"].reshape(1, -1))
    hl, z_lru, a_lru, mult_lru = _lru_fwd(proj, *lru_args, S=S)
    kv = _matmul(mem_n, W["w_mem_kv"], M=N_MEM, N=2 * MEM_WIDTH, K=D_MODEL, mode="nn", bm=N_MEM, bn=512, bk=D_MODEL,
                 name="mm_kv")
    om, om_m, lse_mem = _mem_attn_fwd(proj, kv, S=S)
    b_gate = P["b_gate"].reshape(1, -1)
    merged, *gates = _mix_fwd(z_lru, o_dil_m, om_m, W["w_lru_out"], W["w_dil_out"], W["w_mem_out"], proj, b_gate, S=S)
    g_mlp, g_final, g_mix = (P[n].reshape(1, D_MODEL) for n in ("g_mlp", "g_final", "g_mix"))
    x1, hm = _matmul_rows(merged, W["w_out"], M=S, K=D_MODEL, mode="nn", bm=512, name="mm_out",
                          row_fn=_residual_then_norm, out_dtypes=(F32, MXU_DTYPE), tiles=[x], vecs=[g_mlp])
    W.update(late_weights("mlp", [hm]))

    def relu2(acc):
        rl = jnp.maximum(acc, 0.0)
        return rl * rl, rl

    act, relu_u = _matmul(hm, W["w_mlp_in_t"], M=S, N=D_FF, K=D_MODEL, mode="nt", bm=1024, bn=1024, bk=D_MODEL,
                          name="mm_mlp_in", out_dtypes=(MXU_DTYPE, MXU_DTYPE), epilogue=relu2, j_outer=True)
    dx2, dx2_m, loss, dg_final = _matmul_rows(
        act, W["w_mlp_out"], M=S, K=D_FF, mode="nn", bm=512, name="mm_mlp_out", row_fn=_residual_then_loss,
        out_dtypes=(F32, MXU_DTYPE), tiles=[x1, tgt], vecs=[g_final], acc_widths=(1, D_MODEL))

    G, Gs = {}, {}
    Gs["g_final"] = dg_final
    dw = dict(mode="tn", K=S, bk=S, out_dtypes=(MXU_DTYPE,))
    G["w_mlp_out"] = _matmul(act, dx2_m, M=D_FF, N=D_MODEL, bm=512, bn=D_MODEL, name="mm_dw_mlp_out",
                             parts=("rows", D_FF // N_DEV), **dw)
    du = _matmul(dx2_m, W["w_mlp_out"], M=S, N=D_FF, K=D_MODEL, mode="nt", bm=1024, bn=1024, bk=D_MODEL, name="mm_du",
                 out_dtypes=(MXU_DTYPE,), epilogue=lambda acc, rl: (acc * (2.0 * rl.astype(F32)),),
                 extras=[(relu_u, (0, 0))], j_outer=True)
    G["w_mlp_in"] = _matmul(hm, du, M=D_MODEL, N=D_FF, bm=D_MODEL, bn=512, name="mm_dw_mlp_in",
                            parts=("cols", D_FF // N_DEV), **dw)
    tie1 = send_grads({n: G.pop(n) for n in ("w_mlp_out", "w_mlp_in")})
    dx1, dx1_m, Gs["g_mlp"] = _matmul_rows(
        du, W["w_mlp_in_t"], M=S, K=D_FF, mode="nn", bm=512, name="mm_dhm", row_fn=_norm_bwd_then_residual(2),
        out_dtypes=(F32, MXU_DTYPE), tiles=[x1, dx2], vecs=[g_mlp], acc_widths=(D_MODEL,), deps=[tie1])
    G["w_out"] = _matmul(merged, dx1_m, M=D_MODEL, N=D_MODEL, bm=512, bn=D_MODEL, name="mm_dw_out",
                         parts=("rows", D_MODEL // N_DEV), **dw)
    (dg0, dg1, dg2, dy_lru, dy_dil, dy_mem, db0, db1, db2) = _mix_bwd(
        dx1_m, W["w_out"], z_lru, o_dil_m, om_m, W["w_lru_out"], W["w_dil_out"], W["w_mem_out"], gates, S=S)
    Gs["b_gate0"], Gs["b_gate1"], Gs["b_gate2"] = db0, db1, db2

    G["w_mem_out"] = _matmul(om_m, dy_mem, M=MEM_WIDTH, N=D_MODEL, bm=MEM_WIDTH, bn=D_MODEL, name="mm_dw_mem_out",
                             parts=("cols", D_MODEL // N_DEV), **dw)
    dqm, dk_mem, dv_mem = _mem_attn_bwd(proj, kv, om, lse_mem, dy_mem, W["w_mem_out"], S=S)
    dkv = jnp.concatenate([dk_mem, dv_mem], axis=1)
    G["w_mem_kv"] = _matmul(mem_n, dkv, M=D_MODEL, N=2 * MEM_WIDTH, K=N_MEM, mode="tn", bm=D_MODEL, bn=2 * MEM_WIDTH,
                            bk=N_MEM, name="mm_dw_kv", out_dtypes=(MXU_DTYPE,), parts=("rows", D_MODEL // N_DEV))
    dmem_n = _matmul(dkv, W["w_mem_kv"], M=N_MEM, N=D_MODEL, K=2 * MEM_WIDTH, mode="nt", bm=N_MEM, bn=D_MODEL,
                     bk=2 * MEM_WIDTH, name="mm_dmem")
    (Gs["g_mem"],) = _rmsnorm_bwd(mem, P["g_mem"], dmem_n, None, rows=N_MEM, name="norm_mem_bwd", dx_dtypes=())

    G["w_dil_out"] = _matmul(o_dil_m, dy_dil, M=256, N=D_MODEL, bm=256, bn=D_MODEL, name="mm_dw_dil_out",
                             parts=("cols", D_MODEL // N_DEV), **dw)
    do_dil, delta = _matmul(dy_dil, W["w_dil_out"], M=S, N=256, K=D_MODEL, mode="nt", bm=512, bn=256, bk=D_MODEL,
                            name="mm_do_dil", out_dtypes=(F32, F32), epilogue=_with_delta, extras=[(o_dil, (0, 0))])
    G["w_lru_out"] = _matmul(z_lru, dy_lru, M=D_RNN, N=D_MODEL, bm=D_RNN, bn=D_MODEL, name="mm_dw_lru_out",
                             parts=("cols", D_MODEL // N_DEV), **dw)
    tie2 = send_grads({n: G.pop(n) for n in ("w_out", "w_mem_out", "w_mem_kv", "w_dil_out", "w_lru_out")})
    bias = bias + tie2[0, 0]
    dqkv, dbias = None, []
    for g in range(len(DIL_GROUPS)):
        *dqkv, db_g = _dilated_bwd(proj, do_dil, lse_dil, delta, bias, g, S=S, into=dqkv)
        dbias.append(db_g)
    drel = _dil_bias_bwd(jnp.stack(dbias, axis=0).reshape(len(DIL_GROUPS), DIL_HEADS, SPAN, 2 * SPAN), buckets)
    Gs["rel_bias"] = drel

    dxl, dgl, dcw, dcb, dwa, dwx, dba, dbx, dlam = _lru_bwd(proj, hl, a_lru, mult_lru, dy_lru, W["w_lru_out"], *lru_args,
                                                            S=S)
    Gs["conv_w"], Gs["conv_b"] = dcw, dcb
    Gs["w_rg_a"], Gs["w_rg_x"] = _block_diag_extract(dwa), _block_diag_extract(dwx)
    Gs["b_rg_a"], Gs["b_rg_x"], Gs["lru_lambda"] = dba, dbx, dlam
    Gs["loss"] = loss

    dproj = [dxl, dgl] + dqkv + [dqm, dg0, dg1, dg2]
    tie = []
    for q in range(W_IN_PIECES):
        dw_q = None
        for half in range(2):
            dw_q = _dw_in_t_half(h, dproj, q, half, S=S, name=f"mm_dw_in_{q}_{half}", into=dw_q, deps=tie)
        tie = [send_grads({f"w_in_{q}": dw_q})]
    grad_x, Gs["g_mix"] = _matmul_rows(
        dproj, W["w_in_t"], M=S, K=D_IN, mode="nn", bm=256, name="mm_dh", row_fn=_norm_bwd_then_residual(1),
        out_dtypes=(F32,), tiles=[x, dx1], vecs=[g_mix], acc_widths=(D_MODEL,), deps=tie)
    return grad_x, reduce_small(Gs)


BIG = ("w_in", "w_lru_out", "w_dil_out", "w_mem_kv", "w_mem_out", "w_out", "w_mlp_in", "w_mlp_out")
W_IN_PIECES = 2
COL_SHARDED = ("w_lru_out", "w_dil_out", "w_mem_out", "w_mlp_in")
GATHERED_TRANSPOSED = ("w_mlp_in",)
SMALL = ("g_mix", "b_gate", "conv_b", "w_rg_a", "b_rg_a", "w_rg_x", "b_rg_x", "lru_lambda", "rel_bias", "g_mem",
         "g_mlp", "g_final")
WEIGHTS = ("g_mix", "w_in", "b_gate", "conv_w", "conv_b", "w_rg_a", "b_rg_a", "w_rg_x", "b_rg_x", "lru_lambda",
           "w_lru_out", "rel_bias", "w_dil_out", "g_mem", "w_mem_kv", "w_mem_out", "w_out", "g_mlp", "w_mlp_in",
           "w_mlp_out", "g_final")


def _gathered_to_full(name, gathered):
    if name in COL_SHARDED:
        n, r, c = gathered.shape
        return gathered.transpose(1, 0, 2).reshape(r, n * c)
    n, r, c = gathered.shape
    return gathered.reshape(n * r, c)


SMALL_GRADS = (("g_mix", (1, 1024)), ("b_gate0", (1, 1024)), ("b_gate1", (1, 1024)), ("b_gate2", (1, 1024)),
               ("conv_b", (1, 768)), ("w_rg_a", (12, 64, 64)), ("b_rg_a", (1, 768)), ("w_rg_x", (12, 64, 64)),
               ("b_rg_x", (1, 768)), ("lru_lambda", (1, 768)), ("rel_bias", (32, 128)), ("g_mem", (1, 1024)),
               ("g_mlp", (1, 1024)), ("g_final", (1, 1024)), ("conv_w", (4, 768)), ("loss", (1, 1)))


def _pack(parts):
    flat = jnp.concatenate([p.reshape(-1) for p in parts])
    return jnp.pad(flat, (0, (-flat.shape[0]) % 1024)).reshape(-1, 128)


def _unpack(pack, shapes):
    flat = pack.reshape(-1)
    out, off = [], 0
    for shp in shapes:
        size = math.prod(shp)
        out.append(flat[off:off + size].reshape(shp))
        off += size
    return out


def kernel(x, mem, g_mix, w_in, b_gate, conv_w, conv_b, w_rg_a, b_rg_a, w_rg_x, b_rg_x, lru_lambda, w_lru_out, rel_bias, w_dil_out, g_mem, w_mem_kv, w_mem_out, w_out, g_mlp, w_mlp_in, w_mlp_out, g_final, loss_target, m_g_mix, m_w_in, m_b_gate, m_conv_w, m_conv_b, m_w_rg_a, m_b_rg_a, m_w_rg_x, m_b_rg_x, m_lru_lambda, m_w_lru_out, m_rel_bias, m_w_dil_out, m_g_mem, m_w_mem_kv, m_w_mem_out, m_w_out, m_g_mlp, m_w_mlp_in, m_w_mlp_out, m_g_final, v_g_mix, v_w_in, v_b_gate, v_conv_w, v_conv_b, v_w_rg_a, v_b_rg_a, v_w_rg_x, v_b_rg_x, v_lru_lambda, v_w_lru_out, v_rel_bias, v_w_dil_out, v_g_mem, v_w_mem_kv, v_w_mem_out, v_w_out, v_g_mlp, v_w_mlp_in, v_w_mlp_out, v_g_final):
    w = dict(g_mix=g_mix, w_in=w_in, b_gate=b_gate, conv_w=conv_w, conv_b=conv_b, w_rg_a=w_rg_a, b_rg_a=b_rg_a,
             w_rg_x=w_rg_x, b_rg_x=b_rg_x, lru_lambda=lru_lambda, w_lru_out=w_lru_out, rel_bias=rel_bias,
             w_dil_out=w_dil_out, g_mem=g_mem, w_mem_kv=w_mem_kv, w_mem_out=w_mem_out, w_out=w_out, g_mlp=g_mlp,
             w_mlp_in=w_mlp_in, w_mlp_out=w_mlp_out, g_final=g_final)
    m = dict(g_mix=m_g_mix, w_in=m_w_in, b_gate=m_b_gate, conv_w=m_conv_w, conv_b=m_conv_b, w_rg_a=m_w_rg_a,
             b_rg_a=m_b_rg_a, w_rg_x=m_w_rg_x, b_rg_x=m_b_rg_x, lru_lambda=m_lru_lambda, w_lru_out=m_w_lru_out,
             rel_bias=m_rel_bias, w_dil_out=m_w_dil_out, g_mem=m_g_mem, w_mem_kv=m_w_mem_kv, w_mem_out=m_w_mem_out,
             w_out=m_w_out, g_mlp=m_g_mlp, w_mlp_in=m_w_mlp_in, w_mlp_out=m_w_mlp_out, g_final=m_g_final)
    v = dict(g_mix=v_g_mix, w_in=v_w_in, b_gate=v_b_gate, conv_w=v_conv_w, conv_b=v_conv_b, w_rg_a=v_w_rg_a,
             b_rg_a=v_b_rg_a, w_rg_x=v_w_rg_x, b_rg_x=v_b_rg_x, lru_lambda=v_lru_lambda, w_lru_out=v_w_lru_out,
             rel_bias=v_rel_bias, w_dil_out=v_w_dil_out, g_mem=v_g_mem, w_mem_kv=v_w_mem_kv, w_mem_out=v_w_mem_out,
             w_out=v_w_out, g_mlp=v_g_mlp, w_mlp_in=v_w_mlp_in, w_mlp_out=v_w_mlp_out, g_final=v_g_final)

    my_idx = _dev_index(_my_pos())

    w_in_shard = _mx(w["w_in"].T)
    first = _push_start([w_in_shard], [(N_DEV,) + w_in_shard.shape], _gather_refs, "gather_in_start",
                        relations=ONE_PER_CHIP)
    cw_cols = D_RNN // N_DEV
    conv_pad = jnp.zeros((64, D_MODEL), F32).at[:CONV_WIDTH, :cw_cols].set(w["conv_w"])
    late = {}
    P = {n: w[n] for n in SMALL}

    def start_late(order_after):
        for group, names in (("branch", ("w_mem_kv", "w_lru_out", "w_dil_out", "w_mem_out", "w_out", "conv_w")),
                             ("mlp", ("w_mlp_in", "w_mlp_out"))):
            shards = [conv_pad if n == "conv_w" else _mx(w[n].T if n in GATHERED_TRANSPOSED else w[n]) for n in names]
            started = _push_start(shards, [(N_DEV,) + s.shape for s in shards], _gather_refs, f"gather_{group}_start",
                                  after=order_after)
            late[group] = (names, shards, started)
            order_after = [started["token"]]

    start_late([first["token"]])

    def late_weights(group, after):
        if group == "first":
            (land,) = _push_wait(first, after)
            forward = _forward_start(land, "forward_in_start")
            full = lax.dynamic_update_index_in_dim(_forward_wait(forward, forward["token"]), w_in_shard, my_idx, 0)
            return {"w_in_t": full.reshape(D_IN, D_MODEL), "started": late["mlp"][2]["token"]}
        names, shards, started = late[group]
        out = {}
        for n, land, own in zip(names, _push_wait(started, after), shards):
            full = lax.dynamic_update_index_in_dim(land, own, my_idx, 0)
            if n == "conv_w":
                out[n] = full[:, :CONV_WIDTH, :cw_cols].transpose(1, 0, 2).reshape(CONV_WIDTH, D_RNN)
            elif n in GATHERED_TRANSPOSED:
                out[n + "_t"] = full.reshape(-1, full.shape[2])
            else:
                out[n] = _gathered_to_full(n, full)
        return out

    sent, small = [], {}

    def send_grads(gs):
        names = list(gs)
        parts = [gs[n] for n in names]
        own = [lax.dynamic_index_in_dim(p, my_idx, 0, keepdims=False) for p in parts]
        started = _push_start(parts, [(N_DEV - 1,) + p.shape[1:] for p in parts], _scatter_refs,
                              f"scatter{len(sent)}_start")
        sent.append((names, own, started))
        return started["token"]

    def reduce_small(gs):
        small["pack"] = _pack([gs[n] for n, _ in SMALL_GRADS])
        small["started"] = _push_start([small["pack"]], [(N_DEV,) + small["pack"].shape], _gather_refs, "small_start")
        return small["started"]["token"]

    grad_x, last_token = _local_step(x[0], mem[0], loss_target[0], {}, P, late_weights, send_grads, reduce_small,
                                     late["mlp"][2]["token"][0, 0])

    grads, deltas, new_m, new_v = {}, {}, {}, {}
    after = last_token
    for names, own, started in sent[:-W_IN_PIECES]:
        for n, o, land in zip(names, own, _push_wait(started, after)):
            grads[n], deltas[n], new_m[n], new_v[n] = _adamw_landed(w[n], o, land, m[n], v[n], name=f"adamw_{n}")
            after = deltas[n]
    prev = None
    for q, (names, own, started) in enumerate(sent[-W_IN_PIECES:]):
        (land,) = _push_wait(started, after)
        prev = _adamw_landed(w["w_in"].T, own[0], land, m["w_in"].T, v["w_in"].T, name=f"adamw_{names[0]}",
                             col_blk=q, prev=prev)
        after = prev[1]
    grads["w_in"], deltas["w_in"], new_m["w_in"], new_v["w_in"] = [t.T for t in prev]
    (small_land,) = _push_wait(small["started"], [after] + [deltas[n] for n in BIG if n != "w_in"])
    total = _sum_slots(lax.dynamic_update_index_in_dim(small_land, small["pack"], my_idx, 0))
    summed = dict(zip([n for n, _ in SMALL_GRADS], _unpack(total, [shp for _, shp in SMALL_GRADS])))
    summed["b_gate"] = jnp.concatenate([summed.pop(f"b_gate{b}") for b in range(3)], axis=1)
    summed["rel_bias"] = summed["rel_bias"][:, :3 * DIL_HEADS]
    for n in SMALL:
        grads[n] = summed[n].reshape(w[n].shape)
    small_updates = _adamw_many([w[n] for n in SMALL], [grads[n] for n in SMALL], [m[n] for n in SMALL],
                                [v[n] for n in SMALL])
    for n, (d_, nm_, nv_) in zip(SMALL, small_updates):
        deltas[n], new_m[n], new_v[n] = d_, nm_, nv_
    conv_w_sum, loss_sum = summed["conv_w"], summed["loss"]
    grads["conv_w"] = lax.dynamic_slice(conv_w_sum, (0, my_idx * cw_cols), (CONV_WIDTH, cw_cols))
    deltas["conv_w"], new_m["conv_w"], new_v["conv_w"] = _adamw_plain(
        w["conv_w"], grads["conv_w"], m["conv_w"], v["conv_w"], name="adamw_conv_w")

    return (loss_sum.reshape(()), grad_x[None], *[grads[n] for n in WEIGHTS], *[deltas[n] for n in WEIGHTS],
            *[new_m[n] for n in WEIGHTS], *[new_v[n] for n in WEIGHTS])
```

```python
import functools
import math

import jax
import jax.numpy as jnp
from jax import lax
from jax.experimental import pallas as pl
from jax.experimental.pallas import tpu as pltpu

F32 = jnp.float32
MXU_DTYPE = jnp.bfloat16
VMEM_LIMIT_BYTES = 56 * 1024 * 1024
N_DEV = 8

D_MODEL = 1024
N_MEM = 256
MEM_HEADS = 4
MEM_HEAD_DIM = 128
MEM_WIDTH = 512
D_RNN = 768
LRU_BLOCK = 64
N_LRU_BLOCKS = 12
LRU_GROUP = 256
N_LRU_GROUPS = 3
CONV_WIDTH = 4
LRU_C = 8.0
DIL_GROUPS = ((128, 1), (512, 4), (2048, 16))
SPAN = 128
DIL_HEADS = 4
DIL_HEAD_DIM = 64
NUM_BUCKETS = 32
MAX_DISTANCE = 2048
D_FF = 4096
D_IN = 7424
EPS = 1e-6
NEG = -1e30
C_XL, C_GATE, C_QKV, C_QM, C_GATES = 0, 768, 1536, 3840, 4352

ADAM_LR = 0.001
ADAM_B1 = 0.9
ADAM_B2 = 0.999
ADAM_EPS = 1e-08
ADAM_WD = 0.01
ADAM_STEP = 10

MESH = pl.DeviceIdType.MESH
GELU_K = math.sqrt(2.0 / math.pi)


def _cparams(sem=None):
    kw = dict(vmem_limit_bytes=VMEM_LIMIT_BYTES)
    if sem is not None:
        kw["dimension_semantics"] = sem
    return pltpu.CompilerParams(**kw)


def _mx(v):
    return v.astype(MXU_DTYPE)


def _dot(a, b, mode="nn"):
    dims = {"nn": (((1,), (0,)), ((), ())), "nt": (((1,), (1,)), ((), ())), "tn": (((0,), (0,)), ((), ()))}[mode]
    return lax.dot_general(_mx(a), _mx(b), dims, preferred_element_type=F32)


def _colsum(v):
    return jnp.sum(v, axis=0, keepdims=True)


def _matmul(a, b, *, M, N, K, mode, bm, bn, bk, name, out_dtypes=(F32,), epilogue=None, extras=(),
            a_off=(0, 0), b_off=(0, 0), j_outer=False, deps=(), parts=None):
    assert M % bm == 0 and N % bn == 0 and K % bk == 0, (name, M, N, K, bm, bn, bk)
    nm, nn, nk = M // bm, N // bn, K // bk

    def ij(f):
        if j_outer:
            return lambda j, i, k: f(i, j, k)
        return f

    if mode == "tn":
        a_spec = pl.BlockSpec((bk, bm), ij(lambda i, j, k: (k + a_off[0], i + a_off[1])))
    else:
        a_spec = pl.BlockSpec((bm, bk), ij(lambda i, j, k: (i + a_off[0], k + a_off[1])))
    if mode == "nt":
        b_spec = pl.BlockSpec((bn, bk), ij(lambda i, j, k: (j + b_off[0], k + b_off[1])))
    else:
        b_spec = pl.BlockSpec((bk, bn), ij(lambda i, j, k: (k + b_off[0], j + b_off[1])))
    ex_specs = [pl.BlockSpec((bm, bn), ij(functools.partial(lambda i, j, k, o: (i + o[0], j + o[1]), o=off)))
                for _, off in extras]
    if parts is None:
        out_dims = (M, N)
        out_spec = pl.BlockSpec((bm, bn), ij(lambda i, j, k: (i, j)))
    elif parts[0] == "rows":
        r = parts[1]
        assert bm % r == 0
        out_dims = (M // r, r, N)
        out_spec = pl.BlockSpec((bm // r, r, bn), ij(lambda i, j, k: (i, 0, j)))
    elif parts[0] == "rows_t":
        r = parts[1]
        assert bn % r == 0
        out_dims = (N // r, r, M)
        out_spec = pl.BlockSpec((bn // r, r, bm), ij(lambda i, j, k: (j, 0, i)))
    else:
        c = parts[1]
        assert bn % c == 0
        out_dims = (N // c, M, c)
        out_spec = pl.BlockSpec((bn // c, bm, c), ij(lambda i, j, k: (j, i, 0)))
    n_ex, n_out, n_dep = len(extras), len(out_dtypes), len(deps)

    def body(*refs):
        a_ref, b_ref = refs[0], refs[1]
        ex = refs[2:2 + n_ex]
        outs = refs[2 + n_ex + n_dep:2 + n_ex + n_dep + n_out]
        part = _dot(a_ref[...], b_ref[...], mode)

        def finish(acc):
            vals = epilogue(acc, *[e[...] for e in ex]) if epilogue is not None else (acc,)
            for o, v in zip(outs, vals):
                if parts is not None and parts[0] == "rows_t":
                    v = v.T
                v = v.astype(o.dtype)
                if parts is None:
                    o[...] = v
                elif parts[0] in ("rows", "rows_t"):
                    for ch in range(v.shape[0] // parts[1]):
                        o[ch] = v[ch * parts[1]:(ch + 1) * parts[1], :]
                else:
                    for ch in range(bn // parts[1]):
                        o[ch] = v[:, ch * parts[1]:(ch + 1) * parts[1]]

        if nk == 1:
            finish(part)
        else:
            acc_ref = refs[-1]
            k = pl.program_id(2)

            @pl.when(k == 0)
            def _():
                acc_ref[...] = part

            @pl.when(k > 0)
            def _():
                acc_ref[...] += part

            @pl.when(k == nk - 1)
            def _():
                finish(acc_ref[...])

    grid = (nn, nm, nk) if j_outer else (nm, nn, nk)
    res = pl.pallas_call(
        body, name=name, grid=grid,
        in_specs=[a_spec, b_spec] + ex_specs + [pl.BlockSpec(memory_space=pl.ANY)] * n_dep,
        out_specs=[out_spec] * n_out,
        out_shape=[jax.ShapeDtypeStruct(out_dims, dt) for dt in out_dtypes],
        scratch_shapes=[pltpu.VMEM((bm, bn), F32)] if nk > 1 else [],
        compiler_params=_cparams(("parallel", "parallel", "arbitrary")),
    )(a, b, *[e for e, _ in extras], *deps)
    return res[0] if n_out == 1 else res


ROW_SUBTILES = 2


def _matmul_rows(a, b, *, M, K, mode, bm, name, row_fn, out_dtypes, tiles=(), vecs=(), acc_widths=(), deps=()):
    N = D_MODEL
    assert M % bm == 0
    segs = list(a) if isinstance(a, (list, tuple)) else [a]
    widths = [s_.shape[1] for s_ in segs]
    assert sum(widths) == K and (len(segs) == 1 or mode == "nn")
    n_s, n_t, n_v, n_o, n_a, n_d = len(segs), len(tiles), len(vecs), len(out_dtypes), len(acc_widths), len(deps)
    row = pl.BlockSpec((bm, N), lambda i: (i, 0))
    b_shape = (K, N) if mode == "nn" else (N, K)

    def body(*refs):
        b_ref = refs[n_s]
        ins = refs[n_s + 1:n_s + 1 + n_t + n_v]
        outs = refs[n_s + 1 + n_t + n_v + n_d:n_s + 1 + n_t + n_v + n_d + n_o]
        accs = refs[n_s + 1 + n_t + n_v + n_d + n_o:]
        for o in accs:
            @pl.when(pl.program_id(0) == 0)
            def _(o=o):
                o[...] = jnp.zeros_like(o)

        for s_ in range(ROW_SUBTILES):
            rows = pl.ds(s_ * (bm // ROW_SUBTILES), bm // ROW_SUBTILES)
            if n_s == 1:
                acc = _dot(refs[0][rows, :], b_ref[...], mode)
            else:
                acc, k0 = None, 0
                for a_ref, w_ in zip(refs[:n_s], widths):
                    part = _dot(a_ref[rows, :], b_ref[k0:k0 + w_, :])
                    acc = part if acc is None else acc + part
                    k0 += w_
            tile_vals, partials = row_fn(acc, *[r[rows, :] for r in ins[:n_t]], *[r[...] for r in ins[n_t:]])
            for o, val in zip(outs, tile_vals):
                o[rows, :] = val.astype(o.dtype)
            for o, val in zip(accs, partials):
                o[...] += val

    res = pl.pallas_call(
        body, name=name, grid=(M // bm,),
        in_specs=[pl.BlockSpec((bm, w_), lambda i: (i, 0)) for w_ in widths] + [pl.BlockSpec(b_shape, lambda i: (0, 0))]
        + [row] * n_t + [pl.BlockSpec((1, N), lambda i: (0, 0))] * n_v + [pl.BlockSpec(memory_space=pl.ANY)] * n_d,
        out_specs=[row] * n_o + [pl.BlockSpec((1, w_), lambda i: (0, 0)) for w_ in acc_widths],
        out_shape=[jax.ShapeDtypeStruct((M, N), dt) for dt in out_dtypes]
        + [jax.ShapeDtypeStruct((1, w_), F32) for w_ in acc_widths],
        compiler_params=_cparams(("arbitrary",) if n_a else ("parallel",)),
    )(*segs, b, *tiles, *vecs, *deps)
    return res


def _dw_in_t_half(h, pieces, q, half, *, S, name, into=None, deps=(), bk=1024):
    half_w, cols = D_IN // 2, D_MODEL // W_IN_PIECES
    lo, hi = half * half_w, (half + 1) * half_w
    use, c0 = [], 0
    for p in pieces:
        w_ = p.shape[1]
        a0, a1 = max(lo, c0), min(hi, c0 + w_)
        if a1 > a0:
            use.append((p, a0 - c0, a1 - a0))
        c0 += w_
    n_p, n_into, n_d, nk = len(use), 0 if into is None else 1, len(deps), S // bk
    rows = D_IN // N_DEV

    def body(*refs):
        h_ref, p_refs = refs[0], refs[1:1 + n_p]
        o_ref, acc_ref = refs[1 + n_p + n_into + n_d], refs[-1]
        k = pl.program_id(0)
        dp = jnp.concatenate([r[:, s0:s0 + w_] for r, (_, s0, w_) in zip(p_refs, use)], axis=1)
        part = _dot(h_ref[...], dp, "tn")

        @pl.when(k == 0)
        def _():
            acc_ref[...] = part

        @pl.when(k > 0)
        def _():
            acc_ref[...] += part

        @pl.when(k == nk - 1)
        def _():
            vt = acc_ref[...].T.astype(o_ref.dtype)
            for ch in range(half_w // rows):
                o_ref[ch] = vt[ch * rows:(ch + 1) * rows, :]

    return pl.pallas_call(
        body, name=name, grid=(nk,),
        in_specs=[pl.BlockSpec((bk, cols), lambda k: (k, q))]
        + [pl.BlockSpec((bk, p.shape[1]), lambda k: (k, 0)) for p, _, _ in use]
        + [pl.BlockSpec(memory_space=pl.ANY)] * (n_into + n_d),
        out_specs=pl.BlockSpec((half_w // rows, rows, cols), lambda k: (half, 0, 0)),
        out_shape=jax.ShapeDtypeStruct((N_DEV, rows, cols), MXU_DTYPE),
        input_output_aliases={1 + n_p: 0} if n_into else {},
        scratch_shapes=[pltpu.VMEM((cols, half_w), F32)],
        compiler_params=_cparams(("arbitrary",)),
    )(h, *[p for p, _, _ in use], *([into] if n_into else []), *deps)


def _rmsnorm_fwd(x, g, *, rows, name, bt=512):
    bt = min(bt, rows)

    def body(x_ref, g_ref, o_ref):
        xv = x_ref[...]
        r = lax.rsqrt(jnp.mean(xv * xv, axis=-1, keepdims=True) + EPS)
        o_ref[...] = (xv * r * g_ref[...]).astype(o_ref.dtype)

    return pl.pallas_call(
        body, name=name, grid=(rows // bt,),
        in_specs=[pl.BlockSpec((bt, D_MODEL), lambda i: (i, 0)), pl.BlockSpec((1, D_MODEL), lambda i: (0, 0))],
        out_specs=pl.BlockSpec((bt, D_MODEL), lambda i: (i, 0)),
        out_shape=jax.ShapeDtypeStruct((rows, D_MODEL), MXU_DTYPE),
        compiler_params=_cparams(("parallel",)),
    )(x, g.reshape(1, D_MODEL))


def _rms_bwd_tile(xv, gv, dyv):
    r = lax.rsqrt(jnp.mean(xv * xv, axis=-1, keepdims=True) + EPS)
    w = dyv * gv
    dx = r * w - xv * (r * r * r) * jnp.mean(w * xv, axis=-1, keepdims=True)
    dg = _colsum(dyv * xv * r)
    return dx, dg


def _residual_then_norm(acc, x_t, g):
    x1 = x_t + acc
    r = lax.rsqrt(jnp.mean(x1 * x1, axis=-1, keepdims=True) + EPS)
    return (x1, x1 * r * g), ()


def _residual_then_loss(acc, x_t, tgt_t, g):
    x2 = x_t + acc
    r = lax.rsqrt(jnp.mean(x2 * x2, axis=-1, keepdims=True) + EPS)
    diff = x2 * r * g - tgt_t
    part = jnp.sum(jnp.mean(diff * diff, axis=-1, keepdims=True), axis=0, keepdims=True) * 0.5
    dx, dg = _rms_bwd_tile(x2, g, diff * (1.0 / D_MODEL))
    return (dx, dx), (part, dg)


def _norm_bwd_then_residual(n_out):
    def fn(acc, x_t, res_t, g):
        dx, dg = _rms_bwd_tile(x_t, g, acc)
        return (dx + res_t,) * n_out, (dg,)

    return fn


def _rmsnorm_bwd(x, g, dy, res, *, rows, name, bt=512, dx_dtypes=(F32,)):
    bt = min(bt, rows)
    has_res = res is not None

    def body(*refs):
        x_ref, g_ref, dy_ref = refs[:3]
        res_ref = refs[3] if has_res else None
        outs = refs[3 + int(has_res):]
        dx, dg = _rms_bwd_tile(x_ref[...], g_ref[...], dy_ref[...])
        if has_res:
            dx = dx + res_ref[...]
        dg_ref = outs[-1]

        @pl.when(pl.program_id(0) == 0)
        def _():
            dg_ref[...] = jnp.zeros_like(dg_ref)

        dg_ref[...] += dg
        for o in outs[:-1]:
            o[...] = dx.astype(o.dtype)

    row_spec = pl.BlockSpec((bt, D_MODEL), lambda i: (i, 0))
    vec_spec = pl.BlockSpec((1, D_MODEL), lambda i: (0, 0))
    ins = [x, g.reshape(1, D_MODEL), dy] + ([res] if has_res else [])
    return pl.pallas_call(
        body, name=name, grid=(rows // bt,),
        in_specs=[row_spec, vec_spec, row_spec] + ([row_spec] if has_res else []),
        out_specs=[row_spec] * len(dx_dtypes) + [vec_spec],
        out_shape=[jax.ShapeDtypeStruct((rows, D_MODEL), dt) for dt in dx_dtypes] + [jax.ShapeDtypeStruct((1, D_MODEL), F32)],
        compiler_params=_cparams(("arbitrary",)),
    )(*ins)


LRU_T = 512
SCAN_GROUPS = 4


def _gelu(x):
    t = jnp.tanh(GELU_K * (x + 0.044715 * x * x * x))
    return 0.5 * x * (1.0 + t), t


def _gelu_grad(x, t):
    return 0.5 * (1.0 + t) + 0.5 * x * (1.0 - t * t) * GELU_K * (1.0 + 3.0 * 0.044715 * x * x)


def _softplus_neg(lam):
    z = -lam
    u = jnp.exp(-jnp.abs(z))
    w = 1.0 + u
    l1p = jnp.where(w == 1.0, u, jnp.log(w) * u / jnp.where(w == 1.0, 1.0, w - 1.0))
    return jnp.maximum(z, 0.0) + l1p


def _shift_down(cur, prev8, k, row8):
    y = pltpu.roll(cur, k, 0)
    head = jnp.where(row8 < k, pltpu.roll(prev8, k, 0), y[0:8])
    return jnp.concatenate([head, y[8:]], axis=0)


def _shift_up(cur, next8, k, row8):
    n = cur.shape[0]
    y = pltpu.roll(cur, n - k, 0)
    tail = jnp.where(row8 >= 8 - k, pltpu.roll(next8, 8 - k, 0), y[n - 8:n])
    return jnp.concatenate([y[0:n - 8], tail], axis=0)


def _lru_gates(xl, p8, cw, cb, wa, wx, ba, bx, lam, row8, a_mult=None):
    sh = [xl] + [_shift_down(xl, p8, k, row8) for k in (1, 2, 3)]
    xc = cb + cw[3:4] * sh[0] + cw[2:3] * sh[1] + cw[1:2] * sh[2] + cw[0:1] * sh[3]
    r = jax.nn.sigmoid(_dot(xc, wa) + ba)
    i = jax.nn.sigmoid(_dot(xc, wx) + bx)
    sp = _softplus_neg(lam)
    if a_mult is None:
        la = -LRU_C * r * sp
        a = jnp.exp(la)
        mult = jnp.sqrt(jnp.tanh(-la) * (a * a + 1.0))
    else:
        a, mult = a_mult
    return dict(sh=sh, xc=xc, r=r, i=i, sp=sp, a=a, mult=mult)


def _lru_specs(n_t, reverse):
    T = LRU_T
    tt = (lambda t: n_t - 1 - t) if reverse else (lambda t: t)
    blk = lambda col0: pl.BlockSpec((T, LRU_GROUP), lambda g, t: (tt(t), col0 + g))
    prev8 = lambda col0: pl.BlockSpec((8, LRU_GROUP), lambda g, t: (jnp.maximum(tt(t) * (T // 8) - 1, 0), col0 + g))
    vec = lambda rows: pl.BlockSpec((rows, LRU_GROUP), lambda g, t: (0, g))
    wbd = pl.BlockSpec((1, LRU_GROUP, LRU_GROUP), lambda g, t: (g, 0, 0))
    return blk, prev8, vec, wbd


def _lru_fwd(proj, conv_w, conv_b, wa_bd, wx_bd, b_a, b_x, lam, *, S):
    T = LRU_T
    n_t = S // T
    blk, _, vec, wbd = _lru_specs(n_t, False)

    def body(xl_ref, gate_ref, cw_ref, cb_ref, wa_ref, wx_ref, ba_ref, bx_ref, lam_ref,
             hl_ref, z_ref, a_s, m_ref, prev8, hcar, b_s):
        @pl.when(pl.program_id(1) == 0)
        def _():
            prev8[...] = jnp.zeros_like(prev8)
            hcar[...] = jnp.zeros_like(hcar)

        row8 = lax.broadcasted_iota(jnp.int32, (8, LRU_GROUP), 0)
        xl = xl_ref[...]
        q = _lru_gates(xl, prev8[...], cw_ref[...], cb_ref[...], wa_ref[0], wx_ref[0], ba_ref[...], bx_ref[...],
                       lam_ref[...], row8)
        prev8[...] = xl[T - 8:T]
        a_s[...] = q["a"]
        m_ref[...] = q["mult"]
        b_s[...] = q["mult"] * q["i"] * q["xc"]

        def step(c, carry):
            local = []
            for u in range(SCAN_GROUPS):
                off = pl.multiple_of((c * SCAN_GROUPS + u) * 8, 8)
                A = a_s[pl.ds(off, 8), :]
                B = b_s[pl.ds(off, 8), :]
                for k in (1, 2, 4):
                    a_sh = jnp.where(row8 >= k, pltpu.roll(A, k, 0), 1.0)
                    b_sh = jnp.where(row8 >= k, pltpu.roll(B, k, 0), 0.0)
                    B = A * b_sh + B
                    A = A * a_sh
                local.append((off, A, B))
            for off, A, B in local:
                h = A * carry + B
                hl_ref[pl.ds(off, 8), :] = h
                carry = h[7:8, :]
            return carry

        hcar[...] = lax.fori_loop(0, T // (8 * SCAN_GROUPS), step, hcar[...])
        ge, _ = _gelu(gate_ref[...])
        z_ref[...] = (ge * hl_ref[...]).astype(z_ref.dtype)

    return pl.pallas_call(
        body, name="lru_fwd", grid=(N_LRU_GROUPS, n_t),
        in_specs=[blk(C_XL // LRU_GROUP), blk(C_GATE // LRU_GROUP), vec(4), vec(1), wbd, wbd, vec(1), vec(1), vec(1)],
        out_specs=[blk(0)] * 4,
        out_shape=[jax.ShapeDtypeStruct((S, D_RNN), F32), jax.ShapeDtypeStruct((S, D_RNN), MXU_DTYPE),
                   jax.ShapeDtypeStruct((S, D_RNN), F32), jax.ShapeDtypeStruct((S, D_RNN), F32)],
        scratch_shapes=[pltpu.VMEM((8, LRU_GROUP), F32), pltpu.VMEM((1, LRU_GROUP), F32), pltpu.VMEM((T, LRU_GROUP), F32)],
        compiler_params=_cparams(("parallel", "arbitrary")),
    )(proj, proj, conv_w, conv_b, wa_bd, wx_bd, b_a, b_x, lam)


def _lru_bwd(proj, hl, a_fwd, mult_fwd, dy, w_out, conv_w, conv_b, wa_bd, wx_bd, b_a, b_x, lam, *, S):
    T = LRU_T
    n_t = S // T
    blk, prev8s, vec, wbd = _lru_specs(n_t, True)

    def body(xl_ref, xlp_ref, gate_ref, hl_ref, hlp_ref, a_ref, m_ref, dy_ref, wo_ref, cw_ref, cb_ref, wa_ref, wx_ref,
             ba_ref, bx_ref, lam_ref, dxl_ref, dgate_ref, dcw_ref, dcb_ref, dwa_ref, dwx_ref, dba_ref, dbx_ref, dlam_ref,
             next8, gcar, c_s, b_s, l_s):
        t = pl.program_id(1)
        first_chunk = t == n_t - 1

        @pl.when(t == 0)
        def _():
            next8[...] = jnp.zeros_like(next8)
            gcar[...] = jnp.zeros_like(gcar)
            for ref in (dcw_ref, dcb_ref, dwa_ref, dwx_ref, dba_ref, dbx_ref, dlam_ref):
                ref[...] = jnp.zeros_like(ref)

        row8 = lax.broadcasted_iota(jnp.int32, (8, LRU_GROUP), 0)
        rowT = lax.broadcasted_iota(jnp.int32, (T, LRU_GROUP), 0)
        keep = jnp.where(first_chunk, 0.0, 1.0)
        xl = xl_ref[...]
        wa, wx, lam_v = wa_ref[0], wx_ref[0], lam_ref[...]
        q = _lru_gates(xl, xlp_ref[...] * keep, cw_ref[...], cb_ref[...], wa, wx, ba_ref[...], bx_ref[...], lam_v, row8,
                       a_mult=(a_ref[...], m_ref[...]))
        a, mult, r, i, xc, sp = q["a"], q["mult"], q["r"], q["i"], q["xc"], q["sp"]
        hl_v = hl_ref[...]
        dz_v = _dot(dy_ref[...], wo_ref[...], "nt")
        gate = gate_ref[...]
        ge, th = _gelu(gate)
        dgate_ref[...] = (dz_v * hl_v * _gelu_grad(gate, th)).astype(dgate_ref.dtype)

        c_s[...] = jnp.where(rowT == T - 1, 0.0, pltpu.roll(a, T - 1, 0))
        b_s[...] = dz_v * ge + jnp.where(rowT == T - 1, gcar[...], 0.0)

        def step(n, carry):
            local = []
            for u in range(SCAN_GROUPS):
                off = pl.multiple_of((T // 8 - 1 - (n * SCAN_GROUPS + u)) * 8, 8)
                C = c_s[pl.ds(off, 8), :]
                B = b_s[pl.ds(off, 8), :]
                for k in (1, 2, 4):
                    c_sh = jnp.where(row8 < 8 - k, pltpu.roll(C, 8 - k, 0), 1.0)
                    b_sh = jnp.where(row8 < 8 - k, pltpu.roll(B, 8 - k, 0), 0.0)
                    B = B + C * b_sh
                    C = C * c_sh
                local.append((off, C, B))
            for off, C, B in local:
                lam_t = B + C * carry
                l_s[pl.ds(off, 8), :] = lam_t
                carry = lam_t[0:1, :]
            return carry

        lax.fori_loop(0, T // (8 * SCAN_GROUPS), step, jnp.zeros((1, LRU_GROUP), F32))
        lmb = l_s[...]
        gcar[...] = a[0:1, :] * lmb[0:1, :]

        h_prev = _shift_down(hl_v, hlp_ref[...] * keep, 1, row8)
        da = lmb * h_prev
        dmult = lmb * i * xc
        di = lmb * mult * xc
        dxc = lmb * mult * i
        dla = da * a - dmult * (a * a) / mult
        dr = dla * (-LRU_C * sp)
        dlam_ref[...] += _colsum(dla * (-LRU_C * r)) * (-jax.nn.sigmoid(-lam_v))
        dpa = dr * r * (1.0 - r)
        dpx = di * i * (1.0 - i)
        dxc = dxc + _dot(dpa, wa, "nt") + _dot(dpx, wx, "nt")
        dwa_ref[0] += _dot(xc, dpa, "tn")
        dwx_ref[0] += _dot(xc, dpx, "tn")
        dba_ref[...] += _colsum(dpa)
        dbx_ref[...] += _colsum(dpx)
        dcb_ref[...] += _colsum(dxc)
        cw = cw_ref[...]
        n8 = next8[...]
        dxl = cw[3:4] * dxc
        for k in (1, 2, 3):
            dxl = dxl + cw[3 - k:4 - k] * _shift_up(dxc, n8, k, row8)
        for k in range(4):
            dcw_ref[3 - k:4 - k, :] += _colsum(dxc * q["sh"][k])
        next8[...] = dxc[0:8]
        dxl_ref[...] = dxl.astype(dxl_ref.dtype)

    res = pl.pallas_call(
        body, name="lru_bwd", grid=(N_LRU_GROUPS, n_t),
        in_specs=[blk(C_XL // LRU_GROUP), prev8s(C_XL // LRU_GROUP), blk(C_GATE // LRU_GROUP), blk(0), prev8s(0), blk(0),
                  blk(0), pl.BlockSpec((T, D_MODEL), lambda g, t: (n_t - 1 - t, 0)),
                  pl.BlockSpec((LRU_GROUP, D_MODEL), lambda g, t: (g, 0)), vec(4), vec(1), wbd, wbd, vec(1), vec(1), vec(1)],
        out_specs=[blk(0), blk(0), vec(4), vec(1), wbd, wbd, vec(1), vec(1), vec(1)],
        out_shape=[jax.ShapeDtypeStruct((S, D_RNN), MXU_DTYPE), jax.ShapeDtypeStruct((S, D_RNN), MXU_DTYPE),
                   jax.ShapeDtypeStruct((4, D_RNN), F32), jax.ShapeDtypeStruct((1, D_RNN), F32),
                   jax.ShapeDtypeStruct((N_LRU_GROUPS, LRU_GROUP, LRU_GROUP), F32),
                   jax.ShapeDtypeStruct((N_LRU_GROUPS, LRU_GROUP, LRU_GROUP), F32),
                   jax.ShapeDtypeStruct((1, D_RNN), F32), jax.ShapeDtypeStruct((1, D_RNN), F32),
                   jax.ShapeDtypeStruct((1, D_RNN), F32)],
        scratch_shapes=[pltpu.VMEM((8, LRU_GROUP), F32), pltpu.VMEM((1, LRU_GROUP), F32),
                        pltpu.VMEM((T, LRU_GROUP), F32), pltpu.VMEM((T, LRU_GROUP), F32), pltpu.VMEM((T, LRU_GROUP), F32)],
        compiler_params=_cparams(("parallel", "arbitrary")),
    )(proj, proj, proj, hl, hl, a_fwd, mult_fwd, dy, w_out, conv_w, conv_b, wa_bd, wx_bd, b_a, b_x, lam)
    return res


def _block_diag(w):
    w4 = w.reshape(N_LRU_GROUPS, 4, LRU_BLOCK, 1, LRU_BLOCK)
    eye = jnp.eye(4, dtype=w.dtype).reshape(1, 4, 1, 4, 1)
    return (w4 * eye).reshape(N_LRU_GROUPS, LRU_GROUP, LRU_GROUP)


def _block_diag_extract(wbd):
    w5 = wbd.reshape(N_LRU_GROUPS, 4, LRU_BLOCK, 4, LRU_BLOCK)
    return jnp.stack([w5[:, a, :, a, :] for a in range(4)], axis=1).reshape(N_LRU_BLOCKS, LRU_BLOCK, LRU_BLOCK)


def _t5_bucket(dist):
    max_exact = NUM_BUCKETS // 2
    df = jnp.maximum(dist, 1).astype(jnp.float32)
    large = max_exact + (jnp.log(df / max_exact) / math.log(MAX_DISTANCE / max_exact)
                         * (NUM_BUCKETS - max_exact)).astype(jnp.int32)
    large = jnp.minimum(large, NUM_BUCKETS - 1)
    return jnp.where(dist < max_exact, dist, large)


def _band_offsets():
    qi = jnp.arange(SPAN)[:, None]
    kj = jnp.arange(2 * SPAN)[None, :]
    return qi + SPAN - kj


def _dil_buckets():
    off = _band_offsets()
    return jnp.stack([_t5_bucket(jnp.maximum(off, 0) * dil) for _, dil in DIL_GROUPS]).astype(jnp.int32)


def _dil_bias(rel_bias, buckets):
    def body(tbl_ref, bk_ref, o_ref):
        g = pl.program_id(0)
        qi = lax.broadcasted_iota(jnp.int32, (SPAN, 2 * SPAN), 0)
        kj = lax.broadcasted_iota(jnp.int32, (SPAN, 2 * SPAN), 1)
        off = qi + SPAN - kj
        valid = (off >= 0) & (off <= SPAN)
        bk = bk_ref[0]
        for h in range(DIL_HEADS):
            acc = jnp.zeros((SPAN, 2 * SPAN), F32)
            for b in range(NUM_BUCKETS):
                acc = jnp.where(bk == b, tbl_ref[b, g * DIL_HEADS + h], acc)
            o_ref[0, h] = jnp.where(valid, acc, NEG)

    return pl.pallas_call(
        body, name="dil_bias", grid=(3,),
        in_specs=[pl.BlockSpec(memory_space=pltpu.SMEM), pl.BlockSpec((1, SPAN, 2 * SPAN), lambda g: (g, 0, 0))],
        out_specs=pl.BlockSpec((1, DIL_HEADS, SPAN, 2 * SPAN), lambda g: (g, 0, 0, 0)),
        out_shape=jax.ShapeDtypeStruct((3, DIL_HEADS, SPAN, 2 * SPAN), F32),
        compiler_params=_cparams(("parallel",)),
    )(rel_bias, buckets)


def _dil_bias_bwd(dbias, buckets):
    def body(db_ref, bk_ref, o_ref):
        lane = lax.broadcasted_iota(jnp.int32, (1, 128), 1)
        rows = [jnp.zeros((1, 128), F32) for _ in range(NUM_BUCKETS)]
        for g in range(3):
            bk = bk_ref[g]
            for h in range(DIL_HEADS):
                d = db_ref[g, h]
                for b in range(NUM_BUCKETS):
                    tot = jnp.sum(_colsum(jnp.where(bk == b, d, 0.0)), axis=1, keepdims=True)
                    rows[b] = jnp.where(lane == g * DIL_HEADS + h, tot, rows[b])
        for b in range(NUM_BUCKETS):
            o_ref[b:b + 1, :] = rows[b]

    return pl.pallas_call(
        body, name="dil_bias_bwd",
        out_shape=jax.ShapeDtypeStruct((NUM_BUCKETS, 128), F32),
        compiler_params=_cparams(),
    )(dbias, buckets)


DIL_SUBBLOCKS = (8, 4, 1)


def _dil_layout(g, S):
    dil, m = DIL_GROUPS[g][1], DIL_SUBBLOCKS[g]
    sub = SPAN * dil
    col = [(C_QKV + t * 768 + g * 256) // 128 for t in range(3)]
    return dil, m, sub, S // (sub * m), col


def _residue_rows(b, r, dil):
    return pl.ds(b * SPAN * dil + r, SPAN, stride=dil) if dil > 1 else pl.ds(b * SPAN, SPAN)


def _for_residues(dil, fn):
    if dil <= 4:
        for r in range(dil):
            fn(r)
    else:
        lax.fori_loop(0, dil, lambda r, c: (fn(r), c)[1], 0, unroll=4)


def _pair_scores(qm, k2, bias, first_cols):
    s = _dot(qm, k2, "nt") * (DIL_HEAD_DIM ** -0.5) + bias
    kj = lax.broadcasted_iota(jnp.int32, s.shape, 1)
    return jnp.where(kj < first_cols, NEG, s)


def _dilated_fwd(proj, bias, g, *, S):
    dil, m, sub, nc, (qc, kc, vc) = _dil_layout(g, S)
    R = sub * m
    cur = lambda cb: pl.BlockSpec((R, 128), lambda p, i: (i, cb + p))
    prv = lambda cb: pl.BlockSpec((sub, 128), lambda p, i: (jnp.maximum(i * m - 1, 0), cb + p))
    out = pl.BlockSpec((R, 128), lambda p, i: (i, p))

    def body(q_ref, kp_ref, kc_ref, vp_ref, vc_ref, b_ref, o_ref, lse_ref):
        lane = lax.broadcasted_iota(jnp.int32, (SPAN, 128), 1)
        sels = (lane < DIL_HEAD_DIM, lane >= DIL_HEAD_DIM)
        for b in range(m):
            first_cols = jnp.where(pl.program_id(1) == 0, SPAN, 0) if b == 0 else 0

            def one(r, b=b, first_cols=first_cols):
                rows = _residue_rows(b, r, dil)
                before = (kc_ref, vc_ref, _residue_rows(b - 1, r, dil)) if b else (kp_ref, vp_ref, _residue_rows(0, r, dil))
                q2 = q_ref[rows, :]
                k2 = _mx(jnp.concatenate([before[0][before[2], :], kc_ref[rows, :]], axis=0))
                v2 = _mx(jnp.concatenate([before[1][before[2], :], vc_ref[rows, :]], axis=0))
                qq = jnp.concatenate([jnp.where(sels[0], q2, 0.0), jnp.where(sels[1], q2, 0.0)], axis=0)
                s = _pair_scores(qq, k2, b_ref[0, 0], first_cols)
                mx = jnp.max(s, axis=-1, keepdims=True)
                p = jnp.exp(s - mx)
                den = jnp.sum(p, axis=-1, keepdims=True)
                o = _dot(p, v2) / den
                st = mx + jnp.log(den)
                o_ref[rows, :] = jnp.where(sels[0], o[0:SPAN], o[SPAN:2 * SPAN])
                lse_ref[rows, :] = jnp.where(sels[0], st[0:SPAN], st[SPAN:2 * SPAN])

            _for_residues(dil, one)

    return pl.pallas_call(
        body, name=f"dil_fwd{g}", grid=(2, nc),
        in_specs=[cur(qc), prv(kc), cur(kc), prv(vc), cur(vc),
                  pl.BlockSpec((1, 1, 2 * SPAN, 2 * SPAN), lambda p, i: (g, p, 0, 0))],
        out_specs=[out, out],
        out_shape=[jax.ShapeDtypeStruct((S, 256), F32), jax.ShapeDtypeStruct((S, 256), F32)],
        compiler_params=_cparams(("parallel", "parallel")),
    )(proj, proj, proj, proj, proj, bias.reshape(3, 2, 2 * SPAN, 2 * SPAN))


def _dilated_bwd(proj, do, lse, delta, bias, g, *, S, into=None):
    dil, m, sub, nc, (qc, kc, vc) = _dil_layout(g, S)
    R = sub * m
    cl = lambda i: jnp.minimum(i, nc - 1)
    cur = lambda cb: pl.BlockSpec((R, 128), lambda p, i: (cl(i), cb + p))
    prv = lambda cb: pl.BlockSpec((sub, 128), lambda p, i: (jnp.maximum(cl(i) * m - 1, 0), cb + p))
    q_out = pl.BlockSpec((R, 128), lambda p, i: (cl(i), 2 * g + p))
    kv_out = pl.BlockSpec((R, 128), lambda p, i: (jnp.maximum(i - 1, 0), 2 * g + p))
    scale = DIL_HEAD_DIM ** -0.5
    n_into = 0 if into is None else 3

    def body(q_ref, kp_ref, kc_ref, vp_ref, vc_ref, do_ref, lse_ref, dl_ref, b_ref, *rest):
        dq_ref, dk_ref, dv_ref, db_ref, dq_s, kc_s, vc_s, kp_s, vp_s, kcar, vcar = rest[n_into:]
        i = pl.program_id(1)

        @pl.when(i == 0)
        def _():
            kcar[...] = jnp.zeros_like(kcar)
            vcar[...] = jnp.zeros_like(vcar)
            db_ref[...] = jnp.zeros_like(db_ref)

        @pl.when(i < nc)
        def _():
            lane = lax.broadcasted_iota(jnp.int32, (SPAN, 128), 1)
            sels = (lane < DIL_HEAD_DIM, lane >= DIL_HEAD_DIM)
            for b in range(m):
                first_cols = jnp.where(i == 0, SPAN, 0) if b == 0 else 0

                def one(r, b=b, first_cols=first_cols):
                    rows = _residue_rows(b, r, dil)
                    rows_before = _residue_rows(b - 1 if b else 0, r, dil)
                    k_before, v_before = (kc_ref, vc_ref) if b else (kp_ref, vp_ref)
                    q2, do2 = q_ref[rows, :], do_ref[rows, :]
                    k2 = _mx(jnp.concatenate([k_before[rows_before, :], kc_ref[rows, :]], axis=0))
                    v2 = _mx(jnp.concatenate([v_before[rows_before, :], vc_ref[rows, :]], axis=0))
                    lse_t, dl_t = lse_ref[rows, :], dl_ref[rows, :]
                    qq = _mx(jnp.concatenate([jnp.where(sels[0], q2, 0.0), jnp.where(sels[1], q2, 0.0)], axis=0))
                    dd = _mx(jnp.concatenate([jnp.where(sels[0], do2, 0.0), jnp.where(sels[1], do2, 0.0)], axis=0))
                    lse2 = jnp.concatenate([lse_t[:, 0:1], lse_t[:, DIL_HEAD_DIM:DIL_HEAD_DIM + 1]], axis=0)
                    dl2 = jnp.concatenate([dl_t[:, 0:1], dl_t[:, 1:2]], axis=0)
                    p = jnp.exp(_pair_scores(qq, k2, b_ref[0, 0], first_cols) - lse2)
                    ds = p * (_dot(dd, v2, "nt") - dl2)
                    db_ref[0] += ds
                    dqq = _dot(ds, k2) * scale
                    dq2 = jnp.where(sels[0], dqq[0:SPAN], dqq[SPAN:2 * SPAN])
                    dk2 = _dot(ds, qq, "tn") * scale
                    dv2 = _dot(p, dd, "tn")
                    dq_s[rows, :] = dq2
                    kc_s[rows, :] = dk2[SPAN:2 * SPAN]
                    vc_s[rows, :] = dv2[SPAN:2 * SPAN]
                    if b:
                        kc_s[rows_before, :] += dk2[0:SPAN]
                        vc_s[rows_before, :] += dv2[0:SPAN]
                    else:
                        kp_s[rows_before, :] = dk2[0:SPAN]
                        vp_s[rows_before, :] = dv2[0:SPAN]

                _for_residues(dil, one)
            dq_ref[...] = dq_s[...].astype(dq_ref.dtype)
            last = pl.ds((m - 1) * sub, sub)
            kcar[last, :] += kp_s[...]
            vcar[last, :] += vp_s[...]
            dk_ref[...] = kcar[...].astype(dk_ref.dtype)
            dv_ref[...] = vcar[...].astype(dv_ref.dtype)
            kcar[...] = kc_s[...]
            vcar[...] = vc_s[...]

        @pl.when(i == nc)
        def _():
            dk_ref[...] = kcar[...].astype(dk_ref.dtype)
            dv_ref[...] = vcar[...].astype(dv_ref.dtype)

    stat = pl.BlockSpec((R, 128), lambda p, i: (cl(i), p))
    big = jax.ShapeDtypeStruct((S, len(DIL_GROUPS) * 256), MXU_DTYPE)
    return pl.pallas_call(
        body, name=f"dil_bwd{g}", grid=(2, nc + 1),
        in_specs=[cur(qc), prv(kc), cur(kc), prv(vc), cur(vc), stat, stat, stat,
                  pl.BlockSpec((1, 1, 2 * SPAN, 2 * SPAN), lambda p, i: (g, p, 0, 0))]
        + [pl.BlockSpec(memory_space=pl.ANY)] * n_into,
        out_specs=[q_out, kv_out, kv_out, pl.BlockSpec((1, 2 * SPAN, 2 * SPAN), lambda p, i: (p, 0, 0))],
        out_shape=[big, big, big, jax.ShapeDtypeStruct((2, 2 * SPAN, 2 * SPAN), F32)],
        input_output_aliases={9 + j: j for j in range(n_into)},
        scratch_shapes=[pltpu.VMEM((R, 128), F32)] * 3 + [pltpu.VMEM((sub, 128), F32)] * 2 + [pltpu.VMEM((R, 128), F32)] * 2,
        compiler_params=_cparams(("parallel", "arbitrary")),
    )(proj, proj, proj, proj, proj, do, lse, delta, bias.reshape(3, 2, 2 * SPAN, 2 * SPAN), *(into or ()))


def _dilated_merge(os_, lses, *, S, bt=1024):
    tile = pl.BlockSpec((bt, 128), lambda i, p: (i, p))

    def body(o0, o1, o2, l0, l1, l2, o_ref, om_ref, lse_ref):
        a = [l0[...], l1[...], l2[...]]
        m = jnp.maximum(jnp.maximum(a[0], a[1]), a[2])
        ex = [jnp.exp(v - m) for v in a]
        tot = ex[0] + ex[1] + ex[2]
        acc = (ex[0] * o0[...] + ex[1] * o1[...] + ex[2] * o2[...]) / tot
        o_ref[...] = acc
        om_ref[...] = _mx(acc)
        lse_ref[...] = m + jnp.log(tot)

    return pl.pallas_call(
        body, name="dil_merge", grid=(S // bt, 2),
        in_specs=[tile] * 6, out_specs=[tile, tile, tile],
        out_shape=[jax.ShapeDtypeStruct((S, 256), F32), jax.ShapeDtypeStruct((S, 256), MXU_DTYPE),
                   jax.ShapeDtypeStruct((S, 256), F32)],
        compiler_params=_cparams(("parallel", "parallel")),
    )(*os_, *lses)


def _with_delta(do, o):
    lane = lax.broadcasted_iota(jnp.int32, (do.shape[0], 128), 1)
    stats = []
    for p in range(2):
        prod = do[:, 128 * p:128 * (p + 1)] * o[:, 128 * p:128 * (p + 1)]
        d0 = jnp.sum(jnp.where(lane < DIL_HEAD_DIM, prod, 0.0), axis=-1, keepdims=True)
        d1 = jnp.sum(jnp.where(lane >= DIL_HEAD_DIM, prod, 0.0), axis=-1, keepdims=True)
        stats.append(jnp.where(lane == 0, d0, jnp.where(lane == 1, d1, 0.0)))
    return do, jnp.concatenate(stats, axis=1)


MEM_T = 2048
QM_BLK = C_QM // MEM_HEAD_DIM


def _mem_attn_fwd(proj, kv, *, S):
    scale = MEM_HEAD_DIM ** -0.5

    def body(q_ref, k_ref, v_ref, o_ref, om_ref, lse_ref):
        s = _dot(q_ref[...], k_ref[...], "nt") * scale
        m = jnp.max(s, axis=-1, keepdims=True)
        p = jnp.exp(s - m)
        den = jnp.sum(p, axis=-1, keepdims=True)
        o = _dot(p, v_ref[...]) / den
        o_ref[...] = o
        om_ref[...] = _mx(o)
        lse_ref[0] = m + jnp.log(den)

    return pl.pallas_call(
        body, name="mem_attn_fwd", grid=(S // MEM_T, MEM_HEADS),
        in_specs=[pl.BlockSpec((MEM_T, MEM_HEAD_DIM), lambda i, h: (i, QM_BLK + h)),
                  pl.BlockSpec((N_MEM, MEM_HEAD_DIM), lambda i, h: (0, h)),
                  pl.BlockSpec((N_MEM, MEM_HEAD_DIM), lambda i, h: (0, MEM_HEADS + h))],
        out_specs=[pl.BlockSpec((MEM_T, MEM_HEAD_DIM), lambda i, h: (i, h)),
                   pl.BlockSpec((MEM_T, MEM_HEAD_DIM), lambda i, h: (i, h)),
                   pl.BlockSpec((1, MEM_T, 1), lambda i, h: (h, i, 0))],
        out_shape=[jax.ShapeDtypeStruct((S, MEM_WIDTH), F32), jax.ShapeDtypeStruct((S, MEM_WIDTH), MXU_DTYPE),
                   jax.ShapeDtypeStruct((MEM_HEADS, S, 1), F32)],
        compiler_params=_cparams(("parallel", "parallel")),
    )(proj, kv, kv)


def _mem_attn_bwd(proj, kv, om, lse, dy, w_out, *, S):
    scale = MEM_HEAD_DIM ** -0.5

    def body(q_ref, k_ref, v_ref, o_ref, lse_ref, dy_ref, wo_ref, dq_ref, dk_ref, dv_ref):
        @pl.when(pl.program_id(1) == 0)
        def _():
            dk_ref[...] = jnp.zeros_like(dk_ref)
            dv_ref[...] = jnp.zeros_like(dv_ref)

        qv, kv_, vv, dov = q_ref[...], k_ref[...], v_ref[...], _dot(dy_ref[...], wo_ref[...], "nt")
        p = jnp.exp(_dot(qv, kv_, "nt") * scale - lse_ref[0])
        delta = jnp.sum(dov * o_ref[...], axis=-1, keepdims=True)
        ds = p * (_dot(dov, vv, "nt") - delta)
        dq_ref[...] = (_dot(ds, kv_) * scale).astype(dq_ref.dtype)
        dk_ref[...] += _dot(ds, qv, "tn") * scale
        dv_ref[...] += _dot(p, dov, "tn")

    tile = pl.BlockSpec((MEM_T, MEM_HEAD_DIM), lambda h, i: (i, h))
    kvo = pl.BlockSpec((N_MEM, MEM_HEAD_DIM), lambda h, i: (0, h))
    return pl.pallas_call(
        body, name="mem_attn_bwd", grid=(MEM_HEADS, S // MEM_T),
        in_specs=[pl.BlockSpec((MEM_T, MEM_HEAD_DIM), lambda h, i: (i, QM_BLK + h)),
                  pl.BlockSpec((N_MEM, MEM_HEAD_DIM), lambda h, i: (0, h)),
                  pl.BlockSpec((N_MEM, MEM_HEAD_DIM), lambda h, i: (0, MEM_HEADS + h)),
                  tile, pl.BlockSpec((1, MEM_T, 1), lambda h, i: (h, i, 0)),
                  pl.BlockSpec((MEM_T, D_MODEL), lambda h, i: (i, 0)),
                  pl.BlockSpec((MEM_HEAD_DIM, D_MODEL), lambda h, i: (h, 0))],
        out_specs=[tile, kvo, kvo],
        out_shape=[jax.ShapeDtypeStruct((S, MEM_WIDTH), MXU_DTYPE), jax.ShapeDtypeStruct((N_MEM, MEM_WIDTH), F32),
                   jax.ShapeDtypeStruct((N_MEM, MEM_WIDTH), F32)],
        compiler_params=_cparams(("parallel", "arbitrary")),
    )(proj, kv, kv, om, lse, dy, w_out)


MIX_BM = 1024
MIX_BN = 256
GATES_BLK = C_GATES // MIX_BN


def _mix_specs(j_outer):
    ix = (lambda f: (lambda j, i: f(i, j))) if j_outer else (lambda f: f)
    act = lambda width: pl.BlockSpec((MIX_BM, width), ix(lambda i, j: (i, 0)))
    wgt = lambda width: pl.BlockSpec((width, MIX_BN), ix(lambda i, j: (0, j)))
    gate = lambda b: pl.BlockSpec((MIX_BM, MIX_BN), ix(lambda i, j: (i, GATES_BLK + 4 * b + j)))
    bias = lambda b: pl.BlockSpec((1, MIX_BN), ix(lambda i, j: (0, 4 * b + j)))
    tile = pl.BlockSpec((MIX_BM, MIX_BN), ix(lambda i, j: (i, j)))
    return act, wgt, gate, bias, tile


def _mix_fwd(z_lru, o_dil, om, w_lru, w_dil, w_mem, proj, b_gate, *, S):
    act, wgt, gate, bias, tile = _mix_specs(False)

    def body(zl, od, mo, wl, wd, wm, g0, g1, g2, b0, b1, b2, o_ref, t0, t1, t2):
        acc = None
        for a_ref, w_ref, g_ref, b_ref, t_ref in ((zl, wl, g0, b0, t0), (od, wd, g1, b1, t1), (mo, wm, g2, b2, t2)):
            gt = jax.nn.sigmoid(g_ref[...] + b_ref[...])
            t_ref[...] = gt.astype(t_ref.dtype)
            term = gt * _dot(a_ref[...], w_ref[...])
            acc = term if acc is None else acc + term
        o_ref[...] = acc.astype(o_ref.dtype)

    return pl.pallas_call(
        body, name="mix_fwd", grid=(S // MIX_BM, D_MODEL // MIX_BN),
        in_specs=[act(D_RNN), act(256), act(MEM_WIDTH), wgt(D_RNN), wgt(256), wgt(MEM_WIDTH),
                  gate(0), gate(1), gate(2), bias(0), bias(1), bias(2)],
        out_specs=[tile] * 4, out_shape=[jax.ShapeDtypeStruct((S, D_MODEL), MXU_DTYPE)] * 4,
        compiler_params=_cparams(("parallel", "parallel")),
    )(z_lru, o_dil, om, w_lru, w_dil, w_mem, proj, proj, proj, b_gate, b_gate, b_gate)


def _mix_bwd(dx1, w_out, z_lru, o_dil, om, w_lru, w_dil, w_mem, gates, *, S):
    act, wgt, _, _, tile = _mix_specs(False)
    n_j = D_MODEL // MIX_BN

    def body(dx, wo, zl, od, mo, wl, wd, wm, t0, t1, t2,
             dg0, dg1, dg2, dy0, dy1, dy2, db0, db1, db2):
        j = pl.program_id(1)

        @pl.when((pl.program_id(0) == 0) & (j == 0))
        def _():
            for r in (db0, db1, db2):
                r[...] = jnp.zeros_like(r)

        dmv = _dot(dx[...], wo[...], "nt")
        for act_ref, w_ref, t_ref, dg_ref, dy_ref, db_ref in (
                (zl, wl, t0, dg0, dy0, db0), (od, wd, t1, dg1, dy1, db1), (mo, wm, t2, dg2, dy2, db2)):
            y = _dot(act_ref[...], w_ref[...])
            gt = t_ref[...].astype(F32)
            dgate = dmv * y * gt * (1.0 - gt)
            dg_ref[...] = dgate.astype(dg_ref.dtype)
            dy_ref[...] = (dmv * gt).astype(dy_ref.dtype)
            db_ref[j] += _colsum(dgate)

    big = jax.ShapeDtypeStruct((S, D_MODEL), MXU_DTYPE)
    vec = jax.ShapeDtypeStruct((n_j, 1, MIX_BN), F32)
    vspec = pl.BlockSpec((n_j, 1, MIX_BN), lambda i, j: (0, 0, 0))
    res = pl.pallas_call(
        body, name="mix_bwd", grid=(S // MIX_BM, n_j),
        in_specs=[pl.BlockSpec((MIX_BM, D_MODEL), lambda i, j: (i, 0)), pl.BlockSpec((MIX_BN, D_MODEL), lambda i, j: (j, 0)),
                  act(D_RNN), act(256), act(MEM_WIDTH), wgt(D_RNN), wgt(256), wgt(MEM_WIDTH), tile, tile, tile],
        out_specs=[tile] * 6 + [vspec] * 3, out_shape=[big] * 6 + [vec] * 3,
        compiler_params=_cparams(("arbitrary", "arbitrary")),
    )(dx1, w_out, z_lru, o_dil, om, w_lru, w_dil, w_mem, *gates)
    return list(res[:6]) + [r.reshape(1, D_MODEL) for r in res[6:]]


def _adamw_math(w, g, m, v):
    m = ADAM_B1 * m + (1.0 - ADAM_B1) * g
    v = ADAM_B2 * v + (1.0 - ADAM_B2) * (g * g)
    m_hat = m / (1.0 - ADAM_B1 ** ADAM_STEP)
    v_hat = v / (1.0 - ADAM_B2 ** ADAM_STEP)
    delta = -ADAM_LR * (m_hat / (jnp.sqrt(v_hat) + ADAM_EPS) + ADAM_WD * w)
    return delta, m, v


def _adamw_landed(w, own, land, m, v, *, name, col_blk=0, prev=None):
    R = w.shape[0]
    n_parts, C = land.shape[0], land.shape[2]
    br = next(d for d in (256, 464, 128) if R % d == 0)
    tile = pl.BlockSpec((br, C), lambda i: (i, col_blk))
    part = pl.BlockSpec((br, C), lambda i: (i, 0))
    n_prev = 0 if prev is None else 4

    def body(w_ref, o_ref, l_ref, m_ref, v_ref, *rest):
        g_ref, d_ref, nm_ref, nv_ref = rest[n_prev:]
        g = o_ref[...].astype(F32)
        for p in range(n_parts):
            g = g + l_ref[p].astype(F32)
        d, nm, nv = _adamw_math(w_ref[...], g, m_ref[...], v_ref[...])
        g_ref[...] = g
        d_ref[...] = d
        nm_ref[...] = nm
        nv_ref[...] = nv

    return pl.pallas_call(
        body, name=name, grid=(R // br,),
        in_specs=[tile, part, pl.BlockSpec((n_parts, br, C), lambda i: (0, i, 0)), tile, tile]
        + [pl.BlockSpec(memory_space=pl.ANY)] * n_prev,
        out_specs=[tile] * 4, out_shape=[jax.ShapeDtypeStruct(w.shape, F32)] * 4,
        input_output_aliases={5 + j: j for j in range(n_prev)},
        compiler_params=_cparams(("parallel",)),
    )(w, own, land, m, v, *(prev or ()))


def _adamw_plain(w, g, m, v, *, name):
    def body(w_ref, g_ref, m_ref, v_ref, d_ref, nm_ref, nv_ref):
        d, nm, nv = _adamw_math(w_ref[...], g_ref[...], m_ref[...], v_ref[...])
        d_ref[...] = d
        nm_ref[...] = nm
        nv_ref[...] = nv

    return pl.pallas_call(
        body, name=name, out_shape=[jax.ShapeDtypeStruct(w.shape, F32)] * 3, compiler_params=_cparams(),
    )(w, g, m, v)


def _my_pos():
    return lax.axis_index("x"), lax.axis_index("y"), lax.axis_index("c")


def _dev_index(p):
    return 4 * p[0] + 2 * p[1] + p[2]


def _peers(me):
    x, y, c = me
    out = []
    for k in range(1, 8):
        fx, fy, fc = (k >> 2) & 1, (k >> 1) & 1, k & 1
        out.append((k - 1, (1 - x if fx else x, 1 - y if fy else y, 1 - c if fc else c)))
    return out


HBM_SPEC = pl.BlockSpec(memory_space=pltpu.HBM)
SEM_SPEC = pl.BlockSpec(memory_space=pltpu.SEMAPHORE)
DATAFLOW_EFFECT = pltpu.SideEffectType.DATAFLOW_SIDE_EFFECTING


def _gather_refs(src, land, me, peer, k):
    return src, land.at[_dev_index(me)]


def _scatter_refs(src, land, me, peer, k):
    return src.at[_dev_index(peer)], land.at[k]


ALL_RELATIONS = tuple(range(7))
ONE_PER_CHIP = (0, 1, 3, 5)


def _push_start(srcs, land_shapes, refs_of, name, after=(), relations=ALL_RELATIONS):
    n, n_after = len(srcs), len(after)

    def body(*refs):
        ins, lands = refs[:n], refs[n:2 * n]
        send_sems, recv_sems, token = refs[2 * n + n_after], refs[2 * n + n_after + 1], refs[-1]
        me = _my_pos()
        for k, peer in _peers(me):
            if k not in relations:
                continue
            for a in range(n):
                src, dst = refs_of(ins[a], lands[a], me, peer, k)
                pltpu.make_async_remote_copy(src_ref=src, dst_ref=dst, send_sem=send_sems.at[7 * a + k],
                                             recv_sem=recv_sems.at[7 * a + k], device_id=peer, device_id_type=MESH).start()
        token[...] = jnp.zeros_like(token)

    lands = [lax.empty(shp, s.dtype) for shp, s in zip(land_shapes, srcs)]
    hbm = lambda a: pltpu.with_memory_space_constraint(a, pltpu.HBM)
    res = pl.pallas_call(
        body, name=name,
        out_shape=(pltpu.SemaphoreType.DMA((7 * n,)), pltpu.SemaphoreType.DMA((7 * n,)),
                   *[pltpu.HBM(s.shape, s.dtype) for s in srcs], *[pltpu.HBM(l.shape, l.dtype) for l in lands],
                   jax.ShapeDtypeStruct((8, 128), F32)),
        in_specs=[HBM_SPEC] * (2 * n) + [pl.BlockSpec(memory_space=pl.ANY)] * n_after,
        out_specs=(SEM_SPEC, SEM_SPEC, *[HBM_SPEC] * (2 * n), pl.BlockSpec(memory_space=pltpu.VMEM)),
        input_output_aliases={i: 2 + i for i in range(2 * n)},
        compiler_params=pltpu.CompilerParams(has_side_effects=DATAFLOW_EFFECT),
    )(*[hbm(s) for s in srcs], *[hbm(l) for l in lands], *after)
    return dict(sems=(res[0], res[1]), srcs=list(res[2:2 + n]), lands=list(res[2 + n:2 + 2 * n]), token=res[-1], n=n,
                refs_of=refs_of, name=name, relations=relations)


def _push_wait(started, after):
    n, refs_of, relations = started["n"], started["refs_of"], started["relations"]
    after = list(after) if isinstance(after, (list, tuple)) else [after]

    def body(*refs):
        ins, lands = refs[:n], refs[n:2 * n]
        send_sems, recv_sems = refs[2 * n], refs[2 * n + 1]
        me = _my_pos()
        for k, peer in _peers(me):
            if k not in relations:
                continue
            for a in range(n):
                src, dst = refs_of(ins[a], lands[a], me, peer, k)
                cp = pltpu.make_async_remote_copy(src_ref=src, dst_ref=dst, send_sem=send_sems.at[7 * a + k],
                                                  recv_sem=recv_sems.at[7 * a + k], device_id=peer, device_id_type=MESH)
                cp.wait_send()
                cp.wait_recv()

    arrs = started["srcs"] + started["lands"]
    res = pl.pallas_call(
        body, name=started["name"].replace("start", "wait"),
        out_shape=tuple(pltpu.HBM(a.shape, a.dtype) for a in arrs),
        in_specs=[HBM_SPEC] * (2 * n) + [SEM_SPEC, SEM_SPEC] + [pl.BlockSpec(memory_space=pl.ANY)] * len(after),
        out_specs=tuple([HBM_SPEC] * (2 * n)),
        input_output_aliases={i: i for i in range(2 * n)},
        compiler_params=pltpu.CompilerParams(has_side_effects=DATAFLOW_EFFECT),
    )(*arrs, *started["sems"], *after)
    return list(res[n:2 * n])


def _other_chips(x, y):
    return ((1 - x, y), (x, 1 - y), (1 - x, 1 - y))


def _forward_start(land, name, after=()):
    n_after = len(after)

    def body(*refs):
        land_ref, send_sems, recv_sems, token = refs[0], refs[1 + n_after], refs[2 + n_after], refs[-1]
        x, y, c = _my_pos()
        for j, (cx, cy) in enumerate(_other_chips(x, y)):
            blk = land_ref.at[_dev_index((cx, cy, c))]
            pltpu.make_async_remote_copy(src_ref=blk, dst_ref=blk, send_sem=send_sems.at[j], recv_sem=recv_sems.at[j],
                                         device_id=(x, y, 1 - c), device_id_type=MESH).start()
        token[...] = jnp.zeros_like(token)

    res = pl.pallas_call(
        body, name=name,
        out_shape=(pltpu.SemaphoreType.DMA((3,)), pltpu.SemaphoreType.DMA((3,)), pltpu.HBM(land.shape, land.dtype),
                   jax.ShapeDtypeStruct((8, 128), F32)),
        in_specs=[HBM_SPEC] + [pl.BlockSpec(memory_space=pl.ANY)] * n_after,
        out_specs=(SEM_SPEC, SEM_SPEC, HBM_SPEC, pl.BlockSpec(memory_space=pltpu.VMEM)),
        input_output_aliases={0: 2},
        compiler_params=pltpu.CompilerParams(has_side_effects=DATAFLOW_EFFECT),
    )(pltpu.with_memory_space_constraint(land, pltpu.HBM), *after)
    return dict(sems=(res[0], res[1]), land=res[2], token=res[3], name=name)


def _forward_wait(started, after):
    after = list(after) if isinstance(after, (list, tuple)) else [after]

    def body(land_ref, send_sems, recv_sems, *rest):
        x, y, c = _my_pos()
        for j, (cx, cy) in enumerate(_other_chips(x, y)):
            cp = pltpu.make_async_remote_copy(
                src_ref=land_ref.at[_dev_index((cx, cy, c))], dst_ref=land_ref.at[_dev_index((cx, cy, 1 - c))],
                send_sem=send_sems.at[j], recv_sem=recv_sems.at[j], device_id=(x, y, 1 - c), device_id_type=MESH)
            cp.wait_send()
            cp.wait_recv()

    land = started["land"]
    return pl.pallas_call(
        body, name=started["name"].replace("start", "wait"), out_shape=pltpu.HBM(land.shape, land.dtype),
        in_specs=[HBM_SPEC, SEM_SPEC, SEM_SPEC] + [pl.BlockSpec(memory_space=pl.ANY)] * len(after),
        out_specs=HBM_SPEC, input_output_aliases={0: 0},
        compiler_params=pltpu.CompilerParams(has_side_effects=DATAFLOW_EFFECT),
    )(land, *started["sems"], *after)


def _sum_slots(slots):
    def body(in_ref, out_ref):
        acc = in_ref[0]
        for d in range(1, N_DEV):
            acc = acc + in_ref[d]
        out_ref[...] = acc

    return pl.pallas_call(body, name="sum_small", out_shape=jax.ShapeDtypeStruct(slots.shape[1:], F32),
                          compiler_params=_cparams())(slots)


def _adamw_many(ws, gs, ms, vs):
    n = len(ws)

    def body(*refs):
        for i in range(n):
            w_ref, g_ref, m_ref, v_ref = (refs[j * n + i] for j in range(4))
            d_, nm, nv = _adamw_math(w_ref[...], g_ref[...], m_ref[...], v_ref[...])
            for j, val in enumerate((d_, nm, nv)):
                refs[(4 + j) * n + i][...] = val

    res = pl.pallas_call(body, name="adamw_small", out_shape=[jax.ShapeDtypeStruct(w_.shape, F32) for w_ in ws] * 3,
                         compiler_params=_cparams())(*ws, *gs, *ms, *vs)
    return [(res[i], res[n + i], res[2 * n + i]) for i in range(n)]


def _local_step(x, mem, tgt, W, P, late_weights, send_grads, reduce_small, tie0):
    S = x.shape[0]
    W = dict(W)
    h = _rmsnorm_fwd(x, P["g_mix"] + tie0, rows=S, name="norm_mix")
    mem_n = _rmsnorm_fwd(mem, P["g_mem"], rows=N_MEM, name="norm_mem")
    buckets = _dil_buckets()
    bias = _dil_bias(P["rel_bias"], buckets)
    wa_bd, wx_bd = _mx(_block_diag(P["w_rg_a"])), _mx(_block_diag(P["w_rg_x"]))
    W.update(late_weights("first", [h, mem_n, bias, wa_bd, wx_bd]))
    proj = _matmul(h, W["w_in_t"], M=S, N=D_IN, K=D_MODEL, mode="nt", bm=512, bn=D_IN // 2, bk=D_MODEL, name="mm_in",
                   j_outer=True, deps=[W["started"]])

    group_out = [_dilated_fwd(proj, bias, g, S=S) for g in range(len(DIL_GROUPS))]
    o_dil, o_dil_m, lse_dil = _dilated_merge([o for o, _ in group_out], [l for _, l in group_out], S=S)

    W.update(late_weights("branch", [o_dil]))
    lru_args = (W["conv_w"], P["conv_b"].reshape(1, -1), wa_bd, wx_bd, P["b_rg_a"].reshape(1, -1),
                P["b_rg_x"].reshape(1, -1), P["lru_lambda"].reshape(1, -1))
    hl, z_lru, a_lru, mult_lru = _lru_fwd(proj, *lru_args, S=S)
    kv = _matmul(mem_n, W["w_mem_kv"], M=N_MEM, N=2 * MEM_WIDTH, K=D_MODEL, mode="nn", bm=N_MEM, bn=512, bk=D_MODEL,
                 name="mm_kv")
    om, om_m, lse_mem = _mem_attn_fwd(proj, kv, S=S)
    b_gate = P["b_gate"].reshape(1, -1)
    merged, *gates = _mix_fwd(z_lru, o_dil_m, om_m, W["w_lru_out"], W["w_dil_out"], W["w_mem_out"], proj, b_gate, S=S)
    g_mlp, g_final, g_mix = (P[n].reshape(1, D_MODEL) for n in ("g_mlp", "g_final", "g_mix"))
    x1, hm = _matmul_rows(merged, W["w_out"], M=S, K=D_MODEL, mode="nn", bm=512, name="mm_out",
                          row_fn=_residual_then_norm, out_dtypes=(F32, MXU_DTYPE), tiles=[x], vecs=[g_mlp])
    W.update(late_weights("mlp", [hm]))

    def relu2(acc):
        rl = jnp.maximum(acc, 0.0)
        return rl * rl, rl

    act, relu_u = _matmul(hm, W["w_mlp_in_t"], M=S, N=D_FF, K=D_MODEL, mode="nt", bm=1024, bn=1024, bk=D_MODEL,
                          name="mm_mlp_in", out_dtypes=(MXU_DTYPE, MXU_DTYPE), epilogue=relu2, j_outer=True)
    dx2, dx2_m, loss, dg_final = _matmul_rows(
        act, W["w_mlp_out"], M=S, K=D_FF, mode="nn", bm=512, name="mm_mlp_out", row_fn=_residual_then_loss,
        out_dtypes=(F32, MXU_DTYPE), tiles=[x1, tgt], vecs=[g_final], acc_widths=(1, D_MODEL))

    G, Gs = {}, {}
    Gs["g_final"] = dg_final
    dw = dict(mode="tn", K=S, bk=S, out_dtypes=(MXU_DTYPE,))
    G["w_mlp_out"] = _matmul(act, dx2_m, M=D_FF, N=D_MODEL, bm=512, bn=D_MODEL, name="mm_dw_mlp_out",
                             parts=("rows", D_FF // N_DEV), **dw)
    du = _matmul(dx2_m, W["w_mlp_out"], M=S, N=D_FF, K=D_MODEL, mode="nt", bm=1024, bn=1024, bk=D_MODEL, name="mm_du",
                 out_dtypes=(MXU_DTYPE,), epilogue=lambda acc, rl: (acc * (2.0 * rl.astype(F32)),),
                 extras=[(relu_u, (0, 0))], j_outer=True)
    G["w_mlp_in"] = _matmul(hm, du, M=D_MODEL, N=D_FF, bm=D_MODEL, bn=512, name="mm_dw_mlp_in",
                            parts=("cols", D_FF // N_DEV), **dw)
    tie1 = send_grads({n: G.pop(n) for n in ("w_mlp_out", "w_mlp_in")})
    dx1, dx1_m, Gs["g_mlp"] = _matmul_rows(
        du, W["w_mlp_in_t"], M=S, K=D_FF, mode="nn", bm=512, name="mm_dhm", row_fn=_norm_bwd_then_residual(2),
        out_dtypes=(F32, MXU_DTYPE), tiles=[x1, dx2], vecs=[g_mlp], acc_widths=(D_MODEL,), deps=[tie1])
    G["w_out"] = _matmul(merged, dx1_m, M=D_MODEL, N=D_MODEL, bm=512, bn=D_MODEL, name="mm_dw_out",
                         parts=("rows", D_MODEL // N_DEV), **dw)
    (dg0, dg1, dg2, dy_lru, dy_dil, dy_mem, db0, db1, db2) = _mix_bwd(
        dx1_m, W["w_out"], z_lru, o_dil_m, om_m, W["w_lru_out"], W["w_dil_out"], W["w_mem_out"], gates, S=S)
    Gs["b_gate0"], Gs["b_gate1"], Gs["b_gate2"] = db0, db1, db2

    G["w_mem_out"] = _matmul(om_m, dy_mem, M=MEM_WIDTH, N=D_MODEL, bm=MEM_WIDTH, bn=D_MODEL, name="mm_dw_mem_out",
                             parts=("cols", D_MODEL // N_DEV), **dw)
    dqm, dk_mem, dv_mem = _mem_attn_bwd(proj, kv, om, lse_mem, dy_mem, W["w_mem_out"], S=S)
    dkv = jnp.concatenate([dk_mem, dv_mem], axis=1)
    G["w_mem_kv"] = _matmul(mem_n, dkv, M=D_MODEL, N=2 * MEM_WIDTH, K=N_MEM, mode="tn", bm=D_MODEL, bn=2 * MEM_WIDTH,
                            bk=N_MEM, name="mm_dw_kv", out_dtypes=(MXU_DTYPE,), parts=("rows", D_MODEL // N_DEV))
    dmem_n = _matmul(dkv, W["w_mem_kv"], M=N_MEM, N=D_MODEL, K=2 * MEM_WIDTH, mode="nt", bm=N_MEM, bn=D_MODEL,
                     bk=2 * MEM_WIDTH, name="mm_dmem")
    (Gs["g_mem"],) = _rmsnorm_bwd(mem, P["g_mem"], dmem_n, None, rows=N_MEM, name="norm_mem_bwd", dx_dtypes=())

    G["w_dil_out"] = _matmul(o_dil_m, dy_dil, M=256, N=D_MODEL, bm=256, bn=D_MODEL, name="mm_dw_dil_out",
                             parts=("cols", D_MODEL // N_DEV), **dw)
    do_dil, delta = _matmul(dy_dil, W["w_dil_out"], M=S, N=256, K=D_MODEL, mode="nt", bm=512, bn=256, bk=D_MODEL,
                            name="mm_do_dil", out_dtypes=(F32, F32), epilogue=_with_delta, extras=[(o_dil, (0, 0))])
    G["w_lru_out"] = _matmul(z_lru, dy_lru, M=D_RNN, N=D_MODEL, bm=D_RNN, bn=D_MODEL, name="mm_dw_lru_out",
                             parts=("cols", D_MODEL // N_DEV), **dw)
    tie2 = send_grads({n: G.pop(n) for n in ("w_out", "w_mem_out", "w_mem_kv", "w_dil_out", "w_lru_out")})
    bias = bias + tie2[0, 0]
    dqkv, dbias = None, []
    for g in range(len(DIL_GROUPS)):
        *dqkv, db_g = _dilated_bwd(proj, do_dil, lse_dil, delta, bias, g, S=S, into=dqkv)
        dbias.append(db_g)
    drel = _dil_bias_bwd(jnp.stack(dbias, axis=0).reshape(len(DIL_GROUPS), DIL_HEADS, SPAN, 2 * SPAN), buckets)
    Gs["rel_bias"] = drel

    dxl, dgl, dcw, dcb, dwa, dwx, dba, dbx, dlam = _lru_bwd(proj, hl, a_lru, mult_lru, dy_lru, W["w_lru_out"], *lru_args,
                                                            S=S)
    Gs["conv_w"], Gs["conv_b"] = dcw, dcb
    Gs["w_rg_a"], Gs["w_rg_x"] = _block_diag_extract(dwa), _block_diag_extract(dwx)
    Gs["b_rg_a"], Gs["b_rg_x"], Gs["lru_lambda"] = dba, dbx, dlam
    Gs["loss"] = loss

    dproj = [dxl, dgl] + dqkv + [dqm, dg0, dg1, dg2]
    tie = []
    for q in range(W_IN_PIECES):
        dw_q = None
        for half in range(2):
            dw_q = _dw_in_t_half(h, dproj, q, half, S=S, name=f"mm_dw_in_{q}_{half}", into=dw_q, deps=tie)
        tie = [send_grads({f"w_in_{q}": dw_q})]
    grad_x, Gs["g_mix"] = _matmul_rows(
        dproj, W["w_in_t"], M=S, K=D_IN, mode="nn", bm=256, name="mm_dh", row_fn=_norm_bwd_then_residual(1),
        out_dtypes=(F32,), tiles=[x, dx1], vecs=[g_mix], acc_widths=(D_MODEL,), deps=tie)
    return grad_x, reduce_small(Gs)


BIG = ("w_in", "w_lru_out", "w_dil_out", "w_mem_kv", "w_mem_out", "w_out", "w_mlp_in", "w_mlp_out")
W_IN_PIECES = 2
COL_SHARDED = ("w_lru_out", "w_dil_out", "w_mem_out", "w_mlp_in")
GATHERED_TRANSPOSED = ("w_mlp_in",)
SMALL = ("g_mix", "b_gate", "conv_b", "w_rg_a", "b_rg_a", "w_rg_x", "b_rg_x", "lru_lambda", "rel_bias", "g_mem",
         "g_mlp", "g_final")
WEIGHTS = ("g_mix", "w_in", "b_gate", "conv_w", "conv_b", "w_rg_a", "b_rg_a", "w_rg_x", "b_rg_x", "lru_lambda",
           "w_lru_out", "rel_bias", "w_dil_out", "g_mem", "w_mem_kv", "w_mem_out", "w_out", "g_mlp", "w_mlp_in",
           "w_mlp_out", "g_final")


def _gathered_to_full(name, gathered):
    if name in COL_SHARDED:
        n, r, c = gathered.shape
        return gathered.transpose(1, 0, 2).reshape(r, n * c)
    n, r, c = gathered.shape
    return gathered.reshape(n * r, c)


SMALL_GRADS = (("g_mix", (1, 1024)), ("b_gate0", (1, 1024)), ("b_gate1", (1, 1024)), ("b_gate2", (1, 1024)),
               ("conv_b", (1, 768)), ("w_rg_a", (12, 64, 64)), ("b_rg_a", (1, 768)), ("w_rg_x", (12, 64, 64)),
               ("b_rg_x", (1, 768)), ("lru_lambda", (1, 768)), ("rel_bias", (32, 128)), ("g_mem", (1, 1024)),
               ("g_mlp", (1, 1024)), ("g_final", (1, 1024)), ("conv_w", (4, 768)), ("loss", (1, 1)))


def _pack(parts):
    flat = jnp.concatenate([p.reshape(-1) for p in parts])
    return jnp.pad(flat, (0, (-flat.shape[0]) % 1024)).reshape(-1, 128)


def _unpack(pack, shapes):
    flat = pack.reshape(-1)
    out, off = [], 0
    for shp in shapes:
        size = math.prod(shp)
        out.append(flat[off:off + size].reshape(shp))
        off += size
    return out


def kernel(x, mem, g_mix, w_in, b_gate, conv_w, conv_b, w_rg_a, b_rg_a, w_rg_x, b_rg_x, lru_lambda, w_lru_out, rel_bias, w_dil_out, g_mem, w_mem_kv, w_mem_out, w_out, g_mlp, w_mlp_in, w_mlp_out, g_final, loss_target, m_g_mix, m_w_in, m_b_gate, m_conv_w, m_conv_b, m_w_rg_a, m_b_rg_a, m_w_rg_x, m_b_rg_x, m_lru_lambda, m_w_lru_out, m_rel_bias, m_w_dil_out, m_g_mem, m_w_mem_kv, m_w_mem_out, m_w_out, m_g_mlp, m_w_mlp_in, m_w_mlp_out, m_g_final, v_g_mix, v_w_in, v_b_gate, v_conv_w, v_conv_b, v_w_rg_a, v_b_rg_a, v_w_rg_x, v_b_rg_x, v_lru_lambda, v_w_lru_out, v_rel_bias, v_w_dil_out, v_g_mem, v_w_mem_kv, v_w_mem_out, v_w_out, v_g_mlp, v_w_mlp_in, v_w_mlp_out, v_g_final):
    w = dict(g_mix=g_mix, w_in=w_in, b_gate=b_gate, conv_w=conv_w, conv_b=conv_b, w_rg_a=w_rg_a, b_rg_a=b_rg_a,
             w_rg_x=w_rg_x, b_rg_x=b_rg_x, lru_lambda=lru_lambda, w_lru_out=w_lru_out, rel_bias=rel_bias,
             w_dil_out=w_dil_out, g_mem=g_mem, w_mem_kv=w_mem_kv, w_mem_out=w_mem_out, w_out=w_out, g_mlp=g_mlp,
             w_mlp_in=w_mlp_in, w_mlp_out=w_mlp_out, g_final=g_final)
    m = dict(g_mix=m_g_mix, w_in=m_w_in, b_gate=m_b_gate, conv_w=m_conv_w, conv_b=m_conv_b, w_rg_a=m_w_rg_a,
             b_rg_a=m_b_rg_a, w_rg_x=m_w_rg_x, b_rg_x=m_b_rg_x, lru_lambda=m_lru_lambda, w_lru_out=m_w_lru_out,
             rel_bias=m_rel_bias, w_dil_out=m_w_dil_out, g_mem=m_g_mem, w_mem_kv=m_w_mem_kv, w_mem_out=m_w_mem_out,
             w_out=m_w_out, g_mlp=m_g_mlp, w_mlp_in=m_w_mlp_in, w_mlp_out=m_w_mlp_out, g_final=m_g_final)
    v = dict(g_mix=v_g_mix, w_in=v_w_in, b_gate=v_b_gate, conv_w=v_conv_w, conv_b=v_conv_b, w_rg_a=v_w_rg_a,
             b_rg_a=v_b_rg_a, w_rg_x=v_w_rg_x, b_rg_x=v_b_rg_x, lru_lambda=v_lru_lambda, w_lru_out=v_w_lru_out,
             rel_bias=v_rel_bias, w_dil_out=v_w_dil_out, g_mem=v_g_mem, w_mem_kv=v_w_mem_kv, w_mem_out=v_w_mem_out,
             w_out=v_w_out, g_mlp=v_g_mlp, w_mlp_in=v_w_mlp_in, w_mlp_out=v_w_mlp_out, g_final=v_g_final)

    my_idx = _dev_index(_my_pos())

    w_in_shard = _mx(w["w_in"].T)
    first = _push_start([w_in_shard], [(N_DEV,) + w_in_shard.shape], _gather_refs, "gather_in_start",
                        relations=ONE_PER_CHIP)
    cw_cols = D_RNN // N_DEV
    conv_pad = jnp.zeros((64, D_MODEL), F32).at[:CONV_WIDTH, :cw_cols].set(w["conv_w"])
    late = {}
    P = {n: w[n] for n in SMALL}

    def start_late(order_after):
        for group, names in (("branch", ("w_mem_kv", "w_lru_out", "w_dil_out", "w_mem_out", "w_out", "conv_w")),
                             ("mlp", ("w_mlp_in", "w_mlp_out"))):
            shards = [conv_pad if n == "conv_w" else _mx(w[n].T if n in GATHERED_TRANSPOSED else w[n]) for n in names]
            started = _push_start(shards, [(N_DEV,) + s.shape for s in shards], _gather_refs, f"gather_{group}_start",
                                  after=order_after)
            late[group] = (names, shards, started)
            order_after = [started["token"]]

    start_late([first["token"]])

    def late_weights(group, after):
        if group == "first":
            (land,) = _push_wait(first, after)
            forward = _forward_start(land, "forward_in_start")
            full = lax.dynamic_update_index_in_dim(_forward_wait(forward, forward["token"]), w_in_shard, my_idx, 0)
            return {"w_in_t": full.reshape(D_IN, D_MODEL), "started": late["mlp"][2]["token"]}
        names, shards, started = late[group]
        out = {}
        for n, land, own in zip(names, _push_wait(started, after), shards):
            full = lax.dynamic_update_index_in_dim(land, own, my_idx, 0)
            if n == "conv_w":
                out[n] = full[:, :CONV_WIDTH, :cw_cols].transpose(1, 0, 2).reshape(CONV_WIDTH, D_RNN)
            elif n in GATHERED_TRANSPOSED:
                out[n + "_t"] = full.reshape(-1, full.shape[2])
            else:
                out[n] = _gathered_to_full(n, full)
        return out

    sent, small = [], {}

    def send_grads(gs):
        names = list(gs)
        parts = [gs[n] for n in names]
        own = [lax.dynamic_index_in_dim(p, my_idx, 0, keepdims=False) for p in parts]
        started = _push_start(parts, [(N_DEV - 1,) + p.shape[1:] for p in parts], _scatter_refs,
                              f"scatter{len(sent)}_start")
        sent.append((names, own, started))
        return started["token"]

    def reduce_small(gs):
        small["pack"] = _pack([gs[n] for n, _ in SMALL_GRADS])
        small["started"] = _push_start([small["pack"]], [(N_DEV,) + small["pack"].shape], _gather_refs, "small_start")
        return small["started"]["token"]

    grad_x, last_token = _local_step(x[0], mem[0], loss_target[0], {}, P, late_weights, send_grads, reduce_small,
                                     late["mlp"][2]["token"][0, 0])

    grads, deltas, new_m, new_v = {}, {}, {}, {}
    after = last_token
    for names, own, started in sent[:-W_IN_PIECES]:
        for n, o, land in zip(names, own, _push_wait(started, after)):
            grads[n], deltas[n], new_m[n], new_v[n] = _adamw_landed(w[n], o, land, m[n], v[n], name=f"adamw_{n}")
            after = deltas[n]
    prev = None
    for q, (names, own, started) in enumerate(sent[-W_IN_PIECES:]):
        (land,) = _push_wait(started, after)
        prev = _adamw_landed(w["w_in"].T, own[0], land, m["w_in"].T, v["w_in"].T, name=f"adamw_{names[0]}",
                             col_blk=q, prev=prev)
        after = prev[1]
    grads["w_in"], deltas["w_in"], new_m["w_in"], new_v["w_in"] = [t.T for t in prev]
    (small_land,) = _push_wait(small["started"], [after] + [deltas[n] for n in BIG if n != "w_in"])
    total = _sum_slots(lax.dynamic_update_index_in_dim(small_land, small["pack"], my_idx, 0))
    summed = dict(zip([n for n, _ in SMALL_GRADS], _unpack(total, [shp for _, shp in SMALL_GRADS])))
    summed["b_gate"] = jnp.concatenate([summed.pop(f"b_gate{b}") for b in range(3)], axis=1)
    summed["rel_bias"] = summed["rel_bias"][:, :3 * DIL_HEADS]
    for n in SMALL:
        grads[n] = summed[n].reshape(w[n].shape)
    small_updates = _adamw_many([w[n] for n in SMALL], [grads[n] for n in SMALL], [m[n] for n in SMALL],
                                [v[n] for n in SMALL])
    for n, (d_, nm_, nv_) in zip(SMALL, small_updates):
        deltas[n], new_m[n], new_v[n] = d_, nm_, nv_
    conv_w_sum, loss_sum = summed["conv_w"], summed["loss"]
    grads["conv_w"] = lax.dynamic_slice(conv_w_sum, (0, my_idx * cw_cols), (CONV_WIDTH, cw_cols))
    deltas["conv_w"], new_m["conv_w"], new_v["conv_w"] = _adamw_plain(
        w["conv_w"], grads["conv_w"], m["conv_w"], v["conv_w"], name="adamw_conv_w")

    return (loss_sum.reshape(()), grad_x[None], *[grads[n] for n in WEIGHTS], *[deltas[n] for n in WEIGHTS],
            *[new_m[n] for n in WEIGHTS], *[new_v[n] for n in WEIGHTS])
```

```python
import functools
import math

import jax
import jax.numpy as jnp
from jax import lax
from jax.experimental import pallas as pl
from jax.experimental.pallas import tpu as pltpu

F32 = jnp.float32
MXU_DTYPE = jnp.bfloat16
VMEM_LIMIT_BYTES = 56 * 1024 * 1024
N_DEV = 8

D_MODEL = 1024
N_MEM = 256
MEM_HEADS = 4
MEM_HEAD_DIM = 128
MEM_WIDTH = 512
D_RNN = 768
LRU_BLOCK = 64
N_LRU_BLOCKS = 12
LRU_GROUP = 256
N_LRU_GROUPS = 3
CONV_WIDTH = 4
LRU_C = 8.0
DIL_GROUPS = ((128, 1), (512, 4), (2048, 16))
SPAN = 128
DIL_HEADS = 4
DIL_HEAD_DIM = 64
NUM_BUCKETS = 32
MAX_DISTANCE = 2048
D_FF = 4096
D_IN = 7424
EPS = 1e-6
NEG = -1e30
C_XL, C_GATE, C_QKV, C_QM, C_GATES = 0, 768, 1536, 3840, 4352

ADAM_LR = 0.001
ADAM_B1 = 0.9
ADAM_B2 = 0.999
ADAM_EPS = 1e-08
ADAM_WD = 0.01
ADAM_STEP = 10

MESH = pl.DeviceIdType.MESH
GELU_K = math.sqrt(2.0 / math.pi)


def _cparams(sem=None):
    kw = dict(vmem_limit_bytes=VMEM_LIMIT_BYTES)
    if sem is not None:
        kw["dimension_semantics"] = sem
    return pltpu.CompilerParams(**kw)


def _mx(v):
    return v.astype(MXU_DTYPE)


def _dot(a, b, mode="nn"):
    dims = {"nn": (((1,), (0,)), ((), ())), "nt": (((1,), (1,)), ((), ())), "tn": (((0,), (0,)), ((), ()))}[mode]
    return lax.dot_general(_mx(a), _mx(b), dims, preferred_element_type=F32)


def _colsum(v):
    return jnp.sum(v, axis=0, keepdims=True)


def _matmul(a, b, *, M, N, K, mode, bm, bn, bk, name, out_dtypes=(F32,), epilogue=None, extras=(),
            a_off=(0, 0), b_off=(0, 0), j_outer=False, deps=(), parts=None):
    assert M % bm == 0 and N % bn == 0 and K % bk == 0, (name, M, N, K, bm, bn, bk)
    nm, nn, nk = M // bm, N // bn, K // bk

    def ij(f):
        if j_outer:
            return lambda j, i, k: f(i, j, k)
        return f

    if mode == "tn":
        a_spec = pl.BlockSpec((bk, bm), ij(lambda i, j, k: (k + a_off[0], i + a_off[1])))
    else:
        a_spec = pl.BlockSpec((bm, bk), ij(lambda i, j, k: (i + a_off[0], k + a_off[1])))
    if mode == "nt":
        b_spec = pl.BlockSpec((bn, bk), ij(lambda i, j, k: (j + b_off[0], k + b_off[1])))
    else:
        b_spec = pl.BlockSpec((bk, bn), ij(lambda i, j, k: (k + b_off[0], j + b_off[1])))
    ex_specs = [pl.BlockSpec((bm, bn), ij(functools.partial(lambda i, j, k, o: (i + o[0], j + o[1]), o=off)))
                for _, off in extras]
    if parts is None:
        out_dims = (M, N)
        out_spec = pl.BlockSpec((bm, bn), ij(lambda i, j, k: (i, j)))
    elif parts[0] == "rows":
        r = parts[1]
        assert bm % r == 0
        out_dims = (M // r, r, N)
        out_spec = pl.BlockSpec((bm // r, r, bn), ij(lambda i, j, k: (i, 0, j)))
    elif parts[0] == "rows_t":
        r = parts[1]
        assert bn % r == 0
        out_dims = (N // r, r, M)
        out_spec = pl.BlockSpec((bn // r, r, bm), ij(lambda i, j, k: (j, 0, i)))
    else:
        c = parts[1]
        assert bn % c == 0
        out_dims = (N // c, M, c)
        out_spec = pl.BlockSpec((bn // c, bm, c), ij(lambda i, j, k: (j, i, 0)))
    n_ex, n_out, n_dep = len(extras), len(out_dtypes), len(deps)

    def body(*refs):
        a_ref, b_ref = refs[0], refs[1]
        ex = refs[2:2 + n_ex]
        outs = refs[2 + n_ex + n_dep:2 + n_ex + n_dep + n_out]
        part = _dot(a_ref[...], b_ref[...], mode)

        def finish(acc):
            vals = epilogue(acc, *[e[...] for e in ex]) if epilogue is not None else (acc,)
            for o, v in zip(outs, vals):
                if parts is not None and parts[0] == "rows_t":
                    v = v.T
                v = v.astype(o.dtype)
                if parts is None:
                    o[...] = v
                elif parts[0] in ("rows", "rows_t"):
                    for ch in range(v.shape[0] // parts[1]):
                        o[ch] = v[ch * parts[1]:(ch + 1) * parts[1], :]
                else:
                    for ch in range(bn // parts[1]):
                        o[ch] = v[:, ch * parts[1]:(ch + 1) * parts[1]]

        if nk == 1:
            finish(part)
        else:
            acc_ref = refs[-1]
            k = pl.program_id(2)

            @pl.when(k == 0)
            def _():
                acc_ref[...] = part

            @pl.when(k > 0)
            def _():
                acc_ref[...] += part

            @pl.when(k == nk - 1)
            def _():
                finish(acc_ref[...])

    grid = (nn, nm, nk) if j_outer else (nm, nn, nk)
    res = pl.pallas_call(
        body, name=name, grid=grid,
        in_specs=[a_spec, b_spec] + ex_specs + [pl.BlockSpec(memory_space=pl.ANY)] * n_dep,
        out_specs=[out_spec] * n_out,
        out_shape=[jax.ShapeDtypeStruct(out_dims, dt) for dt in out_dtypes],
        scratch_shapes=[pltpu.VMEM((bm, bn), F32)] if nk > 1 else [],
        compiler_params=_cparams(("parallel", "parallel", "arbitrary")),
    )(a, b, *[e for e, _ in extras], *deps)
    return res[0] if n_out == 1 else res


ROW_SUBTILES = 2


def _matmul_rows(a, b, *, M, K, mode, bm, name, row_fn, out_dtypes, tiles=(), vecs=(), acc_widths=(), deps=()):
    N = D_MODEL
    assert M % bm == 0
    segs = list(a) if isinstance(a, (list, tuple)) else [a]
    widths = [s_.shape[1] for s_ in segs]
    assert sum(widths) == K and (len(segs) == 1 or mode == "nn")
    n_s, n_t, n_v, n_o, n_a, n_d = len(segs), len(tiles), len(vecs), len(out_dtypes), len(acc_widths), len(deps)
    row = pl.BlockSpec((bm, N), lambda i: (i, 0))
    b_shape = (K, N) if mode == "nn" else (N, K)

    def body(*refs):
        b_ref = refs[n_s]
        ins = refs[n_s + 1:n_s + 1 + n_t + n_v]
        outs = refs[n_s + 1 + n_t + n_v + n_d:n_s + 1 + n_t + n_v + n_d + n_o]
        accs = refs[n_s + 1 + n_t + n_v + n_d + n_o:]
        for o in accs:
            @pl.when(pl.program_id(0) == 0)
            def _(o=o):
                o[...] = jnp.zeros_like(o)

        for s_ in range(ROW_SUBTILES):
            rows = pl.ds(s_ * (bm // ROW_SUBTILES), bm // ROW_SUBTILES)
            if n_s == 1:
                acc = _dot(refs[0][rows, :], b_ref[...], mode)
            else:
                acc, k0 = None, 0
                for a_ref, w_ in zip(refs[:n_s], widths):
                    part = _dot(a_ref[rows, :], b_ref[k0:k0 + w_, :])
                    acc = part if acc is None else acc + part
                    k0 += w_
            tile_vals, partials = row_fn(acc, *[r[rows, :] for r in ins[:n_t]], *[r[...] for r in ins[n_t:]])
            for o, val in zip(outs, tile_vals):
                o[rows, :] = val.astype(o.dtype)
            for o, val in zip(accs, partials):
                o[...] += val

    res = pl.pallas_call(
        body, name=name, grid=(M // bm,),
        in_specs=[pl.BlockSpec((bm, w_), lambda i: (i, 0)) for w_ in widths] + [pl.BlockSpec(b_shape, lambda i: (0, 0))]
        + [row] * n_t + [pl.BlockSpec((1, N), lambda i: (0, 0))] * n_v + [pl.BlockSpec(memory_space=pl.ANY)] * n_d,
        out_specs=[row] * n_o + [pl.BlockSpec((1, w_), lambda i: (0, 0)) for w_ in acc_widths],
        out_shape=[jax.ShapeDtypeStruct((M, N), dt) for dt in out_dtypes]
        + [jax.ShapeDtypeStruct((1, w_), F32) for w_ in acc_widths],
        compiler_params=_cparams(("arbitrary",) if n_a else ("parallel",)),
    )(*segs, b, *tiles, *vecs, *deps)
    return res


def _dw_in_t_half(h, pieces, q, half, *, S, name, into=None, deps=(), bk=1024):
    half_w, cols = D_IN // 2, D_MODEL // W_IN_PIECES
    lo, hi = half * half_w, (half + 1) * half_w
    use, c0 = [], 0
    for p in pieces:
        w_ = p.shape[1]
        a0, a1 = max(lo, c0), min(hi, c0 + w_)
        if a1 > a0:
            use.append((p, a0 - c0, a1 - a0))
        c0 += w_
    n_p, n_into, n_d, nk = len(use), 0 if into is None else 1, len(deps), S // bk
    rows = D_IN // N_DEV

    def body(*refs):
        h_ref, p_refs = refs[0], refs[1:1 + n_p]
        o_ref, acc_ref = refs[1 + n_p + n_into + n_d], refs[-1]
        k = pl.program_id(0)
        dp = jnp.concatenate([r[:, s0:s0 + w_] for r, (_, s0, w_) in zip(p_refs, use)], axis=1)
        part = _dot(h_ref[...], dp, "tn")

        @pl.when(k == 0)
        def _():
            acc_ref[...] = part

        @pl.when(k > 0)
        def _():
            acc_ref[...] += part

        @pl.when(k == nk - 1)
        def _():
            vt = acc_ref[...].T.astype(o_ref.dtype)
            for ch in range(half_w // rows):
                o_ref[ch] = vt[ch * rows:(ch + 1) * rows, :]

    return pl.pallas_call(
        body, name=name, grid=(nk,),
        in_specs=[pl.BlockSpec((bk, cols), lambda k: (k, q))]
        + [pl.BlockSpec((bk, p.shape[1]), lambda k: (k, 0)) for p, _, _ in use]
        + [pl.BlockSpec(memory_space=pl.ANY)] * (n_into + n_d),
        out_specs=pl.BlockSpec((half_w // rows, rows, cols), lambda k: (half, 0, 0)),
        out_shape=jax.ShapeDtypeStruct((N_DEV, rows, cols), MXU_DTYPE),
        input_output_aliases={1 + n_p: 0} if n_into else {},
        scratch_shapes=[pltpu.VMEM((cols, half_w), F32)],
        compiler_params=_cparams(("arbitrary",)),
    )(h, *[p for p, _, _ in use], *([into] if n_into else []), *deps)


def _rmsnorm_fwd(x, g, *, rows, name, bt=512):
    bt = min(bt, rows)

    def body(x_ref, g_ref, o_ref):
        xv = x_ref[...]
        r = lax.rsqrt(jnp.mean(xv * xv, axis=-1, keepdims=True) + EPS)
        o_ref[...] = (xv * r * g_ref[...]).astype(o_ref.dtype)

    return pl.pallas_call(
        body, name=name, grid=(rows // bt,),
        in_specs=[pl.BlockSpec((bt, D_MODEL), lambda i: (i, 0)), pl.BlockSpec((1, D_MODEL), lambda i: (0, 0))],
        out_specs=pl.BlockSpec((bt, D_MODEL), lambda i: (i, 0)),
        out_shape=jax.ShapeDtypeStruct((rows, D_MODEL), MXU_DTYPE),
        compiler_params=_cparams(("parallel",)),
    )(x, g.reshape(1, D_MODEL))


def _rms_bwd_tile(xv, gv, dyv):
    r = lax.rsqrt(jnp.mean(xv * xv, axis=-1, keepdims=True) + EPS)
    w = dyv * gv
    dx = r * w - xv * (r * r * r) * jnp.mean(w * xv, axis=-1, keepdims=True)
    dg = _colsum(dyv * xv * r)
    return dx, dg


def _residual_then_norm(acc, x_t, g):
    x1 = x_t + acc
    r = lax.rsqrt(jnp.mean(x1 * x1, axis=-1, keepdims=True) + EPS)
    return (x1, x1 * r * g), ()


def _residual_then_loss(acc, x_t, tgt_t, g):
    x2 = x_t + acc
    r = lax.rsqrt(jnp.mean(x2 * x2, axis=-1, keepdims=True) + EPS)
    diff = x2 * r * g - tgt_t
    part = jnp.sum(jnp.mean(diff * diff, axis=-1, keepdims=True), axis=0, keepdims=True) * 0.5
    dx, dg = _rms_bwd_tile(x2, g, diff * (1.0 / D_MODEL))
    return (dx, dx), (part, dg)


def _norm_bwd_then_residual(n_out):
    def fn(acc, x_t, res_t, g):
        dx, dg = _rms_bwd_tile(x_t, g, acc)
        return (dx + res_t,) * n_out, (dg,)

    return fn


def _rmsnorm_bwd(x, g, dy, res, *, rows, name, bt=512, dx_dtypes=(F32,)):
    bt = min(bt, rows)
    has_res = res is not None

    def body(*refs):
        x_ref, g_ref, dy_ref = refs[:3]
        res_ref = refs[3] if has_res else None
        outs = refs[3 + int(has_res):]
        dx, dg = _rms_bwd_tile(x_ref[...], g_ref[...], dy_ref[...])
        if has_res:
            dx = dx + res_ref[...]
        dg_ref = outs[-1]

        @pl.when(pl.program_id(0) == 0)
        def _():
            dg_ref[...] = jnp.zeros_like(dg_ref)

        dg_ref[...] += dg
        for o in outs[:-1]:
            o[...] = dx.astype(o.dtype)

    row_spec = pl.BlockSpec((bt, D_MODEL), lambda i: (i, 0))
    vec_spec = pl.BlockSpec((1, D_MODEL), lambda i: (0, 0))
    ins = [x, g.reshape(1, D_MODEL), dy] + ([res] if has_res else [])
    return pl.pallas_call(
        body, name=name, grid=(rows // bt,),
        in_specs=[row_spec, vec_spec, row_spec] + ([row_spec] if has_res else []),
        out_specs=[row_spec] * len(dx_dtypes) + [vec_spec],
        out_shape=[jax.ShapeDtypeStruct((rows, D_MODEL), dt) for dt in dx_dtypes] + [jax.ShapeDtypeStruct((1, D_MODEL), F32)],
        compiler_params=_cparams(("arbitrary",)),
    )(*ins)


LRU_T = 512
SCAN_GROUPS = 4


def _gelu(x):
    t = jnp.tanh(GELU_K * (x + 0.044715 * x * x * x))
    return 0.5 * x * (1.0 + t), t


def _gelu_grad(x, t):
    return 0.5 * (1.0 + t) + 0.5 * x * (1.0 - t * t) * GELU_K * (1.0 + 3.0 * 0.044715 * x * x)


def _softplus_neg(lam):
    z = -lam
    u = jnp.exp(-jnp.abs(z))
    w = 1.0 + u
    l1p = jnp.where(w == 1.0, u, jnp.log(w) * u / jnp.where(w == 1.0, 1.0, w - 1.0))
    return jnp.maximum(z, 0.0) + l1p


def _shift_down(cur, prev8, k, row8):
    y = pltpu.roll(cur, k, 0)
    head = jnp.where(row8 < k, pltpu.roll(prev8, k, 0), y[0:8])
    return jnp.concatenate([head, y[8:]], axis=0)


def _shift_up(cur, next8, k, row8):
    n = cur.shape[0]
    y = pltpu.roll(cur, n - k, 0)
    tail = jnp.where(row8 >= 8 - k, pltpu.roll(next8, 8 - k, 0), y[n - 8:n])
    return jnp.concatenate([y[0:n - 8], tail], axis=0)


def _lru_gates(xl, p8, cw, cb, wa, wx, ba, bx, lam, row8, a_mult=None):
    sh = [xl] + [_shift_down(xl, p8, k, row8) for k in (1, 2, 3)]
    xc = cb + cw[3:4] * sh[0] + cw[2:3] * sh[1] + cw[1:2] * sh[2] + cw[0:1] * sh[3]
    r = jax.nn.sigmoid(_dot(xc, wa) + ba)
    i = jax.nn.sigmoid(_dot(xc, wx) + bx)
    sp = _softplus_neg(lam)
    if a_mult is None:
        la = -LRU_C * r * sp
        a = jnp.exp(la)
        mult = jnp.sqrt(jnp.tanh(-la) * (a * a + 1.0))
    else:
        a, mult = a_mult
    return dict(sh=sh, xc=xc, r=r, i=i, sp=sp, a=a, mult=mult)


def _lru_specs(n_t, reverse):
    T = LRU_T
    tt = (lambda t: n_t - 1 - t) if reverse else (lambda t: t)
    blk = lambda col0: pl.BlockSpec((T, LRU_GROUP), lambda g, t: (tt(t), col0 + g))
    prev8 = lambda col0: pl.BlockSpec((8, LRU_GROUP), lambda g, t: (jnp.maximum(tt(t) * (T // 8) - 1, 0), col0 + g))
    vec = lambda rows: pl.BlockSpec((rows, LRU_GROUP), lambda g, t: (0, g))
    wbd = pl.BlockSpec((1, LRU_GROUP, LRU_GROUP), lambda g, t: (g, 0, 0))
    return blk, prev8, vec, wbd


def _lru_fwd(proj, conv_w, conv_b, wa_bd, wx_bd, b_a, b_x, lam, *, S):
    T = LRU_T
    n_t = S // T
    blk, _, vec, wbd = _lru_specs(n_t, False)

    def body(xl_ref, gate_ref, cw_ref, cb_ref, wa_ref, wx_ref, ba_ref, bx_ref, lam_ref,
             hl_ref, z_ref, a_s, m_ref, prev8, hcar, b_s):
        @pl.when(pl.program_id(1) == 0)
        def _():
            prev8[...] = jnp.zeros_like(prev8)
            hcar[...] = jnp.zeros_like(hcar)

        row8 = lax.broadcasted_iota(jnp.int32, (8, LRU_GROUP), 0)
        xl = xl_ref[...]
        q = _lru_gates(xl, prev8[...], cw_ref[...], cb_ref[...], wa_ref[0], wx_ref[0], ba_ref[...], bx_ref[...],
                       lam_ref[...], row8)
        prev8[...] = xl[T - 8:T]
        a_s[...] = q["a"]
        m_ref[...] = q["mult"]
        b_s[...] = q["mult"] * q["i"] * q["xc"]

        def step(c, carry):
            local = []
            for u in range(SCAN_GROUPS):
                off = pl.multiple_of((c * SCAN_GROUPS + u) * 8, 8)
                A = a_s[pl.ds(off, 8), :]
                B = b_s[pl.ds(off, 8), :]
                for k in (1, 2, 4):
                    a_sh = jnp.where(row8 >= k, pltpu.roll(A, k, 0), 1.0)
                    b_sh = jnp.where(row8 >= k, pltpu.roll(B, k, 0), 0.0)
                    B = A * b_sh + B
                    A = A * a_sh
                local.append((off, A, B))
            for off, A, B in local:
                h = A * carry + B
                hl_ref[pl.ds(off, 8), :] = h
                carry = h[7:8, :]
            return carry

        hcar[...] = lax.fori_loop(0, T // (8 * SCAN_GROUPS), step, hcar[...])
        ge, _ = _gelu(gate_ref[...])
        z_ref[...] = (ge * hl_ref[...]).astype(z_ref.dtype)

    return pl.pallas_call(
        body, name="lru_fwd", grid=(N_LRU_GROUPS, n_t),
        in_specs=[blk(C_XL // LRU_GROUP), blk(C_GATE // LRU_GROUP), vec(4), vec(1), wbd, wbd, vec(1), vec(1), vec(1)],
        out_specs=[blk(0)] * 4,
        out_shape=[jax.ShapeDtypeStruct((S, D_RNN), F32), jax.ShapeDtypeStruct((S, D_RNN), MXU_DTYPE),
                   jax.ShapeDtypeStruct((S, D_RNN), F32), jax.ShapeDtypeStruct((S, D_RNN), F32)],
        scratch_shapes=[pltpu.VMEM((8, LRU_GROUP), F32), pltpu.VMEM((1, LRU_GROUP), F32), pltpu.VMEM((T, LRU_GROUP), F32)],
        compiler_params=_cparams(("parallel", "arbitrary")),
    )(proj, proj, conv_w, conv_b, wa_bd, wx_bd, b_a, b_x, lam)


def _lru_bwd(proj, hl, a_fwd, mult_fwd, dy, w_out, conv_w, conv_b, wa_bd, wx_bd, b_a, b_x, lam, *, S):
    T = LRU_T
    n_t = S // T
    blk, prev8s, vec, wbd = _lru_specs(n_t, True)

    def body(xl_ref, xlp_ref, gate_ref, hl_ref, hlp_ref, a_ref, m_ref, dy_ref, wo_ref, cw_ref, cb_ref, wa_ref, wx_ref,
             ba_ref, bx_ref, lam_ref, dxl_ref, dgate_ref, dcw_ref, dcb_ref, dwa_ref, dwx_ref, dba_ref, dbx_ref, dlam_ref,
             next8, gcar, c_s, b_s, l_s):
        t = pl.program_id(1)
        first_chunk = t == n_t - 1

        @pl.when(t == 0)
        def _():
            next8[...] = jnp.zeros_like(next8)
            gcar[...] = jnp.zeros_like(gcar)
            for ref in (dcw_ref, dcb_ref, dwa_ref, dwx_ref, dba_ref, dbx_ref, dlam_ref):
                ref[...] = jnp.zeros_like(ref)

        row8 = lax.broadcasted_iota(jnp.int32, (8, LRU_GROUP), 0)
        rowT = lax.broadcasted_iota(jnp.int32, (T, LRU_GROUP), 0)
        keep = jnp.where(first_chunk, 0.0, 1.0)
        xl = xl_ref[...]
        wa, wx, lam_v = wa_ref[0], wx_ref[0], lam_ref[...]
        q = _lru_gates(xl, xlp_ref[...] * keep, cw_ref[...], cb_ref[...], wa, wx, ba_ref[...], bx_ref[...], lam_v, row8,
                       a_mult=(a_ref[...], m_ref[...]))
        a, mult, r, i, xc, sp = q["a"], q["mult"], q["r"], q["i"], q["xc"], q["sp"]
        hl_v = hl_ref[...]
        dz_v = _dot(dy_ref[...], wo_ref[...], "nt")
        gate = gate_ref[...]
        ge, th = _gelu(gate)
        dgate_ref[...] = (dz_v * hl_v * _gelu_grad(gate, th)).astype(dgate_ref.dtype)

        c_s[...] = jnp.where(rowT == T - 1, 0.0, pltpu.roll(a, T - 1, 0))
        b_s[...] = dz_v * ge + jnp.where(rowT == T - 1, gcar[...], 0.0)

        def step(n, carry):
            local = []
            for u in range(SCAN_GROUPS):
                off = pl.multiple_of((T // 8 - 1 - (n * SCAN_GROUPS + u)) * 8, 8)
                C = c_s[pl.ds(off, 8), :]
                B = b_s[pl.ds(off, 8), :]
                for k in (1, 2, 4):
                    c_sh = jnp.where(row8 < 8 - k, pltpu.roll(C, 8 - k, 0), 1.0)
                    b_sh = jnp.where(row8 < 8 - k, pltpu.roll(B, 8 - k, 0), 0.0)
                    B = B + C * b_sh
                    C = C * c_sh
                local.append((off, C, B))
            for off, C, B in local:
                lam_t = B + C * carry
                l_s[pl.ds(off, 8), :] = lam_t
                carry = lam_t[0:1, :]
            return carry

        lax.fori_loop(0, T // (8 * SCAN_GROUPS), step, jnp.zeros((1, LRU_GROUP), F32))
        lmb = l_s[...]
        gcar[...] = a[0:1, :] * lmb[0:1, :]

        h_prev = _shift_down(hl_v, hlp_ref[...] * keep, 1, row8)
        da = lmb * h_prev
        dmult = lmb * i * xc
        di = lmb * mult * xc
        dxc = lmb * mult * i
        dla = da * a - dmult * (a * a) / mult
        dr = dla * (-LRU_C * sp)
        dlam_ref[...] += _colsum(dla * (-LRU_C * r)) * (-jax.nn.sigmoid(-lam_v))
        dpa = dr * r * (1.0 - r)
        dpx = di * i * (1.0 - i)
        dxc = dxc + _dot(dpa, wa, "nt") + _dot(dpx, wx, "nt")
        dwa_ref[0] += _dot(xc, dpa, "tn")
        dwx_ref[0] += _dot(xc, dpx, "tn")
        dba_ref[...] += _colsum(dpa)
        dbx_ref[...] += _colsum(dpx)
        dcb_ref[...] += _colsum(dxc)
        cw = cw_ref[...]
        n8 = next8[...]
        dxl = cw[3:4] * dxc
        for k in (1, 2, 3):
            dxl = dxl + cw[3 - k:4 - k] * _shift_up(dxc, n8, k, row8)
        for k in range(4):
            dcw_ref[3 - k:4 - k, :] += _colsum(dxc * q["sh"][k])
        next8[...] = dxc[0:8]
        dxl_ref[...] = dxl.astype(dxl_ref.dtype)

    res = pl.pallas_call(
        body, name="lru_bwd", grid=(N_LRU_GROUPS, n_t),
        in_specs=[blk(C_XL // LRU_GROUP), prev8s(C_XL // LRU_GROUP), blk(C_GATE // LRU_GROUP), blk(0), prev8s(0), blk(0),
                  blk(0), pl.BlockSpec((T, D_MODEL), lambda g, t: (n_t - 1 - t, 0)),
                  pl.BlockSpec((LRU_GROUP, D_MODEL), lambda g, t: (g, 0)), vec(4), vec(1), wbd, wbd, vec(1), vec(1), vec(1)],
        out_specs=[blk(0), blk(0), vec(4), vec(1), wbd, wbd, vec(1), vec(1), vec(1)],
        out_shape=[jax.ShapeDtypeStruct((S, D_RNN), MXU_DTYPE), jax.ShapeDtypeStruct((S, D_RNN), MXU_DTYPE),
                   jax.ShapeDtypeStruct((4, D_RNN), F32), jax.ShapeDtypeStruct((1, D_RNN), F32),
                   jax.ShapeDtypeStruct((N_LRU_GROUPS, LRU_GROUP, LRU_GROUP), F32),
                   jax.ShapeDtypeStruct((N_LRU_GROUPS, LRU_GROUP, LRU_GROUP), F32),
                   jax.ShapeDtypeStruct((1, D_RNN), F32), jax.ShapeDtypeStruct((1, D_RNN), F32),
                   jax.ShapeDtypeStruct((1, D_RNN), F32)],
        scratch_shapes=[pltpu.VMEM((8, LRU_GROUP), F32), pltpu.VMEM((1, LRU_GROUP), F32),
                        pltpu.VMEM((T, LRU_GROUP), F32), pltpu.VMEM((T, LRU_GROUP), F32), pltpu.VMEM((T, LRU_GROUP), F32)],
        compiler_params=_cparams(("parallel", "arbitrary")),
    )(proj, proj, proj, hl, hl, a_fwd, mult_fwd, dy, w_out, conv_w, conv_b, wa_bd, wx_bd, b_a, b_x, lam)
    return res


def _block_diag(w):
    w4 = w.reshape(N_LRU_GROUPS, 4, LRU_BLOCK, 1, LRU_BLOCK)
    eye = jnp.eye(4, dtype=w.dtype).reshape(1, 4, 1, 4, 1)
    return (w4 * eye).reshape(N_LRU_GROUPS, LRU_GROUP, LRU_GROUP)


def _block_diag_extract(wbd):
    w5 = wbd.reshape(N_LRU_GROUPS, 4, LRU_BLOCK, 4, LRU_BLOCK)
    return jnp.stack([w5[:, a, :, a, :] for a in range(4)], axis=1).reshape(N_LRU_BLOCKS, LRU_BLOCK, LRU_BLOCK)


def _t5_bucket(dist):
    max_exact = NUM_BUCKETS // 2
    df = jnp.maximum(dist, 1).astype(jnp.float32)
    large = max_exact + (jnp.log(df / max_exact) / math.log(MAX_DISTANCE / max_exact)
                         * (NUM_BUCKETS - max_exact)).astype(jnp.int32)
    large = jnp.minimum(large, NUM_BUCKETS - 1)
    return jnp.where(dist < max_exact, dist, large)


def _band_offsets():
    qi = jnp.arange(SPAN)[:, None]
    kj = jnp.arange(2 * SPAN)[None, :]
    return qi + SPAN - kj


def _dil_buckets():
    off = _band_offsets()
    return jnp.stack([_t5_bucket(jnp.maximum(off, 0) * dil) for _, dil in DIL_GROUPS]).astype(jnp.int32)


def _dil_bias(rel_bias, buckets):
    def body(tbl_ref, bk_ref, o_ref):
        g = pl.program_id(0)
        qi = lax.broadcasted_iota(jnp.int32, (SPAN, 2 * SPAN), 0)
        kj = lax.broadcasted_iota(jnp.int32, (SPAN, 2 * SPAN), 1)
        off = qi + SPAN - kj
        valid = (off >= 0) & (off <= SPAN)
        bk = bk_ref[0]
        for h in range(DIL_HEADS):
            acc = jnp.zeros((SPAN, 2 * SPAN), F32)
            for b in range(NUM_BUCKETS):
                acc = jnp.where(bk == b, tbl_ref[b, g * DIL_HEADS + h], acc)
            o_ref[0, h] = jnp.where(valid, acc, NEG)

    return pl.pallas_call(
        body, name="dil_bias", grid=(3,),
        in_specs=[pl.BlockSpec(memory_space=pltpu.SMEM), pl.BlockSpec((1, SPAN, 2 * SPAN), lambda g: (g, 0, 0))],
        out_specs=pl.BlockSpec((1, DIL_HEADS, SPAN, 2 * SPAN), lambda g: (g, 0, 0, 0)),
        out_shape=jax.ShapeDtypeStruct((3, DIL_HEADS, SPAN, 2 * SPAN), F32),
        compiler_params=_cparams(("parallel",)),
    )(rel_bias, buckets)


def _dil_bias_bwd(dbias, buckets):
    def body(db_ref, bk_ref, o_ref):
        lane = lax.broadcasted_iota(jnp.int32, (1, 128), 1)
        rows = [jnp.zeros((1, 128), F32) for _ in range(NUM_BUCKETS)]
        for g in range(3):
            bk = bk_ref[g]
            for h in range(DIL_HEADS):
                d = db_ref[g, h]
                for b in range(NUM_BUCKETS):
                    tot = jnp.sum(_colsum(jnp.where(bk == b, d, 0.0)), axis=1, keepdims=True)
                    rows[b] = jnp.where(lane == g * DIL_HEADS + h, tot, rows[b])
        for b in range(NUM_BUCKETS):
            o_ref[b:b + 1, :] = rows[b]

    return pl.pallas_call(
        body, name="dil_bias_bwd",
        out_shape=jax.ShapeDtypeStruct((NUM_BUCKETS, 128), F32),
        compiler_params=_cparams(),
    )(dbias, buckets)


DIL_SUBBLOCKS = (8, 4, 1)


def _dil_layout(g, S):
    dil, m = DIL_GROUPS[g][1], DIL_SUBBLOCKS[g]
    sub = SPAN * dil
    col = [(C_QKV + t * 768 + g * 256) // 128 for t in range(3)]
    return dil, m, sub, S // (sub * m), col


def _residue_rows(b, r, dil):
    return pl.ds(b * SPAN * dil + r, SPAN, stride=dil) if dil > 1 else pl.ds(b * SPAN, SPAN)


def _for_residues(dil, fn):
    if dil <= 4:
        for r in range(dil):
            fn(r)
    else:
        lax.fori_loop(0, dil, lambda r, c: (fn(r), c)[1], 0, unroll=4)


def _pair_scores(qm, k2, bias, first_cols):
    s = _dot(qm, k2, "nt") * (DIL_HEAD_DIM ** -0.5) + bias
    kj = lax.broadcasted_iota(jnp.int32, s.shape, 1)
    return jnp.where(kj < first_cols, NEG, s)


def _dilated_fwd(proj, bias, g, *, S):
    dil, m, sub, nc, (qc, kc, vc) = _dil_layout(g, S)
    R = sub * m
    cur = lambda cb: pl.BlockSpec((R, 128), lambda p, i: (i, cb + p))
    prv = lambda cb: pl.BlockSpec((sub, 128), lambda p, i: (jnp.maximum(i * m - 1, 0), cb + p))
    out = pl.BlockSpec((R, 128), lambda p, i: (i, p))

    def body(q_ref, kp_ref, kc_ref, vp_ref, vc_ref, b_ref, o_ref, lse_ref):
        lane = lax.broadcasted_iota(jnp.int32, (SPAN, 128), 1)
        sels = (lane < DIL_HEAD_DIM, lane >= DIL_HEAD_DIM)
        for b in range(m):
            first_cols = jnp.where(pl.program_id(1) == 0, SPAN, 0) if b == 0 else 0

            def one(r, b=b, first_cols=first_cols):
                rows = _residue_rows(b, r, dil)
                before = (kc_ref, vc_ref, _residue_rows(b - 1, r, dil)) if b else (kp_ref, vp_ref, _residue_rows(0, r, dil))
                q2 = q_ref[rows, :]
                k2 = _mx(jnp.concatenate([before[0][before[2], :], kc_ref[rows, :]], axis=0))
                v2 = _mx(jnp.concatenate([before[1][before[2], :], vc_ref[rows, :]], axis=0))
                qq = jnp.concatenate([jnp.where(sels[0], q2, 0.0), jnp.where(sels[1], q2, 0.0)], axis=0)
                s = _pair_scores(qq, k2, b_ref[0, 0], first_cols)
                mx = jnp.max(s, axis=-1, keepdims=True)
                p = jnp.exp(s - mx)
                den = jnp.sum(p, axis=-1, keepdims=True)
                o = _dot(p, v2) / den
                st = mx + jnp.log(den)
                o_ref[rows, :] = jnp.where(sels[0], o[0:SPAN], o[SPAN:2 * SPAN])
                lse_ref[rows, :] = jnp.where(sels[0], st[0:SPAN], st[SPAN:2 * SPAN])

            _for_residues(dil, one)

    return pl.pallas_call(
        body, name=f"dil_fwd{g}", grid=(2, nc),
        in_specs=[cur(qc), prv(kc), cur(kc), prv(vc), cur(vc),
                  pl.BlockSpec((1, 1, 2 * SPAN, 2 * SPAN), lambda p, i: (g, p, 0, 0))],
        out_specs=[out, out],
        out_shape=[jax.ShapeDtypeStruct((S, 256), F32), jax.ShapeDtypeStruct((S, 256), F32)],
        compiler_params=_cparams(("parallel", "parallel")),
    )(proj, proj, proj, proj, proj, bias.reshape(3, 2, 2 * SPAN, 2 * SPAN))


def _dilated_bwd(proj, do, lse, delta, bias, g, *, S, into=None):
    dil, m, sub, nc, (qc, kc, vc) = _dil_layout(g, S)
    R = sub * m
    cl = lambda i: jnp.minimum(i, nc - 1)
    cur = lambda cb: pl.BlockSpec((R, 128), lambda p, i: (cl(i), cb + p))
    prv = lambda cb: pl.BlockSpec((sub, 128), lambda p, i: (jnp.maximum(cl(i) * m - 1, 0), cb + p))
    q_out = pl.BlockSpec((R, 128), lambda p, i: (cl(i), 2 * g + p))
    kv_out = pl.BlockSpec((R, 128), lambda p, i: (jnp.maximum(i - 1, 0), 2 * g + p))
    scale = DIL_HEAD_DIM ** -0.5
    n_into = 0 if into is None else 3

    def body(q_ref, kp_ref, kc_ref, vp_ref, vc_ref, do_ref, lse_ref, dl_ref, b_ref, *rest):
        dq_ref, dk_ref, dv_ref, db_ref, dq_s, kc_s, vc_s, kp_s, vp_s, kcar, vcar = rest[n_into:]
        i = pl.program_id(1)

        @pl.when(i == 0)
        def _():
            kcar[...] = jnp.zeros_like(kcar)
            vcar[...] = jnp.zeros_like(vcar)
            db_ref[...] = jnp.zeros_like(db_ref)

        @pl.when(i < nc)
        def _():
            lane = lax.broadcasted_iota(jnp.int32, (SPAN, 128), 1)
            sels = (lane < DIL_HEAD_DIM, lane >= DIL_HEAD_DIM)
            for b in range(m):
                first_cols = jnp.where(i == 0, SPAN, 0) if b == 0 else 0

                def one(r, b=b, first_cols=first_cols):
                    rows = _residue_rows(b, r, dil)
                    rows_before = _residue_rows(b - 1 if b else 0, r, dil)
                    k_before, v_before = (kc_ref, vc_ref) if b else (kp_ref, vp_ref)
                    q2, do2 = q_ref[rows, :], do_ref[rows, :]
                    k2 = _mx(jnp.concatenate([k_before[rows_before, :], kc_ref[rows, :]], axis=0))
                    v2 = _mx(jnp.concatenate([v_before[rows_before, :], vc_ref[rows, :]], axis=0))
                    lse_t, dl_t = lse_ref[rows, :], dl_ref[rows, :]
                    qq = _mx(jnp.concatenate([jnp.where(sels[0], q2, 0.0), jnp.where(sels[1], q2, 0.0)], axis=0))
                    dd = _mx(jnp.concatenate([jnp.where(sels[0], do2, 0.0), jnp.where(sels[1], do2, 0.0)], axis=0))
                    lse2 = jnp.concatenate([lse_t[:, 0:1], lse_t[:, DIL_HEAD_DIM:DIL_HEAD_DIM + 1]], axis=0)
                    dl2 = jnp.concatenate([dl_t[:, 0:1], dl_t[:, 1:2]], axis=0)
                    p = jnp.exp(_pair_scores(qq, k2, b_ref[0, 0], first_cols) - lse2)
                    ds = p * (_dot(dd, v2, "nt") - dl2)
                    db_ref[0] += ds
                    dqq = _dot(ds, k2) * scale
                    dq2 = jnp.where(sels[0], dqq[0:SPAN], dqq[SPAN:2 * SPAN])
                    dk2 = _dot(ds, qq, "tn") * scale
                    dv2 = _dot(p, dd, "tn")
                    dq_s[rows, :] = dq2
                    kc_s[rows, :] = dk2[SPAN:2 * SPAN]
                    vc_s[rows, :] = dv2[SPAN:2 * SPAN]
                    if b:
                        kc_s[rows_before, :] += dk2[0:SPAN]
                        vc_s[rows_before, :] += dv2[0:SPAN]
                    else:
                        kp_s[rows_before, :] = dk2[0:SPAN]
                        vp_s[rows_before, :] = dv2[0:SPAN]

                _for_residues(dil, one)
            dq_ref[...] = dq_s[...].astype(dq_ref.dtype)
            last = pl.ds((m - 1) * sub, sub)
            kcar[last, :] += kp_s[...]
            vcar[last, :] += vp_s[...]
            dk_ref[...] = kcar[...].astype(dk_ref.dtype)
            dv_ref[...] = vcar[...].astype(dv_ref.dtype)
            kcar[...] = kc_s[...]
            vcar[...] = vc_s[...]

        @pl.when(i == nc)
        def _():
            dk_ref[...] = kcar[...].astype(dk_ref.dtype)
            dv_ref[...] = vcar[...].astype(dv_ref.dtype)

    stat = pl.BlockSpec((R, 128), lambda p, i: (cl(i), p))
    big = jax.ShapeDtypeStruct((S, len(DIL_GROUPS) * 256), MXU_DTYPE)
    return pl.pallas_call(
        body, name=f"dil_bwd{g}", grid=(2, nc + 1),
        in_specs=[cur(qc), prv(kc), cur(kc), prv(vc), cur(vc), stat, stat, stat,
                  pl.BlockSpec((1, 1, 2 * SPAN, 2 * SPAN), lambda p, i: (g, p, 0, 0))]
        + [pl.BlockSpec(memory_space=pl.ANY)] * n_into,
        out_specs=[q_out, kv_out, kv_out, pl.BlockSpec((1, 2 * SPAN, 2 * SPAN), lambda p, i: (p, 0, 0))],
        out_shape=[big, big, big, jax.ShapeDtypeStruct((2, 2 * SPAN, 2 * SPAN), F32)],
        input_output_aliases={9 + j: j for j in range(n_into)},
        scratch_shapes=[pltpu.VMEM((R, 128), F32)] * 3 + [pltpu.VMEM((sub, 128), F32)] * 2 + [pltpu.VMEM((R, 128), F32)] * 2,
        compiler_params=_cparams(("parallel", "arbitrary")),
    )(proj, proj, proj, proj, proj, do, lse, delta, bias.reshape(3, 2, 2 * SPAN, 2 * SPAN), *(into or ()))


def _dilated_merge(os_, lses, *, S, bt=1024):
    tile = pl.BlockSpec((bt, 128), lambda i, p: (i, p))

    def body(o0, o1, o2, l0, l1, l2, o_ref, om_ref, lse_ref):
        a = [l0[...], l1[...], l2[...]]
        m = jnp.maximum(jnp.maximum(a[0], a[1]), a[2])
        ex = [jnp.exp(v - m) for v in a]
        tot = ex[0] + ex[1] + ex[2]
        acc = (ex[0] * o0[...] + ex[1] * o1[...] + ex[2] * o2[...]) / tot
        o_ref[...] = acc
        om_ref[...] = _mx(acc)
        lse_ref[...] = m + jnp.log(tot)

    return pl.pallas_call(
        body, name="dil_merge", grid=(S // bt, 2),
        in_specs=[tile] * 6, out_specs=[tile, tile, tile],
        out_shape=[jax.ShapeDtypeStruct((S, 256), F32), jax.ShapeDtypeStruct((S, 256), MXU_DTYPE),
                   jax.ShapeDtypeStruct((S, 256), F32)],
        compiler_params=_cparams(("parallel", "parallel")),
    )(*os_, *lses)


def _with_delta(do, o):
    lane = lax.broadcasted_iota(jnp.int32, (do.shape[0], 128), 1)
    stats = []
    for p in range(2):
        prod = do[:, 128 * p:128 * (p + 1)] * o[:, 128 * p:128 * (p + 1)]
        d0 = jnp.sum(jnp.where(lane < DIL_HEAD_DIM, prod, 0.0), axis=-1, keepdims=True)
        d1 = jnp.sum(jnp.where(lane >= DIL_HEAD_DIM, prod, 0.0), axis=-1, keepdims=True)
        stats.append(jnp.where(lane == 0, d0, jnp.where(lane == 1, d1, 0.0)))
    return do, jnp.concatenate(stats, axis=1)


MEM_T = 2048
QM_BLK = C_QM // MEM_HEAD_DIM


def _mem_attn_fwd(proj, kv, *, S):
    scale = MEM_HEAD_DIM ** -0.5

    def body(q_ref, k_ref, v_ref, o_ref, om_ref, lse_ref):
        s = _dot(q_ref[...], k_ref[...], "nt") * scale
        m = jnp.max(s, axis=-1, keepdims=True)
        p = jnp.exp(s - m)
        den = jnp.sum(p, axis=-1, keepdims=True)
        o = _dot(p, v_ref[...]) / den
        o_ref[...] = o
        om_ref[...] = _mx(o)
        lse_ref[0] = m + jnp.log(den)

    return pl.pallas_call(
        body, name="mem_attn_fwd", grid=(S // MEM_T, MEM_HEADS),
        in_specs=[pl.BlockSpec((MEM_T, MEM_HEAD_DIM), lambda i, h: (i, QM_BLK + h)),
                  pl.BlockSpec((N_MEM, MEM_HEAD_DIM), lambda i, h: (0, h)),
                  pl.BlockSpec((N_MEM, MEM_HEAD_DIM), lambda i, h: (0, MEM_HEADS + h))],
        out_specs=[pl.BlockSpec((MEM_T, MEM_HEAD_DIM), lambda i, h: (i, h)),
                   pl.BlockSpec((MEM_T, MEM_HEAD_DIM), lambda i, h: (i, h)),
                   pl.BlockSpec((1, MEM_T, 1), lambda i, h: (h, i, 0))],
        out_shape=[jax.ShapeDtypeStruct((S, MEM_WIDTH), F32), jax.ShapeDtypeStruct((S, MEM_WIDTH), MXU_DTYPE),
                   jax.ShapeDtypeStruct((MEM_HEADS, S, 1), F32)],
        compiler_params=_cparams(("parallel", "parallel")),
    )(proj, kv, kv)


def _mem_attn_bwd(proj, kv, om, lse, dy, w_out, *, S):
    scale = MEM_HEAD_DIM ** -0.5

    def body(q_ref, k_ref, v_ref, o_ref, lse_ref, dy_ref, wo_ref, dq_ref, dk_ref, dv_ref):
        @pl.when(pl.program_id(1) == 0)
        def _():
            dk_ref[...] = jnp.zeros_like(dk_ref)
            dv_ref[...] = jnp.zeros_like(dv_ref)

        qv, kv_, vv, dov = q_ref[...], k_ref[...], v_ref[...], _dot(dy_ref[...], wo_ref[...], "nt")
        p = jnp.exp(_dot(qv, kv_, "nt") * scale - lse_ref[0])
        delta = jnp.sum(dov * o_ref[...], axis=-1, keepdims=True)
        ds = p * (_dot(dov, vv, "nt") - delta)
        dq_ref[...] = (_dot(ds, kv_) * scale).astype(dq_ref.dtype)
        dk_ref[...] += _dot(ds, qv, "tn") * scale
        dv_ref[...] += _dot(p, dov, "tn")

    tile = pl.BlockSpec((MEM_T, MEM_HEAD_DIM), lambda h, i: (i, h))
    kvo = pl.BlockSpec((N_MEM, MEM_HEAD_DIM), lambda h, i: (0, h))
    return pl.pallas_call(
        body, name="mem_attn_bwd", grid=(MEM_HEADS, S // MEM_T),
        in_specs=[pl.BlockSpec((MEM_T, MEM_HEAD_DIM), lambda h, i: (i, QM_BLK + h)),
                  pl.BlockSpec((N_MEM, MEM_HEAD_DIM), lambda h, i: (0, h)),
                  pl.BlockSpec((N_MEM, MEM_HEAD_DIM), lambda h, i: (0, MEM_HEADS + h)),
                  tile, pl.BlockSpec((1, MEM_T, 1), lambda h, i: (h, i, 0)),
                  pl.BlockSpec((MEM_T, D_MODEL), lambda h, i: (i, 0)),
                  pl.BlockSpec((MEM_HEAD_DIM, D_MODEL), lambda h, i: (h, 0))],
        out_specs=[tile, kvo, kvo],
        out_shape=[jax.ShapeDtypeStruct((S, MEM_WIDTH), MXU_DTYPE), jax.ShapeDtypeStruct((N_MEM, MEM_WIDTH), F32),
                   jax.ShapeDtypeStruct((N_MEM, MEM_WIDTH), F32)],
        compiler_params=_cparams(("parallel", "arbitrary")),
    )(proj, kv, kv, om, lse, dy, w_out)


MIX_BM = 1024
MIX_BN = 256
GATES_BLK = C_GATES // MIX_BN


def _mix_specs(j_outer):
    ix = (lambda f: (lambda j, i: f(i, j))) if j_outer else (lambda f: f)
    act = lambda width: pl.BlockSpec((MIX_BM, width), ix(lambda i, j: (i, 0)))
    wgt = lambda width: pl.BlockSpec((width, MIX_BN), ix(lambda i, j: (0, j)))
    gate = lambda b: pl.BlockSpec((MIX_BM, MIX_BN), ix(lambda i, j: (i, GATES_BLK + 4 * b + j)))
    bias = lambda b: pl.BlockSpec((1, MIX_BN), ix(lambda i, j: (0, 4 * b + j)))
    tile = pl.BlockSpec((MIX_BM, MIX_BN), ix(lambda i, j: (i, j)))
    return act, wgt, gate, bias, tile


def _mix_fwd(z_lru, o_dil, om, w_lru, w_dil, w_mem, proj, b_gate, *, S):
    act, wgt, gate, bias, tile = _mix_specs(False)

    def body(zl, od, mo, wl, wd, wm, g0, g1, g2, b0, b1, b2, o_ref, t0, t1, t2):
        acc = None
        for a_ref, w_ref, g_ref, b_ref, t_ref in ((zl, wl, g0, b0, t0), (od, wd, g1, b1, t1), (mo, wm, g2, b2, t2)):
            gt = jax.nn.sigmoid(g_ref[...] + b_ref[...])
            t_ref[...] = gt.astype(t_ref.dtype)
            term = gt * _dot(a_ref[...], w_ref[...])
            acc = term if acc is None else acc + term
        o_ref[...] = acc.astype(o_ref.dtype)

    return pl.pallas_call(
        body, name="mix_fwd", grid=(S // MIX_BM, D_MODEL // MIX_BN),
        in_specs=[act(D_RNN), act(256), act(MEM_WIDTH), wgt(D_RNN), wgt(256), wgt(MEM_WIDTH),
                  gate(0), gate(1), gate(2), bias(0), bias(1), bias(2)],
        out_specs=[tile] * 4, out_shape=[jax.ShapeDtypeStruct((S, D_MODEL), MXU_DTYPE)] * 4,
        compiler_params=_cparams(("parallel", "parallel")),
    )(z_lru, o_dil, om, w_lru, w_dil, w_mem, proj, proj, proj, b_gate, b_gate, b_gate)


def _mix_bwd(dx1, w_out, z_lru, o_dil, om, w_lru, w_dil, w_mem, gates, *, S):
    act, wgt, _, _, tile = _mix_specs(False)
    n_j = D_MODEL // MIX_BN

    def body(dx, wo, zl, od, mo, wl, wd, wm, t0, t1, t2,
             dg0, dg1, dg2, dy0, dy1, dy2, db0, db1, db2):
        j = pl.program_id(1)

        @pl.when((pl.program_id(0) == 0) & (j == 0))
        def _():
            for r in (db0, db1, db2):
                r[...] = jnp.zeros_like(r)

        dmv = _dot(dx[...], wo[...], "nt")
        for act_ref, w_ref, t_ref, dg_ref, dy_ref, db_ref in (
                (zl, wl, t0, dg0, dy0, db0), (od, wd, t1, dg1, dy1, db1), (mo, wm, t2, dg2, dy2, db2)):
            y = _dot(act_ref[...], w_ref[...])
            gt = t_ref[...].astype(F32)
            dgate = dmv * y * gt * (1.0 - gt)
            dg_ref[...] = dgate.astype(dg_ref.dtype)
            dy_ref[...] = (dmv * gt).astype(dy_ref.dtype)
            db_ref[j] += _colsum(dgate)

    big = jax.ShapeDtypeStruct((S, D_MODEL), MXU_DTYPE)
    vec = jax.ShapeDtypeStruct((n_j, 1, MIX_BN), F32)
    vspec = pl.BlockSpec((n_j, 1, MIX_BN), lambda i, j: (0, 0, 0))
    res = pl.pallas_call(
        body, name="mix_bwd", grid=(S // MIX_BM, n_j),
        in_specs=[pl.BlockSpec((MIX_BM, D_MODEL), lambda i, j: (i, 0)), pl.BlockSpec((MIX_BN, D_MODEL), lambda i, j: (j, 0)),
                  act(D_RNN), act(256), act(MEM_WIDTH), wgt(D_RNN), wgt(256), wgt(MEM_WIDTH), tile, tile, tile],
        out_specs=[tile] * 6 + [vspec] * 3, out_shape=[big] * 6 + [vec] * 3,
        compiler_params=_cparams(("arbitrary", "arbitrary")),
    )(dx1, w_out, z_lru, o_dil, om, w_lru, w_dil, w_mem, *gates)
    return list(res[:6]) + [r.reshape(1, D_MODEL) for r in res[6:]]


def _adamw_math(w, g, m, v):
    m = ADAM_B1 * m + (1.0 - ADAM_B1) * g
    v = ADAM_B2 * v + (1.0 - ADAM_B2) * (g * g)
    m_hat = m / (1.0 - ADAM_B1 ** ADAM_STEP)
    v_hat = v / (1.0 - ADAM_B2 ** ADAM_STEP)
    delta = -ADAM_LR * (m_hat / (jnp.sqrt(v_hat) + ADAM_EPS) + ADAM_WD * w)
    return delta, m, v


def _adamw_landed(w, parts, slot, land, m, v, *, name, col_blk=0, prev=None):
    R = w.shape[0]
    n_parts, C = land.shape[0], land.shape[2]
    br = next(d for d in (256, 464, 128) if R % d == 0)
    tile = pl.BlockSpec((br, C), lambda i, s: (i, col_blk))
    part = pl.BlockSpec((None, br, C), lambda i, s: (s[0], i, 0))
    n_prev = 0 if prev is None else 4

    def body(s_ref, w_ref, o_ref, l_ref, m_ref, v_ref, *rest):
        g_ref, d_ref, nm_ref, nv_ref = rest[n_prev:]
        g = o_ref[...].astype(F32)
        for p in range(n_parts):
            g = g + l_ref[p].astype(F32)
        d, nm, nv = _adamw_math(w_ref[...], g, m_ref[...], v_ref[...])
        g_ref[...] = g
        d_ref[...] = d
        nm_ref[...] = nm
        nv_ref[...] = nv

    return pl.pallas_call(
        body, name=name,
        grid_spec=pltpu.PrefetchScalarGridSpec(
            num_scalar_prefetch=1, grid=(R // br,),
            in_specs=[tile, part, pl.BlockSpec((n_parts, br, C), lambda i, s: (0, i, 0)), tile, tile]
            + [pl.BlockSpec(memory_space=pl.ANY)] * n_prev,
            out_specs=[tile] * 4),
        out_shape=[jax.ShapeDtypeStruct(w.shape, F32)] * 4,
        input_output_aliases={6 + j: j for j in range(n_prev)},
        compiler_params=_cparams(("parallel",)),
    )(slot, w, parts, land, m, v, *(prev or ()))


def _adamw_plain(w, g, m, v, *, name):
    def body(w_ref, g_ref, m_ref, v_ref, d_ref, nm_ref, nv_ref):
        d, nm, nv = _adamw_math(w_ref[...], g_ref[...], m_ref[...], v_ref[...])
        d_ref[...] = d
        nm_ref[...] = nm
        nv_ref[...] = nv

    return pl.pallas_call(
        body, name=name, out_shape=[jax.ShapeDtypeStruct(w.shape, F32)] * 3, compiler_params=_cparams(),
    )(w, g, m, v)


def _my_pos():
    return lax.axis_index("x"), lax.axis_index("y"), lax.axis_index("c")


def _dev_index(p):
    return 4 * p[0] + 2 * p[1] + p[2]


def _peers(me):
    x, y, c = me
    out = []
    for k in range(1, 8):
        fx, fy, fc = (k >> 2) & 1, (k >> 1) & 1, k & 1
        out.append((k - 1, (1 - x if fx else x, 1 - y if fy else y, 1 - c if fc else c)))
    return out


HBM_SPEC = pl.BlockSpec(memory_space=pltpu.HBM)
SEM_SPEC = pl.BlockSpec(memory_space=pltpu.SEMAPHORE)
DATAFLOW_EFFECT = pltpu.SideEffectType.DATAFLOW_SIDE_EFFECTING


def _gather_refs(src, land, me, peer, k):
    return src, land.at[_dev_index(me)]


def _scatter_refs(src, land, me, peer, k):
    return src.at[_dev_index(peer)], land.at[k]


ALL_RELATIONS = tuple(range(7))
ONE_PER_CHIP = (0, 1, 3, 5)


def _push_start(srcs, land_shapes, refs_of, name, after=(), relations=ALL_RELATIONS):
    n, n_after = len(srcs), len(after)

    def body(*refs):
        ins, lands = refs[:n], refs[n:2 * n]
        send_sems, recv_sems, token = refs[2 * n + n_after], refs[2 * n + n_after + 1], refs[-1]
        me = _my_pos()
        for k, peer in _peers(me):
            if k not in relations:
                continue
            for a in range(n):
                src, dst = refs_of(ins[a], lands[a], me, peer, k)
                pltpu.make_async_remote_copy(src_ref=src, dst_ref=dst, send_sem=send_sems.at[7 * a + k],
                                             recv_sem=recv_sems.at[7 * a + k], device_id=peer, device_id_type=MESH).start()
        token[...] = jnp.zeros_like(token)

    lands = [lax.empty(shp, s.dtype) for shp, s in zip(land_shapes, srcs)]
    hbm = lambda a: pltpu.with_memory_space_constraint(a, pltpu.HBM)
    res = pl.pallas_call(
        body, name=name,
        out_shape=(pltpu.SemaphoreType.DMA((7 * n,)), pltpu.SemaphoreType.DMA((7 * n,)),
                   *[pltpu.HBM(s.shape, s.dtype) for s in srcs], *[pltpu.HBM(l.shape, l.dtype) for l in lands],
                   jax.ShapeDtypeStruct((8, 128), F32)),
        in_specs=[HBM_SPEC] * (2 * n) + [pl.BlockSpec(memory_space=pl.ANY)] * n_after,
        out_specs=(SEM_SPEC, SEM_SPEC, *[HBM_SPEC] * (2 * n), pl.BlockSpec(memory_space=pltpu.VMEM)),
        input_output_aliases={i: 2 + i for i in range(2 * n)},
        compiler_params=pltpu.CompilerParams(has_side_effects=DATAFLOW_EFFECT),
    )(*[hbm(s) for s in srcs], *[hbm(l) for l in lands], *after)
    return dict(sems=(res[0], res[1]), srcs=list(res[2:2 + n]), lands=list(res[2 + n:2 + 2 * n]), token=res[-1], n=n,
                refs_of=refs_of, name=name, relations=relations)


def _push_wait(started, after, with_sources=False):
    n, refs_of, relations = started["n"], started["refs_of"], started["relations"]
    after = list(after) if isinstance(after, (list, tuple)) else [after]

    def body(*refs):
        ins, lands = refs[:n], refs[n:2 * n]
        send_sems, recv_sems = refs[2 * n], refs[2 * n + 1]
        me = _my_pos()
        for k, peer in _peers(me):
            if k not in relations:
                continue
            for a in range(n):
                src, dst = refs_of(ins[a], lands[a], me, peer, k)
                cp = pltpu.make_async_remote_copy(src_ref=src, dst_ref=dst, send_sem=send_sems.at[7 * a + k],
                                                  recv_sem=recv_sems.at[7 * a + k], device_id=peer, device_id_type=MESH)
                cp.wait_send()
                cp.wait_recv()

    arrs = started["srcs"] + started["lands"]
    res = pl.pallas_call(
        body, name=started["name"].replace("start", "wait"),
        out_shape=tuple(pltpu.HBM(a.shape, a.dtype) for a in arrs),
        in_specs=[HBM_SPEC] * (2 * n) + [SEM_SPEC, SEM_SPEC] + [pl.BlockSpec(memory_space=pl.ANY)] * len(after),
        out_specs=tuple([HBM_SPEC] * (2 * n)),
        input_output_aliases={i: i for i in range(2 * n)},
        compiler_params=pltpu.CompilerParams(has_side_effects=DATAFLOW_EFFECT),
    )(*arrs, *started["sems"], *after)
    return (list(res[:n]), list(res[n:2 * n])) if with_sources else list(res[n:2 * n])


def _other_chips(x, y):
    return ((1 - x, y), (x, 1 - y), (1 - x, 1 - y))


def _forward_start(land, name, after=()):
    n_after = len(after)

    def body(*refs):
        land_ref, send_sems, recv_sems, token = refs[0], refs[1 + n_after], refs[2 + n_after], refs[-1]
        x, y, c = _my_pos()
        for j, (cx, cy) in enumerate(_other_chips(x, y)):
            blk = land_ref.at[_dev_index((cx, cy, c))]
            pltpu.make_async_remote_copy(src_ref=blk, dst_ref=blk, send_sem=send_sems.at[j], recv_sem=recv_sems.at[j],
                                         device_id=(x, y, 1 - c), device_id_type=MESH).start()
        token[...] = jnp.zeros_like(token)

    res = pl.pallas_call(
        body, name=name,
        out_shape=(pltpu.SemaphoreType.DMA((3,)), pltpu.SemaphoreType.DMA((3,)), pltpu.HBM(land.shape, land.dtype),
                   jax.ShapeDtypeStruct((8, 128), F32)),
        in_specs=[HBM_SPEC] + [pl.BlockSpec(memory_space=pl.ANY)] * n_after,
        out_specs=(SEM_SPEC, SEM_SPEC, HBM_SPEC, pl.BlockSpec(memory_space=pltpu.VMEM)),
        input_output_aliases={0: 2},
        compiler_params=pltpu.CompilerParams(has_side_effects=DATAFLOW_EFFECT),
    )(pltpu.with_memory_space_constraint(land, pltpu.HBM), *after)
    return dict(sems=(res[0], res[1]), land=res[2], token=res[3], name=name)


def _forward_wait(started, after):
    after = list(after) if isinstance(after, (list, tuple)) else [after]

    def body(land_ref, send_sems, recv_sems, *rest):
        x, y, c = _my_pos()
        for j, (cx, cy) in enumerate(_other_chips(x, y)):
            cp = pltpu.make_async_remote_copy(
                src_ref=land_ref.at[_dev_index((cx, cy, c))], dst_ref=land_ref.at[_dev_index((cx, cy, 1 - c))],
                send_sem=send_sems.at[j], recv_sem=recv_sems.at[j], device_id=(x, y, 1 - c), device_id_type=MESH)
            cp.wait_send()
            cp.wait_recv()

    land = started["land"]
    return pl.pallas_call(
        body, name=started["name"].replace("start", "wait"), out_shape=pltpu.HBM(land.shape, land.dtype),
        in_specs=[HBM_SPEC, SEM_SPEC, SEM_SPEC] + [pl.BlockSpec(memory_space=pl.ANY)] * len(after),
        out_specs=HBM_SPEC, input_output_aliases={0: 0},
        compiler_params=pltpu.CompilerParams(has_side_effects=DATAFLOW_EFFECT),
    )(land, *started["sems"], *after)


def _sum_slots(slots):
    def body(in_ref, out_ref):
        acc = in_ref[0]
        for d in range(1, N_DEV):
            acc = acc + in_ref[d]
        out_ref[...] = acc

    return pl.pallas_call(body, name="sum_small", out_shape=jax.ShapeDtypeStruct(slots.shape[1:], F32),
                          compiler_params=_cparams())(slots)


def _adamw_many(ws, gs, ms, vs):
    n = len(ws)

    def body(*refs):
        for i in range(n):
            w_ref, g_ref, m_ref, v_ref = (refs[j * n + i] for j in range(4))
            d_, nm, nv = _adamw_math(w_ref[...], g_ref[...], m_ref[...], v_ref[...])
            for j, val in enumerate((d_, nm, nv)):
                refs[(4 + j) * n + i][...] = val

    res = pl.pallas_call(body, name="adamw_small", out_shape=[jax.ShapeDtypeStruct(w_.shape, F32) for w_ in ws] * 3,
                         compiler_params=_cparams())(*ws, *gs, *ms, *vs)
    return [(res[i], res[n + i], res[2 * n + i]) for i in range(n)]


def _local_step(x, mem, tgt, W, P, late_weights, send_grads, reduce_small, tie0):
    S = x.shape[0]
    W = dict(W)
    h = _rmsnorm_fwd(x, P["g_mix"] + tie0, rows=S, name="norm_mix")
    mem_n = _rmsnorm_fwd(mem, P["g_mem"], rows=N_MEM, name="norm_mem")
    buckets = _dil_buckets()
    bias = _dil_bias(P["rel_bias"], buckets)
    wa_bd, wx_bd = _mx(_block_diag(P["w_rg_a"])), _mx(_block_diag(P["w_rg_x"]))
    W.update(late_weights("first", [h, mem_n, bias, wa_bd, wx_bd]))
    proj = _matmul(h, W["w_in_t"], M=S, N=D_IN, K=D_MODEL, mode="nt", bm=512, bn=D_IN // 2, bk=D_MODEL, name="mm_in",
                   j_outer=True, deps=[W["started"]])

    group_out = [_dilated_fwd(proj, bias, g, S=S) for g in range(len(DIL_GROUPS))]
    o_dil, o_dil_m, lse_dil = _dilated_merge([o for o, _ in group_out], [l for _, l in group_out], S=S)

    W.update(late_weights("branch", [o_dil]))
    lru_args = (W["conv_w"], P["conv_b"].reshape(1, -1), wa_bd, wx_bd, P["b_rg_a"].reshape(1, -1),
                P["b_rg_x"].reshape(1, -1), P["lru_lambda"].reshape(1, -1))
    hl, z_lru, a_lru, mult_lru = _lru_fwd(proj, *lru_args, S=S)
    kv = _matmul(mem_n, W["w_mem_kv"], M=N_MEM, N=2 * MEM_WIDTH, K=D_MODEL, mode="nn", bm=N_MEM, bn=512, bk=D_MODEL,
                 name="mm_kv")
    om, om_m, lse_mem = _mem_attn_fwd(proj, kv, S=S)
    b_gate = P["b_gate"].reshape(1, -1)
    merged, *gates = _mix_fwd(z_lru, o_dil_m, om_m, W["w_lru_out"], W["w_dil_out"], W["w_mem_out"], proj, b_gate, S=S)
    g_mlp, g_final, g_mix = (P[n].reshape(1, D_MODEL) for n in ("g_mlp", "g_final", "g_mix"))
    x1, hm = _matmul_rows(merged, W["w_out"], M=S, K=D_MODEL, mode="nn", bm=512, name="mm_out",
                          row_fn=_residual_then_norm, out_dtypes=(F32, MXU_DTYPE), tiles=[x], vecs=[g_mlp])
    W.update(late_weights("mlp", [hm]))

    def relu2(acc):
        rl = jnp.maximum(acc, 0.0)
        return rl * rl, rl

    act, relu_u = _matmul(hm, W["w_mlp_in_t"], M=S, N=D_FF, K=D_MODEL, mode="nt", bm=1024, bn=1024, bk=D_MODEL,
                          name="mm_mlp_in", out_dtypes=(MXU_DTYPE, MXU_DTYPE), epilogue=relu2, j_outer=True)
    dx2, dx2_m, loss, dg_final = _matmul_rows(
        act, W["w_mlp_out"], M=S, K=D_FF, mode="nn", bm=512, name="mm_mlp_out", row_fn=_residual_then_loss,
        out_dtypes=(F32, MXU_DTYPE), tiles=[x1, tgt], vecs=[g_final], acc_widths=(1, D_MODEL))

    G, Gs = {}, {}
    Gs["g_final"] = dg_final
    dw = dict(mode="tn", K=S, bk=S, out_dtypes=(MXU_DTYPE,))
    G["w_mlp_out"] = _matmul(act, dx2_m, M=D_FF, N=D_MODEL, bm=512, bn=D_MODEL, name="mm_dw_mlp_out",
                             parts=("rows", D_FF // N_DEV), **dw)
    du = _matmul(dx2_m, W["w_mlp_out"], M=S, N=D_FF, K=D_MODEL, mode="nt", bm=1024, bn=1024, bk=D_MODEL, name="mm_du",
                 out_dtypes=(MXU_DTYPE,), epilogue=lambda acc, rl: (acc * (2.0 * rl.astype(F32)),),
                 extras=[(relu_u, (0, 0))], j_outer=True)
    G["w_mlp_in"] = _matmul(hm, du, M=D_MODEL, N=D_FF, bm=D_MODEL, bn=512, name="mm_dw_mlp_in",
                            parts=("cols", D_FF // N_DEV), **dw)
    tie1 = send_grads({n: G.pop(n) for n in ("w_mlp_out", "w_mlp_in")})
    dx1, dx1_m, Gs["g_mlp"] = _matmul_rows(
        du, W["w_mlp_in_t"], M=S, K=D_FF, mode="nn", bm=512, name="mm_dhm", row_fn=_norm_bwd_then_residual(2),
        out_dtypes=(F32, MXU_DTYPE), tiles=[x1, dx2], vecs=[g_mlp], acc_widths=(D_MODEL,), deps=[tie1])
    G["w_out"] = _matmul(merged, dx1_m, M=D_MODEL, N=D_MODEL, bm=512, bn=D_MODEL, name="mm_dw_out",
                         parts=("rows", D_MODEL // N_DEV), **dw)
    (dg0, dg1, dg2, dy_lru, dy_dil, dy_mem, db0, db1, db2) = _mix_bwd(
        dx1_m, W["w_out"], z_lru, o_dil_m, om_m, W["w_lru_out"], W["w_dil_out"], W["w_mem_out"], gates, S=S)
    Gs["b_gate0"], Gs["b_gate1"], Gs["b_gate2"] = db0, db1, db2

    G["w_mem_out"] = _matmul(om_m, dy_mem, M=MEM_WIDTH, N=D_MODEL, bm=MEM_WIDTH, bn=D_MODEL, name="mm_dw_mem_out",
                             parts=("cols", D_MODEL // N_DEV), **dw)
    dqm, dk_mem, dv_mem = _mem_attn_bwd(proj, kv, om, lse_mem, dy_mem, W["w_mem_out"], S=S)
    dkv = jnp.concatenate([dk_mem, dv_mem], axis=1)
    G["w_mem_kv"] = _matmul(mem_n, dkv, M=D_MODEL, N=2 * MEM_WIDTH, K=N_MEM, mode="tn", bm=D_MODEL, bn=2 * MEM_WIDTH,
                            bk=N_MEM, name="mm_dw_kv", out_dtypes=(MXU_DTYPE,), parts=("rows", D_MODEL // N_DEV))
    dmem_n = _matmul(dkv, W["w_mem_kv"], M=N_MEM, N=D_MODEL, K=2 * MEM_WIDTH, mode="nt", bm=N_MEM, bn=D_MODEL,
                     bk=2 * MEM_WIDTH, name="mm_dmem")
    (Gs["g_mem"],) = _rmsnorm_bwd(mem, P["g_mem"], dmem_n, None, rows=N_MEM, name="norm_mem_bwd", dx_dtypes=())

    G["w_dil_out"] = _matmul(o_dil_m, dy_dil, M=256, N=D_MODEL, bm=256, bn=D_MODEL, name="mm_dw_dil_out",
                             parts=("cols", D_MODEL // N_DEV), **dw)
    do_dil, delta = _matmul(dy_dil, W["w_dil_out"], M=S, N=256, K=D_MODEL, mode="nt", bm=512, bn=256, bk=D_MODEL,
                            name="mm_do_dil", out_dtypes=(F32, F32), epilogue=_with_delta, extras=[(o_dil, (0, 0))])
    G["w_lru_out"] = _matmul(z_lru, dy_lru, M=D_RNN, N=D_MODEL, bm=D_RNN, bn=D_MODEL, name="mm_dw_lru_out",
                             parts=("cols", D_MODEL // N_DEV), **dw)
    tie2 = send_grads({n: G.pop(n) for n in ("w_out", "w_mem_out", "w_mem_kv", "w_dil_out", "w_lru_out")})
    bias = bias + tie2[0, 0]
    dqkv, dbias = None, []
    for g in range(len(DIL_GROUPS)):
        *dqkv, db_g = _dilated_bwd(proj, do_dil, lse_dil, delta, bias, g, S=S, into=dqkv)
        dbias.append(db_g)
    drel = _dil_bias_bwd(jnp.stack(dbias, axis=0).reshape(len(DIL_GROUPS), DIL_HEADS, SPAN, 2 * SPAN), buckets)
    Gs["rel_bias"] = drel

    dxl, dgl, dcw, dcb, dwa, dwx, dba, dbx, dlam = _lru_bwd(proj, hl, a_lru, mult_lru, dy_lru, W["w_lru_out"], *lru_args,
                                                            S=S)
    Gs["conv_w"], Gs["conv_b"] = dcw, dcb
    Gs["w_rg_a"], Gs["w_rg_x"] = _block_diag_extract(dwa), _block_diag_extract(dwx)
    Gs["b_rg_a"], Gs["b_rg_x"], Gs["lru_lambda"] = dba, dbx, dlam
    Gs["loss"] = loss

    dproj = [dxl, dgl] + dqkv + [dqm, dg0, dg1, dg2]
    tie = []
    for q in range(W_IN_PIECES):
        dw_q = None
        for half in range(2):
            dw_q = _dw_in_t_half(h, dproj, q, half, S=S, name=f"mm_dw_in_{q}_{half}", into=dw_q, deps=tie)
        tie = [send_grads({f"w_in_{q}": dw_q})]
    grad_x, Gs["g_mix"] = _matmul_rows(
        dproj, W["w_in_t"], M=S, K=D_IN, mode="nn", bm=256, name="mm_dh", row_fn=_norm_bwd_then_residual(1),
        out_dtypes=(F32,), tiles=[x, dx1], vecs=[g_mix], acc_widths=(D_MODEL,), deps=tie)
    return grad_x, reduce_small(Gs)


BIG = ("w_in", "w_lru_out", "w_dil_out", "w_mem_kv", "w_mem_out", "w_out", "w_mlp_in", "w_mlp_out")
W_IN_PIECES = 2
COL_SHARDED = ("w_lru_out", "w_dil_out", "w_mem_out", "w_mlp_in")
GATHERED_TRANSPOSED = ("w_mlp_in",)
SMALL = ("g_mix", "b_gate", "conv_b", "w_rg_a", "b_rg_a", "w_rg_x", "b_rg_x", "lru_lambda", "rel_bias", "g_mem",
         "g_mlp", "g_final")
WEIGHTS = ("g_mix", "w_in", "b_gate", "conv_w", "conv_b", "w_rg_a", "b_rg_a", "w_rg_x", "b_rg_x", "lru_lambda",
           "w_lru_out", "rel_bias", "w_dil_out", "g_mem", "w_mem_kv", "w_mem_out", "w_out", "g_mlp", "w_mlp_in",
           "w_mlp_out", "g_final")


def _gathered_to_full(name, gathered):
    if name in COL_SHARDED:
        n, r, c = gathered.shape
        return gathered.transpose(1, 0, 2).reshape(r, n * c)
    n, r, c = gathered.shape
    return gathered.reshape(n * r, c)


SMALL_GRADS = (("g_mix", (1, 1024)), ("b_gate0", (1, 1024)), ("b_gate1", (1, 1024)), ("b_gate2", (1, 1024)),
               ("conv_b", (1, 768)), ("w_rg_a", (12, 64, 64)), ("b_rg_a", (1, 768)), ("w_rg_x", (12, 64, 64)),
               ("b_rg_x", (1, 768)), ("lru_lambda", (1, 768)), ("rel_bias", (32, 128)), ("g_mem", (1, 1024)),
               ("g_mlp", (1, 1024)), ("g_final", (1, 1024)), ("conv_w", (4, 768)), ("loss", (1, 1)))


def _pack(parts):
    flat = jnp.concatenate([p.reshape(-1) for p in parts])
    return jnp.pad(flat, (0, (-flat.shape[0]) % 1024)).reshape(-1, 128)


def _unpack(pack, shapes):
    flat = pack.reshape(-1)
    out, off = [], 0
    for shp in shapes:
        size = math.prod(shp)
        out.append(flat[off:off + size].reshape(shp))
        off += size
    return out


def kernel(x, mem, g_mix, w_in, b_gate, conv_w, conv_b, w_rg_a, b_rg_a, w_rg_x, b_rg_x, lru_lambda, w_lru_out, rel_bias, w_dil_out, g_mem, w_mem_kv, w_mem_out, w_out, g_mlp, w_mlp_in, w_mlp_out, g_final, loss_target, m_g_mix, m_w_in, m_b_gate, m_conv_w, m_conv_b, m_w_rg_a, m_b_rg_a, m_w_rg_x, m_b_rg_x, m_lru_lambda, m_w_lru_out, m_rel_bias, m_w_dil_out, m_g_mem, m_w_mem_kv, m_w_mem_out, m_w_out, m_g_mlp, m_w_mlp_in, m_w_mlp_out, m_g_final, v_g_mix, v_w_in, v_b_gate, v_conv_w, v_conv_b, v_w_rg_a, v_b_rg_a, v_w_rg_x, v_b_rg_x, v_lru_lambda, v_w_lru_out, v_rel_bias, v_w_dil_out, v_g_mem, v_w_mem_kv, v_w_mem_out, v_w_out, v_g_mlp, v_w_mlp_in, v_w_mlp_out, v_g_final):
    w = dict(g_mix=g_mix, w_in=w_in, b_gate=b_gate, conv_w=conv_w, conv_b=conv_b, w_rg_a=w_rg_a, b_rg_a=b_rg_a,
             w_rg_x=w_rg_x, b_rg_x=b_rg_x, lru_lambda=lru_lambda, w_lru_out=w_lru_out, rel_bias=rel_bias,
             w_dil_out=w_dil_out, g_mem=g_mem, w_mem_kv=w_mem_kv, w_mem_out=w_mem_out, w_out=w_out, g_mlp=g_mlp,
             w_mlp_in=w_mlp_in, w_mlp_out=w_mlp_out, g_final=g_final)
    m = dict(g_mix=m_g_mix, w_in=m_w_in, b_gate=m_b_gate, conv_w=m_conv_w, conv_b=m_conv_b, w_rg_a=m_w_rg_a,
             b_rg_a=m_b_rg_a, w_rg_x=m_w_rg_x, b_rg_x=m_b_rg_x, lru_lambda=m_lru_lambda, w_lru_out=m_w_lru_out,
             rel_bias=m_rel_bias, w_dil_out=m_w_dil_out, g_mem=m_g_mem, w_mem_kv=m_w_mem_kv, w_mem_out=m_w_mem_out,
             w_out=m_w_out, g_mlp=m_g_mlp, w_mlp_in=m_w_mlp_in, w_mlp_out=m_w_mlp_out, g_final=m_g_final)
    v = dict(g_mix=v_g_mix, w_in=v_w_in, b_gate=v_b_gate, conv_w=v_conv_w, conv_b=v_conv_b, w_rg_a=v_w_rg_a,
             b_rg_a=v_b_rg_a, w_rg_x=v_w_rg_x, b_rg_x=v_b_rg_x, lru_lambda=v_lru_lambda, w_lru_out=v_w_lru_out,
             rel_bias=v_rel_bias, w_dil_out=v_w_dil_out, g_mem=v_g_mem, w_mem_kv=v_w_mem_kv, w_mem_out=v_w_mem_out,
             w_out=v_w_out, g_mlp=v_g_mlp, w_mlp_in=v_w_mlp_in, w_mlp_out=v_w_mlp_out, g_final=v_g_final)

    my_idx = _dev_index(_my_pos())

    w_in_shard = _mx(w["w_in"].T)
    first = _push_start([w_in_shard], [(N_DEV,) + w_in_shard.shape], _gather_refs, "gather_in_start",
                        relations=ONE_PER_CHIP)
    cw_cols = D_RNN // N_DEV
    conv_pad = jnp.zeros((64, D_MODEL), F32).at[:CONV_WIDTH, :cw_cols].set(w["conv_w"])
    late = {}
    P = {n: w[n] for n in SMALL}

    def start_late(order_after):
        for group, names in (("branch", ("w_mem_kv", "w_lru_out", "w_dil_out", "w_mem_out", "w_out", "conv_w")),
                             ("mlp", ("w_mlp_in", "w_mlp_out"))):
            shards = [conv_pad if n == "conv_w" else _mx(w[n].T if n in GATHERED_TRANSPOSED else w[n]) for n in names]
            started = _push_start(shards, [(N_DEV,) + s.shape for s in shards], _gather_refs, f"gather_{group}_start",
                                  after=order_after)
            late[group] = (names, shards, started)
            order_after = [started["token"]]

    start_late([first["token"]])

    def late_weights(group, after):
        if group == "first":
            (land,) = _push_wait(first, after)
            forward = _forward_start(land, "forward_in_start")
            full = lax.dynamic_update_index_in_dim(_forward_wait(forward, forward["token"]), w_in_shard, my_idx, 0)
            return {"w_in_t": full.reshape(D_IN, D_MODEL), "started": late["mlp"][2]["token"]}
        names, shards, started = late[group]
        out = {}
        for n, land, own in zip(names, _push_wait(started, after), shards):
            full = lax.dynamic_update_index_in_dim(land, own, my_idx, 0)
            if n == "conv_w":
                out[n] = full[:, :CONV_WIDTH, :cw_cols].transpose(1, 0, 2).reshape(CONV_WIDTH, D_RNN)
            elif n in GATHERED_TRANSPOSED:
                out[n + "_t"] = full.reshape(-1, full.shape[2])
            else:
                out[n] = _gathered_to_full(n, full)
        return out

    sent, small = [], {}

    def send_grads(gs):
        names = list(gs)
        parts = [gs[n] for n in names]
        started = _push_start(parts, [(N_DEV - 1,) + p.shape[1:] for p in parts], _scatter_refs,
                              f"scatter{len(sent)}_start")
        sent.append((names, started))
        return started["token"]

    def reduce_small(gs):
        small["pack"] = _pack([gs[n] for n, _ in SMALL_GRADS])
        small["started"] = _push_start([small["pack"]], [(N_DEV,) + small["pack"].shape], _gather_refs, "small_start")
        return small["started"]["token"]

    grad_x, last_token = _local_step(x[0], mem[0], loss_target[0], {}, P, late_weights, send_grads, reduce_small,
                                     late["mlp"][2]["token"][0, 0])

    grads, deltas, new_m, new_v = {}, {}, {}, {}
    after = last_token
    my_slot = jnp.reshape(my_idx, (1,)).astype(jnp.int32)
    for names, started in sent[:-W_IN_PIECES]:
        for n, parts, land in zip(names, *_push_wait(started, after, with_sources=True)):
            grads[n], deltas[n], new_m[n], new_v[n] = _adamw_landed(w[n], parts, my_slot, land, m[n], v[n],
                                                                    name=f"adamw_{n}")
            after = deltas[n]
    prev = None
    for q, (names, started) in enumerate(sent[-W_IN_PIECES:]):
        (parts,), (land,) = _push_wait(started, after, with_sources=True)
        prev = _adamw_landed(w["w_in"].T, parts, my_slot, land, m["w_in"].T, v["w_in"].T, name=f"adamw_{names[0]}",
                             col_blk=q, prev=prev)
        after = prev[1]
    grads["w_in"], deltas["w_in"], new_m["w_in"], new_v["w_in"] = [t.T for t in prev]
    (small_land,) = _push_wait(small["started"], [after] + [deltas[n] for n in BIG if n != "w_in"])
    total = _sum_slots(lax.dynamic_update_index_in_dim(small_land, small["pack"], my_idx, 0))
    summed = dict(zip([n for n, _ in SMALL_GRADS], _unpack(total, [shp for _, shp in SMALL_GRADS])))
    summed["b_gate"] = jnp.concatenate([summed.pop(f"b_gate{b}") for b in range(3)], axis=1)
    summed["rel_bias"] = summed["rel_bias"][:, :3 * DIL_HEADS]
    for n in SMALL:
        grads[n] = summed[n].reshape(w[n].shape)
    small_updates = _adamw_many([w[n] for n in SMALL], [grads[n] for n in SMALL], [m[n] for n in SMALL],
                                [v[n] for n in SMALL])
    for n, (d_, nm_, nv_) in zip(SMALL, small_updates):
        deltas[n], new_m[n], new_v[n] = d_, nm_, nv_
    conv_w_sum, loss_sum = summed["conv_w"], summed["loss"]
    grads["conv_w"] = lax.dynamic_slice(conv_w_sum, (0, my_idx * cw_cols), (CONV_WIDTH, cw_cols))
    deltas["conv_w"], new_m["conv_w"], new_v["conv_w"] = _adamw_plain(
        w["conv_w"], grads["conv_w"], m["conv_w"], v["conv_w"], name="adamw_conv_w")

    return (loss_sum.reshape(()), grad_x[None], *[grads[n] for n in WEIGHTS], *[deltas[n] for n in WEIGHTS],
            *[new_m[n] for n in WEIGHTS], *[new_v[n] for n in WEIGHTS])
```

```python
import functools
import math

import jax
import jax.numpy as jnp
from jax import lax
from jax.experimental import pallas as pl
from jax.experimental.pallas import tpu as pltpu

F32 = jnp.float32
MXU_DTYPE = jnp.bfloat16
VMEM_LIMIT_BYTES = 56 * 1024 * 1024
N_DEV = 8

D_MODEL = 1024
N_MEM = 256
MEM_HEADS = 4
MEM_HEAD_DIM = 128
MEM_WIDTH = 512
D_RNN = 768
LRU_BLOCK = 64
N_LRU_BLOCKS = 12
LRU_GROUP = 256
N_LRU_GROUPS = 3
CONV_WIDTH = 4
LRU_C = 8.0
DIL_GROUPS = ((128, 1), (512, 4), (2048, 16))
SPAN = 128
DIL_HEADS = 4
DIL_HEAD_DIM = 64
NUM_BUCKETS = 32
MAX_DISTANCE = 2048
D_FF = 4096
D_IN = 7424
EPS = 1e-6
NEG = -1e30
C_XL, C_GATE, C_QKV, C_QM, C_GATES = 0, 768, 1536, 3840, 4352

ADAM_LR = 0.001
ADAM_B1 = 0.9
ADAM_B2 = 0.999
ADAM_EPS = 1e-08
ADAM_WD = 0.01
ADAM_STEP = 10

MESH = pl.DeviceIdType.MESH
GELU_K = math.sqrt(2.0 / math.pi)


def _cparams(sem=None):
    kw = dict(vmem_limit_bytes=VMEM_LIMIT_BYTES)
    if sem is not None:
        kw["dimension_semantics"] = sem
    return pltpu.CompilerParams(**kw)


def _mx(v):
    return v.astype(MXU_DTYPE)


def _dot(a, b, mode="nn"):
    dims = {"nn": (((1,), (0,)), ((), ())), "nt": (((1,), (1,)), ((), ())), "tn": (((0,), (0,)), ((), ()))}[mode]
    return lax.dot_general(_mx(a), _mx(b), dims, preferred_element_type=F32)


def _colsum(v):
    return jnp.sum(v, axis=0, keepdims=True)


def _matmul(a, b, *, M, N, K, mode, bm, bn, bk, name, out_dtypes=(F32,), epilogue=None, extras=(),
            a_off=(0, 0), b_off=(0, 0), j_outer=False, deps=(), parts=None):
    assert M % bm == 0 and N % bn == 0 and K % bk == 0, (name, M, N, K, bm, bn, bk)
    nm, nn, nk = M // bm, N // bn, K // bk

    def ij(f):
        if j_outer:
            return lambda j, i, k: f(i, j, k)
        return f

    if mode == "tn":
        a_spec = pl.BlockSpec((bk, bm), ij(lambda i, j, k: (k + a_off[0], i + a_off[1])))
    else:
        a_spec = pl.BlockSpec((bm, bk), ij(lambda i, j, k: (i + a_off[0], k + a_off[1])))
    if mode == "nt":
        b_spec = pl.BlockSpec((bn, bk), ij(lambda i, j, k: (j + b_off[0], k + b_off[1])))
    else:
        b_spec = pl.BlockSpec((bk, bn), ij(lambda i, j, k: (k + b_off[0], j + b_off[1])))
    ex_specs = [pl.BlockSpec((bm, bn), ij(functools.partial(lambda i, j, k, o: (i + o[0], j + o[1]), o=off)))
                for _, off in extras]
    if parts is None:
        out_dims = (M, N)
        out_spec = pl.BlockSpec((bm, bn), ij(lambda i, j, k: (i, j)))
    elif parts[0] == "rows":
        r = parts[1]
        assert bm % r == 0
        out_dims = (M // r, r, N)
        out_spec = pl.BlockSpec((bm // r, r, bn), ij(lambda i, j, k: (i, 0, j)))
    elif parts[0] == "rows_t":
        r = parts[1]
        assert bn % r == 0
        out_dims = (N // r, r, M)
        out_spec = pl.BlockSpec((bn // r, r, bm), ij(lambda i, j, k: (j, 0, i)))
    else:
        c = parts[1]
        assert bn % c == 0
        out_dims = (N // c, M, c)
        out_spec = pl.BlockSpec((bn // c, bm, c), ij(lambda i, j, k: (j, i, 0)))
    n_ex, n_out, n_dep = len(extras), len(out_dtypes), len(deps)

    def body(*refs):
        a_ref, b_ref = refs[0], refs[1]
        ex = refs[2:2 + n_ex]
        outs = refs[2 + n_ex + n_dep:2 + n_ex + n_dep + n_out]
        part = _dot(a_ref[...], b_ref[...], mode)

        def finish(acc):
            vals = epilogue(acc, *[e[...] for e in ex]) if epilogue is not None else (acc,)
            for o, v in zip(outs, vals):
                if parts is not None and parts[0] == "rows_t":
                    v = v.T
                v = v.astype(o.dtype)
                if parts is None:
                    o[...] = v
                elif parts[0] in ("rows", "rows_t"):
                    for ch in range(v.shape[0] // parts[1]):
                        o[ch] = v[ch * parts[1]:(ch + 1) * parts[1], :]
                else:
                    for ch in range(bn // parts[1]):
                        o[ch] = v[:, ch * parts[1]:(ch + 1) * parts[1]]

        if nk == 1:
            finish(part)
        else:
            acc_ref = refs[-1]
            k = pl.program_id(2)

            @pl.when(k == 0)
            def _():
                acc_ref[...] = part

            @pl.when(k > 0)
            def _():
                acc_ref[...] += part

            @pl.when(k == nk - 1)
            def _():
                finish(acc_ref[...])

    grid = (nn, nm, nk) if j_outer else (nm, nn, nk)
    res = pl.pallas_call(
        body, name=name, grid=grid,
        in_specs=[a_spec, b_spec] + ex_specs + [pl.BlockSpec(memory_space=pl.ANY)] * n_dep,
        out_specs=[out_spec] * n_out,
        out_shape=[jax.ShapeDtypeStruct(out_dims, dt) for dt in out_dtypes],
        scratch_shapes=[pltpu.VMEM((bm, bn), F32)] if nk > 1 else [],
        compiler_params=_cparams(("parallel", "parallel", "arbitrary")),
    )(a, b, *[e for e, _ in extras], *deps)
    return res[0] if n_out == 1 else res


ROW_SUBTILES = 2


def _matmul_rows(a, b, *, M, K, mode, bm, name, row_fn, out_dtypes, tiles=(), vecs=(), acc_widths=(), deps=()):
    N = D_MODEL
    assert M % bm == 0
    segs = list(a) if isinstance(a, (list, tuple)) else [a]
    widths = [s_.shape[1] for s_ in segs]
    assert sum(widths) == K and (len(segs) == 1 or mode == "nn")
    n_s, n_t, n_v, n_o, n_a, n_d = len(segs), len(tiles), len(vecs), len(out_dtypes), len(acc_widths), len(deps)
    row = pl.BlockSpec((bm, N), lambda i: (i, 0))
    b_shape = (K, N) if mode == "nn" else (N, K)

    def body(*refs):
        b_ref = refs[n_s]
        ins = refs[n_s + 1:n_s + 1 + n_t + n_v]
        outs = refs[n_s + 1 + n_t + n_v + n_d:n_s + 1 + n_t + n_v + n_d + n_o]
        accs = refs[n_s + 1 + n_t + n_v + n_d + n_o:]
        for o in accs:
            @pl.when(pl.program_id(0) == 0)
            def _(o=o):
                o[...] = jnp.zeros_like(o)

        for s_ in range(ROW_SUBTILES):
            rows = pl.ds(s_ * (bm // ROW_SUBTILES), bm // ROW_SUBTILES)
            if n_s == 1:
                acc = _dot(refs[0][rows, :], b_ref[...], mode)
            else:
                acc, k0 = None, 0
                for a_ref, w_ in zip(refs[:n_s], widths):
                    part = _dot(a_ref[rows, :], b_ref[k0:k0 + w_, :])
                    acc = part if acc is None else acc + part
                    k0 += w_
            tile_vals, partials = row_fn(acc, *[r[rows, :] for r in ins[:n_t]], *[r[...] for r in ins[n_t:]])
            for o, val in zip(outs, tile_vals):
                o[rows, :] = val.astype(o.dtype)
            for o, val in zip(accs, partials):
                o[...] += val

    res = pl.pallas_call(
        body, name=name, grid=(M // bm,),
        in_specs=[pl.BlockSpec((bm, w_), lambda i: (i, 0)) for w_ in widths] + [pl.BlockSpec(b_shape, lambda i: (0, 0))]
        + [row] * n_t + [pl.BlockSpec((1, N), lambda i: (0, 0))] * n_v + [pl.BlockSpec(memory_space=pl.ANY)] * n_d,
        out_specs=[row] * n_o + [pl.BlockSpec((1, w_), lambda i: (0, 0)) for w_ in acc_widths],
        out_shape=[jax.ShapeDtypeStruct((M, N), dt) for dt in out_dtypes]
        + [jax.ShapeDtypeStruct((1, w_), F32) for w_ in acc_widths],
        compiler_params=_cparams(("arbitrary",) if n_a else ("parallel",)),
    )(*segs, b, *tiles, *vecs, *deps)
    return res


def _dw_in_t_half(h, pieces, q, half, *, S, name, into=None, deps=(), bk=1024):
    half_w, cols = D_IN // 2, D_MODEL // W_IN_PIECES
    lo, hi = half * half_w, (half + 1) * half_w
    use, c0 = [], 0
    for p in pieces:
        w_ = p.shape[1]
        a0, a1 = max(lo, c0), min(hi, c0 + w_)
        if a1 > a0:
            use.append((p, a0 - c0, a1 - a0))
        c0 += w_
    n_p, n_into, n_d, nk = len(use), 0 if into is None else 1, len(deps), S // bk
    rows = D_IN // N_DEV

    def body(*refs):
        h_ref, p_refs = refs[0], refs[1:1 + n_p]
        o_ref, acc_ref = refs[1 + n_p + n_into + n_d], refs[-1]
        k = pl.program_id(0)
        dp = jnp.concatenate([r[:, s0:s0 + w_] for r, (_, s0, w_) in zip(p_refs, use)], axis=1)
        part = _dot(h_ref[...], dp, "tn")

        @pl.when(k == 0)
        def _():
            acc_ref[...] = part

        @pl.when(k > 0)
        def _():
            acc_ref[...] += part

        @pl.when(k == nk - 1)
        def _():
            vt = acc_ref[...].T.astype(o_ref.dtype)
            for ch in range(half_w // rows):
                o_ref[ch] = vt[ch * rows:(ch + 1) * rows, :]

    return pl.pallas_call(
        body, name=name, grid=(nk,),
        in_specs=[pl.BlockSpec((bk, cols), lambda k: (k, q))]
        + [pl.BlockSpec((bk, p.shape[1]), lambda k: (k, 0)) for p, _, _ in use]
        + [pl.BlockSpec(memory_space=pl.ANY)] * (n_into + n_d),
        out_specs=pl.BlockSpec((half_w // rows, rows, cols), lambda k: (half, 0, 0)),
        out_shape=jax.ShapeDtypeStruct((N_DEV, rows, cols), MXU_DTYPE),
        input_output_aliases={1 + n_p: 0} if n_into else {},
        scratch_shapes=[pltpu.VMEM((cols, half_w), F32)],
        compiler_params=_cparams(("arbitrary",)),
    )(h, *[p for p, _, _ in use], *([into] if n_into else []), *deps)


def _rmsnorm_fwd(x, g, *, rows, name, bt=512):
    bt = min(bt, rows)

    def body(x_ref, g_ref, o_ref):
        xv = x_ref[...]
        r = lax.rsqrt(jnp.mean(xv * xv, axis=-1, keepdims=True) + EPS)
        o_ref[...] = (xv * r * g_ref[...]).astype(o_ref.dtype)

    return pl.pallas_call(
        body, name=name, grid=(rows // bt,),
        in_specs=[pl.BlockSpec((bt, D_MODEL), lambda i: (i, 0)), pl.BlockSpec((1, D_MODEL), lambda i: (0, 0))],
        out_specs=pl.BlockSpec((bt, D_MODEL), lambda i: (i, 0)),
        out_shape=jax.ShapeDtypeStruct((rows, D_MODEL), MXU_DTYPE),
        compiler_params=_cparams(("parallel",)),
    )(x, g.reshape(1, D_MODEL))


def _rms_bwd_tile(xv, gv, dyv):
    r = lax.rsqrt(jnp.mean(xv * xv, axis=-1, keepdims=True) + EPS)
    w = dyv * gv
    dx = r * w - xv * (r * r * r) * jnp.mean(w * xv, axis=-1, keepdims=True)
    dg = _colsum(dyv * xv * r)
    return dx, dg


def _residual_then_norm(acc, x_t, g):
    x1 = x_t + acc
    r = lax.rsqrt(jnp.mean(x1 * x1, axis=-1, keepdims=True) + EPS)
    return (x1, x1 * r * g), ()


def _residual_then_loss(acc, x_t, tgt_t, g):
    x2 = x_t + acc
    r = lax.rsqrt(jnp.mean(x2 * x2, axis=-1, keepdims=True) + EPS)
    diff = x2 * r * g - tgt_t
    part = jnp.sum(jnp.mean(diff * diff, axis=-1, keepdims=True), axis=0, keepdims=True) * 0.5
    dx, dg = _rms_bwd_tile(x2, g, diff * (1.0 / D_MODEL))
    return (dx, dx), (part, dg)


def _norm_bwd_then_residual(n_out):
    def fn(acc, x_t, res_t, g):
        dx, dg = _rms_bwd_tile(x_t, g, acc)
        return (dx + res_t,) * n_out, (dg,)

    return fn


def _rmsnorm_bwd(x, g, dy, res, *, rows, name, bt=512, dx_dtypes=(F32,)):
    bt = min(bt, rows)
    has_res = res is not None

    def body(*refs):
        x_ref, g_ref, dy_ref = refs[:3]
        res_ref = refs[3] if has_res else None
        outs = refs[3 + int(has_res):]
        dx, dg = _rms_bwd_tile(x_ref[...], g_ref[...], dy_ref[...])
        if has_res:
            dx = dx + res_ref[...]
        dg_ref = outs[-1]

        @pl.when(pl.program_id(0) == 0)
        def _():
            dg_ref[...] = jnp.zeros_like(dg_ref)

        dg_ref[...] += dg
        for o in outs[:-1]:
            o[...] = dx.astype(o.dtype)

    row_spec = pl.BlockSpec((bt, D_MODEL), lambda i: (i, 0))
    vec_spec = pl.BlockSpec((1, D_MODEL), lambda i: (0, 0))
    ins = [x, g.reshape(1, D_MODEL), dy] + ([res] if has_res else [])
    return pl.pallas_call(
        body, name=name, grid=(rows // bt,),
        in_specs=[row_spec, vec_spec, row_spec] + ([row_spec] if has_res else []),
        out_specs=[row_spec] * len(dx_dtypes) + [vec_spec],
        out_shape=[jax.ShapeDtypeStruct((rows, D_MODEL), dt) for dt in dx_dtypes] + [jax.ShapeDtypeStruct((1, D_MODEL), F32)],
        compiler_params=_cparams(("arbitrary",)),
    )(*ins)


LRU_T = 512
SCAN_GROUPS = 4


def _gelu(x):
    t = jnp.tanh(GELU_K * (x + 0.044715 * x * x * x))
    return 0.5 * x * (1.0 + t), t


def _gelu_grad(x, t):
    return 0.5 * (1.0 + t) + 0.5 * x * (1.0 - t * t) * GELU_K * (1.0 + 3.0 * 0.044715 * x * x)


def _softplus_neg(lam):
    z = -lam
    u = jnp.exp(-jnp.abs(z))
    w = 1.0 + u
    l1p = jnp.where(w == 1.0, u, jnp.log(w) * u / jnp.where(w == 1.0, 1.0, w - 1.0))
    return jnp.maximum(z, 0.0) + l1p


def _shift_down(cur, prev8, k, row8):
    y = pltpu.roll(cur, k, 0)
    head = jnp.where(row8 < k, pltpu.roll(prev8, k, 0), y[0:8])
    return jnp.concatenate([head, y[8:]], axis=0)


def _shift_up(cur, next8, k, row8):
    n = cur.shape[0]
    y = pltpu.roll(cur, n - k, 0)
    tail = jnp.where(row8 >= 8 - k, pltpu.roll(next8, 8 - k, 0), y[n - 8:n])
    return jnp.concatenate([y[0:n - 8], tail], axis=0)


def _lru_gates(xl, p8, cw, cb, wa, wx, ba, bx, lam, row8, a_mult=None):
    sh = [xl] + [_shift_down(xl, p8, k, row8) for k in (1, 2, 3)]
    xc = cb + cw[3:4] * sh[0] + cw[2:3] * sh[1] + cw[1:2] * sh[2] + cw[0:1] * sh[3]
    r = jax.nn.sigmoid(_dot(xc, wa) + ba)
    i = jax.nn.sigmoid(_dot(xc, wx) + bx)
    sp = _softplus_neg(lam)
    if a_mult is None:
        la = -LRU_C * r * sp
        a = jnp.exp(la)
        mult = jnp.sqrt(jnp.tanh(-la) * (a * a + 1.0))
    else:
        a, mult = a_mult
    return dict(sh=sh, xc=xc, r=r, i=i, sp=sp, a=a, mult=mult)


def _lru_specs(n_t, reverse):
    T = LRU_T
    tt = (lambda t: n_t - 1 - t) if reverse else (lambda t: t)
    blk = lambda col0: pl.BlockSpec((T, LRU_GROUP), lambda g, t: (tt(t), col0 + g))
    prev8 = lambda col0: pl.BlockSpec((8, LRU_GROUP), lambda g, t: (jnp.maximum(tt(t) * (T // 8) - 1, 0), col0 + g))
    vec = lambda rows: pl.BlockSpec((rows, LRU_GROUP), lambda g, t: (0, g))
    wbd = pl.BlockSpec((1, LRU_GROUP, LRU_GROUP), lambda g, t: (g, 0, 0))
    return blk, prev8, vec, wbd


def _lru_fwd(proj, conv_w, conv_b, wa_bd, wx_bd, b_a, b_x, lam, *, S):
    T = LRU_T
    n_t = S // T
    blk, _, vec, wbd = _lru_specs(n_t, False)

    def body(xl_ref, gate_ref, cw_ref, cb_ref, wa_ref, wx_ref, ba_ref, bx_ref, lam_ref,
             hl_ref, z_ref, a_s, m_ref, prev8, hcar, b_s):
        @pl.when(pl.program_id(1) == 0)
        def _():
            prev8[...] = jnp.zeros_like(prev8)
            hcar[...] = jnp.zeros_like(hcar)

        row8 = lax.broadcasted_iota(jnp.int32, (8, LRU_GROUP), 0)
        xl = xl_ref[...]
        q = _lru_gates(xl, prev8[...], cw_ref[...], cb_ref[...], wa_ref[0], wx_ref[0], ba_ref[...], bx_ref[...],
                       lam_ref[...], row8)
        prev8[...] = xl[T - 8:T]
        a_s[...] = q["a"]
        m_ref[...] = q["mult"]
        b_s[...] = q["mult"] * q["i"] * q["xc"]

        def step(c, carry):
            local = []
            for u in range(SCAN_GROUPS):
                off = pl.multiple_of((c * SCAN_GROUPS + u) * 8, 8)
                A = a_s[pl.ds(off, 8), :]
                B = b_s[pl.ds(off, 8), :]
                for k in (1, 2, 4):
                    a_sh = jnp.where(row8 >= k, pltpu.roll(A, k, 0), 1.0)
                    b_sh = jnp.where(row8 >= k, pltpu.roll(B, k, 0), 0.0)
                    B = A * b_sh + B
                    A = A * a_sh
                local.append((off, A, B))
            for off, A, B in local:
                h = A * carry + B
                hl_ref[pl.ds(off, 8), :] = h
                carry = h[7:8, :]
            return carry

        hcar[...] = lax.fori_loop(0, T // (8 * SCAN_GROUPS), step, hcar[...])
        ge, _ = _gelu(gate_ref[...])
        z_ref[...] = (ge * hl_ref[...]).astype(z_ref.dtype)

    return pl.pallas_call(
        body, name="lru_fwd", grid=(N_LRU_GROUPS, n_t),
        in_specs=[blk(C_XL // LRU_GROUP), blk(C_GATE // LRU_GROUP), vec(4), vec(1), wbd, wbd, vec(1), vec(1), vec(1)],
        out_specs=[blk(0)] * 4,
        out_shape=[jax.ShapeDtypeStruct((S, D_RNN), F32), jax.ShapeDtypeStruct((S, D_RNN), MXU_DTYPE),
                   jax.ShapeDtypeStruct((S, D_RNN), F32), jax.ShapeDtypeStruct((S, D_RNN), F32)],
        scratch_shapes=[pltpu.VMEM((8, LRU_GROUP), F32), pltpu.VMEM((1, LRU_GROUP), F32), pltpu.VMEM((T, LRU_GROUP), F32)],
        compiler_params=_cparams(("parallel", "arbitrary")),
    )(proj, proj, conv_w, conv_b, wa_bd, wx_bd, b_a, b_x, lam)


def _lru_bwd(proj, hl, a_fwd, mult_fwd, dy, w_out, conv_w, conv_b, wa_bd, wx_bd, b_a, b_x, lam, *, S):
    T = LRU_T
    n_t = S // T
    blk, prev8s, vec, wbd = _lru_specs(n_t, True)

    def body(xl_ref, xlp_ref, gate_ref, hl_ref, hlp_ref, a_ref, m_ref, dy_ref, wo_ref, cw_ref, cb_ref, wa_ref, wx_ref,
             ba_ref, bx_ref, lam_ref, dxl_ref, dgate_ref, dcw_ref, dcb_ref, dwa_ref, dwx_ref, dba_ref, dbx_ref, dlam_ref,
             next8, gcar, c_s, b_s, l_s):
        t = pl.program_id(1)
        first_chunk = t == n_t - 1

        @pl.when(t == 0)
        def _():
            next8[...] = jnp.zeros_like(next8)
            gcar[...] = jnp.zeros_like(gcar)
            for ref in (dcw_ref, dcb_ref, dwa_ref, dwx_ref, dba_ref, dbx_ref, dlam_ref):
                ref[...] = jnp.zeros_like(ref)

        row8 = lax.broadcasted_iota(jnp.int32, (8, LRU_GROUP), 0)
        rowT = lax.broadcasted_iota(jnp.int32, (T, LRU_GROUP), 0)
        keep = jnp.where(first_chunk, 0.0, 1.0)
        xl = xl_ref[...]
        wa, wx, lam_v = wa_ref[0], wx_ref[0], lam_ref[...]
        q = _lru_gates(xl, xlp_ref[...] * keep, cw_ref[...], cb_ref[...], wa, wx, ba_ref[...], bx_ref[...], lam_v, row8,
                       a_mult=(a_ref[...], m_ref[...]))
        a, mult, r, i, xc, sp = q["a"], q["mult"], q["r"], q["i"], q["xc"], q["sp"]
        hl_v = hl_ref[...]
        dz_v = _dot(dy_ref[...], wo_ref[...], "nt")
        gate = gate_ref[...]
        ge, th = _gelu(gate)
        dgate_ref[...] = (dz_v * hl_v * _gelu_grad(gate, th)).astype(dgate_ref.dtype)

        c_s[...] = jnp.where(rowT == T - 1, 0.0, pltpu.roll(a, T - 1, 0))
        b_s[...] = dz_v * ge + jnp.where(rowT == T - 1, gcar[...], 0.0)

        def step(n, carry):
            local = []
            for u in range(SCAN_GROUPS):
                off = pl.multiple_of((T // 8 - 1 - (n * SCAN_GROUPS + u)) * 8, 8)
                C = c_s[pl.ds(off, 8), :]
                B = b_s[pl.ds(off, 8), :]
                for k in (1, 2, 4):
                    c_sh = jnp.where(row8 < 8 - k, pltpu.roll(C, 8 - k, 0), 1.0)
                    b_sh = jnp.where(row8 < 8 - k, pltpu.roll(B, 8 - k, 0), 0.0)
                    B = B + C * b_sh
                    C = C * c_sh
                local.append((off, C, B))
            for off, C, B in local:
                lam_t = B + C * carry
                l_s[pl.ds(off, 8), :] = lam_t
                carry = lam_t[0:1, :]
            return carry

        lax.fori_loop(0, T // (8 * SCAN_GROUPS), step, jnp.zeros((1, LRU_GROUP), F32))
        lmb = l_s[...]
        gcar[...] = a[0:1, :] * lmb[0:1, :]

        h_prev = _shift_down(hl_v, hlp_ref[...] * keep, 1, row8)
        da = lmb * h_prev
        dmult = lmb * i * xc
        di = lmb * mult * xc
        dxc = lmb * mult * i
        dla = da * a - dmult * (a * a) / mult
        dr = dla * (-LRU_C * sp)
        dlam_ref[...] += _colsum(dla * (-LRU_C * r)) * (-jax.nn.sigmoid(-lam_v))
        dpa = dr * r * (1.0 - r)
        dpx = di * i * (1.0 - i)
        dxc = dxc + _dot(dpa, wa, "nt") + _dot(dpx, wx, "nt")
        dwa_ref[0] += _dot(xc, dpa, "tn")
        dwx_ref[0] += _dot(xc, dpx, "tn")
        dba_ref[...] += _colsum(dpa)
        dbx_ref[...] += _colsum(dpx)
        dcb_ref[...] += _colsum(dxc)
        cw = cw_ref[...]
        n8 = next8[...]
        dxl = cw[3:4] * dxc
        for k in (1, 2, 3):
            dxl = dxl + cw[3 - k:4 - k] * _shift_up(dxc, n8, k, row8)
        for k in range(4):
            dcw_ref[3 - k:4 - k, :] += _colsum(dxc * q["sh"][k])
        next8[...] = dxc[0:8]
        dxl_ref[...] = dxl.astype(dxl_ref.dtype)

    res = pl.pallas_call(
        body, name="lru_bwd", grid=(N_LRU_GROUPS, n_t),
        in_specs=[blk(C_XL // LRU_GROUP), prev8s(C_XL // LRU_GROUP), blk(C_GATE // LRU_GROUP), blk(0), prev8s(0), blk(0),
                  blk(0), pl.BlockSpec((T, D_MODEL), lambda g, t: (n_t - 1 - t, 0)),
                  pl.BlockSpec((LRU_GROUP, D_MODEL), lambda g, t: (g, 0)), vec(4), vec(1), wbd, wbd, vec(1), vec(1), vec(1)],
        out_specs=[blk(0), blk(0), vec(4), vec(1), wbd, wbd, vec(1), vec(1), vec(1)],
        out_shape=[jax.ShapeDtypeStruct((S, D_RNN), MXU_DTYPE), jax.ShapeDtypeStruct((S, D_RNN), MXU_DTYPE),
                   jax.ShapeDtypeStruct((4, D_RNN), F32), jax.ShapeDtypeStruct((1, D_RNN), F32),
                   jax.ShapeDtypeStruct((N_LRU_GROUPS, LRU_GROUP, LRU_GROUP), F32),
                   jax.ShapeDtypeStruct((N_LRU_GROUPS, LRU_GROUP, LRU_GROUP), F32),
                   jax.ShapeDtypeStruct((1, D_RNN), F32), jax.ShapeDtypeStruct((1, D_RNN), F32),
                   jax.ShapeDtypeStruct((1, D_RNN), F32)],
        scratch_shapes=[pltpu.VMEM((8, LRU_GROUP), F32), pltpu.VMEM((1, LRU_GROUP), F32),
                        pltpu.VMEM((T, LRU_GROUP), F32), pltpu.VMEM((T, LRU_GROUP), F32), pltpu.VMEM((T, LRU_GROUP), F32)],
        compiler_params=_cparams(("parallel", "arbitrary")),
    )(proj, proj, proj, hl, hl, a_fwd, mult_fwd, dy, w_out, conv_w, conv_b, wa_bd, wx_bd, b_a, b_x, lam)
    return res


def _block_diag(w):
    w4 = w.reshape(N_LRU_GROUPS, 4, LRU_BLOCK, 1, LRU_BLOCK)
    eye = jnp.eye(4, dtype=w.dtype).reshape(1, 4, 1, 4, 1)
    return (w4 * eye).reshape(N_LRU_GROUPS, LRU_GROUP, LRU_GROUP)


def _block_diag_extract(wbd):
    w5 = wbd.reshape(N_LRU_GROUPS, 4, LRU_BLOCK, 4, LRU_BLOCK)
    return jnp.stack([w5[:, a, :, a, :] for a in range(4)], axis=1).reshape(N_LRU_BLOCKS, LRU_BLOCK, LRU_BLOCK)


def _t5_bucket(dist):
    max_exact = NUM_BUCKETS // 2
    df = jnp.maximum(dist, 1).astype(jnp.float32)
    large = max_exact + (jnp.log(df / max_exact) / math.log(MAX_DISTANCE / max_exact)
                         * (NUM_BUCKETS - max_exact)).astype(jnp.int32)
    large = jnp.minimum(large, NUM_BUCKETS - 1)
    return jnp.where(dist < max_exact, dist, large)


def _band_offsets():
    qi = jnp.arange(SPAN)[:, None]
    kj = jnp.arange(2 * SPAN)[None, :]
    return qi + SPAN - kj


def _dil_buckets():
    off = _band_offsets()
    return jnp.stack([_t5_bucket(jnp.maximum(off, 0) * dil) for _, dil in DIL_GROUPS]).astype(jnp.int32)


def _dil_bias(rel_bias, buckets):
    def body(tbl_ref, bk_ref, o_ref):
        g = pl.program_id(0)
        qi = lax.broadcasted_iota(jnp.int32, (SPAN, 2 * SPAN), 0)
        kj = lax.broadcasted_iota(jnp.int32, (SPAN, 2 * SPAN), 1)
        off = qi + SPAN - kj
        valid = (off >= 0) & (off <= SPAN)
        bk = bk_ref[0]
        for h in range(DIL_HEADS):
            acc = jnp.zeros((SPAN, 2 * SPAN), F32)
            for b in range(NUM_BUCKETS):
                acc = jnp.where(bk == b, tbl_ref[b, g * DIL_HEADS + h], acc)
            o_ref[0, h] = jnp.where(valid, acc, NEG)

    return pl.pallas_call(
        body, name="dil_bias", grid=(3,),
        in_specs=[pl.BlockSpec(memory_space=pltpu.SMEM), pl.BlockSpec((1, SPAN, 2 * SPAN), lambda g: (g, 0, 0))],
        out_specs=pl.BlockSpec((1, DIL_HEADS, SPAN, 2 * SPAN), lambda g: (g, 0, 0, 0)),
        out_shape=jax.ShapeDtypeStruct((3, DIL_HEADS, SPAN, 2 * SPAN), F32),
        compiler_params=_cparams(("parallel",)),
    )(rel_bias, buckets)


def _dil_bias_bwd(dbias, buckets):
    def body(db_ref, bk_ref, o_ref):
        lane = lax.broadcasted_iota(jnp.int32, (1, 128), 1)
        rows = [jnp.zeros((1, 128), F32) for _ in range(NUM_BUCKETS)]
        for g in range(3):
            bk = bk_ref[g]
            for h in range(DIL_HEADS):
                d = db_ref[g, h]
                for b in range(NUM_BUCKETS):
                    tot = jnp.sum(_colsum(jnp.where(bk == b, d, 0.0)), axis=1, keepdims=True)
                    rows[b] = jnp.where(lane == g * DIL_HEADS + h, tot, rows[b])
        for b in range(NUM_BUCKETS):
            o_ref[b:b + 1, :] = rows[b]

    return pl.pallas_call(
        body, name="dil_bias_bwd",
        out_shape=jax.ShapeDtypeStruct((NUM_BUCKETS, 128), F32),
        compiler_params=_cparams(),
    )(dbias, buckets)


DIL_SUBBLOCKS = (8, 4, 1)


def _dil_layout(g, S):
    dil, m = DIL_GROUPS[g][1], DIL_SUBBLOCKS[g]
    sub = SPAN * dil
    col = [(C_QKV + t * 768 + g * 256) // 128 for t in range(3)]
    return dil, m, sub, S // (sub * m), col


def _residue_rows(b, r, dil):
    return pl.ds(b * SPAN * dil + r, SPAN, stride=dil) if dil > 1 else pl.ds(b * SPAN, SPAN)


def _for_residues(dil, fn):
    if dil <= 4:
        for r in range(dil):
            fn(r)
    else:
        lax.fori_loop(0, dil, lambda r, c: (fn(r), c)[1], 0, unroll=4)


def _pair_scores(qm, k2, bias, first_cols):
    s = _dot(qm, k2, "nt") * (DIL_HEAD_DIM ** -0.5) + bias
    kj = lax.broadcasted_iota(jnp.int32, s.shape, 1)
    return jnp.where(kj < first_cols, NEG, s)


def _dilated_fwd(proj, bias, g, *, S):
    dil, m, sub, nc, (qc, kc, vc) = _dil_layout(g, S)
    R = sub * m
    cur = lambda cb: pl.BlockSpec((R, 128), lambda p, i: (i, cb + p))
    prv = lambda cb: pl.BlockSpec((sub, 128), lambda p, i: (jnp.maximum(i * m - 1, 0), cb + p))
    out = pl.BlockSpec((R, 128), lambda p, i: (i, p))

    def body(q_ref, kp_ref, kc_ref, vp_ref, vc_ref, b_ref, o_ref, lse_ref):
        lane = lax.broadcasted_iota(jnp.int32, (SPAN, 128), 1)
        sels = (lane < DIL_HEAD_DIM, lane >= DIL_HEAD_DIM)
        for b in range(m):
            first_cols = jnp.where(pl.program_id(1) == 0, SPAN, 0) if b == 0 else 0

            def one(r, b=b, first_cols=first_cols):
                rows = _residue_rows(b, r, dil)
                before = (kc_ref, vc_ref, _residue_rows(b - 1, r, dil)) if b else (kp_ref, vp_ref, _residue_rows(0, r, dil))
                q2 = q_ref[rows, :]
                k2 = _mx(jnp.concatenate([before[0][before[2], :], kc_ref[rows, :]], axis=0))
                v2 = _mx(jnp.concatenate([before[1][before[2], :], vc_ref[rows, :]], axis=0))
                qq = jnp.concatenate([jnp.where(sels[0], q2, 0.0), jnp.where(sels[1], q2, 0.0)], axis=0)
                s = _pair_scores(qq, k2, b_ref[0, 0], first_cols)
                mx = jnp.max(s, axis=-1, keepdims=True)
                p = jnp.exp(s - mx)
                den = jnp.sum(p, axis=-1, keepdims=True)
                o = _dot(p, v2) / den
                st = mx + jnp.log(den)
                o_ref[rows, :] = jnp.where(sels[0], o[0:SPAN], o[SPAN:2 * SPAN])
                lse_ref[rows, :] = jnp.where(sels[0], st[0:SPAN], st[SPAN:2 * SPAN])

            _for_residues(dil, one)

    return pl.pallas_call(
        body, name=f"dil_fwd{g}", grid=(2, nc),
        in_specs=[cur(qc), prv(kc), cur(kc), prv(vc), cur(vc),
                  pl.BlockSpec((1, 1, 2 * SPAN, 2 * SPAN), lambda p, i: (g, p, 0, 0))],
        out_specs=[out, out],
        out_shape=[jax.ShapeDtypeStruct((S, 256), F32), jax.ShapeDtypeStruct((S, 256), F32)],
        compiler_params=_cparams(("parallel", "parallel")),
    )(proj, proj, proj, proj, proj, bias.reshape(3, 2, 2 * SPAN, 2 * SPAN))


def _dilated_bwd(proj, do, lse, delta, bias, g, *, S, into=None):
    dil, m, sub, nc, (qc, kc, vc) = _dil_layout(g, S)
    R = sub * m
    cl = lambda i: jnp.minimum(i, nc - 1)
    cur = lambda cb: pl.BlockSpec((R, 128), lambda p, i: (cl(i), cb + p))
    prv = lambda cb: pl.BlockSpec((sub, 128), lambda p, i: (jnp.maximum(cl(i) * m - 1, 0), cb + p))
    q_out = pl.BlockSpec((R, 128), lambda p, i: (cl(i), 2 * g + p))
    kv_out = pl.BlockSpec((R, 128), lambda p, i: (jnp.maximum(i - 1, 0), 2 * g + p))
    scale = DIL_HEAD_DIM ** -0.5
    n_into = 0 if into is None else 3

    def body(q_ref, kp_ref, kc_ref, vp_ref, vc_ref, do_ref, lse_ref, dl_ref, b_ref, *rest):
        dq_ref, dk_ref, dv_ref, db_ref, dq_s, kc_s, vc_s, kp_s, vp_s, kcar, vcar = rest[n_into:]
        i = pl.program_id(1)

        @pl.when(i == 0)
        def _():
            kcar[...] = jnp.zeros_like(kcar)
            vcar[...] = jnp.zeros_like(vcar)
            db_ref[...] = jnp.zeros_like(db_ref)

        @pl.when(i < nc)
        def _():
            lane = lax.broadcasted_iota(jnp.int32, (SPAN, 128), 1)
            sels = (lane < DIL_HEAD_DIM, lane >= DIL_HEAD_DIM)
            for b in range(m):
                first_cols = jnp.where(i == 0, SPAN, 0) if b == 0 else 0

                def one(r, b=b, first_cols=first_cols):
                    rows = _residue_rows(b, r, dil)
                    rows_before = _residue_rows(b - 1 if b else 0, r, dil)
                    k_before, v_before = (kc_ref, vc_ref) if b else (kp_ref, vp_ref)
                    q2, do2 = q_ref[rows, :], do_ref[rows, :]
                    k2 = _mx(jnp.concatenate([k_before[rows_before, :], kc_ref[rows, :]], axis=0))
                    v2 = _mx(jnp.concatenate([v_before[rows_before, :], vc_ref[rows, :]], axis=0))
                    lse_t, dl_t = lse_ref[rows, :], dl_ref[rows, :]
                    qq = _mx(jnp.concatenate([jnp.where(sels[0], q2, 0.0), jnp.where(sels[1], q2, 0.0)], axis=0))
                    dd = _mx(jnp.concatenate([jnp.where(sels[0], do2, 0.0), jnp.where(sels[1], do2, 0.0)], axis=0))
                    lse2 = jnp.concatenate([lse_t[:, 0:1], lse_t[:, DIL_HEAD_DIM:DIL_HEAD_DIM + 1]], axis=0)
                    dl2 = jnp.concatenate([dl_t[:, 0:1], dl_t[:, 1:2]], axis=0)
                    p = jnp.exp(_pair_scores(qq, k2, b_ref[0, 0], first_cols) - lse2)
                    ds = p * (_dot(dd, v2, "nt") - dl2)
                    db_ref[0] += ds
                    dqq = _dot(ds, k2) * scale
                    dq2 = jnp.where(sels[0], dqq[0:SPAN], dqq[SPAN:2 * SPAN])
                    dk2 = _dot(ds, qq, "tn") * scale
                    dv2 = _dot(p, dd, "tn")
                    dq_s[rows, :] = dq2
                    kc_s[rows, :] = dk2[SPAN:2 * SPAN]
                    vc_s[rows, :] = dv2[SPAN:2 * SPAN]
                    if b:
                        kc_s[rows_before, :] += dk2[0:SPAN]
                        vc_s[rows_before, :] += dv2[0:SPAN]
                    else:
                        kp_s[rows_before, :] = dk2[0:SPAN]
                        vp_s[rows_before, :] = dv2[0:SPAN]

                _for_residues(dil, one)
            dq_ref[...] = dq_s[...].astype(dq_ref.dtype)
            last = pl.ds((m - 1) * sub, sub)
            kcar[last, :] += kp_s[...]
            vcar[last, :] += vp_s[...]
            dk_ref[...] = kcar[...].astype(dk_ref.dtype)
            dv_ref[...] = vcar[...].astype(dv_ref.dtype)
            kcar[...] = kc_s[...]
            vcar[...] = vc_s[...]

        @pl.when(i == nc)
        def _():
            dk_ref[...] = kcar[...].astype(dk_ref.dtype)
            dv_ref[...] = vcar[...].astype(dv_ref.dtype)

    stat = pl.BlockSpec((R, 128), lambda p, i: (cl(i), p))
    big = jax.ShapeDtypeStruct((S, len(DIL_GROUPS) * 256), MXU_DTYPE)
    return pl.pallas_call(
        body, name=f"dil_bwd{g}", grid=(2, nc + 1),
        in_specs=[cur(qc), prv(kc), cur(kc), prv(vc), cur(vc), stat, stat, stat,
                  pl.BlockSpec((1, 1, 2 * SPAN, 2 * SPAN), lambda p, i: (g, p, 0, 0))]
        + [pl.BlockSpec(memory_space=pl.ANY)] * n_into,
        out_specs=[q_out, kv_out, kv_out, pl.BlockSpec((1, 2 * SPAN, 2 * SPAN), lambda p, i: (p, 0, 0))],
        out_shape=[big, big, big, jax.ShapeDtypeStruct((2, 2 * SPAN, 2 * SPAN), F32)],
        input_output_aliases={9 + j: j for j in range(n_into)},
        scratch_shapes=[pltpu.VMEM((R, 128), F32)] * 3 + [pltpu.VMEM((sub, 128), F32)] * 2 + [pltpu.VMEM((R, 128), F32)] * 2,
        compiler_params=_cparams(("parallel", "arbitrary")),
    )(proj, proj, proj, proj, proj, do, lse, delta, bias.reshape(3, 2, 2 * SPAN, 2 * SPAN), *(into or ()))


def _dilated_merge(os_, lses, *, S, bt=1024):
    tile = pl.BlockSpec((bt, 128), lambda i, p: (i, p))

    def body(o0, o1, o2, l0, l1, l2, o_ref, om_ref, lse_ref):
        a = [l0[...], l1[...], l2[...]]
        m = jnp.maximum(jnp.maximum(a[0], a[1]), a[2])
        ex = [jnp.exp(v - m) for v in a]
        tot = ex[0] + ex[1] + ex[2]
        acc = (ex[0] * o0[...] + ex[1] * o1[...] + ex[2] * o2[...]) / tot
        o_ref[...] = acc
        om_ref[...] = _mx(acc)
        lse_ref[...] = m + jnp.log(tot)

    return pl.pallas_call(
        body, name="dil_merge", grid=(S // bt, 2),
        in_specs=[tile] * 6, out_specs=[tile, tile, tile],
        out_shape=[jax.ShapeDtypeStruct((S, 256), F32), jax.ShapeDtypeStruct((S, 256), MXU_DTYPE),
                   jax.ShapeDtypeStruct((S, 256), F32)],
        compiler_params=_cparams(("parallel", "parallel")),
    )(*os_, *lses)


def _with_delta(do, o):
    lane = lax.broadcasted_iota(jnp.int32, (do.shape[0], 128), 1)
    stats = []
    for p in range(2):
        prod = do[:, 128 * p:128 * (p + 1)] * o[:, 128 * p:128 * (p + 1)]
        d0 = jnp.sum(jnp.where(lane < DIL_HEAD_DIM, prod, 0.0), axis=-1, keepdims=True)
        d1 = jnp.sum(jnp.where(lane >= DIL_HEAD_DIM, prod, 0.0), axis=-1, keepdims=True)
        stats.append(jnp.where(lane == 0, d0, jnp.where(lane == 1, d1, 0.0)))
    return do, jnp.concatenate(stats, axis=1)


MEM_T = 2048
QM_BLK = C_QM // MEM_HEAD_DIM


def _mem_attn_fwd(proj, kv, *, S):
    scale = MEM_HEAD_DIM ** -0.5

    def body(q_ref, k_ref, v_ref, o_ref, om_ref, lse_ref):
        s = _dot(q_ref[...], k_ref[...], "nt") * scale
        m = jnp.max(s, axis=-1, keepdims=True)
        p = jnp.exp(s - m)
        den = jnp.sum(p, axis=-1, keepdims=True)
        o = _dot(p, v_ref[...]) / den
        o_ref[...] = o
        om_ref[...] = _mx(o)
        lse_ref[0] = m + jnp.log(den)

    return pl.pallas_call(
        body, name="mem_attn_fwd", grid=(S // MEM_T, MEM_HEADS),
        in_specs=[pl.BlockSpec((MEM_T, MEM_HEAD_DIM), lambda i, h: (i, QM_BLK + h)),
                  pl.BlockSpec((N_MEM, MEM_HEAD_DIM), lambda i, h: (0, h)),
                  pl.BlockSpec((N_MEM, MEM_HEAD_DIM), lambda i, h: (0, MEM_HEADS + h))],
        out_specs=[pl.BlockSpec((MEM_T, MEM_HEAD_DIM), lambda i, h: (i, h)),
                   pl.BlockSpec((MEM_T, MEM_HEAD_DIM), lambda i, h: (i, h)),
                   pl.BlockSpec((1, MEM_T, 1), lambda i, h: (h, i, 0))],
        out_shape=[jax.ShapeDtypeStruct((S, MEM_WIDTH), F32), jax.ShapeDtypeStruct((S, MEM_WIDTH), MXU_DTYPE),
                   jax.ShapeDtypeStruct((MEM_HEADS, S, 1), F32)],
        compiler_params=_cparams(("parallel", "parallel")),
    )(proj, kv, kv)


def _mem_attn_bwd(proj, kv, om, lse, dy, w_out, *, S):
    scale = MEM_HEAD_DIM ** -0.5

    def body(q_ref, k_ref, v_ref, o_ref, lse_ref, dy_ref, wo_ref, dq_ref, dk_ref, dv_ref):
        @pl.when(pl.program_id(1) == 0)
        def _():
            dk_ref[...] = jnp.zeros_like(dk_ref)
            dv_ref[...] = jnp.zeros_like(dv_ref)

        qv, kv_, vv, dov = q_ref[...], k_ref[...], v_ref[...], _dot(dy_ref[...], wo_ref[...], "nt")
        p = jnp.exp(_dot(qv, kv_, "nt") * scale - lse_ref[0])
        delta = jnp.sum(dov * o_ref[...], axis=-1, keepdims=True)
        ds = p * (_dot(dov, vv, "nt") - delta)
        dq_ref[...] = (_dot(ds, kv_) * scale).astype(dq_ref.dtype)
        dk_ref[...] += _dot(ds, qv, "tn") * scale
        dv_ref[...] += _dot(p, dov, "tn")

    tile = pl.BlockSpec((MEM_T, MEM_HEAD_DIM), lambda h, i: (i, h))
    kvo = pl.BlockSpec((N_MEM, MEM_HEAD_DIM), lambda h, i: (0, h))
    return pl.pallas_call(
        body, name="mem_attn_bwd", grid=(MEM_HEADS, S // MEM_T),
        in_specs=[pl.BlockSpec((MEM_T, MEM_HEAD_DIM), lambda h, i: (i, QM_BLK + h)),
                  pl.BlockSpec((N_MEM, MEM_HEAD_DIM), lambda h, i: (0, h)),
                  pl.BlockSpec((N_MEM, MEM_HEAD_DIM), lambda h, i: (0, MEM_HEADS + h)),
                  tile, pl.BlockSpec((1, MEM_T, 1), lambda h, i: (h, i, 0)),
                  pl.BlockSpec((MEM_T, D_MODEL), lambda h, i: (i, 0)),
                  pl.BlockSpec((MEM_HEAD_DIM, D_MODEL), lambda h, i: (h, 0))],
        out_specs=[tile, kvo, kvo],
        out_shape=[jax.ShapeDtypeStruct((S, MEM_WIDTH), MXU_DTYPE), jax.ShapeDtypeStruct((N_MEM, MEM_WIDTH), F32),
                   jax.ShapeDtypeStruct((N_MEM, MEM_WIDTH), F32)],
        compiler_params=_cparams(("parallel", "arbitrary")),
    )(proj, kv, kv, om, lse, dy, w_out)


MIX_BM = 1024
MIX_BN = 256
GATES_BLK = C_GATES // MIX_BN


def _mix_specs(j_outer):
    ix = (lambda f: (lambda j, i: f(i, j))) if j_outer else (lambda f: f)
    act = lambda width: pl.BlockSpec((MIX_BM, width), ix(lambda i, j: (i, 0)))
    wgt = lambda width: pl.BlockSpec((width, MIX_BN), ix(lambda i, j: (0, j)))
    gate = lambda b: pl.BlockSpec((MIX_BM, MIX_BN), ix(lambda i, j: (i, GATES_BLK + 4 * b + j)))
    bias = lambda b: pl.BlockSpec((1, MIX_BN), ix(lambda i, j: (0, 4 * b + j)))
    tile = pl.BlockSpec((MIX_BM, MIX_BN), ix(lambda i, j: (i, j)))
    return act, wgt, gate, bias, tile


def _mix_fwd(z_lru, o_dil, om, w_lru, w_dil, w_mem, proj, b_gate, *, S):
    act, wgt, gate, bias, tile = _mix_specs(False)

    def body(zl, od, mo, wl, wd, wm, g0, g1, g2, b0, b1, b2, o_ref, t0, t1, t2):
        acc = None
        for a_ref, w_ref, g_ref, b_ref, t_ref in ((zl, wl, g0, b0, t0), (od, wd, g1, b1, t1), (mo, wm, g2, b2, t2)):
            gt = jax.nn.sigmoid(g_ref[...] + b_ref[...])
            t_ref[...] = gt.astype(t_ref.dtype)
            term = gt * _dot(a_ref[...], w_ref[...])
            acc = term if acc is None else acc + term
        o_ref[...] = acc.astype(o_ref.dtype)

    return pl.pallas_call(
        body, name="mix_fwd", grid=(S // MIX_BM, D_MODEL // MIX_BN),
        in_specs=[act(D_RNN), act(256), act(MEM_WIDTH), wgt(D_RNN), wgt(256), wgt(MEM_WIDTH),
                  gate(0), gate(1), gate(2), bias(0), bias(1), bias(2)],
        out_specs=[tile] * 4, out_shape=[jax.ShapeDtypeStruct((S, D_MODEL), MXU_DTYPE)] * 4,
        compiler_params=_cparams(("parallel", "parallel")),
    )(z_lru, o_dil, om, w_lru, w_dil, w_mem, proj, proj, proj, b_gate, b_gate, b_gate)


def _mix_bwd(dx1, w_out, z_lru, o_dil, om, w_lru, w_dil, w_mem, gates, *, S):
    act, wgt, _, _, tile = _mix_specs(False)
    n_j = D_MODEL // MIX_BN

    def body(dx, wo, zl, od, mo, wl, wd, wm, t0, t1, t2,
             dg0, dg1, dg2, dy0, dy1, dy2, db0, db1, db2):
        j = pl.program_id(1)

        @pl.when((pl.program_id(0) == 0) & (j == 0))
        def _():
            for r in (db0, db1, db2):
                r[...] = jnp.zeros_like(r)

        dmv = _dot(dx[...], wo[...], "nt")
        for act_ref, w_ref, t_ref, dg_ref, dy_ref, db_ref in (
                (zl, wl, t0, dg0, dy0, db0), (od, wd, t1, dg1, dy1, db1), (mo, wm, t2, dg2, dy2, db2)):
            y = _dot(act_ref[...], w_ref[...])
            gt = t_ref[...].astype(F32)
            dgate = dmv * y * gt * (1.0 - gt)
            dg_ref[...] = dgate.astype(dg_ref.dtype)
            dy_ref[...] = (dmv * gt).astype(dy_ref.dtype)
            db_ref[j] += _colsum(dgate)

    big = jax.ShapeDtypeStruct((S, D_MODEL), MXU_DTYPE)
    vec = jax.ShapeDtypeStruct((n_j, 1, MIX_BN), F32)
    vspec = pl.BlockSpec((n_j, 1, MIX_BN), lambda i, j: (0, 0, 0))
    res = pl.pallas_call(
        body, name="mix_bwd", grid=(S // MIX_BM, n_j),
        in_specs=[pl.BlockSpec((MIX_BM, D_MODEL), lambda i, j: (i, 0)), pl.BlockSpec((MIX_BN, D_MODEL), lambda i, j: (j, 0)),
                  act(D_RNN), act(256), act(MEM_WIDTH), wgt(D_RNN), wgt(256), wgt(MEM_WIDTH), tile, tile, tile],
        out_specs=[tile] * 6 + [vspec] * 3, out_shape=[big] * 6 + [vec] * 3,
        compiler_params=_cparams(("arbitrary", "arbitrary")),
    )(dx1, w_out, z_lru, o_dil, om, w_lru, w_dil, w_mem, *gates)
    return list(res[:6]) + [r.reshape(1, D_MODEL) for r in res[6:]]


def _adamw_math(w, g, m, v):
    m = ADAM_B1 * m + (1.0 - ADAM_B1) * g
    v = ADAM_B2 * v + (1.0 - ADAM_B2) * (g * g)
    m_hat = m / (1.0 - ADAM_B1 ** ADAM_STEP)
    v_hat = v / (1.0 - ADAM_B2 ** ADAM_STEP)
    delta = -ADAM_LR * (m_hat / (jnp.sqrt(v_hat) + ADAM_EPS) + ADAM_WD * w)
    return delta, m, v


def _adamw_landed(w, parts, slot, land, m, v, *, name, col_blk=0, prev=None):
    R = w.shape[0]
    n_parts, C = land.shape[0], land.shape[2]
    br = next(d for d in (256, 464, 128) if R % d == 0)
    tile = pl.BlockSpec((br, C), lambda i, s: (i, col_blk))
    part = pl.BlockSpec((None, br, C), lambda i, s: (s[0], i, 0))
    n_prev = 0 if prev is None else 4

    def body(s_ref, w_ref, o_ref, l_ref, m_ref, v_ref, *rest):
        g_ref, d_ref, nm_ref, nv_ref = rest[n_prev:]
        g = o_ref[...].astype(F32)
        for p in range(n_parts):
            g = g + l_ref[p].astype(F32)
        d, nm, nv = _adamw_math(w_ref[...], g, m_ref[...], v_ref[...])
        g_ref[...] = g
        d_ref[...] = d
        nm_ref[...] = nm
        nv_ref[...] = nv

    return pl.pallas_call(
        body, name=name,
        grid_spec=pltpu.PrefetchScalarGridSpec(
            num_scalar_prefetch=1, grid=(R // br,),
            in_specs=[tile, part, pl.BlockSpec((n_parts, br, C), lambda i, s: (0, i, 0)), tile, tile]
            + [pl.BlockSpec(memory_space=pl.ANY)] * n_prev,
            out_specs=[tile] * 4),
        out_shape=[jax.ShapeDtypeStruct(w.shape, F32)] * 4,
        input_output_aliases={6 + j: j for j in range(n_prev)},
        compiler_params=_cparams(("parallel",)),
    )(slot, w, parts, land, m, v, *(prev or ()))


def _adamw_plain(w, g, m, v, *, name):
    def body(w_ref, g_ref, m_ref, v_ref, d_ref, nm_ref, nv_ref):
        d, nm, nv = _adamw_math(w_ref[...], g_ref[...], m_ref[...], v_ref[...])
        d_ref[...] = d
        nm_ref[...] = nm
        nv_ref[...] = nv

    return pl.pallas_call(
        body, name=name, out_shape=[jax.ShapeDtypeStruct(w.shape, F32)] * 3, compiler_params=_cparams(),
    )(w, g, m, v)


def _my_pos():
    return lax.axis_index("x"), lax.axis_index("y"), lax.axis_index("c")


def _dev_index(p):
    return 4 * p[0] + 2 * p[1] + p[2]


def _peers(me):
    x, y, c = me
    out = []
    for k in range(1, 8):
        fx, fy, fc = (k >> 2) & 1, (k >> 1) & 1, k & 1
        out.append((k - 1, (1 - x if fx else x, 1 - y if fy else y, 1 - c if fc else c)))
    return out


HBM_SPEC = pl.BlockSpec(memory_space=pltpu.HBM)
SEM_SPEC = pl.BlockSpec(memory_space=pltpu.SEMAPHORE)
DATAFLOW_EFFECT = pltpu.SideEffectType.DATAFLOW_SIDE_EFFECTING


def _gather_refs(src, land, me, peer, k):
    return src, land.at[_dev_index(me)]


def _scatter_refs(src, land, me, peer, k):
    return src.at[_dev_index(peer)], land.at[k]


ALL_RELATIONS = tuple(range(7))
ONE_PER_CHIP = (0, 1, 3, 5)


def _push_start(srcs, land_shapes, refs_of, name, after=(), relations=ALL_RELATIONS):
    n, n_after = len(srcs), len(after)

    def body(*refs):
        ins, lands = refs[:n], refs[n:2 * n]
        send_sems, recv_sems, token = refs[2 * n + n_after], refs[2 * n + n_after + 1], refs[-1]
        me = _my_pos()
        for k, peer in _peers(me):
            if k not in relations:
                continue
            for a in range(n):
                src, dst = refs_of(ins[a], lands[a], me, peer, k)
                pltpu.make_async_remote_copy(src_ref=src, dst_ref=dst, send_sem=send_sems.at[7 * a + k],
                                             recv_sem=recv_sems.at[7 * a + k], device_id=peer, device_id_type=MESH).start()
        token[...] = jnp.zeros_like(token)

    lands = [lax.empty(shp, s.dtype) for shp, s in zip(land_shapes, srcs)]
    hbm = lambda a: pltpu.with_memory_space_constraint(a, pltpu.HBM)
    res = pl.pallas_call(
        body, name=name,
        out_shape=(pltpu.SemaphoreType.DMA((7 * n,)), pltpu.SemaphoreType.DMA((7 * n,)),
                   *[pltpu.HBM(s.shape, s.dtype) for s in srcs], *[pltpu.HBM(l.shape, l.dtype) for l in lands],
                   jax.ShapeDtypeStruct((8, 128), F32)),
        in_specs=[HBM_SPEC] * (2 * n) + [pl.BlockSpec(memory_space=pl.ANY)] * n_after,
        out_specs=(SEM_SPEC, SEM_SPEC, *[HBM_SPEC] * (2 * n), pl.BlockSpec(memory_space=pltpu.VMEM)),
        input_output_aliases={i: 2 + i for i in range(2 * n)},
        compiler_params=pltpu.CompilerParams(has_side_effects=DATAFLOW_EFFECT),
    )(*[hbm(s) for s in srcs], *[hbm(l) for l in lands], *after)
    return dict(sems=(res[0], res[1]), srcs=list(res[2:2 + n]), lands=list(res[2 + n:2 + 2 * n]), token=res[-1], n=n,
                refs_of=refs_of, name=name, relations=relations)


def _push_wait(started, after, with_sources=False):
    n, refs_of, relations = started["n"], started["refs_of"], started["relations"]
    after = list(after) if isinstance(after, (list, tuple)) else [after]

    def body(*refs):
        ins, lands = refs[:n], refs[n:2 * n]
        send_sems, recv_sems = refs[2 * n], refs[2 * n + 1]
        me = _my_pos()
        for k, peer in _peers(me):
            if k not in relations:
                continue
            for a in range(n):
                src, dst = refs_of(ins[a], lands[a], me, peer, k)
                cp = pltpu.make_async_remote_copy(src_ref=src, dst_ref=dst, send_sem=send_sems.at[7 * a + k],
                                                  recv_sem=recv_sems.at[7 * a + k], device_id=peer, device_id_type=MESH)
                cp.wait_send()
                cp.wait_recv()

    arrs = started["srcs"] + started["lands"]
    res = pl.pallas_call(
        body, name=started["name"].replace("start", "wait"),
        out_shape=tuple(pltpu.HBM(a.shape, a.dtype) for a in arrs),
        in_specs=[HBM_SPEC] * (2 * n) + [SEM_SPEC, SEM_SPEC] + [pl.BlockSpec(memory_space=pl.ANY)] * len(after),
        out_specs=tuple([HBM_SPEC] * (2 * n)),
        input_output_aliases={i: i for i in range(2 * n)},
        compiler_params=pltpu.CompilerParams(has_side_effects=DATAFLOW_EFFECT),
    )(*arrs, *started["sems"], *after)
    return (list(res[:n]), list(res[n:2 * n])) if with_sources else list(res[n:2 * n])


def _other_chips(x, y):
    return ((1 - x, y), (x, 1 - y), (1 - x, 1 - y))


def _forward_start(land, name, after=()):
    n_after = len(after)

    def body(*refs):
        land_ref, send_sems, recv_sems, token = refs[0], refs[1 + n_after], refs[2 + n_after], refs[-1]
        x, y, c = _my_pos()
        for j, (cx, cy) in enumerate(_other_chips(x, y)):
            blk = land_ref.at[_dev_index((cx, cy, c))]
            pltpu.make_async_remote_copy(src_ref=blk, dst_ref=blk, send_sem=send_sems.at[j], recv_sem=recv_sems.at[j],
                                         device_id=(x, y, 1 - c), device_id_type=MESH).start()
        token[...] = jnp.zeros_like(token)

    res = pl.pallas_call(
        body, name=name,
        out_shape=(pltpu.SemaphoreType.DMA((3,)), pltpu.SemaphoreType.DMA((3,)), pltpu.HBM(land.shape, land.dtype),
                   jax.ShapeDtypeStruct((8, 128), F32)),
        in_specs=[HBM_SPEC] + [pl.BlockSpec(memory_space=pl.ANY)] * n_after,
        out_specs=(SEM_SPEC, SEM_SPEC, HBM_SPEC, pl.BlockSpec(memory_space=pltpu.VMEM)),
        input_output_aliases={0: 2},
        compiler_params=pltpu.CompilerParams(has_side_effects=DATAFLOW_EFFECT),
    )(pltpu.with_memory_space_constraint(land, pltpu.HBM), *after)
    return dict(sems=(res[0], res[1]), land=res[2], token=res[3], name=name)


def _forward_wait(started, after):
    after = list(after) if isinstance(after, (list, tuple)) else [after]

    def body(land_ref, send_sems, recv_sems, *rest):
        x, y, c = _my_pos()
        for j, (cx, cy) in enumerate(_other_chips(x, y)):
            cp = pltpu.make_async_remote_copy(
                src_ref=land_ref.at[_dev_index((cx, cy, c))], dst_ref=land_ref.at[_dev_index((cx, cy, 1 - c))],
                send_sem=send_sems.at[j], recv_sem=recv_sems.at[j], device_id=(x, y, 1 - c), device_id_type=MESH)
            cp.wait_send()
            cp.wait_recv()

    land = started["land"]
    return pl.pallas_call(
        body, name=started["name"].replace("start", "wait"), out_shape=pltpu.HBM(land.shape, land.dtype),
        in_specs=[HBM_SPEC, SEM_SPEC, SEM_SPEC] + [pl.BlockSpec(memory_space=pl.ANY)] * len(after),
        out_specs=HBM_SPEC, input_output_aliases={0: 0},
        compiler_params=pltpu.CompilerParams(has_side_effects=DATAFLOW_EFFECT),
    )(land, *started["sems"], *after)


def _sum_slots(slots):
    def body(in_ref, out_ref):
        acc = in_ref[0]
        for d in range(1, N_DEV):
            acc = acc + in_ref[d]
        out_ref[...] = acc

    return pl.pallas_call(body, name="sum_small", out_shape=jax.ShapeDtypeStruct(slots.shape[1:], F32),
                          compiler_params=_cparams())(slots)


def _adamw_many(ws, gs, ms, vs):
    n = len(ws)

    def body(*refs):
        for i in range(n):
            w_ref, g_ref, m_ref, v_ref = (refs[j * n + i] for j in range(4))
            d_, nm, nv = _adamw_math(w_ref[...], g_ref[...], m_ref[...], v_ref[...])
            for j, val in enumerate((d_, nm, nv)):
                refs[(4 + j) * n + i][...] = val

    res = pl.pallas_call(body, name="adamw_small", out_shape=[jax.ShapeDtypeStruct(w_.shape, F32) for w_ in ws] * 3,
                         compiler_params=_cparams())(*ws, *gs, *ms, *vs)
    return [(res[i], res[n + i], res[2 * n + i]) for i in range(n)]


def _local_step(x, mem, tgt, W, P, late_weights, send_grads, reduce_small, tie0):
    S = x.shape[0]
    W = dict(W)
    h = _rmsnorm_fwd(x, P["g_mix"] + tie0, rows=S, name="norm_mix")
    mem_n = _rmsnorm_fwd(mem, P["g_mem"], rows=N_MEM, name="norm_mem")
    buckets = _dil_buckets()
    bias = _dil_bias(P["rel_bias"], buckets)
    wa_bd, wx_bd = _mx(_block_diag(P["w_rg_a"])), _mx(_block_diag(P["w_rg_x"]))
    W.update(late_weights("first", [h, mem_n, bias, wa_bd, wx_bd]))
    proj = _matmul(h, W["w_in_t"], M=S, N=D_IN, K=D_MODEL, mode="nt", bm=512, bn=D_IN // 2, bk=D_MODEL, name="mm_in",
                   j_outer=True, deps=[W["started"]])

    group_out = [_dilated_fwd(proj, bias, g, S=S) for g in range(len(DIL_GROUPS))]
    o_dil, o_dil_m, lse_dil = _dilated_merge([o for o, _ in group_out], [l for _, l in group_out], S=S)

    W.update(late_weights("branch", [o_dil]))
    lru_args = (W["conv_w"], P["conv_b"].reshape(1, -1), wa_bd, wx_bd, P["b_rg_a"].reshape(1, -1),
                P["b_rg_x"].reshape(1, -1), P["lru_lambda"].reshape(1, -1))
    hl, z_lru, a_lru, mult_lru = _lru_fwd(proj, *lru_args, S=S)
    kv = _matmul(mem_n, W["w_mem_kv"], M=N_MEM, N=2 * MEM_WIDTH, K=D_MODEL, mode="nn", bm=N_MEM, bn=512, bk=D_MODEL,
                 name="mm_kv")
    om, om_m, lse_mem = _mem_attn_fwd(proj, kv, S=S)
    b_gate = P["b_gate"].reshape(1, -1)
    merged, *gates = _mix_fwd(z_lru, o_dil_m, om_m, W["w_lru_out"], W["w_dil_out"], W["w_mem_out"], proj, b_gate, S=S)
    g_mlp, g_final, g_mix = (P[n].reshape(1, D_MODEL) for n in ("g_mlp", "g_final", "g_mix"))
    x1, hm = _matmul_rows(merged, W["w_out"], M=S, K=D_MODEL, mode="nn", bm=512, name="mm_out",
                          row_fn=_residual_then_norm, out_dtypes=(F32, MXU_DTYPE), tiles=[x], vecs=[g_mlp])
    W.update(late_weights("mlp", [hm]))

    def relu2(acc):
        rl = jnp.maximum(acc, 0.0)
        return rl * rl, rl

    act, relu_u = _matmul(hm, W["w_mlp_in_t"], M=S, N=D_FF, K=D_MODEL, mode="nt", bm=1024, bn=1024, bk=D_MODEL,
                          name="mm_mlp_in", out_dtypes=(MXU_DTYPE, MXU_DTYPE), epilogue=relu2, j_outer=True)
    dx2, dx2_m, loss, dg_final = _matmul_rows(
        act, W["w_mlp_out"], M=S, K=D_FF, mode="nn", bm=512, name="mm_mlp_out", row_fn=_residual_then_loss,
        out_dtypes=(F32, MXU_DTYPE), tiles=[x1, tgt], vecs=[g_final], acc_widths=(1, D_MODEL))

    G, Gs = {}, {}
    Gs["g_final"] = dg_final
    dw = dict(mode="tn", K=S, bk=S, out_dtypes=(MXU_DTYPE,))
    G["w_mlp_out"] = _matmul(act, dx2_m, M=D_FF, N=D_MODEL, bm=512, bn=D_MODEL, name="mm_dw_mlp_out",
                             parts=("rows", D_FF // N_DEV), **dw)
    du = _matmul(dx2_m, W["w_mlp_out"], M=S, N=D_FF, K=D_MODEL, mode="nt", bm=1024, bn=1024, bk=D_MODEL, name="mm_du",
                 out_dtypes=(MXU_DTYPE,), epilogue=lambda acc, rl: (acc * (2.0 * rl.astype(F32)),),
                 extras=[(relu_u, (0, 0))], j_outer=True)
    G["w_mlp_in"] = _matmul(hm, du, M=D_MODEL, N=D_FF, bm=D_MODEL, bn=512, name="mm_dw_mlp_in",
                            parts=("cols", D_FF // N_DEV), **dw)
    tie1 = send_grads({n: G.pop(n) for n in ("w_mlp_out", "w_mlp_in")})
    dx1, dx1_m, Gs["g_mlp"] = _matmul_rows(
        du, W["w_mlp_in_t"], M=S, K=D_FF, mode="nn", bm=512, name="mm_dhm", row_fn=_norm_bwd_then_residual(2),
        out_dtypes=(F32, MXU_DTYPE), tiles=[x1, dx2], vecs=[g_mlp], acc_widths=(D_MODEL,), deps=[tie1])
    G["w_out"] = _matmul(merged, dx1_m, M=D_MODEL, N=D_MODEL, bm=512, bn=D_MODEL, name="mm_dw_out",
                         parts=("rows", D_MODEL // N_DEV), **dw)
    (dg0, dg1, dg2, dy_lru, dy_dil, dy_mem, db0, db1, db2) = _mix_bwd(
        dx1_m, W["w_out"], z_lru, o_dil_m, om_m, W["w_lru_out"], W["w_dil_out"], W["w_mem_out"], gates, S=S)
    Gs["b_gate0"], Gs["b_gate1"], Gs["b_gate2"] = db0, db1, db2

    G["w_mem_out"] = _matmul(om_m, dy_mem, M=MEM_WIDTH, N=D_MODEL, bm=MEM_WIDTH, bn=D_MODEL, name="mm_dw_mem_out",
                             parts=("cols", D_MODEL // N_DEV), **dw)
    dqm, dk_mem, dv_mem = _mem_attn_bwd(proj, kv, om, lse_mem, dy_mem, W["w_mem_out"], S=S)
    dkv = jnp.concatenate([dk_mem, dv_mem], axis=1)
    G["w_mem_kv"] = _matmul(mem_n, dkv, M=D_MODEL, N=2 * MEM_WIDTH, K=N_MEM, mode="tn", bm=D_MODEL, bn=2 * MEM_WIDTH,
                            bk=N_MEM, name="mm_dw_kv", out_dtypes=(MXU_DTYPE,), parts=("rows", D_MODEL // N_DEV))
    dmem_n = _matmul(dkv, W["w_mem_kv"], M=N_MEM, N=D_MODEL, K=2 * MEM_WIDTH, mode="nt", bm=N_MEM, bn=D_MODEL,
                     bk=2 * MEM_WIDTH, name="mm_dmem")
    (Gs["g_mem"],) = _rmsnorm_bwd(mem, P["g_mem"], dmem_n, None, rows=N_MEM, name="norm_mem_bwd", dx_dtypes=())

    G["w_dil_out"] = _matmul(o_dil_m, dy_dil, M=256, N=D_MODEL, bm=256, bn=D_MODEL, name="mm_dw_dil_out",
                             parts=("cols", D_MODEL // N_DEV), **dw)
    do_dil, delta = _matmul(dy_dil, W["w_dil_out"], M=S, N=256, K=D_MODEL, mode="nt", bm=512, bn=256, bk=D_MODEL,
                            name="mm_do_dil", out_dtypes=(F32, F32), epilogue=_with_delta, extras=[(o_dil, (0, 0))])
    G["w_lru_out"] = _matmul(z_lru, dy_lru, M=D_RNN, N=D_MODEL, bm=D_RNN, bn=D_MODEL, name="mm_dw_lru_out",
                             parts=("cols", D_MODEL // N_DEV), **dw)
    tie2 = send_grads({n: G.pop(n) for n in ("w_out", "w_mem_out", "w_mem_kv", "w_dil_out", "w_lru_out")})
    bias = bias + tie2[0, 0]
    dqkv, dbias = None, []
    for g in range(len(DIL_GROUPS)):
        *dqkv, db_g = _dilated_bwd(proj, do_dil, lse_dil, delta, bias, g, S=S, into=dqkv)
        dbias.append(db_g)
    drel = _dil_bias_bwd(jnp.stack(dbias, axis=0).reshape(len(DIL_GROUPS), DIL_HEADS, SPAN, 2 * SPAN), buckets)
    Gs["rel_bias"] = drel

    dxl, dgl, dcw, dcb, dwa, dwx, dba, dbx, dlam = _lru_bwd(proj, hl, a_lru, mult_lru, dy_lru, W["w_lru_out"], *lru_args,
                                                            S=S)
    Gs["conv_w"], Gs["conv_b"] = dcw, dcb
    Gs["w_rg_a"], Gs["w_rg_x"] = _block_diag_extract(dwa), _block_diag_extract(dwx)
    Gs["b_rg_a"], Gs["b_rg_x"], Gs["lru_lambda"] = dba, dbx, dlam
    Gs["loss"] = loss

    dproj = [dxl, dgl] + dqkv + [dqm, dg0, dg1, dg2]
    tie = []
    for q in range(W_IN_PIECES):
        dw_q = None
        for half in range(2):
            dw_q = _dw_in_t_half(h, dproj, q, half, S=S, name=f"mm_dw_in_{q}_{half}", into=dw_q, deps=tie)
        tie = [send_grads({f"w_in_{q}": dw_q})]
    grad_x, Gs["g_mix"] = _matmul_rows(
        dproj, W["w_in_t"], M=S, K=D_IN, mode="nn", bm=256, name="mm_dh", row_fn=_norm_bwd_then_residual(1),
        out_dtypes=(F32,), tiles=[x, dx1], vecs=[g_mix], acc_widths=(D_MODEL,), deps=tie)
    return grad_x, reduce_small(Gs)


BIG = ("w_in", "w_lru_out", "w_dil_out", "w_mem_kv", "w_mem_out", "w_out", "w_mlp_in", "w_mlp_out")
W_IN_PIECES = 2
COL_SHARDED = ("w_lru_out", "w_dil_out", "w_mem_out", "w_mlp_in")
GATHERED_TRANSPOSED = ("w_mlp_in",)
SMALL = ("g_mix", "b_gate", "conv_b", "w_rg_a", "b_rg_a", "w_rg_x", "b_rg_x", "lru_lambda", "rel_bias", "g_mem",
         "g_mlp", "g_final")
WEIGHTS = ("g_mix", "w_in", "b_gate", "conv_w", "conv_b", "w_rg_a", "b_rg_a", "w_rg_x", "b_rg_x", "lru_lambda",
           "w_lru_out", "rel_bias", "w_dil_out", "g_mem", "w_mem_kv", "w_mem_out", "w_out", "g_mlp", "w_mlp_in",
           "w_mlp_out", "g_final")


def _gathered_to_full(name, gathered):
    if name in COL_SHARDED:
        n, r, c = gathered.shape
        return gathered.transpose(1, 0, 2).reshape(r, n * c)
    n, r, c = gathered.shape
    return gathered.reshape(n * r, c)


SMALL_GRADS = (("g_mix", (1, 1024)), ("b_gate0", (1, 1024)), ("b_gate1", (1, 1024)), ("b_gate2", (1, 1024)),
               ("conv_b", (1, 768)), ("w_rg_a", (12, 64, 64)), ("b_rg_a", (1, 768)), ("w_rg_x", (12, 64, 64)),
               ("b_rg_x", (1, 768)), ("lru_lambda", (1, 768)), ("rel_bias", (32, 128)), ("g_mem", (1, 1024)),
               ("g_mlp", (1, 1024)), ("g_final", (1, 1024)), ("conv_w", (4, 768)), ("loss", (1, 1)))


def _pack(parts):
    flat = jnp.concatenate([p.reshape(-1) for p in parts])
    return jnp.pad(flat, (0, (-flat.shape[0]) % 1024)).reshape(-1, 128)


def _unpack(pack, shapes):
    flat = pack.reshape(-1)
    out, off = [], 0
    for shp in shapes:
        size = math.prod(shp)
        out.append(flat[off:off + size].reshape(shp))
        off += size
    return out


def kernel(x, mem, g_mix, w_in, b_gate, conv_w, conv_b, w_rg_a, b_rg_a, w_rg_x, b_rg_x, lru_lambda, w_lru_out, rel_bias, w_dil_out, g_mem, w_mem_kv, w_mem_out, w_out, g_mlp, w_mlp_in, w_mlp_out, g_final, loss_target, m_g_mix, m_w_in, m_b_gate, m_conv_w, m_conv_b, m_w_rg_a, m_b_rg_a, m_w_rg_x, m_b_rg_x, m_lru_lambda, m_w_lru_out, m_rel_bias, m_w_dil_out, m_g_mem, m_w_mem_kv, m_w_mem_out, m_w_out, m_g_mlp, m_w_mlp_in, m_w_mlp_out, m_g_final, v_g_mix, v_w_in, v_b_gate, v_conv_w, v_conv_b, v_w_rg_a, v_b_rg_a, v_w_rg_x, v_b_rg_x, v_lru_lambda, v_w_lru_out, v_rel_bias, v_w_dil_out, v_g_mem, v_w_mem_kv, v_w_mem_out, v_w_out, v_g_mlp, v_w_mlp_in, v_w_mlp_out, v_g_final):
    w = dict(g_mix=g_mix, w_in=w_in, b_gate=b_gate, conv_w=conv_w, conv_b=conv_b, w_rg_a=w_rg_a, b_rg_a=b_rg_a,
             w_rg_x=w_rg_x, b_rg_x=b_rg_x, lru_lambda=lru_lambda, w_lru_out=w_lru_out, rel_bias=rel_bias,
             w_dil_out=w_dil_out, g_mem=g_mem, w_mem_kv=w_mem_kv, w_mem_out=w_mem_out, w_out=w_out, g_mlp=g_mlp,
             w_mlp_in=w_mlp_in, w_mlp_out=w_mlp_out, g_final=g_final)
    m = dict(g_mix=m_g_mix, w_in=m_w_in, b_gate=m_b_gate, conv_w=m_conv_w, conv_b=m_conv_b, w_rg_a=m_w_rg_a,
             b_rg_a=m_b_rg_a, w_rg_x=m_w_rg_x, b_rg_x=m_b_rg_x, lru_lambda=m_lru_lambda, w_lru_out=m_w_lru_out,
             rel_bias=m_rel_bias, w_dil_out=m_w_dil_out, g_mem=m_g_mem, w_mem_kv=m_w_mem_kv, w_mem_out=m_w_mem_out,
             w_out=m_w_out, g_mlp=m_g_mlp, w_mlp_in=m_w_mlp_in, w_mlp_out=m_w_mlp_out, g_final=m_g_final)
    v = dict(g_mix=v_g_mix, w_in=v_w_in, b_gate=v_b_gate, conv_w=v_conv_w, conv_b=v_conv_b, w_rg_a=v_w_rg_a,
             b_rg_a=v_b_rg_a, w_rg_x=v_w_rg_x, b_rg_x=v_b_rg_x, lru_lambda=v_lru_lambda, w_lru_out=v_w_lru_out,
             rel_bias=v_rel_bias, w_dil_out=v_w_dil_out, g_mem=v_g_mem, w_mem_kv=v_w_mem_kv, w_mem_out=v_w_mem_out,
             w_out=v_w_out, g_mlp=v_g_mlp, w_mlp_in=v_w_mlp_in, w_mlp_out=v_w_mlp_out, g_final=v_g_final)

    my_idx = _dev_index(_my_pos())

    w_in_shard = _mx(w["w_in"].T)
    first = _push_start([w_in_shard], [(N_DEV,) + w_in_shard.shape], _gather_refs, "gather_in_start",
                        relations=ONE_PER_CHIP)
    cw_cols = D_RNN // N_DEV
    conv_pad = jnp.zeros((64, D_MODEL), F32).at[:CONV_WIDTH, :cw_cols].set(w["conv_w"])
    late = {}
    P = {n: w[n] for n in SMALL}

    def start_late(order_after):
        for group, names in (("branch", ("w_mem_kv", "w_lru_out", "w_dil_out", "w_mem_out", "w_out", "conv_w")),
                             ("mlp", ("w_mlp_in", "w_mlp_out"))):
            f32, order_after = lax.optimization_barrier(({n: w[n] for n in names}, order_after))
            shards = [conv_pad if n == "conv_w" else _mx(f32[n].T if n in GATHERED_TRANSPOSED else f32[n])
                      for n in names]
            started = _push_start(shards, [(N_DEV,) + s.shape for s in shards], _gather_refs, f"gather_{group}_start",
                                  after=order_after)
            late[group] = (names, shards, started)
            order_after = [started["token"]]

    start_late([first["token"]])

    def late_weights(group, after):
        if group == "first":
            (land,) = _push_wait(first, after)
            forward = _forward_start(land, "forward_in_start")
            full = lax.dynamic_update_index_in_dim(_forward_wait(forward, forward["token"]), w_in_shard, my_idx, 0)
            return {"w_in_t": full.reshape(D_IN, D_MODEL), "started": late["mlp"][2]["token"]}
        names, shards, started = late[group]
        out = {}
        for n, land, own in zip(names, _push_wait(started, after), shards):
            full = lax.dynamic_update_index_in_dim(land, own, my_idx, 0)
            if n == "conv_w":
                out[n] = full[:, :CONV_WIDTH, :cw_cols].transpose(1, 0, 2).reshape(CONV_WIDTH, D_RNN)
            elif n in GATHERED_TRANSPOSED:
                out[n + "_t"] = full.reshape(-1, full.shape[2])
            else:
                out[n] = _gathered_to_full(n, full)
        return out

    sent, small = [], {}

    def send_grads(gs):
        names = list(gs)
        parts = [gs[n] for n in names]
        started = _push_start(parts, [(N_DEV - 1,) + p.shape[1:] for p in parts], _scatter_refs,
                              f"scatter{len(sent)}_start")
        sent.append((names, started))
        return started["token"]

    def reduce_small(gs):
        small["pack"] = _pack([gs[n] for n, _ in SMALL_GRADS])
        small["started"] = _push_start([small["pack"]], [(N_DEV,) + small["pack"].shape], _gather_refs, "small_start")
        return small["started"]["token"]

    grad_x, last_token = _local_step(x[0], mem[0], loss_target[0], {}, P, late_weights, send_grads, reduce_small,
                                     late["mlp"][2]["token"][0, 0])

    grads, deltas, new_m, new_v = {}, {}, {}, {}
    after = last_token
    my_slot = jnp.reshape(my_idx, (1,)).astype(jnp.int32)
    for names, started in sent[:-W_IN_PIECES]:
        for n, parts, land in zip(names, *_push_wait(started, after, with_sources=True)):
            grads[n], deltas[n], new_m[n], new_v[n] = _adamw_landed(w[n], parts, my_slot, land, m[n], v[n],
                                                                    name=f"adamw_{n}")
            after = deltas[n]
    prev = None
    for q, (names, started) in enumerate(sent[-W_IN_PIECES:]):
        (parts,), (land,) = _push_wait(started, after, with_sources=True)
        prev = _adamw_landed(w["w_in"].T, parts, my_slot, land, m["w_in"].T, v["w_in"].T, name=f"adamw_{names[0]}",
                             col_blk=q, prev=prev)
        after = prev[1]
    grads["w_in"], deltas["w_in"], new_m["w_in"], new_v["w_in"] = [t.T for t in prev]
    (small_land,) = _push_wait(small["started"], [after] + [deltas[n] for n in BIG if n != "w_in"])
    total = _sum_slots(lax.dynamic_update_index_in_dim(small_land, small["pack"], my_idx, 0))
    summed = dict(zip([n for n, _ in SMALL_GRADS], _unpack(total, [shp for _, shp in SMALL_GRADS])))
    summed["b_gate"] = jnp.concatenate([summed.pop(f"b_gate{b}") for b in range(3)], axis=1)
    summed["rel_bias"] = summed["rel_bias"][:, :3 * DIL_HEADS]
    for n in SMALL:
        grads[n] = summed[n].reshape(w[n].shape)
    small_updates = _adamw_many([w[n] for n in SMALL], [grads[n] for n in SMALL], [m[n] for n in SMALL],
                                [v[n] for n in SMALL])
    for n, (d_, nm_, nv_) in zip(SMALL, small_updates):
        deltas[n], new_m[n], new_v[n] = d_, nm_, nv_
    conv_w_sum, loss_sum = summed["conv_w"], summed["loss"]
    grads["conv_w"] = lax.dynamic_slice(conv_w_sum, (0, my_idx * cw_cols), (CONV_WIDTH, cw_cols))
    deltas["conv_w"], new_m["conv_w"], new_v["conv_w"] = _adamw_plain(
        w["conv_w"], grads["conv_w"], m["conv_w"], v["conv_w"], name="adamw_conv_w")

    return (loss_sum.reshape(()), grad_x[None], *[grads[n] for n in WEIGHTS], *[deltas[n] for n in WEIGHTS],
            *[new_m[n] for n in WEIGHTS], *[new_v[n] for n in WEIGHTS])
```

```python
import functools
import math

import jax
import jax.numpy as jnp
from jax import lax
from jax.experimental import pallas as pl
from jax.experimental.pallas import tpu as pltpu

F32 = jnp.float32
MXU_DTYPE = jnp.bfloat16
VMEM_LIMIT_BYTES = 56 * 1024 * 1024
N_DEV = 8

D_MODEL = 1024
N_MEM = 256
MEM_HEADS = 4
MEM_HEAD_DIM = 128
MEM_WIDTH = 512
D_RNN = 768
LRU_BLOCK = 64
N_LRU_BLOCKS = 12
LRU_GROUP = 256
N_LRU_GROUPS = 3
CONV_WIDTH = 4
LRU_C = 8.0
DIL_GROUPS = ((128, 1), (512, 4), (2048, 16))
SPAN = 128
DIL_HEADS = 4
DIL_HEAD_DIM = 64
NUM_BUCKETS = 32
MAX_DISTANCE = 2048
D_FF = 4096
D_IN = 7424
EPS = 1e-6
NEG = -1e30
C_XL, C_GATE, C_QKV, C_QM, C_GATES = 0, 768, 1536, 3840, 4352

ADAM_LR = 0.001
ADAM_B1 = 0.9
ADAM_B2 = 0.999
ADAM_EPS = 1e-08
ADAM_WD = 0.01
ADAM_STEP = 10

MESH = pl.DeviceIdType.MESH
GELU_K = math.sqrt(2.0 / math.pi)


def _cparams(sem=None):
    kw = dict(vmem_limit_bytes=VMEM_LIMIT_BYTES)
    if sem is not None:
        kw["dimension_semantics"] = sem
    return pltpu.CompilerParams(**kw)


def _mx(v):
    return v.astype(MXU_DTYPE)


def _dot(a, b, mode="nn"):
    dims = {"nn": (((1,), (0,)), ((), ())), "nt": (((1,), (1,)), ((), ())), "tn": (((0,), (0,)), ((), ()))}[mode]
    return lax.dot_general(_mx(a), _mx(b), dims, preferred_element_type=F32)


def _colsum(v):
    return jnp.sum(v, axis=0, keepdims=True)


def _matmul(a, b, *, M, N, K, mode, bm, bn, bk, name, out_dtypes=(F32,), epilogue=None, extras=(),
            a_off=(0, 0), b_off=(0, 0), j_outer=False, deps=(), parts=None):
    assert M % bm == 0 and N % bn == 0 and K % bk == 0, (name, M, N, K, bm, bn, bk)
    nm, nn, nk = M // bm, N // bn, K // bk

    def ij(f):
        if j_outer:
            return lambda j, i, k: f(i, j, k)
        return f

    if mode == "tn":
        a_spec = pl.BlockSpec((bk, bm), ij(lambda i, j, k: (k + a_off[0], i + a_off[1])))
    else:
        a_spec = pl.BlockSpec((bm, bk), ij(lambda i, j, k: (i + a_off[0], k + a_off[1])))
    if mode == "nt":
        b_spec = pl.BlockSpec((bn, bk), ij(lambda i, j, k: (j + b_off[0], k + b_off[1])))
    else:
        b_spec = pl.BlockSpec((bk, bn), ij(lambda i, j, k: (k + b_off[0], j + b_off[1])))
    ex_specs = [pl.BlockSpec((bm, bn), ij(functools.partial(lambda i, j, k, o: (i + o[0], j + o[1]), o=off)))
                for _, off in extras]
    if parts is None:
        out_dims = (M, N)
        out_spec = pl.BlockSpec((bm, bn), ij(lambda i, j, k: (i, j)))
    elif parts[0] == "rows":
        r = parts[1]
        assert bm % r == 0
        out_dims = (M // r, r, N)
        out_spec = pl.BlockSpec((bm // r, r, bn), ij(lambda i, j, k: (i, 0, j)))
    elif parts[0] == "rows_t":
        r = parts[1]
        assert bn % r == 0
        out_dims = (N // r, r, M)
        out_spec = pl.BlockSpec((bn // r, r, bm), ij(lambda i, j, k: (j, 0, i)))
    else:
        c = parts[1]
        assert bn % c == 0
        out_dims = (N // c, M, c)
        out_spec = pl.BlockSpec((bn // c, bm, c), ij(lambda i, j, k: (j, i, 0)))
    n_ex, n_out, n_dep = len(extras), len(out_dtypes), len(deps)

    def body(*refs):
        a_ref, b_ref = refs[0], refs[1]
        ex = refs[2:2 + n_ex]
        outs = refs[2 + n_ex + n_dep:2 + n_ex + n_dep + n_out]
        part = _dot(a_ref[...], b_ref[...], mode)

        def finish(acc):
            vals = epilogue(acc, *[e[...] for e in ex]) if epilogue is not None else (acc,)
            for o, v in zip(outs, vals):
                if parts is not None and parts[0] == "rows_t":
                    v = v.T
                v = v.astype(o.dtype)
                if parts is None:
                    o[...] = v
                elif parts[0] in ("rows", "rows_t"):
                    for ch in range(v.shape[0] // parts[1]):
                        o[ch] = v[ch * parts[1]:(ch + 1) * parts[1], :]
                else:
                    for ch in range(bn // parts[1]):
                        o[ch] = v[:, ch * parts[1]:(ch + 1) * parts[1]]

        if nk == 1:
            finish(part)
        else:
            acc_ref = refs[-1]
            k = pl.program_id(2)

            @pl.when(k == 0)
            def _():
                acc_ref[...] = part

            @pl.when(k > 0)
            def _():
                acc_ref[...] += part

            @pl.when(k == nk - 1)
            def _():
                finish(acc_ref[...])

    grid = (nn, nm, nk) if j_outer else (nm, nn, nk)
    res = pl.pallas_call(
        body, name=name, grid=grid,
        in_specs=[a_spec, b_spec] + ex_specs + [pl.BlockSpec(memory_space=pl.ANY)] * n_dep,
        out_specs=[out_spec] * n_out,
        out_shape=[jax.ShapeDtypeStruct(out_dims, dt) for dt in out_dtypes],
        scratch_shapes=[pltpu.VMEM((bm, bn), F32)] if nk > 1 else [],
        compiler_params=_cparams(("parallel", "parallel", "arbitrary")),
    )(a, b, *[e for e, _ in extras], *deps)
    return res[0] if n_out == 1 else res


ROW_SUBTILES = 2


def _matmul_rows(a, b, *, M, K, mode, bm, name, row_fn, out_dtypes, tiles=(), vecs=(), acc_widths=(), deps=()):
    N = D_MODEL
    assert M % bm == 0
    segs = list(a) if isinstance(a, (list, tuple)) else [a]
    widths = [s_.shape[1] for s_ in segs]
    assert sum(widths) == K and (len(segs) == 1 or mode == "nn")
    n_s, n_t, n_v, n_o, n_a, n_d = len(segs), len(tiles), len(vecs), len(out_dtypes), len(acc_widths), len(deps)
    row = pl.BlockSpec((bm, N), lambda i: (i, 0))
    b_shape = (K, N) if mode == "nn" else (N, K)

    def body(*refs):
        b_ref = refs[n_s]
        ins = refs[n_s + 1:n_s + 1 + n_t + n_v]
        outs = refs[n_s + 1 + n_t + n_v + n_d:n_s + 1 + n_t + n_v + n_d + n_o]
        accs = refs[n_s + 1 + n_t + n_v + n_d + n_o:]
        for o in accs:
            @pl.when(pl.program_id(0) == 0)
            def _(o=o):
                o[...] = jnp.zeros_like(o)

        for s_ in range(ROW_SUBTILES):
            rows = pl.ds(s_ * (bm // ROW_SUBTILES), bm // ROW_SUBTILES)
            if n_s == 1:
                acc = _dot(refs[0][rows, :], b_ref[...], mode)
            else:
                acc, k0 = None, 0
                for a_ref, w_ in zip(refs[:n_s], widths):
                    part = _dot(a_ref[rows, :], b_ref[k0:k0 + w_, :])
                    acc = part if acc is None else acc + part
                    k0 += w_
            tile_vals, partials = row_fn(acc, *[r[rows, :] for r in ins[:n_t]], *[r[...] for r in ins[n_t:]])
            for o, val in zip(outs, tile_vals):
                o[rows, :] = val.astype(o.dtype)
            for o, val in zip(accs, partials):
                o[...] += val

    res = pl.pallas_call(
        body, name=name, grid=(M // bm,),
        in_specs=[pl.BlockSpec((bm, w_), lambda i: (i, 0)) for w_ in widths] + [pl.BlockSpec(b_shape, lambda i: (0, 0))]
        + [row] * n_t + [pl.BlockSpec((1, N), lambda i: (0, 0))] * n_v + [pl.BlockSpec(memory_space=pl.ANY)] * n_d,
        out_specs=[row] * n_o + [pl.BlockSpec((1, w_), lambda i: (0, 0)) for w_ in acc_widths],
        out_shape=[jax.ShapeDtypeStruct((M, N), dt) for dt in out_dtypes]
        + [jax.ShapeDtypeStruct((1, w_), F32) for w_ in acc_widths],
        compiler_params=_cparams(("arbitrary",) if n_a else ("parallel",)),
    )(*segs, b, *tiles, *vecs, *deps)
    return res


def _dw_in_t_half(h, pieces, q, half, *, S, name, into=None, deps=(), bk=1024):
    half_w, cols = D_IN // 2, D_MODEL // W_IN_PIECES
    lo, hi = half * half_w, (half + 1) * half_w
    use, c0 = [], 0
    for p in pieces:
        w_ = p.shape[1]
        a0, a1 = max(lo, c0), min(hi, c0 + w_)
        if a1 > a0:
            use.append((p, a0 - c0, a1 - a0))
        c0 += w_
    n_p, n_into, n_d, nk = len(use), 0 if into is None else 1, len(deps), S // bk
    rows = D_IN // N_DEV

    def body(*refs):
        h_ref, p_refs = refs[0], refs[1:1 + n_p]
        o_ref, acc_ref = refs[1 + n_p + n_into + n_d], refs[-1]
        k = pl.program_id(0)
        dp = jnp.concatenate([r[:, s0:s0 + w_] for r, (_, s0, w_) in zip(p_refs, use)], axis=1)
        part = _dot(h_ref[...], dp, "tn")

        @pl.when(k == 0)
        def _():
            acc_ref[...] = part

        @pl.when(k > 0)
        def _():
            acc_ref[...] += part

        @pl.when(k == nk - 1)
        def _():
            vt = acc_ref[...].T.astype(o_ref.dtype)
            for ch in range(half_w // rows):
                o_ref[ch] = vt[ch * rows:(ch + 1) * rows, :]

    return pl.pallas_call(
        body, name=name, grid=(nk,),
        in_specs=[pl.BlockSpec((bk, cols), lambda k: (k, q))]
        + [pl.BlockSpec((bk, p.shape[1]), lambda k: (k, 0)) for p, _, _ in use]
        + [pl.BlockSpec(memory_space=pl.ANY)] * (n_into + n_d),
        out_specs=pl.BlockSpec((half_w // rows, rows, cols), lambda k: (half, 0, 0)),
        out_shape=jax.ShapeDtypeStruct((N_DEV, rows, cols), MXU_DTYPE),
        input_output_aliases={1 + n_p: 0} if n_into else {},
        scratch_shapes=[pltpu.VMEM((cols, half_w), F32)],
        compiler_params=_cparams(("arbitrary",)),
    )(h, *[p for p, _, _ in use], *([into] if n_into else []), *deps)


def _rmsnorm_fwd(x, g, *, rows, name, bt=512):
    bt = min(bt, rows)

    def body(x_ref, g_ref, o_ref):
        xv = x_ref[...]
        r = lax.rsqrt(jnp.mean(xv * xv, axis=-1, keepdims=True) + EPS)
        o_ref[...] = (xv * r * g_ref[...]).astype(o_ref.dtype)

    return pl.pallas_call(
        body, name=name, grid=(rows // bt,),
        in_specs=[pl.BlockSpec((bt, D_MODEL), lambda i: (i, 0)), pl.BlockSpec((1, D_MODEL), lambda i: (0, 0))],
        out_specs=pl.BlockSpec((bt, D_MODEL), lambda i: (i, 0)),
        out_shape=jax.ShapeDtypeStruct((rows, D_MODEL), MXU_DTYPE),
        compiler_params=_cparams(("parallel",)),
    )(x, g.reshape(1, D_MODEL))


def _rms_bwd_tile(xv, gv, dyv):
    r = lax.rsqrt(jnp.mean(xv * xv, axis=-1, keepdims=True) + EPS)
    w = dyv * gv
    dx = r * w - xv * (r * r * r) * jnp.mean(w * xv, axis=-1, keepdims=True)
    dg = _colsum(dyv * xv * r)
    return dx, dg


def _residual_then_norm(acc, x_t, g):
    x1 = x_t + acc
    r = lax.rsqrt(jnp.mean(x1 * x1, axis=-1, keepdims=True) + EPS)
    return (x1, x1 * r * g), ()


def _residual_then_loss(acc, x_t, tgt_t, g):
    x2 = x_t + acc
    r = lax.rsqrt(jnp.mean(x2 * x2, axis=-1, keepdims=True) + EPS)
    diff = x2 * r * g - tgt_t
    part = jnp.sum(jnp.mean(diff * diff, axis=-1, keepdims=True), axis=0, keepdims=True) * 0.5
    dx, dg = _rms_bwd_tile(x2, g, diff * (1.0 / D_MODEL))
    return (dx, dx), (part, dg)


def _norm_bwd_then_residual(n_out):
    def fn(acc, x_t, res_t, g):
        dx, dg = _rms_bwd_tile(x_t, g, acc)
        return (dx + res_t,) * n_out, (dg,)

    return fn


def _rmsnorm_bwd(x, g, dy, res, *, rows, name, bt=512, dx_dtypes=(F32,)):
    bt = min(bt, rows)
    has_res = res is not None

    def body(*refs):
        x_ref, g_ref, dy_ref = refs[:3]
        res_ref = refs[3] if has_res else None
        outs = refs[3 + int(has_res):]
        dx, dg = _rms_bwd_tile(x_ref[...], g_ref[...], dy_ref[...])
        if has_res:
            dx = dx + res_ref[...]
        dg_ref = outs[-1]

        @pl.when(pl.program_id(0) == 0)
        def _():
            dg_ref[...] = jnp.zeros_like(dg_ref)

        dg_ref[...] += dg
        for o in outs[:-1]:
            o[...] = dx.astype(o.dtype)

    row_spec = pl.BlockSpec((bt, D_MODEL), lambda i: (i, 0))
    vec_spec = pl.BlockSpec((1, D_MODEL), lambda i: (0, 0))
    ins = [x, g.reshape(1, D_MODEL), dy] + ([res] if has_res else [])
    return pl.pallas_call(
        body, name=name, grid=(rows // bt,),
        in_specs=[row_spec, vec_spec, row_spec] + ([row_spec] if has_res else []),
        out_specs=[row_spec] * len(dx_dtypes) + [vec_spec],
        out_shape=[jax.ShapeDtypeStruct((rows, D_MODEL), dt) for dt in dx_dtypes] + [jax.ShapeDtypeStruct((1, D_MODEL), F32)],
        compiler_params=_cparams(("arbitrary",)),
    )(*ins)


LRU_T = 512
SCAN_GROUPS = 4


def _gelu(x):
    t = jnp.tanh(GELU_K * (x + 0.044715 * x * x * x))
    return 0.5 * x * (1.0 + t), t


def _gelu_grad(x, t):
    return 0.5 * (1.0 + t) + 0.5 * x * (1.0 - t * t) * GELU_K * (1.0 + 3.0 * 0.044715 * x * x)


def _softplus_neg(lam):
    z = -lam
    u = jnp.exp(-jnp.abs(z))
    w = 1.0 + u
    l1p = jnp.where(w == 1.0, u, jnp.log(w) * u / jnp.where(w == 1.0, 1.0, w - 1.0))
    return jnp.maximum(z, 0.0) + l1p


def _shift_down(cur, prev8, k, row8):
    y = pltpu.roll(cur, k, 0)
    head = jnp.where(row8 < k, pltpu.roll(prev8, k, 0), y[0:8])
    return jnp.concatenate([head, y[8:]], axis=0)


def _shift_up(cur, next8, k, row8):
    n = cur.shape[0]
    y = pltpu.roll(cur, n - k, 0)
    tail = jnp.where(row8 >= 8 - k, pltpu.roll(next8, 8 - k, 0), y[n - 8:n])
    return jnp.concatenate([y[0:n - 8], tail], axis=0)


def _lru_gates(xl, p8, cw, cb, wa, wx, ba, bx, lam, row8, a_mult=None):
    sh = [xl] + [_shift_down(xl, p8, k, row8) for k in (1, 2, 3)]
    xc = cb + cw[3:4] * sh[0] + cw[2:3] * sh[1] + cw[1:2] * sh[2] + cw[0:1] * sh[3]
    r = jax.nn.sigmoid(_dot(xc, wa) + ba)
    i = jax.nn.sigmoid(_dot(xc, wx) + bx)
    sp = _softplus_neg(lam)
    if a_mult is None:
        la = -LRU_C * r * sp
        a = jnp.exp(la)
        mult = jnp.sqrt(jnp.tanh(-la) * (a * a + 1.0))
    else:
        a, mult = a_mult
    return dict(sh=sh, xc=xc, r=r, i=i, sp=sp, a=a, mult=mult)


def _lru_specs(n_t, reverse):
    T = LRU_T
    tt = (lambda t: n_t - 1 - t) if reverse else (lambda t: t)
    blk = lambda col0: pl.BlockSpec((T, LRU_GROUP), lambda g, t: (tt(t), col0 + g))
    prev8 = lambda col0: pl.BlockSpec((8, LRU_GROUP), lambda g, t: (jnp.maximum(tt(t) * (T // 8) - 1, 0), col0 + g))
    vec = lambda rows: pl.BlockSpec((rows, LRU_GROUP), lambda g, t: (0, g))
    wbd = pl.BlockSpec((1, LRU_GROUP, LRU_GROUP), lambda g, t: (g, 0, 0))
    return blk, prev8, vec, wbd


def _lru_fwd(proj, conv_w, conv_b, wa_bd, wx_bd, b_a, b_x, lam, *, S):
    T = LRU_T
    n_t = S // T
    blk, _, vec, wbd = _lru_specs(n_t, False)

    def body(xl_ref, gate_ref, cw_ref, cb_ref, wa_ref, wx_ref, ba_ref, bx_ref, lam_ref,
             hl_ref, z_ref, a_s, m_ref, prev8, hcar, b_s):
        @pl.when(pl.program_id(1) == 0)
        def _():
            prev8[...] = jnp.zeros_like(prev8)
            hcar[...] = jnp.zeros_like(hcar)

        row8 = lax.broadcasted_iota(jnp.int32, (8, LRU_GROUP), 0)
        xl = xl_ref[...]
        q = _lru_gates(xl, prev8[...], cw_ref[...], cb_ref[...], wa_ref[0], wx_ref[0], ba_ref[...], bx_ref[...],
                       lam_ref[...], row8)
        prev8[...] = xl[T - 8:T]
        a_s[...] = q["a"]
        m_ref[...] = q["mult"]
        b_s[...] = q["mult"] * q["i"] * q["xc"]

        def step(c, carry):
            local = []
            for u in range(SCAN_GROUPS):
                off = pl.multiple_of((c * SCAN_GROUPS + u) * 8, 8)
                A = a_s[pl.ds(off, 8), :]
                B = b_s[pl.ds(off, 8), :]
                for k in (1, 2, 4):
                    a_sh = jnp.where(row8 >= k, pltpu.roll(A, k, 0), 1.0)
                    b_sh = jnp.where(row8 >= k, pltpu.roll(B, k, 0), 0.0)
                    B = A * b_sh + B
                    A = A * a_sh
                local.append((off, A, B))
            for off, A, B in local:
                h = A * carry + B
                hl_ref[pl.ds(off, 8), :] = h
                carry = h[7:8, :]
            return carry

        hcar[...] = lax.fori_loop(0, T // (8 * SCAN_GROUPS), step, hcar[...])
        ge, _ = _gelu(gate_ref[...])
        z_ref[...] = (ge * hl_ref[...]).astype(z_ref.dtype)

    return pl.pallas_call(
        body, name="lru_fwd", grid=(N_LRU_GROUPS, n_t),
        in_specs=[blk(C_XL // LRU_GROUP), blk(C_GATE // LRU_GROUP), vec(4), vec(1), wbd, wbd, vec(1), vec(1), vec(1)],
        out_specs=[blk(0)] * 4,
        out_shape=[jax.ShapeDtypeStruct((S, D_RNN), F32), jax.ShapeDtypeStruct((S, D_RNN), MXU_DTYPE),
                   jax.ShapeDtypeStruct((S, D_RNN), F32), jax.ShapeDtypeStruct((S, D_RNN), F32)],
        scratch_shapes=[pltpu.VMEM((8, LRU_GROUP), F32), pltpu.VMEM((1, LRU_GROUP), F32), pltpu.VMEM((T, LRU_GROUP), F32)],
        compiler_params=_cparams(("parallel", "arbitrary")),
    )(proj, proj, conv_w, conv_b, wa_bd, wx_bd, b_a, b_x, lam)


def _lru_bwd(proj, hl, a_fwd, mult_fwd, dy, w_out, conv_w, conv_b, wa_bd, wx_bd, b_a, b_x, lam, *, S):
    T = LRU_T
    n_t = S // T
    blk, prev8s, vec, wbd = _lru_specs(n_t, True)

    def body(xl_ref, xlp_ref, gate_ref, hl_ref, hlp_ref, a_ref, m_ref, dy_ref, wo_ref, cw_ref, cb_ref, wa_ref, wx_ref,
             ba_ref, bx_ref, lam_ref, dxl_ref, dgate_ref, dcw_ref, dcb_ref, dwa_ref, dwx_ref, dba_ref, dbx_ref, dlam_ref,
             next8, gcar, c_s, b_s, l_s):
        t = pl.program_id(1)
        first_chunk = t == n_t - 1

        @pl.when(t == 0)
        def _():
            next8[...] = jnp.zeros_like(next8)
            gcar[...] = jnp.zeros_like(gcar)
            for ref in (dcw_ref, dcb_ref, dwa_ref, dwx_ref, dba_ref, dbx_ref, dlam_ref):
                ref[...] = jnp.zeros_like(ref)

        row8 = lax.broadcasted_iota(jnp.int32, (8, LRU_GROUP), 0)
        rowT = lax.broadcasted_iota(jnp.int32, (T, LRU_GROUP), 0)
        keep = jnp.where(first_chunk, 0.0, 1.0)
        xl = xl_ref[...]
        wa, wx, lam_v = wa_ref[0], wx_ref[0], lam_ref[...]
        q = _lru_gates(xl, xlp_ref[...] * keep, cw_ref[...], cb_ref[...], wa, wx, ba_ref[...], bx_ref[...], lam_v, row8,
                       a_mult=(a_ref[...], m_ref[...]))
        a, mult, r, i, xc, sp = q["a"], q["mult"], q["r"], q["i"], q["xc"], q["sp"]
        hl_v = hl_ref[...]
        dz_v = _dot(dy_ref[...], wo_ref[...], "nt")
        gate = gate_ref[...]
        ge, th = _gelu(gate)
        dgate_ref[...] = (dz_v * hl_v * _gelu_grad(gate, th)).astype(dgate_ref.dtype)

        c_s[...] = jnp.where(rowT == T - 1, 0.0, pltpu.roll(a, T - 1, 0))
        b_s[...] = dz_v * ge + jnp.where(rowT == T - 1, gcar[...], 0.0)

        def step(n, carry):
            local = []
            for u in range(SCAN_GROUPS):
                off = pl.multiple_of((T // 8 - 1 - (n * SCAN_GROUPS + u)) * 8, 8)
                C = c_s[pl.ds(off, 8), :]
                B = b_s[pl.ds(off, 8), :]
                for k in (1, 2, 4):
                    c_sh = jnp.where(row8 < 8 - k, pltpu.roll(C, 8 - k, 0), 1.0)
                    b_sh = jnp.where(row8 < 8 - k, pltpu.roll(B, 8 - k, 0), 0.0)
                    B = B + C * b_sh
                    C = C * c_sh
                local.append((off, C, B))
            for off, C, B in local:
                lam_t = B + C * carry
                l_s[pl.ds(off, 8), :] = lam_t
                carry = lam_t[0:1, :]
            return carry

        lax.fori_loop(0, T // (8 * SCAN_GROUPS), step, jnp.zeros((1, LRU_GROUP), F32))
        lmb = l_s[...]
        gcar[...] = a[0:1, :] * lmb[0:1, :]

        h_prev = _shift_down(hl_v, hlp_ref[...] * keep, 1, row8)
        da = lmb * h_prev
        dmult = lmb * i * xc
        di = lmb * mult * xc
        dxc = lmb * mult * i
        dla = da * a - dmult * (a * a) / mult
        dr = dla * (-LRU_C * sp)
        dlam_ref[...] += _colsum(dla * (-LRU_C * r)) * (-jax.nn.sigmoid(-lam_v))
        dpa = dr * r * (1.0 - r)
        dpx = di * i * (1.0 - i)
        dxc = dxc + _dot(dpa, wa, "nt") + _dot(dpx, wx, "nt")
        dwa_ref[0] += _dot(xc, dpa, "tn")
        dwx_ref[0] += _dot(xc, dpx, "tn")
        dba_ref[...] += _colsum(dpa)
        dbx_ref[...] += _colsum(dpx)
        dcb_ref[...] += _colsum(dxc)
        cw = cw_ref[...]
        n8 = next8[...]
        dxl = cw[3:4] * dxc
        for k in (1, 2, 3):
            dxl = dxl + cw[3 - k:4 - k] * _shift_up(dxc, n8, k, row8)
        for k in range(4):
            dcw_ref[3 - k:4 - k, :] += _colsum(dxc * q["sh"][k])
        next8[...] = dxc[0:8]
        dxl_ref[...] = dxl.astype(dxl_ref.dtype)

    res = pl.pallas_call(
        body, name="lru_bwd", grid=(N_LRU_GROUPS, n_t),
        in_specs=[blk(C_XL // LRU_GROUP), prev8s(C_XL // LRU_GROUP), blk(C_GATE // LRU_GROUP), blk(0), prev8s(0), blk(0),
                  blk(0), pl.BlockSpec((T, D_MODEL), lambda g, t: (n_t - 1 - t, 0)),
                  pl.BlockSpec((LRU_GROUP, D_MODEL), lambda g, t: (g, 0)), vec(4), vec(1), wbd, wbd, vec(1), vec(1), vec(1)],
        out_specs=[blk(0), blk(0), vec(4), vec(1), wbd, wbd, vec(1), vec(1), vec(1)],
        out_shape=[jax.ShapeDtypeStruct((S, D_RNN), MXU_DTYPE), jax.ShapeDtypeStruct((S, D_RNN), MXU_DTYPE),
                   jax.ShapeDtypeStruct((4, D_RNN), F32), jax.ShapeDtypeStruct((1, D_RNN), F32),
                   jax.ShapeDtypeStruct((N_LRU_GROUPS, LRU_GROUP, LRU_GROUP), F32),
                   jax.ShapeDtypeStruct((N_LRU_GROUPS, LRU_GROUP, LRU_GROUP), F32),
                   jax.ShapeDtypeStruct((1, D_RNN), F32), jax.ShapeDtypeStruct((1, D_RNN), F32),
                   jax.ShapeDtypeStruct((1, D_RNN), F32)],
        scratch_shapes=[pltpu.VMEM((8, LRU_GROUP), F32), pltpu.VMEM((1, LRU_GROUP), F32),
                        pltpu.VMEM((T, LRU_GROUP), F32), pltpu.VMEM((T, LRU_GROUP), F32), pltpu.VMEM((T, LRU_GROUP), F32)],
        compiler_params=_cparams(("parallel", "arbitrary")),
    )(proj, proj, proj, hl, hl, a_fwd, mult_fwd, dy, w_out, conv_w, conv_b, wa_bd, wx_bd, b_a, b_x, lam)
    return res


def _block_diag(w):
    w4 = w.reshape(N_LRU_GROUPS, 4, LRU_BLOCK, 1, LRU_BLOCK)
    eye = jnp.eye(4, dtype=w.dtype).reshape(1, 4, 1, 4, 1)
    return (w4 * eye).reshape(N_LRU_GROUPS, LRU_GROUP, LRU_GROUP)


def _block_diag_extract(wbd):
    w5 = wbd.reshape(N_LRU_GROUPS, 4, LRU_BLOCK, 4, LRU_BLOCK)
    return jnp.stack([w5[:, a, :, a, :] for a in range(4)], axis=1).reshape(N_LRU_BLOCKS, LRU_BLOCK, LRU_BLOCK)


def _t5_bucket(dist):
    max_exact = NUM_BUCKETS // 2
    df = jnp.maximum(dist, 1).astype(jnp.float32)
    large = max_exact + (jnp.log(df / max_exact) / math.log(MAX_DISTANCE / max_exact)
                         * (NUM_BUCKETS - max_exact)).astype(jnp.int32)
    large = jnp.minimum(large, NUM_BUCKETS - 1)
    return jnp.where(dist < max_exact, dist, large)


def _band_offsets():
    qi = jnp.arange(SPAN)[:, None]
    kj = jnp.arange(2 * SPAN)[None, :]
    return qi + SPAN - kj


def _dil_buckets():
    off = _band_offsets()
    return jnp.stack([_t5_bucket(jnp.maximum(off, 0) * dil) for _, dil in DIL_GROUPS]).astype(jnp.int32)


def _dil_bias(rel_bias, buckets):
    def body(tbl_ref, bk_ref, o_ref):
        g = pl.program_id(0)
        qi = lax.broadcasted_iota(jnp.int32, (SPAN, 2 * SPAN), 0)
        kj = lax.broadcasted_iota(jnp.int32, (SPAN, 2 * SPAN), 1)
        off = qi + SPAN - kj
        valid = (off >= 0) & (off <= SPAN)
        bk = bk_ref[0]
        for h in range(DIL_HEADS):
            acc = jnp.zeros((SPAN, 2 * SPAN), F32)
            for b in range(NUM_BUCKETS):
                acc = jnp.where(bk == b, tbl_ref[b, g * DIL_HEADS + h], acc)
            o_ref[0, h] = jnp.where(valid, acc, NEG)

    return pl.pallas_call(
        body, name="dil_bias", grid=(3,),
        in_specs=[pl.BlockSpec(memory_space=pltpu.SMEM), pl.BlockSpec((1, SPAN, 2 * SPAN), lambda g: (g, 0, 0))],
        out_specs=pl.BlockSpec((1, DIL_HEADS, SPAN, 2 * SPAN), lambda g: (g, 0, 0, 0)),
        out_shape=jax.ShapeDtypeStruct((3, DIL_HEADS, SPAN, 2 * SPAN), F32),
        compiler_params=_cparams(("parallel",)),
    )(rel_bias, buckets)


def _dil_bias_bwd(dbias, buckets):
    def body(db_ref, bk_ref, o_ref):
        lane = lax.broadcasted_iota(jnp.int32, (1, 128), 1)
        rows = [jnp.zeros((1, 128), F32) for _ in range(NUM_BUCKETS)]
        for g in range(3):
            bk = bk_ref[g]
            for h in range(DIL_HEADS):
                d = db_ref[g, h]
                for b in range(NUM_BUCKETS):
                    tot = jnp.sum(_colsum(jnp.where(bk == b, d, 0.0)), axis=1, keepdims=True)
                    rows[b] = jnp.where(lane == g * DIL_HEADS + h, tot, rows[b])
        for b in range(NUM_BUCKETS):
            o_ref[b:b + 1, :] = rows[b]

    return pl.pallas_call(
        body, name="dil_bias_bwd",
        out_shape=jax.ShapeDtypeStruct((NUM_BUCKETS, 128), F32),
        compiler_params=_cparams(),
    )(dbias, buckets)


DIL_SUBBLOCKS = (8, 4, 1)


def _dil_layout(g, S):
    dil, m = DIL_GROUPS[g][1], DIL_SUBBLOCKS[g]
    sub = SPAN * dil
    col = [(C_QKV + t * 768 + g * 256) // 128 for t in range(3)]
    return dil, m, sub, S // (sub * m), col


def _residue_rows(b, r, dil):
    return pl.ds(b * SPAN * dil + r, SPAN, stride=dil) if dil > 1 else pl.ds(b * SPAN, SPAN)


def _for_residues(dil, fn):
    if dil <= 4:
        for r in range(dil):
            fn(r)
    else:
        lax.fori_loop(0, dil, lambda r, c: (fn(r), c)[1], 0, unroll=4)


def _pair_scores(qm, k2, bias, first_cols):
    s = _dot(qm, k2, "nt") * (DIL_HEAD_DIM ** -0.5) + bias
    kj = lax.broadcasted_iota(jnp.int32, s.shape, 1)
    return jnp.where(kj < first_cols, NEG, s)


def _dilated_fwd(proj, bias, g, *, S):
    dil, m, sub, nc, (qc, kc, vc) = _dil_layout(g, S)
    R = sub * m
    cur = lambda cb: pl.BlockSpec((R, 128), lambda p, i: (i, cb + p))
    prv = lambda cb: pl.BlockSpec((sub, 128), lambda p, i: (jnp.maximum(i * m - 1, 0), cb + p))
    out = pl.BlockSpec((R, 128), lambda p, i: (i, p))

    def body(q_ref, kp_ref, kc_ref, vp_ref, vc_ref, b_ref, o_ref, lse_ref):
        lane = lax.broadcasted_iota(jnp.int32, (SPAN, 128), 1)
        sels = (lane < DIL_HEAD_DIM, lane >= DIL_HEAD_DIM)
        for b in range(m):
            first_cols = jnp.where(pl.program_id(1) == 0, SPAN, 0) if b == 0 else 0

            def one(r, b=b, first_cols=first_cols):
                rows = _residue_rows(b, r, dil)
                before = (kc_ref, vc_ref, _residue_rows(b - 1, r, dil)) if b else (kp_ref, vp_ref, _residue_rows(0, r, dil))
                q2 = q_ref[rows, :]
                k2 = _mx(jnp.concatenate([before[0][before[2], :], kc_ref[rows, :]], axis=0))
                v2 = _mx(jnp.concatenate([before[1][before[2], :], vc_ref[rows, :]], axis=0))
                qq = jnp.concatenate([jnp.where(sels[0], q2, 0.0), jnp.where(sels[1], q2, 0.0)], axis=0)
                s = _pair_scores(qq, k2, b_ref[0, 0], first_cols)
                mx = jnp.max(s, axis=-1, keepdims=True)
                p = jnp.exp(s - mx)
                den = jnp.sum(p, axis=-1, keepdims=True)
                o = _dot(p, v2) / den
                st = mx + jnp.log(den)
                o_ref[rows, :] = jnp.where(sels[0], o[0:SPAN], o[SPAN:2 * SPAN])
                lse_ref[rows, :] = jnp.where(sels[0], st[0:SPAN], st[SPAN:2 * SPAN])

            _for_residues(dil, one)

    return pl.pallas_call(
        body, name=f"dil_fwd{g}", grid=(2, nc),
        in_specs=[cur(qc), prv(kc), cur(kc), prv(vc), cur(vc),
                  pl.BlockSpec((1, 1, 2 * SPAN, 2 * SPAN), lambda p, i: (g, p, 0, 0))],
        out_specs=[out, out],
        out_shape=[jax.ShapeDtypeStruct((S, 256), F32), jax.ShapeDtypeStruct((S, 256), F32)],
        compiler_params=_cparams(("parallel", "parallel")),
    )(proj, proj, proj, proj, proj, bias.reshape(3, 2, 2 * SPAN, 2 * SPAN))


def _dilated_bwd(proj, do, lse, delta, bias, g, *, S, into=None):
    dil, m, sub, nc, (qc, kc, vc) = _dil_layout(g, S)
    R = sub * m
    cl = lambda i: jnp.minimum(i, nc - 1)
    cur = lambda cb: pl.BlockSpec((R, 128), lambda p, i: (cl(i), cb + p))
    prv = lambda cb: pl.BlockSpec((sub, 128), lambda p, i: (jnp.maximum(cl(i) * m - 1, 0), cb + p))
    q_out = pl.BlockSpec((R, 128), lambda p, i: (cl(i), 2 * g + p))
    kv_out = pl.BlockSpec((R, 128), lambda p, i: (jnp.maximum(i - 1, 0), 2 * g + p))
    scale = DIL_HEAD_DIM ** -0.5
    n_into = 0 if into is None else 3

    def body(q_ref, kp_ref, kc_ref, vp_ref, vc_ref, do_ref, lse_ref, dl_ref, b_ref, *rest):
        dq_ref, dk_ref, dv_ref, db_ref, dq_s, kc_s, vc_s, kp_s, vp_s, kcar, vcar = rest[n_into:]
        i = pl.program_id(1)

        @pl.when(i == 0)
        def _():
            kcar[...] = jnp.zeros_like(kcar)
            vcar[...] = jnp.zeros_like(vcar)
            db_ref[...] = jnp.zeros_like(db_ref)

        @pl.when(i < nc)
        def _():
            lane = lax.broadcasted_iota(jnp.int32, (SPAN, 128), 1)
            sels = (lane < DIL_HEAD_DIM, lane >= DIL_HEAD_DIM)
            for b in range(m):
                first_cols = jnp.where(i == 0, SPAN, 0) if b == 0 else 0

                def one(r, b=b, first_cols=first_cols):
                    rows = _residue_rows(b, r, dil)
                    rows_before = _residue_rows(b - 1 if b else 0, r, dil)
                    k_before, v_before = (kc_ref, vc_ref) if b else (kp_ref, vp_ref)
                    q2, do2 = q_ref[rows, :], do_ref[rows, :]
                    k2 = _mx(jnp.concatenate([k_before[rows_before, :], kc_ref[rows, :]], axis=0))
                    v2 = _mx(jnp.concatenate([v_before[rows_before, :], vc_ref[rows, :]], axis=0))
                    lse_t, dl_t = lse_ref[rows, :], dl_ref[rows, :]
                    qq = _mx(jnp.concatenate([jnp.where(sels[0], q2, 0.0), jnp.where(sels[1], q2, 0.0)], axis=0))
                    dd = _mx(jnp.concatenate([jnp.where(sels[0], do2, 0.0), jnp.where(sels[1], do2, 0.0)], axis=0))
                    lse2 = jnp.concatenate([lse_t[:, 0:1], lse_t[:, DIL_HEAD_DIM:DIL_HEAD_DIM + 1]], axis=0)
                    dl2 = jnp.concatenate([dl_t[:, 0:1], dl_t[:, 1:2]], axis=0)
                    p = jnp.exp(_pair_scores(qq, k2, b_ref[0, 0], first_cols) - lse2)
                    ds = p * (_dot(dd, v2, "nt") - dl2)
                    db_ref[0] += ds
                    dqq = _dot(ds, k2) * scale
                    dq2 = jnp.where(sels[0], dqq[0:SPAN], dqq[SPAN:2 * SPAN])
                    dk2 = _dot(ds, qq, "tn") * scale
                    dv2 = _dot(p, dd, "tn")
                    dq_s[rows, :] = dq2
                    kc_s[rows, :] = dk2[SPAN:2 * SPAN]
                    vc_s[rows, :] = dv2[SPAN:2 * SPAN]
                    if b:
                        kc_s[rows_before, :] += dk2[0:SPAN]
                        vc_s[rows_before, :] += dv2[0:SPAN]
                    else:
                        kp_s[rows_before, :] = dk2[0:SPAN]
                        vp_s[rows_before, :] = dv2[0:SPAN]

                _for_residues(dil, one)
            dq_ref[...] = dq_s[...].astype(dq_ref.dtype)
            last = pl.ds((m - 1) * sub, sub)
            kcar[last, :] += kp_s[...]
            vcar[last, :] += vp_s[...]
            dk_ref[...] = kcar[...].astype(dk_ref.dtype)
            dv_ref[...] = vcar[...].astype(dv_ref.dtype)
            kcar[...] = kc_s[...]
            vcar[...] = vc_s[...]

        @pl.when(i == nc)
        def _():
            dk_ref[...] = kcar[...].astype(dk_ref.dtype)
            dv_ref[...] = vcar[...].astype(dv_ref.dtype)

    stat = pl.BlockSpec((R, 128), lambda p, i: (cl(i), p))
    big = jax.ShapeDtypeStruct((S, len(DIL_GROUPS) * 256), MXU_DTYPE)
    return pl.pallas_call(
        body, name=f"dil_bwd{g}", grid=(2, nc + 1),
        in_specs=[cur(qc), prv(kc), cur(kc), prv(vc), cur(vc), stat, stat, stat,
                  pl.BlockSpec((1, 1, 2 * SPAN, 2 * SPAN), lambda p, i: (g, p, 0, 0))]
        + [pl.BlockSpec(memory_space=pl.ANY)] * n_into,
        out_specs=[q_out, kv_out, kv_out, pl.BlockSpec((1, 2 * SPAN, 2 * SPAN), lambda p, i: (p, 0, 0))],
        out_shape=[big, big, big, jax.ShapeDtypeStruct((2, 2 * SPAN, 2 * SPAN), F32)],
        input_output_aliases={9 + j: j for j in range(n_into)},
        scratch_shapes=[pltpu.VMEM((R, 128), F32)] * 3 + [pltpu.VMEM((sub, 128), F32)] * 2 + [pltpu.VMEM((R, 128), F32)] * 2,
        compiler_params=_cparams(("parallel", "arbitrary")),
    )(proj, proj, proj, proj, proj, do, lse, delta, bias.reshape(3, 2, 2 * SPAN, 2 * SPAN), *(into or ()))


def _dilated_merge(os_, lses, *, S, bt=1024):
    tile = pl.BlockSpec((bt, 128), lambda i, p: (i, p))

    def body(o0, o1, o2, l0, l1, l2, o_ref, om_ref, lse_ref):
        a = [l0[...], l1[...], l2[...]]
        m = jnp.maximum(jnp.maximum(a[0], a[1]), a[2])
        ex = [jnp.exp(v - m) for v in a]
        tot = ex[0] + ex[1] + ex[2]
        acc = (ex[0] * o0[...] + ex[1] * o1[...] + ex[2] * o2[...]) / tot
        o_ref[...] = acc
        om_ref[...] = _mx(acc)
        lse_ref[...] = m + jnp.log(tot)

    return pl.pallas_call(
        body, name="dil_merge", grid=(S // bt, 2),
        in_specs=[tile] * 6, out_specs=[tile, tile, tile],
        out_shape=[jax.ShapeDtypeStruct((S, 256), F32), jax.ShapeDtypeStruct((S, 256), MXU_DTYPE),
                   jax.ShapeDtypeStruct((S, 256), F32)],
        compiler_params=_cparams(("parallel", "parallel")),
    )(*os_, *lses)


def _with_delta(do, o):
    lane = lax.broadcasted_iota(jnp.int32, (do.shape[0], 128), 1)
    stats = []
    for p in range(2):
        prod = do[:, 128 * p:128 * (p + 1)] * o[:, 128 * p:128 * (p + 1)]
        d0 = jnp.sum(jnp.where(lane < DIL_HEAD_DIM, prod, 0.0), axis=-1, keepdims=True)
        d1 = jnp.sum(jnp.where(lane >= DIL_HEAD_DIM, prod, 0.0), axis=-1, keepdims=True)
        stats.append(jnp.where(lane == 0, d0, jnp.where(lane == 1, d1, 0.0)))
    return do, jnp.concatenate(stats, axis=1)


MEM_T = 2048
QM_BLK = C_QM // MEM_HEAD_DIM


def _mem_attn_fwd(proj, kv, *, S):
    scale = MEM_HEAD_DIM ** -0.5

    def body(q_ref, k_ref, v_ref, o_ref, om_ref, lse_ref):
        s = _dot(q_ref[...], k_ref[...], "nt") * scale
        m = jnp.max(s, axis=-1, keepdims=True)
        p = jnp.exp(s - m)
        den = jnp.sum(p, axis=-1, keepdims=True)
        o = _dot(p, v_ref[...]) / den
        o_ref[...] = o
        om_ref[...] = _mx(o)
        lse_ref[0] = m + jnp.log(den)

    return pl.pallas_call(
        body, name="mem_attn_fwd", grid=(S // MEM_T, MEM_HEADS),
        in_specs=[pl.BlockSpec((MEM_T, MEM_HEAD_DIM), lambda i, h: (i, QM_BLK + h)),
                  pl.BlockSpec((N_MEM, MEM_HEAD_DIM), lambda i, h: (0, h)),
                  pl.BlockSpec((N_MEM, MEM_HEAD_DIM), lambda i, h: (0, MEM_HEADS + h))],
        out_specs=[pl.BlockSpec((MEM_T, MEM_HEAD_DIM), lambda i, h: (i, h)),
                   pl.BlockSpec((MEM_T, MEM_HEAD_DIM), lambda i, h: (i, h)),
                   pl.BlockSpec((1, MEM_T, 1), lambda i, h: (h, i, 0))],
        out_shape=[jax.ShapeDtypeStruct((S, MEM_WIDTH), F32), jax.ShapeDtypeStruct((S, MEM_WIDTH), MXU_DTYPE),
                   jax.ShapeDtypeStruct((MEM_HEADS, S, 1), F32)],
        compiler_params=_cparams(("parallel", "parallel")),
    )(proj, kv, kv)


def _mem_attn_bwd(proj, kv, om, lse, dy, w_out, *, S):
    scale = MEM_HEAD_DIM ** -0.5

    def body(q_ref, k_ref, v_ref, o_ref, lse_ref, dy_ref, wo_ref, dq_ref, dk_ref, dv_ref):
        @pl.when(pl.program_id(1) == 0)
        def _():
            dk_ref[...] = jnp.zeros_like(dk_ref)
            dv_ref[...] = jnp.zeros_like(dv_ref)

        qv, kv_, vv, dov = q_ref[...], k_ref[...], v_ref[...], _dot(dy_ref[...], wo_ref[...], "nt")
        p = jnp.exp(_dot(qv, kv_, "nt") * scale - lse_ref[0])
        delta = jnp.sum(dov * o_ref[...], axis=-1, keepdims=True)
        ds = p * (_dot(dov, vv, "nt") - delta)
        dq_ref[...] = (_dot(ds, kv_) * scale).astype(dq_ref.dtype)
        dk_ref[...] += _dot(ds, qv, "tn") * scale
        dv_ref[...] += _dot(p, dov, "tn")

    tile = pl.BlockSpec((MEM_T, MEM_HEAD_DIM), lambda h, i: (i, h))
    kvo = pl.BlockSpec((N_MEM, MEM_HEAD_DIM), lambda h, i: (0, h))
    return pl.pallas_call(
        body, name="mem_attn_bwd", grid=(MEM_HEADS, S // MEM_T),
        in_specs=[pl.BlockSpec((MEM_T, MEM_HEAD_DIM), lambda h, i: (i, QM_BLK + h)),
                  pl.BlockSpec((N_MEM, MEM_HEAD_DIM), lambda h, i: (0, h)),
                  pl.BlockSpec((N_MEM, MEM_HEAD_DIM), lambda h, i: (0, MEM_HEADS + h)),
                  tile, pl.BlockSpec((1, MEM_T, 1), lambda h, i: (h, i, 0)),
                  pl.BlockSpec((MEM_T, D_MODEL), lambda h, i: (i, 0)),
                  pl.BlockSpec((MEM_HEAD_DIM, D_MODEL), lambda h, i: (h, 0))],
        out_specs=[tile, kvo, kvo],
        out_shape=[jax.ShapeDtypeStruct((S, MEM_WIDTH), MXU_DTYPE), jax.ShapeDtypeStruct((N_MEM, MEM_WIDTH), F32),
                   jax.ShapeDtypeStruct((N_MEM, MEM_WIDTH), F32)],
        compiler_params=_cparams(("parallel", "arbitrary")),
    )(proj, kv, kv, om, lse, dy, w_out)


MIX_BM = 1024
MIX_BN = 256
GATES_BLK = C_GATES // MIX_BN


def _mix_specs(j_outer):
    ix = (lambda f: (lambda j, i: f(i, j))) if j_outer else (lambda f: f)
    act = lambda width: pl.BlockSpec((MIX_BM, width), ix(lambda i, j: (i, 0)))
    wgt = lambda width: pl.BlockSpec((width, MIX_BN), ix(lambda i, j: (0, j)))
    gate = lambda b: pl.BlockSpec((MIX_BM, MIX_BN), ix(lambda i, j: (i, GATES_BLK + 4 * b + j)))
    bias = lambda b: pl.BlockSpec((1, MIX_BN), ix(lambda i, j: (0, 4 * b + j)))
    tile = pl.BlockSpec((MIX_BM, MIX_BN), ix(lambda i, j: (i, j)))
    return act, wgt, gate, bias, tile


def _mix_fwd(z_lru, o_dil, om, w_lru, w_dil, w_mem, proj, b_gate, *, S):
    act, wgt, gate, bias, tile = _mix_specs(False)

    def body(zl, od, mo, wl, wd, wm, g0, g1, g2, b0, b1, b2, o_ref, t0, t1, t2):
        acc = None
        for a_ref, w_ref, g_ref, b_ref, t_ref in ((zl, wl, g0, b0, t0), (od, wd, g1, b1, t1), (mo, wm, g2, b2, t2)):
            gt = jax.nn.sigmoid(g_ref[...] + b_ref[...])
            t_ref[...] = gt.astype(t_ref.dtype)
            term = gt * _dot(a_ref[...], w_ref[...])
            acc = term if acc is None else acc + term
        o_ref[...] = acc.astype(o_ref.dtype)

    return pl.pallas_call(
        body, name="mix_fwd", grid=(S // MIX_BM, D_MODEL // MIX_BN),
        in_specs=[act(D_RNN), act(256), act(MEM_WIDTH), wgt(D_RNN), wgt(256), wgt(MEM_WIDTH),
                  gate(0), gate(1), gate(2), bias(0), bias(1), bias(2)],
        out_specs=[tile] * 4, out_shape=[jax.ShapeDtypeStruct((S, D_MODEL), MXU_DTYPE)] * 4,
        compiler_params=_cparams(("parallel", "parallel")),
    )(z_lru, o_dil, om, w_lru, w_dil, w_mem, proj, proj, proj, b_gate, b_gate, b_gate)


def _mix_bwd(dx1, w_out, z_lru, o_dil, om, w_lru, w_dil, w_mem, gates, *, S):
    act, wgt, _, _, tile = _mix_specs(False)
    n_j = D_MODEL // MIX_BN

    def body(dx, wo, zl, od, mo, wl, wd, wm, t0, t1, t2,
             dg0, dg1, dg2, dy0, dy1, dy2, db0, db1, db2):
        j = pl.program_id(1)

        @pl.when((pl.program_id(0) == 0) & (j == 0))
        def _():
            for r in (db0, db1, db2):
                r[...] = jnp.zeros_like(r)

        dmv = _dot(dx[...], wo[...], "nt")
        for act_ref, w_ref, t_ref, dg_ref, dy_ref, db_ref in (
                (zl, wl, t0, dg0, dy0, db0), (od, wd, t1, dg1, dy1, db1), (mo, wm, t2, dg2, dy2, db2)):
            y = _dot(act_ref[...], w_ref[...])
            gt = t_ref[...].astype(F32)
            dgate = dmv * y * gt * (1.0 - gt)
            dg_ref[...] = dgate.astype(dg_ref.dtype)
            dy_ref[...] = (dmv * gt).astype(dy_ref.dtype)
            db_ref[j] += _colsum(dgate)

    big = jax.ShapeDtypeStruct((S, D_MODEL), MXU_DTYPE)
    vec = jax.ShapeDtypeStruct((n_j, 1, MIX_BN), F32)
    vspec = pl.BlockSpec((n_j, 1, MIX_BN), lambda i, j: (0, 0, 0))
    res = pl.pallas_call(
        body, name="mix_bwd", grid=(S // MIX_BM, n_j),
        in_specs=[pl.BlockSpec((MIX_BM, D_MODEL), lambda i, j: (i, 0)), pl.BlockSpec((MIX_BN, D_MODEL), lambda i, j: (j, 0)),
                  act(D_RNN), act(256), act(MEM_WIDTH), wgt(D_RNN), wgt(256), wgt(MEM_WIDTH), tile, tile, tile],
        out_specs=[tile] * 6 + [vspec] * 3, out_shape=[big] * 6 + [vec] * 3,
        compiler_params=_cparams(("arbitrary", "arbitrary")),
    )(dx1, w_out, z_lru, o_dil, om, w_lru, w_dil, w_mem, *gates)
    return list(res[:6]) + [r.reshape(1, D_MODEL) for r in res[6:]]


def _adamw_math(w, g, m, v):
    m = ADAM_B1 * m + (1.0 - ADAM_B1) * g
    v = ADAM_B2 * v + (1.0 - ADAM_B2) * (g * g)
    m_hat = m / (1.0 - ADAM_B1 ** ADAM_STEP)
    v_hat = v / (1.0 - ADAM_B2 ** ADAM_STEP)
    delta = -ADAM_LR * (m_hat / (jnp.sqrt(v_hat) + ADAM_EPS) + ADAM_WD * w)
    return delta, m, v


def _adamw_landed(w, parts, slot, land, m, v, *, name, col_blk=0, prev=None):
    R = w.shape[0]
    n_parts, C = land.shape[0], land.shape[2]
    ch = next(d for d in (128, 464) if R % d == 0)
    n_chunks = R // ch
    n_buf = min(3, n_chunks)
    n_prev = 0 if prev is None else 4
    cols = pl.ds(col_blk * C, C)

    def body(s_ref, w_hbm, p_hbm, l_hbm, m_hbm, v_hbm, *rest):
        outs_hbm = rest[n_prev:n_prev + 4]
        wb, pb, lb, mb, vb, gb, db, nmb, nvb, in_sem, out_sem = rest[n_prev + 4:]

        def reads(i, b):
            rows = pl.ds(i * ch, ch)
            pairs = [(w_hbm.at[rows, cols], wb), (p_hbm.at[s_ref[0], rows, :], pb), (l_hbm.at[:, rows, :], lb),
                     (m_hbm.at[rows, cols], mb), (v_hbm.at[rows, cols], vb)]
            return [pltpu.make_async_copy(src, buf.at[b], in_sem.at[5 * b + j]) for j, (src, buf) in enumerate(pairs)]

        def writes(i, b):
            rows = pl.ds(i * ch, ch)
            return [pltpu.make_async_copy(buf.at[b], dst.at[rows, cols], out_sem.at[4 * b + j])
                    for j, (buf, dst) in enumerate(zip((gb, db, nmb, nvb), outs_hbm))]

        for b in range(n_buf):
            for c in reads(b, b):
                c.start()
        for i in range(n_chunks):
            b = i % n_buf
            for c in reads(i, b):
                c.wait()
            if i >= n_buf:
                for c in writes(i - n_buf, b):
                    c.wait()
            g = pb[b].astype(F32)
            for p in range(n_parts):
                g = g + lb[b, p].astype(F32)
            d, nm, nv = _adamw_math(wb[b], g, mb[b], vb[b])
            gb[b] = g
            db[b] = d
            nmb[b] = nm
            nvb[b] = nv
            for c in writes(i, b):
                c.start()
            if i + n_buf < n_chunks:
                for c in reads(i + n_buf, b):
                    c.start()
        for i in range(max(0, n_chunks - n_buf), n_chunks):
            for c in writes(i, i % n_buf):
                c.wait()

    any_spec = pl.BlockSpec(memory_space=pl.ANY)
    f32_buf = pltpu.VMEM((n_buf, ch, C), F32)
    return pl.pallas_call(
        body, name=name,
        in_specs=[pl.BlockSpec(memory_space=pltpu.SMEM)] + [any_spec] * (5 + n_prev), out_specs=[any_spec] * 4,
        out_shape=[jax.ShapeDtypeStruct(w.shape, F32)] * 4,
        scratch_shapes=[f32_buf, pltpu.VMEM((n_buf, ch, C), parts.dtype), pltpu.VMEM((n_buf, n_parts, ch, C), land.dtype),
                        f32_buf, f32_buf, f32_buf, f32_buf, f32_buf, f32_buf,
                        pltpu.SemaphoreType.DMA((5 * n_buf,)), pltpu.SemaphoreType.DMA((4 * n_buf,))],
        input_output_aliases={6 + j: j for j in range(n_prev)},
        compiler_params=_cparams(),
    )(slot, w, parts, land, m, v, *(prev or ()))


def _adamw_plain(w, g, m, v, *, name):
    def body(w_ref, g_ref, m_ref, v_ref, d_ref, nm_ref, nv_ref):
        d, nm, nv = _adamw_math(w_ref[...], g_ref[...], m_ref[...], v_ref[...])
        d_ref[...] = d
        nm_ref[...] = nm
        nv_ref[...] = nv

    return pl.pallas_call(
        body, name=name, out_shape=[jax.ShapeDtypeStruct(w.shape, F32)] * 3, compiler_params=_cparams(),
    )(w, g, m, v)


def _my_pos():
    return lax.axis_index("x"), lax.axis_index("y"), lax.axis_index("c")


def _dev_index(p):
    return 4 * p[0] + 2 * p[1] + p[2]


def _peers(me):
    x, y, c = me
    out = []
    for k in range(1, 8):
        fx, fy, fc = (k >> 2) & 1, (k >> 1) & 1, k & 1
        out.append((k - 1, (1 - x if fx else x, 1 - y if fy else y, 1 - c if fc else c)))
    return out


HBM_SPEC = pl.BlockSpec(memory_space=pltpu.HBM)
SEM_SPEC = pl.BlockSpec(memory_space=pltpu.SEMAPHORE)
DATAFLOW_EFFECT = pltpu.SideEffectType.DATAFLOW_SIDE_EFFECTING


def _gather_refs(src, land, me, peer, k):
    return src, land.at[_dev_index(me)]


def _scatter_refs(src, land, me, peer, k):
    return src.at[_dev_index(peer)], land.at[k]


ALL_RELATIONS = tuple(range(7))
ONE_PER_CHIP = (0, 1, 3, 5)


def _push_start(srcs, land_shapes, refs_of, name, after=(), relations=ALL_RELATIONS):
    n, n_after = len(srcs), len(after)

    def body(*refs):
        ins, lands = refs[:n], refs[n:2 * n]
        send_sems, recv_sems, token = refs[2 * n + n_after], refs[2 * n + n_after + 1], refs[-1]
        me = _my_pos()
        for k, peer in _peers(me):
            if k not in relations:
                continue
            for a in range(n):
                src, dst = refs_of(ins[a], lands[a], me, peer, k)
                pltpu.make_async_remote_copy(src_ref=src, dst_ref=dst, send_sem=send_sems.at[7 * a + k],
                                             recv_sem=recv_sems.at[7 * a + k], device_id=peer, device_id_type=MESH).start()
        token[...] = jnp.zeros_like(token)

    lands = [lax.empty(shp, s.dtype) for shp, s in zip(land_shapes, srcs)]
    hbm = lambda a: pltpu.with_memory_space_constraint(a, pltpu.HBM)
    res = pl.pallas_call(
        body, name=name,
        out_shape=(pltpu.SemaphoreType.DMA((7 * n,)), pltpu.SemaphoreType.DMA((7 * n,)),
                   *[pltpu.HBM(s.shape, s.dtype) for s in srcs], *[pltpu.HBM(l.shape, l.dtype) for l in lands],
                   jax.ShapeDtypeStruct((8, 128), F32)),
        in_specs=[HBM_SPEC] * (2 * n) + [pl.BlockSpec(memory_space=pl.ANY)] * n_after,
        out_specs=(SEM_SPEC, SEM_SPEC, *[HBM_SPEC] * (2 * n), pl.BlockSpec(memory_space=pltpu.VMEM)),
        input_output_aliases={i: 2 + i for i in range(2 * n)},
        compiler_params=pltpu.CompilerParams(has_side_effects=DATAFLOW_EFFECT),
    )(*[hbm(s) for s in srcs], *[hbm(l) for l in lands], *after)
    return dict(sems=(res[0], res[1]), srcs=list(res[2:2 + n]), lands=list(res[2 + n:2 + 2 * n]), token=res[-1], n=n,
                refs_of=refs_of, name=name, relations=relations)


def _push_wait(started, after, with_sources=False):
    n, refs_of, relations = started["n"], started["refs_of"], started["relations"]
    after = list(after) if isinstance(after, (list, tuple)) else [after]

    def body(*refs):
        ins, lands = refs[:n], refs[n:2 * n]
        send_sems, recv_sems = refs[2 * n], refs[2 * n + 1]
        me = _my_pos()
        for k, peer in _peers(me):
            if k not in relations:
                continue
            for a in range(n):
                src, dst = refs_of(ins[a], lands[a], me, peer, k)
                cp = pltpu.make_async_remote_copy(src_ref=src, dst_ref=dst, send_sem=send_sems.at[7 * a + k],
                                                  recv_sem=recv_sems.at[7 * a + k], device_id=peer, device_id_type=MESH)
                cp.wait_send()
                cp.wait_recv()

    arrs = started["srcs"] + started["lands"]
    res = pl.pallas_call(
        body, name=started["name"].replace("start", "wait"),
        out_shape=tuple(pltpu.HBM(a.shape, a.dtype) for a in arrs),
        in_specs=[HBM_SPEC] * (2 * n) + [SEM_SPEC, SEM_SPEC] + [pl.BlockSpec(memory_space=pl.ANY)] * len(after),
        out_specs=tuple([HBM_SPEC] * (2 * n)),
        input_output_aliases={i: i for i in range(2 * n)},
        compiler_params=pltpu.CompilerParams(has_side_effects=DATAFLOW_EFFECT),
    )(*arrs, *started["sems"], *after)
    return (list(res[:n]), list(res[n:2 * n])) if with_sources else list(res[n:2 * n])


def _other_chips(x, y):
    return ((1 - x, y), (x, 1 - y), (1 - x, 1 - y))


def _forward_start(land, name, after=()):
    n_after = len(after)

    def body(*refs):
        land_ref, send_sems, recv_sems, token = refs[0], refs[1 + n_after], refs[2 + n_after], refs[-1]
        x, y, c = _my_pos()
        for j, (cx, cy) in enumerate(_other_chips(x, y)):
            blk = land_ref.at[_dev_index((cx, cy, c))]
            pltpu.make_async_remote_copy(src_ref=blk, dst_ref=blk, send_sem=send_sems.at[j], recv_sem=recv_sems.at[j],
                                         device_id=(x, y, 1 - c), device_id_type=MESH).start()
        token[...] = jnp.zeros_like(token)

    res = pl.pallas_call(
        body, name=name,
        out_shape=(pltpu.SemaphoreType.DMA((3,)), pltpu.SemaphoreType.DMA((3,)), pltpu.HBM(land.shape, land.dtype),
                   jax.ShapeDtypeStruct((8, 128), F32)),
        in_specs=[HBM_SPEC] + [pl.BlockSpec(memory_space=pl.ANY)] * n_after,
        out_specs=(SEM_SPEC, SEM_SPEC, HBM_SPEC, pl.BlockSpec(memory_space=pltpu.VMEM)),
        input_output_aliases={0: 2},
        compiler_params=pltpu.CompilerParams(has_side_effects=DATAFLOW_EFFECT),
    )(pltpu.with_memory_space_constraint(land, pltpu.HBM), *after)
    return dict(sems=(res[0], res[1]), land=res[2], token=res[3], name=name)


def _forward_wait(started, after):
    after = list(after) if isinstance(after, (list, tuple)) else [after]

    def body(land_ref, send_sems, recv_sems, *rest):
        x, y, c = _my_pos()
        for j, (cx, cy) in enumerate(_other_chips(x, y)):
            cp = pltpu.make_async_remote_copy(
                src_ref=land_ref.at[_dev_index((cx, cy, c))], dst_ref=land_ref.at[_dev_index((cx, cy, 1 - c))],
                send_sem=send_sems.at[j], recv_sem=recv_sems.at[j], device_id=(x, y, 1 - c), device_id_type=MESH)
            cp.wait_send()
            cp.wait_recv()

    land = started["land"]
    return pl.pallas_call(
        body, name=started["name"].replace("start", "wait"), out_shape=pltpu.HBM(land.shape, land.dtype),
        in_specs=[HBM_SPEC, SEM_SPEC, SEM_SPEC] + [pl.BlockSpec(memory_space=pl.ANY)] * len(after),
        out_specs=HBM_SPEC, input_output_aliases={0: 0},
        compiler_params=pltpu.CompilerParams(has_side_effects=DATAFLOW_EFFECT),
    )(land, *started["sems"], *after)


def _sum_slots(slots):
    def body(in_ref, out_ref):
        acc = in_ref[0]
        for d in range(1, N_DEV):
            acc = acc + in_ref[d]
        out_ref[...] = acc

    return pl.pallas_call(body, name="sum_small", out_shape=jax.ShapeDtypeStruct(slots.shape[1:], F32),
                          compiler_params=_cparams())(slots)


def _adamw_many(ws, gs, ms, vs):
    n = len(ws)

    def body(*refs):
        for i in range(n):
            w_ref, g_ref, m_ref, v_ref = (refs[j * n + i] for j in range(4))
            d_, nm, nv = _adamw_math(w_ref[...], g_ref[...], m_ref[...], v_ref[...])
            for j, val in enumerate((d_, nm, nv)):
                refs[(4 + j) * n + i][...] = val

    res = pl.pallas_call(body, name="adamw_small", out_shape=[jax.ShapeDtypeStruct(w_.shape, F32) for w_ in ws] * 3,
                         compiler_params=_cparams())(*ws, *gs, *ms, *vs)
    return [(res[i], res[n + i], res[2 * n + i]) for i in range(n)]


def _local_step(x, mem, tgt, W, P, late_weights, send_grads, reduce_small, tie0):
    S = x.shape[0]
    W = dict(W)
    h = _rmsnorm_fwd(x, P["g_mix"] + tie0, rows=S, name="norm_mix")
    mem_n = _rmsnorm_fwd(mem, P["g_mem"], rows=N_MEM, name="norm_mem")
    buckets = _dil_buckets()
    bias = _dil_bias(P["rel_bias"], buckets)
    wa_bd, wx_bd = _mx(_block_diag(P["w_rg_a"])), _mx(_block_diag(P["w_rg_x"]))
    W.update(late_weights("first", [h, mem_n, bias, wa_bd, wx_bd]))
    proj = _matmul(h, W["w_in_t"], M=S, N=D_IN, K=D_MODEL, mode="nt", bm=512, bn=D_IN // 2, bk=D_MODEL, name="mm_in",
                   j_outer=True, deps=[W["started"]])

    group_out = [_dilated_fwd(proj, bias, g, S=S) for g in range(len(DIL_GROUPS))]
    o_dil, o_dil_m, lse_dil = _dilated_merge([o for o, _ in group_out], [l for _, l in group_out], S=S)

    W.update(late_weights("branch", [o_dil]))
    lru_args = (W["conv_w"], P["conv_b"].reshape(1, -1), wa_bd, wx_bd, P["b_rg_a"].reshape(1, -1),
                P["b_rg_x"].reshape(1, -1), P["lru_lambda"].reshape(1, -1))
    hl, z_lru, a_lru, mult_lru = _lru_fwd(proj, *lru_args, S=S)
    kv = _matmul(mem_n, W["w_mem_kv"], M=N_MEM, N=2 * MEM_WIDTH, K=D_MODEL, mode="nn", bm=N_MEM, bn=512, bk=D_MODEL,
                 name="mm_kv")
    om, om_m, lse_mem = _mem_attn_fwd(proj, kv, S=S)
    b_gate = P["b_gate"].reshape(1, -1)
    merged, *gates = _mix_fwd(z_lru, o_dil_m, om_m, W["w_lru_out"], W["w_dil_out"], W["w_mem_out"], proj, b_gate, S=S)
    g_mlp, g_final, g_mix = (P[n].reshape(1, D_MODEL) for n in ("g_mlp", "g_final", "g_mix"))
    x1, hm = _matmul_rows(merged, W["w_out"], M=S, K=D_MODEL, mode="nn", bm=512, name="mm_out",
                          row_fn=_residual_then_norm, out_dtypes=(F32, MXU_DTYPE), tiles=[x], vecs=[g_mlp])
    W.update(late_weights("mlp", [hm]))

    def relu2(acc):
        rl = jnp.maximum(acc, 0.0)
        return rl * rl, rl

    act, relu_u = _matmul(hm, W["w_mlp_in_t"], M=S, N=D_FF, K=D_MODEL, mode="nt", bm=1024, bn=1024, bk=D_MODEL,
                          name="mm_mlp_in", out_dtypes=(MXU_DTYPE, MXU_DTYPE), epilogue=relu2, j_outer=True)
    dx2, dx2_m, loss, dg_final = _matmul_rows(
        act, W["w_mlp_out"], M=S, K=D_FF, mode="nn", bm=512, name="mm_mlp_out", row_fn=_residual_then_loss,
        out_dtypes=(F32, MXU_DTYPE), tiles=[x1, tgt], vecs=[g_final], acc_widths=(1, D_MODEL))

    G, Gs = {}, {}
    Gs["g_final"] = dg_final
    dw = dict(mode="tn", K=S, bk=S, out_dtypes=(MXU_DTYPE,))
    G["w_mlp_out"] = _matmul(act, dx2_m, M=D_FF, N=D_MODEL, bm=512, bn=D_MODEL, name="mm_dw_mlp_out",
                             parts=("rows", D_FF // N_DEV), **dw)
    du = _matmul(dx2_m, W["w_mlp_out"], M=S, N=D_FF, K=D_MODEL, mode="nt", bm=1024, bn=1024, bk=D_MODEL, name="mm_du",
                 out_dtypes=(MXU_DTYPE,), epilogue=lambda acc, rl: (acc * (2.0 * rl.astype(F32)),),
                 extras=[(relu_u, (0, 0))], j_outer=True)
    G["w_mlp_in"] = _matmul(hm, du, M=D_MODEL, N=D_FF, bm=D_MODEL, bn=512, name="mm_dw_mlp_in",
                            parts=("cols", D_FF // N_DEV), **dw)
    tie1 = send_grads({n: G.pop(n) for n in ("w_mlp_out", "w_mlp_in")})
    dx1, dx1_m, Gs["g_mlp"] = _matmul_rows(
        du, W["w_mlp_in_t"], M=S, K=D_FF, mode="nn", bm=512, name="mm_dhm", row_fn=_norm_bwd_then_residual(2),
        out_dtypes=(F32, MXU_DTYPE), tiles=[x1, dx2], vecs=[g_mlp], acc_widths=(D_MODEL,), deps=[tie1])
    G["w_out"] = _matmul(merged, dx1_m, M=D_MODEL, N=D_MODEL, bm=512, bn=D_MODEL, name="mm_dw_out",
                         parts=("rows", D_MODEL // N_DEV), **dw)
    (dg0, dg1, dg2, dy_lru, dy_dil, dy_mem, db0, db1, db2) = _mix_bwd(
        dx1_m, W["w_out"], z_lru, o_dil_m, om_m, W["w_lru_out"], W["w_dil_out"], W["w_mem_out"], gates, S=S)
    Gs["b_gate0"], Gs["b_gate1"], Gs["b_gate2"] = db0, db1, db2

    G["w_mem_out"] = _matmul(om_m, dy_mem, M=MEM_WIDTH, N=D_MODEL, bm=MEM_WIDTH, bn=D_MODEL, name="mm_dw_mem_out",
                             parts=("cols", D_MODEL // N_DEV), **dw)
    dqm, dk_mem, dv_mem = _mem_attn_bwd(proj, kv, om, lse_mem, dy_mem, W["w_mem_out"], S=S)
    dkv = jnp.concatenate([dk_mem, dv_mem], axis=1)
    G["w_mem_kv"] = _matmul(mem_n, dkv, M=D_MODEL, N=2 * MEM_WIDTH, K=N_MEM, mode="tn", bm=D_MODEL, bn=2 * MEM_WIDTH,
                            bk=N_MEM, name="mm_dw_kv", out_dtypes=(MXU_DTYPE,), parts=("rows", D_MODEL // N_DEV))
    dmem_n = _matmul(dkv, W["w_mem_kv"], M=N_MEM, N=D_MODEL, K=2 * MEM_WIDTH, mode="nt", bm=N_MEM, bn=D_MODEL,
                     bk=2 * MEM_WIDTH, name="mm_dmem")
    (Gs["g_mem"],) = _rmsnorm_bwd(mem, P["g_mem"], dmem_n, None, rows=N_MEM, name="norm_mem_bwd", dx_dtypes=())

    G["w_dil_out"] = _matmul(o_dil_m, dy_dil, M=256, N=D_MODEL, bm=256, bn=D_MODEL, name="mm_dw_dil_out",
                             parts=("cols", D_MODEL // N_DEV), **dw)
    do_dil, delta = _matmul(dy_dil, W["w_dil_out"], M=S, N=256, K=D_MODEL, mode="nt", bm=512, bn=256, bk=D_MODEL,
                            name="mm_do_dil", out_dtypes=(F32, F32), epilogue=_with_delta, extras=[(o_dil, (0, 0))])
    G["w_lru_out"] = _matmul(z_lru, dy_lru, M=D_RNN, N=D_MODEL, bm=D_RNN, bn=D_MODEL, name="mm_dw_lru_out",
                             parts=("cols", D_MODEL // N_DEV), **dw)
    tie2 = send_grads({n: G.pop(n) for n in ("w_out", "w_mem_out", "w_mem_kv", "w_dil_out", "w_lru_out")})
    bias = bias + tie2[0, 0]
    dqkv, dbias = None, []
    for g in range(len(DIL_GROUPS)):
        *dqkv, db_g = _dilated_bwd(proj, do_dil, lse_dil, delta, bias, g, S=S, into=dqkv)
        dbias.append(db_g)
    drel = _dil_bias_bwd(jnp.stack(dbias, axis=0).reshape(len(DIL_GROUPS), DIL_HEADS, SPAN, 2 * SPAN), buckets)
    Gs["rel_bias"] = drel

    dxl, dgl, dcw, dcb, dwa, dwx, dba, dbx, dlam = _lru_bwd(proj, hl, a_lru, mult_lru, dy_lru, W["w_lru_out"], *lru_args,
                                                            S=S)
    Gs["conv_w"], Gs["conv_b"] = dcw, dcb
    Gs["w_rg_a"], Gs["w_rg_x"] = _block_diag_extract(dwa), _block_diag_extract(dwx)
    Gs["b_rg_a"], Gs["b_rg_x"], Gs["lru_lambda"] = dba, dbx, dlam
    Gs["loss"] = loss

    dproj = [dxl, dgl] + dqkv + [dqm, dg0, dg1, dg2]
    tie = []
    for q in range(W_IN_PIECES):
        dw_q = None
        for half in range(2):
            dw_q = _dw_in_t_half(h, dproj, q, half, S=S, name=f"mm_dw_in_{q}_{half}", into=dw_q, deps=tie)
        tie = [send_grads({f"w_in_{q}": dw_q})]
    grad_x, Gs["g_mix"] = _matmul_rows(
        dproj, W["w_in_t"], M=S, K=D_IN, mode="nn", bm=256, name="mm_dh", row_fn=_norm_bwd_then_residual(1),
        out_dtypes=(F32,), tiles=[x, dx1], vecs=[g_mix], acc_widths=(D_MODEL,), deps=tie)
    return grad_x, reduce_small(Gs)


BIG = ("w_in", "w_lru_out", "w_dil_out", "w_mem_kv", "w_mem_out", "w_out", "w_mlp_in", "w_mlp_out")
W_IN_PIECES = 2
COL_SHARDED = ("w_lru_out", "w_dil_out", "w_mem_out", "w_mlp_in")
GATHERED_TRANSPOSED = ("w_mlp_in",)
SMALL = ("g_mix", "b_gate", "conv_b", "w_rg_a", "b_rg_a", "w_rg_x", "b_rg_x", "lru_lambda", "rel_bias", "g_mem",
         "g_mlp", "g_final")
WEIGHTS = ("g_mix", "w_in", "b_gate", "conv_w", "conv_b", "w_rg_a", "b_rg_a", "w_rg_x", "b_rg_x", "lru_lambda",
           "w_lru_out", "rel_bias", "w_dil_out", "g_mem", "w_mem_kv", "w_mem_out", "w_out", "g_mlp", "w_mlp_in",
           "w_mlp_out", "g_final")


def _gathered_to_full(name, gathered):
    if name in COL_SHARDED:
        n, r, c = gathered.shape
        return gathered.transpose(1, 0, 2).reshape(r, n * c)
    n, r, c = gathered.shape
    return gathered.reshape(n * r, c)


SMALL_GRADS = (("g_mix", (1, 1024)), ("b_gate0", (1, 1024)), ("b_gate1", (1, 1024)), ("b_gate2", (1, 1024)),
               ("conv_b", (1, 768)), ("w_rg_a", (12, 64, 64)), ("b_rg_a", (1, 768)), ("w_rg_x", (12, 64, 64)),
               ("b_rg_x", (1, 768)), ("lru_lambda", (1, 768)), ("rel_bias", (32, 128)), ("g_mem", (1, 1024)),
               ("g_mlp", (1, 1024)), ("g_final", (1, 1024)), ("conv_w", (4, 768)), ("loss", (1, 1)))


def _pack(parts):
    flat = jnp.concatenate([p.reshape(-1) for p in parts])
    return jnp.pad(flat, (0, (-flat.shape[0]) % 1024)).reshape(-1, 128)


def _unpack(pack, shapes):
    flat = pack.reshape(-1)
    out, off = [], 0
    for shp in shapes:
        size = math.prod(shp)
        out.append(flat[off:off + size].reshape(shp))
        off += size
    return out


def kernel(x, mem, g_mix, w_in, b_gate, conv_w, conv_b, w_rg_a, b_rg_a, w_rg_x, b_rg_x, lru_lambda, w_lru_out, rel_bias, w_dil_out, g_mem, w_mem_kv, w_mem_out, w_out, g_mlp, w_mlp_in, w_mlp_out, g_final, loss_target, m_g_mix, m_w_in, m_b_gate, m_conv_w, m_conv_b, m_w_rg_a, m_b_rg_a, m_w_rg_x, m_b_rg_x, m_lru_lambda, m_w_lru_out, m_rel_bias, m_w_dil_out, m_g_mem, m_w_mem_kv, m_w_mem_out, m_w_out, m_g_mlp, m_w_mlp_in, m_w_mlp_out, m_g_final, v_g_mix, v_w_in, v_b_gate, v_conv_w, v_conv_b, v_w_rg_a, v_b_rg_a, v_w_rg_x, v_b_rg_x, v_lru_lambda, v_w_lru_out, v_rel_bias, v_w_dil_out, v_g_mem, v_w_mem_kv, v_w_mem_out, v_w_out, v_g_mlp, v_w_mlp_in, v_w_mlp_out, v_g_final):
    w = dict(g_mix=g_mix, w_in=w_in, b_gate=b_gate, conv_w=conv_w, conv_b=conv_b, w_rg_a=w_rg_a, b_rg_a=b_rg_a,
             w_rg_x=w_rg_x, b_rg_x=b_rg_x, lru_lambda=lru_lambda, w_lru_out=w_lru_out, rel_bias=rel_bias,
             w_dil_out=w_dil_out, g_mem=g_mem, w_mem_kv=w_mem_kv, w_mem_out=w_mem_out, w_out=w_out, g_mlp=g_mlp,
             w_mlp_in=w_mlp_in, w_mlp_out=w_mlp_out, g_final=g_final)
    m = dict(g_mix=m_g_mix, w_in=m_w_in, b_gate=m_b_gate, conv_w=m_conv_w, conv_b=m_conv_b, w_rg_a=m_w_rg_a,
             b_rg_a=m_b_rg_a, w_rg_x=m_w_rg_x, b_rg_x=m_b_rg_x, lru_lambda=m_lru_lambda, w_lru_out=m_w_lru_out,
             rel_bias=m_rel_bias, w_dil_out=m_w_dil_out, g_mem=m_g_mem, w_mem_kv=m_w_mem_kv, w_mem_out=m_w_mem_out,
             w_out=m_w_out, g_mlp=m_g_mlp, w_mlp_in=m_w_mlp_in, w_mlp_out=m_w_mlp_out, g_final=m_g_final)
    v = dict(g_mix=v_g_mix, w_in=v_w_in, b_gate=v_b_gate, conv_w=v_conv_w, conv_b=v_conv_b, w_rg_a=v_w_rg_a,
             b_rg_a=v_b_rg_a, w_rg_x=v_w_rg_x, b_rg_x=v_b_rg_x, lru_lambda=v_lru_lambda, w_lru_out=v_w_lru_out,
             rel_bias=v_rel_bias, w_dil_out=v_w_dil_out, g_mem=v_g_mem, w_mem_kv=v_w_mem_kv, w_mem_out=v_w_mem_out,
             w_out=v_w_out, g_mlp=v_g_mlp, w_mlp_in=v_w_mlp_in, w_mlp_out=v_w_mlp_out, g_final=v_g_final)

    my_idx = _dev_index(_my_pos())

    w_in_shard = _mx(w["w_in"].T)
    first = _push_start([w_in_shard], [(N_DEV,) + w_in_shard.shape], _gather_refs, "gather_in_start",
                        relations=ONE_PER_CHIP)
    cw_cols = D_RNN // N_DEV
    conv_pad = jnp.zeros((64, D_MODEL), F32).at[:CONV_WIDTH, :cw_cols].set(w["conv_w"])
    late = {}
    P = {n: w[n] for n in SMALL}

    def start_late(order_after):
        for group, names in (("branch", ("w_mem_kv", "w_lru_out", "w_dil_out", "w_mem_out", "w_out", "conv_w")),
                             ("mlp", ("w_mlp_in", "w_mlp_out"))):
            f32, order_after = lax.optimization_barrier(({n: w[n] for n in names}, order_after))
            shards = [conv_pad if n == "conv_w" else _mx(f32[n].T if n in GATHERED_TRANSPOSED else f32[n])
                      for n in names]
            started = _push_start(shards, [(N_DEV,) + s.shape for s in shards], _gather_refs, f"gather_{group}_start",
                                  after=order_after)
            late[group] = (names, shards, started)
            order_after = [started["token"]]

    start_late([first["token"]])

    def late_weights(group, after):
        if group == "first":
            (land,) = _push_wait(first, after)
            forward = _forward_start(land, "forward_in_start")
            full = lax.dynamic_update_index_in_dim(_forward_wait(forward, forward["token"]), w_in_shard, my_idx, 0)
            return {"w_in_t": full.reshape(D_IN, D_MODEL), "started": late["mlp"][2]["token"]}
        names, shards, started = late[group]
        out = {}
        for n, land, own in zip(names, _push_wait(started, after), shards):
            full = lax.dynamic_update_index_in_dim(land, own, my_idx, 0)
            if n == "conv_w":
                out[n] = full[:, :CONV_WIDTH, :cw_cols].transpose(1, 0, 2).reshape(CONV_WIDTH, D_RNN)
            elif n in GATHERED_TRANSPOSED:
                out[n + "_t"] = full.reshape(-1, full.shape[2])
            else:
                out[n] = _gathered_to_full(n, full)
        return out

    sent, small = [], {}

    def send_grads(gs):
        names = list(gs)
        parts = [gs[n] for n in names]
        started = _push_start(parts, [(N_DEV - 1,) + p.shape[1:] for p in parts], _scatter_refs,
                              f"scatter{len(sent)}_start")
        sent.append((names, started))
        return started["token"]

    def reduce_small(gs):
        small["pack"] = _pack([gs[n] for n, _ in SMALL_GRADS])
        small["started"] = _push_start([small["pack"]], [(N_DEV,) + small["pack"].shape], _gather_refs, "small_start")
        return small["started"]["token"]

    grad_x, last_token = _local_step(x[0], mem[0], loss_target[0], {}, P, late_weights, send_grads, reduce_small,
                                     late["mlp"][2]["token"][0, 0])

    grads, deltas, new_m, new_v = {}, {}, {}, {}
    after = last_token
    my_slot = jnp.reshape(my_idx, (1,)).astype(jnp.int32)
    for names, started in sent[:-W_IN_PIECES]:
        for n, parts, land in zip(names, *_push_wait(started, after, with_sources=True)):
            grads[n], deltas[n], new_m[n], new_v[n] = _adamw_landed(w[n], parts, my_slot, land, m[n], v[n],
                                                                    name=f"adamw_{n}")
            after = deltas[n]
    prev = None
    for q, (names, started) in enumerate(sent[-W_IN_PIECES:]):
        (parts,), (land,) = _push_wait(started, after, with_sources=True)
        prev = _adamw_landed(w["w_in"].T, parts, my_slot, land, m["w_in"].T, v["w_in"].T, name=f"adamw_{names[0]}",
                             col_blk=q, prev=prev)
        after = prev[1]
    grads["w_in"], deltas["w_in"], new_m["w_in"], new_v["w_in"] = [t.T for t in prev]
    (small_land,) = _push_wait(small["started"], [after] + [deltas[n] for n in BIG if n != "w_in"])
    total = _sum_slots(lax.dynamic_update_index_in_dim(small_land, small["pack"], my_idx, 0))
    summed = dict(zip([n for n, _ in SMALL_GRADS], _unpack(total, [shp for _, shp in SMALL_GRADS])))
    summed["b_gate"] = jnp.concatenate([summed.pop(f"b_gate{b}") for b in range(3)], axis=1)
    summed["rel_bias"] = summed["rel_bias"][:, :3 * DIL_HEADS]
    for n in SMALL:
        grads[n] = summed[n].reshape(w[n].shape)
    small_updates = _adamw_many([w[n] for n in SMALL], [grads[n] for n in SMALL], [m[n] for n in SMALL],
                                [v[n] for n in SMALL])
    for n, (d_, nm_, nv_) in zip(SMALL, small_updates):
        deltas[n], new_m[n], new_v[n] = d_, nm_, nv_
    conv_w_sum, loss_sum = summed["conv_w"], summed["loss"]
    grads["conv_w"] = lax.dynamic_slice(conv_w_sum, (0, my_idx * cw_cols), (CONV_WIDTH, cw_cols))
    deltas["conv_w"], new_m["conv_w"], new_v["conv_w"] = _adamw_plain(
        w["conv_w"], grads["conv_w"], m["conv_w"], v["conv_w"], name="adamw_conv_w")

    return (loss_sum.reshape(()), grad_x[None], *[grads[n] for n in WEIGHTS], *[deltas[n] for n in WEIGHTS],
            *[new_m[n] for n in WEIGHTS], *[new_v[n] for n in WEIGHTS])
```
